```python
import math
import jax, jax.numpy as jnp
from jax import lax
import numpy as np

D_MODEL = 2048
BATCH = 8
SEQ = 2048
DEPTH = 1

D_MIX = D_MODEL
D_SSM = D_MIX // 2
D_SGU = D_MIX - D_SSM
SSM_GROUP = 16
SSM_GROUPS = D_SSM // SSM_GROUP
SSM_STATE = 64
DT_MIN = 1e-3
DT_MAX = 1e-1
SGU_HEADS = 8
SGU_HEAD_DIM = D_SGU // SGU_HEADS
CHUNK = 128
D_IN = D_SSM + 2 * D_SGU
D_FF = 5632
CONV_W = 3
N_MOD = 6
EPS = 1e-6

kernel_name = "hybrid_s5_sgu_convffn_layer"


def rms_norm(x, g):
    xf = x.astype(jnp.float32)
    y = xf * lax.rsqrt(jnp.mean(xf * xf, axis=-1, keepdims=True) + EPS)
    return (y * g.astype(jnp.float32)).astype(x.dtype)


def layer_norm(x, g, b):
    xf = x.astype(jnp.float32)
    mu = jnp.mean(xf, axis=-1, keepdims=True)
    xc = xf - mu
    y = xc * lax.rsqrt(jnp.mean(xc * xc, axis=-1, keepdims=True) + EPS)
    return (y * g.astype(jnp.float32) + b.astype(jnp.float32)).astype(x.dtype)


def causal_dwconv(h, w, b):
    C = h.shape[-1]
    k = w.shape[0]
    out = lax.conv_general_dilated(h, w[:, None, :].astype(h.dtype), window_strides=(1,), padding=[(k - 1, 0)],
                                   dimension_numbers=("NWC", "WIO", "NWC"), feature_group_count=C)
    return out + b


def _ssm_combine(left, right):
    alr, ali, blr, bli = left
    arr, ari, brr, bri = right
    return (arr * alr - ari * ali,
            arr * ali + ari * alr,
            arr * blr - ari * bli + brr,
            arr * bli + ari * blr + bri)


def s5_mixer(u, log_dt, a_re, a_im, b_re, b_im, c_re, c_im, d, w_glu, b_glu):
    f32 = jnp.float32
    Bsz, S, _ = u.shape
    uf = u.astype(f32).reshape(Bsz, S, SSM_GROUPS, SSM_GROUP)
    dt = jnp.exp(log_dt.astype(f32))[:, None]
    are = a_re.astype(f32)
    aim = a_im.astype(f32)
    mag = jnp.exp(are * dt)
    lb_re = mag * jnp.cos(aim * dt)
    lb_im = mag * jnp.sin(aim * dt)
    den = are * are + aim * aim
    nr = lb_re - 1.0
    ni = lb_im
    f_re = ((nr * are + ni * aim) / den)[:, :, None]
    f_im = ((ni * are - nr * aim) / den)[:, :, None]
    br = b_re.astype(f32)
    bi = b_im.astype(f32)
    bb_re = f_re * br - f_im * bi
    bb_im = f_re * bi + f_im * br
    bu_re = jnp.einsum("bsgc,gnc->bsgn", uf, bb_re)
    bu_im = jnp.einsum("bsgc,gnc->bsgn", uf, bb_im)
    a_t_re = jnp.broadcast_to(lb_re[None, None], (1, S, SSM_GROUPS, SSM_STATE))
    a_t_im = jnp.broadcast_to(lb_im[None, None], (1, S, SSM_GROUPS, SSM_STATE))
    _, _, h_re, h_im = lax.associative_scan(_ssm_combine, (a_t_re, a_t_im, bu_re, bu_im), axis=1)
    y = (jnp.einsum("bsgn,gcn->bsgc", h_re, c_re.astype(f32))
         - jnp.einsum("bsgn,gcn->bsgc", h_im, c_im.astype(f32))
         + d.astype(f32).reshape(SSM_GROUPS, SSM_GROUP) * uf)
    y = jax.nn.gelu(y)
    y = y * jax.nn.sigmoid(jnp.einsum("bsgc,gce->bsge", y, w_glu.astype(f32)) + b_glu.astype(f32))
    return y.reshape(Bsz, S, D_SSM).astype(u.dtype)


def sgu_mixer(z_u, z_v, ln_g, ln_b, w_s, b_s):
    Bsz, S, _ = z_u.shape
    n_chunks = S // CHUNK
    u = jax.nn.gelu(z_u)
    v = layer_norm(jax.nn.gelu(z_v), ln_g, ln_b)
    v = v.reshape(Bsz, n_chunks, CHUNK, SGU_HEADS, SGU_HEAD_DIM)
    mask = jnp.tril(jnp.ones((CHUNK, CHUNK), dtype=bool))
    w = jnp.where(mask[None], w_s, 0.0).astype(v.dtype)
    mixed = jnp.einsum("hij,bcjhd->bcihd", w, v) + jnp.transpose(b_s)[None, None, :, :, None]
    return u * mixed.reshape(Bsz, S, D_SGU)


def _fwd_setup_inputs(seed: int = 0) -> dict:
    key = jax.random.key(seed)
    ks = jax.random.split(key, 32)
    f32 = jnp.float32
    nrm = lambda k, shape, s: jax.random.normal(k, shape, f32) * s
    L = DEPTH
    G, N, C = SSM_GROUPS, SSM_STATE, SSM_GROUP
    n_idx = jnp.arange(N, dtype=f32)
    return {
        "x": nrm(ks[0], (BATCH, SEQ, D_MODEL), 1.0),
        "c": nrm(ks[1], (BATCH, D_MODEL), 1.0),
        "w_ada": nrm(ks[2], (L, D_MODEL, N_MOD * D_MODEL), 0.5 * D_MODEL ** -0.5),
        "b_ada": nrm(ks[3], (L, N_MOD * D_MODEL), 0.01),
        "g_pre_mix": 1.0 + nrm(ks[4], (L, D_MODEL), 0.01),
        "g_post_mix": 1.0 + nrm(ks[5], (L, D_MODEL), 0.01),
        "w_in": nrm(ks[6], (L, D_MODEL, D_IN), D_MODEL ** -0.5),
        "ssm_log_dt": jax.random.uniform(ks[7], (L, G), f32, math.log(DT_MIN), math.log(DT_MAX)),
        "ssm_a_re": -0.5 + nrm(ks[8], (L, G, N), 0.01),
        "ssm_a_im": math.pi * n_idx[None, None, :] + nrm(ks[9], (L, G, N), 0.01),
        "ssm_b_re": nrm(ks[10], (L, G, N, C), (2.0 * C) ** -0.5),
        "ssm_b_im": nrm(ks[11], (L, G, N, C), (2.0 * C) ** -0.5),
        "ssm_c_re": nrm(ks[12], (L, G, C, N), (2.0 * N) ** -0.5),
        "ssm_c_im": nrm(ks[13], (L, G, C, N), (2.0 * N) ** -0.5),
        "ssm_d": nrm(ks[14], (L, D_SSM), 1.0),
        "ssm_w_glu": nrm(ks[15], (L, G, C, C), C ** -0.5),
        "ssm_b_glu": nrm(ks[16], (L, G, C), 0.01),
        "sgu_ln_g": 1.0 + nrm(ks[17], (L, D_SGU), 0.01),
        "sgu_ln_b": nrm(ks[18], (L, D_SGU), 0.01),
        "sgu_w": nrm(ks[19], (L, SGU_HEADS, CHUNK, CHUNK), CHUNK ** -0.5),
        "sgu_b": 1.0 + nrm(ks[20], (L, SGU_HEADS, CHUNK), 0.01),
        "g_out_ssm": 1.0 + nrm(ks[21], (L, D_SSM), 0.01),
        "g_out_sgu": 1.0 + nrm(ks[22], (L, D_SGU), 0.01),
        "w_out": nrm(ks[23], (L, D_MIX, D_MODEL), D_MIX ** -0.5),
        "g_pre_ffn": 1.0 + nrm(ks[24], (L, D_MODEL), 0.01),
        "g_post_ffn": 1.0 + nrm(ks[25], (L, D_MODEL), 0.01),
        "w_up": nrm(ks[26], (L, D_MODEL, 2 * D_FF), D_MODEL ** -0.5),
        "conv_w": nrm(ks[27], (L, CONV_W, 2 * D_FF), CONV_W ** -0.5),
        "conv_b": nrm(ks[28], (L, 2 * D_FF), 0.01),
        "w_down": nrm(ks[29], (L, D_FF, D_MODEL), D_FF ** -0.5),
    }


def _fwd_reference(x, c, w_ada, b_ada, g_pre_mix, g_post_mix, w_in, ssm_log_dt, ssm_a_re, ssm_a_im,
              ssm_b_re, ssm_b_im, ssm_c_re, ssm_c_im, ssm_d, ssm_w_glu, ssm_b_glu,
              sgu_ln_g, sgu_ln_b, sgu_w, sgu_b, g_out_ssm, g_out_sgu, w_out,
              g_pre_ffn, g_post_ffn, w_up, conv_w, conv_b, w_down):
    c_act = jax.nn.silu(c)
    for l in range(DEPTH):
        mod = jnp.einsum("bd,de->be", c_act, w_ada[l]) + b_ada[l]
        sh1, sc1, gt1, sh2, sc2, gt2 = jnp.split(mod[:, None, :], N_MOD, axis=-1)

        h = rms_norm(x, g_pre_mix[l]) * (1.0 + sc1) + sh1
        z = jnp.einsum("bsd,de->bse", h, w_in[l])
        z_ssm, z_u, z_v = jnp.split(z, [D_SSM, D_SSM + D_SGU], axis=-1)
        y_ssm = s5_mixer(z_ssm, ssm_log_dt[l], ssm_a_re[l], ssm_a_im[l], ssm_b_re[l], ssm_b_im[l],
                         ssm_c_re[l], ssm_c_im[l], ssm_d[l], ssm_w_glu[l], ssm_b_glu[l])
        y_sgu = sgu_mixer(z_u, z_v, sgu_ln_g[l], sgu_ln_b[l], sgu_w[l], sgu_b[l])
        y = jnp.concatenate([rms_norm(y_ssm, g_out_ssm[l]), rms_norm(y_sgu, g_out_sgu[l])], axis=-1)
        y = jnp.einsum("bse,ed->bsd", y, w_out[l])
        x = x + gt1 * rms_norm(y, g_post_mix[l])

        h = rms_norm(x, g_pre_ffn[l]) * (1.0 + sc2) + sh2
        up = causal_dwconv(jnp.einsum("bsd,df->bsf", h, w_up[l]), conv_w[l], conv_b[l])
        a, b = jnp.split(up, 2, axis=-1)
        f = jnp.einsum("bsf,fd->bsd", jax.nn.silu(a) * b, w_down[l])
        x = x + gt2 * rms_norm(f, g_post_ffn[l])
    return x


import jax as _jax
import jax.numpy as _jnp

TWIN_FORMAT = 'train_step'
FWD_PARAMS = ['x', 'c', 'w_ada', 'b_ada', 'g_pre_mix', 'g_post_mix', 'w_in', 'ssm_log_dt', 'ssm_a_re', 'ssm_a_im', 'ssm_b_re', 'ssm_b_im', 'ssm_c_re', 'ssm_c_im', 'ssm_d', 'ssm_w_glu', 'ssm_b_glu', 'sgu_ln_g', 'sgu_ln_b', 'sgu_w', 'sgu_b', 'g_out_ssm', 'g_out_sgu', 'w_out', 'g_pre_ffn', 'g_post_ffn', 'w_up', 'conv_w', 'conv_b', 'w_down']
TWIN_WEIGHTS = ['w_ada', 'b_ada', 'g_pre_mix', 'g_post_mix', 'w_in', 'ssm_log_dt', 'ssm_a_re', 'ssm_a_im', 'ssm_b_re', 'ssm_b_im', 'ssm_c_re', 'ssm_c_im', 'ssm_d', 'ssm_w_glu', 'ssm_b_glu', 'sgu_ln_g', 'sgu_ln_b', 'sgu_w', 'sgu_b', 'g_out_ssm', 'g_out_sgu', 'w_out', 'g_pre_ffn', 'g_post_ffn', 'w_up', 'conv_w', 'conv_b', 'w_down']
TWIN_DIFF_INPUT = 'x'
TWIN_INPUTS = ['x', 'c', 'w_ada', 'b_ada', 'g_pre_mix', 'g_post_mix', 'w_in', 'ssm_log_dt', 'ssm_a_re', 'ssm_a_im', 'ssm_b_re', 'ssm_b_im', 'ssm_c_re', 'ssm_c_im', 'ssm_d', 'ssm_w_glu', 'ssm_b_glu', 'sgu_ln_g', 'sgu_ln_b', 'sgu_w', 'sgu_b', 'g_out_ssm', 'g_out_sgu', 'w_out', 'g_pre_ffn', 'g_post_ffn', 'w_up', 'conv_w', 'conv_b', 'w_down', 'loss_target', 'm_w_ada', 'm_b_ada', 'm_g_pre_mix', 'm_g_post_mix', 'm_w_in', 'm_ssm_log_dt', 'm_ssm_a_re', 'm_ssm_a_im', 'm_ssm_b_re', 'm_ssm_b_im', 'm_ssm_c_re', 'm_ssm_c_im', 'm_ssm_d', 'm_ssm_w_glu', 'm_ssm_b_glu', 'm_sgu_ln_g', 'm_sgu_ln_b', 'm_sgu_w', 'm_sgu_b', 'm_g_out_ssm', 'm_g_out_sgu', 'm_w_out', 'm_g_pre_ffn', 'm_g_post_ffn', 'm_w_up', 'm_conv_w', 'm_conv_b', 'm_w_down', 'v_w_ada', 'v_b_ada', 'v_g_pre_mix', 'v_g_post_mix', 'v_w_in', 'v_ssm_log_dt', 'v_ssm_a_re', 'v_ssm_a_im', 'v_ssm_b_re', 'v_ssm_b_im', 'v_ssm_c_re', 'v_ssm_c_im', 'v_ssm_d', 'v_ssm_w_glu', 'v_ssm_b_glu', 'v_sgu_ln_g', 'v_sgu_ln_b', 'v_sgu_w', 'v_sgu_b', 'v_g_out_ssm', 'v_g_out_sgu', 'v_w_out', 'v_g_pre_ffn', 'v_g_post_ffn', 'v_w_up', 'v_conv_w', 'v_conv_b', 'v_w_down']
TWIN_OUTPUTS = ['loss', 'grad_x', 'grad_w_ada', 'grad_b_ada', 'grad_g_pre_mix', 'grad_g_post_mix', 'grad_w_in', 'grad_ssm_log_dt', 'grad_ssm_a_re', 'grad_ssm_a_im', 'grad_ssm_b_re', 'grad_ssm_b_im', 'grad_ssm_c_re', 'grad_ssm_c_im', 'grad_ssm_d', 'grad_ssm_w_glu', 'grad_ssm_b_glu', 'grad_sgu_ln_g', 'grad_sgu_ln_b', 'grad_sgu_w', 'grad_sgu_b', 'grad_g_out_ssm', 'grad_g_out_sgu', 'grad_w_out', 'grad_g_pre_ffn', 'grad_g_post_ffn', 'grad_w_up', 'grad_conv_w', 'grad_conv_b', 'grad_w_down', 'delta_w_ada', 'delta_b_ada', 'delta_g_pre_mix', 'delta_g_post_mix', 'delta_w_in', 'delta_ssm_log_dt', 'delta_ssm_a_re', 'delta_ssm_a_im', 'delta_ssm_b_re', 'delta_ssm_b_im', 'delta_ssm_c_re', 'delta_ssm_c_im', 'delta_ssm_d', 'delta_ssm_w_glu', 'delta_ssm_b_glu', 'delta_sgu_ln_g', 'delta_sgu_ln_b', 'delta_sgu_w', 'delta_sgu_b', 'delta_g_out_ssm', 'delta_g_out_sgu', 'delta_w_out', 'delta_g_pre_ffn', 'delta_g_post_ffn', 'delta_w_up', 'delta_conv_w', 'delta_conv_b', 'delta_w_down', 'new_m_w_ada', 'new_m_b_ada', 'new_m_g_pre_mix', 'new_m_g_post_mix', 'new_m_w_in', 'new_m_ssm_log_dt', 'new_m_ssm_a_re', 'new_m_ssm_a_im', 'new_m_ssm_b_re', 'new_m_ssm_b_im', 'new_m_ssm_c_re', 'new_m_ssm_c_im', 'new_m_ssm_d', 'new_m_ssm_w_glu', 'new_m_ssm_b_glu', 'new_m_sgu_ln_g', 'new_m_sgu_ln_b', 'new_m_sgu_w', 'new_m_sgu_b', 'new_m_g_out_ssm', 'new_m_g_out_sgu', 'new_m_w_out', 'new_m_g_pre_ffn', 'new_m_g_post_ffn', 'new_m_w_up', 'new_m_conv_w', 'new_m_conv_b', 'new_m_w_down', 'new_v_w_ada', 'new_v_b_ada', 'new_v_g_pre_mix', 'new_v_g_post_mix', 'new_v_w_in', 'new_v_ssm_log_dt', 'new_v_ssm_a_re', 'new_v_ssm_a_im', 'new_v_ssm_b_re', 'new_v_ssm_b_im', 'new_v_ssm_c_re', 'new_v_ssm_c_im', 'new_v_ssm_d', 'new_v_ssm_w_glu', 'new_v_ssm_b_glu', 'new_v_sgu_ln_g', 'new_v_sgu_ln_b', 'new_v_sgu_w', 'new_v_sgu_b', 'new_v_g_out_ssm', 'new_v_g_out_sgu', 'new_v_w_out', 'new_v_g_pre_ffn', 'new_v_g_post_ffn', 'new_v_w_up', 'new_v_conv_w', 'new_v_conv_b', 'new_v_w_down']
TWIN_LEAF_KINDS = {'loss': 'loss', 'grad_x': 'grad_x', 'grad_w_ada': 'grad_w', 'grad_b_ada': 'grad_w', 'grad_g_pre_mix': 'grad_w', 'grad_g_post_mix': 'grad_w', 'grad_w_in': 'grad_w', 'grad_ssm_log_dt': 'grad_w', 'grad_ssm_a_re': 'grad_w', 'grad_ssm_a_im': 'grad_w', 'grad_ssm_b_re': 'grad_w', 'grad_ssm_b_im': 'grad_w', 'grad_ssm_c_re': 'grad_w', 'grad_ssm_c_im': 'grad_w', 'grad_ssm_d': 'grad_w', 'grad_ssm_w_glu': 'grad_w', 'grad_ssm_b_glu': 'grad_w', 'grad_sgu_ln_g': 'grad_w', 'grad_sgu_ln_b': 'grad_w', 'grad_sgu_w': 'grad_w', 'grad_sgu_b': 'grad_w', 'grad_g_out_ssm': 'grad_w', 'grad_g_out_sgu': 'grad_w', 'grad_w_out': 'grad_w', 'grad_g_pre_ffn': 'grad_w', 'grad_g_post_ffn': 'grad_w', 'grad_w_up': 'grad_w', 'grad_conv_w': 'grad_w', 'grad_conv_b': 'grad_w', 'grad_w_down': 'grad_w', 'delta_w_ada': 'delta_w', 'delta_b_ada': 'delta_w', 'delta_g_pre_mix': 'delta_w', 'delta_g_post_mix': 'delta_w', 'delta_w_in': 'delta_w', 'delta_ssm_log_dt': 'delta_w', 'delta_ssm_a_re': 'delta_w', 'delta_ssm_a_im': 'delta_w', 'delta_ssm_b_re': 'delta_w', 'delta_ssm_b_im': 'delta_w', 'delta_ssm_c_re': 'delta_w', 'delta_ssm_c_im': 'delta_w', 'delta_ssm_d': 'delta_w', 'delta_ssm_w_glu': 'delta_w', 'delta_ssm_b_glu': 'delta_w', 'delta_sgu_ln_g': 'delta_w', 'delta_sgu_ln_b': 'delta_w', 'delta_sgu_w': 'delta_w', 'delta_sgu_b': 'delta_w', 'delta_g_out_ssm': 'delta_w', 'delta_g_out_sgu': 'delta_w', 'delta_w_out': 'delta_w', 'delta_g_pre_ffn': 'delta_w', 'delta_g_post_ffn': 'delta_w', 'delta_w_up': 'delta_w', 'delta_conv_w': 'delta_w', 'delta_conv_b': 'delta_w', 'delta_w_down': 'delta_w', 'new_m_w_ada': 'new_m', 'new_m_b_ada': 'new_m', 'new_m_g_pre_mix': 'new_m', 'new_m_g_post_mix': 'new_m', 'new_m_w_in': 'new_m', 'new_m_ssm_log_dt': 'new_m', 'new_m_ssm_a_re': 'new_m', 'new_m_ssm_a_im': 'new_m', 'new_m_ssm_b_re': 'new_m', 'new_m_ssm_b_im': 'new_m', 'new_m_ssm_c_re': 'new_m', 'new_m_ssm_c_im': 'new_m', 'new_m_ssm_d': 'new_m', 'new_m_ssm_w_glu': 'new_m', 'new_m_ssm_b_glu': 'new_m', 'new_m_sgu_ln_g': 'new_m', 'new_m_sgu_ln_b': 'new_m', 'new_m_sgu_w': 'new_m', 'new_m_sgu_b': 'new_m', 'new_m_g_out_ssm': 'new_m', 'new_m_g_out_sgu': 'new_m', 'new_m_w_out': 'new_m', 'new_m_g_pre_ffn': 'new_m', 'new_m_g_post_ffn': 'new_m', 'new_m_w_up': 'new_m', 'new_m_conv_w': 'new_m', 'new_m_conv_b': 'new_m', 'new_m_w_down': 'new_m', 'new_v_w_ada': 'new_v', 'new_v_b_ada': 'new_v', 'new_v_g_pre_mix': 'new_v', 'new_v_g_post_mix': 'new_v', 'new_v_w_in': 'new_v', 'new_v_ssm_log_dt': 'new_v', 'new_v_ssm_a_re': 'new_v', 'new_v_ssm_a_im': 'new_v', 'new_v_ssm_b_re': 'new_v', 'new_v_ssm_b_im': 'new_v', 'new_v_ssm_c_re': 'new_v', 'new_v_ssm_c_im': 'new_v', 'new_v_ssm_d': 'new_v', 'new_v_ssm_w_glu': 'new_v', 'new_v_ssm_b_glu': 'new_v', 'new_v_sgu_ln_g': 'new_v', 'new_v_sgu_ln_b': 'new_v', 'new_v_sgu_w': 'new_v', 'new_v_sgu_b': 'new_v', 'new_v_g_out_ssm': 'new_v', 'new_v_g_out_sgu': 'new_v', 'new_v_w_out': 'new_v', 'new_v_g_pre_ffn': 'new_v', 'new_v_g_post_ffn': 'new_v', 'new_v_w_up': 'new_v', 'new_v_conv_w': 'new_v', 'new_v_conv_b': 'new_v', 'new_v_w_down': 'new_v'}


def _forward(args):
    return _fwd_reference(*[args[k] for k in FWD_PARAMS])


def _output_shape():
    out = _jax.eval_shape(lambda: _forward(_fwd_setup_inputs(0)))
    return out.shape, out.dtype

N_MICROBATCH = 1
ADAM_LR = 0.001
ADAM_B1 = 0.9
ADAM_B2 = 0.999
ADAM_EPS = 1e-08
ADAM_WD = 0.01
ADAM_STEP = 10
PER_EXAMPLE_BATCH_AXIS = {'x': 0, 'c': 0, 'loss_target': 0}
SHARED_INPUTS = []
_WEIGHT_DTYPES = {'w_ada': _jnp.float32, 'b_ada': _jnp.float32, 'g_pre_mix': _jnp.float32, 'g_post_mix': _jnp.float32, 'w_in': _jnp.float32, 'ssm_log_dt': _jnp.float32, 'ssm_a_re': _jnp.float32, 'ssm_a_im': _jnp.float32, 'ssm_b_re': _jnp.float32, 'ssm_b_im': _jnp.float32, 'ssm_c_re': _jnp.float32, 'ssm_c_im': _jnp.float32, 'ssm_d': _jnp.float32, 'ssm_w_glu': _jnp.float32, 'ssm_b_glu': _jnp.float32, 'sgu_ln_g': _jnp.float32, 'sgu_ln_b': _jnp.float32, 'sgu_w': _jnp.float32, 'sgu_b': _jnp.float32, 'g_out_ssm': _jnp.float32, 'g_out_sgu': _jnp.float32, 'w_out': _jnp.float32, 'g_pre_ffn': _jnp.float32, 'g_post_ffn': _jnp.float32, 'w_up': _jnp.float32, 'conv_w': _jnp.float32, 'conv_b': _jnp.float32, 'w_down': _jnp.float32}
MOMENT_SCALE = {'w_ada': 4.341585e-01, 'b_ada': 8.223825e-01, 'g_pre_mix': 3.780720e-02, 'g_post_mix': 9.342699e-01, 'w_in': 4.279091e-02, 'ssm_log_dt': 1.101773e+00, 'ssm_a_re': 6.261303e-03, 'ssm_a_im': 5.968596e-03, 'ssm_b_re': 3.759050e-03, 'ssm_b_im': 4.054137e-03, 'ssm_c_re': 7.471927e-03, 'ssm_c_im': 7.785991e-03, 'ssm_d': 1.327874e-01, 'ssm_w_glu': 3.598186e-02, 'ssm_b_glu': 5.176631e-02, 'sgu_ln_g': 1.758642e-02, 'sgu_ln_b': 1.797745e-02, 'sgu_w': 1.703017e-02, 'sgu_b': 2.489543e-02, 'g_out_ssm': 1.179771e-01, 'g_out_sgu': 9.986681e-02, 'w_out': 1.071414e-01, 'g_pre_ffn': 3.996426e-02, 'g_post_ffn': 9.006153e-01, 'w_up': 2.008311e-02, 'conv_w': 2.096536e-02, 'conv_b': 3.350580e-02, 'w_down': 3.670152e-02}


def _to_microbatches(a, axis):
    t = _jnp.moveaxis(a, axis, 0)
    t = t.reshape((N_MICROBATCH, t.shape[0] // N_MICROBATCH) + t.shape[1:])
    return _jnp.moveaxis(t, 1, axis + 1)


def setup_inputs(seed: int = 0) -> dict:
    inp = _fwd_setup_inputs(seed)
    key = _jax.random.fold_in(_jax.random.key(seed), 7919)
    shape, _ = _output_shape()
    out = dict(inp)
    out["loss_target"] = _jax.random.normal(_jax.random.fold_in(key, 0), shape, _jnp.float32)
    for i, name in enumerate(TWIN_WEIGHTS):
        w = inp[name].astype(_jnp.float32)
        if MOMENT_SCALE is None:
            s = _jnp.sqrt(_jnp.mean(_jnp.square(w)) + 1e-30)
        else:
            s = MOMENT_SCALE[name]
        km, kv = _jax.random.split(_jax.random.fold_in(key, i + 1))
        out[name] = w
        out["m_" + name] = s * _jax.random.normal(km, w.shape, _jnp.float32)
        out["v_" + name] = (s * s) * _jax.random.uniform(kv, w.shape, _jnp.float32, 0.5, 1.5)
    if N_MICROBATCH > 1:
        for name, axis in PER_EXAMPLE_BATCH_AXIS.items():
            out[name] = _to_microbatches(out[name], axis)
    return {'x': out['x'], 'c': out['c'], 'w_ada': out['w_ada'], 'b_ada': out['b_ada'], 'g_pre_mix': out['g_pre_mix'], 'g_post_mix': out['g_post_mix'], 'w_in': out['w_in'], 'ssm_log_dt': out['ssm_log_dt'], 'ssm_a_re': out['ssm_a_re'], 'ssm_a_im': out['ssm_a_im'], 'ssm_b_re': out['ssm_b_re'], 'ssm_b_im': out['ssm_b_im'], 'ssm_c_re': out['ssm_c_re'], 'ssm_c_im': out['ssm_c_im'], 'ssm_d': out['ssm_d'], 'ssm_w_glu': out['ssm_w_glu'], 'ssm_b_glu': out['ssm_b_glu'], 'sgu_ln_g': out['sgu_ln_g'], 'sgu_ln_b': out['sgu_ln_b'], 'sgu_w': out['sgu_w'], 'sgu_b': out['sgu_b'], 'g_out_ssm': out['g_out_ssm'], 'g_out_sgu': out['g_out_sgu'], 'w_out': out['w_out'], 'g_pre_ffn': out['g_pre_ffn'], 'g_post_ffn': out['g_post_ffn'], 'w_up': out['w_up'], 'conv_w': out['conv_w'], 'conv_b': out['conv_b'], 'w_down': out['w_down'], 'loss_target': out['loss_target'], 'm_w_ada': out['m_w_ada'], 'm_b_ada': out['m_b_ada'], 'm_g_pre_mix': out['m_g_pre_mix'], 'm_g_post_mix': out['m_g_post_mix'], 'm_w_in': out['m_w_in'], 'm_ssm_log_dt': out['m_ssm_log_dt'], 'm_ssm_a_re': out['m_ssm_a_re'], 'm_ssm_a_im': out['m_ssm_a_im'], 'm_ssm_b_re': out['m_ssm_b_re'], 'm_ssm_b_im': out['m_ssm_b_im'], 'm_ssm_c_re': out['m_ssm_c_re'], 'm_ssm_c_im': out['m_ssm_c_im'], 'm_ssm_d': out['m_ssm_d'], 'm_ssm_w_glu': out['m_ssm_w_glu'], 'm_ssm_b_glu': out['m_ssm_b_glu'], 'm_sgu_ln_g': out['m_sgu_ln_g'], 'm_sgu_ln_b': out['m_sgu_ln_b'], 'm_sgu_w': out['m_sgu_w'], 'm_sgu_b': out['m_sgu_b'], 'm_g_out_ssm': out['m_g_out_ssm'], 'm_g_out_sgu': out['m_g_out_sgu'], 'm_w_out': out['m_w_out'], 'm_g_pre_ffn': out['m_g_pre_ffn'], 'm_g_post_ffn': out['m_g_post_ffn'], 'm_w_up': out['m_w_up'], 'm_conv_w': out['m_conv_w'], 'm_conv_b': out['m_conv_b'], 'm_w_down': out['m_w_down'], 'v_w_ada': out['v_w_ada'], 'v_b_ada': out['v_b_ada'], 'v_g_pre_mix': out['v_g_pre_mix'], 'v_g_post_mix': out['v_g_post_mix'], 'v_w_in': out['v_w_in'], 'v_ssm_log_dt': out['v_ssm_log_dt'], 'v_ssm_a_re': out['v_ssm_a_re'], 'v_ssm_a_im': out['v_ssm_a_im'], 'v_ssm_b_re': out['v_ssm_b_re'], 'v_ssm_b_im': out['v_ssm_b_im'], 'v_ssm_c_re': out['v_ssm_c_re'], 'v_ssm_c_im': out['v_ssm_c_im'], 'v_ssm_d': out['v_ssm_d'], 'v_ssm_w_glu': out['v_ssm_w_glu'], 'v_ssm_b_glu': out['v_ssm_b_glu'], 'v_sgu_ln_g': out['v_sgu_ln_g'], 'v_sgu_ln_b': out['v_sgu_ln_b'], 'v_sgu_w': out['v_sgu_w'], 'v_sgu_b': out['v_sgu_b'], 'v_g_out_ssm': out['v_g_out_ssm'], 'v_g_out_sgu': out['v_g_out_sgu'], 'v_w_out': out['v_w_out'], 'v_g_pre_ffn': out['v_g_pre_ffn'], 'v_g_post_ffn': out['v_g_post_ffn'], 'v_w_up': out['v_w_up'], 'v_conv_w': out['v_conv_w'], 'v_conv_b': out['v_conv_b'], 'v_w_down': out['v_w_down']}


def _loss(weights, diff, rest, loss_target):
    with _jax.named_scope("forward"):
        args = {**rest, TWIN_DIFF_INPUT: diff, **{k: w.astype(_WEIGHT_DTYPES[k]) for k, w in weights.items()}}
        y = _forward(args)
    with _jax.named_scope("loss_head"):
        err = _jnp.square(y.astype(_jnp.float32) - loss_target)
        return 0.5 * _jnp.sum(_jnp.mean(err, axis=-1)) if err.ndim else 0.5 * err


def _adamw(w, g, m, v):
    m = ADAM_B1 * m + (1.0 - ADAM_B1) * g
    v = ADAM_B2 * v + (1.0 - ADAM_B2) * _jnp.square(g)
    m_hat = m / (1.0 - ADAM_B1 ** ADAM_STEP)
    v_hat = v / (1.0 - ADAM_B2 ** ADAM_STEP)
    delta = -ADAM_LR * (m_hat / (_jnp.sqrt(v_hat) + ADAM_EPS) + ADAM_WD * w)
    return delta, m, v


def reference(x, c, w_ada, b_ada, g_pre_mix, g_post_mix, w_in, ssm_log_dt, ssm_a_re, ssm_a_im, ssm_b_re, ssm_b_im, ssm_c_re, ssm_c_im, ssm_d, ssm_w_glu, ssm_b_glu, sgu_ln_g, sgu_ln_b, sgu_w, sgu_b, g_out_ssm, g_out_sgu, w_out, g_pre_ffn, g_post_ffn, w_up, conv_w, conv_b, w_down, loss_target, m_w_ada, m_b_ada, m_g_pre_mix, m_g_post_mix, m_w_in, m_ssm_log_dt, m_ssm_a_re, m_ssm_a_im, m_ssm_b_re, m_ssm_b_im, m_ssm_c_re, m_ssm_c_im, m_ssm_d, m_ssm_w_glu, m_ssm_b_glu, m_sgu_ln_g, m_sgu_ln_b, m_sgu_w, m_sgu_b, m_g_out_ssm, m_g_out_sgu, m_w_out, m_g_pre_ffn, m_g_post_ffn, m_w_up, m_conv_w, m_conv_b, m_w_down, v_w_ada, v_b_ada, v_g_pre_mix, v_g_post_mix, v_w_in, v_ssm_log_dt, v_ssm_a_re, v_ssm_a_im, v_ssm_b_re, v_ssm_b_im, v_ssm_c_re, v_ssm_c_im, v_ssm_d, v_ssm_w_glu, v_ssm_b_glu, v_sgu_ln_g, v_sgu_ln_b, v_sgu_w, v_sgu_b, v_g_out_ssm, v_g_out_sgu, v_w_out, v_g_pre_ffn, v_g_post_ffn, v_w_up, v_conv_w, v_conv_b, v_w_down):
    given = dict(x=x, c=c, w_ada=w_ada, b_ada=b_ada, g_pre_mix=g_pre_mix, g_post_mix=g_post_mix, w_in=w_in, ssm_log_dt=ssm_log_dt, ssm_a_re=ssm_a_re, ssm_a_im=ssm_a_im, ssm_b_re=ssm_b_re, ssm_b_im=ssm_b_im, ssm_c_re=ssm_c_re, ssm_c_im=ssm_c_im, ssm_d=ssm_d, ssm_w_glu=ssm_w_glu, ssm_b_glu=ssm_b_glu, sgu_ln_g=sgu_ln_g, sgu_ln_b=sgu_ln_b, sgu_w=sgu_w, sgu_b=sgu_b, g_out_ssm=g_out_ssm, g_out_sgu=g_out_sgu, w_out=w_out, g_pre_ffn=g_pre_ffn, g_post_ffn=g_post_ffn, w_up=w_up, conv_w=conv_w, conv_b=conv_b, w_down=w_down, loss_target=loss_target, m_w_ada=m_w_ada, m_b_ada=m_b_ada, m_g_pre_mix=m_g_pre_mix, m_g_post_mix=m_g_post_mix, m_w_in=m_w_in, m_ssm_log_dt=m_ssm_log_dt, m_ssm_a_re=m_ssm_a_re, m_ssm_a_im=m_ssm_a_im, m_ssm_b_re=m_ssm_b_re, m_ssm_b_im=m_ssm_b_im, m_ssm_c_re=m_ssm_c_re, m_ssm_c_im=m_ssm_c_im, m_ssm_d=m_ssm_d, m_ssm_w_glu=m_ssm_w_glu, m_ssm_b_glu=m_ssm_b_glu, m_sgu_ln_g=m_sgu_ln_g, m_sgu_ln_b=m_sgu_ln_b, m_sgu_w=m_sgu_w, m_sgu_b=m_sgu_b, m_g_out_ssm=m_g_out_ssm, m_g_out_sgu=m_g_out_sgu, m_w_out=m_w_out, m_g_pre_ffn=m_g_pre_ffn, m_g_post_ffn=m_g_post_ffn, m_w_up=m_w_up, m_conv_w=m_conv_w, m_conv_b=m_conv_b, m_w_down=m_w_down, v_w_ada=v_w_ada, v_b_ada=v_b_ada, v_g_pre_mix=v_g_pre_mix, v_g_post_mix=v_g_post_mix, v_w_in=v_w_in, v_ssm_log_dt=v_ssm_log_dt, v_ssm_a_re=v_ssm_a_re, v_ssm_a_im=v_ssm_a_im, v_ssm_b_re=v_ssm_b_re, v_ssm_b_im=v_ssm_b_im, v_ssm_c_re=v_ssm_c_re, v_ssm_c_im=v_ssm_c_im, v_ssm_d=v_ssm_d, v_ssm_w_glu=v_ssm_w_glu, v_ssm_b_glu=v_ssm_b_glu, v_sgu_ln_g=v_sgu_ln_g, v_sgu_ln_b=v_sgu_ln_b, v_sgu_w=v_sgu_w, v_sgu_b=v_sgu_b, v_g_out_ssm=v_g_out_ssm, v_g_out_sgu=v_g_out_sgu, v_w_out=v_w_out, v_g_pre_ffn=v_g_pre_ffn, v_g_post_ffn=v_g_post_ffn, v_w_up=v_w_up, v_conv_w=v_conv_w, v_conv_b=v_conv_b, v_w_down=v_w_down)
    weights = {n: given[n] for n in TWIN_WEIGHTS}
    shared = {n: given[n] for n in SHARED_INPUTS}
    per_example = {n: given[n] for n in ['x', 'c']}
    grad_fn = _jax.value_and_grad(_loss, argnums=(0, 1))

    def one_microbatch(ex, loss_target):
        ex = dict(ex)
        diff = ex.pop(TWIN_DIFF_INPUT)
        return grad_fn(weights, diff, {**shared, **ex}, loss_target)

    if N_MICROBATCH == 1:
        loss, (grad_w, grad_x) = one_microbatch(per_example, given["loss_target"])
    else:
        def body(carry, xs):
            loss_sum, grad_sum = carry
            l_k, (gw_k, gx_k) = one_microbatch(xs[0], xs[1])
            with _jax.named_scope("update"):
                return (loss_sum + l_k, _jax.tree.map(_jnp.add, grad_sum, gw_k)), gx_k

        init = (_jnp.zeros((), _jnp.float32), _jax.tree.map(_jnp.zeros_like, weights))
        (loss, grad_w), grad_x = _jax.lax.scan(body, init, (per_example, given["loss_target"]))
    with _jax.named_scope("update"):
        delta_w, new_m, new_v = {}, {}, {}
        for n in TWIN_WEIGHTS:
            delta_w[n], new_m[n], new_v[n] = _adamw(weights[n], grad_w[n], given["m_" + n], given["v_" + n])
    return (loss, grad_x, *[grad_w[n] for n in TWIN_WEIGHTS], *[delta_w[n] for n in TWIN_WEIGHTS],
            *[new_m[n] for n in TWIN_WEIGHTS], *[new_v[n] for n in TWIN_WEIGHTS])
```

```python
import functools
import math

import jax
import jax.numpy as jnp
from jax import lax
from jax.experimental import pallas as pl
from jax.experimental.pallas import tpu as pltpu

_F32 = jnp.float32
_MXU = jnp.bfloat16
_WIRE = jnp.bfloat16

EPS = 1e-6
SSM_GROUP = 16
SSM_STATE = 64
GROUPS_PER_BLOCK = 8
BLOCK_CH = SSM_GROUP * GROUPS_PER_BLOCK
BLOCK_ST = SSM_STATE * GROUPS_PER_BLOCK
CHUNK = 128
TIME_TILE = 256
SUBLANES = 8
LANES = 128
N_MOD = 6
ADAM_LR, ADAM_B1, ADAM_B2, ADAM_EPS, ADAM_WD, ADAM_STEP = 0.001, 0.9, 0.999, 1e-08, 0.01, 10
_VMEM_LIMIT = 56 * 1024 * 1024
_MESH = pl.DeviceIdType.MESH
_ANY = pl.BlockSpec(memory_space=pl.ANY)
_GELU_C = math.sqrt(2.0 / math.pi)


def _cp(*sem):
    return pltpu.CompilerParams(dimension_semantics=sem, vmem_limit_bytes=_VMEM_LIMIT)


def _tile(dim, target, align=LANES):
    if dim <= target:
        return dim
    best = None
    for t in range(align, target + 1, align):
        if dim % t == 0:
            best = t
    assert best is not None, (dim, target, align)
    return best


def _gelu(x):
    return 0.5 * x * (1.0 + jnp.tanh(_GELU_C * (x + 0.044715 * (x * x * x))))


def _gelu_grad(x):
    t = jnp.tanh(_GELU_C * (x + 0.044715 * (x * x * x)))
    return 0.5 * (1.0 + t) + 0.5 * x * (1.0 - t * t) * (_GELU_C * (1.0 + 3.0 * 0.044715 * x * x))


def _sigmoid(x):
    return 1.0 / (1.0 + jnp.exp(-x))


def _colsum(x):
    return jnp.sum(x, axis=0, keepdims=True)


def _rowmean(x):
    return jnp.mean(x, axis=-1, keepdims=True)


def _acc(ref, first, val):
    @pl.when(first)
    def _():
        ref[...] = val

    @pl.when(jnp.logical_not(first))
    def _():
        ref[...] += val


def _place():
    mx, my, mc = lax.axis_index("x"), lax.axis_index("y"), lax.axis_index("c")
    chips = [(1 - mx, my), (mx, 1 - my), (1 - mx, 1 - my)]
    return mx, my, mc, chips


def _all_gather8(x, name):
    r, c = x.shape

    def body(x_ref, out_ref, send_sems, recv_sems, local_sem):
        mx, my, mc, chips = _place()
        me, sibling = (mx, my, mc), (mx, my, 1 - mc)

        def slot(px, py, pc):
            return out_ref.at[4 * px + 2 * py + pc]

        def copy(k, block, to, src=None):
            return pltpu.make_async_remote_copy(
                src_ref=slot(*block) if src is None else src, dst_ref=slot(*block),
                send_sem=send_sems.at[k], recv_sem=recv_sems.at[k], device_id=to, device_id_type=_MESH)

        mine = pltpu.make_async_copy(x_ref, slot(*me), local_sem)
        mine.start()
        first = [copy(0, me, sibling, src=x_ref)]
        first += [copy(1 + j, me, (*chip, mc), src=x_ref) for j, chip in enumerate(chips)]
        for cp in first:
            cp.start()
        passed = [copy(4 + j, (*chip, mc), sibling) for j, chip in enumerate(chips)]
        for j, chip in enumerate(chips):
            copy(1 + j, (*chip, mc), me).wait_recv()
            passed[j].start()
        copy(0, sibling, me).wait_recv()
        for j, chip in enumerate(chips):
            copy(4 + j, (*chip, 1 - mc), me).wait_recv()
        for cp in first + passed:
            cp.wait_send()
        mine.wait()

    return pl.pallas_call(
        body, name=name, out_shape=jax.ShapeDtypeStruct((8, r, c), x.dtype),
        in_specs=[_ANY], out_specs=_ANY,
        scratch_shapes=[pltpu.SemaphoreType.DMA((7,)), pltpu.SemaphoreType.DMA((7,)), pltpu.SemaphoreType.DMA(())],
    )(x)


def _gather_weights(shards):
    n = len(shards)

    def body(*refs):
        ins, outs = refs[:n], refs[n:2 * n]
        send_sems, recv_sems, local_sems = refs[2 * n:]
        mx, my, mc, chips = _place()
        sibling = (mx, my, 1 - mc)
        j_me = 2 * mx + my
        sends = []
        for i in range(n):
            mine = pltpu.make_async_copy(ins[i], outs[i].at[j_me], local_sems.at[i])
            mine.start()
            sends.append(mine)
        remote = []
        for i in range(n):
            for k, chip in enumerate(chips):
                cp = pltpu.make_async_remote_copy(
                    src_ref=ins[i].at[mc], dst_ref=outs[i].at[j_me, mc],
                    send_sem=send_sems.at[6 * i + k], recv_sem=recv_sems.at[6 * i + k],
                    device_id=(*chip, mc), device_id_type=_MESH)
                cp.start()
                remote.append(cp)
        for i in range(n):
            for k, chip in enumerate(chips):
                j_k = 2 * chip[0] + chip[1]
                landed = outs[i].at[j_k, mc]
                pltpu.make_async_remote_copy(
                    src_ref=landed, dst_ref=landed, send_sem=send_sems.at[6 * i + k], recv_sem=recv_sems.at[6 * i + k],
                    device_id=(*chip, mc), device_id_type=_MESH).wait_recv()
                cp = pltpu.make_async_remote_copy(
                    src_ref=landed, dst_ref=landed, send_sem=send_sems.at[6 * i + 3 + k],
                    recv_sem=recv_sems.at[6 * i + 3 + k], device_id=sibling, device_id_type=_MESH)
                cp.start()
                remote.append(cp)
        for i in range(n):
            for k, chip in enumerate(chips):
                j_k = 2 * chip[0] + chip[1]
                other = outs[i].at[j_k, 1 - mc]
                pltpu.make_async_remote_copy(
                    src_ref=other, dst_ref=other, send_sem=send_sems.at[6 * i + 3 + k],
                    recv_sem=recv_sems.at[6 * i + 3 + k], device_id=sibling, device_id_type=_MESH).wait_recv()
        for cp in remote:
            cp.wait_send()
        for cp in sends:
            cp.wait()

    return pl.pallas_call(
        body, name="gather_weights",
        out_shape=[jax.ShapeDtypeStruct((4,) + s.shape, s.dtype) for s in shards],
        in_specs=[_ANY] * n, out_specs=[_ANY] * n,
        scratch_shapes=[pltpu.SemaphoreType.DMA((6 * n,)), pltpu.SemaphoreType.DMA((6 * n,)),
                        pltpu.SemaphoreType.DMA((n,))],
    )(*shards)


def _pair_swap(arrs, name):
    n = len(arrs)

    def body(*refs):
        ins, outs = refs[:n], refs[n:2 * n]
        send_sems, recv_sems = refs[2 * n:]
        mx, my, mc, _ = _place()
        sibling = (mx, my, 1 - mc)
        cps = []
        for i in range(n):
            cp = pltpu.make_async_remote_copy(
                src_ref=ins[i].at[1 - mc], dst_ref=outs[i], send_sem=send_sems.at[i], recv_sem=recv_sems.at[i],
                device_id=sibling, device_id_type=_MESH)
            cp.start()
            cps.append(cp)
        for cp in cps:
            cp.wait()

    return pl.pallas_call(
        body, name=name, out_shape=[jax.ShapeDtypeStruct(a.shape[1:], a.dtype) for a in arrs],
        in_specs=[_ANY] * n, out_specs=[_ANY] * n,
        scratch_shapes=[pltpu.SemaphoreType.DMA((n,)), pltpu.SemaphoreType.DMA((n,))],
    )(*arrs)


def _chip_scatter(arrs):
    n = len(arrs)

    def body(*refs):
        ins, outs = refs[:n], refs[n:2 * n]
        send_sems, recv_sems = refs[2 * n:]
        mx, my, mc, chips = _place()
        cps = []
        for i in range(n):
            for k, chip in enumerate(chips):
                cp = pltpu.make_async_remote_copy(
                    src_ref=ins[i].at[2 * chip[0] + chip[1]], dst_ref=outs[i].at[k],
                    send_sem=send_sems.at[3 * i + k], recv_sem=recv_sems.at[3 * i + k],
                    device_id=(*chip, mc), device_id_type=_MESH)
                cp.start()
                cps.append(cp)
        for cp in cps:
            cp.wait()

    return pl.pallas_call(
        body, name="chip_scatter", out_shape=[jax.ShapeDtypeStruct((3,) + a.shape[1:], a.dtype) for a in arrs],
        in_specs=[_ANY] * n, out_specs=[_ANY] * n,
        scratch_shapes=[pltpu.SemaphoreType.DMA((3 * n,)), pltpu.SemaphoreType.DMA((3 * n,))],
    )(*arrs)


def _pair_join(halves):
    n = len(halves)

    def body(*refs):
        ins, outs = refs[:n], refs[n:2 * n]
        send_sems, recv_sems, local_sems = refs[2 * n:]
        mx, my, mc, _ = _place()
        sibling = (mx, my, 1 - mc)
        cps, loc = [], []
        for i in range(n):
            lc = pltpu.make_async_copy(ins[i], outs[i].at[mc], local_sems.at[i])
            lc.start()
            loc.append(lc)
            cp = pltpu.make_async_remote_copy(
                src_ref=ins[i], dst_ref=outs[i].at[mc], send_sem=send_sems.at[i], recv_sem=recv_sems.at[i],
                device_id=sibling, device_id_type=_MESH)
            cp.start()
            cps.append(cp)
        for i in range(n):
            pltpu.make_async_remote_copy(
                src_ref=ins[i], dst_ref=outs[i].at[1 - mc], send_sem=send_sems.at[i], recv_sem=recv_sems.at[i],
                device_id=sibling, device_id_type=_MESH).wait_recv()
        for cp in cps:
            cp.wait_send()
        for lc in loc:
            lc.wait()

    return pl.pallas_call(
        body, name="pair_join", out_shape=[jax.ShapeDtypeStruct((2,) + h.shape, h.dtype) for h in halves],
        in_specs=[_ANY] * n, out_specs=[_ANY] * n,
        scratch_shapes=[pltpu.SemaphoreType.DMA((n,)), pltpu.SemaphoreType.DMA((n,)), pltpu.SemaphoreType.DMA((n,))],
    )(*halves)


def _pair_sum(g, got, sel, name):
    _, four, hr, c = g.shape
    tr = _tile(hr, 512, 16)

    def body(sel_ref, g_ref, p_ref, o_ref):
        o_ref[...] = (g_ref[...].astype(_F32) + p_ref[...].astype(_F32)).astype(o_ref.dtype)

    return pl.pallas_call(
        body, name=name, out_shape=jax.ShapeDtypeStruct((four, hr, c), g.dtype),
        grid_spec=pltpu.PrefetchScalarGridSpec(
            num_scalar_prefetch=1, grid=(four, hr // tr),
            in_specs=[pl.BlockSpec((None, None, tr, c), lambda j, i, s: (s[0], j, i, 0)),
                      pl.BlockSpec((None, tr, c), lambda j, i, s: (j, i, 0))],
            out_specs=pl.BlockSpec((None, tr, c), lambda j, i, s: (j, i, 0))),
        compiler_params=_cp("parallel", "parallel"),
    )(sel, g, got)


def _chip_sum(pair, got, sel, name):
    _, hr, c = pair.shape
    tr = _tile(hr, 512, 16)

    def body(sel_ref, p_ref, q_ref, o_ref):
        o_ref[...] = ((p_ref[...].astype(_F32) + q_ref[0].astype(_F32)) + q_ref[1].astype(_F32)) + q_ref[2].astype(_F32)

    return pl.pallas_call(
        body, name=name, out_shape=jax.ShapeDtypeStruct((hr, c), _F32),
        grid_spec=pltpu.PrefetchScalarGridSpec(
            num_scalar_prefetch=1, grid=(hr // tr,),
            in_specs=[pl.BlockSpec((None, tr, c), lambda i, s: (s[0], i, 0)),
                      pl.BlockSpec((3, tr, c), lambda i, s: (0, i, 0))],
            out_specs=pl.BlockSpec((tr, c), lambda i, s: (i, 0))),
        compiler_params=_cp("parallel"),
    )(sel, pair, got)


def _matmul(a, b, dims, out_struct, grid, a_spec, b_spec, o_spec, acc_shape, k_axis, name):
    nk = grid[k_axis]

    def body(a_ref, b_ref, o_ref, acc_ref):
        prod = lax.dot_general(a_ref[...].astype(_MXU), b_ref[...].astype(_MXU), dims, preferred_element_type=_F32)
        if nk == 1:
            o_ref[...] = prod.astype(o_ref.dtype)
        else:
            k = pl.program_id(k_axis)

            @pl.when(k == 0)
            def _():
                acc_ref[...] = prod

            @pl.when(k > 0)
            def _():
                acc_ref[...] += prod

            @pl.when(k == nk - 1)
            def _():
                o_ref[...] = acc_ref[...].astype(o_ref.dtype)

    sem = ["parallel"] * len(grid)
    sem[k_axis] = "arbitrary"
    return pl.pallas_call(
        body, name=name, out_shape=out_struct, grid=grid, in_specs=[a_spec, b_spec], out_specs=o_spec,
        scratch_shapes=[pltpu.VMEM(acc_shape, _F32)], compiler_params=_cp(*sem),
    )(a, b)


def _mm_nn(a, w4, out_dtype, name, tm=512, tn=1536, tk=2048):
    m, k = a.shape
    j, _, ns = w4.shape
    tm, tn, tk = _tile(m, tm, 16), _tile(ns, tn), _tile(k, tk)
    nps = ns // tn
    return _matmul(
        a, w4, (((1,), (0,)), ((), ())), jax.ShapeDtypeStruct((m, j * ns), out_dtype),
        (m // tm, j * nps, k // tk),
        pl.BlockSpec((tm, tk), lambda mi, ni, ki: (mi, ki)),
        pl.BlockSpec((None, tk, tn), lambda mi, ni, ki: (ni // nps, ki, ni % nps)),
        pl.BlockSpec((tm, tn), lambda mi, ni, ki: (mi, ni)), (tm, tn), 2, name)


def _mm_nt(a, w4, out_dtype, name, tm=512, tn=2048, tk=1536):
    m, _ = a.shape
    j, kw, ns = w4.shape
    tm, tn, tk = _tile(m, tm, 16), _tile(kw, tn), _tile(ns, tk)
    kps = ns // tk
    return _matmul(
        a, w4, (((1,), (1,)), ((), ())), jax.ShapeDtypeStruct((m, kw), out_dtype),
        (m // tm, kw // tn, j * kps),
        pl.BlockSpec((tm, tk), lambda mi, ni, ki: (mi, ki)),
        pl.BlockSpec((None, tn, tk), lambda mi, ni, ki: (ki // kps, ni, ki % kps)),
        pl.BlockSpec((tm, tn), lambda mi, ni, ki: (mi, ni)), (tm, tn), 2, name)


def _mm_tn_cols(a, b, name, tm=1024, tn=1536, tk=2048):
    m, ka = a.shape
    ns = b.shape[1] // 4
    hr = ka // 2
    tm, tn, tk = _tile(hr, tm), _tile(ns, tn), _tile(m, tk, 16)
    mph, nps = hr // tm, ns // tn
    return _matmul(
        a, b, (((0,), (0,)), ((), ())), jax.ShapeDtypeStruct((2, 4, hr, ns), _WIRE),
        (4 * nps, 2 * mph, m // tk),
        pl.BlockSpec((tk, tm), lambda ni, mi, ki: (ki, mi)),
        pl.BlockSpec((tk, tn), lambda ni, mi, ki: (ki, ni)),
        pl.BlockSpec((None, None, tm, tn), lambda ni, mi, ki: (mi // mph, ni // nps, mi % mph, ni % nps)),
        (tm, tn), 2, name)


def _mm_tn_rows(a, b, name, tm=1536, tn=1024, tk=2048):
    m, ka = a.shape
    r = ka // 4
    hc = b.shape[1] // 2
    tm, tn, tk = _tile(r, tm), _tile(hc, tn), _tile(m, tk, 16)
    mpr, nph = r // tm, hc // tn
    return _matmul(
        a, b, (((0,), (0,)), ((), ())), jax.ShapeDtypeStruct((2, 4, r, hc), _WIRE),
        (2 * nph, 4 * mpr, m // tk),
        pl.BlockSpec((tk, tm), lambda ni, mi, ki: (ki, mi)),
        pl.BlockSpec((tk, tn), lambda ni, mi, ki: (ki, ni)),
        pl.BlockSpec((None, None, tm, tn), lambda ni, mi, ki: (ni // nph, mi // mpr, mi % mpr, ni % nph)),
        (tm, tn), 2, name)


def _row_call(body, name, rows, ins, outs, tm=256):
    tm = _tile(rows, tm, 16)

    def spec(shape, kind):
        if kind == "rows":
            return pl.BlockSpec((tm, shape[1]), lambda i: (i, 0))
        return pl.BlockSpec(shape, lambda i: (0,) * len(shape))

    return pl.pallas_call(
        body, name=name, grid=(rows // tm,),
        in_specs=[spec(a.shape, kind) for a, kind in ins],
        out_specs=[spec(o.shape, kind) for o, kind in outs],
        out_shape=[o for o, _ in outs],
        compiler_params=_cp("arbitrary"),
    )(*[a for a, _ in ins])


def _rms(x):
    r = lax.rsqrt(_rowmean(x * x) + EPS)
    return x * r, r


def _rms_bwd(dxh, xh, r):
    return r * (dxh - xh * _rowmean(dxh * xh))


def _fwd_pre_mix(x, g, sc, sh):
    s, d = x.shape

    def body(x_ref, g_ref, sc_ref, sh_ref, h_ref):
        xh, _ = _rms(x_ref[...])
        h_ref[...] = (xh * g_ref[...] * (1.0 + sc_ref[...]) + sh_ref[...]).astype(h_ref.dtype)

    return _row_call(body, "fwd_pre_mix", s, [(x, "rows"), (g, "vec"), (sc, "vec"), (sh, "vec")],
                     [(jax.ShapeDtypeStruct((s, d), _MXU), "rows")])[0]


def _fwd_mid(o, x, gt1, g_post, g_pre2, sc2, sh2):
    s, d = x.shape

    def body(o_ref, x_ref, gt_ref, gp_ref, g2_ref, sc_ref, sh_ref, x1_ref, h2_ref):
        oh, _ = _rms(o_ref[...])
        x1 = x_ref[...] + gt_ref[...] * (oh * gp_ref[...])
        x1_ref[...] = x1
        xh, _ = _rms(x1)
        h2_ref[...] = (xh * g2_ref[...] * (1.0 + sc_ref[...]) + sh_ref[...]).astype(h2_ref.dtype)

    return _row_call(body, "fwd_mid", s,
                     [(o, "rows"), (x, "rows"), (gt1, "vec"), (g_post, "vec"), (g_pre2, "vec"), (sc2, "vec"),
                      (sh2, "vec")],
                     [(jax.ShapeDtypeStruct((s, d), _F32), "rows"), (jax.ShapeDtypeStruct((s, d), _MXU), "rows")])


def _loss_and_post_ffn_bwd(f, x1, tgt, gt2, g_post):
    s, d = x1.shape

    def body(f_ref, x1_ref, t_ref, gt_ref, g_ref, dx2_ref, df_ref, dgt_ref, dg_ref, loss_ref):
        first = pl.program_id(0) == 0
        fh, r = _rms(f_ref[...])
        n = fh * g_ref[...]
        e = x1_ref[...] + gt_ref[...] * n - t_ref[...]
        _acc(loss_ref, first, jnp.sum(_colsum(e * e), axis=1, keepdims=True) * (0.5 / d))
        dx2 = e * (1.0 / d)
        dx2_ref[...] = dx2
        _acc(dgt_ref, first, _colsum(dx2 * n))
        dn = dx2 * gt_ref[...]
        _acc(dg_ref, first, _colsum(dn * fh))
        df_ref[...] = _rms_bwd(dn * g_ref[...], fh, r).astype(df_ref.dtype)

    vec = jax.ShapeDtypeStruct((1, d), _F32)
    return _row_call(body, "loss_post_ffn_bwd", s,
                     [(f, "rows"), (x1, "rows"), (tgt, "rows"), (gt2, "vec"), (g_post, "vec")],
                     [(jax.ShapeDtypeStruct((s, d), _F32), "rows"), (jax.ShapeDtypeStruct((s, d), _MXU), "rows"),
                      (vec, "vec"), (vec, "vec"), (jax.ShapeDtypeStruct((1, 1), _F32), "vec")])


def _bwd_mid(dh2, x1, dx2, o, g_pre2, sc2, gt1, g_post):
    s, d = x1.shape

    def body(dh_ref, x1_ref, dx2_ref, o_ref, g2_ref, sc_ref, gt_ref, gp_ref,
             dx1_ref, do_ref, dsc_ref, dsh_ref, dg2_ref, dgt_ref, dgp_ref):
        first = pl.program_id(0) == 0
        dh = dh_ref[...]
        xh, r = _rms(x1_ref[...])
        _acc(dsh_ref, first, _colsum(dh))
        _acc(dsc_ref, first, _colsum(dh * (xh * g2_ref[...])))
        dn = dh * (1.0 + sc_ref[...])
        _acc(dg2_ref, first, _colsum(dn * xh))
        dx1 = dx2_ref[...] + _rms_bwd(dn * g2_ref[...], xh, r)
        dx1_ref[...] = dx1
        oh, ro = _rms(o_ref[...])
        _acc(dgt_ref, first, _colsum(dx1 * (oh * gp_ref[...])))
        dno = dx1 * gt_ref[...]
        _acc(dgp_ref, first, _colsum(dno * oh))
        do_ref[...] = _rms_bwd(dno * gp_ref[...], oh, ro).astype(do_ref.dtype)

    vec = jax.ShapeDtypeStruct((1, d), _F32)
    return _row_call(body, "bwd_mid", s,
                     [(dh2, "rows"), (x1, "rows"), (dx2, "rows"), (o, "rows"), (g_pre2, "vec"), (sc2, "vec"),
                      (gt1, "vec"), (g_post, "vec")],
                     [(jax.ShapeDtypeStruct((s, d), _F32), "rows"), (jax.ShapeDtypeStruct((s, d), _MXU), "rows"),
                      (vec, "vec"), (vec, "vec"), (vec, "vec"), (vec, "vec"), (vec, "vec")])


def _bwd_pre_mix(dh1, x, dx1, g, sc1):
    s, d = x.shape

    def body(dh_ref, x_ref, dx1_ref, g_ref, sc_ref, dx_ref, dsc_ref, dsh_ref, dg_ref):
        first = pl.program_id(0) == 0
        dh = dh_ref[...]
        xh, r = _rms(x_ref[...])
        _acc(dsh_ref, first, _colsum(dh))
        _acc(dsc_ref, first, _colsum(dh * (xh * g_ref[...])))
        dn = dh * (1.0 + sc_ref[...])
        _acc(dg_ref, first, _colsum(dn * xh))
        dx_ref[...] = dx1_ref[...] + _rms_bwd(dn * g_ref[...], xh, r)

    vec = jax.ShapeDtypeStruct((1, d), _F32)
    return _row_call(body, "bwd_pre_mix", s,
                     [(dh1, "rows"), (x, "rows"), (dx1, "rows"), (g, "vec"), (sc1, "vec")],
                     [(jax.ShapeDtypeStruct((s, d), _F32), "rows"), (vec, "vec"), (vec, "vec"), (vec, "vec")])


def _mix_norm_fwd(y_ssm, y_sgu, g_ssm, g_sgu):
    s, h = y_ssm.shape

    def body(a_ref, b_ref, ga_ref, gb_ref, o_ref):
        ah, _ = _rms(a_ref[...])
        bh, _ = _rms(b_ref[...])
        o_ref[:, 0:h] = (ah * ga_ref[...]).astype(o_ref.dtype)
        o_ref[:, h:2 * h] = (bh * gb_ref[...]).astype(o_ref.dtype)

    return _row_call(body, "mix_norm_fwd", s, [(y_ssm, "rows"), (y_sgu, "rows"), (g_ssm, "vec"), (g_sgu, "vec")],
                     [(jax.ShapeDtypeStruct((s, 2 * h), _MXU), "rows")])[0]


def _mix_norm_bwd(dyc, y_ssm, y_sgu, g_ssm, g_sgu):
    s, h = y_ssm.shape

    def body(d_ref, a_ref, b_ref, ga_ref, gb_ref, da_ref, db_ref, dga_ref, dgb_ref):
        first = pl.program_id(0) == 0
        for lo, y_ref, g_ref, dy_ref, dg_ref in ((0, a_ref, ga_ref, da_ref, dga_ref), (h, b_ref, gb_ref, db_ref, dgb_ref)):
            d = d_ref[:, lo:lo + h]
            yh, r = _rms(y_ref[...])
            _acc(dg_ref, first, _colsum(d * yh))
            dy_ref[...] = _rms_bwd(d * g_ref[...], yh, r)

    vec = jax.ShapeDtypeStruct((1, h), _F32)
    full = jax.ShapeDtypeStruct((s, h), _F32)
    return _row_call(body, "mix_norm_bwd", s,
                     [(dyc, "rows"), (y_ssm, "rows"), (y_sgu, "rows"), (g_ssm, "vec"), (g_sgu, "vec")],
                     [(full, "rows"), (full, "rows"), (vec, "vec"), (vec, "vec")])


def _shift_down(x, k):
    row = lax.broadcasted_iota(jnp.int32, x.shape, 0)
    return jnp.where(row >= k, pltpu.roll(x, k, 0), 0.0)


def _shift_up(x, k):
    n = x.shape[0]
    row = lax.broadcasted_iota(jnp.int32, x.shape, 0)
    return jnp.where(row < n - k, pltpu.roll(x, n - k, 0), 0.0)


def _conv(x, w_ref, b_ref):
    return b_ref[...] + w_ref[0:1, :] * _shift_down(x, 2) + w_ref[1:2, :] * _shift_down(x, 1) + w_ref[2:3, :] * x


def _conv_act_fwd(up_pre, conv_w, conv_b):
    s, f2 = up_pre.shape
    f = f2 // 2
    tc = _tile(f, 256)
    nf = f // tc

    def body(a_ref, b_ref, wa_ref, wb_ref, ba_ref, bb_ref, o_ref):
        a = _conv(a_ref[...], wa_ref, ba_ref)
        b = _conv(b_ref[...], wb_ref, bb_ref)
        o_ref[...] = (a * _sigmoid(a) * b).astype(o_ref.dtype)

    return pl.pallas_call(
        body, name="conv_act_fwd", grid=(nf,), out_shape=jax.ShapeDtypeStruct((s, f), _MXU),
        in_specs=[pl.BlockSpec((s, tc), lambda n: (0, n)), pl.BlockSpec((s, tc), lambda n: (0, n + nf)),
                  pl.BlockSpec((3, tc), lambda n: (0, n)), pl.BlockSpec((3, tc), lambda n: (0, n + nf)),
                  pl.BlockSpec((1, tc), lambda n: (0, n)), pl.BlockSpec((1, tc), lambda n: (0, n + nf))],
        out_specs=pl.BlockSpec((s, tc), lambda n: (0, n)), compiler_params=_cp("parallel"),
    )(up_pre, up_pre, conv_w, conv_w, conv_b, conv_b)


def _conv_act_bwd(up_pre, d_act, conv_w, conv_b):
    s, f2 = up_pre.shape
    f = f2 // 2
    tc = _tile(f, 256)
    nf = f // tc

    def body(a_ref, b_ref, d_ref, wa_ref, wb_ref, ba_ref, bb_ref,
             dua_ref, dub_ref, w0a, w0b, w1a, w1b, w2a, w2b, dba, dbb):
        xa, xb = a_ref[...], b_ref[...]
        a = _conv(xa, wa_ref, ba_ref)
        b = _conv(xb, wb_ref, bb_ref)
        sg = _sigmoid(a)
        d = d_ref[...]
        d_a = d * b * (sg * (1.0 + a * (1.0 - sg)))
        d_b = d * (a * sg)
        for x, du, w_ref, o_ref, o0, o1, o2, ob in ((xa, d_a, wa_ref, dua_ref, w0a, w1a, w2a, dba),
                                                     (xb, d_b, wb_ref, dub_ref, w0b, w1b, w2b, dbb)):
            ob[...] = _colsum(du)
            o0[...] = _colsum(du * _shift_down(x, 2))
            o1[...] = _colsum(du * _shift_down(x, 1))
            o2[...] = _colsum(du * x)
            o_ref[...] = (w_ref[2:3, :] * du + w_ref[1:2, :] * _shift_up(du, 1)
                          + w_ref[0:1, :] * _shift_up(du, 2)).astype(o_ref.dtype)

    col_a = pl.BlockSpec((s, tc), lambda n: (0, n))
    col_b = pl.BlockSpec((s, tc), lambda n: (0, n + nf))
    vec_a = pl.BlockSpec((1, tc), lambda n: (0, n))
    vec_b = pl.BlockSpec((1, tc), lambda n: (0, n + nf))
    half = jax.ShapeDtypeStruct((s, f), _MXU)
    vec = jax.ShapeDtypeStruct((1, f), _F32)
    outs = pl.pallas_call(
        body, name="conv_act_bwd", grid=(nf,),
        in_specs=[col_a, col_b, col_a, pl.BlockSpec((3, tc), lambda n: (0, n)),
                  pl.BlockSpec((3, tc), lambda n: (0, n + nf)), vec_a, vec_b],
        out_specs=[col_a, col_a] + [vec_a] * 8,
        out_shape=[half, half] + [vec] * 8, compiler_params=_cp("parallel"),
    )(up_pre, up_pre, d_act, conv_w, conv_w, conv_b, conv_b)
    dua, dub, w0a, w0b, w1a, w1b, w2a, w2b, dba, dbb = outs
    cat = lambda p, q: jnp.concatenate([p, q], axis=1)
    return cat(dua, dub), cat(w0a, w0b), cat(w1a, w1b), cat(w2a, w2b), cat(dba, dbb)


def _sgu_recompute(zu_ref, zv_ref, lng_ref, lnb_ref, wm_ref, bs_ref, nh):
    zu, zv = zu_ref[...], zv_ref[...]
    u = _gelu(zu)
    gv = _gelu(zv)
    xc = gv - _rowmean(gv)
    rs = lax.rsqrt(_rowmean(xc * xc) + EPS)
    vh = xc * rs
    v = vh * lng_ref[...] + lnb_ref[...]
    mixed = []
    for h in range(nh):
        vhd = v[:, h * CHUNK:(h + 1) * CHUNK].astype(_MXU)
        mixed.append(jnp.dot(wm_ref[h].astype(_MXU), vhd, preferred_element_type=_F32) + bs_ref[h])
    return zu, zv, u, vh, rs, v, mixed


def _sgu_fwd(z, ln_g, ln_b, wm, bs):
    s = z.shape[0]
    nh = wm.shape[0]
    hd = nh * CHUNK

    def body(zu_ref, zv_ref, lng_ref, lnb_ref, wm_ref, bs_ref, y_ref):
        _, _, u, _, _, _, mixed = _sgu_recompute(zu_ref, zv_ref, lng_ref, lnb_ref, wm_ref, bs_ref, nh)
        for h in range(nh):
            y_ref[:, h * CHUNK:(h + 1) * CHUNK] = u[:, h * CHUNK:(h + 1) * CHUNK] * mixed[h]

    vec = pl.BlockSpec((1, hd), lambda i: (0, 0))
    return pl.pallas_call(
        body, name="sgu_fwd", grid=(s // CHUNK,), out_shape=jax.ShapeDtypeStruct((s, hd), _F32),
        in_specs=[pl.BlockSpec((CHUNK, hd), lambda i: (i, 1)), pl.BlockSpec((CHUNK, hd), lambda i: (i, 2)), vec, vec,
                  pl.BlockSpec((nh, CHUNK, CHUNK), lambda i: (0, 0, 0)), pl.BlockSpec((nh, CHUNK, 1), lambda i: (0, 0, 0))],
        out_specs=pl.BlockSpec((CHUNK, hd), lambda i: (i, 0)), compiler_params=_cp("parallel"),
    )(z, z, ln_g, ln_b, wm, bs)


def _sgu_bwd(z, dy, ln_g, ln_b, wm, bs):
    s = z.shape[0]
    nh = wm.shape[0]
    hd = nh * CHUNK

    def body(zu_ref, zv_ref, dy_ref, lng_ref, lnb_ref, wm_ref, bs_ref, dz_ref, dlg_ref, dlb_ref, dwm_ref, dbs_ref, dv_scr):
        first = pl.program_id(0) == 0
        zu, zv, u, vh, rs, v, mixed = _sgu_recompute(zu_ref, zv_ref, lng_ref, lnb_ref, wm_ref, bs_ref, nh)
        dy = dy_ref[...]
        for h in range(nh):
            cols = slice(h * CHUNK, (h + 1) * CHUNK)
            dyh = dy[:, cols]
            dz_ref[:, cols] = dyh * mixed[h] * _gelu_grad(zu[:, cols])
            dm = dyh * u[:, cols]
            dmx = dm.astype(_MXU)
            _acc(dbs_ref.at[h], first, jnp.sum(dm, axis=1, keepdims=True))
            _acc(dwm_ref.at[h], first,
                 lax.dot_general(dmx, v[:, cols].astype(_MXU), (((1,), (1,)), ((), ())), preferred_element_type=_F32))
            dv_scr[:, cols] = lax.dot_general(wm_ref[h].astype(_MXU), dmx, (((0,), (0,)), ((), ())),
                                              preferred_element_type=_F32)
        dv = dv_scr[...]
        _acc(dlg_ref, first, _colsum(dv * vh))
        _acc(dlb_ref, first, _colsum(dv))
        dvh = dv * lng_ref[...]
        dgv = rs * (dvh - _rowmean(dvh) - vh * _rowmean(dvh * vh))
        dz_ref[:, hd:2 * hd] = dgv * _gelu_grad(zv)

    vec = pl.BlockSpec((1, hd), lambda i: (0, 0))
    wspec = pl.BlockSpec((nh, CHUNK, CHUNK), lambda i: (0, 0, 0))
    bspec = pl.BlockSpec((nh, CHUNK, 1), lambda i: (0, 0, 0))
    return pl.pallas_call(
        body, name="sgu_bwd", grid=(s // CHUNK,),
        out_shape=[jax.ShapeDtypeStruct((s, 2 * hd), _F32), jax.ShapeDtypeStruct((1, hd), _F32),
                   jax.ShapeDtypeStruct((1, hd), _F32), jax.ShapeDtypeStruct((nh, CHUNK, CHUNK), _F32),
                   jax.ShapeDtypeStruct((nh, CHUNK, 1), _F32)],
        in_specs=[pl.BlockSpec((CHUNK, hd), lambda i: (i, 1)), pl.BlockSpec((CHUNK, hd), lambda i: (i, 2)),
                  pl.BlockSpec((CHUNK, hd), lambda i: (i, 0)), vec, vec, wspec, bspec],
        out_specs=[pl.BlockSpec((CHUNK, 2 * hd), lambda i: (i, 0)), vec, vec, wspec, bspec],
        scratch_shapes=[pltpu.VMEM((CHUNK, hd), _F32)], compiler_params=_cp("arbitrary"),
    )(z, z, dy, ln_g, ln_b, wm, bs)


def _ssm_prep(log_dt, a_re, a_im, b_re_t, b_im_t):
    gn = a_re.shape[1]

    def body(ldt_ref, are_ref, aim_ref, br_ref, bi_ref, pr_ref, pi_ref, bbr_ref, bbi_ref):
        dt = jnp.exp(ldt_ref[...])
        are, aim = are_ref[...], aim_ref[...]
        k = (lax.broadcasted_iota(jnp.int32, (SUBLANES, gn), 0) + 1).astype(_F32)
        mag = jnp.exp(k * (are * dt))
        ang = k * (aim * dt)
        pr_ref[...] = mag * jnp.cos(ang)
        pi_ref[...] = mag * jnp.sin(ang)
        m1 = jnp.exp(are * dt)
        lr, li = m1 * jnp.cos(aim * dt), m1 * jnp.sin(aim * dt)
        den = are * are + aim * aim
        nr = lr - 1.0
        f_re = (nr * are + li * aim) / den
        f_im = (li * are - nr * aim) / den
        bbr_ref[...] = f_re * br_ref[...] - f_im * bi_ref[...]
        bbi_ref[...] = f_re * bi_ref[...] + f_im * br_ref[...]

    pw = jax.ShapeDtypeStruct((SUBLANES, gn), _F32)
    bb = jax.ShapeDtypeStruct(b_re_t.shape, _F32)
    return pl.pallas_call(body, name="ssm_prep", out_shape=[pw, pw, bb, bb])(log_dt, a_re, a_im, b_re_t, b_im_t)


def _ssm_prep_bwd(log_dt, a_re, a_im, b_re_t, b_im_t, d_bbr, d_bbi, d_lr, d_li):
    def body(ldt_ref, are_ref, aim_ref, br_ref, bi_ref, dbr_ref, dbi_ref, dlr_ref, dli_ref,
             obr_ref, obi_ref, oar_ref, oai_ref, odt_ref):
        dt = jnp.exp(ldt_ref[...])
        are, aim = are_ref[...], aim_ref[...]
        m1 = jnp.exp(are * dt)
        lr, li = m1 * jnp.cos(aim * dt), m1 * jnp.sin(aim * dt)
        den = are * are + aim * aim
        nr = lr - 1.0
        f_re = (nr * are + li * aim) / den
        f_im = (li * are - nr * aim) / den
        br, bi, dbr, dbi = br_ref[...], bi_ref[...], dbr_ref[...], dbi_ref[...]
        obr_ref[...] = f_re * dbr + f_im * dbi
        obi_ref[...] = f_re * dbi - f_im * dbr
        gf_re = _colsum(br * dbr + bi * dbi)
        gf_im = _colsum(br * dbi - bi * dbr)
        il_re, il_im = are / den, -aim / den
        glb_re = dlr_ref[...] + (il_re * gf_re + il_im * gf_im)
        glb_im = dli_ref[...] + (il_re * gf_im - il_im * gf_re)
        q_re = -(f_re * il_re - f_im * il_im)
        q_im = -(f_re * il_im + f_im * il_re)
        gl_re = q_re * gf_re + q_im * gf_im
        gl_im = q_re * gf_im - q_im * gf_re
        gl_re = gl_re + dt * (lr * glb_re + li * glb_im)
        gl_im = gl_im + dt * (lr * glb_im - li * glb_re)
        w_re = are * lr - aim * li
        w_im = are * li + aim * lr
        oar_ref[...] = gl_re
        oai_ref[...] = gl_im
        odt_ref[...] = w_re * glb_re + w_im * glb_im

    bb = jax.ShapeDtypeStruct(b_re_t.shape, _F32)
    v = jax.ShapeDtypeStruct(a_re.shape, _F32)
    return pl.pallas_call(body, name="ssm_prep_bwd", out_shape=[bb, bb, v, v, v])(
        log_dt, a_re, a_im, b_re_t, b_im_t, d_bbr, d_bbi, d_lr, d_li)


def _group_sum(d_dt, log_dt):
    def body(d_ref, l_ref, o_ref):
        o_ref[...] = jnp.sum(d_ref[...], axis=1, keepdims=True) * jnp.exp(l_ref[...])

    return pl.pallas_call(body, name="ssm_dt_grad", out_shape=jax.ShapeDtypeStruct(log_dt.shape, _F32))(d_dt, log_dt)


def _scan_rows(src_ref, dst_ref, nrt, steps, ptab, carry0, reverse):
    ns = BLOCK_ST
    row = lax.broadcasted_iota(jnp.int32, (SUBLANES, ns), 0)
    pr, pi = ptab

    def body(i, carry):
        cr, ci = carry
        it = (nrt - 1 - i) if reverse else i
        r0 = pl.multiple_of(it * SUBLANES, SUBLANES)
        xr = src_ref[pl.ds(r0, SUBLANES), 0:ns]
        xi = src_ref[pl.ds(r0, SUBLANES), ns:2 * ns]
        for k, (ar, ai) in zip((1, 2, 4), steps):
            if reverse:
                keep = row < SUBLANES - k
                sr = jnp.where(keep, pltpu.roll(xr, SUBLANES - k, 0), 0.0)
                si = jnp.where(keep, pltpu.roll(xi, SUBLANES - k, 0), 0.0)
            else:
                keep = row >= k
                sr = jnp.where(keep, pltpu.roll(xr, k, 0), 0.0)
                si = jnp.where(keep, pltpu.roll(xi, k, 0), 0.0)
            xr, xi = xr + ar * sr - ai * si, xi + ar * si + ai * sr
        xr, xi = xr + pr * cr - pi * ci, xi + pr * ci + pi * cr
        dst_ref[pl.ds(r0, SUBLANES), 0:ns] = xr
        dst_ref[pl.ds(r0, SUBLANES), ns:2 * ns] = xi
        if reverse:
            return xr[0:1, :], xi[0:1, :]
        return xr[SUBLANES - 1:SUBLANES, :], xi[SUBLANES - 1:SUBLANES, :]

    return lax.fori_loop(0, nrt, body, carry0)


def _scan_consts(p_ref, conj):
    ns = BLOCK_ST
    sign = -1.0 if conj else 1.0
    bc = lambda r: jnp.broadcast_to(r, (SUBLANES, ns))
    steps = [(bc(p_ref[k - 1:k, 0:ns]), bc(sign * p_ref[k - 1:k, ns:2 * ns])) for k in (1, 2, 4)]
    return steps


def _ssm_block_fwd(u, bbt_ref, ct_ref, d_ref, wg_ref, bg_ref, p_ref, bu_scr, h_scr, carry_in, nrt):
    ns = BLOCK_ST
    bu_scr[...] = jnp.dot(u.astype(_MXU), bbt_ref[...].astype(_MXU), preferred_element_type=_F32)
    ptab = (p_ref[:, 0:ns], p_ref[:, ns:2 * ns])
    carry = _scan_rows(bu_scr, h_scr, nrt, _scan_consts(p_ref, False), ptab, carry_in, False)
    y = jnp.dot(h_scr[...].astype(_MXU), ct_ref[...].astype(_MXU), preferred_element_type=_F32) + d_ref[...] * u
    yg = _gelu(y)
    gate = _sigmoid(jnp.dot(yg.astype(_MXU), wg_ref[...].astype(_MXU), preferred_element_type=_F32) + bg_ref[...])
    return y, yg, gate, carry


def _ssm_specs(nb, nt, t, reverse):
    tt = (lambda ti: nt - 1 - ti) if reverse else (lambda ti: ti)
    ns2 = 2 * BLOCK_ST
    return dict(
        z=pl.BlockSpec((t, BLOCK_CH), lambda b, ti: (tt(ti), b)),
        bbt=pl.BlockSpec((None, BLOCK_CH, ns2), lambda b, ti: (b, 0, 0)),
        ct=pl.BlockSpec((None, ns2, BLOCK_CH), lambda b, ti: (b, 0, 0)),
        vec=pl.BlockSpec((1, BLOCK_CH), lambda b, ti: (0, b)),
        wg=pl.BlockSpec((None, BLOCK_CH, BLOCK_CH), lambda b, ti: (b, 0, 0)),
        p=pl.BlockSpec((None, SUBLANES, ns2), lambda b, ti: (b, 0, 0)),
        hb=pl.BlockSpec((None, None, SUBLANES, ns2), lambda b, ti: (b, tt(ti), 0, 0)),
        acc_vec=pl.BlockSpec((None, 1, ns2), lambda b, ti: (b, 0, 0)),
    )


def _ssm_fwd(z, bbt, ct, dvec, wg, bglu, ptab):
    s = z.shape[0]
    nb = bbt.shape[0]
    t = _tile(s, TIME_TILE, SUBLANES)
    nt = s // t
    ns = BLOCK_ST
    sp = _ssm_specs(nb, nt, t, False)

    def body(z_ref, bbt_ref, ct_ref, d_ref, wg_ref, bg_ref, p_ref, y2_ref, hb_ref, bu_scr, h_scr, carry_scr):
        ti = pl.program_id(1)

        @pl.when(ti == 0)
        def _():
            carry_scr[...] = jnp.zeros_like(carry_scr)

        hb_ref[...] = carry_scr[...]
        carry_in = (carry_scr[0:1, 0:ns], carry_scr[0:1, ns:2 * ns])
        _, yg, gate, (cr, ci) = _ssm_block_fwd(z_ref[...], bbt_ref, ct_ref, d_ref, wg_ref, bg_ref, p_ref,
                                               bu_scr, h_scr, carry_in, t // SUBLANES)
        y2_ref[...] = yg * gate
        carry_scr[:, 0:ns] = jnp.broadcast_to(cr, (SUBLANES, ns))
        carry_scr[:, ns:2 * ns] = jnp.broadcast_to(ci, (SUBLANES, ns))

    return pl.pallas_call(
        body, name="ssm_fwd", grid=(nb, nt),
        out_shape=[jax.ShapeDtypeStruct((s, nb * BLOCK_CH), _F32), jax.ShapeDtypeStruct((nb, nt, SUBLANES, 2 * ns), _F32)],
        in_specs=[sp["z"], sp["bbt"], sp["ct"], sp["vec"], sp["wg"], sp["vec"], sp["p"]],
        out_specs=[sp["z"], sp["hb"]],
        scratch_shapes=[pltpu.VMEM((t, 2 * ns), _F32), pltpu.VMEM((t, 2 * ns), _F32), pltpu.VMEM((SUBLANES, 2 * ns), _F32)],
        compiler_params=_cp("parallel", "arbitrary"),
    )(z, bbt, ct, dvec, wg, bglu, ptab)


def _ssm_bwd(z, dy2, hb, bbt, ct, dvec, wg, bglu, ptab, ptab_rev):
    s = z.shape[0]
    nb = bbt.shape[0]
    t = _tile(s, TIME_TILE, SUBLANES)
    nt = s // t
    ns = BLOCK_ST
    sp = _ssm_specs(nb, nt, t, True)
    tn_dims = (((0,), (0,)), ((), ()))
    nt_dims = (((1,), (1,)), ((), ()))

    def body(z_ref, dy2_ref, hb_ref, bbt_ref, ct_ref, d_ref, wg_ref, bg_ref, p_ref, pr_ref,
             dz_ref, dbbt_ref, dct_ref, dwg_ref, dlb_ref, dd_ref, dbg_ref, bu_scr, h_scr, g_scr, gcarry_scr):
        first = pl.program_id(1) == 0

        @pl.when(first)
        def _():
            gcarry_scr[...] = jnp.zeros_like(gcarry_scr)

        u = z_ref[...]
        hin = hb_ref[...]
        carry_in = (hin[0:1, 0:ns], hin[0:1, ns:2 * ns])
        y, yg, gate, _ = _ssm_block_fwd(u, bbt_ref, ct_ref, d_ref, wg_ref, bg_ref, p_ref, bu_scr, h_scr, carry_in,
                                        t // SUBLANES)
        dy2 = dy2_ref[...]
        dpre = dy2 * yg * gate * (1.0 - gate)
        _acc(dbg_ref, first, _colsum(dpre))
        dpx = dpre.astype(_MXU)
        _acc(dwg_ref, first, lax.dot_general(yg.astype(_MXU), dpx, tn_dims, preferred_element_type=_F32))
        dyg = dy2 * gate + lax.dot_general(dpx, wg_ref[...].astype(_MXU), nt_dims, preferred_element_type=_F32)
        dy = dyg * _gelu_grad(y)
        _acc(dd_ref, first, _colsum(dy * u))
        dyx = dy.astype(_MXU)
        h = h_scr[...]
        _acc(dct_ref, first, lax.dot_general(h.astype(_MXU), dyx, tn_dims, preferred_element_type=_F32))
        bu_scr[...] = lax.dot_general(dyx, ct_ref[...].astype(_MXU), nt_dims, preferred_element_type=_F32)
        gin = (gcarry_scr[0:1, 0:ns], gcarry_scr[0:1, ns:2 * ns])
        ptab = (pr_ref[:, 0:ns], pr_ref[:, ns:2 * ns])
        gr, gi = _scan_rows(bu_scr, g_scr, t // SUBLANES, _scan_consts(p_ref, True), ptab, gin, True)
        gcarry_scr[:, 0:ns] = jnp.broadcast_to(gr, (SUBLANES, ns))
        gcarry_scr[:, ns:2 * ns] = jnp.broadcast_to(gi, (SUBLANES, ns))
        g = g_scr[...]
        row = lax.broadcasted_iota(jnp.int32, (t, ns), 0)
        hp_re = jnp.where(row == 0, hin[0:1, 0:ns], pltpu.roll(h[:, 0:ns], 1, 0))
        hp_im = jnp.where(row == 0, hin[0:1, ns:2 * ns], pltpu.roll(h[:, ns:2 * ns], 1, 0))
        g_re, g_im = g[:, 0:ns], g[:, ns:2 * ns]
        d_ar = _colsum(g_re * hp_re + g_im * hp_im)
        d_ai = _colsum(g_im * hp_re - g_re * hp_im)
        _acc(dlb_ref, first, jnp.concatenate([d_ar, d_ai], axis=1))
        gx = g.astype(_MXU)
        _acc(dbbt_ref, first, lax.dot_general(u.astype(_MXU), gx, tn_dims, preferred_element_type=_F32))
        dz_ref[...] = dy * d_ref[...] + lax.dot_general(gx, bbt_ref[...].astype(_MXU), nt_dims, preferred_element_type=_F32)

    f = lambda shape: jax.ShapeDtypeStruct(shape, _F32)
    return pl.pallas_call(
        body, name="ssm_bwd", grid=(nb, nt),
        out_shape=[f((s, nb * BLOCK_CH)), f(bbt.shape), f(ct.shape), f(wg.shape), f((nb, 1, 2 * ns)),
                   f((1, nb * BLOCK_CH)), f((1, nb * BLOCK_CH))],
        in_specs=[sp["z"], sp["z"], sp["hb"], sp["bbt"], sp["ct"], sp["vec"], sp["wg"], sp["vec"], sp["p"], sp["p"]],
        out_specs=[sp["z"], sp["bbt"], sp["ct"], sp["wg"], sp["acc_vec"], sp["vec"], sp["vec"]],
        scratch_shapes=[pltpu.VMEM((t, 2 * ns), _F32), pltpu.VMEM((t, 2 * ns), _F32), pltpu.VMEM((t, 2 * ns), _F32),
                        pltpu.VMEM((SUBLANES, 2 * ns), _F32)],
        compiler_params=_cp("parallel", "arbitrary"),
    )(z, dy2, hb, bbt, ct, dvec, wg, bglu, ptab, ptab_rev)


def _mod_part(c_all, w, b):
    d, ns = w.shape
    tn = _tile(ns, 512)

    def body(c_ref, w_ref, b_ref, o_ref):
        c = c_ref[...]
        ca = (c * _sigmoid(c)).astype(_MXU)
        o_ref[...] = jnp.dot(ca, w_ref[...].astype(_MXU), preferred_element_type=_F32) + b_ref[...]

    return pl.pallas_call(
        body, name="mod_part", grid=(ns // tn,), out_shape=jax.ShapeDtypeStruct((8, ns), _F32),
        in_specs=[pl.BlockSpec((8, d), lambda n: (0, 0)), pl.BlockSpec((d, tn), lambda n: (0, n)),
                  pl.BlockSpec((1, tn), lambda n: (0, n))],
        out_specs=pl.BlockSpec((8, tn), lambda n: (0, n)), compiler_params=_cp("parallel"),
    )(c_all, w, b)


def _adamw_math(w, g, m, v):
    m = ADAM_B1 * m + (1.0 - ADAM_B1) * g
    v = ADAM_B2 * v + (1.0 - ADAM_B2) * (g * g)
    m_hat = m / (1.0 - ADAM_B1 ** ADAM_STEP)
    v_hat = v / (1.0 - ADAM_B2 ** ADAM_STEP)
    delta = -ADAM_LR * (m_hat / (jnp.sqrt(v_hat) + ADAM_EPS) + ADAM_WD * w)
    return delta, m, v


def _adamw(w, g, m, v, name):
    r, c = w.shape
    tr, tc = _tile(r, 256, SUBLANES), _tile(c, 1024)

    def body(w_ref, g_ref, m_ref, v_ref, d_ref, mo_ref, vo_ref):
        d_ref[...], mo_ref[...], vo_ref[...] = _adamw_math(w_ref[...], g_ref[...], m_ref[...], v_ref[...])

    spec = pl.BlockSpec((tr, tc), lambda i, j: (i, j))
    out = jax.ShapeDtypeStruct((r, c), _F32)
    return pl.pallas_call(
        body, name=name, grid=(r // tr, c // tc), in_specs=[spec] * 4, out_specs=[spec] * 3, out_shape=[out] * 3,
        compiler_params=_cp("parallel", "parallel"),
    )(w, g, m, v)


def _wada_update(c_t, dm, w, m, v):
    d, ns = w.shape
    tr, tc = _tile(d, 256, SUBLANES), _tile(ns, 1024)

    def body(c_ref, dm_ref, w_ref, m_ref, v_ref, g_ref, d_ref, mo_ref, vo_ref):
        c = c_ref[...]
        ca = c * _sigmoid(c)
        dmv = dm_ref[...]
        g = ca[:, 0:1] * dmv[0:1, :]
        for b in range(1, 8):
            g = g + ca[:, b:b + 1] * dmv[b:b + 1, :]
        g_ref[...] = g
        d_ref[...], mo_ref[...], vo_ref[...] = _adamw_math(w_ref[...], g, m_ref[...], v_ref[...])

    spec = pl.BlockSpec((tr, tc), lambda i, j: (i, j))
    out = jax.ShapeDtypeStruct((d, ns), _F32)
    return pl.pallas_call(
        body, name="wada_update", grid=(d // tr, ns // tc),
        in_specs=[pl.BlockSpec((tr, 8), lambda i, j: (i, 0)), pl.BlockSpec((8, tc), lambda i, j: (0, j)), spec, spec, spec],
        out_specs=[spec] * 4, out_shape=[out] * 4, compiler_params=_cp("parallel", "parallel"),
    )(c_t, dm, w, m, v)


def _small_reduce_adamw(gathered, w, m, v):
    _, r, c = gathered.shape
    tr = _tile(r, 256, SUBLANES)

    def body(q_ref, w_ref, m_ref, v_ref, g_ref, d_ref, mo_ref, vo_ref):
        g = q_ref[0]
        for k in range(1, 8):
            g = g + q_ref[k]
        g_ref[...] = g
        d_ref[...], mo_ref[...], vo_ref[...] = _adamw_math(w_ref[...], g, m_ref[...], v_ref[...])

    spec = pl.BlockSpec((tr, c), lambda i: (i, 0))
    out = jax.ShapeDtypeStruct((r, c), _F32)
    return pl.pallas_call(
        body, name="small_reduce_adamw", grid=(r // tr,),
        in_specs=[pl.BlockSpec((8, tr, c), lambda i: (0, i, 0)), spec, spec, spec],
        out_specs=[spec] * 4, out_shape=[out] * 4, compiler_params=_cp("parallel"),
    )(gathered, w, m, v)


def _block_diag(x):
    nb, g, p, q = x.shape
    eye = jnp.eye(g, dtype=x.dtype)
    return (x[:, :, :, None, :] * eye[None, :, None, :, None]).reshape(nb, g * p, g * q)


def _block_diag_take(x, p, q):
    nb = x.shape[0]
    g = GROUPS_PER_BLOCK
    eye = jnp.eye(g, dtype=x.dtype)
    return jnp.sum(x.reshape(nb, g, p, g, q) * eye[None, :, None, :, None], axis=3)


class _Pack:
    def __init__(self, shapes):
        self.shapes = shapes
        self.offsets = {}
        off = 0
        for name, shape in shapes.items():
            n = math.prod(shape)
            self.offsets[name] = (off, n)
            off += -(-n // (SUBLANES * LANES)) * (SUBLANES * LANES)
        self.rows = -(-off // (256 * LANES)) * 256

    def pack(self, arrays):
        parts = []
        off = 0
        for name, shape in self.shapes.items():
            start, n = self.offsets[name]
            if start > off:
                parts.append(jnp.zeros((start - off,), _F32))
            parts.append(arrays[name].reshape(-1).astype(_F32))
            off = start + n
        total = self.rows * LANES
        if total > off:
            parts.append(jnp.zeros((total - off,), _F32))
        return jnp.concatenate(parts).reshape(self.rows, LANES)

    def unpack(self, buf):
        flat = buf.reshape(-1)
        return {name: flat[start:start + n].reshape(self.shapes[name]) for name, (start, n) in self.offsets.items()}


_SMALL = ["b_ada", "g_pre_mix", "g_post_mix", "ssm_log_dt", "ssm_a_re", "ssm_a_im", "ssm_b_re", "ssm_b_im", "ssm_c_re",
          "ssm_c_im", "ssm_d", "ssm_w_glu", "ssm_b_glu", "sgu_ln_g", "sgu_ln_b", "sgu_w", "sgu_b", "g_out_ssm",
          "g_out_sgu", "g_pre_ffn", "g_post_ffn", "conv_b"]
_WEIGHTS = ["w_ada", "b_ada", "g_pre_mix", "g_post_mix", "w_in", "ssm_log_dt", "ssm_a_re", "ssm_a_im", "ssm_b_re",
            "ssm_b_im", "ssm_c_re", "ssm_c_im", "ssm_d", "ssm_w_glu", "ssm_b_glu", "sgu_ln_g", "sgu_ln_b", "sgu_w", "sgu_b",
            "g_out_ssm", "g_out_sgu", "w_out", "g_pre_ffn", "g_post_ffn", "w_up", "conv_w", "conv_b", "w_down"]


def _step(p, m, v, x, c, tgt):
    s, d = x.shape
    mx, my, mc = lax.axis_index("x"), lax.axis_index("y"), lax.axis_index("c")
    chip = 2 * mx + my
    dev = 4 * mx + 2 * my + mc
    sel_c = jnp.reshape(mc, (1,)).astype(jnp.int32)
    sel_chip = jnp.reshape(chip, (1,)).astype(jnp.int32)
    g_cnt, n_st = p["ssm_a_re"].shape
    nb = g_cnt // GROUPS_PER_BLOCK
    gn = g_cnt * n_st
    d_ssm = g_cnt * SSM_GROUP
    nh = p["sgu_w"].shape[0]
    assert nh * CHUNK == d_ssm and 2 * d_ssm == d and n_st == SSM_STATE

    halves = lambda w: w.astype(_WIRE).reshape(2, w.shape[0] // 2, w.shape[1])
    w_in4, w_out4, w_up4, w_down4 = [
        g.reshape(4, g.shape[1] * g.shape[2], g.shape[3])
        for g in _gather_weights([halves(p["w_in"]), halves(p["w_out"]), halves(p["w_up"]), halves(p["w_down"])])]
    w_out_full = w_out4.reshape(1, d, d)
    w_down_full = w_down4.reshape(1, -1, d)

    ns_ada = p["w_ada"].shape[1]
    nc_conv = p["conv_w"].shape[1]
    first = jnp.concatenate([jnp.broadcast_to(c, (8, d)), jnp.pad(p["conv_w"], ((0, 5), (0, 0)))], axis=1)
    first_all = _all_gather8(first, "gather_c_conv")
    c_all = first_all[:, 0, :d]
    conv_w_full = jnp.concatenate([first_all[2 * j, 0:3, d:] for j in range(4)], axis=1)
    b_ada_mine = lax.dynamic_slice_in_dim(p["b_ada"], chip * ns_ada, ns_ada, axis=1)
    mod_all = _all_gather8(_mod_part(c_all, p["w_ada"], b_ada_mine), "gather_mod")
    mod_rows = lax.dynamic_index_in_dim(mod_all, dev, axis=1, keepdims=False)
    mod = jnp.concatenate([mod_rows[0], mod_rows[2], mod_rows[4], mod_rows[6]]).reshape(N_MOD, 1, d)
    sh1, sc1, gt1, sh2, sc2, gt2 = [mod[i] for i in range(N_MOD)]

    ldt_l = jnp.repeat(p["ssm_log_dt"], n_st, axis=1)
    are_l, aim_l = p["ssm_a_re"].reshape(1, gn), p["ssm_a_im"].reshape(1, gn)
    bre_t, bim_t = p["ssm_b_re"].reshape(gn, SSM_GROUP).T, p["ssm_b_im"].reshape(gn, SSM_GROUP).T
    pw_re, pw_im, bb_re, bb_im = _ssm_prep(ldt_l, are_l, aim_l, bre_t, bim_t)
    blocks = lambda t: t.reshape(t.shape[0], nb, GROUPS_PER_BLOCK * n_st).transpose(1, 0, 2)
    ptab = jnp.concatenate([blocks(pw_re), blocks(pw_im)], axis=2)
    ptab_rev = jnp.concatenate([blocks(pw_re)[:, ::-1], -blocks(pw_im)[:, ::-1]], axis=2)
    bd = lambda t: t.reshape(SSM_GROUP, nb, GROUPS_PER_BLOCK, n_st).transpose(1, 2, 0, 3)
    bbt = jnp.concatenate([_block_diag(bd(bb_re)), _block_diag(bd(bb_im))], axis=2).astype(_MXU)
    cd = lambda t: t.reshape(nb, GROUPS_PER_BLOCK, SSM_GROUP, n_st).transpose(0, 1, 3, 2)
    ct = jnp.concatenate([_block_diag(cd(p["ssm_c_re"])), -_block_diag(cd(p["ssm_c_im"]))], axis=1).astype(_MXU)
    wg = _block_diag(p["ssm_w_glu"].reshape(nb, GROUPS_PER_BLOCK, SSM_GROUP, SSM_GROUP)).astype(_MXU)
    dvec = p["ssm_d"]
    bglu = p["ssm_b_glu"].reshape(1, d_ssm)
    mask = jnp.tril(jnp.ones((CHUNK, CHUNK), _F32))
    wm = (p["sgu_w"] * mask[None]).astype(_MXU)
    bs = p["sgu_b"].reshape(nh, CHUNK, 1)

    h1 = _fwd_pre_mix(x, p["g_pre_mix"], sc1, sh1)
    z = _mm_nn(h1, w_in4, _F32, "mm_in")
    y_ssm, hb = _ssm_fwd(z, bbt, ct, dvec, wg, bglu, ptab)
    y_sgu = _sgu_fwd(z, p["sgu_ln_g"], p["sgu_ln_b"], wm, bs)
    ycat = _mix_norm_fwd(y_ssm, y_sgu, p["g_out_ssm"], p["g_out_sgu"])
    o = _mm_nn(ycat, w_out_full, _F32, "mm_out")
    x1, h2 = _fwd_mid(o, x, gt1, p["g_post_mix"], p["g_pre_ffn"], sc2, sh2)
    up_pre = _mm_nn(h2, w_up4, _F32, "mm_up")
    act = _conv_act_fwd(up_pre, conv_w_full, p["conv_b"])
    f = _mm_nn(act, w_down_full, _F32, "mm_down", tk=2816)
    dx2, df, d_gt2, d_g_post_ffn, loss = _loss_and_post_ffn_bwd(f, x1, tgt, gt2, p["g_post_ffn"])

    d_act = _mm_nt(df, w_down_full, _F32, "mm_d_act")
    gw_down = _mm_tn_rows(act, df, "mm_gw_down")
    d_up_pre, d_cw0, d_cw1, d_cw2, d_conv_b = _conv_act_bwd(up_pre, d_act, conv_w_full, p["conv_b"])
    dh2 = _mm_nt(d_up_pre, w_up4, _F32, "mm_dh2")
    gw_up = _mm_tn_cols(h2, d_up_pre, "mm_gw_up")
    dx1, d_o, d_sc2, d_sh2, d_g_pre_ffn, d_gt1, d_g_post_mix = _bwd_mid(
        dh2, x1, dx2, o, p["g_pre_ffn"], sc2, gt1, p["g_post_mix"])
    d_ycat = _mm_nt(d_o, w_out_full, _F32, "mm_d_ycat")
    gw_out = _mm_tn_rows(ycat, d_o, "mm_gw_out")
    dy_ssm, dy_sgu, d_g_out_ssm, d_g_out_sgu = _mix_norm_bwd(d_ycat, y_ssm, y_sgu, p["g_out_ssm"], p["g_out_sgu"])
    dz_uv, d_ln_g, d_ln_b, d_wm, d_bs = _sgu_bwd(z, dy_sgu, p["sgu_ln_g"], p["sgu_ln_b"], wm, bs)
    dz_ssm, d_bbt, d_ct, d_wg, d_lb, d_ssm_d, d_bglu = _ssm_bwd(z, dy_ssm, hb, bbt, ct, dvec, wg, bglu, ptab, ptab_rev)
    dz = jnp.concatenate([dz_ssm.astype(_MXU), dz_uv.astype(_MXU)], axis=1)
    dh1 = _mm_nt(dz, w_in4, _F32, "mm_dh1")
    gw_in = _mm_tn_cols(h1, dz, "mm_gw_in")
    dx, d_sc1, d_sh1, d_g_pre_mix = _bwd_pre_mix(dh1, x, dx1, p["g_pre_mix"], sc1)

    nsb = BLOCK_ST
    lanes = lambda t: t.transpose(2, 0, 1, 3).reshape(SSM_GROUP, gn)
    d_bbr = lanes(_block_diag_take(d_bbt[:, :, :nsb], SSM_GROUP, n_st))
    d_bbi = lanes(_block_diag_take(d_bbt[:, :, nsb:], SSM_GROUP, n_st))
    d_lr, d_li = d_lb[:, 0, :nsb].reshape(1, gn), d_lb[:, 0, nsb:].reshape(1, gn)
    d_bre_t, d_bim_t, d_are, d_aim, d_dt = _ssm_prep_bwd(ldt_l, are_l, aim_l, bre_t, bim_t, d_bbr, d_bbi, d_lr, d_li)
    d_log_dt = _group_sum(d_dt.reshape(g_cnt, n_st), p["ssm_log_dt"].reshape(g_cnt, 1))
    c_grad = lambda t: _block_diag_take(t, n_st, SSM_GROUP).transpose(0, 1, 3, 2).reshape(g_cnt, SSM_GROUP, n_st)
    small = {
        "b_ada": jnp.concatenate([d_sh1, d_sc1, d_gt1, d_sh2, d_sc2, d_gt2], axis=1),
        "g_pre_mix": d_g_pre_mix, "g_post_mix": d_g_post_mix,
        "ssm_log_dt": d_log_dt, "ssm_a_re": d_are, "ssm_a_im": d_aim,
        "ssm_b_re": d_bre_t.T, "ssm_b_im": d_bim_t.T,
        "ssm_c_re": c_grad(d_ct[:, :nsb, :]), "ssm_c_im": -c_grad(d_ct[:, nsb:, :]),
        "ssm_d": d_ssm_d, "ssm_w_glu": _block_diag_take(d_wg, SSM_GROUP, SSM_GROUP), "ssm_b_glu": d_bglu,
        "sgu_ln_g": d_ln_g, "sgu_ln_b": d_ln_b, "sgu_w": d_wm * mask[None], "sgu_b": d_bs,
        "g_out_ssm": d_g_out_ssm, "g_out_sgu": d_g_out_sgu, "g_pre_ffn": d_g_pre_ffn, "g_post_ffn": d_g_post_ffn,
        "conv_b": d_conv_b, "conv_w_all": jnp.concatenate([d_cw0, d_cw1, d_cw2], axis=0),
    }
    shapes = {name: p[name].shape for name in _SMALL}
    shapes["conv_w_all"] = (3, 4 * nc_conv)
    pk = _Pack(shapes)
    zeros_cw = jnp.zeros(shapes["conv_w_all"], _F32)
    gathered = _all_gather8(pk.pack(small), "gather_small")
    g_pk, d_pk, m_pk, v_pk = _small_reduce_adamw(
        gathered, pk.pack({**{n: p[n] for n in _SMALL}, "conv_w_all": zeros_cw}),
        pk.pack({**{n: m[n] for n in _SMALL}, "conv_w_all": zeros_cw}),
        pk.pack({**{n: v[n] for n in _SMALL}, "conv_w_all": zeros_cw}))
    grads, deltas, new_m, new_v = pk.unpack(g_pk), pk.unpack(d_pk), pk.unpack(m_pk), pk.unpack(v_pk)

    grads["conv_w"] = lax.dynamic_slice_in_dim(grads.pop("conv_w_all"), chip * nc_conv, nc_conv, axis=1)
    deltas["conv_w"], new_m["conv_w"], new_v["conv_w"] = _adamw(p["conv_w"], grads["conv_w"], m["conv_w"], v["conv_w"],
                                                                 "adamw_conv_w")
    d_mod_all = gathered.reshape(8, -1)[:, :N_MOD * d]
    d_mod_mine = lax.dynamic_slice_in_dim(d_mod_all, chip * ns_ada, ns_ada, axis=1)
    grads["w_ada"], deltas["w_ada"], new_m["w_ada"], new_v["w_ada"] = _wada_update(
        c_all.T, d_mod_mine, p["w_ada"], m["w_ada"], v["w_ada"])

    big = ["w_in", "w_out", "w_up", "w_down"]
    gws = [gw_in, gw_out, gw_up, gw_down]
    got = _pair_swap(gws, "pair_swap")
    pairs = [_pair_sum(g, q, sel_c, "pair_sum_" + n) for g, q, n in zip(gws, got, big)]
    recv = _chip_scatter(pairs)
    mine = [_chip_sum(pr, q, sel_chip, "chip_sum_" + n) for pr, q, n in zip(pairs, recv, big)]
    joined = _pair_join(mine)
    for n, j in zip(big, joined):
        if n in ("w_in", "w_up"):
            grads[n] = j.reshape(p[n].shape)
        else:
            grads[n] = jnp.concatenate([j[0], j[1]], axis=1)
        deltas[n], new_m[n], new_v[n] = _adamw(p[n], grads[n], m[n], v[n], "adamw_" + n)
    return loss[0, 0], dx, grads, deltas, new_m, new_v


def kernel(x, c, w_ada, b_ada, g_pre_mix, g_post_mix, w_in, ssm_log_dt, ssm_a_re, ssm_a_im, ssm_b_re, ssm_b_im, ssm_c_re, ssm_c_im, ssm_d, ssm_w_glu, ssm_b_glu, sgu_ln_g, sgu_ln_b, sgu_w, sgu_b, g_out_ssm, g_out_sgu, w_out, g_pre_ffn, g_post_ffn, w_up, conv_w, conv_b, w_down, loss_target, m_w_ada, m_b_ada, m_g_pre_mix, m_g_post_mix, m_w_in, m_ssm_log_dt, m_ssm_a_re, m_ssm_a_im, m_ssm_b_re, m_ssm_b_im, m_ssm_c_re, m_ssm_c_im, m_ssm_d, m_ssm_w_glu, m_ssm_b_glu, m_sgu_ln_g, m_sgu_ln_b, m_sgu_w, m_sgu_b, m_g_out_ssm, m_g_out_sgu, m_w_out, m_g_pre_ffn, m_g_post_ffn, m_w_up, m_conv_w, m_conv_b, m_w_down, v_w_ada, v_b_ada, v_g_pre_mix, v_g_post_mix, v_w_in, v_ssm_log_dt, v_ssm_a_re, v_ssm_a_im, v_ssm_b_re, v_ssm_b_im, v_ssm_c_re, v_ssm_c_im, v_ssm_d, v_ssm_w_glu, v_ssm_b_glu, v_sgu_ln_g, v_sgu_ln_b, v_sgu_w, v_sgu_b, v_g_out_ssm, v_g_out_sgu, v_w_out, v_g_pre_ffn, v_g_post_ffn, v_w_up, v_conv_w, v_conv_b, v_w_down):
    given = dict(locals())
    drop = lambda a: a if a.ndim == 2 else a[0]
    p = {n: drop(given[n]) for n in _WEIGHTS}
    m = {n: drop(given["m_" + n]) for n in _WEIGHTS}
    v = {n: drop(given["v_" + n]) for n in _WEIGHTS}
    loss, dx, grads, deltas, new_m, new_v = _step(p, m, v, x[0], c, loss_target[0])
    loss = lax.psum(loss, ("x", "y", "c"))
    outs = [loss, dx[None]]
    for group in (grads, deltas, new_m, new_v):
        outs += [group[n].reshape(given[n].shape) for n in _WEIGHTS]
    return tuple(outs)
```

```python
import functools
import math

import jax
import jax.numpy as jnp
from jax import lax
from jax.experimental import pallas as pl
from jax.experimental.pallas import tpu as pltpu

_F32 = jnp.float32
_MXU = jnp.bfloat16
_WIRE = jnp.bfloat16

EPS = 1e-6
SSM_GROUP = 16
SSM_STATE = 64
GROUPS_PER_BLOCK = 8
BLOCK_CH = SSM_GROUP * GROUPS_PER_BLOCK
BLOCK_ST = SSM_STATE * GROUPS_PER_BLOCK
CHUNK = 128
TIME_TILE = 256
SUBLANES = 8
LANES = 128
N_MOD = 6
ADAM_LR, ADAM_B1, ADAM_B2, ADAM_EPS, ADAM_WD, ADAM_STEP = 0.001, 0.9, 0.999, 1e-08, 0.01, 10
_VMEM_LIMIT = 56 * 1024 * 1024
_MESH = pl.DeviceIdType.MESH
_ANY = pl.BlockSpec(memory_space=pl.ANY)
_GELU_C = math.sqrt(2.0 / math.pi)


def _cp(*sem):
    return pltpu.CompilerParams(dimension_semantics=sem, vmem_limit_bytes=_VMEM_LIMIT)


def _tile(dim, target, align=LANES):
    if dim <= target:
        return dim
    best = None
    for t in range(align, target + 1, align):
        if dim % t == 0:
            best = t
    assert best is not None, (dim, target, align)
    return best


def _gelu(x):
    return 0.5 * x * (1.0 + jnp.tanh(_GELU_C * (x + 0.044715 * (x * x * x))))


def _gelu_grad(x):
    t = jnp.tanh(_GELU_C * (x + 0.044715 * (x * x * x)))
    return 0.5 * (1.0 + t) + 0.5 * x * (1.0 - t * t) * (_GELU_C * (1.0 + 3.0 * 0.044715 * x * x))


def _sigmoid(x):
    return 1.0 / (1.0 + jnp.exp(-x))


def _colsum(x):
    return jnp.sum(x, axis=0, keepdims=True)


def _rowmean(x):
    return jnp.mean(x, axis=-1, keepdims=True)


def _acc(ref, first, val):
    @pl.when(first)
    def _():
        ref[...] = val

    @pl.when(jnp.logical_not(first))
    def _():
        ref[...] += val


def _place():
    mx, my, mc = lax.axis_index("x"), lax.axis_index("y"), lax.axis_index("c")
    chips = [(1 - mx, my), (mx, 1 - my), (1 - mx, 1 - my)]
    return mx, my, mc, chips


def _all_gather8(buf, name):
    def body(in_ref, out_ref, send_sems, recv_sems):
        mx, my, mc, chips = _place()
        me, sibling = (mx, my, mc), (mx, my, 1 - mc)

        def slot(ref, px, py, pc):
            return ref.at[4 * px + 2 * py + pc]

        def copy(k, block, to, src_ref=out_ref):
            return pltpu.make_async_remote_copy(
                src_ref=slot(src_ref, *block), dst_ref=slot(out_ref, *block),
                send_sem=send_sems.at[k], recv_sem=recv_sems.at[k], device_id=to, device_id_type=_MESH)

        first = [copy(0, me, sibling, in_ref)]
        first += [copy(1 + j, me, (*chip, mc), in_ref) for j, chip in enumerate(chips)]
        for cp in first:
            cp.start()
        passed = [copy(4 + j, (*chip, mc), sibling) for j, chip in enumerate(chips)]
        for j, chip in enumerate(chips):
            copy(1 + j, (*chip, mc), me).wait_recv()
            passed[j].start()
        copy(0, sibling, me).wait_recv()
        for j, chip in enumerate(chips):
            copy(4 + j, (*chip, 1 - mc), me).wait_recv()
        for cp in first + passed:
            cp.wait_send()

    return pl.pallas_call(
        body, name=name, out_shape=jax.ShapeDtypeStruct(buf.shape, buf.dtype),
        in_specs=[_ANY], out_specs=_ANY, input_output_aliases={0: 0},
        scratch_shapes=[pltpu.SemaphoreType.DMA((7,)), pltpu.SemaphoreType.DMA((7,))],
    )(buf)


def _own_slot(x, dev):
    return lax.dynamic_update_slice(jnp.zeros((8,) + x.shape, x.dtype), x[None], (dev, 0, 0))


def _cast_into_slot(w, sel):
    r, c = w.shape
    hr = r // 2
    tr = _tile(hr, 256, 16)
    nr = hr // tr

    def body(sel_ref, w_ref, o_ref):
        o_ref[...] = w_ref[...].astype(o_ref.dtype)

    return pl.pallas_call(
        body, name="cast_into_slot", out_shape=jax.ShapeDtypeStruct((4, 2, hr, c), _WIRE),
        grid_spec=pltpu.PrefetchScalarGridSpec(
            num_scalar_prefetch=1, grid=(2, nr),
            in_specs=[pl.BlockSpec((tr, c), lambda h, i, s: (h * nr + i, 0))],
            out_specs=pl.BlockSpec((None, None, tr, c), lambda h, i, s: (s[0], h, i, 0))),
        compiler_params=_cp("parallel", "parallel"),
    )(sel, w)


def _gather_weights(bufs):
    n = len(bufs)

    def body(*refs):
        ins, outs = refs[:n], refs[n:2 * n]
        send_sems, recv_sems = refs[2 * n:]
        mx, my, mc, chips = _place()
        sibling = (mx, my, 1 - mc)
        j_me = 2 * mx + my
        remote = []
        for i in range(n):
            for k, chip in enumerate(chips):
                cp = pltpu.make_async_remote_copy(
                    src_ref=ins[i].at[j_me, mc], dst_ref=outs[i].at[j_me, mc],
                    send_sem=send_sems.at[6 * i + k], recv_sem=recv_sems.at[6 * i + k],
                    device_id=(*chip, mc), device_id_type=_MESH)
                cp.start()
                remote.append(cp)
        for i in range(n):
            for k, chip in enumerate(chips):
                j_k = 2 * chip[0] + chip[1]
                landed = outs[i].at[j_k, mc]
                pltpu.make_async_remote_copy(
                    src_ref=landed, dst_ref=landed, send_sem=send_sems.at[6 * i + k], recv_sem=recv_sems.at[6 * i + k],
                    device_id=(*chip, mc), device_id_type=_MESH).wait_recv()
                cp = pltpu.make_async_remote_copy(
                    src_ref=landed, dst_ref=landed, send_sem=send_sems.at[6 * i + 3 + k],
                    recv_sem=recv_sems.at[6 * i + 3 + k], device_id=sibling, device_id_type=_MESH)
                cp.start()
                remote.append(cp)
        for i in range(n):
            for k, chip in enumerate(chips):
                j_k = 2 * chip[0] + chip[1]
                other = outs[i].at[j_k, 1 - mc]
                pltpu.make_async_remote_copy(
                    src_ref=other, dst_ref=other, send_sem=send_sems.at[6 * i + 3 + k],
                    recv_sem=recv_sems.at[6 * i + 3 + k], device_id=sibling, device_id_type=_MESH).wait_recv()
        for cp in remote:
            cp.wait_send()

    return pl.pallas_call(
        body, name="gather_weights", out_shape=[jax.ShapeDtypeStruct(b.shape, b.dtype) for b in bufs],
        in_specs=[_ANY] * n, out_specs=[_ANY] * n, input_output_aliases={i: i for i in range(n)},
        scratch_shapes=[pltpu.SemaphoreType.DMA((6 * n,)), pltpu.SemaphoreType.DMA((6 * n,))],
    )(*bufs)


def _pair_swap(arrs, name):
    n = len(arrs)

    def body(*refs):
        ins, outs = refs[:n], refs[n:2 * n]
        send_sems, recv_sems = refs[2 * n:]
        mx, my, mc, _ = _place()
        sibling = (mx, my, 1 - mc)
        cps = []
        for i in range(n):
            cp = pltpu.make_async_remote_copy(
                src_ref=ins[i].at[1 - mc], dst_ref=outs[i], send_sem=send_sems.at[i], recv_sem=recv_sems.at[i],
                device_id=sibling, device_id_type=_MESH)
            cp.start()
            cps.append(cp)
        for cp in cps:
            cp.wait()

    return pl.pallas_call(
        body, name=name, out_shape=[jax.ShapeDtypeStruct(a.shape[1:], a.dtype) for a in arrs],
        in_specs=[_ANY] * n, out_specs=[_ANY] * n,
        scratch_shapes=[pltpu.SemaphoreType.DMA((n,)), pltpu.SemaphoreType.DMA((n,))],
    )(*arrs)


def _chip_scatter(arrs):
    n = len(arrs)

    def body(*refs):
        ins, outs = refs[:n], refs[n:2 * n]
        send_sems, recv_sems = refs[2 * n:]
        mx, my, mc, chips = _place()
        cps = []
        for i in range(n):
            for k, chip in enumerate(chips):
                cp = pltpu.make_async_remote_copy(
                    src_ref=ins[i].at[2 * chip[0] + chip[1]], dst_ref=outs[i].at[k],
                    send_sem=send_sems.at[3 * i + k], recv_sem=recv_sems.at[3 * i + k],
                    device_id=(*chip, mc), device_id_type=_MESH)
                cp.start()
                cps.append(cp)
        for cp in cps:
            cp.wait()

    return pl.pallas_call(
        body, name="chip_scatter", out_shape=[jax.ShapeDtypeStruct((3,) + a.shape[1:], a.dtype) for a in arrs],
        in_specs=[_ANY] * n, out_specs=[_ANY] * n,
        scratch_shapes=[pltpu.SemaphoreType.DMA((3 * n,)), pltpu.SemaphoreType.DMA((3 * n,))],
    )(*arrs)


def _pair_join(bufs):
    n = len(bufs)

    def body(*refs):
        ins, outs = refs[:n], refs[n:2 * n]
        send_sems, recv_sems = refs[2 * n:]
        mx, my, mc, _ = _place()
        sibling = (mx, my, 1 - mc)
        cps = []
        for i in range(n):
            cp = pltpu.make_async_remote_copy(
                src_ref=ins[i].at[mc], dst_ref=outs[i].at[mc], send_sem=send_sems.at[i], recv_sem=recv_sems.at[i],
                device_id=sibling, device_id_type=_MESH)
            cp.start()
            cps.append(cp)
        for i in range(n):
            other = outs[i].at[1 - mc]
            pltpu.make_async_remote_copy(
                src_ref=other, dst_ref=other, send_sem=send_sems.at[i], recv_sem=recv_sems.at[i],
                device_id=sibling, device_id_type=_MESH).wait_recv()
        for cp in cps:
            cp.wait_send()

    return pl.pallas_call(
        body, name="pair_join", out_shape=[jax.ShapeDtypeStruct(b.shape, b.dtype) for b in bufs],
        in_specs=[_ANY] * n, out_specs=[_ANY] * n, input_output_aliases={i: i for i in range(n)},
        scratch_shapes=[pltpu.SemaphoreType.DMA((n,)), pltpu.SemaphoreType.DMA((n,))],
    )(*bufs)


def _pair_sum(g, got, sel, name):
    _, four, hr, c = g.shape
    tr = _tile(hr, 512, 16)

    def body(sel_ref, g_ref, p_ref, o_ref):
        o_ref[...] = (g_ref[...].astype(_F32) + p_ref[...].astype(_F32)).astype(o_ref.dtype)

    return pl.pallas_call(
        body, name=name, out_shape=jax.ShapeDtypeStruct((four, hr, c), g.dtype),
        grid_spec=pltpu.PrefetchScalarGridSpec(
            num_scalar_prefetch=1, grid=(four, hr // tr),
            in_specs=[pl.BlockSpec((None, None, tr, c), lambda j, i, s: (s[1], j, i, 0)),
                      pl.BlockSpec((None, tr, c), lambda j, i, s: (j, i, 0))],
            out_specs=pl.BlockSpec((None, tr, c), lambda j, i, s: (j, i, 0))),
        compiler_params=_cp("parallel", "parallel"),
    )(sel, g, got)


def _chip_sum(pair, got, sel, name):
    _, hr, c = pair.shape
    tr = _tile(hr, 512, 16)

    def body(sel_ref, p_ref, q_ref, o_ref):
        o_ref[...] = ((p_ref[...].astype(_F32) + q_ref[0].astype(_F32)) + q_ref[1].astype(_F32)) + q_ref[2].astype(_F32)

    return pl.pallas_call(
        body, name=name, out_shape=jax.ShapeDtypeStruct((2, hr, c), _F32),
        grid_spec=pltpu.PrefetchScalarGridSpec(
            num_scalar_prefetch=1, grid=(hr // tr,),
            in_specs=[pl.BlockSpec((None, tr, c), lambda i, s: (s[0], i, 0)),
                      pl.BlockSpec((3, tr, c), lambda i, s: (0, i, 0))],
            out_specs=pl.BlockSpec((None, tr, c), lambda i, s: (s[1], i, 0))),
        compiler_params=_cp("parallel"),
    )(sel, pair, got)


def _matmul(a, b, dims, out_struct, grid, a_spec, b_spec, o_spec, acc_shape, k_axis, name):
    nk = grid[k_axis]

    def body(a_ref, b_ref, o_ref, acc_ref):
        prod = lax.dot_general(a_ref[...].astype(_MXU), b_ref[...].astype(_MXU), dims, preferred_element_type=_F32)
        if nk == 1:
            o_ref[...] = prod.astype(o_ref.dtype)
        else:
            k = pl.program_id(k_axis)

            @pl.when(k == 0)
            def _():
                acc_ref[...] = prod

            @pl.when(k > 0)
            def _():
                acc_ref[...] += prod

            @pl.when(k == nk - 1)
            def _():
                o_ref[...] = acc_ref[...].astype(o_ref.dtype)

    sem = ["parallel"] * len(grid)
    sem[k_axis] = "arbitrary"
    return pl.pallas_call(
        body, name=name, out_shape=out_struct, grid=grid, in_specs=[a_spec, b_spec], out_specs=o_spec,
        scratch_shapes=[pltpu.VMEM(acc_shape, _F32)], compiler_params=_cp(*sem),
    )(a, b)


def _mm_nn(a, w4, out_dtype, name, tm=512, tn=1536, tk=2048):
    m, k = a.shape
    j, _, ns = w4.shape
    tm, tn, tk = _tile(m, tm, 16), _tile(ns, tn), _tile(k, tk)
    nps = ns // tn
    return _matmul(
        a, w4, (((1,), (0,)), ((), ())), jax.ShapeDtypeStruct((m, j * ns), out_dtype),
        (m // tm, j * nps, k // tk),
        pl.BlockSpec((tm, tk), lambda mi, ni, ki: (mi, ki)),
        pl.BlockSpec((None, tk, tn), lambda mi, ni, ki: (ni // nps, ki, ni % nps)),
        pl.BlockSpec((tm, tn), lambda mi, ni, ki: (mi, ni)), (tm, tn), 2, name)


def _mm_nt(a, w4, out_dtype, name, tm=512, tn=2048, tk=1536):
    m = a.shape[-2]
    j, kw, ns = w4.shape
    tm, tn, tk = _tile(m, tm, 16), _tile(kw, tn), _tile(ns, tk)
    kps = ns // tk
    if a.ndim == 3:
        kph = a.shape[2] // tk
        a_spec = pl.BlockSpec((None, tm, tk), lambda mi, ni, ki: (ki // kph, mi, ki % kph))
    else:
        a_spec = pl.BlockSpec((tm, tk), lambda mi, ni, ki: (mi, ki))
    return _matmul(
        a, w4, (((1,), (1,)), ((), ())), jax.ShapeDtypeStruct((m, kw), out_dtype),
        (m // tm, kw // tn, j * kps),
        a_spec,
        pl.BlockSpec((None, tn, tk), lambda mi, ni, ki: (ki // kps, ni, ki % kps)),
        pl.BlockSpec((tm, tn), lambda mi, ni, ki: (mi, ni)), (tm, tn), 2, name)


def _mm_tn_cols(a, b, name, tm=1024, tn=1536, tk=2048):
    m, ka = a.shape
    ns = (b.shape[-1] * (2 if b.ndim == 3 else 1)) // 4
    hr = ka // 2
    tm, tn, tk = _tile(hr, tm), _tile(ns, tn), _tile(m, tk, 16)
    mph, nps = hr // tm, ns // tn
    if b.ndim == 3:
        b_spec = pl.BlockSpec((None, tk, tn), lambda ni, mi, ki: (ni // (2 * nps), ki, ni % (2 * nps)))
    else:
        b_spec = pl.BlockSpec((tk, tn), lambda ni, mi, ki: (ki, ni))
    return _matmul(
        a, b, (((0,), (0,)), ((), ())), jax.ShapeDtypeStruct((2, 4, hr, ns), _WIRE),
        (4 * nps, 2 * mph, m // tk),
        pl.BlockSpec((tk, tm), lambda ni, mi, ki: (ki, mi)),
        b_spec,
        pl.BlockSpec((None, None, tm, tn), lambda ni, mi, ki: (mi // mph, ni // nps, mi % mph, ni % nps)),
        (tm, tn), 2, name)


def _mm_tn_rows(a, b, name, tm=1536, tn=1024, tk=2048):
    m, ka = a.shape
    r = ka // 4
    hc = b.shape[1] // 2
    tm, tn, tk = _tile(r, tm), _tile(hc, tn), _tile(m, tk, 16)
    mpr, nph = r // tm, hc // tn
    return _matmul(
        a, b, (((0,), (0,)), ((), ())), jax.ShapeDtypeStruct((2, 4, r, hc), _WIRE),
        (2 * nph, 4 * mpr, m // tk),
        pl.BlockSpec((tk, tm), lambda ni, mi, ki: (ki, mi)),
        pl.BlockSpec((tk, tn), lambda ni, mi, ki: (ki, ni)),
        pl.BlockSpec((None, None, tm, tn), lambda ni, mi, ki: (ni // nph, mi // mpr, mi % mpr, ni % nph)),
        (tm, tn), 2, name)


def _row_call(body, name, rows, ins, outs, tm=256):
    tm = _tile(rows, tm, 16)

    def spec(shape, kind):
        if kind == "rows":
            return pl.BlockSpec((tm, shape[1]), lambda i: (i, 0))
        return pl.BlockSpec(shape, lambda i: (0,) * len(shape))

    return pl.pallas_call(
        body, name=name, grid=(rows // tm,),
        in_specs=[spec(a.shape, kind) for a, kind in ins],
        out_specs=[spec(o.shape, kind) for o, kind in outs],
        out_shape=[o for o, _ in outs],
        compiler_params=_cp("arbitrary"),
    )(*[a for a, _ in ins])


def _rms(x):
    r = lax.rsqrt(_rowmean(x * x) + EPS)
    return x * r, r


def _rms_bwd(dxh, xh, r):
    return r * (dxh - xh * _rowmean(dxh * xh))


def _fwd_pre_mix(x, g, sc, sh):
    s, d = x.shape

    def body(x_ref, g_ref, sc_ref, sh_ref, h_ref):
        xh, _ = _rms(x_ref[...])
        h_ref[...] = (xh * g_ref[...] * (1.0 + sc_ref[...]) + sh_ref[...]).astype(h_ref.dtype)

    return _row_call(body, "fwd_pre_mix", s, [(x, "rows"), (g, "vec"), (sc, "vec"), (sh, "vec")],
                     [(jax.ShapeDtypeStruct((s, d), _MXU), "rows")])[0]


def _fwd_mid(o, x, gt1, g_post, g_pre2, sc2, sh2):
    s, d = x.shape

    def body(o_ref, x_ref, gt_ref, gp_ref, g2_ref, sc_ref, sh_ref, x1_ref, h2_ref):
        oh, _ = _rms(o_ref[...])
        x1 = x_ref[...] + gt_ref[...] * (oh * gp_ref[...])
        x1_ref[...] = x1
        xh, _ = _rms(x1)
        h2_ref[...] = (xh * g2_ref[...] * (1.0 + sc_ref[...]) + sh_ref[...]).astype(h2_ref.dtype)

    return _row_call(body, "fwd_mid", s,
                     [(o, "rows"), (x, "rows"), (gt1, "vec"), (g_post, "vec"), (g_pre2, "vec"), (sc2, "vec"),
                      (sh2, "vec")],
                     [(jax.ShapeDtypeStruct((s, d), _F32), "rows"), (jax.ShapeDtypeStruct((s, d), _MXU), "rows")])


def _loss_and_post_ffn_bwd(f, x1, tgt, gt2, g_post):
    s, d = x1.shape

    def body(f_ref, x1_ref, t_ref, gt_ref, g_ref, dx2_ref, df_ref, dgt_ref, dg_ref, loss_ref):
        first = pl.program_id(0) == 0
        fh, r = _rms(f_ref[...])
        n = fh * g_ref[...]
        e = x1_ref[...] + gt_ref[...] * n - t_ref[...]
        _acc(loss_ref, first, jnp.sum(_colsum(e * e), axis=1, keepdims=True) * (0.5 / d))
        dx2 = e * (1.0 / d)
        dx2_ref[...] = dx2
        _acc(dgt_ref, first, _colsum(dx2 * n))
        dn = dx2 * gt_ref[...]
        _acc(dg_ref, first, _colsum(dn * fh))
        df_ref[...] = _rms_bwd(dn * g_ref[...], fh, r).astype(df_ref.dtype)

    vec = jax.ShapeDtypeStruct((1, d), _F32)
    return _row_call(body, "loss_post_ffn_bwd", s,
                     [(f, "rows"), (x1, "rows"), (tgt, "rows"), (gt2, "vec"), (g_post, "vec")],
                     [(jax.ShapeDtypeStruct((s, d), _F32), "rows"), (jax.ShapeDtypeStruct((s, d), _MXU), "rows"),
                      (vec, "vec"), (vec, "vec"), (jax.ShapeDtypeStruct((1, 1), _F32), "vec")])


def _bwd_mid(dh2, x1, dx2, o, g_pre2, sc2, gt1, g_post):
    s, d = x1.shape

    def body(dh_ref, x1_ref, dx2_ref, o_ref, g2_ref, sc_ref, gt_ref, gp_ref,
             dx1_ref, do_ref, dsc_ref, dsh_ref, dg2_ref, dgt_ref, dgp_ref):
        first = pl.program_id(0) == 0
        dh = dh_ref[...]
        xh, r = _rms(x1_ref[...])
        _acc(dsh_ref, first, _colsum(dh))
        _acc(dsc_ref, first, _colsum(dh * (xh * g2_ref[...])))
        dn = dh * (1.0 + sc_ref[...])
        _acc(dg2_ref, first, _colsum(dn * xh))
        dx1 = dx2_ref[...] + _rms_bwd(dn * g2_ref[...], xh, r)
        dx1_ref[...] = dx1
        oh, ro = _rms(o_ref[...])
        _acc(dgt_ref, first, _colsum(dx1 * (oh * gp_ref[...])))
        dno = dx1 * gt_ref[...]
        _acc(dgp_ref, first, _colsum(dno * oh))
        do_ref[...] = _rms_bwd(dno * gp_ref[...], oh, ro).astype(do_ref.dtype)

    vec = jax.ShapeDtypeStruct((1, d), _F32)
    return _row_call(body, "bwd_mid", s,
                     [(dh2, "rows"), (x1, "rows"), (dx2, "rows"), (o, "rows"), (g_pre2, "vec"), (sc2, "vec"),
                      (gt1, "vec"), (g_post, "vec")],
                     [(jax.ShapeDtypeStruct((s, d), _F32), "rows"), (jax.ShapeDtypeStruct((s, d), _MXU), "rows"),
                      (vec, "vec"), (vec, "vec"), (vec, "vec"), (vec, "vec"), (vec, "vec")])


def _bwd_pre_mix(dh1, x, dx1, g, sc1):
    s, d = x.shape

    def body(dh_ref, x_ref, dx1_ref, g_ref, sc_ref, dx_ref, dsc_ref, dsh_ref, dg_ref):
        first = pl.program_id(0) == 0
        dh = dh_ref[...]
        xh, r = _rms(x_ref[...])
        _acc(dsh_ref, first, _colsum(dh))
        _acc(dsc_ref, first, _colsum(dh * (xh * g_ref[...])))
        dn = dh * (1.0 + sc_ref[...])
        _acc(dg_ref, first, _colsum(dn * xh))
        dx_ref[...] = dx1_ref[...] + _rms_bwd(dn * g_ref[...], xh, r)

    vec = jax.ShapeDtypeStruct((1, d), _F32)
    return _row_call(body, "bwd_pre_mix", s,
                     [(dh1, "rows"), (x, "rows"), (dx1, "rows"), (g, "vec"), (sc1, "vec")],
                     [(jax.ShapeDtypeStruct((s, d), _F32), "rows"), (vec, "vec"), (vec, "vec"), (vec, "vec")])


def _mix_norm_fwd(y_ssm, y_sgu, g_ssm, g_sgu):
    s, h = y_ssm.shape

    def body(a_ref, b_ref, ga_ref, gb_ref, o_ref):
        ah, _ = _rms(a_ref[...])
        bh, _ = _rms(b_ref[...])
        o_ref[:, 0:h] = (ah * ga_ref[...]).astype(o_ref.dtype)
        o_ref[:, h:2 * h] = (bh * gb_ref[...]).astype(o_ref.dtype)

    return _row_call(body, "mix_norm_fwd", s, [(y_ssm, "rows"), (y_sgu, "rows"), (g_ssm, "vec"), (g_sgu, "vec")],
                     [(jax.ShapeDtypeStruct((s, 2 * h), _MXU), "rows")])[0]


def _mix_norm_bwd(dyc, y_ssm, y_sgu, g_ssm, g_sgu):
    s, h = y_ssm.shape

    def body(d_ref, a_ref, b_ref, ga_ref, gb_ref, da_ref, db_ref, dga_ref, dgb_ref):
        first = pl.program_id(0) == 0
        for lo, y_ref, g_ref, dy_ref, dg_ref in ((0, a_ref, ga_ref, da_ref, dga_ref), (h, b_ref, gb_ref, db_ref, dgb_ref)):
            d = d_ref[:, lo:lo + h]
            yh, r = _rms(y_ref[...])
            _acc(dg_ref, first, _colsum(d * yh))
            dy_ref[...] = _rms_bwd(d * g_ref[...], yh, r)

    vec = jax.ShapeDtypeStruct((1, h), _F32)
    full = jax.ShapeDtypeStruct((s, h), _F32)
    return _row_call(body, "mix_norm_bwd", s,
                     [(dyc, "rows"), (y_ssm, "rows"), (y_sgu, "rows"), (g_ssm, "vec"), (g_sgu, "vec")],
                     [(full, "rows"), (full, "rows"), (vec, "vec"), (vec, "vec")])


def _shift_down(x, k):
    row = lax.broadcasted_iota(jnp.int32, x.shape, 0)
    return jnp.where(row >= k, pltpu.roll(x, k, 0), 0.0)


def _shift_up(x, k):
    n = x.shape[0]
    row = lax.broadcasted_iota(jnp.int32, x.shape, 0)
    return jnp.where(row < n - k, pltpu.roll(x, n - k, 0), 0.0)


def _conv(x, w_ref, b_ref):
    return b_ref[...] + w_ref[0:1, :] * _shift_down(x, 2) + w_ref[1:2, :] * _shift_down(x, 1) + w_ref[2:3, :] * x


def _conv_act_fwd(up_pre, conv_w, conv_b):
    s, f2 = up_pre.shape
    f = f2 // 2
    tc = _tile(f, 256)
    nf = f // tc

    def body(a_ref, b_ref, wa_ref, wb_ref, ba_ref, bb_ref, o_ref):
        a = _conv(a_ref[...], wa_ref, ba_ref)
        b = _conv(b_ref[...], wb_ref, bb_ref)
        o_ref[...] = (a * _sigmoid(a) * b).astype(o_ref.dtype)

    return pl.pallas_call(
        body, name="conv_act_fwd", grid=(nf,), out_shape=jax.ShapeDtypeStruct((s, f), _MXU),
        in_specs=[pl.BlockSpec((s, tc), lambda n: (0, n)), pl.BlockSpec((s, tc), lambda n: (0, n + nf)),
                  pl.BlockSpec((3, tc), lambda n: (0, n)), pl.BlockSpec((3, tc), lambda n: (0, n + nf)),
                  pl.BlockSpec((1, tc), lambda n: (0, n)), pl.BlockSpec((1, tc), lambda n: (0, n + nf))],
        out_specs=pl.BlockSpec((s, tc), lambda n: (0, n)), compiler_params=_cp("parallel"),
    )(up_pre, up_pre, conv_w, conv_w, conv_b, conv_b)


def _conv_act_bwd(up_pre, d_act, conv_w, conv_b):
    s, f2 = up_pre.shape
    f = f2 // 2
    tc = _tile(f, 256)
    nf = f // tc

    def body(a_ref, b_ref, d_ref, wa_ref, wb_ref, ba_ref, bb_ref,
             du_ref, w0a, w0b, w1a, w1b, w2a, w2b, dba, dbb):
        xa, xb = a_ref[...], b_ref[...]
        a = _conv(xa, wa_ref, ba_ref)
        b = _conv(xb, wb_ref, bb_ref)
        sg = _sigmoid(a)
        d = d_ref[...]
        d_a = d * b * (sg * (1.0 + a * (1.0 - sg)))
        d_b = d * (a * sg)
        for x, du, w_ref, o_ref, o0, o1, o2, ob in ((xa, d_a, wa_ref, du_ref.at[0], w0a, w1a, w2a, dba),
                                                     (xb, d_b, wb_ref, du_ref.at[1], w0b, w1b, w2b, dbb)):
            ob[...] = _colsum(du)
            o0[...] = _colsum(du * _shift_down(x, 2))
            o1[...] = _colsum(du * _shift_down(x, 1))
            o2[...] = _colsum(du * x)
            o_ref[...] = (w_ref[2:3, :] * du + w_ref[1:2, :] * _shift_up(du, 1)
                          + w_ref[0:1, :] * _shift_up(du, 2)).astype(o_ref.dtype)

    col_a = pl.BlockSpec((s, tc), lambda n: (0, n))
    col_b = pl.BlockSpec((s, tc), lambda n: (0, n + nf))
    vec_a = pl.BlockSpec((1, tc), lambda n: (0, n))
    vec_b = pl.BlockSpec((1, tc), lambda n: (0, n + nf))
    vec = jax.ShapeDtypeStruct((1, f), _F32)
    outs = pl.pallas_call(
        body, name="conv_act_bwd", grid=(nf,),
        in_specs=[col_a, col_b, col_a, pl.BlockSpec((3, tc), lambda n: (0, n)),
                  pl.BlockSpec((3, tc), lambda n: (0, n + nf)), vec_a, vec_b],
        out_specs=[pl.BlockSpec((2, s, tc), lambda n: (0, 0, n))] + [vec_a] * 8,
        out_shape=[jax.ShapeDtypeStruct((2, s, f), _MXU)] + [vec] * 8, compiler_params=_cp("parallel"),
    )(up_pre, up_pre, d_act, conv_w, conv_w, conv_b, conv_b)
    du, w0a, w0b, w1a, w1b, w2a, w2b, dba, dbb = outs
    cat = lambda p, q: jnp.concatenate([p, q], axis=1)
    return du, cat(w0a, w0b), cat(w1a, w1b), cat(w2a, w2b), cat(dba, dbb)


def _sgu_recompute(zu_ref, zv_ref, lng_ref, lnb_ref, wm_ref, bs_ref, nh):
    zu, zv = zu_ref[...], zv_ref[...]
    u = _gelu(zu)
    gv = _gelu(zv)
    xc = gv - _rowmean(gv)
    rs = lax.rsqrt(_rowmean(xc * xc) + EPS)
    vh = xc * rs
    v = vh * lng_ref[...] + lnb_ref[...]
    mixed = []
    for h in range(nh):
        vhd = v[:, h * CHUNK:(h + 1) * CHUNK].astype(_MXU)
        mixed.append(jnp.dot(wm_ref[h].astype(_MXU), vhd, preferred_element_type=_F32) + bs_ref[h])
    return zu, zv, u, vh, rs, v, mixed


def _sgu_fwd(z, ln_g, ln_b, wm, bs):
    s = z.shape[0]
    nh = wm.shape[0]
    hd = nh * CHUNK

    def body(zu_ref, zv_ref, lng_ref, lnb_ref, wm_ref, bs_ref, y_ref):
        _, _, u, _, _, _, mixed = _sgu_recompute(zu_ref, zv_ref, lng_ref, lnb_ref, wm_ref, bs_ref, nh)
        for h in range(nh):
            y_ref[:, h * CHUNK:(h + 1) * CHUNK] = u[:, h * CHUNK:(h + 1) * CHUNK] * mixed[h]

    vec = pl.BlockSpec((1, hd), lambda i: (0, 0))
    return pl.pallas_call(
        body, name="sgu_fwd", grid=(s // CHUNK,), out_shape=jax.ShapeDtypeStruct((s, hd), _F32),
        in_specs=[pl.BlockSpec((CHUNK, hd), lambda i: (i, 1)), pl.BlockSpec((CHUNK, hd), lambda i: (i, 2)), vec, vec,
                  pl.BlockSpec((nh, CHUNK, CHUNK), lambda i: (0, 0, 0)), pl.BlockSpec((nh, CHUNK, 1), lambda i: (0, 0, 0))],
        out_specs=pl.BlockSpec((CHUNK, hd), lambda i: (i, 0)), compiler_params=_cp("parallel"),
    )(z, z, ln_g, ln_b, wm, bs)


def _sgu_bwd(z, dy, dz_ssm, ln_g, ln_b, wm, bs):
    s = z.shape[0]
    nh = wm.shape[0]
    hd = nh * CHUNK

    def body(zu_ref, zv_ref, dy_ref, dzs_ref, lng_ref, lnb_ref, wm_ref, bs_ref,
             dz_ref, dlg_ref, dlb_ref, dwm_ref, dbs_ref, dv_scr):
        first = pl.program_id(0) == 0
        zu, zv, u, vh, rs, v, mixed = _sgu_recompute(zu_ref, zv_ref, lng_ref, lnb_ref, wm_ref, bs_ref, nh)
        dy = dy_ref[...]
        dz_ref[:, 0:hd] = dzs_ref[...]
        for h in range(nh):
            cols = slice(h * CHUNK, (h + 1) * CHUNK)
            dyh = dy[:, cols]
            dz_ref[:, hd + h * CHUNK:hd + (h + 1) * CHUNK] = (dyh * mixed[h] * _gelu_grad(zu[:, cols])).astype(dz_ref.dtype)
            dm = dyh * u[:, cols]
            dmx = dm.astype(_MXU)
            _acc(dbs_ref.at[h], first, jnp.sum(dm, axis=1, keepdims=True))
            _acc(dwm_ref.at[h], first,
                 lax.dot_general(dmx, v[:, cols].astype(_MXU), (((1,), (1,)), ((), ())), preferred_element_type=_F32))
            dv_scr[:, cols] = lax.dot_general(wm_ref[h].astype(_MXU), dmx, (((0,), (0,)), ((), ())),
                                              preferred_element_type=_F32)
        dv = dv_scr[...]
        _acc(dlg_ref, first, _colsum(dv * vh))
        _acc(dlb_ref, first, _colsum(dv))
        dvh = dv * lng_ref[...]
        dgv = rs * (dvh - _rowmean(dvh) - vh * _rowmean(dvh * vh))
        dz_ref[:, 2 * hd:3 * hd] = (dgv * _gelu_grad(zv)).astype(dz_ref.dtype)

    vec = pl.BlockSpec((1, hd), lambda i: (0, 0))
    wspec = pl.BlockSpec((nh, CHUNK, CHUNK), lambda i: (0, 0, 0))
    bspec = pl.BlockSpec((nh, CHUNK, 1), lambda i: (0, 0, 0))
    rows = pl.BlockSpec((CHUNK, hd), lambda i: (i, 0))
    return pl.pallas_call(
        body, name="sgu_bwd", grid=(s // CHUNK,),
        out_shape=[jax.ShapeDtypeStruct((s, 3 * hd), _MXU), jax.ShapeDtypeStruct((1, hd), _F32),
                   jax.ShapeDtypeStruct((1, hd), _F32), jax.ShapeDtypeStruct((nh, CHUNK, CHUNK), _F32),
                   jax.ShapeDtypeStruct((nh, CHUNK, 1), _F32)],
        in_specs=[pl.BlockSpec((CHUNK, hd), lambda i: (i, 1)), pl.BlockSpec((CHUNK, hd), lambda i: (i, 2)),
                  rows, rows, vec, vec, wspec, bspec],
        out_specs=[pl.BlockSpec((CHUNK, 3 * hd), lambda i: (i, 0)), vec, vec, wspec, bspec],
        scratch_shapes=[pltpu.VMEM((CHUNK, hd), _F32)], compiler_params=_cp("arbitrary"),
    )(z, z, dy, dz_ssm, ln_g, ln_b, wm, bs)


def _ssm_prep(log_dt, a_re, a_im, b_re_t, b_im_t):
    gn = a_re.shape[1]

    def body(ldt_ref, are_ref, aim_ref, br_ref, bi_ref, pr_ref, pi_ref, bbr_ref, bbi_ref):
        dt = jnp.exp(ldt_ref[...])
        are, aim = are_ref[...], aim_ref[...]
        k = (lax.broadcasted_iota(jnp.int32, (SUBLANES, gn), 0) + 1).astype(_F32)
        mag = jnp.exp(k * (are * dt))
        ang = k * (aim * dt)
        pr_ref[...] = mag * jnp.cos(ang)
        pi_ref[...] = mag * jnp.sin(ang)
        m1 = jnp.exp(are * dt)
        lr, li = m1 * jnp.cos(aim * dt), m1 * jnp.sin(aim * dt)
        den = are * are + aim * aim
        nr = lr - 1.0
        f_re = (nr * are + li * aim) / den
        f_im = (li * are - nr * aim) / den
        bbr_ref[...] = f_re * br_ref[...] - f_im * bi_ref[...]
        bbi_ref[...] = f_re * bi_ref[...] + f_im * br_ref[...]

    pw = jax.ShapeDtypeStruct((SUBLANES, gn), _F32)
    bb = jax.ShapeDtypeStruct(b_re_t.shape, _F32)
    return pl.pallas_call(body, name="ssm_prep", out_shape=[pw, pw, bb, bb])(log_dt, a_re, a_im, b_re_t, b_im_t)


def _ssm_prep_bwd(log_dt, a_re, a_im, b_re_t, b_im_t, d_bbr, d_bbi, d_lr, d_li):
    def body(ldt_ref, are_ref, aim_ref, br_ref, bi_ref, dbr_ref, dbi_ref, dlr_ref, dli_ref,
             obr_ref, obi_ref, oar_ref, oai_ref, odt_ref):
        dt = jnp.exp(ldt_ref[...])
        are, aim = are_ref[...], aim_ref[...]
        m1 = jnp.exp(are * dt)
        lr, li = m1 * jnp.cos(aim * dt), m1 * jnp.sin(aim * dt)
        den = are * are + aim * aim
        nr = lr - 1.0
        f_re = (nr * are + li * aim) / den
        f_im = (li * are - nr * aim) / den
        br, bi, dbr, dbi = br_ref[...], bi_ref[...], dbr_ref[...], dbi_ref[...]
        obr_ref[...] = f_re * dbr + f_im * dbi
        obi_ref[...] = f_re * dbi - f_im * dbr
        gf_re = _colsum(br * dbr + bi * dbi)
        gf_im = _colsum(br * dbi - bi * dbr)
        il_re, il_im = are / den, -aim / den
        glb_re = dlr_ref[...] + (il_re * gf_re + il_im * gf_im)
        glb_im = dli_ref[...] + (il_re * gf_im - il_im * gf_re)
        q_re = -(f_re * il_re - f_im * il_im)
        q_im = -(f_re * il_im + f_im * il_re)
        gl_re = q_re * gf_re + q_im * gf_im
        gl_im = q_re * gf_im - q_im * gf_re
        gl_re = gl_re + dt * (lr * glb_re + li * glb_im)
        gl_im = gl_im + dt * (lr * glb_im - li * glb_re)
        w_re = are * lr - aim * li
        w_im = are * li + aim * lr
        oar_ref[...] = gl_re
        oai_ref[...] = gl_im
        odt_ref[...] = w_re * glb_re + w_im * glb_im

    bb = jax.ShapeDtypeStruct(b_re_t.shape, _F32)
    v = jax.ShapeDtypeStruct(a_re.shape, _F32)
    return pl.pallas_call(body, name="ssm_prep_bwd", out_shape=[bb, bb, v, v, v])(
        log_dt, a_re, a_im, b_re_t, b_im_t, d_bbr, d_bbi, d_lr, d_li)


def _group_sum(d_dt, log_dt):
    def body(d_ref, l_ref, o_ref):
        o_ref[...] = jnp.sum(d_ref[...], axis=1, keepdims=True) * jnp.exp(l_ref[...])

    return pl.pallas_call(body, name="ssm_dt_grad", out_shape=jax.ShapeDtypeStruct(log_dt.shape, _F32))(d_dt, log_dt)


def _scan_rows(src_ref, dst_ref, nrt, steps, ptab, carry0, reverse):
    ns = BLOCK_ST
    row = lax.broadcasted_iota(jnp.int32, (SUBLANES, ns), 0)
    pr, pi = ptab

    def body(i, carry):
        cr, ci = carry
        it = (nrt - 1 - i) if reverse else i
        r0 = pl.multiple_of(it * SUBLANES, SUBLANES)
        xr = src_ref[pl.ds(r0, SUBLANES), 0:ns]
        xi = src_ref[pl.ds(r0, SUBLANES), ns:2 * ns]
        for k, (ar, ai) in zip((1, 2, 4), steps):
            if reverse:
                keep = row < SUBLANES - k
                sr = jnp.where(keep, pltpu.roll(xr, SUBLANES - k, 0), 0.0)
                si = jnp.where(keep, pltpu.roll(xi, SUBLANES - k, 0), 0.0)
            else:
                keep = row >= k
                sr = jnp.where(keep, pltpu.roll(xr, k, 0), 0.0)
                si = jnp.where(keep, pltpu.roll(xi, k, 0), 0.0)
            xr, xi = xr + ar * sr - ai * si, xi + ar * si + ai * sr
        xr, xi = xr + pr * cr - pi * ci, xi + pr * ci + pi * cr
        dst_ref[pl.ds(r0, SUBLANES), 0:ns] = xr
        dst_ref[pl.ds(r0, SUBLANES), ns:2 * ns] = xi
        if reverse:
            return xr[0:1, :], xi[0:1, :]
        return xr[SUBLANES - 1:SUBLANES, :], xi[SUBLANES - 1:SUBLANES, :]

    return lax.fori_loop(0, nrt, body, carry0)


def _scan_consts(p_ref, conj):
    ns = BLOCK_ST
    sign = -1.0 if conj else 1.0
    bc = lambda r: jnp.broadcast_to(r, (SUBLANES, ns))
    steps = [(bc(p_ref[k - 1:k, 0:ns]), bc(sign * p_ref[k - 1:k, ns:2 * ns])) for k in (1, 2, 4)]
    return steps


def _ssm_block_fwd(u, bbt_ref, ct_ref, d_ref, wg_ref, bg_ref, p_ref, bu_scr, h_scr, carry_in, nrt):
    ns = BLOCK_ST
    bu_scr[...] = jnp.dot(u.astype(_MXU), bbt_ref[...].astype(_MXU), preferred_element_type=_F32)
    ptab = (p_ref[:, 0:ns], p_ref[:, ns:2 * ns])
    carry = _scan_rows(bu_scr, h_scr, nrt, _scan_consts(p_ref, False), ptab, carry_in, False)
    y = jnp.dot(h_scr[...].astype(_MXU), ct_ref[...].astype(_MXU), preferred_element_type=_F32) + d_ref[...] * u
    yg = _gelu(y)
    gate = _sigmoid(jnp.dot(yg.astype(_MXU), wg_ref[...].astype(_MXU), preferred_element_type=_F32) + bg_ref[...])
    return y, yg, gate, carry


def _ssm_specs(nb, nt, t, reverse):
    tt = (lambda ti: nt - 1 - ti) if reverse else (lambda ti: ti)
    ns2 = 2 * BLOCK_ST
    return dict(
        z=pl.BlockSpec((t, BLOCK_CH), lambda b, ti: (tt(ti), b)),
        bbt=pl.BlockSpec((None, BLOCK_CH, ns2), lambda b, ti: (b, 0, 0)),
        ct=pl.BlockSpec((None, ns2, BLOCK_CH), lambda b, ti: (b, 0, 0)),
        vec=pl.BlockSpec((1, BLOCK_CH), lambda b, ti: (0, b)),
        wg=pl.BlockSpec((None, BLOCK_CH, BLOCK_CH), lambda b, ti: (b, 0, 0)),
        p=pl.BlockSpec((None, SUBLANES, ns2), lambda b, ti: (b, 0, 0)),
        hb=pl.BlockSpec((None, None, SUBLANES, ns2), lambda b, ti: (b, tt(ti), 0, 0)),
        acc_vec=pl.BlockSpec((None, 1, ns2), lambda b, ti: (b, 0, 0)),
    )


def _ssm_fwd(z, bbt, ct, dvec, wg, bglu, ptab):
    s = z.shape[0]
    nb = bbt.shape[0]
    t = _tile(s, TIME_TILE, SUBLANES)
    nt = s // t
    ns = BLOCK_ST
    sp = _ssm_specs(nb, nt, t, False)

    def body(z_ref, bbt_ref, ct_ref, d_ref, wg_ref, bg_ref, p_ref, y2_ref, hb_ref, bu_scr, h_scr, carry_scr):
        ti = pl.program_id(1)

        @pl.when(ti == 0)
        def _():
            carry_scr[...] = jnp.zeros_like(carry_scr)

        hb_ref[...] = carry_scr[...]
        carry_in = (carry_scr[0:1, 0:ns], carry_scr[0:1, ns:2 * ns])
        _, yg, gate, (cr, ci) = _ssm_block_fwd(z_ref[...], bbt_ref, ct_ref, d_ref, wg_ref, bg_ref, p_ref,
                                               bu_scr, h_scr, carry_in, t // SUBLANES)
        y2_ref[...] = yg * gate
        carry_scr[:, 0:ns] = jnp.broadcast_to(cr, (SUBLANES, ns))
        carry_scr[:, ns:2 * ns] = jnp.broadcast_to(ci, (SUBLANES, ns))

    return pl.pallas_call(
        body, name="ssm_fwd", grid=(nb, nt),
        out_shape=[jax.ShapeDtypeStruct((s, nb * BLOCK_CH), _F32), jax.ShapeDtypeStruct((nb, nt, SUBLANES, 2 * ns), _F32)],
        in_specs=[sp["z"], sp["bbt"], sp["ct"], sp["vec"], sp["wg"], sp["vec"], sp["p"]],
        out_specs=[sp["z"], sp["hb"]],
        scratch_shapes=[pltpu.VMEM((t, 2 * ns), _F32), pltpu.VMEM((t, 2 * ns), _F32), pltpu.VMEM((SUBLANES, 2 * ns), _F32)],
        compiler_params=_cp("parallel", "arbitrary"),
    )(z, bbt, ct, dvec, wg, bglu, ptab)


def _ssm_bwd(z, dy2, hb, bbt, ct, dvec, wg, bglu, ptab, ptab_rev):
    s = z.shape[0]
    nb = bbt.shape[0]
    t = _tile(s, TIME_TILE, SUBLANES)
    nt = s // t
    ns = BLOCK_ST
    sp = _ssm_specs(nb, nt, t, True)
    tn_dims = (((0,), (0,)), ((), ()))
    nt_dims = (((1,), (1,)), ((), ()))

    def body(z_ref, dy2_ref, hb_ref, bbt_ref, ct_ref, d_ref, wg_ref, bg_ref, p_ref, pr_ref,
             dz_ref, dbbt_ref, dct_ref, dwg_ref, dlb_ref, dd_ref, dbg_ref, bu_scr, h_scr, g_scr, gcarry_scr):
        first = pl.program_id(1) == 0

        @pl.when(first)
        def _():
            gcarry_scr[...] = jnp.zeros_like(gcarry_scr)

        u = z_ref[...]
        hin = hb_ref[...]
        carry_in = (hin[0:1, 0:ns], hin[0:1, ns:2 * ns])
        y, yg, gate, _ = _ssm_block_fwd(u, bbt_ref, ct_ref, d_ref, wg_ref, bg_ref, p_ref, bu_scr, h_scr, carry_in,
                                        t // SUBLANES)
        dy2 = dy2_ref[...]
        dpre = dy2 * yg * gate * (1.0 - gate)
        _acc(dbg_ref, first, _colsum(dpre))
        dpx = dpre.astype(_MXU)
        _acc(dwg_ref, first, lax.dot_general(yg.astype(_MXU), dpx, tn_dims, preferred_element_type=_F32))
        dyg = dy2 * gate + lax.dot_general(dpx, wg_ref[...].astype(_MXU), nt_dims, preferred_element_type=_F32)
        dy = dyg * _gelu_grad(y)
        _acc(dd_ref, first, _colsum(dy * u))
        dyx = dy.astype(_MXU)
        h = h_scr[...]
        _acc(dct_ref, first, lax.dot_general(h.astype(_MXU), dyx, tn_dims, preferred_element_type=_F32))
        bu_scr[...] = lax.dot_general(dyx, ct_ref[...].astype(_MXU), nt_dims, preferred_element_type=_F32)
        gin = (gcarry_scr[0:1, 0:ns], gcarry_scr[0:1, ns:2 * ns])
        ptab = (pr_ref[:, 0:ns], pr_ref[:, ns:2 * ns])
        gr, gi = _scan_rows(bu_scr, g_scr, t // SUBLANES, _scan_consts(p_ref, True), ptab, gin, True)
        gcarry_scr[:, 0:ns] = jnp.broadcast_to(gr, (SUBLANES, ns))
        gcarry_scr[:, ns:2 * ns] = jnp.broadcast_to(gi, (SUBLANES, ns))
        g = g_scr[...]
        row = lax.broadcasted_iota(jnp.int32, (t, ns), 0)
        hp_re = jnp.where(row == 0, hin[0:1, 0:ns], pltpu.roll(h[:, 0:ns], 1, 0))
        hp_im = jnp.where(row == 0, hin[0:1, ns:2 * ns], pltpu.roll(h[:, ns:2 * ns], 1, 0))
        g_re, g_im = g[:, 0:ns], g[:, ns:2 * ns]
        d_ar = _colsum(g_re * hp_re + g_im * hp_im)
        d_ai = _colsum(g_im * hp_re - g_re * hp_im)
        _acc(dlb_ref, first, jnp.concatenate([d_ar, d_ai], axis=1))
        gx = g.astype(_MXU)
        _acc(dbbt_ref, first, lax.dot_general(u.astype(_MXU), gx, tn_dims, preferred_element_type=_F32))
        dz_ref[...] = (dy * d_ref[...] + lax.dot_general(gx, bbt_ref[...].astype(_MXU), nt_dims,
                                                         preferred_element_type=_F32)).astype(dz_ref.dtype)

    f = lambda shape: jax.ShapeDtypeStruct(shape, _F32)
    return pl.pallas_call(
        body, name="ssm_bwd", grid=(nb, nt),
        out_shape=[jax.ShapeDtypeStruct((s, nb * BLOCK_CH), _MXU), f(bbt.shape), f(ct.shape), f(wg.shape), f((nb, 1, 2 * ns)),
                   f((1, nb * BLOCK_CH)), f((1, nb * BLOCK_CH))],
        in_specs=[sp["z"], sp["z"], sp["hb"], sp["bbt"], sp["ct"], sp["vec"], sp["wg"], sp["vec"], sp["p"], sp["p"]],
        out_specs=[sp["z"], sp["bbt"], sp["ct"], sp["wg"], sp["acc_vec"], sp["vec"], sp["vec"]],
        scratch_shapes=[pltpu.VMEM((t, 2 * ns), _F32), pltpu.VMEM((t, 2 * ns), _F32), pltpu.VMEM((t, 2 * ns), _F32),
                        pltpu.VMEM((SUBLANES, 2 * ns), _F32)],
        compiler_params=_cp("parallel", "arbitrary"),
    )(z, dy2, hb, bbt, ct, dvec, wg, bglu, ptab, ptab_rev)


def _mod_part(c_all, w, b):
    d, ns = w.shape
    tn = _tile(ns, 512)

    def body(c_ref, w_ref, b_ref, o_ref):
        c = c_ref[...]
        ca = (c * _sigmoid(c)).astype(_MXU)
        o_ref[...] = jnp.dot(ca, w_ref[...].astype(_MXU), preferred_element_type=_F32) + b_ref[...]

    return pl.pallas_call(
        body, name="mod_part", grid=(ns // tn,), out_shape=jax.ShapeDtypeStruct((8, ns), _F32),
        in_specs=[pl.BlockSpec((8, d), lambda n: (0, 0)), pl.BlockSpec((d, tn), lambda n: (0, n)),
                  pl.BlockSpec((1, tn), lambda n: (0, n))],
        out_specs=pl.BlockSpec((8, tn), lambda n: (0, n)), compiler_params=_cp("parallel"),
    )(c_all, w, b)


def _adamw_math(w, g, m, v):
    m = ADAM_B1 * m + (1.0 - ADAM_B1) * g
    v = ADAM_B2 * v + (1.0 - ADAM_B2) * (g * g)
    m_hat = m / (1.0 - ADAM_B1 ** ADAM_STEP)
    v_hat = v / (1.0 - ADAM_B2 ** ADAM_STEP)
    delta = -ADAM_LR * (m_hat / (jnp.sqrt(v_hat) + ADAM_EPS) + ADAM_WD * w)
    return delta, m, v


def _adamw(w, g, m, v, name):
    r, c = w.shape
    tr, tc = _tile(r, 256, SUBLANES), _tile(c, 1024)

    def body(w_ref, g_ref, m_ref, v_ref, d_ref, mo_ref, vo_ref):
        d_ref[...], mo_ref[...], vo_ref[...] = _adamw_math(w_ref[...], g_ref[...], m_ref[...], v_ref[...])

    spec = pl.BlockSpec((tr, tc), lambda i, j: (i, j))
    out = jax.ShapeDtypeStruct((r, c), _F32)
    return pl.pallas_call(
        body, name=name, grid=(r // tr, c // tc), in_specs=[spec] * 4, out_specs=[spec] * 3, out_shape=[out] * 3,
        compiler_params=_cp("parallel", "parallel"),
    )(w, g, m, v)


def _adamw_halves(w, g2, m, v, name):
    r, c = w.shape
    tr, tc = _tile(r, 256, SUBLANES), _tile(c // 2, 1024)
    nph = (c // 2) // tc

    def body(w_ref, g_ref, m_ref, v_ref, go_ref, d_ref, mo_ref, vo_ref):
        g = g_ref[...]
        go_ref[...] = g
        d_ref[...], mo_ref[...], vo_ref[...] = _adamw_math(w_ref[...], g, m_ref[...], v_ref[...])

    spec = pl.BlockSpec((tr, tc), lambda i, j: (i, j))
    out = jax.ShapeDtypeStruct((r, c), _F32)
    return pl.pallas_call(
        body, name=name, grid=(r // tr, c // tc),
        in_specs=[spec, pl.BlockSpec((None, tr, tc), lambda i, j: (j // nph, i, j % nph)), spec, spec],
        out_specs=[spec] * 4, out_shape=[out] * 4, compiler_params=_cp("parallel", "parallel"),
    )(w, g2, m, v)


def _wada_update(c_t, dm, w, m, v):
    d, ns = w.shape
    tr, tc = _tile(d, 256, SUBLANES), _tile(ns, 1024)

    def body(c_ref, dm_ref, w_ref, m_ref, v_ref, g_ref, d_ref, mo_ref, vo_ref):
        c = c_ref[...]
        ca = c * _sigmoid(c)
        dmv = dm_ref[...]
        g = ca[:, 0:1] * dmv[0:1, :]
        for b in range(1, 8):
            g = g + ca[:, b:b + 1] * dmv[b:b + 1, :]
        g_ref[...] = g
        d_ref[...], mo_ref[...], vo_ref[...] = _adamw_math(w_ref[...], g, m_ref[...], v_ref[...])

    spec = pl.BlockSpec((tr, tc), lambda i, j: (i, j))
    out = jax.ShapeDtypeStruct((d, ns), _F32)
    return pl.pallas_call(
        body, name="wada_update", grid=(d // tr, ns // tc),
        in_specs=[pl.BlockSpec((tr, 8), lambda i, j: (i, 0)), pl.BlockSpec((8, tc), lambda i, j: (0, j)), spec, spec, spec],
        out_specs=[spec] * 4, out_shape=[out] * 4, compiler_params=_cp("parallel", "parallel"),
    )(c_t, dm, w, m, v)


def _small_reduce_adamw(gathered, w, m, v):
    _, r, c = gathered.shape
    tr = _tile(r, 256, SUBLANES)

    def body(q_ref, w_ref, m_ref, v_ref, g_ref, d_ref, mo_ref, vo_ref):
        g = q_ref[0]
        for k in range(1, 8):
            g = g + q_ref[k]
        g_ref[...] = g
        d_ref[...], mo_ref[...], vo_ref[...] = _adamw_math(w_ref[...], g, m_ref[...], v_ref[...])

    spec = pl.BlockSpec((tr, c), lambda i: (i, 0))
    out = jax.ShapeDtypeStruct((r, c), _F32)
    return pl.pallas_call(
        body, name="small_reduce_adamw", grid=(r // tr,),
        in_specs=[pl.BlockSpec((8, tr, c), lambda i: (0, i, 0)), spec, spec, spec],
        out_specs=[spec] * 4, out_shape=[out] * 4, compiler_params=_cp("parallel"),
    )(gathered, w, m, v)


def _block_diag(x):
    nb, g, p, q = x.shape
    eye = jnp.eye(g, dtype=x.dtype)
    return (x[:, :, :, None, :] * eye[None, :, None, :, None]).reshape(nb, g * p, g * q)


def _block_diag_take(x, p, q):
    nb = x.shape[0]
    g = GROUPS_PER_BLOCK
    eye = jnp.eye(g, dtype=x.dtype)
    return jnp.sum(x.reshape(nb, g, p, g, q) * eye[None, :, None, :, None], axis=3)


class _Pack:
    def __init__(self, shapes):
        self.shapes = shapes
        self.offsets = {}
        off = 0
        for name, shape in shapes.items():
            n = math.prod(shape)
            self.offsets[name] = (off, n)
            off += -(-n // (SUBLANES * LANES)) * (SUBLANES * LANES)
        self.rows = -(-off // (256 * LANES)) * 256

    def pack(self, arrays):
        parts = []
        off = 0
        for name, shape in self.shapes.items():
            start, n = self.offsets[name]
            if start > off:
                parts.append(jnp.zeros((start - off,), _F32))
            parts.append(arrays[name].reshape(-1).astype(_F32))
            off = start + n
        total = self.rows * LANES
        if total > off:
            parts.append(jnp.zeros((total - off,), _F32))
        return jnp.concatenate(parts).reshape(self.rows, LANES)

    def unpack(self, buf):
        flat = buf.reshape(-1)
        return {name: flat[start:start + n].reshape(self.shapes[name]) for name, (start, n) in self.offsets.items()}


_SMALL = ["b_ada", "g_pre_mix", "g_post_mix", "ssm_log_dt", "ssm_a_re", "ssm_a_im", "ssm_b_re", "ssm_b_im", "ssm_c_re",
          "ssm_c_im", "ssm_d", "ssm_w_glu", "ssm_b_glu", "sgu_ln_g", "sgu_ln_b", "sgu_w", "sgu_b", "g_out_ssm",
          "g_out_sgu", "g_pre_ffn", "g_post_ffn", "conv_b"]
_WEIGHTS = ["w_ada", "b_ada", "g_pre_mix", "g_post_mix", "w_in", "ssm_log_dt", "ssm_a_re", "ssm_a_im", "ssm_b_re",
            "ssm_b_im", "ssm_c_re", "ssm_c_im", "ssm_d", "ssm_w_glu", "ssm_b_glu", "sgu_ln_g", "sgu_ln_b", "sgu_w", "sgu_b",
            "g_out_ssm", "g_out_sgu", "w_out", "g_pre_ffn", "g_post_ffn", "w_up", "conv_w", "conv_b", "w_down"]


def _step(p, m, v, x, c, tgt):
    s, d = x.shape
    mx, my, mc = lax.axis_index("x"), lax.axis_index("y"), lax.axis_index("c")
    chip = 2 * mx + my
    dev = 4 * mx + 2 * my + mc
    sel = jnp.stack([chip, mc]).astype(jnp.int32)
    g_cnt, n_st = p["ssm_a_re"].shape
    nb = g_cnt // GROUPS_PER_BLOCK
    gn = g_cnt * n_st
    d_ssm = g_cnt * SSM_GROUP
    nh = p["sgu_w"].shape[0]
    assert nh * CHUNK == d_ssm and 2 * d_ssm == d and n_st == SSM_STATE

    w_in4, w_out4, w_up4, w_down4 = [
        g.reshape(4, g.shape[1] * g.shape[2], g.shape[3])
        for g in _gather_weights([_cast_into_slot(p[n], sel) for n in ("w_in", "w_out", "w_up", "w_down")])]
    w_out_full = w_out4.reshape(1, d, d)
    w_down_full = w_down4.reshape(1, -1, d)

    ns_ada = p["w_ada"].shape[1]
    nc_conv = p["conv_w"].shape[1]
    first = jnp.concatenate([jnp.broadcast_to(c, (8, d)), jnp.pad(p["conv_w"], ((0, 5), (0, 0)))], axis=1)
    first_all = _all_gather8(_own_slot(first, dev), "gather_c_conv")
    c_all = first_all[:, 0, :d]
    conv_w_full = jnp.concatenate([first_all[2 * j, 0:3, d:] for j in range(4)], axis=1)
    b_ada_mine = lax.dynamic_slice_in_dim(p["b_ada"], chip * ns_ada, ns_ada, axis=1)
    mod_all = _all_gather8(_own_slot(_mod_part(c_all, p["w_ada"], b_ada_mine), dev), "gather_mod")
    mod_rows = lax.dynamic_index_in_dim(mod_all, dev, axis=1, keepdims=False)
    mod = jnp.concatenate([mod_rows[0], mod_rows[2], mod_rows[4], mod_rows[6]]).reshape(N_MOD, 1, d)
    sh1, sc1, gt1, sh2, sc2, gt2 = [mod[i] for i in range(N_MOD)]

    ldt_l = jnp.repeat(p["ssm_log_dt"], n_st, axis=1)
    are_l, aim_l = p["ssm_a_re"].reshape(1, gn), p["ssm_a_im"].reshape(1, gn)
    bre_t, bim_t = p["ssm_b_re"].reshape(gn, SSM_GROUP).T, p["ssm_b_im"].reshape(gn, SSM_GROUP).T
    pw_re, pw_im, bb_re, bb_im = _ssm_prep(ldt_l, are_l, aim_l, bre_t, bim_t)
    blocks = lambda t: t.reshape(t.shape[0], nb, GROUPS_PER_BLOCK * n_st).transpose(1, 0, 2)
    ptab = jnp.concatenate([blocks(pw_re), blocks(pw_im)], axis=2)
    ptab_rev = jnp.concatenate([blocks(pw_re)[:, ::-1], -blocks(pw_im)[:, ::-1]], axis=2)
    bd = lambda t: t.reshape(SSM_GROUP, nb, GROUPS_PER_BLOCK, n_st).transpose(1, 2, 0, 3)
    bbt = jnp.concatenate([_block_diag(bd(bb_re)), _block_diag(bd(bb_im))], axis=2).astype(_MXU)
    cd = lambda t: t.reshape(nb, GROUPS_PER_BLOCK, SSM_GROUP, n_st).transpose(0, 1, 3, 2)
    ct = jnp.concatenate([_block_diag(cd(p["ssm_c_re"])), -_block_diag(cd(p["ssm_c_im"]))], axis=1).astype(_MXU)
    wg = _block_diag(p["ssm_w_glu"].reshape(nb, GROUPS_PER_BLOCK, SSM_GROUP, SSM_GROUP)).astype(_MXU)
    dvec = p["ssm_d"]
    bglu = p["ssm_b_glu"].reshape(1, d_ssm)
    mask = jnp.tril(jnp.ones((CHUNK, CHUNK), _F32))
    wm = (p["sgu_w"] * mask[None]).astype(_MXU)
    bs = p["sgu_b"].reshape(nh, CHUNK, 1)

    h1 = _fwd_pre_mix(x, p["g_pre_mix"], sc1, sh1)
    z = _mm_nn(h1, w_in4, _F32, "mm_in")
    y_ssm, hb = _ssm_fwd(z, bbt, ct, dvec, wg, bglu, ptab)
    y_sgu = _sgu_fwd(z, p["sgu_ln_g"], p["sgu_ln_b"], wm, bs)
    ycat = _mix_norm_fwd(y_ssm, y_sgu, p["g_out_ssm"], p["g_out_sgu"])
    o = _mm_nn(ycat, w_out_full, _F32, "mm_out")
    x1, h2 = _fwd_mid(o, x, gt1, p["g_post_mix"], p["g_pre_ffn"], sc2, sh2)
    up_pre = _mm_nn(h2, w_up4, _F32, "mm_up")
    act = _conv_act_fwd(up_pre, conv_w_full, p["conv_b"])
    f = _mm_nn(act, w_down_full, _F32, "mm_down", tk=2816)
    dx2, df, d_gt2, d_g_post_ffn, loss = _loss_and_post_ffn_bwd(f, x1, tgt, gt2, p["g_post_ffn"])

    d_act = _mm_nt(df, w_down_full, _F32, "mm_d_act")
    gw_down = _mm_tn_rows(act, df, "mm_gw_down")
    d_up_pre, d_cw0, d_cw1, d_cw2, d_conv_b = _conv_act_bwd(up_pre, d_act, conv_w_full, p["conv_b"])
    dh2 = _mm_nt(d_up_pre, w_up4, _F32, "mm_dh2")
    gw_up = _mm_tn_cols(h2, d_up_pre, "mm_gw_up")
    dx1, d_o, d_sc2, d_sh2, d_g_pre_ffn, d_gt1, d_g_post_mix = _bwd_mid(
        dh2, x1, dx2, o, p["g_pre_ffn"], sc2, gt1, p["g_post_mix"])
    d_ycat = _mm_nt(d_o, w_out_full, _F32, "mm_d_ycat")
    gw_out = _mm_tn_rows(ycat, d_o, "mm_gw_out")
    dy_ssm, dy_sgu, d_g_out_ssm, d_g_out_sgu = _mix_norm_bwd(d_ycat, y_ssm, y_sgu, p["g_out_ssm"], p["g_out_sgu"])
    dz_ssm, d_bbt, d_ct, d_wg, d_lb, d_ssm_d, d_bglu = _ssm_bwd(z, dy_ssm, hb, bbt, ct, dvec, wg, bglu, ptab, ptab_rev)
    dz, d_ln_g, d_ln_b, d_wm, d_bs = _sgu_bwd(z, dy_sgu, dz_ssm, p["sgu_ln_g"], p["sgu_ln_b"], wm, bs)
    dh1 = _mm_nt(dz, w_in4, _F32, "mm_dh1")
    gw_in = _mm_tn_cols(h1, dz, "mm_gw_in")
    dx, d_sc1, d_sh1, d_g_pre_mix = _bwd_pre_mix(dh1, x, dx1, p["g_pre_mix"], sc1)

    nsb = BLOCK_ST
    lanes = lambda t: t.transpose(2, 0, 1, 3).reshape(SSM_GROUP, gn)
    d_bbr = lanes(_block_diag_take(d_bbt[:, :, :nsb], SSM_GROUP, n_st))
    d_bbi = lanes(_block_diag_take(d_bbt[:, :, nsb:], SSM_GROUP, n_st))
    d_lr, d_li = d_lb[:, 0, :nsb].reshape(1, gn), d_lb[:, 0, nsb:].reshape(1, gn)
    d_bre_t, d_bim_t, d_are, d_aim, d_dt = _ssm_prep_bwd(ldt_l, are_l, aim_l, bre_t, bim_t, d_bbr, d_bbi, d_lr, d_li)
    d_log_dt = _group_sum(d_dt.reshape(g_cnt, n_st), p["ssm_log_dt"].reshape(g_cnt, 1))
    c_grad = lambda t: _block_diag_take(t, n_st, SSM_GROUP).transpose(0, 1, 3, 2).reshape(g_cnt, SSM_GROUP, n_st)
    small = {
        "b_ada": jnp.concatenate([d_sh1, d_sc1, d_gt1, d_sh2, d_sc2, d_gt2], axis=1),
        "g_pre_mix": d_g_pre_mix, "g_post_mix": d_g_post_mix,
        "ssm_log_dt": d_log_dt, "ssm_a_re": d_are, "ssm_a_im": d_aim,
        "ssm_b_re": d_bre_t.T, "ssm_b_im": d_bim_t.T,
        "ssm_c_re": c_grad(d_ct[:, :nsb, :]), "ssm_c_im": -c_grad(d_ct[:, nsb:, :]),
        "ssm_d": d_ssm_d, "ssm_w_glu": _block_diag_take(d_wg, SSM_GROUP, SSM_GROUP), "ssm_b_glu": d_bglu,
        "sgu_ln_g": d_ln_g, "sgu_ln_b": d_ln_b, "sgu_w": d_wm * mask[None], "sgu_b": d_bs,
        "g_out_ssm": d_g_out_ssm, "g_out_sgu": d_g_out_sgu, "g_pre_ffn": d_g_pre_ffn, "g_post_ffn": d_g_post_ffn,
        "conv_b": d_conv_b, "conv_w_all": jnp.concatenate([d_cw0, d_cw1, d_cw2], axis=0),
    }
    shapes = {name: p[name].shape for name in _SMALL}
    shapes["conv_w_all"] = (3, 4 * nc_conv)
    pk = _Pack(shapes)
    zeros_cw = jnp.zeros(shapes["conv_w_all"], _F32)
    gathered = _all_gather8(_own_slot(pk.pack(small), dev), "gather_small")
    g_pk, d_pk, m_pk, v_pk = _small_reduce_adamw(
        gathered, pk.pack({**{n: p[n] for n in _SMALL}, "conv_w_all": zeros_cw}),
        pk.pack({**{n: m[n] for n in _SMALL}, "conv_w_all": zeros_cw}),
        pk.pack({**{n: v[n] for n in _SMALL}, "conv_w_all": zeros_cw}))
    grads, deltas, new_m, new_v = pk.unpack(g_pk), pk.unpack(d_pk), pk.unpack(m_pk), pk.unpack(v_pk)

    grads["conv_w"] = lax.dynamic_slice_in_dim(grads.pop("conv_w_all"), chip * nc_conv, nc_conv, axis=1)
    deltas["conv_w"], new_m["conv_w"], new_v["conv_w"] = _adamw(p["conv_w"], grads["conv_w"], m["conv_w"], v["conv_w"],
                                                                 "adamw_conv_w")
    d_mod_all = gathered.reshape(8, -1)[:, :N_MOD * d]
    d_mod_mine = lax.dynamic_slice_in_dim(d_mod_all, chip * ns_ada, ns_ada, axis=1)
    grads["w_ada"], deltas["w_ada"], new_m["w_ada"], new_v["w_ada"] = _wada_update(
        c_all.T, d_mod_mine, p["w_ada"], m["w_ada"], v["w_ada"])

    big = ["w_in", "w_out", "w_up", "w_down"]
    gws = [gw_in, gw_out, gw_up, gw_down]
    got = _pair_swap(gws, "pair_swap")
    pairs = [_pair_sum(g, q, sel, "pair_sum_" + n) for g, q, n in zip(gws, got, big)]
    recv = _chip_scatter(pairs)
    mine = [_chip_sum(pr, q, sel, "chip_sum_" + n) for pr, q, n in zip(pairs, recv, big)]
    joined = _pair_join(mine)
    for n, j in zip(big, joined):
        if n in ("w_in", "w_up"):
            grads[n] = j.reshape(p[n].shape)
            deltas[n], new_m[n], new_v[n] = _adamw(p[n], grads[n], m[n], v[n], "adamw_" + n)
        else:
            grads[n], deltas[n], new_m[n], new_v[n] = _adamw_halves(p[n], j, m[n], v[n], "adamw_" + n)
    return loss[0, 0], dx, grads, deltas, new_m, new_v


def kernel(x, c, w_ada, b_ada, g_pre_mix, g_post_mix, w_in, ssm_log_dt, ssm_a_re, ssm_a_im, ssm_b_re, ssm_b_im, ssm_c_re, ssm_c_im, ssm_d, ssm_w_glu, ssm_b_glu, sgu_ln_g, sgu_ln_b, sgu_w, sgu_b, g_out_ssm, g_out_sgu, w_out, g_pre_ffn, g_post_ffn, w_up, conv_w, conv_b, w_down, loss_target, m_w_ada, m_b_ada, m_g_pre_mix, m_g_post_mix, m_w_in, m_ssm_log_dt, m_ssm_a_re, m_ssm_a_im, m_ssm_b_re, m_ssm_b_im, m_ssm_c_re, m_ssm_c_im, m_ssm_d, m_ssm_w_glu, m_ssm_b_glu, m_sgu_ln_g, m_sgu_ln_b, m_sgu_w, m_sgu_b, m_g_out_ssm, m_g_out_sgu, m_w_out, m_g_pre_ffn, m_g_post_ffn, m_w_up, m_conv_w, m_conv_b, m_w_down, v_w_ada, v_b_ada, v_g_pre_mix, v_g_post_mix, v_w_in, v_ssm_log_dt, v_ssm_a_re, v_ssm_a_im, v_ssm_b_re, v_ssm_b_im, v_ssm_c_re, v_ssm_c_im, v_ssm_d, v_ssm_w_glu, v_ssm_b_glu, v_sgu_ln_g, v_sgu_ln_b, v_sgu_w, v_sgu_b, v_g_out_ssm, v_g_out_sgu, v_w_out, v_g_pre_ffn, v_g_post_ffn, v_w_up, v_conv_w, v_conv_b, v_w_down):
    given = dict(locals())
    drop = lambda a: a if a.ndim == 2 else a[0]
    p = {n: drop(given[n]) for n in _WEIGHTS}
    m = {n: drop(given["m_" + n]) for n in _WEIGHTS}
    v = {n: drop(given["v_" + n]) for n in _WEIGHTS}
    loss, dx, grads, deltas, new_m, new_v = _step(p, m, v, x[0], c, loss_target[0])
    loss = lax.psum(loss, ("x", "y", "c"))
    outs = [loss, dx[None]]
    for group in (grads, deltas, new_m, new_v):
        outs += [group[n].reshape(given[n].shape) for n in _WEIGHTS]
    return tuple(outs)
```

```python
import functools
import math

import jax
import jax.numpy as jnp
from jax import lax
from jax.experimental import pallas as pl
from jax.experimental.pallas import tpu as pltpu

_F32 = jnp.float32
_MXU = jnp.bfloat16
_WIRE = jnp.bfloat16

EPS = 1e-6
SSM_GROUP = 16
SSM_STATE = 64
GROUPS_PER_BLOCK = 8
BLOCK_CH = SSM_GROUP * GROUPS_PER_BLOCK
BLOCK_ST = SSM_STATE * GROUPS_PER_BLOCK
CHUNK = 128
TIME_TILE = 256
SUBLANES = 8
LANES = 128
N_MOD = 6
ADAM_LR, ADAM_B1, ADAM_B2, ADAM_EPS, ADAM_WD, ADAM_STEP = 0.001, 0.9, 0.999, 1e-08, 0.01, 10
_VMEM_LIMIT = 56 * 1024 * 1024
_MESH = pl.DeviceIdType.MESH
_ANY = pl.BlockSpec(memory_space=pl.ANY)
_HBM = pl.BlockSpec(memory_space=pltpu.HBM)
_SEM = pl.BlockSpec(memory_space=pltpu.SEMAPHORE)
_VMEM_WHOLE = pl.BlockSpec(memory_space=pltpu.VMEM)
_EFFECT = pltpu.SideEffectType.DATAFLOW_SIDE_EFFECTING
_GELU_C = math.sqrt(2.0 / math.pi)


def _cp(*sem):
    return pltpu.CompilerParams(dimension_semantics=sem, vmem_limit_bytes=_VMEM_LIMIT)


def _tile(dim, target, align=LANES):
    if dim <= target:
        return dim
    best = None
    for t in range(align, target + 1, align):
        if dim % t == 0:
            best = t
    assert best is not None, (dim, target, align)
    return best


def _gelu(x):
    return 0.5 * x * (1.0 + jnp.tanh(_GELU_C * (x + 0.044715 * (x * x * x))))


def _gelu_grad(x):
    t = jnp.tanh(_GELU_C * (x + 0.044715 * (x * x * x)))
    return 0.5 * (1.0 + t) + 0.5 * x * (1.0 - t * t) * (_GELU_C * (1.0 + 3.0 * 0.044715 * x * x))


def _sigmoid(x):
    return 1.0 / (1.0 + jnp.exp(-x))


def _colsum(x):
    return jnp.sum(x, axis=0, keepdims=True)


def _rowmean(x):
    return jnp.mean(x, axis=-1, keepdims=True)


def _acc(ref, first, val):
    @pl.when(first)
    def _():
        ref[...] = val

    @pl.when(jnp.logical_not(first))
    def _():
        ref[...] += val


def _place():
    mx, my, mc = lax.axis_index("x"), lax.axis_index("y"), lax.axis_index("c")
    chips = [(1 - mx, my), (mx, 1 - my), (1 - mx, 1 - my)]
    return mx, my, mc, chips


def _all_gather8(buf, name):
    def body(in_ref, out_ref, send_sems, recv_sems):
        mx, my, mc, chips = _place()
        me, sibling = (mx, my, mc), (mx, my, 1 - mc)

        def slot(ref, px, py, pc):
            return ref.at[4 * px + 2 * py + pc]

        def copy(k, block, to, src_ref=out_ref):
            return pltpu.make_async_remote_copy(
                src_ref=slot(src_ref, *block), dst_ref=slot(out_ref, *block),
                send_sem=send_sems.at[k], recv_sem=recv_sems.at[k], device_id=to, device_id_type=_MESH)

        first = [copy(0, me, sibling, in_ref)]
        first += [copy(1 + j, me, (*chip, mc), in_ref) for j, chip in enumerate(chips)]
        for cp in first:
            cp.start()
        passed = [copy(4 + j, (*chip, mc), sibling) for j, chip in enumerate(chips)]
        for j, chip in enumerate(chips):
            copy(1 + j, (*chip, mc), me).wait_recv()
            passed[j].start()
        copy(0, sibling, me).wait_recv()
        for j, chip in enumerate(chips):
            copy(4 + j, (*chip, 1 - mc), me).wait_recv()
        for cp in first + passed:
            cp.wait_send()

    return pl.pallas_call(
        body, name=name, out_shape=jax.ShapeDtypeStruct(buf.shape, buf.dtype),
        in_specs=[_ANY], out_specs=_ANY, input_output_aliases={0: 0},
        scratch_shapes=[pltpu.SemaphoreType.DMA((7,)), pltpu.SemaphoreType.DMA((7,))],
    )(buf)


def _own_slot(x, dev):
    return lax.dynamic_update_slice(jnp.zeros((8,) + x.shape, x.dtype), x[None], (dev, 0, 0))


def _cast_into_slot(w, sel):
    r, c = w.shape
    hr = r // 2
    tr = _tile(hr, 256, 16)
    nr = hr // tr

    def body(sel_ref, w_ref, o_ref):
        o_ref[...] = w_ref[...].astype(o_ref.dtype)

    return pl.pallas_call(
        body, name="cast_into_slot", out_shape=jax.ShapeDtypeStruct((4, 2, hr, c), _WIRE),
        grid_spec=pltpu.PrefetchScalarGridSpec(
            num_scalar_prefetch=1, grid=(2, nr),
            in_specs=[pl.BlockSpec((tr, c), lambda h, i, s: (h * nr + i, 0))],
            out_specs=pl.BlockSpec((None, None, tr, c), lambda h, i, s: (s[0], h, i, 0))),
        compiler_params=_cp("parallel", "parallel"),
    )(sel, w)


def _gather_weights(bufs):
    n = len(bufs)

    def body(*refs):
        ins, outs = refs[:n], refs[n:2 * n]
        send_sems, recv_sems = refs[2 * n:]
        mx, my, mc, chips = _place()
        sibling = (mx, my, 1 - mc)
        j_me = 2 * mx + my
        remote = []
        for i in range(n):
            for k, chip in enumerate(chips):
                cp = pltpu.make_async_remote_copy(
                    src_ref=ins[i].at[j_me, mc], dst_ref=outs[i].at[j_me, mc],
                    send_sem=send_sems.at[6 * i + k], recv_sem=recv_sems.at[6 * i + k],
                    device_id=(*chip, mc), device_id_type=_MESH)
                cp.start()
                remote.append(cp)
        for i in range(n):
            for k, chip in enumerate(chips):
                j_k = 2 * chip[0] + chip[1]
                landed = outs[i].at[j_k, mc]
                pltpu.make_async_remote_copy(
                    src_ref=landed, dst_ref=landed, send_sem=send_sems.at[6 * i + k], recv_sem=recv_sems.at[6 * i + k],
                    device_id=(*chip, mc), device_id_type=_MESH).wait_recv()
                cp = pltpu.make_async_remote_copy(
                    src_ref=landed, dst_ref=landed, send_sem=send_sems.at[6 * i + 3 + k],
                    recv_sem=recv_sems.at[6 * i + 3 + k], device_id=sibling, device_id_type=_MESH)
                cp.start()
                remote.append(cp)
        for i in range(n):
            for k, chip in enumerate(chips):
                j_k = 2 * chip[0] + chip[1]
                other = outs[i].at[j_k, 1 - mc]
                pltpu.make_async_remote_copy(
                    src_ref=other, dst_ref=other, send_sem=send_sems.at[6 * i + 3 + k],
                    recv_sem=recv_sems.at[6 * i + 3 + k], device_id=sibling, device_id_type=_MESH).wait_recv()
        for cp in remote:
            cp.wait_send()

    return pl.pallas_call(
        body, name="gather_weights", out_shape=[jax.ShapeDtypeStruct(b.shape, b.dtype) for b in bufs],
        in_specs=[_ANY] * n, out_specs=[_ANY] * n, input_output_aliases={i: i for i in range(n)},
        scratch_shapes=[pltpu.SemaphoreType.DMA((6 * n,)), pltpu.SemaphoreType.DMA((6 * n,))],
    )(*bufs)


def _hbm(a):
    return pltpu.with_memory_space_constraint(a, pltpu.HBM)


def _after(vec, token):
    return vec + token[0:1, 0:1]


def _gather_start(bufs, name):
    n = len(bufs)
    nc = 3 * n

    def body(*refs):
        ins, send, recv, token = refs[:n], refs[n:n + nc], refs[n + nc:n + 2 * nc], refs[-1]
        mx, my, mc, chips = _place()
        j_me = 2 * mx + my
        for i in range(n):
            for k, chip in enumerate(chips):
                half = ins[i].at[j_me, mc]
                pltpu.make_async_remote_copy(
                    src_ref=half, dst_ref=half, send_sem=send[3 * i + k], recv_sem=recv[3 * i + k],
                    device_id=(*chip, mc), device_id_type=_MESH).start()
        token[...] = jnp.zeros_like(token)

    outs = pl.pallas_call(
        body, name=name,
        out_shape=tuple([pltpu.SemaphoreType.DMA(())] * (2 * nc) + [pltpu.HBM(b.shape, b.dtype) for b in bufs]
                        + [jax.ShapeDtypeStruct((SUBLANES, LANES), _F32)]),
        in_specs=tuple([_HBM] * n), out_specs=tuple([_SEM] * (2 * nc) + [_HBM] * n + [_VMEM_WHOLE]),
        input_output_aliases={i: 2 * nc + i for i in range(n)},
        compiler_params=pltpu.CompilerParams(has_side_effects=_EFFECT),
    )(*[_hbm(b) for b in bufs])
    sems = [(outs[3 * i:3 * i + 3], outs[nc + 3 * i:nc + 3 * i + 3]) for i in range(n)]
    return sems, list(outs[2 * nc:2 * nc + n]), outs[-1]


def _gather_wait(sems, buf, after, name):
    send, recv = sems

    def body(buf_ref, s0, s1, s2, r0, r1, r2, after_ref, out_ref):
        mx, my, mc, chips = _place()
        j_me = 2 * mx + my
        for k, (chip, s_k, r_k) in enumerate(zip(chips, (s0, s1, s2), (r0, r1, r2))):
            cp = pltpu.make_async_remote_copy(
                src_ref=buf_ref.at[j_me, mc], dst_ref=buf_ref.at[2 * chip[0] + chip[1], mc], send_sem=s_k, recv_sem=r_k,
                device_id=(*chip, mc), device_id_type=_MESH)
            cp.wait_send()
            cp.wait_recv()

    return pl.pallas_call(
        body, name=name, out_shape=pltpu.HBM(buf.shape, buf.dtype),
        in_specs=(_HBM,) + (_SEM,) * 6 + (_ANY,), out_specs=_HBM, input_output_aliases={0: 0},
        compiler_params=pltpu.CompilerParams(has_side_effects=_EFFECT),
    )(buf, *send, *recv, after)


def _pair_forward(bufs, name):
    n = len(bufs)

    def body(*refs):
        ins, outs = refs[:n], refs[n:2 * n]
        send_sems, recv_sems = refs[2 * n:]
        mx, my, mc, chips = _place()
        sibling = (mx, my, 1 - mc)
        cps = []
        for i in range(n):
            for k, chip in enumerate(chips):
                j_k = 2 * chip[0] + chip[1]
                cp = pltpu.make_async_remote_copy(
                    src_ref=ins[i].at[j_k, mc], dst_ref=outs[i].at[j_k, mc], send_sem=send_sems.at[3 * i + k],
                    recv_sem=recv_sems.at[3 * i + k], device_id=sibling, device_id_type=_MESH)
                cp.start()
                cps.append(cp)
        for i in range(n):
            for k, chip in enumerate(chips):
                other = outs[i].at[2 * chip[0] + chip[1], 1 - mc]
                pltpu.make_async_remote_copy(
                    src_ref=other, dst_ref=other, send_sem=send_sems.at[3 * i + k], recv_sem=recv_sems.at[3 * i + k],
                    device_id=sibling, device_id_type=_MESH).wait_recv()
        for cp in cps:
            cp.wait_send()

    return pl.pallas_call(
        body, name=name, out_shape=[jax.ShapeDtypeStruct(b.shape, b.dtype) for b in bufs],
        in_specs=[_ANY] * n, out_specs=[_ANY] * n, input_output_aliases={i: i for i in range(n)},
        scratch_shapes=[pltpu.SemaphoreType.DMA((3 * n,)), pltpu.SemaphoreType.DMA((3 * n,))],
    )(*bufs)


def _scatter_start(pair, name):
    land = lax.empty((3,) + pair.shape[1:], pair.dtype)

    def body(pair_ref, land_ref, s0, s1, s2, r0, r1, r2, pair_thru, land_thru, token):
        mx, my, mc, chips = _place()
        for k, (chip, s_k, r_k) in enumerate(zip(chips, (s0, s1, s2), (r0, r1, r2))):
            pltpu.make_async_remote_copy(
                src_ref=pair_ref.at[2 * chip[0] + chip[1]], dst_ref=land_ref.at[k], send_sem=s_k, recv_sem=r_k,
                device_id=(*chip, mc), device_id_type=_MESH).start()
        token[...] = jnp.zeros_like(token)

    outs = pl.pallas_call(
        body, name=name,
        out_shape=tuple([pltpu.SemaphoreType.DMA(())] * 6 + [pltpu.HBM(pair.shape, pair.dtype), pltpu.HBM(land.shape, land.dtype),
                                                             jax.ShapeDtypeStruct((SUBLANES, LANES), _F32)]),
        in_specs=(_HBM, _HBM), out_specs=tuple([_SEM] * 6 + [_HBM, _HBM, _VMEM_WHOLE]),
        input_output_aliases={0: 6, 1: 7}, compiler_params=pltpu.CompilerParams(has_side_effects=_EFFECT),
    )(_hbm(pair), _hbm(land))
    return (outs[0:3], outs[3:6]), outs[6], outs[7], outs[8]


def _scatter_wait(sems, pair, land, after, name):
    send, recv = sems

    def body(pair_ref, land_ref, s0, s1, s2, r0, r1, r2, after_ref, pair_out, land_out):
        mx, my, mc, chips = _place()
        for k, (chip, s_k, r_k) in enumerate(zip(chips, (s0, s1, s2), (r0, r1, r2))):
            cp = pltpu.make_async_remote_copy(
                src_ref=pair_ref.at[2 * chip[0] + chip[1]], dst_ref=land_ref.at[k], send_sem=s_k, recv_sem=r_k,
                device_id=(*chip, mc), device_id_type=_MESH)
            cp.wait_send()
            cp.wait_recv()

    return pl.pallas_call(
        body, name=name, out_shape=(pltpu.HBM(pair.shape, pair.dtype), pltpu.HBM(land.shape, land.dtype)),
        in_specs=(_HBM, _HBM) + (_SEM,) * 6 + (_ANY,), out_specs=(_HBM, _HBM), input_output_aliases={0: 0, 1: 1},
        compiler_params=pltpu.CompilerParams(has_side_effects=_EFFECT),
    )(pair, land, *send, *recv, after)


def _pair_swap(arrs, name):
    n = len(arrs)

    def body(*refs):
        ins, outs = refs[:n], refs[n:2 * n]
        send_sems, recv_sems = refs[2 * n:]
        mx, my, mc, _ = _place()
        sibling = (mx, my, 1 - mc)
        cps = []
        for i in range(n):
            cp = pltpu.make_async_remote_copy(
                src_ref=ins[i].at[1 - mc], dst_ref=outs[i], send_sem=send_sems.at[i], recv_sem=recv_sems.at[i],
                device_id=sibling, device_id_type=_MESH)
            cp.start()
            cps.append(cp)
        for cp in cps:
            cp.wait()

    return pl.pallas_call(
        body, name=name, out_shape=[jax.ShapeDtypeStruct(a.shape[1:], a.dtype) for a in arrs],
        in_specs=[_ANY] * n, out_specs=[_ANY] * n,
        scratch_shapes=[pltpu.SemaphoreType.DMA((n,)), pltpu.SemaphoreType.DMA((n,))],
    )(*arrs)


def _chip_scatter(arrs):
    n = len(arrs)

    def body(*refs):
        ins, outs = refs[:n], refs[n:2 * n]
        send_sems, recv_sems = refs[2 * n:]
        mx, my, mc, chips = _place()
        cps = []
        for i in range(n):
            for k, chip in enumerate(chips):
                cp = pltpu.make_async_remote_copy(
                    src_ref=ins[i].at[2 * chip[0] + chip[1]], dst_ref=outs[i].at[k],
                    send_sem=send_sems.at[3 * i + k], recv_sem=recv_sems.at[3 * i + k],
                    device_id=(*chip, mc), device_id_type=_MESH)
                cp.start()
                cps.append(cp)
        for cp in cps:
            cp.wait()

    return pl.pallas_call(
        body, name="chip_scatter", out_shape=[jax.ShapeDtypeStruct((3,) + a.shape[1:], a.dtype) for a in arrs],
        in_specs=[_ANY] * n, out_specs=[_ANY] * n,
        scratch_shapes=[pltpu.SemaphoreType.DMA((3 * n,)), pltpu.SemaphoreType.DMA((3 * n,))],
    )(*arrs)


def _pair_join(bufs):
    n = len(bufs)

    def body(*refs):
        ins, outs = refs[:n], refs[n:2 * n]
        send_sems, recv_sems = refs[2 * n:]
        mx, my, mc, _ = _place()
        sibling = (mx, my, 1 - mc)
        cps = []
        for i in range(n):
            cp = pltpu.make_async_remote_copy(
                src_ref=ins[i].at[mc], dst_ref=outs[i].at[mc], send_sem=send_sems.at[i], recv_sem=recv_sems.at[i],
                device_id=sibling, device_id_type=_MESH)
            cp.start()
            cps.append(cp)
        for i in range(n):
            other = outs[i].at[1 - mc]
            pltpu.make_async_remote_copy(
                src_ref=other, dst_ref=other, send_sem=send_sems.at[i], recv_sem=recv_sems.at[i],
                device_id=sibling, device_id_type=_MESH).wait_recv()
        for cp in cps:
            cp.wait_send()

    return pl.pallas_call(
        body, name="pair_join", out_shape=[jax.ShapeDtypeStruct(b.shape, b.dtype) for b in bufs],
        in_specs=[_ANY] * n, out_specs=[_ANY] * n, input_output_aliases={i: i for i in range(n)},
        scratch_shapes=[pltpu.SemaphoreType.DMA((n,)), pltpu.SemaphoreType.DMA((n,))],
    )(*bufs)


def _pair_sum(g, got, sel, name):
    _, four, hr, c = g.shape
    tr = _tile(hr, 512, 16)

    def body(sel_ref, g_ref, p_ref, o_ref):
        o_ref[...] = (g_ref[...].astype(_F32) + p_ref[...].astype(_F32)).astype(o_ref.dtype)

    return pl.pallas_call(
        body, name=name, out_shape=jax.ShapeDtypeStruct((four, hr, c), g.dtype),
        grid_spec=pltpu.PrefetchScalarGridSpec(
            num_scalar_prefetch=1, grid=(four, hr // tr),
            in_specs=[pl.BlockSpec((None, None, tr, c), lambda j, i, s: (s[1], j, i, 0)),
                      pl.BlockSpec((None, tr, c), lambda j, i, s: (j, i, 0))],
            out_specs=pl.BlockSpec((None, tr, c), lambda j, i, s: (j, i, 0))),
        compiler_params=_cp("parallel", "parallel"),
    )(sel, g, got)


def _chip_sum(pair, got, sel, name):
    _, hr, c = pair.shape
    tr = _tile(hr, 512, 16)

    def body(sel_ref, p_ref, q_ref, o_ref):
        o_ref[...] = ((p_ref[...].astype(_F32) + q_ref[0].astype(_F32)) + q_ref[1].astype(_F32)) + q_ref[2].astype(_F32)

    return pl.pallas_call(
        body, name=name, out_shape=jax.ShapeDtypeStruct((2, hr, c), _F32),
        grid_spec=pltpu.PrefetchScalarGridSpec(
            num_scalar_prefetch=1, grid=(hr // tr,),
            in_specs=[pl.BlockSpec((None, tr, c), lambda i, s: (s[0], i, 0)),
                      pl.BlockSpec((3, tr, c), lambda i, s: (0, i, 0))],
            out_specs=pl.BlockSpec((None, tr, c), lambda i, s: (s[1], i, 0))),
        compiler_params=_cp("parallel"),
    )(sel, pair, got)


def _matmul(a, b, dims, out_struct, grid, a_spec, b_spec, o_spec, acc_shape, k_axis, name):
    nk = grid[k_axis]

    def body(a_ref, b_ref, o_ref, acc_ref):
        prod = lax.dot_general(a_ref[...].astype(_MXU), b_ref[...].astype(_MXU), dims, preferred_element_type=_F32)
        if nk == 1:
            o_ref[...] = prod.astype(o_ref.dtype)
        else:
            k = pl.program_id(k_axis)

            @pl.when(k == 0)
            def _():
                acc_ref[...] = prod

            @pl.when(k > 0)
            def _():
                acc_ref[...] += prod

            @pl.when(k == nk - 1)
            def _():
                o_ref[...] = acc_ref[...].astype(o_ref.dtype)

    sem = ["parallel"] * len(grid)
    sem[k_axis] = "arbitrary"
    return pl.pallas_call(
        body, name=name, out_shape=out_struct, grid=grid, in_specs=[a_spec, b_spec], out_specs=o_spec,
        scratch_shapes=[pltpu.VMEM(acc_shape, _F32)], compiler_params=_cp(*sem),
    )(a, b)


def _mm_nn(a, w4, out_dtype, name, tm=512, tn=1536, tk=2048):
    m, k = a.shape
    j, _, ns = w4.shape
    tm, tn, tk = _tile(m, tm, 16), _tile(ns, tn), _tile(k, tk)
    nps = ns // tn
    return _matmul(
        a, w4, (((1,), (0,)), ((), ())), jax.ShapeDtypeStruct((m, j * ns), out_dtype),
        (m // tm, j * nps, k // tk),
        pl.BlockSpec((tm, tk), lambda mi, ni, ki: (mi, ki)),
        pl.BlockSpec((None, tk, tn), lambda mi, ni, ki: (ni // nps, ki, ni % nps)),
        pl.BlockSpec((tm, tn), lambda mi, ni, ki: (mi, ni)), (tm, tn), 2, name)


def _mm_nt(a, w4, out_dtype, name, tm=512, tn=2048, tk=1536):
    m = a.shape[-2]
    j, kw, ns = w4.shape
    tm, tn, tk = _tile(m, tm, 16), _tile(kw, tn), _tile(ns, tk)
    kps = ns // tk
    if a.ndim == 3:
        kph = a.shape[2] // tk
        a_spec = pl.BlockSpec((None, tm, tk), lambda mi, ni, ki: (ki // kph, mi, ki % kph))
    else:
        a_spec = pl.BlockSpec((tm, tk), lambda mi, ni, ki: (mi, ki))
    return _matmul(
        a, w4, (((1,), (1,)), ((), ())), jax.ShapeDtypeStruct((m, kw), out_dtype),
        (m // tm, kw // tn, j * kps),
        a_spec,
        pl.BlockSpec((None, tn, tk), lambda mi, ni, ki: (ki // kps, ni, ki % kps)),
        pl.BlockSpec((tm, tn), lambda mi, ni, ki: (mi, ni)), (tm, tn), 2, name)


def _mm_tn_cols(a, b, name, tm=1024, tn=1536, tk=2048):
    m, ka = a.shape
    ns = (b.shape[-1] * (2 if b.ndim == 3 else 1)) // 4
    hr = ka // 2
    tm, tn, tk = _tile(hr, tm), _tile(ns, tn), _tile(m, tk, 16)
    mph, nps = hr // tm, ns // tn
    if b.ndim == 3:
        b_spec = pl.BlockSpec((None, tk, tn), lambda ni, mi, ki: (ni // (2 * nps), ki, ni % (2 * nps)))
    else:
        b_spec = pl.BlockSpec((tk, tn), lambda ni, mi, ki: (ki, ni))
    return _matmul(
        a, b, (((0,), (0,)), ((), ())), jax.ShapeDtypeStruct((2, 4, hr, ns), _WIRE),
        (4 * nps, 2 * mph, m // tk),
        pl.BlockSpec((tk, tm), lambda ni, mi, ki: (ki, mi)),
        b_spec,
        pl.BlockSpec((None, None, tm, tn), lambda ni, mi, ki: (mi // mph, ni // nps, mi % mph, ni % nps)),
        (tm, tn), 2, name)


def _mm_tn_rows(a, b, name, tm=1536, tn=1024, tk=2048):
    m, ka = a.shape
    r = ka // 4
    hc = b.shape[1] // 2
    tm, tn, tk = _tile(r, tm), _tile(hc, tn), _tile(m, tk, 16)
    mpr, nph = r // tm, hc // tn
    return _matmul(
        a, b, (((0,), (0,)), ((), ())), jax.ShapeDtypeStruct((2, 4, r, hc), _WIRE),
        (2 * nph, 4 * mpr, m // tk),
        pl.BlockSpec((tk, tm), lambda ni, mi, ki: (ki, mi)),
        pl.BlockSpec((tk, tn), lambda ni, mi, ki: (ki, ni)),
        pl.BlockSpec((None, None, tm, tn), lambda ni, mi, ki: (ni // nph, mi // mpr, mi % mpr, ni % nph)),
        (tm, tn), 2, name)


def _row_call(body, name, rows, ins, outs, tm=256):
    tm = _tile(rows, tm, 16)

    def spec(shape, kind):
        if kind == "rows":
            return pl.BlockSpec((tm, shape[1]), lambda i: (i, 0))
        return pl.BlockSpec(shape, lambda i: (0,) * len(shape))

    return pl.pallas_call(
        body, name=name, grid=(rows // tm,),
        in_specs=[spec(a.shape, kind) for a, kind in ins],
        out_specs=[spec(o.shape, kind) for o, kind in outs],
        out_shape=[o for o, _ in outs],
        compiler_params=_cp("arbitrary"),
    )(*[a for a, _ in ins])


def _rms(x):
    r = lax.rsqrt(_rowmean(x * x) + EPS)
    return x * r, r


def _rms_bwd(dxh, xh, r):
    return r * (dxh - xh * _rowmean(dxh * xh))


def _fwd_pre_mix(x, g, sc, sh):
    s, d = x.shape

    def body(x_ref, g_ref, sc_ref, sh_ref, h_ref):
        xh, _ = _rms(x_ref[...])
        h_ref[...] = (xh * g_ref[...] * (1.0 + sc_ref[...]) + sh_ref[...]).astype(h_ref.dtype)

    return _row_call(body, "fwd_pre_mix", s, [(x, "rows"), (g, "vec"), (sc, "vec"), (sh, "vec")],
                     [(jax.ShapeDtypeStruct((s, d), _MXU), "rows")])[0]


def _fwd_mid(o, x, gt1, g_post, g_pre2, sc2, sh2):
    s, d = x.shape

    def body(o_ref, x_ref, gt_ref, gp_ref, g2_ref, sc_ref, sh_ref, x1_ref, h2_ref):
        oh, _ = _rms(o_ref[...])
        x1 = x_ref[...] + gt_ref[...] * (oh * gp_ref[...])
        x1_ref[...] = x1
        xh, _ = _rms(x1)
        h2_ref[...] = (xh * g2_ref[...] * (1.0 + sc_ref[...]) + sh_ref[...]).astype(h2_ref.dtype)

    return _row_call(body, "fwd_mid", s,
                     [(o, "rows"), (x, "rows"), (gt1, "vec"), (g_post, "vec"), (g_pre2, "vec"), (sc2, "vec"),
                      (sh2, "vec")],
                     [(jax.ShapeDtypeStruct((s, d), _F32), "rows"), (jax.ShapeDtypeStruct((s, d), _MXU), "rows")])


def _loss_and_post_ffn_bwd(f, x1, tgt, gt2, g_post):
    s, d = x1.shape

    def body(f_ref, x1_ref, t_ref, gt_ref, g_ref, dx2_ref, df_ref, dgt_ref, dg_ref, loss_ref):
        first = pl.program_id(0) == 0
        fh, r = _rms(f_ref[...])
        n = fh * g_ref[...]
        e = x1_ref[...] + gt_ref[...] * n - t_ref[...]
        _acc(loss_ref, first, jnp.sum(_colsum(e * e), axis=1, keepdims=True) * (0.5 / d))
        dx2 = e * (1.0 / d)
        dx2_ref[...] = dx2
        _acc(dgt_ref, first, _colsum(dx2 * n))
        dn = dx2 * gt_ref[...]
        _acc(dg_ref, first, _colsum(dn * fh))
        df_ref[...] = _rms_bwd(dn * g_ref[...], fh, r).astype(df_ref.dtype)

    vec = jax.ShapeDtypeStruct((1, d), _F32)
    return _row_call(body, "loss_post_ffn_bwd", s,
                     [(f, "rows"), (x1, "rows"), (tgt, "rows"), (gt2, "vec"), (g_post, "vec")],
                     [(jax.ShapeDtypeStruct((s, d), _F32), "rows"), (jax.ShapeDtypeStruct((s, d), _MXU), "rows"),
                      (vec, "vec"), (vec, "vec"), (jax.ShapeDtypeStruct((1, 1), _F32), "vec")])


def _bwd_mid(dh2, x1, dx2, o, g_pre2, sc2, gt1, g_post):
    s, d = x1.shape

    def body(dh_ref, x1_ref, dx2_ref, o_ref, g2_ref, sc_ref, gt_ref, gp_ref,
             dx1_ref, do_ref, dsc_ref, dsh_ref, dg2_ref, dgt_ref, dgp_ref):
        first = pl.program_id(0) == 0
        dh = dh_ref[...]
        xh, r = _rms(x1_ref[...])
        _acc(dsh_ref, first, _colsum(dh))
        _acc(dsc_ref, first, _colsum(dh * (xh * g2_ref[...])))
        dn = dh * (1.0 + sc_ref[...])
        _acc(dg2_ref, first, _colsum(dn * xh))
        dx1 = dx2_ref[...] + _rms_bwd(dn * g2_ref[...], xh, r)
        dx1_ref[...] = dx1
        oh, ro = _rms(o_ref[...])
        _acc(dgt_ref, first, _colsum(dx1 * (oh * gp_ref[...])))
        dno = dx1 * gt_ref[...]
        _acc(dgp_ref, first, _colsum(dno * oh))
        do_ref[...] = _rms_bwd(dno * gp_ref[...], oh, ro).astype(do_ref.dtype)

    vec = jax.ShapeDtypeStruct((1, d), _F32)
    return _row_call(body, "bwd_mid", s,
                     [(dh2, "rows"), (x1, "rows"), (dx2, "rows"), (o, "rows"), (g_pre2, "vec"), (sc2, "vec"),
                      (gt1, "vec"), (g_post, "vec")],
                     [(jax.ShapeDtypeStruct((s, d), _F32), "rows"), (jax.ShapeDtypeStruct((s, d), _MXU), "rows"),
                      (vec, "vec"), (vec, "vec"), (vec, "vec"), (vec, "vec"), (vec, "vec")])


def _bwd_pre_mix(dh1, x, dx1, g, sc1):
    s, d = x.shape

    def body(dh_ref, x_ref, dx1_ref, g_ref, sc_ref, dx_ref, dsc_ref, dsh_ref, dg_ref):
        first = pl.program_id(0) == 0
        dh = dh_ref[...]
        xh, r = _rms(x_ref[...])
        _acc(dsh_ref, first, _colsum(dh))
        _acc(dsc_ref, first, _colsum(dh * (xh * g_ref[...])))
        dn = dh * (1.0 + sc_ref[...])
        _acc(dg_ref, first, _colsum(dn * xh))
        dx_ref[...] = dx1_ref[...] + _rms_bwd(dn * g_ref[...], xh, r)

    vec = jax.ShapeDtypeStruct((1, d), _F32)
    return _row_call(body, "bwd_pre_mix", s,
                     [(dh1, "rows"), (x, "rows"), (dx1, "rows"), (g, "vec"), (sc1, "vec")],
                     [(jax.ShapeDtypeStruct((s, d), _F32), "rows"), (vec, "vec"), (vec, "vec"), (vec, "vec")])


def _mix_norm_fwd(y_ssm, y_sgu, g_ssm, g_sgu):
    s, h = y_ssm.shape

    def body(a_ref, b_ref, ga_ref, gb_ref, o_ref):
        ah, _ = _rms(a_ref[...])
        bh, _ = _rms(b_ref[...])
        o_ref[:, 0:h] = (ah * ga_ref[...]).astype(o_ref.dtype)
        o_ref[:, h:2 * h] = (bh * gb_ref[...]).astype(o_ref.dtype)

    return _row_call(body, "mix_norm_fwd", s, [(y_ssm, "rows"), (y_sgu, "rows"), (g_ssm, "vec"), (g_sgu, "vec")],
                     [(jax.ShapeDtypeStruct((s, 2 * h), _MXU), "rows")])[0]


def _mix_norm_bwd(dyc, y_ssm, y_sgu, g_ssm, g_sgu):
    s, h = y_ssm.shape

    def body(d_ref, a_ref, b_ref, ga_ref, gb_ref, da_ref, db_ref, dga_ref, dgb_ref):
        first = pl.program_id(0) == 0
        for lo, y_ref, g_ref, dy_ref, dg_ref in ((0, a_ref, ga_ref, da_ref, dga_ref), (h, b_ref, gb_ref, db_ref, dgb_ref)):
            d = d_ref[:, lo:lo + h]
            yh, r = _rms(y_ref[...])
            _acc(dg_ref, first, _colsum(d * yh))
            dy_ref[...] = _rms_bwd(d * g_ref[...], yh, r)

    vec = jax.ShapeDtypeStruct((1, h), _F32)
    full = jax.ShapeDtypeStruct((s, h), _F32)
    return _row_call(body, "mix_norm_bwd", s,
                     [(dyc, "rows"), (y_ssm, "rows"), (y_sgu, "rows"), (g_ssm, "vec"), (g_sgu, "vec")],
                     [(full, "rows"), (full, "rows"), (vec, "vec"), (vec, "vec")])


def _shift_down(x, k):
    row = lax.broadcasted_iota(jnp.int32, x.shape, 0)
    return jnp.where(row >= k, pltpu.roll(x, k, 0), 0.0)


def _shift_up(x, k):
    n = x.shape[0]
    row = lax.broadcasted_iota(jnp.int32, x.shape, 0)
    return jnp.where(row < n - k, pltpu.roll(x, n - k, 0), 0.0)


def _conv(x, w_ref, b_ref):
    return b_ref[...] + w_ref[0:1, :] * _shift_down(x, 2) + w_ref[1:2, :] * _shift_down(x, 1) + w_ref[2:3, :] * x


def _conv_act_fwd(up_pre, conv_w, conv_b):
    s, f2 = up_pre.shape
    f = f2 // 2
    tc = _tile(f, 256)
    nf = f // tc

    def body(a_ref, b_ref, wa_ref, wb_ref, ba_ref, bb_ref, o_ref):
        a = _conv(a_ref[...], wa_ref, ba_ref)
        b = _conv(b_ref[...], wb_ref, bb_ref)
        o_ref[...] = (a * _sigmoid(a) * b).astype(o_ref.dtype)

    return pl.pallas_call(
        body, name="conv_act_fwd", grid=(nf,), out_shape=jax.ShapeDtypeStruct((s, f), _MXU),
        in_specs=[pl.BlockSpec((s, tc), lambda n: (0, n)), pl.BlockSpec((s, tc), lambda n: (0, n + nf)),
                  pl.BlockSpec((3, tc), lambda n: (0, n)), pl.BlockSpec((3, tc), lambda n: (0, n + nf)),
                  pl.BlockSpec((1, tc), lambda n: (0, n)), pl.BlockSpec((1, tc), lambda n: (0, n + nf))],
        out_specs=pl.BlockSpec((s, tc), lambda n: (0, n)), compiler_params=_cp("parallel"),
    )(up_pre, up_pre, conv_w, conv_w, conv_b, conv_b)


def _conv_act_bwd(up_pre, d_act, conv_w, conv_b):
    s, f2 = up_pre.shape
    f = f2 // 2
    tc = _tile(f, 256)
    nf = f // tc

    def body(a_ref, b_ref, d_ref, wa_ref, wb_ref, ba_ref, bb_ref,
             du_ref, w0a, w0b, w1a, w1b, w2a, w2b, dba, dbb):
        xa, xb = a_ref[...], b_ref[...]
        a = _conv(xa, wa_ref, ba_ref)
        b = _conv(xb, wb_ref, bb_ref)
        sg = _sigmoid(a)
        d = d_ref[...]
        d_a = d * b * (sg * (1.0 + a * (1.0 - sg)))
        d_b = d * (a * sg)
        for x, du, w_ref, o_ref, o0, o1, o2, ob in ((xa, d_a, wa_ref, du_ref.at[0], w0a, w1a, w2a, dba),
                                                     (xb, d_b, wb_ref, du_ref.at[1], w0b, w1b, w2b, dbb)):
            ob[...] = _colsum(du)
            o0[...] = _colsum(du * _shift_down(x, 2))
            o1[...] = _colsum(du * _shift_down(x, 1))
            o2[...] = _colsum(du * x)
            o_ref[...] = (w_ref[2:3, :] * du + w_ref[1:2, :] * _shift_up(du, 1)
                          + w_ref[0:1, :] * _shift_up(du, 2)).astype(o_ref.dtype)

    col_a = pl.BlockSpec((s, tc), lambda n: (0, n))
    col_b = pl.BlockSpec((s, tc), lambda n: (0, n + nf))
    vec_a = pl.BlockSpec((1, tc), lambda n: (0, n))
    vec_b = pl.BlockSpec((1, tc), lambda n: (0, n + nf))
    vec = jax.ShapeDtypeStruct((1, f), _F32)
    outs = pl.pallas_call(
        body, name="conv_act_bwd", grid=(nf,),
        in_specs=[col_a, col_b, col_a, pl.BlockSpec((3, tc), lambda n: (0, n)),
                  pl.BlockSpec((3, tc), lambda n: (0, n + nf)), vec_a, vec_b],
        out_specs=[pl.BlockSpec((2, s, tc), lambda n: (0, 0, n))] + [vec_a] * 8,
        out_shape=[jax.ShapeDtypeStruct((2, s, f), _MXU)] + [vec] * 8, compiler_params=_cp("parallel"),
    )(up_pre, up_pre, d_act, conv_w, conv_w, conv_b, conv_b)
    du, w0a, w0b, w1a, w1b, w2a, w2b, dba, dbb = outs
    cat = lambda p, q: jnp.concatenate([p, q], axis=1)
    return du, cat(w0a, w0b), cat(w1a, w1b), cat(w2a, w2b), cat(dba, dbb)


def _sgu_recompute(zu_ref, zv_ref, lng_ref, lnb_ref, wm_ref, bs_ref, nh):
    zu, zv = zu_ref[...], zv_ref[...]
    u = _gelu(zu)
    gv = _gelu(zv)
    xc = gv - _rowmean(gv)
    rs = lax.rsqrt(_rowmean(xc * xc) + EPS)
    vh = xc * rs
    v = vh * lng_ref[...] + lnb_ref[...]
    mixed = []
    for h in range(nh):
        vhd = v[:, h * CHUNK:(h + 1) * CHUNK].astype(_MXU)
        mixed.append(jnp.dot(wm_ref[h].astype(_MXU), vhd, preferred_element_type=_F32) + bs_ref[h])
    return zu, zv, u, vh, rs, v, mixed


def _sgu_fwd(z, ln_g, ln_b, wm, bs):
    s = z.shape[0]
    nh = wm.shape[0]
    hd = nh * CHUNK

    def body(zu_ref, zv_ref, lng_ref, lnb_ref, wm_ref, bs_ref, y_ref):
        _, _, u, _, _, _, mixed = _sgu_recompute(zu_ref, zv_ref, lng_ref, lnb_ref, wm_ref, bs_ref, nh)
        for h in range(nh):
            y_ref[:, h * CHUNK:(h + 1) * CHUNK] = u[:, h * CHUNK:(h + 1) * CHUNK] * mixed[h]

    vec = pl.BlockSpec((1, hd), lambda i: (0, 0))
    return pl.pallas_call(
        body, name="sgu_fwd", grid=(s // CHUNK,), out_shape=jax.ShapeDtypeStruct((s, hd), _F32),
        in_specs=[pl.BlockSpec((CHUNK, hd), lambda i: (i, 1)), pl.BlockSpec((CHUNK, hd), lambda i: (i, 2)), vec, vec,
                  pl.BlockSpec((nh, CHUNK, CHUNK), lambda i: (0, 0, 0)), pl.BlockSpec((nh, CHUNK, 1), lambda i: (0, 0, 0))],
        out_specs=pl.BlockSpec((CHUNK, hd), lambda i: (i, 0)), compiler_params=_cp("parallel"),
    )(z, z, ln_g, ln_b, wm, bs)


def _sgu_bwd(z, dy, dz_ssm, ln_g, ln_b, wm, bs):
    s = z.shape[0]
    nh = wm.shape[0]
    hd = nh * CHUNK

    def body(zu_ref, zv_ref, dy_ref, dzs_ref, lng_ref, lnb_ref, wm_ref, bs_ref,
             dz_ref, dlg_ref, dlb_ref, dwm_ref, dbs_ref, dv_scr):
        first = pl.program_id(0) == 0
        zu, zv, u, vh, rs, v, mixed = _sgu_recompute(zu_ref, zv_ref, lng_ref, lnb_ref, wm_ref, bs_ref, nh)
        dy = dy_ref[...]
        dz_ref[:, 0:hd] = dzs_ref[...]
        for h in range(nh):
            cols = slice(h * CHUNK, (h + 1) * CHUNK)
            dyh = dy[:, cols]
            dz_ref[:, hd + h * CHUNK:hd + (h + 1) * CHUNK] = (dyh * mixed[h] * _gelu_grad(zu[:, cols])).astype(dz_ref.dtype)
            dm = dyh * u[:, cols]
            dmx = dm.astype(_MXU)
            _acc(dbs_ref.at[h], first, jnp.sum(dm, axis=1, keepdims=True))
            _acc(dwm_ref.at[h], first,
                 lax.dot_general(dmx, v[:, cols].astype(_MXU), (((1,), (1,)), ((), ())), preferred_element_type=_F32))
            dv_scr[:, cols] = lax.dot_general(wm_ref[h].astype(_MXU), dmx, (((0,), (0,)), ((), ())),
                                              preferred_element_type=_F32)
        dv = dv_scr[...]
        _acc(dlg_ref, first, _colsum(dv * vh))
        _acc(dlb_ref, first, _colsum(dv))
        dvh = dv * lng_ref[...]
        dgv = rs * (dvh - _rowmean(dvh) - vh * _rowmean(dvh * vh))
        dz_ref[:, 2 * hd:3 * hd] = (dgv * _gelu_grad(zv)).astype(dz_ref.dtype)

    vec = pl.BlockSpec((1, hd), lambda i: (0, 0))
    wspec = pl.BlockSpec((nh, CHUNK, CHUNK), lambda i: (0, 0, 0))
    bspec = pl.BlockSpec((nh, CHUNK, 1), lambda i: (0, 0, 0))
    rows = pl.BlockSpec((CHUNK, hd), lambda i: (i, 0))
    return pl.pallas_call(
        body, name="sgu_bwd", grid=(s // CHUNK,),
        out_shape=[jax.ShapeDtypeStruct((s, 3 * hd), _MXU), jax.ShapeDtypeStruct((1, hd), _F32),
                   jax.ShapeDtypeStruct((1, hd), _F32), jax.ShapeDtypeStruct((nh, CHUNK, CHUNK), _F32),
                   jax.ShapeDtypeStruct((nh, CHUNK, 1), _F32)],
        in_specs=[pl.BlockSpec((CHUNK, hd), lambda i: (i, 1)), pl.BlockSpec((CHUNK, hd), lambda i: (i, 2)),
                  rows, rows, vec, vec, wspec, bspec],
        out_specs=[pl.BlockSpec((CHUNK, 3 * hd), lambda i: (i, 0)), vec, vec, wspec, bspec],
        scratch_shapes=[pltpu.VMEM((CHUNK, hd), _F32)], compiler_params=_cp("arbitrary"),
    )(z, z, dy, dz_ssm, ln_g, ln_b, wm, bs)


def _ssm_prep(log_dt, a_re, a_im, b_re_t, b_im_t):
    gn = a_re.shape[1]

    def body(ldt_ref, are_ref, aim_ref, br_ref, bi_ref, pr_ref, pi_ref, bbr_ref, bbi_ref):
        dt = jnp.exp(ldt_ref[...])
        are, aim = are_ref[...], aim_ref[...]
        k = (lax.broadcasted_iota(jnp.int32, (SUBLANES, gn), 0) + 1).astype(_F32)
        mag = jnp.exp(k * (are * dt))
        ang = k * (aim * dt)
        pr_ref[...] = mag * jnp.cos(ang)
        pi_ref[...] = mag * jnp.sin(ang)
        m1 = jnp.exp(are * dt)
        lr, li = m1 * jnp.cos(aim * dt), m1 * jnp.sin(aim * dt)
        den = are * are + aim * aim
        nr = lr - 1.0
        f_re = (nr * are + li * aim) / den
        f_im = (li * are - nr * aim) / den
        bbr_ref[...] = f_re * br_ref[...] - f_im * bi_ref[...]
        bbi_ref[...] = f_re * bi_ref[...] + f_im * br_ref[...]

    pw = jax.ShapeDtypeStruct((SUBLANES, gn), _F32)
    bb = jax.ShapeDtypeStruct(b_re_t.shape, _F32)
    return pl.pallas_call(body, name="ssm_prep", out_shape=[pw, pw, bb, bb])(log_dt, a_re, a_im, b_re_t, b_im_t)


def _ssm_prep_bwd(log_dt, a_re, a_im, b_re_t, b_im_t, d_bbr, d_bbi, d_lr, d_li):
    def body(ldt_ref, are_ref, aim_ref, br_ref, bi_ref, dbr_ref, dbi_ref, dlr_ref, dli_ref,
             obr_ref, obi_ref, oar_ref, oai_ref, odt_ref):
        dt = jnp.exp(ldt_ref[...])
        are, aim = are_ref[...], aim_ref[...]
        m1 = jnp.exp(are * dt)
        lr, li = m1 * jnp.cos(aim * dt), m1 * jnp.sin(aim * dt)
        den = are * are + aim * aim
        nr = lr - 1.0
        f_re = (nr * are + li * aim) / den
        f_im = (li * are - nr * aim) / den
        br, bi, dbr, dbi = br_ref[...], bi_ref[...], dbr_ref[...], dbi_ref[...]
        obr_ref[...] = f_re * dbr + f_im * dbi
        obi_ref[...] = f_re * dbi - f_im * dbr
        gf_re = _colsum(br * dbr + bi * dbi)
        gf_im = _colsum(br * dbi - bi * dbr)
        il_re, il_im = are / den, -aim / den
        glb_re = dlr_ref[...] + (il_re * gf_re + il_im * gf_im)
        glb_im = dli_ref[...] + (il_re * gf_im - il_im * gf_re)
        q_re = -(f_re * il_re - f_im * il_im)
        q_im = -(f_re * il_im + f_im * il_re)
        gl_re = q_re * gf_re + q_im * gf_im
        gl_im = q_re * gf_im - q_im * gf_re
        gl_re = gl_re + dt * (lr * glb_re + li * glb_im)
        gl_im = gl_im + dt * (lr * glb_im - li * glb_re)
        w_re = are * lr - aim * li
        w_im = are * li + aim * lr
        oar_ref[...] = gl_re
        oai_ref[...] = gl_im
        odt_ref[...] = w_re * glb_re + w_im * glb_im

    bb = jax.ShapeDtypeStruct(b_re_t.shape, _F32)
    v = jax.ShapeDtypeStruct(a_re.shape, _F32)
    return pl.pallas_call(body, name="ssm_prep_bwd", out_shape=[bb, bb, v, v, v])(
        log_dt, a_re, a_im, b_re_t, b_im_t, d_bbr, d_bbi, d_lr, d_li)


def _group_sum(d_dt, log_dt):
    def body(d_ref, l_ref, o_ref):
        o_ref[...] = jnp.sum(d_ref[...], axis=1, keepdims=True) * jnp.exp(l_ref[...])

    return pl.pallas_call(body, name="ssm_dt_grad", out_shape=jax.ShapeDtypeStruct(log_dt.shape, _F32))(d_dt, log_dt)


def _scan_rows(src_ref, dst_ref, nrt, steps, ptab, carry0, reverse):
    ns = BLOCK_ST
    row = lax.broadcasted_iota(jnp.int32, (SUBLANES, ns), 0)
    pr, pi = ptab

    def body(i, carry):
        cr, ci = carry
        it = (nrt - 1 - i) if reverse else i
        r0 = pl.multiple_of(it * SUBLANES, SUBLANES)
        xr = src_ref[pl.ds(r0, SUBLANES), 0:ns]
        xi = src_ref[pl.ds(r0, SUBLANES), ns:2 * ns]
        for k, (ar, ai) in zip((1, 2, 4), steps):
            if reverse:
                keep = row < SUBLANES - k
                sr = jnp.where(keep, pltpu.roll(xr, SUBLANES - k, 0), 0.0)
                si = jnp.where(keep, pltpu.roll(xi, SUBLANES - k, 0), 0.0)
            else:
                keep = row >= k
                sr = jnp.where(keep, pltpu.roll(xr, k, 0), 0.0)
                si = jnp.where(keep, pltpu.roll(xi, k, 0), 0.0)
            xr, xi = xr + ar * sr - ai * si, xi + ar * si + ai * sr
        xr, xi = xr + pr * cr - pi * ci, xi + pr * ci + pi * cr
        dst_ref[pl.ds(r0, SUBLANES), 0:ns] = xr
        dst_ref[pl.ds(r0, SUBLANES), ns:2 * ns] = xi
        if reverse:
            return xr[0:1, :], xi[0:1, :]
        return xr[SUBLANES - 1:SUBLANES, :], xi[SUBLANES - 1:SUBLANES, :]

    return lax.fori_loop(0, nrt, body, carry0)


def _scan_consts(p_ref, conj):
    ns = BLOCK_ST
    sign = -1.0 if conj else 1.0
    bc = lambda r: jnp.broadcast_to(r, (SUBLANES, ns))
    steps = [(bc(p_ref[k - 1:k, 0:ns]), bc(sign * p_ref[k - 1:k, ns:2 * ns])) for k in (1, 2, 4)]
    return steps


def _ssm_block_fwd(u, bbt_ref, ct_ref, d_ref, wg_ref, bg_ref, p_ref, bu_scr, h_scr, carry_in, nrt):
    ns = BLOCK_ST
    bu_scr[...] = jnp.dot(u.astype(_MXU), bbt_ref[...].astype(_MXU), preferred_element_type=_F32)
    ptab = (p_ref[:, 0:ns], p_ref[:, ns:2 * ns])
    carry = _scan_rows(bu_scr, h_scr, nrt, _scan_consts(p_ref, False), ptab, carry_in, False)
    y = jnp.dot(h_scr[...].astype(_MXU), ct_ref[...].astype(_MXU), preferred_element_type=_F32) + d_ref[...] * u
    yg = _gelu(y)
    gate = _sigmoid(jnp.dot(yg.astype(_MXU), wg_ref[...].astype(_MXU), preferred_element_type=_F32) + bg_ref[...])
    return y, yg, gate, carry


def _ssm_specs(nb, nt, t, reverse):
    tt = (lambda ti: nt - 1 - ti) if reverse else (lambda ti: ti)
    ns2 = 2 * BLOCK_ST
    return dict(
        z=pl.BlockSpec((t, BLOCK_CH), lambda b, ti: (tt(ti), b)),
        bbt=pl.BlockSpec((None, BLOCK_CH, ns2), lambda b, ti: (b, 0, 0)),
        ct=pl.BlockSpec((None, ns2, BLOCK_CH), lambda b, ti: (b, 0, 0)),
        vec=pl.BlockSpec((1, BLOCK_CH), lambda b, ti: (0, b)),
        wg=pl.BlockSpec((None, BLOCK_CH, BLOCK_CH), lambda b, ti: (b, 0, 0)),
        p=pl.BlockSpec((None, SUBLANES, ns2), lambda b, ti: (b, 0, 0)),
        hb=pl.BlockSpec((None, None, SUBLANES, ns2), lambda b, ti: (b, tt(ti), 0, 0)),
        acc_vec=pl.BlockSpec((None, 1, ns2), lambda b, ti: (b, 0, 0)),
    )


def _ssm_fwd(z, bbt, ct, dvec, wg, bglu, ptab):
    s = z.shape[0]
    nb = bbt.shape[0]
    t = _tile(s, TIME_TILE, SUBLANES)
    nt = s // t
    ns = BLOCK_ST
    sp = _ssm_specs(nb, nt, t, False)

    def body(z_ref, bbt_ref, ct_ref, d_ref, wg_ref, bg_ref, p_ref, y2_ref, hb_ref, bu_scr, h_scr, carry_scr):
        ti = pl.program_id(1)

        @pl.when(ti == 0)
        def _():
            carry_scr[...] = jnp.zeros_like(carry_scr)

        hb_ref[...] = carry_scr[...]
        carry_in = (carry_scr[0:1, 0:ns], carry_scr[0:1, ns:2 * ns])
        _, yg, gate, (cr, ci) = _ssm_block_fwd(z_ref[...], bbt_ref, ct_ref, d_ref, wg_ref, bg_ref, p_ref,
                                               bu_scr, h_scr, carry_in, t // SUBLANES)
        y2_ref[...] = yg * gate
        carry_scr[:, 0:ns] = jnp.broadcast_to(cr, (SUBLANES, ns))
        carry_scr[:, ns:2 * ns] = jnp.broadcast_to(ci, (SUBLANES, ns))

    return pl.pallas_call(
        body, name="ssm_fwd", grid=(nb, nt),
        out_shape=[jax.ShapeDtypeStruct((s, nb * BLOCK_CH), _F32), jax.ShapeDtypeStruct((nb, nt, SUBLANES, 2 * ns), _F32)],
        in_specs=[sp["z"], sp["bbt"], sp["ct"], sp["vec"], sp["wg"], sp["vec"], sp["p"]],
        out_specs=[sp["z"], sp["hb"]],
        scratch_shapes=[pltpu.VMEM((t, 2 * ns), _F32), pltpu.VMEM((t, 2 * ns), _F32), pltpu.VMEM((SUBLANES, 2 * ns), _F32)],
        compiler_params=_cp("parallel", "arbitrary"),
    )(z, bbt, ct, dvec, wg, bglu, ptab)


def _ssm_bwd(z, dy2, hb, bbt, ct, dvec, wg, bglu, ptab, ptab_rev):
    s = z.shape[0]
    nb = bbt.shape[0]
    t = _tile(s, TIME_TILE, SUBLANES)
    nt = s // t
    ns = BLOCK_ST
    sp = _ssm_specs(nb, nt, t, True)
    tn_dims = (((0,), (0,)), ((), ()))
    nt_dims = (((1,), (1,)), ((), ()))

    def body(z_ref, dy2_ref, hb_ref, bbt_ref, ct_ref, d_ref, wg_ref, bg_ref, p_ref, pr_ref,
             dz_ref, dbbt_ref, dct_ref, dwg_ref, dlb_ref, dd_ref, dbg_ref, bu_scr, h_scr, g_scr, gcarry_scr):
        first = pl.program_id(1) == 0

        @pl.when(first)
        def _():
            gcarry_scr[...] = jnp.zeros_like(gcarry_scr)

        u = z_ref[...]
        hin = hb_ref[...]
        carry_in = (hin[0:1, 0:ns], hin[0:1, ns:2 * ns])
        y, yg, gate, _ = _ssm_block_fwd(u, bbt_ref, ct_ref, d_ref, wg_ref, bg_ref, p_ref, bu_scr, h_scr, carry_in,
                                        t // SUBLANES)
        dy2 = dy2_ref[...]
        dpre = dy2 * yg * gate * (1.0 - gate)
        _acc(dbg_ref, first, _colsum(dpre))
        dpx = dpre.astype(_MXU)
        _acc(dwg_ref, first, lax.dot_general(yg.astype(_MXU), dpx, tn_dims, preferred_element_type=_F32))
        dyg = dy2 * gate + lax.dot_general(dpx, wg_ref[...].astype(_MXU), nt_dims, preferred_element_type=_F32)
        dy = dyg * _gelu_grad(y)
        _acc(dd_ref, first, _colsum(dy * u))
        dyx = dy.astype(_MXU)
        h = h_scr[...]
        _acc(dct_ref, first, lax.dot_general(h.astype(_MXU), dyx, tn_dims, preferred_element_type=_F32))
        bu_scr[...] = lax.dot_general(dyx, ct_ref[...].astype(_MXU), nt_dims, preferred_element_type=_F32)
        gin = (gcarry_scr[0:1, 0:ns], gcarry_scr[0:1, ns:2 * ns])
        ptab = (pr_ref[:, 0:ns], pr_ref[:, ns:2 * ns])
        gr, gi = _scan_rows(bu_scr, g_scr, t // SUBLANES, _scan_consts(p_ref, True), ptab, gin, True)
        gcarry_scr[:, 0:ns] = jnp.broadcast_to(gr, (SUBLANES, ns))
        gcarry_scr[:, ns:2 * ns] = jnp.broadcast_to(gi, (SUBLANES, ns))
        g = g_scr[...]
        row = lax.broadcasted_iota(jnp.int32, (t, ns), 0)
        hp_re = jnp.where(row == 0, hin[0:1, 0:ns], pltpu.roll(h[:, 0:ns], 1, 0))
        hp_im = jnp.where(row == 0, hin[0:1, ns:2 * ns], pltpu.roll(h[:, ns:2 * ns], 1, 0))
        g_re, g_im = g[:, 0:ns], g[:, ns:2 * ns]
        d_ar = _colsum(g_re * hp_re + g_im * hp_im)
        d_ai = _colsum(g_im * hp_re - g_re * hp_im)
        _acc(dlb_ref, first, jnp.concatenate([d_ar, d_ai], axis=1))
        gx = g.astype(_MXU)
        _acc(dbbt_ref, first, lax.dot_general(u.astype(_MXU), gx, tn_dims, preferred_element_type=_F32))
        dz_ref[...] = (dy * d_ref[...] + lax.dot_general(gx, bbt_ref[...].astype(_MXU), nt_dims,
                                                         preferred_element_type=_F32)).astype(dz_ref.dtype)

    f = lambda shape: jax.ShapeDtypeStruct(shape, _F32)
    return pl.pallas_call(
        body, name="ssm_bwd", grid=(nb, nt),
        out_shape=[jax.ShapeDtypeStruct((s, nb * BLOCK_CH), _MXU), f(bbt.shape), f(ct.shape), f(wg.shape), f((nb, 1, 2 * ns)),
                   f((1, nb * BLOCK_CH)), f((1, nb * BLOCK_CH))],
        in_specs=[sp["z"], sp["z"], sp["hb"], sp["bbt"], sp["ct"], sp["vec"], sp["wg"], sp["vec"], sp["p"], sp["p"]],
        out_specs=[sp["z"], sp["bbt"], sp["ct"], sp["wg"], sp["acc_vec"], sp["vec"], sp["vec"]],
        scratch_shapes=[pltpu.VMEM((t, 2 * ns), _F32), pltpu.VMEM((t, 2 * ns), _F32), pltpu.VMEM((t, 2 * ns), _F32),
                        pltpu.VMEM((SUBLANES, 2 * ns), _F32)],
        compiler_params=_cp("parallel", "arbitrary"),
    )(z, dy2, hb, bbt, ct, dvec, wg, bglu, ptab, ptab_rev)


def _mod_part(c_all, w, b):
    d, ns = w.shape
    tn = _tile(ns, 512)

    def body(c_ref, w_ref, b_ref, o_ref):
        c = c_ref[...]
        ca = (c * _sigmoid(c)).astype(_MXU)
        o_ref[...] = jnp.dot(ca, w_ref[...].astype(_MXU), preferred_element_type=_F32) + b_ref[...]

    return pl.pallas_call(
        body, name="mod_part", grid=(ns // tn,), out_shape=jax.ShapeDtypeStruct((8, ns), _F32),
        in_specs=[pl.BlockSpec((8, d), lambda n: (0, 0)), pl.BlockSpec((d, tn), lambda n: (0, n)),
                  pl.BlockSpec((1, tn), lambda n: (0, n))],
        out_specs=pl.BlockSpec((8, tn), lambda n: (0, n)), compiler_params=_cp("parallel"),
    )(c_all, w, b)


def _adamw_math(w, g, m, v):
    m = ADAM_B1 * m + (1.0 - ADAM_B1) * g
    v = ADAM_B2 * v + (1.0 - ADAM_B2) * (g * g)
    m_hat = m / (1.0 - ADAM_B1 ** ADAM_STEP)
    v_hat = v / (1.0 - ADAM_B2 ** ADAM_STEP)
    delta = -ADAM_LR * (m_hat / (jnp.sqrt(v_hat) + ADAM_EPS) + ADAM_WD * w)
    return delta, m, v


def _adamw(w, g, m, v, name):
    r, c = w.shape
    tr, tc = _tile(r, 256, SUBLANES), _tile(c, 1024)

    def body(w_ref, g_ref, m_ref, v_ref, d_ref, mo_ref, vo_ref):
        d_ref[...], mo_ref[...], vo_ref[...] = _adamw_math(w_ref[...], g_ref[...], m_ref[...], v_ref[...])

    spec = pl.BlockSpec((tr, tc), lambda i, j: (i, j))
    out = jax.ShapeDtypeStruct((r, c), _F32)
    return pl.pallas_call(
        body, name=name, grid=(r // tr, c // tc), in_specs=[spec] * 4, out_specs=[spec] * 3, out_shape=[out] * 3,
        compiler_params=_cp("parallel", "parallel"),
    )(w, g, m, v)


def _adamw_halves(w, g2, m, v, name):
    r, c = w.shape
    tr, tc = _tile(r, 256, SUBLANES), _tile(c // 2, 1024)
    nph = (c // 2) // tc

    def body(w_ref, g_ref, m_ref, v_ref, go_ref, d_ref, mo_ref, vo_ref):
        g = g_ref[...]
        go_ref[...] = g
        d_ref[...], mo_ref[...], vo_ref[...] = _adamw_math(w_ref[...], g, m_ref[...], v_ref[...])

    spec = pl.BlockSpec((tr, tc), lambda i, j: (i, j))
    out = jax.ShapeDtypeStruct((r, c), _F32)
    return pl.pallas_call(
        body, name=name, grid=(r // tr, c // tc),
        in_specs=[spec, pl.BlockSpec((None, tr, tc), lambda i, j: (j // nph, i, j % nph)), spec, spec],
        out_specs=[spec] * 4, out_shape=[out] * 4, compiler_params=_cp("parallel", "parallel"),
    )(w, g2, m, v)


def _wada_update(c_t, dm, w, m, v):
    d, ns = w.shape
    tr, tc = _tile(d, 256, SUBLANES), _tile(ns, 1024)

    def body(c_ref, dm_ref, w_ref, m_ref, v_ref, g_ref, d_ref, mo_ref, vo_ref):
        c = c_ref[...]
        ca = c * _sigmoid(c)
        dmv = dm_ref[...]
        g = ca[:, 0:1] * dmv[0:1, :]
        for b in range(1, 8):
            g = g + ca[:, b:b + 1] * dmv[b:b + 1, :]
        g_ref[...] = g
        d_ref[...], mo_ref[...], vo_ref[...] = _adamw_math(w_ref[...], g, m_ref[...], v_ref[...])

    spec = pl.BlockSpec((tr, tc), lambda i, j: (i, j))
    out = jax.ShapeDtypeStruct((d, ns), _F32)
    return pl.pallas_call(
        body, name="wada_update", grid=(d // tr, ns // tc),
        in_specs=[pl.BlockSpec((tr, 8), lambda i, j: (i, 0)), pl.BlockSpec((8, tc), lambda i, j: (0, j)), spec, spec, spec],
        out_specs=[spec] * 4, out_shape=[out] * 4, compiler_params=_cp("parallel", "parallel"),
    )(c_t, dm, w, m, v)


def _small_reduce_adamw(gathered, w, m, v):
    _, r, c = gathered.shape
    tr = _tile(r, 256, SUBLANES)

    def body(q_ref, w_ref, m_ref, v_ref, g_ref, d_ref, mo_ref, vo_ref):
        g = q_ref[0]
        for k in range(1, 8):
            g = g + q_ref[k]
        g_ref[...] = g
        d_ref[...], mo_ref[...], vo_ref[...] = _adamw_math(w_ref[...], g, m_ref[...], v_ref[...])

    spec = pl.BlockSpec((tr, c), lambda i: (i, 0))
    out = jax.ShapeDtypeStruct((r, c), _F32)
    return pl.pallas_call(
        body, name="small_reduce_adamw", grid=(r // tr,),
        in_specs=[pl.BlockSpec((8, tr, c), lambda i: (0, i, 0)), spec, spec, spec],
        out_specs=[spec] * 4, out_shape=[out] * 4, compiler_params=_cp("parallel"),
    )(gathered, w, m, v)


def _block_diag(x):
    nb, g, p, q = x.shape
    eye = jnp.eye(g, dtype=x.dtype)
    return (x[:, :, :, None, :] * eye[None, :, None, :, None]).reshape(nb, g * p, g * q)


def _block_diag_take(x, p, q):
    nb = x.shape[0]
    g = GROUPS_PER_BLOCK
    eye = jnp.eye(g, dtype=x.dtype)
    return jnp.sum(x.reshape(nb, g, p, g, q) * eye[None, :, None, :, None], axis=3)


class _Pack:
    def __init__(self, shapes):
        self.shapes = shapes
        self.offsets = {}
        off = 0
        for name, shape in shapes.items():
            n = math.prod(shape)
            self.offsets[name] = (off, n)
            off += -(-n // (SUBLANES * LANES)) * (SUBLANES * LANES)
        self.rows = -(-off // (256 * LANES)) * 256

    def pack(self, arrays):
        parts = []
        off = 0
        for name, shape in self.shapes.items():
            start, n = self.offsets[name]
            if start > off:
                parts.append(jnp.zeros((start - off,), _F32))
            parts.append(arrays[name].reshape(-1).astype(_F32))
            off = start + n
        total = self.rows * LANES
        if total > off:
            parts.append(jnp.zeros((total - off,), _F32))
        return jnp.concatenate(parts).reshape(self.rows, LANES)

    def unpack(self, buf):
        flat = buf.reshape(-1)
        return {name: flat[start:start + n].reshape(self.shapes[name]) for name, (start, n) in self.offsets.items()}


_SMALL = ["b_ada", "g_pre_mix", "g_post_mix", "ssm_log_dt", "ssm_a_re", "ssm_a_im", "ssm_b_re", "ssm_b_im", "ssm_c_re",
          "ssm_c_im", "ssm_d", "ssm_w_glu", "ssm_b_glu", "sgu_ln_g", "sgu_ln_b", "sgu_w", "sgu_b", "g_out_ssm",
          "g_out_sgu", "g_pre_ffn", "g_post_ffn", "conv_b"]
_WEIGHTS = ["w_ada", "b_ada", "g_pre_mix", "g_post_mix", "w_in", "ssm_log_dt", "ssm_a_re", "ssm_a_im", "ssm_b_re",
            "ssm_b_im", "ssm_c_re", "ssm_c_im", "ssm_d", "ssm_w_glu", "ssm_b_glu", "sgu_ln_g", "sgu_ln_b", "sgu_w", "sgu_b",
            "g_out_ssm", "g_out_sgu", "w_out", "g_pre_ffn", "g_post_ffn", "w_up", "conv_w", "conv_b", "w_down"]


def _step(p, m, v, x, c, tgt):
    s, d = x.shape
    mx, my, mc = lax.axis_index("x"), lax.axis_index("y"), lax.axis_index("c")
    chip = 2 * mx + my
    dev = 4 * mx + 2 * my + mc
    sel = jnp.stack([chip, mc]).astype(jnp.int32)
    g_cnt, n_st = p["ssm_a_re"].shape
    nb = g_cnt // GROUPS_PER_BLOCK
    gn = g_cnt * n_st
    d_ssm = g_cnt * SSM_GROUP
    nh = p["sgu_w"].shape[0]
    assert nh * CHUNK == d_ssm and 2 * d_ssm == d and n_st == SSM_STATE

    shards = lambda g: g.reshape(4, g.shape[1] * g.shape[2], g.shape[3])
    buf_in, buf_out, buf_up, buf_down = [_cast_into_slot(p[n], sel) for n in ("w_in", "w_out", "w_up", "w_down")]
    (sems_in,), (buf_in,), tok = _gather_start([buf_in], "gather_start_in")

    ns_ada = p["w_ada"].shape[1]
    nc_conv = p["conv_w"].shape[1]
    first = jnp.concatenate([jnp.broadcast_to(_after(c, tok), (8, d)), jnp.pad(p["conv_w"], ((0, 5), (0, 0)))], axis=1)
    first_all = _all_gather8(_own_slot(first, dev), "gather_c_conv")
    c_all = first_all[:, 0, :d]
    conv_w_full = jnp.concatenate([first_all[2 * j, 0:3, d:] for j in range(4)], axis=1)
    b_ada_mine = lax.dynamic_slice_in_dim(p["b_ada"], chip * ns_ada, ns_ada, axis=1)
    mod_all = _all_gather8(_own_slot(_mod_part(c_all, p["w_ada"], b_ada_mine), dev), "gather_mod")
    mod_rows = lax.dynamic_index_in_dim(mod_all, dev, axis=1, keepdims=False)
    mod = jnp.concatenate([mod_rows[0], mod_rows[2], mod_rows[4], mod_rows[6]]).reshape(N_MOD, 1, d)
    sh1, sc1, gt1, sh2, sc2, gt2 = [mod[i] for i in range(N_MOD)]

    ldt_l = jnp.repeat(p["ssm_log_dt"], n_st, axis=1)
    are_l, aim_l = p["ssm_a_re"].reshape(1, gn), p["ssm_a_im"].reshape(1, gn)
    bre_t, bim_t = p["ssm_b_re"].reshape(gn, SSM_GROUP).T, p["ssm_b_im"].reshape(gn, SSM_GROUP).T
    pw_re, pw_im, bb_re, bb_im = _ssm_prep(ldt_l, are_l, aim_l, bre_t, bim_t)
    blocks = lambda t: t.reshape(t.shape[0], nb, GROUPS_PER_BLOCK * n_st).transpose(1, 0, 2)
    ptab = jnp.concatenate([blocks(pw_re), blocks(pw_im)], axis=2)
    ptab_rev = jnp.concatenate([blocks(pw_re)[:, ::-1], -blocks(pw_im)[:, ::-1]], axis=2)
    bd = lambda t: t.reshape(SSM_GROUP, nb, GROUPS_PER_BLOCK, n_st).transpose(1, 2, 0, 3)
    bbt = jnp.concatenate([_block_diag(bd(bb_re)), _block_diag(bd(bb_im))], axis=2).astype(_MXU)
    cd = lambda t: t.reshape(nb, GROUPS_PER_BLOCK, SSM_GROUP, n_st).transpose(0, 1, 3, 2)
    ct = jnp.concatenate([_block_diag(cd(p["ssm_c_re"])), -_block_diag(cd(p["ssm_c_im"]))], axis=1).astype(_MXU)
    wg = _block_diag(p["ssm_w_glu"].reshape(nb, GROUPS_PER_BLOCK, SSM_GROUP, SSM_GROUP)).astype(_MXU)
    dvec = p["ssm_d"]
    bglu = p["ssm_b_glu"].reshape(1, d_ssm)
    mask = jnp.tril(jnp.ones((CHUNK, CHUNK), _F32))
    wm = (p["sgu_w"] * mask[None]).astype(_MXU)
    bs = p["sgu_b"].reshape(nh, CHUNK, 1)

    buf_in = _gather_wait(sems_in, buf_in, mod_all, "gather_wait_in")
    w_in4 = shards(_pair_forward([buf_in], "pair_forward_in")[0])
    (sems_out, sems_up, sems_down), (buf_out, buf_up, buf_down), tok = _gather_start(
        [buf_out, buf_up, buf_down], "gather_start_rest")
    h1 = _fwd_pre_mix(x, p["g_pre_mix"], _after(sc1, tok), sh1)
    z = _mm_nn(h1, w_in4, _F32, "mm_in")
    y_ssm, hb = _ssm_fwd(z, bbt, ct, dvec, wg, bglu, ptab)
    y_sgu = _sgu_fwd(z, p["sgu_ln_g"], p["sgu_ln_b"], wm, bs)
    ycat = _mix_norm_fwd(y_ssm, y_sgu, p["g_out_ssm"], p["g_out_sgu"])
    buf_out = _gather_wait(sems_out, buf_out, ycat, "gather_wait_out")
    w_out_full = _pair_forward([buf_out], "pair_forward_out")[0].reshape(1, d, d)
    o = _mm_nn(ycat, w_out_full, _F32, "mm_out")
    x1, h2 = _fwd_mid(o, x, gt1, p["g_post_mix"], p["g_pre_ffn"], sc2, sh2)
    buf_up = _gather_wait(sems_up, buf_up, h2, "gather_wait_up")
    w_up4 = shards(_pair_forward([buf_up], "pair_forward_up")[0])
    up_pre = _mm_nn(h2, w_up4, _F32, "mm_up")
    act = _conv_act_fwd(up_pre, conv_w_full, p["conv_b"])
    buf_down = _gather_wait(sems_down, buf_down, act, "gather_wait_down")
    w_down_full = _pair_forward([buf_down], "pair_forward_down")[0].reshape(1, -1, d)
    f = _mm_nn(act, w_down_full, _F32, "mm_down", tk=2816)
    dx2, df, d_gt2, d_g_post_ffn, loss = _loss_and_post_ffn_bwd(f, x1, tgt, gt2, p["g_post_ffn"])

    def reduce_start(gw, n):
        got = _pair_swap([gw], "pair_swap_" + n)[0]
        pair = _pair_sum(gw, got, sel, "pair_sum_" + n)
        return _scatter_start(pair, "scatter_start_" + n)

    d_act = _mm_nt(df, w_down_full, _F32, "mm_d_act")
    gw_down = _mm_tn_rows(act, df, "mm_gw_down")
    red_down = reduce_start(gw_down, "w_down")
    d_up_pre, d_cw0, d_cw1, d_cw2, d_conv_b = _conv_act_bwd(up_pre, d_act, conv_w_full, _after(p["conv_b"], red_down[3]))
    dh2 = _mm_nt(d_up_pre, w_up4, _F32, "mm_dh2")
    gw_up = _mm_tn_cols(h2, d_up_pre, "mm_gw_up")
    red_up = reduce_start(gw_up, "w_up")
    dx1, d_o, d_sc2, d_sh2, d_g_pre_ffn, d_gt1, d_g_post_mix = _bwd_mid(
        dh2, x1, dx2, o, p["g_pre_ffn"], _after(sc2, red_up[3]), gt1, p["g_post_mix"])
    d_ycat = _mm_nt(d_o, w_out_full, _F32, "mm_d_ycat")
    gw_out = _mm_tn_rows(ycat, d_o, "mm_gw_out")
    red_out = reduce_start(gw_out, "w_out")
    dy_ssm, dy_sgu, d_g_out_ssm, d_g_out_sgu = _mix_norm_bwd(d_ycat, y_ssm, y_sgu, _after(p["g_out_ssm"], red_out[3]),
                                                              p["g_out_sgu"])
    dz_ssm, d_bbt, d_ct, d_wg, d_lb, d_ssm_d, d_bglu = _ssm_bwd(z, dy_ssm, hb, bbt, ct, dvec, wg, bglu, ptab, ptab_rev)
    dz, d_ln_g, d_ln_b, d_wm, d_bs = _sgu_bwd(z, dy_sgu, dz_ssm, p["sgu_ln_g"], p["sgu_ln_b"], wm, bs)
    dh1 = _mm_nt(dz, w_in4, _F32, "mm_dh1")
    gw_in = _mm_tn_cols(h1, dz, "mm_gw_in")
    red_in = reduce_start(gw_in, "w_in")
    dx, d_sc1, d_sh1, d_g_pre_mix = _bwd_pre_mix(dh1, x, dx1, p["g_pre_mix"], _after(sc1, red_in[3]))

    nsb = BLOCK_ST
    lanes = lambda t: t.transpose(2, 0, 1, 3).reshape(SSM_GROUP, gn)
    d_bbr = lanes(_block_diag_take(d_bbt[:, :, :nsb], SSM_GROUP, n_st))
    d_bbi = lanes(_block_diag_take(d_bbt[:, :, nsb:], SSM_GROUP, n_st))
    d_lr, d_li = d_lb[:, 0, :nsb].reshape(1, gn), d_lb[:, 0, nsb:].reshape(1, gn)
    d_bre_t, d_bim_t, d_are, d_aim, d_dt = _ssm_prep_bwd(ldt_l, are_l, aim_l, bre_t, bim_t, d_bbr, d_bbi, d_lr, d_li)
    d_log_dt = _group_sum(d_dt.reshape(g_cnt, n_st), p["ssm_log_dt"].reshape(g_cnt, 1))
    c_grad = lambda t: _block_diag_take(t, n_st, SSM_GROUP).transpose(0, 1, 3, 2).reshape(g_cnt, SSM_GROUP, n_st)
    small = {
        "b_ada": jnp.concatenate([d_sh1, d_sc1, d_gt1, d_sh2, d_sc2, d_gt2], axis=1),
        "g_pre_mix": d_g_pre_mix, "g_post_mix": d_g_post_mix,
        "ssm_log_dt": d_log_dt, "ssm_a_re": d_are, "ssm_a_im": d_aim,
        "ssm_b_re": d_bre_t.T, "ssm_b_im": d_bim_t.T,
        "ssm_c_re": c_grad(d_ct[:, :nsb, :]), "ssm_c_im": -c_grad(d_ct[:, nsb:, :]),
        "ssm_d": d_ssm_d, "ssm_w_glu": _block_diag_take(d_wg, SSM_GROUP, SSM_GROUP), "ssm_b_glu": d_bglu,
        "sgu_ln_g": d_ln_g, "sgu_ln_b": d_ln_b, "sgu_w": d_wm * mask[None], "sgu_b": d_bs,
        "g_out_ssm": d_g_out_ssm, "g_out_sgu": d_g_out_sgu, "g_pre_ffn": d_g_pre_ffn, "g_post_ffn": d_g_post_ffn,
        "conv_b": d_conv_b, "conv_w_all": jnp.concatenate([d_cw0, d_cw1, d_cw2], axis=0),
    }
    shapes = {name: p[name].shape for name in _SMALL}
    shapes["conv_w_all"] = (3, 4 * nc_conv)
    pk = _Pack(shapes)
    zeros_cw = jnp.zeros(shapes["conv_w_all"], _F32)
    gathered = _all_gather8(_own_slot(pk.pack(small), dev), "gather_small")
    g_pk, d_pk, m_pk, v_pk = _small_reduce_adamw(
        gathered, pk.pack({**{n: p[n] for n in _SMALL}, "conv_w_all": zeros_cw}),
        pk.pack({**{n: m[n] for n in _SMALL}, "conv_w_all": zeros_cw}),
        pk.pack({**{n: v[n] for n in _SMALL}, "conv_w_all": zeros_cw}))
    grads, deltas, new_m, new_v = pk.unpack(g_pk), pk.unpack(d_pk), pk.unpack(m_pk), pk.unpack(v_pk)

    grads["conv_w"] = lax.dynamic_slice_in_dim(grads.pop("conv_w_all"), chip * nc_conv, nc_conv, axis=1)
    deltas["conv_w"], new_m["conv_w"], new_v["conv_w"] = _adamw(p["conv_w"], grads["conv_w"], m["conv_w"], v["conv_w"],
                                                                 "adamw_conv_w")
    d_mod_all = gathered.reshape(8, -1)[:, :N_MOD * d]
    d_mod_mine = lax.dynamic_slice_in_dim(d_mod_all, chip * ns_ada, ns_ada, axis=1)
    grads["w_ada"], deltas["w_ada"], new_m["w_ada"], new_v["w_ada"] = _wada_update(
        c_all.T, d_mod_mine, p["w_ada"], m["w_ada"], v["w_ada"])

    big = ["w_down", "w_up", "w_out", "w_in"]
    mine = []
    after = grads["w_ada"]
    for n, (sems, pair, land, _) in zip(big, (red_down, red_up, red_out, red_in)):
        pair, land = _scatter_wait(sems, pair, land, after, "scatter_wait_" + n)
        mine.append(_chip_sum(pair, land, sel, "chip_sum_" + n))
        after = mine[-1]
    joined = _pair_join(mine)
    for n, j in zip(big, joined):
        if n in ("w_in", "w_up"):
            grads[n] = j.reshape(p[n].shape)
            deltas[n], new_m[n], new_v[n] = _adamw(p[n], grads[n], m[n], v[n], "adamw_" + n)
        else:
            grads[n], deltas[n], new_m[n], new_v[n] = _adamw_halves(p[n], j, m[n], v[n], "adamw_" + n)
    return loss[0, 0], dx, grads, deltas, new_m, new_v


def kernel(x, c, w_ada, b_ada, g_pre_mix, g_post_mix, w_in, ssm_log_dt, ssm_a_re, ssm_a_im, ssm_b_re, ssm_b_im, ssm_c_re, ssm_c_im, ssm_d, ssm_w_glu, ssm_b_glu, sgu_ln_g, sgu_ln_b, sgu_w, sgu_b, g_out_ssm, g_out_sgu, w_out, g_pre_ffn, g_post_ffn, w_up, conv_w, conv_b, w_down, loss_target, m_w_ada, m_b_ada, m_g_pre_mix, m_g_post_mix, m_w_in, m_ssm_log_dt, m_ssm_a_re, m_ssm_a_im, m_ssm_b_re, m_ssm_b_im, m_ssm_c_re, m_ssm_c_im, m_ssm_d, m_ssm_w_glu, m_ssm_b_glu, m_sgu_ln_g, m_sgu_ln_b, m_sgu_w, m_sgu_b, m_g_out_ssm, m_g_out_sgu, m_w_out, m_g_pre_ffn, m_g_post_ffn, m_w_up, m_conv_w, m_conv_b, m_w_down, v_w_ada, v_b_ada, v_g_pre_mix, v_g_post_mix, v_w_in, v_ssm_log_dt, v_ssm_a_re, v_ssm_a_im, v_ssm_b_re, v_ssm_b_im, v_ssm_c_re, v_ssm_c_im, v_ssm_d, v_ssm_w_glu, v_ssm_b_glu, v_sgu_ln_g, v_sgu_ln_b, v_sgu_w, v_sgu_b, v_g_out_ssm, v_g_out_sgu, v_w_out, v_g_pre_ffn, v_g_post_ffn, v_w_up, v_conv_w, v_conv_b, v_w_down):
    given = dict(locals())
    drop = lambda a: a if a.ndim == 2 else a[0]
    p = {n: drop(given[n]) for n in _WEIGHTS}
    m = {n: drop(given["m_" + n]) for n in _WEIGHTS}
    v = {n: drop(given["v_" + n]) for n in _WEIGHTS}
    loss, dx, grads, deltas, new_m, new_v = _step(p, m, v, x[0], c, loss_target[0])
    loss = lax.psum(loss, ("x", "y", "c"))
    outs = [loss, dx[None]]
    for group in (grads, deltas, new_m, new_v):
        outs += [group[n].reshape(given[n].shape) for n in _WEIGHTS]
    return tuple(outs)
```

```python
import functools
import math

import jax
import jax.numpy as jnp
from jax import lax
from jax.experimental import pallas as pl
from jax.experimental.pallas import tpu as pltpu

_F32 = jnp.float32
_MXU = jnp.bfloat16
_WIRE = jnp.bfloat16

EPS = 1e-6
SSM_GROUP = 16
SSM_STATE = 64
GROUPS_PER_BLOCK = 8
BLOCK_CH = SSM_GROUP * GROUPS_PER_BLOCK
BLOCK_ST = SSM_STATE * GROUPS_PER_BLOCK
CHUNK = 128
TIME_TILE = 512
SUBLANES = 8
LANES = 128
N_MOD = 6
ADAM_LR, ADAM_B1, ADAM_B2, ADAM_EPS, ADAM_WD, ADAM_STEP = 0.001, 0.9, 0.999, 1e-08, 0.01, 10
_VMEM_LIMIT = 56 * 1024 * 1024
_MESH = pl.DeviceIdType.MESH
_ANY = pl.BlockSpec(memory_space=pl.ANY)
_HBM = pl.BlockSpec(memory_space=pltpu.HBM)
_SEM = pl.BlockSpec(memory_space=pltpu.SEMAPHORE)
_VMEM_WHOLE = pl.BlockSpec(memory_space=pltpu.VMEM)
_EFFECT = pltpu.SideEffectType.DATAFLOW_SIDE_EFFECTING
_GELU_C = math.sqrt(2.0 / math.pi)


def _cp(*sem):
    return pltpu.CompilerParams(dimension_semantics=sem, vmem_limit_bytes=_VMEM_LIMIT)


def _tile(dim, target, align=LANES):
    if dim <= target:
        return dim
    best = None
    for t in range(align, target + 1, align):
        if dim % t == 0:
            best = t
    assert best is not None, (dim, target, align)
    return best


def _gelu(x):
    return 0.5 * x * (1.0 + jnp.tanh(_GELU_C * (x + 0.044715 * (x * x * x))))


def _gelu_grad(x):
    t = jnp.tanh(_GELU_C * (x + 0.044715 * (x * x * x)))
    return 0.5 * (1.0 + t) + 0.5 * x * (1.0 - t * t) * (_GELU_C * (1.0 + 3.0 * 0.044715 * x * x))


def _sigmoid(x):
    return 1.0 / (1.0 + jnp.exp(-x))


def _colsum(x):
    return jnp.sum(x, axis=0, keepdims=True)


def _rowmean(x):
    return jnp.mean(x, axis=-1, keepdims=True)


def _acc(ref, first, val):
    @pl.when(first)
    def _():
        ref[...] = val

    @pl.when(jnp.logical_not(first))
    def _():
        ref[...] += val


def _place():
    mx, my, mc = lax.axis_index("x"), lax.axis_index("y"), lax.axis_index("c")
    chips = [(1 - mx, my), (mx, 1 - my), (1 - mx, 1 - my)]
    return mx, my, mc, chips


def _all_gather8(buf, name):
    def body(in_ref, out_ref, send_sems, recv_sems):
        mx, my, mc, chips = _place()
        me, sibling = (mx, my, mc), (mx, my, 1 - mc)

        def slot(ref, px, py, pc):
            return ref.at[4 * px + 2 * py + pc]

        def copy(k, block, to, src_ref=out_ref):
            return pltpu.make_async_remote_copy(
                src_ref=slot(src_ref, *block), dst_ref=slot(out_ref, *block),
                send_sem=send_sems.at[k], recv_sem=recv_sems.at[k], device_id=to, device_id_type=_MESH)

        first = [copy(0, me, sibling, in_ref)]
        first += [copy(1 + j, me, (*chip, mc), in_ref) for j, chip in enumerate(chips)]
        for cp in first:
            cp.start()
        passed = [copy(4 + j, (*chip, mc), sibling) for j, chip in enumerate(chips)]
        for j, chip in enumerate(chips):
            copy(1 + j, (*chip, mc), me).wait_recv()
            passed[j].start()
        copy(0, sibling, me).wait_recv()
        for j, chip in enumerate(chips):
            copy(4 + j, (*chip, 1 - mc), me).wait_recv()
        for cp in first + passed:
            cp.wait_send()

    return pl.pallas_call(
        body, name=name, out_shape=jax.ShapeDtypeStruct(buf.shape, buf.dtype),
        in_specs=[_ANY], out_specs=_ANY, input_output_aliases={0: 0},
        scratch_shapes=[pltpu.SemaphoreType.DMA((7,)), pltpu.SemaphoreType.DMA((7,))],
    )(buf)


def _own_slot(x, dev):
    return lax.dynamic_update_slice(jnp.zeros((8,) + x.shape, x.dtype), x[None], (dev, 0, 0))


def _cast_into_slot(w, sel):
    r, c = w.shape
    hr = r // 2
    tr = _tile(hr, 256, 16)
    nr = hr // tr

    def body(sel_ref, w_ref, o_ref):
        o_ref[...] = w_ref[...].astype(o_ref.dtype)

    return pl.pallas_call(
        body, name="cast_into_slot", out_shape=jax.ShapeDtypeStruct((4, 2, hr, c), _WIRE),
        grid_spec=pltpu.PrefetchScalarGridSpec(
            num_scalar_prefetch=1, grid=(2, nr),
            in_specs=[pl.BlockSpec((tr, c), lambda h, i, s: (h * nr + i, 0))],
            out_specs=pl.BlockSpec((None, None, tr, c), lambda h, i, s: (s[0], h, i, 0))),
        compiler_params=_cp("parallel", "parallel"),
    )(sel, w)


def _hbm(a):
    return pltpu.with_memory_space_constraint(a, pltpu.HBM)


def _after(vec, token):
    return vec + token[0:1, 0:1]


def _gather_start(bufs, name):
    n = len(bufs)
    nc = 3 * n

    def body(*refs):
        ins, send, recv, token = refs[:n], refs[n:n + nc], refs[n + nc:n + 2 * nc], refs[-1]
        mx, my, mc, chips = _place()
        j_me = 2 * mx + my
        for i in range(n):
            for k, chip in enumerate(chips):
                half = ins[i].at[j_me, mc]
                pltpu.make_async_remote_copy(
                    src_ref=half, dst_ref=half, send_sem=send[3 * i + k], recv_sem=recv[3 * i + k],
                    device_id=(*chip, mc), device_id_type=_MESH).start()
        token[...] = jnp.zeros_like(token)

    outs = pl.pallas_call(
        body, name=name,
        out_shape=tuple([pltpu.SemaphoreType.DMA(())] * (2 * nc) + [pltpu.HBM(b.shape, b.dtype) for b in bufs]
                        + [jax.ShapeDtypeStruct((SUBLANES, LANES), _F32)]),
        in_specs=tuple([_HBM] * n), out_specs=tuple([_SEM] * (2 * nc) + [_HBM] * n + [_VMEM_WHOLE]),
        input_output_aliases={i: 2 * nc + i for i in range(n)},
        compiler_params=pltpu.CompilerParams(has_side_effects=_EFFECT),
    )(*[_hbm(b) for b in bufs])
    sems = [(outs[3 * i:3 * i + 3], outs[nc + 3 * i:nc + 3 * i + 3]) for i in range(n)]
    return sems, list(outs[2 * nc:2 * nc + n]), outs[-1]


def _gather_wait(sems, buf, after, name):
    send, recv = sems

    def body(buf_ref, s0, s1, s2, r0, r1, r2, after_ref, out_ref):
        mx, my, mc, chips = _place()
        j_me = 2 * mx + my
        for k, (chip, s_k, r_k) in enumerate(zip(chips, (s0, s1, s2), (r0, r1, r2))):
            cp = pltpu.make_async_remote_copy(
                src_ref=buf_ref.at[j_me, mc], dst_ref=buf_ref.at[2 * chip[0] + chip[1], mc], send_sem=s_k, recv_sem=r_k,
                device_id=(*chip, mc), device_id_type=_MESH)
            cp.wait_send()
            cp.wait_recv()

    return pl.pallas_call(
        body, name=name, out_shape=pltpu.HBM(buf.shape, buf.dtype),
        in_specs=(_HBM,) + (_SEM,) * 6 + (_ANY,), out_specs=_HBM, input_output_aliases={0: 0},
        compiler_params=pltpu.CompilerParams(has_side_effects=_EFFECT),
    )(buf, *send, *recv, after)


def _pair_forward(bufs, name):
    n = len(bufs)

    def body(*refs):
        ins, outs = refs[:n], refs[n:2 * n]
        send_sems, recv_sems = refs[2 * n:]
        mx, my, mc, chips = _place()
        sibling = (mx, my, 1 - mc)
        cps = []
        for i in range(n):
            for k, chip in enumerate(chips):
                j_k = 2 * chip[0] + chip[1]
                cp = pltpu.make_async_remote_copy(
                    src_ref=ins[i].at[j_k, mc], dst_ref=outs[i].at[j_k, mc], send_sem=send_sems.at[3 * i + k],
                    recv_sem=recv_sems.at[3 * i + k], device_id=sibling, device_id_type=_MESH)
                cp.start()
                cps.append(cp)
        for i in range(n):
            for k, chip in enumerate(chips):
                other = outs[i].at[2 * chip[0] + chip[1], 1 - mc]
                pltpu.make_async_remote_copy(
                    src_ref=other, dst_ref=other, send_sem=send_sems.at[3 * i + k], recv_sem=recv_sems.at[3 * i + k],
                    device_id=sibling, device_id_type=_MESH).wait_recv()
        for cp in cps:
            cp.wait_send()

    return pl.pallas_call(
        body, name=name, out_shape=[jax.ShapeDtypeStruct(b.shape, b.dtype) for b in bufs],
        in_specs=[_ANY] * n, out_specs=[_ANY] * n, input_output_aliases={i: i for i in range(n)},
        scratch_shapes=[pltpu.SemaphoreType.DMA((3 * n,)), pltpu.SemaphoreType.DMA((3 * n,))],
    )(*bufs)


def _gather8_peers(buf_ref, mx, my, mc, chips):
    mine = buf_ref.at[4 * mx + 2 * my + mc]
    peers = [((mx, my, 1 - mc), mine, buf_ref.at[4 * mx + 2 * my + 1 - mc])]
    peers += [((*chip, mc), mine, buf_ref.at[4 * chip[0] + 2 * chip[1] + mc]) for chip in chips]
    return peers


def _gather8_start(buf, name):
    def body(buf_ref, *rest):
        send, recv, token = rest[0:4], rest[4:8], rest[-1]
        mx, my, mc, chips = _place()
        for k, (peer, src, _) in enumerate(_gather8_peers(buf_ref, mx, my, mc, chips)):
            pltpu.make_async_remote_copy(src_ref=src, dst_ref=src, send_sem=send[k], recv_sem=recv[k],
                                         device_id=peer, device_id_type=_MESH).start()
        token[...] = jnp.zeros_like(token)

    outs = pl.pallas_call(
        body, name=name,
        out_shape=tuple([pltpu.SemaphoreType.DMA(())] * 8 + [pltpu.HBM(buf.shape, buf.dtype),
                                                             jax.ShapeDtypeStruct((SUBLANES, LANES), _F32)]),
        in_specs=(_HBM,), out_specs=tuple([_SEM] * 8 + [_HBM, _VMEM_WHOLE]), input_output_aliases={0: 8},
        compiler_params=pltpu.CompilerParams(has_side_effects=_EFFECT),
    )(_hbm(buf))
    return (outs[0:4], outs[4:8]), outs[8], outs[9]


def _gather8_wait(sems, buf, after, name):
    send, recv = sems

    def body(buf_ref, s0, s1, s2, s3, r0, r1, r2, r3, after_ref, out_ref):
        mx, my, mc, chips = _place()
        for (peer, src, dst), s_k, r_k in zip(_gather8_peers(buf_ref, mx, my, mc, chips), (s0, s1, s2, s3), (r0, r1, r2, r3)):
            cp = pltpu.make_async_remote_copy(src_ref=src, dst_ref=dst, send_sem=s_k, recv_sem=r_k,
                                              device_id=peer, device_id_type=_MESH)
            cp.wait_send()
            cp.wait_recv()

    return pl.pallas_call(
        body, name=name, out_shape=pltpu.HBM(buf.shape, buf.dtype),
        in_specs=(_HBM,) + (_SEM,) * 8 + (_ANY,), out_specs=_HBM, input_output_aliases={0: 0},
        compiler_params=pltpu.CompilerParams(has_side_effects=_EFFECT),
    )(buf, *send, *recv, after)


def _gather8_forward(buf, name):
    def body(in_ref, out_ref, send_sems, recv_sems):
        mx, my, mc, chips = _place()
        sibling = (mx, my, 1 - mc)
        cps = []
        for k, chip in enumerate(chips):
            idx = 4 * chip[0] + 2 * chip[1] + mc
            cp = pltpu.make_async_remote_copy(src_ref=in_ref.at[idx], dst_ref=out_ref.at[idx], send_sem=send_sems.at[k],
                                              recv_sem=recv_sems.at[k], device_id=sibling, device_id_type=_MESH)
            cp.start()
            cps.append(cp)
        for k, chip in enumerate(chips):
            other = out_ref.at[4 * chip[0] + 2 * chip[1] + 1 - mc]
            pltpu.make_async_remote_copy(src_ref=other, dst_ref=other, send_sem=send_sems.at[k], recv_sem=recv_sems.at[k],
                                         device_id=sibling, device_id_type=_MESH).wait_recv()
        for cp in cps:
            cp.wait_send()

    return pl.pallas_call(
        body, name=name, out_shape=jax.ShapeDtypeStruct(buf.shape, buf.dtype),
        in_specs=[_ANY], out_specs=_ANY, input_output_aliases={0: 0},
        scratch_shapes=[pltpu.SemaphoreType.DMA((3,)), pltpu.SemaphoreType.DMA((3,))],
    )(buf)


def _scatter_start(pair, name):
    land = lax.empty((3,) + pair.shape[1:], pair.dtype)

    def body(pair_ref, land_ref, s0, s1, s2, r0, r1, r2, pair_thru, land_thru, token):
        mx, my, mc, chips = _place()
        for k, (chip, s_k, r_k) in enumerate(zip(chips, (s0, s1, s2), (r0, r1, r2))):
            pltpu.make_async_remote_copy(
                src_ref=pair_ref.at[2 * chip[0] + chip[1]], dst_ref=land_ref.at[k], send_sem=s_k, recv_sem=r_k,
                device_id=(*chip, mc), device_id_type=_MESH).start()
        token[...] = jnp.zeros_like(token)

    outs = pl.pallas_call(
        body, name=name,
        out_shape=tuple([pltpu.SemaphoreType.DMA(())] * 6 + [pltpu.HBM(pair.shape, pair.dtype), pltpu.HBM(land.shape, land.dtype),
                                                             jax.ShapeDtypeStruct((SUBLANES, LANES), _F32)]),
        in_specs=(_HBM, _HBM), out_specs=tuple([_SEM] * 6 + [_HBM, _HBM, _VMEM_WHOLE]),
        input_output_aliases={0: 6, 1: 7}, compiler_params=pltpu.CompilerParams(has_side_effects=_EFFECT),
    )(_hbm(pair), _hbm(land))
    return (outs[0:3], outs[3:6]), outs[6], outs[7], outs[8]


def _scatter_wait(sems, pair, land, after, name):
    send, recv = sems

    def body(pair_ref, land_ref, s0, s1, s2, r0, r1, r2, after_ref, pair_out, land_out):
        mx, my, mc, chips = _place()
        for k, (chip, s_k, r_k) in enumerate(zip(chips, (s0, s1, s2), (r0, r1, r2))):
            cp = pltpu.make_async_remote_copy(
                src_ref=pair_ref.at[2 * chip[0] + chip[1]], dst_ref=land_ref.at[k], send_sem=s_k, recv_sem=r_k,
                device_id=(*chip, mc), device_id_type=_MESH)
            cp.wait_send()
            cp.wait_recv()

    return pl.pallas_call(
        body, name=name, out_shape=(pltpu.HBM(pair.shape, pair.dtype), pltpu.HBM(land.shape, land.dtype)),
        in_specs=(_HBM, _HBM) + (_SEM,) * 6 + (_ANY,), out_specs=(_HBM, _HBM), input_output_aliases={0: 0, 1: 1},
        compiler_params=pltpu.CompilerParams(has_side_effects=_EFFECT),
    )(pair, land, *send, *recv, after)


def _pair_swap(arrs, name):
    n = len(arrs)

    def body(*refs):
        ins, outs = refs[:n], refs[n:2 * n]
        send_sems, recv_sems = refs[2 * n:]
        mx, my, mc, _ = _place()
        sibling = (mx, my, 1 - mc)
        cps = []
        for i in range(n):
            cp = pltpu.make_async_remote_copy(
                src_ref=ins[i].at[1 - mc], dst_ref=outs[i], send_sem=send_sems.at[i], recv_sem=recv_sems.at[i],
                device_id=sibling, device_id_type=_MESH)
            cp.start()
            cps.append(cp)
        for cp in cps:
            cp.wait()

    return pl.pallas_call(
        body, name=name, out_shape=[jax.ShapeDtypeStruct(a.shape[1:], a.dtype) for a in arrs],
        in_specs=[_ANY] * n, out_specs=[_ANY] * n,
        scratch_shapes=[pltpu.SemaphoreType.DMA((n,)), pltpu.SemaphoreType.DMA((n,))],
    )(*arrs)


def _pair_join(bufs):
    n = len(bufs)

    def body(*refs):
        ins, outs = refs[:n], refs[n:2 * n]
        send_sems, recv_sems = refs[2 * n:]
        mx, my, mc, _ = _place()
        sibling = (mx, my, 1 - mc)
        cps = []
        for i in range(n):
            cp = pltpu.make_async_remote_copy(
                src_ref=ins[i].at[mc], dst_ref=outs[i].at[mc], send_sem=send_sems.at[i], recv_sem=recv_sems.at[i],
                device_id=sibling, device_id_type=_MESH)
            cp.start()
            cps.append(cp)
        for i in range(n):
            other = outs[i].at[1 - mc]
            pltpu.make_async_remote_copy(
                src_ref=other, dst_ref=other, send_sem=send_sems.at[i], recv_sem=recv_sems.at[i],
                device_id=sibling, device_id_type=_MESH).wait_recv()
        for cp in cps:
            cp.wait_send()

    return pl.pallas_call(
        body, name="pair_join", out_shape=[jax.ShapeDtypeStruct(b.shape, b.dtype) for b in bufs],
        in_specs=[_ANY] * n, out_specs=[_ANY] * n, input_output_aliases={i: i for i in range(n)},
        scratch_shapes=[pltpu.SemaphoreType.DMA((n,)), pltpu.SemaphoreType.DMA((n,))],
    )(*bufs)


def _pair_sum(g, got, sel, name):
    _, four, hr, c = g.shape
    tr = _tile(hr, 512, 16)

    def body(sel_ref, g_ref, p_ref, o_ref):
        o_ref[...] = (g_ref[...].astype(_F32) + p_ref[...].astype(_F32)).astype(o_ref.dtype)

    return pl.pallas_call(
        body, name=name, out_shape=jax.ShapeDtypeStruct((four, hr, c), g.dtype),
        grid_spec=pltpu.PrefetchScalarGridSpec(
            num_scalar_prefetch=1, grid=(four, hr // tr),
            in_specs=[pl.BlockSpec((None, None, tr, c), lambda j, i, s: (s[1], j, i, 0)),
                      pl.BlockSpec((None, tr, c), lambda j, i, s: (j, i, 0))],
            out_specs=pl.BlockSpec((None, tr, c), lambda j, i, s: (j, i, 0))),
        compiler_params=_cp("parallel", "parallel"),
    )(sel, g, got)


def _chip_sum(pair, got, sel, name):
    _, hr, c = pair.shape
    tr = _tile(hr, 512, 16)

    def body(sel_ref, p_ref, q_ref, o_ref):
        o_ref[...] = ((p_ref[...].astype(_F32) + q_ref[0].astype(_F32)) + q_ref[1].astype(_F32)) + q_ref[2].astype(_F32)

    return pl.pallas_call(
        body, name=name, out_shape=jax.ShapeDtypeStruct((2, hr, c), _F32),
        grid_spec=pltpu.PrefetchScalarGridSpec(
            num_scalar_prefetch=1, grid=(hr // tr,),
            in_specs=[pl.BlockSpec((None, tr, c), lambda i, s: (s[0], i, 0)),
                      pl.BlockSpec((3, tr, c), lambda i, s: (0, i, 0))],
            out_specs=pl.BlockSpec((None, tr, c), lambda i, s: (s[1], i, 0))),
        compiler_params=_cp("parallel"),
    )(sel, pair, got)


def _matmul(a, b, dims, out_struct, grid, a_spec, b_spec, o_spec, acc_shape, k_axis, name):
    nk = grid[k_axis]

    def body(a_ref, b_ref, o_ref, acc_ref):
        prod = lax.dot_general(a_ref[...].astype(_MXU), b_ref[...].astype(_MXU), dims, preferred_element_type=_F32)
        if nk == 1:
            o_ref[...] = prod.astype(o_ref.dtype)
        else:
            k = pl.program_id(k_axis)

            @pl.when(k == 0)
            def _():
                acc_ref[...] = prod

            @pl.when(k > 0)
            def _():
                acc_ref[...] += prod

            @pl.when(k == nk - 1)
            def _():
                o_ref[...] = acc_ref[...].astype(o_ref.dtype)

    sem = ["parallel"] * len(grid)
    sem[k_axis] = "arbitrary"
    return pl.pallas_call(
        body, name=name, out_shape=out_struct, grid=grid, in_specs=[a_spec, b_spec], out_specs=o_spec,
        scratch_shapes=[pltpu.VMEM(acc_shape, _F32)], compiler_params=_cp(*sem),
    )(a, b)


def _mm_nn(a, w4, out_dtype, name, tm=512, tn=1536, tk=2048):
    m, k = a.shape
    j, _, ns = w4.shape
    tm, tn, tk = _tile(m, tm, 16), _tile(ns, tn), _tile(k, tk)
    nps = ns // tn
    return _matmul(
        a, w4, (((1,), (0,)), ((), ())), jax.ShapeDtypeStruct((m, j * ns), out_dtype),
        (m // tm, j * nps, k // tk),
        pl.BlockSpec((tm, tk), lambda mi, ni, ki: (mi, ki)),
        pl.BlockSpec((None, tk, tn), lambda mi, ni, ki: (ni // nps, ki, ni % nps)),
        pl.BlockSpec((tm, tn), lambda mi, ni, ki: (mi, ni)), (tm, tn), 2, name)


def _mm_nt(a, w4, out_dtype, name, tm=512, tn=2048, tk=1536):
    m = a.shape[-2]
    j, kw, ns = w4.shape
    tm, tn, tk = _tile(m, tm, 16), _tile(kw, tn), _tile(ns, tk)
    kps = ns // tk
    if a.ndim == 3:
        kph = a.shape[2] // tk
        a_spec = pl.BlockSpec((None, tm, tk), lambda mi, ni, ki: (ki // kph, mi, ki % kph))
    else:
        a_spec = pl.BlockSpec((tm, tk), lambda mi, ni, ki: (mi, ki))
    return _matmul(
        a, w4, (((1,), (1,)), ((), ())), jax.ShapeDtypeStruct((m, kw), out_dtype),
        (m // tm, kw // tn, j * kps),
        a_spec,
        pl.BlockSpec((None, tn, tk), lambda mi, ni, ki: (ki // kps, ni, ki % kps)),
        pl.BlockSpec((tm, tn), lambda mi, ni, ki: (mi, ni)), (tm, tn), 2, name)


def _mm_tn_cols(a, b, name, tm=1024, tn=1536, tk=2048):
    m, ka = a.shape
    ns = (b.shape[-1] * (2 if b.ndim == 3 else 1)) // 4
    hr = ka // 2
    tm, tn, tk = _tile(hr, tm), _tile(ns, tn), _tile(m, tk, 16)
    mph, nps = hr // tm, ns // tn
    if b.ndim == 3:
        b_spec = pl.BlockSpec((None, tk, tn), lambda ni, mi, ki: (ni // (2 * nps), ki, ni % (2 * nps)))
    else:
        b_spec = pl.BlockSpec((tk, tn), lambda ni, mi, ki: (ki, ni))
    return _matmul(
        a, b, (((0,), (0,)), ((), ())), jax.ShapeDtypeStruct((2, 4, hr, ns), _WIRE),
        (4 * nps, 2 * mph, m // tk),
        pl.BlockSpec((tk, tm), lambda ni, mi, ki: (ki, mi)),
        b_spec,
        pl.BlockSpec((None, None, tm, tn), lambda ni, mi, ki: (mi // mph, ni // nps, mi % mph, ni % nps)),
        (tm, tn), 2, name)


def _mm_tn_rows(a, b, name, tm=1536, tn=1024, tk=2048):
    m, ka = a.shape
    r = ka // 4
    hc = b.shape[1] // 2
    tm, tn, tk = _tile(r, tm), _tile(hc, tn), _tile(m, tk, 16)
    mpr, nph = r // tm, hc // tn
    return _matmul(
        a, b, (((0,), (0,)), ((), ())), jax.ShapeDtypeStruct((2, 4, r, hc), _WIRE),
        (2 * nph, 4 * mpr, m // tk),
        pl.BlockSpec((tk, tm), lambda ni, mi, ki: (ki, mi)),
        pl.BlockSpec((tk, tn), lambda ni, mi, ki: (ki, ni)),
        pl.BlockSpec((None, None, tm, tn), lambda ni, mi, ki: (ni // nph, mi // mpr, mi % mpr, ni % nph)),
        (tm, tn), 2, name)


def _row_call(body, name, rows, ins, outs, tm=256):
    tm = _tile(rows, tm, 16)

    def spec(shape, kind):
        if kind == "rows":
            return pl.BlockSpec((tm, shape[1]), lambda i: (i, 0))
        return pl.BlockSpec(shape, lambda i: (0,) * len(shape))

    return pl.pallas_call(
        body, name=name, grid=(rows // tm,),
        in_specs=[spec(a.shape, kind) for a, kind in ins],
        out_specs=[spec(o.shape, kind) for o, kind in outs],
        out_shape=[o for o, _ in outs],
        compiler_params=_cp("arbitrary"),
    )(*[a for a, _ in ins])


def _rms(x):
    r = lax.rsqrt(_rowmean(x * x) + EPS)
    return x * r, r


def _rms_bwd(dxh, xh, r):
    return r * (dxh - xh * _rowmean(dxh * xh))


def _fwd_pre_mix(x, g, sc, sh):
    s, d = x.shape

    def body(x_ref, g_ref, sc_ref, sh_ref, h_ref):
        xh, _ = _rms(x_ref[...])
        h_ref[...] = (xh * g_ref[...] * (1.0 + sc_ref[...]) + sh_ref[...]).astype(h_ref.dtype)

    return _row_call(body, "fwd_pre_mix", s, [(x, "rows"), (g, "vec"), (sc, "vec"), (sh, "vec")],
                     [(jax.ShapeDtypeStruct((s, d), _MXU), "rows")])[0]


def _fwd_mid(o, x, gt1, g_post, g_pre2, sc2, sh2):
    s, d = x.shape

    def body(o_ref, x_ref, gt_ref, gp_ref, g2_ref, sc_ref, sh_ref, x1_ref, h2_ref):
        oh, _ = _rms(o_ref[...])
        x1 = x_ref[...] + gt_ref[...] * (oh * gp_ref[...])
        x1_ref[...] = x1
        xh, _ = _rms(x1)
        h2_ref[...] = (xh * g2_ref[...] * (1.0 + sc_ref[...]) + sh_ref[...]).astype(h2_ref.dtype)

    return _row_call(body, "fwd_mid", s,
                     [(o, "rows"), (x, "rows"), (gt1, "vec"), (g_post, "vec"), (g_pre2, "vec"), (sc2, "vec"),
                      (sh2, "vec")],
                     [(jax.ShapeDtypeStruct((s, d), _F32), "rows"), (jax.ShapeDtypeStruct((s, d), _MXU), "rows")])


def _loss_and_post_ffn_bwd(f, x1, tgt, gt2, g_post):
    s, d = x1.shape

    def body(f_ref, x1_ref, t_ref, gt_ref, g_ref, dx2_ref, df_ref, dgt_ref, dg_ref, loss_ref):
        first = pl.program_id(0) == 0
        fh, r = _rms(f_ref[...])
        n = fh * g_ref[...]
        e = x1_ref[...] + gt_ref[...] * n - t_ref[...]
        _acc(loss_ref, first, jnp.sum(_colsum(e * e), axis=1, keepdims=True) * (0.5 / d))
        dx2 = e * (1.0 / d)
        dx2_ref[...] = dx2
        _acc(dgt_ref, first, _colsum(dx2 * n))
        dn = dx2 * gt_ref[...]
        _acc(dg_ref, first, _colsum(dn * fh))
        df_ref[...] = _rms_bwd(dn * g_ref[...], fh, r).astype(df_ref.dtype)

    vec = jax.ShapeDtypeStruct((1, d), _F32)
    return _row_call(body, "loss_post_ffn_bwd", s,
                     [(f, "rows"), (x1, "rows"), (tgt, "rows"), (gt2, "vec"), (g_post, "vec")],
                     [(jax.ShapeDtypeStruct((s, d), _F32), "rows"), (jax.ShapeDtypeStruct((s, d), _MXU), "rows"),
                      (vec, "vec"), (vec, "vec"), (jax.ShapeDtypeStruct((1, 1), _F32), "vec")])


def _bwd_mid(dh2, x1, dx2, o, g_pre2, sc2, gt1, g_post):
    s, d = x1.shape

    def body(dh_ref, x1_ref, dx2_ref, o_ref, g2_ref, sc_ref, gt_ref, gp_ref,
             dx1_ref, do_ref, dsc_ref, dsh_ref, dg2_ref, dgt_ref, dgp_ref):
        first = pl.program_id(0) == 0
        dh = dh_ref[...]
        xh, r = _rms(x1_ref[...])
        _acc(dsh_ref, first, _colsum(dh))
        _acc(dsc_ref, first, _colsum(dh * (xh * g2_ref[...])))
        dn = dh * (1.0 + sc_ref[...])
        _acc(dg2_ref, first, _colsum(dn * xh))
        dx1 = dx2_ref[...] + _rms_bwd(dn * g2_ref[...], xh, r)
        dx1_ref[...] = dx1
        oh, ro = _rms(o_ref[...])
        _acc(dgt_ref, first, _colsum(dx1 * (oh * gp_ref[...])))
        dno = dx1 * gt_ref[...]
        _acc(dgp_ref, first, _colsum(dno * oh))
        do_ref[...] = _rms_bwd(dno * gp_ref[...], oh, ro).astype(do_ref.dtype)

    vec = jax.ShapeDtypeStruct((1, d), _F32)
    return _row_call(body, "bwd_mid", s,
                     [(dh2, "rows"), (x1, "rows"), (dx2, "rows"), (o, "rows"), (g_pre2, "vec"), (sc2, "vec"),
                      (gt1, "vec"), (g_post, "vec")],
                     [(jax.ShapeDtypeStruct((s, d), _F32), "rows"), (jax.ShapeDtypeStruct((s, d), _MXU), "rows"),
                      (vec, "vec"), (vec, "vec"), (vec, "vec"), (vec, "vec"), (vec, "vec")])


def _bwd_pre_mix(dh1, x, dx1, g, sc1):
    s, d = x.shape

    def body(dh_ref, x_ref, dx1_ref, g_ref, sc_ref, dx_ref, dsc_ref, dsh_ref, dg_ref):
        first = pl.program_id(0) == 0
        dh = dh_ref[...]
        xh, r = _rms(x_ref[...])
        _acc(dsh_ref, first, _colsum(dh))
        _acc(dsc_ref, first, _colsum(dh * (xh * g_ref[...])))
        dn = dh * (1.0 + sc_ref[...])
        _acc(dg_ref, first, _colsum(dn * xh))
        dx_ref[...] = dx1_ref[...] + _rms_bwd(dn * g_ref[...], xh, r)

    vec = jax.ShapeDtypeStruct((1, d), _F32)
    return _row_call(body, "bwd_pre_mix", s,
                     [(dh1, "rows"), (x, "rows"), (dx1, "rows"), (g, "vec"), (sc1, "vec")],
                     [(jax.ShapeDtypeStruct((s, d), _F32), "rows"), (vec, "vec"), (vec, "vec"), (vec, "vec")])


def _mix_norm_fwd(y_ssm, y_sgu, g_ssm, g_sgu):
    s, h = y_ssm.shape

    def body(a_ref, b_ref, ga_ref, gb_ref, o_ref):
        ah, _ = _rms(a_ref[...])
        bh, _ = _rms(b_ref[...])
        o_ref[:, 0:h] = (ah * ga_ref[...]).astype(o_ref.dtype)
        o_ref[:, h:2 * h] = (bh * gb_ref[...]).astype(o_ref.dtype)

    return _row_call(body, "mix_norm_fwd", s, [(y_ssm, "rows"), (y_sgu, "rows"), (g_ssm, "vec"), (g_sgu, "vec")],
                     [(jax.ShapeDtypeStruct((s, 2 * h), _MXU), "rows")])[0]


def _mix_norm_bwd(dyc, y_ssm, y_sgu, g_ssm, g_sgu):
    s, h = y_ssm.shape

    def body(d_ref, a_ref, b_ref, ga_ref, gb_ref, da_ref, db_ref, dga_ref, dgb_ref):
        first = pl.program_id(0) == 0
        for lo, y_ref, g_ref, dy_ref, dg_ref in ((0, a_ref, ga_ref, da_ref, dga_ref), (h, b_ref, gb_ref, db_ref, dgb_ref)):
            d = d_ref[:, lo:lo + h]
            yh, r = _rms(y_ref[...])
            _acc(dg_ref, first, _colsum(d * yh))
            dy_ref[...] = _rms_bwd(d * g_ref[...], yh, r)

    vec = jax.ShapeDtypeStruct((1, h), _F32)
    full = jax.ShapeDtypeStruct((s, h), _F32)
    return _row_call(body, "mix_norm_bwd", s,
                     [(dyc, "rows"), (y_ssm, "rows"), (y_sgu, "rows"), (g_ssm, "vec"), (g_sgu, "vec")],
                     [(full, "rows"), (full, "rows"), (vec, "vec"), (vec, "vec")])


def _shift_down(x, k):
    row = lax.broadcasted_iota(jnp.int32, x.shape, 0)
    return jnp.where(row >= k, pltpu.roll(x, k, 0), 0.0)


def _shift_up(x, k):
    n = x.shape[0]
    row = lax.broadcasted_iota(jnp.int32, x.shape, 0)
    return jnp.where(row < n - k, pltpu.roll(x, n - k, 0), 0.0)


def _conv(x, w_ref, b_ref):
    return b_ref[...] + w_ref[0:1, :] * _shift_down(x, 2) + w_ref[1:2, :] * _shift_down(x, 1) + w_ref[2:3, :] * x


def _conv_act_fwd(up_pre, conv_w, conv_b):
    s, f2 = up_pre.shape
    f = f2 // 2
    tc = _tile(f, 256)
    nf = f // tc

    def body(a_ref, b_ref, wa_ref, wb_ref, ba_ref, bb_ref, o_ref):
        a = _conv(a_ref[...], wa_ref, ba_ref)
        b = _conv(b_ref[...], wb_ref, bb_ref)
        o_ref[...] = (a * _sigmoid(a) * b).astype(o_ref.dtype)

    return pl.pallas_call(
        body, name="conv_act_fwd", grid=(nf,), out_shape=jax.ShapeDtypeStruct((s, f), _MXU),
        in_specs=[pl.BlockSpec((s, tc), lambda n: (0, n)), pl.BlockSpec((s, tc), lambda n: (0, n + nf)),
                  pl.BlockSpec((3, tc), lambda n: (0, n)), pl.BlockSpec((3, tc), lambda n: (0, n + nf)),
                  pl.BlockSpec((1, tc), lambda n: (0, n)), pl.BlockSpec((1, tc), lambda n: (0, n + nf))],
        out_specs=pl.BlockSpec((s, tc), lambda n: (0, n)), compiler_params=_cp("parallel"),
    )(up_pre, up_pre, conv_w, conv_w, conv_b, conv_b)


def _conv_act_bwd(up_pre, d_act, conv_w, conv_b):
    s, f2 = up_pre.shape
    f = f2 // 2
    tc = _tile(f, 256)
    nf = f // tc

    def body(a_ref, b_ref, d_ref, wa_ref, wb_ref, ba_ref, bb_ref,
             du_ref, w0a, w0b, w1a, w1b, w2a, w2b, dba, dbb):
        xa, xb = a_ref[...], b_ref[...]
        a = _conv(xa, wa_ref, ba_ref)
        b = _conv(xb, wb_ref, bb_ref)
        sg = _sigmoid(a)
        d = d_ref[...]
        d_a = d * b * (sg * (1.0 + a * (1.0 - sg)))
        d_b = d * (a * sg)
        for x, du, w_ref, o_ref, o0, o1, o2, ob in ((xa, d_a, wa_ref, du_ref.at[0], w0a, w1a, w2a, dba),
                                                     (xb, d_b, wb_ref, du_ref.at[1], w0b, w1b, w2b, dbb)):
            ob[...] = _colsum(du)
            o0[...] = _colsum(du * _shift_down(x, 2))
            o1[...] = _colsum(du * _shift_down(x, 1))
            o2[...] = _colsum(du * x)
            o_ref[...] = (w_ref[2:3, :] * du + w_ref[1:2, :] * _shift_up(du, 1)
                          + w_ref[0:1, :] * _shift_up(du, 2)).astype(o_ref.dtype)

    col_a = pl.BlockSpec((s, tc), lambda n: (0, n))
    col_b = pl.BlockSpec((s, tc), lambda n: (0, n + nf))
    vec_a = pl.BlockSpec((1, tc), lambda n: (0, n))
    vec_b = pl.BlockSpec((1, tc), lambda n: (0, n + nf))
    vec = jax.ShapeDtypeStruct((1, f), _F32)
    outs = pl.pallas_call(
        body, name="conv_act_bwd", grid=(nf,),
        in_specs=[col_a, col_b, col_a, pl.BlockSpec((3, tc), lambda n: (0, n)),
                  pl.BlockSpec((3, tc), lambda n: (0, n + nf)), vec_a, vec_b],
        out_specs=[pl.BlockSpec((2, s, tc), lambda n: (0, 0, n))] + [vec_a] * 8,
        out_shape=[jax.ShapeDtypeStruct((2, s, f), _MXU)] + [vec] * 8, compiler_params=_cp("parallel"),
    )(up_pre, up_pre, d_act, conv_w, conv_w, conv_b, conv_b)
    du, w0a, w0b, w1a, w1b, w2a, w2b, dba, dbb = outs
    cat = lambda p, q: jnp.concatenate([p, q], axis=1)
    return du, cat(w0a, w0b), cat(w1a, w1b), cat(w2a, w2b), cat(dba, dbb)


def _sgu_recompute(zu_ref, zv_ref, lng_ref, lnb_ref, wm_ref, bs_ref, nh):
    zu, zv = zu_ref[...], zv_ref[...]
    u = _gelu(zu)
    gv = _gelu(zv)
    xc = gv - _rowmean(gv)
    rs = lax.rsqrt(_rowmean(xc * xc) + EPS)
    vh = xc * rs
    v = vh * lng_ref[...] + lnb_ref[...]
    mixed = []
    for h in range(nh):
        vhd = v[:, h * CHUNK:(h + 1) * CHUNK].astype(_MXU)
        mixed.append(jnp.dot(wm_ref[h].astype(_MXU), vhd, preferred_element_type=_F32) + bs_ref[h])
    return zu, zv, u, vh, rs, v, mixed


def _sgu_fwd(z, ln_g, ln_b, wm, bs):
    s = z.shape[0]
    nh = wm.shape[0]
    hd = nh * CHUNK

    def body(zu_ref, zv_ref, lng_ref, lnb_ref, wm_ref, bs_ref, y_ref):
        _, _, u, _, _, _, mixed = _sgu_recompute(zu_ref, zv_ref, lng_ref, lnb_ref, wm_ref, bs_ref, nh)
        for h in range(nh):
            y_ref[:, h * CHUNK:(h + 1) * CHUNK] = u[:, h * CHUNK:(h + 1) * CHUNK] * mixed[h]

    vec = pl.BlockSpec((1, hd), lambda i: (0, 0))
    return pl.pallas_call(
        body, name="sgu_fwd", grid=(s // CHUNK,), out_shape=jax.ShapeDtypeStruct((s, hd), _F32),
        in_specs=[pl.BlockSpec((CHUNK, hd), lambda i: (i, 1)), pl.BlockSpec((CHUNK, hd), lambda i: (i, 2)), vec, vec,
                  pl.BlockSpec((nh, CHUNK, CHUNK), lambda i: (0, 0, 0)), pl.BlockSpec((nh, CHUNK, 1), lambda i: (0, 0, 0))],
        out_specs=pl.BlockSpec((CHUNK, hd), lambda i: (i, 0)), compiler_params=_cp("parallel"),
    )(z, z, ln_g, ln_b, wm, bs)


def _sgu_bwd(z, dy, dz_ssm, ln_g, ln_b, wm, bs):
    s = z.shape[0]
    nh = wm.shape[0]
    hd = nh * CHUNK

    def body(zu_ref, zv_ref, dy_ref, dzs_ref, lng_ref, lnb_ref, wm_ref, bs_ref,
             dz_ref, dlg_ref, dlb_ref, dwm_ref, dbs_ref, dv_scr):
        first = pl.program_id(0) == 0
        zu, zv, u, vh, rs, v, mixed = _sgu_recompute(zu_ref, zv_ref, lng_ref, lnb_ref, wm_ref, bs_ref, nh)
        dy = dy_ref[...]
        dz_ref[:, 0:hd] = dzs_ref[...]
        for h in range(nh):
            cols = slice(h * CHUNK, (h + 1) * CHUNK)
            dyh = dy[:, cols]
            dz_ref[:, hd + h * CHUNK:hd + (h + 1) * CHUNK] = (dyh * mixed[h] * _gelu_grad(zu[:, cols])).astype(dz_ref.dtype)
            dm = dyh * u[:, cols]
            dmx = dm.astype(_MXU)
            _acc(dbs_ref.at[h], first, jnp.sum(dm, axis=1, keepdims=True))
            _acc(dwm_ref.at[h], first,
                 lax.dot_general(dmx, v[:, cols].astype(_MXU), (((1,), (1,)), ((), ())), preferred_element_type=_F32))
            dv_scr[:, cols] = lax.dot_general(wm_ref[h].astype(_MXU), dmx, (((0,), (0,)), ((), ())),
                                              preferred_element_type=_F32)
        dv = dv_scr[...]
        _acc(dlg_ref, first, _colsum(dv * vh))
        _acc(dlb_ref, first, _colsum(dv))
        dvh = dv * lng_ref[...]
        dgv = rs * (dvh - _rowmean(dvh) - vh * _rowmean(dvh * vh))
        dz_ref[:, 2 * hd:3 * hd] = (dgv * _gelu_grad(zv)).astype(dz_ref.dtype)

    vec = pl.BlockSpec((1, hd), lambda i: (0, 0))
    wspec = pl.BlockSpec((nh, CHUNK, CHUNK), lambda i: (0, 0, 0))
    bspec = pl.BlockSpec((nh, CHUNK, 1), lambda i: (0, 0, 0))
    rows = pl.BlockSpec((CHUNK, hd), lambda i: (i, 0))
    return pl.pallas_call(
        body, name="sgu_bwd", grid=(s // CHUNK,),
        out_shape=[jax.ShapeDtypeStruct((s, 3 * hd), _MXU), jax.ShapeDtypeStruct((1, hd), _F32),
                   jax.ShapeDtypeStruct((1, hd), _F32), jax.ShapeDtypeStruct((nh, CHUNK, CHUNK), _F32),
                   jax.ShapeDtypeStruct((nh, CHUNK, 1), _F32)],
        in_specs=[pl.BlockSpec((CHUNK, hd), lambda i: (i, 1)), pl.BlockSpec((CHUNK, hd), lambda i: (i, 2)),
                  rows, rows, vec, vec, wspec, bspec],
        out_specs=[pl.BlockSpec((CHUNK, 3 * hd), lambda i: (i, 0)), vec, vec, wspec, bspec],
        scratch_shapes=[pltpu.VMEM((CHUNK, hd), _F32)], compiler_params=_cp("arbitrary"),
    )(z, z, dy, dz_ssm, ln_g, ln_b, wm, bs)


def _ssm_prep(log_dt, a_re, a_im, b_re_t, b_im_t):
    gn = a_re.shape[1]

    def body(ldt_ref, are_ref, aim_ref, br_ref, bi_ref, pr_ref, pi_ref, bbr_ref, bbi_ref):
        dt = jnp.exp(ldt_ref[...])
        are, aim = are_ref[...], aim_ref[...]
        k = (lax.broadcasted_iota(jnp.int32, (SUBLANES, gn), 0) + 1).astype(_F32)
        mag = jnp.exp(k * (are * dt))
        ang = k * (aim * dt)
        pr_ref[...] = mag * jnp.cos(ang)
        pi_ref[...] = mag * jnp.sin(ang)
        m1 = jnp.exp(are * dt)
        lr, li = m1 * jnp.cos(aim * dt), m1 * jnp.sin(aim * dt)
        den = are * are + aim * aim
        nr = lr - 1.0
        f_re = (nr * are + li * aim) / den
        f_im = (li * are - nr * aim) / den
        bbr_ref[...] = f_re * br_ref[...] - f_im * bi_ref[...]
        bbi_ref[...] = f_re * bi_ref[...] + f_im * br_ref[...]

    pw = jax.ShapeDtypeStruct((SUBLANES, gn), _F32)
    bb = jax.ShapeDtypeStruct(b_re_t.shape, _F32)
    return pl.pallas_call(body, name="ssm_prep", out_shape=[pw, pw, bb, bb])(log_dt, a_re, a_im, b_re_t, b_im_t)


def _ssm_prep_bwd(log_dt, a_re, a_im, b_re_t, b_im_t, d_bbr, d_bbi, d_lr, d_li):
    def body(ldt_ref, are_ref, aim_ref, br_ref, bi_ref, dbr_ref, dbi_ref, dlr_ref, dli_ref,
             obr_ref, obi_ref, oar_ref, oai_ref, odt_ref):
        dt = jnp.exp(ldt_ref[...])
        are, aim = are_ref[...], aim_ref[...]
        m1 = jnp.exp(are * dt)
        lr, li = m1 * jnp.cos(aim * dt), m1 * jnp.sin(aim * dt)
        den = are * are + aim * aim
        nr = lr - 1.0
        f_re = (nr * are + li * aim) / den
        f_im = (li * are - nr * aim) / den
        br, bi, dbr, dbi = br_ref[...], bi_ref[...], dbr_ref[...], dbi_ref[...]
        obr_ref[...] = f_re * dbr + f_im * dbi
        obi_ref[...] = f_re * dbi - f_im * dbr
        gf_re = _colsum(br * dbr + bi * dbi)
        gf_im = _colsum(br * dbi - bi * dbr)
        il_re, il_im = are / den, -aim / den
        glb_re = dlr_ref[...] + (il_re * gf_re + il_im * gf_im)
        glb_im = dli_ref[...] + (il_re * gf_im - il_im * gf_re)
        q_re = -(f_re * il_re - f_im * il_im)
        q_im = -(f_re * il_im + f_im * il_re)
        gl_re = q_re * gf_re + q_im * gf_im
        gl_im = q_re * gf_im - q_im * gf_re
        gl_re = gl_re + dt * (lr * glb_re + li * glb_im)
        gl_im = gl_im + dt * (lr * glb_im - li * glb_re)
        w_re = are * lr - aim * li
        w_im = are * li + aim * lr
        oar_ref[...] = gl_re
        oai_ref[...] = gl_im
        odt_ref[...] = w_re * glb_re + w_im * glb_im

    bb = jax.ShapeDtypeStruct(b_re_t.shape, _F32)
    v = jax.ShapeDtypeStruct(a_re.shape, _F32)
    return pl.pallas_call(body, name="ssm_prep_bwd", out_shape=[bb, bb, v, v, v])(
        log_dt, a_re, a_im, b_re_t, b_im_t, d_bbr, d_bbi, d_lr, d_li)


def _group_sum(d_dt, log_dt):
    def body(d_ref, l_ref, o_ref):
        o_ref[...] = jnp.sum(d_ref[...], axis=1, keepdims=True) * jnp.exp(l_ref[...])

    return pl.pallas_call(body, name="ssm_dt_grad", out_shape=jax.ShapeDtypeStruct(log_dt.shape, _F32))(d_dt, log_dt)


def _scan_rows(src_ref, dst_ref, nrt, steps, ptab, carry0, reverse):
    ns = BLOCK_ST
    row = lax.broadcasted_iota(jnp.int32, (SUBLANES, ns), 0)
    pr, pi = ptab

    def body(i, carry):
        cr, ci = carry
        it = (nrt - 1 - i) if reverse else i
        r0 = pl.multiple_of(it * SUBLANES, SUBLANES)
        xr = src_ref[pl.ds(r0, SUBLANES), 0:ns]
        xi = src_ref[pl.ds(r0, SUBLANES), ns:2 * ns]
        for k, (ar, ai) in zip((1, 2, 4), steps):
            if reverse:
                keep = row < SUBLANES - k
                sr = jnp.where(keep, pltpu.roll(xr, SUBLANES - k, 0), 0.0)
                si = jnp.where(keep, pltpu.roll(xi, SUBLANES - k, 0), 0.0)
            else:
                keep = row >= k
                sr = jnp.where(keep, pltpu.roll(xr, k, 0), 0.0)
                si = jnp.where(keep, pltpu.roll(xi, k, 0), 0.0)
            xr, xi = xr + ar * sr - ai * si, xi + ar * si + ai * sr
        xr, xi = xr + pr * cr - pi * ci, xi + pr * ci + pi * cr
        dst_ref[pl.ds(r0, SUBLANES), 0:ns] = xr
        dst_ref[pl.ds(r0, SUBLANES), ns:2 * ns] = xi
        if reverse:
            return xr[0:1, :], xi[0:1, :]
        return xr[SUBLANES - 1:SUBLANES, :], xi[SUBLANES - 1:SUBLANES, :]

    return lax.fori_loop(0, nrt, body, carry0)


def _scan_consts(p_ref, conj):
    ns = BLOCK_ST
    sign = -1.0 if conj else 1.0
    bc = lambda r: jnp.broadcast_to(r, (SUBLANES, ns))
    steps = [(bc(p_ref[k - 1:k, 0:ns]), bc(sign * p_ref[k - 1:k, ns:2 * ns])) for k in (1, 2, 4)]
    return steps


def _ssm_block_fwd(u, bbt_ref, ct_ref, d_ref, wg_ref, bg_ref, p_ref, bu_scr, h_scr, carry_in, nrt):
    ns = BLOCK_ST
    bu_scr[...] = jnp.dot(u.astype(_MXU), bbt_ref[...].astype(_MXU), preferred_element_type=_F32)
    ptab = (p_ref[:, 0:ns], p_ref[:, ns:2 * ns])
    carry = _scan_rows(bu_scr, h_scr, nrt, _scan_consts(p_ref, False), ptab, carry_in, False)
    y = jnp.dot(h_scr[...].astype(_MXU), ct_ref[...].astype(_MXU), preferred_element_type=_F32) + d_ref[...] * u
    yg = _gelu(y)
    gate = _sigmoid(jnp.dot(yg.astype(_MXU), wg_ref[...].astype(_MXU), preferred_element_type=_F32) + bg_ref[...])
    return y, yg, gate, carry


def _ssm_specs(nb, nt, t, reverse):
    tt = (lambda ti: nt - 1 - ti) if reverse else (lambda ti: ti)
    ns2 = 2 * BLOCK_ST
    return dict(
        z=pl.BlockSpec((t, BLOCK_CH), lambda b, ti: (tt(ti), b)),
        bbt=pl.BlockSpec((None, BLOCK_CH, ns2), lambda b, ti: (b, 0, 0)),
        ct=pl.BlockSpec((None, ns2, BLOCK_CH), lambda b, ti: (b, 0, 0)),
        vec=pl.BlockSpec((1, BLOCK_CH), lambda b, ti: (0, b)),
        wg=pl.BlockSpec((None, BLOCK_CH, BLOCK_CH), lambda b, ti: (b, 0, 0)),
        p=pl.BlockSpec((None, SUBLANES, ns2), lambda b, ti: (b, 0, 0)),
        hb=pl.BlockSpec((None, None, SUBLANES, ns2), lambda b, ti: (b, tt(ti), 0, 0)),
        h=pl.BlockSpec((None, t, ns2), lambda b, ti: (b, tt(ti), 0)),
        acc_vec=pl.BlockSpec((None, 1, ns2), lambda b, ti: (b, 0, 0)),
    )


def _ssm_fwd(z, bbt, ct, dvec, wg, bglu, ptab):
    s = z.shape[0]
    nb = bbt.shape[0]
    t = _tile(s, TIME_TILE, SUBLANES)
    nt = s // t
    ns = BLOCK_ST
    sp = _ssm_specs(nb, nt, t, False)

    def body(z_ref, bbt_ref, ct_ref, d_ref, wg_ref, bg_ref, p_ref, y2_ref, y_ref, h_ref, hb_ref, bu_scr, h_scr, carry_scr):
        ti = pl.program_id(1)

        @pl.when(ti == 0)
        def _():
            carry_scr[...] = jnp.zeros_like(carry_scr)

        hb_ref[...] = carry_scr[...]
        carry_in = (carry_scr[0:1, 0:ns], carry_scr[0:1, ns:2 * ns])
        y, yg, gate, (cr, ci) = _ssm_block_fwd(z_ref[...], bbt_ref, ct_ref, d_ref, wg_ref, bg_ref, p_ref,
                                               bu_scr, h_scr, carry_in, t // SUBLANES)
        y2_ref[...] = yg * gate
        y_ref[...] = y
        h_ref[...] = h_scr[...].astype(h_ref.dtype)
        carry_scr[:, 0:ns] = jnp.broadcast_to(cr, (SUBLANES, ns))
        carry_scr[:, ns:2 * ns] = jnp.broadcast_to(ci, (SUBLANES, ns))

    ych = jax.ShapeDtypeStruct((s, nb * BLOCK_CH), _F32)
    return pl.pallas_call(
        body, name="ssm_fwd", grid=(nb, nt),
        out_shape=[ych, ych, jax.ShapeDtypeStruct((nb, s, 2 * ns), _MXU),
                   jax.ShapeDtypeStruct((nb, nt, SUBLANES, 2 * ns), _F32)],
        in_specs=[sp["z"], sp["bbt"], sp["ct"], sp["vec"], sp["wg"], sp["vec"], sp["p"]],
        out_specs=[sp["z"], sp["z"], sp["h"], sp["hb"]],
        scratch_shapes=[pltpu.VMEM((t, 2 * ns), _F32), pltpu.VMEM((t, 2 * ns), _F32), pltpu.VMEM((SUBLANES, 2 * ns), _F32)],
        compiler_params=_cp("parallel", "arbitrary"),
    )(z, bbt, ct, dvec, wg, bglu, ptab)


def _ssm_bwd(z, y_pre, h_all, dy2, hb, bbt, ct, dvec, wg, bglu, ptab, ptab_rev):
    s = z.shape[0]
    nb = bbt.shape[0]
    t = _tile(s, TIME_TILE, SUBLANES)
    nt = s // t
    ns = BLOCK_ST
    sp = _ssm_specs(nb, nt, t, True)
    tn_dims = (((0,), (0,)), ((), ()))
    nt_dims = (((1,), (1,)), ((), ()))

    def body(z_ref, y_ref, h_ref, dy2_ref, hb_ref, bbt_ref, ct_ref, d_ref, wg_ref, bg_ref, p_ref, pr_ref,
             dz_ref, dbbt_ref, dct_ref, dwg_ref, dlb_ref, dd_ref, dbg_ref, bu_scr, g_scr, gcarry_scr):
        first = pl.program_id(1) == 0

        @pl.when(first)
        def _():
            gcarry_scr[...] = jnp.zeros_like(gcarry_scr)

        u = z_ref[...]
        hin = hb_ref[...]
        y = y_ref[...]
        yg = _gelu(y)
        gate = _sigmoid(jnp.dot(yg.astype(_MXU), wg_ref[...].astype(_MXU), preferred_element_type=_F32) + bg_ref[...])
        dy2 = dy2_ref[...]
        dpre = dy2 * yg * gate * (1.0 - gate)
        _acc(dbg_ref, first, _colsum(dpre))
        dpx = dpre.astype(_MXU)
        _acc(dwg_ref, first, lax.dot_general(yg.astype(_MXU), dpx, tn_dims, preferred_element_type=_F32))
        dyg = dy2 * gate + lax.dot_general(dpx, wg_ref[...].astype(_MXU), nt_dims, preferred_element_type=_F32)
        dy = dyg * _gelu_grad(y)
        _acc(dd_ref, first, _colsum(dy * u))
        dyx = dy.astype(_MXU)
        hx = h_ref[...]
        h = hx.astype(_F32)
        _acc(dct_ref, first, lax.dot_general(hx, dyx, tn_dims, preferred_element_type=_F32))
        bu_scr[...] = lax.dot_general(dyx, ct_ref[...].astype(_MXU), nt_dims, preferred_element_type=_F32)
        gin = (gcarry_scr[0:1, 0:ns], gcarry_scr[0:1, ns:2 * ns])
        ptab = (pr_ref[:, 0:ns], pr_ref[:, ns:2 * ns])
        gr, gi = _scan_rows(bu_scr, g_scr, t // SUBLANES, _scan_consts(p_ref, True), ptab, gin, True)
        gcarry_scr[:, 0:ns] = jnp.broadcast_to(gr, (SUBLANES, ns))
        gcarry_scr[:, ns:2 * ns] = jnp.broadcast_to(gi, (SUBLANES, ns))
        g = g_scr[...]
        row = lax.broadcasted_iota(jnp.int32, (t, ns), 0)
        hp_re = jnp.where(row == 0, hin[0:1, 0:ns], pltpu.roll(h[:, 0:ns], 1, 0))
        hp_im = jnp.where(row == 0, hin[0:1, ns:2 * ns], pltpu.roll(h[:, ns:2 * ns], 1, 0))
        g_re, g_im = g[:, 0:ns], g[:, ns:2 * ns]
        d_ar = _colsum(g_re * hp_re + g_im * hp_im)
        d_ai = _colsum(g_im * hp_re - g_re * hp_im)
        _acc(dlb_ref, first, jnp.concatenate([d_ar, d_ai], axis=1))
        gx = g.astype(_MXU)
        _acc(dbbt_ref, first, lax.dot_general(u.astype(_MXU), gx, tn_dims, preferred_element_type=_F32))
        dz_ref[...] = (dy * d_ref[...] + lax.dot_general(gx, bbt_ref[...].astype(_MXU), nt_dims,
                                                         preferred_element_type=_F32)).astype(dz_ref.dtype)

    f = lambda shape: jax.ShapeDtypeStruct(shape, _F32)
    return pl.pallas_call(
        body, name="ssm_bwd", grid=(nb, nt),
        out_shape=[jax.ShapeDtypeStruct((s, nb * BLOCK_CH), _MXU), f(bbt.shape), f(ct.shape), f(wg.shape), f((nb, 1, 2 * ns)),
                   f((1, nb * BLOCK_CH)), f((1, nb * BLOCK_CH))],
        in_specs=[sp["z"], sp["z"], sp["h"], sp["z"], sp["hb"], sp["bbt"], sp["ct"], sp["vec"], sp["wg"], sp["vec"], sp["p"],
                  sp["p"]],
        out_specs=[sp["z"], sp["bbt"], sp["ct"], sp["wg"], sp["acc_vec"], sp["vec"], sp["vec"]],
        scratch_shapes=[pltpu.VMEM((t, 2 * ns), _F32), pltpu.VMEM((t, 2 * ns), _F32), pltpu.VMEM((SUBLANES, 2 * ns), _F32)],
        compiler_params=_cp("parallel", "arbitrary"),
    )(z, y_pre, h_all, dy2, hb, bbt, ct, dvec, wg, bglu, ptab, ptab_rev)


def _mod_part(c_all, w, b):
    d, ns = w.shape
    tn = _tile(ns, 512)

    def body(c_ref, w_ref, b_ref, o_ref):
        c = c_ref[...]
        ca = (c * _sigmoid(c)).astype(_MXU)
        o_ref[...] = jnp.dot(ca, w_ref[...].astype(_MXU), preferred_element_type=_F32) + b_ref[...]

    return pl.pallas_call(
        body, name="mod_part", grid=(ns // tn,), out_shape=jax.ShapeDtypeStruct((8, ns), _F32),
        in_specs=[pl.BlockSpec((8, d), lambda n: (0, 0)), pl.BlockSpec((d, tn), lambda n: (0, n)),
                  pl.BlockSpec((1, tn), lambda n: (0, n))],
        out_specs=pl.BlockSpec((8, tn), lambda n: (0, n)), compiler_params=_cp("parallel"),
    )(c_all, w, b)


def _adamw_math(w, g, m, v):
    m = ADAM_B1 * m + (1.0 - ADAM_B1) * g
    v = ADAM_B2 * v + (1.0 - ADAM_B2) * (g * g)
    m_hat = m / (1.0 - ADAM_B1 ** ADAM_STEP)
    v_hat = v / (1.0 - ADAM_B2 ** ADAM_STEP)
    delta = -ADAM_LR * (m_hat / (jnp.sqrt(v_hat) + ADAM_EPS) + ADAM_WD * w)
    return delta, m, v


def _adamw(w, g, m, v, name):
    r, c = w.shape
    tc = c if c <= 4096 else _tile(c, 4096)
    tr = _tile(r, max(SUBLANES, (1 << 18) // tc), SUBLANES)

    def body(w_ref, g_ref, m_ref, v_ref, d_ref, mo_ref, vo_ref):
        d_ref[...], mo_ref[...], vo_ref[...] = _adamw_math(w_ref[...], g_ref[...], m_ref[...], v_ref[...])

    spec = pl.BlockSpec((tr, tc), lambda i, j: (i, j))
    out = jax.ShapeDtypeStruct((r, c), _F32)
    return pl.pallas_call(
        body, name=name, grid=(r // tr, c // tc), in_specs=[spec] * 4, out_specs=[spec] * 3, out_shape=[out] * 3,
        compiler_params=_cp("parallel", "parallel"),
    )(w, g, m, v)


def _adamw_halves(w, g2, m, v, name):
    r, c = w.shape
    tr, tc = _tile(r, 256, SUBLANES), _tile(c // 2, 1024)
    nph = (c // 2) // tc

    def body(w_ref, g_ref, m_ref, v_ref, go_ref, d_ref, mo_ref, vo_ref):
        g = g_ref[...]
        go_ref[...] = g
        d_ref[...], mo_ref[...], vo_ref[...] = _adamw_math(w_ref[...], g, m_ref[...], v_ref[...])

    spec = pl.BlockSpec((tr, tc), lambda i, j: (i, j))
    out = jax.ShapeDtypeStruct((r, c), _F32)
    return pl.pallas_call(
        body, name=name, grid=(r // tr, c // tc),
        in_specs=[spec, pl.BlockSpec((None, tr, tc), lambda i, j: (j // nph, i, j % nph)), spec, spec],
        out_specs=[spec] * 4, out_shape=[out] * 4, compiler_params=_cp("parallel", "parallel"),
    )(w, g2, m, v)


def _wada_update(c_t, dm, w, m, v):
    d, ns = w.shape
    tr, tc = _tile(d, 256, SUBLANES), _tile(ns, 1024)

    def body(c_ref, dm_ref, w_ref, m_ref, v_ref, g_ref, d_ref, mo_ref, vo_ref):
        c = c_ref[...]
        ca = c * _sigmoid(c)
        dmv = dm_ref[...]
        g = ca[:, 0:1] * dmv[0:1, :]
        for b in range(1, 8):
            g = g + ca[:, b:b + 1] * dmv[b:b + 1, :]
        g_ref[...] = g
        d_ref[...], mo_ref[...], vo_ref[...] = _adamw_math(w_ref[...], g, m_ref[...], v_ref[...])

    spec = pl.BlockSpec((tr, tc), lambda i, j: (i, j))
    out = jax.ShapeDtypeStruct((d, ns), _F32)
    return pl.pallas_call(
        body, name="wada_update", grid=(d // tr, ns // tc),
        in_specs=[pl.BlockSpec((tr, 8), lambda i, j: (i, 0)), pl.BlockSpec((8, tc), lambda i, j: (0, j)), spec, spec, spec],
        out_specs=[spec] * 4, out_shape=[out] * 4, compiler_params=_cp("parallel", "parallel"),
    )(c_t, dm, w, m, v)


def _small_reduce_adamw(gathered, w, m, v):
    _, r, c = gathered.shape
    tr = _tile(r, 256, SUBLANES)

    def body(q_ref, w_ref, m_ref, v_ref, g_ref, d_ref, mo_ref, vo_ref):
        g = q_ref[0]
        for k in range(1, 8):
            g = g + q_ref[k]
        g_ref[...] = g
        d_ref[...], mo_ref[...], vo_ref[...] = _adamw_math(w_ref[...], g, m_ref[...], v_ref[...])

    spec = pl.BlockSpec((tr, c), lambda i: (i, 0))
    out = jax.ShapeDtypeStruct((r, c), _F32)
    return pl.pallas_call(
        body, name="small_reduce_adamw", grid=(r // tr,),
        in_specs=[pl.BlockSpec((8, tr, c), lambda i: (0, i, 0)), spec, spec, spec],
        out_specs=[spec] * 4, out_shape=[out] * 4, compiler_params=_cp("parallel"),
    )(gathered, w, m, v)


def _block_diag(x):
    nb, g, p, q = x.shape
    eye = jnp.eye(g, dtype=x.dtype)
    return (x[:, :, :, None, :] * eye[None, :, None, :, None]).reshape(nb, g * p, g * q)


def _block_diag_take(x, p, q):
    nb = x.shape[0]
    g = GROUPS_PER_BLOCK
    eye = jnp.eye(g, dtype=x.dtype)
    return jnp.sum(x.reshape(nb, g, p, g, q) * eye[None, :, None, :, None], axis=3)


class _Pack:
    def __init__(self, shapes):
        self.shapes = shapes
        self.offsets = {}
        off = 0
        for name, shape in shapes.items():
            n = math.prod(shape)
            self.offsets[name] = (off, n)
            off += -(-n // (SUBLANES * LANES)) * (SUBLANES * LANES)
        self.rows = -(-off // (256 * LANES)) * 256

    def pack(self, arrays):
        parts = []
        off = 0
        for name, shape in self.shapes.items():
            start, n = self.offsets[name]
            if start > off:
                parts.append(jnp.zeros((start - off,), _F32))
            parts.append(arrays[name].reshape(-1).astype(_F32))
            off = start + n
        total = self.rows * LANES
        if total > off:
            parts.append(jnp.zeros((total - off,), _F32))
        return jnp.concatenate(parts).reshape(self.rows, LANES)

    def unpack(self, buf):
        flat = buf.reshape(-1)
        return {name: flat[start:start + n].reshape(self.shapes[name]) for name, (start, n) in self.offsets.items()}


_SMALL = ["b_ada", "g_pre_mix", "g_post_mix", "ssm_log_dt", "ssm_a_re", "ssm_a_im", "ssm_b_re", "ssm_b_im", "ssm_c_re",
          "ssm_c_im", "ssm_d", "ssm_w_glu", "ssm_b_glu", "sgu_ln_g", "sgu_ln_b", "sgu_w", "sgu_b", "g_out_ssm",
          "g_out_sgu", "g_pre_ffn", "g_post_ffn", "conv_b"]
_WEIGHTS = ["w_ada", "b_ada", "g_pre_mix", "g_post_mix", "w_in", "ssm_log_dt", "ssm_a_re", "ssm_a_im", "ssm_b_re",
            "ssm_b_im", "ssm_c_re", "ssm_c_im", "ssm_d", "ssm_w_glu", "ssm_b_glu", "sgu_ln_g", "sgu_ln_b", "sgu_w", "sgu_b",
            "g_out_ssm", "g_out_sgu", "w_out", "g_pre_ffn", "g_post_ffn", "w_up", "conv_w", "conv_b", "w_down"]


def _step(p, m, v, x, c, tgt):
    s, d = x.shape
    mx, my, mc = lax.axis_index("x"), lax.axis_index("y"), lax.axis_index("c")
    chip = 2 * mx + my
    dev = 4 * mx + 2 * my + mc
    sel = jnp.stack([chip, mc]).astype(jnp.int32)
    g_cnt, n_st = p["ssm_a_re"].shape
    nb = g_cnt // GROUPS_PER_BLOCK
    gn = g_cnt * n_st
    d_ssm = g_cnt * SSM_GROUP
    nh = p["sgu_w"].shape[0]
    assert nh * CHUNK == d_ssm and 2 * d_ssm == d and n_st == SSM_STATE

    shards = lambda g: g.reshape(4, g.shape[1] * g.shape[2], g.shape[3])
    buf_in, buf_out, buf_up, buf_down = [_cast_into_slot(p[n], sel) for n in ("w_in", "w_out", "w_up", "w_down")]
    (sems_in,), (buf_in,), tok = _gather_start([buf_in], "gather_start_in")

    ns_ada = p["w_ada"].shape[1]
    nc_conv = p["conv_w"].shape[1]
    first = jnp.concatenate([jnp.broadcast_to(_after(c, tok), (8, d)), jnp.pad(p["conv_w"], ((0, 5), (0, 0)))], axis=1)
    first_all = _all_gather8(_own_slot(first, dev), "gather_c_conv")
    c_all = first_all[:, 0, :d]
    conv_w_full = jnp.concatenate([first_all[2 * j, 0:3, d:] for j in range(4)], axis=1)
    b_ada_mine = lax.dynamic_slice_in_dim(p["b_ada"], chip * ns_ada, ns_ada, axis=1)
    mod_all = _all_gather8(_own_slot(_mod_part(c_all, p["w_ada"], b_ada_mine), dev), "gather_mod")
    mod_rows = lax.dynamic_index_in_dim(mod_all, dev, axis=1, keepdims=False)
    mod = jnp.concatenate([mod_rows[0], mod_rows[2], mod_rows[4], mod_rows[6]]).reshape(N_MOD, 1, d)
    sh1, sc1, gt1, sh2, sc2, gt2 = [mod[i] for i in range(N_MOD)]

    ldt_l = jnp.repeat(p["ssm_log_dt"], n_st, axis=1)
    are_l, aim_l = p["ssm_a_re"].reshape(1, gn), p["ssm_a_im"].reshape(1, gn)
    bre_t, bim_t = p["ssm_b_re"].reshape(gn, SSM_GROUP).T, p["ssm_b_im"].reshape(gn, SSM_GROUP).T
    pw_re, pw_im, bb_re, bb_im = _ssm_prep(ldt_l, are_l, aim_l, bre_t, bim_t)
    blocks = lambda t: t.reshape(t.shape[0], nb, GROUPS_PER_BLOCK * n_st).transpose(1, 0, 2)
    ptab = jnp.concatenate([blocks(pw_re), blocks(pw_im)], axis=2)
    ptab_rev = jnp.concatenate([blocks(pw_re)[:, ::-1], -blocks(pw_im)[:, ::-1]], axis=2)
    bd = lambda t: t.reshape(SSM_GROUP, nb, GROUPS_PER_BLOCK, n_st).transpose(1, 2, 0, 3)
    bbt = jnp.concatenate([_block_diag(bd(bb_re)), _block_diag(bd(bb_im))], axis=2).astype(_MXU)
    cd = lambda t: t.reshape(nb, GROUPS_PER_BLOCK, SSM_GROUP, n_st).transpose(0, 1, 3, 2)
    ct = jnp.concatenate([_block_diag(cd(p["ssm_c_re"])), -_block_diag(cd(p["ssm_c_im"]))], axis=1).astype(_MXU)
    wg = _block_diag(p["ssm_w_glu"].reshape(nb, GROUPS_PER_BLOCK, SSM_GROUP, SSM_GROUP)).astype(_MXU)
    dvec = p["ssm_d"]
    bglu = p["ssm_b_glu"].reshape(1, d_ssm)
    mask = jnp.tril(jnp.ones((CHUNK, CHUNK), _F32))
    wm = (p["sgu_w"] * mask[None]).astype(_MXU)
    bs = p["sgu_b"].reshape(nh, CHUNK, 1)

    buf_in = _gather_wait(sems_in, buf_in, mod_all, "gather_wait_in")
    w_in4 = shards(_pair_forward([buf_in], "pair_forward_in")[0])
    (sems_out, sems_up, sems_down), (buf_out, buf_up, buf_down), tok = _gather_start(
        [buf_out, buf_up, buf_down], "gather_start_rest")
    h1 = _fwd_pre_mix(x, p["g_pre_mix"], _after(sc1, tok), sh1)
    z = _mm_nn(h1, w_in4, _F32, "mm_in")
    y_ssm, y_pre, h_all, hb = _ssm_fwd(z, bbt, ct, dvec, wg, bglu, ptab)
    y_sgu = _sgu_fwd(z, p["sgu_ln_g"], p["sgu_ln_b"], wm, bs)
    ycat = _mix_norm_fwd(y_ssm, y_sgu, p["g_out_ssm"], p["g_out_sgu"])
    buf_out = _gather_wait(sems_out, buf_out, ycat, "gather_wait_out")
    w_out_full = _pair_forward([buf_out], "pair_forward_out")[0].reshape(1, d, d)
    o = _mm_nn(ycat, w_out_full, _F32, "mm_out")
    x1, h2 = _fwd_mid(o, x, gt1, p["g_post_mix"], p["g_pre_ffn"], sc2, sh2)
    buf_up = _gather_wait(sems_up, buf_up, h2, "gather_wait_up")
    w_up4 = shards(_pair_forward([buf_up], "pair_forward_up")[0])
    up_pre = _mm_nn(h2, w_up4, _F32, "mm_up")
    act = _conv_act_fwd(up_pre, conv_w_full, p["conv_b"])
    buf_down = _gather_wait(sems_down, buf_down, act, "gather_wait_down")
    w_down_full = _pair_forward([buf_down], "pair_forward_down")[0].reshape(1, -1, d)
    f = _mm_nn(act, w_down_full, _F32, "mm_down", tk=2816)
    dx2, df, d_gt2, d_g_post_ffn, loss = _loss_and_post_ffn_bwd(f, x1, tgt, gt2, p["g_post_ffn"])

    def reduce_start(gw, n):
        got = _pair_swap([gw], "pair_swap_" + n)[0]
        pair = _pair_sum(gw, got, sel, "pair_sum_" + n)
        return _scatter_start(pair, "scatter_start_" + n)

    d_act = _mm_nt(df, w_down_full, _F32, "mm_d_act")
    gw_down = _mm_tn_rows(act, df, "mm_gw_down")
    red_down = reduce_start(gw_down, "w_down")
    d_up_pre, d_cw0, d_cw1, d_cw2, d_conv_b = _conv_act_bwd(up_pre, d_act, conv_w_full, _after(p["conv_b"], red_down[3]))
    dh2 = _mm_nt(d_up_pre, w_up4, _F32, "mm_dh2")
    gw_up = _mm_tn_cols(h2, d_up_pre, "mm_gw_up")
    red_up = reduce_start(gw_up, "w_up")
    dx1, d_o, d_sc2, d_sh2, d_g_pre_ffn, d_gt1, d_g_post_mix = _bwd_mid(
        dh2, x1, dx2, o, p["g_pre_ffn"], _after(sc2, red_up[3]), gt1, p["g_post_mix"])
    d_ycat = _mm_nt(d_o, w_out_full, _F32, "mm_d_ycat")
    gw_out = _mm_tn_rows(ycat, d_o, "mm_gw_out")
    red_out = reduce_start(gw_out, "w_out")
    dy_ssm, dy_sgu, d_g_out_ssm, d_g_out_sgu = _mix_norm_bwd(d_ycat, y_ssm, y_sgu, _after(p["g_out_ssm"], red_out[3]),
                                                              p["g_out_sgu"])
    dz_ssm, d_bbt, d_ct, d_wg, d_lb, d_ssm_d, d_bglu = _ssm_bwd(z, y_pre, h_all, dy_ssm, hb, bbt, ct, dvec, wg, bglu, ptab,
                                                                ptab_rev)
    dz, d_ln_g, d_ln_b, d_wm, d_bs = _sgu_bwd(z, dy_sgu, dz_ssm, p["sgu_ln_g"], p["sgu_ln_b"], wm, bs)
    dh1 = _mm_nt(dz, w_in4, _F32, "mm_dh1")
    gw_in = _mm_tn_cols(h1, dz, "mm_gw_in")
    red_in = reduce_start(gw_in, "w_in")
    dx, d_sc1, d_sh1, d_g_pre_mix = _bwd_pre_mix(dh1, x, dx1, p["g_pre_mix"], _after(sc1, red_in[3]))

    nsb = BLOCK_ST
    lanes = lambda t: t.transpose(2, 0, 1, 3).reshape(SSM_GROUP, gn)
    d_bbr = lanes(_block_diag_take(d_bbt[:, :, :nsb], SSM_GROUP, n_st))
    d_bbi = lanes(_block_diag_take(d_bbt[:, :, nsb:], SSM_GROUP, n_st))
    d_lr, d_li = d_lb[:, 0, :nsb].reshape(1, gn), d_lb[:, 0, nsb:].reshape(1, gn)
    d_bre_t, d_bim_t, d_are, d_aim, d_dt = _ssm_prep_bwd(ldt_l, are_l, aim_l, bre_t, bim_t, d_bbr, d_bbi, d_lr, d_li)
    d_log_dt = _group_sum(d_dt.reshape(g_cnt, n_st), p["ssm_log_dt"].reshape(g_cnt, 1))
    c_grad = lambda t: _block_diag_take(t, n_st, SSM_GROUP).transpose(0, 1, 3, 2).reshape(g_cnt, SSM_GROUP, n_st)
    small = {
        "b_ada": jnp.concatenate([d_sh1, d_sc1, d_gt1, d_sh2, d_sc2, d_gt2], axis=1),
        "g_pre_mix": d_g_pre_mix, "g_post_mix": d_g_post_mix,
        "ssm_log_dt": d_log_dt, "ssm_a_re": d_are, "ssm_a_im": d_aim,
        "ssm_b_re": d_bre_t.T, "ssm_b_im": d_bim_t.T,
        "ssm_c_re": c_grad(d_ct[:, :nsb, :]), "ssm_c_im": -c_grad(d_ct[:, nsb:, :]),
        "ssm_d": d_ssm_d, "ssm_w_glu": _block_diag_take(d_wg, SSM_GROUP, SSM_GROUP), "ssm_b_glu": d_bglu,
        "sgu_ln_g": d_ln_g, "sgu_ln_b": d_ln_b, "sgu_w": d_wm * mask[None], "sgu_b": d_bs,
        "g_out_ssm": d_g_out_ssm, "g_out_sgu": d_g_out_sgu, "g_pre_ffn": d_g_pre_ffn, "g_post_ffn": d_g_post_ffn,
        "conv_b": d_conv_b, "conv_w_all": jnp.concatenate([d_cw0, d_cw1, d_cw2], axis=0),
    }
    shapes = {name: p[name].shape for name in _SMALL}
    shapes["conv_w_all"] = (3, 4 * nc_conv)
    pk = _Pack(shapes)
    zeros_cw = jnp.zeros(shapes["conv_w_all"], _F32)
    sems_small, small_buf, tok = _gather8_start(_own_slot(pk.pack(small), dev), "gather_small_start")

    big = ["w_down", "w_up", "w_out", "w_in"]
    mine = []
    after = tok
    for n, (sems, pair, land, _) in zip(big, (red_down, red_up, red_out, red_in)):
        pair, land = _scatter_wait(sems, pair, land, after, "scatter_wait_" + n)
        mine.append(_chip_sum(pair, land, sel, "chip_sum_" + n))
        after = mine[-1]
    joined = _pair_join(mine)
    big_out = {}
    for n, j in zip(big, joined):
        if n in ("w_in", "w_up"):
            g_n = j.reshape(p[n].shape)
            big_out[n] = (g_n,) + tuple(_adamw(p[n], g_n, m[n], v[n], "adamw_" + n))
        else:
            big_out[n] = tuple(_adamw_halves(p[n], j, m[n], v[n], "adamw_" + n))

    gathered = _gather8_forward(_gather8_wait(sems_small, small_buf, big_out["w_in"][1], "gather_small_wait"),
                                "gather_small_forward")
    g_pk, d_pk, m_pk, v_pk = _small_reduce_adamw(
        gathered, pk.pack({**{n: p[n] for n in _SMALL}, "conv_w_all": zeros_cw}),
        pk.pack({**{n: m[n] for n in _SMALL}, "conv_w_all": zeros_cw}),
        pk.pack({**{n: v[n] for n in _SMALL}, "conv_w_all": zeros_cw}))
    grads, deltas, new_m, new_v = pk.unpack(g_pk), pk.unpack(d_pk), pk.unpack(m_pk), pk.unpack(v_pk)

    grads["conv_w"] = lax.dynamic_slice_in_dim(grads.pop("conv_w_all"), chip * nc_conv, nc_conv, axis=1)
    deltas["conv_w"], new_m["conv_w"], new_v["conv_w"] = _adamw(p["conv_w"], grads["conv_w"], m["conv_w"], v["conv_w"],
                                                                 "adamw_conv_w")
    d_mod_all = gathered.reshape(8, -1)[:, :N_MOD * d]
    d_mod_mine = lax.dynamic_slice_in_dim(d_mod_all, chip * ns_ada, ns_ada, axis=1)
    grads["w_ada"], deltas["w_ada"], new_m["w_ada"], new_v["w_ada"] = _wada_update(
        c_all.T, d_mod_mine, p["w_ada"], m["w_ada"], v["w_ada"])
    for n in big:
        grads[n], deltas[n], new_m[n], new_v[n] = big_out[n]
    return loss[0, 0], dx, grads, deltas, new_m, new_v


def kernel(x, c, w_ada, b_ada, g_pre_mix, g_post_mix, w_in, ssm_log_dt, ssm_a_re, ssm_a_im, ssm_b_re, ssm_b_im, ssm_c_re, ssm_c_im, ssm_d, ssm_w_glu, ssm_b_glu, sgu_ln_g, sgu_ln_b, sgu_w, sgu_b, g_out_ssm, g_out_sgu, w_out, g_pre_ffn, g_post_ffn, w_up, conv_w, conv_b, w_down, loss_target, m_w_ada, m_b_ada, m_g_pre_mix, m_g_post_mix, m_w_in, m_ssm_log_dt, m_ssm_a_re, m_ssm_a_im, m_ssm_b_re, m_ssm_b_im, m_ssm_c_re, m_ssm_c_im, m_ssm_d, m_ssm_w_glu, m_ssm_b_glu, m_sgu_ln_g, m_sgu_ln_b, m_sgu_w, m_sgu_b, m_g_out_ssm, m_g_out_sgu, m_w_out, m_g_pre_ffn, m_g_post_ffn, m_w_up, m_conv_w, m_conv_b, m_w_down, v_w_ada, v_b_ada, v_g_pre_mix, v_g_post_mix, v_w_in, v_ssm_log_dt, v_ssm_a_re, v_ssm_a_im, v_ssm_b_re, v_ssm_b_im, v_ssm_c_re, v_ssm_c_im, v_ssm_d, v_ssm_w_glu, v_ssm_b_glu, v_sgu_ln_g, v_sgu_ln_b, v_sgu_w, v_sgu_b, v_g_out_ssm, v_g_out_sgu, v_w_out, v_g_pre_ffn, v_g_post_ffn, v_w_up, v_conv_w, v_conv_b, v_w_down):
    given = dict(locals())
    drop = lambda a: a if a.ndim == 2 else a[0]
    p = {n: drop(given[n]) for n in _WEIGHTS}
    m = {n: drop(given["m_" + n]) for n in _WEIGHTS}
    v = {n: drop(given["v_" + n]) for n in _WEIGHTS}
    loss, dx, grads, deltas, new_m, new_v = _step(p, m, v, x[0], c, loss_target[0])
    loss = lax.psum(loss, ("x", "y", "c"))
    outs = [loss, dx[None]]
    for group in (grads, deltas, new_m, new_v):
        outs += [group[n].reshape(given[n].shape) for n in _WEIGHTS]
    return tuple(outs)
```

```python
import functools
import math

import jax
import jax.numpy as jnp
from jax import lax
from jax.experimental import pallas as pl
from jax.experimental.pallas import tpu as pltpu

_F32 = jnp.float32
_MXU = jnp.bfloat16
_WIRE = jnp.bfloat16

EPS = 1e-6
SSM_GROUP = 16
SSM_STATE = 64
GROUPS_PER_BLOCK = 8
BLOCK_CH = SSM_GROUP * GROUPS_PER_BLOCK
BLOCK_ST = SSM_STATE * GROUPS_PER_BLOCK
CHUNK = 128
TIME_TILE = 512
SUBLANES = 8
LANES = 128
N_MOD = 6
ADAM_LR, ADAM_B1, ADAM_B2, ADAM_EPS, ADAM_WD, ADAM_STEP = 0.001, 0.9, 0.999, 1e-08, 0.01, 10
_VMEM_LIMIT = 56 * 1024 * 1024
_MESH = pl.DeviceIdType.MESH
_ANY = pl.BlockSpec(memory_space=pl.ANY)
_HBM = pl.BlockSpec(memory_space=pltpu.HBM)
_SEM = pl.BlockSpec(memory_space=pltpu.SEMAPHORE)
_VMEM_WHOLE = pl.BlockSpec(memory_space=pltpu.VMEM)
_EFFECT = pltpu.SideEffectType.DATAFLOW_SIDE_EFFECTING
_GELU_C = math.sqrt(2.0 / math.pi)


def _cp(*sem):
    return pltpu.CompilerParams(dimension_semantics=sem, vmem_limit_bytes=_VMEM_LIMIT)


def _tile(dim, target, align=LANES):
    if dim <= target:
        return dim
    best = None
    for t in range(align, target + 1, align):
        if dim % t == 0:
            best = t
    assert best is not None, (dim, target, align)
    return best


def _gelu(x):
    return 0.5 * x * (1.0 + jnp.tanh(_GELU_C * (x + 0.044715 * (x * x * x))))


def _gelu_grad(x):
    t = jnp.tanh(_GELU_C * (x + 0.044715 * (x * x * x)))
    return 0.5 * (1.0 + t) + 0.5 * x * (1.0 - t * t) * (_GELU_C * (1.0 + 3.0 * 0.044715 * x * x))


def _sigmoid(x):
    return 1.0 / (1.0 + jnp.exp(-x))


def _colsum(x):
    return jnp.sum(x, axis=0, keepdims=True)


def _rowmean(x):
    return jnp.mean(x, axis=-1, keepdims=True)


def _acc(ref, first, val):
    @pl.when(first)
    def _():
        ref[...] = val

    @pl.when(jnp.logical_not(first))
    def _():
        ref[...] += val


def _place():
    mx, my, mc = lax.axis_index("x"), lax.axis_index("y"), lax.axis_index("c")
    chips = [(1 - mx, my), (mx, 1 - my), (1 - mx, 1 - my)]
    return mx, my, mc, chips


def _all_gather8(buf, name):
    def body(in_ref, out_ref, send_sems, recv_sems):
        mx, my, mc, chips = _place()
        me, sibling = (mx, my, mc), (mx, my, 1 - mc)

        def slot(ref, px, py, pc):
            return ref.at[4 * px + 2 * py + pc]

        def copy(k, block, to, src_ref=out_ref):
            return pltpu.make_async_remote_copy(
                src_ref=slot(src_ref, *block), dst_ref=slot(out_ref, *block),
                send_sem=send_sems.at[k], recv_sem=recv_sems.at[k], device_id=to, device_id_type=_MESH)

        first = [copy(0, me, sibling, in_ref)]
        first += [copy(1 + j, me, (*chip, mc), in_ref) for j, chip in enumerate(chips)]
        for cp in first:
            cp.start()
        passed = [copy(4 + j, (*chip, mc), sibling) for j, chip in enumerate(chips)]
        for j, chip in enumerate(chips):
            copy(1 + j, (*chip, mc), me).wait_recv()
            passed[j].start()
        copy(0, sibling, me).wait_recv()
        for j, chip in enumerate(chips):
            copy(4 + j, (*chip, 1 - mc), me).wait_recv()
        for cp in first + passed:
            cp.wait_send()

    return pl.pallas_call(
        body, name=name, out_shape=jax.ShapeDtypeStruct(buf.shape, buf.dtype),
        in_specs=[_ANY], out_specs=_ANY, input_output_aliases={0: 0},
        scratch_shapes=[pltpu.SemaphoreType.DMA((7,)), pltpu.SemaphoreType.DMA((7,))],
    )(buf)


def _own_slot(x, dev):
    return lax.dynamic_update_slice(jnp.zeros((8,) + x.shape, x.dtype), x[None], (dev, 0, 0))


def _cast_into_slot(w, sel):
    r, c = w.shape
    hr = r // 2
    tr = _tile(hr, 256, 16)
    nr = hr // tr

    def body(sel_ref, w_ref, o_ref):
        o_ref[...] = w_ref[...].astype(o_ref.dtype)

    return pl.pallas_call(
        body, name="cast_into_slot", out_shape=jax.ShapeDtypeStruct((4, 2, hr, c), _WIRE),
        grid_spec=pltpu.PrefetchScalarGridSpec(
            num_scalar_prefetch=1, grid=(2, nr),
            in_specs=[pl.BlockSpec((tr, c), lambda h, i, s: (h * nr + i, 0))],
            out_specs=pl.BlockSpec((None, None, tr, c), lambda h, i, s: (s[0], h, i, 0))),
        compiler_params=_cp("parallel", "parallel"),
    )(sel, w)


def _hbm(a):
    return pltpu.with_memory_space_constraint(a, pltpu.HBM)


def _after(vec, token):
    return vec + token[0:1, 0:1]


def _gather_start(bufs, after, name):
    n = len(bufs)
    nc = 3 * n

    def body(*refs):
        ins, send, recv, token = refs[:n], refs[n + 1:n + 1 + nc], refs[n + 1 + nc:n + 1 + 2 * nc], refs[-1]
        mx, my, mc, chips = _place()
        j_me = 2 * mx + my
        for i in range(n):
            for k, chip in enumerate(chips):
                half = ins[i].at[j_me, mc]
                pltpu.make_async_remote_copy(
                    src_ref=half, dst_ref=half, send_sem=send[3 * i + k], recv_sem=recv[3 * i + k],
                    device_id=(*chip, mc), device_id_type=_MESH).start()
        token[...] = jnp.zeros_like(token)

    outs = pl.pallas_call(
        body, name=name,
        out_shape=tuple([pltpu.SemaphoreType.DMA(())] * (2 * nc) + [pltpu.HBM(b.shape, b.dtype) for b in bufs]
                        + [jax.ShapeDtypeStruct((SUBLANES, LANES), _F32)]),
        in_specs=tuple([_HBM] * n + [_ANY]), out_specs=tuple([_SEM] * (2 * nc) + [_HBM] * n + [_VMEM_WHOLE]),
        input_output_aliases={i: 2 * nc + i for i in range(n)},
        compiler_params=pltpu.CompilerParams(has_side_effects=_EFFECT),
    )(*[_hbm(b) for b in bufs], after)
    sems = [(outs[3 * i:3 * i + 3], outs[nc + 3 * i:nc + 3 * i + 3]) for i in range(n)]
    return sems, list(outs[2 * nc:2 * nc + n]), outs[-1]


def _gather_wait(sems, buf, after, name):
    send, recv = sems

    def body(buf_ref, s0, s1, s2, r0, r1, r2, after_ref, out_ref):
        mx, my, mc, chips = _place()
        j_me = 2 * mx + my
        for k, (chip, s_k, r_k) in enumerate(zip(chips, (s0, s1, s2), (r0, r1, r2))):
            cp = pltpu.make_async_remote_copy(
                src_ref=buf_ref.at[j_me, mc], dst_ref=buf_ref.at[2 * chip[0] + chip[1], mc], send_sem=s_k, recv_sem=r_k,
                device_id=(*chip, mc), device_id_type=_MESH)
            cp.wait_send()
            cp.wait_recv()

    return pl.pallas_call(
        body, name=name, out_shape=pltpu.HBM(buf.shape, buf.dtype),
        in_specs=(_HBM,) + (_SEM,) * 6 + (_ANY,), out_specs=_HBM, input_output_aliases={0: 0},
        compiler_params=pltpu.CompilerParams(has_side_effects=_EFFECT),
    )(buf, *send, *recv, after)


def _pair_forward(bufs, name):
    n = len(bufs)

    def body(*refs):
        ins, outs = refs[:n], refs[n:2 * n]
        send_sems, recv_sems = refs[2 * n:]
        mx, my, mc, chips = _place()
        sibling = (mx, my, 1 - mc)
        cps = []
        for i in range(n):
            for k, chip in enumerate(chips):
                j_k = 2 * chip[0] + chip[1]
                cp = pltpu.make_async_remote_copy(
                    src_ref=ins[i].at[j_k, mc], dst_ref=outs[i].at[j_k, mc], send_sem=send_sems.at[3 * i + k],
                    recv_sem=recv_sems.at[3 * i + k], device_id=sibling, device_id_type=_MESH)
                cp.start()
                cps.append(cp)
        for i in range(n):
            for k, chip in enumerate(chips):
                other = outs[i].at[2 * chip[0] + chip[1], 1 - mc]
                pltpu.make_async_remote_copy(
                    src_ref=other, dst_ref=other, send_sem=send_sems.at[3 * i + k], recv_sem=recv_sems.at[3 * i + k],
                    device_id=sibling, device_id_type=_MESH).wait_recv()
        for cp in cps:
            cp.wait_send()

    return pl.pallas_call(
        body, name=name, out_shape=[jax.ShapeDtypeStruct(b.shape, b.dtype) for b in bufs],
        in_specs=[_ANY] * n, out_specs=[_ANY] * n, input_output_aliases={i: i for i in range(n)},
        scratch_shapes=[pltpu.SemaphoreType.DMA((3 * n,)), pltpu.SemaphoreType.DMA((3 * n,))],
    )(*bufs)


def _gather8_peers(buf_ref, mx, my, mc, chips):
    mine = buf_ref.at[4 * mx + 2 * my + mc]
    peers = [((mx, my, 1 - mc), mine, buf_ref.at[4 * mx + 2 * my + 1 - mc])]
    peers += [((*chip, mc), mine, buf_ref.at[4 * chip[0] + 2 * chip[1] + mc]) for chip in chips]
    return peers


def _gather8_start(buf, name):
    def body(buf_ref, *rest):
        send, recv, token = rest[0:4], rest[4:8], rest[-1]
        mx, my, mc, chips = _place()
        for k, (peer, src, _) in enumerate(_gather8_peers(buf_ref, mx, my, mc, chips)):
            pltpu.make_async_remote_copy(src_ref=src, dst_ref=src, send_sem=send[k], recv_sem=recv[k],
                                         device_id=peer, device_id_type=_MESH).start()
        token[...] = jnp.zeros_like(token)

    outs = pl.pallas_call(
        body, name=name,
        out_shape=tuple([pltpu.SemaphoreType.DMA(())] * 8 + [pltpu.HBM(buf.shape, buf.dtype),
                                                             jax.ShapeDtypeStruct((SUBLANES, LANES), _F32)]),
        in_specs=(_HBM,), out_specs=tuple([_SEM] * 8 + [_HBM, _VMEM_WHOLE]), input_output_aliases={0: 8},
        compiler_params=pltpu.CompilerParams(has_side_effects=_EFFECT),
    )(_hbm(buf))
    return (outs[0:4], outs[4:8]), outs[8], outs[9]


def _gather8_wait(sems, buf, after, name):
    send, recv = sems

    def body(buf_ref, s0, s1, s2, s3, r0, r1, r2, r3, after_ref, out_ref):
        mx, my, mc, chips = _place()
        for (peer, src, dst), s_k, r_k in zip(_gather8_peers(buf_ref, mx, my, mc, chips), (s0, s1, s2, s3), (r0, r1, r2, r3)):
            cp = pltpu.make_async_remote_copy(src_ref=src, dst_ref=dst, send_sem=s_k, recv_sem=r_k,
                                              device_id=peer, device_id_type=_MESH)
            cp.wait_send()
            cp.wait_recv()

    return pl.pallas_call(
        body, name=name, out_shape=pltpu.HBM(buf.shape, buf.dtype),
        in_specs=(_HBM,) + (_SEM,) * 8 + (_ANY,), out_specs=_HBM, input_output_aliases={0: 0},
        compiler_params=pltpu.CompilerParams(has_side_effects=_EFFECT),
    )(buf, *send, *recv, after)


def _gather8_forward(buf, name):
    def body(in_ref, out_ref, send_sems, recv_sems):
        mx, my, mc, chips = _place()
        sibling = (mx, my, 1 - mc)
        cps = []
        for k, chip in enumerate(chips):
            idx = 4 * chip[0] + 2 * chip[1] + mc
            cp = pltpu.make_async_remote_copy(src_ref=in_ref.at[idx], dst_ref=out_ref.at[idx], send_sem=send_sems.at[k],
                                              recv_sem=recv_sems.at[k], device_id=sibling, device_id_type=_MESH)
            cp.start()
            cps.append(cp)
        for k, chip in enumerate(chips):
            other = out_ref.at[4 * chip[0] + 2 * chip[1] + 1 - mc]
            pltpu.make_async_remote_copy(src_ref=other, dst_ref=other, send_sem=send_sems.at[k], recv_sem=recv_sems.at[k],
                                         device_id=sibling, device_id_type=_MESH).wait_recv()
        for cp in cps:
            cp.wait_send()

    return pl.pallas_call(
        body, name=name, out_shape=jax.ShapeDtypeStruct(buf.shape, buf.dtype),
        in_specs=[_ANY], out_specs=_ANY, input_output_aliases={0: 0},
        scratch_shapes=[pltpu.SemaphoreType.DMA((3,)), pltpu.SemaphoreType.DMA((3,))],
    )(buf)


def _scatter_start(pair, name):
    land = lax.empty((3,) + pair.shape[1:], pair.dtype)

    def body(pair_ref, land_ref, s0, s1, s2, r0, r1, r2, pair_thru, land_thru, token):
        mx, my, mc, chips = _place()
        for k, (chip, s_k, r_k) in enumerate(zip(chips, (s0, s1, s2), (r0, r1, r2))):
            pltpu.make_async_remote_copy(
                src_ref=pair_ref.at[2 * chip[0] + chip[1]], dst_ref=land_ref.at[k], send_sem=s_k, recv_sem=r_k,
                device_id=(*chip, mc), device_id_type=_MESH).start()
        token[...] = jnp.zeros_like(token)

    outs = pl.pallas_call(
        body, name=name,
        out_shape=tuple([pltpu.SemaphoreType.DMA(())] * 6 + [pltpu.HBM(pair.shape, pair.dtype), pltpu.HBM(land.shape, land.dtype),
                                                             jax.ShapeDtypeStruct((SUBLANES, LANES), _F32)]),
        in_specs=(_HBM, _HBM), out_specs=tuple([_SEM] * 6 + [_HBM, _HBM, _VMEM_WHOLE]),
        input_output_aliases={0: 6, 1: 7}, compiler_params=pltpu.CompilerParams(has_side_effects=_EFFECT),
    )(_hbm(pair), _hbm(land))
    return (outs[0:3], outs[3:6]), outs[6], outs[7], outs[8]


def _scatter_wait(sems, pair, land, after, name):
    send, recv = sems

    def body(pair_ref, land_ref, s0, s1, s2, r0, r1, r2, after_ref, pair_out, land_out):
        mx, my, mc, chips = _place()
        for k, (chip, s_k, r_k) in enumerate(zip(chips, (s0, s1, s2), (r0, r1, r2))):
            cp = pltpu.make_async_remote_copy(
                src_ref=pair_ref.at[2 * chip[0] + chip[1]], dst_ref=land_ref.at[k], send_sem=s_k, recv_sem=r_k,
                device_id=(*chip, mc), device_id_type=_MESH)
            cp.wait_send()
            cp.wait_recv()

    return pl.pallas_call(
        body, name=name, out_shape=(pltpu.HBM(pair.shape, pair.dtype), pltpu.HBM(land.shape, land.dtype)),
        in_specs=(_HBM, _HBM) + (_SEM,) * 6 + (_ANY,), out_specs=(_HBM, _HBM), input_output_aliases={0: 0, 1: 1},
        compiler_params=pltpu.CompilerParams(has_side_effects=_EFFECT),
    )(pair, land, *send, *recv, after)


def _pair_swap(arrs, name):
    n = len(arrs)

    def body(*refs):
        ins, outs = refs[:n], refs[n:2 * n]
        send_sems, recv_sems = refs[2 * n:]
        mx, my, mc, _ = _place()
        sibling = (mx, my, 1 - mc)
        cps = []
        for i in range(n):
            cp = pltpu.make_async_remote_copy(
                src_ref=ins[i].at[1 - mc], dst_ref=outs[i], send_sem=send_sems.at[i], recv_sem=recv_sems.at[i],
                device_id=sibling, device_id_type=_MESH)
            cp.start()
            cps.append(cp)
        for cp in cps:
            cp.wait()

    return pl.pallas_call(
        body, name=name, out_shape=[jax.ShapeDtypeStruct(a.shape[1:], a.dtype) for a in arrs],
        in_specs=[_ANY] * n, out_specs=[_ANY] * n,
        scratch_shapes=[pltpu.SemaphoreType.DMA((n,)), pltpu.SemaphoreType.DMA((n,))],
    )(*arrs)


def _pair_join(bufs):
    n = len(bufs)

    def body(*refs):
        ins, outs = refs[:n], refs[n:2 * n]
        send_sems, recv_sems = refs[2 * n:]
        mx, my, mc, _ = _place()
        sibling = (mx, my, 1 - mc)
        cps = []
        for i in range(n):
            cp = pltpu.make_async_remote_copy(
                src_ref=ins[i].at[mc], dst_ref=outs[i].at[mc], send_sem=send_sems.at[i], recv_sem=recv_sems.at[i],
                device_id=sibling, device_id_type=_MESH)
            cp.start()
            cps.append(cp)
        for i in range(n):
            other = outs[i].at[1 - mc]
            pltpu.make_async_remote_copy(
                src_ref=other, dst_ref=other, send_sem=send_sems.at[i], recv_sem=recv_sems.at[i],
                device_id=sibling, device_id_type=_MESH).wait_recv()
        for cp in cps:
            cp.wait_send()

    return pl.pallas_call(
        body, name="pair_join", out_shape=[jax.ShapeDtypeStruct(b.shape, b.dtype) for b in bufs],
        in_specs=[_ANY] * n, out_specs=[_ANY] * n, input_output_aliases={i: i for i in range(n)},
        scratch_shapes=[pltpu.SemaphoreType.DMA((n,)), pltpu.SemaphoreType.DMA((n,))],
    )(*bufs)


def _pair_sum(g, got, sel, name):
    _, four, hr, c = g.shape
    tr = _tile(hr, 512, 16)

    def body(sel_ref, g_ref, p_ref, o_ref):
        o_ref[...] = (g_ref[...].astype(_F32) + p_ref[...].astype(_F32)).astype(o_ref.dtype)

    return pl.pallas_call(
        body, name=name, out_shape=jax.ShapeDtypeStruct((four, hr, c), g.dtype),
        grid_spec=pltpu.PrefetchScalarGridSpec(
            num_scalar_prefetch=1, grid=(four, hr // tr),
            in_specs=[pl.BlockSpec((None, None, tr, c), lambda j, i, s: (s[1], j, i, 0)),
                      pl.BlockSpec((None, tr, c), lambda j, i, s: (j, i, 0))],
            out_specs=pl.BlockSpec((None, tr, c), lambda j, i, s: (j, i, 0))),
        compiler_params=_cp("parallel", "parallel"),
    )(sel, g, got)


def _chip_sum(pair, got, sel, name):
    _, hr, c = pair.shape
    tr = _tile(hr, 512, 16)

    def body(sel_ref, p_ref, q_ref, o_ref):
        o_ref[...] = ((p_ref[...].astype(_F32) + q_ref[0].astype(_F32)) + q_ref[1].astype(_F32)) + q_ref[2].astype(_F32)

    return pl.pallas_call(
        body, name=name, out_shape=jax.ShapeDtypeStruct((2, hr, c), _F32),
        grid_spec=pltpu.PrefetchScalarGridSpec(
            num_scalar_prefetch=1, grid=(hr // tr,),
            in_specs=[pl.BlockSpec((None, tr, c), lambda i, s: (s[0], i, 0)),
                      pl.BlockSpec((3, tr, c), lambda i, s: (0, i, 0))],
            out_specs=pl.BlockSpec((None, tr, c), lambda i, s: (s[1], i, 0))),
        compiler_params=_cp("parallel"),
    )(sel, pair, got)


def _matmul(a, b, dims, out_struct, grid, a_spec, b_spec, o_spec, acc_shape, k_axis, name):
    nk = grid[k_axis]

    def body(a_ref, b_ref, o_ref, *acc):
        prod = lax.dot_general(a_ref[...].astype(_MXU), b_ref[...].astype(_MXU), dims, preferred_element_type=_F32)
        if nk == 1:
            o_ref[...] = prod.astype(o_ref.dtype)
        else:
            acc_ref, = acc
            k = pl.program_id(k_axis)

            @pl.when(k == 0)
            def _():
                acc_ref[...] = prod

            @pl.when(jnp.logical_and(k > 0, k < nk - 1))
            def _():
                acc_ref[...] += prod

            @pl.when(k == nk - 1)
            def _():
                o_ref[...] = (acc_ref[...] + prod).astype(o_ref.dtype)

    sem = ["parallel"] * len(grid)
    sem[k_axis] = "arbitrary"
    return pl.pallas_call(
        body, name=name, out_shape=out_struct, grid=grid, in_specs=[a_spec, b_spec], out_specs=o_spec,
        scratch_shapes=[pltpu.VMEM(acc_shape, _F32)] if nk > 1 else [], compiler_params=_cp(*sem),
    )(a, b)


def _mm_nn(a, w4, out_dtype, name, tm=512, tn=1536, tk=2048):
    m, k = a.shape
    j, _, ns = w4.shape
    tm, tn, tk = _tile(m, tm, 16), _tile(ns, tn), _tile(k, tk)
    nps = ns // tn
    return _matmul(
        a, w4, (((1,), (0,)), ((), ())), jax.ShapeDtypeStruct((m, j * ns), out_dtype),
        (m // tm, j * nps, k // tk),
        pl.BlockSpec((tm, tk), lambda mi, ni, ki: (mi, ki)),
        pl.BlockSpec((None, tk, tn), lambda mi, ni, ki: (ni // nps, ki, ni % nps)),
        pl.BlockSpec((tm, tn), lambda mi, ni, ki: (mi, ni)), (tm, tn), 2, name)


def _mm_nt(a, w4, out_dtype, name, tm=512, tn=2048, tk=1536):
    m = a.shape[-2]
    j, kw, ns = w4.shape
    tm, tn, tk = _tile(m, tm, 16), _tile(kw, tn), _tile(ns, tk)
    kps = ns // tk
    if a.ndim == 3:
        kph = a.shape[2] // tk
        a_spec = pl.BlockSpec((None, tm, tk), lambda mi, ni, ki: (ki // kph, mi, ki % kph))
    else:
        a_spec = pl.BlockSpec((tm, tk), lambda mi, ni, ki: (mi, ki))
    return _matmul(
        a, w4, (((1,), (1,)), ((), ())), jax.ShapeDtypeStruct((m, kw), out_dtype),
        (m // tm, kw // tn, j * kps),
        a_spec,
        pl.BlockSpec((None, tn, tk), lambda mi, ni, ki: (ki // kps, ni, ki % kps)),
        pl.BlockSpec((tm, tn), lambda mi, ni, ki: (mi, ni)), (tm, tn), 2, name)


def _mm_tn_cols(a, b, name, tm=1024, tn=1536, tk=2048):
    m, ka = a.shape
    ns = (b.shape[-1] * (2 if b.ndim == 3 else 1)) // 4
    hr = ka // 2
    tm, tn, tk = _tile(hr, tm), _tile(ns, tn), _tile(m, tk, 16)
    mph, nps = hr // tm, ns // tn
    if b.ndim == 3:
        b_spec = pl.BlockSpec((None, tk, tn), lambda ni, mi, ki: (ni // (2 * nps), ki, ni % (2 * nps)))
    else:
        b_spec = pl.BlockSpec((tk, tn), lambda ni, mi, ki: (ki, ni))
    return _matmul(
        a, b, (((0,), (0,)), ((), ())), jax.ShapeDtypeStruct((2, 4, hr, ns), _WIRE),
        (4 * nps, 2 * mph, m // tk),
        pl.BlockSpec((tk, tm), lambda ni, mi, ki: (ki, mi)),
        b_spec,
        pl.BlockSpec((None, None, tm, tn), lambda ni, mi, ki: (mi // mph, ni // nps, mi % mph, ni % nps)),
        (tm, tn), 2, name)


def _mm_tn_rows(a, b, name, tm=1536, tn=1024, tk=2048):
    m, ka = a.shape
    r = ka // 4
    hc = b.shape[1] // 2
    tm, tn, tk = _tile(r, tm), _tile(hc, tn), _tile(m, tk, 16)
    mpr, nph = r // tm, hc // tn
    return _matmul(
        a, b, (((0,), (0,)), ((), ())), jax.ShapeDtypeStruct((2, 4, r, hc), _WIRE),
        (2 * nph, 4 * mpr, m // tk),
        pl.BlockSpec((tk, tm), lambda ni, mi, ki: (ki, mi)),
        pl.BlockSpec((tk, tn), lambda ni, mi, ki: (ki, ni)),
        pl.BlockSpec((None, None, tm, tn), lambda ni, mi, ki: (ni // nph, mi // mpr, mi % mpr, ni % nph)),
        (tm, tn), 2, name)


def _row_call(body, name, rows, ins, outs, tm=256):
    tm = _tile(rows, tm, 16)

    def spec(shape, kind):
        if kind == "rows":
            return pl.BlockSpec((tm, shape[1]), lambda i: (i, 0))
        return pl.BlockSpec(shape, lambda i: (0,) * len(shape))

    return pl.pallas_call(
        body, name=name, grid=(rows // tm,),
        in_specs=[spec(a.shape, kind) for a, kind in ins],
        out_specs=[spec(o.shape, kind) for o, kind in outs],
        out_shape=[o for o, _ in outs],
        compiler_params=_cp("arbitrary"),
    )(*[a for a, _ in ins])


def _rms(x):
    r = lax.rsqrt(_rowmean(x * x) + EPS)
    return x * r, r


def _rms_bwd(dxh, xh, r):
    return r * (dxh - xh * _rowmean(dxh * xh))


def _fwd_pre_mix(x, g, sc, sh):
    s, d = x.shape

    def body(x_ref, g_ref, sc_ref, sh_ref, h_ref):
        xh, _ = _rms(x_ref[...])
        h_ref[...] = (xh * g_ref[...] * (1.0 + sc_ref[...]) + sh_ref[...]).astype(h_ref.dtype)

    return _row_call(body, "fwd_pre_mix", s, [(x, "rows"), (g, "vec"), (sc, "vec"), (sh, "vec")],
                     [(jax.ShapeDtypeStruct((s, d), _MXU), "rows")])[0]


def _fwd_mid(o, x, gt1, g_post, g_pre2, sc2, sh2):
    s, d = x.shape

    def body(o_ref, x_ref, gt_ref, gp_ref, g2_ref, sc_ref, sh_ref, x1_ref, h2_ref):
        oh, _ = _rms(o_ref[...])
        x1 = x_ref[...] + gt_ref[...] * (oh * gp_ref[...])
        x1_ref[...] = x1
        xh, _ = _rms(x1)
        h2_ref[...] = (xh * g2_ref[...] * (1.0 + sc_ref[...]) + sh_ref[...]).astype(h2_ref.dtype)

    return _row_call(body, "fwd_mid", s,
                     [(o, "rows"), (x, "rows"), (gt1, "vec"), (g_post, "vec"), (g_pre2, "vec"), (sc2, "vec"),
                      (sh2, "vec")],
                     [(jax.ShapeDtypeStruct((s, d), _F32), "rows"), (jax.ShapeDtypeStruct((s, d), _MXU), "rows")])


def _loss_and_post_ffn_bwd(f, x1, tgt, gt2, g_post):
    s, d = x1.shape

    def body(f_ref, x1_ref, t_ref, gt_ref, g_ref, dx2_ref, df_ref, dgt_ref, dg_ref, loss_ref):
        first = pl.program_id(0) == 0
        fh, r = _rms(f_ref[...])
        n = fh * g_ref[...]
        e = x1_ref[...] + gt_ref[...] * n - t_ref[...]
        _acc(loss_ref, first, jnp.sum(_colsum(e * e), axis=1, keepdims=True) * (0.5 / d))
        dx2 = e * (1.0 / d)
        dx2_ref[...] = dx2
        _acc(dgt_ref, first, _colsum(dx2 * n))
        dn = dx2 * gt_ref[...]
        _acc(dg_ref, first, _colsum(dn * fh))
        df_ref[...] = _rms_bwd(dn * g_ref[...], fh, r).astype(df_ref.dtype)

    vec = jax.ShapeDtypeStruct((1, d), _F32)
    return _row_call(body, "loss_post_ffn_bwd", s,
                     [(f, "rows"), (x1, "rows"), (tgt, "rows"), (gt2, "vec"), (g_post, "vec")],
                     [(jax.ShapeDtypeStruct((s, d), _F32), "rows"), (jax.ShapeDtypeStruct((s, d), _MXU), "rows"),
                      (vec, "vec"), (vec, "vec"), (jax.ShapeDtypeStruct((1, 1), _F32), "vec")])


def _bwd_mid(dh2, x1, dx2, o, g_pre2, sc2, gt1, g_post):
    s, d = x1.shape

    def body(dh_ref, x1_ref, dx2_ref, o_ref, g2_ref, sc_ref, gt_ref, gp_ref,
             dx1_ref, do_ref, dsc_ref, dsh_ref, dg2_ref, dgt_ref, dgp_ref):
        first = pl.program_id(0) == 0
        dh = dh_ref[...]
        xh, r = _rms(x1_ref[...])
        _acc(dsh_ref, first, _colsum(dh))
        _acc(dsc_ref, first, _colsum(dh * (xh * g2_ref[...])))
        dn = dh * (1.0 + sc_ref[...])
        _acc(dg2_ref, first, _colsum(dn * xh))
        dx1 = dx2_ref[...] + _rms_bwd(dn * g2_ref[...], xh, r)
        dx1_ref[...] = dx1
        oh, ro = _rms(o_ref[...])
        _acc(dgt_ref, first, _colsum(dx1 * (oh * gp_ref[...])))
        dno = dx1 * gt_ref[...]
        _acc(dgp_ref, first, _colsum(dno * oh))
        do_ref[...] = _rms_bwd(dno * gp_ref[...], oh, ro).astype(do_ref.dtype)

    vec = jax.ShapeDtypeStruct((1, d), _F32)
    return _row_call(body, "bwd_mid", s,
                     [(dh2, "rows"), (x1, "rows"), (dx2, "rows"), (o, "rows"), (g_pre2, "vec"), (sc2, "vec"),
                      (gt1, "vec"), (g_post, "vec")],
                     [(jax.ShapeDtypeStruct((s, d), _F32), "rows"), (jax.ShapeDtypeStruct((s, d), _MXU), "rows"),
                      (vec, "vec"), (vec, "vec"), (vec, "vec"), (vec, "vec"), (vec, "vec")])


def _bwd_pre_mix(dh1, x, dx1, g, sc1):
    s, d = x.shape

    def body(dh_ref, x_ref, dx1_ref, g_ref, sc_ref, dx_ref, dsc_ref, dsh_ref, dg_ref):
        first = pl.program_id(0) == 0
        dh = dh_ref[...]
        xh, r = _rms(x_ref[...])
        _acc(dsh_ref, first, _colsum(dh))
        _acc(dsc_ref, first, _colsum(dh * (xh * g_ref[...])))
        dn = dh * (1.0 + sc_ref[...])
        _acc(dg_ref, first, _colsum(dn * xh))
        dx_ref[...] = dx1_ref[...] + _rms_bwd(dn * g_ref[...], xh, r)

    vec = jax.ShapeDtypeStruct((1, d), _F32)
    return _row_call(body, "bwd_pre_mix", s,
                     [(dh1, "rows"), (x, "rows"), (dx1, "rows"), (g, "vec"), (sc1, "vec")],
                     [(jax.ShapeDtypeStruct((s, d), _F32), "rows"), (vec, "vec"), (vec, "vec"), (vec, "vec")])


def _mix_norm_fwd(y_ssm, y_sgu, g_ssm, g_sgu):
    s, h = y_ssm.shape

    def body(a_ref, b_ref, ga_ref, gb_ref, o_ref):
        ah, _ = _rms(a_ref[...])
        bh, _ = _rms(b_ref[...])
        o_ref[:, 0:h] = (ah * ga_ref[...]).astype(o_ref.dtype)
        o_ref[:, h:2 * h] = (bh * gb_ref[...]).astype(o_ref.dtype)

    return _row_call(body, "mix_norm_fwd", s, [(y_ssm, "rows"), (y_sgu, "rows"), (g_ssm, "vec"), (g_sgu, "vec")],
                     [(jax.ShapeDtypeStruct((s, 2 * h), _MXU), "rows")])[0]


def _mix_norm_bwd(dyc, y_ssm, y_sgu, g_ssm, g_sgu):
    s, h = y_ssm.shape

    def body(d_ref, a_ref, b_ref, ga_ref, gb_ref, da_ref, db_ref, dga_ref, dgb_ref):
        first = pl.program_id(0) == 0
        for lo, y_ref, g_ref, dy_ref, dg_ref in ((0, a_ref, ga_ref, da_ref, dga_ref), (h, b_ref, gb_ref, db_ref, dgb_ref)):
            d = d_ref[:, lo:lo + h]
            yh, r = _rms(y_ref[...])
            _acc(dg_ref, first, _colsum(d * yh))
            dy_ref[...] = _rms_bwd(d * g_ref[...], yh, r)

    vec = jax.ShapeDtypeStruct((1, h), _F32)
    full = jax.ShapeDtypeStruct((s, h), _F32)
    return _row_call(body, "mix_norm_bwd", s,
                     [(dyc, "rows"), (y_ssm, "rows"), (y_sgu, "rows"), (g_ssm, "vec"), (g_sgu, "vec")],
                     [(full, "rows"), (full, "rows"), (vec, "vec"), (vec, "vec")])


def _shift_down(x, k):
    row = lax.broadcasted_iota(jnp.int32, x.shape, 0)
    return jnp.where(row >= k, pltpu.roll(x, k, 0), 0.0)


def _shift_up(x, k):
    n = x.shape[0]
    row = lax.broadcasted_iota(jnp.int32, x.shape, 0)
    return jnp.where(row < n - k, pltpu.roll(x, n - k, 0), 0.0)


def _conv(x, w_ref, b_ref):
    return b_ref[...] + w_ref[0:1, :] * _shift_down(x, 2) + w_ref[1:2, :] * _shift_down(x, 1) + w_ref[2:3, :] * x


def _conv_act_fwd(up_pre, conv_w, conv_b):
    s, f2 = up_pre.shape
    f = f2 // 2
    tc = _tile(f, 256)
    nf = f // tc

    def body(a_ref, b_ref, wa_ref, wb_ref, ba_ref, bb_ref, o_ref):
        a = _conv(a_ref[...], wa_ref, ba_ref)
        b = _conv(b_ref[...], wb_ref, bb_ref)
        o_ref[...] = (a * _sigmoid(a) * b).astype(o_ref.dtype)

    return pl.pallas_call(
        body, name="conv_act_fwd", grid=(nf,), out_shape=jax.ShapeDtypeStruct((s, f), _MXU),
        in_specs=[pl.BlockSpec((s, tc), lambda n: (0, n)), pl.BlockSpec((s, tc), lambda n: (0, n + nf)),
                  pl.BlockSpec((3, tc), lambda n: (0, n)), pl.BlockSpec((3, tc), lambda n: (0, n + nf)),
                  pl.BlockSpec((1, tc), lambda n: (0, n)), pl.BlockSpec((1, tc), lambda n: (0, n + nf))],
        out_specs=pl.BlockSpec((s, tc), lambda n: (0, n)), compiler_params=_cp("parallel"),
    )(up_pre, up_pre, conv_w, conv_w, conv_b, conv_b)


def _conv_act_bwd(up_pre, d_act, conv_w, conv_b):
    s, f2 = up_pre.shape
    f = f2 // 2
    tc = _tile(f, 256)
    nf = f // tc

    def body(a_ref, b_ref, d_ref, wa_ref, wb_ref, ba_ref, bb_ref,
             du_ref, w0a, w0b, w1a, w1b, w2a, w2b, dba, dbb):
        xa, xb = a_ref[...], b_ref[...]
        a = _conv(xa, wa_ref, ba_ref)
        b = _conv(xb, wb_ref, bb_ref)
        sg = _sigmoid(a)
        d = d_ref[...]
        d_a = d * b * (sg * (1.0 + a * (1.0 - sg)))
        d_b = d * (a * sg)
        for x, du, w_ref, o_ref, o0, o1, o2, ob in ((xa, d_a, wa_ref, du_ref.at[0], w0a, w1a, w2a, dba),
                                                     (xb, d_b, wb_ref, du_ref.at[1], w0b, w1b, w2b, dbb)):
            ob[...] = _colsum(du)
            o0[...] = _colsum(du * _shift_down(x, 2))
            o1[...] = _colsum(du * _shift_down(x, 1))
            o2[...] = _colsum(du * x)
            o_ref[...] = (w_ref[2:3, :] * du + w_ref[1:2, :] * _shift_up(du, 1)
                          + w_ref[0:1, :] * _shift_up(du, 2)).astype(o_ref.dtype)

    col_a = pl.BlockSpec((s, tc), lambda n: (0, n))
    col_b = pl.BlockSpec((s, tc), lambda n: (0, n + nf))
    vec_a = pl.BlockSpec((1, tc), lambda n: (0, n))
    vec_b = pl.BlockSpec((1, tc), lambda n: (0, n + nf))
    vec = jax.ShapeDtypeStruct((1, f), _F32)
    outs = pl.pallas_call(
        body, name="conv_act_bwd", grid=(nf,),
        in_specs=[col_a, col_b, col_a, pl.BlockSpec((3, tc), lambda n: (0, n)),
                  pl.BlockSpec((3, tc), lambda n: (0, n + nf)), vec_a, vec_b],
        out_specs=[pl.BlockSpec((2, s, tc), lambda n: (0, 0, n))] + [vec_a] * 8,
        out_shape=[jax.ShapeDtypeStruct((2, s, f), _MXU)] + [vec] * 8, compiler_params=_cp("parallel"),
    )(up_pre, up_pre, d_act, conv_w, conv_w, conv_b, conv_b)
    du, w0a, w0b, w1a, w1b, w2a, w2b, dba, dbb = outs
    cat = lambda p, q: jnp.concatenate([p, q], axis=1)
    return du, cat(w0a, w0b), cat(w1a, w1b), cat(w2a, w2b), cat(dba, dbb)


def _sgu_recompute(zu_ref, zv_ref, lng_ref, lnb_ref, wm_ref, bs_ref, nh):
    zu, zv = zu_ref[...], zv_ref[...]
    u = _gelu(zu)
    gv = _gelu(zv)
    xc = gv - _rowmean(gv)
    rs = lax.rsqrt(_rowmean(xc * xc) + EPS)
    vh = xc * rs
    v = vh * lng_ref[...] + lnb_ref[...]
    mixed = []
    for h in range(nh):
        vhd = v[:, h * CHUNK:(h + 1) * CHUNK].astype(_MXU)
        mixed.append(jnp.dot(wm_ref[h].astype(_MXU), vhd, preferred_element_type=_F32) + bs_ref[h])
    return zu, zv, u, vh, rs, v, mixed


def _sgu_fwd(z, ln_g, ln_b, wm, bs):
    s = z.shape[0]
    nh = wm.shape[0]
    hd = nh * CHUNK

    def body(zu_ref, zv_ref, lng_ref, lnb_ref, wm_ref, bs_ref, y_ref):
        _, _, u, _, _, _, mixed = _sgu_recompute(zu_ref, zv_ref, lng_ref, lnb_ref, wm_ref, bs_ref, nh)
        for h in range(nh):
            y_ref[:, h * CHUNK:(h + 1) * CHUNK] = u[:, h * CHUNK:(h + 1) * CHUNK] * mixed[h]

    vec = pl.BlockSpec((1, hd), lambda i: (0, 0))
    return pl.pallas_call(
        body, name="sgu_fwd", grid=(s // CHUNK,), out_shape=jax.ShapeDtypeStruct((s, hd), _F32),
        in_specs=[pl.BlockSpec((CHUNK, hd), lambda i: (i, 1)), pl.BlockSpec((CHUNK, hd), lambda i: (i, 2)), vec, vec,
                  pl.BlockSpec((nh, CHUNK, CHUNK), lambda i: (0, 0, 0)), pl.BlockSpec((nh, CHUNK, 1), lambda i: (0, 0, 0))],
        out_specs=pl.BlockSpec((CHUNK, hd), lambda i: (i, 0)), compiler_params=_cp("parallel"),
    )(z, z, ln_g, ln_b, wm, bs)


def _sgu_bwd(z, dy, dz_ssm, ln_g, ln_b, wm, bs):
    s = z.shape[0]
    nh = wm.shape[0]
    hd = nh * CHUNK

    def body(zu_ref, zv_ref, dy_ref, dzs_ref, lng_ref, lnb_ref, wm_ref, bs_ref,
             dz_ref, dlg_ref, dlb_ref, dwm_ref, dbs_ref, dv_scr):
        first = pl.program_id(0) == 0
        zu, zv, u, vh, rs, v, mixed = _sgu_recompute(zu_ref, zv_ref, lng_ref, lnb_ref, wm_ref, bs_ref, nh)
        dy = dy_ref[...]
        dz_ref[:, 0:hd] = dzs_ref[...]
        for h in range(nh):
            cols = slice(h * CHUNK, (h + 1) * CHUNK)
            dyh = dy[:, cols]
            dz_ref[:, hd + h * CHUNK:hd + (h + 1) * CHUNK] = (dyh * mixed[h] * _gelu_grad(zu[:, cols])).astype(dz_ref.dtype)
            dm = dyh * u[:, cols]
            dmx = dm.astype(_MXU)
            _acc(dbs_ref.at[h], first, jnp.sum(dm, axis=1, keepdims=True))
            _acc(dwm_ref.at[h], first,
                 lax.dot_general(dmx, v[:, cols].astype(_MXU), (((1,), (1,)), ((), ())), preferred_element_type=_F32))
            dv_scr[:, cols] = lax.dot_general(wm_ref[h].astype(_MXU), dmx, (((0,), (0,)), ((), ())),
                                              preferred_element_type=_F32)
        dv = dv_scr[...]
        _acc(dlg_ref, first, _colsum(dv * vh))
        _acc(dlb_ref, first, _colsum(dv))
        dvh = dv * lng_ref[...]
        dgv = rs * (dvh - _rowmean(dvh) - vh * _rowmean(dvh * vh))
        dz_ref[:, 2 * hd:3 * hd] = (dgv * _gelu_grad(zv)).astype(dz_ref.dtype)

    vec = pl.BlockSpec((1, hd), lambda i: (0, 0))
    wspec = pl.BlockSpec((nh, CHUNK, CHUNK), lambda i: (0, 0, 0))
    bspec = pl.BlockSpec((nh, CHUNK, 1), lambda i: (0, 0, 0))
    rows = pl.BlockSpec((CHUNK, hd), lambda i: (i, 0))
    return pl.pallas_call(
        body, name="sgu_bwd", grid=(s // CHUNK,),
        out_shape=[jax.ShapeDtypeStruct((s, 3 * hd), _MXU), jax.ShapeDtypeStruct((1, hd), _F32),
                   jax.ShapeDtypeStruct((1, hd), _F32), jax.ShapeDtypeStruct((nh, CHUNK, CHUNK), _F32),
                   jax.ShapeDtypeStruct((nh, CHUNK, 1), _F32)],
        in_specs=[pl.BlockSpec((CHUNK, hd), lambda i: (i, 1)), pl.BlockSpec((CHUNK, hd), lambda i: (i, 2)),
                  rows, rows, vec, vec, wspec, bspec],
        out_specs=[pl.BlockSpec((CHUNK, 3 * hd), lambda i: (i, 0)), vec, vec, wspec, bspec],
        scratch_shapes=[pltpu.VMEM((CHUNK, hd), _F32)], compiler_params=_cp("arbitrary"),
    )(z, z, dy, dz_ssm, ln_g, ln_b, wm, bs)


def _ssm_prep(log_dt, a_re, a_im, b_re_t, b_im_t):
    gn = a_re.shape[1]

    def body(ldt_ref, are_ref, aim_ref, br_ref, bi_ref, pr_ref, pi_ref, bbr_ref, bbi_ref):
        dt = jnp.exp(ldt_ref[...])
        are, aim = are_ref[...], aim_ref[...]
        k = (lax.broadcasted_iota(jnp.int32, (SUBLANES, gn), 0) + 1).astype(_F32)
        mag = jnp.exp(k * (are * dt))
        ang = k * (aim * dt)
        pr_ref[...] = mag * jnp.cos(ang)
        pi_ref[...] = mag * jnp.sin(ang)
        m1 = jnp.exp(are * dt)
        lr, li = m1 * jnp.cos(aim * dt), m1 * jnp.sin(aim * dt)
        den = are * are + aim * aim
        nr = lr - 1.0
        f_re = (nr * are + li * aim) / den
        f_im = (li * are - nr * aim) / den
        bbr_ref[...] = f_re * br_ref[...] - f_im * bi_ref[...]
        bbi_ref[...] = f_re * bi_ref[...] + f_im * br_ref[...]

    pw = jax.ShapeDtypeStruct((SUBLANES, gn), _F32)
    bb = jax.ShapeDtypeStruct(b_re_t.shape, _F32)
    return pl.pallas_call(body, name="ssm_prep", out_shape=[pw, pw, bb, bb])(log_dt, a_re, a_im, b_re_t, b_im_t)


def _ssm_prep_bwd(log_dt, a_re, a_im, b_re_t, b_im_t, d_bbr, d_bbi, d_lr, d_li):
    def body(ldt_ref, are_ref, aim_ref, br_ref, bi_ref, dbr_ref, dbi_ref, dlr_ref, dli_ref,
             obr_ref, obi_ref, oar_ref, oai_ref, odt_ref):
        dt = jnp.exp(ldt_ref[...])
        are, aim = are_ref[...], aim_ref[...]
        m1 = jnp.exp(are * dt)
        lr, li = m1 * jnp.cos(aim * dt), m1 * jnp.sin(aim * dt)
        den = are * are + aim * aim
        nr = lr - 1.0
        f_re = (nr * are + li * aim) / den
        f_im = (li * are - nr * aim) / den
        br, bi, dbr, dbi = br_ref[...], bi_ref[...], dbr_ref[...], dbi_ref[...]
        obr_ref[...] = f_re * dbr + f_im * dbi
        obi_ref[...] = f_re * dbi - f_im * dbr
        gf_re = _colsum(br * dbr + bi * dbi)
        gf_im = _colsum(br * dbi - bi * dbr)
        il_re, il_im = are / den, -aim / den
        glb_re = dlr_ref[...] + (il_re * gf_re + il_im * gf_im)
        glb_im = dli_ref[...] + (il_re * gf_im - il_im * gf_re)
        q_re = -(f_re * il_re - f_im * il_im)
        q_im = -(f_re * il_im + f_im * il_re)
        gl_re = q_re * gf_re + q_im * gf_im
        gl_im = q_re * gf_im - q_im * gf_re
        gl_re = gl_re + dt * (lr * glb_re + li * glb_im)
        gl_im = gl_im + dt * (lr * glb_im - li * glb_re)
        w_re = are * lr - aim * li
        w_im = are * li + aim * lr
        oar_ref[...] = gl_re
        oai_ref[...] = gl_im
        odt_ref[...] = w_re * glb_re + w_im * glb_im

    bb = jax.ShapeDtypeStruct(b_re_t.shape, _F32)
    v = jax.ShapeDtypeStruct(a_re.shape, _F32)
    return pl.pallas_call(body, name="ssm_prep_bwd", out_shape=[bb, bb, v, v, v])(
        log_dt, a_re, a_im, b_re_t, b_im_t, d_bbr, d_bbi, d_lr, d_li)


def _group_sum(d_dt, log_dt):
    def body(d_ref, l_ref, o_ref):
        o_ref[...] = jnp.sum(d_ref[...], axis=1, keepdims=True) * jnp.exp(l_ref[...])

    return pl.pallas_call(body, name="ssm_dt_grad", out_shape=jax.ShapeDtypeStruct(log_dt.shape, _F32))(d_dt, log_dt)


def _scan_rows(src_ref, dst_ref, nrt, steps, ptab, carry0, reverse):
    ns = BLOCK_ST
    row = lax.broadcasted_iota(jnp.int32, (SUBLANES, ns), 0)
    pr, pi = ptab

    def body(i, carry):
        cr, ci = carry
        it = (nrt - 1 - i) if reverse else i
        r0 = pl.multiple_of(it * SUBLANES, SUBLANES)
        xr = src_ref[pl.ds(r0, SUBLANES), 0:ns]
        xi = src_ref[pl.ds(r0, SUBLANES), ns:2 * ns]
        for k, (ar, ai) in zip((1, 2, 4), steps):
            if reverse:
                keep = row < SUBLANES - k
                sr = jnp.where(keep, pltpu.roll(xr, SUBLANES - k, 0), 0.0)
                si = jnp.where(keep, pltpu.roll(xi, SUBLANES - k, 0), 0.0)
            else:
                keep = row >= k
                sr = jnp.where(keep, pltpu.roll(xr, k, 0), 0.0)
                si = jnp.where(keep, pltpu.roll(xi, k, 0), 0.0)
            xr, xi = xr + ar * sr - ai * si, xi + ar * si + ai * sr
        xr, xi = xr + pr * cr - pi * ci, xi + pr * ci + pi * cr
        dst_ref[pl.ds(r0, SUBLANES), 0:ns] = xr
        dst_ref[pl.ds(r0, SUBLANES), ns:2 * ns] = xi
        if reverse:
            return xr[0:1, :], xi[0:1, :]
        return xr[SUBLANES - 1:SUBLANES, :], xi[SUBLANES - 1:SUBLANES, :]

    return lax.fori_loop(0, nrt, body, carry0)


def _scan_consts(p_ref, conj):
    ns = BLOCK_ST
    sign = -1.0 if conj else 1.0
    bc = lambda r: jnp.broadcast_to(r, (SUBLANES, ns))
    steps = [(bc(p_ref[k - 1:k, 0:ns]), bc(sign * p_ref[k - 1:k, ns:2 * ns])) for k in (1, 2, 4)]
    return steps


def _ssm_block_fwd(u, bbt_ref, ct_ref, d_ref, wg_ref, bg_ref, p_ref, bu_scr, h_scr, carry_in, nrt):
    ns = BLOCK_ST
    bu_scr[...] = jnp.dot(u.astype(_MXU), bbt_ref[...].astype(_MXU), preferred_element_type=_F32)
    ptab = (p_ref[:, 0:ns], p_ref[:, ns:2 * ns])
    carry = _scan_rows(bu_scr, h_scr, nrt, _scan_consts(p_ref, False), ptab, carry_in, False)
    y = jnp.dot(h_scr[...].astype(_MXU), ct_ref[...].astype(_MXU), preferred_element_type=_F32) + d_ref[...] * u
    yg = _gelu(y)
    gate = _sigmoid(jnp.dot(yg.astype(_MXU), wg_ref[...].astype(_MXU), preferred_element_type=_F32) + bg_ref[...])
    return y, yg, gate, carry


def _ssm_specs(nb, nt, t, reverse):
    tt = (lambda ti: nt - 1 - ti) if reverse else (lambda ti: ti)
    ns2 = 2 * BLOCK_ST
    return dict(
        z=pl.BlockSpec((t, BLOCK_CH), lambda b, ti: (tt(ti), b)),
        bbt=pl.BlockSpec((None, BLOCK_CH, ns2), lambda b, ti: (b, 0, 0)),
        ct=pl.BlockSpec((None, ns2, BLOCK_CH), lambda b, ti: (b, 0, 0)),
        vec=pl.BlockSpec((1, BLOCK_CH), lambda b, ti: (0, b)),
        wg=pl.BlockSpec((None, BLOCK_CH, BLOCK_CH), lambda b, ti: (b, 0, 0)),
        p=pl.BlockSpec((None, SUBLANES, ns2), lambda b, ti: (b, 0, 0)),
        hb=pl.BlockSpec((None, None, SUBLANES, ns2), lambda b, ti: (b, tt(ti), 0, 0)),
        h=pl.BlockSpec((None, t, ns2), lambda b, ti: (b, tt(ti), 0)),
        acc_vec=pl.BlockSpec((None, 1, ns2), lambda b, ti: (b, 0, 0)),
    )


def _ssm_fwd(z, bbt, ct, dvec, wg, bglu, ptab):
    s = z.shape[0]
    nb = bbt.shape[0]
    t = _tile(s, TIME_TILE, SUBLANES)
    nt = s // t
    ns = BLOCK_ST
    sp = _ssm_specs(nb, nt, t, False)

    def body(z_ref, bbt_ref, ct_ref, d_ref, wg_ref, bg_ref, p_ref, y2_ref, y_ref, h_ref, hb_ref, bu_scr, h_scr, carry_scr):
        ti = pl.program_id(1)

        @pl.when(ti == 0)
        def _():
            carry_scr[...] = jnp.zeros_like(carry_scr)

        hb_ref[...] = carry_scr[...]
        carry_in = (carry_scr[0:1, 0:ns], carry_scr[0:1, ns:2 * ns])
        y, yg, gate, (cr, ci) = _ssm_block_fwd(z_ref[...], bbt_ref, ct_ref, d_ref, wg_ref, bg_ref, p_ref,
                                               bu_scr, h_scr, carry_in, t // SUBLANES)
        y2_ref[...] = yg * gate
        y_ref[...] = y
        h_ref[...] = h_scr[...].astype(h_ref.dtype)
        carry_scr[:, 0:ns] = jnp.broadcast_to(cr, (SUBLANES, ns))
        carry_scr[:, ns:2 * ns] = jnp.broadcast_to(ci, (SUBLANES, ns))

    ych = jax.ShapeDtypeStruct((s, nb * BLOCK_CH), _F32)
    return pl.pallas_call(
        body, name="ssm_fwd", grid=(nb, nt),
        out_shape=[ych, ych, jax.ShapeDtypeStruct((nb, s, 2 * ns), _MXU),
                   jax.ShapeDtypeStruct((nb, nt, SUBLANES, 2 * ns), _F32)],
        in_specs=[sp["z"], sp["bbt"], sp["ct"], sp["vec"], sp["wg"], sp["vec"], sp["p"]],
        out_specs=[sp["z"], sp["z"], sp["h"], sp["hb"]],
        scratch_shapes=[pltpu.VMEM((t, 2 * ns), _F32), pltpu.VMEM((t, 2 * ns), _F32), pltpu.VMEM((SUBLANES, 2 * ns), _F32)],
        compiler_params=_cp("parallel", "arbitrary"),
    )(z, bbt, ct, dvec, wg, bglu, ptab)


def _ssm_bwd(z, y_pre, h_all, dy2, hb, bbt, ct, dvec, wg, bglu, ptab, ptab_rev):
    s = z.shape[0]
    nb = bbt.shape[0]
    t = _tile(s, TIME_TILE, SUBLANES)
    nt = s // t
    ns = BLOCK_ST
    sp = _ssm_specs(nb, nt, t, True)
    tn_dims = (((0,), (0,)), ((), ()))
    nt_dims = (((1,), (1,)), ((), ()))

    def body(z_ref, y_ref, h_ref, dy2_ref, hb_ref, bbt_ref, ct_ref, d_ref, wg_ref, bg_ref, p_ref, pr_ref,
             dz_ref, dbbt_ref, dct_ref, dwg_ref, dlb_ref, dd_ref, dbg_ref, bu_scr, g_scr, gcarry_scr):
        first = pl.program_id(1) == 0

        @pl.when(first)
        def _():
            gcarry_scr[...] = jnp.zeros_like(gcarry_scr)

        u = z_ref[...]
        hin = hb_ref[...]
        y = y_ref[...]
        yg = _gelu(y)
        gate = _sigmoid(jnp.dot(yg.astype(_MXU), wg_ref[...].astype(_MXU), preferred_element_type=_F32) + bg_ref[...])
        dy2 = dy2_ref[...]
        dpre = dy2 * yg * gate * (1.0 - gate)
        _acc(dbg_ref, first, _colsum(dpre))
        dpx = dpre.astype(_MXU)
        _acc(dwg_ref, first, lax.dot_general(yg.astype(_MXU), dpx, tn_dims, preferred_element_type=_F32))
        dyg = dy2 * gate + lax.dot_general(dpx, wg_ref[...].astype(_MXU), nt_dims, preferred_element_type=_F32)
        dy = dyg * _gelu_grad(y)
        _acc(dd_ref, first, _colsum(dy * u))
        dyx = dy.astype(_MXU)
        hx = h_ref[...]
        h = hx.astype(_F32)
        _acc(dct_ref, first, lax.dot_general(hx, dyx, tn_dims, preferred_element_type=_F32))
        bu_scr[...] = lax.dot_general(dyx, ct_ref[...].astype(_MXU), nt_dims, preferred_element_type=_F32)
        gin = (gcarry_scr[0:1, 0:ns], gcarry_scr[0:1, ns:2 * ns])
        ptab = (pr_ref[:, 0:ns], pr_ref[:, ns:2 * ns])
        gr, gi = _scan_rows(bu_scr, g_scr, t // SUBLANES, _scan_consts(p_ref, True), ptab, gin, True)
        gcarry_scr[:, 0:ns] = jnp.broadcast_to(gr, (SUBLANES, ns))
        gcarry_scr[:, ns:2 * ns] = jnp.broadcast_to(gi, (SUBLANES, ns))
        g = g_scr[...]
        row = lax.broadcasted_iota(jnp.int32, (t, ns), 0)
        hp_re = jnp.where(row == 0, hin[0:1, 0:ns], pltpu.roll(h[:, 0:ns], 1, 0))
        hp_im = jnp.where(row == 0, hin[0:1, ns:2 * ns], pltpu.roll(h[:, ns:2 * ns], 1, 0))
        g_re, g_im = g[:, 0:ns], g[:, ns:2 * ns]
        d_ar = _colsum(g_re * hp_re + g_im * hp_im)
        d_ai = _colsum(g_im * hp_re - g_re * hp_im)
        _acc(dlb_ref, first, jnp.concatenate([d_ar, d_ai], axis=1))
        gx = g.astype(_MXU)
        _acc(dbbt_ref, first, lax.dot_general(u.astype(_MXU), gx, tn_dims, preferred_element_type=_F32))
        dz_ref[...] = (dy * d_ref[...] + lax.dot_general(gx, bbt_ref[...].astype(_MXU), nt_dims,
                                                         preferred_element_type=_F32)).astype(dz_ref.dtype)

    f = lambda shape: jax.ShapeDtypeStruct(shape, _F32)
    return pl.pallas_call(
        body, name="ssm_bwd", grid=(nb, nt),
        out_shape=[jax.ShapeDtypeStruct((s, nb * BLOCK_CH), _MXU), f(bbt.shape), f(ct.shape), f(wg.shape), f((nb, 1, 2 * ns)),
                   f((1, nb * BLOCK_CH)), f((1, nb * BLOCK_CH))],
        in_specs=[sp["z"], sp["z"], sp["h"], sp["z"], sp["hb"], sp["bbt"], sp["ct"], sp["vec"], sp["wg"], sp["vec"], sp["p"],
                  sp["p"]],
        out_specs=[sp["z"], sp["bbt"], sp["ct"], sp["wg"], sp["acc_vec"], sp["vec"], sp["vec"]],
        scratch_shapes=[pltpu.VMEM((t, 2 * ns), _F32), pltpu.VMEM((t, 2 * ns), _F32), pltpu.VMEM((SUBLANES, 2 * ns), _F32)],
        compiler_params=_cp("parallel", "arbitrary"),
    )(z, y_pre, h_all, dy2, hb, bbt, ct, dvec, wg, bglu, ptab, ptab_rev)


def _mod_part(c_all, w, b):
    d, ns = w.shape
    tn = _tile(ns, 512)

    def body(c_ref, w_ref, b_ref, o_ref):
        c = c_ref[...]
        ca = (c * _sigmoid(c)).astype(_MXU)
        o_ref[...] = jnp.dot(ca, w_ref[...].astype(_MXU), preferred_element_type=_F32) + b_ref[...]

    return pl.pallas_call(
        body, name="mod_part", grid=(ns // tn,), out_shape=jax.ShapeDtypeStruct((8, ns), _F32),
        in_specs=[pl.BlockSpec((8, d), lambda n: (0, 0)), pl.BlockSpec((d, tn), lambda n: (0, n)),
                  pl.BlockSpec((1, tn), lambda n: (0, n))],
        out_specs=pl.BlockSpec((8, tn), lambda n: (0, n)), compiler_params=_cp("parallel"),
    )(c_all, w, b)


def _adamw_math(w, g, m, v):
    m = ADAM_B1 * m + (1.0 - ADAM_B1) * g
    v = ADAM_B2 * v + (1.0 - ADAM_B2) * (g * g)
    m_hat = m / (1.0 - ADAM_B1 ** ADAM_STEP)
    v_hat = v / (1.0 - ADAM_B2 ** ADAM_STEP)
    delta = -ADAM_LR * (m_hat / (jnp.sqrt(v_hat) + ADAM_EPS) + ADAM_WD * w)
    return delta, m, v


def _adamw(w, g, m, v, name):
    r, c = w.shape
    tc = c if c <= 4096 else _tile(c, 4096)
    tr = _tile(r, max(SUBLANES, (1 << 18) // tc), SUBLANES)

    def body(w_ref, g_ref, m_ref, v_ref, d_ref, mo_ref, vo_ref):
        d_ref[...], mo_ref[...], vo_ref[...] = _adamw_math(w_ref[...], g_ref[...], m_ref[...], v_ref[...])

    spec = pl.BlockSpec((tr, tc), lambda i, j: (i, j))
    out = jax.ShapeDtypeStruct((r, c), _F32)
    return pl.pallas_call(
        body, name=name, grid=(r // tr, c // tc), in_specs=[spec] * 4, out_specs=[spec] * 3, out_shape=[out] * 3,
        compiler_params=_cp("parallel", "parallel"),
    )(w, g, m, v)


def _adamw_halves(w, g2, m, v, name):
    r, c = w.shape
    tr, tc = _tile(r, 256, SUBLANES), _tile(c // 2, 1024)
    nph = (c // 2) // tc

    def body(w_ref, g_ref, m_ref, v_ref, go_ref, d_ref, mo_ref, vo_ref):
        g = g_ref[...]
        go_ref[...] = g
        d_ref[...], mo_ref[...], vo_ref[...] = _adamw_math(w_ref[...], g, m_ref[...], v_ref[...])

    spec = pl.BlockSpec((tr, tc), lambda i, j: (i, j))
    out = jax.ShapeDtypeStruct((r, c), _F32)
    return pl.pallas_call(
        body, name=name, grid=(r // tr, c // tc),
        in_specs=[spec, pl.BlockSpec((None, tr, tc), lambda i, j: (j // nph, i, j % nph)), spec, spec],
        out_specs=[spec] * 4, out_shape=[out] * 4, compiler_params=_cp("parallel", "parallel"),
    )(w, g2, m, v)


def _wada_update(c_t, dm, w, m, v):
    d, ns = w.shape
    tr, tc = _tile(d, 256, SUBLANES), _tile(ns, 1024)

    def body(c_ref, dm_ref, w_ref, m_ref, v_ref, g_ref, d_ref, mo_ref, vo_ref):
        c = c_ref[...]
        ca = c * _sigmoid(c)
        dmv = dm_ref[...]
        g = ca[:, 0:1] * dmv[0:1, :]
        for b in range(1, 8):
            g = g + ca[:, b:b + 1] * dmv[b:b + 1, :]
        g_ref[...] = g
        d_ref[...], mo_ref[...], vo_ref[...] = _adamw_math(w_ref[...], g, m_ref[...], v_ref[...])

    spec = pl.BlockSpec((tr, tc), lambda i, j: (i, j))
    out = jax.ShapeDtypeStruct((d, ns), _F32)
    return pl.pallas_call(
        body, name="wada_update", grid=(d // tr, ns // tc),
        in_specs=[pl.BlockSpec((tr, 8), lambda i, j: (i, 0)), pl.BlockSpec((8, tc), lambda i, j: (0, j)), spec, spec, spec],
        out_specs=[spec] * 4, out_shape=[out] * 4, compiler_params=_cp("parallel", "parallel"),
    )(c_t, dm, w, m, v)


def _small_reduce(gathered):
    _, r, c = gathered.shape
    tr = _tile(r, 512, SUBLANES)

    def body(q_ref, g_ref):
        g = q_ref[0]
        for k in range(1, 8):
            g = g + q_ref[k]
        g_ref[...] = g

    return pl.pallas_call(
        body, name="small_reduce", grid=(r // tr,), out_shape=jax.ShapeDtypeStruct((r, c), _F32),
        in_specs=[pl.BlockSpec((8, tr, c), lambda i: (0, i, 0))], out_specs=pl.BlockSpec((tr, c), lambda i: (i, 0)),
        compiler_params=_cp("parallel"),
    )(gathered)


def _adamw_many(ws, gs, ms, vs, steps, name):
    n = len(ws)

    def body(*refs):
        w_refs, g_refs, m_refs, v_refs = refs[0:n], refs[n:2 * n], refs[2 * n:3 * n], refs[3 * n:4 * n]
        d_refs, mo_refs, vo_refs = refs[4 * n:5 * n], refs[5 * n:6 * n], refs[6 * n:7 * n]
        for i in range(n):
            d_refs[i][...], mo_refs[i][...], vo_refs[i][...] = _adamw_math(
                w_refs[i][...], g_refs[i][...], m_refs[i][...], v_refs[i][...])

    def spec(a):
        nd = a.ndim
        if steps == 1:
            return pl.BlockSpec(a.shape, lambda i: (0,) * nd)
        return pl.BlockSpec((a.shape[0] // steps,) + a.shape[1:], lambda i: (i,) + (0,) * (nd - 1))

    specs = [spec(w) for w in ws]
    outs = pl.pallas_call(
        body, name=name, grid=(steps,), in_specs=specs * 4, out_specs=specs * 3,
        out_shape=[jax.ShapeDtypeStruct(w.shape, _F32) for w in ws] * 3, compiler_params=_cp("parallel"),
    )(*ws, *gs, *ms, *vs)
    return outs[0:n], outs[n:2 * n], outs[2 * n:3 * n]


def _block_diag(x):
    nb, g, p, q = x.shape
    eye = jnp.eye(g, dtype=x.dtype)
    return (x[:, :, :, None, :] * eye[None, :, None, :, None]).reshape(nb, g * p, g * q)


def _block_diag_take(x, p, q):
    nb = x.shape[0]
    g = GROUPS_PER_BLOCK
    eye = jnp.eye(g, dtype=x.dtype)
    return jnp.sum(x.reshape(nb, g, p, g, q) * eye[None, :, None, :, None], axis=3)


class _Pack:
    def __init__(self, shapes):
        self.shapes = shapes
        self.offsets = {}
        off = 0
        for name, shape in shapes.items():
            n = math.prod(shape)
            self.offsets[name] = (off, n)
            off += -(-n // (SUBLANES * LANES)) * (SUBLANES * LANES)
        self.rows = -(-off // (256 * LANES)) * 256

    def pack(self, arrays):
        parts = []
        off = 0
        for name, shape in self.shapes.items():
            start, n = self.offsets[name]
            if start > off:
                parts.append(jnp.zeros((start - off,), _F32))
            parts.append(arrays[name].reshape(-1).astype(_F32))
            off = start + n
        total = self.rows * LANES
        if total > off:
            parts.append(jnp.zeros((total - off,), _F32))
        return jnp.concatenate(parts).reshape(self.rows, LANES)

    def unpack(self, buf):
        flat = buf.reshape(-1)
        return {name: flat[start:start + n].reshape(self.shapes[name]) for name, (start, n) in self.offsets.items()}


_SMALL = ["b_ada", "g_pre_mix", "g_post_mix", "ssm_log_dt", "ssm_a_re", "ssm_a_im", "ssm_b_re", "ssm_b_im", "ssm_c_re",
          "ssm_c_im", "ssm_d", "ssm_w_glu", "ssm_b_glu", "sgu_ln_g", "sgu_ln_b", "sgu_w", "sgu_b", "g_out_ssm",
          "g_out_sgu", "g_pre_ffn", "g_post_ffn", "conv_b"]
_WEIGHTS = ["w_ada", "b_ada", "g_pre_mix", "g_post_mix", "w_in", "ssm_log_dt", "ssm_a_re", "ssm_a_im", "ssm_b_re",
            "ssm_b_im", "ssm_c_re", "ssm_c_im", "ssm_d", "ssm_w_glu", "ssm_b_glu", "sgu_ln_g", "sgu_ln_b", "sgu_w", "sgu_b",
            "g_out_ssm", "g_out_sgu", "w_out", "g_pre_ffn", "g_post_ffn", "w_up", "conv_w", "conv_b", "w_down"]


def _step(p, m, v, x, c, tgt):
    s, d = x.shape
    mx, my, mc = lax.axis_index("x"), lax.axis_index("y"), lax.axis_index("c")
    chip = 2 * mx + my
    dev = 4 * mx + 2 * my + mc
    sel = jnp.stack([chip, mc]).astype(jnp.int32)
    g_cnt, n_st = p["ssm_a_re"].shape
    nb = g_cnt // GROUPS_PER_BLOCK
    gn = g_cnt * n_st
    d_ssm = g_cnt * SSM_GROUP
    nh = p["sgu_w"].shape[0]
    assert nh * CHUNK == d_ssm and 2 * d_ssm == d and n_st == SSM_STATE

    shards = lambda g: g.reshape(4, g.shape[1] * g.shape[2], g.shape[3])
    buf_in, buf_out, buf_up, buf_down = [_cast_into_slot(p[n], sel) for n in ("w_in", "w_out", "w_up", "w_down")]

    ns_ada = p["w_ada"].shape[1]
    nc_conv = p["conv_w"].shape[1]
    first = jnp.concatenate([jnp.broadcast_to(c, (8, d)), jnp.pad(p["conv_w"], ((0, 5), (0, 0)))], axis=1)
    first_all = _all_gather8(_own_slot(first, dev), "gather_c_conv")
    (sems_in,), (buf_in,), tok = _gather_start([buf_in], first_all, "gather_start_in")
    c_all = _after(first_all[:, 0, :d], tok)
    conv_w_full = jnp.concatenate([first_all[2 * j, 0:3, d:] for j in range(4)], axis=1)
    b_ada_mine = lax.dynamic_slice_in_dim(p["b_ada"], chip * ns_ada, ns_ada, axis=1)
    mod_all = _all_gather8(_own_slot(_mod_part(c_all, p["w_ada"], b_ada_mine), dev), "gather_mod")
    mod_rows = lax.dynamic_index_in_dim(mod_all, dev, axis=1, keepdims=False)
    mod = jnp.concatenate([mod_rows[0], mod_rows[2], mod_rows[4], mod_rows[6]]).reshape(N_MOD, 1, d)
    sh1, sc1, gt1, sh2, sc2, gt2 = [mod[i] for i in range(N_MOD)]

    ldt_l = jnp.repeat(p["ssm_log_dt"], n_st, axis=1)
    are_l, aim_l = p["ssm_a_re"].reshape(1, gn), p["ssm_a_im"].reshape(1, gn)
    bre_t, bim_t = p["ssm_b_re"].reshape(gn, SSM_GROUP).T, p["ssm_b_im"].reshape(gn, SSM_GROUP).T
    pw_re, pw_im, bb_re, bb_im = _ssm_prep(ldt_l, are_l, aim_l, bre_t, bim_t)
    blocks = lambda t: t.reshape(t.shape[0], nb, GROUPS_PER_BLOCK * n_st).transpose(1, 0, 2)
    ptab = jnp.concatenate([blocks(pw_re), blocks(pw_im)], axis=2)
    ptab_rev = jnp.concatenate([blocks(pw_re)[:, ::-1], -blocks(pw_im)[:, ::-1]], axis=2)
    bd = lambda t: t.reshape(SSM_GROUP, nb, GROUPS_PER_BLOCK, n_st).transpose(1, 2, 0, 3)
    bbt = jnp.concatenate([_block_diag(bd(bb_re)), _block_diag(bd(bb_im))], axis=2).astype(_MXU)
    cd = lambda t: t.reshape(nb, GROUPS_PER_BLOCK, SSM_GROUP, n_st).transpose(0, 1, 3, 2)
    ct = jnp.concatenate([_block_diag(cd(p["ssm_c_re"])), -_block_diag(cd(p["ssm_c_im"]))], axis=1).astype(_MXU)
    wg = _block_diag(p["ssm_w_glu"].reshape(nb, GROUPS_PER_BLOCK, SSM_GROUP, SSM_GROUP)).astype(_MXU)
    dvec = p["ssm_d"]
    bglu = p["ssm_b_glu"].reshape(1, d_ssm)
    mask = jnp.tril(jnp.ones((CHUNK, CHUNK), _F32))
    wm = (p["sgu_w"] * mask[None]).astype(_MXU)
    bs = p["sgu_b"].reshape(nh, CHUNK, 1)

    buf_in = _gather_wait(sems_in, buf_in, mod_all, "gather_wait_in")
    w_in4 = shards(_pair_forward([buf_in], "pair_forward_in")[0])
    (sems_out, sems_up, sems_down), (buf_out, buf_up, buf_down), tok = _gather_start(
        [buf_out, buf_up, buf_down], w_in4, "gather_start_rest")
    h1 = _fwd_pre_mix(x, p["g_pre_mix"], _after(sc1, tok), sh1)
    z = _mm_nn(h1, w_in4, _F32, "mm_in")
    y_ssm, y_pre, h_all, hb = _ssm_fwd(z, bbt, ct, dvec, wg, bglu, ptab)
    y_sgu = _sgu_fwd(z, p["sgu_ln_g"], p["sgu_ln_b"], wm, bs)
    ycat = _mix_norm_fwd(y_ssm, y_sgu, p["g_out_ssm"], p["g_out_sgu"])
    buf_out = _gather_wait(sems_out, buf_out, ycat, "gather_wait_out")
    w_out_full = _pair_forward([buf_out], "pair_forward_out")[0].reshape(1, d, d)
    o = _mm_nn(ycat, w_out_full, _F32, "mm_out")
    x1, h2 = _fwd_mid(o, x, gt1, p["g_post_mix"], p["g_pre_ffn"], sc2, sh2)
    buf_up = _gather_wait(sems_up, buf_up, h2, "gather_wait_up")
    w_up4 = shards(_pair_forward([buf_up], "pair_forward_up")[0])
    up_pre = _mm_nn(h2, w_up4, _F32, "mm_up")
    act = _conv_act_fwd(up_pre, conv_w_full, p["conv_b"])
    buf_down = _gather_wait(sems_down, buf_down, act, "gather_wait_down")
    w_down_full = _pair_forward([buf_down], "pair_forward_down")[0].reshape(1, -1, d)
    f = _mm_nn(act, w_down_full, _F32, "mm_down", tk=5632)
    dx2, df, d_gt2, d_g_post_ffn, loss = _loss_and_post_ffn_bwd(f, x1, tgt, gt2, p["g_post_ffn"])

    def reduce_start(gw, n):
        got = _pair_swap([gw], "pair_swap_" + n)[0]
        pair = _pair_sum(gw, got, sel, "pair_sum_" + n)
        return _scatter_start(pair, "scatter_start_" + n)

    d_act = _mm_nt(df, w_down_full, _F32, "mm_d_act", tk=2048)
    gw_down = _mm_tn_rows(act, df, "mm_gw_down")
    red_down = reduce_start(gw_down, "w_down")
    d_up_pre, d_cw0, d_cw1, d_cw2, d_conv_b = _conv_act_bwd(up_pre, d_act, conv_w_full, _after(p["conv_b"], red_down[3]))
    dh2 = _mm_nt(d_up_pre, w_up4, _F32, "mm_dh2", tk=2816)
    gw_up = _mm_tn_cols(h2, d_up_pre, "mm_gw_up")
    red_up = reduce_start(gw_up, "w_up")
    dx1, d_o, d_sc2, d_sh2, d_g_pre_ffn, d_gt1, d_g_post_mix = _bwd_mid(
        dh2, x1, dx2, o, p["g_pre_ffn"], _after(sc2, red_up[3]), gt1, p["g_post_mix"])
    d_ycat = _mm_nt(d_o, w_out_full, _F32, "mm_d_ycat", tn=1024, tk=2048)
    gw_out = _mm_tn_rows(ycat, d_o, "mm_gw_out")
    red_out = reduce_start(gw_out, "w_out")
    dy_ssm, dy_sgu, d_g_out_ssm, d_g_out_sgu = _mix_norm_bwd(d_ycat, y_ssm, y_sgu, _after(p["g_out_ssm"], red_out[3]),
                                                              p["g_out_sgu"])
    dz_ssm, d_bbt, d_ct, d_wg, d_lb, d_ssm_d, d_bglu = _ssm_bwd(z, y_pre, h_all, dy_ssm, hb, bbt, ct, dvec, wg, bglu, ptab,
                                                                ptab_rev)
    dz, d_ln_g, d_ln_b, d_wm, d_bs = _sgu_bwd(z, dy_sgu, dz_ssm, p["sgu_ln_g"], p["sgu_ln_b"], wm, bs)
    dh1 = _mm_nt(dz, w_in4, _F32, "mm_dh1")
    gw_in = _mm_tn_cols(h1, dz, "mm_gw_in")
    red_in = reduce_start(gw_in, "w_in")
    dx, d_sc1, d_sh1, d_g_pre_mix = _bwd_pre_mix(dh1, x, dx1, p["g_pre_mix"], _after(sc1, red_in[3]))

    nsb = BLOCK_ST
    lanes = lambda t: t.transpose(2, 0, 1, 3).reshape(SSM_GROUP, gn)
    d_bbr = lanes(_block_diag_take(d_bbt[:, :, :nsb], SSM_GROUP, n_st))
    d_bbi = lanes(_block_diag_take(d_bbt[:, :, nsb:], SSM_GROUP, n_st))
    d_lr, d_li = d_lb[:, 0, :nsb].reshape(1, gn), d_lb[:, 0, nsb:].reshape(1, gn)
    d_bre_t, d_bim_t, d_are, d_aim, d_dt = _ssm_prep_bwd(ldt_l, are_l, aim_l, bre_t, bim_t, d_bbr, d_bbi, d_lr, d_li)
    d_log_dt = _group_sum(d_dt.reshape(g_cnt, n_st), p["ssm_log_dt"].reshape(g_cnt, 1))
    c_grad = lambda t: _block_diag_take(t, n_st, SSM_GROUP).transpose(0, 1, 3, 2).reshape(g_cnt, SSM_GROUP, n_st)
    small = {
        "b_ada": jnp.concatenate([d_sh1, d_sc1, d_gt1, d_sh2, d_sc2, d_gt2], axis=1),
        "g_pre_mix": d_g_pre_mix, "g_post_mix": d_g_post_mix,
        "ssm_log_dt": d_log_dt, "ssm_a_re": d_are, "ssm_a_im": d_aim,
        "ssm_b_re": d_bre_t.T, "ssm_b_im": d_bim_t.T,
        "ssm_c_re": c_grad(d_ct[:, :nsb, :]), "ssm_c_im": -c_grad(d_ct[:, nsb:, :]),
        "ssm_d": d_ssm_d, "ssm_w_glu": _block_diag_take(d_wg, SSM_GROUP, SSM_GROUP), "ssm_b_glu": d_bglu,
        "sgu_ln_g": d_ln_g, "sgu_ln_b": d_ln_b, "sgu_w": d_wm * mask[None], "sgu_b": d_bs,
        "g_out_ssm": d_g_out_ssm, "g_out_sgu": d_g_out_sgu, "g_pre_ffn": d_g_pre_ffn, "g_post_ffn": d_g_post_ffn,
        "conv_b": d_conv_b, "conv_w_all": jnp.concatenate([d_cw0, d_cw1, d_cw2], axis=0),
    }
    shapes = {name: p[name].shape for name in _SMALL}
    shapes["conv_w_all"] = (3, 4 * nc_conv)
    pk = _Pack(shapes)
    sems_small, small_buf, tok = _gather8_start(_own_slot(pk.pack(small), dev), "gather_small_start")

    big = ["w_down", "w_up", "w_out", "w_in"]
    mine = []
    after = tok
    for n, (sems, pair, land, _) in zip(big, (red_down, red_up, red_out, red_in)):
        pair, land = _scatter_wait(sems, pair, land, after, "scatter_wait_" + n)
        mine.append(_chip_sum(pair, land, sel, "chip_sum_" + n))
        after = mine[-1]
    joined = _pair_join(mine)
    big_out = {}
    for n, j in zip(big, joined):
        if n in ("w_in", "w_up"):
            g_n = j.reshape(p[n].shape)
            big_out[n] = (g_n,) + tuple(_adamw(p[n], g_n, m[n], v[n], "adamw_" + n))
        else:
            big_out[n] = tuple(_adamw_halves(p[n], j, m[n], v[n], "adamw_" + n))

    gathered = _gather8_forward(_gather8_wait(sems_small, small_buf, big_out["w_in"][1], "gather_small_wait"),
                                "gather_small_forward")
    grads = pk.unpack(_small_reduce(gathered))
    grads["conv_w"] = lax.dynamic_slice_in_dim(grads.pop("conv_w_all"), chip * nc_conv, nc_conv, axis=1)
    per_group = [n for n in _SMALL if p[n].ndim >= 2 and p[n].shape[0] == g_cnt]
    others = [n for n in _SMALL if n not in per_group] + ["conv_w"]
    deltas, new_m, new_v = {}, {}, {}
    for names, steps, call in ((per_group, g_cnt // GROUPS_PER_BLOCK, "adamw_s5"), (others, 1, "adamw_small")):
        res = _adamw_many([p[n] for n in names], [grads[n] for n in names], [m[n] for n in names], [v[n] for n in names],
                          steps, call)
        for n, dl, mo, vo in zip(names, *res):
            deltas[n], new_m[n], new_v[n] = dl, mo, vo

    d_mod_all = gathered.reshape(8, -1)[:, :N_MOD * d]
    d_mod_mine = lax.dynamic_slice_in_dim(d_mod_all, chip * ns_ada, ns_ada, axis=1)
    grads["w_ada"], deltas["w_ada"], new_m["w_ada"], new_v["w_ada"] = _wada_update(
        c_all.T, d_mod_mine, p["w_ada"], m["w_ada"], v["w_ada"])
    for n in big:
        grads[n], deltas[n], new_m[n], new_v[n] = big_out[n]
    return loss[0, 0], dx, grads, deltas, new_m, new_v


def kernel(x, c, w_ada, b_ada, g_pre_mix, g_post_mix, w_in, ssm_log_dt, ssm_a_re, ssm_a_im, ssm_b_re, ssm_b_im, ssm_c_re, ssm_c_im, ssm_d, ssm_w_glu, ssm_b_glu, sgu_ln_g, sgu_ln_b, sgu_w, sgu_b, g_out_ssm, g_out_sgu, w_out, g_pre_ffn, g_post_ffn, w_up, conv_w, conv_b, w_down, loss_target, m_w_ada, m_b_ada, m_g_pre_mix, m_g_post_mix, m_w_in, m_ssm_log_dt, m_ssm_a_re, m_ssm_a_im, m_ssm_b_re, m_ssm_b_im, m_ssm_c_re, m_ssm_c_im, m_ssm_d, m_ssm_w_glu, m_ssm_b_glu, m_sgu_ln_g, m_sgu_ln_b, m_sgu_w, m_sgu_b, m_g_out_ssm, m_g_out_sgu, m_w_out, m_g_pre_ffn, m_g_post_ffn, m_w_up, m_conv_w, m_conv_b, m_w_down, v_w_ada, v_b_ada, v_g_pre_mix, v_g_post_mix, v_w_in, v_ssm_log_dt, v_ssm_a_re, v_ssm_a_im, v_ssm_b_re, v_ssm_b_im, v_ssm_c_re, v_ssm_c_im, v_ssm_d, v_ssm_w_glu, v_ssm_b_glu, v_sgu_ln_g, v_sgu_ln_b, v_sgu_w, v_sgu_b, v_g_out_ssm, v_g_out_sgu, v_w_out, v_g_pre_ffn, v_g_post_ffn, v_w_up, v_conv_w, v_conv_b, v_w_down):
    given = dict(locals())
    drop = lambda a: a if a.ndim == 2 else a[0]
    p = {n: drop(given[n]) for n in _WEIGHTS}
    m = {n: drop(given["m_" + n]) for n in _WEIGHTS}
    v = {n: drop(given["v_" + n]) for n in _WEIGHTS}
    loss, dx, grads, deltas, new_m, new_v = _step(p, m, v, x[0], c, loss_target[0])
    loss = lax.psum(loss, ("x", "y", "c"))
    outs = [loss, dx[None]]
    for group in (grads, deltas, new_m, new_v):
        outs += [group[n].reshape(given[n].shape) for n in _WEIGHTS]
    return tuple(outs)
```

```python
import functools
import math

import jax
import jax.numpy as jnp
from jax import lax
from jax.experimental import pallas as pl
from jax.experimental.pallas import tpu as pltpu

_F32 = jnp.float32
_MXU = jnp.bfloat16
_WIRE = jnp.bfloat16

EPS = 1e-6
SSM_GROUP = 16
SSM_STATE = 64
GROUPS_PER_BLOCK = 8
BLOCK_CH = SSM_GROUP * GROUPS_PER_BLOCK
BLOCK_ST = SSM_STATE * GROUPS_PER_BLOCK
CHUNK = 128
TIME_TILE = 512
SUBLANES = 8
LANES = 128
N_MOD = 6
ADAM_LR, ADAM_B1, ADAM_B2, ADAM_EPS, ADAM_WD, ADAM_STEP = 0.001, 0.9, 0.999, 1e-08, 0.01, 10
_VMEM_LIMIT = 56 * 1024 * 1024
_MESH = pl.DeviceIdType.MESH
_ANY = pl.BlockSpec(memory_space=pl.ANY)
_HBM = pl.BlockSpec(memory_space=pltpu.HBM)
_SEM = pl.BlockSpec(memory_space=pltpu.SEMAPHORE)
_VMEM_WHOLE = pl.BlockSpec(memory_space=pltpu.VMEM)
_EFFECT = pltpu.SideEffectType.DATAFLOW_SIDE_EFFECTING
_GELU_C = math.sqrt(2.0 / math.pi)


def _cp(*sem):
    return pltpu.CompilerParams(dimension_semantics=sem, vmem_limit_bytes=_VMEM_LIMIT)


def _tile(dim, target, align=LANES):
    if dim <= target:
        return dim
    best = None
    for t in range(align, target + 1, align):
        if dim % t == 0:
            best = t
    assert best is not None, (dim, target, align)
    return best


def _gelu(x):
    return 0.5 * x * (1.0 + jnp.tanh(_GELU_C * (x + 0.044715 * (x * x * x))))


def _gelu_grad(x):
    t = jnp.tanh(_GELU_C * (x + 0.044715 * (x * x * x)))
    return 0.5 * (1.0 + t) + 0.5 * x * (1.0 - t * t) * (_GELU_C * (1.0 + 3.0 * 0.044715 * x * x))


def _sigmoid(x):
    return 1.0 / (1.0 + jnp.exp(-x))


def _colsum(x):
    return jnp.sum(x, axis=0, keepdims=True)


def _rowmean(x):
    return jnp.mean(x, axis=-1, keepdims=True)


def _acc(ref, first, val):
    @pl.when(first)
    def _():
        ref[...] = val

    @pl.when(jnp.logical_not(first))
    def _():
        ref[...] += val


def _place():
    mx, my, mc = lax.axis_index("x"), lax.axis_index("y"), lax.axis_index("c")
    chips = [(1 - mx, my), (mx, 1 - my), (1 - mx, 1 - my)]
    return mx, my, mc, chips


def _all_gather8(buf, name):
    def body(in_ref, out_ref, send_sems, recv_sems):
        mx, my, mc, chips = _place()
        me, sibling = (mx, my, mc), (mx, my, 1 - mc)

        def slot(ref, px, py, pc):
            return ref.at[4 * px + 2 * py + pc]

        def copy(k, block, to, src_ref=out_ref):
            return pltpu.make_async_remote_copy(
                src_ref=slot(src_ref, *block), dst_ref=slot(out_ref, *block),
                send_sem=send_sems.at[k], recv_sem=recv_sems.at[k], device_id=to, device_id_type=_MESH)

        first = [copy(0, me, sibling, in_ref)]
        first += [copy(1 + j, me, (*chip, mc), in_ref) for j, chip in enumerate(chips)]
        for cp in first:
            cp.start()
        passed = [copy(4 + j, (*chip, mc), sibling) for j, chip in enumerate(chips)]
        for j, chip in enumerate(chips):
            copy(1 + j, (*chip, mc), me).wait_recv()
            passed[j].start()
        copy(0, sibling, me).wait_recv()
        for j, chip in enumerate(chips):
            copy(4 + j, (*chip, 1 - mc), me).wait_recv()
        for cp in first + passed:
            cp.wait_send()

    return pl.pallas_call(
        body, name=name, out_shape=jax.ShapeDtypeStruct(buf.shape, buf.dtype),
        in_specs=[_ANY], out_specs=_ANY, input_output_aliases={0: 0},
        scratch_shapes=[pltpu.SemaphoreType.DMA((7,)), pltpu.SemaphoreType.DMA((7,))],
    )(buf)


def _own_slot(x, dev):
    return lax.dynamic_update_slice(jnp.zeros((8,) + x.shape, x.dtype), x[None], (dev, 0, 0))


def _cast_into_slot(w, sel, after, name):
    r, c = w.shape
    hr = r // 2
    tr = _tile(hr, 256, 16)
    nr = hr // tr

    def body(sel_ref, w_ref, after_ref, o_ref):
        o_ref[...] = w_ref[...].astype(o_ref.dtype)

    return pl.pallas_call(
        body, name=name, out_shape=jax.ShapeDtypeStruct((4, 2, hr, c), _WIRE),
        grid_spec=pltpu.PrefetchScalarGridSpec(
            num_scalar_prefetch=1, grid=(2, nr),
            in_specs=[pl.BlockSpec((tr, c), lambda h, i, s: (h * nr + i, 0)), _ANY],
            out_specs=pl.BlockSpec((None, None, tr, c), lambda h, i, s: (s[0], h, i, 0))),
        compiler_params=_cp("parallel", "parallel"),
    )(sel, w, after)


def _hbm(a):
    return pltpu.with_memory_space_constraint(a, pltpu.HBM)


def _after(vec, token):
    return vec + token[0:1, 0:1]


def _gather_start(bufs, after, name):
    n = len(bufs)
    nc = 3 * n

    def body(*refs):
        ins, send, recv, token = refs[:n], refs[n + 1:n + 1 + nc], refs[n + 1 + nc:n + 1 + 2 * nc], refs[-1]
        mx, my, mc, chips = _place()
        j_me = 2 * mx + my
        for i in range(n):
            for k, chip in enumerate(chips):
                half = ins[i].at[j_me, mc]
                pltpu.make_async_remote_copy(
                    src_ref=half, dst_ref=half, send_sem=send[3 * i + k], recv_sem=recv[3 * i + k],
                    device_id=(*chip, mc), device_id_type=_MESH).start()
        token[...] = jnp.zeros_like(token)

    outs = pl.pallas_call(
        body, name=name,
        out_shape=tuple([pltpu.SemaphoreType.DMA(())] * (2 * nc) + [pltpu.HBM(b.shape, b.dtype) for b in bufs]
                        + [jax.ShapeDtypeStruct((SUBLANES, LANES), _F32)]),
        in_specs=tuple([_HBM] * n + [_ANY]), out_specs=tuple([_SEM] * (2 * nc) + [_HBM] * n + [_VMEM_WHOLE]),
        input_output_aliases={i: 2 * nc + i for i in range(n)},
        compiler_params=pltpu.CompilerParams(has_side_effects=_EFFECT),
    )(*[_hbm(b) for b in bufs], after)
    sems = [(outs[3 * i:3 * i + 3], outs[nc + 3 * i:nc + 3 * i + 3]) for i in range(n)]
    return sems, list(outs[2 * nc:2 * nc + n]), outs[-1]


def _gather_wait(sems, buf, after, name):
    send, recv = sems

    def body(buf_ref, s0, s1, s2, r0, r1, r2, after_ref, out_ref):
        mx, my, mc, chips = _place()
        j_me = 2 * mx + my
        for k, (chip, s_k, r_k) in enumerate(zip(chips, (s0, s1, s2), (r0, r1, r2))):
            cp = pltpu.make_async_remote_copy(
                src_ref=buf_ref.at[j_me, mc], dst_ref=buf_ref.at[2 * chip[0] + chip[1], mc], send_sem=s_k, recv_sem=r_k,
                device_id=(*chip, mc), device_id_type=_MESH)
            cp.wait_send()
            cp.wait_recv()

    return pl.pallas_call(
        body, name=name, out_shape=pltpu.HBM(buf.shape, buf.dtype),
        in_specs=(_HBM,) + (_SEM,) * 6 + (_ANY,), out_specs=_HBM, input_output_aliases={0: 0},
        compiler_params=pltpu.CompilerParams(has_side_effects=_EFFECT),
    )(buf, *send, *recv, after)


def _pair_forward(bufs, name):
    n = len(bufs)

    def body(*refs):
        ins, outs = refs[:n], refs[n:2 * n]
        send_sems, recv_sems = refs[2 * n:]
        mx, my, mc, chips = _place()
        sibling = (mx, my, 1 - mc)
        cps = []
        for i in range(n):
            for k, chip in enumerate(chips):
                j_k = 2 * chip[0] + chip[1]
                cp = pltpu.make_async_remote_copy(
                    src_ref=ins[i].at[j_k, mc], dst_ref=outs[i].at[j_k, mc], send_sem=send_sems.at[3 * i + k],
                    recv_sem=recv_sems.at[3 * i + k], device_id=sibling, device_id_type=_MESH)
                cp.start()
                cps.append(cp)
        for i in range(n):
            for k, chip in enumerate(chips):
                other = outs[i].at[2 * chip[0] + chip[1], 1 - mc]
                pltpu.make_async_remote_copy(
                    src_ref=other, dst_ref=other, send_sem=send_sems.at[3 * i + k], recv_sem=recv_sems.at[3 * i + k],
                    device_id=sibling, device_id_type=_MESH).wait_recv()
        for cp in cps:
            cp.wait_send()

    return pl.pallas_call(
        body, name=name, out_shape=[jax.ShapeDtypeStruct(b.shape, b.dtype) for b in bufs],
        in_specs=[_ANY] * n, out_specs=[_ANY] * n, input_output_aliases={i: i for i in range(n)},
        scratch_shapes=[pltpu.SemaphoreType.DMA((3 * n,)), pltpu.SemaphoreType.DMA((3 * n,))],
    )(*bufs)


def _gather8_peers(buf_ref, mx, my, mc, chips):
    mine = buf_ref.at[4 * mx + 2 * my + mc]
    peers = [((mx, my, 1 - mc), mine, buf_ref.at[4 * mx + 2 * my + 1 - mc])]
    peers += [((*chip, mc), mine, buf_ref.at[4 * chip[0] + 2 * chip[1] + mc]) for chip in chips]
    return peers


def _gather8_start(buf, name):
    def body(buf_ref, *rest):
        send, recv, token = rest[0:4], rest[4:8], rest[-1]
        mx, my, mc, chips = _place()
        for k, (peer, src, _) in enumerate(_gather8_peers(buf_ref, mx, my, mc, chips)):
            pltpu.make_async_remote_copy(src_ref=src, dst_ref=src, send_sem=send[k], recv_sem=recv[k],
                                         device_id=peer, device_id_type=_MESH).start()
        token[...] = jnp.zeros_like(token)

    outs = pl.pallas_call(
        body, name=name,
        out_shape=tuple([pltpu.SemaphoreType.DMA(())] * 8 + [pltpu.HBM(buf.shape, buf.dtype),
                                                             jax.ShapeDtypeStruct((SUBLANES, LANES), _F32)]),
        in_specs=(_HBM,), out_specs=tuple([_SEM] * 8 + [_HBM, _VMEM_WHOLE]), input_output_aliases={0: 8},
        compiler_params=pltpu.CompilerParams(has_side_effects=_EFFECT),
    )(_hbm(buf))
    return (outs[0:4], outs[4:8]), outs[8], outs[9]


def _gather8_wait(sems, buf, after, name):
    send, recv = sems

    def body(buf_ref, s0, s1, s2, s3, r0, r1, r2, r3, after_ref, out_ref):
        mx, my, mc, chips = _place()
        for (peer, src, dst), s_k, r_k in zip(_gather8_peers(buf_ref, mx, my, mc, chips), (s0, s1, s2, s3), (r0, r1, r2, r3)):
            cp = pltpu.make_async_remote_copy(src_ref=src, dst_ref=dst, send_sem=s_k, recv_sem=r_k,
                                              device_id=peer, device_id_type=_MESH)
            cp.wait_send()
            cp.wait_recv()

    return pl.pallas_call(
        body, name=name, out_shape=pltpu.HBM(buf.shape, buf.dtype),
        in_specs=(_HBM,) + (_SEM,) * 8 + (_ANY,), out_specs=_HBM, input_output_aliases={0: 0},
        compiler_params=pltpu.CompilerParams(has_side_effects=_EFFECT),
    )(buf, *send, *recv, after)


def _gather8_forward(buf, name):
    def body(in_ref, out_ref, send_sems, recv_sems):
        mx, my, mc, chips = _place()
        sibling = (mx, my, 1 - mc)
        cps = []
        for k, chip in enumerate(chips):
            idx = 4 * chip[0] + 2 * chip[1] + mc
            cp = pltpu.make_async_remote_copy(src_ref=in_ref.at[idx], dst_ref=out_ref.at[idx], send_sem=send_sems.at[k],
                                              recv_sem=recv_sems.at[k], device_id=sibling, device_id_type=_MESH)
            cp.start()
            cps.append(cp)
        for k, chip in enumerate(chips):
            other = out_ref.at[4 * chip[0] + 2 * chip[1] + 1 - mc]
            pltpu.make_async_remote_copy(src_ref=other, dst_ref=other, send_sem=send_sems.at[k], recv_sem=recv_sems.at[k],
                                         device_id=sibling, device_id_type=_MESH).wait_recv()
        for cp in cps:
            cp.wait_send()

    return pl.pallas_call(
        body, name=name, out_shape=jax.ShapeDtypeStruct(buf.shape, buf.dtype),
        in_specs=[_ANY], out_specs=_ANY, input_output_aliases={0: 0},
        scratch_shapes=[pltpu.SemaphoreType.DMA((3,)), pltpu.SemaphoreType.DMA((3,))],
    )(buf)


def _scatter_start(pair, name):
    land = lax.empty((3,) + pair.shape[1:], pair.dtype)

    def body(pair_ref, land_ref, s0, s1, s2, r0, r1, r2, pair_thru, land_thru, token):
        mx, my, mc, chips = _place()
        for k, (chip, s_k, r_k) in enumerate(zip(chips, (s0, s1, s2), (r0, r1, r2))):
            pltpu.make_async_remote_copy(
                src_ref=pair_ref.at[2 * chip[0] + chip[1]], dst_ref=land_ref.at[k], send_sem=s_k, recv_sem=r_k,
                device_id=(*chip, mc), device_id_type=_MESH).start()
        token[...] = jnp.zeros_like(token)

    outs = pl.pallas_call(
        body, name=name,
        out_shape=tuple([pltpu.SemaphoreType.DMA(())] * 6 + [pltpu.HBM(pair.shape, pair.dtype), pltpu.HBM(land.shape, land.dtype),
                                                             jax.ShapeDtypeStruct((SUBLANES, LANES), _F32)]),
        in_specs=(_HBM, _HBM), out_specs=tuple([_SEM] * 6 + [_HBM, _HBM, _VMEM_WHOLE]),
        input_output_aliases={0: 6, 1: 7}, compiler_params=pltpu.CompilerParams(has_side_effects=_EFFECT),
    )(_hbm(pair), _hbm(land))
    return (outs[0:3], outs[3:6]), outs[6], outs[7], outs[8]


def _scatter_wait(sems, pair, land, after, name):
    send, recv = sems

    def body(pair_ref, land_ref, s0, s1, s2, r0, r1, r2, after_ref, pair_out, land_out):
        mx, my, mc, chips = _place()
        for k, (chip, s_k, r_k) in enumerate(zip(chips, (s0, s1, s2), (r0, r1, r2))):
            cp = pltpu.make_async_remote_copy(
                src_ref=pair_ref.at[2 * chip[0] + chip[1]], dst_ref=land_ref.at[k], send_sem=s_k, recv_sem=r_k,
                device_id=(*chip, mc), device_id_type=_MESH)
            cp.wait_send()
            cp.wait_recv()

    return pl.pallas_call(
        body, name=name, out_shape=(pltpu.HBM(pair.shape, pair.dtype), pltpu.HBM(land.shape, land.dtype)),
        in_specs=(_HBM, _HBM) + (_SEM,) * 6 + (_ANY,), out_specs=(_HBM, _HBM), input_output_aliases={0: 0, 1: 1},
        compiler_params=pltpu.CompilerParams(has_side_effects=_EFFECT),
    )(pair, land, *send, *recv, after)


def _sibling_copy(src_ref, dst_ref, send_sem, recv_sem):
    mx, my, mc, _ = _place()
    return pltpu.make_async_remote_copy(src_ref=src_ref, dst_ref=dst_ref, send_sem=send_sem, recv_sem=recv_sem,
                                        device_id=(mx, my, 1 - mc), device_id_type=_MESH)


def _swap_start(g, name):
    land = lax.empty(g.shape[1:], g.dtype)

    def body(g_ref, land_ref, send_sem, recv_sem, g_thru, land_thru, token):
        _sibling_copy(g_ref.at[1 - lax.axis_index("c")], land_ref, send_sem, recv_sem).start()
        token[...] = jnp.zeros_like(token)

    outs = pl.pallas_call(
        body, name=name,
        out_shape=(pltpu.SemaphoreType.DMA(()), pltpu.SemaphoreType.DMA(()), pltpu.HBM(g.shape, g.dtype),
                   pltpu.HBM(land.shape, land.dtype), jax.ShapeDtypeStruct((SUBLANES, LANES), _F32)),
        in_specs=(_HBM, _HBM), out_specs=(_SEM, _SEM, _HBM, _HBM, _VMEM_WHOLE), input_output_aliases={0: 2, 1: 3},
        compiler_params=pltpu.CompilerParams(has_side_effects=_EFFECT),
    )(_hbm(g), _hbm(land))
    return (outs[0], outs[1]), outs[2], outs[3], outs[4]


def _swap_wait(sems, g, land, after, name):
    def body(g_ref, land_ref, send_sem, recv_sem, after_ref, g_out, land_out):
        cp = _sibling_copy(g_ref.at[1 - lax.axis_index("c")], land_ref, send_sem, recv_sem)
        cp.wait_send()
        cp.wait_recv()

    return pl.pallas_call(
        body, name=name, out_shape=(pltpu.HBM(g.shape, g.dtype), pltpu.HBM(land.shape, land.dtype)),
        in_specs=(_HBM, _HBM, _SEM, _SEM, _ANY), out_specs=(_HBM, _HBM), input_output_aliases={0: 0, 1: 1},
        compiler_params=pltpu.CompilerParams(has_side_effects=_EFFECT),
    )(g, land, *sems, after)


def _join_start(buf, name):
    def body(buf_ref, send_sem, recv_sem, buf_thru, token):
        mine = buf_ref.at[lax.axis_index("c")]
        _sibling_copy(mine, mine, send_sem, recv_sem).start()
        token[...] = jnp.zeros_like(token)

    outs = pl.pallas_call(
        body, name=name,
        out_shape=(pltpu.SemaphoreType.DMA(()), pltpu.SemaphoreType.DMA(()), pltpu.HBM(buf.shape, buf.dtype),
                   jax.ShapeDtypeStruct((SUBLANES, LANES), _F32)),
        in_specs=(_HBM,), out_specs=(_SEM, _SEM, _HBM, _VMEM_WHOLE), input_output_aliases={0: 2},
        compiler_params=pltpu.CompilerParams(has_side_effects=_EFFECT),
    )(_hbm(buf))
    return (outs[0], outs[1]), outs[2], outs[3]


def _join_wait(sems, buf, after, name):
    def body(buf_ref, send_sem, recv_sem, after_ref, buf_out):
        mc = lax.axis_index("c")
        cp = _sibling_copy(buf_ref.at[mc], buf_ref.at[1 - mc], send_sem, recv_sem)
        cp.wait_send()
        cp.wait_recv()

    return pl.pallas_call(
        body, name=name, out_shape=pltpu.HBM(buf.shape, buf.dtype),
        in_specs=(_HBM, _SEM, _SEM, _ANY), out_specs=_HBM, input_output_aliases={0: 0},
        compiler_params=pltpu.CompilerParams(has_side_effects=_EFFECT),
    )(buf, *sems, after)


def _pair_sum(g, got, sel, name):
    _, four, hr, c = g.shape
    tr = _tile(hr, 512, 16)

    def body(sel_ref, g_ref, p_ref, o_ref):
        o_ref[...] = (g_ref[...].astype(_F32) + p_ref[...].astype(_F32)).astype(o_ref.dtype)

    return pl.pallas_call(
        body, name=name, out_shape=jax.ShapeDtypeStruct((four, hr, c), g.dtype),
        grid_spec=pltpu.PrefetchScalarGridSpec(
            num_scalar_prefetch=1, grid=(four, hr // tr),
            in_specs=[pl.BlockSpec((None, None, tr, c), lambda j, i, s: (s[1], j, i, 0)),
                      pl.BlockSpec((None, tr, c), lambda j, i, s: (j, i, 0))],
            out_specs=pl.BlockSpec((None, tr, c), lambda j, i, s: (j, i, 0))),
        compiler_params=_cp("parallel", "parallel"),
    )(sel, g, got)


def _chip_sum(pair, got, sel, name):
    _, hr, c = pair.shape
    tr = _tile(hr, 512, 16)

    def body(sel_ref, p_ref, q_ref, o_ref):
        o_ref[...] = ((p_ref[...].astype(_F32) + q_ref[0].astype(_F32)) + q_ref[1].astype(_F32)) + q_ref[2].astype(_F32)

    return pl.pallas_call(
        body, name=name, out_shape=jax.ShapeDtypeStruct((2, hr, c), _F32),
        grid_spec=pltpu.PrefetchScalarGridSpec(
            num_scalar_prefetch=1, grid=(hr // tr,),
            in_specs=[pl.BlockSpec((None, tr, c), lambda i, s: (s[0], i, 0)),
                      pl.BlockSpec((3, tr, c), lambda i, s: (0, i, 0))],
            out_specs=pl.BlockSpec((None, tr, c), lambda i, s: (s[1], i, 0))),
        compiler_params=_cp("parallel"),
    )(sel, pair, got)


def _matmul(a, b, dims, out_struct, grid, a_spec, b_spec, o_spec, acc_shape, k_axis, name, after=None):
    nk = grid[k_axis]
    extra = [] if after is None else [after]

    def body(a_ref, b_ref, *rest):
        o_ref, acc = rest[len(extra)], rest[len(extra) + 1:]
        prod = lax.dot_general(a_ref[...].astype(_MXU), b_ref[...].astype(_MXU), dims, preferred_element_type=_F32)
        if nk == 1:
            o_ref[...] = prod.astype(o_ref.dtype)
        else:
            acc_ref, = acc
            k = pl.program_id(k_axis)

            @pl.when(k == 0)
            def _():
                acc_ref[...] = prod

            @pl.when(jnp.logical_and(k > 0, k < nk - 1))
            def _():
                acc_ref[...] += prod

            @pl.when(k == nk - 1)
            def _():
                o_ref[...] = (acc_ref[...] + prod).astype(o_ref.dtype)

    sem = ["parallel"] * len(grid)
    sem[k_axis] = "arbitrary"
    return pl.pallas_call(
        body, name=name, out_shape=out_struct, grid=grid, in_specs=[a_spec, b_spec] + [_ANY] * len(extra), out_specs=o_spec,
        scratch_shapes=[pltpu.VMEM(acc_shape, _F32)] if nk > 1 else [], compiler_params=_cp(*sem),
    )(a, b, *extra)


def _mm_nn(a, w4, out_dtype, name, tm=512, tn=1536, tk=2048, after=None):
    m, k = a.shape
    j, _, ns = w4.shape
    tm, tn, tk = _tile(m, tm, 16), _tile(ns, tn), _tile(k, tk)
    nps = ns // tn
    return _matmul(
        a, w4, (((1,), (0,)), ((), ())), jax.ShapeDtypeStruct((m, j * ns), out_dtype),
        (m // tm, j * nps, k // tk),
        pl.BlockSpec((tm, tk), lambda mi, ni, ki: (mi, ki)),
        pl.BlockSpec((None, tk, tn), lambda mi, ni, ki: (ni // nps, ki, ni % nps)),
        pl.BlockSpec((tm, tn), lambda mi, ni, ki: (mi, ni)), (tm, tn), 2, name, after)


def _mm_nt(a, w4, out_dtype, name, tm=512, tn=2048, tk=1536, after=None):
    m = a.shape[-2]
    j, kw, ns = w4.shape
    tm, tn, tk = _tile(m, tm, 16), _tile(kw, tn), _tile(ns, tk)
    kps = ns // tk
    if a.ndim == 3:
        kph = a.shape[2] // tk
        a_spec = pl.BlockSpec((None, tm, tk), lambda mi, ni, ki: (ki // kph, mi, ki % kph))
    else:
        a_spec = pl.BlockSpec((tm, tk), lambda mi, ni, ki: (mi, ki))
    return _matmul(
        a, w4, (((1,), (1,)), ((), ())), jax.ShapeDtypeStruct((m, kw), out_dtype),
        (m // tm, kw // tn, j * kps),
        a_spec,
        pl.BlockSpec((None, tn, tk), lambda mi, ni, ki: (ki // kps, ni, ki % kps)),
        pl.BlockSpec((tm, tn), lambda mi, ni, ki: (mi, ni)), (tm, tn), 2, name, after)


def _mm_tn_cols(a, b, name, tm=1024, tn=1536, tk=2048):
    m, ka = a.shape
    ns = (b.shape[-1] * (2 if b.ndim == 3 else 1)) // 4
    hr = ka // 2
    tm, tn, tk = _tile(hr, tm), _tile(ns, tn), _tile(m, tk, 16)
    mph, nps = hr // tm, ns // tn
    if b.ndim == 3:
        b_spec = pl.BlockSpec((None, tk, tn), lambda ni, mi, ki: (ni // (2 * nps), ki, ni % (2 * nps)))
    else:
        b_spec = pl.BlockSpec((tk, tn), lambda ni, mi, ki: (ki, ni))
    return _matmul(
        a, b, (((0,), (0,)), ((), ())), jax.ShapeDtypeStruct((2, 4, hr, ns), _WIRE),
        (4 * nps, 2 * mph, m // tk),
        pl.BlockSpec((tk, tm), lambda ni, mi, ki: (ki, mi)),
        b_spec,
        pl.BlockSpec((None, None, tm, tn), lambda ni, mi, ki: (mi // mph, ni // nps, mi % mph, ni % nps)),
        (tm, tn), 2, name)


def _mm_tn_rows(a, b, name, tm=1536, tn=1024, tk=2048):
    m, ka = a.shape
    r = ka // 4
    hc = b.shape[1] // 2
    tm, tn, tk = _tile(r, tm), _tile(hc, tn), _tile(m, tk, 16)
    mpr, nph = r // tm, hc // tn
    return _matmul(
        a, b, (((0,), (0,)), ((), ())), jax.ShapeDtypeStruct((2, 4, r, hc), _WIRE),
        (2 * nph, 4 * mpr, m // tk),
        pl.BlockSpec((tk, tm), lambda ni, mi, ki: (ki, mi)),
        pl.BlockSpec((tk, tn), lambda ni, mi, ki: (ki, ni)),
        pl.BlockSpec((None, None, tm, tn), lambda ni, mi, ki: (ni // nph, mi // mpr, mi % mpr, ni % nph)),
        (tm, tn), 2, name)


def _row_call(body, name, rows, ins, outs, tm=256):
    tm = _tile(rows, tm, 16)

    def spec(shape, kind):
        if kind == "rows":
            return pl.BlockSpec((tm, shape[1]), lambda i: (i, 0))
        return pl.BlockSpec(shape, lambda i: (0,) * len(shape))

    return pl.pallas_call(
        body, name=name, grid=(rows // tm,),
        in_specs=[spec(a.shape, kind) for a, kind in ins],
        out_specs=[spec(o.shape, kind) for o, kind in outs],
        out_shape=[o for o, _ in outs],
        compiler_params=_cp("arbitrary"),
    )(*[a for a, _ in ins])


def _rms(x):
    r = lax.rsqrt(_rowmean(x * x) + EPS)
    return x * r, r


def _rms_bwd(dxh, xh, r):
    return r * (dxh - xh * _rowmean(dxh * xh))


def _fwd_pre_mix(x, g, sc, sh):
    s, d = x.shape

    def body(x_ref, g_ref, sc_ref, sh_ref, h_ref):
        xh, _ = _rms(x_ref[...])
        h_ref[...] = (xh * g_ref[...] * (1.0 + sc_ref[...]) + sh_ref[...]).astype(h_ref.dtype)

    return _row_call(body, "fwd_pre_mix", s, [(x, "rows"), (g, "vec"), (sc, "vec"), (sh, "vec")],
                     [(jax.ShapeDtypeStruct((s, d), _MXU), "rows")])[0]


def _fwd_mid(o, x, gt1, g_post, g_pre2, sc2, sh2):
    s, d = x.shape

    def body(o_ref, x_ref, gt_ref, gp_ref, g2_ref, sc_ref, sh_ref, x1_ref, h2_ref):
        oh, _ = _rms(o_ref[...])
        x1 = x_ref[...] + gt_ref[...] * (oh * gp_ref[...])
        x1_ref[...] = x1
        xh, _ = _rms(x1)
        h2_ref[...] = (xh * g2_ref[...] * (1.0 + sc_ref[...]) + sh_ref[...]).astype(h2_ref.dtype)

    return _row_call(body, "fwd_mid", s,
                     [(o, "rows"), (x, "rows"), (gt1, "vec"), (g_post, "vec"), (g_pre2, "vec"), (sc2, "vec"),
                      (sh2, "vec")],
                     [(jax.ShapeDtypeStruct((s, d), _F32), "rows"), (jax.ShapeDtypeStruct((s, d), _MXU), "rows")])


def _loss_and_post_ffn_bwd(f, x1, tgt, gt2, g_post):
    s, d = x1.shape

    def body(f_ref, x1_ref, t_ref, gt_ref, g_ref, dx2_ref, df_ref, dgt_ref, dg_ref, loss_ref):
        first = pl.program_id(0) == 0
        fh, r = _rms(f_ref[...])
        n = fh * g_ref[...]
        e = x1_ref[...] + gt_ref[...] * n - t_ref[...]
        _acc(loss_ref, first, jnp.sum(_colsum(e * e), axis=1, keepdims=True) * (0.5 / d))
        dx2 = e * (1.0 / d)
        dx2_ref[...] = dx2
        _acc(dgt_ref, first, _colsum(dx2 * n))
        dn = dx2 * gt_ref[...]
        _acc(dg_ref, first, _colsum(dn * fh))
        df_ref[...] = _rms_bwd(dn * g_ref[...], fh, r).astype(df_ref.dtype)

    vec = jax.ShapeDtypeStruct((1, d), _F32)
    return _row_call(body, "loss_post_ffn_bwd", s,
                     [(f, "rows"), (x1, "rows"), (tgt, "rows"), (gt2, "vec"), (g_post, "vec")],
                     [(jax.ShapeDtypeStruct((s, d), _F32), "rows"), (jax.ShapeDtypeStruct((s, d), _MXU), "rows"),
                      (vec, "vec"), (vec, "vec"), (jax.ShapeDtypeStruct((1, 1), _F32), "vec")])


def _bwd_mid(dh2, x1, dx2, o, g_pre2, sc2, gt1, g_post):
    s, d = x1.shape

    def body(dh_ref, x1_ref, dx2_ref, o_ref, g2_ref, sc_ref, gt_ref, gp_ref,
             dx1_ref, do_ref, dsc_ref, dsh_ref, dg2_ref, dgt_ref, dgp_ref):
        first = pl.program_id(0) == 0
        dh = dh_ref[...]
        xh, r = _rms(x1_ref[...])
        _acc(dsh_ref, first, _colsum(dh))
        _acc(dsc_ref, first, _colsum(dh * (xh * g2_ref[...])))
        dn = dh * (1.0 + sc_ref[...])
        _acc(dg2_ref, first, _colsum(dn * xh))
        dx1 = dx2_ref[...] + _rms_bwd(dn * g2_ref[...], xh, r)
        dx1_ref[...] = dx1
        oh, ro = _rms(o_ref[...])
        _acc(dgt_ref, first, _colsum(dx1 * (oh * gp_ref[...])))
        dno = dx1 * gt_ref[...]
        _acc(dgp_ref, first, _colsum(dno * oh))
        do_ref[...] = _rms_bwd(dno * gp_ref[...], oh, ro).astype(do_ref.dtype)

    vec = jax.ShapeDtypeStruct((1, d), _F32)
    return _row_call(body, "bwd_mid", s,
                     [(dh2, "rows"), (x1, "rows"), (dx2, "rows"), (o, "rows"), (g_pre2, "vec"), (sc2, "vec"),
                      (gt1, "vec"), (g_post, "vec")],
                     [(jax.ShapeDtypeStruct((s, d), _F32), "rows"), (jax.ShapeDtypeStruct((s, d), _MXU), "rows"),
                      (vec, "vec"), (vec, "vec"), (vec, "vec"), (vec, "vec"), (vec, "vec")])


def _bwd_pre_mix(dh1, x, dx1, g, sc1):
    s, d = x.shape

    def body(dh_ref, x_ref, dx1_ref, g_ref, sc_ref, dx_ref, dsc_ref, dsh_ref, dg_ref):
        first = pl.program_id(0) == 0
        dh = dh_ref[...]
        xh, r = _rms(x_ref[...])
        _acc(dsh_ref, first, _colsum(dh))
        _acc(dsc_ref, first, _colsum(dh * (xh * g_ref[...])))
        dn = dh * (1.0 + sc_ref[...])
        _acc(dg_ref, first, _colsum(dn * xh))
        dx_ref[...] = dx1_ref[...] + _rms_bwd(dn * g_ref[...], xh, r)

    vec = jax.ShapeDtypeStruct((1, d), _F32)
    return _row_call(body, "bwd_pre_mix", s,
                     [(dh1, "rows"), (x, "rows"), (dx1, "rows"), (g, "vec"), (sc1, "vec")],
                     [(jax.ShapeDtypeStruct((s, d), _F32), "rows"), (vec, "vec"), (vec, "vec"), (vec, "vec")])


def _mix_norm_fwd(y_ssm, y_sgu, g_ssm, g_sgu):
    s, h = y_ssm.shape

    def body(a_ref, b_ref, ga_ref, gb_ref, o_ref):
        ah, _ = _rms(a_ref[...])
        bh, _ = _rms(b_ref[...])
        o_ref[:, 0:h] = (ah * ga_ref[...]).astype(o_ref.dtype)
        o_ref[:, h:2 * h] = (bh * gb_ref[...]).astype(o_ref.dtype)

    return _row_call(body, "mix_norm_fwd", s, [(y_ssm, "rows"), (y_sgu, "rows"), (g_ssm, "vec"), (g_sgu, "vec")],
                     [(jax.ShapeDtypeStruct((s, 2 * h), _MXU), "rows")])[0]


def _mix_norm_bwd(dyc, y_ssm, y_sgu, g_ssm, g_sgu):
    s, h = y_ssm.shape

    def body(d_ref, a_ref, b_ref, ga_ref, gb_ref, da_ref, db_ref, dga_ref, dgb_ref):
        first = pl.program_id(0) == 0
        for lo, y_ref, g_ref, dy_ref, dg_ref in ((0, a_ref, ga_ref, da_ref, dga_ref), (h, b_ref, gb_ref, db_ref, dgb_ref)):
            d = d_ref[:, lo:lo + h]
            yh, r = _rms(y_ref[...])
            _acc(dg_ref, first, _colsum(d * yh))
            dy_ref[...] = _rms_bwd(d * g_ref[...], yh, r)

    vec = jax.ShapeDtypeStruct((1, h), _F32)
    full = jax.ShapeDtypeStruct((s, h), _F32)
    return _row_call(body, "mix_norm_bwd", s,
                     [(dyc, "rows"), (y_ssm, "rows"), (y_sgu, "rows"), (g_ssm, "vec"), (g_sgu, "vec")],
                     [(full, "rows"), (full, "rows"), (vec, "vec"), (vec, "vec")])


def _shift_down(x, k):
    row = lax.broadcasted_iota(jnp.int32, x.shape, 0)
    return jnp.where(row >= k, pltpu.roll(x, k, 0), 0.0)


def _shift_up(x, k):
    n = x.shape[0]
    row = lax.broadcasted_iota(jnp.int32, x.shape, 0)
    return jnp.where(row < n - k, pltpu.roll(x, n - k, 0), 0.0)


def _conv(x, w_ref, b_ref):
    return b_ref[...] + w_ref[0:1, :] * _shift_down(x, 2) + w_ref[1:2, :] * _shift_down(x, 1) + w_ref[2:3, :] * x


def _conv_act_fwd(up_pre, conv_w, conv_b):
    s, f2 = up_pre.shape
    f = f2 // 2
    tc = _tile(f, 256)
    nf = f // tc

    def body(a_ref, b_ref, wa_ref, wb_ref, ba_ref, bb_ref, o_ref):
        a = _conv(a_ref[...], wa_ref, ba_ref)
        b = _conv(b_ref[...], wb_ref, bb_ref)
        o_ref[...] = (a * _sigmoid(a) * b).astype(o_ref.dtype)

    return pl.pallas_call(
        body, name="conv_act_fwd", grid=(nf,), out_shape=jax.ShapeDtypeStruct((s, f), _MXU),
        in_specs=[pl.BlockSpec((s, tc), lambda n: (0, n)), pl.BlockSpec((s, tc), lambda n: (0, n + nf)),
                  pl.BlockSpec((3, tc), lambda n: (0, n)), pl.BlockSpec((3, tc), lambda n: (0, n + nf)),
                  pl.BlockSpec((1, tc), lambda n: (0, n)), pl.BlockSpec((1, tc), lambda n: (0, n + nf))],
        out_specs=pl.BlockSpec((s, tc), lambda n: (0, n)), compiler_params=_cp("parallel"),
    )(up_pre, up_pre, conv_w, conv_w, conv_b, conv_b)


def _conv_act_bwd(up_pre, d_act, conv_w, conv_b):
    s, f2 = up_pre.shape
    f = f2 // 2
    tc = _tile(f, 256)
    nf = f // tc

    def body(a_ref, b_ref, d_ref, wa_ref, wb_ref, ba_ref, bb_ref,
             du_ref, w0a, w0b, w1a, w1b, w2a, w2b, dba, dbb):
        xa, xb = a_ref[...], b_ref[...]
        a = _conv(xa, wa_ref, ba_ref)
        b = _conv(xb, wb_ref, bb_ref)
        sg = _sigmoid(a)
        d = d_ref[...]
        d_a = d * b * (sg * (1.0 + a * (1.0 - sg)))
        d_b = d * (a * sg)
        for x, du, w_ref, o_ref, o0, o1, o2, ob in ((xa, d_a, wa_ref, du_ref.at[0], w0a, w1a, w2a, dba),
                                                     (xb, d_b, wb_ref, du_ref.at[1], w0b, w1b, w2b, dbb)):
            ob[...] = _colsum(du)
            o0[...] = _colsum(du * _shift_down(x, 2))
            o1[...] = _colsum(du * _shift_down(x, 1))
            o2[...] = _colsum(du * x)
            o_ref[...] = (w_ref[2:3, :] * du + w_ref[1:2, :] * _shift_up(du, 1)
                          + w_ref[0:1, :] * _shift_up(du, 2)).astype(o_ref.dtype)

    col_a = pl.BlockSpec((s, tc), lambda n: (0, n))
    col_b = pl.BlockSpec((s, tc), lambda n: (0, n + nf))
    vec_a = pl.BlockSpec((1, tc), lambda n: (0, n))
    vec_b = pl.BlockSpec((1, tc), lambda n: (0, n + nf))
    vec = jax.ShapeDtypeStruct((1, f), _F32)
    outs = pl.pallas_call(
        body, name="conv_act_bwd", grid=(nf,),
        in_specs=[col_a, col_b, col_a, pl.BlockSpec((3, tc), lambda n: (0, n)),
                  pl.BlockSpec((3, tc), lambda n: (0, n + nf)), vec_a, vec_b],
        out_specs=[pl.BlockSpec((2, s, tc), lambda n: (0, 0, n))] + [vec_a] * 8,
        out_shape=[jax.ShapeDtypeStruct((2, s, f), _MXU)] + [vec] * 8, compiler_params=_cp("parallel"),
    )(up_pre, up_pre, d_act, conv_w, conv_w, conv_b, conv_b)
    du, w0a, w0b, w1a, w1b, w2a, w2b, dba, dbb = outs
    cat = lambda p, q: jnp.concatenate([p, q], axis=1)
    return du, cat(w0a, w0b), cat(w1a, w1b), cat(w2a, w2b), cat(dba, dbb)


def _sgu_recompute(zu_ref, zv_ref, lng_ref, lnb_ref, wm_ref, bs_ref, nh):
    zu, zv = zu_ref[...], zv_ref[...]
    u = _gelu(zu)
    gv = _gelu(zv)
    xc = gv - _rowmean(gv)
    rs = lax.rsqrt(_rowmean(xc * xc) + EPS)
    vh = xc * rs
    v = vh * lng_ref[...] + lnb_ref[...]
    mixed = []
    for h in range(nh):
        vhd = v[:, h * CHUNK:(h + 1) * CHUNK].astype(_MXU)
        mixed.append(jnp.dot(wm_ref[h].astype(_MXU), vhd, preferred_element_type=_F32) + bs_ref[h])
    return zu, zv, u, vh, rs, v, mixed


def _sgu_fwd(z, ln_g, ln_b, wm, bs):
    s = z.shape[0]
    nh = wm.shape[0]
    hd = nh * CHUNK

    def body(zu_ref, zv_ref, lng_ref, lnb_ref, wm_ref, bs_ref, y_ref):
        _, _, u, _, _, _, mixed = _sgu_recompute(zu_ref, zv_ref, lng_ref, lnb_ref, wm_ref, bs_ref, nh)
        for h in range(nh):
            y_ref[:, h * CHUNK:(h + 1) * CHUNK] = u[:, h * CHUNK:(h + 1) * CHUNK] * mixed[h]

    vec = pl.BlockSpec((1, hd), lambda i: (0, 0))
    return pl.pallas_call(
        body, name="sgu_fwd", grid=(s // CHUNK,), out_shape=jax.ShapeDtypeStruct((s, hd), _F32),
        in_specs=[pl.BlockSpec((CHUNK, hd), lambda i: (i, 1)), pl.BlockSpec((CHUNK, hd), lambda i: (i, 2)), vec, vec,
                  pl.BlockSpec((nh, CHUNK, CHUNK), lambda i: (0, 0, 0)), pl.BlockSpec((nh, CHUNK, 1), lambda i: (0, 0, 0))],
        out_specs=pl.BlockSpec((CHUNK, hd), lambda i: (i, 0)), compiler_params=_cp("parallel"),
    )(z, z, ln_g, ln_b, wm, bs)


def _sgu_bwd(z, dy, dz_ssm, ln_g, ln_b, wm, bs):
    s = z.shape[0]
    nh = wm.shape[0]
    hd = nh * CHUNK

    def body(zu_ref, zv_ref, dy_ref, dzs_ref, lng_ref, lnb_ref, wm_ref, bs_ref,
             dz_ref, dlg_ref, dlb_ref, dwm_ref, dbs_ref, dv_scr):
        first = pl.program_id(0) == 0
        zu, zv, u, vh, rs, v, mixed = _sgu_recompute(zu_ref, zv_ref, lng_ref, lnb_ref, wm_ref, bs_ref, nh)
        dy = dy_ref[...]
        dz_ref[:, 0:hd] = dzs_ref[...]
        for h in range(nh):
            cols = slice(h * CHUNK, (h + 1) * CHUNK)
            dyh = dy[:, cols]
            dz_ref[:, hd + h * CHUNK:hd + (h + 1) * CHUNK] = (dyh * mixed[h] * _gelu_grad(zu[:, cols])).astype(dz_ref.dtype)
            dm = dyh * u[:, cols]
            dmx = dm.astype(_MXU)
            _acc(dbs_ref.at[h], first, jnp.sum(dm, axis=1, keepdims=True))
            _acc(dwm_ref.at[h], first,
                 lax.dot_general(dmx, v[:, cols].astype(_MXU), (((1,), (1,)), ((), ())), preferred_element_type=_F32))
            dv_scr[:, cols] = lax.dot_general(wm_ref[h].astype(_MXU), dmx, (((0,), (0,)), ((), ())),
                                              preferred_element_type=_F32)
        dv = dv_scr[...]
        _acc(dlg_ref, first, _colsum(dv * vh))
        _acc(dlb_ref, first, _colsum(dv))
        dvh = dv * lng_ref[...]
        dgv = rs * (dvh - _rowmean(dvh) - vh * _rowmean(dvh * vh))
        dz_ref[:, 2 * hd:3 * hd] = (dgv * _gelu_grad(zv)).astype(dz_ref.dtype)

    vec = pl.BlockSpec((1, hd), lambda i: (0, 0))
    wspec = pl.BlockSpec((nh, CHUNK, CHUNK), lambda i: (0, 0, 0))
    bspec = pl.BlockSpec((nh, CHUNK, 1), lambda i: (0, 0, 0))
    rows = pl.BlockSpec((CHUNK, hd), lambda i: (i, 0))
    return pl.pallas_call(
        body, name="sgu_bwd", grid=(s // CHUNK,),
        out_shape=[jax.ShapeDtypeStruct((s, 3 * hd), _MXU), jax.ShapeDtypeStruct((1, hd), _F32),
                   jax.ShapeDtypeStruct((1, hd), _F32), jax.ShapeDtypeStruct((nh, CHUNK, CHUNK), _F32),
                   jax.ShapeDtypeStruct((nh, CHUNK, 1), _F32)],
        in_specs=[pl.BlockSpec((CHUNK, hd), lambda i: (i, 1)), pl.BlockSpec((CHUNK, hd), lambda i: (i, 2)),
                  rows, rows, vec, vec, wspec, bspec],
        out_specs=[pl.BlockSpec((CHUNK, 3 * hd), lambda i: (i, 0)), vec, vec, wspec, bspec],
        scratch_shapes=[pltpu.VMEM((CHUNK, hd), _F32)], compiler_params=_cp("arbitrary"),
    )(z, z, dy, dz_ssm, ln_g, ln_b, wm, bs)


def _ssm_prep(log_dt, a_re, a_im, b_re_t, b_im_t):
    gn = a_re.shape[1]

    def body(ldt_ref, are_ref, aim_ref, br_ref, bi_ref, pr_ref, pi_ref, bbr_ref, bbi_ref):
        dt = jnp.exp(ldt_ref[...])
        are, aim = are_ref[...], aim_ref[...]
        k = (lax.broadcasted_iota(jnp.int32, (SUBLANES, gn), 0) + 1).astype(_F32)
        mag = jnp.exp(k * (are * dt))
        ang = k * (aim * dt)
        pr_ref[...] = mag * jnp.cos(ang)
        pi_ref[...] = mag * jnp.sin(ang)
        m1 = jnp.exp(are * dt)
        lr, li = m1 * jnp.cos(aim * dt), m1 * jnp.sin(aim * dt)
        den = are * are + aim * aim
        nr = lr - 1.0
        f_re = (nr * are + li * aim) / den
        f_im = (li * are - nr * aim) / den
        bbr_ref[...] = f_re * br_ref[...] - f_im * bi_ref[...]
        bbi_ref[...] = f_re * bi_ref[...] + f_im * br_ref[...]

    pw = jax.ShapeDtypeStruct((SUBLANES, gn), _F32)
    bb = jax.ShapeDtypeStruct(b_re_t.shape, _F32)
    return pl.pallas_call(body, name="ssm_prep", out_shape=[pw, pw, bb, bb])(log_dt, a_re, a_im, b_re_t, b_im_t)


def _ssm_prep_bwd(log_dt, a_re, a_im, b_re_t, b_im_t, d_bbr, d_bbi, d_lr, d_li):
    def body(ldt_ref, are_ref, aim_ref, br_ref, bi_ref, dbr_ref, dbi_ref, dlr_ref, dli_ref,
             obr_ref, obi_ref, oar_ref, oai_ref, odt_ref):
        dt = jnp.exp(ldt_ref[...])
        are, aim = are_ref[...], aim_ref[...]
        m1 = jnp.exp(are * dt)
        lr, li = m1 * jnp.cos(aim * dt), m1 * jnp.sin(aim * dt)
        den = are * are + aim * aim
        nr = lr - 1.0
        f_re = (nr * are + li * aim) / den
        f_im = (li * are - nr * aim) / den
        br, bi, dbr, dbi = br_ref[...], bi_ref[...], dbr_ref[...], dbi_ref[...]
        obr_ref[...] = f_re * dbr + f_im * dbi
        obi_ref[...] = f_re * dbi - f_im * dbr
        gf_re = _colsum(br * dbr + bi * dbi)
        gf_im = _colsum(br * dbi - bi * dbr)
        il_re, il_im = are / den, -aim / den
        glb_re = dlr_ref[...] + (il_re * gf_re + il_im * gf_im)
        glb_im = dli_ref[...] + (il_re * gf_im - il_im * gf_re)
        q_re = -(f_re * il_re - f_im * il_im)
        q_im = -(f_re * il_im + f_im * il_re)
        gl_re = q_re * gf_re + q_im * gf_im
        gl_im = q_re * gf_im - q_im * gf_re
        gl_re = gl_re + dt * (lr * glb_re + li * glb_im)
        gl_im = gl_im + dt * (lr * glb_im - li * glb_re)
        w_re = are * lr - aim * li
        w_im = are * li + aim * lr
        oar_ref[...] = gl_re
        oai_ref[...] = gl_im
        odt_ref[...] = w_re * glb_re + w_im * glb_im

    bb = jax.ShapeDtypeStruct(b_re_t.shape, _F32)
    v = jax.ShapeDtypeStruct(a_re.shape, _F32)
    return pl.pallas_call(body, name="ssm_prep_bwd", out_shape=[bb, bb, v, v, v])(
        log_dt, a_re, a_im, b_re_t, b_im_t, d_bbr, d_bbi, d_lr, d_li)


def _group_sum(d_dt, log_dt):
    def body(d_ref, l_ref, o_ref):
        o_ref[...] = jnp.sum(d_ref[...], axis=1, keepdims=True) * jnp.exp(l_ref[...])

    return pl.pallas_call(body, name="ssm_dt_grad", out_shape=jax.ShapeDtypeStruct(log_dt.shape, _F32))(d_dt, log_dt)


def _scan_rows(src_ref, dst_ref, nrt, steps, ptab, carry0, reverse):
    ns = BLOCK_ST
    row = lax.broadcasted_iota(jnp.int32, (SUBLANES, ns), 0)
    pr, pi = ptab

    def body(i, carry):
        cr, ci = carry
        it = (nrt - 1 - i) if reverse else i
        r0 = pl.multiple_of(it * SUBLANES, SUBLANES)
        xr = src_ref[pl.ds(r0, SUBLANES), 0:ns]
        xi = src_ref[pl.ds(r0, SUBLANES), ns:2 * ns]
        for k, (ar, ai) in zip((1, 2, 4), steps):
            if reverse:
                keep = row < SUBLANES - k
                sr = jnp.where(keep, pltpu.roll(xr, SUBLANES - k, 0), 0.0)
                si = jnp.where(keep, pltpu.roll(xi, SUBLANES - k, 0), 0.0)
            else:
                keep = row >= k
                sr = jnp.where(keep, pltpu.roll(xr, k, 0), 0.0)
                si = jnp.where(keep, pltpu.roll(xi, k, 0), 0.0)
            xr, xi = xr + ar * sr - ai * si, xi + ar * si + ai * sr
        xr, xi = xr + pr * cr - pi * ci, xi + pr * ci + pi * cr
        dst_ref[pl.ds(r0, SUBLANES), 0:ns] = xr
        dst_ref[pl.ds(r0, SUBLANES), ns:2 * ns] = xi
        if reverse:
            return xr[0:1, :], xi[0:1, :]
        return xr[SUBLANES - 1:SUBLANES, :], xi[SUBLANES - 1:SUBLANES, :]

    return lax.fori_loop(0, nrt, body, carry0)


def _scan_consts(p_ref, conj):
    ns = BLOCK_ST
    sign = -1.0 if conj else 1.0
    bc = lambda r: jnp.broadcast_to(r, (SUBLANES, ns))
    steps = [(bc(p_ref[k - 1:k, 0:ns]), bc(sign * p_ref[k - 1:k, ns:2 * ns])) for k in (1, 2, 4)]
    return steps


def _ssm_block_fwd(u, bbt_ref, ct_ref, d_ref, wg_ref, bg_ref, p_ref, bu_scr, h_scr, carry_in, nrt):
    ns = BLOCK_ST
    bu_scr[...] = jnp.dot(u.astype(_MXU), bbt_ref[...].astype(_MXU), preferred_element_type=_F32)
    ptab = (p_ref[:, 0:ns], p_ref[:, ns:2 * ns])
    carry = _scan_rows(bu_scr, h_scr, nrt, _scan_consts(p_ref, False), ptab, carry_in, False)
    y = jnp.dot(h_scr[...].astype(_MXU), ct_ref[...].astype(_MXU), preferred_element_type=_F32) + d_ref[...] * u
    yg = _gelu(y)
    gate = _sigmoid(jnp.dot(yg.astype(_MXU), wg_ref[...].astype(_MXU), preferred_element_type=_F32) + bg_ref[...])
    return y, yg, gate, carry


def _ssm_specs(nb, nt, t, reverse):
    tt = (lambda ti: nt - 1 - ti) if reverse else (lambda ti: ti)
    ns2 = 2 * BLOCK_ST
    return dict(
        z=pl.BlockSpec((t, BLOCK_CH), lambda b, ti: (tt(ti), b)),
        bbt=pl.BlockSpec((None, BLOCK_CH, ns2), lambda b, ti: (b, 0, 0)),
        ct=pl.BlockSpec((None, ns2, BLOCK_CH), lambda b, ti: (b, 0, 0)),
        vec=pl.BlockSpec((1, BLOCK_CH), lambda b, ti: (0, b)),
        wg=pl.BlockSpec((None, BLOCK_CH, BLOCK_CH), lambda b, ti: (b, 0, 0)),
        p=pl.BlockSpec((None, SUBLANES, ns2), lambda b, ti: (b, 0, 0)),
        hb=pl.BlockSpec((None, None, SUBLANES, ns2), lambda b, ti: (b, tt(ti), 0, 0)),
        h=pl.BlockSpec((None, t, ns2), lambda b, ti: (b, tt(ti), 0)),
        acc_vec=pl.BlockSpec((None, 1, ns2), lambda b, ti: (b, 0, 0)),
    )


def _ssm_fwd(z, bbt, ct, dvec, wg, bglu, ptab):
    s = z.shape[0]
    nb = bbt.shape[0]
    t = _tile(s, TIME_TILE, SUBLANES)
    nt = s // t
    ns = BLOCK_ST
    sp = _ssm_specs(nb, nt, t, False)

    def body(z_ref, bbt_ref, ct_ref, d_ref, wg_ref, bg_ref, p_ref, y2_ref, y_ref, h_ref, hb_ref, bu_scr, h_scr, carry_scr):
        ti = pl.program_id(1)

        @pl.when(ti == 0)
        def _():
            carry_scr[...] = jnp.zeros_like(carry_scr)

        hb_ref[...] = carry_scr[...]
        carry_in = (carry_scr[0:1, 0:ns], carry_scr[0:1, ns:2 * ns])
        y, yg, gate, (cr, ci) = _ssm_block_fwd(z_ref[...], bbt_ref, ct_ref, d_ref, wg_ref, bg_ref, p_ref,
                                               bu_scr, h_scr, carry_in, t // SUBLANES)
        y2_ref[...] = yg * gate
        y_ref[...] = y
        h_ref[...] = h_scr[...].astype(h_ref.dtype)
        carry_scr[:, 0:ns] = jnp.broadcast_to(cr, (SUBLANES, ns))
        carry_scr[:, ns:2 * ns] = jnp.broadcast_to(ci, (SUBLANES, ns))

    ych = jax.ShapeDtypeStruct((s, nb * BLOCK_CH), _F32)
    return pl.pallas_call(
        body, name="ssm_fwd", grid=(nb, nt),
        out_shape=[ych, ych, jax.ShapeDtypeStruct((nb, s, 2 * ns), _MXU),
                   jax.ShapeDtypeStruct((nb, nt, SUBLANES, 2 * ns), _F32)],
        in_specs=[sp["z"], sp["bbt"], sp["ct"], sp["vec"], sp["wg"], sp["vec"], sp["p"]],
        out_specs=[sp["z"], sp["z"], sp["h"], sp["hb"]],
        scratch_shapes=[pltpu.VMEM((t, 2 * ns), _F32), pltpu.VMEM((t, 2 * ns), _F32), pltpu.VMEM((SUBLANES, 2 * ns), _F32)],
        compiler_params=_cp("parallel", "arbitrary"),
    )(z, bbt, ct, dvec, wg, bglu, ptab)


def _ssm_bwd(z, y_pre, h_all, dy2, hb, bbt, ct, dvec, wg, bglu, ptab, ptab_rev):
    s = z.shape[0]
    nb = bbt.shape[0]
    t = _tile(s, TIME_TILE, SUBLANES)
    nt = s // t
    ns = BLOCK_ST
    sp = _ssm_specs(nb, nt, t, True)
    tn_dims = (((0,), (0,)), ((), ()))
    nt_dims = (((1,), (1,)), ((), ()))

    def body(z_ref, y_ref, h_ref, dy2_ref, hb_ref, bbt_ref, ct_ref, d_ref, wg_ref, bg_ref, p_ref, pr_ref,
             dz_ref, dbbt_ref, dct_ref, dwg_ref, dlb_ref, dd_ref, dbg_ref, bu_scr, g_scr, gcarry_scr):
        first = pl.program_id(1) == 0

        @pl.when(first)
        def _():
            gcarry_scr[...] = jnp.zeros_like(gcarry_scr)

        u = z_ref[...]
        hin = hb_ref[...]
        y = y_ref[...]
        yg = _gelu(y)
        gate = _sigmoid(jnp.dot(yg.astype(_MXU), wg_ref[...].astype(_MXU), preferred_element_type=_F32) + bg_ref[...])
        dy2 = dy2_ref[...]
        dpre = dy2 * yg * gate * (1.0 - gate)
        _acc(dbg_ref, first, _colsum(dpre))
        dpx = dpre.astype(_MXU)
        _acc(dwg_ref, first, lax.dot_general(yg.astype(_MXU), dpx, tn_dims, preferred_element_type=_F32))
        dyg = dy2 * gate + lax.dot_general(dpx, wg_ref[...].astype(_MXU), nt_dims, preferred_element_type=_F32)
        dy = dyg * _gelu_grad(y)
        _acc(dd_ref, first, _colsum(dy * u))
        dyx = dy.astype(_MXU)
        hx = h_ref[...]
        h = hx.astype(_F32)
        _acc(dct_ref, first, lax.dot_general(hx, dyx, tn_dims, preferred_element_type=_F32))
        bu_scr[...] = lax.dot_general(dyx, ct_ref[...].astype(_MXU), nt_dims, preferred_element_type=_F32)
        gin = (gcarry_scr[0:1, 0:ns], gcarry_scr[0:1, ns:2 * ns])
        ptab = (pr_ref[:, 0:ns], pr_ref[:, ns:2 * ns])
        gr, gi = _scan_rows(bu_scr, g_scr, t // SUBLANES, _scan_consts(p_ref, True), ptab, gin, True)
        gcarry_scr[:, 0:ns] = jnp.broadcast_to(gr, (SUBLANES, ns))
        gcarry_scr[:, ns:2 * ns] = jnp.broadcast_to(gi, (SUBLANES, ns))
        g = g_scr[...]
        row = lax.broadcasted_iota(jnp.int32, (t, ns), 0)
        hp_re = jnp.where(row == 0, hin[0:1, 0:ns], pltpu.roll(h[:, 0:ns], 1, 0))
        hp_im = jnp.where(row == 0, hin[0:1, ns:2 * ns], pltpu.roll(h[:, ns:2 * ns], 1, 0))
        g_re, g_im = g[:, 0:ns], g[:, ns:2 * ns]
        d_ar = _colsum(g_re * hp_re + g_im * hp_im)
        d_ai = _colsum(g_im * hp_re - g_re * hp_im)
        _acc(dlb_ref, first, jnp.concatenate([d_ar, d_ai], axis=1))
        gx = g.astype(_MXU)
        _acc(dbbt_ref, first, lax.dot_general(u.astype(_MXU), gx, tn_dims, preferred_element_type=_F32))
        dz_ref[...] = (dy * d_ref[...] + lax.dot_general(gx, bbt_ref[...].astype(_MXU), nt_dims,
                                                         preferred_element_type=_F32)).astype(dz_ref.dtype)

    f = lambda shape: jax.ShapeDtypeStruct(shape, _F32)
    return pl.pallas_call(
        body, name="ssm_bwd", grid=(nb, nt),
        out_shape=[jax.ShapeDtypeStruct((s, nb * BLOCK_CH), _MXU), f(bbt.shape), f(ct.shape), f(wg.shape), f((nb, 1, 2 * ns)),
                   f((1, nb * BLOCK_CH)), f((1, nb * BLOCK_CH))],
        in_specs=[sp["z"], sp["z"], sp["h"], sp["z"], sp["hb"], sp["bbt"], sp["ct"], sp["vec"], sp["wg"], sp["vec"], sp["p"],
                  sp["p"]],
        out_specs=[sp["z"], sp["bbt"], sp["ct"], sp["wg"], sp["acc_vec"], sp["vec"], sp["vec"]],
        scratch_shapes=[pltpu.VMEM((t, 2 * ns), _F32), pltpu.VMEM((t, 2 * ns), _F32), pltpu.VMEM((SUBLANES, 2 * ns), _F32)],
        compiler_params=_cp("parallel", "arbitrary"),
    )(z, y_pre, h_all, dy2, hb, bbt, ct, dvec, wg, bglu, ptab, ptab_rev)


def _mod_part(c_all, w, b):
    d, ns = w.shape
    tn = _tile(ns, 512)

    def body(c_ref, w_ref, b_ref, o_ref):
        c = c_ref[...]
        ca = (c * _sigmoid(c)).astype(_MXU)
        o_ref[...] = jnp.dot(ca, w_ref[...].astype(_MXU), preferred_element_type=_F32) + b_ref[...]

    return pl.pallas_call(
        body, name="mod_part", grid=(ns // tn,), out_shape=jax.ShapeDtypeStruct((8, ns), _F32),
        in_specs=[pl.BlockSpec((8, d), lambda n: (0, 0)), pl.BlockSpec((d, tn), lambda n: (0, n)),
                  pl.BlockSpec((1, tn), lambda n: (0, n))],
        out_specs=pl.BlockSpec((8, tn), lambda n: (0, n)), compiler_params=_cp("parallel"),
    )(c_all, w, b)


def _adamw_math(w, g, m, v):
    m = ADAM_B1 * m + (1.0 - ADAM_B1) * g
    v = ADAM_B2 * v + (1.0 - ADAM_B2) * (g * g)
    m_hat = m / (1.0 - ADAM_B1 ** ADAM_STEP)
    v_hat = v / (1.0 - ADAM_B2 ** ADAM_STEP)
    delta = -ADAM_LR * (m_hat / (jnp.sqrt(v_hat) + ADAM_EPS) + ADAM_WD * w)
    return delta, m, v


def _adamw(w, g, m, v, name):
    r, c = w.shape
    tc = c if c <= 4096 else _tile(c, 4096)
    tr = _tile(r, max(SUBLANES, (1 << 18) // tc), SUBLANES)

    def body(w_ref, g_ref, m_ref, v_ref, go_ref, d_ref, mo_ref, vo_ref):
        g = g_ref[...]
        go_ref[...] = g
        d_ref[...], mo_ref[...], vo_ref[...] = _adamw_math(w_ref[...], g, m_ref[...], v_ref[...])

    spec = pl.BlockSpec((tr, tc), lambda i, j: (i, j))
    out = jax.ShapeDtypeStruct((r, c), _F32)
    return pl.pallas_call(
        body, name=name, grid=(r // tr, c // tc), in_specs=[spec] * 4, out_specs=[spec] * 4, out_shape=[out] * 4,
        compiler_params=_cp("parallel", "parallel"),
    )(w, g, m, v)


def _adamw_halves(w, g2, m, v, name):
    r, c = w.shape
    tr, tc = _tile(r, 256, SUBLANES), _tile(c // 2, 1024)
    nph = (c // 2) // tc

    def body(w_ref, g_ref, m_ref, v_ref, go_ref, d_ref, mo_ref, vo_ref):
        g = g_ref[...]
        go_ref[...] = g
        d_ref[...], mo_ref[...], vo_ref[...] = _adamw_math(w_ref[...], g, m_ref[...], v_ref[...])

    spec = pl.BlockSpec((tr, tc), lambda i, j: (i, j))
    out = jax.ShapeDtypeStruct((r, c), _F32)
    return pl.pallas_call(
        body, name=name, grid=(r // tr, c // tc),
        in_specs=[spec, pl.BlockSpec((None, tr, tc), lambda i, j: (j // nph, i, j % nph)), spec, spec],
        out_specs=[spec] * 4, out_shape=[out] * 4, compiler_params=_cp("parallel", "parallel"),
    )(w, g2, m, v)


def _wada_update(c_t, dm, w, m, v):
    d, ns = w.shape
    tr, tc = _tile(d, 256, SUBLANES), _tile(ns, 1024)

    def body(c_ref, dm_ref, w_ref, m_ref, v_ref, g_ref, d_ref, mo_ref, vo_ref):
        c = c_ref[...]
        ca = c * _sigmoid(c)
        dmv = dm_ref[...]
        g = ca[:, 0:1] * dmv[0:1, :]
        for b in range(1, 8):
            g = g + ca[:, b:b + 1] * dmv[b:b + 1, :]
        g_ref[...] = g
        d_ref[...], mo_ref[...], vo_ref[...] = _adamw_math(w_ref[...], g, m_ref[...], v_ref[...])

    spec = pl.BlockSpec((tr, tc), lambda i, j: (i, j))
    out = jax.ShapeDtypeStruct((d, ns), _F32)
    return pl.pallas_call(
        body, name="wada_update", grid=(d // tr, ns // tc),
        in_specs=[pl.BlockSpec((tr, 8), lambda i, j: (i, 0)), pl.BlockSpec((8, tc), lambda i, j: (0, j)), spec, spec, spec],
        out_specs=[spec] * 4, out_shape=[out] * 4, compiler_params=_cp("parallel", "parallel"),
    )(c_t, dm, w, m, v)


def _small_reduce(gathered):
    _, r, c = gathered.shape
    tr = _tile(r, 512, SUBLANES)

    def body(q_ref, g_ref):
        g = q_ref[0]
        for k in range(1, 8):
            g = g + q_ref[k]
        g_ref[...] = g

    return pl.pallas_call(
        body, name="small_reduce", grid=(r // tr,), out_shape=jax.ShapeDtypeStruct((r, c), _F32),
        in_specs=[pl.BlockSpec((8, tr, c), lambda i: (0, i, 0))], out_specs=pl.BlockSpec((tr, c), lambda i: (i, 0)),
        compiler_params=_cp("parallel"),
    )(gathered)


def _adamw_many(ws, gs, ms, vs, steps, name):
    n = len(ws)

    def body(*refs):
        w_refs, g_refs, m_refs, v_refs = refs[0:n], refs[n:2 * n], refs[2 * n:3 * n], refs[3 * n:4 * n]
        d_refs, mo_refs, vo_refs = refs[4 * n:5 * n], refs[5 * n:6 * n], refs[6 * n:7 * n]
        for i in range(n):
            d_refs[i][...], mo_refs[i][...], vo_refs[i][...] = _adamw_math(
                w_refs[i][...], g_refs[i][...], m_refs[i][...], v_refs[i][...])

    def spec(a):
        nd = a.ndim
        if steps == 1:
            return pl.BlockSpec(a.shape, lambda i: (0,) * nd)
        return pl.BlockSpec((a.shape[0] // steps,) + a.shape[1:], lambda i: (i,) + (0,) * (nd - 1))

    specs = [spec(w) for w in ws]
    outs = pl.pallas_call(
        body, name=name, grid=(steps,), in_specs=specs * 4, out_specs=specs * 3,
        out_shape=[jax.ShapeDtypeStruct(w.shape, _F32) for w in ws] * 3, compiler_params=_cp("parallel"),
    )(*ws, *gs, *ms, *vs)
    return outs[0:n], outs[n:2 * n], outs[2 * n:3 * n]


def _block_diag(x):
    nb, g, p, q = x.shape
    eye = jnp.eye(g, dtype=x.dtype)
    return (x[:, :, :, None, :] * eye[None, :, None, :, None]).reshape(nb, g * p, g * q)


def _block_diag_take(x, p, q):
    nb = x.shape[0]
    g = GROUPS_PER_BLOCK
    eye = jnp.eye(g, dtype=x.dtype)
    return jnp.sum(x.reshape(nb, g, p, g, q) * eye[None, :, None, :, None], axis=3)


_VIEWS = {"ssm_b_re": ((0, 2, 1), (0, 2, 1)), "ssm_b_im": ((0, 2, 1), (0, 2, 1)),
          "ssm_w_glu": ((1, 2, 0), (2, 0, 1)), "ssm_b_glu": ((1, 0), (1, 0))}


def _to_view(name, a):
    return a.transpose(_VIEWS[name][0]) if name in _VIEWS else a


def _from_view(name, a):
    return a.transpose(_VIEWS[name][1]) if name in _VIEWS else a


class _Pack:
    def __init__(self, shapes):
        self.shapes = shapes
        self.offsets = {}
        off = 0
        for name, shape in shapes.items():
            n = math.prod(shape)
            self.offsets[name] = (off, n)
            off += -(-n // (SUBLANES * LANES)) * (SUBLANES * LANES)
        self.rows = -(-off // (256 * LANES)) * 256

    def pack(self, arrays):
        parts = []
        off = 0
        for name, shape in self.shapes.items():
            start, n = self.offsets[name]
            if start > off:
                parts.append(jnp.zeros((start - off,), _F32))
            parts.append(arrays[name].reshape(-1).astype(_F32))
            off = start + n
        total = self.rows * LANES
        if total > off:
            parts.append(jnp.zeros((total - off,), _F32))
        return jnp.concatenate(parts).reshape(self.rows, LANES)

    def unpack(self, buf):
        flat = buf.reshape(-1)
        return {name: flat[start:start + n].reshape(self.shapes[name]) for name, (start, n) in self.offsets.items()}


_SMALL = ["b_ada", "g_pre_mix", "g_post_mix", "ssm_log_dt", "ssm_a_re", "ssm_a_im", "ssm_b_re", "ssm_b_im", "ssm_c_re",
          "ssm_c_im", "ssm_d", "ssm_w_glu", "ssm_b_glu", "sgu_ln_g", "sgu_ln_b", "sgu_w", "sgu_b", "g_out_ssm",
          "g_out_sgu", "g_pre_ffn", "g_post_ffn", "conv_b"]
_WEIGHTS = ["w_ada", "b_ada", "g_pre_mix", "g_post_mix", "w_in", "ssm_log_dt", "ssm_a_re", "ssm_a_im", "ssm_b_re",
            "ssm_b_im", "ssm_c_re", "ssm_c_im", "ssm_d", "ssm_w_glu", "ssm_b_glu", "sgu_ln_g", "sgu_ln_b", "sgu_w", "sgu_b",
            "g_out_ssm", "g_out_sgu", "w_out", "g_pre_ffn", "g_post_ffn", "w_up", "conv_w", "conv_b", "w_down"]


def _step(p, m, v, x, c, tgt):
    s, d = x.shape
    mx, my, mc = lax.axis_index("x"), lax.axis_index("y"), lax.axis_index("c")
    chip = 2 * mx + my
    dev = 4 * mx + 2 * my + mc
    sel = jnp.stack([chip, mc]).astype(jnp.int32)
    g_cnt, n_st = p["ssm_a_re"].shape
    nb = g_cnt // GROUPS_PER_BLOCK
    gn = g_cnt * n_st
    d_ssm = g_cnt * SSM_GROUP
    nh = p["sgu_w"].shape[0]
    assert nh * CHUNK == d_ssm and 2 * d_ssm == d and n_st == SSM_STATE

    shards = lambda g: g.reshape(4, g.shape[1] * g.shape[2], g.shape[3])
    buf_in = _cast_into_slot(p["w_in"], sel, sel, "cast_w_in")

    ns_ada = p["w_ada"].shape[1]
    nc_conv = p["conv_w"].shape[1]
    first = jnp.concatenate([jnp.broadcast_to(c, (8, d)), jnp.pad(p["conv_w"], ((0, 5), (0, 0)))], axis=1)
    first_all = _all_gather8(_own_slot(first, dev), "gather_c_conv")
    c_all = first_all[:, 0, :d]
    conv_w_full = jnp.concatenate([first_all[2 * j, 0:3, d:] for j in range(4)], axis=1)
    b_ada_mine = lax.dynamic_slice_in_dim(p["b_ada"], chip * ns_ada, ns_ada, axis=1)
    mod_all = _all_gather8(_own_slot(_mod_part(c_all, p["w_ada"], b_ada_mine), dev), "gather_mod")
    (sems_in,), (buf_in,), tok = _gather_start([buf_in], mod_all, "gather_start_in")
    buf_out, buf_up, buf_down = [_cast_into_slot(p[n], sel, tok, "cast_" + n) for n in ("w_out", "w_up", "w_down")]
    mod_rows = lax.dynamic_index_in_dim(mod_all, dev, axis=1, keepdims=False)
    mod = jnp.concatenate([mod_rows[0], mod_rows[2], mod_rows[4], mod_rows[6]]).reshape(N_MOD, 1, d)
    sh1, sc1, gt1, sh2, sc2, gt2 = [mod[i] for i in range(N_MOD)]

    ldt_l = jnp.repeat(p["ssm_log_dt"], n_st, axis=1)
    are_l, aim_l = p["ssm_a_re"].reshape(1, gn), p["ssm_a_im"].reshape(1, gn)
    bre_t, bim_t = p["ssm_b_re"].reshape(gn, SSM_GROUP).T, p["ssm_b_im"].reshape(gn, SSM_GROUP).T
    pw_re, pw_im, bb_re, bb_im = _ssm_prep(ldt_l, are_l, aim_l, bre_t, bim_t)
    blocks = lambda t: t.reshape(t.shape[0], nb, GROUPS_PER_BLOCK * n_st).transpose(1, 0, 2)
    ptab = jnp.concatenate([blocks(pw_re), blocks(pw_im)], axis=2)
    ptab_rev = jnp.concatenate([blocks(pw_re)[:, ::-1], -blocks(pw_im)[:, ::-1]], axis=2)
    bd = lambda t: t.reshape(SSM_GROUP, nb, GROUPS_PER_BLOCK, n_st).transpose(1, 2, 0, 3)
    bbt = jnp.concatenate([_block_diag(bd(bb_re)), _block_diag(bd(bb_im))], axis=2).astype(_MXU)
    cd = lambda t: t.reshape(nb, GROUPS_PER_BLOCK, SSM_GROUP, n_st).transpose(0, 1, 3, 2)
    ct = jnp.concatenate([_block_diag(cd(p["ssm_c_re"])), -_block_diag(cd(p["ssm_c_im"]))], axis=1).astype(_MXU)
    wg = _block_diag(p["ssm_w_glu"].reshape(nb, GROUPS_PER_BLOCK, SSM_GROUP, SSM_GROUP)).astype(_MXU)
    dvec = p["ssm_d"]
    bglu = p["ssm_b_glu"].reshape(1, d_ssm)
    mask = jnp.tril(jnp.ones((CHUNK, CHUNK), _F32))
    wm = (p["sgu_w"] * mask[None]).astype(_MXU)
    bs = p["sgu_b"].reshape(nh, CHUNK, 1)

    h1 = _fwd_pre_mix(x, p["g_pre_mix"], _after(sc1, tok), sh1)
    buf_in = _gather_wait(sems_in, buf_in, h1, "gather_wait_in")
    w_in4 = shards(_pair_forward([buf_in], "pair_forward_in")[0])
    (sems_out, sems_up, sems_down), (buf_out, buf_up, buf_down), tok = _gather_start(
        [buf_out, buf_up, buf_down], w_in4, "gather_start_rest")
    z = _mm_nn(h1, w_in4, _F32, "mm_in", after=tok)
    y_ssm, y_pre, h_all, hb = _ssm_fwd(z, bbt, ct, dvec, wg, bglu, ptab)
    y_sgu = _sgu_fwd(z, p["sgu_ln_g"], p["sgu_ln_b"], wm, bs)
    ycat = _mix_norm_fwd(y_ssm, y_sgu, p["g_out_ssm"], p["g_out_sgu"])
    buf_out = _gather_wait(sems_out, buf_out, ycat, "gather_wait_out")
    w_out_full = _pair_forward([buf_out], "pair_forward_out")[0].reshape(1, d, d)
    o = _mm_nn(ycat, w_out_full, _F32, "mm_out")
    x1, h2 = _fwd_mid(o, x, gt1, p["g_post_mix"], p["g_pre_ffn"], sc2, sh2)
    buf_up = _gather_wait(sems_up, buf_up, h2, "gather_wait_up")
    w_up4 = shards(_pair_forward([buf_up], "pair_forward_up")[0])
    up_pre = _mm_nn(h2, w_up4, _F32, "mm_up")
    act = _conv_act_fwd(up_pre, conv_w_full, p["conv_b"])
    buf_down = _gather_wait(sems_down, buf_down, act, "gather_wait_down")
    w_down_full = _pair_forward([buf_down], "pair_forward_down")[0].reshape(1, -1, d)
    f = _mm_nn(act, w_down_full, _F32, "mm_down", tk=5632)
    dx2, df, d_gt2, d_g_post_ffn, loss = _loss_and_post_ffn_bwd(f, x1, tgt, gt2, p["g_post_ffn"])

    def reduce_next(swap, n, after):
        sems, gw, land, _ = swap
        gw, got = _swap_wait(sems, gw, land, after, "swap_wait_" + n)
        return _scatter_start(_pair_sum(gw, got, sel, "pair_sum_" + n), "scatter_start_" + n)

    d_act = _mm_nt(df, w_down_full, _F32, "mm_d_act", tk=2048)
    swap_down = _swap_start(_mm_tn_rows(act, df, "mm_gw_down"), "swap_start_w_down")
    d_up_pre, d_cw0, d_cw1, d_cw2, d_conv_b = _conv_act_bwd(up_pre, d_act, conv_w_full, _after(p["conv_b"], swap_down[3]))
    red_down = reduce_next(swap_down, "w_down", d_conv_b)
    dh2 = _mm_nt(d_up_pre, w_up4, _F32, "mm_dh2", tk=2816, after=red_down[3])
    swap_up = _swap_start(_mm_tn_cols(h2, d_up_pre, "mm_gw_up"), "swap_start_w_up")
    dx1, d_o, d_sc2, d_sh2, d_g_pre_ffn, d_gt1, d_g_post_mix = _bwd_mid(
        dh2, x1, dx2, o, p["g_pre_ffn"], _after(sc2, swap_up[3]), gt1, p["g_post_mix"])
    red_up = reduce_next(swap_up, "w_up", d_g_post_mix)
    d_ycat = _mm_nt(d_o, w_out_full, _F32, "mm_d_ycat", tn=1024, tk=2048, after=red_up[3])
    swap_out = _swap_start(_mm_tn_rows(ycat, d_o, "mm_gw_out"), "swap_start_w_out")
    dy_ssm, dy_sgu, d_g_out_ssm, d_g_out_sgu = _mix_norm_bwd(
        d_ycat, y_ssm, y_sgu, _after(p["g_out_ssm"], swap_out[3]), p["g_out_sgu"])
    red_out = reduce_next(swap_out, "w_out", d_g_out_sgu)
    dz_ssm, d_bbt, d_ct, d_wg, d_lb, d_ssm_d, d_bglu = _ssm_bwd(z, y_pre, h_all, dy_ssm, hb, bbt, ct,
                                                                _after(dvec, red_out[3]), wg, bglu, ptab, ptab_rev)
    dz, d_ln_g, d_ln_b, d_wm, d_bs = _sgu_bwd(z, dy_sgu, dz_ssm, p["sgu_ln_g"], p["sgu_ln_b"], wm, bs)
    dh1 = _mm_nt(dz, w_in4, _F32, "mm_dh1")
    swap_in = _swap_start(_mm_tn_cols(h1, dz, "mm_gw_in"), "swap_start_w_in")
    dx, d_sc1, d_sh1, d_g_pre_mix = _bwd_pre_mix(dh1, x, dx1, p["g_pre_mix"], _after(sc1, swap_in[3]))
    red_in = reduce_next(swap_in, "w_in", d_g_pre_mix)

    nsb = BLOCK_ST
    lanes = lambda t: t.transpose(2, 0, 1, 3).reshape(SSM_GROUP, gn)
    d_bbr = lanes(_block_diag_take(d_bbt[:, :, :nsb], SSM_GROUP, n_st))
    d_bbi = lanes(_block_diag_take(d_bbt[:, :, nsb:], SSM_GROUP, n_st))
    d_lr, d_li = d_lb[:, 0, :nsb].reshape(1, gn), d_lb[:, 0, nsb:].reshape(1, gn)
    d_bre_t, d_bim_t, d_are, d_aim, d_dt = _ssm_prep_bwd(ldt_l, are_l, aim_l, bre_t, bim_t, d_bbr, d_bbi, d_lr, d_li)
    d_log_dt = _group_sum(d_dt.reshape(g_cnt, n_st), p["ssm_log_dt"].reshape(g_cnt, 1))
    c_grad = lambda t: _block_diag_take(t, n_st, SSM_GROUP).transpose(0, 1, 3, 2).reshape(g_cnt, SSM_GROUP, n_st)
    small = {
        "b_ada": jnp.concatenate([d_sh1, _after(d_sc1, red_in[3]), d_gt1, d_sh2, d_sc2, d_gt2], axis=1),
        "g_pre_mix": d_g_pre_mix, "g_post_mix": d_g_post_mix,
        "ssm_log_dt": d_log_dt, "ssm_a_re": d_are, "ssm_a_im": d_aim,
        "ssm_b_re": d_bre_t.T, "ssm_b_im": d_bim_t.T,
        "ssm_c_re": c_grad(d_ct[:, :nsb, :]), "ssm_c_im": -c_grad(d_ct[:, nsb:, :]),
        "ssm_d": d_ssm_d, "ssm_w_glu": _block_diag_take(d_wg, SSM_GROUP, SSM_GROUP), "ssm_b_glu": d_bglu,
        "sgu_ln_g": d_ln_g, "sgu_ln_b": d_ln_b, "sgu_w": d_wm * mask[None], "sgu_b": d_bs,
        "g_out_ssm": d_g_out_ssm, "g_out_sgu": d_g_out_sgu, "g_pre_ffn": d_g_pre_ffn, "g_post_ffn": d_g_post_ffn,
        "conv_b": d_conv_b, "conv_w_all": jnp.concatenate([d_cw0, d_cw1, d_cw2], axis=0),
    }
    small = {n: _to_view(n, a.reshape(p[n].shape)) if n in p else a for n, a in small.items()}
    pk = _Pack({n: a.shape for n, a in small.items()})
    sems_small, small_buf, tok = _gather8_start(_own_slot(pk.pack(small), dev), "gather_small_start")

    big = ["w_down", "w_up", "w_out", "w_in"]
    joins = []
    after = tok
    for n, (sems, pair, land, _) in zip(big, (red_down, red_up, red_out, red_in)):
        pair, land = _scatter_wait(sems, pair, land, after, "scatter_wait_" + n)
        sems_j, half, after = _join_start(_chip_sum(pair, land, sel, "chip_sum_" + n), "join_start_" + n)
        joins.append((sems_j, half))
    big_out = {}
    for n, (sems_j, half) in zip(big, joins):
        j = _join_wait(sems_j, half, after, "join_wait_" + n)
        if n in ("w_in", "w_up"):
            big_out[n] = tuple(_adamw(p[n], j.reshape(p[n].shape), m[n], v[n], "adamw_" + n))
        else:
            big_out[n] = tuple(_adamw_halves(p[n], j, m[n], v[n], "adamw_" + n))
        after = big_out[n][1]

    gathered = _gather8_forward(_gather8_wait(sems_small, small_buf, after, "gather_small_wait"),
                                "gather_small_forward")
    gview = pk.unpack(_small_reduce(gathered))
    gview["conv_w"] = lax.dynamic_slice_in_dim(gview.pop("conv_w_all"), chip * nc_conv, nc_conv, axis=1)
    small_names = _SMALL + ["conv_w"]
    per_group = [n for n in small_names if gview[n].ndim >= 2 and gview[n].shape[0] == g_cnt]
    others = [n for n in small_names if n not in per_group]
    grads = {n: _from_view(n, gview[n]) for n in small_names}
    deltas, new_m, new_v = {}, {}, {}
    for names, steps, call in ((per_group, g_cnt // GROUPS_PER_BLOCK, "adamw_s5"), (others, 1, "adamw_small")):
        res = _adamw_many([_to_view(n, p[n]) for n in names], [gview[n] for n in names],
                          [_to_view(n, m[n]) for n in names], [_to_view(n, v[n]) for n in names], steps, call)
        for n, dl, mo, vo in zip(names, *res):
            deltas[n], new_m[n], new_v[n] = _from_view(n, dl), _from_view(n, mo), _from_view(n, vo)

    d_mod_all = gathered.reshape(8, -1)[:, :N_MOD * d]
    d_mod_mine = lax.dynamic_slice_in_dim(d_mod_all, chip * ns_ada, ns_ada, axis=1)
    grads["w_ada"], deltas["w_ada"], new_m["w_ada"], new_v["w_ada"] = _wada_update(
        c_all.T, d_mod_mine, p["w_ada"], m["w_ada"], v["w_ada"])
    for n in big:
        grads[n], deltas[n], new_m[n], new_v[n] = big_out[n]
    return loss[0, 0], dx, grads, deltas, new_m, new_v


def kernel(x, c, w_ada, b_ada, g_pre_mix, g_post_mix, w_in, ssm_log_dt, ssm_a_re, ssm_a_im, ssm_b_re, ssm_b_im, ssm_c_re, ssm_c_im, ssm_d, ssm_w_glu, ssm_b_glu, sgu_ln_g, sgu_ln_b, sgu_w, sgu_b, g_out_ssm, g_out_sgu, w_out, g_pre_ffn, g_post_ffn, w_up, conv_w, conv_b, w_down, loss_target, m_w_ada, m_b_ada, m_g_pre_mix, m_g_post_mix, m_w_in, m_ssm_log_dt, m_ssm_a_re, m_ssm_a_im, m_ssm_b_re, m_ssm_b_im, m_ssm_c_re, m_ssm_c_im, m_ssm_d, m_ssm_w_glu, m_ssm_b_glu, m_sgu_ln_g, m_sgu_ln_b, m_sgu_w, m_sgu_b, m_g_out_ssm, m_g_out_sgu, m_w_out, m_g_pre_ffn, m_g_post_ffn, m_w_up, m_conv_w, m_conv_b, m_w_down, v_w_ada, v_b_ada, v_g_pre_mix, v_g_post_mix, v_w_in, v_ssm_log_dt, v_ssm_a_re, v_ssm_a_im, v_ssm_b_re, v_ssm_b_im, v_ssm_c_re, v_ssm_c_im, v_ssm_d, v_ssm_w_glu, v_ssm_b_glu, v_sgu_ln_g, v_sgu_ln_b, v_sgu_w, v_sgu_b, v_g_out_ssm, v_g_out_sgu, v_w_out, v_g_pre_ffn, v_g_post_ffn, v_w_up, v_conv_w, v_conv_b, v_w_down):
    given = dict(locals())
    drop = lambda a: a if a.ndim == 2 else a[0]
    p = {n: drop(given[n]) for n in _WEIGHTS}
    m = {n: drop(given["m_" + n]) for n in _WEIGHTS}
    v = {n: drop(given["v_" + n]) for n in _WEIGHTS}
    loss, dx, grads, deltas, new_m, new_v = _step(p, m, v, x[0], c, loss_target[0])
    loss = lax.psum(loss, ("x", "y", "c"))
    outs = [loss, dx[None]]
    for group in (grads, deltas, new_m, new_v):
        outs += [group[n].reshape(given[n].shape) for n in _WEIGHTS]
    return tuple(outs)
```

```python
import functools
import math

import jax
import jax.numpy as jnp
from jax import lax
from jax.experimental import pallas as pl
from jax.experimental.pallas import tpu as pltpu

_F32 = jnp.float32
_MXU = jnp.bfloat16
_WIRE = jnp.bfloat16

EPS = 1e-6
SSM_GROUP = 16
SSM_STATE = 64
GROUPS_PER_BLOCK = 8
BLOCK_CH = SSM_GROUP * GROUPS_PER_BLOCK
BLOCK_ST = SSM_STATE * GROUPS_PER_BLOCK
CHUNK = 128
TIME_TILE = 512
SUBLANES = 8
LANES = 128
N_MOD = 6
ADAM_LR, ADAM_B1, ADAM_B2, ADAM_EPS, ADAM_WD, ADAM_STEP = 0.001, 0.9, 0.999, 1e-08, 0.01, 10
_VMEM_LIMIT = 56 * 1024 * 1024
_MESH = pl.DeviceIdType.MESH
_ANY = pl.BlockSpec(memory_space=pl.ANY)
_HBM = pl.BlockSpec(memory_space=pltpu.HBM)
_SEM = pl.BlockSpec(memory_space=pltpu.SEMAPHORE)
_VMEM_WHOLE = pl.BlockSpec(memory_space=pltpu.VMEM)
_EFFECT = pltpu.SideEffectType.DATAFLOW_SIDE_EFFECTING
_GELU_C = math.sqrt(2.0 / math.pi)


def _cp(*sem):
    return pltpu.CompilerParams(dimension_semantics=sem, vmem_limit_bytes=_VMEM_LIMIT)


def _tile(dim, target, align=LANES):
    if dim <= target:
        return dim
    best = None
    for t in range(align, target + 1, align):
        if dim % t == 0:
            best = t
    assert best is not None, (dim, target, align)
    return best


def _gelu(x):
    return 0.5 * x * (1.0 + jnp.tanh(_GELU_C * (x + 0.044715 * (x * x * x))))


def _gelu_grad(x):
    t = jnp.tanh(_GELU_C * (x + 0.044715 * (x * x * x)))
    return 0.5 * (1.0 + t) + 0.5 * x * (1.0 - t * t) * (_GELU_C * (1.0 + 3.0 * 0.044715 * x * x))


def _sigmoid(x):
    return 1.0 / (1.0 + jnp.exp(-x))


def _colsum(x):
    return jnp.sum(x, axis=0, keepdims=True)


def _rowmean(x):
    return jnp.mean(x, axis=-1, keepdims=True)


def _acc(ref, first, val):
    @pl.when(first)
    def _():
        ref[...] = val

    @pl.when(jnp.logical_not(first))
    def _():
        ref[...] += val


def _place():
    mx, my, mc = lax.axis_index("x"), lax.axis_index("y"), lax.axis_index("c")
    chips = [(1 - mx, my), (mx, 1 - my), (1 - mx, 1 - my)]
    return mx, my, mc, chips


def _all_gather8(buf, name):
    def body(in_ref, out_ref, send_sems, recv_sems):
        mx, my, mc, chips = _place()
        me, sibling = (mx, my, mc), (mx, my, 1 - mc)

        def slot(ref, px, py, pc):
            return ref.at[4 * px + 2 * py + pc]

        def copy(k, block, to, src_ref=out_ref):
            return pltpu.make_async_remote_copy(
                src_ref=slot(src_ref, *block), dst_ref=slot(out_ref, *block),
                send_sem=send_sems.at[k], recv_sem=recv_sems.at[k], device_id=to, device_id_type=_MESH)

        first = [copy(0, me, sibling, in_ref)]
        first += [copy(1 + j, me, (*chip, mc), in_ref) for j, chip in enumerate(chips)]
        for cp in first:
            cp.start()
        passed = [copy(4 + j, (*chip, mc), sibling) for j, chip in enumerate(chips)]
        for j, chip in enumerate(chips):
            copy(1 + j, (*chip, mc), me).wait_recv()
            passed[j].start()
        copy(0, sibling, me).wait_recv()
        for j, chip in enumerate(chips):
            copy(4 + j, (*chip, 1 - mc), me).wait_recv()
        for cp in first + passed:
            cp.wait_send()

    return pl.pallas_call(
        body, name=name, out_shape=jax.ShapeDtypeStruct(buf.shape, buf.dtype),
        in_specs=[_ANY], out_specs=_ANY, input_output_aliases={0: 0},
        scratch_shapes=[pltpu.SemaphoreType.DMA((7,)), pltpu.SemaphoreType.DMA((7,))],
    )(buf)


def _own_slot(x, dev):
    return lax.dynamic_update_slice(jnp.zeros((8,) + x.shape, x.dtype), x[None], (dev, 0, 0))


def _cast_into_slot(w, sel, after, name):
    r, c = w.shape
    hr = r // 2
    tr = _tile(hr, 256, 16)
    nr = hr // tr

    def body(sel_ref, w_ref, after_ref, o_ref):
        o_ref[...] = w_ref[...].astype(o_ref.dtype)

    return pl.pallas_call(
        body, name=name, out_shape=jax.ShapeDtypeStruct((4, 2, hr, c), _WIRE),
        grid_spec=pltpu.PrefetchScalarGridSpec(
            num_scalar_prefetch=1, grid=(2, nr),
            in_specs=[pl.BlockSpec((tr, c), lambda h, i, s: (h * nr + i, 0)), _ANY],
            out_specs=pl.BlockSpec((None, None, tr, c), lambda h, i, s: (s[0], h, i, 0))),
        compiler_params=_cp("parallel", "parallel"),
    )(sel, w, after)


def _hbm(a):
    return pltpu.with_memory_space_constraint(a, pltpu.HBM)


def _after(vec, token):
    return vec + token[0:1, 0:1]


def _gather_start(bufs, after, name):
    n = len(bufs)
    nc = 3 * n

    def body(*refs):
        ins, send, recv, token = refs[:n], refs[n + 1:n + 1 + nc], refs[n + 1 + nc:n + 1 + 2 * nc], refs[-1]
        mx, my, mc, chips = _place()
        j_me = 2 * mx + my
        for i in range(n):
            for k, chip in enumerate(chips):
                half = ins[i].at[j_me, mc]
                pltpu.make_async_remote_copy(
                    src_ref=half, dst_ref=half, send_sem=send[3 * i + k], recv_sem=recv[3 * i + k],
                    device_id=(*chip, mc), device_id_type=_MESH).start()
        token[...] = jnp.zeros_like(token)

    outs = pl.pallas_call(
        body, name=name,
        out_shape=tuple([pltpu.SemaphoreType.DMA(())] * (2 * nc) + [pltpu.HBM(b.shape, b.dtype) for b in bufs]
                        + [jax.ShapeDtypeStruct((SUBLANES, LANES), _F32)]),
        in_specs=tuple([_HBM] * n + [_ANY]), out_specs=tuple([_SEM] * (2 * nc) + [_HBM] * n + [_VMEM_WHOLE]),
        input_output_aliases={i: 2 * nc + i for i in range(n)},
        compiler_params=pltpu.CompilerParams(has_side_effects=_EFFECT),
    )(*[_hbm(b) for b in bufs], after)
    sems = [(outs[3 * i:3 * i + 3], outs[nc + 3 * i:nc + 3 * i + 3]) for i in range(n)]
    return sems, list(outs[2 * nc:2 * nc + n]), outs[-1]


def _gather_wait(sems, buf, after, name):
    send, recv = sems

    def body(buf_ref, s0, s1, s2, r0, r1, r2, after_ref, out_ref):
        mx, my, mc, chips = _place()
        j_me = 2 * mx + my
        for k, (chip, s_k, r_k) in enumerate(zip(chips, (s0, s1, s2), (r0, r1, r2))):
            cp = pltpu.make_async_remote_copy(
                src_ref=buf_ref.at[j_me, mc], dst_ref=buf_ref.at[2 * chip[0] + chip[1], mc], send_sem=s_k, recv_sem=r_k,
                device_id=(*chip, mc), device_id_type=_MESH)
            cp.wait_send()
            cp.wait_recv()

    return pl.pallas_call(
        body, name=name, out_shape=pltpu.HBM(buf.shape, buf.dtype),
        in_specs=(_HBM,) + (_SEM,) * 6 + (_ANY,), out_specs=_HBM, input_output_aliases={0: 0},
        compiler_params=pltpu.CompilerParams(has_side_effects=_EFFECT),
    )(buf, *send, *recv, after)


def _pair_forward(bufs, name):
    n = len(bufs)

    def body(*refs):
        ins, outs = refs[:n], refs[n:2 * n]
        send_sems, recv_sems = refs[2 * n:]
        mx, my, mc, chips = _place()
        sibling = (mx, my, 1 - mc)
        cps = []
        for i in range(n):
            for k, chip in enumerate(chips):
                j_k = 2 * chip[0] + chip[1]
                cp = pltpu.make_async_remote_copy(
                    src_ref=ins[i].at[j_k, mc], dst_ref=outs[i].at[j_k, mc], send_sem=send_sems.at[3 * i + k],
                    recv_sem=recv_sems.at[3 * i + k], device_id=sibling, device_id_type=_MESH)
                cp.start()
                cps.append(cp)
        for i in range(n):
            for k, chip in enumerate(chips):
                other = outs[i].at[2 * chip[0] + chip[1], 1 - mc]
                pltpu.make_async_remote_copy(
                    src_ref=other, dst_ref=other, send_sem=send_sems.at[3 * i + k], recv_sem=recv_sems.at[3 * i + k],
                    device_id=sibling, device_id_type=_MESH).wait_recv()
        for cp in cps:
            cp.wait_send()

    return pl.pallas_call(
        body, name=name, out_shape=[jax.ShapeDtypeStruct(b.shape, b.dtype) for b in bufs],
        in_specs=[_ANY] * n, out_specs=[_ANY] * n, input_output_aliases={i: i for i in range(n)},
        scratch_shapes=[pltpu.SemaphoreType.DMA((3 * n,)), pltpu.SemaphoreType.DMA((3 * n,))],
    )(*bufs)


def _gather8_peers(buf_ref, mx, my, mc, chips):
    mine = buf_ref.at[4 * mx + 2 * my + mc]
    peers = [((mx, my, 1 - mc), mine, buf_ref.at[4 * mx + 2 * my + 1 - mc])]
    peers += [((*chip, mc), mine, buf_ref.at[4 * chip[0] + 2 * chip[1] + mc]) for chip in chips]
    return peers


def _gather8_start(buf, name):
    def body(buf_ref, *rest):
        send, recv, token = rest[0:4], rest[4:8], rest[-1]
        mx, my, mc, chips = _place()
        for k, (peer, src, _) in enumerate(_gather8_peers(buf_ref, mx, my, mc, chips)):
            pltpu.make_async_remote_copy(src_ref=src, dst_ref=src, send_sem=send[k], recv_sem=recv[k],
                                         device_id=peer, device_id_type=_MESH).start()
        token[...] = jnp.zeros_like(token)

    outs = pl.pallas_call(
        body, name=name,
        out_shape=tuple([pltpu.SemaphoreType.DMA(())] * 8 + [pltpu.HBM(buf.shape, buf.dtype),
                                                             jax.ShapeDtypeStruct((SUBLANES, LANES), _F32)]),
        in_specs=(_HBM,), out_specs=tuple([_SEM] * 8 + [_HBM, _VMEM_WHOLE]), input_output_aliases={0: 8},
        compiler_params=pltpu.CompilerParams(has_side_effects=_EFFECT),
    )(_hbm(buf))
    return (outs[0:4], outs[4:8]), outs[8], outs[9]


def _gather8_wait(sems, buf, after, name):
    send, recv = sems

    def body(buf_ref, s0, s1, s2, s3, r0, r1, r2, r3, after_ref, out_ref):
        mx, my, mc, chips = _place()
        for (peer, src, dst), s_k, r_k in zip(_gather8_peers(buf_ref, mx, my, mc, chips), (s0, s1, s2, s3), (r0, r1, r2, r3)):
            cp = pltpu.make_async_remote_copy(src_ref=src, dst_ref=dst, send_sem=s_k, recv_sem=r_k,
                                              device_id=peer, device_id_type=_MESH)
            cp.wait_send()
            cp.wait_recv()

    return pl.pallas_call(
        body, name=name, out_shape=pltpu.HBM(buf.shape, buf.dtype),
        in_specs=(_HBM,) + (_SEM,) * 8 + (_ANY,), out_specs=_HBM, input_output_aliases={0: 0},
        compiler_params=pltpu.CompilerParams(has_side_effects=_EFFECT),
    )(buf, *send, *recv, after)


def _gather8_forward(buf, name):
    def body(in_ref, out_ref, send_sems, recv_sems):
        mx, my, mc, chips = _place()
        sibling = (mx, my, 1 - mc)
        cps = []
        for k, chip in enumerate(chips):
            idx = 4 * chip[0] + 2 * chip[1] + mc
            cp = pltpu.make_async_remote_copy(src_ref=in_ref.at[idx], dst_ref=out_ref.at[idx], send_sem=send_sems.at[k],
                                              recv_sem=recv_sems.at[k], device_id=sibling, device_id_type=_MESH)
            cp.start()
            cps.append(cp)
        for k, chip in enumerate(chips):
            other = out_ref.at[4 * chip[0] + 2 * chip[1] + 1 - mc]
            pltpu.make_async_remote_copy(src_ref=other, dst_ref=other, send_sem=send_sems.at[k], recv_sem=recv_sems.at[k],
                                         device_id=sibling, device_id_type=_MESH).wait_recv()
        for cp in cps:
            cp.wait_send()

    return pl.pallas_call(
        body, name=name, out_shape=jax.ShapeDtypeStruct(buf.shape, buf.dtype),
        in_specs=[_ANY], out_specs=_ANY, input_output_aliases={0: 0},
        scratch_shapes=[pltpu.SemaphoreType.DMA((3,)), pltpu.SemaphoreType.DMA((3,))],
    )(buf)


def _scatter_start(pair, name):
    land = lax.empty((3,) + pair.shape[1:], pair.dtype)

    def body(pair_ref, land_ref, s0, s1, s2, r0, r1, r2, pair_thru, land_thru, token):
        mx, my, mc, chips = _place()
        for k, (chip, s_k, r_k) in enumerate(zip(chips, (s0, s1, s2), (r0, r1, r2))):
            pltpu.make_async_remote_copy(
                src_ref=pair_ref.at[2 * chip[0] + chip[1]], dst_ref=land_ref.at[k], send_sem=s_k, recv_sem=r_k,
                device_id=(*chip, mc), device_id_type=_MESH).start()
        token[...] = jnp.zeros_like(token)

    outs = pl.pallas_call(
        body, name=name,
        out_shape=tuple([pltpu.SemaphoreType.DMA(())] * 6 + [pltpu.HBM(pair.shape, pair.dtype), pltpu.HBM(land.shape, land.dtype),
                                                             jax.ShapeDtypeStruct((SUBLANES, LANES), _F32)]),
        in_specs=(_HBM, _HBM), out_specs=tuple([_SEM] * 6 + [_HBM, _HBM, _VMEM_WHOLE]),
        input_output_aliases={0: 6, 1: 7}, compiler_params=pltpu.CompilerParams(has_side_effects=_EFFECT),
    )(_hbm(pair), _hbm(land))
    return (outs[0:3], outs[3:6]), outs[6], outs[7], outs[8]


def _scatter_wait(sems, pair, land, after, name):
    send, recv = sems

    def body(pair_ref, land_ref, s0, s1, s2, r0, r1, r2, after_ref, pair_out, land_out):
        mx, my, mc, chips = _place()
        for k, (chip, s_k, r_k) in enumerate(zip(chips, (s0, s1, s2), (r0, r1, r2))):
            cp = pltpu.make_async_remote_copy(
                src_ref=pair_ref.at[2 * chip[0] + chip[1]], dst_ref=land_ref.at[k], send_sem=s_k, recv_sem=r_k,
                device_id=(*chip, mc), device_id_type=_MESH)
            cp.wait_send()
            cp.wait_recv()

    return pl.pallas_call(
        body, name=name, out_shape=(pltpu.HBM(pair.shape, pair.dtype), pltpu.HBM(land.shape, land.dtype)),
        in_specs=(_HBM, _HBM) + (_SEM,) * 6 + (_ANY,), out_specs=(_HBM, _HBM), input_output_aliases={0: 0, 1: 1},
        compiler_params=pltpu.CompilerParams(has_side_effects=_EFFECT),
    )(pair, land, *send, *recv, after)


def _sibling_copy(src_ref, dst_ref, send_sem, recv_sem):
    mx, my, mc, _ = _place()
    return pltpu.make_async_remote_copy(src_ref=src_ref, dst_ref=dst_ref, send_sem=send_sem, recv_sem=recv_sem,
                                        device_id=(mx, my, 1 - mc), device_id_type=_MESH)


def _swap_start(g, name):
    land = lax.empty(g.shape[1:], g.dtype)

    def body(g_ref, land_ref, send_sem, recv_sem, g_thru, land_thru, token):
        _sibling_copy(g_ref.at[1 - lax.axis_index("c")], land_ref, send_sem, recv_sem).start()
        token[...] = jnp.zeros_like(token)

    outs = pl.pallas_call(
        body, name=name,
        out_shape=(pltpu.SemaphoreType.DMA(()), pltpu.SemaphoreType.DMA(()), pltpu.HBM(g.shape, g.dtype),
                   pltpu.HBM(land.shape, land.dtype), jax.ShapeDtypeStruct((SUBLANES, LANES), _F32)),
        in_specs=(_HBM, _HBM), out_specs=(_SEM, _SEM, _HBM, _HBM, _VMEM_WHOLE), input_output_aliases={0: 2, 1: 3},
        compiler_params=pltpu.CompilerParams(has_side_effects=_EFFECT),
    )(_hbm(g), _hbm(land))
    return (outs[0], outs[1]), outs[2], outs[3], outs[4]


def _swap_wait(sems, g, land, after, name):
    def body(g_ref, land_ref, send_sem, recv_sem, after_ref, g_out, land_out):
        cp = _sibling_copy(g_ref.at[1 - lax.axis_index("c")], land_ref, send_sem, recv_sem)
        cp.wait_send()
        cp.wait_recv()

    return pl.pallas_call(
        body, name=name, out_shape=(pltpu.HBM(g.shape, g.dtype), pltpu.HBM(land.shape, land.dtype)),
        in_specs=(_HBM, _HBM, _SEM, _SEM, _ANY), out_specs=(_HBM, _HBM), input_output_aliases={0: 0, 1: 1},
        compiler_params=pltpu.CompilerParams(has_side_effects=_EFFECT),
    )(g, land, *sems, after)


def _join_start(buf, name):
    def body(buf_ref, send_sem, recv_sem, buf_thru, token):
        mine = buf_ref.at[lax.axis_index("c")]
        _sibling_copy(mine, mine, send_sem, recv_sem).start()
        token[...] = jnp.zeros_like(token)

    outs = pl.pallas_call(
        body, name=name,
        out_shape=(pltpu.SemaphoreType.DMA(()), pltpu.SemaphoreType.DMA(()), pltpu.HBM(buf.shape, buf.dtype),
                   jax.ShapeDtypeStruct((SUBLANES, LANES), _F32)),
        in_specs=(_HBM,), out_specs=(_SEM, _SEM, _HBM, _VMEM_WHOLE), input_output_aliases={0: 2},
        compiler_params=pltpu.CompilerParams(has_side_effects=_EFFECT),
    )(_hbm(buf))
    return (outs[0], outs[1]), outs[2], outs[3]


def _join_wait(sems, buf, after, name):
    def body(buf_ref, send_sem, recv_sem, after_ref, buf_out):
        mc = lax.axis_index("c")
        cp = _sibling_copy(buf_ref.at[mc], buf_ref.at[1 - mc], send_sem, recv_sem)
        cp.wait_send()
        cp.wait_recv()

    return pl.pallas_call(
        body, name=name, out_shape=pltpu.HBM(buf.shape, buf.dtype),
        in_specs=(_HBM, _SEM, _SEM, _ANY), out_specs=_HBM, input_output_aliases={0: 0},
        compiler_params=pltpu.CompilerParams(has_side_effects=_EFFECT),
    )(buf, *sems, after)


def _pair_sum(g, got, sel, name):
    _, four, hr, c = g.shape
    tr = _tile(hr, 512, 16)

    def body(sel_ref, g_ref, p_ref, o_ref):
        o_ref[...] = (g_ref[...].astype(_F32) + p_ref[...].astype(_F32)).astype(o_ref.dtype)

    return pl.pallas_call(
        body, name=name, out_shape=jax.ShapeDtypeStruct((four, hr, c), g.dtype),
        grid_spec=pltpu.PrefetchScalarGridSpec(
            num_scalar_prefetch=1, grid=(four, hr // tr),
            in_specs=[pl.BlockSpec((None, None, tr, c), lambda j, i, s: (s[1], j, i, 0)),
                      pl.BlockSpec((None, tr, c), lambda j, i, s: (j, i, 0))],
            out_specs=pl.BlockSpec((None, tr, c), lambda j, i, s: (j, i, 0))),
        compiler_params=_cp("parallel", "parallel"),
    )(sel, g, got)


def _chip_sum(pair, got, sel, name):
    _, hr, c = pair.shape
    tr = _tile(hr, 512, 16)

    def body(sel_ref, p_ref, q_ref, o_ref):
        o_ref[...] = ((p_ref[...].astype(_F32) + q_ref[0].astype(_F32)) + q_ref[1].astype(_F32)) + q_ref[2].astype(_F32)

    return pl.pallas_call(
        body, name=name, out_shape=jax.ShapeDtypeStruct((2, hr, c), _F32),
        grid_spec=pltpu.PrefetchScalarGridSpec(
            num_scalar_prefetch=1, grid=(hr // tr,),
            in_specs=[pl.BlockSpec((None, tr, c), lambda i, s: (s[0], i, 0)),
                      pl.BlockSpec((3, tr, c), lambda i, s: (0, i, 0))],
            out_specs=pl.BlockSpec((None, tr, c), lambda i, s: (s[1], i, 0))),
        compiler_params=_cp("parallel"),
    )(sel, pair, got)


def _matmul(a, b, dims, out_struct, grid, a_spec, b_spec, o_spec, acc_shape, k_axis, name, after=None):
    nk = grid[k_axis]
    extra = [] if after is None else [after]

    def body(a_ref, b_ref, *rest):
        o_ref, acc = rest[len(extra)], rest[len(extra) + 1:]
        prod = lax.dot_general(a_ref[...].astype(_MXU), b_ref[...].astype(_MXU), dims, preferred_element_type=_F32)
        if nk == 1:
            o_ref[...] = prod.astype(o_ref.dtype)
        else:
            acc_ref, = acc
            k = pl.program_id(k_axis)

            @pl.when(k == 0)
            def _():
                acc_ref[...] = prod

            @pl.when(jnp.logical_and(k > 0, k < nk - 1))
            def _():
                acc_ref[...] += prod

            @pl.when(k == nk - 1)
            def _():
                o_ref[...] = (acc_ref[...] + prod).astype(o_ref.dtype)

    sem = ["parallel"] * len(grid)
    sem[k_axis] = "arbitrary"
    return pl.pallas_call(
        body, name=name, out_shape=out_struct, grid=grid, in_specs=[a_spec, b_spec] + [_ANY] * len(extra), out_specs=o_spec,
        scratch_shapes=[pltpu.VMEM(acc_shape, _F32)] if nk > 1 else [], compiler_params=_cp(*sem),
    )(a, b, *extra)


def _mm_nn(a, w4, out_dtype, name, tm=512, tn=1536, tk=2048, after=None):
    m, k = a.shape
    j, _, ns = w4.shape
    tm, tn, tk = _tile(m, tm, 16), _tile(ns, tn), _tile(k, tk)
    nps = ns // tn
    return _matmul(
        a, w4, (((1,), (0,)), ((), ())), jax.ShapeDtypeStruct((m, j * ns), out_dtype),
        (m // tm, j * nps, k // tk),
        pl.BlockSpec((tm, tk), lambda mi, ni, ki: (mi, ki)),
        pl.BlockSpec((None, tk, tn), lambda mi, ni, ki: (ni // nps, ki, ni % nps)),
        pl.BlockSpec((tm, tn), lambda mi, ni, ki: (mi, ni)), (tm, tn), 2, name, after)


def _mm_nt(a, w4, out_dtype, name, tm=512, tn=2048, tk=1536, after=None):
    m = a.shape[-2]
    j, kw, ns = w4.shape
    tm, tn, tk = _tile(m, tm, 16), _tile(kw, tn), _tile(ns, tk)
    kps = ns // tk
    if a.ndim == 3:
        kph = a.shape[2] // tk
        a_spec = pl.BlockSpec((None, tm, tk), lambda mi, ni, ki: (ki // kph, mi, ki % kph))
    else:
        a_spec = pl.BlockSpec((tm, tk), lambda mi, ni, ki: (mi, ki))
    return _matmul(
        a, w4, (((1,), (1,)), ((), ())), jax.ShapeDtypeStruct((m, kw), out_dtype),
        (m // tm, kw // tn, j * kps),
        a_spec,
        pl.BlockSpec((None, tn, tk), lambda mi, ni, ki: (ki // kps, ni, ki % kps)),
        pl.BlockSpec((tm, tn), lambda mi, ni, ki: (mi, ni)), (tm, tn), 2, name, after)


def _mm_tn_cols(a, b, name, tm=1024, tn=1536, tk=2048):
    m, ka = a.shape
    ns = (b.shape[-1] * (2 if b.ndim == 3 else 1)) // 4
    hr = ka // 2
    tm, tn, tk = _tile(hr, tm), _tile(ns, tn), _tile(m, tk, 16)
    mph, nps = hr // tm, ns // tn
    if b.ndim == 3:
        b_spec = pl.BlockSpec((None, tk, tn), lambda ni, mi, ki: (ni // (2 * nps), ki, ni % (2 * nps)))
    else:
        b_spec = pl.BlockSpec((tk, tn), lambda ni, mi, ki: (ki, ni))
    return _matmul(
        a, b, (((0,), (0,)), ((), ())), jax.ShapeDtypeStruct((2, 4, hr, ns), _WIRE),
        (4 * nps, 2 * mph, m // tk),
        pl.BlockSpec((tk, tm), lambda ni, mi, ki: (ki, mi)),
        b_spec,
        pl.BlockSpec((None, None, tm, tn), lambda ni, mi, ki: (mi // mph, ni // nps, mi % mph, ni % nps)),
        (tm, tn), 2, name)


def _mm_tn_rows(a, b, name, tm=1536, tn=1024, tk=2048):
    m, ka = a.shape
    r = ka // 4
    hc = b.shape[1] // 2
    tm, tn, tk = _tile(r, tm), _tile(hc, tn), _tile(m, tk, 16)
    mpr, nph = r // tm, hc // tn
    return _matmul(
        a, b, (((0,), (0,)), ((), ())), jax.ShapeDtypeStruct((2, 4, r, hc), _WIRE),
        (2 * nph, 4 * mpr, m // tk),
        pl.BlockSpec((tk, tm), lambda ni, mi, ki: (ki, mi)),
        pl.BlockSpec((tk, tn), lambda ni, mi, ki: (ki, ni)),
        pl.BlockSpec((None, None, tm, tn), lambda ni, mi, ki: (ni // nph, mi // mpr, mi % mpr, ni % nph)),
        (tm, tn), 2, name)


def _row_call(body, name, rows, ins, outs, tm=256):
    tm = _tile(rows, tm, 16)

    def spec(shape, kind):
        if kind == "rows":
            return pl.BlockSpec((tm, shape[1]), lambda i: (i, 0))
        return pl.BlockSpec(shape, lambda i: (0,) * len(shape))

    return pl.pallas_call(
        body, name=name, grid=(rows // tm,),
        in_specs=[spec(a.shape, kind) for a, kind in ins],
        out_specs=[spec(o.shape, kind) for o, kind in outs],
        out_shape=[o for o, _ in outs],
        compiler_params=_cp("arbitrary"),
    )(*[a for a, _ in ins])


def _rms(x):
    r = lax.rsqrt(_rowmean(x * x) + EPS)
    return x * r, r


def _rms_bwd(dxh, xh, r):
    return r * (dxh - xh * _rowmean(dxh * xh))


def _fwd_pre_mix(x, g, sc, sh):
    s, d = x.shape

    def body(x_ref, g_ref, sc_ref, sh_ref, h_ref):
        xh, _ = _rms(x_ref[...])
        h_ref[...] = (xh * g_ref[...] * (1.0 + sc_ref[...]) + sh_ref[...]).astype(h_ref.dtype)

    return _row_call(body, "fwd_pre_mix", s, [(x, "rows"), (g, "vec"), (sc, "vec"), (sh, "vec")],
                     [(jax.ShapeDtypeStruct((s, d), _MXU), "rows")])[0]


def _fwd_mid(o, x, gt1, g_post, g_pre2, sc2, sh2):
    s, d = x.shape

    def body(o_ref, x_ref, gt_ref, gp_ref, g2_ref, sc_ref, sh_ref, x1_ref, h2_ref):
        oh, _ = _rms(o_ref[...])
        x1 = x_ref[...] + gt_ref[...] * (oh * gp_ref[...])
        x1_ref[...] = x1
        xh, _ = _rms(x1)
        h2_ref[...] = (xh * g2_ref[...] * (1.0 + sc_ref[...]) + sh_ref[...]).astype(h2_ref.dtype)

    return _row_call(body, "fwd_mid", s,
                     [(o, "rows"), (x, "rows"), (gt1, "vec"), (g_post, "vec"), (g_pre2, "vec"), (sc2, "vec"),
                      (sh2, "vec")],
                     [(jax.ShapeDtypeStruct((s, d), _F32), "rows"), (jax.ShapeDtypeStruct((s, d), _MXU), "rows")])


def _loss_and_post_ffn_bwd(f, x1, tgt, gt2, g_post):
    s, d = x1.shape

    def body(f_ref, x1_ref, t_ref, gt_ref, g_ref, dx2_ref, df_ref, dgt_ref, dg_ref, loss_ref):
        first = pl.program_id(0) == 0
        fh, r = _rms(f_ref[...])
        n = fh * g_ref[...]
        e = x1_ref[...] + gt_ref[...] * n - t_ref[...]
        _acc(loss_ref, first, jnp.sum(_colsum(e * e), axis=1, keepdims=True) * (0.5 / d))
        dx2 = e * (1.0 / d)
        dx2_ref[...] = dx2
        _acc(dgt_ref, first, _colsum(dx2 * n))
        dn = dx2 * gt_ref[...]
        _acc(dg_ref, first, _colsum(dn * fh))
        df_ref[...] = _rms_bwd(dn * g_ref[...], fh, r).astype(df_ref.dtype)

    vec = jax.ShapeDtypeStruct((1, d), _F32)
    return _row_call(body, "loss_post_ffn_bwd", s,
                     [(f, "rows"), (x1, "rows"), (tgt, "rows"), (gt2, "vec"), (g_post, "vec")],
                     [(jax.ShapeDtypeStruct((s, d), _F32), "rows"), (jax.ShapeDtypeStruct((s, d), _MXU), "rows"),
                      (vec, "vec"), (vec, "vec"), (jax.ShapeDtypeStruct((1, 1), _F32), "vec")])


def _bwd_mid(dh2, x1, dx2, o, g_pre2, sc2, gt1, g_post):
    s, d = x1.shape

    def body(dh_ref, x1_ref, dx2_ref, o_ref, g2_ref, sc_ref, gt_ref, gp_ref,
             dx1_ref, do_ref, dsc_ref, dsh_ref, dg2_ref, dgt_ref, dgp_ref):
        first = pl.program_id(0) == 0
        dh = dh_ref[...]
        xh, r = _rms(x1_ref[...])
        _acc(dsh_ref, first, _colsum(dh))
        _acc(dsc_ref, first, _colsum(dh * (xh * g2_ref[...])))
        dn = dh * (1.0 + sc_ref[...])
        _acc(dg2_ref, first, _colsum(dn * xh))
        dx1 = dx2_ref[...] + _rms_bwd(dn * g2_ref[...], xh, r)
        dx1_ref[...] = dx1
        oh, ro = _rms(o_ref[...])
        _acc(dgt_ref, first, _colsum(dx1 * (oh * gp_ref[...])))
        dno = dx1 * gt_ref[...]
        _acc(dgp_ref, first, _colsum(dno * oh))
        do_ref[...] = _rms_bwd(dno * gp_ref[...], oh, ro).astype(do_ref.dtype)

    vec = jax.ShapeDtypeStruct((1, d), _F32)
    return _row_call(body, "bwd_mid", s,
                     [(dh2, "rows"), (x1, "rows"), (dx2, "rows"), (o, "rows"), (g_pre2, "vec"), (sc2, "vec"),
                      (gt1, "vec"), (g_post, "vec")],
                     [(jax.ShapeDtypeStruct((s, d), _F32), "rows"), (jax.ShapeDtypeStruct((s, d), _MXU), "rows"),
                      (vec, "vec"), (vec, "vec"), (vec, "vec"), (vec, "vec"), (vec, "vec")])


def _bwd_pre_mix(dh1, x, dx1, g, sc1):
    s, d = x.shape

    def body(dh_ref, x_ref, dx1_ref, g_ref, sc_ref, dx_ref, dsc_ref, dsh_ref, dg_ref):
        first = pl.program_id(0) == 0
        dh = dh_ref[...]
        xh, r = _rms(x_ref[...])
        _acc(dsh_ref, first, _colsum(dh))
        _acc(dsc_ref, first, _colsum(dh * (xh * g_ref[...])))
        dn = dh * (1.0 + sc_ref[...])
        _acc(dg_ref, first, _colsum(dn * xh))
        dx_ref[...] = dx1_ref[...] + _rms_bwd(dn * g_ref[...], xh, r)

    vec = jax.ShapeDtypeStruct((1, d), _F32)
    return _row_call(body, "bwd_pre_mix", s,
                     [(dh1, "rows"), (x, "rows"), (dx1, "rows"), (g, "vec"), (sc1, "vec")],
                     [(jax.ShapeDtypeStruct((s, d), _F32), "rows"), (vec, "vec"), (vec, "vec"), (vec, "vec")])


def _mix_norm_fwd(y_ssm, y_sgu, g_ssm, g_sgu):
    s, h = y_ssm.shape

    def body(a_ref, b_ref, ga_ref, gb_ref, o_ref):
        ah, _ = _rms(a_ref[...])
        bh, _ = _rms(b_ref[...])
        o_ref[:, 0:h] = (ah * ga_ref[...]).astype(o_ref.dtype)
        o_ref[:, h:2 * h] = (bh * gb_ref[...]).astype(o_ref.dtype)

    return _row_call(body, "mix_norm_fwd", s, [(y_ssm, "rows"), (y_sgu, "rows"), (g_ssm, "vec"), (g_sgu, "vec")],
                     [(jax.ShapeDtypeStruct((s, 2 * h), _MXU), "rows")])[0]


def _mix_norm_bwd(dyc, y_ssm, y_sgu, g_ssm, g_sgu):
    s, h = y_ssm.shape

    def body(d_ref, a_ref, b_ref, ga_ref, gb_ref, da_ref, db_ref, dga_ref, dgb_ref):
        first = pl.program_id(0) == 0
        for lo, y_ref, g_ref, dy_ref, dg_ref in ((0, a_ref, ga_ref, da_ref, dga_ref), (h, b_ref, gb_ref, db_ref, dgb_ref)):
            d = d_ref[:, lo:lo + h]
            yh, r = _rms(y_ref[...])
            _acc(dg_ref, first, _colsum(d * yh))
            dy_ref[...] = _rms_bwd(d * g_ref[...], yh, r)

    vec = jax.ShapeDtypeStruct((1, h), _F32)
    full = jax.ShapeDtypeStruct((s, h), _F32)
    return _row_call(body, "mix_norm_bwd", s,
                     [(dyc, "rows"), (y_ssm, "rows"), (y_sgu, "rows"), (g_ssm, "vec"), (g_sgu, "vec")],
                     [(full, "rows"), (full, "rows"), (vec, "vec"), (vec, "vec")])


CONV_ROWS = 64


def _conv_rows(ext, w_ref, b_ref):
    x = ext[SUBLANES:]
    s1 = pltpu.roll(ext, 1, 0)[SUBLANES:]
    s2 = pltpu.roll(ext, 2, 0)[SUBLANES:]
    return b_ref[...] + w_ref[0:1, :] * s2 + w_ref[1:2, :] * s1 + w_ref[2:3, :] * x, x, s1, s2


def _conv_window(x_ref, r0):
    if isinstance(r0, int):
        assert r0 == 0
        return jnp.concatenate([jnp.zeros((SUBLANES, x_ref.shape[1]), _F32), x_ref[0:CONV_ROWS, :]], axis=0)
    return x_ref[pl.ds(pl.multiple_of(r0 - SUBLANES, SUBLANES), CONV_ROWS + SUBLANES), :]


def _conv_act_fwd(up_pre, conv_w, conv_b):
    s, f2 = up_pre.shape
    f = f2 // 2
    tc = _tile(f, 256)
    nf = f // tc

    def shift_down(x, k):
        row = lax.broadcasted_iota(jnp.int32, x.shape, 0)
        return jnp.where(row >= k, pltpu.roll(x, k, 0), 0.0)

    def conv(x, w_ref, b_ref):
        return b_ref[...] + w_ref[0:1, :] * shift_down(x, 2) + w_ref[1:2, :] * shift_down(x, 1) + w_ref[2:3, :] * x

    def body(a_ref, b_ref, wa_ref, wb_ref, ba_ref, bb_ref, o_ref):
        a = conv(a_ref[...], wa_ref, ba_ref)
        b = conv(b_ref[...], wb_ref, bb_ref)
        o_ref[...] = (a * _sigmoid(a) * b).astype(o_ref.dtype)

    return pl.pallas_call(
        body, name="conv_act_fwd", grid=(nf,), out_shape=jax.ShapeDtypeStruct((s, f), _MXU),
        in_specs=[pl.BlockSpec((s, tc), lambda n: (0, n)), pl.BlockSpec((s, tc), lambda n: (0, n + nf)),
                  pl.BlockSpec((3, tc), lambda n: (0, n)), pl.BlockSpec((3, tc), lambda n: (0, n + nf)),
                  pl.BlockSpec((1, tc), lambda n: (0, n)), pl.BlockSpec((1, tc), lambda n: (0, n + nf))],
        out_specs=pl.BlockSpec((s, tc), lambda n: (0, n)), compiler_params=_cp("parallel"),
    )(up_pre, up_pre, conv_w, conv_w, conv_b, conv_b)


def _conv_act_bwd(up_pre, d_act, conv_w, conv_b):
    s, f2 = up_pre.shape
    f = f2 // 2
    tc = _tile(f, 256)
    nf = f // tc

    def body(a_ref, b_ref, d_ref, wa_ref, wb_ref, ba_ref, bb_ref,
             du_ref, w0a, w0b, w1a, w1b, w2a, w2b, dba, dbb):
        n = s // CONV_ROWS
        zero8 = jnp.zeros((SUBLANES, tc), _F32)
        ext_rows = CONV_ROWS + SUBLANES

        def fold(x):
            out = x[0:SUBLANES]
            for k in range(1, CONV_ROWS // SUBLANES):
                out = out + x[k * SUBLANES:(k + 1) * SUBLANES]
            return out

        def chunk(r0, carry):
            nxt, acc = carry
            a, xa, xa1, xa2 = _conv_rows(_conv_window(a_ref, r0), wa_ref, ba_ref)
            b, xb, xb1, xb2 = _conv_rows(_conv_window(b_ref, r0), wb_ref, bb_ref)
            sg = _sigmoid(a)
            d = d_ref[pl.ds(r0, CONV_ROWS), :]
            du_a = d * b * (sg * (1.0 + a * (1.0 - sg)))
            du_b = d * (a * sg)
            new_acc = []
            for h, (du, x0, x1, x2, w_ref) in enumerate(((du_a, xa, xa1, xa2, wa_ref), (du_b, xb, xb1, xb2, wb_ref))):
                ext = jnp.concatenate([du, nxt[h]], axis=0)
                u1 = pltpu.roll(ext, ext_rows - 1, 0)[:CONV_ROWS]
                u2 = pltpu.roll(ext, ext_rows - 2, 0)[:CONV_ROWS]
                du_ref[h, pl.ds(r0, CONV_ROWS), :] = (w_ref[2:3, :] * du + w_ref[1:2, :] * u1
                                                      + w_ref[0:1, :] * u2).astype(du_ref.dtype)
                new_acc += [acc[4 * h] + fold(du * x2), acc[4 * h + 1] + fold(du * x1), acc[4 * h + 2] + fold(du * x0),
                            acc[4 * h + 3] + fold(du)]
            return (du_a[:SUBLANES], du_b[:SUBLANES]), tuple(new_acc)

        def step(i, carry):
            return chunk(pl.multiple_of((n - 1 - i) * CONV_ROWS, CONV_ROWS), carry)

        carry = lax.fori_loop(0, n - 1, step, ((zero8, zero8), (zero8,) * 8))
        _, acc = chunk(0, carry)
        for ref, val in zip((w0a, w1a, w2a, dba, w0b, w1b, w2b, dbb), acc):
            ref[...] = _colsum(val)

    col_a = pl.BlockSpec((s, tc), lambda n: (0, n))
    col_b = pl.BlockSpec((s, tc), lambda n: (0, n + nf))
    vec_a = pl.BlockSpec((1, tc), lambda n: (0, n))
    vec_b = pl.BlockSpec((1, tc), lambda n: (0, n + nf))
    vec = jax.ShapeDtypeStruct((1, f), _F32)
    outs = pl.pallas_call(
        body, name="conv_act_bwd", grid=(nf,),
        in_specs=[col_a, col_b, col_a, pl.BlockSpec((3, tc), lambda n: (0, n)),
                  pl.BlockSpec((3, tc), lambda n: (0, n + nf)), vec_a, vec_b],
        out_specs=[pl.BlockSpec((2, s, tc), lambda n: (0, 0, n))] + [vec_a] * 8,
        out_shape=[jax.ShapeDtypeStruct((2, s, f), _MXU)] + [vec] * 8, compiler_params=_cp("parallel"),
    )(up_pre, up_pre, d_act, conv_w, conv_w, conv_b, conv_b)
    du, w0a, w0b, w1a, w1b, w2a, w2b, dba, dbb = outs
    cat = lambda p, q: jnp.concatenate([p, q], axis=1)
    return du, cat(w0a, w0b), cat(w1a, w1b), cat(w2a, w2b), cat(dba, dbb)


def _sgu_recompute(zu_ref, zv_ref, lng_ref, lnb_ref, wm_ref, bs_ref, nh):
    zu, zv = zu_ref[...], zv_ref[...]
    u = _gelu(zu)
    gv = _gelu(zv)
    xc = gv - _rowmean(gv)
    rs = lax.rsqrt(_rowmean(xc * xc) + EPS)
    vh = xc * rs
    v = vh * lng_ref[...] + lnb_ref[...]
    mixed = []
    for h in range(nh):
        vhd = v[:, h * CHUNK:(h + 1) * CHUNK].astype(_MXU)
        mixed.append(jnp.dot(wm_ref[h].astype(_MXU), vhd, preferred_element_type=_F32) + bs_ref[h])
    return zu, zv, u, vh, rs, v, mixed


def _sgu_fwd(z, ln_g, ln_b, wm, bs):
    s = z.shape[0]
    nh = wm.shape[0]
    hd = nh * CHUNK

    def body(zu_ref, zv_ref, lng_ref, lnb_ref, wm_ref, bs_ref, y_ref):
        _, _, u, _, _, _, mixed = _sgu_recompute(zu_ref, zv_ref, lng_ref, lnb_ref, wm_ref, bs_ref, nh)
        for h in range(nh):
            y_ref[:, h * CHUNK:(h + 1) * CHUNK] = u[:, h * CHUNK:(h + 1) * CHUNK] * mixed[h]

    vec = pl.BlockSpec((1, hd), lambda i: (0, 0))
    return pl.pallas_call(
        body, name="sgu_fwd", grid=(s // CHUNK,), out_shape=jax.ShapeDtypeStruct((s, hd), _F32),
        in_specs=[pl.BlockSpec((CHUNK, hd), lambda i: (i, 1)), pl.BlockSpec((CHUNK, hd), lambda i: (i, 2)), vec, vec,
                  pl.BlockSpec((nh, CHUNK, CHUNK), lambda i: (0, 0, 0)), pl.BlockSpec((nh, CHUNK, 1), lambda i: (0, 0, 0))],
        out_specs=pl.BlockSpec((CHUNK, hd), lambda i: (i, 0)), compiler_params=_cp("parallel"),
    )(z, z, ln_g, ln_b, wm, bs)


def _sgu_bwd(z, dy, dz_ssm, ln_g, ln_b, wm, bs):
    s = z.shape[0]
    nh = wm.shape[0]
    hd = nh * CHUNK

    def body(zu_ref, zv_ref, dy_ref, dzs_ref, lng_ref, lnb_ref, wm_ref, bs_ref,
             dz_ref, dlg_ref, dlb_ref, dwm_ref, dbs_ref, dv_scr):
        first = pl.program_id(0) == 0
        zu, zv, u, vh, rs, v, mixed = _sgu_recompute(zu_ref, zv_ref, lng_ref, lnb_ref, wm_ref, bs_ref, nh)
        dy = dy_ref[...]
        dz_ref[:, 0:hd] = dzs_ref[...]
        for h in range(nh):
            cols = slice(h * CHUNK, (h + 1) * CHUNK)
            dyh = dy[:, cols]
            dz_ref[:, hd + h * CHUNK:hd + (h + 1) * CHUNK] = (dyh * mixed[h] * _gelu_grad(zu[:, cols])).astype(dz_ref.dtype)
            dm = dyh * u[:, cols]
            dmx = dm.astype(_MXU)
            _acc(dbs_ref.at[h], first, jnp.sum(dm, axis=1, keepdims=True))
            _acc(dwm_ref.at[h], first,
                 lax.dot_general(dmx, v[:, cols].astype(_MXU), (((1,), (1,)), ((), ())), preferred_element_type=_F32))
            dv_scr[:, cols] = lax.dot_general(wm_ref[h].astype(_MXU), dmx, (((0,), (0,)), ((), ())),
                                              preferred_element_type=_F32)
        dv = dv_scr[...]
        _acc(dlg_ref, first, _colsum(dv * vh))
        _acc(dlb_ref, first, _colsum(dv))
        dvh = dv * lng_ref[...]
        dgv = rs * (dvh - _rowmean(dvh) - vh * _rowmean(dvh * vh))
        dz_ref[:, 2 * hd:3 * hd] = (dgv * _gelu_grad(zv)).astype(dz_ref.dtype)

    vec = pl.BlockSpec((1, hd), lambda i: (0, 0))
    wspec = pl.BlockSpec((nh, CHUNK, CHUNK), lambda i: (0, 0, 0))
    bspec = pl.BlockSpec((nh, CHUNK, 1), lambda i: (0, 0, 0))
    rows = pl.BlockSpec((CHUNK, hd), lambda i: (i, 0))
    return pl.pallas_call(
        body, name="sgu_bwd", grid=(s // CHUNK,),
        out_shape=[jax.ShapeDtypeStruct((s, 3 * hd), _MXU), jax.ShapeDtypeStruct((1, hd), _F32),
                   jax.ShapeDtypeStruct((1, hd), _F32), jax.ShapeDtypeStruct((nh, CHUNK, CHUNK), _F32),
                   jax.ShapeDtypeStruct((nh, CHUNK, 1), _F32)],
        in_specs=[pl.BlockSpec((CHUNK, hd), lambda i: (i, 1)), pl.BlockSpec((CHUNK, hd), lambda i: (i, 2)),
                  rows, rows, vec, vec, wspec, bspec],
        out_specs=[pl.BlockSpec((CHUNK, 3 * hd), lambda i: (i, 0)), vec, vec, wspec, bspec],
        scratch_shapes=[pltpu.VMEM((CHUNK, hd), _F32)], compiler_params=_cp("arbitrary"),
    )(z, z, dy, dz_ssm, ln_g, ln_b, wm, bs)


def _ssm_prep(log_dt, a_re, a_im, b_re_t, b_im_t):
    gn = a_re.shape[1]

    def body(ldt_ref, are_ref, aim_ref, br_ref, bi_ref, pr_ref, pi_ref, bbr_ref, bbi_ref):
        dt = jnp.exp(ldt_ref[...])
        are, aim = are_ref[...], aim_ref[...]
        k = (lax.broadcasted_iota(jnp.int32, (SUBLANES, gn), 0) + 1).astype(_F32)
        mag = jnp.exp(k * (are * dt))
        ang = k * (aim * dt)
        pr_ref[...] = mag * jnp.cos(ang)
        pi_ref[...] = mag * jnp.sin(ang)
        m1 = jnp.exp(are * dt)
        lr, li = m1 * jnp.cos(aim * dt), m1 * jnp.sin(aim * dt)
        den = are * are + aim * aim
        nr = lr - 1.0
        f_re = (nr * are + li * aim) / den
        f_im = (li * are - nr * aim) / den
        bbr_ref[...] = f_re * br_ref[...] - f_im * bi_ref[...]
        bbi_ref[...] = f_re * bi_ref[...] + f_im * br_ref[...]

    pw = jax.ShapeDtypeStruct((SUBLANES, gn), _F32)
    bb = jax.ShapeDtypeStruct(b_re_t.shape, _F32)
    return pl.pallas_call(body, name="ssm_prep", out_shape=[pw, pw, bb, bb])(log_dt, a_re, a_im, b_re_t, b_im_t)


def _ssm_prep_bwd(log_dt, a_re, a_im, b_re_t, b_im_t, d_bbr, d_bbi, d_lr, d_li):
    def body(ldt_ref, are_ref, aim_ref, br_ref, bi_ref, dbr_ref, dbi_ref, dlr_ref, dli_ref,
             obr_ref, obi_ref, oar_ref, oai_ref, odt_ref):
        dt = jnp.exp(ldt_ref[...])
        are, aim = are_ref[...], aim_ref[...]
        m1 = jnp.exp(are * dt)
        lr, li = m1 * jnp.cos(aim * dt), m1 * jnp.sin(aim * dt)
        den = are * are + aim * aim
        nr = lr - 1.0
        f_re = (nr * are + li * aim) / den
        f_im = (li * are - nr * aim) / den
        br, bi, dbr, dbi = br_ref[...], bi_ref[...], dbr_ref[...], dbi_ref[...]
        obr_ref[...] = f_re * dbr + f_im * dbi
        obi_ref[...] = f_re * dbi - f_im * dbr
        gf_re = _colsum(br * dbr + bi * dbi)
        gf_im = _colsum(br * dbi - bi * dbr)
        il_re, il_im = are / den, -aim / den
        glb_re = dlr_ref[...] + (il_re * gf_re + il_im * gf_im)
        glb_im = dli_ref[...] + (il_re * gf_im - il_im * gf_re)
        q_re = -(f_re * il_re - f_im * il_im)
        q_im = -(f_re * il_im + f_im * il_re)
        gl_re = q_re * gf_re + q_im * gf_im
        gl_im = q_re * gf_im - q_im * gf_re
        gl_re = gl_re + dt * (lr * glb_re + li * glb_im)
        gl_im = gl_im + dt * (lr * glb_im - li * glb_re)
        w_re = are * lr - aim * li
        w_im = are * li + aim * lr
        oar_ref[...] = gl_re
        oai_ref[...] = gl_im
        odt_ref[...] = w_re * glb_re + w_im * glb_im

    bb = jax.ShapeDtypeStruct(b_re_t.shape, _F32)
    v = jax.ShapeDtypeStruct(a_re.shape, _F32)
    return pl.pallas_call(body, name="ssm_prep_bwd", out_shape=[bb, bb, v, v, v])(
        log_dt, a_re, a_im, b_re_t, b_im_t, d_bbr, d_bbi, d_lr, d_li)


def _group_sum(d_dt, log_dt):
    def body(d_ref, l_ref, o_ref):
        o_ref[...] = jnp.sum(d_ref[...], axis=1, keepdims=True) * jnp.exp(l_ref[...])

    return pl.pallas_call(body, name="ssm_dt_grad", out_shape=jax.ShapeDtypeStruct(log_dt.shape, _F32))(d_dt, log_dt)


def _scan_rows(src_ref, dst_ref, nrt, steps, ptab, carry0, reverse):
    ns = BLOCK_ST
    row = lax.broadcasted_iota(jnp.int32, (SUBLANES, ns), 0)
    pr, pi = ptab

    def body(i, carry):
        cr, ci = carry
        it = (nrt - 1 - i) if reverse else i
        r0 = pl.multiple_of(it * SUBLANES, SUBLANES)
        xr = src_ref[pl.ds(r0, SUBLANES), 0:ns]
        xi = src_ref[pl.ds(r0, SUBLANES), ns:2 * ns]
        for k, (ar, ai) in zip((1, 2, 4), steps):
            if reverse:
                keep = row < SUBLANES - k
                sr = jnp.where(keep, pltpu.roll(xr, SUBLANES - k, 0), 0.0)
                si = jnp.where(keep, pltpu.roll(xi, SUBLANES - k, 0), 0.0)
            else:
                keep = row >= k
                sr = jnp.where(keep, pltpu.roll(xr, k, 0), 0.0)
                si = jnp.where(keep, pltpu.roll(xi, k, 0), 0.0)
            xr, xi = xr + ar * sr - ai * si, xi + ar * si + ai * sr
        xr, xi = xr + pr * cr - pi * ci, xi + pr * ci + pi * cr
        dst_ref[pl.ds(r0, SUBLANES), 0:ns] = xr
        dst_ref[pl.ds(r0, SUBLANES), ns:2 * ns] = xi
        if reverse:
            return xr[0:1, :], xi[0:1, :]
        return xr[SUBLANES - 1:SUBLANES, :], xi[SUBLANES - 1:SUBLANES, :]

    return lax.fori_loop(0, nrt, body, carry0)


def _scan_consts(p_ref, conj):
    ns = BLOCK_ST
    sign = -1.0 if conj else 1.0
    bc = lambda r: jnp.broadcast_to(r, (SUBLANES, ns))
    steps = [(bc(p_ref[k - 1:k, 0:ns]), bc(sign * p_ref[k - 1:k, ns:2 * ns])) for k in (1, 2, 4)]
    return steps


def _ssm_block_fwd(u, bbt_ref, ct_ref, d_ref, wg_ref, bg_ref, p_ref, bu_scr, h_scr, carry_in, nrt):
    ns = BLOCK_ST
    bu_scr[...] = jnp.dot(u.astype(_MXU), bbt_ref[...].astype(_MXU), preferred_element_type=_F32)
    ptab = (p_ref[:, 0:ns], p_ref[:, ns:2 * ns])
    carry = _scan_rows(bu_scr, h_scr, nrt, _scan_consts(p_ref, False), ptab, carry_in, False)
    y = jnp.dot(h_scr[...].astype(_MXU), ct_ref[...].astype(_MXU), preferred_element_type=_F32) + d_ref[...] * u
    yg = _gelu(y)
    gate = _sigmoid(jnp.dot(yg.astype(_MXU), wg_ref[...].astype(_MXU), preferred_element_type=_F32) + bg_ref[...])
    return y, yg, gate, carry


def _ssm_specs(nb, nt, t, reverse):
    tt = (lambda ti: nt - 1 - ti) if reverse else (lambda ti: ti)
    ns2 = 2 * BLOCK_ST
    return dict(
        z=pl.BlockSpec((t, BLOCK_CH), lambda b, ti: (tt(ti), b)),
        bbt=pl.BlockSpec((None, BLOCK_CH, ns2), lambda b, ti: (b, 0, 0)),
        ct=pl.BlockSpec((None, ns2, BLOCK_CH), lambda b, ti: (b, 0, 0)),
        vec=pl.BlockSpec((1, BLOCK_CH), lambda b, ti: (0, b)),
        wg=pl.BlockSpec((None, BLOCK_CH, BLOCK_CH), lambda b, ti: (b, 0, 0)),
        p=pl.BlockSpec((None, SUBLANES, ns2), lambda b, ti: (b, 0, 0)),
        hb=pl.BlockSpec((None, None, SUBLANES, ns2), lambda b, ti: (b, tt(ti), 0, 0)),
        h=pl.BlockSpec((None, t, ns2), lambda b, ti: (b, tt(ti), 0)),
        acc_vec=pl.BlockSpec((None, 1, ns2), lambda b, ti: (b, 0, 0)),
    )


def _ssm_fwd(z, bbt, ct, dvec, wg, bglu, ptab):
    s = z.shape[0]
    nb = bbt.shape[0]
    t = _tile(s, TIME_TILE, SUBLANES)
    nt = s // t
    ns = BLOCK_ST
    sp = _ssm_specs(nb, nt, t, False)

    def body(z_ref, bbt_ref, ct_ref, d_ref, wg_ref, bg_ref, p_ref, y2_ref, y_ref, h_ref, hb_ref, bu_scr, h_scr, carry_scr):
        ti = pl.program_id(1)

        @pl.when(ti == 0)
        def _():
            carry_scr[...] = jnp.zeros_like(carry_scr)

        hb_ref[...] = carry_scr[...]
        carry_in = (carry_scr[0:1, 0:ns], carry_scr[0:1, ns:2 * ns])
        y, yg, gate, (cr, ci) = _ssm_block_fwd(z_ref[...], bbt_ref, ct_ref, d_ref, wg_ref, bg_ref, p_ref,
                                               bu_scr, h_scr, carry_in, t // SUBLANES)
        y2_ref[...] = yg * gate
        y_ref[...] = y
        h_ref[...] = h_scr[...].astype(h_ref.dtype)
        carry_scr[:, 0:ns] = jnp.broadcast_to(cr, (SUBLANES, ns))
        carry_scr[:, ns:2 * ns] = jnp.broadcast_to(ci, (SUBLANES, ns))

    ych = jax.ShapeDtypeStruct((s, nb * BLOCK_CH), _F32)
    return pl.pallas_call(
        body, name="ssm_fwd", grid=(nb, nt),
        out_shape=[ych, ych, jax.ShapeDtypeStruct((nb, s, 2 * ns), _MXU),
                   jax.ShapeDtypeStruct((nb, nt, SUBLANES, 2 * ns), _F32)],
        in_specs=[sp["z"], sp["bbt"], sp["ct"], sp["vec"], sp["wg"], sp["vec"], sp["p"]],
        out_specs=[sp["z"], sp["z"], sp["h"], sp["hb"]],
        scratch_shapes=[pltpu.VMEM((t, 2 * ns), _F32), pltpu.VMEM((t, 2 * ns), _F32), pltpu.VMEM((SUBLANES, 2 * ns), _F32)],
        compiler_params=_cp("parallel", "arbitrary"),
    )(z, bbt, ct, dvec, wg, bglu, ptab)


def _ssm_bwd(z, y_pre, h_all, dy2, hb, bbt, ct, dvec, wg, bglu, ptab, ptab_rev):
    s = z.shape[0]
    nb = bbt.shape[0]
    t = _tile(s, TIME_TILE, SUBLANES)
    nt = s // t
    ns = BLOCK_ST
    sp = _ssm_specs(nb, nt, t, True)
    tn_dims = (((0,), (0,)), ((), ()))
    nt_dims = (((1,), (1,)), ((), ()))

    def body(z_ref, y_ref, h_ref, dy2_ref, hb_ref, bbt_ref, ct_ref, d_ref, wg_ref, bg_ref, p_ref, pr_ref,
             dz_ref, dbbt_ref, dct_ref, dwg_ref, dlb_ref, dd_ref, dbg_ref, bu_scr, g_scr, gcarry_scr):
        first = pl.program_id(1) == 0

        @pl.when(first)
        def _():
            gcarry_scr[...] = jnp.zeros_like(gcarry_scr)

        u = z_ref[...]
        hin = hb_ref[...]
        y = y_ref[...]
        yg = _gelu(y)
        gate = _sigmoid(jnp.dot(yg.astype(_MXU), wg_ref[...].astype(_MXU), preferred_element_type=_F32) + bg_ref[...])
        dy2 = dy2_ref[...]
        dpre = dy2 * yg * gate * (1.0 - gate)
        _acc(dbg_ref, first, _colsum(dpre))
        dpx = dpre.astype(_MXU)
        _acc(dwg_ref, first, lax.dot_general(yg.astype(_MXU), dpx, tn_dims, preferred_element_type=_F32))
        dyg = dy2 * gate + lax.dot_general(dpx, wg_ref[...].astype(_MXU), nt_dims, preferred_element_type=_F32)
        dy = dyg * _gelu_grad(y)
        _acc(dd_ref, first, _colsum(dy * u))
        dyx = dy.astype(_MXU)
        hx = h_ref[...]
        h = hx.astype(_F32)
        _acc(dct_ref, first, lax.dot_general(hx, dyx, tn_dims, preferred_element_type=_F32))
        bu_scr[...] = lax.dot_general(dyx, ct_ref[...].astype(_MXU), nt_dims, preferred_element_type=_F32)
        gin = (gcarry_scr[0:1, 0:ns], gcarry_scr[0:1, ns:2 * ns])
        ptab = (pr_ref[:, 0:ns], pr_ref[:, ns:2 * ns])
        gr, gi = _scan_rows(bu_scr, g_scr, t // SUBLANES, _scan_consts(p_ref, True), ptab, gin, True)
        gcarry_scr[:, 0:ns] = jnp.broadcast_to(gr, (SUBLANES, ns))
        gcarry_scr[:, ns:2 * ns] = jnp.broadcast_to(gi, (SUBLANES, ns))
        g = g_scr[...]
        row = lax.broadcasted_iota(jnp.int32, (t, ns), 0)
        hp_re = jnp.where(row == 0, hin[0:1, 0:ns], pltpu.roll(h[:, 0:ns], 1, 0))
        hp_im = jnp.where(row == 0, hin[0:1, ns:2 * ns], pltpu.roll(h[:, ns:2 * ns], 1, 0))
        g_re, g_im = g[:, 0:ns], g[:, ns:2 * ns]
        d_ar = _colsum(g_re * hp_re + g_im * hp_im)
        d_ai = _colsum(g_im * hp_re - g_re * hp_im)
        _acc(dlb_ref, first, jnp.concatenate([d_ar, d_ai], axis=1))
        gx = g.astype(_MXU)
        _acc(dbbt_ref, first, lax.dot_general(u.astype(_MXU), gx, tn_dims, preferred_element_type=_F32))
        dz_ref[...] = (dy * d_ref[...] + lax.dot_general(gx, bbt_ref[...].astype(_MXU), nt_dims,
                                                         preferred_element_type=_F32)).astype(dz_ref.dtype)

    f = lambda shape: jax.ShapeDtypeStruct(shape, _F32)
    return pl.pallas_call(
        body, name="ssm_bwd", grid=(nb, nt),
        out_shape=[jax.ShapeDtypeStruct((s, nb * BLOCK_CH), _MXU), f(bbt.shape), f(ct.shape), f(wg.shape), f((nb, 1, 2 * ns)),
                   f((1, nb * BLOCK_CH)), f((1, nb * BLOCK_CH))],
        in_specs=[sp["z"], sp["z"], sp["h"], sp["z"], sp["hb"], sp["bbt"], sp["ct"], sp["vec"], sp["wg"], sp["vec"], sp["p"],
                  sp["p"]],
        out_specs=[sp["z"], sp["bbt"], sp["ct"], sp["wg"], sp["acc_vec"], sp["vec"], sp["vec"]],
        scratch_shapes=[pltpu.VMEM((t, 2 * ns), _F32), pltpu.VMEM((t, 2 * ns), _F32), pltpu.VMEM((SUBLANES, 2 * ns), _F32)],
        compiler_params=_cp("parallel", "arbitrary"),
    )(z, y_pre, h_all, dy2, hb, bbt, ct, dvec, wg, bglu, ptab, ptab_rev)


def _mod_part(c_all, w, b):
    d, ns = w.shape
    tn = _tile(ns, 512)

    def body(c_ref, w_ref, b_ref, o_ref):
        c = c_ref[...]
        ca = (c * _sigmoid(c)).astype(_MXU)
        o_ref[...] = jnp.dot(ca, w_ref[...].astype(_MXU), preferred_element_type=_F32) + b_ref[...]

    return pl.pallas_call(
        body, name="mod_part", grid=(ns // tn,), out_shape=jax.ShapeDtypeStruct((8, ns), _F32),
        in_specs=[pl.BlockSpec((8, d), lambda n: (0, 0)), pl.BlockSpec((d, tn), lambda n: (0, n)),
                  pl.BlockSpec((1, tn), lambda n: (0, n))],
        out_specs=pl.BlockSpec((8, tn), lambda n: (0, n)), compiler_params=_cp("parallel"),
    )(c_all, w, b)


def _adamw_math(w, g, m, v):
    m = ADAM_B1 * m + (1.0 - ADAM_B1) * g
    v = ADAM_B2 * v + (1.0 - ADAM_B2) * (g * g)
    m_hat = m / (1.0 - ADAM_B1 ** ADAM_STEP)
    v_hat = v / (1.0 - ADAM_B2 ** ADAM_STEP)
    delta = -ADAM_LR * (m_hat / (jnp.sqrt(v_hat) + ADAM_EPS) + ADAM_WD * w)
    return delta, m, v


def _adamw(w, g, m, v, name):
    r, c = w.shape
    tc = c if c <= 4096 else _tile(c, 4096)
    tr = _tile(r, max(SUBLANES, (1 << 18) // tc), SUBLANES)

    def body(w_ref, g_ref, m_ref, v_ref, go_ref, d_ref, mo_ref, vo_ref):
        g = g_ref[...]
        go_ref[...] = g
        d_ref[...], mo_ref[...], vo_ref[...] = _adamw_math(w_ref[...], g, m_ref[...], v_ref[...])

    spec = pl.BlockSpec((tr, tc), lambda i, j: (i, j))
    out = jax.ShapeDtypeStruct((r, c), _F32)
    return pl.pallas_call(
        body, name=name, grid=(r // tr, c // tc), in_specs=[spec] * 4, out_specs=[spec] * 4, out_shape=[out] * 4,
        compiler_params=_cp("parallel", "parallel"),
    )(w, g, m, v)


def _adamw_halves(w, g2, m, v, name):
    r, c = w.shape
    tr, tc = _tile(r, 256, SUBLANES), _tile(c // 2, 1024)
    nph = (c // 2) // tc

    def body(w_ref, g_ref, m_ref, v_ref, go_ref, d_ref, mo_ref, vo_ref):
        g = g_ref[...]
        go_ref[...] = g
        d_ref[...], mo_ref[...], vo_ref[...] = _adamw_math(w_ref[...], g, m_ref[...], v_ref[...])

    spec = pl.BlockSpec((tr, tc), lambda i, j: (i, j))
    out = jax.ShapeDtypeStruct((r, c), _F32)
    return pl.pallas_call(
        body, name=name, grid=(r // tr, c // tc),
        in_specs=[spec, pl.BlockSpec((None, tr, tc), lambda i, j: (j // nph, i, j % nph)), spec, spec],
        out_specs=[spec] * 4, out_shape=[out] * 4, compiler_params=_cp("parallel", "parallel"),
    )(w, g2, m, v)


def _wada_update(c_t, dm, w, m, v):
    d, ns = w.shape
    tr, tc = _tile(d, 256, SUBLANES), _tile(ns, 1024)

    def body(c_ref, dm_ref, w_ref, m_ref, v_ref, g_ref, d_ref, mo_ref, vo_ref):
        c = c_ref[...]
        ca = c * _sigmoid(c)
        dmv = dm_ref[...]
        g = ca[:, 0:1] * dmv[0:1, :]
        for b in range(1, 8):
            g = g + ca[:, b:b + 1] * dmv[b:b + 1, :]
        g_ref[...] = g
        d_ref[...], mo_ref[...], vo_ref[...] = _adamw_math(w_ref[...], g, m_ref[...], v_ref[...])

    spec = pl.BlockSpec((tr, tc), lambda i, j: (i, j))
    out = jax.ShapeDtypeStruct((d, ns), _F32)
    return pl.pallas_call(
        body, name="wada_update", grid=(d // tr, ns // tc),
        in_specs=[pl.BlockSpec((tr, 8), lambda i, j: (i, 0)), pl.BlockSpec((8, tc), lambda i, j: (0, j)), spec, spec, spec],
        out_specs=[spec] * 4, out_shape=[out] * 4, compiler_params=_cp("parallel", "parallel"),
    )(c_t, dm, w, m, v)


def _small_reduce(gathered):
    _, r, c = gathered.shape
    tr = _tile(r, 512, SUBLANES)

    def body(q_ref, g_ref):
        g = q_ref[0]
        for k in range(1, 8):
            g = g + q_ref[k]
        g_ref[...] = g

    return pl.pallas_call(
        body, name="small_reduce", grid=(r // tr,), out_shape=jax.ShapeDtypeStruct((r, c), _F32),
        in_specs=[pl.BlockSpec((8, tr, c), lambda i: (0, i, 0))], out_specs=pl.BlockSpec((tr, c), lambda i: (i, 0)),
        compiler_params=_cp("parallel"),
    )(gathered)


def _adamw_many(ws, gs, ms, vs, steps, name):
    n = len(ws)

    def body(*refs):
        w_refs, g_refs, m_refs, v_refs = refs[0:n], refs[n:2 * n], refs[2 * n:3 * n], refs[3 * n:4 * n]
        d_refs, mo_refs, vo_refs = refs[4 * n:5 * n], refs[5 * n:6 * n], refs[6 * n:7 * n]
        for i in range(n):
            d_refs[i][...], mo_refs[i][...], vo_refs[i][...] = _adamw_math(
                w_refs[i][...], g_refs[i][...], m_refs[i][...], v_refs[i][...])

    def spec(a):
        nd = a.ndim
        if steps == 1:
            return pl.BlockSpec(a.shape, lambda i: (0,) * nd)
        return pl.BlockSpec((a.shape[0] // steps,) + a.shape[1:], lambda i: (i,) + (0,) * (nd - 1))

    specs = [spec(w) for w in ws]
    outs = pl.pallas_call(
        body, name=name, grid=(steps,), in_specs=specs * 4, out_specs=specs * 3,
        out_shape=[jax.ShapeDtypeStruct(w.shape, _F32) for w in ws] * 3, compiler_params=_cp("parallel"),
    )(*ws, *gs, *ms, *vs)
    return outs[0:n], outs[n:2 * n], outs[2 * n:3 * n]


def _block_diag(x):
    nb, g, p, q = x.shape
    eye = jnp.eye(g, dtype=x.dtype)
    return (x[:, :, :, None, :] * eye[None, :, None, :, None]).reshape(nb, g * p, g * q)


def _block_diag_take(x, p, q):
    nb = x.shape[0]
    g = GROUPS_PER_BLOCK
    eye = jnp.eye(g, dtype=x.dtype)
    return jnp.sum(x.reshape(nb, g, p, g, q) * eye[None, :, None, :, None], axis=3)


_VIEWS = {"ssm_b_re": ((0, 2, 1), (0, 2, 1)), "ssm_b_im": ((0, 2, 1), (0, 2, 1)),
          "ssm_w_glu": ((1, 2, 0), (2, 0, 1)), "ssm_b_glu": ((1, 0), (1, 0))}


def _to_view(name, a):
    return a.transpose(_VIEWS[name][0]) if name in _VIEWS else a


def _from_view(name, a):
    return a.transpose(_VIEWS[name][1]) if name in _VIEWS else a


class _Pack:
    def __init__(self, shapes):
        self.shapes = shapes
        self.offsets = {}
        off = 0
        for name, shape in shapes.items():
            n = math.prod(shape)
            self.offsets[name] = (off, n)
            off += -(-n // (SUBLANES * LANES)) * (SUBLANES * LANES)
        self.rows = -(-off // (256 * LANES)) * 256

    def pack(self, arrays):
        parts = []
        off = 0
        for name, shape in self.shapes.items():
            start, n = self.offsets[name]
            if start > off:
                parts.append(jnp.zeros((start - off,), _F32))
            parts.append(arrays[name].reshape(-1).astype(_F32))
            off = start + n
        total = self.rows * LANES
        if total > off:
            parts.append(jnp.zeros((total - off,), _F32))
        return jnp.concatenate(parts).reshape(self.rows, LANES)

    def unpack(self, buf):
        flat = buf.reshape(-1)
        return {name: flat[start:start + n].reshape(self.shapes[name]) for name, (start, n) in self.offsets.items()}


_SMALL = ["b_ada", "g_pre_mix", "g_post_mix", "ssm_log_dt", "ssm_a_re", "ssm_a_im", "ssm_b_re", "ssm_b_im", "ssm_c_re",
          "ssm_c_im", "ssm_d", "ssm_w_glu", "ssm_b_glu", "sgu_ln_g", "sgu_ln_b", "sgu_w", "sgu_b", "g_out_ssm",
          "g_out_sgu", "g_pre_ffn", "g_post_ffn", "conv_b"]
_WEIGHTS = ["w_ada", "b_ada", "g_pre_mix", "g_post_mix", "w_in", "ssm_log_dt", "ssm_a_re", "ssm_a_im", "ssm_b_re",
            "ssm_b_im", "ssm_c_re", "ssm_c_im", "ssm_d", "ssm_w_glu", "ssm_b_glu", "sgu_ln_g", "sgu_ln_b", "sgu_w", "sgu_b",
            "g_out_ssm", "g_out_sgu", "w_out", "g_pre_ffn", "g_post_ffn", "w_up", "conv_w", "conv_b", "w_down"]


def _step(p, m, v, x, c, tgt):
    s, d = x.shape
    mx, my, mc = lax.axis_index("x"), lax.axis_index("y"), lax.axis_index("c")
    chip = 2 * mx + my
    dev = 4 * mx + 2 * my + mc
    sel = jnp.stack([chip, mc]).astype(jnp.int32)
    g_cnt, n_st = p["ssm_a_re"].shape
    nb = g_cnt // GROUPS_PER_BLOCK
    gn = g_cnt * n_st
    d_ssm = g_cnt * SSM_GROUP
    nh = p["sgu_w"].shape[0]
    assert nh * CHUNK == d_ssm and 2 * d_ssm == d and n_st == SSM_STATE

    shards = lambda g: g.reshape(4, g.shape[1] * g.shape[2], g.shape[3])
    buf_in = _cast_into_slot(p["w_in"], sel, sel, "cast_w_in")

    ns_ada = p["w_ada"].shape[1]
    nc_conv = p["conv_w"].shape[1]
    first = jnp.concatenate([jnp.broadcast_to(c, (8, d)), jnp.pad(p["conv_w"], ((0, 5), (0, 0)))], axis=1)
    first_all = _all_gather8(_own_slot(first, dev), "gather_c_conv")
    c_all = first_all[:, 0, :d]
    conv_w_full = jnp.concatenate([first_all[2 * j, 0:3, d:] for j in range(4)], axis=1)
    b_ada_mine = lax.dynamic_slice_in_dim(p["b_ada"], chip * ns_ada, ns_ada, axis=1)
    mod_all = _all_gather8(_own_slot(_mod_part(c_all, p["w_ada"], b_ada_mine), dev), "gather_mod")
    (sems_in,), (buf_in,), tok = _gather_start([buf_in], mod_all, "gather_start_in")
    buf_out, buf_up, buf_down = [_cast_into_slot(p[n], sel, tok, "cast_" + n) for n in ("w_out", "w_up", "w_down")]
    mod_rows = lax.dynamic_index_in_dim(mod_all, dev, axis=1, keepdims=False)
    mod = jnp.concatenate([mod_rows[0], mod_rows[2], mod_rows[4], mod_rows[6]]).reshape(N_MOD, 1, d)
    sh1, sc1, gt1, sh2, sc2, gt2 = [mod[i] for i in range(N_MOD)]

    ldt_l = jnp.repeat(p["ssm_log_dt"], n_st, axis=1)
    are_l, aim_l = p["ssm_a_re"].reshape(1, gn), p["ssm_a_im"].reshape(1, gn)
    bre_t, bim_t = p["ssm_b_re"].reshape(gn, SSM_GROUP).T, p["ssm_b_im"].reshape(gn, SSM_GROUP).T
    pw_re, pw_im, bb_re, bb_im = _ssm_prep(ldt_l, are_l, aim_l, bre_t, bim_t)
    blocks = lambda t: t.reshape(t.shape[0], nb, GROUPS_PER_BLOCK * n_st).transpose(1, 0, 2)
    ptab = jnp.concatenate([blocks(pw_re), blocks(pw_im)], axis=2)
    ptab_rev = jnp.concatenate([blocks(pw_re)[:, ::-1], -blocks(pw_im)[:, ::-1]], axis=2)
    bd = lambda t: t.reshape(SSM_GROUP, nb, GROUPS_PER_BLOCK, n_st).transpose(1, 2, 0, 3)
    bbt = jnp.concatenate([_block_diag(bd(bb_re)), _block_diag(bd(bb_im))], axis=2).astype(_MXU)
    cd = lambda t: t.reshape(nb, GROUPS_PER_BLOCK, SSM_GROUP, n_st).transpose(0, 1, 3, 2)
    ct = jnp.concatenate([_block_diag(cd(p["ssm_c_re"])), -_block_diag(cd(p["ssm_c_im"]))], axis=1).astype(_MXU)
    wg = _block_diag(p["ssm_w_glu"].reshape(nb, GROUPS_PER_BLOCK, SSM_GROUP, SSM_GROUP)).astype(_MXU)
    dvec = p["ssm_d"]
    bglu = p["ssm_b_glu"].reshape(1, d_ssm)
    mask = jnp.tril(jnp.ones((CHUNK, CHUNK), _F32))
    wm = (p["sgu_w"] * mask[None]).astype(_MXU)
    bs = p["sgu_b"].reshape(nh, CHUNK, 1)

    h1 = _fwd_pre_mix(x, p["g_pre_mix"], _after(sc1, tok), sh1)
    buf_in = _gather_wait(sems_in, buf_in, h1, "gather_wait_in")
    w_in4 = shards(_pair_forward([buf_in], "pair_forward_in")[0])
    (sems_out, sems_up, sems_down), (buf_out, buf_up, buf_down), tok = _gather_start(
        [buf_out, buf_up, buf_down], w_in4, "gather_start_rest")
    z = _mm_nn(h1, w_in4, _F32, "mm_in", after=tok)
    y_ssm, y_pre, h_all, hb = _ssm_fwd(z, bbt, ct, dvec, wg, bglu, ptab)
    y_sgu = _sgu_fwd(z, p["sgu_ln_g"], p["sgu_ln_b"], wm, bs)
    ycat = _mix_norm_fwd(y_ssm, y_sgu, p["g_out_ssm"], p["g_out_sgu"])
    buf_out = _gather_wait(sems_out, buf_out, ycat, "gather_wait_out")
    w_out_full = _pair_forward([buf_out], "pair_forward_out")[0].reshape(1, d, d)
    o = _mm_nn(ycat, w_out_full, _F32, "mm_out")
    x1, h2 = _fwd_mid(o, x, gt1, p["g_post_mix"], p["g_pre_ffn"], sc2, sh2)
    buf_up = _gather_wait(sems_up, buf_up, h2, "gather_wait_up")
    w_up4 = shards(_pair_forward([buf_up], "pair_forward_up")[0])
    up_pre = _mm_nn(h2, w_up4, _F32, "mm_up")
    act = _conv_act_fwd(up_pre, conv_w_full, p["conv_b"])
    buf_down = _gather_wait(sems_down, buf_down, act, "gather_wait_down")
    w_down_full = _pair_forward([buf_down], "pair_forward_down")[0].reshape(1, -1, d)
    f = _mm_nn(act, w_down_full, _F32, "mm_down", tk=5632)
    dx2, df, d_gt2, d_g_post_ffn, loss = _loss_and_post_ffn_bwd(f, x1, tgt, gt2, p["g_post_ffn"])

    def reduce_next(swap, n, after):
        sems, gw, land, _ = swap
        gw, got = _swap_wait(sems, gw, land, after, "swap_wait_" + n)
        return _scatter_start(_pair_sum(gw, got, sel, "pair_sum_" + n), "scatter_start_" + n)

    d_act = _mm_nt(df, w_down_full, _F32, "mm_d_act", tk=2048)
    swap_down = _swap_start(_mm_tn_rows(act, df, "mm_gw_down"), "swap_start_w_down")
    d_up_pre, d_cw0, d_cw1, d_cw2, d_conv_b = _conv_act_bwd(up_pre, d_act, conv_w_full, _after(p["conv_b"], swap_down[3]))
    red_down = reduce_next(swap_down, "w_down", d_conv_b)
    dh2 = _mm_nt(d_up_pre, w_up4, _F32, "mm_dh2", tk=2816, after=red_down[3])
    swap_up = _swap_start(_mm_tn_cols(h2, d_up_pre, "mm_gw_up"), "swap_start_w_up")
    dx1, d_o, d_sc2, d_sh2, d_g_pre_ffn, d_gt1, d_g_post_mix = _bwd_mid(
        dh2, x1, dx2, o, p["g_pre_ffn"], _after(sc2, swap_up[3]), gt1, p["g_post_mix"])
    red_up = reduce_next(swap_up, "w_up", d_g_post_mix)
    d_ycat = _mm_nt(d_o, w_out_full, _F32, "mm_d_ycat", tn=1024, tk=2048, after=red_up[3])
    swap_out = _swap_start(_mm_tn_rows(ycat, d_o, "mm_gw_out"), "swap_start_w_out")
    dy_ssm, dy_sgu, d_g_out_ssm, d_g_out_sgu = _mix_norm_bwd(
        d_ycat, y_ssm, y_sgu, _after(p["g_out_ssm"], swap_out[3]), p["g_out_sgu"])
    red_out = reduce_next(swap_out, "w_out", d_g_out_sgu)
    dz_ssm, d_bbt, d_ct, d_wg, d_lb, d_ssm_d, d_bglu = _ssm_bwd(z, y_pre, h_all, dy_ssm, hb, bbt, ct,
                                                                _after(dvec, red_out[3]), wg, bglu, ptab, ptab_rev)
    dz, d_ln_g, d_ln_b, d_wm, d_bs = _sgu_bwd(z, dy_sgu, dz_ssm, p["sgu_ln_g"], p["sgu_ln_b"], wm, bs)
    dh1 = _mm_nt(dz, w_in4, _F32, "mm_dh1")
    swap_in = _swap_start(_mm_tn_cols(h1, dz, "mm_gw_in"), "swap_start_w_in")
    dx, d_sc1, d_sh1, d_g_pre_mix = _bwd_pre_mix(dh1, x, dx1, p["g_pre_mix"], _after(sc1, swap_in[3]))
    red_in = reduce_next(swap_in, "w_in", d_g_pre_mix)

    nsb = BLOCK_ST
    lanes = lambda t: t.transpose(2, 0, 1, 3).reshape(SSM_GROUP, gn)
    d_bbr = lanes(_block_diag_take(d_bbt[:, :, :nsb], SSM_GROUP, n_st))
    d_bbi = lanes(_block_diag_take(d_bbt[:, :, nsb:], SSM_GROUP, n_st))
    d_lr, d_li = d_lb[:, 0, :nsb].reshape(1, gn), d_lb[:, 0, nsb:].reshape(1, gn)
    d_bre_t, d_bim_t, d_are, d_aim, d_dt = _ssm_prep_bwd(ldt_l, are_l, aim_l, bre_t, bim_t, d_bbr, d_bbi, d_lr, d_li)
    d_log_dt = _group_sum(d_dt.reshape(g_cnt, n_st), p["ssm_log_dt"].reshape(g_cnt, 1))
    c_grad = lambda t: _block_diag_take(t, n_st, SSM_GROUP).transpose(0, 1, 3, 2).reshape(g_cnt, SSM_GROUP, n_st)
    small = {
        "b_ada": jnp.concatenate([d_sh1, _after(d_sc1, red_in[3]), d_gt1, d_sh2, d_sc2, d_gt2], axis=1),
        "g_pre_mix": d_g_pre_mix, "g_post_mix": d_g_post_mix,
        "ssm_log_dt": d_log_dt, "ssm_a_re": d_are, "ssm_a_im": d_aim,
        "ssm_b_re": d_bre_t.T, "ssm_b_im": d_bim_t.T,
        "ssm_c_re": c_grad(d_ct[:, :nsb, :]), "ssm_c_im": -c_grad(d_ct[:, nsb:, :]),
        "ssm_d": d_ssm_d, "ssm_w_glu": _block_diag_take(d_wg, SSM_GROUP, SSM_GROUP), "ssm_b_glu": d_bglu,
        "sgu_ln_g": d_ln_g, "sgu_ln_b": d_ln_b, "sgu_w": d_wm * mask[None], "sgu_b": d_bs,
        "g_out_ssm": d_g_out_ssm, "g_out_sgu": d_g_out_sgu, "g_pre_ffn": d_g_pre_ffn, "g_post_ffn": d_g_post_ffn,
        "conv_b": d_conv_b, "conv_w_all": jnp.concatenate([d_cw0, d_cw1, d_cw2], axis=0),
        "loss_sum": loss,
    }
    small = {n: _to_view(n, a.reshape(p[n].shape)) if n in p else a for n, a in small.items()}
    pk = _Pack({n: a.shape for n, a in small.items()})
    sems_small, small_buf, tok = _gather8_start(_own_slot(pk.pack(small), dev), "gather_small_start")

    big = ["w_down", "w_up", "w_out", "w_in"]
    joins = []
    after = tok
    for n, (sems, pair, land, _) in zip(big, (red_down, red_up, red_out, red_in)):
        pair, land = _scatter_wait(sems, pair, land, after, "scatter_wait_" + n)
        sems_j, half, after = _join_start(_chip_sum(pair, land, sel, "chip_sum_" + n), "join_start_" + n)
        joins.append((sems_j, half))
    big_out = {}
    for n, (sems_j, half) in zip(big, joins):
        j = _join_wait(sems_j, half, after, "join_wait_" + n)
        if n in ("w_in", "w_up"):
            big_out[n] = tuple(_adamw(p[n], j.reshape(p[n].shape), m[n], v[n], "adamw_" + n))
        else:
            big_out[n] = tuple(_adamw_halves(p[n], j, m[n], v[n], "adamw_" + n))
        after = big_out[n][1]

    gathered = _gather8_forward(_gather8_wait(sems_small, small_buf, after, "gather_small_wait"),
                                "gather_small_forward")
    gview = pk.unpack(_small_reduce(gathered))
    gview["conv_w"] = lax.dynamic_slice_in_dim(gview.pop("conv_w_all"), chip * nc_conv, nc_conv, axis=1)
    loss = gview.pop("loss_sum")
    small_names = _SMALL + ["conv_w"]
    per_group = [n for n in small_names if gview[n].ndim >= 2 and gview[n].shape[0] == g_cnt]
    others = [n for n in small_names if n not in per_group]
    grads = {n: _from_view(n, gview[n]) for n in small_names}
    deltas, new_m, new_v = {}, {}, {}
    for names, steps, call in ((per_group, g_cnt // GROUPS_PER_BLOCK, "adamw_s5"), (others, 1, "adamw_small")):
        res = _adamw_many([_to_view(n, p[n]) for n in names], [gview[n] for n in names],
                          [_to_view(n, m[n]) for n in names], [_to_view(n, v[n]) for n in names], steps, call)
        for n, dl, mo, vo in zip(names, *res):
            deltas[n], new_m[n], new_v[n] = _from_view(n, dl), _from_view(n, mo), _from_view(n, vo)

    d_mod_all = gathered.reshape(8, -1)[:, :N_MOD * d]
    d_mod_mine = lax.dynamic_slice_in_dim(d_mod_all, chip * ns_ada, ns_ada, axis=1)
    grads["w_ada"], deltas["w_ada"], new_m["w_ada"], new_v["w_ada"] = _wada_update(
        c_all.T, d_mod_mine, p["w_ada"], m["w_ada"], v["w_ada"])
    for n in big:
        grads[n], deltas[n], new_m[n], new_v[n] = big_out[n]
    return loss[0, 0], dx, grads, deltas, new_m, new_v


def kernel(x, c, w_ada, b_ada, g_pre_mix, g_post_mix, w_in, ssm_log_dt, ssm_a_re, ssm_a_im, ssm_b_re, ssm_b_im, ssm_c_re, ssm_c_im, ssm_d, ssm_w_glu, ssm_b_glu, sgu_ln_g, sgu_ln_b, sgu_w, sgu_b, g_out_ssm, g_out_sgu, w_out, g_pre_ffn, g_post_ffn, w_up, conv_w, conv_b, w_down, loss_target, m_w_ada, m_b_ada, m_g_pre_mix, m_g_post_mix, m_w_in, m_ssm_log_dt, m_ssm_a_re, m_ssm_a_im, m_ssm_b_re, m_ssm_b_im, m_ssm_c_re, m_ssm_c_im, m_ssm_d, m_ssm_w_glu, m_ssm_b_glu, m_sgu_ln_g, m_sgu_ln_b, m_sgu_w, m_sgu_b, m_g_out_ssm, m_g_out_sgu, m_w_out, m_g_pre_ffn, m_g_post_ffn, m_w_up, m_conv_w, m_conv_b, m_w_down, v_w_ada, v_b_ada, v_g_pre_mix, v_g_post_mix, v_w_in, v_ssm_log_dt, v_ssm_a_re, v_ssm_a_im, v_ssm_b_re, v_ssm_b_im, v_ssm_c_re, v_ssm_c_im, v_ssm_d, v_ssm_w_glu, v_ssm_b_glu, v_sgu_ln_g, v_sgu_ln_b, v_sgu_w, v_sgu_b, v_g_out_ssm, v_g_out_sgu, v_w_out, v_g_pre_ffn, v_g_post_ffn, v_w_up, v_conv_w, v_conv_b, v_w_down):
    given = dict(locals())
    drop = lambda a: a if a.ndim == 2 else a[0]
    p = {n: drop(given[n]) for n in _WEIGHTS}
    m = {n: drop(given["m_" + n]) for n in _WEIGHTS}
    v = {n: drop(given["v_" + n]) for n in _WEIGHTS}
    loss, dx, grads, deltas, new_m, new_v = _step(p, m, v, x[0], c, loss_target[0])
    outs = [loss, dx[None]]
    for group in (grads, deltas, new_m, new_v):
        outs += [group[n].reshape(given[n].shape) for n in _WEIGHTS]
    return tuple(outs)
```

```python
import functools
import math

import jax
import jax.numpy as jnp
from jax import lax
from jax.experimental import pallas as pl
from jax.experimental.pallas import tpu as pltpu

_F32 = jnp.float32
_MXU = jnp.bfloat16
_WIRE = jnp.bfloat16

EPS = 1e-6
SSM_GROUP = 16
SSM_STATE = 64
GROUPS_PER_BLOCK = 8
BLOCK_CH = SSM_GROUP * GROUPS_PER_BLOCK
BLOCK_ST = SSM_STATE * GROUPS_PER_BLOCK
CHUNK = 128
TIME_TILE = 512
SUBLANES = 8
LANES = 128
N_MOD = 6
ADAM_LR, ADAM_B1, ADAM_B2, ADAM_EPS, ADAM_WD, ADAM_STEP = 0.001, 0.9, 0.999, 1e-08, 0.01, 10
_VMEM_LIMIT = 56 * 1024 * 1024
_MESH = pl.DeviceIdType.MESH
_ANY = pl.BlockSpec(memory_space=pl.ANY)
_HBM = pl.BlockSpec(memory_space=pltpu.HBM)
_SEM = pl.BlockSpec(memory_space=pltpu.SEMAPHORE)
_VMEM_WHOLE = pl.BlockSpec(memory_space=pltpu.VMEM)
_EFFECT = pltpu.SideEffectType.DATAFLOW_SIDE_EFFECTING
_GELU_C = math.sqrt(2.0 / math.pi)


def _cp(*sem):
    return pltpu.CompilerParams(dimension_semantics=sem, vmem_limit_bytes=_VMEM_LIMIT)


def _tile(dim, target, align=LANES):
    if dim <= target:
        return dim
    best = None
    for t in range(align, target + 1, align):
        if dim % t == 0:
            best = t
    assert best is not None, (dim, target, align)
    return best


def _gelu(x):
    return 0.5 * x * (1.0 + jnp.tanh(_GELU_C * (x + 0.044715 * (x * x * x))))


def _gelu_grad(x):
    t = jnp.tanh(_GELU_C * (x + 0.044715 * (x * x * x)))
    return 0.5 * (1.0 + t) + 0.5 * x * (1.0 - t * t) * (_GELU_C * (1.0 + 3.0 * 0.044715 * x * x))


def _sigmoid(x):
    return 1.0 / (1.0 + jnp.exp(-x))


def _colsum(x):
    return jnp.sum(x, axis=0, keepdims=True)


def _rowmean(x):
    return jnp.mean(x, axis=-1, keepdims=True)


def _zero_first(first, *refs):
    @pl.when(first)
    def _():
        for ref in refs:
            ref[...] = jnp.zeros_like(ref)


def _acc(ref, first, val):
    del first
    ref[...] += val


def _place():
    mx, my, mc = lax.axis_index("x"), lax.axis_index("y"), lax.axis_index("c")
    chips = [(1 - mx, my), (mx, 1 - my), (1 - mx, 1 - my)]
    return mx, my, mc, chips


def _all_gather8(buf, name):
    def body(in_ref, out_ref, send_sems, recv_sems):
        mx, my, mc, chips = _place()
        me, sibling = (mx, my, mc), (mx, my, 1 - mc)

        def slot(ref, px, py, pc):
            return ref.at[4 * px + 2 * py + pc]

        def copy(k, block, to, src_ref=out_ref):
            return pltpu.make_async_remote_copy(
                src_ref=slot(src_ref, *block), dst_ref=slot(out_ref, *block),
                send_sem=send_sems.at[k], recv_sem=recv_sems.at[k], device_id=to, device_id_type=_MESH)

        first = [copy(0, me, sibling, in_ref)]
        first += [copy(1 + j, me, (*chip, mc), in_ref) for j, chip in enumerate(chips)]
        for cp in first:
            cp.start()
        passed = [copy(4 + j, (*chip, mc), sibling) for j, chip in enumerate(chips)]
        for j, chip in enumerate(chips):
            copy(1 + j, (*chip, mc), me).wait_recv()
            passed[j].start()
        copy(0, sibling, me).wait_recv()
        for j, chip in enumerate(chips):
            copy(4 + j, (*chip, 1 - mc), me).wait_recv()
        for cp in first + passed:
            cp.wait_send()

    return pl.pallas_call(
        body, name=name, out_shape=jax.ShapeDtypeStruct(buf.shape, buf.dtype),
        in_specs=[_ANY], out_specs=_ANY, input_output_aliases={0: 0},
        scratch_shapes=[pltpu.SemaphoreType.DMA((7,)), pltpu.SemaphoreType.DMA((7,))],
    )(buf)


def _own_slot(x, dev):
    return lax.dynamic_update_slice(jnp.zeros((8,) + x.shape, x.dtype), x[None], (dev, 0, 0))


def _cast_into_slot(w, sel, after, name):
    r, c = w.shape
    hr = r // 2
    tr = _tile(hr, 256, 16)
    nr = hr // tr

    def body(sel_ref, w_ref, after_ref, o_ref):
        o_ref[...] = w_ref[...].astype(o_ref.dtype)

    return pl.pallas_call(
        body, name=name, out_shape=jax.ShapeDtypeStruct((4, 2, hr, c), _WIRE),
        grid_spec=pltpu.PrefetchScalarGridSpec(
            num_scalar_prefetch=1, grid=(2, nr),
            in_specs=[pl.BlockSpec((tr, c), lambda h, i, s: (h * nr + i, 0)), _ANY],
            out_specs=pl.BlockSpec((None, None, tr, c), lambda h, i, s: (s[0], h, i, 0))),
        compiler_params=_cp("parallel", "parallel"),
    )(sel, w, after)


def _hbm(a):
    return pltpu.with_memory_space_constraint(a, pltpu.HBM)


def _after(vec, token):
    return vec + token[0:1, 0:1]


def _gather_start(bufs, after, name):
    n = len(bufs)
    nc = 3 * n

    def body(*refs):
        ins, send, recv, token = refs[:n], refs[n + 1:n + 1 + nc], refs[n + 1 + nc:n + 1 + 2 * nc], refs[-1]
        mx, my, mc, chips = _place()
        j_me = 2 * mx + my
        for i in range(n):
            for k, chip in enumerate(chips):
                half = ins[i].at[j_me, mc]
                pltpu.make_async_remote_copy(
                    src_ref=half, dst_ref=half, send_sem=send[3 * i + k], recv_sem=recv[3 * i + k],
                    device_id=(*chip, mc), device_id_type=_MESH).start()
        token[...] = jnp.zeros_like(token)

    outs = pl.pallas_call(
        body, name=name,
        out_shape=tuple([pltpu.SemaphoreType.DMA(())] * (2 * nc) + [pltpu.HBM(b.shape, b.dtype) for b in bufs]
                        + [jax.ShapeDtypeStruct((SUBLANES, LANES), _F32)]),
        in_specs=tuple([_HBM] * n + [_ANY]), out_specs=tuple([_SEM] * (2 * nc) + [_HBM] * n + [_VMEM_WHOLE]),
        input_output_aliases={i: 2 * nc + i for i in range(n)},
        compiler_params=pltpu.CompilerParams(has_side_effects=_EFFECT),
    )(*[_hbm(b) for b in bufs], after)
    sems = [(outs[3 * i:3 * i + 3], outs[nc + 3 * i:nc + 3 * i + 3]) for i in range(n)]
    return sems, list(outs[2 * nc:2 * nc + n]), outs[-1]


def _gather_wait(sems, buf, after, name):
    send, recv = sems

    def body(buf_ref, s0, s1, s2, r0, r1, r2, after_ref, out_ref):
        mx, my, mc, chips = _place()
        j_me = 2 * mx + my
        for k, (chip, s_k, r_k) in enumerate(zip(chips, (s0, s1, s2), (r0, r1, r2))):
            cp = pltpu.make_async_remote_copy(
                src_ref=buf_ref.at[j_me, mc], dst_ref=buf_ref.at[2 * chip[0] + chip[1], mc], send_sem=s_k, recv_sem=r_k,
                device_id=(*chip, mc), device_id_type=_MESH)
            cp.wait_send()
            cp.wait_recv()

    return pl.pallas_call(
        body, name=name, out_shape=pltpu.HBM(buf.shape, buf.dtype),
        in_specs=(_HBM,) + (_SEM,) * 6 + (_ANY,), out_specs=_HBM, input_output_aliases={0: 0},
        compiler_params=pltpu.CompilerParams(has_side_effects=_EFFECT),
    )(buf, *send, *recv, after)


def _pair_forward(bufs, name):
    n = len(bufs)

    def body(*refs):
        ins, outs = refs[:n], refs[n:2 * n]
        send_sems, recv_sems = refs[2 * n:]
        mx, my, mc, chips = _place()
        sibling = (mx, my, 1 - mc)
        cps = []
        for i in range(n):
            for k, chip in enumerate(chips):
                j_k = 2 * chip[0] + chip[1]
                cp = pltpu.make_async_remote_copy(
                    src_ref=ins[i].at[j_k, mc], dst_ref=outs[i].at[j_k, mc], send_sem=send_sems.at[3 * i + k],
                    recv_sem=recv_sems.at[3 * i + k], device_id=sibling, device_id_type=_MESH)
                cp.start()
                cps.append(cp)
        for i in range(n):
            for k, chip in enumerate(chips):
                other = outs[i].at[2 * chip[0] + chip[1], 1 - mc]
                pltpu.make_async_remote_copy(
                    src_ref=other, dst_ref=other, send_sem=send_sems.at[3 * i + k], recv_sem=recv_sems.at[3 * i + k],
                    device_id=sibling, device_id_type=_MESH).wait_recv()
        for cp in cps:
            cp.wait_send()

    return pl.pallas_call(
        body, name=name, out_shape=[jax.ShapeDtypeStruct(b.shape, b.dtype) for b in bufs],
        in_specs=[_ANY] * n, out_specs=[_ANY] * n, input_output_aliases={i: i for i in range(n)},
        scratch_shapes=[pltpu.SemaphoreType.DMA((3 * n,)), pltpu.SemaphoreType.DMA((3 * n,))],
    )(*bufs)


def _gather8_peers(buf_ref, mx, my, mc, chips):
    mine = buf_ref.at[4 * mx + 2 * my + mc]
    peers = [((mx, my, 1 - mc), mine, buf_ref.at[4 * mx + 2 * my + 1 - mc])]
    peers += [((*chip, mc), mine, buf_ref.at[4 * chip[0] + 2 * chip[1] + mc]) for chip in chips]
    return peers


def _gather8_start(buf, name):
    def body(buf_ref, *rest):
        send, recv, token = rest[0:4], rest[4:8], rest[-1]
        mx, my, mc, chips = _place()
        for k, (peer, src, _) in enumerate(_gather8_peers(buf_ref, mx, my, mc, chips)):
            pltpu.make_async_remote_copy(src_ref=src, dst_ref=src, send_sem=send[k], recv_sem=recv[k],
                                         device_id=peer, device_id_type=_MESH).start()
        token[...] = jnp.zeros_like(token)

    outs = pl.pallas_call(
        body, name=name,
        out_shape=tuple([pltpu.SemaphoreType.DMA(())] * 8 + [pltpu.HBM(buf.shape, buf.dtype),
                                                             jax.ShapeDtypeStruct((SUBLANES, LANES), _F32)]),
        in_specs=(_HBM,), out_specs=tuple([_SEM] * 8 + [_HBM, _VMEM_WHOLE]), input_output_aliases={0: 8},
        compiler_params=pltpu.CompilerParams(has_side_effects=_EFFECT),
    )(_hbm(buf))
    return (outs[0:4], outs[4:8]), outs[8], outs[9]


def _gather8_wait(sems, buf, after, name):
    send, recv = sems

    def body(buf_ref, s0, s1, s2, s3, r0, r1, r2, r3, after_ref, out_ref):
        mx, my, mc, chips = _place()
        for (peer, src, dst), s_k, r_k in zip(_gather8_peers(buf_ref, mx, my, mc, chips), (s0, s1, s2, s3), (r0, r1, r2, r3)):
            cp = pltpu.make_async_remote_copy(src_ref=src, dst_ref=dst, send_sem=s_k, recv_sem=r_k,
                                              device_id=peer, device_id_type=_MESH)
            cp.wait_send()
            cp.wait_recv()

    return pl.pallas_call(
        body, name=name, out_shape=pltpu.HBM(buf.shape, buf.dtype),
        in_specs=(_HBM,) + (_SEM,) * 8 + (_ANY,), out_specs=_HBM, input_output_aliases={0: 0},
        compiler_params=pltpu.CompilerParams(has_side_effects=_EFFECT),
    )(buf, *send, *recv, after)


def _gather8_forward(buf, name):
    def body(in_ref, out_ref, send_sems, recv_sems):
        mx, my, mc, chips = _place()
        sibling = (mx, my, 1 - mc)
        cps = []
        for k, chip in enumerate(chips):
            idx = 4 * chip[0] + 2 * chip[1] + mc
            cp = pltpu.make_async_remote_copy(src_ref=in_ref.at[idx], dst_ref=out_ref.at[idx], send_sem=send_sems.at[k],
                                              recv_sem=recv_sems.at[k], device_id=sibling, device_id_type=_MESH)
            cp.start()
            cps.append(cp)
        for k, chip in enumerate(chips):
            other = out_ref.at[4 * chip[0] + 2 * chip[1] + 1 - mc]
            pltpu.make_async_remote_copy(src_ref=other, dst_ref=other, send_sem=send_sems.at[k], recv_sem=recv_sems.at[k],
                                         device_id=sibling, device_id_type=_MESH).wait_recv()
        for cp in cps:
            cp.wait_send()

    return pl.pallas_call(
        body, name=name, out_shape=jax.ShapeDtypeStruct(buf.shape, buf.dtype),
        in_specs=[_ANY], out_specs=_ANY, input_output_aliases={0: 0},
        scratch_shapes=[pltpu.SemaphoreType.DMA((3,)), pltpu.SemaphoreType.DMA((3,))],
    )(buf)


def _scatter_start(pair, name):
    land = lax.empty((3,) + pair.shape[1:], pair.dtype)

    def body(pair_ref, land_ref, s0, s1, s2, r0, r1, r2, pair_thru, land_thru, token):
        mx, my, mc, chips = _place()
        for k, (chip, s_k, r_k) in enumerate(zip(chips, (s0, s1, s2), (r0, r1, r2))):
            pltpu.make_async_remote_copy(
                src_ref=pair_ref.at[2 * chip[0] + chip[1]], dst_ref=land_ref.at[k], send_sem=s_k, recv_sem=r_k,
                device_id=(*chip, mc), device_id_type=_MESH).start()
        token[...] = jnp.zeros_like(token)

    outs = pl.pallas_call(
        body, name=name,
        out_shape=tuple([pltpu.SemaphoreType.DMA(())] * 6 + [pltpu.HBM(pair.shape, pair.dtype), pltpu.HBM(land.shape, land.dtype),
                                                             jax.ShapeDtypeStruct((SUBLANES, LANES), _F32)]),
        in_specs=(_HBM, _HBM), out_specs=tuple([_SEM] * 6 + [_HBM, _HBM, _VMEM_WHOLE]),
        input_output_aliases={0: 6, 1: 7}, compiler_params=pltpu.CompilerParams(has_side_effects=_EFFECT),
    )(_hbm(pair), _hbm(land))
    return (outs[0:3], outs[3:6]), outs[6], outs[7], outs[8]


def _scatter_wait(sems, pair, land, after, name):
    send, recv = sems

    def body(pair_ref, land_ref, s0, s1, s2, r0, r1, r2, after_ref, pair_out, land_out):
        mx, my, mc, chips = _place()
        for k, (chip, s_k, r_k) in enumerate(zip(chips, (s0, s1, s2), (r0, r1, r2))):
            cp = pltpu.make_async_remote_copy(
                src_ref=pair_ref.at[2 * chip[0] + chip[1]], dst_ref=land_ref.at[k], send_sem=s_k, recv_sem=r_k,
                device_id=(*chip, mc), device_id_type=_MESH)
            cp.wait_send()
            cp.wait_recv()

    return pl.pallas_call(
        body, name=name, out_shape=(pltpu.HBM(pair.shape, pair.dtype), pltpu.HBM(land.shape, land.dtype)),
        in_specs=(_HBM, _HBM) + (_SEM,) * 6 + (_ANY,), out_specs=(_HBM, _HBM), input_output_aliases={0: 0, 1: 1},
        compiler_params=pltpu.CompilerParams(has_side_effects=_EFFECT),
    )(pair, land, *send, *recv, after)


def _sibling_copy(src_ref, dst_ref, send_sem, recv_sem):
    mx, my, mc, _ = _place()
    return pltpu.make_async_remote_copy(src_ref=src_ref, dst_ref=dst_ref, send_sem=send_sem, recv_sem=recv_sem,
                                        device_id=(mx, my, 1 - mc), device_id_type=_MESH)


def _swap_start(g, name):
    land = lax.empty(g.shape[1:], g.dtype)

    def body(g_ref, land_ref, send_sem, recv_sem, g_thru, land_thru, token):
        _sibling_copy(g_ref.at[1 - lax.axis_index("c")], land_ref, send_sem, recv_sem).start()
        token[...] = jnp.zeros_like(token)

    outs = pl.pallas_call(
        body, name=name,
        out_shape=(pltpu.SemaphoreType.DMA(()), pltpu.SemaphoreType.DMA(()), pltpu.HBM(g.shape, g.dtype),
                   pltpu.HBM(land.shape, land.dtype), jax.ShapeDtypeStruct((SUBLANES, LANES), _F32)),
        in_specs=(_HBM, _HBM), out_specs=(_SEM, _SEM, _HBM, _HBM, _VMEM_WHOLE), input_output_aliases={0: 2, 1: 3},
        compiler_params=pltpu.CompilerParams(has_side_effects=_EFFECT),
    )(_hbm(g), _hbm(land))
    return (outs[0], outs[1]), outs[2], outs[3], outs[4]


def _swap_wait(sems, g, land, after, name):
    def body(g_ref, land_ref, send_sem, recv_sem, after_ref, g_out, land_out):
        cp = _sibling_copy(g_ref.at[1 - lax.axis_index("c")], land_ref, send_sem, recv_sem)
        cp.wait_send()
        cp.wait_recv()

    return pl.pallas_call(
        body, name=name, out_shape=(pltpu.HBM(g.shape, g.dtype), pltpu.HBM(land.shape, land.dtype)),
        in_specs=(_HBM, _HBM, _SEM, _SEM, _ANY), out_specs=(_HBM, _HBM), input_output_aliases={0: 0, 1: 1},
        compiler_params=pltpu.CompilerParams(has_side_effects=_EFFECT),
    )(g, land, *sems, after)


def _join_start(buf, name):
    def body(buf_ref, send_sem, recv_sem, buf_thru, token):
        mine = buf_ref.at[lax.axis_index("c")]
        _sibling_copy(mine, mine, send_sem, recv_sem).start()
        token[...] = jnp.zeros_like(token)

    outs = pl.pallas_call(
        body, name=name,
        out_shape=(pltpu.SemaphoreType.DMA(()), pltpu.SemaphoreType.DMA(()), pltpu.HBM(buf.shape, buf.dtype),
                   jax.ShapeDtypeStruct((SUBLANES, LANES), _F32)),
        in_specs=(_HBM,), out_specs=(_SEM, _SEM, _HBM, _VMEM_WHOLE), input_output_aliases={0: 2},
        compiler_params=pltpu.CompilerParams(has_side_effects=_EFFECT),
    )(_hbm(buf))
    return (outs[0], outs[1]), outs[2], outs[3]


def _join_wait(sems, buf, after, name):
    def body(buf_ref, send_sem, recv_sem, after_ref, buf_out):
        mc = lax.axis_index("c")
        cp = _sibling_copy(buf_ref.at[mc], buf_ref.at[1 - mc], send_sem, recv_sem)
        cp.wait_send()
        cp.wait_recv()

    return pl.pallas_call(
        body, name=name, out_shape=pltpu.HBM(buf.shape, buf.dtype),
        in_specs=(_HBM, _SEM, _SEM, _ANY), out_specs=_HBM, input_output_aliases={0: 0},
        compiler_params=pltpu.CompilerParams(has_side_effects=_EFFECT),
    )(buf, *sems, after)


def _pair_sum(g, got, sel, name):
    _, four, hr, c = g.shape
    tr = _tile(hr, 512, 16)

    def body(sel_ref, g_ref, p_ref, o_ref):
        o_ref[...] = (g_ref[...].astype(_F32) + p_ref[...].astype(_F32)).astype(o_ref.dtype)

    return pl.pallas_call(
        body, name=name, out_shape=jax.ShapeDtypeStruct((four, hr, c), g.dtype),
        grid_spec=pltpu.PrefetchScalarGridSpec(
            num_scalar_prefetch=1, grid=(four, hr // tr),
            in_specs=[pl.BlockSpec((None, None, tr, c), lambda j, i, s: (s[1], j, i, 0)),
                      pl.BlockSpec((None, tr, c), lambda j, i, s: (j, i, 0))],
            out_specs=pl.BlockSpec((None, tr, c), lambda j, i, s: (j, i, 0))),
        compiler_params=_cp("parallel", "parallel"),
    )(sel, g, got)


def _chip_sum(pair, got, sel, name):
    _, hr, c = pair.shape
    tr = _tile(hr, 512, 16)

    def body(sel_ref, p_ref, q_ref, o_ref):
        o_ref[...] = ((p_ref[...].astype(_F32) + q_ref[0].astype(_F32)) + q_ref[1].astype(_F32)) + q_ref[2].astype(_F32)

    return pl.pallas_call(
        body, name=name, out_shape=jax.ShapeDtypeStruct((2, hr, c), _F32),
        grid_spec=pltpu.PrefetchScalarGridSpec(
            num_scalar_prefetch=1, grid=(hr // tr,),
            in_specs=[pl.BlockSpec((None, tr, c), lambda i, s: (s[0], i, 0)),
                      pl.BlockSpec((3, tr, c), lambda i, s: (0, i, 0))],
            out_specs=pl.BlockSpec((None, tr, c), lambda i, s: (s[1], i, 0))),
        compiler_params=_cp("parallel"),
    )(sel, pair, got)


def _matmul(a, b, dims, out_struct, grid, a_spec, b_spec, o_spec, acc_shape, k_axis, name, after=None):
    nk = grid[k_axis]
    extra = [] if after is None else [after]

    def body(a_ref, b_ref, *rest):
        o_ref, acc = rest[len(extra)], rest[len(extra) + 1:]
        prod = lax.dot_general(a_ref[...].astype(_MXU), b_ref[...].astype(_MXU), dims, preferred_element_type=_F32)
        if nk == 1:
            o_ref[...] = prod.astype(o_ref.dtype)
        else:
            acc_ref, = acc
            k = pl.program_id(k_axis)
            _zero_first(k == 0, acc_ref)
            acc_ref[...] += prod

            @pl.when(k == nk - 1)
            def _():
                o_ref[...] = acc_ref[...].astype(o_ref.dtype)

    sem = ["parallel"] * len(grid)
    sem[k_axis] = "arbitrary"
    return pl.pallas_call(
        body, name=name, out_shape=out_struct, grid=grid, in_specs=[a_spec, b_spec] + [_ANY] * len(extra), out_specs=o_spec,
        scratch_shapes=[pltpu.VMEM(acc_shape, _F32)] if nk > 1 else [], compiler_params=_cp(*sem),
    )(a, b, *extra)


def _mm_nn(a, w4, out_dtype, name, tm=512, tn=1536, tk=2048, after=None):
    m, k = a.shape
    j, _, ns = w4.shape
    tm, tn, tk = _tile(m, tm, 16), _tile(ns, tn), _tile(k, tk)
    nps = ns // tn
    return _matmul(
        a, w4, (((1,), (0,)), ((), ())), jax.ShapeDtypeStruct((m, j * ns), out_dtype),
        (m // tm, j * nps, k // tk),
        pl.BlockSpec((tm, tk), lambda mi, ni, ki: (mi, ki)),
        pl.BlockSpec((None, tk, tn), lambda mi, ni, ki: (ni // nps, ki, ni % nps)),
        pl.BlockSpec((tm, tn), lambda mi, ni, ki: (mi, ni)), (tm, tn), 2, name, after)


def _mm_nt(a, w4, out_dtype, name, tm=512, tn=2048, tk=1536, after=None):
    m = a.shape[-2]
    j, kw, ns = w4.shape
    tm, tn, tk = _tile(m, tm, 16), _tile(kw, tn), _tile(ns, tk)
    kps = ns // tk
    if a.ndim == 3:
        kph = a.shape[2] // tk
        a_spec = pl.BlockSpec((None, tm, tk), lambda mi, ni, ki: (ki // kph, mi, ki % kph))
    else:
        a_spec = pl.BlockSpec((tm, tk), lambda mi, ni, ki: (mi, ki))
    return _matmul(
        a, w4, (((1,), (1,)), ((), ())), jax.ShapeDtypeStruct((m, kw), out_dtype),
        (m // tm, kw // tn, j * kps),
        a_spec,
        pl.BlockSpec((None, tn, tk), lambda mi, ni, ki: (ki // kps, ni, ki % kps)),
        pl.BlockSpec((tm, tn), lambda mi, ni, ki: (mi, ni)), (tm, tn), 2, name, after)


def _mm_tn_cols(a, b, name, tm=1024, tn=1536, tk=2048):
    m, ka = a.shape
    ns = (b.shape[-1] * (2 if b.ndim == 3 else 1)) // 4
    hr = ka // 2
    tm, tn, tk = _tile(hr, tm), _tile(ns, tn), _tile(m, tk, 16)
    mph, nps = hr // tm, ns // tn
    if b.ndim == 3:
        b_spec = pl.BlockSpec((None, tk, tn), lambda ni, mi, ki: (ni // (2 * nps), ki, ni % (2 * nps)))
    else:
        b_spec = pl.BlockSpec((tk, tn), lambda ni, mi, ki: (ki, ni))
    return _matmul(
        a, b, (((0,), (0,)), ((), ())), jax.ShapeDtypeStruct((2, 4, hr, ns), _WIRE),
        (4 * nps, 2 * mph, m // tk),
        pl.BlockSpec((tk, tm), lambda ni, mi, ki: (ki, mi)),
        b_spec,
        pl.BlockSpec((None, None, tm, tn), lambda ni, mi, ki: (mi // mph, ni // nps, mi % mph, ni % nps)),
        (tm, tn), 2, name)


def _mm_tn_rows(a, b, name, tm=1536, tn=1024, tk=2048):
    m, ka = a.shape
    r = ka // 4
    hc = b.shape[1] // 2
    tm, tn, tk = _tile(r, tm), _tile(hc, tn), _tile(m, tk, 16)
    mpr, nph = r // tm, hc // tn
    return _matmul(
        a, b, (((0,), (0,)), ((), ())), jax.ShapeDtypeStruct((2, 4, r, hc), _WIRE),
        (2 * nph, 4 * mpr, m // tk),
        pl.BlockSpec((tk, tm), lambda ni, mi, ki: (ki, mi)),
        pl.BlockSpec((tk, tn), lambda ni, mi, ki: (ki, ni)),
        pl.BlockSpec((None, None, tm, tn), lambda ni, mi, ki: (ni // nph, mi // mpr, mi % mpr, ni % nph)),
        (tm, tn), 2, name)


def _row_call(body, name, rows, ins, outs, tm=256):
    tm = _tile(rows, tm, 16)

    def spec(shape, kind):
        if kind == "rows":
            return pl.BlockSpec((tm, shape[1]), lambda i: (i, 0))
        return pl.BlockSpec(shape, lambda i: (0,) * len(shape))

    return pl.pallas_call(
        body, name=name, grid=(rows // tm,),
        in_specs=[spec(a.shape, kind) for a, kind in ins],
        out_specs=[spec(o.shape, kind) for o, kind in outs],
        out_shape=[o for o, _ in outs],
        compiler_params=_cp("arbitrary"),
    )(*[a for a, _ in ins])


def _rms(x):
    r = lax.rsqrt(_rowmean(x * x) + EPS)
    return x * r, r


def _rms_bwd(dxh, xh, r):
    return r * (dxh - xh * _rowmean(dxh * xh))


def _fwd_pre_mix(x, g, sc, sh):
    s, d = x.shape

    def body(x_ref, g_ref, sc_ref, sh_ref, h_ref):
        xh, _ = _rms(x_ref[...])
        h_ref[...] = (xh * g_ref[...] * (1.0 + sc_ref[...]) + sh_ref[...]).astype(h_ref.dtype)

    return _row_call(body, "fwd_pre_mix", s, [(x, "rows"), (g, "vec"), (sc, "vec"), (sh, "vec")],
                     [(jax.ShapeDtypeStruct((s, d), _MXU), "rows")])[0]


def _fwd_mid(o, x, gt1, g_post, g_pre2, sc2, sh2):
    s, d = x.shape

    def body(o_ref, x_ref, gt_ref, gp_ref, g2_ref, sc_ref, sh_ref, x1_ref, h2_ref):
        oh, _ = _rms(o_ref[...])
        x1 = x_ref[...] + gt_ref[...] * (oh * gp_ref[...])
        x1_ref[...] = x1
        xh, _ = _rms(x1)
        h2_ref[...] = (xh * g2_ref[...] * (1.0 + sc_ref[...]) + sh_ref[...]).astype(h2_ref.dtype)

    return _row_call(body, "fwd_mid", s,
                     [(o, "rows"), (x, "rows"), (gt1, "vec"), (g_post, "vec"), (g_pre2, "vec"), (sc2, "vec"),
                      (sh2, "vec")],
                     [(jax.ShapeDtypeStruct((s, d), _F32), "rows"), (jax.ShapeDtypeStruct((s, d), _MXU), "rows")])


def _loss_and_post_ffn_bwd(f, x1, tgt, gt2, g_post):
    s, d = x1.shape

    def body(f_ref, x1_ref, t_ref, gt_ref, g_ref, dx2_ref, df_ref, dgt_ref, dg_ref, loss_ref):
        first = pl.program_id(0) == 0
        _zero_first(first, dgt_ref, dg_ref, loss_ref)
        fh, r = _rms(f_ref[...])
        n = fh * g_ref[...]
        e = x1_ref[...] + gt_ref[...] * n - t_ref[...]
        _acc(loss_ref, first, jnp.sum(_colsum(e * e), axis=1, keepdims=True) * (0.5 / d))
        dx2 = e * (1.0 / d)
        dx2_ref[...] = dx2
        _acc(dgt_ref, first, _colsum(dx2 * n))
        dn = dx2 * gt_ref[...]
        _acc(dg_ref, first, _colsum(dn * fh))
        df_ref[...] = _rms_bwd(dn * g_ref[...], fh, r).astype(df_ref.dtype)

    vec = jax.ShapeDtypeStruct((1, d), _F32)
    return _row_call(body, "loss_post_ffn_bwd", s,
                     [(f, "rows"), (x1, "rows"), (tgt, "rows"), (gt2, "vec"), (g_post, "vec")],
                     [(jax.ShapeDtypeStruct((s, d), _F32), "rows"), (jax.ShapeDtypeStruct((s, d), _MXU), "rows"),
                      (vec, "vec"), (vec, "vec"), (jax.ShapeDtypeStruct((1, 1), _F32), "vec")])


def _bwd_mid(dh2, x1, dx2, o, g_pre2, sc2, gt1, g_post):
    s, d = x1.shape

    def body(dh_ref, x1_ref, dx2_ref, o_ref, g2_ref, sc_ref, gt_ref, gp_ref,
             dx1_ref, do_ref, dsc_ref, dsh_ref, dg2_ref, dgt_ref, dgp_ref):
        first = pl.program_id(0) == 0
        _zero_first(first, dsc_ref, dsh_ref, dg2_ref, dgt_ref, dgp_ref)
        dh = dh_ref[...]
        xh, r = _rms(x1_ref[...])
        _acc(dsh_ref, first, _colsum(dh))
        _acc(dsc_ref, first, _colsum(dh * (xh * g2_ref[...])))
        dn = dh * (1.0 + sc_ref[...])
        _acc(dg2_ref, first, _colsum(dn * xh))
        dx1 = dx2_ref[...] + _rms_bwd(dn * g2_ref[...], xh, r)
        dx1_ref[...] = dx1
        oh, ro = _rms(o_ref[...])
        _acc(dgt_ref, first, _colsum(dx1 * (oh * gp_ref[...])))
        dno = dx1 * gt_ref[...]
        _acc(dgp_ref, first, _colsum(dno * oh))
        do_ref[...] = _rms_bwd(dno * gp_ref[...], oh, ro).astype(do_ref.dtype)

    vec = jax.ShapeDtypeStruct((1, d), _F32)
    return _row_call(body, "bwd_mid", s,
                     [(dh2, "rows"), (x1, "rows"), (dx2, "rows"), (o, "rows"), (g_pre2, "vec"), (sc2, "vec"),
                      (gt1, "vec"), (g_post, "vec")],
                     [(jax.ShapeDtypeStruct((s, d), _F32), "rows"), (jax.ShapeDtypeStruct((s, d), _MXU), "rows"),
                      (vec, "vec"), (vec, "vec"), (vec, "vec"), (vec, "vec"), (vec, "vec")])


def _bwd_pre_mix(dh1, x, dx1, g, sc1):
    s, d = x.shape

    def body(dh_ref, x_ref, dx1_ref, g_ref, sc_ref, dx_ref, dsc_ref, dsh_ref, dg_ref):
        first = pl.program_id(0) == 0
        _zero_first(first, dsc_ref, dsh_ref, dg_ref)
        dh = dh_ref[...]
        xh, r = _rms(x_ref[...])
        _acc(dsh_ref, first, _colsum(dh))
        _acc(dsc_ref, first, _colsum(dh * (xh * g_ref[...])))
        dn = dh * (1.0 + sc_ref[...])
        _acc(dg_ref, first, _colsum(dn * xh))
        dx_ref[...] = dx1_ref[...] + _rms_bwd(dn * g_ref[...], xh, r)

    vec = jax.ShapeDtypeStruct((1, d), _F32)
    return _row_call(body, "bwd_pre_mix", s,
                     [(dh1, "rows"), (x, "rows"), (dx1, "rows"), (g, "vec"), (sc1, "vec")],
                     [(jax.ShapeDtypeStruct((s, d), _F32), "rows"), (vec, "vec"), (vec, "vec"), (vec, "vec")])


def _mix_norm_fwd(y_ssm, y_sgu, g_ssm, g_sgu):
    s, h = y_ssm.shape

    def body(a_ref, b_ref, ga_ref, gb_ref, o_ref):
        ah, _ = _rms(a_ref[...])
        bh, _ = _rms(b_ref[...])
        o_ref[:, 0:h] = (ah * ga_ref[...]).astype(o_ref.dtype)
        o_ref[:, h:2 * h] = (bh * gb_ref[...]).astype(o_ref.dtype)

    return _row_call(body, "mix_norm_fwd", s, [(y_ssm, "rows"), (y_sgu, "rows"), (g_ssm, "vec"), (g_sgu, "vec")],
                     [(jax.ShapeDtypeStruct((s, 2 * h), _MXU), "rows")])[0]


def _mix_norm_bwd(dyc, y_ssm, y_sgu, g_ssm, g_sgu):
    s, h = y_ssm.shape

    def body(d_ref, a_ref, b_ref, ga_ref, gb_ref, da_ref, db_ref, dga_ref, dgb_ref):
        first = pl.program_id(0) == 0
        _zero_first(first, dga_ref, dgb_ref)
        for lo, y_ref, g_ref, dy_ref, dg_ref in ((0, a_ref, ga_ref, da_ref, dga_ref), (h, b_ref, gb_ref, db_ref, dgb_ref)):
            d = d_ref[:, lo:lo + h]
            yh, r = _rms(y_ref[...])
            _acc(dg_ref, first, _colsum(d * yh))
            dy_ref[...] = _rms_bwd(d * g_ref[...], yh, r)

    vec = jax.ShapeDtypeStruct((1, h), _F32)
    full = jax.ShapeDtypeStruct((s, h), _F32)
    return _row_call(body, "mix_norm_bwd", s,
                     [(dyc, "rows"), (y_ssm, "rows"), (y_sgu, "rows"), (g_ssm, "vec"), (g_sgu, "vec")],
                     [(full, "rows"), (full, "rows"), (vec, "vec"), (vec, "vec")])


CONV_ROWS = 64


def _conv_rows(ext, w_ref, b_ref):
    x = ext[SUBLANES:]
    s1 = pltpu.roll(ext, 1, 0)[SUBLANES:]
    s2 = pltpu.roll(ext, 2, 0)[SUBLANES:]
    return b_ref[...] + w_ref[0:1, :] * s2 + w_ref[1:2, :] * s1 + w_ref[2:3, :] * x, x, s1, s2


def _conv_window(x_ref, r0):
    if isinstance(r0, int):
        assert r0 == 0
        return jnp.concatenate([jnp.zeros((SUBLANES, x_ref.shape[1]), _F32), x_ref[0:CONV_ROWS, :]], axis=0)
    return x_ref[pl.ds(pl.multiple_of(r0 - SUBLANES, SUBLANES), CONV_ROWS + SUBLANES), :]


def _conv_act_fwd(up_pre, conv_w, conv_b):
    s, f2 = up_pre.shape
    f = f2 // 2
    tc = _tile(f, 256)
    nf = f // tc

    def shift_down(x, k):
        row = lax.broadcasted_iota(jnp.int32, x.shape, 0)
        return jnp.where(row >= k, pltpu.roll(x, k, 0), 0.0)

    def conv(x, w_ref, b_ref):
        return b_ref[...] + w_ref[0:1, :] * shift_down(x, 2) + w_ref[1:2, :] * shift_down(x, 1) + w_ref[2:3, :] * x

    def body(a_ref, b_ref, wa_ref, wb_ref, ba_ref, bb_ref, o_ref):
        a = conv(a_ref[...], wa_ref, ba_ref)
        b = conv(b_ref[...], wb_ref, bb_ref)
        o_ref[...] = (a * _sigmoid(a) * b).astype(o_ref.dtype)

    return pl.pallas_call(
        body, name="conv_act_fwd", grid=(nf,), out_shape=jax.ShapeDtypeStruct((s, f), _MXU),
        in_specs=[pl.BlockSpec((s, tc), lambda n: (0, n)), pl.BlockSpec((s, tc), lambda n: (0, n + nf)),
                  pl.BlockSpec((3, tc), lambda n: (0, n)), pl.BlockSpec((3, tc), lambda n: (0, n + nf)),
                  pl.BlockSpec((1, tc), lambda n: (0, n)), pl.BlockSpec((1, tc), lambda n: (0, n + nf))],
        out_specs=pl.BlockSpec((s, tc), lambda n: (0, n)), compiler_params=_cp("parallel"),
    )(up_pre, up_pre, conv_w, conv_w, conv_b, conv_b)


def _conv_act_bwd(up_pre, d_act, conv_w, conv_b):
    s, f2 = up_pre.shape
    f = f2 // 2
    tc = _tile(f, 256)
    nf = f // tc

    def body(a_ref, b_ref, d_ref, wa_ref, wb_ref, ba_ref, bb_ref,
             du_ref, w0a, w0b, w1a, w1b, w2a, w2b, dba, dbb):
        n = s // CONV_ROWS
        zero8 = jnp.zeros((SUBLANES, tc), _F32)
        ext_rows = CONV_ROWS + SUBLANES

        def fold(x):
            out = x[0:SUBLANES]
            for k in range(1, CONV_ROWS // SUBLANES):
                out = out + x[k * SUBLANES:(k + 1) * SUBLANES]
            return out

        def chunk(r0, carry):
            nxt, acc = carry
            a, xa, xa1, xa2 = _conv_rows(_conv_window(a_ref, r0), wa_ref, ba_ref)
            b, xb, xb1, xb2 = _conv_rows(_conv_window(b_ref, r0), wb_ref, bb_ref)
            sg = _sigmoid(a)
            d = d_ref[pl.ds(r0, CONV_ROWS), :]
            du_a = d * b * (sg * (1.0 + a * (1.0 - sg)))
            du_b = d * (a * sg)
            new_acc = []
            for h, (du, x0, x1, x2, w_ref) in enumerate(((du_a, xa, xa1, xa2, wa_ref), (du_b, xb, xb1, xb2, wb_ref))):
                ext = jnp.concatenate([du, nxt[h]], axis=0)
                u1 = pltpu.roll(ext, ext_rows - 1, 0)[:CONV_ROWS]
                u2 = pltpu.roll(ext, ext_rows - 2, 0)[:CONV_ROWS]
                du_ref[h, pl.ds(r0, CONV_ROWS), :] = (w_ref[2:3, :] * du + w_ref[1:2, :] * u1
                                                      + w_ref[0:1, :] * u2).astype(du_ref.dtype)
                new_acc += [acc[4 * h] + fold(du * x2), acc[4 * h + 1] + fold(du * x1), acc[4 * h + 2] + fold(du * x0),
                            acc[4 * h + 3] + fold(du)]
            return (du_a[:SUBLANES], du_b[:SUBLANES]), tuple(new_acc)

        def step(i, carry):
            return chunk(pl.multiple_of((n - 1 - i) * CONV_ROWS, CONV_ROWS), carry)

        carry = lax.fori_loop(0, n - 1, step, ((zero8, zero8), (zero8,) * 8))
        _, acc = chunk(0, carry)
        for ref, val in zip((w0a, w1a, w2a, dba, w0b, w1b, w2b, dbb), acc):
            ref[...] = _colsum(val)

    col_a = pl.BlockSpec((s, tc), lambda n: (0, n))
    col_b = pl.BlockSpec((s, tc), lambda n: (0, n + nf))
    vec_a = pl.BlockSpec((1, tc), lambda n: (0, n))
    vec_b = pl.BlockSpec((1, tc), lambda n: (0, n + nf))
    vec = jax.ShapeDtypeStruct((1, f), _F32)
    outs = pl.pallas_call(
        body, name="conv_act_bwd", grid=(nf,),
        in_specs=[col_a, col_b, col_a, pl.BlockSpec((3, tc), lambda n: (0, n)),
                  pl.BlockSpec((3, tc), lambda n: (0, n + nf)), vec_a, vec_b],
        out_specs=[pl.BlockSpec((2, s, tc), lambda n: (0, 0, n))] + [vec_a] * 8,
        out_shape=[jax.ShapeDtypeStruct((2, s, f), _MXU)] + [vec] * 8, compiler_params=_cp("parallel"),
    )(up_pre, up_pre, d_act, conv_w, conv_w, conv_b, conv_b)
    du, w0a, w0b, w1a, w1b, w2a, w2b, dba, dbb = outs
    cat = lambda p, q: jnp.concatenate([p, q], axis=1)
    return du, cat(w0a, w0b), cat(w1a, w1b), cat(w2a, w2b), cat(dba, dbb)


def _sgu_recompute(zu_ref, zv_ref, lng_ref, lnb_ref, wm_ref, bs_ref, nh):
    zu, zv = zu_ref[...], zv_ref[...]
    u = _gelu(zu)
    gv = _gelu(zv)
    xc = gv - _rowmean(gv)
    rs = lax.rsqrt(_rowmean(xc * xc) + EPS)
    vh = xc * rs
    v = vh * lng_ref[...] + lnb_ref[...]
    mixed = []
    for h in range(nh):
        vhd = v[:, h * CHUNK:(h + 1) * CHUNK].astype(_MXU)
        mixed.append(jnp.dot(wm_ref[h].astype(_MXU), vhd, preferred_element_type=_F32) + bs_ref[h])
    return zu, zv, u, vh, rs, v, mixed


def _sgu_fwd(z, ln_g, ln_b, wm, bs):
    s = z.shape[0]
    nh = wm.shape[0]
    hd = nh * CHUNK

    def body(zu_ref, zv_ref, lng_ref, lnb_ref, wm_ref, bs_ref, y_ref):
        _, _, u, _, _, _, mixed = _sgu_recompute(zu_ref, zv_ref, lng_ref, lnb_ref, wm_ref, bs_ref, nh)
        for h in range(nh):
            y_ref[:, h * CHUNK:(h + 1) * CHUNK] = u[:, h * CHUNK:(h + 1) * CHUNK] * mixed[h]

    vec = pl.BlockSpec((1, hd), lambda i: (0, 0))
    return pl.pallas_call(
        body, name="sgu_fwd", grid=(s // CHUNK,), out_shape=jax.ShapeDtypeStruct((s, hd), _F32),
        in_specs=[pl.BlockSpec((CHUNK, hd), lambda i: (i, 1)), pl.BlockSpec((CHUNK, hd), lambda i: (i, 2)), vec, vec,
                  pl.BlockSpec((nh, CHUNK, CHUNK), lambda i: (0, 0, 0)), pl.BlockSpec((nh, CHUNK, 1), lambda i: (0, 0, 0))],
        out_specs=pl.BlockSpec((CHUNK, hd), lambda i: (i, 0)), compiler_params=_cp("parallel"),
    )(z, z, ln_g, ln_b, wm, bs)


def _sgu_bwd(z, dy, dz_ssm, ln_g, ln_b, wm, bs):
    s = z.shape[0]
    nh = wm.shape[0]
    hd = nh * CHUNK

    def body(zu_ref, zv_ref, dy_ref, dzs_ref, lng_ref, lnb_ref, wm_ref, bs_ref,
             dz_ref, dlg_ref, dlb_ref, dwm_ref, dbs_ref, dv_scr):
        first = pl.program_id(0) == 0
        _zero_first(first, dlg_ref, dlb_ref, dwm_ref, dbs_ref)
        zu, zv, u, vh, rs, v, mixed = _sgu_recompute(zu_ref, zv_ref, lng_ref, lnb_ref, wm_ref, bs_ref, nh)
        dy = dy_ref[...]
        dz_ref[:, 0:hd] = dzs_ref[...]
        for h in range(nh):
            cols = slice(h * CHUNK, (h + 1) * CHUNK)
            dyh = dy[:, cols]
            dz_ref[:, hd + h * CHUNK:hd + (h + 1) * CHUNK] = (dyh * mixed[h] * _gelu_grad(zu[:, cols])).astype(dz_ref.dtype)
            dm = dyh * u[:, cols]
            dmx = dm.astype(_MXU)
            _acc(dbs_ref.at[h], first, jnp.sum(dm, axis=1, keepdims=True))
            _acc(dwm_ref.at[h], first,
                 lax.dot_general(dmx, v[:, cols].astype(_MXU), (((1,), (1,)), ((), ())), preferred_element_type=_F32))
            dv_scr[:, cols] = lax.dot_general(wm_ref[h].astype(_MXU), dmx, (((0,), (0,)), ((), ())),
                                              preferred_element_type=_F32)
        dv = dv_scr[...]
        _acc(dlg_ref, first, _colsum(dv * vh))
        _acc(dlb_ref, first, _colsum(dv))
        dvh = dv * lng_ref[...]
        dgv = rs * (dvh - _rowmean(dvh) - vh * _rowmean(dvh * vh))
        dz_ref[:, 2 * hd:3 * hd] = (dgv * _gelu_grad(zv)).astype(dz_ref.dtype)

    vec = pl.BlockSpec((1, hd), lambda i: (0, 0))
    wspec = pl.BlockSpec((nh, CHUNK, CHUNK), lambda i: (0, 0, 0))
    bspec = pl.BlockSpec((nh, CHUNK, 1), lambda i: (0, 0, 0))
    rows = pl.BlockSpec((CHUNK, hd), lambda i: (i, 0))
    return pl.pallas_call(
        body, name="sgu_bwd", grid=(s // CHUNK,),
        out_shape=[jax.ShapeDtypeStruct((s, 3 * hd), _MXU), jax.ShapeDtypeStruct((1, hd), _F32),
                   jax.ShapeDtypeStruct((1, hd), _F32), jax.ShapeDtypeStruct((nh, CHUNK, CHUNK), _F32),
                   jax.ShapeDtypeStruct((nh, CHUNK, 1), _F32)],
        in_specs=[pl.BlockSpec((CHUNK, hd), lambda i: (i, 1)), pl.BlockSpec((CHUNK, hd), lambda i: (i, 2)),
                  rows, rows, vec, vec, wspec, bspec],
        out_specs=[pl.BlockSpec((CHUNK, 3 * hd), lambda i: (i, 0)), vec, vec, wspec, bspec],
        scratch_shapes=[pltpu.VMEM((CHUNK, hd), _F32)], compiler_params=_cp("arbitrary"),
    )(z, z, dy, dz_ssm, ln_g, ln_b, wm, bs)


def _ssm_prep(log_dt, a_re, a_im, b_re_t, b_im_t):
    gn = a_re.shape[1]

    def body(ldt_ref, are_ref, aim_ref, br_ref, bi_ref, pr_ref, pi_ref, bbr_ref, bbi_ref):
        dt = jnp.exp(ldt_ref[...])
        are, aim = are_ref[...], aim_ref[...]
        k = (lax.broadcasted_iota(jnp.int32, (SUBLANES, gn), 0) + 1).astype(_F32)
        mag = jnp.exp(k * (are * dt))
        ang = k * (aim * dt)
        pr_ref[...] = mag * jnp.cos(ang)
        pi_ref[...] = mag * jnp.sin(ang)
        m1 = jnp.exp(are * dt)
        lr, li = m1 * jnp.cos(aim * dt), m1 * jnp.sin(aim * dt)
        den = are * are + aim * aim
        nr = lr - 1.0
        f_re = (nr * are + li * aim) / den
        f_im = (li * are - nr * aim) / den
        bbr_ref[...] = f_re * br_ref[...] - f_im * bi_ref[...]
        bbi_ref[...] = f_re * bi_ref[...] + f_im * br_ref[...]

    pw = jax.ShapeDtypeStruct((SUBLANES, gn), _F32)
    bb = jax.ShapeDtypeStruct(b_re_t.shape, _F32)
    return pl.pallas_call(body, name="ssm_prep", out_shape=[pw, pw, bb, bb])(log_dt, a_re, a_im, b_re_t, b_im_t)


def _ssm_prep_bwd(log_dt, a_re, a_im, b_re_t, b_im_t, d_bbr, d_bbi, d_lr, d_li):
    def body(ldt_ref, are_ref, aim_ref, br_ref, bi_ref, dbr_ref, dbi_ref, dlr_ref, dli_ref,
             obr_ref, obi_ref, oar_ref, oai_ref, odt_ref):
        dt = jnp.exp(ldt_ref[...])
        are, aim = are_ref[...], aim_ref[...]
        m1 = jnp.exp(are * dt)
        lr, li = m1 * jnp.cos(aim * dt), m1 * jnp.sin(aim * dt)
        den = are * are + aim * aim
        nr = lr - 1.0
        f_re = (nr * are + li * aim) / den
        f_im = (li * are - nr * aim) / den
        br, bi, dbr, dbi = br_ref[...], bi_ref[...], dbr_ref[...], dbi_ref[...]
        obr_ref[...] = f_re * dbr + f_im * dbi
        obi_ref[...] = f_re * dbi - f_im * dbr
        gf_re = _colsum(br * dbr + bi * dbi)
        gf_im = _colsum(br * dbi - bi * dbr)
        il_re, il_im = are / den, -aim / den
        glb_re = dlr_ref[...] + (il_re * gf_re + il_im * gf_im)
        glb_im = dli_ref[...] + (il_re * gf_im - il_im * gf_re)
        q_re = -(f_re * il_re - f_im * il_im)
        q_im = -(f_re * il_im + f_im * il_re)
        gl_re = q_re * gf_re + q_im * gf_im
        gl_im = q_re * gf_im - q_im * gf_re
        gl_re = gl_re + dt * (lr * glb_re + li * glb_im)
        gl_im = gl_im + dt * (lr * glb_im - li * glb_re)
        w_re = are * lr - aim * li
        w_im = are * li + aim * lr
        oar_ref[...] = gl_re
        oai_ref[...] = gl_im
        odt_ref[...] = w_re * glb_re + w_im * glb_im

    bb = jax.ShapeDtypeStruct(b_re_t.shape, _F32)
    v = jax.ShapeDtypeStruct(a_re.shape, _F32)
    return pl.pallas_call(body, name="ssm_prep_bwd", out_shape=[bb, bb, v, v, v])(
        log_dt, a_re, a_im, b_re_t, b_im_t, d_bbr, d_bbi, d_lr, d_li)


def _group_sum(d_dt, log_dt):
    def body(d_ref, l_ref, o_ref):
        o_ref[...] = jnp.sum(d_ref[...], axis=1, keepdims=True) * jnp.exp(l_ref[...])

    return pl.pallas_call(body, name="ssm_dt_grad", out_shape=jax.ShapeDtypeStruct(log_dt.shape, _F32))(d_dt, log_dt)


def _scan_rows(src_ref, dst_ref, nrt, steps, ptab, carry0, reverse):
    ns = BLOCK_ST
    row = lax.broadcasted_iota(jnp.int32, (SUBLANES, ns), 0)
    pr, pi = ptab

    def body(i, carry):
        cr, ci = carry
        it = (nrt - 1 - i) if reverse else i
        r0 = pl.multiple_of(it * SUBLANES, SUBLANES)
        xr = src_ref[pl.ds(r0, SUBLANES), 0:ns]
        xi = src_ref[pl.ds(r0, SUBLANES), ns:2 * ns]
        for k, (ar, ai) in zip((1, 2, 4), steps):
            if reverse:
                keep = row < SUBLANES - k
                sr = jnp.where(keep, pltpu.roll(xr, SUBLANES - k, 0), 0.0)
                si = jnp.where(keep, pltpu.roll(xi, SUBLANES - k, 0), 0.0)
            else:
                keep = row >= k
                sr = jnp.where(keep, pltpu.roll(xr, k, 0), 0.0)
                si = jnp.where(keep, pltpu.roll(xi, k, 0), 0.0)
            xr, xi = xr + ar * sr - ai * si, xi + ar * si + ai * sr
        xr, xi = xr + pr * cr - pi * ci, xi + pr * ci + pi * cr
        dst_ref[pl.ds(r0, SUBLANES), 0:ns] = xr
        dst_ref[pl.ds(r0, SUBLANES), ns:2 * ns] = xi
        if reverse:
            return xr[0:1, :], xi[0:1, :]
        return xr[SUBLANES - 1:SUBLANES, :], xi[SUBLANES - 1:SUBLANES, :]

    return lax.fori_loop(0, nrt, body, carry0)


def _scan_consts(p_ref, conj):
    ns = BLOCK_ST
    sign = -1.0 if conj else 1.0
    bc = lambda r: jnp.broadcast_to(r, (SUBLANES, ns))
    steps = [(bc(p_ref[k - 1:k, 0:ns]), bc(sign * p_ref[k - 1:k, ns:2 * ns])) for k in (1, 2, 4)]
    return steps


def _ssm_block_fwd(u, bbt_ref, ct_ref, d_ref, wg_ref, bg_ref, p_ref, bu_scr, h_scr, carry_in, nrt):
    ns = BLOCK_ST
    bu_scr[...] = jnp.dot(u.astype(_MXU), bbt_ref[...].astype(_MXU), preferred_element_type=_F32)
    ptab = (p_ref[:, 0:ns], p_ref[:, ns:2 * ns])
    carry = _scan_rows(bu_scr, h_scr, nrt, _scan_consts(p_ref, False), ptab, carry_in, False)
    y = jnp.dot(h_scr[...].astype(_MXU), ct_ref[...].astype(_MXU), preferred_element_type=_F32) + d_ref[...] * u
    yg = _gelu(y)
    gate = _sigmoid(jnp.dot(yg.astype(_MXU), wg_ref[...].astype(_MXU), preferred_element_type=_F32) + bg_ref[...])
    return y, yg, gate, carry


def _ssm_specs(nb, nt, t, reverse):
    tt = (lambda ti: nt - 1 - ti) if reverse else (lambda ti: ti)
    ns2 = 2 * BLOCK_ST
    return dict(
        z=pl.BlockSpec((t, BLOCK_CH), lambda b, ti: (tt(ti), b)),
        bbt=pl.BlockSpec((None, BLOCK_CH, ns2), lambda b, ti: (b, 0, 0)),
        ct=pl.BlockSpec((None, ns2, BLOCK_CH), lambda b, ti: (b, 0, 0)),
        vec=pl.BlockSpec((1, BLOCK_CH), lambda b, ti: (0, b)),
        wg=pl.BlockSpec((None, BLOCK_CH, BLOCK_CH), lambda b, ti: (b, 0, 0)),
        p=pl.BlockSpec((None, SUBLANES, ns2), lambda b, ti: (b, 0, 0)),
        hb=pl.BlockSpec((None, None, SUBLANES, ns2), lambda b, ti: (b, tt(ti), 0, 0)),
        h=pl.BlockSpec((None, t, ns2), lambda b, ti: (b, tt(ti), 0)),
        acc_vec=pl.BlockSpec((None, 1, ns2), lambda b, ti: (b, 0, 0)),
    )


def _ssm_fwd(z, bbt, ct, dvec, wg, bglu, ptab):
    s = z.shape[0]
    nb = bbt.shape[0]
    t = _tile(s, TIME_TILE, SUBLANES)
    nt = s // t
    ns = BLOCK_ST
    sp = _ssm_specs(nb, nt, t, False)

    def body(z_ref, bbt_ref, ct_ref, d_ref, wg_ref, bg_ref, p_ref, y2_ref, y_ref, h_ref, hb_ref, bu_scr, h_scr, carry_scr):
        ti = pl.program_id(1)

        @pl.when(ti == 0)
        def _():
            carry_scr[...] = jnp.zeros_like(carry_scr)

        hb_ref[...] = carry_scr[...]
        carry_in = (carry_scr[0:1, 0:ns], carry_scr[0:1, ns:2 * ns])
        y, yg, gate, (cr, ci) = _ssm_block_fwd(z_ref[...], bbt_ref, ct_ref, d_ref, wg_ref, bg_ref, p_ref,
                                               bu_scr, h_scr, carry_in, t // SUBLANES)
        y2_ref[...] = yg * gate
        y_ref[...] = y
        h_ref[...] = h_scr[...].astype(h_ref.dtype)
        carry_scr[:, 0:ns] = jnp.broadcast_to(cr, (SUBLANES, ns))
        carry_scr[:, ns:2 * ns] = jnp.broadcast_to(ci, (SUBLANES, ns))

    ych = jax.ShapeDtypeStruct((s, nb * BLOCK_CH), _F32)
    return pl.pallas_call(
        body, name="ssm_fwd", grid=(nb, nt),
        out_shape=[ych, ych, jax.ShapeDtypeStruct((nb, s, 2 * ns), _MXU),
                   jax.ShapeDtypeStruct((nb, nt, SUBLANES, 2 * ns), _F32)],
        in_specs=[sp["z"], sp["bbt"], sp["ct"], sp["vec"], sp["wg"], sp["vec"], sp["p"]],
        out_specs=[sp["z"], sp["z"], sp["h"], sp["hb"]],
        scratch_shapes=[pltpu.VMEM((t, 2 * ns), _F32), pltpu.VMEM((t, 2 * ns), _F32), pltpu.VMEM((SUBLANES, 2 * ns), _F32)],
        compiler_params=_cp("parallel", "arbitrary"),
    )(z, bbt, ct, dvec, wg, bglu, ptab)


def _ssm_bwd(z, y_pre, h_all, dy2, hb, bbt, ct, dvec, wg, bglu, ptab, ptab_rev):
    s = z.shape[0]
    nb = bbt.shape[0]
    t = _tile(s, TIME_TILE, SUBLANES)
    nt = s // t
    ns = BLOCK_ST
    sp = _ssm_specs(nb, nt, t, True)
    tn_dims = (((0,), (0,)), ((), ()))
    nt_dims = (((1,), (1,)), ((), ()))

    def body(z_ref, y_ref, h_ref, dy2_ref, hb_ref, bbt_ref, ct_ref, d_ref, wg_ref, bg_ref, p_ref, pr_ref,
             dz_ref, dbbt_ref, dct_ref, dwg_ref, dlb_ref, dd_ref, dbg_ref, bu_scr, g_scr, gcarry_scr):
        first = pl.program_id(1) == 0

        _zero_first(first, gcarry_scr, dbbt_ref, dct_ref, dwg_ref, dlb_ref, dd_ref, dbg_ref)
        u = z_ref[...]
        hin = hb_ref[...]
        y = y_ref[...]
        yg = _gelu(y)
        gate = _sigmoid(jnp.dot(yg.astype(_MXU), wg_ref[...].astype(_MXU), preferred_element_type=_F32) + bg_ref[...])
        dy2 = dy2_ref[...]
        dpre = dy2 * yg * gate * (1.0 - gate)
        _acc(dbg_ref, first, _colsum(dpre))
        dpx = dpre.astype(_MXU)
        _acc(dwg_ref, first, lax.dot_general(yg.astype(_MXU), dpx, tn_dims, preferred_element_type=_F32))
        dyg = dy2 * gate + lax.dot_general(dpx, wg_ref[...].astype(_MXU), nt_dims, preferred_element_type=_F32)
        dy = dyg * _gelu_grad(y)
        _acc(dd_ref, first, _colsum(dy * u))
        dyx = dy.astype(_MXU)
        hx = h_ref[...]
        h = hx.astype(_F32)
        _acc(dct_ref, first, lax.dot_general(hx, dyx, tn_dims, preferred_element_type=_F32))
        bu_scr[...] = lax.dot_general(dyx, ct_ref[...].astype(_MXU), nt_dims, preferred_element_type=_F32)
        gin = (gcarry_scr[0:1, 0:ns], gcarry_scr[0:1, ns:2 * ns])
        ptab = (pr_ref[:, 0:ns], pr_ref[:, ns:2 * ns])
        gr, gi = _scan_rows(bu_scr, g_scr, t // SUBLANES, _scan_consts(p_ref, True), ptab, gin, True)
        gcarry_scr[:, 0:ns] = jnp.broadcast_to(gr, (SUBLANES, ns))
        gcarry_scr[:, ns:2 * ns] = jnp.broadcast_to(gi, (SUBLANES, ns))
        g = g_scr[...]
        row = lax.broadcasted_iota(jnp.int32, (t, ns), 0)
        hp_re = jnp.where(row == 0, hin[0:1, 0:ns], pltpu.roll(h[:, 0:ns], 1, 0))
        hp_im = jnp.where(row == 0, hin[0:1, ns:2 * ns], pltpu.roll(h[:, ns:2 * ns], 1, 0))
        g_re, g_im = g[:, 0:ns], g[:, ns:2 * ns]
        d_ar = _colsum(g_re * hp_re + g_im * hp_im)
        d_ai = _colsum(g_im * hp_re - g_re * hp_im)
        _acc(dlb_ref, first, jnp.concatenate([d_ar, d_ai], axis=1))
        gx = g.astype(_MXU)
        _acc(dbbt_ref, first, lax.dot_general(u.astype(_MXU), gx, tn_dims, preferred_element_type=_F32))
        dz_ref[...] = (dy * d_ref[...] + lax.dot_general(gx, bbt_ref[...].astype(_MXU), nt_dims,
                                                         preferred_element_type=_F32)).astype(dz_ref.dtype)

    f = lambda shape: jax.ShapeDtypeStruct(shape, _F32)
    return pl.pallas_call(
        body, name="ssm_bwd", grid=(nb, nt),
        out_shape=[jax.ShapeDtypeStruct((s, nb * BLOCK_CH), _MXU), f(bbt.shape), f(ct.shape), f(wg.shape), f((nb, 1, 2 * ns)),
                   f((1, nb * BLOCK_CH)), f((1, nb * BLOCK_CH))],
        in_specs=[sp["z"], sp["z"], sp["h"], sp["z"], sp["hb"], sp["bbt"], sp["ct"], sp["vec"], sp["wg"], sp["vec"], sp["p"],
                  sp["p"]],
        out_specs=[sp["z"], sp["bbt"], sp["ct"], sp["wg"], sp["acc_vec"], sp["vec"], sp["vec"]],
        scratch_shapes=[pltpu.VMEM((t, 2 * ns), _F32), pltpu.VMEM((t, 2 * ns), _F32), pltpu.VMEM((SUBLANES, 2 * ns), _F32)],
        compiler_params=_cp("parallel", "arbitrary"),
    )(z, y_pre, h_all, dy2, hb, bbt, ct, dvec, wg, bglu, ptab, ptab_rev)


def _mod_part(c_all, w, b):
    d, ns = w.shape
    tn = _tile(ns, 512)

    def body(c_ref, w_ref, b_ref, o_ref):
        c = c_ref[...]
        ca = (c * _sigmoid(c)).astype(_MXU)
        o_ref[...] = jnp.dot(ca, w_ref[...].astype(_MXU), preferred_element_type=_F32) + b_ref[...]

    return pl.pallas_call(
        body, name="mod_part", grid=(ns // tn,), out_shape=jax.ShapeDtypeStruct((8, ns), _F32),
        in_specs=[pl.BlockSpec((8, d), lambda n: (0, 0)), pl.BlockSpec((d, tn), lambda n: (0, n)),
                  pl.BlockSpec((1, tn), lambda n: (0, n))],
        out_specs=pl.BlockSpec((8, tn), lambda n: (0, n)), compiler_params=_cp("parallel"),
    )(c_all, w, b)


def _adamw_math(w, g, m, v):
    m = ADAM_B1 * m + (1.0 - ADAM_B1) * g
    v = ADAM_B2 * v + (1.0 - ADAM_B2) * (g * g)
    m_hat = m / (1.0 - ADAM_B1 ** ADAM_STEP)
    v_hat = v / (1.0 - ADAM_B2 ** ADAM_STEP)
    delta = -ADAM_LR * (m_hat / (jnp.sqrt(v_hat) + ADAM_EPS) + ADAM_WD * w)
    return delta, m, v


def _adamw(w, g, m, v, name):
    r, c = w.shape
    tc = c if c <= 4096 else _tile(c, 4096)
    tr = _tile(r, max(SUBLANES, (1 << 18) // tc), SUBLANES)

    def body(w_ref, g_ref, m_ref, v_ref, go_ref, d_ref, mo_ref, vo_ref):
        g = g_ref[...]
        go_ref[...] = g
        d_ref[...], mo_ref[...], vo_ref[...] = _adamw_math(w_ref[...], g, m_ref[...], v_ref[...])

    spec = pl.BlockSpec((tr, tc), lambda i, j: (i, j))
    out = jax.ShapeDtypeStruct((r, c), _F32)
    return pl.pallas_call(
        body, name=name, grid=(r // tr, c // tc), in_specs=[spec] * 4, out_specs=[spec] * 4, out_shape=[out] * 4,
        compiler_params=_cp("parallel", "parallel"),
    )(w, g, m, v)


def _adamw_halves(w, g2, m, v, name):
    r, c = w.shape
    tr, tc = _tile(r, 256, SUBLANES), _tile(c // 2, 1024)
    nph = (c // 2) // tc

    def body(w_ref, g_ref, m_ref, v_ref, go_ref, d_ref, mo_ref, vo_ref):
        g = g_ref[...]
        go_ref[...] = g
        d_ref[...], mo_ref[...], vo_ref[...] = _adamw_math(w_ref[...], g, m_ref[...], v_ref[...])

    spec = pl.BlockSpec((tr, tc), lambda i, j: (i, j))
    out = jax.ShapeDtypeStruct((r, c), _F32)
    return pl.pallas_call(
        body, name=name, grid=(r // tr, c // tc),
        in_specs=[spec, pl.BlockSpec((None, tr, tc), lambda i, j: (j // nph, i, j % nph)), spec, spec],
        out_specs=[spec] * 4, out_shape=[out] * 4, compiler_params=_cp("parallel", "parallel"),
    )(w, g2, m, v)


def _wada_update(c_t, dm, w, m, v):
    d, ns = w.shape
    tr, tc = _tile(d, 256, SUBLANES), _tile(ns, 1024)

    def body(c_ref, dm_ref, w_ref, m_ref, v_ref, g_ref, d_ref, mo_ref, vo_ref):
        c = c_ref[...]
        ca = c * _sigmoid(c)
        dmv = dm_ref[...]
        g = ca[:, 0:1] * dmv[0:1, :]
        for b in range(1, 8):
            g = g + ca[:, b:b + 1] * dmv[b:b + 1, :]
        g_ref[...] = g
        d_ref[...], mo_ref[...], vo_ref[...] = _adamw_math(w_ref[...], g, m_ref[...], v_ref[...])

    spec = pl.BlockSpec((tr, tc), lambda i, j: (i, j))
    out = jax.ShapeDtypeStruct((d, ns), _F32)
    return pl.pallas_call(
        body, name="wada_update", grid=(d // tr, ns // tc),
        in_specs=[pl.BlockSpec((tr, 8), lambda i, j: (i, 0)), pl.BlockSpec((8, tc), lambda i, j: (0, j)), spec, spec, spec],
        out_specs=[spec] * 4, out_shape=[out] * 4, compiler_params=_cp("parallel", "parallel"),
    )(c_t, dm, w, m, v)


def _small_reduce(gathered):
    _, r, c = gathered.shape
    tr = _tile(r, 512, SUBLANES)

    def body(q_ref, g_ref):
        g = q_ref[0]
        for k in range(1, 8):
            g = g + q_ref[k]
        g_ref[...] = g

    return pl.pallas_call(
        body, name="small_reduce", grid=(r // tr,), out_shape=jax.ShapeDtypeStruct((r, c), _F32),
        in_specs=[pl.BlockSpec((8, tr, c), lambda i: (0, i, 0))], out_specs=pl.BlockSpec((tr, c), lambda i: (i, 0)),
        compiler_params=_cp("parallel"),
    )(gathered)


def _adamw_many(ws, gs, ms, vs, steps, name):
    n = len(ws)

    def body(*refs):
        w_refs, g_refs, m_refs, v_refs = refs[0:n], refs[n:2 * n], refs[2 * n:3 * n], refs[3 * n:4 * n]
        d_refs, mo_refs, vo_refs = refs[4 * n:5 * n], refs[5 * n:6 * n], refs[6 * n:7 * n]
        for i in range(n):
            d_refs[i][...], mo_refs[i][...], vo_refs[i][...] = _adamw_math(
                w_refs[i][...], g_refs[i][...], m_refs[i][...], v_refs[i][...])

    def spec(a):
        nd = a.ndim
        if steps == 1:
            return pl.BlockSpec(a.shape, lambda i: (0,) * nd)
        return pl.BlockSpec((a.shape[0] // steps,) + a.shape[1:], lambda i: (i,) + (0,) * (nd - 1))

    specs = [spec(w) for w in ws]
    outs = pl.pallas_call(
        body, name=name, grid=(steps,), in_specs=specs * 4, out_specs=specs * 3,
        out_shape=[jax.ShapeDtypeStruct(w.shape, _F32) for w in ws] * 3, compiler_params=_cp("parallel"),
    )(*ws, *gs, *ms, *vs)
    return outs[0:n], outs[n:2 * n], outs[2 * n:3 * n]


def _block_diag(x):
    nb, g, p, q = x.shape
    eye = jnp.eye(g, dtype=x.dtype)
    return (x[:, :, :, None, :] * eye[None, :, None, :, None]).reshape(nb, g * p, g * q)


def _block_diag_take(x, p, q):
    nb = x.shape[0]
    g = GROUPS_PER_BLOCK
    eye = jnp.eye(g, dtype=x.dtype)
    return jnp.sum(x.reshape(nb, g, p, g, q) * eye[None, :, None, :, None], axis=3)


_VIEWS = {"ssm_b_re": ((0, 2, 1), (0, 2, 1)), "ssm_b_im": ((0, 2, 1), (0, 2, 1)),
          "ssm_w_glu": ((1, 2, 0), (2, 0, 1)), "ssm_b_glu": ((1, 0), (1, 0))}


def _to_view(name, a):
    return a.transpose(_VIEWS[name][0]) if name in _VIEWS else a


def _from_view(name, a):
    return a.transpose(_VIEWS[name][1]) if name in _VIEWS else a


class _Pack:
    def __init__(self, shapes):
        self.shapes = shapes
        self.offsets = {}
        off = 0
        for name, shape in shapes.items():
            n = math.prod(shape)
            self.offsets[name] = (off, n)
            off += -(-n // (SUBLANES * LANES)) * (SUBLANES * LANES)
        self.rows = -(-off // (256 * LANES)) * 256

    def pack(self, arrays):
        parts = []
        off = 0
        for name, shape in self.shapes.items():
            start, n = self.offsets[name]
            if start > off:
                parts.append(jnp.zeros((start - off,), _F32))
            parts.append(arrays[name].reshape(-1).astype(_F32))
            off = start + n
        total = self.rows * LANES
        if total > off:
            parts.append(jnp.zeros((total - off,), _F32))
        return jnp.concatenate(parts).reshape(self.rows, LANES)

    def unpack(self, buf):
        flat = buf.reshape(-1)
        return {name: flat[start:start + n].reshape(self.shapes[name]) for name, (start, n) in self.offsets.items()}


_SMALL = ["b_ada", "g_pre_mix", "g_post_mix", "ssm_log_dt", "ssm_a_re", "ssm_a_im", "ssm_b_re", "ssm_b_im", "ssm_c_re",
          "ssm_c_im", "ssm_d", "ssm_w_glu", "ssm_b_glu", "sgu_ln_g", "sgu_ln_b", "sgu_w", "sgu_b", "g_out_ssm",
          "g_out_sgu", "g_pre_ffn", "g_post_ffn", "conv_b"]
_WEIGHTS = ["w_ada", "b_ada", "g_pre_mix", "g_post_mix", "w_in", "ssm_log_dt", "ssm_a_re", "ssm_a_im", "ssm_b_re",
            "ssm_b_im", "ssm_c_re", "ssm_c_im", "ssm_d", "ssm_w_glu", "ssm_b_glu", "sgu_ln_g", "sgu_ln_b", "sgu_w", "sgu_b",
            "g_out_ssm", "g_out_sgu", "w_out", "g_pre_ffn", "g_post_ffn", "w_up", "conv_w", "conv_b", "w_down"]


def _step(p, m, v, x, c, tgt):
    s, d = x.shape
    mx, my, mc = lax.axis_index("x"), lax.axis_index("y"), lax.axis_index("c")
    chip = 2 * mx + my
    dev = 4 * mx + 2 * my + mc
    sel = jnp.stack([chip, mc]).astype(jnp.int32)
    g_cnt, n_st = p["ssm_a_re"].shape
    nb = g_cnt // GROUPS_PER_BLOCK
    gn = g_cnt * n_st
    d_ssm = g_cnt * SSM_GROUP
    nh = p["sgu_w"].shape[0]
    assert nh * CHUNK == d_ssm and 2 * d_ssm == d and n_st == SSM_STATE

    shards = lambda g: g.reshape(4, g.shape[1] * g.shape[2], g.shape[3])
    buf_in = _cast_into_slot(p["w_in"], sel, sel, "cast_w_in")

    ns_ada = p["w_ada"].shape[1]
    nc_conv = p["conv_w"].shape[1]
    first = jnp.concatenate([jnp.broadcast_to(c, (8, d)), jnp.pad(p["conv_w"], ((0, 5), (0, 0)))], axis=1)
    first_all = _all_gather8(_own_slot(first, dev), "gather_c_conv")
    c_all = first_all[:, 0, :d]
    conv_w_full = jnp.concatenate([first_all[2 * j, 0:3, d:] for j in range(4)], axis=1)
    b_ada_mine = lax.dynamic_slice_in_dim(p["b_ada"], chip * ns_ada, ns_ada, axis=1)
    mod_all = _all_gather8(_own_slot(_mod_part(c_all, p["w_ada"], b_ada_mine), dev), "gather_mod")
    (sems_in,), (buf_in,), tok = _gather_start([buf_in], mod_all, "gather_start_in")
    buf_out, buf_up, buf_down = [_cast_into_slot(p[n], sel, tok, "cast_" + n) for n in ("w_out", "w_up", "w_down")]
    mod_rows = lax.dynamic_index_in_dim(mod_all, dev, axis=1, keepdims=False)
    mod = jnp.concatenate([mod_rows[0], mod_rows[2], mod_rows[4], mod_rows[6]]).reshape(N_MOD, 1, d)
    sh1, sc1, gt1, sh2, sc2, gt2 = [mod[i] for i in range(N_MOD)]

    ldt_l = jnp.repeat(p["ssm_log_dt"], n_st, axis=1)
    are_l, aim_l = p["ssm_a_re"].reshape(1, gn), p["ssm_a_im"].reshape(1, gn)
    bre_t, bim_t = p["ssm_b_re"].reshape(gn, SSM_GROUP).T, p["ssm_b_im"].reshape(gn, SSM_GROUP).T
    pw_re, pw_im, bb_re, bb_im = _ssm_prep(ldt_l, are_l, aim_l, bre_t, bim_t)
    blocks = lambda t: t.reshape(t.shape[0], nb, GROUPS_PER_BLOCK * n_st).transpose(1, 0, 2)
    ptab = jnp.concatenate([blocks(pw_re), blocks(pw_im)], axis=2)
    ptab_rev = jnp.concatenate([blocks(pw_re)[:, ::-1], -blocks(pw_im)[:, ::-1]], axis=2)
    bd = lambda t: t.reshape(SSM_GROUP, nb, GROUPS_PER_BLOCK, n_st).transpose(1, 2, 0, 3)
    bbt = jnp.concatenate([_block_diag(bd(bb_re)), _block_diag(bd(bb_im))], axis=2).astype(_MXU)
    cd = lambda t: t.reshape(nb, GROUPS_PER_BLOCK, SSM_GROUP, n_st).transpose(0, 1, 3, 2)
    ct = jnp.concatenate([_block_diag(cd(p["ssm_c_re"])), -_block_diag(cd(p["ssm_c_im"]))], axis=1).astype(_MXU)
    wg = _block_diag(p["ssm_w_glu"].reshape(nb, GROUPS_PER_BLOCK, SSM_GROUP, SSM_GROUP)).astype(_MXU)
    dvec = p["ssm_d"]
    bglu = p["ssm_b_glu"].reshape(1, d_ssm)
    mask = jnp.tril(jnp.ones((CHUNK, CHUNK), _F32))
    wm = (p["sgu_w"] * mask[None]).astype(_MXU)
    bs = p["sgu_b"].reshape(nh, CHUNK, 1)

    h1 = _fwd_pre_mix(x, p["g_pre_mix"], _after(sc1, tok), sh1)
    buf_in = _gather_wait(sems_in, buf_in, h1, "gather_wait_in")
    w_in4 = shards(_pair_forward([buf_in], "pair_forward_in")[0])
    (sems_out, sems_up, sems_down), (buf_out, buf_up, buf_down), tok = _gather_start(
        [buf_out, buf_up, buf_down], w_in4, "gather_start_rest")
    z = _mm_nn(h1, w_in4, _F32, "mm_in", after=tok)
    y_ssm, y_pre, h_all, hb = _ssm_fwd(z, bbt, ct, dvec, wg, bglu, ptab)
    y_sgu = _sgu_fwd(z, p["sgu_ln_g"], p["sgu_ln_b"], wm, bs)
    ycat = _mix_norm_fwd(y_ssm, y_sgu, p["g_out_ssm"], p["g_out_sgu"])
    buf_out = _gather_wait(sems_out, buf_out, ycat, "gather_wait_out")
    w_out_full = _pair_forward([buf_out], "pair_forward_out")[0].reshape(1, d, d)
    o = _mm_nn(ycat, w_out_full, _F32, "mm_out")
    x1, h2 = _fwd_mid(o, x, gt1, p["g_post_mix"], p["g_pre_ffn"], sc2, sh2)
    buf_up = _gather_wait(sems_up, buf_up, h2, "gather_wait_up")
    w_up4 = shards(_pair_forward([buf_up], "pair_forward_up")[0])
    up_pre = _mm_nn(h2, w_up4, _F32, "mm_up")
    act = _conv_act_fwd(up_pre, conv_w_full, p["conv_b"])
    buf_down = _gather_wait(sems_down, buf_down, act, "gather_wait_down")
    w_down_full = _pair_forward([buf_down], "pair_forward_down")[0].reshape(1, -1, d)
    f = _mm_nn(act, w_down_full, _F32, "mm_down", tk=5632)
    dx2, df, d_gt2, d_g_post_ffn, loss = _loss_and_post_ffn_bwd(f, x1, tgt, gt2, p["g_post_ffn"])

    def reduce_next(swap, n, after):
        sems, gw, land, _ = swap
        gw, got = _swap_wait(sems, gw, land, after, "swap_wait_" + n)
        return _scatter_start(_pair_sum(gw, got, sel, "pair_sum_" + n), "scatter_start_" + n)

    d_act = _mm_nt(df, w_down_full, _F32, "mm_d_act", tk=2048)
    swap_down = _swap_start(_mm_tn_rows(act, df, "mm_gw_down"), "swap_start_w_down")
    d_up_pre, d_cw0, d_cw1, d_cw2, d_conv_b = _conv_act_bwd(up_pre, d_act, conv_w_full, _after(p["conv_b"], swap_down[3]))
    red_down = reduce_next(swap_down, "w_down", d_conv_b)
    dh2 = _mm_nt(d_up_pre, w_up4, _F32, "mm_dh2", tk=2816, after=red_down[3])
    swap_up = _swap_start(_mm_tn_cols(h2, d_up_pre, "mm_gw_up"), "swap_start_w_up")
    dx1, d_o, d_sc2, d_sh2, d_g_pre_ffn, d_gt1, d_g_post_mix = _bwd_mid(
        dh2, x1, dx2, o, p["g_pre_ffn"], _after(sc2, swap_up[3]), gt1, p["g_post_mix"])
    red_up = reduce_next(swap_up, "w_up", d_g_post_mix)
    d_ycat = _mm_nt(d_o, w_out_full, _F32, "mm_d_ycat", tn=1024, tk=2048, after=red_up[3])
    swap_out = _swap_start(_mm_tn_rows(ycat, d_o, "mm_gw_out"), "swap_start_w_out")
    dy_ssm, dy_sgu, d_g_out_ssm, d_g_out_sgu = _mix_norm_bwd(
        d_ycat, y_ssm, y_sgu, _after(p["g_out_ssm"], swap_out[3]), p["g_out_sgu"])
    red_out = reduce_next(swap_out, "w_out", d_g_out_sgu)
    dz_ssm, d_bbt, d_ct, d_wg, d_lb, d_ssm_d, d_bglu = _ssm_bwd(z, y_pre, h_all, dy_ssm, hb, bbt, ct,
                                                                _after(dvec, red_out[3]), wg, bglu, ptab, ptab_rev)
    dz, d_ln_g, d_ln_b, d_wm, d_bs = _sgu_bwd(z, dy_sgu, dz_ssm, p["sgu_ln_g"], p["sgu_ln_b"], wm, bs)
    dh1 = _mm_nt(dz, w_in4, _F32, "mm_dh1")
    swap_in = _swap_start(_mm_tn_cols(h1, dz, "mm_gw_in"), "swap_start_w_in")
    dx, d_sc1, d_sh1, d_g_pre_mix = _bwd_pre_mix(dh1, x, dx1, p["g_pre_mix"], _after(sc1, swap_in[3]))
    red_in = reduce_next(swap_in, "w_in", d_g_pre_mix)

    nsb = BLOCK_ST
    lanes = lambda t: t.transpose(2, 0, 1, 3).reshape(SSM_GROUP, gn)
    d_bbr = lanes(_block_diag_take(d_bbt[:, :, :nsb], SSM_GROUP, n_st))
    d_bbi = lanes(_block_diag_take(d_bbt[:, :, nsb:], SSM_GROUP, n_st))
    d_lr, d_li = d_lb[:, 0, :nsb].reshape(1, gn), d_lb[:, 0, nsb:].reshape(1, gn)
    d_bre_t, d_bim_t, d_are, d_aim, d_dt = _ssm_prep_bwd(ldt_l, are_l, aim_l, bre_t, bim_t, d_bbr, d_bbi, d_lr, d_li)
    d_log_dt = _group_sum(d_dt.reshape(g_cnt, n_st), p["ssm_log_dt"].reshape(g_cnt, 1))
    c_grad = lambda t: _block_diag_take(t, n_st, SSM_GROUP).transpose(0, 1, 3, 2).reshape(g_cnt, SSM_GROUP, n_st)
    small = {
        "b_ada": jnp.concatenate([d_sh1, _after(d_sc1, red_in[3]), d_gt1, d_sh2, d_sc2, d_gt2], axis=1),
        "g_pre_mix": d_g_pre_mix, "g_post_mix": d_g_post_mix,
        "ssm_log_dt": d_log_dt, "ssm_a_re": d_are, "ssm_a_im": d_aim,
        "ssm_b_re": d_bre_t.T, "ssm_b_im": d_bim_t.T,
        "ssm_c_re": c_grad(d_ct[:, :nsb, :]), "ssm_c_im": -c_grad(d_ct[:, nsb:, :]),
        "ssm_d": d_ssm_d, "ssm_w_glu": _block_diag_take(d_wg, SSM_GROUP, SSM_GROUP), "ssm_b_glu": d_bglu,
        "sgu_ln_g": d_ln_g, "sgu_ln_b": d_ln_b, "sgu_w": d_wm * mask[None], "sgu_b": d_bs,
        "g_out_ssm": d_g_out_ssm, "g_out_sgu": d_g_out_sgu, "g_pre_ffn": d_g_pre_ffn, "g_post_ffn": d_g_post_ffn,
        "conv_b": d_conv_b, "conv_w_all": jnp.concatenate([d_cw0, d_cw1, d_cw2], axis=0),
        "loss_sum": loss,
    }
    small = {n: _to_view(n, a.reshape(p[n].shape)) if n in p else a for n, a in small.items()}
    pk = _Pack({n: a.shape for n, a in small.items()})
    sems_small, small_buf, tok = _gather8_start(_own_slot(pk.pack(small), dev), "gather_small_start")

    big = ["w_down", "w_up", "w_out", "w_in"]
    joins = []
    after = tok
    for n, (sems, pair, land, _) in zip(big, (red_down, red_up, red_out, red_in)):
        pair, land = _scatter_wait(sems, pair, land, after, "scatter_wait_" + n)
        sems_j, half, after = _join_start(_chip_sum(pair, land, sel, "chip_sum_" + n), "join_start_" + n)
        joins.append((sems_j, half))
    big_out = {}
    for n, (sems_j, half) in zip(big, joins):
        j = _join_wait(sems_j, half, after, "join_wait_" + n)
        if n in ("w_in", "w_up"):
            big_out[n] = tuple(_adamw(p[n], j.reshape(p[n].shape), m[n], v[n], "adamw_" + n))
        else:
            big_out[n] = tuple(_adamw_halves(p[n], j, m[n], v[n], "adamw_" + n))
        after = big_out[n][1]

    gathered = _gather8_forward(_gather8_wait(sems_small, small_buf, after, "gather_small_wait"),
                                "gather_small_forward")
    gview = pk.unpack(_small_reduce(gathered))
    gview["conv_w"] = lax.dynamic_slice_in_dim(gview.pop("conv_w_all"), chip * nc_conv, nc_conv, axis=1)
    loss = gview.pop("loss_sum")
    small_names = _SMALL + ["conv_w"]
    per_group = [n for n in small_names if gview[n].ndim >= 2 and gview[n].shape[0] == g_cnt]
    others = [n for n in small_names if n not in per_group]
    grads = {n: _from_view(n, gview[n]) for n in small_names}
    deltas, new_m, new_v = {}, {}, {}
    for names, steps, call in ((per_group, g_cnt // GROUPS_PER_BLOCK, "adamw_s5"), (others, 1, "adamw_small")):
        res = _adamw_many([_to_view(n, p[n]) for n in names], [gview[n] for n in names],
                          [_to_view(n, m[n]) for n in names], [_to_view(n, v[n]) for n in names], steps, call)
        for n, dl, mo, vo in zip(names, *res):
            deltas[n], new_m[n], new_v[n] = _from_view(n, dl), _from_view(n, mo), _from_view(n, vo)

    d_mod_all = gathered.reshape(8, -1)[:, :N_MOD * d]
    d_mod_mine = lax.dynamic_slice_in_dim(d_mod_all, chip * ns_ada, ns_ada, axis=1)
    grads["w_ada"], deltas["w_ada"], new_m["w_ada"], new_v["w_ada"] = _wada_update(
        c_all.T, d_mod_mine, p["w_ada"], m["w_ada"], v["w_ada"])
    for n in big:
        grads[n], deltas[n], new_m[n], new_v[n] = big_out[n]
    return loss[0, 0], dx, grads, deltas, new_m, new_v


def kernel(x, c, w_ada, b_ada, g_pre_mix, g_post_mix, w_in, ssm_log_dt, ssm_a_re, ssm_a_im, ssm_b_re, ssm_b_im, ssm_c_re, ssm_c_im, ssm_d, ssm_w_glu, ssm_b_glu, sgu_ln_g, sgu_ln_b, sgu_w, sgu_b, g_out_ssm, g_out_sgu, w_out, g_pre_ffn, g_post_ffn, w_up, conv_w, conv_b, w_down, loss_target, m_w_ada, m_b_ada, m_g_pre_mix, m_g_post_mix, m_w_in, m_ssm_log_dt, m_ssm_a_re, m_ssm_a_im, m_ssm_b_re, m_ssm_b_im, m_ssm_c_re, m_ssm_c_im, m_ssm_d, m_ssm_w_glu, m_ssm_b_glu, m_sgu_ln_g, m_sgu_ln_b, m_sgu_w, m_sgu_b, m_g_out_ssm, m_g_out_sgu, m_w_out, m_g_pre_ffn, m_g_post_ffn, m_w_up, m_conv_w, m_conv_b, m_w_down, v_w_ada, v_b_ada, v_g_pre_mix, v_g_post_mix, v_w_in, v_ssm_log_dt, v_ssm_a_re, v_ssm_a_im, v_ssm_b_re, v_ssm_b_im, v_ssm_c_re, v_ssm_c_im, v_ssm_d, v_ssm_w_glu, v_ssm_b_glu, v_sgu_ln_g, v_sgu_ln_b, v_sgu_w, v_sgu_b, v_g_out_ssm, v_g_out_sgu, v_w_out, v_g_pre_ffn, v_g_post_ffn, v_w_up, v_conv_w, v_conv_b, v_w_down):
    given = dict(locals())
    drop = lambda a: a if a.ndim == 2 else a[0]
    p = {n: drop(given[n]) for n in _WEIGHTS}
    m = {n: drop(given["m_" + n]) for n in _WEIGHTS}
    v = {n: drop(given["v_" + n]) for n in _WEIGHTS}
    loss, dx, grads, deltas, new_m, new_v = _step(p, m, v, x[0], c, loss_target[0])
    outs = [loss, dx[None]]
    for group in (grads, deltas, new_m, new_v):
        outs += [group[n].reshape(given[n].shape) for n in _WEIGHTS]
    return tuple(outs)
```

```python
import functools
import math

import jax
import jax.numpy as jnp
from jax import lax
from jax.experimental import pallas as pl
from jax.experimental.pallas import tpu as pltpu

_F32 = jnp.float32
_MXU = jnp.bfloat16
_WIRE = jnp.bfloat16

EPS = 1e-6
SSM_GROUP = 16
SSM_STATE = 64
GROUPS_PER_BLOCK = 8
BLOCK_CH = SSM_GROUP * GROUPS_PER_BLOCK
BLOCK_ST = SSM_STATE * GROUPS_PER_BLOCK
CHUNK = 128
TIME_TILE = 512
SUBLANES = 8
LANES = 128
N_MOD = 6
ADAM_LR, ADAM_B1, ADAM_B2, ADAM_EPS, ADAM_WD, ADAM_STEP = 0.001, 0.9, 0.999, 1e-08, 0.01, 10
_VMEM_LIMIT = 56 * 1024 * 1024
_MESH = pl.DeviceIdType.MESH
_ANY = pl.BlockSpec(memory_space=pl.ANY)
_HBM = pl.BlockSpec(memory_space=pltpu.HBM)
_SEM = pl.BlockSpec(memory_space=pltpu.SEMAPHORE)
_VMEM_WHOLE = pl.BlockSpec(memory_space=pltpu.VMEM)
_EFFECT = pltpu.SideEffectType.DATAFLOW_SIDE_EFFECTING
_GELU_C = math.sqrt(2.0 / math.pi)


def _cp(*sem):
    return pltpu.CompilerParams(dimension_semantics=sem, vmem_limit_bytes=_VMEM_LIMIT)


def _tile(dim, target, align=LANES):
    if dim <= target:
        return dim
    best = None
    for t in range(align, target + 1, align):
        if dim % t == 0:
            best = t
    assert best is not None, (dim, target, align)
    return best


def _gelu(x):
    return 0.5 * x * (1.0 + jnp.tanh(_GELU_C * (x + 0.044715 * (x * x * x))))


def _gelu_grad(x):
    t = jnp.tanh(_GELU_C * (x + 0.044715 * (x * x * x)))
    return 0.5 * (1.0 + t) + 0.5 * x * (1.0 - t * t) * (_GELU_C * (1.0 + 3.0 * 0.044715 * x * x))


def _sigmoid(x):
    return 1.0 / (1.0 + jnp.exp(-x))


def _colsum(x):
    return jnp.sum(x, axis=0, keepdims=True)


def _rowmean(x):
    return jnp.mean(x, axis=-1, keepdims=True)


def _zero_first(first, *refs):
    @pl.when(first)
    def _():
        for ref in refs:
            ref[...] = jnp.zeros_like(ref)


def _acc(ref, first, val):
    del first
    ref[...] += val


def _place():
    mx, my, mc = lax.axis_index("x"), lax.axis_index("y"), lax.axis_index("c")
    chips = [(1 - mx, my), (mx, 1 - my), (1 - mx, 1 - my)]
    return mx, my, mc, chips


def _all_gather8(buf, name):
    def body(in_ref, out_ref, send_sems, recv_sems):
        mx, my, mc, chips = _place()
        me, sibling = (mx, my, mc), (mx, my, 1 - mc)

        def slot(ref, px, py, pc):
            return ref.at[4 * px + 2 * py + pc]

        def copy(k, block, to, src_ref=out_ref):
            return pltpu.make_async_remote_copy(
                src_ref=slot(src_ref, *block), dst_ref=slot(out_ref, *block),
                send_sem=send_sems.at[k], recv_sem=recv_sems.at[k], device_id=to, device_id_type=_MESH)

        first = [copy(0, me, sibling, in_ref)]
        first += [copy(1 + j, me, (*chip, mc), in_ref) for j, chip in enumerate(chips)]
        for cp in first:
            cp.start()
        passed = [copy(4 + j, (*chip, mc), sibling) for j, chip in enumerate(chips)]
        for j, chip in enumerate(chips):
            copy(1 + j, (*chip, mc), me).wait_recv()
            passed[j].start()
        copy(0, sibling, me).wait_recv()
        for j, chip in enumerate(chips):
            copy(4 + j, (*chip, 1 - mc), me).wait_recv()
        for cp in first + passed:
            cp.wait_send()

    return pl.pallas_call(
        body, name=name, out_shape=jax.ShapeDtypeStruct(buf.shape, buf.dtype),
        in_specs=[_ANY], out_specs=_ANY, input_output_aliases={0: 0},
        scratch_shapes=[pltpu.SemaphoreType.DMA((7,)), pltpu.SemaphoreType.DMA((7,))],
    )(buf)


def _own_slot(x, dev):
    return lax.dynamic_update_slice(jnp.zeros((8,) + x.shape, x.dtype), x[None], (dev, 0, 0))


def _cast_into_slot(w, sel, after, name):
    r, c = w.shape
    hr = r // 2
    tr = _tile(hr, 256, 16)
    nr = hr // tr

    def body(sel_ref, w_ref, after_ref, o_ref):
        o_ref[...] = w_ref[...].astype(o_ref.dtype)

    return pl.pallas_call(
        body, name=name, out_shape=jax.ShapeDtypeStruct((4, 2, hr, c), _WIRE),
        grid_spec=pltpu.PrefetchScalarGridSpec(
            num_scalar_prefetch=1, grid=(2, nr),
            in_specs=[pl.BlockSpec((tr, c), lambda h, i, s: (h * nr + i, 0)), _ANY],
            out_specs=pl.BlockSpec((None, None, tr, c), lambda h, i, s: (s[0], h, i, 0))),
        compiler_params=_cp("parallel", "parallel"),
    )(sel, w, after)


def _hbm(a):
    return pltpu.with_memory_space_constraint(a, pltpu.HBM)


def _after(vec, token):
    return vec + token[0:1, 0:1]


def _gather_start(bufs, after, name):
    n = len(bufs)
    nc = 3 * n

    def body(*refs):
        ins, send, recv, token = refs[:n], refs[n + 1:n + 1 + nc], refs[n + 1 + nc:n + 1 + 2 * nc], refs[-1]
        mx, my, mc, chips = _place()
        j_me = 2 * mx + my
        for i in range(n):
            for k, chip in enumerate(chips):
                half = ins[i].at[j_me, mc]
                pltpu.make_async_remote_copy(
                    src_ref=half, dst_ref=half, send_sem=send[3 * i + k], recv_sem=recv[3 * i + k],
                    device_id=(*chip, mc), device_id_type=_MESH).start()
        token[...] = jnp.zeros_like(token)

    outs = pl.pallas_call(
        body, name=name,
        out_shape=tuple([pltpu.SemaphoreType.DMA(())] * (2 * nc) + [pltpu.HBM(b.shape, b.dtype) for b in bufs]
                        + [jax.ShapeDtypeStruct((SUBLANES, LANES), _F32)]),
        in_specs=tuple([_HBM] * n + [_ANY]), out_specs=tuple([_SEM] * (2 * nc) + [_HBM] * n + [_VMEM_WHOLE]),
        input_output_aliases={i: 2 * nc + i for i in range(n)},
        compiler_params=pltpu.CompilerParams(has_side_effects=_EFFECT),
    )(*[_hbm(b) for b in bufs], after)
    sems = [(outs[3 * i:3 * i + 3], outs[nc + 3 * i:nc + 3 * i + 3]) for i in range(n)]
    return sems, list(outs[2 * nc:2 * nc + n]), outs[-1]


def _gather_wait(sems, buf, after, name):
    send, recv = sems

    def body(buf_ref, s0, s1, s2, r0, r1, r2, after_ref, out_ref):
        mx, my, mc, chips = _place()
        j_me = 2 * mx + my
        for k, (chip, s_k, r_k) in enumerate(zip(chips, (s0, s1, s2), (r0, r1, r2))):
            cp = pltpu.make_async_remote_copy(
                src_ref=buf_ref.at[j_me, mc], dst_ref=buf_ref.at[2 * chip[0] + chip[1], mc], send_sem=s_k, recv_sem=r_k,
                device_id=(*chip, mc), device_id_type=_MESH)
            cp.wait_send()
            cp.wait_recv()

    return pl.pallas_call(
        body, name=name, out_shape=pltpu.HBM(buf.shape, buf.dtype),
        in_specs=(_HBM,) + (_SEM,) * 6 + (_ANY,), out_specs=_HBM, input_output_aliases={0: 0},
        compiler_params=pltpu.CompilerParams(has_side_effects=_EFFECT),
    )(buf, *send, *recv, after)


def _pair_forward(bufs, name):
    n = len(bufs)

    def body(*refs):
        ins, outs = refs[:n], refs[n:2 * n]
        send_sems, recv_sems = refs[2 * n:]
        mx, my, mc, chips = _place()
        sibling = (mx, my, 1 - mc)
        cps = []
        for i in range(n):
            for k, chip in enumerate(chips):
                j_k = 2 * chip[0] + chip[1]
                cp = pltpu.make_async_remote_copy(
                    src_ref=ins[i].at[j_k, mc], dst_ref=outs[i].at[j_k, mc], send_sem=send_sems.at[3 * i + k],
                    recv_sem=recv_sems.at[3 * i + k], device_id=sibling, device_id_type=_MESH)
                cp.start()
                cps.append(cp)
        for i in range(n):
            for k, chip in enumerate(chips):
                other = outs[i].at[2 * chip[0] + chip[1], 1 - mc]
                pltpu.make_async_remote_copy(
                    src_ref=other, dst_ref=other, send_sem=send_sems.at[3 * i + k], recv_sem=recv_sems.at[3 * i + k],
                    device_id=sibling, device_id_type=_MESH).wait_recv()
        for cp in cps:
            cp.wait_send()

    return pl.pallas_call(
        body, name=name, out_shape=[jax.ShapeDtypeStruct(b.shape, b.dtype) for b in bufs],
        in_specs=[_ANY] * n, out_specs=[_ANY] * n, input_output_aliases={i: i for i in range(n)},
        scratch_shapes=[pltpu.SemaphoreType.DMA((3 * n,)), pltpu.SemaphoreType.DMA((3 * n,))],
    )(*bufs)


def _gather8_peers(buf_ref, mx, my, mc, chips):
    mine = buf_ref.at[4 * mx + 2 * my + mc]
    peers = [((mx, my, 1 - mc), mine, buf_ref.at[4 * mx + 2 * my + 1 - mc])]
    peers += [((*chip, mc), mine, buf_ref.at[4 * chip[0] + 2 * chip[1] + mc]) for chip in chips]
    return peers


def _gather8_start(buf, name):
    def body(buf_ref, *rest):
        send, recv, token = rest[0:4], rest[4:8], rest[-1]
        mx, my, mc, chips = _place()
        for k, (peer, src, _) in enumerate(_gather8_peers(buf_ref, mx, my, mc, chips)):
            pltpu.make_async_remote_copy(src_ref=src, dst_ref=src, send_sem=send[k], recv_sem=recv[k],
                                         device_id=peer, device_id_type=_MESH).start()
        token[...] = jnp.zeros_like(token)

    outs = pl.pallas_call(
        body, name=name,
        out_shape=tuple([pltpu.SemaphoreType.DMA(())] * 8 + [pltpu.HBM(buf.shape, buf.dtype),
                                                             jax.ShapeDtypeStruct((SUBLANES, LANES), _F32)]),
        in_specs=(_HBM,), out_specs=tuple([_SEM] * 8 + [_HBM, _VMEM_WHOLE]), input_output_aliases={0: 8},
        compiler_params=pltpu.CompilerParams(has_side_effects=_EFFECT),
    )(_hbm(buf))
    return (outs[0:4], outs[4:8]), outs[8], outs[9]


def _gather8_wait(sems, buf, after, name):
    send, recv = sems

    def body(buf_ref, s0, s1, s2, s3, r0, r1, r2, r3, after_ref, out_ref):
        mx, my, mc, chips = _place()
        for (peer, src, dst), s_k, r_k in zip(_gather8_peers(buf_ref, mx, my, mc, chips), (s0, s1, s2, s3), (r0, r1, r2, r3)):
            cp = pltpu.make_async_remote_copy(src_ref=src, dst_ref=dst, send_sem=s_k, recv_sem=r_k,
                                              device_id=peer, device_id_type=_MESH)
            cp.wait_send()
            cp.wait_recv()

    return pl.pallas_call(
        body, name=name, out_shape=pltpu.HBM(buf.shape, buf.dtype),
        in_specs=(_HBM,) + (_SEM,) * 8 + (_ANY,), out_specs=_HBM, input_output_aliases={0: 0},
        compiler_params=pltpu.CompilerParams(has_side_effects=_EFFECT),
    )(buf, *send, *recv, after)


def _gather8_forward(buf, name):
    def body(in_ref, out_ref, send_sems, recv_sems):
        mx, my, mc, chips = _place()
        sibling = (mx, my, 1 - mc)
        cps = []
        for k, chip in enumerate(chips):
            idx = 4 * chip[0] + 2 * chip[1] + mc
            cp = pltpu.make_async_remote_copy(src_ref=in_ref.at[idx], dst_ref=out_ref.at[idx], send_sem=send_sems.at[k],
                                              recv_sem=recv_sems.at[k], device_id=sibling, device_id_type=_MESH)
            cp.start()
            cps.append(cp)
        for k, chip in enumerate(chips):
            other = out_ref.at[4 * chip[0] + 2 * chip[1] + 1 - mc]
            pltpu.make_async_remote_copy(src_ref=other, dst_ref=other, send_sem=send_sems.at[k], recv_sem=recv_sems.at[k],
                                         device_id=sibling, device_id_type=_MESH).wait_recv()
        for cp in cps:
            cp.wait_send()

    return pl.pallas_call(
        body, name=name, out_shape=jax.ShapeDtypeStruct(buf.shape, buf.dtype),
        in_specs=[_ANY], out_specs=_ANY, input_output_aliases={0: 0},
        scratch_shapes=[pltpu.SemaphoreType.DMA((3,)), pltpu.SemaphoreType.DMA((3,))],
    )(buf)


def _scatter_start(pair, name):
    land = lax.empty((3,) + pair.shape[1:], pair.dtype)

    def body(pair_ref, land_ref, s0, s1, s2, r0, r1, r2, pair_thru, land_thru, token):
        mx, my, mc, chips = _place()
        for k, (chip, s_k, r_k) in enumerate(zip(chips, (s0, s1, s2), (r0, r1, r2))):
            pltpu.make_async_remote_copy(
                src_ref=pair_ref.at[2 * chip[0] + chip[1]], dst_ref=land_ref.at[k], send_sem=s_k, recv_sem=r_k,
                device_id=(*chip, mc), device_id_type=_MESH).start()
        token[...] = jnp.zeros_like(token)

    outs = pl.pallas_call(
        body, name=name,
        out_shape=tuple([pltpu.SemaphoreType.DMA(())] * 6 + [pltpu.HBM(pair.shape, pair.dtype), pltpu.HBM(land.shape, land.dtype),
                                                             jax.ShapeDtypeStruct((SUBLANES, LANES), _F32)]),
        in_specs=(_HBM, _HBM), out_specs=tuple([_SEM] * 6 + [_HBM, _HBM, _VMEM_WHOLE]),
        input_output_aliases={0: 6, 1: 7}, compiler_params=pltpu.CompilerParams(has_side_effects=_EFFECT),
    )(_hbm(pair), _hbm(land))
    return (outs[0:3], outs[3:6]), outs[6], outs[7], outs[8]


def _scatter_wait(sems, pair, land, after, name):
    send, recv = sems

    def body(pair_ref, land_ref, s0, s1, s2, r0, r1, r2, after_ref, pair_out, land_out):
        mx, my, mc, chips = _place()
        for k, (chip, s_k, r_k) in enumerate(zip(chips, (s0, s1, s2), (r0, r1, r2))):
            cp = pltpu.make_async_remote_copy(
                src_ref=pair_ref.at[2 * chip[0] + chip[1]], dst_ref=land_ref.at[k], send_sem=s_k, recv_sem=r_k,
                device_id=(*chip, mc), device_id_type=_MESH)
            cp.wait_send()
            cp.wait_recv()

    return pl.pallas_call(
        body, name=name, out_shape=(pltpu.HBM(pair.shape, pair.dtype), pltpu.HBM(land.shape, land.dtype)),
        in_specs=(_HBM, _HBM) + (_SEM,) * 6 + (_ANY,), out_specs=(_HBM, _HBM), input_output_aliases={0: 0, 1: 1},
        compiler_params=pltpu.CompilerParams(has_side_effects=_EFFECT),
    )(pair, land, *send, *recv, after)


def _sibling_copy(src_ref, dst_ref, send_sem, recv_sem):
    mx, my, mc, _ = _place()
    return pltpu.make_async_remote_copy(src_ref=src_ref, dst_ref=dst_ref, send_sem=send_sem, recv_sem=recv_sem,
                                        device_id=(mx, my, 1 - mc), device_id_type=_MESH)


def _swap_start(g, name):
    land = lax.empty(g.shape[1:], g.dtype)

    def body(g_ref, land_ref, send_sem, recv_sem, g_thru, land_thru, token):
        _sibling_copy(g_ref.at[1 - lax.axis_index("c")], land_ref, send_sem, recv_sem).start()
        token[...] = jnp.zeros_like(token)

    outs = pl.pallas_call(
        body, name=name,
        out_shape=(pltpu.SemaphoreType.DMA(()), pltpu.SemaphoreType.DMA(()), pltpu.HBM(g.shape, g.dtype),
                   pltpu.HBM(land.shape, land.dtype), jax.ShapeDtypeStruct((SUBLANES, LANES), _F32)),
        in_specs=(_HBM, _HBM), out_specs=(_SEM, _SEM, _HBM, _HBM, _VMEM_WHOLE), input_output_aliases={0: 2, 1: 3},
        compiler_params=pltpu.CompilerParams(has_side_effects=_EFFECT),
    )(_hbm(g), _hbm(land))
    return (outs[0], outs[1]), outs[2], outs[3], outs[4]


def _swap_wait(sems, g, land, after, name):
    def body(g_ref, land_ref, send_sem, recv_sem, after_ref, g_out, land_out):
        cp = _sibling_copy(g_ref.at[1 - lax.axis_index("c")], land_ref, send_sem, recv_sem)
        cp.wait_send()
        cp.wait_recv()

    return pl.pallas_call(
        body, name=name, out_shape=(pltpu.HBM(g.shape, g.dtype), pltpu.HBM(land.shape, land.dtype)),
        in_specs=(_HBM, _HBM, _SEM, _SEM, _ANY), out_specs=(_HBM, _HBM), input_output_aliases={0: 0, 1: 1},
        compiler_params=pltpu.CompilerParams(has_side_effects=_EFFECT),
    )(g, land, *sems, after)


def _join_start(buf, name):
    def body(buf_ref, send_sem, recv_sem, buf_thru, token):
        mine = buf_ref.at[lax.axis_index("c")]
        _sibling_copy(mine, mine, send_sem, recv_sem).start()
        token[...] = jnp.zeros_like(token)

    outs = pl.pallas_call(
        body, name=name,
        out_shape=(pltpu.SemaphoreType.DMA(()), pltpu.SemaphoreType.DMA(()), pltpu.HBM(buf.shape, buf.dtype),
                   jax.ShapeDtypeStruct((SUBLANES, LANES), _F32)),
        in_specs=(_HBM,), out_specs=(_SEM, _SEM, _HBM, _VMEM_WHOLE), input_output_aliases={0: 2},
        compiler_params=pltpu.CompilerParams(has_side_effects=_EFFECT),
    )(_hbm(buf))
    return (outs[0], outs[1]), outs[2], outs[3]


def _join_wait(sems, buf, after, name):
    def body(buf_ref, send_sem, recv_sem, after_ref, buf_out):
        mc = lax.axis_index("c")
        cp = _sibling_copy(buf_ref.at[mc], buf_ref.at[1 - mc], send_sem, recv_sem)
        cp.wait_send()
        cp.wait_recv()

    return pl.pallas_call(
        body, name=name, out_shape=pltpu.HBM(buf.shape, buf.dtype),
        in_specs=(_HBM, _SEM, _SEM, _ANY), out_specs=_HBM, input_output_aliases={0: 0},
        compiler_params=pltpu.CompilerParams(has_side_effects=_EFFECT),
    )(buf, *sems, after)


def _pair_sum(g, got, sel, name):
    _, four, hr, c = g.shape
    tr = _tile(hr, 512, 16)

    def body(sel_ref, g_ref, p_ref, o_ref):
        o_ref[...] = (g_ref[...].astype(_F32) + p_ref[...].astype(_F32)).astype(o_ref.dtype)

    return pl.pallas_call(
        body, name=name, out_shape=jax.ShapeDtypeStruct((four, hr, c), g.dtype),
        grid_spec=pltpu.PrefetchScalarGridSpec(
            num_scalar_prefetch=1, grid=(four, hr // tr),
            in_specs=[pl.BlockSpec((None, None, tr, c), lambda j, i, s: (s[1], j, i, 0)),
                      pl.BlockSpec((None, tr, c), lambda j, i, s: (j, i, 0))],
            out_specs=pl.BlockSpec((None, tr, c), lambda j, i, s: (j, i, 0))),
        compiler_params=_cp("parallel", "parallel"),
    )(sel, g, got)


def _chip_sum(pair, got, sel, name):
    _, hr, c = pair.shape
    tr = _tile(hr, 512, 16)

    def body(sel_ref, p_ref, q_ref, o_ref):
        o_ref[...] = ((p_ref[...].astype(_F32) + q_ref[0].astype(_F32)) + q_ref[1].astype(_F32)) + q_ref[2].astype(_F32)

    return pl.pallas_call(
        body, name=name, out_shape=jax.ShapeDtypeStruct((2, hr, c), _F32),
        grid_spec=pltpu.PrefetchScalarGridSpec(
            num_scalar_prefetch=1, grid=(hr // tr,),
            in_specs=[pl.BlockSpec((None, tr, c), lambda i, s: (s[0], i, 0)),
                      pl.BlockSpec((3, tr, c), lambda i, s: (0, i, 0))],
            out_specs=pl.BlockSpec((None, tr, c), lambda i, s: (s[1], i, 0))),
        compiler_params=_cp("parallel"),
    )(sel, pair, got)


def _matmul(a, b, dims, out_struct, grid, a_spec, b_spec, o_spec, acc_shape, k_axis, name, after=None):
    nk = grid[k_axis]
    extra = [] if after is None else [after]

    def body(a_ref, b_ref, *rest):
        o_ref, acc = rest[len(extra)], rest[len(extra) + 1:]
        prod = lax.dot_general(a_ref[...].astype(_MXU), b_ref[...].astype(_MXU), dims, preferred_element_type=_F32)
        if nk == 1:
            o_ref[...] = prod.astype(o_ref.dtype)
        else:
            acc_ref, = acc
            k = pl.program_id(k_axis)
            _zero_first(k == 0, acc_ref)
            acc_ref[...] += prod

            @pl.when(k == nk - 1)
            def _():
                o_ref[...] = acc_ref[...].astype(o_ref.dtype)

    sem = ["parallel"] * len(grid)
    sem[k_axis] = "arbitrary"
    return pl.pallas_call(
        body, name=name, out_shape=out_struct, grid=grid, in_specs=[a_spec, b_spec] + [_ANY] * len(extra), out_specs=o_spec,
        scratch_shapes=[pltpu.VMEM(acc_shape, _F32)] if nk > 1 else [], compiler_params=_cp(*sem),
    )(a, b, *extra)


def _mm_nn(a, w4, out_dtype, name, tm=512, tn=1536, tk=2048, after=None):
    m, k = a.shape
    j, _, ns = w4.shape
    tm, tn, tk = _tile(m, tm, 16), _tile(ns, tn), _tile(k, tk)
    nps = ns // tn
    return _matmul(
        a, w4, (((1,), (0,)), ((), ())), jax.ShapeDtypeStruct((m, j * ns), out_dtype),
        (j * nps, m // tm, k // tk),
        pl.BlockSpec((tm, tk), lambda ni, mi, ki: (mi, ki)),
        pl.BlockSpec((None, tk, tn), lambda ni, mi, ki: (ni // nps, ki, ni % nps)),
        pl.BlockSpec((tm, tn), lambda ni, mi, ki: (mi, ni)), (tm, tn), 2, name, after)


def _mm_nt(a, w4, out_dtype, name, tm=512, tn=2048, tk=1536, after=None):
    m = a.shape[-2]
    j, kw, ns = w4.shape
    tm, tn, tk = _tile(m, tm, 16), _tile(kw, tn), _tile(ns, tk)
    kps = ns // tk
    if a.ndim == 3:
        kph = a.shape[2] // tk
        a_spec = pl.BlockSpec((None, tm, tk), lambda ni, mi, ki: (ki // kph, mi, ki % kph))
    else:
        a_spec = pl.BlockSpec((tm, tk), lambda ni, mi, ki: (mi, ki))
    return _matmul(
        a, w4, (((1,), (1,)), ((), ())), jax.ShapeDtypeStruct((m, kw), out_dtype),
        (kw // tn, m // tm, j * kps),
        a_spec,
        pl.BlockSpec((None, tn, tk), lambda ni, mi, ki: (ki // kps, ni, ki % kps)),
        pl.BlockSpec((tm, tn), lambda ni, mi, ki: (mi, ni)), (tm, tn), 2, name, after)


def _mm_tn_cols(a, b, name, tm=1024, tn=1536, tk=2048):
    m, ka = a.shape
    ns = (b.shape[-1] * (2 if b.ndim == 3 else 1)) // 4
    hr = ka // 2
    tm, tn, tk = _tile(hr, tm), _tile(ns, tn), _tile(m, tk, 16)
    mph, nps = hr // tm, ns // tn
    if b.ndim == 3:
        b_spec = pl.BlockSpec((None, tk, tn), lambda ni, mi, ki: (ni // (2 * nps), ki, ni % (2 * nps)))
    else:
        b_spec = pl.BlockSpec((tk, tn), lambda ni, mi, ki: (ki, ni))
    return _matmul(
        a, b, (((0,), (0,)), ((), ())), jax.ShapeDtypeStruct((2, 4, hr, ns), _WIRE),
        (4 * nps, 2 * mph, m // tk),
        pl.BlockSpec((tk, tm), lambda ni, mi, ki: (ki, mi)),
        b_spec,
        pl.BlockSpec((None, None, tm, tn), lambda ni, mi, ki: (mi // mph, ni // nps, mi % mph, ni % nps)),
        (tm, tn), 2, name)


def _mm_tn_rows(a, b, name, tm=1536, tn=1024, tk=2048):
    m, ka = a.shape
    r = ka // 4
    hc = b.shape[1] // 2
    tm, tn, tk = _tile(r, tm), _tile(hc, tn), _tile(m, tk, 16)
    mpr, nph = r // tm, hc // tn
    return _matmul(
        a, b, (((0,), (0,)), ((), ())), jax.ShapeDtypeStruct((2, 4, r, hc), _WIRE),
        (2 * nph, 4 * mpr, m // tk),
        pl.BlockSpec((tk, tm), lambda ni, mi, ki: (ki, mi)),
        pl.BlockSpec((tk, tn), lambda ni, mi, ki: (ki, ni)),
        pl.BlockSpec((None, None, tm, tn), lambda ni, mi, ki: (ni // nph, mi // mpr, mi % mpr, ni % nph)),
        (tm, tn), 2, name)


def _row_call(body, name, rows, ins, outs, tm=256):
    tm = _tile(rows, tm, 16)

    def spec(shape, kind):
        if kind == "rows":
            return pl.BlockSpec((tm, shape[1]), lambda i: (i, 0))
        return pl.BlockSpec(shape, lambda i: (0,) * len(shape))

    return pl.pallas_call(
        body, name=name, grid=(rows // tm,),
        in_specs=[spec(a.shape, kind) for a, kind in ins],
        out_specs=[spec(o.shape, kind) for o, kind in outs],
        out_shape=[o for o, _ in outs],
        compiler_params=_cp("arbitrary"),
    )(*[a for a, _ in ins])


def _rms(x):
    r = lax.rsqrt(_rowmean(x * x) + EPS)
    return x * r, r


def _rms_bwd(dxh, xh, r):
    return r * (dxh - xh * _rowmean(dxh * xh))


def _fwd_pre_mix(x, g, sc, sh):
    s, d = x.shape

    def body(x_ref, g_ref, sc_ref, sh_ref, h_ref):
        xh, _ = _rms(x_ref[...])
        h_ref[...] = (xh * g_ref[...] * (1.0 + sc_ref[...]) + sh_ref[...]).astype(h_ref.dtype)

    return _row_call(body, "fwd_pre_mix", s, [(x, "rows"), (g, "vec"), (sc, "vec"), (sh, "vec")],
                     [(jax.ShapeDtypeStruct((s, d), _MXU), "rows")])[0]


def _fwd_mid(o, x, gt1, g_post, g_pre2, sc2, sh2):
    s, d = x.shape

    def body(o_ref, x_ref, gt_ref, gp_ref, g2_ref, sc_ref, sh_ref, x1_ref, h2_ref):
        oh, _ = _rms(o_ref[...])
        x1 = x_ref[...] + gt_ref[...] * (oh * gp_ref[...])
        x1_ref[...] = x1
        xh, _ = _rms(x1)
        h2_ref[...] = (xh * g2_ref[...] * (1.0 + sc_ref[...]) + sh_ref[...]).astype(h2_ref.dtype)

    return _row_call(body, "fwd_mid", s,
                     [(o, "rows"), (x, "rows"), (gt1, "vec"), (g_post, "vec"), (g_pre2, "vec"), (sc2, "vec"),
                      (sh2, "vec")],
                     [(jax.ShapeDtypeStruct((s, d), _F32), "rows"), (jax.ShapeDtypeStruct((s, d), _MXU), "rows")])


def _loss_and_post_ffn_bwd(f, x1, tgt, gt2, g_post):
    s, d = x1.shape

    def body(f_ref, x1_ref, t_ref, gt_ref, g_ref, dx2_ref, df_ref, dgt_ref, dg_ref, loss_ref):
        first = pl.program_id(0) == 0
        _zero_first(first, dgt_ref, dg_ref, loss_ref)
        fh, r = _rms(f_ref[...])
        n = fh * g_ref[...]
        e = x1_ref[...] + gt_ref[...] * n - t_ref[...]
        _acc(loss_ref, first, jnp.sum(_colsum(e * e), axis=1, keepdims=True) * (0.5 / d))
        dx2 = e * (1.0 / d)
        dx2_ref[...] = dx2
        _acc(dgt_ref, first, _colsum(dx2 * n))
        dn = dx2 * gt_ref[...]
        _acc(dg_ref, first, _colsum(dn * fh))
        df_ref[...] = _rms_bwd(dn * g_ref[...], fh, r).astype(df_ref.dtype)

    vec = jax.ShapeDtypeStruct((1, d), _F32)
    return _row_call(body, "loss_post_ffn_bwd", s,
                     [(f, "rows"), (x1, "rows"), (tgt, "rows"), (gt2, "vec"), (g_post, "vec")],
                     [(jax.ShapeDtypeStruct((s, d), _F32), "rows"), (jax.ShapeDtypeStruct((s, d), _MXU), "rows"),
                      (vec, "vec"), (vec, "vec"), (jax.ShapeDtypeStruct((1, 1), _F32), "vec")])


def _bwd_mid(dh2, x1, dx2, o, g_pre2, sc2, gt1, g_post):
    s, d = x1.shape

    def body(dh_ref, x1_ref, dx2_ref, o_ref, g2_ref, sc_ref, gt_ref, gp_ref,
             dx1_ref, do_ref, dsc_ref, dsh_ref, dg2_ref, dgt_ref, dgp_ref):
        first = pl.program_id(0) == 0
        _zero_first(first, dsc_ref, dsh_ref, dg2_ref, dgt_ref, dgp_ref)
        dh = dh_ref[...]
        xh, r = _rms(x1_ref[...])
        _acc(dsh_ref, first, _colsum(dh))
        _acc(dsc_ref, first, _colsum(dh * (xh * g2_ref[...])))
        dn = dh * (1.0 + sc_ref[...])
        _acc(dg2_ref, first, _colsum(dn * xh))
        dx1 = dx2_ref[...] + _rms_bwd(dn * g2_ref[...], xh, r)
        dx1_ref[...] = dx1
        oh, ro = _rms(o_ref[...])
        _acc(dgt_ref, first, _colsum(dx1 * (oh * gp_ref[...])))
        dno = dx1 * gt_ref[...]
        _acc(dgp_ref, first, _colsum(dno * oh))
        do_ref[...] = _rms_bwd(dno * gp_ref[...], oh, ro).astype(do_ref.dtype)

    vec = jax.ShapeDtypeStruct((1, d), _F32)
    return _row_call(body, "bwd_mid", s,
                     [(dh2, "rows"), (x1, "rows"), (dx2, "rows"), (o, "rows"), (g_pre2, "vec"), (sc2, "vec"),
                      (gt1, "vec"), (g_post, "vec")],
                     [(jax.ShapeDtypeStruct((s, d), _F32), "rows"), (jax.ShapeDtypeStruct((s, d), _MXU), "rows"),
                      (vec, "vec"), (vec, "vec"), (vec, "vec"), (vec, "vec"), (vec, "vec")])


def _bwd_pre_mix(dh1, x, dx1, g, sc1):
    s, d = x.shape

    def body(dh_ref, x_ref, dx1_ref, g_ref, sc_ref, dx_ref, dsc_ref, dsh_ref, dg_ref):
        first = pl.program_id(0) == 0
        _zero_first(first, dsc_ref, dsh_ref, dg_ref)
        dh = dh_ref[...]
        xh, r = _rms(x_ref[...])
        _acc(dsh_ref, first, _colsum(dh))
        _acc(dsc_ref, first, _colsum(dh * (xh * g_ref[...])))
        dn = dh * (1.0 + sc_ref[...])
        _acc(dg_ref, first, _colsum(dn * xh))
        dx_ref[...] = dx1_ref[...] + _rms_bwd(dn * g_ref[...], xh, r)

    vec = jax.ShapeDtypeStruct((1, d), _F32)
    return _row_call(body, "bwd_pre_mix", s,
                     [(dh1, "rows"), (x, "rows"), (dx1, "rows"), (g, "vec"), (sc1, "vec")],
                     [(jax.ShapeDtypeStruct((s, d), _F32), "rows"), (vec, "vec"), (vec, "vec"), (vec, "vec")])


def _mix_norm_fwd(y_ssm, y_sgu, g_ssm, g_sgu):
    s, h = y_ssm.shape

    def body(a_ref, b_ref, ga_ref, gb_ref, o_ref):
        ah, _ = _rms(a_ref[...])
        bh, _ = _rms(b_ref[...])
        o_ref[:, 0:h] = (ah * ga_ref[...]).astype(o_ref.dtype)
        o_ref[:, h:2 * h] = (bh * gb_ref[...]).astype(o_ref.dtype)

    return _row_call(body, "mix_norm_fwd", s, [(y_ssm, "rows"), (y_sgu, "rows"), (g_ssm, "vec"), (g_sgu, "vec")],
                     [(jax.ShapeDtypeStruct((s, 2 * h), _MXU), "rows")])[0]


def _mix_norm_bwd(dyc, y_ssm, y_sgu, g_ssm, g_sgu):
    s, h = y_ssm.shape

    def body(d_ref, a_ref, b_ref, ga_ref, gb_ref, da_ref, db_ref, dga_ref, dgb_ref):
        first = pl.program_id(0) == 0
        _zero_first(first, dga_ref, dgb_ref)
        for lo, y_ref, g_ref, dy_ref, dg_ref in ((0, a_ref, ga_ref, da_ref, dga_ref), (h, b_ref, gb_ref, db_ref, dgb_ref)):
            d = d_ref[:, lo:lo + h]
            yh, r = _rms(y_ref[...])
            _acc(dg_ref, first, _colsum(d * yh))
            dy_ref[...] = _rms_bwd(d * g_ref[...], yh, r)

    vec = jax.ShapeDtypeStruct((1, h), _F32)
    full = jax.ShapeDtypeStruct((s, h), _F32)
    return _row_call(body, "mix_norm_bwd", s,
                     [(dyc, "rows"), (y_ssm, "rows"), (y_sgu, "rows"), (g_ssm, "vec"), (g_sgu, "vec")],
                     [(full, "rows"), (full, "rows"), (vec, "vec"), (vec, "vec")])


CONV_ROWS = 64


def _conv_rows(ext, w_ref, b_ref):
    x = ext[SUBLANES:]
    s1 = pltpu.roll(ext, 1, 0)[SUBLANES:]
    s2 = pltpu.roll(ext, 2, 0)[SUBLANES:]
    return b_ref[...] + w_ref[0:1, :] * s2 + w_ref[1:2, :] * s1 + w_ref[2:3, :] * x, x, s1, s2


def _conv_window(x_ref, r0):
    if isinstance(r0, int):
        assert r0 == 0
        return jnp.concatenate([jnp.zeros((SUBLANES, x_ref.shape[1]), _F32), x_ref[0:CONV_ROWS, :]], axis=0)
    return x_ref[pl.ds(pl.multiple_of(r0 - SUBLANES, SUBLANES), CONV_ROWS + SUBLANES), :]


def _conv_act_fwd(up_pre, conv_w, conv_b):
    s, f2 = up_pre.shape
    f = f2 // 2
    tc = _tile(f, 256)
    nf = f // tc

    def shift_down(x, k):
        row = lax.broadcasted_iota(jnp.int32, x.shape, 0)
        return jnp.where(row >= k, pltpu.roll(x, k, 0), 0.0)

    def conv(x, w_ref, b_ref):
        return b_ref[...] + w_ref[0:1, :] * shift_down(x, 2) + w_ref[1:2, :] * shift_down(x, 1) + w_ref[2:3, :] * x

    def body(a_ref, b_ref, wa_ref, wb_ref, ba_ref, bb_ref, o_ref):
        a = conv(a_ref[...], wa_ref, ba_ref)
        b = conv(b_ref[...], wb_ref, bb_ref)
        o_ref[...] = (a * _sigmoid(a) * b).astype(o_ref.dtype)

    return pl.pallas_call(
        body, name="conv_act_fwd", grid=(nf,), out_shape=jax.ShapeDtypeStruct((s, f), _MXU),
        in_specs=[pl.BlockSpec((s, tc), lambda n: (0, n)), pl.BlockSpec((s, tc), lambda n: (0, n + nf)),
                  pl.BlockSpec((3, tc), lambda n: (0, n)), pl.BlockSpec((3, tc), lambda n: (0, n + nf)),
                  pl.BlockSpec((1, tc), lambda n: (0, n)), pl.BlockSpec((1, tc), lambda n: (0, n + nf))],
        out_specs=pl.BlockSpec((s, tc), lambda n: (0, n)), compiler_params=_cp("parallel"),
    )(up_pre, up_pre, conv_w, conv_w, conv_b, conv_b)


def _conv_act_bwd(up_pre, d_act, conv_w, conv_b):
    s, f2 = up_pre.shape
    f = f2 // 2
    tc = _tile(f, 256)
    nf = f // tc

    def body(a_ref, b_ref, d_ref, wa_ref, wb_ref, ba_ref, bb_ref,
             du_ref, w0a, w0b, w1a, w1b, w2a, w2b, dba, dbb):
        n = s // CONV_ROWS
        zero8 = jnp.zeros((SUBLANES, tc), _F32)
        ext_rows = CONV_ROWS + SUBLANES

        def fold(x):
            out = x[0:SUBLANES]
            for k in range(1, CONV_ROWS // SUBLANES):
                out = out + x[k * SUBLANES:(k + 1) * SUBLANES]
            return out

        def chunk(r0, carry):
            nxt, acc = carry
            a, xa, xa1, xa2 = _conv_rows(_conv_window(a_ref, r0), wa_ref, ba_ref)
            b, xb, xb1, xb2 = _conv_rows(_conv_window(b_ref, r0), wb_ref, bb_ref)
            sg = _sigmoid(a)
            d = d_ref[pl.ds(r0, CONV_ROWS), :]
            du_a = d * b * (sg * (1.0 + a * (1.0 - sg)))
            du_b = d * (a * sg)
            new_acc = []
            for h, (du, x0, x1, x2, w_ref) in enumerate(((du_a, xa, xa1, xa2, wa_ref), (du_b, xb, xb1, xb2, wb_ref))):
                ext = jnp.concatenate([du, nxt[h]], axis=0)
                u1 = pltpu.roll(ext, ext_rows - 1, 0)[:CONV_ROWS]
                u2 = pltpu.roll(ext, ext_rows - 2, 0)[:CONV_ROWS]
                du_ref[h, pl.ds(r0, CONV_ROWS), :] = (w_ref[2:3, :] * du + w_ref[1:2, :] * u1
                                                      + w_ref[0:1, :] * u2).astype(du_ref.dtype)
                new_acc += [acc[4 * h] + fold(du * x2), acc[4 * h + 1] + fold(du * x1), acc[4 * h + 2] + fold(du * x0),
                            acc[4 * h + 3] + fold(du)]
            return (du_a[:SUBLANES], du_b[:SUBLANES]), tuple(new_acc)

        def step(i, carry):
            return chunk(pl.multiple_of((n - 1 - i) * CONV_ROWS, CONV_ROWS), carry)

        carry = lax.fori_loop(0, n - 1, step, ((zero8, zero8), (zero8,) * 8))
        _, acc = chunk(0, carry)
        for ref, val in zip((w0a, w1a, w2a, dba, w0b, w1b, w2b, dbb), acc):
            ref[...] = _colsum(val)

    col_a = pl.BlockSpec((s, tc), lambda n: (0, n))
    col_b = pl.BlockSpec((s, tc), lambda n: (0, n + nf))
    vec_a = pl.BlockSpec((1, tc), lambda n: (0, n))
    vec_b = pl.BlockSpec((1, tc), lambda n: (0, n + nf))
    vec = jax.ShapeDtypeStruct((1, f), _F32)
    outs = pl.pallas_call(
        body, name="conv_act_bwd", grid=(nf,),
        in_specs=[col_a, col_b, col_a, pl.BlockSpec((3, tc), lambda n: (0, n)),
                  pl.BlockSpec((3, tc), lambda n: (0, n + nf)), vec_a, vec_b],
        out_specs=[pl.BlockSpec((2, s, tc), lambda n: (0, 0, n))] + [vec_a] * 8,
        out_shape=[jax.ShapeDtypeStruct((2, s, f), _MXU)] + [vec] * 8, compiler_params=_cp("parallel"),
    )(up_pre, up_pre, d_act, conv_w, conv_w, conv_b, conv_b)
    du, w0a, w0b, w1a, w1b, w2a, w2b, dba, dbb = outs
    cat = lambda p, q: jnp.concatenate([p, q], axis=1)
    return du, cat(w0a, w0b), cat(w1a, w1b), cat(w2a, w2b), cat(dba, dbb)


def _sgu_recompute(zu_ref, zv_ref, lng_ref, lnb_ref, wm_ref, bs_ref, nh):
    zu, zv = zu_ref[...], zv_ref[...]
    u = _gelu(zu)
    gv = _gelu(zv)
    xc = gv - _rowmean(gv)
    rs = lax.rsqrt(_rowmean(xc * xc) + EPS)
    vh = xc * rs
    v = vh * lng_ref[...] + lnb_ref[...]
    mixed = []
    for h in range(nh):
        vhd = v[:, h * CHUNK:(h + 1) * CHUNK].astype(_MXU)
        mixed.append(jnp.dot(wm_ref[h].astype(_MXU), vhd, preferred_element_type=_F32) + bs_ref[h])
    return zu, zv, u, vh, rs, v, mixed


def _sgu_fwd(z, ln_g, ln_b, wm, bs):
    s = z.shape[0]
    nh = wm.shape[0]
    hd = nh * CHUNK

    def body(zu_ref, zv_ref, lng_ref, lnb_ref, wm_ref, bs_ref, y_ref):
        _, _, u, _, _, _, mixed = _sgu_recompute(zu_ref, zv_ref, lng_ref, lnb_ref, wm_ref, bs_ref, nh)
        for h in range(nh):
            y_ref[:, h * CHUNK:(h + 1) * CHUNK] = u[:, h * CHUNK:(h + 1) * CHUNK] * mixed[h]

    vec = pl.BlockSpec((1, hd), lambda i: (0, 0))
    return pl.pallas_call(
        body, name="sgu_fwd", grid=(s // CHUNK,), out_shape=jax.ShapeDtypeStruct((s, hd), _F32),
        in_specs=[pl.BlockSpec((CHUNK, hd), lambda i: (i, 1)), pl.BlockSpec((CHUNK, hd), lambda i: (i, 2)), vec, vec,
                  pl.BlockSpec((nh, CHUNK, CHUNK), lambda i: (0, 0, 0)), pl.BlockSpec((nh, CHUNK, 1), lambda i: (0, 0, 0))],
        out_specs=pl.BlockSpec((CHUNK, hd), lambda i: (i, 0)), compiler_params=_cp("parallel"),
    )(z, z, ln_g, ln_b, wm, bs)


def _sgu_bwd(z, dy, dz_ssm, ln_g, ln_b, wm, bs):
    s = z.shape[0]
    nh = wm.shape[0]
    hd = nh * CHUNK

    def body(zu_ref, zv_ref, dy_ref, dzs_ref, lng_ref, lnb_ref, wm_ref, bs_ref,
             dz_ref, dlg_ref, dlb_ref, dwm_ref, dbs_ref, dv_scr):
        first = pl.program_id(0) == 0
        _zero_first(first, dlg_ref, dlb_ref, dwm_ref, dbs_ref)
        zu, zv, u, vh, rs, v, mixed = _sgu_recompute(zu_ref, zv_ref, lng_ref, lnb_ref, wm_ref, bs_ref, nh)
        dy = dy_ref[...]
        dz_ref[:, 0:hd] = dzs_ref[...]
        for h in range(nh):
            cols = slice(h * CHUNK, (h + 1) * CHUNK)
            dyh = dy[:, cols]
            dz_ref[:, hd + h * CHUNK:hd + (h + 1) * CHUNK] = (dyh * mixed[h] * _gelu_grad(zu[:, cols])).astype(dz_ref.dtype)
            dm = dyh * u[:, cols]
            dmx = dm.astype(_MXU)
            _acc(dbs_ref.at[h], first, jnp.sum(dm, axis=1, keepdims=True))
            _acc(dwm_ref.at[h], first,
                 lax.dot_general(dmx, v[:, cols].astype(_MXU), (((1,), (1,)), ((), ())), preferred_element_type=_F32))
            dv_scr[:, cols] = lax.dot_general(wm_ref[h].astype(_MXU), dmx, (((0,), (0,)), ((), ())),
                                              preferred_element_type=_F32)
        dv = dv_scr[...]
        _acc(dlg_ref, first, _colsum(dv * vh))
        _acc(dlb_ref, first, _colsum(dv))
        dvh = dv * lng_ref[...]
        dgv = rs * (dvh - _rowmean(dvh) - vh * _rowmean(dvh * vh))
        dz_ref[:, 2 * hd:3 * hd] = (dgv * _gelu_grad(zv)).astype(dz_ref.dtype)

    vec = pl.BlockSpec((1, hd), lambda i: (0, 0))
    wspec = pl.BlockSpec((nh, CHUNK, CHUNK), lambda i: (0, 0, 0))
    bspec = pl.BlockSpec((nh, CHUNK, 1), lambda i: (0, 0, 0))
    rows = pl.BlockSpec((CHUNK, hd), lambda i: (i, 0))
    return pl.pallas_call(
        body, name="sgu_bwd", grid=(s // CHUNK,),
        out_shape=[jax.ShapeDtypeStruct((s, 3 * hd), _MXU), jax.ShapeDtypeStruct((1, hd), _F32),
                   jax.ShapeDtypeStruct((1, hd), _F32), jax.ShapeDtypeStruct((nh, CHUNK, CHUNK), _F32),
                   jax.ShapeDtypeStruct((nh, CHUNK, 1), _F32)],
        in_specs=[pl.BlockSpec((CHUNK, hd), lambda i: (i, 1)), pl.BlockSpec((CHUNK, hd), lambda i: (i, 2)),
                  rows, rows, vec, vec, wspec, bspec],
        out_specs=[pl.BlockSpec((CHUNK, 3 * hd), lambda i: (i, 0)), vec, vec, wspec, bspec],
        scratch_shapes=[pltpu.VMEM((CHUNK, hd), _F32)], compiler_params=_cp("arbitrary"),
    )(z, z, dy, dz_ssm, ln_g, ln_b, wm, bs)


def _ssm_prep(log_dt, a_re, a_im, b_re_t, b_im_t):
    gn = a_re.shape[1]

    def body(ldt_ref, are_ref, aim_ref, br_ref, bi_ref, pr_ref, pi_ref, bbr_ref, bbi_ref):
        dt = jnp.exp(ldt_ref[...])
        are, aim = are_ref[...], aim_ref[...]
        k = (lax.broadcasted_iota(jnp.int32, (SUBLANES, gn), 0) + 1).astype(_F32)
        mag = jnp.exp(k * (are * dt))
        ang = k * (aim * dt)
        pr_ref[...] = mag * jnp.cos(ang)
        pi_ref[...] = mag * jnp.sin(ang)
        m1 = jnp.exp(are * dt)
        lr, li = m1 * jnp.cos(aim * dt), m1 * jnp.sin(aim * dt)
        den = are * are + aim * aim
        nr = lr - 1.0
        f_re = (nr * are + li * aim) / den
        f_im = (li * are - nr * aim) / den
        bbr_ref[...] = f_re * br_ref[...] - f_im * bi_ref[...]
        bbi_ref[...] = f_re * bi_ref[...] + f_im * br_ref[...]

    pw = jax.ShapeDtypeStruct((SUBLANES, gn), _F32)
    bb = jax.ShapeDtypeStruct(b_re_t.shape, _F32)
    return pl.pallas_call(body, name="ssm_prep", out_shape=[pw, pw, bb, bb])(log_dt, a_re, a_im, b_re_t, b_im_t)


def _ssm_prep_bwd(log_dt, a_re, a_im, b_re_t, b_im_t, d_bbr, d_bbi, d_lr, d_li):
    def body(ldt_ref, are_ref, aim_ref, br_ref, bi_ref, dbr_ref, dbi_ref, dlr_ref, dli_ref,
             obr_ref, obi_ref, oar_ref, oai_ref, odt_ref):
        dt = jnp.exp(ldt_ref[...])
        are, aim = are_ref[...], aim_ref[...]
        m1 = jnp.exp(are * dt)
        lr, li = m1 * jnp.cos(aim * dt), m1 * jnp.sin(aim * dt)
        den = are * are + aim * aim
        nr = lr - 1.0
        f_re = (nr * are + li * aim) / den
        f_im = (li * are - nr * aim) / den
        br, bi, dbr, dbi = br_ref[...], bi_ref[...], dbr_ref[...], dbi_ref[...]
        obr_ref[...] = f_re * dbr + f_im * dbi
        obi_ref[...] = f_re * dbi - f_im * dbr
        gf_re = _colsum(br * dbr + bi * dbi)
        gf_im = _colsum(br * dbi - bi * dbr)
        il_re, il_im = are / den, -aim / den
        glb_re = dlr_ref[...] + (il_re * gf_re + il_im * gf_im)
        glb_im = dli_ref[...] + (il_re * gf_im - il_im * gf_re)
        q_re = -(f_re * il_re - f_im * il_im)
        q_im = -(f_re * il_im + f_im * il_re)
        gl_re = q_re * gf_re + q_im * gf_im
        gl_im = q_re * gf_im - q_im * gf_re
        gl_re = gl_re + dt * (lr * glb_re + li * glb_im)
        gl_im = gl_im + dt * (lr * glb_im - li * glb_re)
        w_re = are * lr - aim * li
        w_im = are * li + aim * lr
        oar_ref[...] = gl_re
        oai_ref[...] = gl_im
        odt_ref[...] = w_re * glb_re + w_im * glb_im

    bb = jax.ShapeDtypeStruct(b_re_t.shape, _F32)
    v = jax.ShapeDtypeStruct(a_re.shape, _F32)
    return pl.pallas_call(body, name="ssm_prep_bwd", out_shape=[bb, bb, v, v, v])(
        log_dt, a_re, a_im, b_re_t, b_im_t, d_bbr, d_bbi, d_lr, d_li)


def _group_sum(d_dt, log_dt):
    def body(d_ref, l_ref, o_ref):
        o_ref[...] = jnp.sum(d_ref[...], axis=1, keepdims=True) * jnp.exp(l_ref[...])

    return pl.pallas_call(body, name="ssm_dt_grad", out_shape=jax.ShapeDtypeStruct(log_dt.shape, _F32))(d_dt, log_dt)


def _scan_rows(src_ref, dst_ref, nrt, steps, ptab, carry0, reverse):
    ns = BLOCK_ST
    row = lax.broadcasted_iota(jnp.int32, (SUBLANES, ns), 0)
    pr, pi = ptab

    def body(i, carry):
        cr, ci = carry
        it = (nrt - 1 - i) if reverse else i
        r0 = pl.multiple_of(it * SUBLANES, SUBLANES)
        xr = src_ref[pl.ds(r0, SUBLANES), 0:ns]
        xi = src_ref[pl.ds(r0, SUBLANES), ns:2 * ns]
        for k, (ar, ai) in zip((1, 2, 4), steps):
            if reverse:
                keep = row < SUBLANES - k
                sr = jnp.where(keep, pltpu.roll(xr, SUBLANES - k, 0), 0.0)
                si = jnp.where(keep, pltpu.roll(xi, SUBLANES - k, 0), 0.0)
            else:
                keep = row >= k
                sr = jnp.where(keep, pltpu.roll(xr, k, 0), 0.0)
                si = jnp.where(keep, pltpu.roll(xi, k, 0), 0.0)
            xr, xi = xr + ar * sr - ai * si, xi + ar * si + ai * sr
        xr, xi = xr + pr * cr - pi * ci, xi + pr * ci + pi * cr
        dst_ref[pl.ds(r0, SUBLANES), 0:ns] = xr
        dst_ref[pl.ds(r0, SUBLANES), ns:2 * ns] = xi
        if reverse:
            return xr[0:1, :], xi[0:1, :]
        return xr[SUBLANES - 1:SUBLANES, :], xi[SUBLANES - 1:SUBLANES, :]

    return lax.fori_loop(0, nrt, body, carry0)


def _scan_consts(p_ref, conj):
    ns = BLOCK_ST
    sign = -1.0 if conj else 1.0
    bc = lambda r: jnp.broadcast_to(r, (SUBLANES, ns))
    steps = [(bc(p_ref[k - 1:k, 0:ns]), bc(sign * p_ref[k - 1:k, ns:2 * ns])) for k in (1, 2, 4)]
    return steps


def _ssm_block_fwd(u, bbt_ref, ct_ref, d_ref, wg_ref, bg_ref, p_ref, bu_scr, h_scr, carry_in, nrt):
    ns = BLOCK_ST
    bu_scr[...] = jnp.dot(u.astype(_MXU), bbt_ref[...].astype(_MXU), preferred_element_type=_F32)
    ptab = (p_ref[:, 0:ns], p_ref[:, ns:2 * ns])
    carry = _scan_rows(bu_scr, h_scr, nrt, _scan_consts(p_ref, False), ptab, carry_in, False)
    y = jnp.dot(h_scr[...].astype(_MXU), ct_ref[...].astype(_MXU), preferred_element_type=_F32) + d_ref[...] * u
    yg = _gelu(y)
    gate = _sigmoid(jnp.dot(yg.astype(_MXU), wg_ref[...].astype(_MXU), preferred_element_type=_F32) + bg_ref[...])
    return y, yg, gate, carry


def _ssm_specs(nb, nt, t, reverse):
    tt = (lambda ti: nt - 1 - ti) if reverse else (lambda ti: ti)
    ns2 = 2 * BLOCK_ST
    return dict(
        z=pl.BlockSpec((t, BLOCK_CH), lambda b, ti: (tt(ti), b)),
        bbt=pl.BlockSpec((None, BLOCK_CH, ns2), lambda b, ti: (b, 0, 0)),
        ct=pl.BlockSpec((None, ns2, BLOCK_CH), lambda b, ti: (b, 0, 0)),
        vec=pl.BlockSpec((1, BLOCK_CH), lambda b, ti: (0, b)),
        wg=pl.BlockSpec((None, BLOCK_CH, BLOCK_CH), lambda b, ti: (b, 0, 0)),
        p=pl.BlockSpec((None, SUBLANES, ns2), lambda b, ti: (b, 0, 0)),
        hb=pl.BlockSpec((None, None, SUBLANES, ns2), lambda b, ti: (b, tt(ti), 0, 0)),
        h=pl.BlockSpec((None, t, ns2), lambda b, ti: (b, tt(ti), 0)),
        acc_vec=pl.BlockSpec((None, 1, ns2), lambda b, ti: (b, 0, 0)),
    )


def _ssm_fwd(z, bbt, ct, dvec, wg, bglu, ptab):
    s = z.shape[0]
    nb = bbt.shape[0]
    t = _tile(s, TIME_TILE, SUBLANES)
    nt = s // t
    ns = BLOCK_ST
    sp = _ssm_specs(nb, nt, t, False)

    def body(z_ref, bbt_ref, ct_ref, d_ref, wg_ref, bg_ref, p_ref, y2_ref, y_ref, h_ref, hb_ref, bu_scr, h_scr, carry_scr):
        ti = pl.program_id(1)

        @pl.when(ti == 0)
        def _():
            carry_scr[...] = jnp.zeros_like(carry_scr)

        hb_ref[...] = carry_scr[...]
        carry_in = (carry_scr[0:1, 0:ns], carry_scr[0:1, ns:2 * ns])
        y, yg, gate, (cr, ci) = _ssm_block_fwd(z_ref[...], bbt_ref, ct_ref, d_ref, wg_ref, bg_ref, p_ref,
                                               bu_scr, h_scr, carry_in, t // SUBLANES)
        y2_ref[...] = yg * gate
        y_ref[...] = y
        h_ref[...] = h_scr[...].astype(h_ref.dtype)
        carry_scr[:, 0:ns] = jnp.broadcast_to(cr, (SUBLANES, ns))
        carry_scr[:, ns:2 * ns] = jnp.broadcast_to(ci, (SUBLANES, ns))

    ych = jax.ShapeDtypeStruct((s, nb * BLOCK_CH), _F32)
    return pl.pallas_call(
        body, name="ssm_fwd", grid=(nb, nt),
        out_shape=[ych, ych, jax.ShapeDtypeStruct((nb, s, 2 * ns), _MXU),
                   jax.ShapeDtypeStruct((nb, nt, SUBLANES, 2 * ns), _F32)],
        in_specs=[sp["z"], sp["bbt"], sp["ct"], sp["vec"], sp["wg"], sp["vec"], sp["p"]],
        out_specs=[sp["z"], sp["z"], sp["h"], sp["hb"]],
        scratch_shapes=[pltpu.VMEM((t, 2 * ns), _F32), pltpu.VMEM((t, 2 * ns), _F32), pltpu.VMEM((SUBLANES, 2 * ns), _F32)],
        compiler_params=_cp("parallel", "arbitrary"),
    )(z, bbt, ct, dvec, wg, bglu, ptab)


def _ssm_bwd(z, y_pre, h_all, dy2, hb, bbt, ct, dvec, wg, bglu, ptab, ptab_rev):
    s = z.shape[0]
    nb = bbt.shape[0]
    t = _tile(s, TIME_TILE, SUBLANES)
    nt = s // t
    ns = BLOCK_ST
    sp = _ssm_specs(nb, nt, t, True)
    tn_dims = (((0,), (0,)), ((), ()))
    nt_dims = (((1,), (1,)), ((), ()))

    def body(z_ref, y_ref, h_ref, dy2_ref, hb_ref, bbt_ref, ct_ref, d_ref, wg_ref, bg_ref, p_ref, pr_ref,
             dz_ref, dbbt_ref, dct_ref, dwg_ref, dlb_ref, dd_ref, dbg_ref, bu_scr, g_scr, gcarry_scr):
        first = pl.program_id(1) == 0

        _zero_first(first, gcarry_scr, dbbt_ref, dct_ref, dwg_ref, dlb_ref, dd_ref, dbg_ref)
        u = z_ref[...]
        hin = hb_ref[...]
        y = y_ref[...]
        yg = _gelu(y)
        gate = _sigmoid(jnp.dot(yg.astype(_MXU), wg_ref[...].astype(_MXU), preferred_element_type=_F32) + bg_ref[...])
        dy2 = dy2_ref[...]
        dpre = dy2 * yg * gate * (1.0 - gate)
        _acc(dbg_ref, first, _colsum(dpre))
        dpx = dpre.astype(_MXU)
        _acc(dwg_ref, first, lax.dot_general(yg.astype(_MXU), dpx, tn_dims, preferred_element_type=_F32))
        dyg = dy2 * gate + lax.dot_general(dpx, wg_ref[...].astype(_MXU), nt_dims, preferred_element_type=_F32)
        dy = dyg * _gelu_grad(y)
        _acc(dd_ref, first, _colsum(dy * u))
        dyx = dy.astype(_MXU)
        hx = h_ref[...]
        h = hx.astype(_F32)
        _acc(dct_ref, first, lax.dot_general(hx, dyx, tn_dims, preferred_element_type=_F32))
        bu_scr[...] = lax.dot_general(dyx, ct_ref[...].astype(_MXU), nt_dims, preferred_element_type=_F32)
        gin = (gcarry_scr[0:1, 0:ns], gcarry_scr[0:1, ns:2 * ns])
        ptab = (pr_ref[:, 0:ns], pr_ref[:, ns:2 * ns])
        gr, gi = _scan_rows(bu_scr, g_scr, t // SUBLANES, _scan_consts(p_ref, True), ptab, gin, True)
        gcarry_scr[:, 0:ns] = jnp.broadcast_to(gr, (SUBLANES, ns))
        gcarry_scr[:, ns:2 * ns] = jnp.broadcast_to(gi, (SUBLANES, ns))
        g = g_scr[...]
        row = lax.broadcasted_iota(jnp.int32, (t, ns), 0)
        hp_re = jnp.where(row == 0, hin[0:1, 0:ns], pltpu.roll(h[:, 0:ns], 1, 0))
        hp_im = jnp.where(row == 0, hin[0:1, ns:2 * ns], pltpu.roll(h[:, ns:2 * ns], 1, 0))
        g_re, g_im = g[:, 0:ns], g[:, ns:2 * ns]
        d_ar = _colsum(g_re * hp_re + g_im * hp_im)
        d_ai = _colsum(g_im * hp_re - g_re * hp_im)
        _acc(dlb_ref, first, jnp.concatenate([d_ar, d_ai], axis=1))
        gx = g.astype(_MXU)
        _acc(dbbt_ref, first, lax.dot_general(u.astype(_MXU), gx, tn_dims, preferred_element_type=_F32))
        dz_ref[...] = (dy * d_ref[...] + lax.dot_general(gx, bbt_ref[...].astype(_MXU), nt_dims,
                                                         preferred_element_type=_F32)).astype(dz_ref.dtype)

    f = lambda shape: jax.ShapeDtypeStruct(shape, _F32)
    return pl.pallas_call(
        body, name="ssm_bwd", grid=(nb, nt),
        out_shape=[jax.ShapeDtypeStruct((s, nb * BLOCK_CH), _MXU), f(bbt.shape), f(ct.shape), f(wg.shape), f((nb, 1, 2 * ns)),
                   f((1, nb * BLOCK_CH)), f((1, nb * BLOCK_CH))],
        in_specs=[sp["z"], sp["z"], sp["h"], sp["z"], sp["hb"], sp["bbt"], sp["ct"], sp["vec"], sp["wg"], sp["vec"], sp["p"],
                  sp["p"]],
        out_specs=[sp["z"], sp["bbt"], sp["ct"], sp["wg"], sp["acc_vec"], sp["vec"], sp["vec"]],
        scratch_shapes=[pltpu.VMEM((t, 2 * ns), _F32), pltpu.VMEM((t, 2 * ns), _F32), pltpu.VMEM((SUBLANES, 2 * ns), _F32)],
        compiler_params=_cp("parallel", "arbitrary"),
    )(z, y_pre, h_all, dy2, hb, bbt, ct, dvec, wg, bglu, ptab, ptab_rev)


def _mod_part(c_all, w, b):
    d, ns = w.shape
    tn = _tile(ns, 512)

    def body(c_ref, w_ref, b_ref, o_ref):
        c = c_ref[...]
        ca = (c * _sigmoid(c)).astype(_MXU)
        o_ref[...] = jnp.dot(ca, w_ref[...].astype(_MXU), preferred_element_type=_F32) + b_ref[...]

    return pl.pallas_call(
        body, name="mod_part", grid=(ns // tn,), out_shape=jax.ShapeDtypeStruct((8, ns), _F32),
        in_specs=[pl.BlockSpec((8, d), lambda n: (0, 0)), pl.BlockSpec((d, tn), lambda n: (0, n)),
                  pl.BlockSpec((1, tn), lambda n: (0, n))],
        out_specs=pl.BlockSpec((8, tn), lambda n: (0, n)), compiler_params=_cp("parallel"),
    )(c_all, w, b)


def _adamw_math(w, g, m, v):
    m = ADAM_B1 * m + (1.0 - ADAM_B1) * g
    v = ADAM_B2 * v + (1.0 - ADAM_B2) * (g * g)
    m_hat = m / (1.0 - ADAM_B1 ** ADAM_STEP)
    v_hat = v / (1.0 - ADAM_B2 ** ADAM_STEP)
    delta = -ADAM_LR * (m_hat / (jnp.sqrt(v_hat) + ADAM_EPS) + ADAM_WD * w)
    return delta, m, v


def _adamw(w, g, m, v, name):
    r, c = w.shape
    tc = c if c <= 4096 else _tile(c, 4096)
    tr = _tile(r, max(SUBLANES, (1 << 18) // tc), SUBLANES)

    def body(w_ref, g_ref, m_ref, v_ref, go_ref, d_ref, mo_ref, vo_ref):
        g = g_ref[...]
        go_ref[...] = g
        d_ref[...], mo_ref[...], vo_ref[...] = _adamw_math(w_ref[...], g, m_ref[...], v_ref[...])

    spec = pl.BlockSpec((tr, tc), lambda i, j: (i, j))
    out = jax.ShapeDtypeStruct((r, c), _F32)
    return pl.pallas_call(
        body, name=name, grid=(r // tr, c // tc), in_specs=[spec] * 4, out_specs=[spec] * 4, out_shape=[out] * 4,
        compiler_params=_cp("parallel", "parallel"),
    )(w, g, m, v)


def _adamw_halves(w, g2, m, v, name):
    r, c = w.shape
    tr, tc = _tile(r, 256, SUBLANES), _tile(c // 2, 1024)
    nph = (c // 2) // tc

    def body(w_ref, g_ref, m_ref, v_ref, go_ref, d_ref, mo_ref, vo_ref):
        g = g_ref[...]
        go_ref[...] = g
        d_ref[...], mo_ref[...], vo_ref[...] = _adamw_math(w_ref[...], g, m_ref[...], v_ref[...])

    spec = pl.BlockSpec((tr, tc), lambda i, j: (i, j))
    out = jax.ShapeDtypeStruct((r, c), _F32)
    return pl.pallas_call(
        body, name=name, grid=(r // tr, c // tc),
        in_specs=[spec, pl.BlockSpec((None, tr, tc), lambda i, j: (j // nph, i, j % nph)), spec, spec],
        out_specs=[spec] * 4, out_shape=[out] * 4, compiler_params=_cp("parallel", "parallel"),
    )(w, g2, m, v)


def _wada_update(c_t, dm, w, m, v):
    d, ns = w.shape
    tr, tc = _tile(d, 256, SUBLANES), _tile(ns, 1024)

    def body(c_ref, dm_ref, w_ref, m_ref, v_ref, g_ref, d_ref, mo_ref, vo_ref):
        c = c_ref[...]
        ca = c * _sigmoid(c)
        dmv = dm_ref[...]
        g = ca[:, 0:1] * dmv[0:1, :]
        for b in range(1, 8):
            g = g + ca[:, b:b + 1] * dmv[b:b + 1, :]
        g_ref[...] = g
        d_ref[...], mo_ref[...], vo_ref[...] = _adamw_math(w_ref[...], g, m_ref[...], v_ref[...])

    spec = pl.BlockSpec((tr, tc), lambda i, j: (i, j))
    out = jax.ShapeDtypeStruct((d, ns), _F32)
    return pl.pallas_call(
        body, name="wada_update", grid=(d // tr, ns // tc),
        in_specs=[pl.BlockSpec((tr, 8), lambda i, j: (i, 0)), pl.BlockSpec((8, tc), lambda i, j: (0, j)), spec, spec, spec],
        out_specs=[spec] * 4, out_shape=[out] * 4, compiler_params=_cp("parallel", "parallel"),
    )(c_t, dm, w, m, v)


def _small_reduce(gathered):
    _, r, c = gathered.shape
    tr = _tile(r, 512, SUBLANES)

    def body(q_ref, g_ref):
        g = q_ref[0]
        for k in range(1, 8):
            g = g + q_ref[k]
        g_ref[...] = g

    return pl.pallas_call(
        body, name="small_reduce", grid=(r // tr,), out_shape=jax.ShapeDtypeStruct((r, c), _F32),
        in_specs=[pl.BlockSpec((8, tr, c), lambda i: (0, i, 0))], out_specs=pl.BlockSpec((tr, c), lambda i: (i, 0)),
        compiler_params=_cp("parallel"),
    )(gathered)


def _adamw_many(ws, gs, ms, vs, steps, name):
    n = len(ws)

    def body(*refs):
        w_refs, g_refs, m_refs, v_refs = refs[0:n], refs[n:2 * n], refs[2 * n:3 * n], refs[3 * n:4 * n]
        d_refs, mo_refs, vo_refs = refs[4 * n:5 * n], refs[5 * n:6 * n], refs[6 * n:7 * n]
        for i in range(n):
            d_refs[i][...], mo_refs[i][...], vo_refs[i][...] = _adamw_math(
                w_refs[i][...], g_refs[i][...], m_refs[i][...], v_refs[i][...])

    def spec(a):
        nd = a.ndim
        if steps == 1:
            return pl.BlockSpec(a.shape, lambda i: (0,) * nd)
        return pl.BlockSpec((a.shape[0] // steps,) + a.shape[1:], lambda i: (i,) + (0,) * (nd - 1))

    specs = [spec(w) for w in ws]
    outs = pl.pallas_call(
        body, name=name, grid=(steps,), in_specs=specs * 4, out_specs=specs * 3,
        out_shape=[jax.ShapeDtypeStruct(w.shape, _F32) for w in ws] * 3, compiler_params=_cp("parallel"),
    )(*ws, *gs, *ms, *vs)
    return outs[0:n], outs[n:2 * n], outs[2 * n:3 * n]


def _block_diag(x):
    nb, g, p, q = x.shape
    eye = jnp.eye(g, dtype=x.dtype)
    return (x[:, :, :, None, :] * eye[None, :, None, :, None]).reshape(nb, g * p, g * q)


def _block_diag_take(x, p, q):
    nb = x.shape[0]
    g = GROUPS_PER_BLOCK
    eye = jnp.eye(g, dtype=x.dtype)
    return jnp.sum(x.reshape(nb, g, p, g, q) * eye[None, :, None, :, None], axis=3)


_VIEWS = {"ssm_b_re": ((0, 2, 1), (0, 2, 1)), "ssm_b_im": ((0, 2, 1), (0, 2, 1)),
          "ssm_w_glu": ((1, 2, 0), (2, 0, 1)), "ssm_b_glu": ((1, 0), (1, 0))}


def _to_view(name, a):
    return a.transpose(_VIEWS[name][0]) if name in _VIEWS else a


def _from_view(name, a):
    return a.transpose(_VIEWS[name][1]) if name in _VIEWS else a


class _Pack:
    def __init__(self, shapes):
        self.shapes = shapes
        self.offsets = {}
        off = 0
        for name, shape in shapes.items():
            n = math.prod(shape)
            self.offsets[name] = (off, n)
            off += -(-n // (SUBLANES * LANES)) * (SUBLANES * LANES)
        self.rows = -(-off // (256 * LANES)) * 256

    def pack(self, arrays):
        parts = []
        off = 0
        for name, shape in self.shapes.items():
            start, n = self.offsets[name]
            if start > off:
                parts.append(jnp.zeros((start - off,), _F32))
            parts.append(arrays[name].reshape(-1).astype(_F32))
            off = start + n
        total = self.rows * LANES
        if total > off:
            parts.append(jnp.zeros((total - off,), _F32))
        return jnp.concatenate(parts).reshape(self.rows, LANES)

    def unpack(self, buf):
        flat = buf.reshape(-1)
        return {name: flat[start:start + n].reshape(self.shapes[name]) for name, (start, n) in self.offsets.items()}


_SMALL = ["b_ada", "g_pre_mix", "g_post_mix", "ssm_log_dt", "ssm_a_re", "ssm_a_im", "ssm_b_re", "ssm_b_im", "ssm_c_re",
          "ssm_c_im", "ssm_d", "ssm_w_glu", "ssm_b_glu", "sgu_ln_g", "sgu_ln_b", "sgu_w", "sgu_b", "g_out_ssm",
          "g_out_sgu", "g_pre_ffn", "g_post_ffn", "conv_b"]
_WEIGHTS = ["w_ada", "b_ada", "g_pre_mix", "g_post_mix", "w_in", "ssm_log_dt", "ssm_a_re", "ssm_a_im", "ssm_b_re",
            "ssm_b_im", "ssm_c_re", "ssm_c_im", "ssm_d", "ssm_w_glu", "ssm_b_glu", "sgu_ln_g", "sgu_ln_b", "sgu_w", "sgu_b",
            "g_out_ssm", "g_out_sgu", "w_out", "g_pre_ffn", "g_post_ffn", "w_up", "conv_w", "conv_b", "w_down"]


def _step(p, m, v, x, c, tgt):
    s, d = x.shape
    mx, my, mc = lax.axis_index("x"), lax.axis_index("y"), lax.axis_index("c")
    chip = 2 * mx + my
    dev = 4 * mx + 2 * my + mc
    sel = jnp.stack([chip, mc]).astype(jnp.int32)
    g_cnt, n_st = p["ssm_a_re"].shape
    nb = g_cnt // GROUPS_PER_BLOCK
    gn = g_cnt * n_st
    d_ssm = g_cnt * SSM_GROUP
    nh = p["sgu_w"].shape[0]
    assert nh * CHUNK == d_ssm and 2 * d_ssm == d and n_st == SSM_STATE

    shards = lambda g: g.reshape(4, g.shape[1] * g.shape[2], g.shape[3])
    buf_in = _cast_into_slot(p["w_in"], sel, sel, "cast_w_in")

    ns_ada = p["w_ada"].shape[1]
    nc_conv = p["conv_w"].shape[1]
    first = jnp.concatenate([jnp.broadcast_to(c, (8, d)), jnp.pad(p["conv_w"], ((0, 5), (0, 0)))], axis=1)
    first_all = _all_gather8(_own_slot(first, dev), "gather_c_conv")
    c_all = first_all[:, 0, :d]
    conv_w_full = jnp.concatenate([first_all[2 * j, 0:3, d:] for j in range(4)], axis=1)
    b_ada_mine = lax.dynamic_slice_in_dim(p["b_ada"], chip * ns_ada, ns_ada, axis=1)
    mod_all = _all_gather8(_own_slot(_mod_part(c_all, p["w_ada"], b_ada_mine), dev), "gather_mod")
    (sems_in,), (buf_in,), tok = _gather_start([buf_in], mod_all, "gather_start_in")
    buf_out, buf_up, buf_down = [_cast_into_slot(p[n], sel, tok, "cast_" + n) for n in ("w_out", "w_up", "w_down")]
    mod_rows = lax.dynamic_index_in_dim(mod_all, dev, axis=1, keepdims=False)
    mod = jnp.concatenate([mod_rows[0], mod_rows[2], mod_rows[4], mod_rows[6]]).reshape(N_MOD, 1, d)
    sh1, sc1, gt1, sh2, sc2, gt2 = [mod[i] for i in range(N_MOD)]

    ldt_l = jnp.repeat(p["ssm_log_dt"], n_st, axis=1)
    are_l, aim_l = p["ssm_a_re"].reshape(1, gn), p["ssm_a_im"].reshape(1, gn)
    bre_t, bim_t = p["ssm_b_re"].reshape(gn, SSM_GROUP).T, p["ssm_b_im"].reshape(gn, SSM_GROUP).T
    pw_re, pw_im, bb_re, bb_im = _ssm_prep(ldt_l, are_l, aim_l, bre_t, bim_t)
    blocks = lambda t: t.reshape(t.shape[0], nb, GROUPS_PER_BLOCK * n_st).transpose(1, 0, 2)
    ptab = jnp.concatenate([blocks(pw_re), blocks(pw_im)], axis=2)
    ptab_rev = jnp.concatenate([blocks(pw_re)[:, ::-1], -blocks(pw_im)[:, ::-1]], axis=2)
    bd = lambda t: t.reshape(SSM_GROUP, nb, GROUPS_PER_BLOCK, n_st).transpose(1, 2, 0, 3)
    bbt = jnp.concatenate([_block_diag(bd(bb_re)), _block_diag(bd(bb_im))], axis=2).astype(_MXU)
    cd = lambda t: t.reshape(nb, GROUPS_PER_BLOCK, SSM_GROUP, n_st).transpose(0, 1, 3, 2)
    ct = jnp.concatenate([_block_diag(cd(p["ssm_c_re"])), -_block_diag(cd(p["ssm_c_im"]))], axis=1).astype(_MXU)
    wg = _block_diag(p["ssm_w_glu"].reshape(nb, GROUPS_PER_BLOCK, SSM_GROUP, SSM_GROUP)).astype(_MXU)
    dvec = p["ssm_d"]
    bglu = p["ssm_b_glu"].reshape(1, d_ssm)
    mask = jnp.tril(jnp.ones((CHUNK, CHUNK), _F32))
    wm = (p["sgu_w"] * mask[None]).astype(_MXU)
    bs = p["sgu_b"].reshape(nh, CHUNK, 1)

    h1 = _fwd_pre_mix(x, p["g_pre_mix"], _after(sc1, tok), sh1)
    buf_in = _gather_wait(sems_in, buf_in, h1, "gather_wait_in")
    w_in4 = shards(_pair_forward([buf_in], "pair_forward_in")[0])
    (sems_out, sems_up, sems_down), (buf_out, buf_up, buf_down), tok = _gather_start(
        [buf_out, buf_up, buf_down], w_in4, "gather_start_rest")
    z = _mm_nn(h1, w_in4, _F32, "mm_in", after=tok)
    y_ssm, y_pre, h_all, hb = _ssm_fwd(z, bbt, ct, dvec, wg, bglu, ptab)
    y_sgu = _sgu_fwd(z, p["sgu_ln_g"], p["sgu_ln_b"], wm, bs)
    ycat = _mix_norm_fwd(y_ssm, y_sgu, p["g_out_ssm"], p["g_out_sgu"])
    buf_out = _gather_wait(sems_out, buf_out, ycat, "gather_wait_out")
    w_out_full = _pair_forward([buf_out], "pair_forward_out")[0].reshape(1, d, d)
    o = _mm_nn(ycat, w_out_full, _F32, "mm_out")
    x1, h2 = _fwd_mid(o, x, gt1, p["g_post_mix"], p["g_pre_ffn"], sc2, sh2)
    buf_up = _gather_wait(sems_up, buf_up, h2, "gather_wait_up")
    w_up4 = shards(_pair_forward([buf_up], "pair_forward_up")[0])
    up_pre = _mm_nn(h2, w_up4, _F32, "mm_up")
    act = _conv_act_fwd(up_pre, conv_w_full, p["conv_b"])
    buf_down = _gather_wait(sems_down, buf_down, act, "gather_wait_down")
    w_down_full = _pair_forward([buf_down], "pair_forward_down")[0].reshape(1, -1, d)
    f = _mm_nn(act, w_down_full, _F32, "mm_down", tk=5632)
    dx2, df, d_gt2, d_g_post_ffn, loss = _loss_and_post_ffn_bwd(f, x1, tgt, gt2, p["g_post_ffn"])

    def reduce_next(swap, n, after):
        sems, gw, land, _ = swap
        gw, got = _swap_wait(sems, gw, land, after, "swap_wait_" + n)
        return _scatter_start(_pair_sum(gw, got, sel, "pair_sum_" + n), "scatter_start_" + n)

    d_act = _mm_nt(df, w_down_full, _F32, "mm_d_act", tk=2048)
    swap_down = _swap_start(_mm_tn_rows(act, df, "mm_gw_down"), "swap_start_w_down")
    d_up_pre, d_cw0, d_cw1, d_cw2, d_conv_b = _conv_act_bwd(up_pre, d_act, conv_w_full, _after(p["conv_b"], swap_down[3]))
    red_down = reduce_next(swap_down, "w_down", d_conv_b)
    dh2 = _mm_nt(d_up_pre, w_up4, _F32, "mm_dh2", tk=2816, after=red_down[3])
    swap_up = _swap_start(_mm_tn_cols(h2, d_up_pre, "mm_gw_up"), "swap_start_w_up")
    dx1, d_o, d_sc2, d_sh2, d_g_pre_ffn, d_gt1, d_g_post_mix = _bwd_mid(
        dh2, x1, dx2, o, p["g_pre_ffn"], _after(sc2, swap_up[3]), gt1, p["g_post_mix"])
    red_up = reduce_next(swap_up, "w_up", d_g_post_mix)
    d_ycat = _mm_nt(d_o, w_out_full, _F32, "mm_d_ycat", tn=1024, tk=2048, after=red_up[3])
    swap_out = _swap_start(_mm_tn_rows(ycat, d_o, "mm_gw_out"), "swap_start_w_out")
    dy_ssm, dy_sgu, d_g_out_ssm, d_g_out_sgu = _mix_norm_bwd(
        d_ycat, y_ssm, y_sgu, _after(p["g_out_ssm"], swap_out[3]), p["g_out_sgu"])
    red_out = reduce_next(swap_out, "w_out", d_g_out_sgu)
    dz_ssm, d_bbt, d_ct, d_wg, d_lb, d_ssm_d, d_bglu = _ssm_bwd(z, y_pre, h_all, dy_ssm, hb, bbt, ct,
                                                                _after(dvec, red_out[3]), wg, bglu, ptab, ptab_rev)
    dz, d_ln_g, d_ln_b, d_wm, d_bs = _sgu_bwd(z, dy_sgu, dz_ssm, p["sgu_ln_g"], p["sgu_ln_b"], wm, bs)
    dh1 = _mm_nt(dz, w_in4, _F32, "mm_dh1")
    swap_in = _swap_start(_mm_tn_cols(h1, dz, "mm_gw_in"), "swap_start_w_in")
    dx, d_sc1, d_sh1, d_g_pre_mix = _bwd_pre_mix(dh1, x, dx1, p["g_pre_mix"], _after(sc1, swap_in[3]))
    red_in = reduce_next(swap_in, "w_in", d_g_pre_mix)

    nsb = BLOCK_ST
    lanes = lambda t: t.transpose(2, 0, 1, 3).reshape(SSM_GROUP, gn)
    d_bbr = lanes(_block_diag_take(d_bbt[:, :, :nsb], SSM_GROUP, n_st))
    d_bbi = lanes(_block_diag_take(d_bbt[:, :, nsb:], SSM_GROUP, n_st))
    d_lr, d_li = d_lb[:, 0, :nsb].reshape(1, gn), d_lb[:, 0, nsb:].reshape(1, gn)
    d_bre_t, d_bim_t, d_are, d_aim, d_dt = _ssm_prep_bwd(ldt_l, are_l, aim_l, bre_t, bim_t, d_bbr, d_bbi, d_lr, d_li)
    d_log_dt = _group_sum(d_dt.reshape(g_cnt, n_st), p["ssm_log_dt"].reshape(g_cnt, 1))
    c_grad = lambda t: _block_diag_take(t, n_st, SSM_GROUP).transpose(0, 1, 3, 2).reshape(g_cnt, SSM_GROUP, n_st)
    small = {
        "b_ada": jnp.concatenate([d_sh1, _after(d_sc1, red_in[3]), d_gt1, d_sh2, d_sc2, d_gt2], axis=1),
        "g_pre_mix": d_g_pre_mix, "g_post_mix": d_g_post_mix,
        "ssm_log_dt": d_log_dt, "ssm_a_re": d_are, "ssm_a_im": d_aim,
        "ssm_b_re": d_bre_t.T, "ssm_b_im": d_bim_t.T,
        "ssm_c_re": c_grad(d_ct[:, :nsb, :]), "ssm_c_im": -c_grad(d_ct[:, nsb:, :]),
        "ssm_d": d_ssm_d, "ssm_w_glu": _block_diag_take(d_wg, SSM_GROUP, SSM_GROUP), "ssm_b_glu": d_bglu,
        "sgu_ln_g": d_ln_g, "sgu_ln_b": d_ln_b, "sgu_w": d_wm * mask[None], "sgu_b": d_bs,
        "g_out_ssm": d_g_out_ssm, "g_out_sgu": d_g_out_sgu, "g_pre_ffn": d_g_pre_ffn, "g_post_ffn": d_g_post_ffn,
        "conv_b": d_conv_b, "conv_w_all": jnp.concatenate([d_cw0, d_cw1, d_cw2], axis=0),
        "loss_sum": loss,
    }
    small = {n: _to_view(n, a.reshape(p[n].shape)) if n in p else a for n, a in small.items()}
    pk = _Pack({n: a.shape for n, a in small.items()})
    sems_small, small_buf, tok = _gather8_start(_own_slot(pk.pack(small), dev), "gather_small_start")

    big = ["w_down", "w_up", "w_out", "w_in"]
    joins = []
    after = tok
    for n, (sems, pair, land, _) in zip(big, (red_down, red_up, red_out, red_in)):
        pair, land = _scatter_wait(sems, pair, land, after, "scatter_wait_" + n)
        sems_j, half, after = _join_start(_chip_sum(pair, land, sel, "chip_sum_" + n), "join_start_" + n)
        joins.append((sems_j, half))
    big_out = {}
    for n, (sems_j, half) in zip(big, joins):
        j = _join_wait(sems_j, half, after, "join_wait_" + n)
        if n in ("w_in", "w_up"):
            big_out[n] = tuple(_adamw(p[n], j.reshape(p[n].shape), m[n], v[n], "adamw_" + n))
        else:
            big_out[n] = tuple(_adamw_halves(p[n], j, m[n], v[n], "adamw_" + n))
        after = big_out[n][1]

    gathered = _gather8_forward(_gather8_wait(sems_small, small_buf, after, "gather_small_wait"),
                                "gather_small_forward")
    gview = pk.unpack(_small_reduce(gathered))
    gview["conv_w"] = lax.dynamic_slice_in_dim(gview.pop("conv_w_all"), chip * nc_conv, nc_conv, axis=1)
    loss = gview.pop("loss_sum")
    small_names = _SMALL + ["conv_w"]
    per_group = [n for n in small_names if gview[n].ndim >= 2 and gview[n].shape[0] == g_cnt]
    others = [n for n in small_names if n not in per_group]
    grads = {n: _from_view(n, gview[n]) for n in small_names}
    deltas, new_m, new_v = {}, {}, {}
    for names, steps, call in ((per_group, g_cnt // GROUPS_PER_BLOCK, "adamw_s5"), (others, 1, "adamw_small")):
        res = _adamw_many([_to_view(n, p[n]) for n in names], [gview[n] for n in names],
                          [_to_view(n, m[n]) for n in names], [_to_view(n, v[n]) for n in names], steps, call)
        for n, dl, mo, vo in zip(names, *res):
            deltas[n], new_m[n], new_v[n] = _from_view(n, dl), _from_view(n, mo), _from_view(n, vo)

    d_mod_all = gathered.reshape(8, -1)[:, :N_MOD * d]
    d_mod_mine = lax.dynamic_slice_in_dim(d_mod_all, chip * ns_ada, ns_ada, axis=1)
    grads["w_ada"], deltas["w_ada"], new_m["w_ada"], new_v["w_ada"] = _wada_update(
        c_all.T, d_mod_mine, p["w_ada"], m["w_ada"], v["w_ada"])
    for n in big:
        grads[n], deltas[n], new_m[n], new_v[n] = big_out[n]
    return loss[0, 0], dx, grads, deltas, new_m, new_v


def kernel(x, c, w_ada, b_ada, g_pre_mix, g_post_mix, w_in, ssm_log_dt, ssm_a_re, ssm_a_im, ssm_b_re, ssm_b_im, ssm_c_re, ssm_c_im, ssm_d, ssm_w_glu, ssm_b_glu, sgu_ln_g, sgu_ln_b, sgu_w, sgu_b, g_out_ssm, g_out_sgu, w_out, g_pre_ffn, g_post_ffn, w_up, conv_w, conv_b, w_down, loss_target, m_w_ada, m_b_ada, m_g_pre_mix, m_g_post_mix, m_w_in, m_ssm_log_dt, m_ssm_a_re, m_ssm_a_im, m_ssm_b_re, m_ssm_b_im, m_ssm_c_re, m_ssm_c_im, m_ssm_d, m_ssm_w_glu, m_ssm_b_glu, m_sgu_ln_g, m_sgu_ln_b, m_sgu_w, m_sgu_b, m_g_out_ssm, m_g_out_sgu, m_w_out, m_g_pre_ffn, m_g_post_ffn, m_w_up, m_conv_w, m_conv_b, m_w_down, v_w_ada, v_b_ada, v_g_pre_mix, v_g_post_mix, v_w_in, v_ssm_log_dt, v_ssm_a_re, v_ssm_a_im, v_ssm_b_re, v_ssm_b_im, v_ssm_c_re, v_ssm_c_im, v_ssm_d, v_ssm_w_glu, v_ssm_b_glu, v_sgu_ln_g, v_sgu_ln_b, v_sgu_w, v_sgu_b, v_g_out_ssm, v_g_out_sgu, v_w_out, v_g_pre_ffn, v_g_post_ffn, v_w_up, v_conv_w, v_conv_b, v_w_down):
    given = dict(locals())
    drop = lambda a: a if a.ndim == 2 else a[0]
    p = {n: drop(given[n]) for n in _WEIGHTS}
    m = {n: drop(given["m_" + n]) for n in _WEIGHTS}
    v = {n: drop(given["v_" + n]) for n in _WEIGHTS}
    loss, dx, grads, deltas, new_m, new_v = _step(p, m, v, x[0], c, loss_target[0])
    outs = [loss, dx[None]]
    for group in (grads, deltas, new_m, new_v):
        outs += [group[n].reshape(given[n].shape) for n in _WEIGHTS]
    return tuple(outs)
```

```python
import functools
import math

import jax
import jax.numpy as jnp
from jax import lax
from jax.experimental import pallas as pl
from jax.experimental.pallas import tpu as pltpu

_F32 = jnp.float32
_MXU = jnp.bfloat16
_WIRE = jnp.bfloat16

EPS = 1e-6
SSM_GROUP = 16
SSM_STATE = 64
GROUPS_PER_BLOCK = 8
BLOCK_CH = SSM_GROUP * GROUPS_PER_BLOCK
BLOCK_ST = SSM_STATE * GROUPS_PER_BLOCK
CHUNK = 128
TIME_TILE = 512
SUBLANES = 8
LANES = 128
N_MOD = 6
ADAM_LR, ADAM_B1, ADAM_B2, ADAM_EPS, ADAM_WD, ADAM_STEP = 0.001, 0.9, 0.999, 1e-08, 0.01, 10
_VMEM_LIMIT = 56 * 1024 * 1024
_MESH = pl.DeviceIdType.MESH
_ANY = pl.BlockSpec(memory_space=pl.ANY)
_HBM = pl.BlockSpec(memory_space=pltpu.HBM)
_SEM = pl.BlockSpec(memory_space=pltpu.SEMAPHORE)
_VMEM_WHOLE = pl.BlockSpec(memory_space=pltpu.VMEM)
_EFFECT = pltpu.SideEffectType.DATAFLOW_SIDE_EFFECTING
_GELU_C = math.sqrt(2.0 / math.pi)


def _cp(*sem):
    return pltpu.CompilerParams(dimension_semantics=sem, vmem_limit_bytes=_VMEM_LIMIT)


def _tile(dim, target, align=LANES):
    if dim <= target:
        return dim
    best = None
    for t in range(align, target + 1, align):
        if dim % t == 0:
            best = t
    assert best is not None, (dim, target, align)
    return best


def _gelu(x):
    return 0.5 * x * (1.0 + jnp.tanh(_GELU_C * (x + 0.044715 * (x * x * x))))


def _gelu_grad(x):
    t = jnp.tanh(_GELU_C * (x + 0.044715 * (x * x * x)))
    return 0.5 * (1.0 + t) + 0.5 * x * (1.0 - t * t) * (_GELU_C * (1.0 + 3.0 * 0.044715 * x * x))


def _sigmoid(x):
    return 1.0 / (1.0 + jnp.exp(-x))


def _colsum(x):
    return jnp.sum(x, axis=0, keepdims=True)


def _rowmean(x):
    return jnp.mean(x, axis=-1, keepdims=True)


def _zero_first(first, *refs):
    @pl.when(first)
    def _():
        for ref in refs:
            ref[...] = jnp.zeros_like(ref)


def _acc(ref, first, val):
    del first
    ref[...] += val


def _place():
    mx, my, mc = lax.axis_index("x"), lax.axis_index("y"), lax.axis_index("c")
    chips = [(1 - mx, my), (mx, 1 - my), (1 - mx, 1 - my)]
    return mx, my, mc, chips


def _all_gather8(buf, name):
    def body(in_ref, out_ref, send_sems, recv_sems):
        mx, my, mc, chips = _place()
        me, sibling = (mx, my, mc), (mx, my, 1 - mc)

        def slot(ref, px, py, pc):
            return ref.at[4 * px + 2 * py + pc]

        def copy(k, block, to, src_ref=out_ref):
            return pltpu.make_async_remote_copy(
                src_ref=slot(src_ref, *block), dst_ref=slot(out_ref, *block),
                send_sem=send_sems.at[k], recv_sem=recv_sems.at[k], device_id=to, device_id_type=_MESH)

        first = [copy(0, me, sibling, in_ref)]
        first += [copy(1 + j, me, (*chip, mc), in_ref) for j, chip in enumerate(chips)]
        for cp in first:
            cp.start()
        passed = [copy(4 + j, (*chip, mc), sibling) for j, chip in enumerate(chips)]
        for j, chip in enumerate(chips):
            copy(1 + j, (*chip, mc), me).wait_recv()
            passed[j].start()
        copy(0, sibling, me).wait_recv()
        for j, chip in enumerate(chips):
            copy(4 + j, (*chip, 1 - mc), me).wait_recv()
        for cp in first + passed:
            cp.wait_send()

    return pl.pallas_call(
        body, name=name, out_shape=jax.ShapeDtypeStruct(buf.shape, buf.dtype),
        in_specs=[_ANY], out_specs=_ANY, input_output_aliases={0: 0},
        scratch_shapes=[pltpu.SemaphoreType.DMA((7,)), pltpu.SemaphoreType.DMA((7,))],
    )(buf)


def _own_slot(x, dev):
    return lax.dynamic_update_slice(jnp.zeros((8,) + x.shape, x.dtype), x[None], (dev, 0, 0))


def _cast_into_slot(w, sel, after, name):
    r, c = w.shape
    hr = r // 2
    tr = _tile(hr, 256, 16)
    nr = hr // tr

    def body(sel_ref, w_ref, after_ref, o_ref):
        o_ref[...] = w_ref[...].astype(o_ref.dtype)

    return pl.pallas_call(
        body, name=name, out_shape=jax.ShapeDtypeStruct((4, 2, hr, c), _WIRE),
        grid_spec=pltpu.PrefetchScalarGridSpec(
            num_scalar_prefetch=1, grid=(2, nr),
            in_specs=[pl.BlockSpec((tr, c), lambda h, i, s: (h * nr + i, 0)), _ANY],
            out_specs=pl.BlockSpec((None, None, tr, c), lambda h, i, s: (s[0], h, i, 0))),
        compiler_params=_cp("parallel", "parallel"),
    )(sel, w, after)


def _hbm(a):
    return pltpu.with_memory_space_constraint(a, pltpu.HBM)


def _after(vec, token):
    return vec + token[0:1, 0:1]


def _gather_start(bufs, after, name):
    n = len(bufs)
    nc = 3 * n

    def body(*refs):
        ins, send, recv, token = refs[:n], refs[n + 1:n + 1 + nc], refs[n + 1 + nc:n + 1 + 2 * nc], refs[-1]
        mx, my, mc, chips = _place()
        j_me = 2 * mx + my
        for i in range(n):
            for k, chip in enumerate(chips):
                half = ins[i].at[j_me, mc]
                pltpu.make_async_remote_copy(
                    src_ref=half, dst_ref=half, send_sem=send[3 * i + k], recv_sem=recv[3 * i + k],
                    device_id=(*chip, mc), device_id_type=_MESH).start()
        token[...] = jnp.zeros_like(token)

    outs = pl.pallas_call(
        body, name=name,
        out_shape=tuple([pltpu.SemaphoreType.DMA(())] * (2 * nc) + [pltpu.HBM(b.shape, b.dtype) for b in bufs]
                        + [jax.ShapeDtypeStruct((SUBLANES, LANES), _F32)]),
        in_specs=tuple([_HBM] * n + [_ANY]), out_specs=tuple([_SEM] * (2 * nc) + [_HBM] * n + [_VMEM_WHOLE]),
        input_output_aliases={i: 2 * nc + i for i in range(n)},
        compiler_params=pltpu.CompilerParams(has_side_effects=_EFFECT),
    )(*[_hbm(b) for b in bufs], after)
    sems = [(outs[3 * i:3 * i + 3], outs[nc + 3 * i:nc + 3 * i + 3]) for i in range(n)]
    return sems, list(outs[2 * nc:2 * nc + n]), outs[-1]


def _gather_wait(sems, buf, after, name):
    send, recv = sems

    after = list(after) if isinstance(after, (list, tuple)) else [after]

    def body(buf_ref, s0, s1, s2, r0, r1, r2, *rest):
        mx, my, mc, chips = _place()
        j_me = 2 * mx + my
        for k, (chip, s_k, r_k) in enumerate(zip(chips, (s0, s1, s2), (r0, r1, r2))):
            cp = pltpu.make_async_remote_copy(
                src_ref=buf_ref.at[j_me, mc], dst_ref=buf_ref.at[2 * chip[0] + chip[1], mc], send_sem=s_k, recv_sem=r_k,
                device_id=(*chip, mc), device_id_type=_MESH)
            cp.wait_send()
            cp.wait_recv()

    return pl.pallas_call(
        body, name=name, out_shape=pltpu.HBM(buf.shape, buf.dtype),
        in_specs=(_HBM,) + (_SEM,) * 6 + (_ANY,) * len(after), out_specs=_HBM, input_output_aliases={0: 0},
        compiler_params=pltpu.CompilerParams(has_side_effects=_EFFECT),
    )(buf, *send, *recv, *after)


def _pair_forward(bufs, name):
    n = len(bufs)

    def body(*refs):
        ins, outs = refs[:n], refs[n:2 * n]
        send_sems, recv_sems = refs[2 * n:]
        mx, my, mc, chips = _place()
        sibling = (mx, my, 1 - mc)
        cps = []
        for i in range(n):
            for k, chip in enumerate(chips):
                j_k = 2 * chip[0] + chip[1]
                cp = pltpu.make_async_remote_copy(
                    src_ref=ins[i].at[j_k, mc], dst_ref=outs[i].at[j_k, mc], send_sem=send_sems.at[3 * i + k],
                    recv_sem=recv_sems.at[3 * i + k], device_id=sibling, device_id_type=_MESH)
                cp.start()
                cps.append(cp)
        for i in range(n):
            for k, chip in enumerate(chips):
                other = outs[i].at[2 * chip[0] + chip[1], 1 - mc]
                pltpu.make_async_remote_copy(
                    src_ref=other, dst_ref=other, send_sem=send_sems.at[3 * i + k], recv_sem=recv_sems.at[3 * i + k],
                    device_id=sibling, device_id_type=_MESH).wait_recv()
        for cp in cps:
            cp.wait_send()

    return pl.pallas_call(
        body, name=name, out_shape=[jax.ShapeDtypeStruct(b.shape, b.dtype) for b in bufs],
        in_specs=[_ANY] * n, out_specs=[_ANY] * n, input_output_aliases={i: i for i in range(n)},
        scratch_shapes=[pltpu.SemaphoreType.DMA((3 * n,)), pltpu.SemaphoreType.DMA((3 * n,))],
    )(*bufs)


def _gather8_peers(buf_ref, mx, my, mc, chips):
    mine = buf_ref.at[4 * mx + 2 * my + mc]
    peers = [((mx, my, 1 - mc), mine, buf_ref.at[4 * mx + 2 * my + 1 - mc])]
    peers += [((*chip, mc), mine, buf_ref.at[4 * chip[0] + 2 * chip[1] + mc]) for chip in chips]
    return peers


def _gather8_start(buf, name):
    def body(buf_ref, *rest):
        send, recv, token = rest[0:4], rest[4:8], rest[-1]
        mx, my, mc, chips = _place()
        for k, (peer, src, _) in enumerate(_gather8_peers(buf_ref, mx, my, mc, chips)):
            pltpu.make_async_remote_copy(src_ref=src, dst_ref=src, send_sem=send[k], recv_sem=recv[k],
                                         device_id=peer, device_id_type=_MESH).start()
        token[...] = jnp.zeros_like(token)

    outs = pl.pallas_call(
        body, name=name,
        out_shape=tuple([pltpu.SemaphoreType.DMA(())] * 8 + [pltpu.HBM(buf.shape, buf.dtype),
                                                             jax.ShapeDtypeStruct((SUBLANES, LANES), _F32)]),
        in_specs=(_HBM,), out_specs=tuple([_SEM] * 8 + [_HBM, _VMEM_WHOLE]), input_output_aliases={0: 8},
        compiler_params=pltpu.CompilerParams(has_side_effects=_EFFECT),
    )(_hbm(buf))
    return (outs[0:4], outs[4:8]), outs[8], outs[9]


def _gather8_wait(sems, buf, after, name):
    send, recv = sems

    def body(buf_ref, s0, s1, s2, s3, r0, r1, r2, r3, after_ref, out_ref):
        mx, my, mc, chips = _place()
        for (peer, src, dst), s_k, r_k in zip(_gather8_peers(buf_ref, mx, my, mc, chips), (s0, s1, s2, s3), (r0, r1, r2, r3)):
            cp = pltpu.make_async_remote_copy(src_ref=src, dst_ref=dst, send_sem=s_k, recv_sem=r_k,
                                              device_id=peer, device_id_type=_MESH)
            cp.wait_send()
            cp.wait_recv()

    return pl.pallas_call(
        body, name=name, out_shape=pltpu.HBM(buf.shape, buf.dtype),
        in_specs=(_HBM,) + (_SEM,) * 8 + (_ANY,), out_specs=_HBM, input_output_aliases={0: 0},
        compiler_params=pltpu.CompilerParams(has_side_effects=_EFFECT),
    )(buf, *send, *recv, after)


def _gather8_forward(buf, name):
    def body(in_ref, out_ref, send_sems, recv_sems):
        mx, my, mc, chips = _place()
        sibling = (mx, my, 1 - mc)
        cps = []
        for k, chip in enumerate(chips):
            idx = 4 * chip[0] + 2 * chip[1] + mc
            cp = pltpu.make_async_remote_copy(src_ref=in_ref.at[idx], dst_ref=out_ref.at[idx], send_sem=send_sems.at[k],
                                              recv_sem=recv_sems.at[k], device_id=sibling, device_id_type=_MESH)
            cp.start()
            cps.append(cp)
        for k, chip in enumerate(chips):
            other = out_ref.at[4 * chip[0] + 2 * chip[1] + 1 - mc]
            pltpu.make_async_remote_copy(src_ref=other, dst_ref=other, send_sem=send_sems.at[k], recv_sem=recv_sems.at[k],
                                         device_id=sibling, device_id_type=_MESH).wait_recv()
        for cp in cps:
            cp.wait_send()

    return pl.pallas_call(
        body, name=name, out_shape=jax.ShapeDtypeStruct(buf.shape, buf.dtype),
        in_specs=[_ANY], out_specs=_ANY, input_output_aliases={0: 0},
        scratch_shapes=[pltpu.SemaphoreType.DMA((3,)), pltpu.SemaphoreType.DMA((3,))],
    )(buf)


def _scatter_start(pair, name):
    land = lax.empty((3,) + pair.shape[1:], pair.dtype)

    def body(pair_ref, land_ref, s0, s1, s2, r0, r1, r2, pair_thru, land_thru, token):
        mx, my, mc, chips = _place()
        for k, (chip, s_k, r_k) in enumerate(zip(chips, (s0, s1, s2), (r0, r1, r2))):
            pltpu.make_async_remote_copy(
                src_ref=pair_ref.at[2 * chip[0] + chip[1]], dst_ref=land_ref.at[k], send_sem=s_k, recv_sem=r_k,
                device_id=(*chip, mc), device_id_type=_MESH).start()
        token[...] = jnp.zeros_like(token)

    outs = pl.pallas_call(
        body, name=name,
        out_shape=tuple([pltpu.SemaphoreType.DMA(())] * 6 + [pltpu.HBM(pair.shape, pair.dtype), pltpu.HBM(land.shape, land.dtype),
                                                             jax.ShapeDtypeStruct((SUBLANES, LANES), _F32)]),
        in_specs=(_HBM, _HBM), out_specs=tuple([_SEM] * 6 + [_HBM, _HBM, _VMEM_WHOLE]),
        input_output_aliases={0: 6, 1: 7}, compiler_params=pltpu.CompilerParams(has_side_effects=_EFFECT),
    )(_hbm(pair), _hbm(land))
    return (outs[0:3], outs[3:6]), outs[6], outs[7], outs[8]


def _scatter_wait(sems, pair, land, after, name):
    send, recv = sems

    def body(pair_ref, land_ref, s0, s1, s2, r0, r1, r2, after_ref, pair_out, land_out):
        mx, my, mc, chips = _place()
        for k, (chip, s_k, r_k) in enumerate(zip(chips, (s0, s1, s2), (r0, r1, r2))):
            cp = pltpu.make_async_remote_copy(
                src_ref=pair_ref.at[2 * chip[0] + chip[1]], dst_ref=land_ref.at[k], send_sem=s_k, recv_sem=r_k,
                device_id=(*chip, mc), device_id_type=_MESH)
            cp.wait_send()
            cp.wait_recv()

    return pl.pallas_call(
        body, name=name, out_shape=(pltpu.HBM(pair.shape, pair.dtype), pltpu.HBM(land.shape, land.dtype)),
        in_specs=(_HBM, _HBM) + (_SEM,) * 6 + (_ANY,), out_specs=(_HBM, _HBM), input_output_aliases={0: 0, 1: 1},
        compiler_params=pltpu.CompilerParams(has_side_effects=_EFFECT),
    )(pair, land, *send, *recv, after)


def _sibling_copy(src_ref, dst_ref, send_sem, recv_sem):
    mx, my, mc, _ = _place()
    return pltpu.make_async_remote_copy(src_ref=src_ref, dst_ref=dst_ref, send_sem=send_sem, recv_sem=recv_sem,
                                        device_id=(mx, my, 1 - mc), device_id_type=_MESH)


def _swap_start(g, name):
    land = lax.empty(g.shape[1:], g.dtype)

    def body(g_ref, land_ref, send_sem, recv_sem, g_thru, land_thru, token):
        _sibling_copy(g_ref.at[1 - lax.axis_index("c")], land_ref, send_sem, recv_sem).start()
        token[...] = jnp.zeros_like(token)

    outs = pl.pallas_call(
        body, name=name,
        out_shape=(pltpu.SemaphoreType.DMA(()), pltpu.SemaphoreType.DMA(()), pltpu.HBM(g.shape, g.dtype),
                   pltpu.HBM(land.shape, land.dtype), jax.ShapeDtypeStruct((SUBLANES, LANES), _F32)),
        in_specs=(_HBM, _HBM), out_specs=(_SEM, _SEM, _HBM, _HBM, _VMEM_WHOLE), input_output_aliases={0: 2, 1: 3},
        compiler_params=pltpu.CompilerParams(has_side_effects=_EFFECT),
    )(_hbm(g), _hbm(land))
    return (outs[0], outs[1]), outs[2], outs[3], outs[4]


def _swap_wait(sems, g, land, after, name):
    def body(g_ref, land_ref, send_sem, recv_sem, after_ref, g_out, land_out):
        cp = _sibling_copy(g_ref.at[1 - lax.axis_index("c")], land_ref, send_sem, recv_sem)
        cp.wait_send()
        cp.wait_recv()

    return pl.pallas_call(
        body, name=name, out_shape=(pltpu.HBM(g.shape, g.dtype), pltpu.HBM(land.shape, land.dtype)),
        in_specs=(_HBM, _HBM, _SEM, _SEM, _ANY), out_specs=(_HBM, _HBM), input_output_aliases={0: 0, 1: 1},
        compiler_params=pltpu.CompilerParams(has_side_effects=_EFFECT),
    )(g, land, *sems, after)


def _join_start(buf, name):
    def body(buf_ref, send_sem, recv_sem, buf_thru, token):
        mine = buf_ref.at[lax.axis_index("c")]
        _sibling_copy(mine, mine, send_sem, recv_sem).start()
        token[...] = jnp.zeros_like(token)

    outs = pl.pallas_call(
        body, name=name,
        out_shape=(pltpu.SemaphoreType.DMA(()), pltpu.SemaphoreType.DMA(()), pltpu.HBM(buf.shape, buf.dtype),
                   jax.ShapeDtypeStruct((SUBLANES, LANES), _F32)),
        in_specs=(_HBM,), out_specs=(_SEM, _SEM, _HBM, _VMEM_WHOLE), input_output_aliases={0: 2},
        compiler_params=pltpu.CompilerParams(has_side_effects=_EFFECT),
    )(_hbm(buf))
    return (outs[0], outs[1]), outs[2], outs[3]


def _join_wait(sems, buf, after, name):
    def body(buf_ref, send_sem, recv_sem, after_ref, buf_out):
        mc = lax.axis_index("c")
        cp = _sibling_copy(buf_ref.at[mc], buf_ref.at[1 - mc], send_sem, recv_sem)
        cp.wait_send()
        cp.wait_recv()

    return pl.pallas_call(
        body, name=name, out_shape=pltpu.HBM(buf.shape, buf.dtype),
        in_specs=(_HBM, _SEM, _SEM, _ANY), out_specs=_HBM, input_output_aliases={0: 0},
        compiler_params=pltpu.CompilerParams(has_side_effects=_EFFECT),
    )(buf, *sems, after)


def _pair_sum(g, got, sel, name):
    _, four, hr, c = g.shape
    tr = _tile(hr, 512, 16)

    def body(sel_ref, g_ref, p_ref, o_ref):
        o_ref[...] = (g_ref[...].astype(_F32) + p_ref[...].astype(_F32)).astype(o_ref.dtype)

    return pl.pallas_call(
        body, name=name, out_shape=jax.ShapeDtypeStruct((four, hr, c), g.dtype),
        grid_spec=pltpu.PrefetchScalarGridSpec(
            num_scalar_prefetch=1, grid=(four, hr // tr),
            in_specs=[pl.BlockSpec((None, None, tr, c), lambda j, i, s: (s[1], j, i, 0)),
                      pl.BlockSpec((None, tr, c), lambda j, i, s: (j, i, 0))],
            out_specs=pl.BlockSpec((None, tr, c), lambda j, i, s: (j, i, 0))),
        compiler_params=_cp("parallel", "parallel"),
    )(sel, g, got)


def _chip_sum(pair, got, sel, name):
    _, hr, c = pair.shape
    tr = _tile(hr, 512, 16)

    def body(sel_ref, p_ref, q_ref, o_ref):
        o_ref[...] = ((p_ref[...].astype(_F32) + q_ref[0].astype(_F32)) + q_ref[1].astype(_F32)) + q_ref[2].astype(_F32)

    return pl.pallas_call(
        body, name=name, out_shape=jax.ShapeDtypeStruct((2, hr, c), _F32),
        grid_spec=pltpu.PrefetchScalarGridSpec(
            num_scalar_prefetch=1, grid=(hr // tr,),
            in_specs=[pl.BlockSpec((None, tr, c), lambda i, s: (s[0], i, 0)),
                      pl.BlockSpec((3, tr, c), lambda i, s: (0, i, 0))],
            out_specs=pl.BlockSpec((None, tr, c), lambda i, s: (s[1], i, 0))),
        compiler_params=_cp("parallel"),
    )(sel, pair, got)


def _matmul(a, b, dims, out_struct, grid, a_spec, b_spec, o_spec, acc_shape, k_axis, name, after=None):
    nk = grid[k_axis]
    extra = [] if after is None else [after]

    def body(a_ref, b_ref, *rest):
        o_ref, acc = rest[len(extra)], rest[len(extra) + 1:]
        prod = lax.dot_general(a_ref[...].astype(_MXU), b_ref[...].astype(_MXU), dims, preferred_element_type=_F32)
        if nk == 1:
            o_ref[...] = prod.astype(o_ref.dtype)
        else:
            acc_ref, = acc
            k = pl.program_id(k_axis)
            _zero_first(k == 0, acc_ref)
            acc_ref[...] += prod

            @pl.when(k == nk - 1)
            def _():
                o_ref[...] = acc_ref[...].astype(o_ref.dtype)

    sem = ["parallel"] * len(grid)
    sem[k_axis] = "arbitrary"
    return pl.pallas_call(
        body, name=name, out_shape=out_struct, grid=grid, in_specs=[a_spec, b_spec] + [_ANY] * len(extra), out_specs=o_spec,
        scratch_shapes=[pltpu.VMEM(acc_shape, _F32)] if nk > 1 else [], compiler_params=_cp(*sem),
    )(a, b, *extra)


def _mm_nn(a, w4, out_dtype, name, tm=512, tn=1536, tk=2048, after=None):
    m, k = a.shape
    j, _, ns = w4.shape
    tm, tn, tk = _tile(m, tm, 16), _tile(ns, tn), _tile(k, tk)
    nps = ns // tn
    return _matmul(
        a, w4, (((1,), (0,)), ((), ())), jax.ShapeDtypeStruct((m, j * ns), out_dtype),
        (j * nps, m // tm, k // tk),
        pl.BlockSpec((tm, tk), lambda ni, mi, ki: (mi, ki)),
        pl.BlockSpec((None, tk, tn), lambda ni, mi, ki: (ni // nps, ki, ni % nps)),
        pl.BlockSpec((tm, tn), lambda ni, mi, ki: (mi, ni)), (tm, tn), 2, name, after)


def _mm_nt(a, w4, out_dtype, name, tm=512, tn=2048, tk=1536, after=None):
    m = a.shape[-2]
    j, kw, ns = w4.shape
    tm, tn, tk = _tile(m, tm, 16), _tile(kw, tn), _tile(ns, tk)
    kps = ns // tk
    if a.ndim == 3:
        kph = a.shape[2] // tk
        a_spec = pl.BlockSpec((None, tm, tk), lambda ni, mi, ki: (ki // kph, mi, ki % kph))
    else:
        a_spec = pl.BlockSpec((tm, tk), lambda ni, mi, ki: (mi, ki))
    return _matmul(
        a, w4, (((1,), (1,)), ((), ())), jax.ShapeDtypeStruct((m, kw), out_dtype),
        (kw // tn, m // tm, j * kps),
        a_spec,
        pl.BlockSpec((None, tn, tk), lambda ni, mi, ki: (ki // kps, ni, ki % kps)),
        pl.BlockSpec((tm, tn), lambda ni, mi, ki: (mi, ni)), (tm, tn), 2, name, after)


def _mm_tn_cols(a, b, name, tm=1024, tn=1536, tk=2048):
    m, ka = a.shape
    ns = (b.shape[-1] * (2 if b.ndim == 3 else 1)) // 4
    hr = ka // 2
    tm, tn, tk = _tile(hr, tm), _tile(ns, tn), _tile(m, tk, 16)
    mph, nps = hr // tm, ns // tn
    if b.ndim == 3:
        b_spec = pl.BlockSpec((None, tk, tn), lambda ni, mi, ki: (ni // (2 * nps), ki, ni % (2 * nps)))
    else:
        b_spec = pl.BlockSpec((tk, tn), lambda ni, mi, ki: (ki, ni))
    return _matmul(
        a, b, (((0,), (0,)), ((), ())), jax.ShapeDtypeStruct((2, 4, hr, ns), _WIRE),
        (4 * nps, 2 * mph, m // tk),
        pl.BlockSpec((tk, tm), lambda ni, mi, ki: (ki, mi)),
        b_spec,
        pl.BlockSpec((None, None, tm, tn), lambda ni, mi, ki: (mi // mph, ni // nps, mi % mph, ni % nps)),
        (tm, tn), 2, name)


def _mm_tn_rows(a, b, name, tm=1536, tn=1024, tk=2048):
    m, ka = a.shape
    r = ka // 4
    hc = b.shape[1] // 2
    tm, tn, tk = _tile(r, tm), _tile(hc, tn), _tile(m, tk, 16)
    mpr, nph = r // tm, hc // tn
    return _matmul(
        a, b, (((0,), (0,)), ((), ())), jax.ShapeDtypeStruct((2, 4, r, hc), _WIRE),
        (2 * nph, 4 * mpr, m // tk),
        pl.BlockSpec((tk, tm), lambda ni, mi, ki: (ki, mi)),
        pl.BlockSpec((tk, tn), lambda ni, mi, ki: (ki, ni)),
        pl.BlockSpec((None, None, tm, tn), lambda ni, mi, ki: (ni // nph, mi // mpr, mi % mpr, ni % nph)),
        (tm, tn), 2, name)


def _row_call(body, name, rows, ins, outs, tm=256):
    tm = _tile(rows, tm, 16)

    def spec(shape, kind):
        if kind == "rows":
            return pl.BlockSpec((tm, shape[1]), lambda i: (i, 0))
        return pl.BlockSpec(shape, lambda i: (0,) * len(shape))

    return pl.pallas_call(
        body, name=name, grid=(rows // tm,),
        in_specs=[spec(a.shape, kind) for a, kind in ins],
        out_specs=[spec(o.shape, kind) for o, kind in outs],
        out_shape=[o for o, _ in outs],
        compiler_params=_cp("arbitrary"),
    )(*[a for a, _ in ins])


def _rms(x):
    r = lax.rsqrt(_rowmean(x * x) + EPS)
    return x * r, r


def _rms_bwd(dxh, xh, r):
    return r * (dxh - xh * _rowmean(dxh * xh))


def _fwd_pre_mix(x, g, sc, sh):
    s, d = x.shape

    def body(x_ref, g_ref, sc_ref, sh_ref, h_ref):
        xh, _ = _rms(x_ref[...])
        h_ref[...] = (xh * g_ref[...] * (1.0 + sc_ref[...]) + sh_ref[...]).astype(h_ref.dtype)

    return _row_call(body, "fwd_pre_mix", s, [(x, "rows"), (g, "vec"), (sc, "vec"), (sh, "vec")],
                     [(jax.ShapeDtypeStruct((s, d), _MXU), "rows")])[0]


def _fwd_mid(o, x, gt1, g_post, g_pre2, sc2, sh2):
    s, d = x.shape

    def body(o_ref, x_ref, gt_ref, gp_ref, g2_ref, sc_ref, sh_ref, x1_ref, h2_ref):
        oh, _ = _rms(o_ref[...])
        x1 = x_ref[...] + gt_ref[...] * (oh * gp_ref[...])
        x1_ref[...] = x1
        xh, _ = _rms(x1)
        h2_ref[...] = (xh * g2_ref[...] * (1.0 + sc_ref[...]) + sh_ref[...]).astype(h2_ref.dtype)

    return _row_call(body, "fwd_mid", s,
                     [(o, "rows"), (x, "rows"), (gt1, "vec"), (g_post, "vec"), (g_pre2, "vec"), (sc2, "vec"),
                      (sh2, "vec")],
                     [(jax.ShapeDtypeStruct((s, d), _F32), "rows"), (jax.ShapeDtypeStruct((s, d), _MXU), "rows")])


def _loss_and_post_ffn_bwd(f, x1, tgt, gt2, g_post):
    s, d = x1.shape

    def body(f_ref, x1_ref, t_ref, gt_ref, g_ref, dx2_ref, df_ref, dgt_ref, dg_ref, loss_ref):
        first = pl.program_id(0) == 0
        _zero_first(first, dgt_ref, dg_ref, loss_ref)
        fh, r = _rms(f_ref[...])
        n = fh * g_ref[...]
        e = x1_ref[...] + gt_ref[...] * n - t_ref[...]
        _acc(loss_ref, first, jnp.sum(_colsum(e * e), axis=1, keepdims=True) * (0.5 / d))
        dx2 = e * (1.0 / d)
        dx2_ref[...] = dx2
        _acc(dgt_ref, first, _colsum(dx2 * n))
        dn = dx2 * gt_ref[...]
        _acc(dg_ref, first, _colsum(dn * fh))
        df_ref[...] = _rms_bwd(dn * g_ref[...], fh, r).astype(df_ref.dtype)

    vec = jax.ShapeDtypeStruct((1, d), _F32)
    return _row_call(body, "loss_post_ffn_bwd", s,
                     [(f, "rows"), (x1, "rows"), (tgt, "rows"), (gt2, "vec"), (g_post, "vec")],
                     [(jax.ShapeDtypeStruct((s, d), _F32), "rows"), (jax.ShapeDtypeStruct((s, d), _MXU), "rows"),
                      (vec, "vec"), (vec, "vec"), (jax.ShapeDtypeStruct((1, 1), _F32), "vec")])


def _bwd_mid(dh2, x1, dx2, o, g_pre2, sc2, gt1, g_post):
    s, d = x1.shape

    def body(dh_ref, x1_ref, dx2_ref, o_ref, g2_ref, sc_ref, gt_ref, gp_ref,
             dx1_ref, do_ref, dsc_ref, dsh_ref, dg2_ref, dgt_ref, dgp_ref):
        first = pl.program_id(0) == 0
        _zero_first(first, dsc_ref, dsh_ref, dg2_ref, dgt_ref, dgp_ref)
        dh = dh_ref[...]
        xh, r = _rms(x1_ref[...])
        _acc(dsh_ref, first, _colsum(dh))
        _acc(dsc_ref, first, _colsum(dh * (xh * g2_ref[...])))
        dn = dh * (1.0 + sc_ref[...])
        _acc(dg2_ref, first, _colsum(dn * xh))
        dx1 = dx2_ref[...] + _rms_bwd(dn * g2_ref[...], xh, r)
        dx1_ref[...] = dx1
        oh, ro = _rms(o_ref[...])
        _acc(dgt_ref, first, _colsum(dx1 * (oh * gp_ref[...])))
        dno = dx1 * gt_ref[...]
        _acc(dgp_ref, first, _colsum(dno * oh))
        do_ref[...] = _rms_bwd(dno * gp_ref[...], oh, ro).astype(do_ref.dtype)

    vec = jax.ShapeDtypeStruct((1, d), _F32)
    return _row_call(body, "bwd_mid", s,
                     [(dh2, "rows"), (x1, "rows"), (dx2, "rows"), (o, "rows"), (g_pre2, "vec"), (sc2, "vec"),
                      (gt1, "vec"), (g_post, "vec")],
                     [(jax.ShapeDtypeStruct((s, d), _F32), "rows"), (jax.ShapeDtypeStruct((s, d), _MXU), "rows"),
                      (vec, "vec"), (vec, "vec"), (vec, "vec"), (vec, "vec"), (vec, "vec")])


def _bwd_pre_mix(dh1, x, dx1, g, sc1):
    s, d = x.shape

    def body(dh_ref, x_ref, dx1_ref, g_ref, sc_ref, dx_ref, dsc_ref, dsh_ref, dg_ref):
        first = pl.program_id(0) == 0
        _zero_first(first, dsc_ref, dsh_ref, dg_ref)
        dh = dh_ref[...]
        xh, r = _rms(x_ref[...])
        _acc(dsh_ref, first, _colsum(dh))
        _acc(dsc_ref, first, _colsum(dh * (xh * g_ref[...])))
        dn = dh * (1.0 + sc_ref[...])
        _acc(dg_ref, first, _colsum(dn * xh))
        dx_ref[...] = dx1_ref[...] + _rms_bwd(dn * g_ref[...], xh, r)

    vec = jax.ShapeDtypeStruct((1, d), _F32)
    return _row_call(body, "bwd_pre_mix", s,
                     [(dh1, "rows"), (x, "rows"), (dx1, "rows"), (g, "vec"), (sc1, "vec")],
                     [(jax.ShapeDtypeStruct((s, d), _F32), "rows"), (vec, "vec"), (vec, "vec"), (vec, "vec")])


def _mix_norm_fwd(y_ssm, y_sgu, g_ssm, g_sgu):
    s, h = y_ssm.shape

    def body(a_ref, b_ref, ga_ref, gb_ref, o_ref):
        ah, _ = _rms(a_ref[...])
        bh, _ = _rms(b_ref[...])
        o_ref[:, 0:h] = (ah * ga_ref[...]).astype(o_ref.dtype)
        o_ref[:, h:2 * h] = (bh * gb_ref[...]).astype(o_ref.dtype)

    return _row_call(body, "mix_norm_fwd", s, [(y_ssm, "rows"), (y_sgu, "rows"), (g_ssm, "vec"), (g_sgu, "vec")],
                     [(jax.ShapeDtypeStruct((s, 2 * h), _MXU), "rows")])[0]


def _mix_norm_bwd(dyc, y_ssm, y_sgu, g_ssm, g_sgu):
    s, h = y_ssm.shape

    def body(d_ref, a_ref, b_ref, ga_ref, gb_ref, da_ref, db_ref, dga_ref, dgb_ref):
        first = pl.program_id(0) == 0
        _zero_first(first, dga_ref, dgb_ref)
        for lo, y_ref, g_ref, dy_ref, dg_ref in ((0, a_ref, ga_ref, da_ref, dga_ref), (h, b_ref, gb_ref, db_ref, dgb_ref)):
            d = d_ref[:, lo:lo + h]
            yh, r = _rms(y_ref[...])
            _acc(dg_ref, first, _colsum(d * yh))
            dy_ref[...] = _rms_bwd(d * g_ref[...], yh, r)

    vec = jax.ShapeDtypeStruct((1, h), _F32)
    full = jax.ShapeDtypeStruct((s, h), _F32)
    return _row_call(body, "mix_norm_bwd", s,
                     [(dyc, "rows"), (y_ssm, "rows"), (y_sgu, "rows"), (g_ssm, "vec"), (g_sgu, "vec")],
                     [(full, "rows"), (full, "rows"), (vec, "vec"), (vec, "vec")])


CONV_ROWS = 64


def _conv_rows(ext, w_ref, b_ref):
    x = ext[SUBLANES:]
    s1 = pltpu.roll(ext, 1, 0)[SUBLANES:]
    s2 = pltpu.roll(ext, 2, 0)[SUBLANES:]
    return b_ref[...] + w_ref[0:1, :] * s2 + w_ref[1:2, :] * s1 + w_ref[2:3, :] * x, x, s1, s2


def _conv_window(x_ref, r0):
    if isinstance(r0, int):
        assert r0 == 0
        return jnp.concatenate([jnp.zeros((SUBLANES, x_ref.shape[1]), _F32), x_ref[0:CONV_ROWS, :]], axis=0)
    return x_ref[pl.ds(pl.multiple_of(r0 - SUBLANES, SUBLANES), CONV_ROWS + SUBLANES), :]


def _conv_act_fwd(up_pre, conv_w, conv_b):
    s, f2 = up_pre.shape
    f = f2 // 2
    tc = _tile(f, 256)
    nf = f // tc

    def shift_down(x, k):
        row = lax.broadcasted_iota(jnp.int32, x.shape, 0)
        return jnp.where(row >= k, pltpu.roll(x, k, 0), 0.0)

    def conv(x, w_ref, b_ref):
        return b_ref[...] + w_ref[0:1, :] * shift_down(x, 2) + w_ref[1:2, :] * shift_down(x, 1) + w_ref[2:3, :] * x

    def body(a_ref, b_ref, wa_ref, wb_ref, ba_ref, bb_ref, o_ref):
        a = conv(a_ref[...], wa_ref, ba_ref)
        b = conv(b_ref[...], wb_ref, bb_ref)
        o_ref[...] = (a * _sigmoid(a) * b).astype(o_ref.dtype)

    return pl.pallas_call(
        body, name="conv_act_fwd", grid=(nf,), out_shape=jax.ShapeDtypeStruct((s, f), _MXU),
        in_specs=[pl.BlockSpec((s, tc), lambda n: (0, n)), pl.BlockSpec((s, tc), lambda n: (0, n + nf)),
                  pl.BlockSpec((3, tc), lambda n: (0, n)), pl.BlockSpec((3, tc), lambda n: (0, n + nf)),
                  pl.BlockSpec((1, tc), lambda n: (0, n)), pl.BlockSpec((1, tc), lambda n: (0, n + nf))],
        out_specs=pl.BlockSpec((s, tc), lambda n: (0, n)), compiler_params=_cp("parallel"),
    )(up_pre, up_pre, conv_w, conv_w, conv_b, conv_b)


def _conv_act_bwd(up_pre, d_act, conv_w, conv_b):
    s, f2 = up_pre.shape
    f = f2 // 2
    tc = _tile(f, 256)
    nf = f // tc

    def body(a_ref, b_ref, d_ref, wa_ref, wb_ref, ba_ref, bb_ref,
             du_ref, w0a, w0b, w1a, w1b, w2a, w2b, dba, dbb):
        n = s // CONV_ROWS
        zero8 = jnp.zeros((SUBLANES, tc), _F32)
        ext_rows = CONV_ROWS + SUBLANES

        def fold(x):
            out = x[0:SUBLANES]
            for k in range(1, CONV_ROWS // SUBLANES):
                out = out + x[k * SUBLANES:(k + 1) * SUBLANES]
            return out

        def chunk(r0, carry):
            nxt, acc = carry
            a, xa, xa1, xa2 = _conv_rows(_conv_window(a_ref, r0), wa_ref, ba_ref)
            b, xb, xb1, xb2 = _conv_rows(_conv_window(b_ref, r0), wb_ref, bb_ref)
            sg = _sigmoid(a)
            d = d_ref[pl.ds(r0, CONV_ROWS), :]
            du_a = d * b * (sg * (1.0 + a * (1.0 - sg)))
            du_b = d * (a * sg)
            new_acc = []
            for h, (du, x0, x1, x2, w_ref) in enumerate(((du_a, xa, xa1, xa2, wa_ref), (du_b, xb, xb1, xb2, wb_ref))):
                ext = jnp.concatenate([du, nxt[h]], axis=0)
                u1 = pltpu.roll(ext, ext_rows - 1, 0)[:CONV_ROWS]
                u2 = pltpu.roll(ext, ext_rows - 2, 0)[:CONV_ROWS]
                du_ref[h, pl.ds(r0, CONV_ROWS), :] = (w_ref[2:3, :] * du + w_ref[1:2, :] * u1
                                                      + w_ref[0:1, :] * u2).astype(du_ref.dtype)
                new_acc += [acc[4 * h] + fold(du * x2), acc[4 * h + 1] + fold(du * x1), acc[4 * h + 2] + fold(du * x0),
                            acc[4 * h + 3] + fold(du)]
            return (du_a[:SUBLANES], du_b[:SUBLANES]), tuple(new_acc)

        def step(i, carry):
            return chunk(pl.multiple_of((n - 1 - i) * CONV_ROWS, CONV_ROWS), carry)

        carry = lax.fori_loop(0, n - 1, step, ((zero8, zero8), (zero8,) * 8))
        _, acc = chunk(0, carry)
        for ref, val in zip((w0a, w1a, w2a, dba, w0b, w1b, w2b, dbb), acc):
            ref[...] = _colsum(val)

    col_a = pl.BlockSpec((s, tc), lambda n: (0, n))
    col_b = pl.BlockSpec((s, tc), lambda n: (0, n + nf))
    vec_a = pl.BlockSpec((1, tc), lambda n: (0, n))
    vec_b = pl.BlockSpec((1, tc), lambda n: (0, n + nf))
    vec = jax.ShapeDtypeStruct((1, f), _F32)
    outs = pl.pallas_call(
        body, name="conv_act_bwd", grid=(nf,),
        in_specs=[col_a, col_b, col_a, pl.BlockSpec((3, tc), lambda n: (0, n)),
                  pl.BlockSpec((3, tc), lambda n: (0, n + nf)), vec_a, vec_b],
        out_specs=[pl.BlockSpec((2, s, tc), lambda n: (0, 0, n))] + [vec_a] * 8,
        out_shape=[jax.ShapeDtypeStruct((2, s, f), _MXU)] + [vec] * 8, compiler_params=_cp("parallel"),
    )(up_pre, up_pre, d_act, conv_w, conv_w, conv_b, conv_b)
    du, w0a, w0b, w1a, w1b, w2a, w2b, dba, dbb = outs
    cat = lambda p, q: jnp.concatenate([p, q], axis=1)
    return du, cat(w0a, w0b), cat(w1a, w1b), cat(w2a, w2b), cat(dba, dbb)


def _sgu_recompute(zu_ref, zv_ref, lng_ref, lnb_ref, wm_ref, bs_ref, nh):
    zu, zv = zu_ref[...], zv_ref[...]
    u = _gelu(zu)
    gv = _gelu(zv)
    xc = gv - _rowmean(gv)
    rs = lax.rsqrt(_rowmean(xc * xc) + EPS)
    vh = xc * rs
    v = vh * lng_ref[...] + lnb_ref[...]
    mixed = []
    for h in range(nh):
        vhd = v[:, h * CHUNK:(h + 1) * CHUNK].astype(_MXU)
        mixed.append(jnp.dot(wm_ref[h].astype(_MXU), vhd, preferred_element_type=_F32) + bs_ref[h])
    return zu, zv, u, vh, rs, v, mixed


def _sgu_fwd(z, ln_g, ln_b, wm, bs):
    s = z.shape[0]
    nh = wm.shape[0]
    hd = nh * CHUNK

    def body(zu_ref, zv_ref, lng_ref, lnb_ref, wm_ref, bs_ref, y_ref):
        _, _, u, _, _, _, mixed = _sgu_recompute(zu_ref, zv_ref, lng_ref, lnb_ref, wm_ref, bs_ref, nh)
        for h in range(nh):
            y_ref[:, h * CHUNK:(h + 1) * CHUNK] = u[:, h * CHUNK:(h + 1) * CHUNK] * mixed[h]

    vec = pl.BlockSpec((1, hd), lambda i: (0, 0))
    return pl.pallas_call(
        body, name="sgu_fwd", grid=(s // CHUNK,), out_shape=jax.ShapeDtypeStruct((s, hd), _F32),
        in_specs=[pl.BlockSpec((CHUNK, hd), lambda i: (i, 1)), pl.BlockSpec((CHUNK, hd), lambda i: (i, 2)), vec, vec,
                  pl.BlockSpec((nh, CHUNK, CHUNK), lambda i: (0, 0, 0)), pl.BlockSpec((nh, CHUNK, 1), lambda i: (0, 0, 0))],
        out_specs=pl.BlockSpec((CHUNK, hd), lambda i: (i, 0)), compiler_params=_cp("parallel"),
    )(z, z, ln_g, ln_b, wm, bs)


def _sgu_bwd(z, dy, dz_ssm, ln_g, ln_b, wm, bs):
    s = z.shape[0]
    nh = wm.shape[0]
    hd = nh * CHUNK

    def body(zu_ref, zv_ref, dy_ref, dzs_ref, lng_ref, lnb_ref, wm_ref, bs_ref,
             dz_ref, dlg_ref, dlb_ref, dwm_ref, dbs_ref, dv_scr):
        first = pl.program_id(0) == 0
        _zero_first(first, dlg_ref, dlb_ref, dwm_ref, dbs_ref)
        zu, zv, u, vh, rs, v, mixed = _sgu_recompute(zu_ref, zv_ref, lng_ref, lnb_ref, wm_ref, bs_ref, nh)
        dy = dy_ref[...]
        dz_ref[:, 0:hd] = dzs_ref[...]
        for h in range(nh):
            cols = slice(h * CHUNK, (h + 1) * CHUNK)
            dyh = dy[:, cols]
            dz_ref[:, hd + h * CHUNK:hd + (h + 1) * CHUNK] = (dyh * mixed[h] * _gelu_grad(zu[:, cols])).astype(dz_ref.dtype)
            dm = dyh * u[:, cols]
            dmx = dm.astype(_MXU)
            _acc(dbs_ref.at[h], first, jnp.sum(dm, axis=1, keepdims=True))
            _acc(dwm_ref.at[h], first,
                 lax.dot_general(dmx, v[:, cols].astype(_MXU), (((1,), (1,)), ((), ())), preferred_element_type=_F32))
            dv_scr[:, cols] = lax.dot_general(wm_ref[h].astype(_MXU), dmx, (((0,), (0,)), ((), ())),
                                              preferred_element_type=_F32)
        dv = dv_scr[...]
        _acc(dlg_ref, first, _colsum(dv * vh))
        _acc(dlb_ref, first, _colsum(dv))
        dvh = dv * lng_ref[...]
        dgv = rs * (dvh - _rowmean(dvh) - vh * _rowmean(dvh * vh))
        dz_ref[:, 2 * hd:3 * hd] = (dgv * _gelu_grad(zv)).astype(dz_ref.dtype)

    vec = pl.BlockSpec((1, hd), lambda i: (0, 0))
    wspec = pl.BlockSpec((nh, CHUNK, CHUNK), lambda i: (0, 0, 0))
    bspec = pl.BlockSpec((nh, CHUNK, 1), lambda i: (0, 0, 0))
    rows = pl.BlockSpec((CHUNK, hd), lambda i: (i, 0))
    return pl.pallas_call(
        body, name="sgu_bwd", grid=(s // CHUNK,),
        out_shape=[jax.ShapeDtypeStruct((s, 3 * hd), _MXU), jax.ShapeDtypeStruct((1, hd), _F32),
                   jax.ShapeDtypeStruct((1, hd), _F32), jax.ShapeDtypeStruct((nh, CHUNK, CHUNK), _F32),
                   jax.ShapeDtypeStruct((nh, CHUNK, 1), _F32)],
        in_specs=[pl.BlockSpec((CHUNK, hd), lambda i: (i, 1)), pl.BlockSpec((CHUNK, hd), lambda i: (i, 2)),
                  rows, rows, vec, vec, wspec, bspec],
        out_specs=[pl.BlockSpec((CHUNK, 3 * hd), lambda i: (i, 0)), vec, vec, wspec, bspec],
        scratch_shapes=[pltpu.VMEM((CHUNK, hd), _F32)], compiler_params=_cp("arbitrary"),
    )(z, z, dy, dz_ssm, ln_g, ln_b, wm, bs)


def _ssm_prep(log_dt, a_re, a_im, b_re_t, b_im_t):
    gn = a_re.shape[1]

    def body(ldt_ref, are_ref, aim_ref, br_ref, bi_ref, pr_ref, pi_ref, bbr_ref, bbi_ref):
        dt = jnp.exp(ldt_ref[...])
        are, aim = are_ref[...], aim_ref[...]
        k = (lax.broadcasted_iota(jnp.int32, (SUBLANES, gn), 0) + 1).astype(_F32)
        mag = jnp.exp(k * (are * dt))
        ang = k * (aim * dt)
        pr_ref[...] = mag * jnp.cos(ang)
        pi_ref[...] = mag * jnp.sin(ang)
        m1 = jnp.exp(are * dt)
        lr, li = m1 * jnp.cos(aim * dt), m1 * jnp.sin(aim * dt)
        den = are * are + aim * aim
        nr = lr - 1.0
        f_re = (nr * are + li * aim) / den
        f_im = (li * are - nr * aim) / den
        bbr_ref[...] = f_re * br_ref[...] - f_im * bi_ref[...]
        bbi_ref[...] = f_re * bi_ref[...] + f_im * br_ref[...]

    pw = jax.ShapeDtypeStruct((SUBLANES, gn), _F32)
    bb = jax.ShapeDtypeStruct(b_re_t.shape, _F32)
    return pl.pallas_call(body, name="ssm_prep", out_shape=[pw, pw, bb, bb])(log_dt, a_re, a_im, b_re_t, b_im_t)


def _ssm_prep_bwd(log_dt, a_re, a_im, b_re_t, b_im_t, d_bbr, d_bbi, d_lr, d_li):
    def body(ldt_ref, are_ref, aim_ref, br_ref, bi_ref, dbr_ref, dbi_ref, dlr_ref, dli_ref,
             obr_ref, obi_ref, oar_ref, oai_ref, odt_ref):
        dt = jnp.exp(ldt_ref[...])
        are, aim = are_ref[...], aim_ref[...]
        m1 = jnp.exp(are * dt)
        lr, li = m1 * jnp.cos(aim * dt), m1 * jnp.sin(aim * dt)
        den = are * are + aim * aim
        nr = lr - 1.0
        f_re = (nr * are + li * aim) / den
        f_im = (li * are - nr * aim) / den
        br, bi, dbr, dbi = br_ref[...], bi_ref[...], dbr_ref[...], dbi_ref[...]
        obr_ref[...] = f_re * dbr + f_im * dbi
        obi_ref[...] = f_re * dbi - f_im * dbr
        gf_re = _colsum(br * dbr + bi * dbi)
        gf_im = _colsum(br * dbi - bi * dbr)
        il_re, il_im = are / den, -aim / den
        glb_re = dlr_ref[...] + (il_re * gf_re + il_im * gf_im)
        glb_im = dli_ref[...] + (il_re * gf_im - il_im * gf_re)
        q_re = -(f_re * il_re - f_im * il_im)
        q_im = -(f_re * il_im + f_im * il_re)
        gl_re = q_re * gf_re + q_im * gf_im
        gl_im = q_re * gf_im - q_im * gf_re
        gl_re = gl_re + dt * (lr * glb_re + li * glb_im)
        gl_im = gl_im + dt * (lr * glb_im - li * glb_re)
        w_re = are * lr - aim * li
        w_im = are * li + aim * lr
        oar_ref[...] = gl_re
        oai_ref[...] = gl_im
        odt_ref[...] = w_re * glb_re + w_im * glb_im

    bb = jax.ShapeDtypeStruct(b_re_t.shape, _F32)
    v = jax.ShapeDtypeStruct(a_re.shape, _F32)
    return pl.pallas_call(body, name="ssm_prep_bwd", out_shape=[bb, bb, v, v, v])(
        log_dt, a_re, a_im, b_re_t, b_im_t, d_bbr, d_bbi, d_lr, d_li)


def _group_sum(d_dt, log_dt):
    def body(d_ref, l_ref, o_ref):
        o_ref[...] = jnp.sum(d_ref[...], axis=1, keepdims=True) * jnp.exp(l_ref[...])

    return pl.pallas_call(body, name="ssm_dt_grad", out_shape=jax.ShapeDtypeStruct(log_dt.shape, _F32))(d_dt, log_dt)


def _scan_rows(src_ref, dst_ref, nrt, steps, ptab, carry0, reverse):
    ns = BLOCK_ST
    row = lax.broadcasted_iota(jnp.int32, (SUBLANES, ns), 0)
    pr, pi = ptab

    def body(i, carry):
        cr, ci = carry
        it = (nrt - 1 - i) if reverse else i
        r0 = pl.multiple_of(it * SUBLANES, SUBLANES)
        xr = src_ref[pl.ds(r0, SUBLANES), 0:ns]
        xi = src_ref[pl.ds(r0, SUBLANES), ns:2 * ns]
        for k, (ar, ai) in zip((1, 2, 4), steps):
            if reverse:
                keep = row < SUBLANES - k
                sr = jnp.where(keep, pltpu.roll(xr, SUBLANES - k, 0), 0.0)
                si = jnp.where(keep, pltpu.roll(xi, SUBLANES - k, 0), 0.0)
            else:
                keep = row >= k
                sr = jnp.where(keep, pltpu.roll(xr, k, 0), 0.0)
                si = jnp.where(keep, pltpu.roll(xi, k, 0), 0.0)
            xr, xi = xr + ar * sr - ai * si, xi + ar * si + ai * sr
        xr, xi = xr + pr * cr - pi * ci, xi + pr * ci + pi * cr
        dst_ref[pl.ds(r0, SUBLANES), 0:ns] = xr
        dst_ref[pl.ds(r0, SUBLANES), ns:2 * ns] = xi
        if reverse:
            return xr[0:1, :], xi[0:1, :]
        return xr[SUBLANES - 1:SUBLANES, :], xi[SUBLANES - 1:SUBLANES, :]

    return lax.fori_loop(0, nrt, body, carry0)


def _scan_consts(p_ref, conj):
    ns = BLOCK_ST
    sign = -1.0 if conj else 1.0
    bc = lambda r: jnp.broadcast_to(r, (SUBLANES, ns))
    steps = [(bc(p_ref[k - 1:k, 0:ns]), bc(sign * p_ref[k - 1:k, ns:2 * ns])) for k in (1, 2, 4)]
    return steps


def _ssm_block_fwd(u, bbt_ref, ct_ref, d_ref, wg_ref, bg_ref, p_ref, bu_scr, h_scr, carry_in, nrt):
    ns = BLOCK_ST
    bu_scr[...] = jnp.dot(u.astype(_MXU), bbt_ref[...].astype(_MXU), preferred_element_type=_F32)
    ptab = (p_ref[:, 0:ns], p_ref[:, ns:2 * ns])
    carry = _scan_rows(bu_scr, h_scr, nrt, _scan_consts(p_ref, False), ptab, carry_in, False)
    y = jnp.dot(h_scr[...].astype(_MXU), ct_ref[...].astype(_MXU), preferred_element_type=_F32) + d_ref[...] * u
    yg = _gelu(y)
    gate = _sigmoid(jnp.dot(yg.astype(_MXU), wg_ref[...].astype(_MXU), preferred_element_type=_F32) + bg_ref[...])
    return y, yg, gate, carry


def _ssm_specs(nb, nt, t, reverse):
    tt = (lambda ti: nt - 1 - ti) if reverse else (lambda ti: ti)
    ns2 = 2 * BLOCK_ST
    return dict(
        z=pl.BlockSpec((t, BLOCK_CH), lambda b, ti: (tt(ti), b)),
        bbt=pl.BlockSpec((None, BLOCK_CH, ns2), lambda b, ti: (b, 0, 0)),
        ct=pl.BlockSpec((None, ns2, BLOCK_CH), lambda b, ti: (b, 0, 0)),
        vec=pl.BlockSpec((1, BLOCK_CH), lambda b, ti: (0, b)),
        wg=pl.BlockSpec((None, BLOCK_CH, BLOCK_CH), lambda b, ti: (b, 0, 0)),
        p=pl.BlockSpec((None, SUBLANES, ns2), lambda b, ti: (b, 0, 0)),
        hb=pl.BlockSpec((None, None, SUBLANES, ns2), lambda b, ti: (b, tt(ti), 0, 0)),
        h=pl.BlockSpec((None, t, ns2), lambda b, ti: (b, tt(ti), 0)),
        acc_vec=pl.BlockSpec((None, 1, ns2), lambda b, ti: (b, 0, 0)),
    )


def _ssm_fwd(z, bbt, ct, dvec, wg, bglu, ptab):
    s = z.shape[0]
    nb = bbt.shape[0]
    t = _tile(s, TIME_TILE, SUBLANES)
    nt = s // t
    ns = BLOCK_ST
    sp = _ssm_specs(nb, nt, t, False)

    def body(z_ref, bbt_ref, ct_ref, d_ref, wg_ref, bg_ref, p_ref, y2_ref, y_ref, h_ref, hb_ref, bu_scr, h_scr, carry_scr):
        ti = pl.program_id(1)

        @pl.when(ti == 0)
        def _():
            carry_scr[...] = jnp.zeros_like(carry_scr)

        hb_ref[...] = carry_scr[...]
        carry_in = (carry_scr[0:1, 0:ns], carry_scr[0:1, ns:2 * ns])
        y, yg, gate, (cr, ci) = _ssm_block_fwd(z_ref[...], bbt_ref, ct_ref, d_ref, wg_ref, bg_ref, p_ref,
                                               bu_scr, h_scr, carry_in, t // SUBLANES)
        y2_ref[...] = yg * gate
        y_ref[...] = y
        h_ref[...] = h_scr[...].astype(h_ref.dtype)
        carry_scr[:, 0:ns] = jnp.broadcast_to(cr, (SUBLANES, ns))
        carry_scr[:, ns:2 * ns] = jnp.broadcast_to(ci, (SUBLANES, ns))

    ych = jax.ShapeDtypeStruct((s, nb * BLOCK_CH), _F32)
    return pl.pallas_call(
        body, name="ssm_fwd", grid=(nb, nt),
        out_shape=[ych, ych, jax.ShapeDtypeStruct((nb, s, 2 * ns), _MXU),
                   jax.ShapeDtypeStruct((nb, nt, SUBLANES, 2 * ns), _F32)],
        in_specs=[sp["z"], sp["bbt"], sp["ct"], sp["vec"], sp["wg"], sp["vec"], sp["p"]],
        out_specs=[sp["z"], sp["z"], sp["h"], sp["hb"]],
        scratch_shapes=[pltpu.VMEM((t, 2 * ns), _F32), pltpu.VMEM((t, 2 * ns), _F32), pltpu.VMEM((SUBLANES, 2 * ns), _F32)],
        compiler_params=_cp("parallel", "arbitrary"),
    )(z, bbt, ct, dvec, wg, bglu, ptab)


def _ssm_bwd(z, y_pre, h_all, dy2, hb, bbt, ct, dvec, wg, bglu, ptab, ptab_rev):
    s = z.shape[0]
    nb = bbt.shape[0]
    t = _tile(s, TIME_TILE, SUBLANES)
    nt = s // t
    ns = BLOCK_ST
    sp = _ssm_specs(nb, nt, t, True)
    tn_dims = (((0,), (0,)), ((), ()))
    nt_dims = (((1,), (1,)), ((), ()))

    def body(z_ref, y_ref, h_ref, dy2_ref, hb_ref, bbt_ref, ct_ref, d_ref, wg_ref, bg_ref, p_ref, pr_ref,
             dz_ref, dbbt_ref, dct_ref, dwg_ref, dlb_ref, dd_ref, dbg_ref, bu_scr, g_scr, gcarry_scr):
        first = pl.program_id(1) == 0

        _zero_first(first, gcarry_scr, dbbt_ref, dct_ref, dwg_ref, dlb_ref, dd_ref, dbg_ref)
        u = z_ref[...]
        hin = hb_ref[...]
        y = y_ref[...]
        yg = _gelu(y)
        gate = _sigmoid(jnp.dot(yg.astype(_MXU), wg_ref[...].astype(_MXU), preferred_element_type=_F32) + bg_ref[...])
        dy2 = dy2_ref[...]
        dpre = dy2 * yg * gate * (1.0 - gate)
        _acc(dbg_ref, first, _colsum(dpre))
        dpx = dpre.astype(_MXU)
        _acc(dwg_ref, first, lax.dot_general(yg.astype(_MXU), dpx, tn_dims, preferred_element_type=_F32))
        dyg = dy2 * gate + lax.dot_general(dpx, wg_ref[...].astype(_MXU), nt_dims, preferred_element_type=_F32)
        dy = dyg * _gelu_grad(y)
        _acc(dd_ref, first, _colsum(dy * u))
        dyx = dy.astype(_MXU)
        hx = h_ref[...]
        h = hx.astype(_F32)
        _acc(dct_ref, first, lax.dot_general(hx, dyx, tn_dims, preferred_element_type=_F32))
        bu_scr[...] = lax.dot_general(dyx, ct_ref[...].astype(_MXU), nt_dims, preferred_element_type=_F32)
        gin = (gcarry_scr[0:1, 0:ns], gcarry_scr[0:1, ns:2 * ns])
        ptab = (pr_ref[:, 0:ns], pr_ref[:, ns:2 * ns])
        gr, gi = _scan_rows(bu_scr, g_scr, t // SUBLANES, _scan_consts(p_ref, True), ptab, gin, True)
        gcarry_scr[:, 0:ns] = jnp.broadcast_to(gr, (SUBLANES, ns))
        gcarry_scr[:, ns:2 * ns] = jnp.broadcast_to(gi, (SUBLANES, ns))
        g = g_scr[...]
        row = lax.broadcasted_iota(jnp.int32, (t, ns), 0)
        hp_re = jnp.where(row == 0, hin[0:1, 0:ns], pltpu.roll(h[:, 0:ns], 1, 0))
        hp_im = jnp.where(row == 0, hin[0:1, ns:2 * ns], pltpu.roll(h[:, ns:2 * ns], 1, 0))
        g_re, g_im = g[:, 0:ns], g[:, ns:2 * ns]
        d_ar = _colsum(g_re * hp_re + g_im * hp_im)
        d_ai = _colsum(g_im * hp_re - g_re * hp_im)
        _acc(dlb_ref, first, jnp.concatenate([d_ar, d_ai], axis=1))
        gx = g.astype(_MXU)
        _acc(dbbt_ref, first, lax.dot_general(u.astype(_MXU), gx, tn_dims, preferred_element_type=_F32))
        dz_ref[...] = (dy * d_ref[...] + lax.dot_general(gx, bbt_ref[...].astype(_MXU), nt_dims,
                                                         preferred_element_type=_F32)).astype(dz_ref.dtype)

    f = lambda shape: jax.ShapeDtypeStruct(shape, _F32)
    return pl.pallas_call(
        body, name="ssm_bwd", grid=(nb, nt),
        out_shape=[jax.ShapeDtypeStruct((s, nb * BLOCK_CH), _MXU), f(bbt.shape), f(ct.shape), f(wg.shape), f((nb, 1, 2 * ns)),
                   f((1, nb * BLOCK_CH)), f((1, nb * BLOCK_CH))],
        in_specs=[sp["z"], sp["z"], sp["h"], sp["z"], sp["hb"], sp["bbt"], sp["ct"], sp["vec"], sp["wg"], sp["vec"], sp["p"],
                  sp["p"]],
        out_specs=[sp["z"], sp["bbt"], sp["ct"], sp["wg"], sp["acc_vec"], sp["vec"], sp["vec"]],
        scratch_shapes=[pltpu.VMEM((t, 2 * ns), _F32), pltpu.VMEM((t, 2 * ns), _F32), pltpu.VMEM((SUBLANES, 2 * ns), _F32)],
        compiler_params=_cp("parallel", "arbitrary"),
    )(z, y_pre, h_all, dy2, hb, bbt, ct, dvec, wg, bglu, ptab, ptab_rev)


def _mod_part(c_all, w, b):
    d, ns = w.shape
    tn = _tile(ns, 512)

    def body(c_ref, w_ref, b_ref, o_ref):
        c = c_ref[...]
        ca = (c * _sigmoid(c)).astype(_MXU)
        o_ref[...] = jnp.dot(ca, w_ref[...].astype(_MXU), preferred_element_type=_F32) + b_ref[...]

    return pl.pallas_call(
        body, name="mod_part", grid=(ns // tn,), out_shape=jax.ShapeDtypeStruct((8, ns), _F32),
        in_specs=[pl.BlockSpec((8, d), lambda n: (0, 0)), pl.BlockSpec((d, tn), lambda n: (0, n)),
                  pl.BlockSpec((1, tn), lambda n: (0, n))],
        out_specs=pl.BlockSpec((8, tn), lambda n: (0, n)), compiler_params=_cp("parallel"),
    )(c_all, w, b)


def _adamw_math(w, g, m, v):
    m = ADAM_B1 * m + (1.0 - ADAM_B1) * g
    v = ADAM_B2 * v + (1.0 - ADAM_B2) * (g * g)
    m_hat = m / (1.0 - ADAM_B1 ** ADAM_STEP)
    v_hat = v / (1.0 - ADAM_B2 ** ADAM_STEP)
    delta = -ADAM_LR * (m_hat / (jnp.sqrt(v_hat) + ADAM_EPS) + ADAM_WD * w)
    return delta, m, v


def _adamw(w, g, m, v, name):
    r, c = w.shape
    tc = c if c <= 4096 else _tile(c, 4096)
    tr = _tile(r, max(SUBLANES, (1 << 18) // tc), SUBLANES)

    def body(w_ref, g_ref, m_ref, v_ref, go_ref, d_ref, mo_ref, vo_ref):
        g = g_ref[...]
        go_ref[...] = g
        d_ref[...], mo_ref[...], vo_ref[...] = _adamw_math(w_ref[...], g, m_ref[...], v_ref[...])

    spec = pl.BlockSpec((tr, tc), lambda i, j: (i, j))
    out = jax.ShapeDtypeStruct((r, c), _F32)
    return pl.pallas_call(
        body, name=name, grid=(r // tr, c // tc), in_specs=[spec] * 4, out_specs=[spec] * 4, out_shape=[out] * 4,
        compiler_params=_cp("parallel", "parallel"),
    )(w, g, m, v)


def _adamw_halves(w, g2, m, v, name):
    r, c = w.shape
    tr, tc = _tile(r, 256, SUBLANES), _tile(c // 2, 1024)
    nph = (c // 2) // tc

    def body(w_ref, g_ref, m_ref, v_ref, go_ref, d_ref, mo_ref, vo_ref):
        g = g_ref[...]
        go_ref[...] = g
        d_ref[...], mo_ref[...], vo_ref[...] = _adamw_math(w_ref[...], g, m_ref[...], v_ref[...])

    spec = pl.BlockSpec((tr, tc), lambda i, j: (i, j))
    out = jax.ShapeDtypeStruct((r, c), _F32)
    return pl.pallas_call(
        body, name=name, grid=(r // tr, c // tc),
        in_specs=[spec, pl.BlockSpec((None, tr, tc), lambda i, j: (j // nph, i, j % nph)), spec, spec],
        out_specs=[spec] * 4, out_shape=[out] * 4, compiler_params=_cp("parallel", "parallel"),
    )(w, g2, m, v)


def _wada_update(c_t, dm, w, m, v):
    d, ns = w.shape
    tr, tc = _tile(d, 256, SUBLANES), _tile(ns, 1024)

    def body(c_ref, dm_ref, w_ref, m_ref, v_ref, g_ref, d_ref, mo_ref, vo_ref):
        c = c_ref[...]
        ca = c * _sigmoid(c)
        dmv = dm_ref[...]
        g = ca[:, 0:1] * dmv[0:1, :]
        for b in range(1, 8):
            g = g + ca[:, b:b + 1] * dmv[b:b + 1, :]
        g_ref[...] = g
        d_ref[...], mo_ref[...], vo_ref[...] = _adamw_math(w_ref[...], g, m_ref[...], v_ref[...])

    spec = pl.BlockSpec((tr, tc), lambda i, j: (i, j))
    out = jax.ShapeDtypeStruct((d, ns), _F32)
    return pl.pallas_call(
        body, name="wada_update", grid=(d // tr, ns // tc),
        in_specs=[pl.BlockSpec((tr, 8), lambda i, j: (i, 0)), pl.BlockSpec((8, tc), lambda i, j: (0, j)), spec, spec, spec],
        out_specs=[spec] * 4, out_shape=[out] * 4, compiler_params=_cp("parallel", "parallel"),
    )(c_t, dm, w, m, v)


def _small_reduce(gathered):
    _, r, c = gathered.shape
    tr = _tile(r, 512, SUBLANES)

    def body(q_ref, g_ref):
        g = q_ref[0]
        for k in range(1, 8):
            g = g + q_ref[k]
        g_ref[...] = g

    return pl.pallas_call(
        body, name="small_reduce", grid=(r // tr,), out_shape=jax.ShapeDtypeStruct((r, c), _F32),
        in_specs=[pl.BlockSpec((8, tr, c), lambda i: (0, i, 0))], out_specs=pl.BlockSpec((tr, c), lambda i: (i, 0)),
        compiler_params=_cp("parallel"),
    )(gathered)


def _adamw_many(ws, gs, ms, vs, steps, name):
    n = len(ws)

    def body(*refs):
        w_refs, g_refs, m_refs, v_refs = refs[0:n], refs[n:2 * n], refs[2 * n:3 * n], refs[3 * n:4 * n]
        d_refs, mo_refs, vo_refs = refs[4 * n:5 * n], refs[5 * n:6 * n], refs[6 * n:7 * n]
        for i in range(n):
            d_refs[i][...], mo_refs[i][...], vo_refs[i][...] = _adamw_math(
                w_refs[i][...], g_refs[i][...], m_refs[i][...], v_refs[i][...])

    def spec(a):
        nd = a.ndim
        if steps == 1:
            return pl.BlockSpec(a.shape, lambda i: (0,) * nd)
        return pl.BlockSpec((a.shape[0] // steps,) + a.shape[1:], lambda i: (i,) + (0,) * (nd - 1))

    specs = [spec(w) for w in ws]
    outs = pl.pallas_call(
        body, name=name, grid=(steps,), in_specs=specs * 4, out_specs=specs * 3,
        out_shape=[jax.ShapeDtypeStruct(w.shape, _F32) for w in ws] * 3, compiler_params=_cp("parallel"),
    )(*ws, *gs, *ms, *vs)
    return outs[0:n], outs[n:2 * n], outs[2 * n:3 * n]


def _block_diag(x, eye=None):
    nb, g, p, q = x.shape
    eye = jnp.eye(g, dtype=x.dtype) if eye is None else eye
    return (x[:, :, :, None, :] * eye[None, :, None, :, None]).reshape(nb, g * p, g * q)


def _block_diag_take(x, p, q):
    nb = x.shape[0]
    g = GROUPS_PER_BLOCK
    eye = jnp.eye(g, dtype=x.dtype)
    return jnp.sum(x.reshape(nb, g, p, g, q) * eye[None, :, None, :, None], axis=3)


_VIEWS = {"ssm_b_re": ((0, 2, 1), (0, 2, 1)), "ssm_b_im": ((0, 2, 1), (0, 2, 1)),
          "ssm_w_glu": ((1, 2, 0), (2, 0, 1)), "ssm_b_glu": ((1, 0), (1, 0))}


def _to_view(name, a):
    return a.transpose(_VIEWS[name][0]) if name in _VIEWS else a


def _from_view(name, a):
    return a.transpose(_VIEWS[name][1]) if name in _VIEWS else a


class _Pack:
    def __init__(self, shapes):
        self.shapes = shapes
        self.offsets = {}
        off = 0
        for name, shape in shapes.items():
            n = math.prod(shape)
            self.offsets[name] = (off, n)
            off += -(-n // (SUBLANES * LANES)) * (SUBLANES * LANES)
        self.rows = -(-off // (256 * LANES)) * 256

    def pack(self, arrays):
        parts = []
        off = 0
        for name, shape in self.shapes.items():
            start, n = self.offsets[name]
            if start > off:
                parts.append(jnp.zeros((start - off,), _F32))
            parts.append(arrays[name].reshape(-1).astype(_F32))
            off = start + n
        total = self.rows * LANES
        if total > off:
            parts.append(jnp.zeros((total - off,), _F32))
        return jnp.concatenate(parts).reshape(self.rows, LANES)

    def unpack(self, buf):
        flat = buf.reshape(-1)
        return {name: flat[start:start + n].reshape(self.shapes[name]) for name, (start, n) in self.offsets.items()}


_SMALL = ["b_ada", "g_pre_mix", "g_post_mix", "ssm_log_dt", "ssm_a_re", "ssm_a_im", "ssm_b_re", "ssm_b_im", "ssm_c_re",
          "ssm_c_im", "ssm_d", "ssm_w_glu", "ssm_b_glu", "sgu_ln_g", "sgu_ln_b", "sgu_w", "sgu_b", "g_out_ssm",
          "g_out_sgu", "g_pre_ffn", "g_post_ffn", "conv_b"]
_WEIGHTS = ["w_ada", "b_ada", "g_pre_mix", "g_post_mix", "w_in", "ssm_log_dt", "ssm_a_re", "ssm_a_im", "ssm_b_re",
            "ssm_b_im", "ssm_c_re", "ssm_c_im", "ssm_d", "ssm_w_glu", "ssm_b_glu", "sgu_ln_g", "sgu_ln_b", "sgu_w", "sgu_b",
            "g_out_ssm", "g_out_sgu", "w_out", "g_pre_ffn", "g_post_ffn", "w_up", "conv_w", "conv_b", "w_down"]


def _step(p, m, v, x, c, tgt):
    s, d = x.shape
    mx, my, mc = lax.axis_index("x"), lax.axis_index("y"), lax.axis_index("c")
    chip = 2 * mx + my
    dev = 4 * mx + 2 * my + mc
    sel = jnp.stack([chip, mc]).astype(jnp.int32)
    g_cnt, n_st = p["ssm_a_re"].shape
    nb = g_cnt // GROUPS_PER_BLOCK
    gn = g_cnt * n_st
    d_ssm = g_cnt * SSM_GROUP
    nh = p["sgu_w"].shape[0]
    assert nh * CHUNK == d_ssm and 2 * d_ssm == d and n_st == SSM_STATE

    shards = lambda g: g.reshape(4, g.shape[1] * g.shape[2], g.shape[3])
    buf_in = _cast_into_slot(p["w_in"], sel, sel, "cast_w_in")

    ns_ada = p["w_ada"].shape[1]
    nc_conv = p["conv_w"].shape[1]
    first = jnp.concatenate([jnp.broadcast_to(c, (8, d)), jnp.pad(p["conv_w"], ((0, 5), (0, 0)))], axis=1)
    first_all = _all_gather8(_own_slot(first, dev), "gather_c_conv")
    c_all = first_all[:, 0, :d]
    conv_w_full = jnp.concatenate([first_all[2 * j, 0:3, d:] for j in range(4)], axis=1)
    b_ada_mine = lax.dynamic_slice_in_dim(p["b_ada"], chip * ns_ada, ns_ada, axis=1)
    mod_all = _all_gather8(_own_slot(_mod_part(c_all, p["w_ada"], b_ada_mine), dev), "gather_mod")
    (sems_in,), (buf_in,), tok = _gather_start([buf_in], mod_all, "gather_start_in")
    buf_out, buf_up, buf_down = [_cast_into_slot(p[n], sel, tok, "cast_" + n) for n in ("w_out", "w_up", "w_down")]
    mod_rows = lax.dynamic_index_in_dim(mod_all, dev, axis=1, keepdims=False)
    mod = jnp.concatenate([mod_rows[0], mod_rows[2], mod_rows[4], mod_rows[6]]).reshape(N_MOD, 1, d)
    sh1, sc1, gt1, sh2, sc2, gt2 = [mod[i] for i in range(N_MOD)]

    eye_t = jnp.eye(GROUPS_PER_BLOCK, dtype=_F32) + tok[0:1, 0:1]
    ldt_l = _after(jnp.repeat(p["ssm_log_dt"], n_st, axis=1), tok)
    are_l, aim_l = p["ssm_a_re"].reshape(1, gn), p["ssm_a_im"].reshape(1, gn)
    bre_t, bim_t = p["ssm_b_re"].reshape(gn, SSM_GROUP).T, p["ssm_b_im"].reshape(gn, SSM_GROUP).T
    pw_re, pw_im, bb_re, bb_im = _ssm_prep(ldt_l, are_l, aim_l, bre_t, bim_t)
    blocks = lambda t: t.reshape(t.shape[0], nb, GROUPS_PER_BLOCK * n_st).transpose(1, 0, 2)
    ptab = jnp.concatenate([blocks(pw_re), blocks(pw_im)], axis=2)
    ptab_rev = jnp.concatenate([blocks(pw_re)[:, ::-1], -blocks(pw_im)[:, ::-1]], axis=2)
    bd = lambda t: t.reshape(SSM_GROUP, nb, GROUPS_PER_BLOCK, n_st).transpose(1, 2, 0, 3)
    bbt = jnp.concatenate([_block_diag(bd(bb_re)), _block_diag(bd(bb_im))], axis=2).astype(_MXU)
    cd = lambda t: t.reshape(nb, GROUPS_PER_BLOCK, SSM_GROUP, n_st).transpose(0, 1, 3, 2)
    ct = jnp.concatenate([_block_diag(cd(p["ssm_c_re"]), eye_t), -_block_diag(cd(p["ssm_c_im"]), eye_t)], axis=1).astype(_MXU)
    wg = _block_diag(p["ssm_w_glu"].reshape(nb, GROUPS_PER_BLOCK, SSM_GROUP, SSM_GROUP), eye_t).astype(_MXU)
    dvec = p["ssm_d"]
    bglu = p["ssm_b_glu"].reshape(1, d_ssm)
    mask = jnp.tril(jnp.ones((CHUNK, CHUNK), _F32)) + tok[0:1, 0:1]
    wm = (p["sgu_w"] * mask[None]).astype(_MXU)
    bs = p["sgu_b"].reshape(nh, CHUNK, 1)

    h1 = _fwd_pre_mix(x, p["g_pre_mix"], _after(sc1, tok), sh1)
    buf_in = _gather_wait(sems_in, buf_in, [h1, buf_out, buf_up, buf_down, bbt, ct, wg, wm, bs, ptab, ptab_rev],
                          "gather_wait_in")
    w_in4 = shards(_pair_forward([buf_in], "pair_forward_in")[0])
    (sems_out, sems_up, sems_down), (buf_out, buf_up, buf_down), tok = _gather_start(
        [buf_out, buf_up, buf_down], w_in4, "gather_start_rest")
    z = _mm_nn(h1, w_in4, _F32, "mm_in", after=tok)
    y_ssm, y_pre, h_all, hb = _ssm_fwd(z, bbt, ct, dvec, wg, bglu, ptab)
    y_sgu = _sgu_fwd(z, p["sgu_ln_g"], p["sgu_ln_b"], wm, bs)
    ycat = _mix_norm_fwd(y_ssm, y_sgu, p["g_out_ssm"], p["g_out_sgu"])
    buf_out = _gather_wait(sems_out, buf_out, ycat, "gather_wait_out")
    w_out_full = _pair_forward([buf_out], "pair_forward_out")[0].reshape(1, d, d)
    o = _mm_nn(ycat, w_out_full, _F32, "mm_out")
    x1, h2 = _fwd_mid(o, x, gt1, p["g_post_mix"], p["g_pre_ffn"], sc2, sh2)
    buf_up = _gather_wait(sems_up, buf_up, h2, "gather_wait_up")
    w_up4 = shards(_pair_forward([buf_up], "pair_forward_up")[0])
    up_pre = _mm_nn(h2, w_up4, _F32, "mm_up")
    act = _conv_act_fwd(up_pre, conv_w_full, p["conv_b"])
    buf_down = _gather_wait(sems_down, buf_down, act, "gather_wait_down")
    w_down_full = _pair_forward([buf_down], "pair_forward_down")[0].reshape(1, -1, d)
    f = _mm_nn(act, w_down_full, _F32, "mm_down", tk=5632)
    dx2, df, d_gt2, d_g_post_ffn, loss = _loss_and_post_ffn_bwd(f, x1, tgt, gt2, p["g_post_ffn"])

    def reduce_next(swap, n, after):
        sems, gw, land, _ = swap
        gw, got = _swap_wait(sems, gw, land, after, "swap_wait_" + n)
        return _scatter_start(_pair_sum(gw, got, sel, "pair_sum_" + n), "scatter_start_" + n)

    d_act = _mm_nt(df, w_down_full, _F32, "mm_d_act", tk=2048)
    swap_down = _swap_start(_mm_tn_rows(act, df, "mm_gw_down"), "swap_start_w_down")
    d_up_pre, d_cw0, d_cw1, d_cw2, d_conv_b = _conv_act_bwd(up_pre, d_act, conv_w_full, _after(p["conv_b"], swap_down[3]))
    red_down = reduce_next(swap_down, "w_down", d_conv_b)
    dh2 = _mm_nt(d_up_pre, w_up4, _F32, "mm_dh2", tk=2816, after=red_down[3])
    swap_up = _swap_start(_mm_tn_cols(h2, d_up_pre, "mm_gw_up"), "swap_start_w_up")
    dx1, d_o, d_sc2, d_sh2, d_g_pre_ffn, d_gt1, d_g_post_mix = _bwd_mid(
        dh2, x1, dx2, o, p["g_pre_ffn"], _after(sc2, swap_up[3]), gt1, p["g_post_mix"])
    red_up = reduce_next(swap_up, "w_up", d_g_post_mix)
    d_ycat = _mm_nt(d_o, w_out_full, _F32, "mm_d_ycat", tn=1024, tk=2048, after=red_up[3])
    swap_out = _swap_start(_mm_tn_rows(ycat, d_o, "mm_gw_out"), "swap_start_w_out")
    dy_ssm, dy_sgu, d_g_out_ssm, d_g_out_sgu = _mix_norm_bwd(
        d_ycat, y_ssm, y_sgu, _after(p["g_out_ssm"], swap_out[3]), p["g_out_sgu"])
    red_out = reduce_next(swap_out, "w_out", d_g_out_sgu)
    dz_ssm, d_bbt, d_ct, d_wg, d_lb, d_ssm_d, d_bglu = _ssm_bwd(z, y_pre, h_all, dy_ssm, hb, bbt, ct,
                                                                _after(dvec, red_out[3]), wg, bglu, ptab, ptab_rev)
    dz, d_ln_g, d_ln_b, d_wm, d_bs = _sgu_bwd(z, dy_sgu, dz_ssm, p["sgu_ln_g"], p["sgu_ln_b"], wm, bs)
    dh1 = _mm_nt(dz, w_in4, _F32, "mm_dh1")
    swap_in = _swap_start(_mm_tn_cols(h1, dz, "mm_gw_in"), "swap_start_w_in")
    dx, d_sc1, d_sh1, d_g_pre_mix = _bwd_pre_mix(dh1, x, dx1, p["g_pre_mix"], _after(sc1, swap_in[3]))
    red_in = reduce_next(swap_in, "w_in", d_g_pre_mix)

    nsb = BLOCK_ST
    lanes = lambda t: t.transpose(2, 0, 1, 3).reshape(SSM_GROUP, gn)
    d_bbr = lanes(_block_diag_take(d_bbt[:, :, :nsb], SSM_GROUP, n_st))
    d_bbi = lanes(_block_diag_take(d_bbt[:, :, nsb:], SSM_GROUP, n_st))
    d_lr, d_li = d_lb[:, 0, :nsb].reshape(1, gn), d_lb[:, 0, nsb:].reshape(1, gn)
    d_bre_t, d_bim_t, d_are, d_aim, d_dt = _ssm_prep_bwd(ldt_l, are_l, aim_l, bre_t, bim_t, d_bbr, d_bbi, d_lr, d_li)
    d_log_dt = _group_sum(d_dt.reshape(g_cnt, n_st), p["ssm_log_dt"].reshape(g_cnt, 1))
    c_grad = lambda t: _block_diag_take(t, n_st, SSM_GROUP).transpose(0, 1, 3, 2).reshape(g_cnt, SSM_GROUP, n_st)
    small = {
        "b_ada": jnp.concatenate([d_sh1, _after(d_sc1, red_in[3]), d_gt1, d_sh2, d_sc2, d_gt2], axis=1),
        "g_pre_mix": d_g_pre_mix, "g_post_mix": d_g_post_mix,
        "ssm_log_dt": d_log_dt, "ssm_a_re": d_are, "ssm_a_im": d_aim,
        "ssm_b_re": d_bre_t.T, "ssm_b_im": d_bim_t.T,
        "ssm_c_re": c_grad(d_ct[:, :nsb, :]), "ssm_c_im": -c_grad(d_ct[:, nsb:, :]),
        "ssm_d": d_ssm_d, "ssm_w_glu": _block_diag_take(d_wg, SSM_GROUP, SSM_GROUP), "ssm_b_glu": d_bglu,
        "sgu_ln_g": d_ln_g, "sgu_ln_b": d_ln_b, "sgu_w": d_wm * mask[None], "sgu_b": d_bs,
        "g_out_ssm": d_g_out_ssm, "g_out_sgu": d_g_out_sgu, "g_pre_ffn": d_g_pre_ffn, "g_post_ffn": d_g_post_ffn,
        "conv_b": d_conv_b, "conv_w_all": jnp.concatenate([d_cw0, d_cw1, d_cw2], axis=0),
        "loss_sum": loss,
    }
    small = {n: _to_view(n, a.reshape(p[n].shape)) if n in p else a for n, a in small.items()}
    pk = _Pack({n: a.shape for n, a in small.items()})
    sems_small, small_buf, tok = _gather8_start(_own_slot(pk.pack(small), dev), "gather_small_start")

    big = ["w_down", "w_up", "w_out", "w_in"]
    joins = []
    after = tok
    for n, (sems, pair, land, _) in zip(big, (red_down, red_up, red_out, red_in)):
        pair, land = _scatter_wait(sems, pair, land, after, "scatter_wait_" + n)
        sems_j, half, after = _join_start(_chip_sum(pair, land, sel, "chip_sum_" + n), "join_start_" + n)
        joins.append((sems_j, half))
    big_out = {}
    for n, (sems_j, half) in zip(big, joins):
        j = _join_wait(sems_j, half, after, "join_wait_" + n)
        if n in ("w_in", "w_up"):
            big_out[n] = tuple(_adamw(p[n], j.reshape(p[n].shape), m[n], v[n], "adamw_" + n))
        else:
            big_out[n] = tuple(_adamw_halves(p[n], j, m[n], v[n], "adamw_" + n))
        after = big_out[n][1]

    gathered = _gather8_forward(_gather8_wait(sems_small, small_buf, after, "gather_small_wait"),
                                "gather_small_forward")
    gview = pk.unpack(_small_reduce(gathered))
    gview["conv_w"] = lax.dynamic_slice_in_dim(gview.pop("conv_w_all"), chip * nc_conv, nc_conv, axis=1)
    loss = gview.pop("loss_sum")
    small_names = _SMALL + ["conv_w"]
    per_group = [n for n in small_names if gview[n].ndim >= 2 and gview[n].shape[0] == g_cnt]
    others = [n for n in small_names if n not in per_group]
    grads = {n: _from_view(n, gview[n]) for n in small_names}
    deltas, new_m, new_v = {}, {}, {}
    for names, steps, call in ((per_group, g_cnt // GROUPS_PER_BLOCK, "adamw_s5"), (others, 1, "adamw_small")):
        res = _adamw_many([_to_view(n, p[n]) for n in names], [gview[n] for n in names],
                          [_to_view(n, m[n]) for n in names], [_to_view(n, v[n]) for n in names], steps, call)
        for n, dl, mo, vo in zip(names, *res):
            deltas[n], new_m[n], new_v[n] = _from_view(n, dl), _from_view(n, mo), _from_view(n, vo)

    d_mod_all = gathered.reshape(8, -1)[:, :N_MOD * d]
    d_mod_mine = lax.dynamic_slice_in_dim(d_mod_all, chip * ns_ada, ns_ada, axis=1)
    grads["w_ada"], deltas["w_ada"], new_m["w_ada"], new_v["w_ada"] = _wada_update(
        c_all.T, d_mod_mine, p["w_ada"], m["w_ada"], v["w_ada"])
    for n in big:
        grads[n], deltas[n], new_m[n], new_v[n] = big_out[n]
    return loss[0, 0], dx, grads, deltas, new_m, new_v


def kernel(x, c, w_ada, b_ada, g_pre_mix, g_post_mix, w_in, ssm_log_dt, ssm_a_re, ssm_a_im, ssm_b_re, ssm_b_im, ssm_c_re, ssm_c_im, ssm_d, ssm_w_glu, ssm_b_glu, sgu_ln_g, sgu_ln_b, sgu_w, sgu_b, g_out_ssm, g_out_sgu, w_out, g_pre_ffn, g_post_ffn, w_up, conv_w, conv_b, w_down, loss_target, m_w_ada, m_b_ada, m_g_pre_mix, m_g_post_mix, m_w_in, m_ssm_log_dt, m_ssm_a_re, m_ssm_a_im, m_ssm_b_re, m_ssm_b_im, m_ssm_c_re, m_ssm_c_im, m_ssm_d, m_ssm_w_glu, m_ssm_b_glu, m_sgu_ln_g, m_sgu_ln_b, m_sgu_w, m_sgu_b, m_g_out_ssm, m_g_out_sgu, m_w_out, m_g_pre_ffn, m_g_post_ffn, m_w_up, m_conv_w, m_conv_b, m_w_down, v_w_ada, v_b_ada, v_g_pre_mix, v_g_post_mix, v_w_in, v_ssm_log_dt, v_ssm_a_re, v_ssm_a_im, v_ssm_b_re, v_ssm_b_im, v_ssm_c_re, v_ssm_c_im, v_ssm_d, v_ssm_w_glu, v_ssm_b_glu, v_sgu_ln_g, v_sgu_ln_b, v_sgu_w, v_sgu_b, v_g_out_ssm, v_g_out_sgu, v_w_out, v_g_pre_ffn, v_g_post_ffn, v_w_up, v_conv_w, v_conv_b, v_w_down):
    given = dict(locals())
    drop = lambda a: a if a.ndim == 2 else a[0]
    p = {n: drop(given[n]) for n in _WEIGHTS}
    m = {n: drop(given["m_" + n]) for n in _WEIGHTS}
    v = {n: drop(given["v_" + n]) for n in _WEIGHTS}
    loss, dx, grads, deltas, new_m, new_v = _step(p, m, v, x[0], c, loss_target[0])
    outs = [loss, dx[None]]
    for group in (grads, deltas, new_m, new_v):
        outs += [group[n].reshape(given[n].shape) for n in _WEIGHTS]
    return tuple(outs)
```

```python
import functools
import math

import jax
import jax.numpy as jnp
from jax import lax
from jax.experimental import pallas as pl
from jax.experimental.pallas import tpu as pltpu

_F32 = jnp.float32
_MXU = jnp.bfloat16
_WIRE = jnp.bfloat16

EPS = 1e-6
SSM_GROUP = 16
SSM_STATE = 64
GROUPS_PER_BLOCK = 8
BLOCK_CH = SSM_GROUP * GROUPS_PER_BLOCK
BLOCK_ST = SSM_STATE * GROUPS_PER_BLOCK
CHUNK = 128
TIME_TILE = 512
SUBLANES = 8
LANES = 128
N_MOD = 6
ADAM_LR, ADAM_B1, ADAM_B2, ADAM_EPS, ADAM_WD, ADAM_STEP = 0.001, 0.9, 0.999, 1e-08, 0.01, 10
_VMEM_LIMIT = 56 * 1024 * 1024
_MESH = pl.DeviceIdType.MESH
_ANY = pl.BlockSpec(memory_space=pl.ANY)
_HBM = pl.BlockSpec(memory_space=pltpu.HBM)
_SEM = pl.BlockSpec(memory_space=pltpu.SEMAPHORE)
_VMEM_WHOLE = pl.BlockSpec(memory_space=pltpu.VMEM)
_EFFECT = pltpu.SideEffectType.DATAFLOW_SIDE_EFFECTING
_GELU_C = math.sqrt(2.0 / math.pi)


def _cp(*sem):
    return pltpu.CompilerParams(dimension_semantics=sem, vmem_limit_bytes=_VMEM_LIMIT)


def _tile(dim, target, align=LANES):
    if dim <= target:
        return dim
    best = None
    for t in range(align, target + 1, align):
        if dim % t == 0:
            best = t
    assert best is not None, (dim, target, align)
    return best


def _gelu(x):
    return 0.5 * x * (1.0 + jnp.tanh(_GELU_C * (x + 0.044715 * (x * x * x))))


def _gelu_grad(x):
    t = jnp.tanh(_GELU_C * (x + 0.044715 * (x * x * x)))
    return 0.5 * (1.0 + t) + 0.5 * x * (1.0 - t * t) * (_GELU_C * (1.0 + 3.0 * 0.044715 * x * x))


def _sigmoid(x):
    return 1.0 / (1.0 + jnp.exp(-x))


def _colsum(x):
    return jnp.sum(x, axis=0, keepdims=True)


def _rowmean(x):
    return jnp.mean(x, axis=-1, keepdims=True)


def _zero_first(first, *refs):
    @pl.when(first)
    def _():
        for ref in refs:
            ref[...] = jnp.zeros_like(ref)


def _acc(ref, first, val):
    del first
    ref[...] += val


def _place():
    mx, my, mc = lax.axis_index("x"), lax.axis_index("y"), lax.axis_index("c")
    chips = [(1 - mx, my), (mx, 1 - my), (1 - mx, 1 - my)]
    return mx, my, mc, chips


def _all_gather8(buf, name, after=None):
    extra = [] if after is None else (list(after) if isinstance(after, (list, tuple)) else [after])

    def body(in_ref, *rest):
        out_ref, send_sems, recv_sems = rest[len(extra):]
        mx, my, mc, chips = _place()
        me, sibling = (mx, my, mc), (mx, my, 1 - mc)

        def slot(ref, px, py, pc):
            return ref.at[4 * px + 2 * py + pc]

        def copy(k, block, to, src_ref=out_ref):
            return pltpu.make_async_remote_copy(
                src_ref=slot(src_ref, *block), dst_ref=slot(out_ref, *block),
                send_sem=send_sems.at[k], recv_sem=recv_sems.at[k], device_id=to, device_id_type=_MESH)

        first = [copy(0, me, sibling, in_ref)]
        first += [copy(1 + j, me, (*chip, mc), in_ref) for j, chip in enumerate(chips)]
        for cp in first:
            cp.start()
        passed = [copy(4 + j, (*chip, mc), sibling) for j, chip in enumerate(chips)]
        for j, chip in enumerate(chips):
            copy(1 + j, (*chip, mc), me).wait_recv()
            passed[j].start()
        copy(0, sibling, me).wait_recv()
        for j, chip in enumerate(chips):
            copy(4 + j, (*chip, 1 - mc), me).wait_recv()
        for cp in first + passed:
            cp.wait_send()

    return pl.pallas_call(
        body, name=name, out_shape=jax.ShapeDtypeStruct(buf.shape, buf.dtype),
        in_specs=[_ANY] * (1 + len(extra)), out_specs=_ANY, input_output_aliases={0: 0},
        scratch_shapes=[pltpu.SemaphoreType.DMA((7,)), pltpu.SemaphoreType.DMA((7,))],
    )(buf, *extra)


def _own_slot(x, dev):
    return lax.dynamic_update_slice(jnp.zeros((8,) + x.shape, x.dtype), x[None], (dev, 0, 0))


def _cast_into_slot(w, sel, after, name):
    r, c = w.shape
    hr = r // 2
    tr = _tile(hr, 256, 16)
    nr = hr // tr

    def body(sel_ref, w_ref, after_ref, o_ref):
        o_ref[...] = w_ref[...].astype(o_ref.dtype)

    return pl.pallas_call(
        body, name=name, out_shape=jax.ShapeDtypeStruct((4, 2, hr, c), _WIRE),
        grid_spec=pltpu.PrefetchScalarGridSpec(
            num_scalar_prefetch=1, grid=(2, nr),
            in_specs=[pl.BlockSpec((tr, c), lambda h, i, s: (h * nr + i, 0)), _ANY],
            out_specs=pl.BlockSpec((None, None, tr, c), lambda h, i, s: (s[0], h, i, 0))),
        compiler_params=_cp("parallel", "parallel"),
    )(sel, w, after)


def _hbm(a):
    return pltpu.with_memory_space_constraint(a, pltpu.HBM)


def _after(vec, token):
    return vec + token[0:1, 0:1]


def _gather_start(bufs, after, name):
    n = len(bufs)
    nc = 3 * n

    def body(*refs):
        ins, send, recv, token = refs[:n], refs[n + 1:n + 1 + nc], refs[n + 1 + nc:n + 1 + 2 * nc], refs[-1]
        mx, my, mc, chips = _place()
        j_me = 2 * mx + my
        for i in range(n):
            for k, chip in enumerate(chips):
                half = ins[i].at[j_me, mc]
                pltpu.make_async_remote_copy(
                    src_ref=half, dst_ref=half, send_sem=send[3 * i + k], recv_sem=recv[3 * i + k],
                    device_id=(*chip, mc), device_id_type=_MESH).start()
        token[...] = jnp.zeros_like(token)

    outs = pl.pallas_call(
        body, name=name,
        out_shape=tuple([pltpu.SemaphoreType.DMA(())] * (2 * nc) + [pltpu.HBM(b.shape, b.dtype) for b in bufs]
                        + [jax.ShapeDtypeStruct((SUBLANES, LANES), _F32)]),
        in_specs=tuple([_HBM] * n + [_ANY]), out_specs=tuple([_SEM] * (2 * nc) + [_HBM] * n + [_VMEM_WHOLE]),
        input_output_aliases={i: 2 * nc + i for i in range(n)},
        compiler_params=pltpu.CompilerParams(has_side_effects=_EFFECT),
    )(*[_hbm(b) for b in bufs], after)
    sems = [(outs[3 * i:3 * i + 3], outs[nc + 3 * i:nc + 3 * i + 3]) for i in range(n)]
    return sems, list(outs[2 * nc:2 * nc + n]), outs[-1]


def _gather_wait(sems, buf, after, name):
    send, recv = sems

    after = list(after) if isinstance(after, (list, tuple)) else [after]

    def body(buf_ref, s0, s1, s2, r0, r1, r2, *rest):
        mx, my, mc, chips = _place()
        j_me = 2 * mx + my
        for k, (chip, s_k, r_k) in enumerate(zip(chips, (s0, s1, s2), (r0, r1, r2))):
            cp = pltpu.make_async_remote_copy(
                src_ref=buf_ref.at[j_me, mc], dst_ref=buf_ref.at[2 * chip[0] + chip[1], mc], send_sem=s_k, recv_sem=r_k,
                device_id=(*chip, mc), device_id_type=_MESH)
            cp.wait_send()
            cp.wait_recv()

    return pl.pallas_call(
        body, name=name, out_shape=pltpu.HBM(buf.shape, buf.dtype),
        in_specs=(_HBM,) + (_SEM,) * 6 + (_ANY,) * len(after), out_specs=_HBM, input_output_aliases={0: 0},
        compiler_params=pltpu.CompilerParams(has_side_effects=_EFFECT),
    )(buf, *send, *recv, *after)


def _pair_forward(bufs, name):
    n = len(bufs)

    def body(*refs):
        ins, outs = refs[:n], refs[n:2 * n]
        send_sems, recv_sems = refs[2 * n:]
        mx, my, mc, chips = _place()
        sibling = (mx, my, 1 - mc)
        cps = []
        for i in range(n):
            for k, chip in enumerate(chips):
                j_k = 2 * chip[0] + chip[1]
                cp = pltpu.make_async_remote_copy(
                    src_ref=ins[i].at[j_k, mc], dst_ref=outs[i].at[j_k, mc], send_sem=send_sems.at[3 * i + k],
                    recv_sem=recv_sems.at[3 * i + k], device_id=sibling, device_id_type=_MESH)
                cp.start()
                cps.append(cp)
        for i in range(n):
            for k, chip in enumerate(chips):
                other = outs[i].at[2 * chip[0] + chip[1], 1 - mc]
                pltpu.make_async_remote_copy(
                    src_ref=other, dst_ref=other, send_sem=send_sems.at[3 * i + k], recv_sem=recv_sems.at[3 * i + k],
                    device_id=sibling, device_id_type=_MESH).wait_recv()
        for cp in cps:
            cp.wait_send()

    return pl.pallas_call(
        body, name=name, out_shape=[jax.ShapeDtypeStruct(b.shape, b.dtype) for b in bufs],
        in_specs=[_ANY] * n, out_specs=[_ANY] * n, input_output_aliases={i: i for i in range(n)},
        scratch_shapes=[pltpu.SemaphoreType.DMA((3 * n,)), pltpu.SemaphoreType.DMA((3 * n,))],
    )(*bufs)


def _gather8_peers(buf_ref, mx, my, mc, chips):
    mine = buf_ref.at[4 * mx + 2 * my + mc]
    peers = [((mx, my, 1 - mc), mine, buf_ref.at[4 * mx + 2 * my + 1 - mc])]
    peers += [((*chip, mc), mine, buf_ref.at[4 * chip[0] + 2 * chip[1] + mc]) for chip in chips]
    return peers


def _gather8_start(buf, name):
    def body(buf_ref, *rest):
        send, recv, token = rest[0:4], rest[4:8], rest[-1]
        mx, my, mc, chips = _place()
        for k, (peer, src, _) in enumerate(_gather8_peers(buf_ref, mx, my, mc, chips)):
            pltpu.make_async_remote_copy(src_ref=src, dst_ref=src, send_sem=send[k], recv_sem=recv[k],
                                         device_id=peer, device_id_type=_MESH).start()
        token[...] = jnp.zeros_like(token)

    outs = pl.pallas_call(
        body, name=name,
        out_shape=tuple([pltpu.SemaphoreType.DMA(())] * 8 + [pltpu.HBM(buf.shape, buf.dtype),
                                                             jax.ShapeDtypeStruct((SUBLANES, LANES), _F32)]),
        in_specs=(_HBM,), out_specs=tuple([_SEM] * 8 + [_HBM, _VMEM_WHOLE]), input_output_aliases={0: 8},
        compiler_params=pltpu.CompilerParams(has_side_effects=_EFFECT),
    )(_hbm(buf))
    return (outs[0:4], outs[4:8]), outs[8], outs[9]


def _gather8_wait(sems, buf, after, name):
    send, recv = sems

    def body(buf_ref, s0, s1, s2, s3, r0, r1, r2, r3, after_ref, out_ref):
        mx, my, mc, chips = _place()
        for (peer, src, dst), s_k, r_k in zip(_gather8_peers(buf_ref, mx, my, mc, chips), (s0, s1, s2, s3), (r0, r1, r2, r3)):
            cp = pltpu.make_async_remote_copy(src_ref=src, dst_ref=dst, send_sem=s_k, recv_sem=r_k,
                                              device_id=peer, device_id_type=_MESH)
            cp.wait_send()
            cp.wait_recv()

    return pl.pallas_call(
        body, name=name, out_shape=pltpu.HBM(buf.shape, buf.dtype),
        in_specs=(_HBM,) + (_SEM,) * 8 + (_ANY,), out_specs=_HBM, input_output_aliases={0: 0},
        compiler_params=pltpu.CompilerParams(has_side_effects=_EFFECT),
    )(buf, *send, *recv, after)


def _gather8_forward(buf, name):
    def body(in_ref, out_ref, send_sems, recv_sems):
        mx, my, mc, chips = _place()
        sibling = (mx, my, 1 - mc)
        cps = []
        for k, chip in enumerate(chips):
            idx = 4 * chip[0] + 2 * chip[1] + mc
            cp = pltpu.make_async_remote_copy(src_ref=in_ref.at[idx], dst_ref=out_ref.at[idx], send_sem=send_sems.at[k],
                                              recv_sem=recv_sems.at[k], device_id=sibling, device_id_type=_MESH)
            cp.start()
            cps.append(cp)
        for k, chip in enumerate(chips):
            other = out_ref.at[4 * chip[0] + 2 * chip[1] + 1 - mc]
            pltpu.make_async_remote_copy(src_ref=other, dst_ref=other, send_sem=send_sems.at[k], recv_sem=recv_sems.at[k],
                                         device_id=sibling, device_id_type=_MESH).wait_recv()
        for cp in cps:
            cp.wait_send()

    return pl.pallas_call(
        body, name=name, out_shape=jax.ShapeDtypeStruct(buf.shape, buf.dtype),
        in_specs=[_ANY], out_specs=_ANY, input_output_aliases={0: 0},
        scratch_shapes=[pltpu.SemaphoreType.DMA((3,)), pltpu.SemaphoreType.DMA((3,))],
    )(buf)


def _scatter_start(pair, name):
    land = lax.empty((3,) + pair.shape[1:], pair.dtype)

    def body(pair_ref, land_ref, s0, s1, s2, r0, r1, r2, pair_thru, land_thru, token):
        mx, my, mc, chips = _place()
        for k, (chip, s_k, r_k) in enumerate(zip(chips, (s0, s1, s2), (r0, r1, r2))):
            pltpu.make_async_remote_copy(
                src_ref=pair_ref.at[2 * chip[0] + chip[1]], dst_ref=land_ref.at[k], send_sem=s_k, recv_sem=r_k,
                device_id=(*chip, mc), device_id_type=_MESH).start()
        token[...] = jnp.zeros_like(token)

    outs = pl.pallas_call(
        body, name=name,
        out_shape=tuple([pltpu.SemaphoreType.DMA(())] * 6 + [pltpu.HBM(pair.shape, pair.dtype), pltpu.HBM(land.shape, land.dtype),
                                                             jax.ShapeDtypeStruct((SUBLANES, LANES), _F32)]),
        in_specs=(_HBM, _HBM), out_specs=tuple([_SEM] * 6 + [_HBM, _HBM, _VMEM_WHOLE]),
        input_output_aliases={0: 6, 1: 7}, compiler_params=pltpu.CompilerParams(has_side_effects=_EFFECT),
    )(_hbm(pair), _hbm(land))
    return (outs[0:3], outs[3:6]), outs[6], outs[7], outs[8]


def _scatter_wait(sems, pair, land, after, name):
    send, recv = sems

    def body(pair_ref, land_ref, s0, s1, s2, r0, r1, r2, after_ref, pair_out, land_out):
        mx, my, mc, chips = _place()
        for k, (chip, s_k, r_k) in enumerate(zip(chips, (s0, s1, s2), (r0, r1, r2))):
            cp = pltpu.make_async_remote_copy(
                src_ref=pair_ref.at[2 * chip[0] + chip[1]], dst_ref=land_ref.at[k], send_sem=s_k, recv_sem=r_k,
                device_id=(*chip, mc), device_id_type=_MESH)
            cp.wait_send()
            cp.wait_recv()

    return pl.pallas_call(
        body, name=name, out_shape=(pltpu.HBM(pair.shape, pair.dtype), pltpu.HBM(land.shape, land.dtype)),
        in_specs=(_HBM, _HBM) + (_SEM,) * 6 + (_ANY,), out_specs=(_HBM, _HBM), input_output_aliases={0: 0, 1: 1},
        compiler_params=pltpu.CompilerParams(has_side_effects=_EFFECT),
    )(pair, land, *send, *recv, after)


def _sibling_copy(src_ref, dst_ref, send_sem, recv_sem):
    mx, my, mc, _ = _place()
    return pltpu.make_async_remote_copy(src_ref=src_ref, dst_ref=dst_ref, send_sem=send_sem, recv_sem=recv_sem,
                                        device_id=(mx, my, 1 - mc), device_id_type=_MESH)


def _swap_start(g, name):
    land = lax.empty(g.shape[1:], g.dtype)

    def body(g_ref, land_ref, send_sem, recv_sem, g_thru, land_thru, token):
        _sibling_copy(g_ref.at[1 - lax.axis_index("c")], land_ref, send_sem, recv_sem).start()
        token[...] = jnp.zeros_like(token)

    outs = pl.pallas_call(
        body, name=name,
        out_shape=(pltpu.SemaphoreType.DMA(()), pltpu.SemaphoreType.DMA(()), pltpu.HBM(g.shape, g.dtype),
                   pltpu.HBM(land.shape, land.dtype), jax.ShapeDtypeStruct((SUBLANES, LANES), _F32)),
        in_specs=(_HBM, _HBM), out_specs=(_SEM, _SEM, _HBM, _HBM, _VMEM_WHOLE), input_output_aliases={0: 2, 1: 3},
        compiler_params=pltpu.CompilerParams(has_side_effects=_EFFECT),
    )(_hbm(g), _hbm(land))
    return (outs[0], outs[1]), outs[2], outs[3], outs[4]


def _swap_wait(sems, g, land, after, name):
    def body(g_ref, land_ref, send_sem, recv_sem, after_ref, g_out, land_out):
        cp = _sibling_copy(g_ref.at[1 - lax.axis_index("c")], land_ref, send_sem, recv_sem)
        cp.wait_send()
        cp.wait_recv()

    return pl.pallas_call(
        body, name=name, out_shape=(pltpu.HBM(g.shape, g.dtype), pltpu.HBM(land.shape, land.dtype)),
        in_specs=(_HBM, _HBM, _SEM, _SEM, _ANY), out_specs=(_HBM, _HBM), input_output_aliases={0: 0, 1: 1},
        compiler_params=pltpu.CompilerParams(has_side_effects=_EFFECT),
    )(g, land, *sems, after)


def _join_start(buf, name):
    def body(buf_ref, send_sem, recv_sem, buf_thru, token):
        mine = buf_ref.at[lax.axis_index("c")]
        _sibling_copy(mine, mine, send_sem, recv_sem).start()
        token[...] = jnp.zeros_like(token)

    outs = pl.pallas_call(
        body, name=name,
        out_shape=(pltpu.SemaphoreType.DMA(()), pltpu.SemaphoreType.DMA(()), pltpu.HBM(buf.shape, buf.dtype),
                   jax.ShapeDtypeStruct((SUBLANES, LANES), _F32)),
        in_specs=(_HBM,), out_specs=(_SEM, _SEM, _HBM, _VMEM_WHOLE), input_output_aliases={0: 2},
        compiler_params=pltpu.CompilerParams(has_side_effects=_EFFECT),
    )(_hbm(buf))
    return (outs[0], outs[1]), outs[2], outs[3]


def _join_wait(sems, buf, after, name):
    def body(buf_ref, send_sem, recv_sem, after_ref, buf_out):
        mc = lax.axis_index("c")
        cp = _sibling_copy(buf_ref.at[mc], buf_ref.at[1 - mc], send_sem, recv_sem)
        cp.wait_send()
        cp.wait_recv()

    return pl.pallas_call(
        body, name=name, out_shape=pltpu.HBM(buf.shape, buf.dtype),
        in_specs=(_HBM, _SEM, _SEM, _ANY), out_specs=_HBM, input_output_aliases={0: 0},
        compiler_params=pltpu.CompilerParams(has_side_effects=_EFFECT),
    )(buf, *sems, after)


def _pair_sum(g, got, sel, name):
    _, four, hr, c = g.shape
    tr = _tile(hr, 512, 16)

    def body(sel_ref, g_ref, p_ref, o_ref):
        o_ref[...] = (g_ref[...].astype(_F32) + p_ref[...].astype(_F32)).astype(o_ref.dtype)

    return pl.pallas_call(
        body, name=name, out_shape=jax.ShapeDtypeStruct((four, hr, c), g.dtype),
        grid_spec=pltpu.PrefetchScalarGridSpec(
            num_scalar_prefetch=1, grid=(four, hr // tr),
            in_specs=[pl.BlockSpec((None, None, tr, c), lambda j, i, s: (s[1], j, i, 0)),
                      pl.BlockSpec((None, tr, c), lambda j, i, s: (j, i, 0))],
            out_specs=pl.BlockSpec((None, tr, c), lambda j, i, s: (j, i, 0))),
        compiler_params=_cp("parallel", "parallel"),
    )(sel, g, got)


def _chip_sum(pair, got, sel, name):
    _, hr, c = pair.shape
    tr = _tile(hr, 512, 16)

    def body(sel_ref, p_ref, q_ref, o_ref):
        o_ref[...] = ((p_ref[...].astype(_F32) + q_ref[0].astype(_F32)) + q_ref[1].astype(_F32)) + q_ref[2].astype(_F32)

    return pl.pallas_call(
        body, name=name, out_shape=jax.ShapeDtypeStruct((2, hr, c), _F32),
        grid_spec=pltpu.PrefetchScalarGridSpec(
            num_scalar_prefetch=1, grid=(hr // tr,),
            in_specs=[pl.BlockSpec((None, tr, c), lambda i, s: (s[0], i, 0)),
                      pl.BlockSpec((3, tr, c), lambda i, s: (0, i, 0))],
            out_specs=pl.BlockSpec((None, tr, c), lambda i, s: (s[1], i, 0))),
        compiler_params=_cp("parallel"),
    )(sel, pair, got)


def _matmul(a, b, dims, out_struct, grid, a_spec, b_spec, o_spec, acc_shape, k_axis, name, after=None):
    nk = grid[k_axis]
    extra = [] if after is None else [after]

    def body(a_ref, b_ref, *rest):
        o_ref, acc = rest[len(extra)], rest[len(extra) + 1:]
        prod = lax.dot_general(a_ref[...].astype(_MXU), b_ref[...].astype(_MXU), dims, preferred_element_type=_F32)
        if nk == 1:
            o_ref[...] = prod.astype(o_ref.dtype)
        else:
            acc_ref, = acc
            k = pl.program_id(k_axis)
            _zero_first(k == 0, acc_ref)
            acc_ref[...] += prod

            @pl.when(k == nk - 1)
            def _():
                o_ref[...] = acc_ref[...].astype(o_ref.dtype)

    sem = ["parallel"] * len(grid)
    sem[k_axis] = "arbitrary"
    return pl.pallas_call(
        body, name=name, out_shape=out_struct, grid=grid, in_specs=[a_spec, b_spec] + [_ANY] * len(extra), out_specs=o_spec,
        scratch_shapes=[pltpu.VMEM(acc_shape, _F32)] if nk > 1 else [], compiler_params=_cp(*sem),
    )(a, b, *extra)


def _mm_nn(a, w4, out_dtype, name, tm=512, tn=1536, tk=2048, after=None):
    m, k = a.shape
    j, _, ns = w4.shape
    tm, tn, tk = _tile(m, tm, 16), _tile(ns, tn), _tile(k, tk)
    nps = ns // tn
    return _matmul(
        a, w4, (((1,), (0,)), ((), ())), jax.ShapeDtypeStruct((m, j * ns), out_dtype),
        (j * nps, m // tm, k // tk),
        pl.BlockSpec((tm, tk), lambda ni, mi, ki: (mi, ki)),
        pl.BlockSpec((None, tk, tn), lambda ni, mi, ki: (ni // nps, ki, ni % nps)),
        pl.BlockSpec((tm, tn), lambda ni, mi, ki: (mi, ni)), (tm, tn), 2, name, after)


def _mm_nt(a, w4, out_dtype, name, tm=512, tn=2048, tk=1536, after=None):
    m = a.shape[-2]
    j, kw, ns = w4.shape
    tm, tn, tk = _tile(m, tm, 16), _tile(kw, tn), _tile(ns, tk)
    kps = ns // tk
    if a.ndim == 3:
        kph = a.shape[2] // tk
        a_spec = pl.BlockSpec((None, tm, tk), lambda ni, mi, ki: (ki // kph, mi, ki % kph))
    else:
        a_spec = pl.BlockSpec((tm, tk), lambda ni, mi, ki: (mi, ki))
    return _matmul(
        a, w4, (((1,), (1,)), ((), ())), jax.ShapeDtypeStruct((m, kw), out_dtype),
        (kw // tn, m // tm, j * kps),
        a_spec,
        pl.BlockSpec((None, tn, tk), lambda ni, mi, ki: (ki // kps, ni, ki % kps)),
        pl.BlockSpec((tm, tn), lambda ni, mi, ki: (mi, ni)), (tm, tn), 2, name, after)


def _mm_tn_cols(a, b, name, tm=1024, tn=1536, tk=2048):
    m, ka = a.shape
    ns = (b.shape[-1] * (2 if b.ndim == 3 else 1)) // 4
    hr = ka // 2
    tm, tn, tk = _tile(hr, tm), _tile(ns, tn), _tile(m, tk, 16)
    mph, nps = hr // tm, ns // tn
    if b.ndim == 3:
        b_spec = pl.BlockSpec((None, tk, tn), lambda ni, mi, ki: (ni // (2 * nps), ki, ni % (2 * nps)))
    else:
        b_spec = pl.BlockSpec((tk, tn), lambda ni, mi, ki: (ki, ni))
    return _matmul(
        a, b, (((0,), (0,)), ((), ())), jax.ShapeDtypeStruct((2, 4, hr, ns), _WIRE),
        (4 * nps, 2 * mph, m // tk),
        pl.BlockSpec((tk, tm), lambda ni, mi, ki: (ki, mi)),
        b_spec,
        pl.BlockSpec((None, None, tm, tn), lambda ni, mi, ki: (mi // mph, ni // nps, mi % mph, ni % nps)),
        (tm, tn), 2, name)


def _mm_tn_rows(a, b, name, tm=1536, tn=1024, tk=2048):
    m, ka = a.shape
    r = ka // 4
    hc = b.shape[1] // 2
    tm, tn, tk = _tile(r, tm), _tile(hc, tn), _tile(m, tk, 16)
    mpr, nph = r // tm, hc // tn
    return _matmul(
        a, b, (((0,), (0,)), ((), ())), jax.ShapeDtypeStruct((2, 4, r, hc), _WIRE),
        (2 * nph, 4 * mpr, m // tk),
        pl.BlockSpec((tk, tm), lambda ni, mi, ki: (ki, mi)),
        pl.BlockSpec((tk, tn), lambda ni, mi, ki: (ki, ni)),
        pl.BlockSpec((None, None, tm, tn), lambda ni, mi, ki: (ni // nph, mi // mpr, mi % mpr, ni % nph)),
        (tm, tn), 2, name)


def _row_call(body, name, rows, ins, outs, tm=256):
    tm = _tile(rows, tm, 16)

    def spec(shape, kind):
        if kind == "rows":
            return pl.BlockSpec((tm, shape[1]), lambda i: (i, 0))
        return pl.BlockSpec(shape, lambda i: (0,) * len(shape))

    return pl.pallas_call(
        body, name=name, grid=(rows // tm,),
        in_specs=[spec(a.shape, kind) for a, kind in ins],
        out_specs=[spec(o.shape, kind) for o, kind in outs],
        out_shape=[o for o, _ in outs],
        compiler_params=_cp("arbitrary"),
    )(*[a for a, _ in ins])


def _rms(x):
    r = lax.rsqrt(_rowmean(x * x) + EPS)
    return x * r, r


def _rms_bwd(dxh, xh, r):
    return r * (dxh - xh * _rowmean(dxh * xh))


def _fwd_pre_mix(x, g, sc, sh):
    s, d = x.shape

    def body(x_ref, g_ref, sc_ref, sh_ref, h_ref):
        xh, _ = _rms(x_ref[...])
        h_ref[...] = (xh * g_ref[...] * (1.0 + sc_ref[...]) + sh_ref[...]).astype(h_ref.dtype)

    return _row_call(body, "fwd_pre_mix", s, [(x, "rows"), (g, "vec"), (sc, "vec"), (sh, "vec")],
                     [(jax.ShapeDtypeStruct((s, d), _MXU), "rows")])[0]


def _fwd_mid(o, x, gt1, g_post, g_pre2, sc2, sh2):
    s, d = x.shape

    def body(o_ref, x_ref, gt_ref, gp_ref, g2_ref, sc_ref, sh_ref, x1_ref, h2_ref):
        oh, _ = _rms(o_ref[...])
        x1 = x_ref[...] + gt_ref[...] * (oh * gp_ref[...])
        x1_ref[...] = x1
        xh, _ = _rms(x1)
        h2_ref[...] = (xh * g2_ref[...] * (1.0 + sc_ref[...]) + sh_ref[...]).astype(h2_ref.dtype)

    return _row_call(body, "fwd_mid", s,
                     [(o, "rows"), (x, "rows"), (gt1, "vec"), (g_post, "vec"), (g_pre2, "vec"), (sc2, "vec"),
                      (sh2, "vec")],
                     [(jax.ShapeDtypeStruct((s, d), _F32), "rows"), (jax.ShapeDtypeStruct((s, d), _MXU), "rows")])


def _loss_and_post_ffn_bwd(f, x1, tgt, gt2, g_post):
    s, d = x1.shape

    def body(f_ref, x1_ref, t_ref, gt_ref, g_ref, dx2_ref, df_ref, dgt_ref, dg_ref, loss_ref):
        first = pl.program_id(0) == 0
        _zero_first(first, dgt_ref, dg_ref, loss_ref)
        fh, r = _rms(f_ref[...])
        n = fh * g_ref[...]
        e = x1_ref[...] + gt_ref[...] * n - t_ref[...]
        _acc(loss_ref, first, jnp.sum(_colsum(e * e), axis=1, keepdims=True) * (0.5 / d))
        dx2 = e * (1.0 / d)
        dx2_ref[...] = dx2
        _acc(dgt_ref, first, _colsum(dx2 * n))
        dn = dx2 * gt_ref[...]
        _acc(dg_ref, first, _colsum(dn * fh))
        df_ref[...] = _rms_bwd(dn * g_ref[...], fh, r).astype(df_ref.dtype)

    vec = jax.ShapeDtypeStruct((1, d), _F32)
    return _row_call(body, "loss_post_ffn_bwd", s,
                     [(f, "rows"), (x1, "rows"), (tgt, "rows"), (gt2, "vec"), (g_post, "vec")],
                     [(jax.ShapeDtypeStruct((s, d), _F32), "rows"), (jax.ShapeDtypeStruct((s, d), _MXU), "rows"),
                      (vec, "vec"), (vec, "vec"), (jax.ShapeDtypeStruct((1, 1), _F32), "vec")])


def _bwd_mid(dh2, x1, dx2, o, g_pre2, sc2, gt1, g_post):
    s, d = x1.shape

    def body(dh_ref, x1_ref, dx2_ref, o_ref, g2_ref, sc_ref, gt_ref, gp_ref,
             dx1_ref, do_ref, dsc_ref, dsh_ref, dg2_ref, dgt_ref, dgp_ref):
        first = pl.program_id(0) == 0
        _zero_first(first, dsc_ref, dsh_ref, dg2_ref, dgt_ref, dgp_ref)
        dh = dh_ref[...]
        xh, r = _rms(x1_ref[...])
        _acc(dsh_ref, first, _colsum(dh))
        _acc(dsc_ref, first, _colsum(dh * (xh * g2_ref[...])))
        dn = dh * (1.0 + sc_ref[...])
        _acc(dg2_ref, first, _colsum(dn * xh))
        dx1 = dx2_ref[...] + _rms_bwd(dn * g2_ref[...], xh, r)
        dx1_ref[...] = dx1
        oh, ro = _rms(o_ref[...])
        _acc(dgt_ref, first, _colsum(dx1 * (oh * gp_ref[...])))
        dno = dx1 * gt_ref[...]
        _acc(dgp_ref, first, _colsum(dno * oh))
        do_ref[...] = _rms_bwd(dno * gp_ref[...], oh, ro).astype(do_ref.dtype)

    vec = jax.ShapeDtypeStruct((1, d), _F32)
    return _row_call(body, "bwd_mid", s,
                     [(dh2, "rows"), (x1, "rows"), (dx2, "rows"), (o, "rows"), (g_pre2, "vec"), (sc2, "vec"),
                      (gt1, "vec"), (g_post, "vec")],
                     [(jax.ShapeDtypeStruct((s, d), _F32), "rows"), (jax.ShapeDtypeStruct((s, d), _MXU), "rows"),
                      (vec, "vec"), (vec, "vec"), (vec, "vec"), (vec, "vec"), (vec, "vec")])


def _bwd_pre_mix(dh1, x, dx1, g, sc1):
    s, d = x.shape

    def body(dh_ref, x_ref, dx1_ref, g_ref, sc_ref, dx_ref, dsc_ref, dsh_ref, dg_ref):
        first = pl.program_id(0) == 0
        _zero_first(first, dsc_ref, dsh_ref, dg_ref)
        dh = dh_ref[...]
        xh, r = _rms(x_ref[...])
        _acc(dsh_ref, first, _colsum(dh))
        _acc(dsc_ref, first, _colsum(dh * (xh * g_ref[...])))
        dn = dh * (1.0 + sc_ref[...])
        _acc(dg_ref, first, _colsum(dn * xh))
        dx_ref[...] = dx1_ref[...] + _rms_bwd(dn * g_ref[...], xh, r)

    vec = jax.ShapeDtypeStruct((1, d), _F32)
    return _row_call(body, "bwd_pre_mix", s,
                     [(dh1, "rows"), (x, "rows"), (dx1, "rows"), (g, "vec"), (sc1, "vec")],
                     [(jax.ShapeDtypeStruct((s, d), _F32), "rows"), (vec, "vec"), (vec, "vec"), (vec, "vec")])


def _mix_norm_fwd(y_ssm, y_sgu, g_ssm, g_sgu):
    s, h = y_ssm.shape

    def body(a_ref, b_ref, ga_ref, gb_ref, o_ref):
        ah, _ = _rms(a_ref[...])
        bh, _ = _rms(b_ref[...])
        o_ref[:, 0:h] = (ah * ga_ref[...]).astype(o_ref.dtype)
        o_ref[:, h:2 * h] = (bh * gb_ref[...]).astype(o_ref.dtype)

    return _row_call(body, "mix_norm_fwd", s, [(y_ssm, "rows"), (y_sgu, "rows"), (g_ssm, "vec"), (g_sgu, "vec")],
                     [(jax.ShapeDtypeStruct((s, 2 * h), _MXU), "rows")])[0]


def _mix_norm_bwd(dyc, y_ssm, y_sgu, g_ssm, g_sgu):
    s, h = y_ssm.shape

    def body(d_ref, a_ref, b_ref, ga_ref, gb_ref, da_ref, db_ref, dga_ref, dgb_ref):
        first = pl.program_id(0) == 0
        _zero_first(first, dga_ref, dgb_ref)
        for lo, y_ref, g_ref, dy_ref, dg_ref in ((0, a_ref, ga_ref, da_ref, dga_ref), (h, b_ref, gb_ref, db_ref, dgb_ref)):
            d = d_ref[:, lo:lo + h]
            yh, r = _rms(y_ref[...])
            _acc(dg_ref, first, _colsum(d * yh))
            dy_ref[...] = _rms_bwd(d * g_ref[...], yh, r)

    vec = jax.ShapeDtypeStruct((1, h), _F32)
    full = jax.ShapeDtypeStruct((s, h), _F32)
    return _row_call(body, "mix_norm_bwd", s,
                     [(dyc, "rows"), (y_ssm, "rows"), (y_sgu, "rows"), (g_ssm, "vec"), (g_sgu, "vec")],
                     [(full, "rows"), (full, "rows"), (vec, "vec"), (vec, "vec")])


CONV_ROWS = 64


def _conv_rows(ext, w_ref, b_ref):
    x = ext[SUBLANES:]
    s1 = pltpu.roll(ext, 1, 0)[SUBLANES:]
    s2 = pltpu.roll(ext, 2, 0)[SUBLANES:]
    return b_ref[...] + w_ref[0:1, :] * s2 + w_ref[1:2, :] * s1 + w_ref[2:3, :] * x, x, s1, s2


def _conv_window(x_ref, r0):
    if isinstance(r0, int):
        assert r0 == 0
        return jnp.concatenate([jnp.zeros((SUBLANES, x_ref.shape[1]), _F32), x_ref[0:CONV_ROWS, :]], axis=0)
    return x_ref[pl.ds(pl.multiple_of(r0 - SUBLANES, SUBLANES), CONV_ROWS + SUBLANES), :]


def _conv_act_fwd(up_pre, conv_w, conv_b):
    s, f2 = up_pre.shape
    f = f2 // 2
    tc = _tile(f, 256)
    nf = f // tc

    def shift_down(x, k):
        row = lax.broadcasted_iota(jnp.int32, x.shape, 0)
        return jnp.where(row >= k, pltpu.roll(x, k, 0), 0.0)

    def conv(x, w_ref, b_ref):
        return b_ref[...] + w_ref[0:1, :] * shift_down(x, 2) + w_ref[1:2, :] * shift_down(x, 1) + w_ref[2:3, :] * x

    def body(a_ref, b_ref, wa_ref, wb_ref, ba_ref, bb_ref, o_ref):
        a = conv(a_ref[...], wa_ref, ba_ref)
        b = conv(b_ref[...], wb_ref, bb_ref)
        o_ref[...] = (a * _sigmoid(a) * b).astype(o_ref.dtype)

    return pl.pallas_call(
        body, name="conv_act_fwd", grid=(nf,), out_shape=jax.ShapeDtypeStruct((s, f), _MXU),
        in_specs=[pl.BlockSpec((s, tc), lambda n: (0, n)), pl.BlockSpec((s, tc), lambda n: (0, n + nf)),
                  pl.BlockSpec((3, tc), lambda n: (0, n)), pl.BlockSpec((3, tc), lambda n: (0, n + nf)),
                  pl.BlockSpec((1, tc), lambda n: (0, n)), pl.BlockSpec((1, tc), lambda n: (0, n + nf))],
        out_specs=pl.BlockSpec((s, tc), lambda n: (0, n)), compiler_params=_cp("parallel"),
    )(up_pre, up_pre, conv_w, conv_w, conv_b, conv_b)


def _conv_act_bwd(up_pre, d_act, conv_w, conv_b):
    s, f2 = up_pre.shape
    f = f2 // 2
    tc = _tile(f, 256)
    nf = f // tc

    def body(a_ref, b_ref, d_ref, wa_ref, wb_ref, ba_ref, bb_ref,
             du_ref, w0a, w0b, w1a, w1b, w2a, w2b, dba, dbb):
        n = s // CONV_ROWS
        zero8 = jnp.zeros((SUBLANES, tc), _F32)
        ext_rows = CONV_ROWS + SUBLANES

        def fold(x):
            out = x[0:SUBLANES]
            for k in range(1, CONV_ROWS // SUBLANES):
                out = out + x[k * SUBLANES:(k + 1) * SUBLANES]
            return out

        def chunk(r0, carry):
            nxt, acc = carry
            a, xa, xa1, xa2 = _conv_rows(_conv_window(a_ref, r0), wa_ref, ba_ref)
            b, xb, xb1, xb2 = _conv_rows(_conv_window(b_ref, r0), wb_ref, bb_ref)
            sg = _sigmoid(a)
            d = d_ref[pl.ds(r0, CONV_ROWS), :]
            du_a = d * b * (sg * (1.0 + a * (1.0 - sg)))
            du_b = d * (a * sg)
            new_acc = []
            for h, (du, x0, x1, x2, w_ref) in enumerate(((du_a, xa, xa1, xa2, wa_ref), (du_b, xb, xb1, xb2, wb_ref))):
                ext = jnp.concatenate([du, nxt[h]], axis=0)
                u1 = pltpu.roll(ext, ext_rows - 1, 0)[:CONV_ROWS]
                u2 = pltpu.roll(ext, ext_rows - 2, 0)[:CONV_ROWS]
                du_ref[h, pl.ds(r0, CONV_ROWS), :] = (w_ref[2:3, :] * du + w_ref[1:2, :] * u1
                                                      + w_ref[0:1, :] * u2).astype(du_ref.dtype)
                new_acc += [acc[4 * h] + fold(du * x2), acc[4 * h + 1] + fold(du * x1), acc[4 * h + 2] + fold(du * x0),
                            acc[4 * h + 3] + fold(du)]
            return (du_a[:SUBLANES], du_b[:SUBLANES]), tuple(new_acc)

        def step(i, carry):
            return chunk(pl.multiple_of((n - 1 - i) * CONV_ROWS, CONV_ROWS), carry)

        carry = lax.fori_loop(0, n - 1, step, ((zero8, zero8), (zero8,) * 8))
        _, acc = chunk(0, carry)
        for ref, val in zip((w0a, w1a, w2a, dba, w0b, w1b, w2b, dbb), acc):
            ref[...] = _colsum(val)

    col_a = pl.BlockSpec((s, tc), lambda n: (0, n))
    col_b = pl.BlockSpec((s, tc), lambda n: (0, n + nf))
    vec_a = pl.BlockSpec((1, tc), lambda n: (0, n))
    vec_b = pl.BlockSpec((1, tc), lambda n: (0, n + nf))
    vec = jax.ShapeDtypeStruct((1, f), _F32)
    outs = pl.pallas_call(
        body, name="conv_act_bwd", grid=(nf,),
        in_specs=[col_a, col_b, col_a, pl.BlockSpec((3, tc), lambda n: (0, n)),
                  pl.BlockSpec((3, tc), lambda n: (0, n + nf)), vec_a, vec_b],
        out_specs=[pl.BlockSpec((2, s, tc), lambda n: (0, 0, n))] + [vec_a] * 8,
        out_shape=[jax.ShapeDtypeStruct((2, s, f), _MXU)] + [vec] * 8, compiler_params=_cp("parallel"),
    )(up_pre, up_pre, d_act, conv_w, conv_w, conv_b, conv_b)
    du, w0a, w0b, w1a, w1b, w2a, w2b, dba, dbb = outs
    cat = lambda p, q: jnp.concatenate([p, q], axis=1)
    return du, cat(w0a, w0b), cat(w1a, w1b), cat(w2a, w2b), cat(dba, dbb)


def _sgu_recompute(zu_ref, zv_ref, lng_ref, lnb_ref, wm_ref, bs_ref, nh):
    zu, zv = zu_ref[...], zv_ref[...]
    u = _gelu(zu)
    gv = _gelu(zv)
    xc = gv - _rowmean(gv)
    rs = lax.rsqrt(_rowmean(xc * xc) + EPS)
    vh = xc * rs
    v = vh * lng_ref[...] + lnb_ref[...]
    mixed = []
    for h in range(nh):
        vhd = v[:, h * CHUNK:(h + 1) * CHUNK].astype(_MXU)
        mixed.append(jnp.dot(wm_ref[h].astype(_MXU), vhd, preferred_element_type=_F32) + bs_ref[h])
    return zu, zv, u, vh, rs, v, mixed


def _sgu_fwd(z, ln_g, ln_b, wm, bs):
    s = z.shape[0]
    nh = wm.shape[0]
    hd = nh * CHUNK

    def body(zu_ref, zv_ref, lng_ref, lnb_ref, wm_ref, bs_ref, y_ref):
        _, _, u, _, _, _, mixed = _sgu_recompute(zu_ref, zv_ref, lng_ref, lnb_ref, wm_ref, bs_ref, nh)
        for h in range(nh):
            y_ref[:, h * CHUNK:(h + 1) * CHUNK] = u[:, h * CHUNK:(h + 1) * CHUNK] * mixed[h]

    vec = pl.BlockSpec((1, hd), lambda i: (0, 0))
    return pl.pallas_call(
        body, name="sgu_fwd", grid=(s // CHUNK,), out_shape=jax.ShapeDtypeStruct((s, hd), _F32),
        in_specs=[pl.BlockSpec((CHUNK, hd), lambda i: (i, 1)), pl.BlockSpec((CHUNK, hd), lambda i: (i, 2)), vec, vec,
                  pl.BlockSpec((nh, CHUNK, CHUNK), lambda i: (0, 0, 0)), pl.BlockSpec((nh, CHUNK, 1), lambda i: (0, 0, 0))],
        out_specs=pl.BlockSpec((CHUNK, hd), lambda i: (i, 0)), compiler_params=_cp("parallel"),
    )(z, z, ln_g, ln_b, wm, bs)


def _sgu_bwd(z, dy, dz_ssm, ln_g, ln_b, wm, bs):
    s = z.shape[0]
    nh = wm.shape[0]
    hd = nh * CHUNK

    def body(zu_ref, zv_ref, dy_ref, dzs_ref, lng_ref, lnb_ref, wm_ref, bs_ref,
             dz_ref, dlg_ref, dlb_ref, dwm_ref, dbs_ref, dv_scr):
        first = pl.program_id(0) == 0
        _zero_first(first, dlg_ref, dlb_ref, dwm_ref, dbs_ref)
        zu, zv, u, vh, rs, v, mixed = _sgu_recompute(zu_ref, zv_ref, lng_ref, lnb_ref, wm_ref, bs_ref, nh)
        dy = dy_ref[...]
        dz_ref[:, 0:hd] = dzs_ref[...]
        for h in range(nh):
            cols = slice(h * CHUNK, (h + 1) * CHUNK)
            dyh = dy[:, cols]
            dz_ref[:, hd + h * CHUNK:hd + (h + 1) * CHUNK] = (dyh * mixed[h] * _gelu_grad(zu[:, cols])).astype(dz_ref.dtype)
            dm = dyh * u[:, cols]
            dmx = dm.astype(_MXU)
            _acc(dbs_ref.at[h], first, jnp.sum(dm, axis=1, keepdims=True))
            _acc(dwm_ref.at[h], first,
                 lax.dot_general(dmx, v[:, cols].astype(_MXU), (((1,), (1,)), ((), ())), preferred_element_type=_F32))
            dv_scr[:, cols] = lax.dot_general(wm_ref[h].astype(_MXU), dmx, (((0,), (0,)), ((), ())),
                                              preferred_element_type=_F32)
        dv = dv_scr[...]
        _acc(dlg_ref, first, _colsum(dv * vh))
        _acc(dlb_ref, first, _colsum(dv))
        dvh = dv * lng_ref[...]
        dgv = rs * (dvh - _rowmean(dvh) - vh * _rowmean(dvh * vh))
        dz_ref[:, 2 * hd:3 * hd] = (dgv * _gelu_grad(zv)).astype(dz_ref.dtype)

    vec = pl.BlockSpec((1, hd), lambda i: (0, 0))
    wspec = pl.BlockSpec((nh, CHUNK, CHUNK), lambda i: (0, 0, 0))
    bspec = pl.BlockSpec((nh, CHUNK, 1), lambda i: (0, 0, 0))
    rows = pl.BlockSpec((CHUNK, hd), lambda i: (i, 0))
    return pl.pallas_call(
        body, name="sgu_bwd", grid=(s // CHUNK,),
        out_shape=[jax.ShapeDtypeStruct((s, 3 * hd), _MXU), jax.ShapeDtypeStruct((1, hd), _F32),
                   jax.ShapeDtypeStruct((1, hd), _F32), jax.ShapeDtypeStruct((nh, CHUNK, CHUNK), _F32),
                   jax.ShapeDtypeStruct((nh, CHUNK, 1), _F32)],
        in_specs=[pl.BlockSpec((CHUNK, hd), lambda i: (i, 1)), pl.BlockSpec((CHUNK, hd), lambda i: (i, 2)),
                  rows, rows, vec, vec, wspec, bspec],
        out_specs=[pl.BlockSpec((CHUNK, 3 * hd), lambda i: (i, 0)), vec, vec, wspec, bspec],
        scratch_shapes=[pltpu.VMEM((CHUNK, hd), _F32)], compiler_params=_cp("arbitrary"),
    )(z, z, dy, dz_ssm, ln_g, ln_b, wm, bs)


def _ssm_prep(log_dt, a_re, a_im, b_re_t, b_im_t):
    gn = a_re.shape[1]

    def body(ldt_ref, are_ref, aim_ref, br_ref, bi_ref, pr_ref, pi_ref, bbr_ref, bbi_ref):
        dt = jnp.exp(ldt_ref[...])
        are, aim = are_ref[...], aim_ref[...]
        k = (lax.broadcasted_iota(jnp.int32, (SUBLANES, gn), 0) + 1).astype(_F32)
        mag = jnp.exp(k * (are * dt))
        ang = k * (aim * dt)
        pr_ref[...] = mag * jnp.cos(ang)
        pi_ref[...] = mag * jnp.sin(ang)
        m1 = jnp.exp(are * dt)
        lr, li = m1 * jnp.cos(aim * dt), m1 * jnp.sin(aim * dt)
        den = are * are + aim * aim
        nr = lr - 1.0
        f_re = (nr * are + li * aim) / den
        f_im = (li * are - nr * aim) / den
        bbr_ref[...] = f_re * br_ref[...] - f_im * bi_ref[...]
        bbi_ref[...] = f_re * bi_ref[...] + f_im * br_ref[...]

    pw = jax.ShapeDtypeStruct((SUBLANES, gn), _F32)
    bb = jax.ShapeDtypeStruct(b_re_t.shape, _F32)
    return pl.pallas_call(body, name="ssm_prep", out_shape=[pw, pw, bb, bb])(log_dt, a_re, a_im, b_re_t, b_im_t)


def _ssm_prep_bwd(log_dt, a_re, a_im, b_re_t, b_im_t, d_bbr, d_bbi, d_lr, d_li):
    def body(ldt_ref, are_ref, aim_ref, br_ref, bi_ref, dbr_ref, dbi_ref, dlr_ref, dli_ref,
             obr_ref, obi_ref, oar_ref, oai_ref, odt_ref):
        dt = jnp.exp(ldt_ref[...])
        are, aim = are_ref[...], aim_ref[...]
        m1 = jnp.exp(are * dt)
        lr, li = m1 * jnp.cos(aim * dt), m1 * jnp.sin(aim * dt)
        den = are * are + aim * aim
        nr = lr - 1.0
        f_re = (nr * are + li * aim) / den
        f_im = (li * are - nr * aim) / den
        br, bi, dbr, dbi = br_ref[...], bi_ref[...], dbr_ref[...], dbi_ref[...]
        obr_ref[...] = f_re * dbr + f_im * dbi
        obi_ref[...] = f_re * dbi - f_im * dbr
        gf_re = _colsum(br * dbr + bi * dbi)
        gf_im = _colsum(br * dbi - bi * dbr)
        il_re, il_im = are / den, -aim / den
        glb_re = dlr_ref[...] + (il_re * gf_re + il_im * gf_im)
        glb_im = dli_ref[...] + (il_re * gf_im - il_im * gf_re)
        q_re = -(f_re * il_re - f_im * il_im)
        q_im = -(f_re * il_im + f_im * il_re)
        gl_re = q_re * gf_re + q_im * gf_im
        gl_im = q_re * gf_im - q_im * gf_re
        gl_re = gl_re + dt * (lr * glb_re + li * glb_im)
        gl_im = gl_im + dt * (lr * glb_im - li * glb_re)
        w_re = are * lr - aim * li
        w_im = are * li + aim * lr
        oar_ref[...] = gl_re
        oai_ref[...] = gl_im
        odt_ref[...] = w_re * glb_re + w_im * glb_im

    bb = jax.ShapeDtypeStruct(b_re_t.shape, _F32)
    v = jax.ShapeDtypeStruct(a_re.shape, _F32)
    return pl.pallas_call(body, name="ssm_prep_bwd", out_shape=[bb, bb, v, v, v])(
        log_dt, a_re, a_im, b_re_t, b_im_t, d_bbr, d_bbi, d_lr, d_li)


def _group_sum(d_dt, log_dt):
    def body(d_ref, l_ref, o_ref):
        o_ref[...] = jnp.sum(d_ref[...], axis=1, keepdims=True) * jnp.exp(l_ref[...])

    return pl.pallas_call(body, name="ssm_dt_grad", out_shape=jax.ShapeDtypeStruct(log_dt.shape, _F32))(d_dt, log_dt)


def _scan_rows(src_ref, dst_ref, nrt, steps, ptab, carry0, reverse):
    ns = BLOCK_ST
    row = lax.broadcasted_iota(jnp.int32, (SUBLANES, ns), 0)
    pr, pi = ptab

    def body(i, carry):
        cr, ci = carry
        it = (nrt - 1 - i) if reverse else i
        r0 = pl.multiple_of(it * SUBLANES, SUBLANES)
        xr = src_ref[pl.ds(r0, SUBLANES), 0:ns]
        xi = src_ref[pl.ds(r0, SUBLANES), ns:2 * ns]
        for k, (ar, ai) in zip((1, 2, 4), steps):
            if reverse:
                keep = row < SUBLANES - k
                sr = jnp.where(keep, pltpu.roll(xr, SUBLANES - k, 0), 0.0)
                si = jnp.where(keep, pltpu.roll(xi, SUBLANES - k, 0), 0.0)
            else:
                keep = row >= k
                sr = jnp.where(keep, pltpu.roll(xr, k, 0), 0.0)
                si = jnp.where(keep, pltpu.roll(xi, k, 0), 0.0)
            xr, xi = xr + ar * sr - ai * si, xi + ar * si + ai * sr
        xr, xi = xr + pr * cr - pi * ci, xi + pr * ci + pi * cr
        dst_ref[pl.ds(r0, SUBLANES), 0:ns] = xr
        dst_ref[pl.ds(r0, SUBLANES), ns:2 * ns] = xi
        if reverse:
            return xr[0:1, :], xi[0:1, :]
        return xr[SUBLANES - 1:SUBLANES, :], xi[SUBLANES - 1:SUBLANES, :]

    return lax.fori_loop(0, nrt, body, carry0)


def _scan_consts(p_ref, conj):
    ns = BLOCK_ST
    sign = -1.0 if conj else 1.0
    bc = lambda r: jnp.broadcast_to(r, (SUBLANES, ns))
    steps = [(bc(p_ref[k - 1:k, 0:ns]), bc(sign * p_ref[k - 1:k, ns:2 * ns])) for k in (1, 2, 4)]
    return steps


def _ssm_block_fwd(u, bbt_ref, ct_ref, d_ref, wg_ref, bg_ref, p_ref, bu_scr, h_scr, carry_in, nrt):
    ns = BLOCK_ST
    bu_scr[...] = jnp.dot(u.astype(_MXU), bbt_ref[...].astype(_MXU), preferred_element_type=_F32)
    ptab = (p_ref[:, 0:ns], p_ref[:, ns:2 * ns])
    carry = _scan_rows(bu_scr, h_scr, nrt, _scan_consts(p_ref, False), ptab, carry_in, False)
    y = jnp.dot(h_scr[...].astype(_MXU), ct_ref[...].astype(_MXU), preferred_element_type=_F32) + d_ref[...] * u
    yg = _gelu(y)
    gate = _sigmoid(jnp.dot(yg.astype(_MXU), wg_ref[...].astype(_MXU), preferred_element_type=_F32) + bg_ref[...])
    return y, yg, gate, carry


def _ssm_specs(nb, nt, t, reverse):
    tt = (lambda ti: nt - 1 - ti) if reverse else (lambda ti: ti)
    ns2 = 2 * BLOCK_ST
    return dict(
        z=pl.BlockSpec((t, BLOCK_CH), lambda b, ti: (tt(ti), b)),
        bbt=pl.BlockSpec((None, BLOCK_CH, ns2), lambda b, ti: (b, 0, 0)),
        ct=pl.BlockSpec((None, ns2, BLOCK_CH), lambda b, ti: (b, 0, 0)),
        vec=pl.BlockSpec((1, BLOCK_CH), lambda b, ti: (0, b)),
        wg=pl.BlockSpec((None, BLOCK_CH, BLOCK_CH), lambda b, ti: (b, 0, 0)),
        p=pl.BlockSpec((None, SUBLANES, ns2), lambda b, ti: (b, 0, 0)),
        hb=pl.BlockSpec((None, None, SUBLANES, ns2), lambda b, ti: (b, tt(ti), 0, 0)),
        h=pl.BlockSpec((None, t, ns2), lambda b, ti: (b, tt(ti), 0)),
        acc_vec=pl.BlockSpec((None, 1, ns2), lambda b, ti: (b, 0, 0)),
    )


def _ssm_fwd(z, bbt, ct, dvec, wg, bglu, ptab):
    s = z.shape[0]
    nb = bbt.shape[0]
    t = _tile(s, TIME_TILE, SUBLANES)
    nt = s // t
    ns = BLOCK_ST
    sp = _ssm_specs(nb, nt, t, False)

    def body(z_ref, bbt_ref, ct_ref, d_ref, wg_ref, bg_ref, p_ref, y2_ref, y_ref, h_ref, hb_ref, bu_scr, h_scr, carry_scr):
        ti = pl.program_id(1)

        @pl.when(ti == 0)
        def _():
            carry_scr[...] = jnp.zeros_like(carry_scr)

        hb_ref[...] = carry_scr[...]
        carry_in = (carry_scr[0:1, 0:ns], carry_scr[0:1, ns:2 * ns])
        y, yg, gate, (cr, ci) = _ssm_block_fwd(z_ref[...], bbt_ref, ct_ref, d_ref, wg_ref, bg_ref, p_ref,
                                               bu_scr, h_scr, carry_in, t // SUBLANES)
        y2_ref[...] = yg * gate
        y_ref[...] = y
        h_ref[...] = h_scr[...].astype(h_ref.dtype)
        carry_scr[:, 0:ns] = jnp.broadcast_to(cr, (SUBLANES, ns))
        carry_scr[:, ns:2 * ns] = jnp.broadcast_to(ci, (SUBLANES, ns))

    ych = jax.ShapeDtypeStruct((s, nb * BLOCK_CH), _F32)
    return pl.pallas_call(
        body, name="ssm_fwd", grid=(nb, nt),
        out_shape=[ych, ych, jax.ShapeDtypeStruct((nb, s, 2 * ns), _MXU),
                   jax.ShapeDtypeStruct((nb, nt, SUBLANES, 2 * ns), _F32)],
        in_specs=[sp["z"], sp["bbt"], sp["ct"], sp["vec"], sp["wg"], sp["vec"], sp["p"]],
        out_specs=[sp["z"], sp["z"], sp["h"], sp["hb"]],
        scratch_shapes=[pltpu.VMEM((t, 2 * ns), _F32), pltpu.VMEM((t, 2 * ns), _F32), pltpu.VMEM((SUBLANES, 2 * ns), _F32)],
        compiler_params=_cp("parallel", "arbitrary"),
    )(z, bbt, ct, dvec, wg, bglu, ptab)


def _ssm_bwd(z, y_pre, h_all, dy2, hb, bbt, ct, dvec, wg, bglu, ptab, ptab_rev):
    s = z.shape[0]
    nb = bbt.shape[0]
    t = _tile(s, TIME_TILE, SUBLANES)
    nt = s // t
    ns = BLOCK_ST
    sp = _ssm_specs(nb, nt, t, True)
    tn_dims = (((0,), (0,)), ((), ()))
    nt_dims = (((1,), (1,)), ((), ()))

    def body(z_ref, y_ref, h_ref, dy2_ref, hb_ref, bbt_ref, ct_ref, d_ref, wg_ref, bg_ref, p_ref, pr_ref,
             dz_ref, dbbt_ref, dct_ref, dwg_ref, dlb_ref, dd_ref, dbg_ref, bu_scr, g_scr, gcarry_scr):
        first = pl.program_id(1) == 0

        _zero_first(first, gcarry_scr, dbbt_ref, dct_ref, dwg_ref, dlb_ref, dd_ref, dbg_ref)
        u = z_ref[...]
        hin = hb_ref[...]
        y = y_ref[...]
        yg = _gelu(y)
        gate = _sigmoid(jnp.dot(yg.astype(_MXU), wg_ref[...].astype(_MXU), preferred_element_type=_F32) + bg_ref[...])
        dy2 = dy2_ref[...]
        dpre = dy2 * yg * gate * (1.0 - gate)
        _acc(dbg_ref, first, _colsum(dpre))
        dpx = dpre.astype(_MXU)
        _acc(dwg_ref, first, lax.dot_general(yg.astype(_MXU), dpx, tn_dims, preferred_element_type=_F32))
        dyg = dy2 * gate + lax.dot_general(dpx, wg_ref[...].astype(_MXU), nt_dims, preferred_element_type=_F32)
        dy = dyg * _gelu_grad(y)
        _acc(dd_ref, first, _colsum(dy * u))
        dyx = dy.astype(_MXU)
        hx = h_ref[...]
        h = hx.astype(_F32)
        _acc(dct_ref, first, lax.dot_general(hx, dyx, tn_dims, preferred_element_type=_F32))
        bu_scr[...] = lax.dot_general(dyx, ct_ref[...].astype(_MXU), nt_dims, preferred_element_type=_F32)
        gin = (gcarry_scr[0:1, 0:ns], gcarry_scr[0:1, ns:2 * ns])
        ptab = (pr_ref[:, 0:ns], pr_ref[:, ns:2 * ns])
        gr, gi = _scan_rows(bu_scr, g_scr, t // SUBLANES, _scan_consts(p_ref, True), ptab, gin, True)
        gcarry_scr[:, 0:ns] = jnp.broadcast_to(gr, (SUBLANES, ns))
        gcarry_scr[:, ns:2 * ns] = jnp.broadcast_to(gi, (SUBLANES, ns))
        g = g_scr[...]
        row = lax.broadcasted_iota(jnp.int32, (t, ns), 0)
        hp_re = jnp.where(row == 0, hin[0:1, 0:ns], pltpu.roll(h[:, 0:ns], 1, 0))
        hp_im = jnp.where(row == 0, hin[0:1, ns:2 * ns], pltpu.roll(h[:, ns:2 * ns], 1, 0))
        g_re, g_im = g[:, 0:ns], g[:, ns:2 * ns]
        d_ar = _colsum(g_re * hp_re + g_im * hp_im)
        d_ai = _colsum(g_im * hp_re - g_re * hp_im)
        _acc(dlb_ref, first, jnp.concatenate([d_ar, d_ai], axis=1))
        gx = g.astype(_MXU)
        _acc(dbbt_ref, first, lax.dot_general(u.astype(_MXU), gx, tn_dims, preferred_element_type=_F32))
        dz_ref[...] = (dy * d_ref[...] + lax.dot_general(gx, bbt_ref[...].astype(_MXU), nt_dims,
                                                         preferred_element_type=_F32)).astype(dz_ref.dtype)

    f = lambda shape: jax.ShapeDtypeStruct(shape, _F32)
    return pl.pallas_call(
        body, name="ssm_bwd", grid=(nb, nt),
        out_shape=[jax.ShapeDtypeStruct((s, nb * BLOCK_CH), _MXU), f(bbt.shape), f(ct.shape), f(wg.shape), f((nb, 1, 2 * ns)),
                   f((1, nb * BLOCK_CH)), f((1, nb * BLOCK_CH))],
        in_specs=[sp["z"], sp["z"], sp["h"], sp["z"], sp["hb"], sp["bbt"], sp["ct"], sp["vec"], sp["wg"], sp["vec"], sp["p"],
                  sp["p"]],
        out_specs=[sp["z"], sp["bbt"], sp["ct"], sp["wg"], sp["acc_vec"], sp["vec"], sp["vec"]],
        scratch_shapes=[pltpu.VMEM((t, 2 * ns), _F32), pltpu.VMEM((t, 2 * ns), _F32), pltpu.VMEM((SUBLANES, 2 * ns), _F32)],
        compiler_params=_cp("parallel", "arbitrary"),
    )(z, y_pre, h_all, dy2, hb, bbt, ct, dvec, wg, bglu, ptab, ptab_rev)


def _mod_part(c_all, w, b):
    d, ns = w.shape
    tn = _tile(ns, 512)

    def body(c_ref, w_ref, b_ref, o_ref):
        c = c_ref[...]
        ca = (c * _sigmoid(c)).astype(_MXU)
        o_ref[...] = jnp.dot(ca, w_ref[...].astype(_MXU), preferred_element_type=_F32) + b_ref[...]

    return pl.pallas_call(
        body, name="mod_part", grid=(ns // tn,), out_shape=jax.ShapeDtypeStruct((8, ns), _F32),
        in_specs=[pl.BlockSpec((8, d), lambda n: (0, 0)), pl.BlockSpec((d, tn), lambda n: (0, n)),
                  pl.BlockSpec((1, tn), lambda n: (0, n))],
        out_specs=pl.BlockSpec((8, tn), lambda n: (0, n)), compiler_params=_cp("parallel"),
    )(c_all, w, b)


def _adamw_math(w, g, m, v):
    m = ADAM_B1 * m + (1.0 - ADAM_B1) * g
    v = ADAM_B2 * v + (1.0 - ADAM_B2) * (g * g)
    m_hat = m / (1.0 - ADAM_B1 ** ADAM_STEP)
    v_hat = v / (1.0 - ADAM_B2 ** ADAM_STEP)
    delta = -ADAM_LR * (m_hat / (jnp.sqrt(v_hat) + ADAM_EPS) + ADAM_WD * w)
    return delta, m, v


def _adamw(w, g, m, v, name):
    r, c = w.shape
    tc = c if c <= 4096 else _tile(c, 4096)
    tr = _tile(r, max(SUBLANES, (1 << 18) // tc), SUBLANES)

    def body(w_ref, g_ref, m_ref, v_ref, go_ref, d_ref, mo_ref, vo_ref):
        g = g_ref[...]
        go_ref[...] = g
        d_ref[...], mo_ref[...], vo_ref[...] = _adamw_math(w_ref[...], g, m_ref[...], v_ref[...])

    spec = pl.BlockSpec((tr, tc), lambda i, j: (i, j))
    out = jax.ShapeDtypeStruct((r, c), _F32)
    return pl.pallas_call(
        body, name=name, grid=(r // tr, c // tc), in_specs=[spec] * 4, out_specs=[spec] * 4, out_shape=[out] * 4,
        compiler_params=_cp("parallel", "parallel"),
    )(w, g, m, v)


def _adamw_halves(w, g2, m, v, name):
    r, c = w.shape
    tr, tc = _tile(r, 256, SUBLANES), _tile(c // 2, 1024)
    nph = (c // 2) // tc

    def body(w_ref, g_ref, m_ref, v_ref, go_ref, d_ref, mo_ref, vo_ref):
        g = g_ref[...]
        go_ref[...] = g
        d_ref[...], mo_ref[...], vo_ref[...] = _adamw_math(w_ref[...], g, m_ref[...], v_ref[...])

    spec = pl.BlockSpec((tr, tc), lambda i, j: (i, j))
    out = jax.ShapeDtypeStruct((r, c), _F32)
    return pl.pallas_call(
        body, name=name, grid=(r // tr, c // tc),
        in_specs=[spec, pl.BlockSpec((None, tr, tc), lambda i, j: (j // nph, i, j % nph)), spec, spec],
        out_specs=[spec] * 4, out_shape=[out] * 4, compiler_params=_cp("parallel", "parallel"),
    )(w, g2, m, v)


def _wada_update(c_t, dm, w, m, v):
    d, ns = w.shape
    tr, tc = _tile(d, 256, SUBLANES), _tile(ns, 1024)

    def body(c_ref, dm_ref, w_ref, m_ref, v_ref, g_ref, d_ref, mo_ref, vo_ref):
        c = c_ref[...]
        ca = c * _sigmoid(c)
        dmv = dm_ref[...]
        g = ca[:, 0:1] * dmv[0:1, :]
        for b in range(1, 8):
            g = g + ca[:, b:b + 1] * dmv[b:b + 1, :]
        g_ref[...] = g
        d_ref[...], mo_ref[...], vo_ref[...] = _adamw_math(w_ref[...], g, m_ref[...], v_ref[...])

    spec = pl.BlockSpec((tr, tc), lambda i, j: (i, j))
    out = jax.ShapeDtypeStruct((d, ns), _F32)
    return pl.pallas_call(
        body, name="wada_update", grid=(d // tr, ns // tc),
        in_specs=[pl.BlockSpec((tr, 8), lambda i, j: (i, 0)), pl.BlockSpec((8, tc), lambda i, j: (0, j)), spec, spec, spec],
        out_specs=[spec] * 4, out_shape=[out] * 4, compiler_params=_cp("parallel", "parallel"),
    )(c_t, dm, w, m, v)


def _small_reduce(gathered):
    _, r, c = gathered.shape
    tr = _tile(r, 512, SUBLANES)

    def body(q_ref, g_ref):
        g = q_ref[0]
        for k in range(1, 8):
            g = g + q_ref[k]
        g_ref[...] = g

    return pl.pallas_call(
        body, name="small_reduce", grid=(r // tr,), out_shape=jax.ShapeDtypeStruct((r, c), _F32),
        in_specs=[pl.BlockSpec((8, tr, c), lambda i: (0, i, 0))], out_specs=pl.BlockSpec((tr, c), lambda i: (i, 0)),
        compiler_params=_cp("parallel"),
    )(gathered)


def _adamw_many(ws, gs, ms, vs, steps, name):
    n = len(ws)

    def body(*refs):
        w_refs, g_refs, m_refs, v_refs = refs[0:n], refs[n:2 * n], refs[2 * n:3 * n], refs[3 * n:4 * n]
        d_refs, mo_refs, vo_refs = refs[4 * n:5 * n], refs[5 * n:6 * n], refs[6 * n:7 * n]
        for i in range(n):
            d_refs[i][...], mo_refs[i][...], vo_refs[i][...] = _adamw_math(
                w_refs[i][...], g_refs[i][...], m_refs[i][...], v_refs[i][...])

    def spec(a):
        nd = a.ndim
        if steps == 1:
            return pl.BlockSpec(a.shape, lambda i: (0,) * nd)
        return pl.BlockSpec((a.shape[0] // steps,) + a.shape[1:], lambda i: (i,) + (0,) * (nd - 1))

    specs = [spec(w) for w in ws]
    outs = pl.pallas_call(
        body, name=name, grid=(steps,), in_specs=specs * 4, out_specs=specs * 3,
        out_shape=[jax.ShapeDtypeStruct(w.shape, _F32) for w in ws] * 3, compiler_params=_cp("parallel"),
    )(*ws, *gs, *ms, *vs)
    return outs[0:n], outs[n:2 * n], outs[2 * n:3 * n]


def _block_diag(x, eye=None):
    nb, g, p, q = x.shape
    eye = jnp.eye(g, dtype=x.dtype) if eye is None else eye
    return (x[:, :, :, None, :] * eye[None, :, None, :, None]).reshape(nb, g * p, g * q)


def _block_diag_take(x, p, q):
    nb = x.shape[0]
    g = GROUPS_PER_BLOCK
    eye = jnp.eye(g, dtype=x.dtype)
    return jnp.sum(x.reshape(nb, g, p, g, q) * eye[None, :, None, :, None], axis=3)


_VIEWS = {"ssm_b_re": ((0, 2, 1), (0, 2, 1)), "ssm_b_im": ((0, 2, 1), (0, 2, 1)),
          "ssm_w_glu": ((1, 2, 0), (2, 0, 1)), "ssm_b_glu": ((1, 0), (1, 0))}


def _to_view(name, a):
    return a.transpose(_VIEWS[name][0]) if name in _VIEWS else a


def _from_view(name, a):
    return a.transpose(_VIEWS[name][1]) if name in _VIEWS else a


class _Pack:
    def __init__(self, shapes):
        self.shapes = shapes
        self.offsets = {}
        off = 0
        for name, shape in shapes.items():
            n = math.prod(shape)
            self.offsets[name] = (off, n)
            off += -(-n // (SUBLANES * LANES)) * (SUBLANES * LANES)
        self.rows = -(-off // (256 * LANES)) * 256

    def pack(self, arrays):
        parts = []
        off = 0
        for name, shape in self.shapes.items():
            start, n = self.offsets[name]
            if start > off:
                parts.append(jnp.zeros((start - off,), _F32))
            parts.append(arrays[name].reshape(-1).astype(_F32))
            off = start + n
        total = self.rows * LANES
        if total > off:
            parts.append(jnp.zeros((total - off,), _F32))
        return jnp.concatenate(parts).reshape(self.rows, LANES)

    def unpack(self, buf):
        flat = buf.reshape(-1)
        return {name: flat[start:start + n].reshape(self.shapes[name]) for name, (start, n) in self.offsets.items()}


_SMALL = ["b_ada", "g_pre_mix", "g_post_mix", "ssm_log_dt", "ssm_a_re", "ssm_a_im", "ssm_b_re", "ssm_b_im", "ssm_c_re",
          "ssm_c_im", "ssm_d", "ssm_w_glu", "ssm_b_glu", "sgu_ln_g", "sgu_ln_b", "sgu_w", "sgu_b", "g_out_ssm",
          "g_out_sgu", "g_pre_ffn", "g_post_ffn", "conv_b"]
_WEIGHTS = ["w_ada", "b_ada", "g_pre_mix", "g_post_mix", "w_in", "ssm_log_dt", "ssm_a_re", "ssm_a_im", "ssm_b_re",
            "ssm_b_im", "ssm_c_re", "ssm_c_im", "ssm_d", "ssm_w_glu", "ssm_b_glu", "sgu_ln_g", "sgu_ln_b", "sgu_w", "sgu_b",
            "g_out_ssm", "g_out_sgu", "w_out", "g_pre_ffn", "g_post_ffn", "w_up", "conv_w", "conv_b", "w_down"]


def _step(p, m, v, x, c, tgt):
    s, d = x.shape
    mx, my, mc = lax.axis_index("x"), lax.axis_index("y"), lax.axis_index("c")
    chip = 2 * mx + my
    dev = 4 * mx + 2 * my + mc
    sel = jnp.stack([chip, mc]).astype(jnp.int32)
    g_cnt, n_st = p["ssm_a_re"].shape
    nb = g_cnt // GROUPS_PER_BLOCK
    gn = g_cnt * n_st
    d_ssm = g_cnt * SSM_GROUP
    nh = p["sgu_w"].shape[0]
    assert nh * CHUNK == d_ssm and 2 * d_ssm == d and n_st == SSM_STATE

    shards = lambda g: g.reshape(4, g.shape[1] * g.shape[2], g.shape[3])
    buf_in = _cast_into_slot(p["w_in"], sel, sel, "cast_w_in")

    ns_ada = p["w_ada"].shape[1]
    nc_conv = p["conv_w"].shape[1]
    first = jnp.concatenate([jnp.broadcast_to(c, (8, d)), jnp.pad(p["conv_w"], ((0, 5), (0, 0)))], axis=1)
    first_all = _all_gather8(_own_slot(first, dev), "gather_c_conv", after=buf_in)
    (sems_in,), (buf_in,), tok = _gather_start([buf_in], first_all, "gather_start_in")
    c_all = _after(first_all[:, 0, :d], tok)
    conv_w_full = jnp.concatenate([first_all[2 * j, 0:3, d:] for j in range(4)], axis=1)
    b_ada_mine = lax.dynamic_slice_in_dim(p["b_ada"], chip * ns_ada, ns_ada, axis=1)
    mod_mine = _mod_part(c_all, p["w_ada"], b_ada_mine)
    buf_out, buf_up, buf_down = [_cast_into_slot(p[n], sel, tok, "cast_" + n) for n in ("w_out", "w_up", "w_down")]

    eye_t = jnp.eye(GROUPS_PER_BLOCK, dtype=_F32) + tok[0:1, 0:1]
    ldt_l = _after(jnp.repeat(p["ssm_log_dt"], n_st, axis=1), tok)
    are_l, aim_l = p["ssm_a_re"].reshape(1, gn), p["ssm_a_im"].reshape(1, gn)
    bre_t, bim_t = p["ssm_b_re"].reshape(gn, SSM_GROUP).T, p["ssm_b_im"].reshape(gn, SSM_GROUP).T
    pw_re, pw_im, bb_re, bb_im = _ssm_prep(ldt_l, are_l, aim_l, bre_t, bim_t)
    blocks = lambda t: t.reshape(t.shape[0], nb, GROUPS_PER_BLOCK * n_st).transpose(1, 0, 2)
    ptab = jnp.concatenate([blocks(pw_re), blocks(pw_im)], axis=2)
    ptab_rev = jnp.concatenate([blocks(pw_re)[:, ::-1], -blocks(pw_im)[:, ::-1]], axis=2)
    bd = lambda t: t.reshape(SSM_GROUP, nb, GROUPS_PER_BLOCK, n_st).transpose(1, 2, 0, 3)
    bbt = jnp.concatenate([_block_diag(bd(bb_re)), _block_diag(bd(bb_im))], axis=2).astype(_MXU)
    cd = lambda t: t.reshape(nb, GROUPS_PER_BLOCK, SSM_GROUP, n_st).transpose(0, 1, 3, 2)
    ct = jnp.concatenate([_block_diag(cd(p["ssm_c_re"]), eye_t), -_block_diag(cd(p["ssm_c_im"]), eye_t)], axis=1).astype(_MXU)
    wg = _block_diag(p["ssm_w_glu"].reshape(nb, GROUPS_PER_BLOCK, SSM_GROUP, SSM_GROUP), eye_t).astype(_MXU)
    dvec = p["ssm_d"]
    bglu = p["ssm_b_glu"].reshape(1, d_ssm)
    mask = jnp.tril(jnp.ones((CHUNK, CHUNK), _F32)) + tok[0:1, 0:1]
    wm = (p["sgu_w"] * mask[None]).astype(_MXU)
    bs = p["sgu_b"].reshape(nh, CHUNK, 1)

    mod_all = _all_gather8(_own_slot(mod_mine, dev), "gather_mod",
                           after=[buf_out, buf_up, buf_down, bbt, ct, wg, wm, bs, ptab, ptab_rev])
    (sems_out, sems_up, sems_down), (buf_out, buf_up, buf_down), tok_rest = _gather_start(
        [buf_out, buf_up, buf_down], mod_all, "gather_start_rest")
    mod_rows = lax.dynamic_index_in_dim(mod_all, dev, axis=1, keepdims=False)
    mod = jnp.concatenate([mod_rows[0], mod_rows[2], mod_rows[4], mod_rows[6]]).reshape(N_MOD, 1, d)
    sh1, sc1, gt1, sh2, sc2, gt2 = [mod[i] for i in range(N_MOD)]

    h1 = _fwd_pre_mix(x, p["g_pre_mix"], _after(sc1, tok_rest), sh1)
    buf_in = _gather_wait(sems_in, buf_in, h1, "gather_wait_in")
    w_in4 = shards(_pair_forward([buf_in], "pair_forward_in")[0])
    z = _mm_nn(h1, w_in4, _F32, "mm_in")
    y_ssm, y_pre, h_all, hb = _ssm_fwd(z, bbt, ct, dvec, wg, bglu, ptab)
    y_sgu = _sgu_fwd(z, p["sgu_ln_g"], p["sgu_ln_b"], wm, bs)
    ycat = _mix_norm_fwd(y_ssm, y_sgu, p["g_out_ssm"], p["g_out_sgu"])
    buf_out = _gather_wait(sems_out, buf_out, ycat, "gather_wait_out")
    w_out_full = _pair_forward([buf_out], "pair_forward_out")[0].reshape(1, d, d)
    o = _mm_nn(ycat, w_out_full, _F32, "mm_out")
    x1, h2 = _fwd_mid(o, x, gt1, p["g_post_mix"], p["g_pre_ffn"], sc2, sh2)
    buf_up = _gather_wait(sems_up, buf_up, h2, "gather_wait_up")
    w_up4 = shards(_pair_forward([buf_up], "pair_forward_up")[0])
    up_pre = _mm_nn(h2, w_up4, _F32, "mm_up")
    act = _conv_act_fwd(up_pre, conv_w_full, p["conv_b"])
    buf_down = _gather_wait(sems_down, buf_down, act, "gather_wait_down")
    w_down_full = _pair_forward([buf_down], "pair_forward_down")[0].reshape(1, -1, d)
    f = _mm_nn(act, w_down_full, _F32, "mm_down", tk=5632)
    dx2, df, d_gt2, d_g_post_ffn, loss = _loss_and_post_ffn_bwd(f, x1, tgt, gt2, p["g_post_ffn"])

    def reduce_next(swap, n, after):
        sems, gw, land, _ = swap
        gw, got = _swap_wait(sems, gw, land, after, "swap_wait_" + n)
        return _scatter_start(_pair_sum(gw, got, sel, "pair_sum_" + n), "scatter_start_" + n)

    d_act = _mm_nt(df, w_down_full, _F32, "mm_d_act", tk=2048)
    swap_down = _swap_start(_mm_tn_rows(act, df, "mm_gw_down"), "swap_start_w_down")
    d_up_pre, d_cw0, d_cw1, d_cw2, d_conv_b = _conv_act_bwd(up_pre, d_act, conv_w_full, _after(p["conv_b"], swap_down[3]))
    red_down = reduce_next(swap_down, "w_down", d_conv_b)
    dh2 = _mm_nt(d_up_pre, w_up4, _F32, "mm_dh2", tk=2816, after=red_down[3])
    swap_up = _swap_start(_mm_tn_cols(h2, d_up_pre, "mm_gw_up"), "swap_start_w_up")
    dx1, d_o, d_sc2, d_sh2, d_g_pre_ffn, d_gt1, d_g_post_mix = _bwd_mid(
        dh2, x1, dx2, o, p["g_pre_ffn"], _after(sc2, swap_up[3]), gt1, p["g_post_mix"])
    red_up = reduce_next(swap_up, "w_up", d_g_post_mix)
    d_ycat = _mm_nt(d_o, w_out_full, _F32, "mm_d_ycat", tn=1024, tk=2048, after=red_up[3])
    swap_out = _swap_start(_mm_tn_rows(ycat, d_o, "mm_gw_out"), "swap_start_w_out")
    dy_ssm, dy_sgu, d_g_out_ssm, d_g_out_sgu = _mix_norm_bwd(
        d_ycat, y_ssm, y_sgu, _after(p["g_out_ssm"], swap_out[3]), p["g_out_sgu"])
    red_out = reduce_next(swap_out, "w_out", d_g_out_sgu)
    dz_ssm, d_bbt, d_ct, d_wg, d_lb, d_ssm_d, d_bglu = _ssm_bwd(z, y_pre, h_all, dy_ssm, hb, bbt, ct,
                                                                _after(dvec, red_out[3]), wg, bglu, ptab, ptab_rev)
    dz, d_ln_g, d_ln_b, d_wm, d_bs = _sgu_bwd(z, dy_sgu, dz_ssm, p["sgu_ln_g"], p["sgu_ln_b"], wm, bs)
    dh1 = _mm_nt(dz, w_in4, _F32, "mm_dh1")
    swap_in = _swap_start(_mm_tn_cols(h1, dz, "mm_gw_in"), "swap_start_w_in")
    dx, d_sc1, d_sh1, d_g_pre_mix = _bwd_pre_mix(dh1, x, dx1, p["g_pre_mix"], _after(sc1, swap_in[3]))
    red_in = reduce_next(swap_in, "w_in", d_g_pre_mix)

    nsb = BLOCK_ST
    lanes = lambda t: t.transpose(2, 0, 1, 3).reshape(SSM_GROUP, gn)
    d_bbr = lanes(_block_diag_take(d_bbt[:, :, :nsb], SSM_GROUP, n_st))
    d_bbi = lanes(_block_diag_take(d_bbt[:, :, nsb:], SSM_GROUP, n_st))
    d_lr, d_li = d_lb[:, 0, :nsb].reshape(1, gn), d_lb[:, 0, nsb:].reshape(1, gn)
    d_bre_t, d_bim_t, d_are, d_aim, d_dt = _ssm_prep_bwd(ldt_l, are_l, aim_l, bre_t, bim_t, d_bbr, d_bbi, d_lr, d_li)
    d_log_dt = _group_sum(d_dt.reshape(g_cnt, n_st), p["ssm_log_dt"].reshape(g_cnt, 1))
    c_grad = lambda t: _block_diag_take(t, n_st, SSM_GROUP).transpose(0, 1, 3, 2).reshape(g_cnt, SSM_GROUP, n_st)
    small = {
        "b_ada": jnp.concatenate([d_sh1, _after(d_sc1, red_in[3]), d_gt1, d_sh2, d_sc2, d_gt2], axis=1),
        "g_pre_mix": d_g_pre_mix, "g_post_mix": d_g_post_mix,
        "ssm_log_dt": d_log_dt, "ssm_a_re": d_are, "ssm_a_im": d_aim,
        "ssm_b_re": d_bre_t.T, "ssm_b_im": d_bim_t.T,
        "ssm_c_re": c_grad(d_ct[:, :nsb, :]), "ssm_c_im": -c_grad(d_ct[:, nsb:, :]),
        "ssm_d": d_ssm_d, "ssm_w_glu": _block_diag_take(d_wg, SSM_GROUP, SSM_GROUP), "ssm_b_glu": d_bglu,
        "sgu_ln_g": d_ln_g, "sgu_ln_b": d_ln_b, "sgu_w": d_wm * mask[None], "sgu_b": d_bs,
        "g_out_ssm": d_g_out_ssm, "g_out_sgu": d_g_out_sgu, "g_pre_ffn": d_g_pre_ffn, "g_post_ffn": d_g_post_ffn,
        "conv_b": d_conv_b, "conv_w_all": jnp.concatenate([d_cw0, d_cw1, d_cw2], axis=0),
        "loss_sum": loss,
    }
    small = {n: _to_view(n, a.reshape(p[n].shape)) if n in p else a for n, a in small.items()}
    pk = _Pack({n: a.shape for n, a in small.items()})
    sems_small, small_buf, tok = _gather8_start(_own_slot(pk.pack(small), dev), "gather_small_start")

    big = ["w_down", "w_up", "w_out", "w_in"]
    joins = []
    after = tok
    for n, (sems, pair, land, _) in zip(big, (red_down, red_up, red_out, red_in)):
        pair, land = _scatter_wait(sems, pair, land, after, "scatter_wait_" + n)
        sems_j, half, after = _join_start(_chip_sum(pair, land, sel, "chip_sum_" + n), "join_start_" + n)
        joins.append((sems_j, half))
    big_out = {}
    for n, (sems_j, half) in zip(big, joins):
        j = _join_wait(sems_j, half, after, "join_wait_" + n)
        if n in ("w_in", "w_up"):
            big_out[n] = tuple(_adamw(p[n], j.reshape(p[n].shape), m[n], v[n], "adamw_" + n))
        else:
            big_out[n] = tuple(_adamw_halves(p[n], j, m[n], v[n], "adamw_" + n))
        after = big_out[n][1]

    gathered = _gather8_forward(_gather8_wait(sems_small, small_buf, after, "gather_small_wait"),
                                "gather_small_forward")
    gview = pk.unpack(_small_reduce(gathered))
    gview["conv_w"] = lax.dynamic_slice_in_dim(gview.pop("conv_w_all"), chip * nc_conv, nc_conv, axis=1)
    loss = gview.pop("loss_sum")
    small_names = _SMALL + ["conv_w"]
    per_group = [n for n in small_names if gview[n].ndim >= 2 and gview[n].shape[0] == g_cnt]
    others = [n for n in small_names if n not in per_group]
    grads = {n: _from_view(n, gview[n]) for n in small_names}
    deltas, new_m, new_v = {}, {}, {}
    for names, steps, call in ((per_group, g_cnt // GROUPS_PER_BLOCK, "adamw_s5"), (others, 1, "adamw_small")):
        res = _adamw_many([_to_view(n, p[n]) for n in names], [gview[n] for n in names],
                          [_to_view(n, m[n]) for n in names], [_to_view(n, v[n]) for n in names], steps, call)
        for n, dl, mo, vo in zip(names, *res):
            deltas[n], new_m[n], new_v[n] = _from_view(n, dl), _from_view(n, mo), _from_view(n, vo)

    d_mod_all = gathered.reshape(8, -1)[:, :N_MOD * d]
    d_mod_mine = lax.dynamic_slice_in_dim(d_mod_all, chip * ns_ada, ns_ada, axis=1)
    grads["w_ada"], deltas["w_ada"], new_m["w_ada"], new_v["w_ada"] = _wada_update(
        c_all.T, d_mod_mine, p["w_ada"], m["w_ada"], v["w_ada"])
    for n in big:
        grads[n], deltas[n], new_m[n], new_v[n] = big_out[n]
    return loss[0, 0], dx, grads, deltas, new_m, new_v


def kernel(x, c, w_ada, b_ada, g_pre_mix, g_post_mix, w_in, ssm_log_dt, ssm_a_re, ssm_a_im, ssm_b_re, ssm_b_im, ssm_c_re, ssm_c_im, ssm_d, ssm_w_glu, ssm_b_glu, sgu_ln_g, sgu_ln_b, sgu_w, sgu_b, g_out_ssm, g_out_sgu, w_out, g_pre_ffn, g_post_ffn, w_up, conv_w, conv_b, w_down, loss_target, m_w_ada, m_b_ada, m_g_pre_mix, m_g_post_mix, m_w_in, m_ssm_log_dt, m_ssm_a_re, m_ssm_a_im, m_ssm_b_re, m_ssm_b_im, m_ssm_c_re, m_ssm_c_im, m_ssm_d, m_ssm_w_glu, m_ssm_b_glu, m_sgu_ln_g, m_sgu_ln_b, m_sgu_w, m_sgu_b, m_g_out_ssm, m_g_out_sgu, m_w_out, m_g_pre_ffn, m_g_post_ffn, m_w_up, m_conv_w, m_conv_b, m_w_down, v_w_ada, v_b_ada, v_g_pre_mix, v_g_post_mix, v_w_in, v_ssm_log_dt, v_ssm_a_re, v_ssm_a_im, v_ssm_b_re, v_ssm_b_im, v_ssm_c_re, v_ssm_c_im, v_ssm_d, v_ssm_w_glu, v_ssm_b_glu, v_sgu_ln_g, v_sgu_ln_b, v_sgu_w, v_sgu_b, v_g_out_ssm, v_g_out_sgu, v_w_out, v_g_pre_ffn, v_g_post_ffn, v_w_up, v_conv_w, v_conv_b, v_w_down):
    given = dict(locals())
    drop = lambda a: a if a.ndim == 2 else a[0]
    p = {n: drop(given[n]) for n in _WEIGHTS}
    m = {n: drop(given["m_" + n]) for n in _WEIGHTS}
    v = {n: drop(given["v_" + n]) for n in _WEIGHTS}
    loss, dx, grads, deltas, new_m, new_v = _step(p, m, v, x[0], c, loss_target[0])
    outs = [loss, dx[None]]
    for group in (grads, deltas, new_m, new_v):
        outs += [group[n].reshape(given[n].shape) for n in _WEIGHTS]
    return tuple(outs)
```

```python
import functools
import math

import jax
import jax.numpy as jnp
from jax import lax
from jax.experimental import pallas as pl
from jax.experimental.pallas import tpu as pltpu

_F32 = jnp.float32
_MXU = jnp.bfloat16
_WIRE = jnp.bfloat16

EPS = 1e-6
SSM_GROUP = 16
SSM_STATE = 64
GROUPS_PER_BLOCK = 8
BLOCK_CH = SSM_GROUP * GROUPS_PER_BLOCK
BLOCK_ST = SSM_STATE * GROUPS_PER_BLOCK
CHUNK = 128
TIME_TILE = 512
SUBLANES = 8
LANES = 128
N_MOD = 6
ADAM_LR, ADAM_B1, ADAM_B2, ADAM_EPS, ADAM_WD, ADAM_STEP = 0.001, 0.9, 0.999, 1e-08, 0.01, 10
_VMEM_LIMIT = 56 * 1024 * 1024
_MESH = pl.DeviceIdType.MESH
_ANY = pl.BlockSpec(memory_space=pl.ANY)
_HBM = pl.BlockSpec(memory_space=pltpu.HBM)
_SEM = pl.BlockSpec(memory_space=pltpu.SEMAPHORE)
_VMEM_WHOLE = pl.BlockSpec(memory_space=pltpu.VMEM)
_EFFECT = pltpu.SideEffectType.DATAFLOW_SIDE_EFFECTING
_GELU_C = math.sqrt(2.0 / math.pi)


def _cp(*sem):
    return pltpu.CompilerParams(dimension_semantics=sem, vmem_limit_bytes=_VMEM_LIMIT)


def _tile(dim, target, align=LANES):
    if dim <= target:
        return dim
    best = None
    for t in range(align, target + 1, align):
        if dim % t == 0:
            best = t
    assert best is not None, (dim, target, align)
    return best


def _gelu(x):
    return 0.5 * x * (1.0 + jnp.tanh(_GELU_C * (x + 0.044715 * (x * x * x))))


def _gelu_grad(x):
    t = jnp.tanh(_GELU_C * (x + 0.044715 * (x * x * x)))
    return 0.5 * (1.0 + t) + 0.5 * x * (1.0 - t * t) * (_GELU_C * (1.0 + 3.0 * 0.044715 * x * x))


def _sigmoid(x):
    return 1.0 / (1.0 + jnp.exp(-x))


def _colsum(x):
    return jnp.sum(x, axis=0, keepdims=True)


def _rowmean(x):
    return jnp.mean(x, axis=-1, keepdims=True)


def _zero_first(first, *refs):
    @pl.when(first)
    def _():
        for ref in refs:
            ref[...] = jnp.zeros_like(ref)


def _acc(ref, first, val):
    del first
    ref[...] += val


def _place():
    mx, my, mc = lax.axis_index("x"), lax.axis_index("y"), lax.axis_index("c")
    chips = [(1 - mx, my), (mx, 1 - my), (1 - mx, 1 - my)]
    return mx, my, mc, chips


def _all_gather8(buf, name, after=None):
    extra = [] if after is None else (list(after) if isinstance(after, (list, tuple)) else [after])

    def body(in_ref, *rest):
        out_ref, send_sems, recv_sems = rest[len(extra):]
        mx, my, mc, chips = _place()
        me, sibling = (mx, my, mc), (mx, my, 1 - mc)

        def slot(ref, px, py, pc):
            return ref.at[4 * px + 2 * py + pc]

        def copy(k, block, to, src_ref=out_ref):
            return pltpu.make_async_remote_copy(
                src_ref=slot(src_ref, *block), dst_ref=slot(out_ref, *block),
                send_sem=send_sems.at[k], recv_sem=recv_sems.at[k], device_id=to, device_id_type=_MESH)

        first = [copy(0, me, sibling, in_ref)]
        first += [copy(1 + j, me, (*chip, mc), in_ref) for j, chip in enumerate(chips)]
        for cp in first:
            cp.start()
        passed = [copy(4 + j, (*chip, mc), sibling) for j, chip in enumerate(chips)]
        for j, chip in enumerate(chips):
            copy(1 + j, (*chip, mc), me).wait_recv()
            passed[j].start()
        copy(0, sibling, me).wait_recv()
        for j, chip in enumerate(chips):
            copy(4 + j, (*chip, 1 - mc), me).wait_recv()
        for cp in first + passed:
            cp.wait_send()

    return pl.pallas_call(
        body, name=name, out_shape=jax.ShapeDtypeStruct(buf.shape, buf.dtype),
        in_specs=[_ANY] * (1 + len(extra)), out_specs=_ANY, input_output_aliases={0: 0},
        scratch_shapes=[pltpu.SemaphoreType.DMA((7,)), pltpu.SemaphoreType.DMA((7,))],
    )(buf, *extra)


def _own_slot(x, dev):
    return lax.dynamic_update_slice(jnp.zeros((8,) + x.shape, x.dtype), x[None], (dev, 0, 0))


def _cast_into_slot(w, sel, after, name):
    r, c = w.shape
    hr = r // 2
    tr = _tile(hr, 256, 16)
    nr = hr // tr

    def body(sel_ref, w_ref, after_ref, o_ref):
        o_ref[...] = w_ref[...].astype(o_ref.dtype)

    return pl.pallas_call(
        body, name=name, out_shape=jax.ShapeDtypeStruct((4, 2, hr, c), _WIRE),
        grid_spec=pltpu.PrefetchScalarGridSpec(
            num_scalar_prefetch=1, grid=(2, nr),
            in_specs=[pl.BlockSpec((tr, c), lambda h, i, s: (h * nr + i, 0)), _ANY],
            out_specs=pl.BlockSpec((None, None, tr, c), lambda h, i, s: (s[0], h, i, 0))),
        compiler_params=_cp("parallel", "parallel"),
    )(sel, w, after)


def _hbm(a):
    return pltpu.with_memory_space_constraint(a, pltpu.HBM)


def _after(vec, token):
    return vec + token[0:1, 0:1]


def _gather_start(bufs, after, name):
    n = len(bufs)
    nc = 3 * n

    def body(*refs):
        ins, send, recv, token = refs[:n], refs[n + 1:n + 1 + nc], refs[n + 1 + nc:n + 1 + 2 * nc], refs[-1]
        mx, my, mc, chips = _place()
        j_me = 2 * mx + my
        for i in range(n):
            for k, chip in enumerate(chips):
                half = ins[i].at[j_me, mc]
                pltpu.make_async_remote_copy(
                    src_ref=half, dst_ref=half, send_sem=send[3 * i + k], recv_sem=recv[3 * i + k],
                    device_id=(*chip, mc), device_id_type=_MESH).start()
        token[...] = jnp.zeros_like(token)

    outs = pl.pallas_call(
        body, name=name,
        out_shape=tuple([pltpu.SemaphoreType.DMA(())] * (2 * nc) + [pltpu.HBM(b.shape, b.dtype) for b in bufs]
                        + [jax.ShapeDtypeStruct((SUBLANES, LANES), _F32)]),
        in_specs=tuple([_HBM] * n + [_ANY]), out_specs=tuple([_SEM] * (2 * nc) + [_HBM] * n + [_VMEM_WHOLE]),
        input_output_aliases={i: 2 * nc + i for i in range(n)},
        compiler_params=pltpu.CompilerParams(has_side_effects=_EFFECT),
    )(*[_hbm(b) for b in bufs], after)
    sems = [(outs[3 * i:3 * i + 3], outs[nc + 3 * i:nc + 3 * i + 3]) for i in range(n)]
    return sems, list(outs[2 * nc:2 * nc + n]), outs[-1]


def _gather_wait(sems, buf, after, name):
    send, recv = sems

    after = list(after) if isinstance(after, (list, tuple)) else [after]

    def body(buf_ref, s0, s1, s2, r0, r1, r2, *rest):
        mx, my, mc, chips = _place()
        j_me = 2 * mx + my
        for k, (chip, s_k, r_k) in enumerate(zip(chips, (s0, s1, s2), (r0, r1, r2))):
            cp = pltpu.make_async_remote_copy(
                src_ref=buf_ref.at[j_me, mc], dst_ref=buf_ref.at[2 * chip[0] + chip[1], mc], send_sem=s_k, recv_sem=r_k,
                device_id=(*chip, mc), device_id_type=_MESH)
            cp.wait_send()
            cp.wait_recv()

    return pl.pallas_call(
        body, name=name, out_shape=pltpu.HBM(buf.shape, buf.dtype),
        in_specs=(_HBM,) + (_SEM,) * 6 + (_ANY,) * len(after), out_specs=_HBM, input_output_aliases={0: 0},
        compiler_params=pltpu.CompilerParams(has_side_effects=_EFFECT),
    )(buf, *send, *recv, *after)


def _pair_forward(bufs, name):
    n = len(bufs)

    def body(*refs):
        ins, outs = refs[:n], refs[n:2 * n]
        send_sems, recv_sems = refs[2 * n:]
        mx, my, mc, chips = _place()
        sibling = (mx, my, 1 - mc)
        cps = []
        for i in range(n):
            for k, chip in enumerate(chips):
                j_k = 2 * chip[0] + chip[1]
                cp = pltpu.make_async_remote_copy(
                    src_ref=ins[i].at[j_k, mc], dst_ref=outs[i].at[j_k, mc], send_sem=send_sems.at[3 * i + k],
                    recv_sem=recv_sems.at[3 * i + k], device_id=sibling, device_id_type=_MESH)
                cp.start()
                cps.append(cp)
        for i in range(n):
            for k, chip in enumerate(chips):
                other = outs[i].at[2 * chip[0] + chip[1], 1 - mc]
                pltpu.make_async_remote_copy(
                    src_ref=other, dst_ref=other, send_sem=send_sems.at[3 * i + k], recv_sem=recv_sems.at[3 * i + k],
                    device_id=sibling, device_id_type=_MESH).wait_recv()
        for cp in cps:
            cp.wait_send()

    return pl.pallas_call(
        body, name=name, out_shape=[jax.ShapeDtypeStruct(b.shape, b.dtype) for b in bufs],
        in_specs=[_ANY] * n, out_specs=[_ANY] * n, input_output_aliases={i: i for i in range(n)},
        scratch_shapes=[pltpu.SemaphoreType.DMA((3 * n,)), pltpu.SemaphoreType.DMA((3 * n,))],
    )(*bufs)


def _gather8_peers(buf_ref, mx, my, mc, chips):
    mine = buf_ref.at[4 * mx + 2 * my + mc]
    peers = [((mx, my, 1 - mc), mine, buf_ref.at[4 * mx + 2 * my + 1 - mc])]
    peers += [((*chip, mc), mine, buf_ref.at[4 * chip[0] + 2 * chip[1] + mc]) for chip in chips]
    return peers


def _gather8_start(buf, name):
    def body(buf_ref, *rest):
        send, recv, token = rest[0:4], rest[4:8], rest[-1]
        mx, my, mc, chips = _place()
        for k, (peer, src, _) in enumerate(_gather8_peers(buf_ref, mx, my, mc, chips)):
            pltpu.make_async_remote_copy(src_ref=src, dst_ref=src, send_sem=send[k], recv_sem=recv[k],
                                         device_id=peer, device_id_type=_MESH).start()
        token[...] = jnp.zeros_like(token)

    outs = pl.pallas_call(
        body, name=name,
        out_shape=tuple([pltpu.SemaphoreType.DMA(())] * 8 + [pltpu.HBM(buf.shape, buf.dtype),
                                                             jax.ShapeDtypeStruct((SUBLANES, LANES), _F32)]),
        in_specs=(_HBM,), out_specs=tuple([_SEM] * 8 + [_HBM, _VMEM_WHOLE]), input_output_aliases={0: 8},
        compiler_params=pltpu.CompilerParams(has_side_effects=_EFFECT),
    )(_hbm(buf))
    return (outs[0:4], outs[4:8]), outs[8], outs[9]


def _gather8_wait(sems, buf, after, name):
    send, recv = sems

    def body(buf_ref, s0, s1, s2, s3, r0, r1, r2, r3, after_ref, out_ref):
        mx, my, mc, chips = _place()
        for (peer, src, dst), s_k, r_k in zip(_gather8_peers(buf_ref, mx, my, mc, chips), (s0, s1, s2, s3), (r0, r1, r2, r3)):
            cp = pltpu.make_async_remote_copy(src_ref=src, dst_ref=dst, send_sem=s_k, recv_sem=r_k,
                                              device_id=peer, device_id_type=_MESH)
            cp.wait_send()
            cp.wait_recv()

    return pl.pallas_call(
        body, name=name, out_shape=pltpu.HBM(buf.shape, buf.dtype),
        in_specs=(_HBM,) + (_SEM,) * 8 + (_ANY,), out_specs=_HBM, input_output_aliases={0: 0},
        compiler_params=pltpu.CompilerParams(has_side_effects=_EFFECT),
    )(buf, *send, *recv, after)


def _gather8_forward(buf, name):
    def body(in_ref, out_ref, send_sems, recv_sems):
        mx, my, mc, chips = _place()
        sibling = (mx, my, 1 - mc)
        cps = []
        for k, chip in enumerate(chips):
            idx = 4 * chip[0] + 2 * chip[1] + mc
            cp = pltpu.make_async_remote_copy(src_ref=in_ref.at[idx], dst_ref=out_ref.at[idx], send_sem=send_sems.at[k],
                                              recv_sem=recv_sems.at[k], device_id=sibling, device_id_type=_MESH)
            cp.start()
            cps.append(cp)
        for k, chip in enumerate(chips):
            other = out_ref.at[4 * chip[0] + 2 * chip[1] + 1 - mc]
            pltpu.make_async_remote_copy(src_ref=other, dst_ref=other, send_sem=send_sems.at[k], recv_sem=recv_sems.at[k],
                                         device_id=sibling, device_id_type=_MESH).wait_recv()
        for cp in cps:
            cp.wait_send()

    return pl.pallas_call(
        body, name=name, out_shape=jax.ShapeDtypeStruct(buf.shape, buf.dtype),
        in_specs=[_ANY], out_specs=_ANY, input_output_aliases={0: 0},
        scratch_shapes=[pltpu.SemaphoreType.DMA((3,)), pltpu.SemaphoreType.DMA((3,))],
    )(buf)


def _scatter_start(pair, name):
    land = lax.empty((3,) + pair.shape[1:], pair.dtype)

    def body(pair_ref, land_ref, s0, s1, s2, r0, r1, r2, pair_thru, land_thru, token):
        mx, my, mc, chips = _place()
        for k, (chip, s_k, r_k) in enumerate(zip(chips, (s0, s1, s2), (r0, r1, r2))):
            pltpu.make_async_remote_copy(
                src_ref=pair_ref.at[2 * chip[0] + chip[1]], dst_ref=land_ref.at[k], send_sem=s_k, recv_sem=r_k,
                device_id=(*chip, mc), device_id_type=_MESH).start()
        token[...] = jnp.zeros_like(token)

    outs = pl.pallas_call(
        body, name=name,
        out_shape=tuple([pltpu.SemaphoreType.DMA(())] * 6 + [pltpu.HBM(pair.shape, pair.dtype), pltpu.HBM(land.shape, land.dtype),
                                                             jax.ShapeDtypeStruct((SUBLANES, LANES), _F32)]),
        in_specs=(_HBM, _HBM), out_specs=tuple([_SEM] * 6 + [_HBM, _HBM, _VMEM_WHOLE]),
        input_output_aliases={0: 6, 1: 7}, compiler_params=pltpu.CompilerParams(has_side_effects=_EFFECT),
    )(_hbm(pair), _hbm(land))
    return (outs[0:3], outs[3:6]), outs[6], outs[7], outs[8]


def _scatter_wait(sems, pair, land, after, name):
    send, recv = sems

    def body(pair_ref, land_ref, s0, s1, s2, r0, r1, r2, after_ref, pair_out, land_out):
        mx, my, mc, chips = _place()
        for k, (chip, s_k, r_k) in enumerate(zip(chips, (s0, s1, s2), (r0, r1, r2))):
            cp = pltpu.make_async_remote_copy(
                src_ref=pair_ref.at[2 * chip[0] + chip[1]], dst_ref=land_ref.at[k], send_sem=s_k, recv_sem=r_k,
                device_id=(*chip, mc), device_id_type=_MESH)
            cp.wait_send()
            cp.wait_recv()

    return pl.pallas_call(
        body, name=name, out_shape=(pltpu.HBM(pair.shape, pair.dtype), pltpu.HBM(land.shape, land.dtype)),
        in_specs=(_HBM, _HBM) + (_SEM,) * 6 + (_ANY,), out_specs=(_HBM, _HBM), input_output_aliases={0: 0, 1: 1},
        compiler_params=pltpu.CompilerParams(has_side_effects=_EFFECT),
    )(pair, land, *send, *recv, after)


def _sibling_copy(src_ref, dst_ref, send_sem, recv_sem):
    mx, my, mc, _ = _place()
    return pltpu.make_async_remote_copy(src_ref=src_ref, dst_ref=dst_ref, send_sem=send_sem, recv_sem=recv_sem,
                                        device_id=(mx, my, 1 - mc), device_id_type=_MESH)


def _swap_start(g, name):
    land = lax.empty(g.shape[1:], g.dtype)

    def body(g_ref, land_ref, send_sem, recv_sem, g_thru, land_thru, token):
        _sibling_copy(g_ref.at[1 - lax.axis_index("c")], land_ref, send_sem, recv_sem).start()
        token[...] = jnp.zeros_like(token)

    outs = pl.pallas_call(
        body, name=name,
        out_shape=(pltpu.SemaphoreType.DMA(()), pltpu.SemaphoreType.DMA(()), pltpu.HBM(g.shape, g.dtype),
                   pltpu.HBM(land.shape, land.dtype), jax.ShapeDtypeStruct((SUBLANES, LANES), _F32)),
        in_specs=(_HBM, _HBM), out_specs=(_SEM, _SEM, _HBM, _HBM, _VMEM_WHOLE), input_output_aliases={0: 2, 1: 3},
        compiler_params=pltpu.CompilerParams(has_side_effects=_EFFECT),
    )(_hbm(g), _hbm(land))
    return (outs[0], outs[1]), outs[2], outs[3], outs[4]


def _swap_wait(sems, g, land, after, name):
    def body(g_ref, land_ref, send_sem, recv_sem, after_ref, g_out, land_out):
        cp = _sibling_copy(g_ref.at[1 - lax.axis_index("c")], land_ref, send_sem, recv_sem)
        cp.wait_send()
        cp.wait_recv()

    return pl.pallas_call(
        body, name=name, out_shape=(pltpu.HBM(g.shape, g.dtype), pltpu.HBM(land.shape, land.dtype)),
        in_specs=(_HBM, _HBM, _SEM, _SEM, _ANY), out_specs=(_HBM, _HBM), input_output_aliases={0: 0, 1: 1},
        compiler_params=pltpu.CompilerParams(has_side_effects=_EFFECT),
    )(g, land, *sems, after)


def _join_start(buf, name):
    def body(buf_ref, send_sem, recv_sem, buf_thru, token):
        mine = buf_ref.at[lax.axis_index("c")]
        _sibling_copy(mine, mine, send_sem, recv_sem).start()
        token[...] = jnp.zeros_like(token)

    outs = pl.pallas_call(
        body, name=name,
        out_shape=(pltpu.SemaphoreType.DMA(()), pltpu.SemaphoreType.DMA(()), pltpu.HBM(buf.shape, buf.dtype),
                   jax.ShapeDtypeStruct((SUBLANES, LANES), _F32)),
        in_specs=(_HBM,), out_specs=(_SEM, _SEM, _HBM, _VMEM_WHOLE), input_output_aliases={0: 2},
        compiler_params=pltpu.CompilerParams(has_side_effects=_EFFECT),
    )(_hbm(buf))
    return (outs[0], outs[1]), outs[2], outs[3]


def _join_wait(sems, buf, after, name):
    def body(buf_ref, send_sem, recv_sem, after_ref, buf_out):
        mc = lax.axis_index("c")
        cp = _sibling_copy(buf_ref.at[mc], buf_ref.at[1 - mc], send_sem, recv_sem)
        cp.wait_send()
        cp.wait_recv()

    return pl.pallas_call(
        body, name=name, out_shape=pltpu.HBM(buf.shape, buf.dtype),
        in_specs=(_HBM, _SEM, _SEM, _ANY), out_specs=_HBM, input_output_aliases={0: 0},
        compiler_params=pltpu.CompilerParams(has_side_effects=_EFFECT),
    )(buf, *sems, after)


def _pair_sum(g, got, sel, name):
    _, four, hr, c = g.shape
    tr = _tile(hr, 512, 16)

    def body(sel_ref, g_ref, p_ref, o_ref):
        o_ref[...] = (g_ref[...].astype(_F32) + p_ref[...].astype(_F32)).astype(o_ref.dtype)

    return pl.pallas_call(
        body, name=name, out_shape=jax.ShapeDtypeStruct((four, hr, c), g.dtype),
        grid_spec=pltpu.PrefetchScalarGridSpec(
            num_scalar_prefetch=1, grid=(four, hr // tr),
            in_specs=[pl.BlockSpec((None, None, tr, c), lambda j, i, s: (s[1], j, i, 0)),
                      pl.BlockSpec((None, tr, c), lambda j, i, s: (j, i, 0))],
            out_specs=pl.BlockSpec((None, tr, c), lambda j, i, s: (j, i, 0))),
        compiler_params=_cp("parallel", "parallel"),
    )(sel, g, got)


def _chip_sum(pair, got, sel, name):
    _, hr, c = pair.shape
    tr = _tile(hr, 512, 16)

    def body(sel_ref, p_ref, q_ref, o_ref):
        o_ref[...] = ((p_ref[...].astype(_F32) + q_ref[0].astype(_F32)) + q_ref[1].astype(_F32)) + q_ref[2].astype(_F32)

    return pl.pallas_call(
        body, name=name, out_shape=jax.ShapeDtypeStruct((2, hr, c), _F32),
        grid_spec=pltpu.PrefetchScalarGridSpec(
            num_scalar_prefetch=1, grid=(hr // tr,),
            in_specs=[pl.BlockSpec((None, tr, c), lambda i, s: (s[0], i, 0)),
                      pl.BlockSpec((3, tr, c), lambda i, s: (0, i, 0))],
            out_specs=pl.BlockSpec((None, tr, c), lambda i, s: (s[1], i, 0))),
        compiler_params=_cp("parallel"),
    )(sel, pair, got)


def _matmul(a, b, dims, out_struct, grid, a_spec, b_spec, o_spec, acc_shape, k_axis, name, after=None):
    nk = grid[k_axis]
    extra = [] if after is None else [after]

    def body(a_ref, b_ref, *rest):
        o_ref, acc = rest[len(extra)], rest[len(extra) + 1:]
        prod = lax.dot_general(a_ref[...].astype(_MXU), b_ref[...].astype(_MXU), dims, preferred_element_type=_F32)
        if nk == 1:
            o_ref[...] = prod.astype(o_ref.dtype)
        else:
            acc_ref, = acc
            k = pl.program_id(k_axis)
            _zero_first(k == 0, acc_ref)
            acc_ref[...] += prod

            @pl.when(k == nk - 1)
            def _():
                o_ref[...] = acc_ref[...].astype(o_ref.dtype)

    sem = ["parallel"] * len(grid)
    sem[k_axis] = "arbitrary"
    return pl.pallas_call(
        body, name=name, out_shape=out_struct, grid=grid, in_specs=[a_spec, b_spec] + [_ANY] * len(extra), out_specs=o_spec,
        scratch_shapes=[pltpu.VMEM(acc_shape, _F32)] if nk > 1 else [], compiler_params=_cp(*sem),
    )(a, b, *extra)


def _mm_nn(a, w4, out_dtype, name, tm=512, tn=1536, tk=2048, after=None):
    m, k = a.shape
    j, _, ns = w4.shape
    tm, tn, tk = _tile(m, tm, 16), _tile(ns, tn), _tile(k, tk)
    nps = ns // tn
    return _matmul(
        a, w4, (((1,), (0,)), ((), ())), jax.ShapeDtypeStruct((m, j * ns), out_dtype),
        (j * nps, m // tm, k // tk),
        pl.BlockSpec((tm, tk), lambda ni, mi, ki: (mi, ki)),
        pl.BlockSpec((None, tk, tn), lambda ni, mi, ki: (ni // nps, ki, ni % nps)),
        pl.BlockSpec((tm, tn), lambda ni, mi, ki: (mi, ni)), (tm, tn), 2, name, after)


def _mm_nt(a, w4, out_dtype, name, tm=512, tn=2048, tk=1536, after=None):
    m = a.shape[-2]
    j, kw, ns = w4.shape
    tm, tn, tk = _tile(m, tm, 16), _tile(kw, tn), _tile(ns, tk)
    kps = ns // tk
    if a.ndim == 3:
        kph = a.shape[2] // tk
        a_spec = pl.BlockSpec((None, tm, tk), lambda ni, mi, ki: (ki // kph, mi, ki % kph))
    else:
        a_spec = pl.BlockSpec((tm, tk), lambda ni, mi, ki: (mi, ki))
    return _matmul(
        a, w4, (((1,), (1,)), ((), ())), jax.ShapeDtypeStruct((m, kw), out_dtype),
        (kw // tn, m // tm, j * kps),
        a_spec,
        pl.BlockSpec((None, tn, tk), lambda ni, mi, ki: (ki // kps, ni, ki % kps)),
        pl.BlockSpec((tm, tn), lambda ni, mi, ki: (mi, ni)), (tm, tn), 2, name, after)


def _mm_tn_cols(a, b, name, tm=1024, tn=1536, tk=2048):
    m, ka = a.shape
    ns = (b.shape[-1] * (2 if b.ndim == 3 else 1)) // 4
    hr = ka // 2
    tm, tn, tk = _tile(hr, tm), _tile(ns, tn), _tile(m, tk, 16)
    mph, nps = hr // tm, ns // tn
    if b.ndim == 3:
        b_spec = pl.BlockSpec((None, tk, tn), lambda ni, mi, ki: (ni // (2 * nps), ki, ni % (2 * nps)))
    else:
        b_spec = pl.BlockSpec((tk, tn), lambda ni, mi, ki: (ki, ni))
    return _matmul(
        a, b, (((0,), (0,)), ((), ())), jax.ShapeDtypeStruct((2, 4, hr, ns), _WIRE),
        (4 * nps, 2 * mph, m // tk),
        pl.BlockSpec((tk, tm), lambda ni, mi, ki: (ki, mi)),
        b_spec,
        pl.BlockSpec((None, None, tm, tn), lambda ni, mi, ki: (mi // mph, ni // nps, mi % mph, ni % nps)),
        (tm, tn), 2, name)


def _mm_tn_rows(a, b, name, tm=1536, tn=1024, tk=2048):
    m, ka = a.shape
    r = ka // 4
    hc = b.shape[1] // 2
    tm, tn, tk = _tile(r, tm), _tile(hc, tn), _tile(m, tk, 16)
    mpr, nph = r // tm, hc // tn
    return _matmul(
        a, b, (((0,), (0,)), ((), ())), jax.ShapeDtypeStruct((2, 4, r, hc), _WIRE),
        (2 * nph, 4 * mpr, m // tk),
        pl.BlockSpec((tk, tm), lambda ni, mi, ki: (ki, mi)),
        pl.BlockSpec((tk, tn), lambda ni, mi, ki: (ki, ni)),
        pl.BlockSpec((None, None, tm, tn), lambda ni, mi, ki: (ni // nph, mi // mpr, mi % mpr, ni % nph)),
        (tm, tn), 2, name)


def _row_call(body, name, rows, ins, outs, tm=256):
    tm = _tile(rows, tm, 16)

    def spec(shape, kind):
        if kind == "rows":
            return pl.BlockSpec((tm, shape[1]), lambda i: (i, 0))
        return pl.BlockSpec(shape, lambda i: (0,) * len(shape))

    return pl.pallas_call(
        body, name=name, grid=(rows // tm,),
        in_specs=[spec(a.shape, kind) for a, kind in ins],
        out_specs=[spec(o.shape, kind) for o, kind in outs],
        out_shape=[o for o, _ in outs],
        compiler_params=_cp("arbitrary"),
    )(*[a for a, _ in ins])


def _rms(x):
    r = lax.rsqrt(_rowmean(x * x) + EPS)
    return x * r, r


def _rms_bwd(dxh, xh, r):
    return r * (dxh - xh * _rowmean(dxh * xh))


def _fwd_pre_mix(x, g, sc, sh):
    s, d = x.shape

    def body(x_ref, g_ref, sc_ref, sh_ref, h_ref):
        xh, _ = _rms(x_ref[...])
        h_ref[...] = (xh * g_ref[...] * (1.0 + sc_ref[...]) + sh_ref[...]).astype(h_ref.dtype)

    return _row_call(body, "fwd_pre_mix", s, [(x, "rows"), (g, "vec"), (sc, "vec"), (sh, "vec")],
                     [(jax.ShapeDtypeStruct((s, d), _MXU), "rows")])[0]


def _fwd_mid(o, x, gt1, g_post, g_pre2, sc2, sh2):
    s, d = x.shape

    def body(o_ref, x_ref, gt_ref, gp_ref, g2_ref, sc_ref, sh_ref, x1_ref, h2_ref):
        oh, _ = _rms(o_ref[...])
        x1 = x_ref[...] + gt_ref[...] * (oh * gp_ref[...])
        x1_ref[...] = x1
        xh, _ = _rms(x1)
        h2_ref[...] = (xh * g2_ref[...] * (1.0 + sc_ref[...]) + sh_ref[...]).astype(h2_ref.dtype)

    return _row_call(body, "fwd_mid", s,
                     [(o, "rows"), (x, "rows"), (gt1, "vec"), (g_post, "vec"), (g_pre2, "vec"), (sc2, "vec"),
                      (sh2, "vec")],
                     [(jax.ShapeDtypeStruct((s, d), _F32), "rows"), (jax.ShapeDtypeStruct((s, d), _MXU), "rows")])


def _loss_and_post_ffn_bwd(f, x1, tgt, gt2, g_post):
    s, d = x1.shape

    def body(f_ref, x1_ref, t_ref, gt_ref, g_ref, dx2_ref, df_ref, dgt_ref, dg_ref, loss_ref):
        first = pl.program_id(0) == 0
        _zero_first(first, dgt_ref, dg_ref, loss_ref)
        fh, r = _rms(f_ref[...])
        n = fh * g_ref[...]
        e = x1_ref[...] + gt_ref[...] * n - t_ref[...]
        _acc(loss_ref, first, jnp.sum(_colsum(e * e), axis=1, keepdims=True) * (0.5 / d))
        dx2 = e * (1.0 / d)
        dx2_ref[...] = dx2
        _acc(dgt_ref, first, _colsum(dx2 * n))
        dn = dx2 * gt_ref[...]
        _acc(dg_ref, first, _colsum(dn * fh))
        df_ref[...] = _rms_bwd(dn * g_ref[...], fh, r).astype(df_ref.dtype)

    vec = jax.ShapeDtypeStruct((1, d), _F32)
    return _row_call(body, "loss_post_ffn_bwd", s,
                     [(f, "rows"), (x1, "rows"), (tgt, "rows"), (gt2, "vec"), (g_post, "vec")],
                     [(jax.ShapeDtypeStruct((s, d), _F32), "rows"), (jax.ShapeDtypeStruct((s, d), _MXU), "rows"),
                      (vec, "vec"), (vec, "vec"), (jax.ShapeDtypeStruct((1, 1), _F32), "vec")])


def _bwd_mid(dh2, x1, dx2, o, g_pre2, sc2, gt1, g_post):
    s, d = x1.shape

    def body(dh_ref, x1_ref, dx2_ref, o_ref, g2_ref, sc_ref, gt_ref, gp_ref,
             dx1_ref, do_ref, dsc_ref, dsh_ref, dg2_ref, dgt_ref, dgp_ref):
        first = pl.program_id(0) == 0
        _zero_first(first, dsc_ref, dsh_ref, dg2_ref, dgt_ref, dgp_ref)
        dh = dh_ref[...]
        xh, r = _rms(x1_ref[...])
        _acc(dsh_ref, first, _colsum(dh))
        _acc(dsc_ref, first, _colsum(dh * (xh * g2_ref[...])))
        dn = dh * (1.0 + sc_ref[...])
        _acc(dg2_ref, first, _colsum(dn * xh))
        dx1 = dx2_ref[...] + _rms_bwd(dn * g2_ref[...], xh, r)
        dx1_ref[...] = dx1
        oh, ro = _rms(o_ref[...])
        _acc(dgt_ref, first, _colsum(dx1 * (oh * gp_ref[...])))
        dno = dx1 * gt_ref[...]
        _acc(dgp_ref, first, _colsum(dno * oh))
        do_ref[...] = _rms_bwd(dno * gp_ref[...], oh, ro).astype(do_ref.dtype)

    vec = jax.ShapeDtypeStruct((1, d), _F32)
    return _row_call(body, "bwd_mid", s,
                     [(dh2, "rows"), (x1, "rows"), (dx2, "rows"), (o, "rows"), (g_pre2, "vec"), (sc2, "vec"),
                      (gt1, "vec"), (g_post, "vec")],
                     [(jax.ShapeDtypeStruct((s, d), _F32), "rows"), (jax.ShapeDtypeStruct((s, d), _MXU), "rows"),
                      (vec, "vec"), (vec, "vec"), (vec, "vec"), (vec, "vec"), (vec, "vec")])


def _bwd_pre_mix(dh1, x, dx1, g, sc1):
    s, d = x.shape

    def body(dh_ref, x_ref, dx1_ref, g_ref, sc_ref, dx_ref, dsc_ref, dsh_ref, dg_ref):
        first = pl.program_id(0) == 0
        _zero_first(first, dsc_ref, dsh_ref, dg_ref)
        dh = dh_ref[...]
        xh, r = _rms(x_ref[...])
        _acc(dsh_ref, first, _colsum(dh))
        _acc(dsc_ref, first, _colsum(dh * (xh * g_ref[...])))
        dn = dh * (1.0 + sc_ref[...])
        _acc(dg_ref, first, _colsum(dn * xh))
        dx_ref[...] = dx1_ref[...] + _rms_bwd(dn * g_ref[...], xh, r)

    vec = jax.ShapeDtypeStruct((1, d), _F32)
    return _row_call(body, "bwd_pre_mix", s,
                     [(dh1, "rows"), (x, "rows"), (dx1, "rows"), (g, "vec"), (sc1, "vec")],
                     [(jax.ShapeDtypeStruct((s, d), _F32), "rows"), (vec, "vec"), (vec, "vec"), (vec, "vec")])


def _mix_norm_fwd(y_ssm, y_sgu, g_ssm, g_sgu):
    s, h = y_ssm.shape

    def body(a_ref, b_ref, ga_ref, gb_ref, o_ref):
        ah, _ = _rms(a_ref[...])
        bh, _ = _rms(b_ref[...])
        o_ref[:, 0:h] = (ah * ga_ref[...]).astype(o_ref.dtype)
        o_ref[:, h:2 * h] = (bh * gb_ref[...]).astype(o_ref.dtype)

    return _row_call(body, "mix_norm_fwd", s, [(y_ssm, "rows"), (y_sgu, "rows"), (g_ssm, "vec"), (g_sgu, "vec")],
                     [(jax.ShapeDtypeStruct((s, 2 * h), _MXU), "rows")])[0]


def _mix_norm_bwd(dyc, y_ssm, y_sgu, g_ssm, g_sgu):
    s, h = y_ssm.shape

    def body(d_ref, a_ref, b_ref, ga_ref, gb_ref, da_ref, db_ref, dga_ref, dgb_ref):
        first = pl.program_id(0) == 0
        _zero_first(first, dga_ref, dgb_ref)
        for lo, y_ref, g_ref, dy_ref, dg_ref in ((0, a_ref, ga_ref, da_ref, dga_ref), (h, b_ref, gb_ref, db_ref, dgb_ref)):
            d = d_ref[:, lo:lo + h]
            yh, r = _rms(y_ref[...])
            _acc(dg_ref, first, _colsum(d * yh))
            dy_ref[...] = _rms_bwd(d * g_ref[...], yh, r)

    vec = jax.ShapeDtypeStruct((1, h), _F32)
    full = jax.ShapeDtypeStruct((s, h), _F32)
    return _row_call(body, "mix_norm_bwd", s,
                     [(dyc, "rows"), (y_ssm, "rows"), (y_sgu, "rows"), (g_ssm, "vec"), (g_sgu, "vec")],
                     [(full, "rows"), (full, "rows"), (vec, "vec"), (vec, "vec")])


CONV_ROWS = 64


def _conv_rows(ext, w_ref, b_ref):
    x = ext[SUBLANES:]
    s1 = pltpu.roll(ext, 1, 0)[SUBLANES:]
    s2 = pltpu.roll(ext, 2, 0)[SUBLANES:]
    return b_ref[...] + w_ref[0:1, :] * s2 + w_ref[1:2, :] * s1 + w_ref[2:3, :] * x, x, s1, s2


def _conv_window(x_ref, r0):
    if isinstance(r0, int):
        assert r0 == 0
        return jnp.concatenate([jnp.zeros((SUBLANES, x_ref.shape[1]), _F32), x_ref[0:CONV_ROWS, :]], axis=0)
    return x_ref[pl.ds(pl.multiple_of(r0 - SUBLANES, SUBLANES), CONV_ROWS + SUBLANES), :]


def _conv_act_fwd(up_pre, conv_w, conv_b):
    s, f2 = up_pre.shape
    f = f2 // 2
    tc = _tile(f, 256)
    nf = f // tc

    def shift_down(x, k):
        row = lax.broadcasted_iota(jnp.int32, x.shape, 0)
        return jnp.where(row >= k, pltpu.roll(x, k, 0), 0.0)

    def conv(x, w_ref, b_ref):
        return b_ref[...] + w_ref[0:1, :] * shift_down(x, 2) + w_ref[1:2, :] * shift_down(x, 1) + w_ref[2:3, :] * x

    def body(a_ref, b_ref, wa_ref, wb_ref, ba_ref, bb_ref, o_ref):
        a = conv(a_ref[...], wa_ref, ba_ref)
        b = conv(b_ref[...], wb_ref, bb_ref)
        o_ref[...] = (a * _sigmoid(a) * b).astype(o_ref.dtype)

    return pl.pallas_call(
        body, name="conv_act_fwd", grid=(nf,), out_shape=jax.ShapeDtypeStruct((s, f), _MXU),
        in_specs=[pl.BlockSpec((s, tc), lambda n: (0, n)), pl.BlockSpec((s, tc), lambda n: (0, n + nf)),
                  pl.BlockSpec((3, tc), lambda n: (0, n)), pl.BlockSpec((3, tc), lambda n: (0, n + nf)),
                  pl.BlockSpec((1, tc), lambda n: (0, n)), pl.BlockSpec((1, tc), lambda n: (0, n + nf))],
        out_specs=pl.BlockSpec((s, tc), lambda n: (0, n)), compiler_params=_cp("parallel"),
    )(up_pre, up_pre, conv_w, conv_w, conv_b, conv_b)


def _conv_act_bwd(up_pre, d_act, conv_w, conv_b):
    s, f2 = up_pre.shape
    f = f2 // 2
    tc = _tile(f, 256)
    nf = f // tc

    def body(a_ref, b_ref, d_ref, wa_ref, wb_ref, ba_ref, bb_ref,
             du_ref, w0a, w0b, w1a, w1b, w2a, w2b, dba, dbb):
        n = s // CONV_ROWS
        zero8 = jnp.zeros((SUBLANES, tc), _F32)
        ext_rows = CONV_ROWS + SUBLANES

        def fold(x):
            out = x[0:SUBLANES]
            for k in range(1, CONV_ROWS // SUBLANES):
                out = out + x[k * SUBLANES:(k + 1) * SUBLANES]
            return out

        def chunk(r0, carry):
            nxt, acc = carry
            a, xa, xa1, xa2 = _conv_rows(_conv_window(a_ref, r0), wa_ref, ba_ref)
            b, xb, xb1, xb2 = _conv_rows(_conv_window(b_ref, r0), wb_ref, bb_ref)
            sg = _sigmoid(a)
            d = d_ref[pl.ds(r0, CONV_ROWS), :]
            du_a = d * b * (sg * (1.0 + a * (1.0 - sg)))
            du_b = d * (a * sg)
            new_acc = []
            for h, (du, x0, x1, x2, w_ref) in enumerate(((du_a, xa, xa1, xa2, wa_ref), (du_b, xb, xb1, xb2, wb_ref))):
                ext = jnp.concatenate([du, nxt[h]], axis=0)
                u1 = pltpu.roll(ext, ext_rows - 1, 0)[:CONV_ROWS]
                u2 = pltpu.roll(ext, ext_rows - 2, 0)[:CONV_ROWS]
                du_ref[h, pl.ds(r0, CONV_ROWS), :] = (w_ref[2:3, :] * du + w_ref[1:2, :] * u1
                                                      + w_ref[0:1, :] * u2).astype(du_ref.dtype)
                new_acc += [acc[4 * h] + fold(du * x2), acc[4 * h + 1] + fold(du * x1), acc[4 * h + 2] + fold(du * x0),
                            acc[4 * h + 3] + fold(du)]
            return (du_a[:SUBLANES], du_b[:SUBLANES]), tuple(new_acc)

        def step(i, carry):
            return chunk(pl.multiple_of((n - 1 - i) * CONV_ROWS, CONV_ROWS), carry)

        carry = lax.fori_loop(0, n - 1, step, ((zero8, zero8), (zero8,) * 8))
        _, acc = chunk(0, carry)
        for ref, val in zip((w0a, w1a, w2a, dba, w0b, w1b, w2b, dbb), acc):
            ref[...] = _colsum(val)

    col_a = pl.BlockSpec((s, tc), lambda n: (0, n))
    col_b = pl.BlockSpec((s, tc), lambda n: (0, n + nf))
    vec_a = pl.BlockSpec((1, tc), lambda n: (0, n))
    vec_b = pl.BlockSpec((1, tc), lambda n: (0, n + nf))
    vec = jax.ShapeDtypeStruct((1, f), _F32)
    outs = pl.pallas_call(
        body, name="conv_act_bwd", grid=(nf,),
        in_specs=[col_a, col_b, col_a, pl.BlockSpec((3, tc), lambda n: (0, n)),
                  pl.BlockSpec((3, tc), lambda n: (0, n + nf)), vec_a, vec_b],
        out_specs=[pl.BlockSpec((2, s, tc), lambda n: (0, 0, n))] + [vec_a] * 8,
        out_shape=[jax.ShapeDtypeStruct((2, s, f), _MXU)] + [vec] * 8, compiler_params=_cp("parallel"),
    )(up_pre, up_pre, d_act, conv_w, conv_w, conv_b, conv_b)
    du, w0a, w0b, w1a, w1b, w2a, w2b, dba, dbb = outs
    cat = lambda p, q: jnp.concatenate([p, q], axis=1)
    return du, cat(w0a, w0b), cat(w1a, w1b), cat(w2a, w2b), cat(dba, dbb)


def _sgu_recompute(zu_ref, zv_ref, lng_ref, lnb_ref, wm_ref, bs_ref, nh):
    zu, zv = zu_ref[...], zv_ref[...]
    u = _gelu(zu)
    gv = _gelu(zv)
    xc = gv - _rowmean(gv)
    rs = lax.rsqrt(_rowmean(xc * xc) + EPS)
    vh = xc * rs
    v = vh * lng_ref[...] + lnb_ref[...]
    mixed = []
    for h in range(nh):
        vhd = v[:, h * CHUNK:(h + 1) * CHUNK].astype(_MXU)
        mixed.append(jnp.dot(wm_ref[h].astype(_MXU), vhd, preferred_element_type=_F32) + bs_ref[h])
    return zu, zv, u, vh, rs, v, mixed


def _sgu_fwd(z, ln_g, ln_b, wm, bs):
    s = z.shape[0]
    nh = wm.shape[0]
    hd = nh * CHUNK

    def body(zu_ref, zv_ref, lng_ref, lnb_ref, wm_ref, bs_ref, y_ref):
        _, _, u, _, _, _, mixed = _sgu_recompute(zu_ref, zv_ref, lng_ref, lnb_ref, wm_ref, bs_ref, nh)
        for h in range(nh):
            y_ref[:, h * CHUNK:(h + 1) * CHUNK] = u[:, h * CHUNK:(h + 1) * CHUNK] * mixed[h]

    vec = pl.BlockSpec((1, hd), lambda i: (0, 0))
    return pl.pallas_call(
        body, name="sgu_fwd", grid=(s // CHUNK,), out_shape=jax.ShapeDtypeStruct((s, hd), _F32),
        in_specs=[pl.BlockSpec((CHUNK, hd), lambda i: (i, 1)), pl.BlockSpec((CHUNK, hd), lambda i: (i, 2)), vec, vec,
                  pl.BlockSpec((nh, CHUNK, CHUNK), lambda i: (0, 0, 0)), pl.BlockSpec((nh, CHUNK, 1), lambda i: (0, 0, 0))],
        out_specs=pl.BlockSpec((CHUNK, hd), lambda i: (i, 0)), compiler_params=_cp("parallel"),
    )(z, z, ln_g, ln_b, wm, bs)


def _sgu_bwd(z, dy, dz_ssm, ln_g, ln_b, wm, bs):
    s = z.shape[0]
    nh = wm.shape[0]
    hd = nh * CHUNK

    def body(zu_ref, zv_ref, dy_ref, dzs_ref, lng_ref, lnb_ref, wm_ref, bs_ref,
             dz_ref, dlg_ref, dlb_ref, dwm_ref, dbs_ref, dv_scr):
        first = pl.program_id(0) == 0
        _zero_first(first, dlg_ref, dlb_ref, dwm_ref, dbs_ref)
        zu, zv, u, vh, rs, v, mixed = _sgu_recompute(zu_ref, zv_ref, lng_ref, lnb_ref, wm_ref, bs_ref, nh)
        dy = dy_ref[...]
        dz_ref[:, 0:hd] = dzs_ref[...].astype(dz_ref.dtype)
        for h in range(nh):
            cols = slice(h * CHUNK, (h + 1) * CHUNK)
            dyh = dy[:, cols]
            dz_ref[:, hd + h * CHUNK:hd + (h + 1) * CHUNK] = (dyh * mixed[h] * _gelu_grad(zu[:, cols])).astype(dz_ref.dtype)
            dm = dyh * u[:, cols]
            dmx = dm.astype(_MXU)
            _acc(dbs_ref.at[h], first, jnp.sum(dm, axis=1, keepdims=True))
            _acc(dwm_ref.at[h], first,
                 lax.dot_general(dmx, v[:, cols].astype(_MXU), (((1,), (1,)), ((), ())), preferred_element_type=_F32))
            dv_scr[:, cols] = lax.dot_general(wm_ref[h].astype(_MXU), dmx, (((0,), (0,)), ((), ())),
                                              preferred_element_type=_F32)
        dv = dv_scr[...]
        _acc(dlg_ref, first, _colsum(dv * vh))
        _acc(dlb_ref, first, _colsum(dv))
        dvh = dv * lng_ref[...]
        dgv = rs * (dvh - _rowmean(dvh) - vh * _rowmean(dvh * vh))
        dz_ref[:, 2 * hd:3 * hd] = (dgv * _gelu_grad(zv)).astype(dz_ref.dtype)

    vec = pl.BlockSpec((1, hd), lambda i: (0, 0))
    wspec = pl.BlockSpec((nh, CHUNK, CHUNK), lambda i: (0, 0, 0))
    bspec = pl.BlockSpec((nh, CHUNK, 1), lambda i: (0, 0, 0))
    rows = pl.BlockSpec((CHUNK, hd), lambda i: (i, 0))
    return pl.pallas_call(
        body, name="sgu_bwd", grid=(s // CHUNK,),
        out_shape=[jax.ShapeDtypeStruct((s, 3 * hd), _MXU), jax.ShapeDtypeStruct((1, hd), _F32),
                   jax.ShapeDtypeStruct((1, hd), _F32), jax.ShapeDtypeStruct((nh, CHUNK, CHUNK), _F32),
                   jax.ShapeDtypeStruct((nh, CHUNK, 1), _F32)],
        in_specs=[pl.BlockSpec((CHUNK, hd), lambda i: (i, 1)), pl.BlockSpec((CHUNK, hd), lambda i: (i, 2)),
                  rows, rows, vec, vec, wspec, bspec],
        out_specs=[pl.BlockSpec((CHUNK, 3 * hd), lambda i: (i, 0)), vec, vec, wspec, bspec],
        scratch_shapes=[pltpu.VMEM((CHUNK, hd), _F32)], compiler_params=_cp("arbitrary"),
    )(z, z, dy, dz_ssm, ln_g, ln_b, wm, bs)


def _ssm_prep(log_dt, a_re, a_im, b_re_t, b_im_t, kvec):
    gn = a_re.shape[1]

    def body(ldt_ref, are_ref, aim_ref, br_ref, bi_ref, k_ref, pr_ref, pi_ref, bbr_ref, bbi_ref):
        dt = jnp.exp(ldt_ref[...])
        are, aim = are_ref[...], aim_ref[...]
        k = k_ref[...]
        mag = jnp.exp(k * (are * dt))
        ang = k * (aim * dt)
        pr_ref[...] = mag * jnp.cos(ang)
        pi_ref[...] = mag * jnp.sin(ang)
        m1 = jnp.exp(are * dt)
        lr, li = m1 * jnp.cos(aim * dt), m1 * jnp.sin(aim * dt)
        den = are * are + aim * aim
        nr = lr - 1.0
        f_re = (nr * are + li * aim) / den
        f_im = (li * are - nr * aim) / den
        bbr_ref[...] = f_re * br_ref[...] - f_im * bi_ref[...]
        bbi_ref[...] = f_re * bi_ref[...] + f_im * br_ref[...]

    pw = jax.ShapeDtypeStruct((kvec.shape[0], gn), _F32)
    bb = jax.ShapeDtypeStruct(b_re_t.shape, _F32)
    return pl.pallas_call(body, name="ssm_prep", out_shape=[pw, pw, bb, bb])(log_dt, a_re, a_im, b_re_t, b_im_t, kvec)


def _ssm_prep_bwd(log_dt, a_re, a_im, b_re_t, b_im_t, d_bbr, d_bbi, d_lr, d_li):
    def body(ldt_ref, are_ref, aim_ref, br_ref, bi_ref, dbr_ref, dbi_ref, dlr_ref, dli_ref,
             obr_ref, obi_ref, oar_ref, oai_ref, odt_ref):
        dt = jnp.exp(ldt_ref[...])
        are, aim = are_ref[...], aim_ref[...]
        m1 = jnp.exp(are * dt)
        lr, li = m1 * jnp.cos(aim * dt), m1 * jnp.sin(aim * dt)
        den = are * are + aim * aim
        nr = lr - 1.0
        f_re = (nr * are + li * aim) / den
        f_im = (li * are - nr * aim) / den
        br, bi, dbr, dbi = br_ref[...], bi_ref[...], dbr_ref[...], dbi_ref[...]
        obr_ref[...] = f_re * dbr + f_im * dbi
        obi_ref[...] = f_re * dbi - f_im * dbr
        gf_re = _colsum(br * dbr + bi * dbi)
        gf_im = _colsum(br * dbi - bi * dbr)
        il_re, il_im = are / den, -aim / den
        glb_re = dlr_ref[...] + (il_re * gf_re + il_im * gf_im)
        glb_im = dli_ref[...] + (il_re * gf_im - il_im * gf_re)
        q_re = -(f_re * il_re - f_im * il_im)
        q_im = -(f_re * il_im + f_im * il_re)
        gl_re = q_re * gf_re + q_im * gf_im
        gl_im = q_re * gf_im - q_im * gf_re
        gl_re = gl_re + dt * (lr * glb_re + li * glb_im)
        gl_im = gl_im + dt * (lr * glb_im - li * glb_re)
        w_re = are * lr - aim * li
        w_im = are * li + aim * lr
        oar_ref[...] = gl_re
        oai_ref[...] = gl_im
        odt_ref[...] = w_re * glb_re + w_im * glb_im

    bb = jax.ShapeDtypeStruct(b_re_t.shape, _F32)
    v = jax.ShapeDtypeStruct(a_re.shape, _F32)
    return pl.pallas_call(body, name="ssm_prep_bwd", out_shape=[bb, bb, v, v, v])(
        log_dt, a_re, a_im, b_re_t, b_im_t, d_bbr, d_bbi, d_lr, d_li)


def _group_sum(d_dt, log_dt):
    def body(d_ref, l_ref, o_ref):
        o_ref[...] = jnp.sum(d_ref[...], axis=1, keepdims=True) * jnp.exp(l_ref[...])

    return pl.pallas_call(body, name="ssm_dt_grad", out_shape=jax.ShapeDtypeStruct(log_dt.shape, _F32))(d_dt, log_dt)


def _load_strided(ref, nr):
    return jnp.concatenate([ref[pl.ds(r, SUBLANES, stride=nr), :] for r in range(nr)], axis=0)


def _store_strided(ref, val, nr):
    for r in range(nr):
        ref[pl.ds(r, SUBLANES, stride=nr), :] = val[r * SUBLANES:(r + 1) * SUBLANES]


def _scan_strided(src_ref, dst_ref, nr, p_ref, carry, reverse, h_ref=None, h_in=None):
    ns = BLOCK_ST
    row = lax.broadcasted_iota(jnp.int32, (SUBLANES, ns), 0)
    bc = lambda v: jnp.broadcast_to(v, (SUBLANES, ns))
    tile = lambda ref, r: (ref[r * SUBLANES:(r + 1) * SUBLANES, 0:ns], ref[r * SUBLANES:(r + 1) * SUBLANES, ns:2 * ns])
    one = nr - 1 if reverse else 0
    ar, ai = bc(p_ref[one:one + 1, 0:ns]), bc(p_ref[one:one + 1, ns:2 * ns])
    xr = xi = None
    for r in (range(nr - 1, -1, -1) if reverse else range(nr)):
        sr, si = tile(src_ref, r)
        xr, xi = (sr, si) if xr is None else (ar * xr - ai * xi + sr, ar * xi + ai * xr + si)
        dst_ref[r * SUBLANES:(r + 1) * SUBLANES, 0:ns] = xr
        dst_ref[r * SUBLANES:(r + 1) * SUBLANES, ns:2 * ns] = xi
    edge, shift = (SUBLANES - 1, SUBLANES - 1) if reverse else (0, 1)
    dr = jnp.where(row == edge, carry[0], pltpu.roll(xr, shift, 0))
    di = jnp.where(row == edge, carry[1], pltpu.roll(xi, shift, 0))
    for i, k in enumerate((1, 2, 4)):
        qr, qi = bc(p_ref[nr + i:nr + i + 1, 0:ns]), bc(p_ref[nr + i:nr + i + 1, ns:2 * ns])
        keep = (row < SUBLANES - k) if reverse else (row >= k)
        sr = jnp.where(keep, pltpu.roll(dr, (SUBLANES - k) if reverse else k, 0), 0.0)
        si = jnp.where(keep, pltpu.roll(di, (SUBLANES - k) if reverse else k, 0), 0.0)
        dr, di = dr + qr * sr - qi * si, di + qr * si + qi * sr
    acc_r = acc_i = jnp.zeros((SUBLANES, ns), _F32)
    out = None
    for r in range(nr):
        wr, wi = p_ref[r:r + 1, 0:ns], p_ref[r:r + 1, ns:2 * ns]
        xr, xi = tile(dst_ref, r)
        xr, xi = xr + wr * dr - wi * di, xi + wr * di + wi * dr
        dst_ref[r * SUBLANES:(r + 1) * SUBLANES, 0:ns] = xr
        dst_ref[r * SUBLANES:(r + 1) * SUBLANES, ns:2 * ns] = xi
        if h_ref is not None:
            if r == 0:
                lr, li = tile(h_ref, nr - 1)
                pr, pi = jnp.where(row == 0, h_in[0], pltpu.roll(lr, 1, 0)), jnp.where(row == 0, h_in[1], pltpu.roll(li, 1, 0))
            else:
                pr, pi = tile(h_ref, r - 1)
            acc_r = acc_r + (xr * pr + xi * pi)
            acc_i = acc_i + (xi * pr - xr * pi)
        if r == (0 if reverse else nr - 1):
            out = (xr[0:1, :], xi[0:1, :]) if reverse else (xr[SUBLANES - 1:SUBLANES, :], xi[SUBLANES - 1:SUBLANES, :])
    if h_ref is None:
        return out
    return out, (_colsum(acc_r), _colsum(acc_i))


def _ssm_gate(y, wg_ref, bg_ref):
    yg = _gelu(y)
    gate = _sigmoid(jnp.dot(yg.astype(_MXU), wg_ref[...].astype(_MXU), preferred_element_type=_F32) + bg_ref[...])
    return yg, gate


def _ssm_specs(nb, nt, t, reverse):
    tt = (lambda ti: nt - 1 - ti) if reverse else (lambda ti: ti)
    ns2 = 2 * BLOCK_ST
    return dict(
        z=pl.BlockSpec((t, BLOCK_CH), lambda b, ti: (tt(ti), b)),
        bbt=pl.BlockSpec((None, BLOCK_CH, ns2), lambda b, ti: (b, 0, 0)),
        ct=pl.BlockSpec((None, ns2, BLOCK_CH), lambda b, ti: (b, 0, 0)),
        vec=pl.BlockSpec((1, BLOCK_CH), lambda b, ti: (0, b)),
        wg=pl.BlockSpec((None, BLOCK_CH, BLOCK_CH), lambda b, ti: (b, 0, 0)),
        p=pl.BlockSpec((None, t // SUBLANES + SUBLANES, ns2), lambda b, ti: (b, 0, 0)),
        hb=pl.BlockSpec((None, None, SUBLANES, ns2), lambda b, ti: (b, tt(ti), 0, 0)),
        h=pl.BlockSpec((None, t, ns2), lambda b, ti: (b, tt(ti), 0)),
        acc_vec=pl.BlockSpec((None, 1, ns2), lambda b, ti: (b, 0, 0)),
    )


def _ssm_fwd(z, bbt, ct, dvec, wg, bglu, ptab):
    s = z.shape[0]
    nb = bbt.shape[0]
    t = _tile(s, TIME_TILE, SUBLANES)
    nt = s // t
    ns = BLOCK_ST
    sp = _ssm_specs(nb, nt, t, False)

    nr = t // SUBLANES

    def body(z_ref, bbt_ref, ct_ref, d_ref, wg_ref, bg_ref, p_ref, y2_ref, y_ref, h_ref, hb_ref, bu_scr, h_scr, carry_scr):
        _zero_first(pl.program_id(1) == 0, carry_scr)
        hb_ref[...] = carry_scr[...]
        carry_in = (carry_scr[0:1, 0:ns], carry_scr[0:1, ns:2 * ns])
        u = _load_strided(z_ref, nr)
        bu_scr[...] = jnp.dot(u.astype(_MXU), bbt_ref[...].astype(_MXU), preferred_element_type=_F32)
        cr, ci = _scan_strided(bu_scr, h_scr, nr, p_ref, carry_in, False)
        hx = h_scr[...].astype(_MXU)
        h_ref[...] = hx
        y = jnp.dot(hx, ct_ref[...].astype(_MXU), preferred_element_type=_F32) + d_ref[...] * u
        yg, gate = _ssm_gate(y, wg_ref, bg_ref)
        _store_strided(y2_ref, yg * gate, nr)
        _store_strided(y_ref, y, nr)
        carry_scr[:, 0:ns] = jnp.broadcast_to(cr, (SUBLANES, ns))
        carry_scr[:, ns:2 * ns] = jnp.broadcast_to(ci, (SUBLANES, ns))

    ych = jax.ShapeDtypeStruct((s, nb * BLOCK_CH), _F32)
    return pl.pallas_call(
        body, name="ssm_fwd", grid=(nb, nt),
        out_shape=[ych, ych, jax.ShapeDtypeStruct((nb, s, 2 * ns), _MXU),
                   jax.ShapeDtypeStruct((nb, nt, SUBLANES, 2 * ns), _F32)],
        in_specs=[sp["z"], sp["bbt"], sp["ct"], sp["vec"], sp["wg"], sp["vec"], sp["p"]],
        out_specs=[sp["z"], sp["z"], sp["h"], sp["hb"]],
        scratch_shapes=[pltpu.VMEM((t, 2 * ns), _F32), pltpu.VMEM((t, 2 * ns), _F32), pltpu.VMEM((SUBLANES, 2 * ns), _F32)],
        compiler_params=_cp("parallel", "arbitrary"),
    )(z, bbt, ct, dvec, wg, bglu, ptab)


def _ssm_bwd(z, y_pre, h_all, dy2, hb, bbt, ct, dvec, wg, bglu, ptab_rev):
    s = z.shape[0]
    nb = bbt.shape[0]
    t = _tile(s, TIME_TILE, SUBLANES)
    nt = s // t
    ns = BLOCK_ST
    sp = _ssm_specs(nb, nt, t, True)
    tn_dims = (((0,), (0,)), ((), ()))
    nt_dims = (((1,), (1,)), ((), ()))

    nr = t // SUBLANES

    def body(z_ref, y_ref, h_ref, dy2_ref, hb_ref, bbt_ref, ct_ref, d_ref, wg_ref, bg_ref, pr_ref,
             dz_ref, dbbt_ref, dct_ref, dwg_ref, dlb_ref, dd_ref, dbg_ref, bu_scr, g_scr, h_scr, gcarry_scr):
        first = pl.program_id(1) == 0

        _zero_first(first, gcarry_scr, dbbt_ref, dct_ref, dwg_ref, dlb_ref, dd_ref, dbg_ref)
        u = _load_strided(z_ref, nr)
        hin = hb_ref[...]
        y = _load_strided(y_ref, nr)
        yg, gate = _ssm_gate(y, wg_ref, bg_ref)
        dy2 = _load_strided(dy2_ref, nr)
        dpre = dy2 * yg * gate * (1.0 - gate)
        _acc(dbg_ref, first, _colsum(dpre))
        dpx = dpre.astype(_MXU)
        _acc(dwg_ref, first, lax.dot_general(yg.astype(_MXU), dpx, tn_dims, preferred_element_type=_F32))
        dyg = dy2 * gate + lax.dot_general(dpx, wg_ref[...].astype(_MXU), nt_dims, preferred_element_type=_F32)
        dy = dyg * _gelu_grad(y)
        _acc(dd_ref, first, _colsum(dy * u))
        dyx = dy.astype(_MXU)
        hx = h_ref[...]
        h_scr[...] = hx.astype(_F32)
        _acc(dct_ref, first, lax.dot_general(hx, dyx, tn_dims, preferred_element_type=_F32))
        bu_scr[...] = lax.dot_general(dyx, ct_ref[...].astype(_MXU), nt_dims, preferred_element_type=_F32)
        gin = (gcarry_scr[0:1, 0:ns], gcarry_scr[0:1, ns:2 * ns])
        (gr, gi), (d_ar, d_ai) = _scan_strided(bu_scr, g_scr, nr, pr_ref, gin, True, h_scr,
                                               (hin[0:1, 0:ns], hin[0:1, ns:2 * ns]))
        gcarry_scr[:, 0:ns] = jnp.broadcast_to(gr, (SUBLANES, ns))
        gcarry_scr[:, ns:2 * ns] = jnp.broadcast_to(gi, (SUBLANES, ns))
        _acc(dlb_ref, first, jnp.concatenate([d_ar, d_ai], axis=1))
        gx = g_scr[...].astype(_MXU)
        _acc(dbbt_ref, first, lax.dot_general(u.astype(_MXU), gx, tn_dims, preferred_element_type=_F32))
        _store_strided(dz_ref, dy * d_ref[...] + lax.dot_general(gx, bbt_ref[...].astype(_MXU), nt_dims,
                                                                 preferred_element_type=_F32), nr)

    f = lambda shape: jax.ShapeDtypeStruct(shape, _F32)
    return pl.pallas_call(
        body, name="ssm_bwd", grid=(nb, nt),
        out_shape=[f((s, nb * BLOCK_CH)), f(bbt.shape), f(ct.shape), f(wg.shape), f((nb, 1, 2 * ns)),
                   f((1, nb * BLOCK_CH)), f((1, nb * BLOCK_CH))],
        in_specs=[sp["z"], sp["z"], sp["h"], sp["z"], sp["hb"], sp["bbt"], sp["ct"], sp["vec"], sp["wg"], sp["vec"], sp["p"]],
        out_specs=[sp["z"], sp["bbt"], sp["ct"], sp["wg"], sp["acc_vec"], sp["vec"], sp["vec"]],
        scratch_shapes=[pltpu.VMEM((t, 2 * ns), _F32), pltpu.VMEM((t, 2 * ns), _F32), pltpu.VMEM((t, 2 * ns), _F32),
                        pltpu.VMEM((SUBLANES, 2 * ns), _F32)],
        compiler_params=_cp("parallel", "arbitrary"),
    )(z, y_pre, h_all, dy2, hb, bbt, ct, dvec, wg, bglu, ptab_rev)


def _mod_part(c_all, w, b):
    d, ns = w.shape
    tn = _tile(ns, 512)

    def body(c_ref, w_ref, b_ref, o_ref):
        c = c_ref[...]
        ca = (c * _sigmoid(c)).astype(_MXU)
        o_ref[...] = jnp.dot(ca, w_ref[...].astype(_MXU), preferred_element_type=_F32) + b_ref[...]

    return pl.pallas_call(
        body, name="mod_part", grid=(ns // tn,), out_shape=jax.ShapeDtypeStruct((8, ns), _F32),
        in_specs=[pl.BlockSpec((8, d), lambda n: (0, 0)), pl.BlockSpec((d, tn), lambda n: (0, n)),
                  pl.BlockSpec((1, tn), lambda n: (0, n))],
        out_specs=pl.BlockSpec((8, tn), lambda n: (0, n)), compiler_params=_cp("parallel"),
    )(c_all, w, b)


def _adamw_math(w, g, m, v):
    m = ADAM_B1 * m + (1.0 - ADAM_B1) * g
    v = ADAM_B2 * v + (1.0 - ADAM_B2) * (g * g)
    m_hat = m / (1.0 - ADAM_B1 ** ADAM_STEP)
    v_hat = v / (1.0 - ADAM_B2 ** ADAM_STEP)
    delta = -ADAM_LR * (m_hat / (jnp.sqrt(v_hat) + ADAM_EPS) + ADAM_WD * w)
    return delta, m, v


def _adamw(w, g, m, v, name):
    r, c = w.shape
    tc = c if c <= 4096 else _tile(c, 4096)
    tr = _tile(r, max(SUBLANES, (1 << 18) // tc), SUBLANES)

    def body(w_ref, g_ref, m_ref, v_ref, go_ref, d_ref, mo_ref, vo_ref):
        g = g_ref[...]
        go_ref[...] = g
        d_ref[...], mo_ref[...], vo_ref[...] = _adamw_math(w_ref[...], g, m_ref[...], v_ref[...])

    spec = pl.BlockSpec((tr, tc), lambda i, j: (i, j))
    out = jax.ShapeDtypeStruct((r, c), _F32)
    return pl.pallas_call(
        body, name=name, grid=(r // tr, c // tc), in_specs=[spec] * 4, out_specs=[spec] * 4, out_shape=[out] * 4,
        compiler_params=_cp("parallel", "parallel"),
    )(w, g, m, v)


def _adamw_halves(w, g2, m, v, name):
    r, c = w.shape
    tr, tc = _tile(r, 256, SUBLANES), _tile(c // 2, 1024)
    nph = (c // 2) // tc

    def body(w_ref, g_ref, m_ref, v_ref, go_ref, d_ref, mo_ref, vo_ref):
        g = g_ref[...]
        go_ref[...] = g
        d_ref[...], mo_ref[...], vo_ref[...] = _adamw_math(w_ref[...], g, m_ref[...], v_ref[...])

    spec = pl.BlockSpec((tr, tc), lambda i, j: (i, j))
    out = jax.ShapeDtypeStruct((r, c), _F32)
    return pl.pallas_call(
        body, name=name, grid=(r // tr, c // tc),
        in_specs=[spec, pl.BlockSpec((None, tr, tc), lambda i, j: (j // nph, i, j % nph)), spec, spec],
        out_specs=[spec] * 4, out_shape=[out] * 4, compiler_params=_cp("parallel", "parallel"),
    )(w, g2, m, v)


def _wada_update(c_t, dm, w, m, v):
    d, ns = w.shape
    tr, tc = _tile(d, 256, SUBLANES), _tile(ns, 1024)

    def body(c_ref, dm_ref, w_ref, m_ref, v_ref, g_ref, d_ref, mo_ref, vo_ref):
        c = c_ref[...]
        ca = c * _sigmoid(c)
        dmv = dm_ref[...]
        g = ca[:, 0:1] * dmv[0:1, :]
        for b in range(1, 8):
            g = g + ca[:, b:b + 1] * dmv[b:b + 1, :]
        g_ref[...] = g
        d_ref[...], mo_ref[...], vo_ref[...] = _adamw_math(w_ref[...], g, m_ref[...], v_ref[...])

    spec = pl.BlockSpec((tr, tc), lambda i, j: (i, j))
    out = jax.ShapeDtypeStruct((d, ns), _F32)
    return pl.pallas_call(
        body, name="wada_update", grid=(d // tr, ns // tc),
        in_specs=[pl.BlockSpec((tr, 8), lambda i, j: (i, 0)), pl.BlockSpec((8, tc), lambda i, j: (0, j)), spec, spec, spec],
        out_specs=[spec] * 4, out_shape=[out] * 4, compiler_params=_cp("parallel", "parallel"),
    )(c_t, dm, w, m, v)


def _small_reduce(gathered):
    _, r, c = gathered.shape
    tr = _tile(r, 512, SUBLANES)

    def body(q_ref, g_ref):
        g = q_ref[0]
        for k in range(1, 8):
            g = g + q_ref[k]
        g_ref[...] = g

    return pl.pallas_call(
        body, name="small_reduce", grid=(r // tr,), out_shape=jax.ShapeDtypeStruct((r, c), _F32),
        in_specs=[pl.BlockSpec((8, tr, c), lambda i: (0, i, 0))], out_specs=pl.BlockSpec((tr, c), lambda i: (i, 0)),
        compiler_params=_cp("parallel"),
    )(gathered)


def _adamw_many(ws, gs, ms, vs, steps, name):
    n = len(ws)

    def body(*refs):
        w_refs, g_refs, m_refs, v_refs = refs[0:n], refs[n:2 * n], refs[2 * n:3 * n], refs[3 * n:4 * n]
        d_refs, mo_refs, vo_refs = refs[4 * n:5 * n], refs[5 * n:6 * n], refs[6 * n:7 * n]
        for i in range(n):
            d_refs[i][...], mo_refs[i][...], vo_refs[i][...] = _adamw_math(
                w_refs[i][...], g_refs[i][...], m_refs[i][...], v_refs[i][...])

    def spec(a):
        nd = a.ndim
        if steps == 1:
            return pl.BlockSpec(a.shape, lambda i: (0,) * nd)
        return pl.BlockSpec((a.shape[0] // steps,) + a.shape[1:], lambda i: (i,) + (0,) * (nd - 1))

    specs = [spec(w) for w in ws]
    outs = pl.pallas_call(
        body, name=name, grid=(steps,), in_specs=specs * 4, out_specs=specs * 3,
        out_shape=[jax.ShapeDtypeStruct(w.shape, _F32) for w in ws] * 3, compiler_params=_cp("parallel"),
    )(*ws, *gs, *ms, *vs)
    return outs[0:n], outs[n:2 * n], outs[2 * n:3 * n]


def _block_diag(x, eye=None):
    nb, g, p, q = x.shape
    eye = jnp.eye(g, dtype=x.dtype) if eye is None else eye
    return (x[:, :, :, None, :] * eye[None, :, None, :, None]).reshape(nb, g * p, g * q)


def _block_diag_take(x, p, q):
    nb = x.shape[0]
    g = GROUPS_PER_BLOCK
    eye = jnp.eye(g, dtype=x.dtype)
    return jnp.sum(x.reshape(nb, g, p, g, q) * eye[None, :, None, :, None], axis=3)


_VIEWS = {"ssm_b_re": ((0, 2, 1), (0, 2, 1)), "ssm_b_im": ((0, 2, 1), (0, 2, 1)),
          "ssm_w_glu": ((1, 2, 0), (2, 0, 1)), "ssm_b_glu": ((1, 0), (1, 0))}


def _to_view(name, a):
    return a.transpose(_VIEWS[name][0]) if name in _VIEWS else a


def _from_view(name, a):
    return a.transpose(_VIEWS[name][1]) if name in _VIEWS else a


class _Pack:
    def __init__(self, shapes):
        self.shapes = shapes
        self.offsets = {}
        off = 0
        for name, shape in shapes.items():
            n = math.prod(shape)
            self.offsets[name] = (off, n)
            off += -(-n // (SUBLANES * LANES)) * (SUBLANES * LANES)
        self.rows = -(-off // (256 * LANES)) * 256

    def pack(self, arrays):
        parts = []
        off = 0
        for name, shape in self.shapes.items():
            start, n = self.offsets[name]
            if start > off:
                parts.append(jnp.zeros((start - off,), _F32))
            parts.append(arrays[name].reshape(-1).astype(_F32))
            off = start + n
        total = self.rows * LANES
        if total > off:
            parts.append(jnp.zeros((total - off,), _F32))
        return jnp.concatenate(parts).reshape(self.rows, LANES)

    def unpack(self, buf):
        flat = buf.reshape(-1)
        return {name: flat[start:start + n].reshape(self.shapes[name]) for name, (start, n) in self.offsets.items()}


_SMALL = ["b_ada", "g_pre_mix", "g_post_mix", "ssm_log_dt", "ssm_a_re", "ssm_a_im", "ssm_b_re", "ssm_b_im", "ssm_c_re",
          "ssm_c_im", "ssm_d", "ssm_w_glu", "ssm_b_glu", "sgu_ln_g", "sgu_ln_b", "sgu_w", "sgu_b", "g_out_ssm",
          "g_out_sgu", "g_pre_ffn", "g_post_ffn", "conv_b"]
_WEIGHTS = ["w_ada", "b_ada", "g_pre_mix", "g_post_mix", "w_in", "ssm_log_dt", "ssm_a_re", "ssm_a_im", "ssm_b_re",
            "ssm_b_im", "ssm_c_re", "ssm_c_im", "ssm_d", "ssm_w_glu", "ssm_b_glu", "sgu_ln_g", "sgu_ln_b", "sgu_w", "sgu_b",
            "g_out_ssm", "g_out_sgu", "w_out", "g_pre_ffn", "g_post_ffn", "w_up", "conv_w", "conv_b", "w_down"]


def _step(p, m, v, x, c, tgt):
    s, d = x.shape
    mx, my, mc = lax.axis_index("x"), lax.axis_index("y"), lax.axis_index("c")
    chip = 2 * mx + my
    dev = 4 * mx + 2 * my + mc
    sel = jnp.stack([chip, mc]).astype(jnp.int32)
    g_cnt, n_st = p["ssm_a_re"].shape
    nb = g_cnt // GROUPS_PER_BLOCK
    gn = g_cnt * n_st
    d_ssm = g_cnt * SSM_GROUP
    nh = p["sgu_w"].shape[0]
    assert nh * CHUNK == d_ssm and 2 * d_ssm == d and n_st == SSM_STATE

    shards = lambda g: g.reshape(4, g.shape[1] * g.shape[2], g.shape[3])
    buf_in = _cast_into_slot(p["w_in"], sel, sel, "cast_w_in")

    ns_ada = p["w_ada"].shape[1]
    nc_conv = p["conv_w"].shape[1]
    first = jnp.concatenate([jnp.broadcast_to(c, (8, d)), jnp.pad(p["conv_w"], ((0, 5), (0, 0)))], axis=1)
    first_all = _all_gather8(_own_slot(first, dev), "gather_c_conv", after=buf_in)
    (sems_in,), (buf_in,), tok = _gather_start([buf_in], first_all, "gather_start_in")
    c_all = _after(first_all[:, 0, :d], tok)
    conv_w_full = jnp.concatenate([first_all[2 * j, 0:3, d:] for j in range(4)], axis=1)
    b_ada_mine = lax.dynamic_slice_in_dim(p["b_ada"], chip * ns_ada, ns_ada, axis=1)
    mod_mine = _mod_part(c_all, p["w_ada"], b_ada_mine)
    buf_out, buf_up, buf_down = [_cast_into_slot(p[n], sel, tok, "cast_" + n) for n in ("w_out", "w_up", "w_down")]

    eye_t = jnp.eye(GROUPS_PER_BLOCK, dtype=_F32) + tok[0:1, 0:1]
    ldt_l = _after(jnp.repeat(p["ssm_log_dt"], n_st, axis=1), tok)
    are_l, aim_l = p["ssm_a_re"].reshape(1, gn), p["ssm_a_im"].reshape(1, gn)
    bre_t, bim_t = p["ssm_b_re"].reshape(gn, SSM_GROUP).T, p["ssm_b_im"].reshape(gn, SSM_GROUP).T
    nr = _tile(s, TIME_TILE, SUBLANES) // SUBLANES
    kvec = jnp.concatenate([jnp.arange(1, nr + 1, dtype=_F32), jnp.array([nr, 2 * nr, 4 * nr, 0, 0, 0, 0, 0], _F32)])
    pw_re, pw_im, bb_re, bb_im = _ssm_prep(ldt_l, are_l, aim_l, bre_t, bim_t, kvec.reshape(nr + SUBLANES, 1))
    blocks = lambda t: t.reshape(t.shape[0], nb, GROUPS_PER_BLOCK * n_st).transpose(1, 0, 2)
    ptab = jnp.concatenate([blocks(pw_re), blocks(pw_im)], axis=2)
    rev = lambda t: jnp.concatenate([t[:, :nr][:, ::-1], t[:, nr:]], axis=1)
    ptab_rev = jnp.concatenate([rev(blocks(pw_re)), -rev(blocks(pw_im))], axis=2)
    bd = lambda t: t.reshape(SSM_GROUP, nb, GROUPS_PER_BLOCK, n_st).transpose(1, 2, 0, 3)
    bbt = jnp.concatenate([_block_diag(bd(bb_re)), _block_diag(bd(bb_im))], axis=2).astype(_MXU)
    cd = lambda t: t.reshape(nb, GROUPS_PER_BLOCK, SSM_GROUP, n_st).transpose(0, 1, 3, 2)
    ct = jnp.concatenate([_block_diag(cd(p["ssm_c_re"]), eye_t), -_block_diag(cd(p["ssm_c_im"]), eye_t)], axis=1).astype(_MXU)
    wg = _block_diag(p["ssm_w_glu"].reshape(nb, GROUPS_PER_BLOCK, SSM_GROUP, SSM_GROUP), eye_t).astype(_MXU)
    dvec = p["ssm_d"]
    bglu = p["ssm_b_glu"].reshape(1, d_ssm)
    mask = jnp.tril(jnp.ones((CHUNK, CHUNK), _F32)) + tok[0:1, 0:1]
    wm = (p["sgu_w"] * mask[None]).astype(_MXU)
    bs = p["sgu_b"].reshape(nh, CHUNK, 1)

    mod_all = _all_gather8(_own_slot(mod_mine, dev), "gather_mod",
                           after=[buf_out, buf_up, buf_down, bbt, ct, wg, wm, bs, ptab, ptab_rev])
    (sems_out, sems_up, sems_down), (buf_out, buf_up, buf_down), tok_rest = _gather_start(
        [buf_out, buf_up, buf_down], mod_all, "gather_start_rest")
    mod_rows = lax.dynamic_index_in_dim(mod_all, dev, axis=1, keepdims=False)
    mod = jnp.concatenate([mod_rows[0], mod_rows[2], mod_rows[4], mod_rows[6]]).reshape(N_MOD, 1, d)
    sh1, sc1, gt1, sh2, sc2, gt2 = [mod[i] for i in range(N_MOD)]

    h1 = _fwd_pre_mix(x, p["g_pre_mix"], _after(sc1, tok_rest), sh1)
    buf_in = _gather_wait(sems_in, buf_in, h1, "gather_wait_in")
    w_in4 = shards(_pair_forward([buf_in], "pair_forward_in")[0])
    z = _mm_nn(h1, w_in4, _F32, "mm_in")
    y_ssm, y_pre, h_all, hb = _ssm_fwd(z, bbt, ct, dvec, wg, bglu, ptab)
    y_sgu = _sgu_fwd(z, p["sgu_ln_g"], p["sgu_ln_b"], wm, bs)
    ycat = _mix_norm_fwd(y_ssm, y_sgu, p["g_out_ssm"], p["g_out_sgu"])
    buf_out = _gather_wait(sems_out, buf_out, ycat, "gather_wait_out")
    w_out_full = _pair_forward([buf_out], "pair_forward_out")[0].reshape(1, d, d)
    o = _mm_nn(ycat, w_out_full, _F32, "mm_out")
    x1, h2 = _fwd_mid(o, x, gt1, p["g_post_mix"], p["g_pre_ffn"], sc2, sh2)
    buf_up = _gather_wait(sems_up, buf_up, h2, "gather_wait_up")
    w_up4 = shards(_pair_forward([buf_up], "pair_forward_up")[0])
    up_pre = _mm_nn(h2, w_up4, _F32, "mm_up")
    act = _conv_act_fwd(up_pre, conv_w_full, p["conv_b"])
    buf_down = _gather_wait(sems_down, buf_down, act, "gather_wait_down")
    w_down_full = _pair_forward([buf_down], "pair_forward_down")[0].reshape(1, -1, d)
    f = _mm_nn(act, w_down_full, _F32, "mm_down", tk=5632)
    dx2, df, d_gt2, d_g_post_ffn, loss = _loss_and_post_ffn_bwd(f, x1, tgt, gt2, p["g_post_ffn"])

    def reduce_next(swap, n, after):
        sems, gw, land, _ = swap
        gw, got = _swap_wait(sems, gw, land, after, "swap_wait_" + n)
        return _scatter_start(_pair_sum(gw, got, sel, "pair_sum_" + n), "scatter_start_" + n)

    d_act = _mm_nt(df, w_down_full, _F32, "mm_d_act", tk=2048)
    swap_down = _swap_start(_mm_tn_rows(act, df, "mm_gw_down"), "swap_start_w_down")
    d_up_pre, d_cw0, d_cw1, d_cw2, d_conv_b = _conv_act_bwd(up_pre, d_act, conv_w_full, _after(p["conv_b"], swap_down[3]))
    red_down = reduce_next(swap_down, "w_down", d_conv_b)
    dh2 = _mm_nt(d_up_pre, w_up4, _F32, "mm_dh2", tk=2816, after=red_down[3])
    swap_up = _swap_start(_mm_tn_cols(h2, d_up_pre, "mm_gw_up"), "swap_start_w_up")
    dx1, d_o, d_sc2, d_sh2, d_g_pre_ffn, d_gt1, d_g_post_mix = _bwd_mid(
        dh2, x1, dx2, o, p["g_pre_ffn"], _after(sc2, swap_up[3]), gt1, p["g_post_mix"])
    red_up = reduce_next(swap_up, "w_up", d_g_post_mix)
    d_ycat = _mm_nt(d_o, w_out_full, _F32, "mm_d_ycat", tn=1024, tk=2048, after=red_up[3])
    swap_out = _swap_start(_mm_tn_rows(ycat, d_o, "mm_gw_out"), "swap_start_w_out")
    dy_ssm, dy_sgu, d_g_out_ssm, d_g_out_sgu = _mix_norm_bwd(
        d_ycat, y_ssm, y_sgu, _after(p["g_out_ssm"], swap_out[3]), p["g_out_sgu"])
    red_out = reduce_next(swap_out, "w_out", d_g_out_sgu)
    dz_ssm, d_bbt, d_ct, d_wg, d_lb, d_ssm_d, d_bglu = _ssm_bwd(z, y_pre, h_all, dy_ssm, hb, bbt, ct,
                                                                _after(dvec, red_out[3]), wg, bglu, ptab_rev)
    dz, d_ln_g, d_ln_b, d_wm, d_bs = _sgu_bwd(z, dy_sgu, dz_ssm, p["sgu_ln_g"], p["sgu_ln_b"], wm, bs)
    dh1 = _mm_nt(dz, w_in4, _F32, "mm_dh1")
    swap_in = _swap_start(_mm_tn_cols(h1, dz, "mm_gw_in"), "swap_start_w_in")
    dx, d_sc1, d_sh1, d_g_pre_mix = _bwd_pre_mix(dh1, x, dx1, p["g_pre_mix"], _after(sc1, swap_in[3]))
    red_in = reduce_next(swap_in, "w_in", d_g_pre_mix)

    nsb = BLOCK_ST
    lanes = lambda t: t.transpose(2, 0, 1, 3).reshape(SSM_GROUP, gn)
    d_bbr = lanes(_block_diag_take(d_bbt[:, :, :nsb], SSM_GROUP, n_st))
    d_bbi = lanes(_block_diag_take(d_bbt[:, :, nsb:], SSM_GROUP, n_st))
    d_lr, d_li = d_lb[:, 0, :nsb].reshape(1, gn), d_lb[:, 0, nsb:].reshape(1, gn)
    d_bre_t, d_bim_t, d_are, d_aim, d_dt = _ssm_prep_bwd(ldt_l, are_l, aim_l, bre_t, bim_t, d_bbr, d_bbi, d_lr, d_li)
    d_log_dt = _group_sum(d_dt.reshape(g_cnt, n_st), p["ssm_log_dt"].reshape(g_cnt, 1))
    c_grad = lambda t: _block_diag_take(t, n_st, SSM_GROUP).transpose(0, 1, 3, 2).reshape(g_cnt, SSM_GROUP, n_st)
    small = {
        "b_ada": jnp.concatenate([d_sh1, _after(d_sc1, red_in[3]), d_gt1, d_sh2, d_sc2, d_gt2], axis=1),
        "g_pre_mix": d_g_pre_mix, "g_post_mix": d_g_post_mix,
        "ssm_log_dt": d_log_dt, "ssm_a_re": d_are, "ssm_a_im": d_aim,
        "ssm_b_re": d_bre_t.T, "ssm_b_im": d_bim_t.T,
        "ssm_c_re": c_grad(d_ct[:, :nsb, :]), "ssm_c_im": -c_grad(d_ct[:, nsb:, :]),
        "ssm_d": d_ssm_d, "ssm_w_glu": _block_diag_take(d_wg, SSM_GROUP, SSM_GROUP), "ssm_b_glu": d_bglu,
        "sgu_ln_g": d_ln_g, "sgu_ln_b": d_ln_b, "sgu_w": d_wm * mask[None], "sgu_b": d_bs,
        "g_out_ssm": d_g_out_ssm, "g_out_sgu": d_g_out_sgu, "g_pre_ffn": d_g_pre_ffn, "g_post_ffn": d_g_post_ffn,
        "conv_b": d_conv_b, "conv_w_all": jnp.concatenate([d_cw0, d_cw1, d_cw2], axis=0),
        "loss_sum": loss,
    }
    small = {n: _to_view(n, a.reshape(p[n].shape)) if n in p else a for n, a in small.items()}
    pk = _Pack({n: a.shape for n, a in small.items()})
    sems_small, small_buf, tok = _gather8_start(_own_slot(pk.pack(small), dev), "gather_small_start")

    big = ["w_down", "w_up", "w_out", "w_in"]
    joins = []
    after = tok
    for n, (sems, pair, land, _) in zip(big, (red_down, red_up, red_out, red_in)):
        pair, land = _scatter_wait(sems, pair, land, after, "scatter_wait_" + n)
        sems_j, half, after = _join_start(_chip_sum(pair, land, sel, "chip_sum_" + n), "join_start_" + n)
        joins.append((sems_j, half))
    big_out = {}
    for n, (sems_j, half) in zip(big, joins):
        j = _join_wait(sems_j, half, after, "join_wait_" + n)
        if n in ("w_in", "w_up"):
            big_out[n] = tuple(_adamw(p[n], j.reshape(p[n].shape), m[n], v[n], "adamw_" + n))
        else:
            big_out[n] = tuple(_adamw_halves(p[n], j, m[n], v[n], "adamw_" + n))
        after = big_out[n][1]

    gathered = _gather8_forward(_gather8_wait(sems_small, small_buf, after, "gather_small_wait"),
                                "gather_small_forward")
    gview = pk.unpack(_small_reduce(gathered))
    gview["conv_w"] = lax.dynamic_slice_in_dim(gview.pop("conv_w_all"), chip * nc_conv, nc_conv, axis=1)
    loss = gview.pop("loss_sum")
    small_names = _SMALL + ["conv_w"]
    per_group = [n for n in small_names if gview[n].ndim >= 2 and gview[n].shape[0] == g_cnt]
    others = [n for n in small_names if n not in per_group]
    grads = {n: _from_view(n, gview[n]) for n in small_names}
    deltas, new_m, new_v = {}, {}, {}
    for names, steps, call in ((per_group, g_cnt // GROUPS_PER_BLOCK, "adamw_s5"), (others, 1, "adamw_small")):
        res = _adamw_many([_to_view(n, p[n]) for n in names], [gview[n] for n in names],
                          [_to_view(n, m[n]) for n in names], [_to_view(n, v[n]) for n in names], steps, call)
        for n, dl, mo, vo in zip(names, *res):
            deltas[n], new_m[n], new_v[n] = _from_view(n, dl), _from_view(n, mo), _from_view(n, vo)

    d_mod_all = gathered.reshape(8, -1)[:, :N_MOD * d]
    d_mod_mine = lax.dynamic_slice_in_dim(d_mod_all, chip * ns_ada, ns_ada, axis=1)
    grads["w_ada"], deltas["w_ada"], new_m["w_ada"], new_v["w_ada"] = _wada_update(
        c_all.T, d_mod_mine, p["w_ada"], m["w_ada"], v["w_ada"])
    for n in big:
        grads[n], deltas[n], new_m[n], new_v[n] = big_out[n]
    return loss[0, 0], dx, grads, deltas, new_m, new_v


def kernel(x, c, w_ada, b_ada, g_pre_mix, g_post_mix, w_in, ssm_log_dt, ssm_a_re, ssm_a_im, ssm_b_re, ssm_b_im, ssm_c_re, ssm_c_im, ssm_d, ssm_w_glu, ssm_b_glu, sgu_ln_g, sgu_ln_b, sgu_w, sgu_b, g_out_ssm, g_out_sgu, w_out, g_pre_ffn, g_post_ffn, w_up, conv_w, conv_b, w_down, loss_target, m_w_ada, m_b_ada, m_g_pre_mix, m_g_post_mix, m_w_in, m_ssm_log_dt, m_ssm_a_re, m_ssm_a_im, m_ssm_b_re, m_ssm_b_im, m_ssm_c_re, m_ssm_c_im, m_ssm_d, m_ssm_w_glu, m_ssm_b_glu, m_sgu_ln_g, m_sgu_ln_b, m_sgu_w, m_sgu_b, m_g_out_ssm, m_g_out_sgu, m_w_out, m_g_pre_ffn, m_g_post_ffn, m_w_up, m_conv_w, m_conv_b, m_w_down, v_w_ada, v_b_ada, v_g_pre_mix, v_g_post_mix, v_w_in, v_ssm_log_dt, v_ssm_a_re, v_ssm_a_im, v_ssm_b_re, v_ssm_b_im, v_ssm_c_re, v_ssm_c_im, v_ssm_d, v_ssm_w_glu, v_ssm_b_glu, v_sgu_ln_g, v_sgu_ln_b, v_sgu_w, v_sgu_b, v_g_out_ssm, v_g_out_sgu, v_w_out, v_g_pre_ffn, v_g_post_ffn, v_w_up, v_conv_w, v_conv_b, v_w_down):
    given = dict(locals())
    drop = lambda a: a if a.ndim == 2 else a[0]
    p = {n: drop(given[n]) for n in _WEIGHTS}
    m = {n: drop(given["m_" + n]) for n in _WEIGHTS}
    v = {n: drop(given["v_" + n]) for n in _WEIGHTS}
    loss, dx, grads, deltas, new_m, new_v = _step(p, m, v, x[0], c, loss_target[0])
    outs = [loss, dx[None]]
    for group in (grads, deltas, new_m, new_v):
        outs += [group[n].reshape(given[n].shape) for n in _WEIGHTS]
    return tuple(outs)
```

```python
import functools
import math

import jax
import jax.numpy as jnp
from jax import lax
from jax.experimental import pallas as pl
from jax.experimental.pallas import tpu as pltpu

_F32 = jnp.float32
_MXU = jnp.bfloat16
_WIRE = jnp.bfloat16

EPS = 1e-6
SSM_GROUP = 16
SSM_STATE = 64
GROUPS_PER_BLOCK = 8
BLOCK_CH = SSM_GROUP * GROUPS_PER_BLOCK
BLOCK_ST = SSM_STATE * GROUPS_PER_BLOCK
CHUNK = 128
TIME_TILE = 512
SUBLANES = 8
LANES = 128
N_MOD = 6
ADAM_LR, ADAM_B1, ADAM_B2, ADAM_EPS, ADAM_WD, ADAM_STEP = 0.001, 0.9, 0.999, 1e-08, 0.01, 10
_VMEM_LIMIT = 56 * 1024 * 1024
_MESH = pl.DeviceIdType.MESH
_ANY = pl.BlockSpec(memory_space=pl.ANY)
_HBM = pl.BlockSpec(memory_space=pltpu.HBM)
_SEM = pl.BlockSpec(memory_space=pltpu.SEMAPHORE)
_VMEM_WHOLE = pl.BlockSpec(memory_space=pltpu.VMEM)
_EFFECT = pltpu.SideEffectType.DATAFLOW_SIDE_EFFECTING
_GELU_C = math.sqrt(2.0 / math.pi)


def _cp(*sem):
    return pltpu.CompilerParams(dimension_semantics=sem, vmem_limit_bytes=_VMEM_LIMIT)


def _tile(dim, target, align=LANES):
    if dim <= target:
        return dim
    best = None
    for t in range(align, target + 1, align):
        if dim % t == 0:
            best = t
    assert best is not None, (dim, target, align)
    return best


def _gelu(x):
    return 0.5 * x * (1.0 + jnp.tanh(_GELU_C * (x + 0.044715 * (x * x * x))))


def _gelu_grad(x):
    t = jnp.tanh(_GELU_C * (x + 0.044715 * (x * x * x)))
    return 0.5 * (1.0 + t) + 0.5 * x * (1.0 - t * t) * (_GELU_C * (1.0 + 3.0 * 0.044715 * x * x))


def _sigmoid(x):
    return 1.0 / (1.0 + jnp.exp(-x))


def _colsum(x):
    return jnp.sum(x, axis=0, keepdims=True)


def _rowmean(x):
    return jnp.mean(x, axis=-1, keepdims=True)


def _zero_first(first, *refs):
    @pl.when(first)
    def _():
        for ref in refs:
            ref[...] = jnp.zeros_like(ref)


def _acc(ref, first, val):
    del first
    ref[...] += val


def _place():
    mx, my, mc = lax.axis_index("x"), lax.axis_index("y"), lax.axis_index("c")
    chips = [(1 - mx, my), (mx, 1 - my), (1 - mx, 1 - my)]
    return mx, my, mc, chips


def _all_gather8(buf, name, after=None):
    extra = [] if after is None else (list(after) if isinstance(after, (list, tuple)) else [after])

    def body(in_ref, *rest):
        out_ref, send_sems, recv_sems = rest[len(extra):]
        mx, my, mc, chips = _place()
        me, sibling = (mx, my, mc), (mx, my, 1 - mc)

        def slot(ref, px, py, pc):
            return ref.at[4 * px + 2 * py + pc]

        def copy(k, block, to, src_ref=out_ref):
            return pltpu.make_async_remote_copy(
                src_ref=slot(src_ref, *block), dst_ref=slot(out_ref, *block),
                send_sem=send_sems.at[k], recv_sem=recv_sems.at[k], device_id=to, device_id_type=_MESH)

        first = [copy(0, me, sibling, in_ref)]
        first += [copy(1 + j, me, (*chip, mc), in_ref) for j, chip in enumerate(chips)]
        for cp in first:
            cp.start()
        passed = [copy(4 + j, (*chip, mc), sibling) for j, chip in enumerate(chips)]
        for j, chip in enumerate(chips):
            copy(1 + j, (*chip, mc), me).wait_recv()
            passed[j].start()
        copy(0, sibling, me).wait_recv()
        for j, chip in enumerate(chips):
            copy(4 + j, (*chip, 1 - mc), me).wait_recv()
        for cp in first + passed:
            cp.wait_send()

    return pl.pallas_call(
        body, name=name, out_shape=jax.ShapeDtypeStruct(buf.shape, buf.dtype),
        in_specs=[_ANY] * (1 + len(extra)), out_specs=_ANY, input_output_aliases={0: 0},
        scratch_shapes=[pltpu.SemaphoreType.DMA((7,)), pltpu.SemaphoreType.DMA((7,))],
    )(buf, *extra)


def _own_slot(x, dev):
    return lax.dynamic_update_slice(jnp.zeros((8,) + x.shape, x.dtype), x[None], (dev, 0, 0))


def _cast_into_slot(w, sel, after, name):
    r, c = w.shape
    hr = r // 2
    tr = _tile(hr, 256, 16)
    nr = hr // tr

    def body(sel_ref, w_ref, after_ref, o_ref):
        o_ref[...] = w_ref[...].astype(o_ref.dtype)

    return pl.pallas_call(
        body, name=name, out_shape=jax.ShapeDtypeStruct((4, 2, hr, c), _WIRE),
        grid_spec=pltpu.PrefetchScalarGridSpec(
            num_scalar_prefetch=1, grid=(2, nr),
            in_specs=[pl.BlockSpec((tr, c), lambda h, i, s: (h * nr + i, 0)), _ANY],
            out_specs=pl.BlockSpec((None, None, tr, c), lambda h, i, s: (s[0], h, i, 0))),
        compiler_params=_cp("parallel", "parallel"),
    )(sel, w, after)


def _hbm(a):
    return pltpu.with_memory_space_constraint(a, pltpu.HBM)


def _after(vec, token):
    return vec + token[0:1, 0:1]


def _gather_start(bufs, after, name):
    n = len(bufs)
    nc = 3 * n

    def body(*refs):
        ins, send, recv, token = refs[:n], refs[n + 1:n + 1 + nc], refs[n + 1 + nc:n + 1 + 2 * nc], refs[-1]
        mx, my, mc, chips = _place()
        j_me = 2 * mx + my
        for i in range(n):
            for k, chip in enumerate(chips):
                half = ins[i].at[j_me, mc]
                pltpu.make_async_remote_copy(
                    src_ref=half, dst_ref=half, send_sem=send[3 * i + k], recv_sem=recv[3 * i + k],
                    device_id=(*chip, mc), device_id_type=_MESH).start()
        token[...] = jnp.zeros_like(token)

    outs = pl.pallas_call(
        body, name=name,
        out_shape=tuple([pltpu.SemaphoreType.DMA(())] * (2 * nc) + [pltpu.HBM(b.shape, b.dtype) for b in bufs]
                        + [jax.ShapeDtypeStruct((SUBLANES, LANES), _F32)]),
        in_specs=tuple([_HBM] * n + [_ANY]), out_specs=tuple([_SEM] * (2 * nc) + [_HBM] * n + [_VMEM_WHOLE]),
        input_output_aliases={i: 2 * nc + i for i in range(n)},
        compiler_params=pltpu.CompilerParams(has_side_effects=_EFFECT),
    )(*[_hbm(b) for b in bufs], after)
    sems = [(outs[3 * i:3 * i + 3], outs[nc + 3 * i:nc + 3 * i + 3]) for i in range(n)]
    return sems, list(outs[2 * nc:2 * nc + n]), outs[-1]


def _gather_wait(sems, buf, after, name):
    send, recv = sems

    after = list(after) if isinstance(after, (list, tuple)) else [after]

    def body(buf_ref, s0, s1, s2, r0, r1, r2, *rest):
        mx, my, mc, chips = _place()
        j_me = 2 * mx + my
        for k, (chip, s_k, r_k) in enumerate(zip(chips, (s0, s1, s2), (r0, r1, r2))):
            cp = pltpu.make_async_remote_copy(
                src_ref=buf_ref.at[j_me, mc], dst_ref=buf_ref.at[2 * chip[0] + chip[1], mc], send_sem=s_k, recv_sem=r_k,
                device_id=(*chip, mc), device_id_type=_MESH)
            cp.wait_send()
            cp.wait_recv()

    return pl.pallas_call(
        body, name=name, out_shape=pltpu.HBM(buf.shape, buf.dtype),
        in_specs=(_HBM,) + (_SEM,) * 6 + (_ANY,) * len(after), out_specs=_HBM, input_output_aliases={0: 0},
        compiler_params=pltpu.CompilerParams(has_side_effects=_EFFECT),
    )(buf, *send, *recv, *after)


def _route_ends(buf_ref, phase):
    mx, my, mc, _ = _place()
    hq = buf_ref.shape[2] // 2
    xn, yn = (1 - mx, my), (mx, 1 - my)
    j_me, j_x, j_y, j_d = 2 * mx + my, 2 * (1 - mx) + my, 2 * mx + (1 - my), 2 * (1 - mx) + (1 - my)
    if phase == 1:
        mine = buf_ref.at[j_me, mc]
        return [((*xn, mc), mine, buf_ref.at[j_x, mc]), ((*yn, mc), mine, buf_ref.at[j_y, mc])]
    lo, hi = pl.ds(0, hq), pl.ds(hq, hq)
    return [((*xn, mc), buf_ref.at[j_y, mc, lo], buf_ref.at[j_d, mc, lo]),
            ((*yn, mc), buf_ref.at[j_x, mc, hi], buf_ref.at[j_d, mc, hi])]


def _route_start(items, after, name):
    n = len(items)

    def body(*refs):
        ins, send, recv, token = refs[:n], refs[n + 1:3 * n + 1], refs[3 * n + 1:5 * n + 1], refs[-1]
        for i, (_, phase) in enumerate(items):
            for k, (peer, src, _) in enumerate(_route_ends(ins[i], phase)):
                pltpu.make_async_remote_copy(src_ref=src, dst_ref=src, send_sem=send[2 * i + k], recv_sem=recv[2 * i + k],
                                             device_id=peer, device_id_type=_MESH).start()
        token[...] = jnp.zeros_like(token)

    bufs = [b for b, _ in items]
    outs = pl.pallas_call(
        body, name=name,
        out_shape=tuple([pltpu.SemaphoreType.DMA(())] * (4 * n) + [pltpu.HBM(b.shape, b.dtype) for b in bufs]
                        + [jax.ShapeDtypeStruct((SUBLANES, LANES), _F32)]),
        in_specs=tuple([_HBM] * n + [_ANY]), out_specs=tuple([_SEM] * (4 * n) + [_HBM] * n + [_VMEM_WHOLE]),
        input_output_aliases={i: 4 * n + i for i in range(n)},
        compiler_params=pltpu.CompilerParams(has_side_effects=_EFFECT),
    )(*[_hbm(b) for b in bufs], after)
    sems = [(outs[2 * i:2 * i + 2], outs[2 * n + 2 * i:2 * n + 2 * i + 2]) for i in range(n)]
    return sems, list(outs[4 * n:5 * n]), outs[-1]


def _route_wait(sems, buf, phase, after, name):
    send, recv = sems

    def body(buf_ref, s0, s1, r0, r1, after_ref, out_ref):
        for (peer, src, land), s_k, r_k in zip(_route_ends(buf_ref, phase), (s0, s1), (r0, r1)):
            cp = pltpu.make_async_remote_copy(src_ref=src, dst_ref=land, send_sem=s_k, recv_sem=r_k,
                                              device_id=peer, device_id_type=_MESH)
            cp.wait_send()
            cp.wait_recv()

    return pl.pallas_call(
        body, name=name, out_shape=pltpu.HBM(buf.shape, buf.dtype),
        in_specs=(_HBM,) + (_SEM,) * 4 + (_ANY,), out_specs=_HBM, input_output_aliases={0: 0},
        compiler_params=pltpu.CompilerParams(has_side_effects=_EFFECT),
    )(buf, *send, *recv, after)


def _pair_forward(bufs, name):
    n = len(bufs)

    def body(*refs):
        ins, outs = refs[:n], refs[n:2 * n]
        send_sems, recv_sems = refs[2 * n:]
        mx, my, mc, chips = _place()
        sibling = (mx, my, 1 - mc)
        cps = []
        for i in range(n):
            for k, chip in enumerate(chips):
                j_k = 2 * chip[0] + chip[1]
                cp = pltpu.make_async_remote_copy(
                    src_ref=ins[i].at[j_k, mc], dst_ref=outs[i].at[j_k, mc], send_sem=send_sems.at[3 * i + k],
                    recv_sem=recv_sems.at[3 * i + k], device_id=sibling, device_id_type=_MESH)
                cp.start()
                cps.append(cp)
        for i in range(n):
            for k, chip in enumerate(chips):
                other = outs[i].at[2 * chip[0] + chip[1], 1 - mc]
                pltpu.make_async_remote_copy(
                    src_ref=other, dst_ref=other, send_sem=send_sems.at[3 * i + k], recv_sem=recv_sems.at[3 * i + k],
                    device_id=sibling, device_id_type=_MESH).wait_recv()
        for cp in cps:
            cp.wait_send()

    return pl.pallas_call(
        body, name=name, out_shape=[jax.ShapeDtypeStruct(b.shape, b.dtype) for b in bufs],
        in_specs=[_ANY] * n, out_specs=[_ANY] * n, input_output_aliases={i: i for i in range(n)},
        scratch_shapes=[pltpu.SemaphoreType.DMA((3 * n,)), pltpu.SemaphoreType.DMA((3 * n,))],
    )(*bufs)


def _gather8_peers(buf_ref, mx, my, mc, chips):
    mine = buf_ref.at[4 * mx + 2 * my + mc]
    peers = [((mx, my, 1 - mc), mine, buf_ref.at[4 * mx + 2 * my + 1 - mc])]
    peers += [((*chip, mc), mine, buf_ref.at[4 * chip[0] + 2 * chip[1] + mc]) for chip in chips]
    return peers


def _gather8_start(buf, name):
    def body(buf_ref, *rest):
        send, recv, token = rest[0:4], rest[4:8], rest[-1]
        mx, my, mc, chips = _place()
        for k, (peer, src, _) in enumerate(_gather8_peers(buf_ref, mx, my, mc, chips)):
            pltpu.make_async_remote_copy(src_ref=src, dst_ref=src, send_sem=send[k], recv_sem=recv[k],
                                         device_id=peer, device_id_type=_MESH).start()
        token[...] = jnp.zeros_like(token)

    outs = pl.pallas_call(
        body, name=name,
        out_shape=tuple([pltpu.SemaphoreType.DMA(())] * 8 + [pltpu.HBM(buf.shape, buf.dtype),
                                                             jax.ShapeDtypeStruct((SUBLANES, LANES), _F32)]),
        in_specs=(_HBM,), out_specs=tuple([_SEM] * 8 + [_HBM, _VMEM_WHOLE]), input_output_aliases={0: 8},
        compiler_params=pltpu.CompilerParams(has_side_effects=_EFFECT),
    )(_hbm(buf))
    return (outs[0:4], outs[4:8]), outs[8], outs[9]


def _gather8_wait(sems, buf, after, name):
    send, recv = sems

    def body(buf_ref, s0, s1, s2, s3, r0, r1, r2, r3, after_ref, out_ref):
        mx, my, mc, chips = _place()
        for (peer, src, dst), s_k, r_k in zip(_gather8_peers(buf_ref, mx, my, mc, chips), (s0, s1, s2, s3), (r0, r1, r2, r3)):
            cp = pltpu.make_async_remote_copy(src_ref=src, dst_ref=dst, send_sem=s_k, recv_sem=r_k,
                                              device_id=peer, device_id_type=_MESH)
            cp.wait_send()
            cp.wait_recv()

    return pl.pallas_call(
        body, name=name, out_shape=pltpu.HBM(buf.shape, buf.dtype),
        in_specs=(_HBM,) + (_SEM,) * 8 + (_ANY,), out_specs=_HBM, input_output_aliases={0: 0},
        compiler_params=pltpu.CompilerParams(has_side_effects=_EFFECT),
    )(buf, *send, *recv, after)


def _gather8_forward(buf, name):
    def body(in_ref, out_ref, send_sems, recv_sems):
        mx, my, mc, chips = _place()
        sibling = (mx, my, 1 - mc)
        cps = []
        for k, chip in enumerate(chips):
            idx = 4 * chip[0] + 2 * chip[1] + mc
            cp = pltpu.make_async_remote_copy(src_ref=in_ref.at[idx], dst_ref=out_ref.at[idx], send_sem=send_sems.at[k],
                                              recv_sem=recv_sems.at[k], device_id=sibling, device_id_type=_MESH)
            cp.start()
            cps.append(cp)
        for k, chip in enumerate(chips):
            other = out_ref.at[4 * chip[0] + 2 * chip[1] + 1 - mc]
            pltpu.make_async_remote_copy(src_ref=other, dst_ref=other, send_sem=send_sems.at[k], recv_sem=recv_sems.at[k],
                                         device_id=sibling, device_id_type=_MESH).wait_recv()
        for cp in cps:
            cp.wait_send()

    return pl.pallas_call(
        body, name=name, out_shape=jax.ShapeDtypeStruct(buf.shape, buf.dtype),
        in_specs=[_ANY], out_specs=_ANY, input_output_aliases={0: 0},
        scratch_shapes=[pltpu.SemaphoreType.DMA((3,)), pltpu.SemaphoreType.DMA((3,))],
    )(buf)


def _scatter_start(pair, name):
    land = lax.empty((3,) + pair.shape[1:], pair.dtype)

    def body(pair_ref, land_ref, s0, s1, s2, r0, r1, r2, pair_thru, land_thru, token):
        mx, my, mc, chips = _place()
        for k, (chip, s_k, r_k) in enumerate(zip(chips, (s0, s1, s2), (r0, r1, r2))):
            pltpu.make_async_remote_copy(
                src_ref=pair_ref.at[2 * chip[0] + chip[1]], dst_ref=land_ref.at[k], send_sem=s_k, recv_sem=r_k,
                device_id=(*chip, mc), device_id_type=_MESH).start()
        token[...] = jnp.zeros_like(token)

    outs = pl.pallas_call(
        body, name=name,
        out_shape=tuple([pltpu.SemaphoreType.DMA(())] * 6 + [pltpu.HBM(pair.shape, pair.dtype), pltpu.HBM(land.shape, land.dtype),
                                                             jax.ShapeDtypeStruct((SUBLANES, LANES), _F32)]),
        in_specs=(_HBM, _HBM), out_specs=tuple([_SEM] * 6 + [_HBM, _HBM, _VMEM_WHOLE]),
        input_output_aliases={0: 6, 1: 7}, compiler_params=pltpu.CompilerParams(has_side_effects=_EFFECT),
    )(_hbm(pair), _hbm(land))
    return (outs[0:3], outs[3:6]), outs[6], outs[7], outs[8]


def _scatter_wait(sems, pair, land, after, name):
    send, recv = sems

    def body(pair_ref, land_ref, s0, s1, s2, r0, r1, r2, after_ref, pair_out, land_out):
        mx, my, mc, chips = _place()
        for k, (chip, s_k, r_k) in enumerate(zip(chips, (s0, s1, s2), (r0, r1, r2))):
            cp = pltpu.make_async_remote_copy(
                src_ref=pair_ref.at[2 * chip[0] + chip[1]], dst_ref=land_ref.at[k], send_sem=s_k, recv_sem=r_k,
                device_id=(*chip, mc), device_id_type=_MESH)
            cp.wait_send()
            cp.wait_recv()

    return pl.pallas_call(
        body, name=name, out_shape=(pltpu.HBM(pair.shape, pair.dtype), pltpu.HBM(land.shape, land.dtype)),
        in_specs=(_HBM, _HBM) + (_SEM,) * 6 + (_ANY,), out_specs=(_HBM, _HBM), input_output_aliases={0: 0, 1: 1},
        compiler_params=pltpu.CompilerParams(has_side_effects=_EFFECT),
    )(pair, land, *send, *recv, after)


def _sibling_copy(src_ref, dst_ref, send_sem, recv_sem):
    mx, my, mc, _ = _place()
    return pltpu.make_async_remote_copy(src_ref=src_ref, dst_ref=dst_ref, send_sem=send_sem, recv_sem=recv_sem,
                                        device_id=(mx, my, 1 - mc), device_id_type=_MESH)


def _swap_start(g, name):
    land = lax.empty(g.shape[1:], g.dtype)

    def body(g_ref, land_ref, send_sem, recv_sem, g_thru, land_thru, token):
        _sibling_copy(g_ref.at[1 - lax.axis_index("c")], land_ref, send_sem, recv_sem).start()
        token[...] = jnp.zeros_like(token)

    outs = pl.pallas_call(
        body, name=name,
        out_shape=(pltpu.SemaphoreType.DMA(()), pltpu.SemaphoreType.DMA(()), pltpu.HBM(g.shape, g.dtype),
                   pltpu.HBM(land.shape, land.dtype), jax.ShapeDtypeStruct((SUBLANES, LANES), _F32)),
        in_specs=(_HBM, _HBM), out_specs=(_SEM, _SEM, _HBM, _HBM, _VMEM_WHOLE), input_output_aliases={0: 2, 1: 3},
        compiler_params=pltpu.CompilerParams(has_side_effects=_EFFECT),
    )(_hbm(g), _hbm(land))
    return (outs[0], outs[1]), outs[2], outs[3], outs[4]


def _swap_wait(sems, g, land, after, name):
    def body(g_ref, land_ref, send_sem, recv_sem, after_ref, g_out, land_out):
        cp = _sibling_copy(g_ref.at[1 - lax.axis_index("c")], land_ref, send_sem, recv_sem)
        cp.wait_send()
        cp.wait_recv()

    return pl.pallas_call(
        body, name=name, out_shape=(pltpu.HBM(g.shape, g.dtype), pltpu.HBM(land.shape, land.dtype)),
        in_specs=(_HBM, _HBM, _SEM, _SEM, _ANY), out_specs=(_HBM, _HBM), input_output_aliases={0: 0, 1: 1},
        compiler_params=pltpu.CompilerParams(has_side_effects=_EFFECT),
    )(g, land, *sems, after)


def _join_start(buf, name):
    def body(buf_ref, send_sem, recv_sem, buf_thru, token):
        mine = buf_ref.at[lax.axis_index("c")]
        _sibling_copy(mine, mine, send_sem, recv_sem).start()
        token[...] = jnp.zeros_like(token)

    outs = pl.pallas_call(
        body, name=name,
        out_shape=(pltpu.SemaphoreType.DMA(()), pltpu.SemaphoreType.DMA(()), pltpu.HBM(buf.shape, buf.dtype),
                   jax.ShapeDtypeStruct((SUBLANES, LANES), _F32)),
        in_specs=(_HBM,), out_specs=(_SEM, _SEM, _HBM, _VMEM_WHOLE), input_output_aliases={0: 2},
        compiler_params=pltpu.CompilerParams(has_side_effects=_EFFECT),
    )(_hbm(buf))
    return (outs[0], outs[1]), outs[2], outs[3]


def _join_wait(sems, buf, after, name):
    def body(buf_ref, send_sem, recv_sem, after_ref, buf_out):
        mc = lax.axis_index("c")
        cp = _sibling_copy(buf_ref.at[mc], buf_ref.at[1 - mc], send_sem, recv_sem)
        cp.wait_send()
        cp.wait_recv()

    return pl.pallas_call(
        body, name=name, out_shape=pltpu.HBM(buf.shape, buf.dtype),
        in_specs=(_HBM, _SEM, _SEM, _ANY), out_specs=_HBM, input_output_aliases={0: 0},
        compiler_params=pltpu.CompilerParams(has_side_effects=_EFFECT),
    )(buf, *sems, after)


def _pair_sum(g, got, sel, name):
    _, four, hr, c = g.shape
    tr = _tile(hr, 512, 16)

    def body(sel_ref, g_ref, p_ref, o_ref):
        o_ref[...] = (g_ref[...].astype(_F32) + p_ref[...].astype(_F32)).astype(o_ref.dtype)

    return pl.pallas_call(
        body, name=name, out_shape=jax.ShapeDtypeStruct((four, hr, c), g.dtype),
        grid_spec=pltpu.PrefetchScalarGridSpec(
            num_scalar_prefetch=1, grid=(four, hr // tr),
            in_specs=[pl.BlockSpec((None, None, tr, c), lambda j, i, s: (s[1], j, i, 0)),
                      pl.BlockSpec((None, tr, c), lambda j, i, s: (j, i, 0))],
            out_specs=pl.BlockSpec((None, tr, c), lambda j, i, s: (j, i, 0))),
        compiler_params=_cp("parallel", "parallel"),
    )(sel, g, got)


def _chip_sum(pair, got, sel, name):
    _, hr, c = pair.shape
    tr = _tile(hr, 512, 16)

    def body(sel_ref, p_ref, q_ref, o_ref):
        o_ref[...] = ((p_ref[...].astype(_F32) + q_ref[0].astype(_F32)) + q_ref[1].astype(_F32)) + q_ref[2].astype(_F32)

    return pl.pallas_call(
        body, name=name, out_shape=jax.ShapeDtypeStruct((2, hr, c), _F32),
        grid_spec=pltpu.PrefetchScalarGridSpec(
            num_scalar_prefetch=1, grid=(hr // tr,),
            in_specs=[pl.BlockSpec((None, tr, c), lambda i, s: (s[0], i, 0)),
                      pl.BlockSpec((3, tr, c), lambda i, s: (0, i, 0))],
            out_specs=pl.BlockSpec((None, tr, c), lambda i, s: (s[1], i, 0))),
        compiler_params=_cp("parallel"),
    )(sel, pair, got)


def _matmul(a, b, dims, out_struct, grid, a_spec, b_spec, o_spec, acc_shape, k_axis, name, after=None):
    nk = grid[k_axis]
    extra = [] if after is None else [after]

    def body(a_ref, b_ref, *rest):
        o_ref, acc = rest[len(extra)], rest[len(extra) + 1:]
        prod = lax.dot_general(a_ref[...].astype(_MXU), b_ref[...].astype(_MXU), dims, preferred_element_type=_F32)
        if nk == 1:
            o_ref[...] = prod.astype(o_ref.dtype)
        else:
            acc_ref, = acc
            k = pl.program_id(k_axis)
            _zero_first(k == 0, acc_ref)
            acc_ref[...] += prod

            @pl.when(k == nk - 1)
            def _():
                o_ref[...] = acc_ref[...].astype(o_ref.dtype)

    sem = ["parallel"] * len(grid)
    sem[k_axis] = "arbitrary"
    return pl.pallas_call(
        body, name=name, out_shape=out_struct, grid=grid, in_specs=[a_spec, b_spec] + [_ANY] * len(extra), out_specs=o_spec,
        scratch_shapes=[pltpu.VMEM(acc_shape, _F32)] if nk > 1 else [], compiler_params=_cp(*sem),
    )(a, b, *extra)


def _mm_nn(a, w4, out_dtype, name, tm=512, tn=1536, tk=2048, after=None):
    m, k = a.shape
    j, _, ns = w4.shape
    tm, tn, tk = _tile(m, tm, 16), _tile(ns, tn), _tile(k, tk)
    nps = ns // tn
    return _matmul(
        a, w4, (((1,), (0,)), ((), ())), jax.ShapeDtypeStruct((m, j * ns), out_dtype),
        (j * nps, m // tm, k // tk),
        pl.BlockSpec((tm, tk), lambda ni, mi, ki: (mi, ki)),
        pl.BlockSpec((None, tk, tn), lambda ni, mi, ki: (ni // nps, ki, ni % nps)),
        pl.BlockSpec((tm, tn), lambda ni, mi, ki: (mi, ni)), (tm, tn), 2, name, after)


def _mm_nt(a, w4, out_dtype, name, tm=512, tn=2048, tk=1536, after=None):
    m = a.shape[-2]
    j, kw, ns = w4.shape
    tm, tn, tk = _tile(m, tm, 16), _tile(kw, tn), _tile(ns, tk)
    kps = ns // tk
    if a.ndim == 3:
        kph = a.shape[2] // tk
        a_spec = pl.BlockSpec((None, tm, tk), lambda ni, mi, ki: (ki // kph, mi, ki % kph))
    else:
        a_spec = pl.BlockSpec((tm, tk), lambda ni, mi, ki: (mi, ki))
    return _matmul(
        a, w4, (((1,), (1,)), ((), ())), jax.ShapeDtypeStruct((m, kw), out_dtype),
        (kw // tn, m // tm, j * kps),
        a_spec,
        pl.BlockSpec((None, tn, tk), lambda ni, mi, ki: (ki // kps, ni, ki % kps)),
        pl.BlockSpec((tm, tn), lambda ni, mi, ki: (mi, ni)), (tm, tn), 2, name, after)


def _mm_tn_cols(a, b, name, tm=1024, tn=1536, tk=2048):
    m, ka = a.shape
    ns = (b.shape[-1] * (2 if b.ndim == 3 else 1)) // 4
    hr = ka // 2
    tm, tn, tk = _tile(hr, tm), _tile(ns, tn), _tile(m, tk, 16)
    mph, nps = hr // tm, ns // tn
    if b.ndim == 3:
        b_spec = pl.BlockSpec((None, tk, tn), lambda ni, mi, ki: (ni // (2 * nps), ki, ni % (2 * nps)))
    else:
        b_spec = pl.BlockSpec((tk, tn), lambda ni, mi, ki: (ki, ni))
    return _matmul(
        a, b, (((0,), (0,)), ((), ())), jax.ShapeDtypeStruct((2, 4, hr, ns), _WIRE),
        (4 * nps, 2 * mph, m // tk),
        pl.BlockSpec((tk, tm), lambda ni, mi, ki: (ki, mi)),
        b_spec,
        pl.BlockSpec((None, None, tm, tn), lambda ni, mi, ki: (mi // mph, ni // nps, mi % mph, ni % nps)),
        (tm, tn), 2, name)


def _mm_tn_rows(a, b, name, tm=1536, tn=1024, tk=2048):
    m, ka = a.shape
    r = ka // 4
    hc = b.shape[1] // 2
    tm, tn, tk = _tile(r, tm), _tile(hc, tn), _tile(m, tk, 16)
    mpr, nph = r // tm, hc // tn
    return _matmul(
        a, b, (((0,), (0,)), ((), ())), jax.ShapeDtypeStruct((2, 4, r, hc), _WIRE),
        (2 * nph, 4 * mpr, m // tk),
        pl.BlockSpec((tk, tm), lambda ni, mi, ki: (ki, mi)),
        pl.BlockSpec((tk, tn), lambda ni, mi, ki: (ki, ni)),
        pl.BlockSpec((None, None, tm, tn), lambda ni, mi, ki: (ni // nph, mi // mpr, mi % mpr, ni % nph)),
        (tm, tn), 2, name)


def _row_call(body, name, rows, ins, outs, tm=256):
    tm = _tile(rows, tm, 16)

    def spec(shape, kind):
        if kind == "rows":
            return pl.BlockSpec((tm, shape[1]), lambda i: (i, 0))
        return pl.BlockSpec(shape, lambda i: (0,) * len(shape))

    return pl.pallas_call(
        body, name=name, grid=(rows // tm,),
        in_specs=[spec(a.shape, kind) for a, kind in ins],
        out_specs=[spec(o.shape, kind) for o, kind in outs],
        out_shape=[o for o, _ in outs],
        compiler_params=_cp("arbitrary"),
    )(*[a for a, _ in ins])


def _rms(x):
    r = lax.rsqrt(_rowmean(x * x) + EPS)
    return x * r, r


def _rms_bwd(dxh, xh, r):
    return r * (dxh - xh * _rowmean(dxh * xh))


def _fwd_pre_mix(x, g, sc, sh):
    s, d = x.shape

    def body(x_ref, g_ref, sc_ref, sh_ref, h_ref):
        xh, _ = _rms(x_ref[...])
        h_ref[...] = (xh * g_ref[...] * (1.0 + sc_ref[...]) + sh_ref[...]).astype(h_ref.dtype)

    return _row_call(body, "fwd_pre_mix", s, [(x, "rows"), (g, "vec"), (sc, "vec"), (sh, "vec")],
                     [(jax.ShapeDtypeStruct((s, d), _MXU), "rows")])[0]


def _fwd_mid(o, x, gt1, g_post, g_pre2, sc2, sh2):
    s, d = x.shape

    def body(o_ref, x_ref, gt_ref, gp_ref, g2_ref, sc_ref, sh_ref, x1_ref, h2_ref):
        oh, _ = _rms(o_ref[...])
        x1 = x_ref[...] + gt_ref[...] * (oh * gp_ref[...])
        x1_ref[...] = x1
        xh, _ = _rms(x1)
        h2_ref[...] = (xh * g2_ref[...] * (1.0 + sc_ref[...]) + sh_ref[...]).astype(h2_ref.dtype)

    return _row_call(body, "fwd_mid", s,
                     [(o, "rows"), (x, "rows"), (gt1, "vec"), (g_post, "vec"), (g_pre2, "vec"), (sc2, "vec"),
                      (sh2, "vec")],
                     [(jax.ShapeDtypeStruct((s, d), _F32), "rows"), (jax.ShapeDtypeStruct((s, d), _MXU), "rows")])


def _loss_and_post_ffn_bwd(f, x1, tgt, gt2, g_post):
    s, d = x1.shape

    def body(f_ref, x1_ref, t_ref, gt_ref, g_ref, dx2_ref, df_ref, dgt_ref, dg_ref, loss_ref):
        first = pl.program_id(0) == 0
        _zero_first(first, dgt_ref, dg_ref, loss_ref)
        fh, r = _rms(f_ref[...])
        n = fh * g_ref[...]
        e = x1_ref[...] + gt_ref[...] * n - t_ref[...]
        _acc(loss_ref, first, jnp.sum(_colsum(e * e), axis=1, keepdims=True) * (0.5 / d))
        dx2 = e * (1.0 / d)
        dx2_ref[...] = dx2
        _acc(dgt_ref, first, _colsum(dx2 * n))
        dn = dx2 * gt_ref[...]
        _acc(dg_ref, first, _colsum(dn * fh))
        df_ref[...] = _rms_bwd(dn * g_ref[...], fh, r).astype(df_ref.dtype)

    vec = jax.ShapeDtypeStruct((1, d), _F32)
    return _row_call(body, "loss_post_ffn_bwd", s,
                     [(f, "rows"), (x1, "rows"), (tgt, "rows"), (gt2, "vec"), (g_post, "vec")],
                     [(jax.ShapeDtypeStruct((s, d), _F32), "rows"), (jax.ShapeDtypeStruct((s, d), _MXU), "rows"),
                      (vec, "vec"), (vec, "vec"), (jax.ShapeDtypeStruct((1, 1), _F32), "vec")])


def _bwd_mid(dh2, x1, dx2, o, g_pre2, sc2, gt1, g_post):
    s, d = x1.shape

    def body(dh_ref, x1_ref, dx2_ref, o_ref, g2_ref, sc_ref, gt_ref, gp_ref,
             dx1_ref, do_ref, dsc_ref, dsh_ref, dg2_ref, dgt_ref, dgp_ref):
        first = pl.program_id(0) == 0
        _zero_first(first, dsc_ref, dsh_ref, dg2_ref, dgt_ref, dgp_ref)
        dh = dh_ref[...]
        xh, r = _rms(x1_ref[...])
        _acc(dsh_ref, first, _colsum(dh))
        _acc(dsc_ref, first, _colsum(dh * (xh * g2_ref[...])))
        dn = dh * (1.0 + sc_ref[...])
        _acc(dg2_ref, first, _colsum(dn * xh))
        dx1 = dx2_ref[...] + _rms_bwd(dn * g2_ref[...], xh, r)
        dx1_ref[...] = dx1
        oh, ro = _rms(o_ref[...])
        _acc(dgt_ref, first, _colsum(dx1 * (oh * gp_ref[...])))
        dno = dx1 * gt_ref[...]
        _acc(dgp_ref, first, _colsum(dno * oh))
        do_ref[...] = _rms_bwd(dno * gp_ref[...], oh, ro).astype(do_ref.dtype)

    vec = jax.ShapeDtypeStruct((1, d), _F32)
    return _row_call(body, "bwd_mid", s,
                     [(dh2, "rows"), (x1, "rows"), (dx2, "rows"), (o, "rows"), (g_pre2, "vec"), (sc2, "vec"),
                      (gt1, "vec"), (g_post, "vec")],
                     [(jax.ShapeDtypeStruct((s, d), _F32), "rows"), (jax.ShapeDtypeStruct((s, d), _MXU), "rows"),
                      (vec, "vec"), (vec, "vec"), (vec, "vec"), (vec, "vec"), (vec, "vec")])


def _bwd_pre_mix(dh1, x, dx1, g, sc1):
    s, d = x.shape

    def body(dh_ref, x_ref, dx1_ref, g_ref, sc_ref, dx_ref, dsc_ref, dsh_ref, dg_ref):
        first = pl.program_id(0) == 0
        _zero_first(first, dsc_ref, dsh_ref, dg_ref)
        dh = dh_ref[...]
        xh, r = _rms(x_ref[...])
        _acc(dsh_ref, first, _colsum(dh))
        _acc(dsc_ref, first, _colsum(dh * (xh * g_ref[...])))
        dn = dh * (1.0 + sc_ref[...])
        _acc(dg_ref, first, _colsum(dn * xh))
        dx_ref[...] = dx1_ref[...] + _rms_bwd(dn * g_ref[...], xh, r)

    vec = jax.ShapeDtypeStruct((1, d), _F32)
    return _row_call(body, "bwd_pre_mix", s,
                     [(dh1, "rows"), (x, "rows"), (dx1, "rows"), (g, "vec"), (sc1, "vec")],
                     [(jax.ShapeDtypeStruct((s, d), _F32), "rows"), (vec, "vec"), (vec, "vec"), (vec, "vec")])


def _mix_norm_fwd(y_ssm, y_sgu, g_ssm, g_sgu):
    s, h = y_ssm.shape

    def body(a_ref, b_ref, ga_ref, gb_ref, o_ref):
        ah, _ = _rms(a_ref[...])
        bh, _ = _rms(b_ref[...])
        o_ref[:, 0:h] = (ah * ga_ref[...]).astype(o_ref.dtype)
        o_ref[:, h:2 * h] = (bh * gb_ref[...]).astype(o_ref.dtype)

    return _row_call(body, "mix_norm_fwd", s, [(y_ssm, "rows"), (y_sgu, "rows"), (g_ssm, "vec"), (g_sgu, "vec")],
                     [(jax.ShapeDtypeStruct((s, 2 * h), _MXU), "rows")])[0]


def _mix_norm_bwd(dyc, y_ssm, y_sgu, g_ssm, g_sgu):
    s, h = y_ssm.shape

    def body(d_ref, a_ref, b_ref, ga_ref, gb_ref, da_ref, db_ref, dga_ref, dgb_ref):
        first = pl.program_id(0) == 0
        _zero_first(first, dga_ref, dgb_ref)
        for lo, y_ref, g_ref, dy_ref, dg_ref in ((0, a_ref, ga_ref, da_ref, dga_ref), (h, b_ref, gb_ref, db_ref, dgb_ref)):
            d = d_ref[:, lo:lo + h]
            yh, r = _rms(y_ref[...])
            _acc(dg_ref, first, _colsum(d * yh))
            dy_ref[...] = _rms_bwd(d * g_ref[...], yh, r)

    vec = jax.ShapeDtypeStruct((1, h), _F32)
    full = jax.ShapeDtypeStruct((s, h), _F32)
    return _row_call(body, "mix_norm_bwd", s,
                     [(dyc, "rows"), (y_ssm, "rows"), (y_sgu, "rows"), (g_ssm, "vec"), (g_sgu, "vec")],
                     [(full, "rows"), (full, "rows"), (vec, "vec"), (vec, "vec")])


CONV_ROWS = 64


def _conv_rows(ext, w_ref, b_ref):
    x = ext[SUBLANES:]
    s1 = pltpu.roll(ext, 1, 0)[SUBLANES:]
    s2 = pltpu.roll(ext, 2, 0)[SUBLANES:]
    return b_ref[...] + w_ref[0:1, :] * s2 + w_ref[1:2, :] * s1 + w_ref[2:3, :] * x, x, s1, s2


def _conv_window(x_ref, r0):
    if isinstance(r0, int):
        assert r0 == 0
        return jnp.concatenate([jnp.zeros((SUBLANES, x_ref.shape[1]), _F32), x_ref[0:CONV_ROWS, :]], axis=0)
    return x_ref[pl.ds(pl.multiple_of(r0 - SUBLANES, SUBLANES), CONV_ROWS + SUBLANES), :]


def _conv_act_fwd(up_pre, conv_w, conv_b):
    s, f2 = up_pre.shape
    f = f2 // 2
    tc = _tile(f, 256)
    nf = f // tc

    def shift_down(x, k):
        row = lax.broadcasted_iota(jnp.int32, x.shape, 0)
        return jnp.where(row >= k, pltpu.roll(x, k, 0), 0.0)

    def conv(x, w_ref, b_ref):
        return b_ref[...] + w_ref[0:1, :] * shift_down(x, 2) + w_ref[1:2, :] * shift_down(x, 1) + w_ref[2:3, :] * x

    def body(a_ref, b_ref, wa_ref, wb_ref, ba_ref, bb_ref, o_ref):
        a = conv(a_ref[...], wa_ref, ba_ref)
        b = conv(b_ref[...], wb_ref, bb_ref)
        o_ref[...] = (a * _sigmoid(a) * b).astype(o_ref.dtype)

    return pl.pallas_call(
        body, name="conv_act_fwd", grid=(nf,), out_shape=jax.ShapeDtypeStruct((s, f), _MXU),
        in_specs=[pl.BlockSpec((s, tc), lambda n: (0, n)), pl.BlockSpec((s, tc), lambda n: (0, n + nf)),
                  pl.BlockSpec((3, tc), lambda n: (0, n)), pl.BlockSpec((3, tc), lambda n: (0, n + nf)),
                  pl.BlockSpec((1, tc), lambda n: (0, n)), pl.BlockSpec((1, tc), lambda n: (0, n + nf))],
        out_specs=pl.BlockSpec((s, tc), lambda n: (0, n)), compiler_params=_cp("parallel"),
    )(up_pre, up_pre, conv_w, conv_w, conv_b, conv_b)


def _conv_act_bwd(up_pre, d_act, conv_w, conv_b):
    s, f2 = up_pre.shape
    f = f2 // 2
    tc = _tile(f, 256)
    nf = f // tc

    def body(a_ref, b_ref, d_ref, wa_ref, wb_ref, ba_ref, bb_ref,
             du_ref, w0a, w0b, w1a, w1b, w2a, w2b, dba, dbb):
        n = s // CONV_ROWS
        zero8 = jnp.zeros((SUBLANES, tc), _F32)
        ext_rows = CONV_ROWS + SUBLANES

        def fold(x):
            out = x[0:SUBLANES]
            for k in range(1, CONV_ROWS // SUBLANES):
                out = out + x[k * SUBLANES:(k + 1) * SUBLANES]
            return out

        def chunk(r0, carry):
            nxt, acc = carry
            a, xa, xa1, xa2 = _conv_rows(_conv_window(a_ref, r0), wa_ref, ba_ref)
            b, xb, xb1, xb2 = _conv_rows(_conv_window(b_ref, r0), wb_ref, bb_ref)
            sg = _sigmoid(a)
            d = d_ref[pl.ds(r0, CONV_ROWS), :]
            du_a = d * b * (sg * (1.0 + a * (1.0 - sg)))
            du_b = d * (a * sg)
            new_acc = []
            for h, (du, x0, x1, x2, w_ref) in enumerate(((du_a, xa, xa1, xa2, wa_ref), (du_b, xb, xb1, xb2, wb_ref))):
                ext = jnp.concatenate([du, nxt[h]], axis=0)
                u1 = pltpu.roll(ext, ext_rows - 1, 0)[:CONV_ROWS]
                u2 = pltpu.roll(ext, ext_rows - 2, 0)[:CONV_ROWS]
                du_ref[h, pl.ds(r0, CONV_ROWS), :] = (w_ref[2:3, :] * du + w_ref[1:2, :] * u1
                                                      + w_ref[0:1, :] * u2).astype(du_ref.dtype)
                new_acc += [acc[4 * h] + fold(du * x2), acc[4 * h + 1] + fold(du * x1), acc[4 * h + 2] + fold(du * x0),
                            acc[4 * h + 3] + fold(du)]
            return (du_a[:SUBLANES], du_b[:SUBLANES]), tuple(new_acc)

        def step(i, carry):
            return chunk(pl.multiple_of((n - 1 - i) * CONV_ROWS, CONV_ROWS), carry)

        carry = lax.fori_loop(0, n - 1, step, ((zero8, zero8), (zero8,) * 8))
        _, acc = chunk(0, carry)
        for ref, val in zip((w0a, w1a, w2a, dba, w0b, w1b, w2b, dbb), acc):
            ref[...] = _colsum(val)

    col_a = pl.BlockSpec((s, tc), lambda n: (0, n))
    col_b = pl.BlockSpec((s, tc), lambda n: (0, n + nf))
    vec_a = pl.BlockSpec((1, tc), lambda n: (0, n))
    vec_b = pl.BlockSpec((1, tc), lambda n: (0, n + nf))
    vec = jax.ShapeDtypeStruct((1, f), _F32)
    outs = pl.pallas_call(
        body, name="conv_act_bwd", grid=(nf,),
        in_specs=[col_a, col_b, col_a, pl.BlockSpec((3, tc), lambda n: (0, n)),
                  pl.BlockSpec((3, tc), lambda n: (0, n + nf)), vec_a, vec_b],
        out_specs=[pl.BlockSpec((2, s, tc), lambda n: (0, 0, n))] + [vec_a] * 8,
        out_shape=[jax.ShapeDtypeStruct((2, s, f), _MXU)] + [vec] * 8, compiler_params=_cp("parallel"),
    )(up_pre, up_pre, d_act, conv_w, conv_w, conv_b, conv_b)
    du, w0a, w0b, w1a, w1b, w2a, w2b, dba, dbb = outs
    cat = lambda p, q: jnp.concatenate([p, q], axis=1)
    return du, cat(w0a, w0b), cat(w1a, w1b), cat(w2a, w2b), cat(dba, dbb)


def _sgu_recompute(zu_ref, zv_ref, lng_ref, lnb_ref, wm_ref, bs_ref, nh):
    zu, zv = zu_ref[...], zv_ref[...]
    u = _gelu(zu)
    gv = _gelu(zv)
    xc = gv - _rowmean(gv)
    rs = lax.rsqrt(_rowmean(xc * xc) + EPS)
    vh = xc * rs
    v = vh * lng_ref[...] + lnb_ref[...]
    mixed = []
    for h in range(nh):
        vhd = v[:, h * CHUNK:(h + 1) * CHUNK].astype(_MXU)
        mixed.append(jnp.dot(wm_ref[h].astype(_MXU), vhd, preferred_element_type=_F32) + bs_ref[h])
    return zu, zv, u, vh, rs, v, mixed


def _sgu_fwd(z, ln_g, ln_b, wm, bs):
    s = z.shape[0]
    nh = wm.shape[0]
    hd = nh * CHUNK

    def body(zu_ref, zv_ref, lng_ref, lnb_ref, wm_ref, bs_ref, y_ref):
        _, _, u, _, _, _, mixed = _sgu_recompute(zu_ref, zv_ref, lng_ref, lnb_ref, wm_ref, bs_ref, nh)
        for h in range(nh):
            y_ref[:, h * CHUNK:(h + 1) * CHUNK] = u[:, h * CHUNK:(h + 1) * CHUNK] * mixed[h]

    vec = pl.BlockSpec((1, hd), lambda i: (0, 0))
    return pl.pallas_call(
        body, name="sgu_fwd", grid=(s // CHUNK,), out_shape=jax.ShapeDtypeStruct((s, hd), _F32),
        in_specs=[pl.BlockSpec((CHUNK, hd), lambda i: (i, 1)), pl.BlockSpec((CHUNK, hd), lambda i: (i, 2)), vec, vec,
                  pl.BlockSpec((nh, CHUNK, CHUNK), lambda i: (0, 0, 0)), pl.BlockSpec((nh, CHUNK, 1), lambda i: (0, 0, 0))],
        out_specs=pl.BlockSpec((CHUNK, hd), lambda i: (i, 0)), compiler_params=_cp("parallel"),
    )(z, z, ln_g, ln_b, wm, bs)


def _sgu_bwd(z, dy, dz_ssm, ln_g, ln_b, wm, bs):
    s = z.shape[0]
    nh = wm.shape[0]
    hd = nh * CHUNK

    def body(zu_ref, zv_ref, dy_ref, dzs_ref, lng_ref, lnb_ref, wm_ref, bs_ref,
             dz_ref, dlg_ref, dlb_ref, dwm_ref, dbs_ref, dv_scr):
        first = pl.program_id(0) == 0
        _zero_first(first, dlg_ref, dlb_ref, dwm_ref, dbs_ref)
        zu, zv, u, vh, rs, v, mixed = _sgu_recompute(zu_ref, zv_ref, lng_ref, lnb_ref, wm_ref, bs_ref, nh)
        dy = dy_ref[...]
        dz_ref[:, 0:hd] = dzs_ref[...].astype(dz_ref.dtype)
        for h in range(nh):
            cols = slice(h * CHUNK, (h + 1) * CHUNK)
            dyh = dy[:, cols]
            dz_ref[:, hd + h * CHUNK:hd + (h + 1) * CHUNK] = (dyh * mixed[h] * _gelu_grad(zu[:, cols])).astype(dz_ref.dtype)
            dm = dyh * u[:, cols]
            dmx = dm.astype(_MXU)
            _acc(dbs_ref.at[h], first, jnp.sum(dm, axis=1, keepdims=True))
            _acc(dwm_ref.at[h], first,
                 lax.dot_general(dmx, v[:, cols].astype(_MXU), (((1,), (1,)), ((), ())), preferred_element_type=_F32))
            dv_scr[:, cols] = lax.dot_general(wm_ref[h].astype(_MXU), dmx, (((0,), (0,)), ((), ())),
                                              preferred_element_type=_F32)
        dv = dv_scr[...]
        _acc(dlg_ref, first, _colsum(dv * vh))
        _acc(dlb_ref, first, _colsum(dv))
        dvh = dv * lng_ref[...]
        dgv = rs * (dvh - _rowmean(dvh) - vh * _rowmean(dvh * vh))
        dz_ref[:, 2 * hd:3 * hd] = (dgv * _gelu_grad(zv)).astype(dz_ref.dtype)

    vec = pl.BlockSpec((1, hd), lambda i: (0, 0))
    wspec = pl.BlockSpec((nh, CHUNK, CHUNK), lambda i: (0, 0, 0))
    bspec = pl.BlockSpec((nh, CHUNK, 1), lambda i: (0, 0, 0))
    rows = pl.BlockSpec((CHUNK, hd), lambda i: (i, 0))
    return pl.pallas_call(
        body, name="sgu_bwd", grid=(s // CHUNK,),
        out_shape=[jax.ShapeDtypeStruct((s, 3 * hd), _MXU), jax.ShapeDtypeStruct((1, hd), _F32),
                   jax.ShapeDtypeStruct((1, hd), _F32), jax.ShapeDtypeStruct((nh, CHUNK, CHUNK), _F32),
                   jax.ShapeDtypeStruct((nh, CHUNK, 1), _F32)],
        in_specs=[pl.BlockSpec((CHUNK, hd), lambda i: (i, 1)), pl.BlockSpec((CHUNK, hd), lambda i: (i, 2)),
                  rows, rows, vec, vec, wspec, bspec],
        out_specs=[pl.BlockSpec((CHUNK, 3 * hd), lambda i: (i, 0)), vec, vec, wspec, bspec],
        scratch_shapes=[pltpu.VMEM((CHUNK, hd), _F32)], compiler_params=_cp("arbitrary"),
    )(z, z, dy, dz_ssm, ln_g, ln_b, wm, bs)


def _ssm_prep(log_dt, a_re, a_im, b_re_t, b_im_t, kvec):
    gn = a_re.shape[1]

    def body(ldt_ref, are_ref, aim_ref, br_ref, bi_ref, k_ref, pr_ref, pi_ref, bbr_ref, bbi_ref):
        dt = jnp.exp(ldt_ref[...])
        are, aim = are_ref[...], aim_ref[...]
        k = k_ref[...]
        mag = jnp.exp(k * (are * dt))
        ang = k * (aim * dt)
        pr_ref[...] = mag * jnp.cos(ang)
        pi_ref[...] = mag * jnp.sin(ang)
        m1 = jnp.exp(are * dt)
        lr, li = m1 * jnp.cos(aim * dt), m1 * jnp.sin(aim * dt)
        den = are * are + aim * aim
        nr = lr - 1.0
        f_re = (nr * are + li * aim) / den
        f_im = (li * are - nr * aim) / den
        bbr_ref[...] = f_re * br_ref[...] - f_im * bi_ref[...]
        bbi_ref[...] = f_re * bi_ref[...] + f_im * br_ref[...]

    pw = jax.ShapeDtypeStruct((kvec.shape[0], gn), _F32)
    bb = jax.ShapeDtypeStruct(b_re_t.shape, _F32)
    return pl.pallas_call(body, name="ssm_prep", out_shape=[pw, pw, bb, bb])(log_dt, a_re, a_im, b_re_t, b_im_t, kvec)


def _ssm_prep_bwd(log_dt, a_re, a_im, b_re_t, b_im_t, d_bbr, d_bbi, d_lr, d_li):
    def body(ldt_ref, are_ref, aim_ref, br_ref, bi_ref, dbr_ref, dbi_ref, dlr_ref, dli_ref,
             obr_ref, obi_ref, oar_ref, oai_ref, odt_ref):
        dt = jnp.exp(ldt_ref[...])
        are, aim = are_ref[...], aim_ref[...]
        m1 = jnp.exp(are * dt)
        lr, li = m1 * jnp.cos(aim * dt), m1 * jnp.sin(aim * dt)
        den = are * are + aim * aim
        nr = lr - 1.0
        f_re = (nr * are + li * aim) / den
        f_im = (li * are - nr * aim) / den
        br, bi, dbr, dbi = br_ref[...], bi_ref[...], dbr_ref[...], dbi_ref[...]
        obr_ref[...] = f_re * dbr + f_im * dbi
        obi_ref[...] = f_re * dbi - f_im * dbr
        gf_re = _colsum(br * dbr + bi * dbi)
        gf_im = _colsum(br * dbi - bi * dbr)
        il_re, il_im = are / den, -aim / den
        glb_re = dlr_ref[...] + (il_re * gf_re + il_im * gf_im)
        glb_im = dli_ref[...] + (il_re * gf_im - il_im * gf_re)
        q_re = -(f_re * il_re - f_im * il_im)
        q_im = -(f_re * il_im + f_im * il_re)
        gl_re = q_re * gf_re + q_im * gf_im
        gl_im = q_re * gf_im - q_im * gf_re
        gl_re = gl_re + dt * (lr * glb_re + li * glb_im)
        gl_im = gl_im + dt * (lr * glb_im - li * glb_re)
        w_re = are * lr - aim * li
        w_im = are * li + aim * lr
        oar_ref[...] = gl_re
        oai_ref[...] = gl_im
        odt_ref[...] = w_re * glb_re + w_im * glb_im

    bb = jax.ShapeDtypeStruct(b_re_t.shape, _F32)
    v = jax.ShapeDtypeStruct(a_re.shape, _F32)
    return pl.pallas_call(body, name="ssm_prep_bwd", out_shape=[bb, bb, v, v, v])(
        log_dt, a_re, a_im, b_re_t, b_im_t, d_bbr, d_bbi, d_lr, d_li)


def _group_sum(d_dt, log_dt):
    def body(d_ref, l_ref, o_ref):
        o_ref[...] = jnp.sum(d_ref[...], axis=1, keepdims=True) * jnp.exp(l_ref[...])

    return pl.pallas_call(body, name="ssm_dt_grad", out_shape=jax.ShapeDtypeStruct(log_dt.shape, _F32))(d_dt, log_dt)


def _load_strided(ref, nr):
    return jnp.concatenate([ref[pl.ds(r, SUBLANES, stride=nr), :] for r in range(nr)], axis=0)


def _store_strided(ref, val, nr):
    for r in range(nr):
        ref[pl.ds(r, SUBLANES, stride=nr), :] = val[r * SUBLANES:(r + 1) * SUBLANES]


def _scan_strided(src_ref, dst_ref, nr, p_ref, carry, reverse, h_ref=None, h_in=None):
    ns = BLOCK_ST
    row = lax.broadcasted_iota(jnp.int32, (SUBLANES, ns), 0)
    bc = lambda v: jnp.broadcast_to(v, (SUBLANES, ns))
    tile = lambda ref, r: (ref[r * SUBLANES:(r + 1) * SUBLANES, 0:ns], ref[r * SUBLANES:(r + 1) * SUBLANES, ns:2 * ns])
    one = nr - 1 if reverse else 0
    ar, ai = bc(p_ref[one:one + 1, 0:ns]), bc(p_ref[one:one + 1, ns:2 * ns])
    xr = xi = None
    for r in (range(nr - 1, -1, -1) if reverse else range(nr)):
        sr, si = tile(src_ref, r)
        xr, xi = (sr, si) if xr is None else (ar * xr - ai * xi + sr, ar * xi + ai * xr + si)
        dst_ref[r * SUBLANES:(r + 1) * SUBLANES, 0:ns] = xr
        dst_ref[r * SUBLANES:(r + 1) * SUBLANES, ns:2 * ns] = xi
    edge, shift = (SUBLANES - 1, SUBLANES - 1) if reverse else (0, 1)
    dr = jnp.where(row == edge, carry[0], pltpu.roll(xr, shift, 0))
    di = jnp.where(row == edge, carry[1], pltpu.roll(xi, shift, 0))
    for i, k in enumerate((1, 2, 4)):
        qr, qi = bc(p_ref[nr + i:nr + i + 1, 0:ns]), bc(p_ref[nr + i:nr + i + 1, ns:2 * ns])
        keep = (row < SUBLANES - k) if reverse else (row >= k)
        sr = jnp.where(keep, pltpu.roll(dr, (SUBLANES - k) if reverse else k, 0), 0.0)
        si = jnp.where(keep, pltpu.roll(di, (SUBLANES - k) if reverse else k, 0), 0.0)
        dr, di = dr + qr * sr - qi * si, di + qr * si + qi * sr
    acc_r = acc_i = jnp.zeros((SUBLANES, ns), _F32)
    out = None
    for r in range(nr):
        wr, wi = p_ref[r:r + 1, 0:ns], p_ref[r:r + 1, ns:2 * ns]
        xr, xi = tile(dst_ref, r)
        xr, xi = xr + wr * dr - wi * di, xi + wr * di + wi * dr
        dst_ref[r * SUBLANES:(r + 1) * SUBLANES, 0:ns] = xr
        dst_ref[r * SUBLANES:(r + 1) * SUBLANES, ns:2 * ns] = xi
        if h_ref is not None:
            if r == 0:
                lr, li = tile(h_ref, nr - 1)
                pr, pi = jnp.where(row == 0, h_in[0], pltpu.roll(lr, 1, 0)), jnp.where(row == 0, h_in[1], pltpu.roll(li, 1, 0))
            else:
                pr, pi = tile(h_ref, r - 1)
            acc_r = acc_r + (xr * pr + xi * pi)
            acc_i = acc_i + (xi * pr - xr * pi)
        if r == (0 if reverse else nr - 1):
            out = (xr[0:1, :], xi[0:1, :]) if reverse else (xr[SUBLANES - 1:SUBLANES, :], xi[SUBLANES - 1:SUBLANES, :])
    if h_ref is None:
        return out
    return out, (_colsum(acc_r), _colsum(acc_i))


def _ssm_gate(y, wg_ref, bg_ref):
    yg = _gelu(y)
    gate = _sigmoid(jnp.dot(yg.astype(_MXU), wg_ref[...].astype(_MXU), preferred_element_type=_F32) + bg_ref[...])
    return yg, gate


def _ssm_specs(nb, nt, t, reverse):
    tt = (lambda ti: nt - 1 - ti) if reverse else (lambda ti: ti)
    ns2 = 2 * BLOCK_ST
    return dict(
        z=pl.BlockSpec((t, BLOCK_CH), lambda b, ti: (tt(ti), b)),
        bbt=pl.BlockSpec((None, BLOCK_CH, ns2), lambda b, ti: (b, 0, 0)),
        ct=pl.BlockSpec((None, ns2, BLOCK_CH), lambda b, ti: (b, 0, 0)),
        vec=pl.BlockSpec((1, BLOCK_CH), lambda b, ti: (0, b)),
        wg=pl.BlockSpec((None, BLOCK_CH, BLOCK_CH), lambda b, ti: (b, 0, 0)),
        p=pl.BlockSpec((None, t // SUBLANES + SUBLANES, ns2), lambda b, ti: (b, 0, 0)),
        hb=pl.BlockSpec((None, None, SUBLANES, ns2), lambda b, ti: (b, tt(ti), 0, 0)),
        h=pl.BlockSpec((None, t, ns2), lambda b, ti: (b, tt(ti), 0)),
        acc_vec=pl.BlockSpec((None, 1, ns2), lambda b, ti: (b, 0, 0)),
    )


def _ssm_fwd(z, bbt, ct, dvec, wg, bglu, ptab):
    s = z.shape[0]
    nb = bbt.shape[0]
    t = _tile(s, TIME_TILE, SUBLANES)
    nt = s // t
    ns = BLOCK_ST
    sp = _ssm_specs(nb, nt, t, False)

    nr = t // SUBLANES

    def body(z_ref, bbt_ref, ct_ref, d_ref, wg_ref, bg_ref, p_ref, y2_ref, y_ref, h_ref, hb_ref, bu_scr, h_scr, carry_scr):
        _zero_first(pl.program_id(1) == 0, carry_scr)
        hb_ref[...] = carry_scr[...]
        carry_in = (carry_scr[0:1, 0:ns], carry_scr[0:1, ns:2 * ns])
        u = _load_strided(z_ref, nr)
        bu_scr[...] = jnp.dot(u.astype(_MXU), bbt_ref[...].astype(_MXU), preferred_element_type=_F32)
        cr, ci = _scan_strided(bu_scr, h_scr, nr, p_ref, carry_in, False)
        hx = h_scr[...].astype(_MXU)
        h_ref[...] = hx
        y = jnp.dot(hx, ct_ref[...].astype(_MXU), preferred_element_type=_F32) + d_ref[...] * u
        yg, gate = _ssm_gate(y, wg_ref, bg_ref)
        _store_strided(y2_ref, yg * gate, nr)
        _store_strided(y_ref, y, nr)
        carry_scr[:, 0:ns] = jnp.broadcast_to(cr, (SUBLANES, ns))
        carry_scr[:, ns:2 * ns] = jnp.broadcast_to(ci, (SUBLANES, ns))

    ych = jax.ShapeDtypeStruct((s, nb * BLOCK_CH), _F32)
    return pl.pallas_call(
        body, name="ssm_fwd", grid=(nb, nt),
        out_shape=[ych, ych, jax.ShapeDtypeStruct((nb, s, 2 * ns), _MXU),
                   jax.ShapeDtypeStruct((nb, nt, SUBLANES, 2 * ns), _F32)],
        in_specs=[sp["z"], sp["bbt"], sp["ct"], sp["vec"], sp["wg"], sp["vec"], sp["p"]],
        out_specs=[sp["z"], sp["z"], sp["h"], sp["hb"]],
        scratch_shapes=[pltpu.VMEM((t, 2 * ns), _F32), pltpu.VMEM((t, 2 * ns), _F32), pltpu.VMEM((SUBLANES, 2 * ns), _F32)],
        compiler_params=_cp("parallel", "arbitrary"),
    )(z, bbt, ct, dvec, wg, bglu, ptab)


def _ssm_bwd(z, y_pre, h_all, dy2, hb, bbt, ct, dvec, wg, bglu, ptab_rev):
    s = z.shape[0]
    nb = bbt.shape[0]
    t = _tile(s, TIME_TILE, SUBLANES)
    nt = s // t
    ns = BLOCK_ST
    sp = _ssm_specs(nb, nt, t, True)
    tn_dims = (((0,), (0,)), ((), ()))
    nt_dims = (((1,), (1,)), ((), ()))

    nr = t // SUBLANES

    def body(z_ref, y_ref, h_ref, dy2_ref, hb_ref, bbt_ref, ct_ref, d_ref, wg_ref, bg_ref, pr_ref,
             dz_ref, dbbt_ref, dct_ref, dwg_ref, dlb_ref, dd_ref, dbg_ref, bu_scr, g_scr, h_scr, gcarry_scr):
        first = pl.program_id(1) == 0

        _zero_first(first, gcarry_scr, dbbt_ref, dct_ref, dwg_ref, dlb_ref, dd_ref, dbg_ref)
        u = _load_strided(z_ref, nr)
        hin = hb_ref[...]
        y = _load_strided(y_ref, nr)
        yg, gate = _ssm_gate(y, wg_ref, bg_ref)
        dy2 = _load_strided(dy2_ref, nr)
        dpre = dy2 * yg * gate * (1.0 - gate)
        _acc(dbg_ref, first, _colsum(dpre))
        dpx = dpre.astype(_MXU)
        _acc(dwg_ref, first, lax.dot_general(yg.astype(_MXU), dpx, tn_dims, preferred_element_type=_F32))
        dyg = dy2 * gate + lax.dot_general(dpx, wg_ref[...].astype(_MXU), nt_dims, preferred_element_type=_F32)
        dy = dyg * _gelu_grad(y)
        _acc(dd_ref, first, _colsum(dy * u))
        dyx = dy.astype(_MXU)
        hx = h_ref[...]
        h_scr[...] = hx.astype(_F32)
        _acc(dct_ref, first, lax.dot_general(hx, dyx, tn_dims, preferred_element_type=_F32))
        bu_scr[...] = lax.dot_general(dyx, ct_ref[...].astype(_MXU), nt_dims, preferred_element_type=_F32)
        gin = (gcarry_scr[0:1, 0:ns], gcarry_scr[0:1, ns:2 * ns])
        (gr, gi), (d_ar, d_ai) = _scan_strided(bu_scr, g_scr, nr, pr_ref, gin, True, h_scr,
                                               (hin[0:1, 0:ns], hin[0:1, ns:2 * ns]))
        gcarry_scr[:, 0:ns] = jnp.broadcast_to(gr, (SUBLANES, ns))
        gcarry_scr[:, ns:2 * ns] = jnp.broadcast_to(gi, (SUBLANES, ns))
        _acc(dlb_ref, first, jnp.concatenate([d_ar, d_ai], axis=1))
        gx = g_scr[...].astype(_MXU)
        _acc(dbbt_ref, first, lax.dot_general(u.astype(_MXU), gx, tn_dims, preferred_element_type=_F32))
        _store_strided(dz_ref, dy * d_ref[...] + lax.dot_general(gx, bbt_ref[...].astype(_MXU), nt_dims,
                                                                 preferred_element_type=_F32), nr)

    f = lambda shape: jax.ShapeDtypeStruct(shape, _F32)
    return pl.pallas_call(
        body, name="ssm_bwd", grid=(nb, nt),
        out_shape=[f((s, nb * BLOCK_CH)), f(bbt.shape), f(ct.shape), f(wg.shape), f((nb, 1, 2 * ns)),
                   f((1, nb * BLOCK_CH)), f((1, nb * BLOCK_CH))],
        in_specs=[sp["z"], sp["z"], sp["h"], sp["z"], sp["hb"], sp["bbt"], sp["ct"], sp["vec"], sp["wg"], sp["vec"], sp["p"]],
        out_specs=[sp["z"], sp["bbt"], sp["ct"], sp["wg"], sp["acc_vec"], sp["vec"], sp["vec"]],
        scratch_shapes=[pltpu.VMEM((t, 2 * ns), _F32), pltpu.VMEM((t, 2 * ns), _F32), pltpu.VMEM((t, 2 * ns), _F32),
                        pltpu.VMEM((SUBLANES, 2 * ns), _F32)],
        compiler_params=_cp("parallel", "arbitrary"),
    )(z, y_pre, h_all, dy2, hb, bbt, ct, dvec, wg, bglu, ptab_rev)


def _mod_part(c_all, w, b):
    d, ns = w.shape
    tn = _tile(ns, 512)

    def body(c_ref, w_ref, b_ref, o_ref):
        c = c_ref[...]
        ca = (c * _sigmoid(c)).astype(_MXU)
        o_ref[...] = jnp.dot(ca, w_ref[...].astype(_MXU), preferred_element_type=_F32) + b_ref[...]

    return pl.pallas_call(
        body, name="mod_part", grid=(ns // tn,), out_shape=jax.ShapeDtypeStruct((8, ns), _F32),
        in_specs=[pl.BlockSpec((8, d), lambda n: (0, 0)), pl.BlockSpec((d, tn), lambda n: (0, n)),
                  pl.BlockSpec((1, tn), lambda n: (0, n))],
        out_specs=pl.BlockSpec((8, tn), lambda n: (0, n)), compiler_params=_cp("parallel"),
    )(c_all, w, b)


def _adamw_math(w, g, m, v):
    m = ADAM_B1 * m + (1.0 - ADAM_B1) * g
    v = ADAM_B2 * v + (1.0 - ADAM_B2) * (g * g)
    m_hat = m / (1.0 - ADAM_B1 ** ADAM_STEP)
    v_hat = v / (1.0 - ADAM_B2 ** ADAM_STEP)
    delta = -ADAM_LR * (m_hat / (jnp.sqrt(v_hat) + ADAM_EPS) + ADAM_WD * w)
    return delta, m, v


def _adamw(w, g, m, v, name):
    r, c = w.shape
    tc = c if c <= 4096 else _tile(c, 4096)
    tr = _tile(r, max(SUBLANES, (1 << 18) // tc), SUBLANES)

    def body(w_ref, g_ref, m_ref, v_ref, go_ref, d_ref, mo_ref, vo_ref):
        g = g_ref[...]
        go_ref[...] = g
        d_ref[...], mo_ref[...], vo_ref[...] = _adamw_math(w_ref[...], g, m_ref[...], v_ref[...])

    spec = pl.BlockSpec((tr, tc), lambda i, j: (i, j))
    out = jax.ShapeDtypeStruct((r, c), _F32)
    return pl.pallas_call(
        body, name=name, grid=(r // tr, c // tc), in_specs=[spec] * 4, out_specs=[spec] * 4, out_shape=[out] * 4,
        compiler_params=_cp("parallel", "parallel"),
    )(w, g, m, v)


def _adamw_halves(w, g2, m, v, name):
    r, c = w.shape
    tr, tc = _tile(r, 256, SUBLANES), _tile(c // 2, 1024)
    nph = (c // 2) // tc

    def body(w_ref, g_ref, m_ref, v_ref, go_ref, d_ref, mo_ref, vo_ref):
        g = g_ref[...]
        go_ref[...] = g
        d_ref[...], mo_ref[...], vo_ref[...] = _adamw_math(w_ref[...], g, m_ref[...], v_ref[...])

    spec = pl.BlockSpec((tr, tc), lambda i, j: (i, j))
    out = jax.ShapeDtypeStruct((r, c), _F32)
    return pl.pallas_call(
        body, name=name, grid=(r // tr, c // tc),
        in_specs=[spec, pl.BlockSpec((None, tr, tc), lambda i, j: (j // nph, i, j % nph)), spec, spec],
        out_specs=[spec] * 4, out_shape=[out] * 4, compiler_params=_cp("parallel", "parallel"),
    )(w, g2, m, v)


def _wada_update(c_t, dm, w, m, v):
    d, ns = w.shape
    tr, tc = _tile(d, 256, SUBLANES), _tile(ns, 1024)

    def body(c_ref, dm_ref, w_ref, m_ref, v_ref, g_ref, d_ref, mo_ref, vo_ref):
        c = c_ref[...]
        ca = c * _sigmoid(c)
        dmv = dm_ref[...]
        g = ca[:, 0:1] * dmv[0:1, :]
        for b in range(1, 8):
            g = g + ca[:, b:b + 1] * dmv[b:b + 1, :]
        g_ref[...] = g
        d_ref[...], mo_ref[...], vo_ref[...] = _adamw_math(w_ref[...], g, m_ref[...], v_ref[...])

    spec = pl.BlockSpec((tr, tc), lambda i, j: (i, j))
    out = jax.ShapeDtypeStruct((d, ns), _F32)
    return pl.pallas_call(
        body, name="wada_update", grid=(d // tr, ns // tc),
        in_specs=[pl.BlockSpec((tr, 8), lambda i, j: (i, 0)), pl.BlockSpec((8, tc), lambda i, j: (0, j)), spec, spec, spec],
        out_specs=[spec] * 4, out_shape=[out] * 4, compiler_params=_cp("parallel", "parallel"),
    )(c_t, dm, w, m, v)


def _small_reduce(gathered):
    _, r, c = gathered.shape
    tr = _tile(r, 512, SUBLANES)

    def body(q_ref, g_ref):
        g = q_ref[0]
        for k in range(1, 8):
            g = g + q_ref[k]
        g_ref[...] = g

    return pl.pallas_call(
        body, name="small_reduce", grid=(r // tr,), out_shape=jax.ShapeDtypeStruct((r, c), _F32),
        in_specs=[pl.BlockSpec((8, tr, c), lambda i: (0, i, 0))], out_specs=pl.BlockSpec((tr, c), lambda i: (i, 0)),
        compiler_params=_cp("parallel"),
    )(gathered)


def _adamw_many(ws, gs, ms, vs, steps, name):
    n = len(ws)

    def body(*refs):
        w_refs, g_refs, m_refs, v_refs = refs[0:n], refs[n:2 * n], refs[2 * n:3 * n], refs[3 * n:4 * n]
        d_refs, mo_refs, vo_refs = refs[4 * n:5 * n], refs[5 * n:6 * n], refs[6 * n:7 * n]
        for i in range(n):
            d_refs[i][...], mo_refs[i][...], vo_refs[i][...] = _adamw_math(
                w_refs[i][...], g_refs[i][...], m_refs[i][...], v_refs[i][...])

    def spec(a):
        nd = a.ndim
        if steps == 1:
            return pl.BlockSpec(a.shape, lambda i: (0,) * nd)
        return pl.BlockSpec((a.shape[0] // steps,) + a.shape[1:], lambda i: (i,) + (0,) * (nd - 1))

    specs = [spec(w) for w in ws]
    outs = pl.pallas_call(
        body, name=name, grid=(steps,), in_specs=specs * 4, out_specs=specs * 3,
        out_shape=[jax.ShapeDtypeStruct(w.shape, _F32) for w in ws] * 3, compiler_params=_cp("parallel"),
    )(*ws, *gs, *ms, *vs)
    return outs[0:n], outs[n:2 * n], outs[2 * n:3 * n]


def _block_diag(x, eye=None):
    nb, g, p, q = x.shape
    eye = jnp.eye(g, dtype=x.dtype) if eye is None else eye
    return (x[:, :, :, None, :] * eye[None, :, None, :, None]).reshape(nb, g * p, g * q)


def _block_diag_take(x, p, q):
    nb = x.shape[0]
    g = GROUPS_PER_BLOCK
    eye = jnp.eye(g, dtype=x.dtype)
    return jnp.sum(x.reshape(nb, g, p, g, q) * eye[None, :, None, :, None], axis=3)


_VIEWS = {"ssm_b_re": ((0, 2, 1), (0, 2, 1)), "ssm_b_im": ((0, 2, 1), (0, 2, 1)),
          "ssm_w_glu": ((1, 2, 0), (2, 0, 1)), "ssm_b_glu": ((1, 0), (1, 0))}


def _to_view(name, a):
    return a.transpose(_VIEWS[name][0]) if name in _VIEWS else a


def _from_view(name, a):
    return a.transpose(_VIEWS[name][1]) if name in _VIEWS else a


class _Pack:
    def __init__(self, shapes):
        self.shapes = shapes
        self.offsets = {}
        off = 0
        for name, shape in shapes.items():
            n = math.prod(shape)
            self.offsets[name] = (off, n)
            off += -(-n // (SUBLANES * LANES)) * (SUBLANES * LANES)
        self.rows = -(-off // (256 * LANES)) * 256

    def pack(self, arrays):
        parts = []
        off = 0
        for name, shape in self.shapes.items():
            start, n = self.offsets[name]
            if start > off:
                parts.append(jnp.zeros((start - off,), _F32))
            parts.append(arrays[name].reshape(-1).astype(_F32))
            off = start + n
        total = self.rows * LANES
        if total > off:
            parts.append(jnp.zeros((total - off,), _F32))
        return jnp.concatenate(parts).reshape(self.rows, LANES)

    def unpack(self, buf):
        flat = buf.reshape(-1)
        return {name: flat[start:start + n].reshape(self.shapes[name]) for name, (start, n) in self.offsets.items()}


_SMALL = ["b_ada", "g_pre_mix", "g_post_mix", "ssm_log_dt", "ssm_a_re", "ssm_a_im", "ssm_b_re", "ssm_b_im", "ssm_c_re",
          "ssm_c_im", "ssm_d", "ssm_w_glu", "ssm_b_glu", "sgu_ln_g", "sgu_ln_b", "sgu_w", "sgu_b", "g_out_ssm",
          "g_out_sgu", "g_pre_ffn", "g_post_ffn", "conv_b"]
_WEIGHTS = ["w_ada", "b_ada", "g_pre_mix", "g_post_mix", "w_in", "ssm_log_dt", "ssm_a_re", "ssm_a_im", "ssm_b_re",
            "ssm_b_im", "ssm_c_re", "ssm_c_im", "ssm_d", "ssm_w_glu", "ssm_b_glu", "sgu_ln_g", "sgu_ln_b", "sgu_w", "sgu_b",
            "g_out_ssm", "g_out_sgu", "w_out", "g_pre_ffn", "g_post_ffn", "w_up", "conv_w", "conv_b", "w_down"]


def _step(p, m, v, x, c, tgt):
    s, d = x.shape
    mx, my, mc = lax.axis_index("x"), lax.axis_index("y"), lax.axis_index("c")
    chip = 2 * mx + my
    dev = 4 * mx + 2 * my + mc
    sel = jnp.stack([chip, mc]).astype(jnp.int32)
    g_cnt, n_st = p["ssm_a_re"].shape
    nb = g_cnt // GROUPS_PER_BLOCK
    gn = g_cnt * n_st
    d_ssm = g_cnt * SSM_GROUP
    nh = p["sgu_w"].shape[0]
    assert nh * CHUNK == d_ssm and 2 * d_ssm == d and n_st == SSM_STATE

    shards = lambda g: g.reshape(4, g.shape[1] * g.shape[2], g.shape[3])
    buf_in = _cast_into_slot(p["w_in"], sel, sel, "cast_w_in")

    ns_ada = p["w_ada"].shape[1]
    nc_conv = p["conv_w"].shape[1]
    first = jnp.concatenate([jnp.broadcast_to(c, (8, d)), jnp.pad(p["conv_w"], ((0, 5), (0, 0)))], axis=1)
    first_all = _all_gather8(_own_slot(first, dev), "gather_c_conv", after=buf_in)
    (sems_in,), (buf_in,), tok = _gather_start([buf_in], first_all, "gather_start_in")
    c_all = _after(first_all[:, 0, :d], tok)
    conv_w_full = jnp.concatenate([first_all[2 * j, 0:3, d:] for j in range(4)], axis=1)
    b_ada_mine = lax.dynamic_slice_in_dim(p["b_ada"], chip * ns_ada, ns_ada, axis=1)
    mod_mine = _mod_part(c_all, p["w_ada"], b_ada_mine)
    buf_out, buf_up, buf_down = [_cast_into_slot(p[n], sel, tok, "cast_" + n) for n in ("w_out", "w_up", "w_down")]

    eye_t = jnp.eye(GROUPS_PER_BLOCK, dtype=_F32) + tok[0:1, 0:1]
    ldt_l = _after(jnp.repeat(p["ssm_log_dt"], n_st, axis=1), tok)
    are_l, aim_l = p["ssm_a_re"].reshape(1, gn), p["ssm_a_im"].reshape(1, gn)
    bre_t, bim_t = p["ssm_b_re"].reshape(gn, SSM_GROUP).T, p["ssm_b_im"].reshape(gn, SSM_GROUP).T
    nr = _tile(s, TIME_TILE, SUBLANES) // SUBLANES
    kvec = jnp.concatenate([jnp.arange(1, nr + 1, dtype=_F32), jnp.array([nr, 2 * nr, 4 * nr, 0, 0, 0, 0, 0], _F32)])
    pw_re, pw_im, bb_re, bb_im = _ssm_prep(ldt_l, are_l, aim_l, bre_t, bim_t, kvec.reshape(nr + SUBLANES, 1))
    blocks = lambda t: t.reshape(t.shape[0], nb, GROUPS_PER_BLOCK * n_st).transpose(1, 0, 2)
    ptab = jnp.concatenate([blocks(pw_re), blocks(pw_im)], axis=2)
    rev = lambda t: jnp.concatenate([t[:, :nr][:, ::-1], t[:, nr:]], axis=1)
    ptab_rev = jnp.concatenate([rev(blocks(pw_re)), -rev(blocks(pw_im))], axis=2)
    bd = lambda t: t.reshape(SSM_GROUP, nb, GROUPS_PER_BLOCK, n_st).transpose(1, 2, 0, 3)
    bbt = jnp.concatenate([_block_diag(bd(bb_re)), _block_diag(bd(bb_im))], axis=2).astype(_MXU)
    cd = lambda t: t.reshape(nb, GROUPS_PER_BLOCK, SSM_GROUP, n_st).transpose(0, 1, 3, 2)
    ct = jnp.concatenate([_block_diag(cd(p["ssm_c_re"]), eye_t), -_block_diag(cd(p["ssm_c_im"]), eye_t)], axis=1).astype(_MXU)
    wg = _block_diag(p["ssm_w_glu"].reshape(nb, GROUPS_PER_BLOCK, SSM_GROUP, SSM_GROUP), eye_t).astype(_MXU)
    dvec = p["ssm_d"]
    bglu = p["ssm_b_glu"].reshape(1, d_ssm)
    mask = jnp.tril(jnp.ones((CHUNK, CHUNK), _F32)) + tok[0:1, 0:1]
    wm = (p["sgu_w"] * mask[None]).astype(_MXU)
    bs = p["sgu_b"].reshape(nh, CHUNK, 1)

    mod_all = _all_gather8(_own_slot(mod_mine, dev), "gather_mod",
                           after=[buf_out, buf_up, buf_down, bbt, ct, wg, wm, bs, ptab, ptab_rev])
    (sems_out, sems_up), (buf_out, buf_up), tok_rest = _route_start([(buf_out, 1), (buf_up, 1)], mod_all, "route_start_a")
    mod_rows = lax.dynamic_index_in_dim(mod_all, dev, axis=1, keepdims=False)
    mod = jnp.concatenate([mod_rows[0], mod_rows[2], mod_rows[4], mod_rows[6]]).reshape(N_MOD, 1, d)
    sh1, sc1, gt1, sh2, sc2, gt2 = [mod[i] for i in range(N_MOD)]

    h1 = _fwd_pre_mix(x, p["g_pre_mix"], _after(sc1, tok_rest), sh1)
    buf_in = _gather_wait(sems_in, buf_in, h1, "gather_wait_in")
    w_in4 = shards(_pair_forward([buf_in], "pair_forward_in")[0])
    z = _mm_nn(h1, w_in4, _F32, "mm_in")
    y_ssm, y_pre, h_all, hb = _ssm_fwd(z, bbt, ct, dvec, wg, bglu, ptab)
    y_sgu = _sgu_fwd(z, p["sgu_ln_g"], p["sgu_ln_b"], wm, bs)
    buf_out = _route_wait(sems_out, buf_out, 1, y_sgu, "route_wait_out_1")
    buf_up = _route_wait(sems_up, buf_up, 1, y_ssm, "route_wait_up_1")
    (sems_out, sems_up, sems_down), (buf_out, buf_up, buf_down), tok = _route_start(
        [(buf_out, 2), (buf_up, 2), (buf_down, 1)], y_sgu, "route_start_b")
    ycat = _mix_norm_fwd(y_ssm, y_sgu, _after(p["g_out_ssm"], tok), p["g_out_sgu"])
    buf_out = _route_wait(sems_out, buf_out, 2, ycat, "route_wait_out_2")
    w_out_full = _pair_forward([buf_out], "pair_forward_out")[0].reshape(1, d, d)
    o = _mm_nn(ycat, w_out_full, _F32, "mm_out")
    x1, h2 = _fwd_mid(o, x, gt1, p["g_post_mix"], p["g_pre_ffn"], sc2, sh2)
    buf_up = _route_wait(sems_up, buf_up, 2, h2, "route_wait_up_2")
    w_up4 = shards(_pair_forward([buf_up], "pair_forward_up")[0])
    up_pre = _mm_nn(h2, w_up4, _F32, "mm_up")
    buf_down = _route_wait(sems_down, buf_down, 1, up_pre, "route_wait_down_1")
    (sems_down,), (buf_down,), tok = _route_start([(buf_down, 2)], up_pre, "route_start_c")
    act = _conv_act_fwd(up_pre, conv_w_full, _after(p["conv_b"], tok))
    buf_down = _route_wait(sems_down, buf_down, 2, act, "route_wait_down_2")
    w_down_full = _pair_forward([buf_down], "pair_forward_down")[0].reshape(1, -1, d)
    f = _mm_nn(act, w_down_full, _F32, "mm_down", tk=5632)
    dx2, df, d_gt2, d_g_post_ffn, loss = _loss_and_post_ffn_bwd(f, x1, tgt, gt2, p["g_post_ffn"])

    def reduce_next(swap, n, after):
        sems, gw, land, _ = swap
        gw, got = _swap_wait(sems, gw, land, after, "swap_wait_" + n)
        return _scatter_start(_pair_sum(gw, got, sel, "pair_sum_" + n), "scatter_start_" + n)

    d_act = _mm_nt(df, w_down_full, _F32, "mm_d_act", tk=2048)
    swap_down = _swap_start(_mm_tn_rows(act, df, "mm_gw_down"), "swap_start_w_down")
    d_up_pre, d_cw0, d_cw1, d_cw2, d_conv_b = _conv_act_bwd(up_pre, d_act, conv_w_full, _after(p["conv_b"], swap_down[3]))
    red_down = reduce_next(swap_down, "w_down", d_conv_b)
    dh2 = _mm_nt(d_up_pre, w_up4, _F32, "mm_dh2", tk=2816, after=red_down[3])
    swap_up = _swap_start(_mm_tn_cols(h2, d_up_pre, "mm_gw_up"), "swap_start_w_up")
    dx1, d_o, d_sc2, d_sh2, d_g_pre_ffn, d_gt1, d_g_post_mix = _bwd_mid(
        dh2, x1, dx2, o, p["g_pre_ffn"], _after(sc2, swap_up[3]), gt1, p["g_post_mix"])
    red_up = reduce_next(swap_up, "w_up", d_g_post_mix)
    d_ycat = _mm_nt(d_o, w_out_full, _F32, "mm_d_ycat", tn=1024, tk=2048, after=red_up[3])
    swap_out = _swap_start(_mm_tn_rows(ycat, d_o, "mm_gw_out"), "swap_start_w_out")
    dy_ssm, dy_sgu, d_g_out_ssm, d_g_out_sgu = _mix_norm_bwd(
        d_ycat, y_ssm, y_sgu, _after(p["g_out_ssm"], swap_out[3]), p["g_out_sgu"])
    red_out = reduce_next(swap_out, "w_out", d_g_out_sgu)
    dz_ssm, d_bbt, d_ct, d_wg, d_lb, d_ssm_d, d_bglu = _ssm_bwd(z, y_pre, h_all, dy_ssm, hb, bbt, ct,
                                                                _after(dvec, red_out[3]), wg, bglu, ptab_rev)
    dz, d_ln_g, d_ln_b, d_wm, d_bs = _sgu_bwd(z, dy_sgu, dz_ssm, p["sgu_ln_g"], p["sgu_ln_b"], wm, bs)
    dh1 = _mm_nt(dz, w_in4, _F32, "mm_dh1")
    swap_in = _swap_start(_mm_tn_cols(h1, dz, "mm_gw_in"), "swap_start_w_in")
    dx, d_sc1, d_sh1, d_g_pre_mix = _bwd_pre_mix(dh1, x, dx1, p["g_pre_mix"], _after(sc1, swap_in[3]))
    red_in = reduce_next(swap_in, "w_in", d_g_pre_mix)

    nsb = BLOCK_ST
    lanes = lambda t: t.transpose(2, 0, 1, 3).reshape(SSM_GROUP, gn)
    d_bbr = lanes(_block_diag_take(d_bbt[:, :, :nsb], SSM_GROUP, n_st))
    d_bbi = lanes(_block_diag_take(d_bbt[:, :, nsb:], SSM_GROUP, n_st))
    d_lr, d_li = d_lb[:, 0, :nsb].reshape(1, gn), d_lb[:, 0, nsb:].reshape(1, gn)
    d_bre_t, d_bim_t, d_are, d_aim, d_dt = _ssm_prep_bwd(ldt_l, are_l, aim_l, bre_t, bim_t, d_bbr, d_bbi, d_lr, d_li)
    d_log_dt = _group_sum(d_dt.reshape(g_cnt, n_st), p["ssm_log_dt"].reshape(g_cnt, 1))
    c_grad = lambda t: _block_diag_take(t, n_st, SSM_GROUP).transpose(0, 1, 3, 2).reshape(g_cnt, SSM_GROUP, n_st)
    small = {
        "b_ada": jnp.concatenate([d_sh1, _after(d_sc1, red_in[3]), d_gt1, d_sh2, d_sc2, d_gt2], axis=1),
        "g_pre_mix": d_g_pre_mix, "g_post_mix": d_g_post_mix,
        "ssm_log_dt": d_log_dt, "ssm_a_re": d_are, "ssm_a_im": d_aim,
        "ssm_b_re": d_bre_t.T, "ssm_b_im": d_bim_t.T,
        "ssm_c_re": c_grad(d_ct[:, :nsb, :]), "ssm_c_im": -c_grad(d_ct[:, nsb:, :]),
        "ssm_d": d_ssm_d, "ssm_w_glu": _block_diag_take(d_wg, SSM_GROUP, SSM_GROUP), "ssm_b_glu": d_bglu,
        "sgu_ln_g": d_ln_g, "sgu_ln_b": d_ln_b, "sgu_w": d_wm * mask[None], "sgu_b": d_bs,
        "g_out_ssm": d_g_out_ssm, "g_out_sgu": d_g_out_sgu, "g_pre_ffn": d_g_pre_ffn, "g_post_ffn": d_g_post_ffn,
        "conv_b": d_conv_b, "conv_w_all": jnp.concatenate([d_cw0, d_cw1, d_cw2], axis=0),
        "loss_sum": loss,
    }
    small = {n: _to_view(n, a.reshape(p[n].shape)) if n in p else a for n, a in small.items()}
    pk = _Pack({n: a.shape for n, a in small.items()})
    sems_small, small_buf, tok = _gather8_start(_own_slot(pk.pack(small), dev), "gather_small_start")

    big = ["w_down", "w_up", "w_out", "w_in"]
    joins = []
    after = tok
    for n, (sems, pair, land, _) in zip(big, (red_down, red_up, red_out, red_in)):
        pair, land = _scatter_wait(sems, pair, land, after, "scatter_wait_" + n)
        sems_j, half, after = _join_start(_chip_sum(pair, land, sel, "chip_sum_" + n), "join_start_" + n)
        joins.append((sems_j, half))
    big_out = {}
    for n, (sems_j, half) in zip(big, joins):
        j = _join_wait(sems_j, half, after, "join_wait_" + n)
        if n in ("w_in", "w_up"):
            big_out[n] = tuple(_adamw(p[n], j.reshape(p[n].shape), m[n], v[n], "adamw_" + n))
        else:
            big_out[n] = tuple(_adamw_halves(p[n], j, m[n], v[n], "adamw_" + n))
        after = big_out[n][1]

    gathered = _gather8_forward(_gather8_wait(sems_small, small_buf, after, "gather_small_wait"),
                                "gather_small_forward")
    gview = pk.unpack(_small_reduce(gathered))
    gview["conv_w"] = lax.dynamic_slice_in_dim(gview.pop("conv_w_all"), chip * nc_conv, nc_conv, axis=1)
    loss = gview.pop("loss_sum")
    small_names = _SMALL + ["conv_w"]
    per_group = [n for n in small_names if gview[n].ndim >= 2 and gview[n].shape[0] == g_cnt]
    others = [n for n in small_names if n not in per_group]
    grads = {n: _from_view(n, gview[n]) for n in small_names}
    deltas, new_m, new_v = {}, {}, {}
    for names, steps, call in ((per_group, g_cnt // GROUPS_PER_BLOCK, "adamw_s5"), (others, 1, "adamw_small")):
        res = _adamw_many([_to_view(n, p[n]) for n in names], [gview[n] for n in names],
                          [_to_view(n, m[n]) for n in names], [_to_view(n, v[n]) for n in names], steps, call)
        for n, dl, mo, vo in zip(names, *res):
            deltas[n], new_m[n], new_v[n] = _from_view(n, dl), _from_view(n, mo), _from_view(n, vo)

    d_mod_all = gathered.reshape(8, -1)[:, :N_MOD * d]
    d_mod_mine = lax.dynamic_slice_in_dim(d_mod_all, chip * ns_ada, ns_ada, axis=1)
    grads["w_ada"], deltas["w_ada"], new_m["w_ada"], new_v["w_ada"] = _wada_update(
        c_all.T, d_mod_mine, p["w_ada"], m["w_ada"], v["w_ada"])
    for n in big:
        grads[n], deltas[n], new_m[n], new_v[n] = big_out[n]
    return loss[0, 0], dx, grads, deltas, new_m, new_v


def kernel(x, c, w_ada, b_ada, g_pre_mix, g_post_mix, w_in, ssm_log_dt, ssm_a_re, ssm_a_im, ssm_b_re, ssm_b_im, ssm_c_re, ssm_c_im, ssm_d, ssm_w_glu, ssm_b_glu, sgu_ln_g, sgu_ln_b, sgu_w, sgu_b, g_out_ssm, g_out_sgu, w_out, g_pre_ffn, g_post_ffn, w_up, conv_w, conv_b, w_down, loss_target, m_w_ada, m_b_ada, m_g_pre_mix, m_g_post_mix, m_w_in, m_ssm_log_dt, m_ssm_a_re, m_ssm_a_im, m_ssm_b_re, m_ssm_b_im, m_ssm_c_re, m_ssm_c_im, m_ssm_d, m_ssm_w_glu, m_ssm_b_glu, m_sgu_ln_g, m_sgu_ln_b, m_sgu_w, m_sgu_b, m_g_out_ssm, m_g_out_sgu, m_w_out, m_g_pre_ffn, m_g_post_ffn, m_w_up, m_conv_w, m_conv_b, m_w_down, v_w_ada, v_b_ada, v_g_pre_mix, v_g_post_mix, v_w_in, v_ssm_log_dt, v_ssm_a_re, v_ssm_a_im, v_ssm_b_re, v_ssm_b_im, v_ssm_c_re, v_ssm_c_im, v_ssm_d, v_ssm_w_glu, v_ssm_b_glu, v_sgu_ln_g, v_sgu_ln_b, v_sgu_w, v_sgu_b, v_g_out_ssm, v_g_out_sgu, v_w_out, v_g_pre_ffn, v_g_post_ffn, v_w_up, v_conv_w, v_conv_b, v_w_down):
    given = dict(locals())
    drop = lambda a: a if a.ndim == 2 else a[0]
    p = {n: drop(given[n]) for n in _WEIGHTS}
    m = {n: drop(given["m_" + n]) for n in _WEIGHTS}
    v = {n: drop(given["v_" + n]) for n in _WEIGHTS}
    loss, dx, grads, deltas, new_m, new_v = _step(p, m, v, x[0], c, loss_target[0])
    outs = [loss, dx[None]]
    for group in (grads, deltas, new_m, new_v):
        outs += [group[n].reshape(given[n].shape) for n in _WEIGHTS]
    return tuple(outs)
```

```python
import functools
import math

import jax
import jax.numpy as jnp
from jax import lax
from jax.experimental import pallas as pl
from jax.experimental.pallas import tpu as pltpu

_F32 = jnp.float32
_MXU = jnp.bfloat16
_WIRE = jnp.bfloat16

EPS = 1e-6
SSM_GROUP = 16
SSM_STATE = 64
GROUPS_PER_BLOCK = 8
BLOCK_CH = SSM_GROUP * GROUPS_PER_BLOCK
BLOCK_ST = SSM_STATE * GROUPS_PER_BLOCK
CHUNK = 128
TIME_TILE = 512
SUBLANES = 8
LANES = 128
N_MOD = 6
ADAM_LR, ADAM_B1, ADAM_B2, ADAM_EPS, ADAM_WD, ADAM_STEP = 0.001, 0.9, 0.999, 1e-08, 0.01, 10
_VMEM_LIMIT = 56 * 1024 * 1024
_MESH = pl.DeviceIdType.MESH
_ANY = pl.BlockSpec(memory_space=pl.ANY)
_HBM = pl.BlockSpec(memory_space=pltpu.HBM)
_SEM = pl.BlockSpec(memory_space=pltpu.SEMAPHORE)
_VMEM_WHOLE = pl.BlockSpec(memory_space=pltpu.VMEM)
_EFFECT = pltpu.SideEffectType.DATAFLOW_SIDE_EFFECTING
_GELU_C = math.sqrt(2.0 / math.pi)


def _cp(*sem):
    return pltpu.CompilerParams(dimension_semantics=sem, vmem_limit_bytes=_VMEM_LIMIT)


def _tile(dim, target, align=LANES):
    if dim <= target:
        return dim
    best = None
    for t in range(align, target + 1, align):
        if dim % t == 0:
            best = t
    assert best is not None, (dim, target, align)
    return best


def _gelu(x):
    return 0.5 * x * (1.0 + jnp.tanh(_GELU_C * (x + 0.044715 * (x * x * x))))


def _gelu_grad(x):
    t = jnp.tanh(_GELU_C * (x + 0.044715 * (x * x * x)))
    return 0.5 * (1.0 + t) + 0.5 * x * (1.0 - t * t) * (_GELU_C * (1.0 + 3.0 * 0.044715 * x * x))


def _sigmoid(x):
    return 1.0 / (1.0 + jnp.exp(-x))


def _colsum(x):
    return jnp.sum(x, axis=0, keepdims=True)


def _rowmean(x):
    return jnp.mean(x, axis=-1, keepdims=True)


def _zero_first(first, *refs):
    @pl.when(first)
    def _():
        for ref in refs:
            ref[...] = jnp.zeros_like(ref)


def _acc(ref, first, val):
    del first
    ref[...] += val


def _place():
    mx, my, mc = lax.axis_index("x"), lax.axis_index("y"), lax.axis_index("c")
    chips = [(1 - mx, my), (mx, 1 - my), (1 - mx, 1 - my)]
    return mx, my, mc, chips


def _all_gather8(buf, name, after=None):
    extra = [] if after is None else (list(after) if isinstance(after, (list, tuple)) else [after])

    def body(in_ref, *rest):
        out_ref, send_sems, recv_sems = rest[len(extra):]
        mx, my, mc, chips = _place()
        me, sibling = (mx, my, mc), (mx, my, 1 - mc)

        def slot(ref, px, py, pc):
            return ref.at[4 * px + 2 * py + pc]

        def copy(k, block, to, src_ref=out_ref):
            return pltpu.make_async_remote_copy(
                src_ref=slot(src_ref, *block), dst_ref=slot(out_ref, *block),
                send_sem=send_sems.at[k], recv_sem=recv_sems.at[k], device_id=to, device_id_type=_MESH)

        first = [copy(0, me, sibling, in_ref)]
        first += [copy(1 + j, me, (*chip, mc), in_ref) for j, chip in enumerate(chips)]
        for cp in first:
            cp.start()
        passed = [copy(4 + j, (*chip, mc), sibling) for j, chip in enumerate(chips)]
        for j, chip in enumerate(chips):
            copy(1 + j, (*chip, mc), me).wait_recv()
            passed[j].start()
        copy(0, sibling, me).wait_recv()
        for j, chip in enumerate(chips):
            copy(4 + j, (*chip, 1 - mc), me).wait_recv()
        for cp in first + passed:
            cp.wait_send()

    return pl.pallas_call(
        body, name=name, out_shape=jax.ShapeDtypeStruct(buf.shape, buf.dtype),
        in_specs=[_ANY] * (1 + len(extra)), out_specs=_ANY, input_output_aliases={0: 0},
        scratch_shapes=[pltpu.SemaphoreType.DMA((7,)), pltpu.SemaphoreType.DMA((7,))],
    )(buf, *extra)


def _own_slot(x, dev):
    return lax.dynamic_update_slice(lax.empty((8,) + x.shape, x.dtype), x[None], (dev, 0, 0))


def _cast_into_slot(w, sel, after, name):
    r, c = w.shape
    hr = r // 2
    tr = _tile(hr, 256, 16)
    nr = hr // tr

    def body(sel_ref, w_ref, after_ref, o_ref):
        o_ref[...] = w_ref[...].astype(o_ref.dtype)

    return pl.pallas_call(
        body, name=name, out_shape=jax.ShapeDtypeStruct((4, 2, hr, c), _WIRE),
        grid_spec=pltpu.PrefetchScalarGridSpec(
            num_scalar_prefetch=1, grid=(2, nr),
            in_specs=[pl.BlockSpec((tr, c), lambda h, i, s: (h * nr + i, 0)), _ANY],
            out_specs=pl.BlockSpec((None, None, tr, c), lambda h, i, s: (s[0], h, i, 0))),
        compiler_params=_cp("parallel", "parallel"),
    )(sel, w, after)


def _hbm(a):
    return pltpu.with_memory_space_constraint(a, pltpu.HBM)


def _after(vec, token):
    return vec + token[0:1, 0:1]


def _gather_start(bufs, after, name):
    n = len(bufs)
    nc = 3 * n

    def body(*refs):
        ins, send, recv, token = refs[:n], refs[n + 1:n + 1 + nc], refs[n + 1 + nc:n + 1 + 2 * nc], refs[-1]
        mx, my, mc, chips = _place()
        j_me = 2 * mx + my
        for i in range(n):
            for k, chip in enumerate(chips):
                half = ins[i].at[j_me, mc]
                pltpu.make_async_remote_copy(
                    src_ref=half, dst_ref=half, send_sem=send[3 * i + k], recv_sem=recv[3 * i + k],
                    device_id=(*chip, mc), device_id_type=_MESH).start()
        token[...] = jnp.zeros_like(token)

    outs = pl.pallas_call(
        body, name=name,
        out_shape=tuple([pltpu.SemaphoreType.DMA(())] * (2 * nc) + [pltpu.HBM(b.shape, b.dtype) for b in bufs]
                        + [jax.ShapeDtypeStruct((SUBLANES, LANES), _F32)]),
        in_specs=tuple([_HBM] * n + [_ANY]), out_specs=tuple([_SEM] * (2 * nc) + [_HBM] * n + [_VMEM_WHOLE]),
        input_output_aliases={i: 2 * nc + i for i in range(n)},
        compiler_params=pltpu.CompilerParams(has_side_effects=_EFFECT),
    )(*[_hbm(b) for b in bufs], after)
    sems = [(outs[3 * i:3 * i + 3], outs[nc + 3 * i:nc + 3 * i + 3]) for i in range(n)]
    return sems, list(outs[2 * nc:2 * nc + n]), outs[-1]


def _gather_wait(sems, buf, after, name):
    send, recv = sems

    after = list(after) if isinstance(after, (list, tuple)) else [after]

    def body(buf_ref, s0, s1, s2, r0, r1, r2, *rest):
        mx, my, mc, chips = _place()
        j_me = 2 * mx + my
        for k, (chip, s_k, r_k) in enumerate(zip(chips, (s0, s1, s2), (r0, r1, r2))):
            cp = pltpu.make_async_remote_copy(
                src_ref=buf_ref.at[j_me, mc], dst_ref=buf_ref.at[2 * chip[0] + chip[1], mc], send_sem=s_k, recv_sem=r_k,
                device_id=(*chip, mc), device_id_type=_MESH)
            cp.wait_send()
            cp.wait_recv()

    return pl.pallas_call(
        body, name=name, out_shape=pltpu.HBM(buf.shape, buf.dtype),
        in_specs=(_HBM,) + (_SEM,) * 6 + (_ANY,) * len(after), out_specs=_HBM, input_output_aliases={0: 0},
        compiler_params=pltpu.CompilerParams(has_side_effects=_EFFECT),
    )(buf, *send, *recv, *after)


def _route_ends(buf_ref, phase):
    mx, my, mc, _ = _place()
    hq = buf_ref.shape[2] // 2
    xn, yn = (1 - mx, my), (mx, 1 - my)
    j_me, j_x, j_y, j_d = 2 * mx + my, 2 * (1 - mx) + my, 2 * mx + (1 - my), 2 * (1 - mx) + (1 - my)
    if phase == 1:
        mine = buf_ref.at[j_me, mc]
        return [((*xn, mc), mine, buf_ref.at[j_x, mc]), ((*yn, mc), mine, buf_ref.at[j_y, mc])]
    lo, hi = pl.ds(0, hq), pl.ds(hq, hq)
    return [((*xn, mc), buf_ref.at[j_y, mc, lo], buf_ref.at[j_d, mc, lo]),
            ((*yn, mc), buf_ref.at[j_x, mc, hi], buf_ref.at[j_d, mc, hi])]


def _route_start(items, after, name):
    n = len(items)

    def body(*refs):
        ins, send, recv, token = refs[:n], refs[n + 1:3 * n + 1], refs[3 * n + 1:5 * n + 1], refs[-1]
        for i, (_, phase) in enumerate(items):
            for k, (peer, src, _) in enumerate(_route_ends(ins[i], phase)):
                pltpu.make_async_remote_copy(src_ref=src, dst_ref=src, send_sem=send[2 * i + k], recv_sem=recv[2 * i + k],
                                             device_id=peer, device_id_type=_MESH).start()
        token[...] = jnp.zeros_like(token)

    bufs = [b for b, _ in items]
    outs = pl.pallas_call(
        body, name=name,
        out_shape=tuple([pltpu.SemaphoreType.DMA(())] * (4 * n) + [pltpu.HBM(b.shape, b.dtype) for b in bufs]
                        + [jax.ShapeDtypeStruct((SUBLANES, LANES), _F32)]),
        in_specs=tuple([_HBM] * n + [_ANY]), out_specs=tuple([_SEM] * (4 * n) + [_HBM] * n + [_VMEM_WHOLE]),
        input_output_aliases={i: 4 * n + i for i in range(n)},
        compiler_params=pltpu.CompilerParams(has_side_effects=_EFFECT),
    )(*[_hbm(b) for b in bufs], after)
    sems = [(outs[2 * i:2 * i + 2], outs[2 * n + 2 * i:2 * n + 2 * i + 2]) for i in range(n)]
    return sems, list(outs[4 * n:5 * n]), outs[-1]


def _route_wait(sems, buf, phase, after, name):
    send, recv = sems

    def body(buf_ref, s0, s1, r0, r1, after_ref, out_ref):
        for (peer, src, land), s_k, r_k in zip(_route_ends(buf_ref, phase), (s0, s1), (r0, r1)):
            cp = pltpu.make_async_remote_copy(src_ref=src, dst_ref=land, send_sem=s_k, recv_sem=r_k,
                                              device_id=peer, device_id_type=_MESH)
            cp.wait_send()
            cp.wait_recv()

    return pl.pallas_call(
        body, name=name, out_shape=pltpu.HBM(buf.shape, buf.dtype),
        in_specs=(_HBM,) + (_SEM,) * 4 + (_ANY,), out_specs=_HBM, input_output_aliases={0: 0},
        compiler_params=pltpu.CompilerParams(has_side_effects=_EFFECT),
    )(buf, *send, *recv, after)


def _pair_forward(bufs, name):
    n = len(bufs)

    def body(*refs):
        ins, outs = refs[:n], refs[n:2 * n]
        send_sems, recv_sems = refs[2 * n:]
        mx, my, mc, chips = _place()
        sibling = (mx, my, 1 - mc)
        cps = []
        for i in range(n):
            for k, chip in enumerate(chips):
                j_k = 2 * chip[0] + chip[1]
                cp = pltpu.make_async_remote_copy(
                    src_ref=ins[i].at[j_k, mc], dst_ref=outs[i].at[j_k, mc], send_sem=send_sems.at[3 * i + k],
                    recv_sem=recv_sems.at[3 * i + k], device_id=sibling, device_id_type=_MESH)
                cp.start()
                cps.append(cp)
        for i in range(n):
            for k, chip in enumerate(chips):
                other = outs[i].at[2 * chip[0] + chip[1], 1 - mc]
                pltpu.make_async_remote_copy(
                    src_ref=other, dst_ref=other, send_sem=send_sems.at[3 * i + k], recv_sem=recv_sems.at[3 * i + k],
                    device_id=sibling, device_id_type=_MESH).wait_recv()
        for cp in cps:
            cp.wait_send()

    return pl.pallas_call(
        body, name=name, out_shape=[jax.ShapeDtypeStruct(b.shape, b.dtype) for b in bufs],
        in_specs=[_ANY] * n, out_specs=[_ANY] * n, input_output_aliases={i: i for i in range(n)},
        scratch_shapes=[pltpu.SemaphoreType.DMA((3 * n,)), pltpu.SemaphoreType.DMA((3 * n,))],
    )(*bufs)


def _gather8_peers(buf_ref, mx, my, mc, chips):
    mine = buf_ref.at[4 * mx + 2 * my + mc]
    peers = [((mx, my, 1 - mc), mine, buf_ref.at[4 * mx + 2 * my + 1 - mc])]
    peers += [((*chip, mc), mine, buf_ref.at[4 * chip[0] + 2 * chip[1] + mc]) for chip in chips]
    return peers


def _gather8_start(buf, name):
    def body(buf_ref, *rest):
        send, recv, token = rest[0:4], rest[4:8], rest[-1]
        mx, my, mc, chips = _place()
        for k, (peer, src, _) in enumerate(_gather8_peers(buf_ref, mx, my, mc, chips)):
            pltpu.make_async_remote_copy(src_ref=src, dst_ref=src, send_sem=send[k], recv_sem=recv[k],
                                         device_id=peer, device_id_type=_MESH).start()
        token[...] = jnp.zeros_like(token)

    outs = pl.pallas_call(
        body, name=name,
        out_shape=tuple([pltpu.SemaphoreType.DMA(())] * 8 + [pltpu.HBM(buf.shape, buf.dtype),
                                                             jax.ShapeDtypeStruct((SUBLANES, LANES), _F32)]),
        in_specs=(_HBM,), out_specs=tuple([_SEM] * 8 + [_HBM, _VMEM_WHOLE]), input_output_aliases={0: 8},
        compiler_params=pltpu.CompilerParams(has_side_effects=_EFFECT),
    )(_hbm(buf))
    return (outs[0:4], outs[4:8]), outs[8], outs[9]


def _gather8_wait(sems, buf, after, name):
    send, recv = sems

    def body(buf_ref, s0, s1, s2, s3, r0, r1, r2, r3, after_ref, out_ref):
        mx, my, mc, chips = _place()
        for (peer, src, dst), s_k, r_k in zip(_gather8_peers(buf_ref, mx, my, mc, chips), (s0, s1, s2, s3), (r0, r1, r2, r3)):
            cp = pltpu.make_async_remote_copy(src_ref=src, dst_ref=dst, send_sem=s_k, recv_sem=r_k,
                                              device_id=peer, device_id_type=_MESH)
            cp.wait_send()
            cp.wait_recv()

    return pl.pallas_call(
        body, name=name, out_shape=pltpu.HBM(buf.shape, buf.dtype),
        in_specs=(_HBM,) + (_SEM,) * 8 + (_ANY,), out_specs=_HBM, input_output_aliases={0: 0},
        compiler_params=pltpu.CompilerParams(has_side_effects=_EFFECT),
    )(buf, *send, *recv, after)


def _gather8_forward(buf, name):
    def body(in_ref, out_ref, send_sems, recv_sems):
        mx, my, mc, chips = _place()
        sibling = (mx, my, 1 - mc)
        cps = []
        for k, chip in enumerate(chips):
            idx = 4 * chip[0] + 2 * chip[1] + mc
            cp = pltpu.make_async_remote_copy(src_ref=in_ref.at[idx], dst_ref=out_ref.at[idx], send_sem=send_sems.at[k],
                                              recv_sem=recv_sems.at[k], device_id=sibling, device_id_type=_MESH)
            cp.start()
            cps.append(cp)
        for k, chip in enumerate(chips):
            other = out_ref.at[4 * chip[0] + 2 * chip[1] + 1 - mc]
            pltpu.make_async_remote_copy(src_ref=other, dst_ref=other, send_sem=send_sems.at[k], recv_sem=recv_sems.at[k],
                                         device_id=sibling, device_id_type=_MESH).wait_recv()
        for cp in cps:
            cp.wait_send()

    return pl.pallas_call(
        body, name=name, out_shape=jax.ShapeDtypeStruct(buf.shape, buf.dtype),
        in_specs=[_ANY], out_specs=_ANY, input_output_aliases={0: 0},
        scratch_shapes=[pltpu.SemaphoreType.DMA((3,)), pltpu.SemaphoreType.DMA((3,))],
    )(buf)


def _scatter_start(pair, name):
    land = lax.empty((3,) + pair.shape[1:], pair.dtype)

    def body(pair_ref, land_ref, s0, s1, s2, r0, r1, r2, pair_thru, land_thru, token):
        mx, my, mc, chips = _place()
        for k, (chip, s_k, r_k) in enumerate(zip(chips, (s0, s1, s2), (r0, r1, r2))):
            pltpu.make_async_remote_copy(
                src_ref=pair_ref.at[2 * chip[0] + chip[1]], dst_ref=land_ref.at[k], send_sem=s_k, recv_sem=r_k,
                device_id=(*chip, mc), device_id_type=_MESH).start()
        token[...] = jnp.zeros_like(token)

    outs = pl.pallas_call(
        body, name=name,
        out_shape=tuple([pltpu.SemaphoreType.DMA(())] * 6 + [pltpu.HBM(pair.shape, pair.dtype), pltpu.HBM(land.shape, land.dtype),
                                                             jax.ShapeDtypeStruct((SUBLANES, LANES), _F32)]),
        in_specs=(_HBM, _HBM), out_specs=tuple([_SEM] * 6 + [_HBM, _HBM, _VMEM_WHOLE]),
        input_output_aliases={0: 6, 1: 7}, compiler_params=pltpu.CompilerParams(has_side_effects=_EFFECT),
    )(_hbm(pair), _hbm(land))
    return (outs[0:3], outs[3:6]), outs[6], outs[7], outs[8]


def _scatter_wait(sems, pair, land, after, name):
    send, recv = sems

    def body(pair_ref, land_ref, s0, s1, s2, r0, r1, r2, after_ref, pair_out, land_out):
        mx, my, mc, chips = _place()
        for k, (chip, s_k, r_k) in enumerate(zip(chips, (s0, s1, s2), (r0, r1, r2))):
            cp = pltpu.make_async_remote_copy(
                src_ref=pair_ref.at[2 * chip[0] + chip[1]], dst_ref=land_ref.at[k], send_sem=s_k, recv_sem=r_k,
                device_id=(*chip, mc), device_id_type=_MESH)
            cp.wait_send()
            cp.wait_recv()

    return pl.pallas_call(
        body, name=name, out_shape=(pltpu.HBM(pair.shape, pair.dtype), pltpu.HBM(land.shape, land.dtype)),
        in_specs=(_HBM, _HBM) + (_SEM,) * 6 + (_ANY,), out_specs=(_HBM, _HBM), input_output_aliases={0: 0, 1: 1},
        compiler_params=pltpu.CompilerParams(has_side_effects=_EFFECT),
    )(pair, land, *send, *recv, after)


def _sibling_copy(src_ref, dst_ref, send_sem, recv_sem):
    mx, my, mc, _ = _place()
    return pltpu.make_async_remote_copy(src_ref=src_ref, dst_ref=dst_ref, send_sem=send_sem, recv_sem=recv_sem,
                                        device_id=(mx, my, 1 - mc), device_id_type=_MESH)


def _swap_start(g, name):
    land = lax.empty(g.shape[1:], g.dtype)

    def body(g_ref, land_ref, send_sem, recv_sem, g_thru, land_thru, token):
        _sibling_copy(g_ref.at[1 - lax.axis_index("c")], land_ref, send_sem, recv_sem).start()
        token[...] = jnp.zeros_like(token)

    outs = pl.pallas_call(
        body, name=name,
        out_shape=(pltpu.SemaphoreType.DMA(()), pltpu.SemaphoreType.DMA(()), pltpu.HBM(g.shape, g.dtype),
                   pltpu.HBM(land.shape, land.dtype), jax.ShapeDtypeStruct((SUBLANES, LANES), _F32)),
        in_specs=(_HBM, _HBM), out_specs=(_SEM, _SEM, _HBM, _HBM, _VMEM_WHOLE), input_output_aliases={0: 2, 1: 3},
        compiler_params=pltpu.CompilerParams(has_side_effects=_EFFECT),
    )(_hbm(g), _hbm(land))
    return (outs[0], outs[1]), outs[2], outs[3], outs[4]


def _swap_wait(sems, g, land, after, name):
    def body(g_ref, land_ref, send_sem, recv_sem, after_ref, g_out, land_out):
        cp = _sibling_copy(g_ref.at[1 - lax.axis_index("c")], land_ref, send_sem, recv_sem)
        cp.wait_send()
        cp.wait_recv()

    return pl.pallas_call(
        body, name=name, out_shape=(pltpu.HBM(g.shape, g.dtype), pltpu.HBM(land.shape, land.dtype)),
        in_specs=(_HBM, _HBM, _SEM, _SEM, _ANY), out_specs=(_HBM, _HBM), input_output_aliases={0: 0, 1: 1},
        compiler_params=pltpu.CompilerParams(has_side_effects=_EFFECT),
    )(g, land, *sems, after)


def _join_start(buf, name):
    def body(buf_ref, send_sem, recv_sem, buf_thru, token):
        mine = buf_ref.at[lax.axis_index("c")]
        _sibling_copy(mine, mine, send_sem, recv_sem).start()
        token[...] = jnp.zeros_like(token)

    outs = pl.pallas_call(
        body, name=name,
        out_shape=(pltpu.SemaphoreType.DMA(()), pltpu.SemaphoreType.DMA(()), pltpu.HBM(buf.shape, buf.dtype),
                   jax.ShapeDtypeStruct((SUBLANES, LANES), _F32)),
        in_specs=(_HBM,), out_specs=(_SEM, _SEM, _HBM, _VMEM_WHOLE), input_output_aliases={0: 2},
        compiler_params=pltpu.CompilerParams(has_side_effects=_EFFECT),
    )(_hbm(buf))
    return (outs[0], outs[1]), outs[2], outs[3]


def _join_wait(sems, buf, after, name):
    def body(buf_ref, send_sem, recv_sem, after_ref, buf_out):
        mc = lax.axis_index("c")
        cp = _sibling_copy(buf_ref.at[mc], buf_ref.at[1 - mc], send_sem, recv_sem)
        cp.wait_send()
        cp.wait_recv()

    return pl.pallas_call(
        body, name=name, out_shape=pltpu.HBM(buf.shape, buf.dtype),
        in_specs=(_HBM, _SEM, _SEM, _ANY), out_specs=_HBM, input_output_aliases={0: 0},
        compiler_params=pltpu.CompilerParams(has_side_effects=_EFFECT),
    )(buf, *sems, after)


def _pair_sum(g, got, sel, name):
    _, four, hr, c = g.shape
    tr = _tile(hr, 512, 16)

    def body(sel_ref, g_ref, p_ref, o_ref):
        o_ref[...] = (g_ref[...].astype(_F32) + p_ref[...].astype(_F32)).astype(o_ref.dtype)

    return pl.pallas_call(
        body, name=name, out_shape=jax.ShapeDtypeStruct((four, hr, c), g.dtype),
        grid_spec=pltpu.PrefetchScalarGridSpec(
            num_scalar_prefetch=1, grid=(four, hr // tr),
            in_specs=[pl.BlockSpec((None, None, tr, c), lambda j, i, s: (s[1], j, i, 0)),
                      pl.BlockSpec((None, tr, c), lambda j, i, s: (j, i, 0))],
            out_specs=pl.BlockSpec((None, tr, c), lambda j, i, s: (j, i, 0))),
        compiler_params=_cp("parallel", "parallel"),
    )(sel, g, got)


def _chip_sum(pair, got, sel, name):
    _, hr, c = pair.shape
    tr = _tile(hr, 512, 16)

    def body(sel_ref, p_ref, q_ref, o_ref):
        o_ref[...] = ((p_ref[...].astype(_F32) + q_ref[0].astype(_F32)) + q_ref[1].astype(_F32)) + q_ref[2].astype(_F32)

    return pl.pallas_call(
        body, name=name, out_shape=jax.ShapeDtypeStruct((2, hr, c), _F32),
        grid_spec=pltpu.PrefetchScalarGridSpec(
            num_scalar_prefetch=1, grid=(hr // tr,),
            in_specs=[pl.BlockSpec((None, tr, c), lambda i, s: (s[0], i, 0)),
                      pl.BlockSpec((3, tr, c), lambda i, s: (0, i, 0))],
            out_specs=pl.BlockSpec((None, tr, c), lambda i, s: (s[1], i, 0))),
        compiler_params=_cp("parallel"),
    )(sel, pair, got)


def _matmul(a, b, dims, out_struct, grid, a_spec, b_spec, o_spec, acc_shape, k_axis, name, after=None):
    nk = grid[k_axis]
    extra = [] if after is None else [after]

    def body(a_ref, b_ref, *rest):
        o_ref, acc = rest[len(extra)], rest[len(extra) + 1:]
        prod = lax.dot_general(a_ref[...].astype(_MXU), b_ref[...].astype(_MXU), dims, preferred_element_type=_F32)
        if nk == 1:
            o_ref[...] = prod.astype(o_ref.dtype)
        else:
            acc_ref, = acc
            k = pl.program_id(k_axis)
            _zero_first(k == 0, acc_ref)
            acc_ref[...] += prod

            @pl.when(k == nk - 1)
            def _():
                o_ref[...] = acc_ref[...].astype(o_ref.dtype)

    sem = ["parallel"] * len(grid)
    sem[k_axis] = "arbitrary"
    return pl.pallas_call(
        body, name=name, out_shape=out_struct, grid=grid, in_specs=[a_spec, b_spec] + [_ANY] * len(extra), out_specs=o_spec,
        scratch_shapes=[pltpu.VMEM(acc_shape, _F32)] if nk > 1 else [], compiler_params=_cp(*sem),
    )(a, b, *extra)


def _mm_nn(a, w4, out_dtype, name, tm=512, tn=1536, tk=2048, after=None):
    m, k = a.shape
    j, _, ns = w4.shape
    tm, tn, tk = _tile(m, tm, 16), _tile(ns, tn), _tile(k, tk)
    nps = ns // tn
    return _matmul(
        a, w4, (((1,), (0,)), ((), ())), jax.ShapeDtypeStruct((m, j * ns), out_dtype),
        (j * nps, m // tm, k // tk),
        pl.BlockSpec((tm, tk), lambda ni, mi, ki: (mi, ki)),
        pl.BlockSpec((None, tk, tn), lambda ni, mi, ki: (ni // nps, ki, ni % nps)),
        pl.BlockSpec((tm, tn), lambda ni, mi, ki: (mi, ni)), (tm, tn), 2, name, after)


def _mm_nt(a, w4, out_dtype, name, tm=512, tn=2048, tk=1536, after=None):
    m = a.shape[-2]
    j, kw, ns = w4.shape
    tm, tn, tk = _tile(m, tm, 16), _tile(kw, tn), _tile(ns, tk)
    kps = ns // tk
    if a.ndim == 3:
        kph = a.shape[2] // tk
        a_spec = pl.BlockSpec((None, tm, tk), lambda ni, mi, ki: (ki // kph, mi, ki % kph))
    else:
        a_spec = pl.BlockSpec((tm, tk), lambda ni, mi, ki: (mi, ki))
    return _matmul(
        a, w4, (((1,), (1,)), ((), ())), jax.ShapeDtypeStruct((m, kw), out_dtype),
        (kw // tn, m // tm, j * kps),
        a_spec,
        pl.BlockSpec((None, tn, tk), lambda ni, mi, ki: (ki // kps, ni, ki % kps)),
        pl.BlockSpec((tm, tn), lambda ni, mi, ki: (mi, ni)), (tm, tn), 2, name, after)


def _mm_tn_cols(a, b, name, tm=1024, tn=1536, tk=2048):
    m, ka = a.shape
    ns = (b.shape[-1] * (2 if b.ndim == 3 else 1)) // 4
    hr = ka // 2
    tm, tn, tk = _tile(hr, tm), _tile(ns, tn), _tile(m, tk, 16)
    mph, nps = hr // tm, ns // tn
    if b.ndim == 3:
        b_spec = pl.BlockSpec((None, tk, tn), lambda ni, mi, ki: (ni // (2 * nps), ki, ni % (2 * nps)))
    else:
        b_spec = pl.BlockSpec((tk, tn), lambda ni, mi, ki: (ki, ni))
    return _matmul(
        a, b, (((0,), (0,)), ((), ())), jax.ShapeDtypeStruct((2, 4, hr, ns), _WIRE),
        (4 * nps, 2 * mph, m // tk),
        pl.BlockSpec((tk, tm), lambda ni, mi, ki: (ki, mi)),
        b_spec,
        pl.BlockSpec((None, None, tm, tn), lambda ni, mi, ki: (mi // mph, ni // nps, mi % mph, ni % nps)),
        (tm, tn), 2, name)


def _mm_tn_rows(a, b, name, tm=1536, tn=1024, tk=2048):
    m, ka = a.shape
    r = ka // 4
    hc = b.shape[1] // 2
    tm, tn, tk = _tile(r, tm), _tile(hc, tn), _tile(m, tk, 16)
    mpr, nph = r // tm, hc // tn
    return _matmul(
        a, b, (((0,), (0,)), ((), ())), jax.ShapeDtypeStruct((2, 4, r, hc), _WIRE),
        (2 * nph, 4 * mpr, m // tk),
        pl.BlockSpec((tk, tm), lambda ni, mi, ki: (ki, mi)),
        pl.BlockSpec((tk, tn), lambda ni, mi, ki: (ki, ni)),
        pl.BlockSpec((None, None, tm, tn), lambda ni, mi, ki: (ni // nph, mi // mpr, mi % mpr, ni % nph)),
        (tm, tn), 2, name)


def _row_call(body, name, rows, ins, outs, tm=256):
    tm = _tile(rows, tm, 16)

    def spec(shape, kind):
        if kind == "rows":
            return pl.BlockSpec((tm, shape[1]), lambda i: (i, 0))
        return pl.BlockSpec(shape, lambda i: (0,) * len(shape))

    return pl.pallas_call(
        body, name=name, grid=(rows // tm,),
        in_specs=[spec(a.shape, kind) for a, kind in ins],
        out_specs=[spec(o.shape, kind) for o, kind in outs],
        out_shape=[o for o, _ in outs],
        compiler_params=_cp("arbitrary"),
    )(*[a for a, _ in ins])


def _rms(x):
    r = lax.rsqrt(_rowmean(x * x) + EPS)
    return x * r, r


def _rms_bwd(dxh, xh, r):
    return r * (dxh - xh * _rowmean(dxh * xh))


def _fwd_pre_mix(x, g, sc, sh):
    s, d = x.shape

    def body(x_ref, g_ref, sc_ref, sh_ref, h_ref):
        xh, _ = _rms(x_ref[...])
        h_ref[...] = (xh * g_ref[...] * (1.0 + sc_ref[...]) + sh_ref[...]).astype(h_ref.dtype)

    return _row_call(body, "fwd_pre_mix", s, [(x, "rows"), (g, "vec"), (sc, "vec"), (sh, "vec")],
                     [(jax.ShapeDtypeStruct((s, d), _MXU), "rows")])[0]


def _fwd_mid(o, x, gt1, g_post, g_pre2, sc2, sh2):
    s, d = x.shape

    def body(o_ref, x_ref, gt_ref, gp_ref, g2_ref, sc_ref, sh_ref, x1_ref, h2_ref):
        oh, _ = _rms(o_ref[...])
        x1 = x_ref[...] + gt_ref[...] * (oh * gp_ref[...])
        x1_ref[...] = x1
        xh, _ = _rms(x1)
        h2_ref[...] = (xh * g2_ref[...] * (1.0 + sc_ref[...]) + sh_ref[...]).astype(h2_ref.dtype)

    return _row_call(body, "fwd_mid", s,
                     [(o, "rows"), (x, "rows"), (gt1, "vec"), (g_post, "vec"), (g_pre2, "vec"), (sc2, "vec"),
                      (sh2, "vec")],
                     [(jax.ShapeDtypeStruct((s, d), _F32), "rows"), (jax.ShapeDtypeStruct((s, d), _MXU), "rows")])


def _loss_and_post_ffn_bwd(f, x1, tgt, gt2, g_post):
    s, d = x1.shape

    def body(f_ref, x1_ref, t_ref, gt_ref, g_ref, dx2_ref, df_ref, dgt_ref, dg_ref, loss_ref):
        first = pl.program_id(0) == 0
        _zero_first(first, dgt_ref, dg_ref, loss_ref)
        fh, r = _rms(f_ref[...])
        n = fh * g_ref[...]
        e = x1_ref[...] + gt_ref[...] * n - t_ref[...]
        _acc(loss_ref, first, jnp.sum(_colsum(e * e), axis=1, keepdims=True) * (0.5 / d))
        dx2 = e * (1.0 / d)
        dx2_ref[...] = dx2
        _acc(dgt_ref, first, _colsum(dx2 * n))
        dn = dx2 * gt_ref[...]
        _acc(dg_ref, first, _colsum(dn * fh))
        df_ref[...] = _rms_bwd(dn * g_ref[...], fh, r).astype(df_ref.dtype)

    vec = jax.ShapeDtypeStruct((1, d), _F32)
    return _row_call(body, "loss_post_ffn_bwd", s,
                     [(f, "rows"), (x1, "rows"), (tgt, "rows"), (gt2, "vec"), (g_post, "vec")],
                     [(jax.ShapeDtypeStruct((s, d), _F32), "rows"), (jax.ShapeDtypeStruct((s, d), _MXU), "rows"),
                      (vec, "vec"), (vec, "vec"), (jax.ShapeDtypeStruct((1, 1), _F32), "vec")])


def _bwd_mid(dh2, x1, dx2, o, g_pre2, sc2, gt1, g_post):
    s, d = x1.shape

    def body(dh_ref, x1_ref, dx2_ref, o_ref, g2_ref, sc_ref, gt_ref, gp_ref,
             dx1_ref, do_ref, dsc_ref, dsh_ref, dg2_ref, dgt_ref, dgp_ref):
        first = pl.program_id(0) == 0
        _zero_first(first, dsc_ref, dsh_ref, dg2_ref, dgt_ref, dgp_ref)
        dh = dh_ref[...]
        xh, r = _rms(x1_ref[...])
        _acc(dsh_ref, first, _colsum(dh))
        _acc(dsc_ref, first, _colsum(dh * (xh * g2_ref[...])))
        dn = dh * (1.0 + sc_ref[...])
        _acc(dg2_ref, first, _colsum(dn * xh))
        dx1 = dx2_ref[...] + _rms_bwd(dn * g2_ref[...], xh, r)
        dx1_ref[...] = dx1
        oh, ro = _rms(o_ref[...])
        _acc(dgt_ref, first, _colsum(dx1 * (oh * gp_ref[...])))
        dno = dx1 * gt_ref[...]
        _acc(dgp_ref, first, _colsum(dno * oh))
        do_ref[...] = _rms_bwd(dno * gp_ref[...], oh, ro).astype(do_ref.dtype)

    vec = jax.ShapeDtypeStruct((1, d), _F32)
    return _row_call(body, "bwd_mid", s,
                     [(dh2, "rows"), (x1, "rows"), (dx2, "rows"), (o, "rows"), (g_pre2, "vec"), (sc2, "vec"),
                      (gt1, "vec"), (g_post, "vec")],
                     [(jax.ShapeDtypeStruct((s, d), _F32), "rows"), (jax.ShapeDtypeStruct((s, d), _MXU), "rows"),
                      (vec, "vec"), (vec, "vec"), (vec, "vec"), (vec, "vec"), (vec, "vec")])


def _bwd_pre_mix(dh1, x, dx1, g, sc1):
    s, d = x.shape

    def body(dh_ref, x_ref, dx1_ref, g_ref, sc_ref, dx_ref, dsc_ref, dsh_ref, dg_ref):
        first = pl.program_id(0) == 0
        _zero_first(first, dsc_ref, dsh_ref, dg_ref)
        dh = dh_ref[...]
        xh, r = _rms(x_ref[...])
        _acc(dsh_ref, first, _colsum(dh))
        _acc(dsc_ref, first, _colsum(dh * (xh * g_ref[...])))
        dn = dh * (1.0 + sc_ref[...])
        _acc(dg_ref, first, _colsum(dn * xh))
        dx_ref[...] = dx1_ref[...] + _rms_bwd(dn * g_ref[...], xh, r)

    vec = jax.ShapeDtypeStruct((1, d), _F32)
    return _row_call(body, "bwd_pre_mix", s,
                     [(dh1, "rows"), (x, "rows"), (dx1, "rows"), (g, "vec"), (sc1, "vec")],
                     [(jax.ShapeDtypeStruct((s, d), _F32), "rows"), (vec, "vec"), (vec, "vec"), (vec, "vec")])


def _mix_norm_fwd(y_ssm, y_sgu, g_ssm, g_sgu):
    s, h = y_ssm.shape

    def body(a_ref, b_ref, ga_ref, gb_ref, o_ref):
        ah, _ = _rms(a_ref[...])
        bh, _ = _rms(b_ref[...])
        o_ref[:, 0:h] = (ah * ga_ref[...]).astype(o_ref.dtype)
        o_ref[:, h:2 * h] = (bh * gb_ref[...]).astype(o_ref.dtype)

    return _row_call(body, "mix_norm_fwd", s, [(y_ssm, "rows"), (y_sgu, "rows"), (g_ssm, "vec"), (g_sgu, "vec")],
                     [(jax.ShapeDtypeStruct((s, 2 * h), _MXU), "rows")])[0]


def _mix_norm_bwd(dyc, y_ssm, y_sgu, g_ssm, g_sgu):
    s, h = y_ssm.shape

    def body(d_ref, a_ref, b_ref, ga_ref, gb_ref, da_ref, db_ref, dga_ref, dgb_ref):
        first = pl.program_id(0) == 0
        _zero_first(first, dga_ref, dgb_ref)
        for lo, y_ref, g_ref, dy_ref, dg_ref in ((0, a_ref, ga_ref, da_ref, dga_ref), (h, b_ref, gb_ref, db_ref, dgb_ref)):
            d = d_ref[:, lo:lo + h]
            yh, r = _rms(y_ref[...])
            _acc(dg_ref, first, _colsum(d * yh))
            dy_ref[...] = _rms_bwd(d * g_ref[...], yh, r)

    vec = jax.ShapeDtypeStruct((1, h), _F32)
    full = jax.ShapeDtypeStruct((s, h), _F32)
    return _row_call(body, "mix_norm_bwd", s,
                     [(dyc, "rows"), (y_ssm, "rows"), (y_sgu, "rows"), (g_ssm, "vec"), (g_sgu, "vec")],
                     [(full, "rows"), (full, "rows"), (vec, "vec"), (vec, "vec")])


CONV_ROWS = 64


def _conv_rows(ext, w_ref, b_ref):
    x = ext[SUBLANES:]
    s1 = pltpu.roll(ext, 1, 0)[SUBLANES:]
    s2 = pltpu.roll(ext, 2, 0)[SUBLANES:]
    return b_ref[...] + w_ref[0:1, :] * s2 + w_ref[1:2, :] * s1 + w_ref[2:3, :] * x, x, s1, s2


def _conv_window(x_ref, r0):
    if isinstance(r0, int):
        assert r0 == 0
        return jnp.concatenate([jnp.zeros((SUBLANES, x_ref.shape[1]), _F32), x_ref[0:CONV_ROWS, :]], axis=0)
    return x_ref[pl.ds(pl.multiple_of(r0 - SUBLANES, SUBLANES), CONV_ROWS + SUBLANES), :]


def _conv_act_fwd(up_pre, conv_w, conv_b):
    s, f2 = up_pre.shape
    f = f2 // 2
    tc = _tile(f, 256)
    nf = f // tc

    def shift_down(x, k):
        row = lax.broadcasted_iota(jnp.int32, x.shape, 0)
        return jnp.where(row >= k, pltpu.roll(x, k, 0), 0.0)

    def conv(x, w_ref, b_ref):
        return b_ref[...] + w_ref[0:1, :] * shift_down(x, 2) + w_ref[1:2, :] * shift_down(x, 1) + w_ref[2:3, :] * x

    def body(a_ref, b_ref, wa_ref, wb_ref, ba_ref, bb_ref, o_ref):
        a = conv(a_ref[...], wa_ref, ba_ref)
        b = conv(b_ref[...], wb_ref, bb_ref)
        o_ref[...] = (a * _sigmoid(a) * b).astype(o_ref.dtype)

    return pl.pallas_call(
        body, name="conv_act_fwd", grid=(nf,), out_shape=jax.ShapeDtypeStruct((s, f), _MXU),
        in_specs=[pl.BlockSpec((s, tc), lambda n: (0, n)), pl.BlockSpec((s, tc), lambda n: (0, n + nf)),
                  pl.BlockSpec((3, tc), lambda n: (0, n)), pl.BlockSpec((3, tc), lambda n: (0, n + nf)),
                  pl.BlockSpec((1, tc), lambda n: (0, n)), pl.BlockSpec((1, tc), lambda n: (0, n + nf))],
        out_specs=pl.BlockSpec((s, tc), lambda n: (0, n)), compiler_params=_cp("parallel"),
    )(up_pre, up_pre, conv_w, conv_w, conv_b, conv_b)


def _conv_act_bwd(up_pre, d_act, conv_w, conv_b):
    s, f2 = up_pre.shape
    f = f2 // 2
    tc = _tile(f, 256)
    nf = f // tc

    def body(a_ref, b_ref, d_ref, wa_ref, wb_ref, ba_ref, bb_ref,
             du_ref, w0a, w0b, w1a, w1b, w2a, w2b, dba, dbb):
        n = s // CONV_ROWS
        zero8 = jnp.zeros((SUBLANES, tc), _F32)
        ext_rows = CONV_ROWS + SUBLANES

        def fold(x):
            out = x[0:SUBLANES]
            for k in range(1, CONV_ROWS // SUBLANES):
                out = out + x[k * SUBLANES:(k + 1) * SUBLANES]
            return out

        def chunk(r0, carry):
            nxt, acc = carry
            a, xa, xa1, xa2 = _conv_rows(_conv_window(a_ref, r0), wa_ref, ba_ref)
            b, xb, xb1, xb2 = _conv_rows(_conv_window(b_ref, r0), wb_ref, bb_ref)
            sg = _sigmoid(a)
            d = d_ref[pl.ds(r0, CONV_ROWS), :]
            du_a = d * b * (sg * (1.0 + a * (1.0 - sg)))
            du_b = d * (a * sg)
            new_acc = []
            for h, (du, x0, x1, x2, w_ref) in enumerate(((du_a, xa, xa1, xa2, wa_ref), (du_b, xb, xb1, xb2, wb_ref))):
                ext = jnp.concatenate([du, nxt[h]], axis=0)
                u1 = pltpu.roll(ext, ext_rows - 1, 0)[:CONV_ROWS]
                u2 = pltpu.roll(ext, ext_rows - 2, 0)[:CONV_ROWS]
                du_ref[h, pl.ds(r0, CONV_ROWS), :] = (w_ref[2:3, :] * du + w_ref[1:2, :] * u1
                                                      + w_ref[0:1, :] * u2).astype(du_ref.dtype)
                new_acc += [acc[4 * h] + fold(du * x2), acc[4 * h + 1] + fold(du * x1), acc[4 * h + 2] + fold(du * x0),
                            acc[4 * h + 3] + fold(du)]
            return (du_a[:SUBLANES], du_b[:SUBLANES]), tuple(new_acc)

        def step(i, carry):
            return chunk(pl.multiple_of((n - 1 - i) * CONV_ROWS, CONV_ROWS), carry)

        carry = lax.fori_loop(0, n - 1, step, ((zero8, zero8), (zero8,) * 8))
        _, acc = chunk(0, carry)
        for ref, val in zip((w0a, w1a, w2a, dba, w0b, w1b, w2b, dbb), acc):
            ref[...] = _colsum(val)

    col_a = pl.BlockSpec((s, tc), lambda n: (0, n))
    col_b = pl.BlockSpec((s, tc), lambda n: (0, n + nf))
    vec_a = pl.BlockSpec((1, tc), lambda n: (0, n))
    vec_b = pl.BlockSpec((1, tc), lambda n: (0, n + nf))
    vec = jax.ShapeDtypeStruct((1, f), _F32)
    outs = pl.pallas_call(
        body, name="conv_act_bwd", grid=(nf,),
        in_specs=[col_a, col_b, col_a, pl.BlockSpec((3, tc), lambda n: (0, n)),
                  pl.BlockSpec((3, tc), lambda n: (0, n + nf)), vec_a, vec_b],
        out_specs=[pl.BlockSpec((2, s, tc), lambda n: (0, 0, n))] + [vec_a] * 8,
        out_shape=[jax.ShapeDtypeStruct((2, s, f), _MXU)] + [vec] * 8, compiler_params=_cp("parallel"),
    )(up_pre, up_pre, d_act, conv_w, conv_w, conv_b, conv_b)
    du, w0a, w0b, w1a, w1b, w2a, w2b, dba, dbb = outs
    cat = lambda p, q: jnp.concatenate([p, q], axis=1)
    return du, cat(w0a, w0b), cat(w1a, w1b), cat(w2a, w2b), cat(dba, dbb)


def _sgu_recompute(zu_ref, zv_ref, lng_ref, lnb_ref, wm_ref, bs_ref, nh):
    zu, zv = zu_ref[...], zv_ref[...]
    u = _gelu(zu)
    gv = _gelu(zv)
    xc = gv - _rowmean(gv)
    rs = lax.rsqrt(_rowmean(xc * xc) + EPS)
    vh = xc * rs
    v = vh * lng_ref[...] + lnb_ref[...]
    mixed = []
    for h in range(nh):
        vhd = v[:, h * CHUNK:(h + 1) * CHUNK].astype(_MXU)
        mixed.append(jnp.dot(wm_ref[h].astype(_MXU), vhd, preferred_element_type=_F32) + bs_ref[h])
    return zu, zv, u, vh, rs, v, mixed


def _sgu_fwd(z, ln_g, ln_b, wm, bs):
    s = z.shape[0]
    nh = wm.shape[0]
    hd = nh * CHUNK

    def body(zu_ref, zv_ref, lng_ref, lnb_ref, wm_ref, bs_ref, y_ref):
        _, _, u, _, _, _, mixed = _sgu_recompute(zu_ref, zv_ref, lng_ref, lnb_ref, wm_ref, bs_ref, nh)
        for h in range(nh):
            y_ref[:, h * CHUNK:(h + 1) * CHUNK] = u[:, h * CHUNK:(h + 1) * CHUNK] * mixed[h]

    vec = pl.BlockSpec((1, hd), lambda i: (0, 0))
    return pl.pallas_call(
        body, name="sgu_fwd", grid=(s // CHUNK,), out_shape=jax.ShapeDtypeStruct((s, hd), _F32),
        in_specs=[pl.BlockSpec((CHUNK, hd), lambda i: (i, 1)), pl.BlockSpec((CHUNK, hd), lambda i: (i, 2)), vec, vec,
                  pl.BlockSpec((nh, CHUNK, CHUNK), lambda i: (0, 0, 0)), pl.BlockSpec((nh, CHUNK, 1), lambda i: (0, 0, 0))],
        out_specs=pl.BlockSpec((CHUNK, hd), lambda i: (i, 0)), compiler_params=_cp("parallel"),
    )(z, z, ln_g, ln_b, wm, bs)


def _sgu_bwd(z, dy, dz_ssm, ln_g, ln_b, wm, bs):
    s = z.shape[0]
    nh = wm.shape[0]
    hd = nh * CHUNK

    def body(zu_ref, zv_ref, dy_ref, dzs_ref, lng_ref, lnb_ref, wm_ref, bs_ref,
             dz_ref, dlg_ref, dlb_ref, dwm_ref, dbs_ref, dv_scr):
        first = pl.program_id(0) == 0
        _zero_first(first, dlg_ref, dlb_ref, dwm_ref, dbs_ref)
        zu, zv, u, vh, rs, v, mixed = _sgu_recompute(zu_ref, zv_ref, lng_ref, lnb_ref, wm_ref, bs_ref, nh)
        dy = dy_ref[...]
        dz_ref[:, 0:hd] = dzs_ref[...].astype(dz_ref.dtype)
        for h in range(nh):
            cols = slice(h * CHUNK, (h + 1) * CHUNK)
            dyh = dy[:, cols]
            dz_ref[:, hd + h * CHUNK:hd + (h + 1) * CHUNK] = (dyh * mixed[h] * _gelu_grad(zu[:, cols])).astype(dz_ref.dtype)
            dm = dyh * u[:, cols]
            dmx = dm.astype(_MXU)
            _acc(dbs_ref.at[h], first, jnp.sum(dm, axis=1, keepdims=True))
            _acc(dwm_ref.at[h], first,
                 lax.dot_general(dmx, v[:, cols].astype(_MXU), (((1,), (1,)), ((), ())), preferred_element_type=_F32))
            dv_scr[:, cols] = lax.dot_general(wm_ref[h].astype(_MXU), dmx, (((0,), (0,)), ((), ())),
                                              preferred_element_type=_F32)
        dv = dv_scr[...]
        _acc(dlg_ref, first, _colsum(dv * vh))
        _acc(dlb_ref, first, _colsum(dv))
        dvh = dv * lng_ref[...]
        dgv = rs * (dvh - _rowmean(dvh) - vh * _rowmean(dvh * vh))
        dz_ref[:, 2 * hd:3 * hd] = (dgv * _gelu_grad(zv)).astype(dz_ref.dtype)

    vec = pl.BlockSpec((1, hd), lambda i: (0, 0))
    wspec = pl.BlockSpec((nh, CHUNK, CHUNK), lambda i: (0, 0, 0))
    bspec = pl.BlockSpec((nh, CHUNK, 1), lambda i: (0, 0, 0))
    rows = pl.BlockSpec((CHUNK, hd), lambda i: (i, 0))
    return pl.pallas_call(
        body, name="sgu_bwd", grid=(s // CHUNK,),
        out_shape=[jax.ShapeDtypeStruct((s, 3 * hd), _MXU), jax.ShapeDtypeStruct((1, hd), _F32),
                   jax.ShapeDtypeStruct((1, hd), _F32), jax.ShapeDtypeStruct((nh, CHUNK, CHUNK), _F32),
                   jax.ShapeDtypeStruct((nh, CHUNK, 1), _F32)],
        in_specs=[pl.BlockSpec((CHUNK, hd), lambda i: (i, 1)), pl.BlockSpec((CHUNK, hd), lambda i: (i, 2)),
                  rows, rows, vec, vec, wspec, bspec],
        out_specs=[pl.BlockSpec((CHUNK, 3 * hd), lambda i: (i, 0)), vec, vec, wspec, bspec],
        scratch_shapes=[pltpu.VMEM((CHUNK, hd), _F32)], compiler_params=_cp("arbitrary"),
    )(z, z, dy, dz_ssm, ln_g, ln_b, wm, bs)


def _ssm_prep(log_dt, a_re, a_im, b_re_t, b_im_t, kvec):
    gn = a_re.shape[1]

    def body(ldt_ref, are_ref, aim_ref, br_ref, bi_ref, k_ref, pr_ref, pi_ref, bbr_ref, bbi_ref):
        dt = jnp.exp(ldt_ref[...])
        are, aim = are_ref[...], aim_ref[...]
        k = k_ref[...]
        mag = jnp.exp(k * (are * dt))
        ang = k * (aim * dt)
        pr_ref[...] = mag * jnp.cos(ang)
        pi_ref[...] = mag * jnp.sin(ang)
        m1 = jnp.exp(are * dt)
        lr, li = m1 * jnp.cos(aim * dt), m1 * jnp.sin(aim * dt)
        den = are * are + aim * aim
        nr = lr - 1.0
        f_re = (nr * are + li * aim) / den
        f_im = (li * are - nr * aim) / den
        bbr_ref[...] = f_re * br_ref[...] - f_im * bi_ref[...]
        bbi_ref[...] = f_re * bi_ref[...] + f_im * br_ref[...]

    pw = jax.ShapeDtypeStruct((kvec.shape[0], gn), _F32)
    bb = jax.ShapeDtypeStruct(b_re_t.shape, _F32)
    return pl.pallas_call(body, name="ssm_prep", out_shape=[pw, pw, bb, bb])(log_dt, a_re, a_im, b_re_t, b_im_t, kvec)


def _ssm_prep_bwd(log_dt, a_re, a_im, b_re_t, b_im_t, d_bbr, d_bbi, d_lr, d_li):
    def body(ldt_ref, are_ref, aim_ref, br_ref, bi_ref, dbr_ref, dbi_ref, dlr_ref, dli_ref,
             obr_ref, obi_ref, oar_ref, oai_ref, odt_ref):
        dt = jnp.exp(ldt_ref[...])
        are, aim = are_ref[...], aim_ref[...]
        m1 = jnp.exp(are * dt)
        lr, li = m1 * jnp.cos(aim * dt), m1 * jnp.sin(aim * dt)
        den = are * are + aim * aim
        nr = lr - 1.0
        f_re = (nr * are + li * aim) / den
        f_im = (li * are - nr * aim) / den
        br, bi, dbr, dbi = br_ref[...], bi_ref[...], dbr_ref[...], dbi_ref[...]
        obr_ref[...] = f_re * dbr + f_im * dbi
        obi_ref[...] = f_re * dbi - f_im * dbr
        gf_re = _colsum(br * dbr + bi * dbi)
        gf_im = _colsum(br * dbi - bi * dbr)
        il_re, il_im = are / den, -aim / den
        glb_re = dlr_ref[...] + (il_re * gf_re + il_im * gf_im)
        glb_im = dli_ref[...] + (il_re * gf_im - il_im * gf_re)
        q_re = -(f_re * il_re - f_im * il_im)
        q_im = -(f_re * il_im + f_im * il_re)
        gl_re = q_re * gf_re + q_im * gf_im
        gl_im = q_re * gf_im - q_im * gf_re
        gl_re = gl_re + dt * (lr * glb_re + li * glb_im)
        gl_im = gl_im + dt * (lr * glb_im - li * glb_re)
        w_re = are * lr - aim * li
        w_im = are * li + aim * lr
        oar_ref[...] = gl_re
        oai_ref[...] = gl_im
        odt_ref[...] = w_re * glb_re + w_im * glb_im

    bb = jax.ShapeDtypeStruct(b_re_t.shape, _F32)
    v = jax.ShapeDtypeStruct(a_re.shape, _F32)
    return pl.pallas_call(body, name="ssm_prep_bwd", out_shape=[bb, bb, v, v, v])(
        log_dt, a_re, a_im, b_re_t, b_im_t, d_bbr, d_bbi, d_lr, d_li)


def _group_sum(d_dt, log_dt):
    def body(d_ref, l_ref, o_ref):
        o_ref[...] = jnp.sum(d_ref[...], axis=1, keepdims=True) * jnp.exp(l_ref[...])

    return pl.pallas_call(body, name="ssm_dt_grad", out_shape=jax.ShapeDtypeStruct(log_dt.shape, _F32))(d_dt, log_dt)


def _load_strided(ref, nr):
    return jnp.concatenate([ref[pl.ds(r, SUBLANES, stride=nr), :] for r in range(nr)], axis=0)


def _store_strided(ref, val, nr):
    for r in range(nr):
        ref[pl.ds(r, SUBLANES, stride=nr), :] = val[r * SUBLANES:(r + 1) * SUBLANES]


def _scan_strided(src_ref, dst_ref, nr, p_ref, carry, reverse, h_ref=None, h_in=None):
    ns = BLOCK_ST
    row = lax.broadcasted_iota(jnp.int32, (SUBLANES, ns), 0)
    bc = lambda v: jnp.broadcast_to(v, (SUBLANES, ns))
    tile = lambda ref, r: (ref[r * SUBLANES:(r + 1) * SUBLANES, 0:ns], ref[r * SUBLANES:(r + 1) * SUBLANES, ns:2 * ns])
    one = nr - 1 if reverse else 0
    ar, ai = bc(p_ref[one:one + 1, 0:ns]), bc(p_ref[one:one + 1, ns:2 * ns])
    xr = xi = None
    for r in (range(nr - 1, -1, -1) if reverse else range(nr)):
        sr, si = tile(src_ref, r)
        xr, xi = (sr, si) if xr is None else (ar * xr - ai * xi + sr, ar * xi + ai * xr + si)
        dst_ref[r * SUBLANES:(r + 1) * SUBLANES, 0:ns] = xr
        dst_ref[r * SUBLANES:(r + 1) * SUBLANES, ns:2 * ns] = xi
    edge, shift = (SUBLANES - 1, SUBLANES - 1) if reverse else (0, 1)
    dr = jnp.where(row == edge, carry[0], pltpu.roll(xr, shift, 0))
    di = jnp.where(row == edge, carry[1], pltpu.roll(xi, shift, 0))
    for i, k in enumerate((1, 2, 4)):
        qr, qi = bc(p_ref[nr + i:nr + i + 1, 0:ns]), bc(p_ref[nr + i:nr + i + 1, ns:2 * ns])
        keep = (row < SUBLANES - k) if reverse else (row >= k)
        sr = jnp.where(keep, pltpu.roll(dr, (SUBLANES - k) if reverse else k, 0), 0.0)
        si = jnp.where(keep, pltpu.roll(di, (SUBLANES - k) if reverse else k, 0), 0.0)
        dr, di = dr + qr * sr - qi * si, di + qr * si + qi * sr
    acc_r = acc_i = jnp.zeros((SUBLANES, ns), _F32)
    out = None
    for r in range(nr):
        wr, wi = p_ref[r:r + 1, 0:ns], p_ref[r:r + 1, ns:2 * ns]
        xr, xi = tile(dst_ref, r)
        xr, xi = xr + wr * dr - wi * di, xi + wr * di + wi * dr
        dst_ref[r * SUBLANES:(r + 1) * SUBLANES, 0:ns] = xr
        dst_ref[r * SUBLANES:(r + 1) * SUBLANES, ns:2 * ns] = xi
        if h_ref is not None:
            if r == 0:
                lr, li = tile(h_ref, nr - 1)
                pr, pi = jnp.where(row == 0, h_in[0], pltpu.roll(lr, 1, 0)), jnp.where(row == 0, h_in[1], pltpu.roll(li, 1, 0))
            else:
                pr, pi = tile(h_ref, r - 1)
            acc_r = acc_r + (xr * pr + xi * pi)
            acc_i = acc_i + (xi * pr - xr * pi)
        if r == (0 if reverse else nr - 1):
            out = (xr[0:1, :], xi[0:1, :]) if reverse else (xr[SUBLANES - 1:SUBLANES, :], xi[SUBLANES - 1:SUBLANES, :])
    if h_ref is None:
        return out
    return out, (_colsum(acc_r), _colsum(acc_i))


def _ssm_gate(y, wg_ref, bg_ref):
    yg = _gelu(y)
    gate = _sigmoid(jnp.dot(yg.astype(_MXU), wg_ref[...].astype(_MXU), preferred_element_type=_F32) + bg_ref[...])
    return yg, gate


def _ssm_specs(nb, nt, t, reverse):
    tt = (lambda ti: nt - 1 - ti) if reverse else (lambda ti: ti)
    ns2 = 2 * BLOCK_ST
    return dict(
        z=pl.BlockSpec((t, BLOCK_CH), lambda b, ti: (tt(ti), b)),
        bbt=pl.BlockSpec((None, BLOCK_CH, ns2), lambda b, ti: (b, 0, 0)),
        ct=pl.BlockSpec((None, ns2, BLOCK_CH), lambda b, ti: (b, 0, 0)),
        vec=pl.BlockSpec((1, BLOCK_CH), lambda b, ti: (0, b)),
        wg=pl.BlockSpec((None, BLOCK_CH, BLOCK_CH), lambda b, ti: (b, 0, 0)),
        p=pl.BlockSpec((None, t // SUBLANES + SUBLANES, ns2), lambda b, ti: (b, 0, 0)),
        hb=pl.BlockSpec((None, None, SUBLANES, ns2), lambda b, ti: (b, tt(ti), 0, 0)),
        h=pl.BlockSpec((None, t, ns2), lambda b, ti: (b, tt(ti), 0)),
        acc_vec=pl.BlockSpec((None, 1, ns2), lambda b, ti: (b, 0, 0)),
    )


def _ssm_fwd(z, bbt, ct, dvec, wg, bglu, ptab):
    s = z.shape[0]
    nb = bbt.shape[0]
    t = _tile(s, TIME_TILE, SUBLANES)
    nt = s // t
    ns = BLOCK_ST
    sp = _ssm_specs(nb, nt, t, False)

    nr = t // SUBLANES

    def body(z_ref, bbt_ref, ct_ref, d_ref, wg_ref, bg_ref, p_ref, y2_ref, y_ref, h_ref, hb_ref, bu_scr, h_scr, carry_scr):
        _zero_first(pl.program_id(1) == 0, carry_scr)
        hb_ref[...] = carry_scr[...]
        carry_in = (carry_scr[0:1, 0:ns], carry_scr[0:1, ns:2 * ns])
        u = _load_strided(z_ref, nr)
        bu_scr[...] = jnp.dot(u.astype(_MXU), bbt_ref[...].astype(_MXU), preferred_element_type=_F32)
        cr, ci = _scan_strided(bu_scr, h_scr, nr, p_ref, carry_in, False)
        hx = h_scr[...].astype(_MXU)
        h_ref[...] = hx
        y = jnp.dot(hx, ct_ref[...].astype(_MXU), preferred_element_type=_F32) + d_ref[...] * u
        yg, gate = _ssm_gate(y, wg_ref, bg_ref)
        _store_strided(y2_ref, yg * gate, nr)
        _store_strided(y_ref, y, nr)
        carry_scr[:, 0:ns] = jnp.broadcast_to(cr, (SUBLANES, ns))
        carry_scr[:, ns:2 * ns] = jnp.broadcast_to(ci, (SUBLANES, ns))

    ych = jax.ShapeDtypeStruct((s, nb * BLOCK_CH), _F32)
    return pl.pallas_call(
        body, name="ssm_fwd", grid=(nb, nt),
        out_shape=[ych, ych, jax.ShapeDtypeStruct((nb, s, 2 * ns), _MXU),
                   jax.ShapeDtypeStruct((nb, nt, SUBLANES, 2 * ns), _F32)],
        in_specs=[sp["z"], sp["bbt"], sp["ct"], sp["vec"], sp["wg"], sp["vec"], sp["p"]],
        out_specs=[sp["z"], sp["z"], sp["h"], sp["hb"]],
        scratch_shapes=[pltpu.VMEM((t, 2 * ns), _F32), pltpu.VMEM((t, 2 * ns), _F32), pltpu.VMEM((SUBLANES, 2 * ns), _F32)],
        compiler_params=_cp("parallel", "arbitrary"),
    )(z, bbt, ct, dvec, wg, bglu, ptab)


def _ssm_bwd(z, y_pre, h_all, dy2, hb, bbt, ct, dvec, wg, bglu, ptab_rev):
    s = z.shape[0]
    nb = bbt.shape[0]
    t = _tile(s, TIME_TILE, SUBLANES)
    nt = s // t
    ns = BLOCK_ST
    sp = _ssm_specs(nb, nt, t, True)
    tn_dims = (((0,), (0,)), ((), ()))
    nt_dims = (((1,), (1,)), ((), ()))

    nr = t // SUBLANES

    def body(z_ref, y_ref, h_ref, dy2_ref, hb_ref, bbt_ref, ct_ref, d_ref, wg_ref, bg_ref, pr_ref,
             dz_ref, dbbt_ref, dct_ref, dwg_ref, dlb_ref, dd_ref, dbg_ref, bu_scr, g_scr, h_scr, gcarry_scr):
        first = pl.program_id(1) == 0

        _zero_first(first, gcarry_scr, dbbt_ref, dct_ref, dwg_ref, dlb_ref, dd_ref, dbg_ref)
        u = _load_strided(z_ref, nr)
        hin = hb_ref[...]
        y = _load_strided(y_ref, nr)
        yg, gate = _ssm_gate(y, wg_ref, bg_ref)
        dy2 = _load_strided(dy2_ref, nr)
        dpre = dy2 * yg * gate * (1.0 - gate)
        _acc(dbg_ref, first, _colsum(dpre))
        dpx = dpre.astype(_MXU)
        _acc(dwg_ref, first, lax.dot_general(yg.astype(_MXU), dpx, tn_dims, preferred_element_type=_F32))
        dyg = dy2 * gate + lax.dot_general(dpx, wg_ref[...].astype(_MXU), nt_dims, preferred_element_type=_F32)
        dy = dyg * _gelu_grad(y)
        _acc(dd_ref, first, _colsum(dy * u))
        dyx = dy.astype(_MXU)
        hx = h_ref[...]
        h_scr[...] = hx.astype(_F32)
        _acc(dct_ref, first, lax.dot_general(hx, dyx, tn_dims, preferred_element_type=_F32))
        bu_scr[...] = lax.dot_general(dyx, ct_ref[...].astype(_MXU), nt_dims, preferred_element_type=_F32)
        gin = (gcarry_scr[0:1, 0:ns], gcarry_scr[0:1, ns:2 * ns])
        (gr, gi), (d_ar, d_ai) = _scan_strided(bu_scr, g_scr, nr, pr_ref, gin, True, h_scr,
                                               (hin[0:1, 0:ns], hin[0:1, ns:2 * ns]))
        gcarry_scr[:, 0:ns] = jnp.broadcast_to(gr, (SUBLANES, ns))
        gcarry_scr[:, ns:2 * ns] = jnp.broadcast_to(gi, (SUBLANES, ns))
        _acc(dlb_ref, first, jnp.concatenate([d_ar, d_ai], axis=1))
        gx = g_scr[...].astype(_MXU)
        _acc(dbbt_ref, first, lax.dot_general(u.astype(_MXU), gx, tn_dims, preferred_element_type=_F32))
        _store_strided(dz_ref, dy * d_ref[...] + lax.dot_general(gx, bbt_ref[...].astype(_MXU), nt_dims,
                                                                 preferred_element_type=_F32), nr)

    f = lambda shape: jax.ShapeDtypeStruct(shape, _F32)
    return pl.pallas_call(
        body, name="ssm_bwd", grid=(nb, nt),
        out_shape=[f((s, nb * BLOCK_CH)), f(bbt.shape), f(ct.shape), f(wg.shape), f((nb, 1, 2 * ns)),
                   f((1, nb * BLOCK_CH)), f((1, nb * BLOCK_CH))],
        in_specs=[sp["z"], sp["z"], sp["h"], sp["z"], sp["hb"], sp["bbt"], sp["ct"], sp["vec"], sp["wg"], sp["vec"], sp["p"]],
        out_specs=[sp["z"], sp["bbt"], sp["ct"], sp["wg"], sp["acc_vec"], sp["vec"], sp["vec"]],
        scratch_shapes=[pltpu.VMEM((t, 2 * ns), _F32), pltpu.VMEM((t, 2 * ns), _F32), pltpu.VMEM((t, 2 * ns), _F32),
                        pltpu.VMEM((SUBLANES, 2 * ns), _F32)],
        compiler_params=_cp("parallel", "arbitrary"),
    )(z, y_pre, h_all, dy2, hb, bbt, ct, dvec, wg, bglu, ptab_rev)


def _mod_part(c_all, w, b):
    d, ns = w.shape
    tn = _tile(ns, 512)

    def body(c_ref, w_ref, b_ref, o_ref):
        c = c_ref[...]
        ca = (c * _sigmoid(c)).astype(_MXU)
        o_ref[...] = jnp.dot(ca, w_ref[...].astype(_MXU), preferred_element_type=_F32) + b_ref[...]

    return pl.pallas_call(
        body, name="mod_part", grid=(ns // tn,), out_shape=jax.ShapeDtypeStruct((8, ns), _F32),
        in_specs=[pl.BlockSpec((8, d), lambda n: (0, 0)), pl.BlockSpec((d, tn), lambda n: (0, n)),
                  pl.BlockSpec((1, tn), lambda n: (0, n))],
        out_specs=pl.BlockSpec((8, tn), lambda n: (0, n)), compiler_params=_cp("parallel"),
    )(c_all, w, b)


def _adamw_math(w, g, m, v):
    m = ADAM_B1 * m + (1.0 - ADAM_B1) * g
    v = ADAM_B2 * v + (1.0 - ADAM_B2) * (g * g)
    m_hat = m / (1.0 - ADAM_B1 ** ADAM_STEP)
    v_hat = v / (1.0 - ADAM_B2 ** ADAM_STEP)
    delta = -ADAM_LR * (m_hat / (jnp.sqrt(v_hat) + ADAM_EPS) + ADAM_WD * w)
    return delta, m, v


def _adamw(w, g, m, v, name):
    r, c = w.shape
    tc = c if c <= 4096 else _tile(c, 4096)
    tr = _tile(r, max(SUBLANES, (1 << 18) // tc), SUBLANES)

    def body(w_ref, g_ref, m_ref, v_ref, go_ref, d_ref, mo_ref, vo_ref):
        g = g_ref[...]
        go_ref[...] = g
        d_ref[...], mo_ref[...], vo_ref[...] = _adamw_math(w_ref[...], g, m_ref[...], v_ref[...])

    spec = pl.BlockSpec((tr, tc), lambda i, j: (i, j))
    out = jax.ShapeDtypeStruct((r, c), _F32)
    return pl.pallas_call(
        body, name=name, grid=(r // tr, c // tc), in_specs=[spec] * 4, out_specs=[spec] * 4, out_shape=[out] * 4,
        compiler_params=_cp("parallel", "parallel"),
    )(w, g, m, v)


def _adamw_halves(w, g2, m, v, name):
    r, c = w.shape
    tr, tc = _tile(r, 256, SUBLANES), _tile(c // 2, 1024)
    nph = (c // 2) // tc

    def body(w_ref, g_ref, m_ref, v_ref, go_ref, d_ref, mo_ref, vo_ref):
        g = g_ref[...]
        go_ref[...] = g
        d_ref[...], mo_ref[...], vo_ref[...] = _adamw_math(w_ref[...], g, m_ref[...], v_ref[...])

    spec = pl.BlockSpec((tr, tc), lambda i, j: (i, j))
    out = jax.ShapeDtypeStruct((r, c), _F32)
    return pl.pallas_call(
        body, name=name, grid=(r // tr, c // tc),
        in_specs=[spec, pl.BlockSpec((None, tr, tc), lambda i, j: (j // nph, i, j % nph)), spec, spec],
        out_specs=[spec] * 4, out_shape=[out] * 4, compiler_params=_cp("parallel", "parallel"),
    )(w, g2, m, v)


def _wada_update(c_t, dm, w, m, v):
    d, ns = w.shape
    tr, tc = _tile(d, 256, SUBLANES), _tile(ns, 1024)

    def body(c_ref, dm_ref, w_ref, m_ref, v_ref, g_ref, d_ref, mo_ref, vo_ref):
        c = c_ref[...]
        ca = c * _sigmoid(c)
        dmv = dm_ref[...]
        g = ca[:, 0:1] * dmv[0:1, :]
        for b in range(1, 8):
            g = g + ca[:, b:b + 1] * dmv[b:b + 1, :]
        g_ref[...] = g
        d_ref[...], mo_ref[...], vo_ref[...] = _adamw_math(w_ref[...], g, m_ref[...], v_ref[...])

    spec = pl.BlockSpec((tr, tc), lambda i, j: (i, j))
    out = jax.ShapeDtypeStruct((d, ns), _F32)
    return pl.pallas_call(
        body, name="wada_update", grid=(d // tr, ns // tc),
        in_specs=[pl.BlockSpec((tr, 8), lambda i, j: (i, 0)), pl.BlockSpec((8, tc), lambda i, j: (0, j)), spec, spec, spec],
        out_specs=[spec] * 4, out_shape=[out] * 4, compiler_params=_cp("parallel", "parallel"),
    )(c_t, dm, w, m, v)


def _small_reduce(gathered):
    _, r, c = gathered.shape
    tr = _tile(r, 512, SUBLANES)

    def body(q_ref, g_ref):
        g = q_ref[0]
        for k in range(1, 8):
            g = g + q_ref[k]
        g_ref[...] = g

    return pl.pallas_call(
        body, name="small_reduce", grid=(r // tr,), out_shape=jax.ShapeDtypeStruct((r, c), _F32),
        in_specs=[pl.BlockSpec((8, tr, c), lambda i: (0, i, 0))], out_specs=pl.BlockSpec((tr, c), lambda i: (i, 0)),
        compiler_params=_cp("parallel"),
    )(gathered)


def _adamw_many(ws, gs, ms, vs, steps, name):
    n = len(ws)

    def body(*refs):
        w_refs, g_refs, m_refs, v_refs = refs[0:n], refs[n:2 * n], refs[2 * n:3 * n], refs[3 * n:4 * n]
        d_refs, mo_refs, vo_refs = refs[4 * n:5 * n], refs[5 * n:6 * n], refs[6 * n:7 * n]
        for i in range(n):
            d_refs[i][...], mo_refs[i][...], vo_refs[i][...] = _adamw_math(
                w_refs[i][...], g_refs[i][...], m_refs[i][...], v_refs[i][...])

    def spec(a):
        nd = a.ndim
        if steps == 1:
            return pl.BlockSpec(a.shape, lambda i: (0,) * nd)
        return pl.BlockSpec((a.shape[0] // steps,) + a.shape[1:], lambda i: (i,) + (0,) * (nd - 1))

    specs = [spec(w) for w in ws]
    outs = pl.pallas_call(
        body, name=name, grid=(steps,), in_specs=specs * 4, out_specs=specs * 3,
        out_shape=[jax.ShapeDtypeStruct(w.shape, _F32) for w in ws] * 3, compiler_params=_cp("parallel"),
    )(*ws, *gs, *ms, *vs)
    return outs[0:n], outs[n:2 * n], outs[2 * n:3 * n]


def _block_diag(x, eye=None):
    nb, g, p, q = x.shape
    eye = jnp.eye(g, dtype=x.dtype) if eye is None else eye
    return (x[:, :, :, None, :] * eye[None, :, None, :, None]).reshape(nb, g * p, g * q)


def _block_diag_take(x, p, q):
    nb = x.shape[0]
    g = GROUPS_PER_BLOCK
    eye = jnp.eye(g, dtype=x.dtype)
    return jnp.sum(x.reshape(nb, g, p, g, q) * eye[None, :, None, :, None], axis=3)


_VIEWS = {"ssm_b_re": ((0, 2, 1), (0, 2, 1)), "ssm_b_im": ((0, 2, 1), (0, 2, 1)),
          "ssm_w_glu": ((1, 2, 0), (2, 0, 1)), "ssm_b_glu": ((1, 0), (1, 0))}


def _to_view(name, a):
    return a.transpose(_VIEWS[name][0]) if name in _VIEWS else a


def _from_view(name, a):
    return a.transpose(_VIEWS[name][1]) if name in _VIEWS else a


class _Pack:
    def __init__(self, shapes):
        self.shapes = shapes
        self.offsets = {}
        off = 0
        for name, shape in shapes.items():
            n = math.prod(shape)
            self.offsets[name] = (off, n)
            off += -(-n // (SUBLANES * LANES)) * (SUBLANES * LANES)
        self.rows = -(-off // (256 * LANES)) * 256

    def pack(self, arrays):
        parts = []
        off = 0
        for name, shape in self.shapes.items():
            start, n = self.offsets[name]
            if start > off:
                parts.append(jnp.zeros((start - off,), _F32))
            parts.append(arrays[name].reshape(-1).astype(_F32))
            off = start + n
        total = self.rows * LANES
        if total > off:
            parts.append(jnp.zeros((total - off,), _F32))
        return jnp.concatenate(parts).reshape(self.rows, LANES)

    def unpack(self, buf):
        flat = buf.reshape(-1)
        return {name: flat[start:start + n].reshape(self.shapes[name]) for name, (start, n) in self.offsets.items()}


_SMALL = ["b_ada", "g_pre_mix", "g_post_mix", "ssm_log_dt", "ssm_a_re", "ssm_a_im", "ssm_b_re", "ssm_b_im", "ssm_c_re",
          "ssm_c_im", "ssm_d", "ssm_w_glu", "ssm_b_glu", "sgu_ln_g", "sgu_ln_b", "sgu_w", "sgu_b", "g_out_ssm",
          "g_out_sgu", "g_pre_ffn", "g_post_ffn", "conv_b"]
_WEIGHTS = ["w_ada", "b_ada", "g_pre_mix", "g_post_mix", "w_in", "ssm_log_dt", "ssm_a_re", "ssm_a_im", "ssm_b_re",
            "ssm_b_im", "ssm_c_re", "ssm_c_im", "ssm_d", "ssm_w_glu", "ssm_b_glu", "sgu_ln_g", "sgu_ln_b", "sgu_w", "sgu_b",
            "g_out_ssm", "g_out_sgu", "w_out", "g_pre_ffn", "g_post_ffn", "w_up", "conv_w", "conv_b", "w_down"]


def _step(p, m, v, x, c, tgt):
    s, d = x.shape
    mx, my, mc = lax.axis_index("x"), lax.axis_index("y"), lax.axis_index("c")
    chip = 2 * mx + my
    dev = 4 * mx + 2 * my + mc
    sel = jnp.stack([chip, mc]).astype(jnp.int32)
    g_cnt, n_st = p["ssm_a_re"].shape
    nb = g_cnt // GROUPS_PER_BLOCK
    gn = g_cnt * n_st
    d_ssm = g_cnt * SSM_GROUP
    nh = p["sgu_w"].shape[0]
    assert nh * CHUNK == d_ssm and 2 * d_ssm == d and n_st == SSM_STATE

    shards = lambda g: g.reshape(4, g.shape[1] * g.shape[2], g.shape[3])
    buf_in = _cast_into_slot(p["w_in"], sel, sel, "cast_w_in")

    ns_ada = p["w_ada"].shape[1]
    nc_conv = p["conv_w"].shape[1]
    first = jnp.concatenate([jnp.broadcast_to(c, (8, d)), jnp.pad(p["conv_w"], ((0, 5), (0, 0)))], axis=1)
    first_all = _all_gather8(_own_slot(first, dev), "gather_c_conv", after=buf_in)
    (sems_in,), (buf_in,), tok = _gather_start([buf_in], first_all, "gather_start_in")
    c_all = _after(first_all[:, 0, :d], tok)
    conv_w_full = jnp.concatenate([first_all[2 * j, 0:3, d:] for j in range(4)], axis=1)
    b_ada_mine = lax.dynamic_slice_in_dim(p["b_ada"], chip * ns_ada, ns_ada, axis=1)
    mod_mine = _mod_part(c_all, p["w_ada"], b_ada_mine)
    buf_out, buf_up, buf_down = [_cast_into_slot(p[n], sel, tok, "cast_" + n) for n in ("w_out", "w_up", "w_down")]

    eye_t = jnp.eye(GROUPS_PER_BLOCK, dtype=_F32) + tok[0:1, 0:1]
    ldt_l = _after(jnp.repeat(p["ssm_log_dt"], n_st, axis=1), tok)
    are_l, aim_l = p["ssm_a_re"].reshape(1, gn), p["ssm_a_im"].reshape(1, gn)
    bre_t, bim_t = p["ssm_b_re"].reshape(gn, SSM_GROUP).T, p["ssm_b_im"].reshape(gn, SSM_GROUP).T
    nr = _tile(s, TIME_TILE, SUBLANES) // SUBLANES
    kvec = jnp.concatenate([jnp.arange(1, nr + 1, dtype=_F32), jnp.array([nr, 2 * nr, 4 * nr, 0, 0, 0, 0, 0], _F32)])
    pw_re, pw_im, bb_re, bb_im = _ssm_prep(ldt_l, are_l, aim_l, bre_t, bim_t, kvec.reshape(nr + SUBLANES, 1))
    blocks = lambda t: t.reshape(t.shape[0], nb, GROUPS_PER_BLOCK * n_st).transpose(1, 0, 2)
    ptab = jnp.concatenate([blocks(pw_re), blocks(pw_im)], axis=2)
    rev = lambda t: jnp.concatenate([t[:, :nr][:, ::-1], t[:, nr:]], axis=1)
    ptab_rev = jnp.concatenate([rev(blocks(pw_re)), -rev(blocks(pw_im))], axis=2)
    bd = lambda t: t.reshape(SSM_GROUP, nb, GROUPS_PER_BLOCK, n_st).transpose(1, 2, 0, 3)
    bbt = jnp.concatenate([_block_diag(bd(bb_re)), _block_diag(bd(bb_im))], axis=2).astype(_MXU)
    cd = lambda t: t.reshape(nb, GROUPS_PER_BLOCK, SSM_GROUP, n_st).transpose(0, 1, 3, 2)
    ct = jnp.concatenate([_block_diag(cd(p["ssm_c_re"]), eye_t), -_block_diag(cd(p["ssm_c_im"]), eye_t)], axis=1).astype(_MXU)
    wg = _block_diag(p["ssm_w_glu"].reshape(nb, GROUPS_PER_BLOCK, SSM_GROUP, SSM_GROUP), eye_t).astype(_MXU)
    dvec = p["ssm_d"]
    bglu = p["ssm_b_glu"].reshape(1, d_ssm)
    mask = jnp.tril(jnp.ones((CHUNK, CHUNK), _F32)) + tok[0:1, 0:1]
    wm = (p["sgu_w"] * mask[None]).astype(_MXU)
    bs = p["sgu_b"].reshape(nh, CHUNK, 1)

    mod_all = _all_gather8(_own_slot(mod_mine, dev), "gather_mod",
                           after=[buf_out, buf_up, buf_down, bbt, ct, wg, wm, bs, ptab, ptab_rev])
    (sems_out, sems_up), (buf_out, buf_up), tok_rest = _route_start([(buf_out, 1), (buf_up, 1)], mod_all, "route_start_a")
    mod_rows = lax.dynamic_index_in_dim(mod_all, dev, axis=1, keepdims=False)
    mod = jnp.concatenate([mod_rows[0], mod_rows[2], mod_rows[4], mod_rows[6]]).reshape(N_MOD, 1, d)
    sh1, sc1, gt1, sh2, sc2, gt2 = [mod[i] for i in range(N_MOD)]

    h1 = _fwd_pre_mix(x, p["g_pre_mix"], _after(sc1, tok_rest), sh1)
    buf_in = _gather_wait(sems_in, buf_in, h1, "gather_wait_in")
    w_in4 = shards(_pair_forward([buf_in], "pair_forward_in")[0])
    z = _mm_nn(h1, w_in4, _F32, "mm_in")
    y_ssm, y_pre, h_all, hb = _ssm_fwd(z, bbt, ct, dvec, wg, bglu, ptab)
    y_sgu = _sgu_fwd(z, p["sgu_ln_g"], p["sgu_ln_b"], wm, bs)
    buf_out = _route_wait(sems_out, buf_out, 1, y_sgu, "route_wait_out_1")
    buf_up = _route_wait(sems_up, buf_up, 1, y_ssm, "route_wait_up_1")
    (sems_out, sems_up, sems_down), (buf_out, buf_up, buf_down), tok = _route_start(
        [(buf_out, 2), (buf_up, 2), (buf_down, 1)], y_sgu, "route_start_b")
    ycat = _mix_norm_fwd(y_ssm, y_sgu, _after(p["g_out_ssm"], tok), p["g_out_sgu"])
    buf_out = _route_wait(sems_out, buf_out, 2, ycat, "route_wait_out_2")
    w_out_full = _pair_forward([buf_out], "pair_forward_out")[0].reshape(1, d, d)
    o = _mm_nn(ycat, w_out_full, _F32, "mm_out")
    x1, h2 = _fwd_mid(o, x, gt1, p["g_post_mix"], p["g_pre_ffn"], sc2, sh2)
    buf_up = _route_wait(sems_up, buf_up, 2, h2, "route_wait_up_2")
    w_up4 = shards(_pair_forward([buf_up], "pair_forward_up")[0])
    up_pre = _mm_nn(h2, w_up4, _F32, "mm_up")
    buf_down = _route_wait(sems_down, buf_down, 1, up_pre, "route_wait_down_1")
    (sems_down,), (buf_down,), tok = _route_start([(buf_down, 2)], up_pre, "route_start_c")
    act = _conv_act_fwd(up_pre, conv_w_full, _after(p["conv_b"], tok))
    buf_down = _route_wait(sems_down, buf_down, 2, act, "route_wait_down_2")
    w_down_full = _pair_forward([buf_down], "pair_forward_down")[0].reshape(1, -1, d)
    f = _mm_nn(act, w_down_full, _F32, "mm_down", tk=5632)
    dx2, df, d_gt2, d_g_post_ffn, loss = _loss_and_post_ffn_bwd(f, x1, tgt, gt2, p["g_post_ffn"])

    def reduce_next(swap, n, after):
        sems, gw, land, _ = swap
        gw, got = _swap_wait(sems, gw, land, after, "swap_wait_" + n)
        return _scatter_start(_pair_sum(gw, got, sel, "pair_sum_" + n), "scatter_start_" + n)

    d_act = _mm_nt(df, w_down_full, _F32, "mm_d_act", tk=2048)
    swap_down = _swap_start(_mm_tn_rows(act, df, "mm_gw_down"), "swap_start_w_down")
    d_up_pre, d_cw0, d_cw1, d_cw2, d_conv_b = _conv_act_bwd(up_pre, d_act, conv_w_full, _after(p["conv_b"], swap_down[3]))
    red_down = reduce_next(swap_down, "w_down", d_conv_b)
    dh2 = _mm_nt(d_up_pre, w_up4, _F32, "mm_dh2", tk=2816, after=red_down[3])
    swap_up = _swap_start(_mm_tn_cols(h2, d_up_pre, "mm_gw_up"), "swap_start_w_up")
    dx1, d_o, d_sc2, d_sh2, d_g_pre_ffn, d_gt1, d_g_post_mix = _bwd_mid(
        dh2, x1, dx2, o, p["g_pre_ffn"], _after(sc2, swap_up[3]), gt1, p["g_post_mix"])
    red_up = reduce_next(swap_up, "w_up", d_g_post_mix)
    d_ycat = _mm_nt(d_o, w_out_full, _F32, "mm_d_ycat", tn=1024, tk=2048, after=red_up[3])
    swap_out = _swap_start(_mm_tn_rows(ycat, d_o, "mm_gw_out"), "swap_start_w_out")
    dy_ssm, dy_sgu, d_g_out_ssm, d_g_out_sgu = _mix_norm_bwd(
        d_ycat, y_ssm, y_sgu, _after(p["g_out_ssm"], swap_out[3]), p["g_out_sgu"])
    red_out = reduce_next(swap_out, "w_out", d_g_out_sgu)
    dz_ssm, d_bbt, d_ct, d_wg, d_lb, d_ssm_d, d_bglu = _ssm_bwd(z, y_pre, h_all, dy_ssm, hb, bbt, ct,
                                                                _after(dvec, red_out[3]), wg, bglu, ptab_rev)
    dz, d_ln_g, d_ln_b, d_wm, d_bs = _sgu_bwd(z, dy_sgu, dz_ssm, p["sgu_ln_g"], p["sgu_ln_b"], wm, bs)
    dh1 = _mm_nt(dz, w_in4, _F32, "mm_dh1")
    swap_in = _swap_start(_mm_tn_cols(h1, dz, "mm_gw_in"), "swap_start_w_in")
    dx, d_sc1, d_sh1, d_g_pre_mix = _bwd_pre_mix(dh1, x, dx1, p["g_pre_mix"], _after(sc1, swap_in[3]))
    red_in = reduce_next(swap_in, "w_in", d_g_pre_mix)

    nsb = BLOCK_ST
    lanes = lambda t: t.transpose(2, 0, 1, 3).reshape(SSM_GROUP, gn)
    d_bbr = lanes(_block_diag_take(d_bbt[:, :, :nsb], SSM_GROUP, n_st))
    d_bbi = lanes(_block_diag_take(d_bbt[:, :, nsb:], SSM_GROUP, n_st))
    d_lr, d_li = d_lb[:, 0, :nsb].reshape(1, gn), d_lb[:, 0, nsb:].reshape(1, gn)
    d_bre_t, d_bim_t, d_are, d_aim, d_dt = _ssm_prep_bwd(ldt_l, are_l, aim_l, bre_t, bim_t, d_bbr, d_bbi, d_lr, d_li)
    d_log_dt = _group_sum(d_dt.reshape(g_cnt, n_st), p["ssm_log_dt"].reshape(g_cnt, 1))
    c_grad = lambda t: _block_diag_take(t, n_st, SSM_GROUP).transpose(0, 1, 3, 2).reshape(g_cnt, SSM_GROUP, n_st)
    small = {
        "b_ada": jnp.concatenate([d_sh1, _after(d_sc1, red_in[3]), d_gt1, d_sh2, d_sc2, d_gt2], axis=1),
        "g_pre_mix": d_g_pre_mix, "g_post_mix": d_g_post_mix,
        "ssm_log_dt": d_log_dt, "ssm_a_re": d_are, "ssm_a_im": d_aim,
        "ssm_b_re": d_bre_t.T, "ssm_b_im": d_bim_t.T,
        "ssm_c_re": c_grad(d_ct[:, :nsb, :]), "ssm_c_im": -c_grad(d_ct[:, nsb:, :]),
        "ssm_d": d_ssm_d, "ssm_w_glu": _block_diag_take(d_wg, SSM_GROUP, SSM_GROUP), "ssm_b_glu": d_bglu,
        "sgu_ln_g": d_ln_g, "sgu_ln_b": d_ln_b, "sgu_w": d_wm * mask[None], "sgu_b": d_bs,
        "g_out_ssm": d_g_out_ssm, "g_out_sgu": d_g_out_sgu, "g_pre_ffn": d_g_pre_ffn, "g_post_ffn": d_g_post_ffn,
        "conv_b": d_conv_b, "conv_w_all": jnp.concatenate([d_cw0, d_cw1, d_cw2], axis=0),
        "loss_sum": loss,
    }
    small = {n: _to_view(n, a.reshape(p[n].shape)) if n in p else a for n, a in small.items()}
    pk = _Pack({n: a.shape for n, a in small.items()})
    sems_small, small_buf, tok = _gather8_start(_own_slot(pk.pack(small), dev), "gather_small_start")

    big = ["w_down", "w_up", "w_out", "w_in"]
    joins = []
    after = tok
    for n, (sems, pair, land, _) in zip(big, (red_down, red_up, red_out, red_in)):
        pair, land = _scatter_wait(sems, pair, land, after, "scatter_wait_" + n)
        sems_j, half, after = _join_start(_chip_sum(pair, land, sel, "chip_sum_" + n), "join_start_" + n)
        joins.append((sems_j, half))
    big_out = {}
    for n, (sems_j, half) in zip(big, joins):
        j = _join_wait(sems_j, half, after, "join_wait_" + n)
        if n in ("w_in", "w_up"):
            big_out[n] = tuple(_adamw(p[n], j.reshape(p[n].shape), m[n], v[n], "adamw_" + n))
        else:
            big_out[n] = tuple(_adamw_halves(p[n], j, m[n], v[n], "adamw_" + n))
        after = big_out[n][1]

    gathered = _gather8_forward(_gather8_wait(sems_small, small_buf, after, "gather_small_wait"),
                                "gather_small_forward")
    gview = pk.unpack(_small_reduce(gathered))
    gview["conv_w"] = lax.dynamic_slice_in_dim(gview.pop("conv_w_all"), chip * nc_conv, nc_conv, axis=1)
    loss = gview.pop("loss_sum")
    small_names = _SMALL + ["conv_w"]
    per_group = [n for n in small_names if gview[n].ndim >= 2 and gview[n].shape[0] == g_cnt]
    others = [n for n in small_names if n not in per_group]
    grads = {n: _from_view(n, gview[n]) for n in small_names}
    deltas, new_m, new_v = {}, {}, {}
    for names, steps, call in ((per_group, g_cnt // GROUPS_PER_BLOCK, "adamw_s5"), (others, 1, "adamw_small")):
        res = _adamw_many([_to_view(n, p[n]) for n in names], [gview[n] for n in names],
                          [_to_view(n, m[n]) for n in names], [_to_view(n, v[n]) for n in names], steps, call)
        for n, dl, mo, vo in zip(names, *res):
            deltas[n], new_m[n], new_v[n] = _from_view(n, dl), _from_view(n, mo), _from_view(n, vo)

    d_mod_all = gathered.reshape(8, -1)[:, :N_MOD * d]
    d_mod_mine = lax.dynamic_slice_in_dim(d_mod_all, chip * ns_ada, ns_ada, axis=1)
    grads["w_ada"], deltas["w_ada"], new_m["w_ada"], new_v["w_ada"] = _wada_update(
        c_all.T, d_mod_mine, p["w_ada"], m["w_ada"], v["w_ada"])
    for n in big:
        grads[n], deltas[n], new_m[n], new_v[n] = big_out[n]
    return loss[0, 0], dx, grads, deltas, new_m, new_v


def kernel(x, c, w_ada, b_ada, g_pre_mix, g_post_mix, w_in, ssm_log_dt, ssm_a_re, ssm_a_im, ssm_b_re, ssm_b_im, ssm_c_re, ssm_c_im, ssm_d, ssm_w_glu, ssm_b_glu, sgu_ln_g, sgu_ln_b, sgu_w, sgu_b, g_out_ssm, g_out_sgu, w_out, g_pre_ffn, g_post_ffn, w_up, conv_w, conv_b, w_down, loss_target, m_w_ada, m_b_ada, m_g_pre_mix, m_g_post_mix, m_w_in, m_ssm_log_dt, m_ssm_a_re, m_ssm_a_im, m_ssm_b_re, m_ssm_b_im, m_ssm_c_re, m_ssm_c_im, m_ssm_d, m_ssm_w_glu, m_ssm_b_glu, m_sgu_ln_g, m_sgu_ln_b, m_sgu_w, m_sgu_b, m_g_out_ssm, m_g_out_sgu, m_w_out, m_g_pre_ffn, m_g_post_ffn, m_w_up, m_conv_w, m_conv_b, m_w_down, v_w_ada, v_b_ada, v_g_pre_mix, v_g_post_mix, v_w_in, v_ssm_log_dt, v_ssm_a_re, v_ssm_a_im, v_ssm_b_re, v_ssm_b_im, v_ssm_c_re, v_ssm_c_im, v_ssm_d, v_ssm_w_glu, v_ssm_b_glu, v_sgu_ln_g, v_sgu_ln_b, v_sgu_w, v_sgu_b, v_g_out_ssm, v_g_out_sgu, v_w_out, v_g_pre_ffn, v_g_post_ffn, v_w_up, v_conv_w, v_conv_b, v_w_down):
    given = dict(locals())
    drop = lambda a: a if a.ndim == 2 else a[0]
    p = {n: drop(given[n]) for n in _WEIGHTS}
    m = {n: drop(given["m_" + n]) for n in _WEIGHTS}
    v = {n: drop(given["v_" + n]) for n in _WEIGHTS}
    loss, dx, grads, deltas, new_m, new_v = _step(p, m, v, x[0], c, loss_target[0])
    outs = [loss, dx[None]]
    for group in (grads, deltas, new_m, new_v):
        outs += [group[n].reshape(given[n].shape) for n in _WEIGHTS]
    return tuple(outs)
```

```python
import functools
import math

import jax
import jax.numpy as jnp
from jax import lax
from jax.experimental import pallas as pl
from jax.experimental.pallas import tpu as pltpu

_F32 = jnp.float32
_MXU = jnp.bfloat16
_WIRE = jnp.bfloat16

EPS = 1e-6
SSM_GROUP = 16
SSM_STATE = 64
GROUPS_PER_BLOCK = 8
BLOCK_CH = SSM_GROUP * GROUPS_PER_BLOCK
BLOCK_ST = SSM_STATE * GROUPS_PER_BLOCK
CHUNK = 128
TIME_TILE = 512
SUBLANES = 8
LANES = 128
N_MOD = 6
ADAM_LR, ADAM_B1, ADAM_B2, ADAM_EPS, ADAM_WD, ADAM_STEP = 0.001, 0.9, 0.999, 1e-08, 0.01, 10
_VMEM_LIMIT = 56 * 1024 * 1024
_MESH = pl.DeviceIdType.MESH
_ANY = pl.BlockSpec(memory_space=pl.ANY)
_HBM = pl.BlockSpec(memory_space=pltpu.HBM)
_SEM = pl.BlockSpec(memory_space=pltpu.SEMAPHORE)
_VMEM_WHOLE = pl.BlockSpec(memory_space=pltpu.VMEM)
_EFFECT = pltpu.SideEffectType.DATAFLOW_SIDE_EFFECTING
_GELU_C = math.sqrt(2.0 / math.pi)


def _cp(*sem):
    return pltpu.CompilerParams(dimension_semantics=sem, vmem_limit_bytes=_VMEM_LIMIT)


def _tile(dim, target, align=LANES):
    if dim <= target:
        return dim
    best = None
    for t in range(align, target + 1, align):
        if dim % t == 0:
            best = t
    assert best is not None, (dim, target, align)
    return best


def _gelu(x):
    return 0.5 * x * (1.0 + jnp.tanh(_GELU_C * (x + 0.044715 * (x * x * x))))


def _gelu_grad(x):
    t = jnp.tanh(_GELU_C * (x + 0.044715 * (x * x * x)))
    return 0.5 * (1.0 + t) + 0.5 * x * (1.0 - t * t) * (_GELU_C * (1.0 + 3.0 * 0.044715 * x * x))


def _sigmoid(x):
    return 1.0 / (1.0 + jnp.exp(-x))


def _colsum(x):
    return jnp.sum(x, axis=0, keepdims=True)


def _rowmean(x):
    return jnp.mean(x, axis=-1, keepdims=True)


def _zero_first(first, *refs):
    @pl.when(first)
    def _():
        for ref in refs:
            ref[...] = jnp.zeros_like(ref)


def _acc(ref, first, val):
    del first
    ref[...] += val


def _place():
    mx, my, mc = lax.axis_index("x"), lax.axis_index("y"), lax.axis_index("c")
    chips = [(1 - mx, my), (mx, 1 - my), (1 - mx, 1 - my)]
    return mx, my, mc, chips


def _all_gather8_direct(buf, name, after=None):
    extra = [] if after is None else (list(after) if isinstance(after, (list, tuple)) else [after])
    flips = [(bx, by, bc) for bx in (0, 1) for by in (0, 1) for bc in (0, 1)][1:]

    def body(in_ref, *rest):
        out_ref, send_sems, recv_sems = rest[len(extra):]
        mx, my, mc, _ = _place()
        mine = 4 * mx + 2 * my + mc
        peers = [(1 - mx if bx else mx, 1 - my if by else my, 1 - mc if bc else mc) for bx, by, bc in flips]
        cps = []
        for k, peer in enumerate(peers):
            cp = pltpu.make_async_remote_copy(src_ref=in_ref.at[mine], dst_ref=out_ref.at[mine], send_sem=send_sems.at[k],
                                              recv_sem=recv_sems.at[k], device_id=peer, device_id_type=_MESH)
            cp.start()
            cps.append(cp)
        for k, (px, py, pc) in enumerate(peers):
            theirs = out_ref.at[4 * px + 2 * py + pc]
            pltpu.make_async_remote_copy(src_ref=theirs, dst_ref=theirs, send_sem=send_sems.at[k], recv_sem=recv_sems.at[k],
                                         device_id=(px, py, pc), device_id_type=_MESH).wait_recv()
        for cp in cps:
            cp.wait_send()

    return pl.pallas_call(
        body, name=name, out_shape=jax.ShapeDtypeStruct(buf.shape, buf.dtype),
        in_specs=[_ANY] * (1 + len(extra)), out_specs=_ANY, input_output_aliases={0: 0},
        scratch_shapes=[pltpu.SemaphoreType.DMA((7,)), pltpu.SemaphoreType.DMA((7,))],
    )(buf, *extra)


def _own_slot(x, dev):
    return lax.dynamic_update_slice(jnp.zeros((8,) + x.shape, x.dtype), x[None], (dev, 0, 0))


def _cast_into_slot(w, sel, after, name):
    r, c = w.shape
    hr = r // 2
    tr = _tile(hr, 256, 16)
    nr = hr // tr

    def body(sel_ref, w_ref, after_ref, o_ref):
        o_ref[...] = w_ref[...].astype(o_ref.dtype)

    return pl.pallas_call(
        body, name=name, out_shape=jax.ShapeDtypeStruct((4, 2, hr, c), _WIRE),
        grid_spec=pltpu.PrefetchScalarGridSpec(
            num_scalar_prefetch=1, grid=(2, nr),
            in_specs=[pl.BlockSpec((tr, c), lambda h, i, s: (h * nr + i, 0)), _ANY],
            out_specs=pl.BlockSpec((None, None, tr, c), lambda h, i, s: (s[0], h, i, 0))),
        compiler_params=_cp("parallel", "parallel"),
    )(sel, w, after)


def _hbm(a):
    return pltpu.with_memory_space_constraint(a, pltpu.HBM)


def _after(vec, token):
    return vec + token[0:1, 0:1]


def _gather_start(bufs, after, name):
    n = len(bufs)
    nc = 3 * n

    def body(*refs):
        ins, send, recv, token = refs[:n], refs[n + 1:n + 1 + nc], refs[n + 1 + nc:n + 1 + 2 * nc], refs[-1]
        mx, my, mc, chips = _place()
        j_me = 2 * mx + my
        for i in range(n):
            for k, chip in enumerate(chips):
                half = ins[i].at[j_me, mc]
                pltpu.make_async_remote_copy(
                    src_ref=half, dst_ref=half, send_sem=send[3 * i + k], recv_sem=recv[3 * i + k],
                    device_id=(*chip, mc), device_id_type=_MESH).start()
        token[...] = jnp.zeros_like(token)

    outs = pl.pallas_call(
        body, name=name,
        out_shape=tuple([pltpu.SemaphoreType.DMA(())] * (2 * nc) + [pltpu.HBM(b.shape, b.dtype) for b in bufs]
                        + [jax.ShapeDtypeStruct((SUBLANES, LANES), _F32)]),
        in_specs=tuple([_HBM] * n + [_ANY]), out_specs=tuple([_SEM] * (2 * nc) + [_HBM] * n + [_VMEM_WHOLE]),
        input_output_aliases={i: 2 * nc + i for i in range(n)},
        compiler_params=pltpu.CompilerParams(has_side_effects=_EFFECT),
    )(*[_hbm(b) for b in bufs], after)
    sems = [(outs[3 * i:3 * i + 3], outs[nc + 3 * i:nc + 3 * i + 3]) for i in range(n)]
    return sems, list(outs[2 * nc:2 * nc + n]), outs[-1]


def _gather_wait(sems, buf, after, name):
    send, recv = sems

    after = list(after) if isinstance(after, (list, tuple)) else [after]

    def body(buf_ref, s0, s1, s2, r0, r1, r2, *rest):
        mx, my, mc, chips = _place()
        j_me = 2 * mx + my
        for k, (chip, s_k, r_k) in enumerate(zip(chips, (s0, s1, s2), (r0, r1, r2))):
            cp = pltpu.make_async_remote_copy(
                src_ref=buf_ref.at[j_me, mc], dst_ref=buf_ref.at[2 * chip[0] + chip[1], mc], send_sem=s_k, recv_sem=r_k,
                device_id=(*chip, mc), device_id_type=_MESH)
            cp.wait_send()
            cp.wait_recv()

    return pl.pallas_call(
        body, name=name, out_shape=pltpu.HBM(buf.shape, buf.dtype),
        in_specs=(_HBM,) + (_SEM,) * 6 + (_ANY,) * len(after), out_specs=_HBM, input_output_aliases={0: 0},
        compiler_params=pltpu.CompilerParams(has_side_effects=_EFFECT),
    )(buf, *send, *recv, *after)


def _route_ends(buf_ref, phase):
    mx, my, mc, _ = _place()
    hq = buf_ref.shape[2] // 2
    xn, yn = (1 - mx, my), (mx, 1 - my)
    j_me, j_x, j_y, j_d = 2 * mx + my, 2 * (1 - mx) + my, 2 * mx + (1 - my), 2 * (1 - mx) + (1 - my)
    if phase == 1:
        mine = buf_ref.at[j_me, mc]
        return [((*xn, mc), mine, buf_ref.at[j_x, mc]), ((*yn, mc), mine, buf_ref.at[j_y, mc])]
    lo, hi = pl.ds(0, hq), pl.ds(hq, hq)
    return [((*xn, mc), buf_ref.at[j_y, mc, lo], buf_ref.at[j_d, mc, lo]),
            ((*yn, mc), buf_ref.at[j_x, mc, hi], buf_ref.at[j_d, mc, hi])]


def _route_start(items, after, name):
    n = len(items)

    def body(*refs):
        ins, send, recv, token = refs[:n], refs[n + 1:3 * n + 1], refs[3 * n + 1:5 * n + 1], refs[-1]
        for i, (_, phase) in enumerate(items):
            for k, (peer, src, _) in enumerate(_route_ends(ins[i], phase)):
                pltpu.make_async_remote_copy(src_ref=src, dst_ref=src, send_sem=send[2 * i + k], recv_sem=recv[2 * i + k],
                                             device_id=peer, device_id_type=_MESH).start()
        token[...] = jnp.zeros_like(token)

    bufs = [b for b, _ in items]
    outs = pl.pallas_call(
        body, name=name,
        out_shape=tuple([pltpu.SemaphoreType.DMA(())] * (4 * n) + [pltpu.HBM(b.shape, b.dtype) for b in bufs]
                        + [jax.ShapeDtypeStruct((SUBLANES, LANES), _F32)]),
        in_specs=tuple([_HBM] * n + [_ANY]), out_specs=tuple([_SEM] * (4 * n) + [_HBM] * n + [_VMEM_WHOLE]),
        input_output_aliases={i: 4 * n + i for i in range(n)},
        compiler_params=pltpu.CompilerParams(has_side_effects=_EFFECT),
    )(*[_hbm(b) for b in bufs], after)
    sems = [(outs[2 * i:2 * i + 2], outs[2 * n + 2 * i:2 * n + 2 * i + 2]) for i in range(n)]
    return sems, list(outs[4 * n:5 * n]), outs[-1]


def _route_wait(sems, buf, phase, after, name):
    send, recv = sems

    def body(buf_ref, s0, s1, r0, r1, after_ref, out_ref):
        for (peer, src, land), s_k, r_k in zip(_route_ends(buf_ref, phase), (s0, s1), (r0, r1)):
            cp = pltpu.make_async_remote_copy(src_ref=src, dst_ref=land, send_sem=s_k, recv_sem=r_k,
                                              device_id=peer, device_id_type=_MESH)
            cp.wait_send()
            cp.wait_recv()

    return pl.pallas_call(
        body, name=name, out_shape=pltpu.HBM(buf.shape, buf.dtype),
        in_specs=(_HBM,) + (_SEM,) * 4 + (_ANY,), out_specs=_HBM, input_output_aliases={0: 0},
        compiler_params=pltpu.CompilerParams(has_side_effects=_EFFECT),
    )(buf, *send, *recv, after)


def _pair_forward(bufs, name):
    n = len(bufs)

    def body(*refs):
        ins, outs = refs[:n], refs[n:2 * n]
        send_sems, recv_sems = refs[2 * n:]
        mx, my, mc, chips = _place()
        sibling = (mx, my, 1 - mc)
        cps = []
        for i in range(n):
            for k, chip in enumerate(chips):
                j_k = 2 * chip[0] + chip[1]
                cp = pltpu.make_async_remote_copy(
                    src_ref=ins[i].at[j_k, mc], dst_ref=outs[i].at[j_k, mc], send_sem=send_sems.at[3 * i + k],
                    recv_sem=recv_sems.at[3 * i + k], device_id=sibling, device_id_type=_MESH)
                cp.start()
                cps.append(cp)
        for i in range(n):
            for k, chip in enumerate(chips):
                other = outs[i].at[2 * chip[0] + chip[1], 1 - mc]
                pltpu.make_async_remote_copy(
                    src_ref=other, dst_ref=other, send_sem=send_sems.at[3 * i + k], recv_sem=recv_sems.at[3 * i + k],
                    device_id=sibling, device_id_type=_MESH).wait_recv()
        for cp in cps:
            cp.wait_send()

    return pl.pallas_call(
        body, name=name, out_shape=[jax.ShapeDtypeStruct(b.shape, b.dtype) for b in bufs],
        in_specs=[_ANY] * n, out_specs=[_ANY] * n, input_output_aliases={i: i for i in range(n)},
        scratch_shapes=[pltpu.SemaphoreType.DMA((3 * n,)), pltpu.SemaphoreType.DMA((3 * n,))],
    )(*bufs)


def _gather8_peers(buf_ref, mx, my, mc, chips):
    mine = buf_ref.at[4 * mx + 2 * my + mc]
    peers = [((mx, my, 1 - mc), mine, buf_ref.at[4 * mx + 2 * my + 1 - mc])]
    peers += [((*chip, mc), mine, buf_ref.at[4 * chip[0] + 2 * chip[1] + mc]) for chip in chips]
    return peers


def _gather8_start(buf, name):
    def body(buf_ref, *rest):
        send, recv, token = rest[0:4], rest[4:8], rest[-1]
        mx, my, mc, chips = _place()
        for k, (peer, src, _) in enumerate(_gather8_peers(buf_ref, mx, my, mc, chips)):
            pltpu.make_async_remote_copy(src_ref=src, dst_ref=src, send_sem=send[k], recv_sem=recv[k],
                                         device_id=peer, device_id_type=_MESH).start()
        token[...] = jnp.zeros_like(token)

    outs = pl.pallas_call(
        body, name=name,
        out_shape=tuple([pltpu.SemaphoreType.DMA(())] * 8 + [pltpu.HBM(buf.shape, buf.dtype),
                                                             jax.ShapeDtypeStruct((SUBLANES, LANES), _F32)]),
        in_specs=(_HBM,), out_specs=tuple([_SEM] * 8 + [_HBM, _VMEM_WHOLE]), input_output_aliases={0: 8},
        compiler_params=pltpu.CompilerParams(has_side_effects=_EFFECT),
    )(_hbm(buf))
    return (outs[0:4], outs[4:8]), outs[8], outs[9]


def _gather8_wait(sems, buf, after, name):
    send, recv = sems

    def body(buf_ref, s0, s1, s2, s3, r0, r1, r2, r3, after_ref, out_ref):
        mx, my, mc, chips = _place()
        for (peer, src, dst), s_k, r_k in zip(_gather8_peers(buf_ref, mx, my, mc, chips), (s0, s1, s2, s3), (r0, r1, r2, r3)):
            cp = pltpu.make_async_remote_copy(src_ref=src, dst_ref=dst, send_sem=s_k, recv_sem=r_k,
                                              device_id=peer, device_id_type=_MESH)
            cp.wait_send()
            cp.wait_recv()

    return pl.pallas_call(
        body, name=name, out_shape=pltpu.HBM(buf.shape, buf.dtype),
        in_specs=(_HBM,) + (_SEM,) * 8 + (_ANY,), out_specs=_HBM, input_output_aliases={0: 0},
        compiler_params=pltpu.CompilerParams(has_side_effects=_EFFECT),
    )(buf, *send, *recv, after)


def _gather8_forward(buf, name):
    def body(in_ref, out_ref, send_sems, recv_sems):
        mx, my, mc, chips = _place()
        sibling = (mx, my, 1 - mc)
        cps = []
        for k, chip in enumerate(chips):
            idx = 4 * chip[0] + 2 * chip[1] + mc
            cp = pltpu.make_async_remote_copy(src_ref=in_ref.at[idx], dst_ref=out_ref.at[idx], send_sem=send_sems.at[k],
                                              recv_sem=recv_sems.at[k], device_id=sibling, device_id_type=_MESH)
            cp.start()
            cps.append(cp)
        for k, chip in enumerate(chips):
            other = out_ref.at[4 * chip[0] + 2 * chip[1] + 1 - mc]
            pltpu.make_async_remote_copy(src_ref=other, dst_ref=other, send_sem=send_sems.at[k], recv_sem=recv_sems.at[k],
                                         device_id=sibling, device_id_type=_MESH).wait_recv()
        for cp in cps:
            cp.wait_send()

    return pl.pallas_call(
        body, name=name, out_shape=jax.ShapeDtypeStruct(buf.shape, buf.dtype),
        in_specs=[_ANY], out_specs=_ANY, input_output_aliases={0: 0},
        scratch_shapes=[pltpu.SemaphoreType.DMA((3,)), pltpu.SemaphoreType.DMA((3,))],
    )(buf)


def _scatter_start(pair, name):
    land = lax.empty((3,) + pair.shape[1:], pair.dtype)

    def body(pair_ref, land_ref, s0, s1, s2, r0, r1, r2, pair_thru, land_thru, token):
        mx, my, mc, chips = _place()
        for k, (chip, s_k, r_k) in enumerate(zip(chips, (s0, s1, s2), (r0, r1, r2))):
            pltpu.make_async_remote_copy(
                src_ref=pair_ref.at[2 * chip[0] + chip[1]], dst_ref=land_ref.at[k], send_sem=s_k, recv_sem=r_k,
                device_id=(*chip, mc), device_id_type=_MESH).start()
        token[...] = jnp.zeros_like(token)

    outs = pl.pallas_call(
        body, name=name,
        out_shape=tuple([pltpu.SemaphoreType.DMA(())] * 6 + [pltpu.HBM(pair.shape, pair.dtype), pltpu.HBM(land.shape, land.dtype),
                                                             jax.ShapeDtypeStruct((SUBLANES, LANES), _F32)]),
        in_specs=(_HBM, _HBM), out_specs=tuple([_SEM] * 6 + [_HBM, _HBM, _VMEM_WHOLE]),
        input_output_aliases={0: 6, 1: 7}, compiler_params=pltpu.CompilerParams(has_side_effects=_EFFECT),
    )(_hbm(pair), _hbm(land))
    return (outs[0:3], outs[3:6]), outs[6], outs[7], outs[8]


def _scatter_wait(sems, pair, land, after, name):
    send, recv = sems

    def body(pair_ref, land_ref, s0, s1, s2, r0, r1, r2, after_ref, pair_out, land_out):
        mx, my, mc, chips = _place()
        for k, (chip, s_k, r_k) in enumerate(zip(chips, (s0, s1, s2), (r0, r1, r2))):
            cp = pltpu.make_async_remote_copy(
                src_ref=pair_ref.at[2 * chip[0] + chip[1]], dst_ref=land_ref.at[k], send_sem=s_k, recv_sem=r_k,
                device_id=(*chip, mc), device_id_type=_MESH)
            cp.wait_send()
            cp.wait_recv()

    return pl.pallas_call(
        body, name=name, out_shape=(pltpu.HBM(pair.shape, pair.dtype), pltpu.HBM(land.shape, land.dtype)),
        in_specs=(_HBM, _HBM) + (_SEM,) * 6 + (_ANY,), out_specs=(_HBM, _HBM), input_output_aliases={0: 0, 1: 1},
        compiler_params=pltpu.CompilerParams(has_side_effects=_EFFECT),
    )(pair, land, *send, *recv, after)


def _sibling_copy(src_ref, dst_ref, send_sem, recv_sem):
    mx, my, mc, _ = _place()
    return pltpu.make_async_remote_copy(src_ref=src_ref, dst_ref=dst_ref, send_sem=send_sem, recv_sem=recv_sem,
                                        device_id=(mx, my, 1 - mc), device_id_type=_MESH)


def _swap_start(g, name):
    land = lax.empty(g.shape[1:], g.dtype)

    def body(g_ref, land_ref, send_sem, recv_sem, g_thru, land_thru, token):
        _sibling_copy(g_ref.at[1 - lax.axis_index("c")], land_ref, send_sem, recv_sem).start()
        token[...] = jnp.zeros_like(token)

    outs = pl.pallas_call(
        body, name=name,
        out_shape=(pltpu.SemaphoreType.DMA(()), pltpu.SemaphoreType.DMA(()), pltpu.HBM(g.shape, g.dtype),
                   pltpu.HBM(land.shape, land.dtype), jax.ShapeDtypeStruct((SUBLANES, LANES), _F32)),
        in_specs=(_HBM, _HBM), out_specs=(_SEM, _SEM, _HBM, _HBM, _VMEM_WHOLE), input_output_aliases={0: 2, 1: 3},
        compiler_params=pltpu.CompilerParams(has_side_effects=_EFFECT),
    )(_hbm(g), _hbm(land))
    return (outs[0], outs[1]), outs[2], outs[3], outs[4]


def _swap_wait(sems, g, land, after, name):
    def body(g_ref, land_ref, send_sem, recv_sem, after_ref, g_out, land_out):
        cp = _sibling_copy(g_ref.at[1 - lax.axis_index("c")], land_ref, send_sem, recv_sem)
        cp.wait_send()
        cp.wait_recv()

    return pl.pallas_call(
        body, name=name, out_shape=(pltpu.HBM(g.shape, g.dtype), pltpu.HBM(land.shape, land.dtype)),
        in_specs=(_HBM, _HBM, _SEM, _SEM, _ANY), out_specs=(_HBM, _HBM), input_output_aliases={0: 0, 1: 1},
        compiler_params=pltpu.CompilerParams(has_side_effects=_EFFECT),
    )(g, land, *sems, after)


def _join_start(buf, name):
    def body(buf_ref, send_sem, recv_sem, buf_thru, token):
        mine = buf_ref.at[lax.axis_index("c")]
        _sibling_copy(mine, mine, send_sem, recv_sem).start()
        token[...] = jnp.zeros_like(token)

    outs = pl.pallas_call(
        body, name=name,
        out_shape=(pltpu.SemaphoreType.DMA(()), pltpu.SemaphoreType.DMA(()), pltpu.HBM(buf.shape, buf.dtype),
                   jax.ShapeDtypeStruct((SUBLANES, LANES), _F32)),
        in_specs=(_HBM,), out_specs=(_SEM, _SEM, _HBM, _VMEM_WHOLE), input_output_aliases={0: 2},
        compiler_params=pltpu.CompilerParams(has_side_effects=_EFFECT),
    )(_hbm(buf))
    return (outs[0], outs[1]), outs[2], outs[3]


def _join_wait(sems, buf, after, name):
    def body(buf_ref, send_sem, recv_sem, after_ref, buf_out):
        mc = lax.axis_index("c")
        cp = _sibling_copy(buf_ref.at[mc], buf_ref.at[1 - mc], send_sem, recv_sem)
        cp.wait_send()
        cp.wait_recv()

    return pl.pallas_call(
        body, name=name, out_shape=pltpu.HBM(buf.shape, buf.dtype),
        in_specs=(_HBM, _SEM, _SEM, _ANY), out_specs=_HBM, input_output_aliases={0: 0},
        compiler_params=pltpu.CompilerParams(has_side_effects=_EFFECT),
    )(buf, *sems, after)


def _pair_sum(g, got, sel, name):
    _, four, hr, c = g.shape
    tr = _tile(hr, 512, 16)

    def body(sel_ref, g_ref, p_ref, o_ref):
        o_ref[...] = (g_ref[...].astype(_F32) + p_ref[...].astype(_F32)).astype(o_ref.dtype)

    return pl.pallas_call(
        body, name=name, out_shape=jax.ShapeDtypeStruct((four, hr, c), g.dtype),
        grid_spec=pltpu.PrefetchScalarGridSpec(
            num_scalar_prefetch=1, grid=(four, hr // tr),
            in_specs=[pl.BlockSpec((None, None, tr, c), lambda j, i, s: (s[1], j, i, 0)),
                      pl.BlockSpec((None, tr, c), lambda j, i, s: (j, i, 0))],
            out_specs=pl.BlockSpec((None, tr, c), lambda j, i, s: (j, i, 0))),
        compiler_params=_cp("parallel", "parallel"),
    )(sel, g, got)


def _chip_sum(pair, got, sel, name):
    _, hr, c = pair.shape
    tr = _tile(hr, 512, 16)

    def body(sel_ref, p_ref, q_ref, o_ref):
        o_ref[...] = ((p_ref[...].astype(_F32) + q_ref[0].astype(_F32)) + q_ref[1].astype(_F32)) + q_ref[2].astype(_F32)

    return pl.pallas_call(
        body, name=name, out_shape=jax.ShapeDtypeStruct((2, hr, c), _F32),
        grid_spec=pltpu.PrefetchScalarGridSpec(
            num_scalar_prefetch=1, grid=(hr // tr,),
            in_specs=[pl.BlockSpec((None, tr, c), lambda i, s: (s[0], i, 0)),
                      pl.BlockSpec((3, tr, c), lambda i, s: (0, i, 0))],
            out_specs=pl.BlockSpec((None, tr, c), lambda i, s: (s[1], i, 0))),
        compiler_params=_cp("parallel"),
    )(sel, pair, got)


def _matmul(a, b, dims, out_struct, grid, a_spec, b_spec, o_spec, acc_shape, k_axis, name, after=None):
    nk = grid[k_axis]
    extra = [] if after is None else [after]

    def body(a_ref, b_ref, *rest):
        o_ref, acc = rest[len(extra)], rest[len(extra) + 1:]
        prod = lax.dot_general(a_ref[...].astype(_MXU), b_ref[...].astype(_MXU), dims, preferred_element_type=_F32)
        if nk == 1:
            o_ref[...] = prod.astype(o_ref.dtype)
        else:
            acc_ref, = acc
            k = pl.program_id(k_axis)
            _zero_first(k == 0, acc_ref)
            acc_ref[...] += prod

            @pl.when(k == nk - 1)
            def _():
                o_ref[...] = acc_ref[...].astype(o_ref.dtype)

    sem = ["parallel"] * len(grid)
    sem[k_axis] = "arbitrary"
    return pl.pallas_call(
        body, name=name, out_shape=out_struct, grid=grid, in_specs=[a_spec, b_spec] + [_ANY] * len(extra), out_specs=o_spec,
        scratch_shapes=[pltpu.VMEM(acc_shape, _F32)] if nk > 1 else [], compiler_params=_cp(*sem),
    )(a, b, *extra)


def _mm_nn(a, w4, out_dtype, name, tm=512, tn=1536, tk=2048, after=None):
    m, k = a.shape
    j, _, ns = w4.shape
    tm, tn, tk = _tile(m, tm, 16), _tile(ns, tn), _tile(k, tk)
    nps = ns // tn
    return _matmul(
        a, w4, (((1,), (0,)), ((), ())), jax.ShapeDtypeStruct((m, j * ns), out_dtype),
        (j * nps, m // tm, k // tk),
        pl.BlockSpec((tm, tk), lambda ni, mi, ki: (mi, ki)),
        pl.BlockSpec((None, tk, tn), lambda ni, mi, ki: (ni // nps, ki, ni % nps)),
        pl.BlockSpec((tm, tn), lambda ni, mi, ki: (mi, ni)), (tm, tn), 2, name, after)


def _mm_nt(a, w4, out_dtype, name, tm=512, tn=2048, tk=1536, after=None):
    m = a.shape[-2]
    j, kw, ns = w4.shape
    tm, tn, tk = _tile(m, tm, 16), _tile(kw, tn), _tile(ns, tk)
    kps = ns // tk
    if a.ndim == 3:
        kph = a.shape[2] // tk
        a_spec = pl.BlockSpec((None, tm, tk), lambda ni, mi, ki: (ki // kph, mi, ki % kph))
    else:
        a_spec = pl.BlockSpec((tm, tk), lambda ni, mi, ki: (mi, ki))
    return _matmul(
        a, w4, (((1,), (1,)), ((), ())), jax.ShapeDtypeStruct((m, kw), out_dtype),
        (kw // tn, m // tm, j * kps),
        a_spec,
        pl.BlockSpec((None, tn, tk), lambda ni, mi, ki: (ki // kps, ni, ki % kps)),
        pl.BlockSpec((tm, tn), lambda ni, mi, ki: (mi, ni)), (tm, tn), 2, name, after)


def _mm_tn_cols(a, b, name, tm=1024, tn=1536, tk=2048):
    m, ka = a.shape
    ns = (b.shape[-1] * (2 if b.ndim == 3 else 1)) // 4
    hr = ka // 2
    tm, tn, tk = _tile(hr, tm), _tile(ns, tn), _tile(m, tk, 16)
    mph, nps = hr // tm, ns // tn
    if b.ndim == 3:
        b_spec = pl.BlockSpec((None, tk, tn), lambda ni, mi, ki: (ni // (2 * nps), ki, ni % (2 * nps)))
    else:
        b_spec = pl.BlockSpec((tk, tn), lambda ni, mi, ki: (ki, ni))
    return _matmul(
        a, b, (((0,), (0,)), ((), ())), jax.ShapeDtypeStruct((2, 4, hr, ns), _WIRE),
        (4 * nps, 2 * mph, m // tk),
        pl.BlockSpec((tk, tm), lambda ni, mi, ki: (ki, mi)),
        b_spec,
        pl.BlockSpec((None, None, tm, tn), lambda ni, mi, ki: (mi // mph, ni // nps, mi % mph, ni % nps)),
        (tm, tn), 2, name)


def _mm_tn_rows(a, b, name, tm=1536, tn=1024, tk=2048):
    m, ka = a.shape
    r = ka // 4
    hc = b.shape[1] // 2
    tm, tn, tk = _tile(r, tm), _tile(hc, tn), _tile(m, tk, 16)
    mpr, nph = r // tm, hc // tn
    return _matmul(
        a, b, (((0,), (0,)), ((), ())), jax.ShapeDtypeStruct((2, 4, r, hc), _WIRE),
        (2 * nph, 4 * mpr, m // tk),
        pl.BlockSpec((tk, tm), lambda ni, mi, ki: (ki, mi)),
        pl.BlockSpec((tk, tn), lambda ni, mi, ki: (ki, ni)),
        pl.BlockSpec((None, None, tm, tn), lambda ni, mi, ki: (ni // nph, mi // mpr, mi % mpr, ni % nph)),
        (tm, tn), 2, name)


def _row_call(body, name, rows, ins, outs, tm=256):
    tm = _tile(rows, tm, 16)

    def spec(shape, kind):
        if kind == "rows":
            return pl.BlockSpec((tm, shape[1]), lambda i: (i, 0))
        return pl.BlockSpec(shape, lambda i: (0,) * len(shape))

    return pl.pallas_call(
        body, name=name, grid=(rows // tm,),
        in_specs=[spec(a.shape, kind) for a, kind in ins],
        out_specs=[spec(o.shape, kind) for o, kind in outs],
        out_shape=[o for o, _ in outs],
        compiler_params=_cp("arbitrary"),
    )(*[a for a, _ in ins])


def _rms(x):
    r = lax.rsqrt(_rowmean(x * x) + EPS)
    return x * r, r


def _rms_bwd(dxh, xh, r):
    return r * (dxh - xh * _rowmean(dxh * xh))


def _fwd_pre_mix(x, g, sc, sh):
    s, d = x.shape

    def body(x_ref, g_ref, sc_ref, sh_ref, h_ref):
        xh, _ = _rms(x_ref[...])
        h_ref[...] = (xh * g_ref[...] * (1.0 + sc_ref[...]) + sh_ref[...]).astype(h_ref.dtype)

    return _row_call(body, "fwd_pre_mix", s, [(x, "rows"), (g, "vec"), (sc, "vec"), (sh, "vec")],
                     [(jax.ShapeDtypeStruct((s, d), _MXU), "rows")])[0]


def _fwd_mid(o, x, gt1, g_post, g_pre2, sc2, sh2):
    s, d = x.shape

    def body(o_ref, x_ref, gt_ref, gp_ref, g2_ref, sc_ref, sh_ref, x1_ref, h2_ref):
        oh, _ = _rms(o_ref[...])
        x1 = x_ref[...] + gt_ref[...] * (oh * gp_ref[...])
        x1_ref[...] = x1
        xh, _ = _rms(x1)
        h2_ref[...] = (xh * g2_ref[...] * (1.0 + sc_ref[...]) + sh_ref[...]).astype(h2_ref.dtype)

    return _row_call(body, "fwd_mid", s,
                     [(o, "rows"), (x, "rows"), (gt1, "vec"), (g_post, "vec"), (g_pre2, "vec"), (sc2, "vec"),
                      (sh2, "vec")],
                     [(jax.ShapeDtypeStruct((s, d), _F32), "rows"), (jax.ShapeDtypeStruct((s, d), _MXU), "rows")])


def _loss_and_post_ffn_bwd(f, x1, tgt, gt2, g_post):
    s, d = x1.shape

    def body(f_ref, x1_ref, t_ref, gt_ref, g_ref, dx2_ref, df_ref, dgt_ref, dg_ref, loss_ref):
        first = pl.program_id(0) == 0
        _zero_first(first, dgt_ref, dg_ref, loss_ref)
        fh, r = _rms(f_ref[...])
        n = fh * g_ref[...]
        e = x1_ref[...] + gt_ref[...] * n - t_ref[...]
        _acc(loss_ref, first, jnp.sum(_colsum(e * e), axis=1, keepdims=True) * (0.5 / d))
        dx2 = e * (1.0 / d)
        dx2_ref[...] = dx2
        _acc(dgt_ref, first, _colsum(dx2 * n))
        dn = dx2 * gt_ref[...]
        _acc(dg_ref, first, _colsum(dn * fh))
        df_ref[...] = _rms_bwd(dn * g_ref[...], fh, r).astype(df_ref.dtype)

    vec = jax.ShapeDtypeStruct((1, d), _F32)
    return _row_call(body, "loss_post_ffn_bwd", s,
                     [(f, "rows"), (x1, "rows"), (tgt, "rows"), (gt2, "vec"), (g_post, "vec")],
                     [(jax.ShapeDtypeStruct((s, d), _F32), "rows"), (jax.ShapeDtypeStruct((s, d), _MXU), "rows"),
                      (vec, "vec"), (vec, "vec"), (jax.ShapeDtypeStruct((1, 1), _F32), "vec")])


def _bwd_mid(dh2, x1, dx2, o, g_pre2, sc2, gt1, g_post):
    s, d = x1.shape

    def body(dh_ref, x1_ref, dx2_ref, o_ref, g2_ref, sc_ref, gt_ref, gp_ref,
             dx1_ref, do_ref, dsc_ref, dsh_ref, dg2_ref, dgt_ref, dgp_ref):
        first = pl.program_id(0) == 0
        _zero_first(first, dsc_ref, dsh_ref, dg2_ref, dgt_ref, dgp_ref)
        dh = dh_ref[...]
        xh, r = _rms(x1_ref[...])
        _acc(dsh_ref, first, _colsum(dh))
        _acc(dsc_ref, first, _colsum(dh * (xh * g2_ref[...])))
        dn = dh * (1.0 + sc_ref[...])
        _acc(dg2_ref, first, _colsum(dn * xh))
        dx1 = dx2_ref[...] + _rms_bwd(dn * g2_ref[...], xh, r)
        dx1_ref[...] = dx1
        oh, ro = _rms(o_ref[...])
        _acc(dgt_ref, first, _colsum(dx1 * (oh * gp_ref[...])))
        dno = dx1 * gt_ref[...]
        _acc(dgp_ref, first, _colsum(dno * oh))
        do_ref[...] = _rms_bwd(dno * gp_ref[...], oh, ro).astype(do_ref.dtype)

    vec = jax.ShapeDtypeStruct((1, d), _F32)
    return _row_call(body, "bwd_mid", s,
                     [(dh2, "rows"), (x1, "rows"), (dx2, "rows"), (o, "rows"), (g_pre2, "vec"), (sc2, "vec"),
                      (gt1, "vec"), (g_post, "vec")],
                     [(jax.ShapeDtypeStruct((s, d), _F32), "rows"), (jax.ShapeDtypeStruct((s, d), _MXU), "rows"),
                      (vec, "vec"), (vec, "vec"), (vec, "vec"), (vec, "vec"), (vec, "vec")])


def _bwd_pre_mix(dh1, x, dx1, g, sc1):
    s, d = x.shape

    def body(dh_ref, x_ref, dx1_ref, g_ref, sc_ref, dx_ref, dsc_ref, dsh_ref, dg_ref):
        first = pl.program_id(0) == 0
        _zero_first(first, dsc_ref, dsh_ref, dg_ref)
        dh = dh_ref[...]
        xh, r = _rms(x_ref[...])
        _acc(dsh_ref, first, _colsum(dh))
        _acc(dsc_ref, first, _colsum(dh * (xh * g_ref[...])))
        dn = dh * (1.0 + sc_ref[...])
        _acc(dg_ref, first, _colsum(dn * xh))
        dx_ref[...] = dx1_ref[...] + _rms_bwd(dn * g_ref[...], xh, r)

    vec = jax.ShapeDtypeStruct((1, d), _F32)
    return _row_call(body, "bwd_pre_mix", s,
                     [(dh1, "rows"), (x, "rows"), (dx1, "rows"), (g, "vec"), (sc1, "vec")],
                     [(jax.ShapeDtypeStruct((s, d), _F32), "rows"), (vec, "vec"), (vec, "vec"), (vec, "vec")])


def _mix_norm_fwd(y_ssm, y_sgu, g_ssm, g_sgu):
    s, h = y_ssm.shape

    def body(a_ref, b_ref, ga_ref, gb_ref, o_ref):
        ah, _ = _rms(a_ref[...])
        bh, _ = _rms(b_ref[...])
        o_ref[:, 0:h] = (ah * ga_ref[...]).astype(o_ref.dtype)
        o_ref[:, h:2 * h] = (bh * gb_ref[...]).astype(o_ref.dtype)

    return _row_call(body, "mix_norm_fwd", s, [(y_ssm, "rows"), (y_sgu, "rows"), (g_ssm, "vec"), (g_sgu, "vec")],
                     [(jax.ShapeDtypeStruct((s, 2 * h), _MXU), "rows")])[0]


def _mix_norm_bwd(dyc, y_ssm, y_sgu, g_ssm, g_sgu):
    s, h = y_ssm.shape

    def body(d_ref, a_ref, b_ref, ga_ref, gb_ref, da_ref, db_ref, dga_ref, dgb_ref):
        first = pl.program_id(0) == 0
        _zero_first(first, dga_ref, dgb_ref)
        for lo, y_ref, g_ref, dy_ref, dg_ref in ((0, a_ref, ga_ref, da_ref, dga_ref), (h, b_ref, gb_ref, db_ref, dgb_ref)):
            d = d_ref[:, lo:lo + h]
            yh, r = _rms(y_ref[...])
            _acc(dg_ref, first, _colsum(d * yh))
            dy_ref[...] = _rms_bwd(d * g_ref[...], yh, r)

    vec = jax.ShapeDtypeStruct((1, h), _F32)
    full = jax.ShapeDtypeStruct((s, h), _F32)
    return _row_call(body, "mix_norm_bwd", s,
                     [(dyc, "rows"), (y_ssm, "rows"), (y_sgu, "rows"), (g_ssm, "vec"), (g_sgu, "vec")],
                     [(full, "rows"), (full, "rows"), (vec, "vec"), (vec, "vec")])


CONV_ROWS = 64


def _conv_rows(ext, w_ref, b_ref):
    x = ext[SUBLANES:]
    s1 = pltpu.roll(ext, 1, 0)[SUBLANES:]
    s2 = pltpu.roll(ext, 2, 0)[SUBLANES:]
    return b_ref[...] + w_ref[0:1, :] * s2 + w_ref[1:2, :] * s1 + w_ref[2:3, :] * x, x, s1, s2


def _conv_window(x_ref, r0):
    if isinstance(r0, int):
        assert r0 == 0
        return jnp.concatenate([jnp.zeros((SUBLANES, x_ref.shape[1]), _F32), x_ref[0:CONV_ROWS, :]], axis=0)
    return x_ref[pl.ds(pl.multiple_of(r0 - SUBLANES, SUBLANES), CONV_ROWS + SUBLANES), :]


def _conv_act_fwd(up_pre, conv_w, conv_b):
    s, f2 = up_pre.shape
    f = f2 // 2
    tc = _tile(f, 256)
    nf = f // tc

    def shift_down(x, k):
        row = lax.broadcasted_iota(jnp.int32, x.shape, 0)
        return jnp.where(row >= k, pltpu.roll(x, k, 0), 0.0)

    def conv(x, w_ref, b_ref):
        return b_ref[...] + w_ref[0:1, :] * shift_down(x, 2) + w_ref[1:2, :] * shift_down(x, 1) + w_ref[2:3, :] * x

    def body(a_ref, b_ref, wa_ref, wb_ref, ba_ref, bb_ref, o_ref):
        a = conv(a_ref[...], wa_ref, ba_ref)
        b = conv(b_ref[...], wb_ref, bb_ref)
        o_ref[...] = (a * _sigmoid(a) * b).astype(o_ref.dtype)

    return pl.pallas_call(
        body, name="conv_act_fwd", grid=(nf,), out_shape=jax.ShapeDtypeStruct((s, f), _MXU),
        in_specs=[pl.BlockSpec((s, tc), lambda n: (0, n)), pl.BlockSpec((s, tc), lambda n: (0, n + nf)),
                  pl.BlockSpec((3, tc), lambda n: (0, n)), pl.BlockSpec((3, tc), lambda n: (0, n + nf)),
                  pl.BlockSpec((1, tc), lambda n: (0, n)), pl.BlockSpec((1, tc), lambda n: (0, n + nf))],
        out_specs=pl.BlockSpec((s, tc), lambda n: (0, n)), compiler_params=_cp("parallel"),
    )(up_pre, up_pre, conv_w, conv_w, conv_b, conv_b)


def _conv_act_bwd(up_pre, d_act, conv_w, conv_b):
    s, f2 = up_pre.shape
    f = f2 // 2
    tc = _tile(f, 256)
    nf = f // tc

    def body(a_ref, b_ref, d_ref, wa_ref, wb_ref, ba_ref, bb_ref,
             du_ref, w0a, w0b, w1a, w1b, w2a, w2b, dba, dbb):
        n = s // CONV_ROWS
        zero8 = jnp.zeros((SUBLANES, tc), _F32)
        ext_rows = CONV_ROWS + SUBLANES

        def fold(x):
            out = x[0:SUBLANES]
            for k in range(1, CONV_ROWS // SUBLANES):
                out = out + x[k * SUBLANES:(k + 1) * SUBLANES]
            return out

        def chunk(r0, carry):
            nxt, acc = carry
            a, xa, xa1, xa2 = _conv_rows(_conv_window(a_ref, r0), wa_ref, ba_ref)
            b, xb, xb1, xb2 = _conv_rows(_conv_window(b_ref, r0), wb_ref, bb_ref)
            sg = _sigmoid(a)
            d = d_ref[pl.ds(r0, CONV_ROWS), :]
            du_a = d * b * (sg * (1.0 + a * (1.0 - sg)))
            du_b = d * (a * sg)
            new_acc = []
            for h, (du, x0, x1, x2, w_ref) in enumerate(((du_a, xa, xa1, xa2, wa_ref), (du_b, xb, xb1, xb2, wb_ref))):
                ext = jnp.concatenate([du, nxt[h]], axis=0)
                u1 = pltpu.roll(ext, ext_rows - 1, 0)[:CONV_ROWS]
                u2 = pltpu.roll(ext, ext_rows - 2, 0)[:CONV_ROWS]
                du_ref[h, pl.ds(r0, CONV_ROWS), :] = (w_ref[2:3, :] * du + w_ref[1:2, :] * u1
                                                      + w_ref[0:1, :] * u2).astype(du_ref.dtype)
                new_acc += [acc[4 * h] + fold(du * x2), acc[4 * h + 1] + fold(du * x1), acc[4 * h + 2] + fold(du * x0),
                            acc[4 * h + 3] + fold(du)]
            return (du_a[:SUBLANES], du_b[:SUBLANES]), tuple(new_acc)

        def step(i, carry):
            return chunk(pl.multiple_of((n - 1 - i) * CONV_ROWS, CONV_ROWS), carry)

        carry = lax.fori_loop(0, n - 1, step, ((zero8, zero8), (zero8,) * 8))
        _, acc = chunk(0, carry)
        for ref, val in zip((w0a, w1a, w2a, dba, w0b, w1b, w2b, dbb), acc):
            ref[...] = _colsum(val)

    col_a = pl.BlockSpec((s, tc), lambda n: (0, n))
    col_b = pl.BlockSpec((s, tc), lambda n: (0, n + nf))
    vec_a = pl.BlockSpec((1, tc), lambda n: (0, n))
    vec_b = pl.BlockSpec((1, tc), lambda n: (0, n + nf))
    vec = jax.ShapeDtypeStruct((1, f), _F32)
    outs = pl.pallas_call(
        body, name="conv_act_bwd", grid=(nf,),
        in_specs=[col_a, col_b, col_a, pl.BlockSpec((3, tc), lambda n: (0, n)),
                  pl.BlockSpec((3, tc), lambda n: (0, n + nf)), vec_a, vec_b],
        out_specs=[pl.BlockSpec((2, s, tc), lambda n: (0, 0, n))] + [vec_a] * 8,
        out_shape=[jax.ShapeDtypeStruct((2, s, f), _MXU)] + [vec] * 8, compiler_params=_cp("parallel"),
    )(up_pre, up_pre, d_act, conv_w, conv_w, conv_b, conv_b)
    du, w0a, w0b, w1a, w1b, w2a, w2b, dba, dbb = outs
    cat = lambda p, q: jnp.concatenate([p, q], axis=1)
    return du, cat(w0a, w0b), cat(w1a, w1b), cat(w2a, w2b), cat(dba, dbb)


def _sgu_recompute(zu_ref, zv_ref, lng_ref, lnb_ref, wm_ref, bs_ref, nh):
    zu, zv = zu_ref[...], zv_ref[...]
    u = _gelu(zu)
    gv = _gelu(zv)
    xc = gv - _rowmean(gv)
    rs = lax.rsqrt(_rowmean(xc * xc) + EPS)
    vh = xc * rs
    v = vh * lng_ref[...] + lnb_ref[...]
    mixed = []
    for h in range(nh):
        vhd = v[:, h * CHUNK:(h + 1) * CHUNK].astype(_MXU)
        mixed.append(jnp.dot(wm_ref[h].astype(_MXU), vhd, preferred_element_type=_F32) + bs_ref[h])
    return zu, zv, u, vh, rs, v, mixed


def _sgu_fwd(z, ln_g, ln_b, wm, bs):
    s = z.shape[0]
    nh = wm.shape[0]
    hd = nh * CHUNK

    def body(zu_ref, zv_ref, lng_ref, lnb_ref, wm_ref, bs_ref, y_ref):
        _, _, u, _, _, _, mixed = _sgu_recompute(zu_ref, zv_ref, lng_ref, lnb_ref, wm_ref, bs_ref, nh)
        for h in range(nh):
            y_ref[:, h * CHUNK:(h + 1) * CHUNK] = u[:, h * CHUNK:(h + 1) * CHUNK] * mixed[h]

    vec = pl.BlockSpec((1, hd), lambda i: (0, 0))
    return pl.pallas_call(
        body, name="sgu_fwd", grid=(s // CHUNK,), out_shape=jax.ShapeDtypeStruct((s, hd), _F32),
        in_specs=[pl.BlockSpec((CHUNK, hd), lambda i: (i, 1)), pl.BlockSpec((CHUNK, hd), lambda i: (i, 2)), vec, vec,
                  pl.BlockSpec((nh, CHUNK, CHUNK), lambda i: (0, 0, 0)), pl.BlockSpec((nh, CHUNK, 1), lambda i: (0, 0, 0))],
        out_specs=pl.BlockSpec((CHUNK, hd), lambda i: (i, 0)), compiler_params=_cp("parallel"),
    )(z, z, ln_g, ln_b, wm, bs)


def _sgu_bwd(z, dy, dz_ssm, ln_g, ln_b, wm, bs):
    s = z.shape[0]
    nh = wm.shape[0]
    hd = nh * CHUNK

    def body(zu_ref, zv_ref, dy_ref, dzs_ref, lng_ref, lnb_ref, wm_ref, bs_ref,
             dz_ref, dlg_ref, dlb_ref, dwm_ref, dbs_ref, dv_scr):
        first = pl.program_id(0) == 0
        _zero_first(first, dlg_ref, dlb_ref, dwm_ref, dbs_ref)
        zu, zv, u, vh, rs, v, mixed = _sgu_recompute(zu_ref, zv_ref, lng_ref, lnb_ref, wm_ref, bs_ref, nh)
        dy = dy_ref[...]
        dz_ref[:, 0:hd] = dzs_ref[...].astype(dz_ref.dtype)
        for h in range(nh):
            cols = slice(h * CHUNK, (h + 1) * CHUNK)
            dyh = dy[:, cols]
            dz_ref[:, hd + h * CHUNK:hd + (h + 1) * CHUNK] = (dyh * mixed[h] * _gelu_grad(zu[:, cols])).astype(dz_ref.dtype)
            dm = dyh * u[:, cols]
            dmx = dm.astype(_MXU)
            _acc(dbs_ref.at[h], first, jnp.sum(dm, axis=1, keepdims=True))
            _acc(dwm_ref.at[h], first,
                 lax.dot_general(dmx, v[:, cols].astype(_MXU), (((1,), (1,)), ((), ())), preferred_element_type=_F32))
            dv_scr[:, cols] = lax.dot_general(wm_ref[h].astype(_MXU), dmx, (((0,), (0,)), ((), ())),
                                              preferred_element_type=_F32)
        dv = dv_scr[...]
        _acc(dlg_ref, first, _colsum(dv * vh))
        _acc(dlb_ref, first, _colsum(dv))
        dvh = dv * lng_ref[...]
        dgv = rs * (dvh - _rowmean(dvh) - vh * _rowmean(dvh * vh))
        dz_ref[:, 2 * hd:3 * hd] = (dgv * _gelu_grad(zv)).astype(dz_ref.dtype)

    vec = pl.BlockSpec((1, hd), lambda i: (0, 0))
    wspec = pl.BlockSpec((nh, CHUNK, CHUNK), lambda i: (0, 0, 0))
    bspec = pl.BlockSpec((nh, CHUNK, 1), lambda i: (0, 0, 0))
    rows = pl.BlockSpec((CHUNK, hd), lambda i: (i, 0))
    return pl.pallas_call(
        body, name="sgu_bwd", grid=(s // CHUNK,),
        out_shape=[jax.ShapeDtypeStruct((s, 3 * hd), _MXU), jax.ShapeDtypeStruct((1, hd), _F32),
                   jax.ShapeDtypeStruct((1, hd), _F32), jax.ShapeDtypeStruct((nh, CHUNK, CHUNK), _F32),
                   jax.ShapeDtypeStruct((nh, CHUNK, 1), _F32)],
        in_specs=[pl.BlockSpec((CHUNK, hd), lambda i: (i, 1)), pl.BlockSpec((CHUNK, hd), lambda i: (i, 2)),
                  rows, rows, vec, vec, wspec, bspec],
        out_specs=[pl.BlockSpec((CHUNK, 3 * hd), lambda i: (i, 0)), vec, vec, wspec, bspec],
        scratch_shapes=[pltpu.VMEM((CHUNK, hd), _F32)], compiler_params=_cp("arbitrary"),
    )(z, z, dy, dz_ssm, ln_g, ln_b, wm, bs)


def _ssm_prep(log_dt, a_re, a_im, b_re_t, b_im_t, kvec):
    gn = a_re.shape[1]

    def body(ldt_ref, are_ref, aim_ref, br_ref, bi_ref, k_ref, pr_ref, pi_ref, bbr_ref, bbi_ref):
        dt = jnp.exp(ldt_ref[...])
        are, aim = are_ref[...], aim_ref[...]
        k = k_ref[...]
        mag = jnp.exp(k * (are * dt))
        ang = k * (aim * dt)
        pr_ref[...] = mag * jnp.cos(ang)
        pi_ref[...] = mag * jnp.sin(ang)
        m1 = jnp.exp(are * dt)
        lr, li = m1 * jnp.cos(aim * dt), m1 * jnp.sin(aim * dt)
        den = are * are + aim * aim
        nr = lr - 1.0
        f_re = (nr * are + li * aim) / den
        f_im = (li * are - nr * aim) / den
        bbr_ref[...] = f_re * br_ref[...] - f_im * bi_ref[...]
        bbi_ref[...] = f_re * bi_ref[...] + f_im * br_ref[...]

    pw = jax.ShapeDtypeStruct((kvec.shape[0], gn), _F32)
    bb = jax.ShapeDtypeStruct(b_re_t.shape, _F32)
    return pl.pallas_call(body, name="ssm_prep", out_shape=[pw, pw, bb, bb])(log_dt, a_re, a_im, b_re_t, b_im_t, kvec)


def _ssm_prep_bwd(log_dt, a_re, a_im, b_re_t, b_im_t, d_bbr, d_bbi, d_lr, d_li):
    def body(ldt_ref, are_ref, aim_ref, br_ref, bi_ref, dbr_ref, dbi_ref, dlr_ref, dli_ref,
             obr_ref, obi_ref, oar_ref, oai_ref, odt_ref):
        dt = jnp.exp(ldt_ref[...])
        are, aim = are_ref[...], aim_ref[...]
        m1 = jnp.exp(are * dt)
        lr, li = m1 * jnp.cos(aim * dt), m1 * jnp.sin(aim * dt)
        den = are * are + aim * aim
        nr = lr - 1.0
        f_re = (nr * are + li * aim) / den
        f_im = (li * are - nr * aim) / den
        br, bi, dbr, dbi = br_ref[...], bi_ref[...], dbr_ref[...], dbi_ref[...]
        obr_ref[...] = f_re * dbr + f_im * dbi
        obi_ref[...] = f_re * dbi - f_im * dbr
        gf_re = _colsum(br * dbr + bi * dbi)
        gf_im = _colsum(br * dbi - bi * dbr)
        il_re, il_im = are / den, -aim / den
        glb_re = dlr_ref[...] + (il_re * gf_re + il_im * gf_im)
        glb_im = dli_ref[...] + (il_re * gf_im - il_im * gf_re)
        q_re = -(f_re * il_re - f_im * il_im)
        q_im = -(f_re * il_im + f_im * il_re)
        gl_re = q_re * gf_re + q_im * gf_im
        gl_im = q_re * gf_im - q_im * gf_re
        gl_re = gl_re + dt * (lr * glb_re + li * glb_im)
        gl_im = gl_im + dt * (lr * glb_im - li * glb_re)
        w_re = are * lr - aim * li
        w_im = are * li + aim * lr
        oar_ref[...] = gl_re
        oai_ref[...] = gl_im
        odt_ref[...] = w_re * glb_re + w_im * glb_im

    bb = jax.ShapeDtypeStruct(b_re_t.shape, _F32)
    v = jax.ShapeDtypeStruct(a_re.shape, _F32)
    return pl.pallas_call(body, name="ssm_prep_bwd", out_shape=[bb, bb, v, v, v])(
        log_dt, a_re, a_im, b_re_t, b_im_t, d_bbr, d_bbi, d_lr, d_li)


def _group_sum(d_dt, log_dt):
    def body(d_ref, l_ref, o_ref):
        o_ref[...] = jnp.sum(d_ref[...], axis=1, keepdims=True) * jnp.exp(l_ref[...])

    return pl.pallas_call(body, name="ssm_dt_grad", out_shape=jax.ShapeDtypeStruct(log_dt.shape, _F32))(d_dt, log_dt)


def _load_strided(ref, nr):
    return jnp.concatenate([ref[pl.ds(r, SUBLANES, stride=nr), :] for r in range(nr)], axis=0)


def _store_strided(ref, val, nr):
    for r in range(nr):
        ref[pl.ds(r, SUBLANES, stride=nr), :] = val[r * SUBLANES:(r + 1) * SUBLANES]


def _scan_strided(src_ref, dst_ref, nr, p_ref, carry, reverse, h_ref=None, h_in=None):
    ns = BLOCK_ST
    row = lax.broadcasted_iota(jnp.int32, (SUBLANES, ns), 0)
    bc = lambda v: jnp.broadcast_to(v, (SUBLANES, ns))
    tile = lambda ref, r: (ref[r * SUBLANES:(r + 1) * SUBLANES, 0:ns], ref[r * SUBLANES:(r + 1) * SUBLANES, ns:2 * ns])
    one = nr - 1 if reverse else 0
    ar, ai = bc(p_ref[one:one + 1, 0:ns]), bc(p_ref[one:one + 1, ns:2 * ns])
    xr = xi = None
    for r in (range(nr - 1, -1, -1) if reverse else range(nr)):
        sr, si = tile(src_ref, r)
        xr, xi = (sr, si) if xr is None else (ar * xr - ai * xi + sr, ar * xi + ai * xr + si)
        dst_ref[r * SUBLANES:(r + 1) * SUBLANES, 0:ns] = xr
        dst_ref[r * SUBLANES:(r + 1) * SUBLANES, ns:2 * ns] = xi
    edge, shift = (SUBLANES - 1, SUBLANES - 1) if reverse else (0, 1)
    dr = jnp.where(row == edge, carry[0], pltpu.roll(xr, shift, 0))
    di = jnp.where(row == edge, carry[1], pltpu.roll(xi, shift, 0))
    for i, k in enumerate((1, 2, 4)):
        qr, qi = bc(p_ref[nr + i:nr + i + 1, 0:ns]), bc(p_ref[nr + i:nr + i + 1, ns:2 * ns])
        keep = (row < SUBLANES - k) if reverse else (row >= k)
        sr = jnp.where(keep, pltpu.roll(dr, (SUBLANES - k) if reverse else k, 0), 0.0)
        si = jnp.where(keep, pltpu.roll(di, (SUBLANES - k) if reverse else k, 0), 0.0)
        dr, di = dr + qr * sr - qi * si, di + qr * si + qi * sr
    acc_r = acc_i = jnp.zeros((SUBLANES, ns), _F32)
    out = None
    for r in range(nr):
        wr, wi = p_ref[r:r + 1, 0:ns], p_ref[r:r + 1, ns:2 * ns]
        xr, xi = tile(dst_ref, r)
        xr, xi = xr + wr * dr - wi * di, xi + wr * di + wi * dr
        dst_ref[r * SUBLANES:(r + 1) * SUBLANES, 0:ns] = xr
        dst_ref[r * SUBLANES:(r + 1) * SUBLANES, ns:2 * ns] = xi
        if h_ref is not None:
            if r == 0:
                lr, li = tile(h_ref, nr - 1)
                pr, pi = jnp.where(row == 0, h_in[0], pltpu.roll(lr, 1, 0)), jnp.where(row == 0, h_in[1], pltpu.roll(li, 1, 0))
            else:
                pr, pi = tile(h_ref, r - 1)
            acc_r = acc_r + (xr * pr + xi * pi)
            acc_i = acc_i + (xi * pr - xr * pi)
        if r == (0 if reverse else nr - 1):
            out = (xr[0:1, :], xi[0:1, :]) if reverse else (xr[SUBLANES - 1:SUBLANES, :], xi[SUBLANES - 1:SUBLANES, :])
    if h_ref is None:
        return out
    return out, (_colsum(acc_r), _colsum(acc_i))


def _ssm_gate(y, wg_ref, bg_ref):
    yg = _gelu(y)
    gate = _sigmoid(jnp.dot(yg.astype(_MXU), wg_ref[...].astype(_MXU), preferred_element_type=_F32) + bg_ref[...])
    return yg, gate


def _ssm_specs(nb, nt, t, reverse):
    tt = (lambda ti: nt - 1 - ti) if reverse else (lambda ti: ti)
    ns2 = 2 * BLOCK_ST
    return dict(
        z=pl.BlockSpec((t, BLOCK_CH), lambda b, ti: (tt(ti), b)),
        bbt=pl.BlockSpec((None, BLOCK_CH, ns2), lambda b, ti: (b, 0, 0)),
        ct=pl.BlockSpec((None, ns2, BLOCK_CH), lambda b, ti: (b, 0, 0)),
        vec=pl.BlockSpec((1, BLOCK_CH), lambda b, ti: (0, b)),
        wg=pl.BlockSpec((None, BLOCK_CH, BLOCK_CH), lambda b, ti: (b, 0, 0)),
        p=pl.BlockSpec((None, t // SUBLANES + SUBLANES, ns2), lambda b, ti: (b, 0, 0)),
        hb=pl.BlockSpec((None, None, SUBLANES, ns2), lambda b, ti: (b, tt(ti), 0, 0)),
        h=pl.BlockSpec((None, t, ns2), lambda b, ti: (b, tt(ti), 0)),
        acc_vec=pl.BlockSpec((None, 1, ns2), lambda b, ti: (b, 0, 0)),
    )


def _ssm_fwd(z, bbt, ct, dvec, wg, bglu, ptab):
    s = z.shape[0]
    nb = bbt.shape[0]
    t = _tile(s, TIME_TILE, SUBLANES)
    nt = s // t
    ns = BLOCK_ST
    sp = _ssm_specs(nb, nt, t, False)

    nr = t // SUBLANES

    def body(z_ref, bbt_ref, ct_ref, d_ref, wg_ref, bg_ref, p_ref, y2_ref, y_ref, h_ref, hb_ref, bu_scr, h_scr, carry_scr):
        _zero_first(pl.program_id(1) == 0, carry_scr)
        hb_ref[...] = carry_scr[...]
        carry_in = (carry_scr[0:1, 0:ns], carry_scr[0:1, ns:2 * ns])
        u = _load_strided(z_ref, nr)
        bu_scr[...] = jnp.dot(u.astype(_MXU), bbt_ref[...].astype(_MXU), preferred_element_type=_F32)
        cr, ci = _scan_strided(bu_scr, h_scr, nr, p_ref, carry_in, False)
        hx = h_scr[...].astype(_MXU)
        h_ref[...] = hx
        y = jnp.dot(hx, ct_ref[...].astype(_MXU), preferred_element_type=_F32) + d_ref[...] * u
        yg, gate = _ssm_gate(y, wg_ref, bg_ref)
        _store_strided(y2_ref, yg * gate, nr)
        _store_strided(y_ref, y, nr)
        carry_scr[:, 0:ns] = jnp.broadcast_to(cr, (SUBLANES, ns))
        carry_scr[:, ns:2 * ns] = jnp.broadcast_to(ci, (SUBLANES, ns))

    ych = jax.ShapeDtypeStruct((s, nb * BLOCK_CH), _F32)
    return pl.pallas_call(
        body, name="ssm_fwd", grid=(nb, nt),
        out_shape=[ych, ych, jax.ShapeDtypeStruct((nb, s, 2 * ns), _MXU),
                   jax.ShapeDtypeStruct((nb, nt, SUBLANES, 2 * ns), _F32)],
        in_specs=[sp["z"], sp["bbt"], sp["ct"], sp["vec"], sp["wg"], sp["vec"], sp["p"]],
        out_specs=[sp["z"], sp["z"], sp["h"], sp["hb"]],
        scratch_shapes=[pltpu.VMEM((t, 2 * ns), _F32), pltpu.VMEM((t, 2 * ns), _F32), pltpu.VMEM((SUBLANES, 2 * ns), _F32)],
        compiler_params=_cp("parallel", "arbitrary"),
    )(z, bbt, ct, dvec, wg, bglu, ptab)


def _ssm_bwd(z, y_pre, h_all, dy2, hb, bbt, ct, dvec, wg, bglu, ptab_rev):
    s = z.shape[0]
    nb = bbt.shape[0]
    t = _tile(s, TIME_TILE, SUBLANES)
    nt = s // t
    ns = BLOCK_ST
    sp = _ssm_specs(nb, nt, t, True)
    tn_dims = (((0,), (0,)), ((), ()))
    nt_dims = (((1,), (1,)), ((), ()))

    nr = t // SUBLANES

    def body(z_ref, y_ref, h_ref, dy2_ref, hb_ref, bbt_ref, ct_ref, d_ref, wg_ref, bg_ref, pr_ref,
             dz_ref, dbbt_ref, dct_ref, dwg_ref, dlb_ref, dd_ref, dbg_ref, bu_scr, g_scr, h_scr, gcarry_scr):
        first = pl.program_id(1) == 0

        _zero_first(first, gcarry_scr, dbbt_ref, dct_ref, dwg_ref, dlb_ref, dd_ref, dbg_ref)
        u = _load_strided(z_ref, nr)
        hin = hb_ref[...]
        y = _load_strided(y_ref, nr)
        yg, gate = _ssm_gate(y, wg_ref, bg_ref)
        dy2 = _load_strided(dy2_ref, nr)
        dpre = dy2 * yg * gate * (1.0 - gate)
        _acc(dbg_ref, first, _colsum(dpre))
        dpx = dpre.astype(_MXU)
        _acc(dwg_ref, first, lax.dot_general(yg.astype(_MXU), dpx, tn_dims, preferred_element_type=_F32))
        dyg = dy2 * gate + lax.dot_general(dpx, wg_ref[...].astype(_MXU), nt_dims, preferred_element_type=_F32)
        dy = dyg * _gelu_grad(y)
        _acc(dd_ref, first, _colsum(dy * u))
        dyx = dy.astype(_MXU)
        hx = h_ref[...]
        h_scr[...] = hx.astype(_F32)
        _acc(dct_ref, first, lax.dot_general(hx, dyx, tn_dims, preferred_element_type=_F32))
        bu_scr[...] = lax.dot_general(dyx, ct_ref[...].astype(_MXU), nt_dims, preferred_element_type=_F32)
        gin = (gcarry_scr[0:1, 0:ns], gcarry_scr[0:1, ns:2 * ns])
        (gr, gi), (d_ar, d_ai) = _scan_strided(bu_scr, g_scr, nr, pr_ref, gin, True, h_scr,
                                               (hin[0:1, 0:ns], hin[0:1, ns:2 * ns]))
        gcarry_scr[:, 0:ns] = jnp.broadcast_to(gr, (SUBLANES, ns))
        gcarry_scr[:, ns:2 * ns] = jnp.broadcast_to(gi, (SUBLANES, ns))
        _acc(dlb_ref, first, jnp.concatenate([d_ar, d_ai], axis=1))
        gx = g_scr[...].astype(_MXU)
        _acc(dbbt_ref, first, lax.dot_general(u.astype(_MXU), gx, tn_dims, preferred_element_type=_F32))
        _store_strided(dz_ref, dy * d_ref[...] + lax.dot_general(gx, bbt_ref[...].astype(_MXU), nt_dims,
                                                                 preferred_element_type=_F32), nr)

    f = lambda shape: jax.ShapeDtypeStruct(shape, _F32)
    return pl.pallas_call(
        body, name="ssm_bwd", grid=(nb, nt),
        out_shape=[f((s, nb * BLOCK_CH)), f(bbt.shape), f(ct.shape), f(wg.shape), f((nb, 1, 2 * ns)),
                   f((1, nb * BLOCK_CH)), f((1, nb * BLOCK_CH))],
        in_specs=[sp["z"], sp["z"], sp["h"], sp["z"], sp["hb"], sp["bbt"], sp["ct"], sp["vec"], sp["wg"], sp["vec"], sp["p"]],
        out_specs=[sp["z"], sp["bbt"], sp["ct"], sp["wg"], sp["acc_vec"], sp["vec"], sp["vec"]],
        scratch_shapes=[pltpu.VMEM((t, 2 * ns), _F32), pltpu.VMEM((t, 2 * ns), _F32), pltpu.VMEM((t, 2 * ns), _F32),
                        pltpu.VMEM((SUBLANES, 2 * ns), _F32)],
        compiler_params=_cp("parallel", "arbitrary"),
    )(z, y_pre, h_all, dy2, hb, bbt, ct, dvec, wg, bglu, ptab_rev)


def _mod_part(c_all, w, b):
    d, ns = w.shape
    tn = _tile(ns, 512)

    def body(c_ref, w_ref, b_ref, o_ref):
        c = c_ref[...]
        ca = (c * _sigmoid(c)).astype(_MXU)
        o_ref[...] = jnp.dot(ca, w_ref[...].astype(_MXU), preferred_element_type=_F32) + b_ref[...]

    return pl.pallas_call(
        body, name="mod_part", grid=(ns // tn,), out_shape=jax.ShapeDtypeStruct((8, ns), _F32),
        in_specs=[pl.BlockSpec((8, d), lambda n: (0, 0)), pl.BlockSpec((d, tn), lambda n: (0, n)),
                  pl.BlockSpec((1, tn), lambda n: (0, n))],
        out_specs=pl.BlockSpec((8, tn), lambda n: (0, n)), compiler_params=_cp("parallel"),
    )(c_all, w, b)


def _adamw_math(w, g, m, v):
    m = ADAM_B1 * m + (1.0 - ADAM_B1) * g
    v = ADAM_B2 * v + (1.0 - ADAM_B2) * (g * g)
    m_hat = m / (1.0 - ADAM_B1 ** ADAM_STEP)
    v_hat = v / (1.0 - ADAM_B2 ** ADAM_STEP)
    delta = -ADAM_LR * (m_hat / (jnp.sqrt(v_hat) + ADAM_EPS) + ADAM_WD * w)
    return delta, m, v


def _adamw(w, g, m, v, name):
    r, c = w.shape
    tc = c if c <= 4096 else _tile(c, 4096)
    tr = _tile(r, max(SUBLANES, (1 << 18) // tc), SUBLANES)

    def body(w_ref, g_ref, m_ref, v_ref, go_ref, d_ref, mo_ref, vo_ref):
        g = g_ref[...]
        go_ref[...] = g
        d_ref[...], mo_ref[...], vo_ref[...] = _adamw_math(w_ref[...], g, m_ref[...], v_ref[...])

    spec = pl.BlockSpec((tr, tc), lambda i, j: (i, j))
    out = jax.ShapeDtypeStruct((r, c), _F32)
    return pl.pallas_call(
        body, name=name, grid=(r // tr, c // tc), in_specs=[spec] * 4, out_specs=[spec] * 4, out_shape=[out] * 4,
        compiler_params=_cp("parallel", "parallel"),
    )(w, g, m, v)


def _adamw_halves(w, g2, m, v, name):
    r, c = w.shape
    tr, tc = _tile(r, 256, SUBLANES), _tile(c // 2, 1024)
    nph = (c // 2) // tc

    def body(w_ref, g_ref, m_ref, v_ref, go_ref, d_ref, mo_ref, vo_ref):
        g = g_ref[...]
        go_ref[...] = g
        d_ref[...], mo_ref[...], vo_ref[...] = _adamw_math(w_ref[...], g, m_ref[...], v_ref[...])

    spec = pl.BlockSpec((tr, tc), lambda i, j: (i, j))
    out = jax.ShapeDtypeStruct((r, c), _F32)
    return pl.pallas_call(
        body, name=name, grid=(r // tr, c // tc),
        in_specs=[spec, pl.BlockSpec((None, tr, tc), lambda i, j: (j // nph, i, j % nph)), spec, spec],
        out_specs=[spec] * 4, out_shape=[out] * 4, compiler_params=_cp("parallel", "parallel"),
    )(w, g2, m, v)


def _wada_update(c_t, dm, w, m, v):
    d, ns = w.shape
    tr, tc = _tile(d, 256, SUBLANES), _tile(ns, 1024)

    def body(c_ref, dm_ref, w_ref, m_ref, v_ref, g_ref, d_ref, mo_ref, vo_ref):
        c = c_ref[...]
        ca = c * _sigmoid(c)
        dmv = dm_ref[...]
        g = ca[:, 0:1] * dmv[0:1, :]
        for b in range(1, 8):
            g = g + ca[:, b:b + 1] * dmv[b:b + 1, :]
        g_ref[...] = g
        d_ref[...], mo_ref[...], vo_ref[...] = _adamw_math(w_ref[...], g, m_ref[...], v_ref[...])

    spec = pl.BlockSpec((tr, tc), lambda i, j: (i, j))
    out = jax.ShapeDtypeStruct((d, ns), _F32)
    return pl.pallas_call(
        body, name="wada_update", grid=(d // tr, ns // tc),
        in_specs=[pl.BlockSpec((tr, 8), lambda i, j: (i, 0)), pl.BlockSpec((8, tc), lambda i, j: (0, j)), spec, spec, spec],
        out_specs=[spec] * 4, out_shape=[out] * 4, compiler_params=_cp("parallel", "parallel"),
    )(c_t, dm, w, m, v)


def _small_reduce(gathered):
    _, r, c = gathered.shape
    tr = _tile(r, 512, SUBLANES)

    def body(q_ref, g_ref):
        g = q_ref[0]
        for k in range(1, 8):
            g = g + q_ref[k]
        g_ref[...] = g

    return pl.pallas_call(
        body, name="small_reduce", grid=(r // tr,), out_shape=jax.ShapeDtypeStruct((r, c), _F32),
        in_specs=[pl.BlockSpec((8, tr, c), lambda i: (0, i, 0))], out_specs=pl.BlockSpec((tr, c), lambda i: (i, 0)),
        compiler_params=_cp("parallel"),
    )(gathered)


def _adamw_many(ws, gs, ms, vs, steps, name):
    n = len(ws)

    def body(*refs):
        w_refs, g_refs, m_refs, v_refs = refs[0:n], refs[n:2 * n], refs[2 * n:3 * n], refs[3 * n:4 * n]
        d_refs, mo_refs, vo_refs = refs[4 * n:5 * n], refs[5 * n:6 * n], refs[6 * n:7 * n]
        for i in range(n):
            d_refs[i][...], mo_refs[i][...], vo_refs[i][...] = _adamw_math(
                w_refs[i][...], g_refs[i][...], m_refs[i][...], v_refs[i][...])

    def spec(a):
        nd = a.ndim
        if steps == 1:
            return pl.BlockSpec(a.shape, lambda i: (0,) * nd)
        return pl.BlockSpec((a.shape[0] // steps,) + a.shape[1:], lambda i: (i,) + (0,) * (nd - 1))

    specs = [spec(w) for w in ws]
    outs = pl.pallas_call(
        body, name=name, grid=(steps,), in_specs=specs * 4, out_specs=specs * 3,
        out_shape=[jax.ShapeDtypeStruct(w.shape, _F32) for w in ws] * 3, compiler_params=_cp("parallel"),
    )(*ws, *gs, *ms, *vs)
    return outs[0:n], outs[n:2 * n], outs[2 * n:3 * n]


def _block_diag(x, eye=None):
    nb, g, p, q = x.shape
    eye = jnp.eye(g, dtype=x.dtype) if eye is None else eye
    return (x[:, :, :, None, :] * eye[None, :, None, :, None]).reshape(nb, g * p, g * q)


def _block_diag_take(x, p, q):
    nb = x.shape[0]
    g = GROUPS_PER_BLOCK
    eye = jnp.eye(g, dtype=x.dtype)
    return jnp.sum(x.reshape(nb, g, p, g, q) * eye[None, :, None, :, None], axis=3)


_VIEWS = {"ssm_b_re": ((0, 2, 1), (0, 2, 1)), "ssm_b_im": ((0, 2, 1), (0, 2, 1)),
          "ssm_w_glu": ((1, 2, 0), (2, 0, 1)), "ssm_b_glu": ((1, 0), (1, 0))}


def _to_view(name, a):
    return a.transpose(_VIEWS[name][0]) if name in _VIEWS else a


def _from_view(name, a):
    return a.transpose(_VIEWS[name][1]) if name in _VIEWS else a


class _Pack:
    def __init__(self, shapes):
        self.shapes = shapes
        self.offsets = {}
        off = 0
        for name, shape in shapes.items():
            n = math.prod(shape)
            self.offsets[name] = (off, n)
            off += -(-n // (SUBLANES * LANES)) * (SUBLANES * LANES)
        self.rows = -(-off // (256 * LANES)) * 256

    def pack(self, arrays):
        parts = []
        off = 0
        for name, shape in self.shapes.items():
            start, n = self.offsets[name]
            if start > off:
                parts.append(jnp.zeros((start - off,), _F32))
            parts.append(arrays[name].reshape(-1).astype(_F32))
            off = start + n
        total = self.rows * LANES
        if total > off:
            parts.append(jnp.zeros((total - off,), _F32))
        return jnp.concatenate(parts).reshape(self.rows, LANES)

    def unpack(self, buf):
        flat = buf.reshape(-1)
        return {name: flat[start:start + n].reshape(self.shapes[name]) for name, (start, n) in self.offsets.items()}


_SMALL = ["b_ada", "g_pre_mix", "g_post_mix", "ssm_log_dt", "ssm_a_re", "ssm_a_im", "ssm_b_re", "ssm_b_im", "ssm_c_re",
          "ssm_c_im", "ssm_d", "ssm_w_glu", "ssm_b_glu", "sgu_ln_g", "sgu_ln_b", "sgu_w", "sgu_b", "g_out_ssm",
          "g_out_sgu", "g_pre_ffn", "g_post_ffn", "conv_b"]
_WEIGHTS = ["w_ada", "b_ada", "g_pre_mix", "g_post_mix", "w_in", "ssm_log_dt", "ssm_a_re", "ssm_a_im", "ssm_b_re",
            "ssm_b_im", "ssm_c_re", "ssm_c_im", "ssm_d", "ssm_w_glu", "ssm_b_glu", "sgu_ln_g", "sgu_ln_b", "sgu_w", "sgu_b",
            "g_out_ssm", "g_out_sgu", "w_out", "g_pre_ffn", "g_post_ffn", "w_up", "conv_w", "conv_b", "w_down"]


def _step(p, m, v, x, c, tgt):
    s, d = x.shape
    mx, my, mc = lax.axis_index("x"), lax.axis_index("y"), lax.axis_index("c")
    chip = 2 * mx + my
    dev = 4 * mx + 2 * my + mc
    sel = jnp.stack([chip, mc]).astype(jnp.int32)
    g_cnt, n_st = p["ssm_a_re"].shape
    nb = g_cnt // GROUPS_PER_BLOCK
    gn = g_cnt * n_st
    d_ssm = g_cnt * SSM_GROUP
    nh = p["sgu_w"].shape[0]
    assert nh * CHUNK == d_ssm and 2 * d_ssm == d and n_st == SSM_STATE

    shards = lambda g: g.reshape(4, g.shape[1] * g.shape[2], g.shape[3])
    buf_in = _cast_into_slot(p["w_in"], sel, sel, "cast_w_in")

    ns_ada = p["w_ada"].shape[1]
    nc_conv = p["conv_w"].shape[1]
    first = jnp.concatenate([jnp.broadcast_to(c, (8, d)), jnp.pad(p["conv_w"], ((0, 5), (0, 0)))], axis=1)
    first_all = _all_gather8_direct(_own_slot(first, dev), "gather_c_conv", after=buf_in)
    (sems_in,), (buf_in,), tok = _gather_start([buf_in], first_all, "gather_start_in")
    c_all = _after(first_all[:, 0, :d], tok)
    conv_w_full = jnp.concatenate([first_all[2 * j, 0:3, d:] for j in range(4)], axis=1)
    b_ada_mine = lax.dynamic_slice_in_dim(p["b_ada"], chip * ns_ada, ns_ada, axis=1)
    mod_mine = _mod_part(c_all, p["w_ada"], b_ada_mine)
    buf_out, buf_up, buf_down = [_cast_into_slot(p[n], sel, tok, "cast_" + n) for n in ("w_out", "w_up", "w_down")]

    eye_t = jnp.eye(GROUPS_PER_BLOCK, dtype=_F32) + tok[0:1, 0:1]
    ldt_l = _after(jnp.repeat(p["ssm_log_dt"], n_st, axis=1), tok)
    are_l, aim_l = p["ssm_a_re"].reshape(1, gn), p["ssm_a_im"].reshape(1, gn)
    bre_t, bim_t = p["ssm_b_re"].reshape(gn, SSM_GROUP).T, p["ssm_b_im"].reshape(gn, SSM_GROUP).T
    nr = _tile(s, TIME_TILE, SUBLANES) // SUBLANES
    kvec = jnp.concatenate([jnp.arange(1, nr + 1, dtype=_F32), jnp.array([nr, 2 * nr, 4 * nr, 0, 0, 0, 0, 0], _F32)])
    pw_re, pw_im, bb_re, bb_im = _ssm_prep(ldt_l, are_l, aim_l, bre_t, bim_t, kvec.reshape(nr + SUBLANES, 1))
    blocks = lambda t: t.reshape(t.shape[0], nb, GROUPS_PER_BLOCK * n_st).transpose(1, 0, 2)
    ptab = jnp.concatenate([blocks(pw_re), blocks(pw_im)], axis=2)
    rev = lambda t: jnp.concatenate([t[:, :nr][:, ::-1], t[:, nr:]], axis=1)
    ptab_rev = jnp.concatenate([rev(blocks(pw_re)), -rev(blocks(pw_im))], axis=2)
    bd = lambda t: t.reshape(SSM_GROUP, nb, GROUPS_PER_BLOCK, n_st).transpose(1, 2, 0, 3)
    bbt = jnp.concatenate([_block_diag(bd(bb_re)), _block_diag(bd(bb_im))], axis=2).astype(_MXU)
    cd = lambda t: t.reshape(nb, GROUPS_PER_BLOCK, SSM_GROUP, n_st).transpose(0, 1, 3, 2)
    ct = jnp.concatenate([_block_diag(cd(p["ssm_c_re"]), eye_t), -_block_diag(cd(p["ssm_c_im"]), eye_t)], axis=1).astype(_MXU)
    wg = _block_diag(p["ssm_w_glu"].reshape(nb, GROUPS_PER_BLOCK, SSM_GROUP, SSM_GROUP), eye_t).astype(_MXU)
    dvec = p["ssm_d"]
    bglu = p["ssm_b_glu"].reshape(1, d_ssm)
    mask = jnp.tril(jnp.ones((CHUNK, CHUNK), _F32)) + tok[0:1, 0:1]
    wm = (p["sgu_w"] * mask[None]).astype(_MXU)
    bs = p["sgu_b"].reshape(nh, CHUNK, 1)

    mod_all = _all_gather8_direct(_own_slot(mod_mine, dev), "gather_mod",
                           after=[buf_out, buf_up, buf_down, bbt, ct, wg, wm, bs, ptab, ptab_rev])
    (sems_out, sems_up), (buf_out, buf_up), tok_rest = _route_start([(buf_out, 1), (buf_up, 1)], mod_all, "route_start_a")
    mod_rows = lax.dynamic_index_in_dim(mod_all, dev, axis=1, keepdims=False)
    mod = jnp.concatenate([mod_rows[0], mod_rows[2], mod_rows[4], mod_rows[6]]).reshape(N_MOD, 1, d)
    sh1, sc1, gt1, sh2, sc2, gt2 = [mod[i] for i in range(N_MOD)]

    h1 = _fwd_pre_mix(x, p["g_pre_mix"], _after(sc1, tok_rest), sh1)
    buf_in = _gather_wait(sems_in, buf_in, h1, "gather_wait_in")
    w_in4 = shards(_pair_forward([buf_in], "pair_forward_in")[0])
    z = _mm_nn(h1, w_in4, _F32, "mm_in")
    y_ssm, y_pre, h_all, hb = _ssm_fwd(z, bbt, ct, dvec, wg, bglu, ptab)
    y_sgu = _sgu_fwd(z, p["sgu_ln_g"], p["sgu_ln_b"], wm, bs)
    buf_out = _route_wait(sems_out, buf_out, 1, y_sgu, "route_wait_out_1")
    buf_up = _route_wait(sems_up, buf_up, 1, y_ssm, "route_wait_up_1")
    (sems_out, sems_up, sems_down), (buf_out, buf_up, buf_down), tok = _route_start(
        [(buf_out, 2), (buf_up, 2), (buf_down, 1)], y_sgu, "route_start_b")
    ycat = _mix_norm_fwd(y_ssm, y_sgu, _after(p["g_out_ssm"], tok), p["g_out_sgu"])
    buf_out = _route_wait(sems_out, buf_out, 2, ycat, "route_wait_out_2")
    w_out_full = _pair_forward([buf_out], "pair_forward_out")[0].reshape(1, d, d)
    o = _mm_nn(ycat, w_out_full, _F32, "mm_out")
    x1, h2 = _fwd_mid(o, x, gt1, p["g_post_mix"], p["g_pre_ffn"], sc2, sh2)
    buf_up = _route_wait(sems_up, buf_up, 2, h2, "route_wait_up_2")
    w_up4 = shards(_pair_forward([buf_up], "pair_forward_up")[0])
    up_pre = _mm_nn(h2, w_up4, _F32, "mm_up")
    buf_down = _route_wait(sems_down, buf_down, 1, up_pre, "route_wait_down_1")
    (sems_down,), (buf_down,), tok = _route_start([(buf_down, 2)], up_pre, "route_start_c")
    act = _conv_act_fwd(up_pre, conv_w_full, _after(p["conv_b"], tok))
    buf_down = _route_wait(sems_down, buf_down, 2, act, "route_wait_down_2")
    w_down_full = _pair_forward([buf_down], "pair_forward_down")[0].reshape(1, -1, d)
    f = _mm_nn(act, w_down_full, _F32, "mm_down", tk=5632)
    dx2, df, d_gt2, d_g_post_ffn, loss = _loss_and_post_ffn_bwd(f, x1, tgt, gt2, p["g_post_ffn"])

    def reduce_next(swap, n, after):
        sems, gw, land, _ = swap
        gw, got = _swap_wait(sems, gw, land, after, "swap_wait_" + n)
        return _scatter_start(_pair_sum(gw, got, sel, "pair_sum_" + n), "scatter_start_" + n)

    d_act = _mm_nt(df, w_down_full, _F32, "mm_d_act", tk=2048)
    swap_down = _swap_start(_mm_tn_rows(act, df, "mm_gw_down"), "swap_start_w_down")
    d_up_pre, d_cw0, d_cw1, d_cw2, d_conv_b = _conv_act_bwd(up_pre, d_act, conv_w_full, _after(p["conv_b"], swap_down[3]))
    red_down = reduce_next(swap_down, "w_down", d_conv_b)
    dh2 = _mm_nt(d_up_pre, w_up4, _F32, "mm_dh2", tk=2816, after=red_down[3])
    swap_up = _swap_start(_mm_tn_cols(h2, d_up_pre, "mm_gw_up"), "swap_start_w_up")
    dx1, d_o, d_sc2, d_sh2, d_g_pre_ffn, d_gt1, d_g_post_mix = _bwd_mid(
        dh2, x1, dx2, o, p["g_pre_ffn"], _after(sc2, swap_up[3]), gt1, p["g_post_mix"])
    red_up = reduce_next(swap_up, "w_up", d_g_post_mix)
    d_ycat = _mm_nt(d_o, w_out_full, _F32, "mm_d_ycat", tn=1024, tk=2048, after=red_up[3])
    swap_out = _swap_start(_mm_tn_rows(ycat, d_o, "mm_gw_out"), "swap_start_w_out")
    dy_ssm, dy_sgu, d_g_out_ssm, d_g_out_sgu = _mix_norm_bwd(
        d_ycat, y_ssm, y_sgu, _after(p["g_out_ssm"], swap_out[3]), p["g_out_sgu"])
    red_out = reduce_next(swap_out, "w_out", d_g_out_sgu)
    dz_ssm, d_bbt, d_ct, d_wg, d_lb, d_ssm_d, d_bglu = _ssm_bwd(z, y_pre, h_all, dy_ssm, hb, bbt, ct,
                                                                _after(dvec, red_out[3]), wg, bglu, ptab_rev)
    dz, d_ln_g, d_ln_b, d_wm, d_bs = _sgu_bwd(z, dy_sgu, dz_ssm, p["sgu_ln_g"], p["sgu_ln_b"], wm, bs)
    dh1 = _mm_nt(dz, w_in4, _F32, "mm_dh1")
    swap_in = _swap_start(_mm_tn_cols(h1, dz, "mm_gw_in"), "swap_start_w_in")
    dx, d_sc1, d_sh1, d_g_pre_mix = _bwd_pre_mix(dh1, x, dx1, p["g_pre_mix"], _after(sc1, swap_in[3]))
    red_in = reduce_next(swap_in, "w_in", d_g_pre_mix)

    nsb = BLOCK_ST
    lanes = lambda t: t.transpose(2, 0, 1, 3).reshape(SSM_GROUP, gn)
    d_bbr = lanes(_block_diag_take(d_bbt[:, :, :nsb], SSM_GROUP, n_st))
    d_bbi = lanes(_block_diag_take(d_bbt[:, :, nsb:], SSM_GROUP, n_st))
    d_lr, d_li = d_lb[:, 0, :nsb].reshape(1, gn), d_lb[:, 0, nsb:].reshape(1, gn)
    d_bre_t, d_bim_t, d_are, d_aim, d_dt = _ssm_prep_bwd(ldt_l, are_l, aim_l, bre_t, bim_t, d_bbr, d_bbi, d_lr, d_li)
    d_log_dt = _group_sum(d_dt.reshape(g_cnt, n_st), p["ssm_log_dt"].reshape(g_cnt, 1))
    c_grad = lambda t: _block_diag_take(t, n_st, SSM_GROUP).transpose(0, 1, 3, 2).reshape(g_cnt, SSM_GROUP, n_st)
    small = {
        "b_ada": jnp.concatenate([d_sh1, _after(d_sc1, red_in[3]), d_gt1, d_sh2, d_sc2, d_gt2], axis=1),
        "g_pre_mix": d_g_pre_mix, "g_post_mix": d_g_post_mix,
        "ssm_log_dt": d_log_dt, "ssm_a_re": d_are, "ssm_a_im": d_aim,
        "ssm_b_re": d_bre_t.T, "ssm_b_im": d_bim_t.T,
        "ssm_c_re": c_grad(d_ct[:, :nsb, :]), "ssm_c_im": -c_grad(d_ct[:, nsb:, :]),
        "ssm_d": d_ssm_d, "ssm_w_glu": _block_diag_take(d_wg, SSM_GROUP, SSM_GROUP), "ssm_b_glu": d_bglu,
        "sgu_ln_g": d_ln_g, "sgu_ln_b": d_ln_b, "sgu_w": d_wm * mask[None], "sgu_b": d_bs,
        "g_out_ssm": d_g_out_ssm, "g_out_sgu": d_g_out_sgu, "g_pre_ffn": d_g_pre_ffn, "g_post_ffn": d_g_post_ffn,
        "conv_b": d_conv_b, "conv_w_all": jnp.concatenate([d_cw0, d_cw1, d_cw2], axis=0),
        "loss_sum": loss,
    }
    small = {n: _to_view(n, a.reshape(p[n].shape)) if n in p else a for n, a in small.items()}
    pk = _Pack({n: a.shape for n, a in small.items()})
    sems_small, small_buf, tok = _gather8_start(_own_slot(pk.pack(small), dev), "gather_small_start")

    big = ["w_down", "w_up", "w_out", "w_in"]
    joins = []
    after = tok
    for n, (sems, pair, land, _) in zip(big, (red_down, red_up, red_out, red_in)):
        pair, land = _scatter_wait(sems, pair, land, after, "scatter_wait_" + n)
        sems_j, half, after = _join_start(_chip_sum(pair, land, sel, "chip_sum_" + n), "join_start_" + n)
        joins.append((sems_j, half))
    big_out = {}
    for n, (sems_j, half) in zip(big, joins):
        j = _join_wait(sems_j, half, after, "join_wait_" + n)
        if n in ("w_in", "w_up"):
            big_out[n] = tuple(_adamw(p[n], j.reshape(p[n].shape), m[n], v[n], "adamw_" + n))
        else:
            big_out[n] = tuple(_adamw_halves(p[n], j, m[n], v[n], "adamw_" + n))
        after = big_out[n][1]

    gathered = _gather8_forward(_gather8_wait(sems_small, small_buf, after, "gather_small_wait"),
                                "gather_small_forward")
    gview = pk.unpack(_small_reduce(gathered))
    gview["conv_w"] = lax.dynamic_slice_in_dim(gview.pop("conv_w_all"), chip * nc_conv, nc_conv, axis=1)
    loss = gview.pop("loss_sum")
    small_names = _SMALL + ["conv_w"]
    per_group = [n for n in small_names if gview[n].ndim >= 2 and gview[n].shape[0] == g_cnt]
    others = [n for n in small_names if n not in per_group]
    grads = {n: _from_view(n, gview[n]) for n in small_names}
    deltas, new_m, new_v = {}, {}, {}
    for names, steps, call in ((per_group, g_cnt // GROUPS_PER_BLOCK, "adamw_s5"), (others, 1, "adamw_small")):
        res = _adamw_many([_to_view(n, p[n]) for n in names], [gview[n] for n in names],
                          [_to_view(n, m[n]) for n in names], [_to_view(n, v[n]) for n in names], steps, call)
        for n, dl, mo, vo in zip(names, *res):
            deltas[n], new_m[n], new_v[n] = _from_view(n, dl), _from_view(n, mo), _from_view(n, vo)

    d_mod_all = gathered.reshape(8, -1)[:, :N_MOD * d]
    d_mod_mine = lax.dynamic_slice_in_dim(d_mod_all, chip * ns_ada, ns_ada, axis=1)
    grads["w_ada"], deltas["w_ada"], new_m["w_ada"], new_v["w_ada"] = _wada_update(
        c_all.T, d_mod_mine, p["w_ada"], m["w_ada"], v["w_ada"])
    for n in big:
        grads[n], deltas[n], new_m[n], new_v[n] = big_out[n]
    return loss[0, 0], dx, grads, deltas, new_m, new_v


def kernel(x, c, w_ada, b_ada, g_pre_mix, g_post_mix, w_in, ssm_log_dt, ssm_a_re, ssm_a_im, ssm_b_re, ssm_b_im, ssm_c_re, ssm_c_im, ssm_d, ssm_w_glu, ssm_b_glu, sgu_ln_g, sgu_ln_b, sgu_w, sgu_b, g_out_ssm, g_out_sgu, w_out, g_pre_ffn, g_post_ffn, w_up, conv_w, conv_b, w_down, loss_target, m_w_ada, m_b_ada, m_g_pre_mix, m_g_post_mix, m_w_in, m_ssm_log_dt, m_ssm_a_re, m_ssm_a_im, m_ssm_b_re, m_ssm_b_im, m_ssm_c_re, m_ssm_c_im, m_ssm_d, m_ssm_w_glu, m_ssm_b_glu, m_sgu_ln_g, m_sgu_ln_b, m_sgu_w, m_sgu_b, m_g_out_ssm, m_g_out_sgu, m_w_out, m_g_pre_ffn, m_g_post_ffn, m_w_up, m_conv_w, m_conv_b, m_w_down, v_w_ada, v_b_ada, v_g_pre_mix, v_g_post_mix, v_w_in, v_ssm_log_dt, v_ssm_a_re, v_ssm_a_im, v_ssm_b_re, v_ssm_b_im, v_ssm_c_re, v_ssm_c_im, v_ssm_d, v_ssm_w_glu, v_ssm_b_glu, v_sgu_ln_g, v_sgu_ln_b, v_sgu_w, v_sgu_b, v_g_out_ssm, v_g_out_sgu, v_w_out, v_g_pre_ffn, v_g_post_ffn, v_w_up, v_conv_w, v_conv_b, v_w_down):
    given = dict(locals())
    drop = lambda a: a if a.ndim == 2 else a[0]
    p = {n: drop(given[n]) for n in _WEIGHTS}
    m = {n: drop(given["m_" + n]) for n in _WEIGHTS}
    v = {n: drop(given["v_" + n]) for n in _WEIGHTS}
    loss, dx, grads, deltas, new_m, new_v = _step(p, m, v, x[0], c, loss_target[0])
    outs = [loss, dx[None]]
    for group in (grads, deltas, new_m, new_v):
        outs += [group[n].reshape(given[n].shape) for n in _WEIGHTS]
    return tuple(outs)
```

```python
import functools
import math

import jax
import jax.numpy as jnp
from jax import lax
from jax.experimental import pallas as pl
from jax.experimental.pallas import tpu as pltpu

_F32 = jnp.float32
_MXU = jnp.bfloat16
_WIRE = jnp.bfloat16

EPS = 1e-6
SSM_GROUP = 16
SSM_STATE = 64
GROUPS_PER_BLOCK = 8
BLOCK_CH = SSM_GROUP * GROUPS_PER_BLOCK
BLOCK_ST = SSM_STATE * GROUPS_PER_BLOCK
CHUNK = 128
TIME_TILE = 1024
SUBLANES = 8
LANES = 128
N_MOD = 6
ADAM_LR, ADAM_B1, ADAM_B2, ADAM_EPS, ADAM_WD, ADAM_STEP = 0.001, 0.9, 0.999, 1e-08, 0.01, 10
_VMEM_LIMIT = 56 * 1024 * 1024
_MESH = pl.DeviceIdType.MESH
_ANY = pl.BlockSpec(memory_space=pl.ANY)
_HBM = pl.BlockSpec(memory_space=pltpu.HBM)
_SEM = pl.BlockSpec(memory_space=pltpu.SEMAPHORE)
_VMEM_WHOLE = pl.BlockSpec(memory_space=pltpu.VMEM)
_EFFECT = pltpu.SideEffectType.DATAFLOW_SIDE_EFFECTING
_GELU_C = math.sqrt(2.0 / math.pi)


def _cp(*sem):
    return pltpu.CompilerParams(dimension_semantics=sem, vmem_limit_bytes=_VMEM_LIMIT)


def _tile(dim, target, align=LANES):
    if dim <= target:
        return dim
    best = None
    for t in range(align, target + 1, align):
        if dim % t == 0:
            best = t
    assert best is not None, (dim, target, align)
    return best


def _gelu(x):
    return 0.5 * x * (1.0 + jnp.tanh(_GELU_C * (x + 0.044715 * (x * x * x))))


def _gelu_grad(x):
    t = jnp.tanh(_GELU_C * (x + 0.044715 * (x * x * x)))
    return 0.5 * (1.0 + t) + 0.5 * x * (1.0 - t * t) * (_GELU_C * (1.0 + 3.0 * 0.044715 * x * x))


def _sigmoid(x):
    return 1.0 / (1.0 + jnp.exp(-x))


def _colsum(x):
    return jnp.sum(x, axis=0, keepdims=True)


def _rowmean(x):
    return jnp.mean(x, axis=-1, keepdims=True)


def _zero_first(first, *refs):
    @pl.when(first)
    def _():
        for ref in refs:
            ref[...] = jnp.zeros_like(ref)


def _acc(ref, first, val):
    del first
    ref[...] += val


def _place():
    mx, my, mc = lax.axis_index("x"), lax.axis_index("y"), lax.axis_index("c")
    chips = [(1 - mx, my), (mx, 1 - my), (1 - mx, 1 - my)]
    return mx, my, mc, chips


def _all_gather8(buf, name, after=None):
    extra = [] if after is None else (list(after) if isinstance(after, (list, tuple)) else [after])

    def body(in_ref, *rest):
        out_ref, send_sems, recv_sems = rest[len(extra):]
        mx, my, mc, chips = _place()
        me, sibling = (mx, my, mc), (mx, my, 1 - mc)

        def slot(ref, px, py, pc):
            return ref.at[4 * px + 2 * py + pc]

        def copy(k, block, to, src_ref=out_ref):
            return pltpu.make_async_remote_copy(
                src_ref=slot(src_ref, *block), dst_ref=slot(out_ref, *block),
                send_sem=send_sems.at[k], recv_sem=recv_sems.at[k], device_id=to, device_id_type=_MESH)

        first = [copy(0, me, sibling, in_ref)]
        first += [copy(1 + j, me, (*chip, mc), in_ref) for j, chip in enumerate(chips)]
        for cp in first:
            cp.start()
        passed = [copy(4 + j, (*chip, mc), sibling) for j, chip in enumerate(chips)]
        for j, chip in enumerate(chips):
            copy(1 + j, (*chip, mc), me).wait_recv()
            passed[j].start()
        copy(0, sibling, me).wait_recv()
        for j, chip in enumerate(chips):
            copy(4 + j, (*chip, 1 - mc), me).wait_recv()
        for cp in first + passed:
            cp.wait_send()

    return pl.pallas_call(
        body, name=name, out_shape=jax.ShapeDtypeStruct(buf.shape, buf.dtype),
        in_specs=[_ANY] * (1 + len(extra)), out_specs=_ANY, input_output_aliases={0: 0},
        scratch_shapes=[pltpu.SemaphoreType.DMA((7,)), pltpu.SemaphoreType.DMA((7,))],
    )(buf, *extra)


def _own_slot(x, dev):
    return lax.dynamic_update_slice(jnp.zeros((8,) + x.shape, x.dtype), x[None], (dev, 0, 0))


def _cast_into_slot(w, sel, after, name):
    r, c = w.shape
    hr = r // 2
    tr = _tile(hr, 256, 16)
    nr = hr // tr

    def body(sel_ref, w_ref, after_ref, o_ref):
        o_ref[...] = w_ref[...].astype(o_ref.dtype)

    return pl.pallas_call(
        body, name=name, out_shape=jax.ShapeDtypeStruct((4, 2, hr, c), _WIRE),
        grid_spec=pltpu.PrefetchScalarGridSpec(
            num_scalar_prefetch=1, grid=(2, nr),
            in_specs=[pl.BlockSpec((tr, c), lambda h, i, s: (h * nr + i, 0)), _ANY],
            out_specs=pl.BlockSpec((None, None, tr, c), lambda h, i, s: (s[0], h, i, 0))),
        compiler_params=_cp("parallel", "parallel"),
    )(sel, w, after)


def _hbm(a):
    return pltpu.with_memory_space_constraint(a, pltpu.HBM)


def _after(vec, token):
    return vec + token[0:1, 0:1]


def _gather_start(bufs, after, name):
    n = len(bufs)
    nc = 3 * n

    def body(*refs):
        ins, send, recv, token = refs[:n], refs[n + 1:n + 1 + nc], refs[n + 1 + nc:n + 1 + 2 * nc], refs[-1]
        mx, my, mc, chips = _place()
        j_me = 2 * mx + my
        for i in range(n):
            for k, chip in enumerate(chips):
                half = ins[i].at[j_me, mc]
                pltpu.make_async_remote_copy(
                    src_ref=half, dst_ref=half, send_sem=send[3 * i + k], recv_sem=recv[3 * i + k],
                    device_id=(*chip, mc), device_id_type=_MESH).start()
        token[...] = jnp.zeros_like(token)

    outs = pl.pallas_call(
        body, name=name,
        out_shape=tuple([pltpu.SemaphoreType.DMA(())] * (2 * nc) + [pltpu.HBM(b.shape, b.dtype) for b in bufs]
                        + [jax.ShapeDtypeStruct((SUBLANES, LANES), _F32)]),
        in_specs=tuple([_HBM] * n + [_ANY]), out_specs=tuple([_SEM] * (2 * nc) + [_HBM] * n + [_VMEM_WHOLE]),
        input_output_aliases={i: 2 * nc + i for i in range(n)},
        compiler_params=pltpu.CompilerParams(has_side_effects=_EFFECT),
    )(*[_hbm(b) for b in bufs], after)
    sems = [(outs[3 * i:3 * i + 3], outs[nc + 3 * i:nc + 3 * i + 3]) for i in range(n)]
    return sems, list(outs[2 * nc:2 * nc + n]), outs[-1]


def _gather_wait(sems, buf, after, name):
    send, recv = sems

    after = list(after) if isinstance(after, (list, tuple)) else [after]

    def body(buf_ref, s0, s1, s2, r0, r1, r2, *rest):
        mx, my, mc, chips = _place()
        j_me = 2 * mx + my
        for k, (chip, s_k, r_k) in enumerate(zip(chips, (s0, s1, s2), (r0, r1, r2))):
            cp = pltpu.make_async_remote_copy(
                src_ref=buf_ref.at[j_me, mc], dst_ref=buf_ref.at[2 * chip[0] + chip[1], mc], send_sem=s_k, recv_sem=r_k,
                device_id=(*chip, mc), device_id_type=_MESH)
            cp.wait_send()
            cp.wait_recv()

    return pl.pallas_call(
        body, name=name, out_shape=pltpu.HBM(buf.shape, buf.dtype),
        in_specs=(_HBM,) + (_SEM,) * 6 + (_ANY,) * len(after), out_specs=_HBM, input_output_aliases={0: 0},
        compiler_params=pltpu.CompilerParams(has_side_effects=_EFFECT),
    )(buf, *send, *recv, *after)


def _route_ends(buf_ref, phase):
    mx, my, mc, _ = _place()
    hq = buf_ref.shape[2] // 2
    xn, yn = (1 - mx, my), (mx, 1 - my)
    j_me, j_x, j_y, j_d = 2 * mx + my, 2 * (1 - mx) + my, 2 * mx + (1 - my), 2 * (1 - mx) + (1 - my)
    if phase == 1:
        mine = buf_ref.at[j_me, mc]
        return [((*xn, mc), mine, buf_ref.at[j_x, mc]), ((*yn, mc), mine, buf_ref.at[j_y, mc])]
    lo, hi = pl.ds(0, hq), pl.ds(hq, hq)
    return [((*xn, mc), buf_ref.at[j_y, mc, lo], buf_ref.at[j_d, mc, lo]),
            ((*yn, mc), buf_ref.at[j_x, mc, hi], buf_ref.at[j_d, mc, hi])]


def _route_start(items, after, name):
    n = len(items)

    def body(*refs):
        ins, send, recv, token = refs[:n], refs[n + 1:3 * n + 1], refs[3 * n + 1:5 * n + 1], refs[-1]
        for i, (_, phase) in enumerate(items):
            for k, (peer, src, _) in enumerate(_route_ends(ins[i], phase)):
                pltpu.make_async_remote_copy(src_ref=src, dst_ref=src, send_sem=send[2 * i + k], recv_sem=recv[2 * i + k],
                                             device_id=peer, device_id_type=_MESH).start()
        token[...] = jnp.zeros_like(token)

    bufs = [b for b, _ in items]
    outs = pl.pallas_call(
        body, name=name,
        out_shape=tuple([pltpu.SemaphoreType.DMA(())] * (4 * n) + [pltpu.HBM(b.shape, b.dtype) for b in bufs]
                        + [jax.ShapeDtypeStruct((SUBLANES, LANES), _F32)]),
        in_specs=tuple([_HBM] * n + [_ANY]), out_specs=tuple([_SEM] * (4 * n) + [_HBM] * n + [_VMEM_WHOLE]),
        input_output_aliases={i: 4 * n + i for i in range(n)},
        compiler_params=pltpu.CompilerParams(has_side_effects=_EFFECT),
    )(*[_hbm(b) for b in bufs], after)
    sems = [(outs[2 * i:2 * i + 2], outs[2 * n + 2 * i:2 * n + 2 * i + 2]) for i in range(n)]
    return sems, list(outs[4 * n:5 * n]), outs[-1]


def _route_wait(sems, buf, phase, after, name):
    send, recv = sems

    def body(buf_ref, s0, s1, r0, r1, after_ref, out_ref):
        for (peer, src, land), s_k, r_k in zip(_route_ends(buf_ref, phase), (s0, s1), (r0, r1)):
            cp = pltpu.make_async_remote_copy(src_ref=src, dst_ref=land, send_sem=s_k, recv_sem=r_k,
                                              device_id=peer, device_id_type=_MESH)
            cp.wait_send()
            cp.wait_recv()

    return pl.pallas_call(
        body, name=name, out_shape=pltpu.HBM(buf.shape, buf.dtype),
        in_specs=(_HBM,) + (_SEM,) * 4 + (_ANY,), out_specs=_HBM, input_output_aliases={0: 0},
        compiler_params=pltpu.CompilerParams(has_side_effects=_EFFECT),
    )(buf, *send, *recv, after)


def _pair_forward(bufs, name):
    n = len(bufs)

    def body(*refs):
        ins, outs = refs[:n], refs[n:2 * n]
        send_sems, recv_sems = refs[2 * n:]
        mx, my, mc, chips = _place()
        sibling = (mx, my, 1 - mc)
        cps = []
        for i in range(n):
            for k, chip in enumerate(chips):
                j_k = 2 * chip[0] + chip[1]
                cp = pltpu.make_async_remote_copy(
                    src_ref=ins[i].at[j_k, mc], dst_ref=outs[i].at[j_k, mc], send_sem=send_sems.at[3 * i + k],
                    recv_sem=recv_sems.at[3 * i + k], device_id=sibling, device_id_type=_MESH)
                cp.start()
                cps.append(cp)
        for i in range(n):
            for k, chip in enumerate(chips):
                other = outs[i].at[2 * chip[0] + chip[1], 1 - mc]
                pltpu.make_async_remote_copy(
                    src_ref=other, dst_ref=other, send_sem=send_sems.at[3 * i + k], recv_sem=recv_sems.at[3 * i + k],
                    device_id=sibling, device_id_type=_MESH).wait_recv()
        for cp in cps:
            cp.wait_send()

    return pl.pallas_call(
        body, name=name, out_shape=[jax.ShapeDtypeStruct(b.shape, b.dtype) for b in bufs],
        in_specs=[_ANY] * n, out_specs=[_ANY] * n, input_output_aliases={i: i for i in range(n)},
        scratch_shapes=[pltpu.SemaphoreType.DMA((3 * n,)), pltpu.SemaphoreType.DMA((3 * n,))],
    )(*bufs)


def _gather8_peers(buf_ref, mx, my, mc, chips):
    mine = buf_ref.at[4 * mx + 2 * my + mc]
    peers = [((mx, my, 1 - mc), mine, buf_ref.at[4 * mx + 2 * my + 1 - mc])]
    peers += [((*chip, mc), mine, buf_ref.at[4 * chip[0] + 2 * chip[1] + mc]) for chip in chips]
    return peers


def _gather8_start(buf, name):
    def body(buf_ref, *rest):
        send, recv, token = rest[0:4], rest[4:8], rest[-1]
        mx, my, mc, chips = _place()
        for k, (peer, src, _) in enumerate(_gather8_peers(buf_ref, mx, my, mc, chips)):
            pltpu.make_async_remote_copy(src_ref=src, dst_ref=src, send_sem=send[k], recv_sem=recv[k],
                                         device_id=peer, device_id_type=_MESH).start()
        token[...] = jnp.zeros_like(token)

    outs = pl.pallas_call(
        body, name=name,
        out_shape=tuple([pltpu.SemaphoreType.DMA(())] * 8 + [pltpu.HBM(buf.shape, buf.dtype),
                                                             jax.ShapeDtypeStruct((SUBLANES, LANES), _F32)]),
        in_specs=(_HBM,), out_specs=tuple([_SEM] * 8 + [_HBM, _VMEM_WHOLE]), input_output_aliases={0: 8},
        compiler_params=pltpu.CompilerParams(has_side_effects=_EFFECT),
    )(_hbm(buf))
    return (outs[0:4], outs[4:8]), outs[8], outs[9]


def _gather8_wait(sems, buf, after, name):
    send, recv = sems

    def body(buf_ref, s0, s1, s2, s3, r0, r1, r2, r3, after_ref, out_ref):
        mx, my, mc, chips = _place()
        for (peer, src, dst), s_k, r_k in zip(_gather8_peers(buf_ref, mx, my, mc, chips), (s0, s1, s2, s3), (r0, r1, r2, r3)):
            cp = pltpu.make_async_remote_copy(src_ref=src, dst_ref=dst, send_sem=s_k, recv_sem=r_k,
                                              device_id=peer, device_id_type=_MESH)
            cp.wait_send()
            cp.wait_recv()

    return pl.pallas_call(
        body, name=name, out_shape=pltpu.HBM(buf.shape, buf.dtype),
        in_specs=(_HBM,) + (_SEM,) * 8 + (_ANY,), out_specs=_HBM, input_output_aliases={0: 0},
        compiler_params=pltpu.CompilerParams(has_side_effects=_EFFECT),
    )(buf, *send, *recv, after)


def _gather8_forward(buf, name):
    def body(in_ref, out_ref, send_sems, recv_sems):
        mx, my, mc, chips = _place()
        sibling = (mx, my, 1 - mc)
        cps = []
        for k, chip in enumerate(chips):
            idx = 4 * chip[0] + 2 * chip[1] + mc
            cp = pltpu.make_async_remote_copy(src_ref=in_ref.at[idx], dst_ref=out_ref.at[idx], send_sem=send_sems.at[k],
                                              recv_sem=recv_sems.at[k], device_id=sibling, device_id_type=_MESH)
            cp.start()
            cps.append(cp)
        for k, chip in enumerate(chips):
            other = out_ref.at[4 * chip[0] + 2 * chip[1] + 1 - mc]
            pltpu.make_async_remote_copy(src_ref=other, dst_ref=other, send_sem=send_sems.at[k], recv_sem=recv_sems.at[k],
                                         device_id=sibling, device_id_type=_MESH).wait_recv()
        for cp in cps:
            cp.wait_send()

    return pl.pallas_call(
        body, name=name, out_shape=jax.ShapeDtypeStruct(buf.shape, buf.dtype),
        in_specs=[_ANY], out_specs=_ANY, input_output_aliases={0: 0},
        scratch_shapes=[pltpu.SemaphoreType.DMA((3,)), pltpu.SemaphoreType.DMA((3,))],
    )(buf)


def _scatter_start(pair, name):
    land = lax.empty((3,) + pair.shape[1:], pair.dtype)

    def body(pair_ref, land_ref, s0, s1, s2, r0, r1, r2, pair_thru, land_thru, token):
        mx, my, mc, chips = _place()
        for k, (chip, s_k, r_k) in enumerate(zip(chips, (s0, s1, s2), (r0, r1, r2))):
            pltpu.make_async_remote_copy(
                src_ref=pair_ref.at[2 * chip[0] + chip[1]], dst_ref=land_ref.at[k], send_sem=s_k, recv_sem=r_k,
                device_id=(*chip, mc), device_id_type=_MESH).start()
        token[...] = jnp.zeros_like(token)

    outs = pl.pallas_call(
        body, name=name,
        out_shape=tuple([pltpu.SemaphoreType.DMA(())] * 6 + [pltpu.HBM(pair.shape, pair.dtype), pltpu.HBM(land.shape, land.dtype),
                                                             jax.ShapeDtypeStruct((SUBLANES, LANES), _F32)]),
        in_specs=(_HBM, _HBM), out_specs=tuple([_SEM] * 6 + [_HBM, _HBM, _VMEM_WHOLE]),
        input_output_aliases={0: 6, 1: 7}, compiler_params=pltpu.CompilerParams(has_side_effects=_EFFECT),
    )(_hbm(pair), _hbm(land))
    return (outs[0:3], outs[3:6]), outs[6], outs[7], outs[8]


def _scatter_wait(sems, pair, land, after, name):
    send, recv = sems

    def body(pair_ref, land_ref, s0, s1, s2, r0, r1, r2, after_ref, pair_out, land_out):
        mx, my, mc, chips = _place()
        for k, (chip, s_k, r_k) in enumerate(zip(chips, (s0, s1, s2), (r0, r1, r2))):
            cp = pltpu.make_async_remote_copy(
                src_ref=pair_ref.at[2 * chip[0] + chip[1]], dst_ref=land_ref.at[k], send_sem=s_k, recv_sem=r_k,
                device_id=(*chip, mc), device_id_type=_MESH)
            cp.wait_send()
            cp.wait_recv()

    return pl.pallas_call(
        body, name=name, out_shape=(pltpu.HBM(pair.shape, pair.dtype), pltpu.HBM(land.shape, land.dtype)),
        in_specs=(_HBM, _HBM) + (_SEM,) * 6 + (_ANY,), out_specs=(_HBM, _HBM), input_output_aliases={0: 0, 1: 1},
        compiler_params=pltpu.CompilerParams(has_side_effects=_EFFECT),
    )(pair, land, *send, *recv, after)


def _sibling_copy(src_ref, dst_ref, send_sem, recv_sem):
    mx, my, mc, _ = _place()
    return pltpu.make_async_remote_copy(src_ref=src_ref, dst_ref=dst_ref, send_sem=send_sem, recv_sem=recv_sem,
                                        device_id=(mx, my, 1 - mc), device_id_type=_MESH)


def _swap_start(g, name):
    land = lax.empty(g.shape[1:], g.dtype)

    def body(g_ref, land_ref, send_sem, recv_sem, g_thru, land_thru, token):
        _sibling_copy(g_ref.at[1 - lax.axis_index("c")], land_ref, send_sem, recv_sem).start()
        token[...] = jnp.zeros_like(token)

    outs = pl.pallas_call(
        body, name=name,
        out_shape=(pltpu.SemaphoreType.DMA(()), pltpu.SemaphoreType.DMA(()), pltpu.HBM(g.shape, g.dtype),
                   pltpu.HBM(land.shape, land.dtype), jax.ShapeDtypeStruct((SUBLANES, LANES), _F32)),
        in_specs=(_HBM, _HBM), out_specs=(_SEM, _SEM, _HBM, _HBM, _VMEM_WHOLE), input_output_aliases={0: 2, 1: 3},
        compiler_params=pltpu.CompilerParams(has_side_effects=_EFFECT),
    )(_hbm(g), _hbm(land))
    return (outs[0], outs[1]), outs[2], outs[3], outs[4]


def _swap_wait(sems, g, land, after, name):
    def body(g_ref, land_ref, send_sem, recv_sem, after_ref, g_out, land_out):
        cp = _sibling_copy(g_ref.at[1 - lax.axis_index("c")], land_ref, send_sem, recv_sem)
        cp.wait_send()
        cp.wait_recv()

    return pl.pallas_call(
        body, name=name, out_shape=(pltpu.HBM(g.shape, g.dtype), pltpu.HBM(land.shape, land.dtype)),
        in_specs=(_HBM, _HBM, _SEM, _SEM, _ANY), out_specs=(_HBM, _HBM), input_output_aliases={0: 0, 1: 1},
        compiler_params=pltpu.CompilerParams(has_side_effects=_EFFECT),
    )(g, land, *sems, after)


def _join_start(buf, name):
    def body(buf_ref, send_sem, recv_sem, buf_thru, token):
        mine = buf_ref.at[lax.axis_index("c")]
        _sibling_copy(mine, mine, send_sem, recv_sem).start()
        token[...] = jnp.zeros_like(token)

    outs = pl.pallas_call(
        body, name=name,
        out_shape=(pltpu.SemaphoreType.DMA(()), pltpu.SemaphoreType.DMA(()), pltpu.HBM(buf.shape, buf.dtype),
                   jax.ShapeDtypeStruct((SUBLANES, LANES), _F32)),
        in_specs=(_HBM,), out_specs=(_SEM, _SEM, _HBM, _VMEM_WHOLE), input_output_aliases={0: 2},
        compiler_params=pltpu.CompilerParams(has_side_effects=_EFFECT),
    )(_hbm(buf))
    return (outs[0], outs[1]), outs[2], outs[3]


def _join_wait(sems, buf, after, name):
    def body(buf_ref, send_sem, recv_sem, after_ref, buf_out):
        mc = lax.axis_index("c")
        cp = _sibling_copy(buf_ref.at[mc], buf_ref.at[1 - mc], send_sem, recv_sem)
        cp.wait_send()
        cp.wait_recv()

    return pl.pallas_call(
        body, name=name, out_shape=pltpu.HBM(buf.shape, buf.dtype),
        in_specs=(_HBM, _SEM, _SEM, _ANY), out_specs=_HBM, input_output_aliases={0: 0},
        compiler_params=pltpu.CompilerParams(has_side_effects=_EFFECT),
    )(buf, *sems, after)


def _pair_sum(g, got, sel, name):
    _, four, hr, c = g.shape
    tr = _tile(hr, 512, 16)

    def body(sel_ref, g_ref, p_ref, o_ref):
        o_ref[...] = (g_ref[...].astype(_F32) + p_ref[...].astype(_F32)).astype(o_ref.dtype)

    return pl.pallas_call(
        body, name=name, out_shape=jax.ShapeDtypeStruct((four, hr, c), g.dtype),
        grid_spec=pltpu.PrefetchScalarGridSpec(
            num_scalar_prefetch=1, grid=(four, hr // tr),
            in_specs=[pl.BlockSpec((None, None, tr, c), lambda j, i, s: (s[1], j, i, 0)),
                      pl.BlockSpec((None, tr, c), lambda j, i, s: (j, i, 0))],
            out_specs=pl.BlockSpec((None, tr, c), lambda j, i, s: (j, i, 0))),
        compiler_params=_cp("parallel", "parallel"),
    )(sel, g, got)


def _chip_sum(pair, got, sel, name):
    _, hr, c = pair.shape
    tr = _tile(hr, 512, 16)

    def body(sel_ref, p_ref, q_ref, o_ref):
        o_ref[...] = ((p_ref[...].astype(_F32) + q_ref[0].astype(_F32)) + q_ref[1].astype(_F32)) + q_ref[2].astype(_F32)

    return pl.pallas_call(
        body, name=name, out_shape=jax.ShapeDtypeStruct((2, hr, c), _F32),
        grid_spec=pltpu.PrefetchScalarGridSpec(
            num_scalar_prefetch=1, grid=(hr // tr,),
            in_specs=[pl.BlockSpec((None, tr, c), lambda i, s: (s[0], i, 0)),
                      pl.BlockSpec((3, tr, c), lambda i, s: (0, i, 0))],
            out_specs=pl.BlockSpec((None, tr, c), lambda i, s: (s[1], i, 0))),
        compiler_params=_cp("parallel"),
    )(sel, pair, got)


def _matmul(a, b, dims, out_struct, grid, a_spec, b_spec, o_spec, acc_shape, k_axis, name, after=None):
    nk = grid[k_axis]
    extra = [] if after is None else [after]

    def body(a_ref, b_ref, *rest):
        o_ref, acc = rest[len(extra)], rest[len(extra) + 1:]
        prod = lax.dot_general(a_ref[...].astype(_MXU), b_ref[...].astype(_MXU), dims, preferred_element_type=_F32)
        if nk == 1:
            o_ref[...] = prod.astype(o_ref.dtype)
        else:
            acc_ref, = acc
            k = pl.program_id(k_axis)
            _zero_first(k == 0, acc_ref)
            acc_ref[...] += prod

            @pl.when(k == nk - 1)
            def _():
                o_ref[...] = acc_ref[...].astype(o_ref.dtype)

    sem = ["parallel"] * len(grid)
    sem[k_axis] = "arbitrary"
    return pl.pallas_call(
        body, name=name, out_shape=out_struct, grid=grid, in_specs=[a_spec, b_spec] + [_ANY] * len(extra), out_specs=o_spec,
        scratch_shapes=[pltpu.VMEM(acc_shape, _F32)] if nk > 1 else [], compiler_params=_cp(*sem),
    )(a, b, *extra)


def _mm_nn(a, w4, out_dtype, name, tm=512, tn=1536, tk=2048, after=None):
    m, k = a.shape
    j, _, ns = w4.shape
    tm, tn, tk = _tile(m, tm, 16), _tile(ns, tn), _tile(k, tk)
    nps = ns // tn
    return _matmul(
        a, w4, (((1,), (0,)), ((), ())), jax.ShapeDtypeStruct((m, j * ns), out_dtype),
        (j * nps, m // tm, k // tk),
        pl.BlockSpec((tm, tk), lambda ni, mi, ki: (mi, ki)),
        pl.BlockSpec((None, tk, tn), lambda ni, mi, ki: (ni // nps, ki, ni % nps)),
        pl.BlockSpec((tm, tn), lambda ni, mi, ki: (mi, ni)), (tm, tn), 2, name, after)


def _mm_nt(a, w4, out_dtype, name, tm=512, tn=2048, tk=1536, after=None):
    m = a.shape[-2]
    j, kw, ns = w4.shape
    tm, tn, tk = _tile(m, tm, 16), _tile(kw, tn), _tile(ns, tk)
    kps = ns // tk
    if a.ndim == 3:
        kph = a.shape[2] // tk
        a_spec = pl.BlockSpec((None, tm, tk), lambda ni, mi, ki: (ki // kph, mi, ki % kph))
    else:
        a_spec = pl.BlockSpec((tm, tk), lambda ni, mi, ki: (mi, ki))
    return _matmul(
        a, w4, (((1,), (1,)), ((), ())), jax.ShapeDtypeStruct((m, kw), out_dtype),
        (kw // tn, m // tm, j * kps),
        a_spec,
        pl.BlockSpec((None, tn, tk), lambda ni, mi, ki: (ki // kps, ni, ki % kps)),
        pl.BlockSpec((tm, tn), lambda ni, mi, ki: (mi, ni)), (tm, tn), 2, name, after)


def _mm_tn_cols(a, b, name, tm=1024, tn=1536, tk=2048):
    m, ka = a.shape
    ns = (b.shape[-1] * (2 if b.ndim == 3 else 1)) // 4
    hr = ka // 2
    tm, tn, tk = _tile(hr, tm), _tile(ns, tn), _tile(m, tk, 16)
    mph, nps = hr // tm, ns // tn
    if b.ndim == 3:
        b_spec = pl.BlockSpec((None, tk, tn), lambda ni, mi, ki: (ni // (2 * nps), ki, ni % (2 * nps)))
    else:
        b_spec = pl.BlockSpec((tk, tn), lambda ni, mi, ki: (ki, ni))
    return _matmul(
        a, b, (((0,), (0,)), ((), ())), jax.ShapeDtypeStruct((2, 4, hr, ns), _WIRE),
        (4 * nps, 2 * mph, m // tk),
        pl.BlockSpec((tk, tm), lambda ni, mi, ki: (ki, mi)),
        b_spec,
        pl.BlockSpec((None, None, tm, tn), lambda ni, mi, ki: (mi // mph, ni // nps, mi % mph, ni % nps)),
        (tm, tn), 2, name)


def _mm_tn_rows(a, b, name, tm=1536, tn=1024, tk=2048):
    m, ka = a.shape
    r = ka // 4
    hc = b.shape[1] // 2
    tm, tn, tk = _tile(r, tm), _tile(hc, tn), _tile(m, tk, 16)
    mpr, nph = r // tm, hc // tn
    return _matmul(
        a, b, (((0,), (0,)), ((), ())), jax.ShapeDtypeStruct((2, 4, r, hc), _WIRE),
        (2 * nph, 4 * mpr, m // tk),
        pl.BlockSpec((tk, tm), lambda ni, mi, ki: (ki, mi)),
        pl.BlockSpec((tk, tn), lambda ni, mi, ki: (ki, ni)),
        pl.BlockSpec((None, None, tm, tn), lambda ni, mi, ki: (ni // nph, mi // mpr, mi % mpr, ni % nph)),
        (tm, tn), 2, name)


def _row_call(body, name, rows, ins, outs, tm=256):
    tm = _tile(rows, tm, 16)

    def spec(shape, kind):
        if kind == "rows":
            return pl.BlockSpec((tm, shape[1]), lambda i: (i, 0))
        return pl.BlockSpec(shape, lambda i: (0,) * len(shape))

    return pl.pallas_call(
        body, name=name, grid=(rows // tm,),
        in_specs=[spec(a.shape, kind) for a, kind in ins],
        out_specs=[spec(o.shape, kind) for o, kind in outs],
        out_shape=[o for o, _ in outs],
        compiler_params=_cp("arbitrary"),
    )(*[a for a, _ in ins])


def _rms(x):
    r = lax.rsqrt(_rowmean(x * x) + EPS)
    return x * r, r


def _rms_bwd(dxh, xh, r):
    return r * (dxh - xh * _rowmean(dxh * xh))


def _fwd_pre_mix(x, g, sc, sh):
    s, d = x.shape

    def body(x_ref, g_ref, sc_ref, sh_ref, h_ref):
        xh, _ = _rms(x_ref[...])
        h_ref[...] = (xh * g_ref[...] * (1.0 + sc_ref[...]) + sh_ref[...]).astype(h_ref.dtype)

    return _row_call(body, "fwd_pre_mix", s, [(x, "rows"), (g, "vec"), (sc, "vec"), (sh, "vec")],
                     [(jax.ShapeDtypeStruct((s, d), _MXU), "rows")])[0]


def _fwd_mid(o, x, gt1, g_post, g_pre2, sc2, sh2):
    s, d = x.shape

    def body(o_ref, x_ref, gt_ref, gp_ref, g2_ref, sc_ref, sh_ref, x1_ref, h2_ref):
        oh, _ = _rms(o_ref[...])
        x1 = x_ref[...] + gt_ref[...] * (oh * gp_ref[...])
        x1_ref[...] = x1
        xh, _ = _rms(x1)
        h2_ref[...] = (xh * g2_ref[...] * (1.0 + sc_ref[...]) + sh_ref[...]).astype(h2_ref.dtype)

    return _row_call(body, "fwd_mid", s,
                     [(o, "rows"), (x, "rows"), (gt1, "vec"), (g_post, "vec"), (g_pre2, "vec"), (sc2, "vec"),
                      (sh2, "vec")],
                     [(jax.ShapeDtypeStruct((s, d), _F32), "rows"), (jax.ShapeDtypeStruct((s, d), _MXU), "rows")])


def _loss_and_post_ffn_bwd(f, x1, tgt, gt2, g_post):
    s, d = x1.shape

    def body(f_ref, x1_ref, t_ref, gt_ref, g_ref, dx2_ref, df_ref, dgt_ref, dg_ref, loss_ref):
        first = pl.program_id(0) == 0
        _zero_first(first, dgt_ref, dg_ref, loss_ref)
        fh, r = _rms(f_ref[...])
        n = fh * g_ref[...]
        e = x1_ref[...] + gt_ref[...] * n - t_ref[...]
        _acc(loss_ref, first, jnp.sum(_colsum(e * e), axis=1, keepdims=True) * (0.5 / d))
        dx2 = e * (1.0 / d)
        dx2_ref[...] = dx2
        _acc(dgt_ref, first, _colsum(dx2 * n))
        dn = dx2 * gt_ref[...]
        _acc(dg_ref, first, _colsum(dn * fh))
        df_ref[...] = _rms_bwd(dn * g_ref[...], fh, r).astype(df_ref.dtype)

    vec = jax.ShapeDtypeStruct((1, d), _F32)
    return _row_call(body, "loss_post_ffn_bwd", s,
                     [(f, "rows"), (x1, "rows"), (tgt, "rows"), (gt2, "vec"), (g_post, "vec")],
                     [(jax.ShapeDtypeStruct((s, d), _F32), "rows"), (jax.ShapeDtypeStruct((s, d), _MXU), "rows"),
                      (vec, "vec"), (vec, "vec"), (jax.ShapeDtypeStruct((1, 1), _F32), "vec")])


def _bwd_mid(dh2, x1, dx2, o, g_pre2, sc2, gt1, g_post):
    s, d = x1.shape

    def body(dh_ref, x1_ref, dx2_ref, o_ref, g2_ref, sc_ref, gt_ref, gp_ref,
             dx1_ref, do_ref, dsc_ref, dsh_ref, dg2_ref, dgt_ref, dgp_ref):
        first = pl.program_id(0) == 0
        _zero_first(first, dsc_ref, dsh_ref, dg2_ref, dgt_ref, dgp_ref)
        dh = dh_ref[...]
        xh, r = _rms(x1_ref[...])
        _acc(dsh_ref, first, _colsum(dh))
        _acc(dsc_ref, first, _colsum(dh * (xh * g2_ref[...])))
        dn = dh * (1.0 + sc_ref[...])
        _acc(dg2_ref, first, _colsum(dn * xh))
        dx1 = dx2_ref[...] + _rms_bwd(dn * g2_ref[...], xh, r)
        dx1_ref[...] = dx1
        oh, ro = _rms(o_ref[...])
        _acc(dgt_ref, first, _colsum(dx1 * (oh * gp_ref[...])))
        dno = dx1 * gt_ref[...]
        _acc(dgp_ref, first, _colsum(dno * oh))
        do_ref[...] = _rms_bwd(dno * gp_ref[...], oh, ro).astype(do_ref.dtype)

    vec = jax.ShapeDtypeStruct((1, d), _F32)
    return _row_call(body, "bwd_mid", s,
                     [(dh2, "rows"), (x1, "rows"), (dx2, "rows"), (o, "rows"), (g_pre2, "vec"), (sc2, "vec"),
                      (gt1, "vec"), (g_post, "vec")],
                     [(jax.ShapeDtypeStruct((s, d), _F32), "rows"), (jax.ShapeDtypeStruct((s, d), _MXU), "rows"),
                      (vec, "vec"), (vec, "vec"), (vec, "vec"), (vec, "vec"), (vec, "vec")])


def _bwd_pre_mix(dh1, x, dx1, g, sc1):
    s, d = x.shape

    def body(dh_ref, x_ref, dx1_ref, g_ref, sc_ref, dx_ref, dsc_ref, dsh_ref, dg_ref):
        first = pl.program_id(0) == 0
        _zero_first(first, dsc_ref, dsh_ref, dg_ref)
        dh = dh_ref[...]
        xh, r = _rms(x_ref[...])
        _acc(dsh_ref, first, _colsum(dh))
        _acc(dsc_ref, first, _colsum(dh * (xh * g_ref[...])))
        dn = dh * (1.0 + sc_ref[...])
        _acc(dg_ref, first, _colsum(dn * xh))
        dx_ref[...] = dx1_ref[...] + _rms_bwd(dn * g_ref[...], xh, r)

    vec = jax.ShapeDtypeStruct((1, d), _F32)
    return _row_call(body, "bwd_pre_mix", s,
                     [(dh1, "rows"), (x, "rows"), (dx1, "rows"), (g, "vec"), (sc1, "vec")],
                     [(jax.ShapeDtypeStruct((s, d), _F32), "rows"), (vec, "vec"), (vec, "vec"), (vec, "vec")])


def _mix_norm_fwd(y_ssm, y_sgu, g_ssm, g_sgu):
    s, h = y_ssm.shape

    def body(a_ref, b_ref, ga_ref, gb_ref, o_ref):
        ah, _ = _rms(a_ref[...])
        bh, _ = _rms(b_ref[...])
        o_ref[:, 0:h] = (ah * ga_ref[...]).astype(o_ref.dtype)
        o_ref[:, h:2 * h] = (bh * gb_ref[...]).astype(o_ref.dtype)

    return _row_call(body, "mix_norm_fwd", s, [(y_ssm, "rows"), (y_sgu, "rows"), (g_ssm, "vec"), (g_sgu, "vec")],
                     [(jax.ShapeDtypeStruct((s, 2 * h), _MXU), "rows")])[0]


def _mix_norm_bwd(dyc, y_ssm, y_sgu, g_ssm, g_sgu):
    s, h = y_ssm.shape

    def body(d_ref, a_ref, b_ref, ga_ref, gb_ref, da_ref, db_ref, dga_ref, dgb_ref):
        first = pl.program_id(0) == 0
        _zero_first(first, dga_ref, dgb_ref)
        for lo, y_ref, g_ref, dy_ref, dg_ref in ((0, a_ref, ga_ref, da_ref, dga_ref), (h, b_ref, gb_ref, db_ref, dgb_ref)):
            d = d_ref[:, lo:lo + h]
            yh, r = _rms(y_ref[...])
            _acc(dg_ref, first, _colsum(d * yh))
            dy_ref[...] = _rms_bwd(d * g_ref[...], yh, r)

    vec = jax.ShapeDtypeStruct((1, h), _F32)
    full = jax.ShapeDtypeStruct((s, h), _F32)
    return _row_call(body, "mix_norm_bwd", s,
                     [(dyc, "rows"), (y_ssm, "rows"), (y_sgu, "rows"), (g_ssm, "vec"), (g_sgu, "vec")],
                     [(full, "rows"), (full, "rows"), (vec, "vec"), (vec, "vec")])


CONV_ROWS = 64


def _conv_rows(ext, w_ref, b_ref):
    x = ext[SUBLANES:]
    s1 = pltpu.roll(ext, 1, 0)[SUBLANES:]
    s2 = pltpu.roll(ext, 2, 0)[SUBLANES:]
    return b_ref[...] + w_ref[0:1, :] * s2 + w_ref[1:2, :] * s1 + w_ref[2:3, :] * x, x, s1, s2


def _conv_window(x_ref, r0):
    if isinstance(r0, int):
        assert r0 == 0
        return jnp.concatenate([jnp.zeros((SUBLANES, x_ref.shape[1]), _F32), x_ref[0:CONV_ROWS, :]], axis=0)
    return x_ref[pl.ds(pl.multiple_of(r0 - SUBLANES, SUBLANES), CONV_ROWS + SUBLANES), :]


def _conv_act_fwd(up_pre, conv_w, conv_b):
    s, f2 = up_pre.shape
    f = f2 // 2
    tc = _tile(f, 256)
    nf = f // tc

    def shift_down(x, k):
        row = lax.broadcasted_iota(jnp.int32, x.shape, 0)
        return jnp.where(row >= k, pltpu.roll(x, k, 0), 0.0)

    def conv(x, w_ref, b_ref):
        return b_ref[...] + w_ref[0:1, :] * shift_down(x, 2) + w_ref[1:2, :] * shift_down(x, 1) + w_ref[2:3, :] * x

    def body(a_ref, b_ref, wa_ref, wb_ref, ba_ref, bb_ref, o_ref):
        a = conv(a_ref[...], wa_ref, ba_ref)
        b = conv(b_ref[...], wb_ref, bb_ref)
        o_ref[...] = (a * _sigmoid(a) * b).astype(o_ref.dtype)

    return pl.pallas_call(
        body, name="conv_act_fwd", grid=(nf,), out_shape=jax.ShapeDtypeStruct((s, f), _MXU),
        in_specs=[pl.BlockSpec((s, tc), lambda n: (0, n)), pl.BlockSpec((s, tc), lambda n: (0, n + nf)),
                  pl.BlockSpec((3, tc), lambda n: (0, n)), pl.BlockSpec((3, tc), lambda n: (0, n + nf)),
                  pl.BlockSpec((1, tc), lambda n: (0, n)), pl.BlockSpec((1, tc), lambda n: (0, n + nf))],
        out_specs=pl.BlockSpec((s, tc), lambda n: (0, n)), compiler_params=_cp("parallel"),
    )(up_pre, up_pre, conv_w, conv_w, conv_b, conv_b)


def _conv_act_bwd(up_pre, d_act, conv_w, conv_b):
    s, f2 = up_pre.shape
    f = f2 // 2
    tc = _tile(f, 256)
    nf = f // tc

    def body(a_ref, b_ref, d_ref, wa_ref, wb_ref, ba_ref, bb_ref,
             du_ref, w0a, w0b, w1a, w1b, w2a, w2b, dba, dbb):
        n = s // CONV_ROWS
        zero8 = jnp.zeros((SUBLANES, tc), _F32)
        ext_rows = CONV_ROWS + SUBLANES

        def fold(x):
            out = x[0:SUBLANES]
            for k in range(1, CONV_ROWS // SUBLANES):
                out = out + x[k * SUBLANES:(k + 1) * SUBLANES]
            return out

        def chunk(r0, carry):
            nxt, acc = carry
            a, xa, xa1, xa2 = _conv_rows(_conv_window(a_ref, r0), wa_ref, ba_ref)
            b, xb, xb1, xb2 = _conv_rows(_conv_window(b_ref, r0), wb_ref, bb_ref)
            sg = _sigmoid(a)
            d = d_ref[pl.ds(r0, CONV_ROWS), :]
            du_a = d * b * (sg * (1.0 + a * (1.0 - sg)))
            du_b = d * (a * sg)
            new_acc = []
            for h, (du, x0, x1, x2, w_ref) in enumerate(((du_a, xa, xa1, xa2, wa_ref), (du_b, xb, xb1, xb2, wb_ref))):
                ext = jnp.concatenate([du, nxt[h]], axis=0)
                u1 = pltpu.roll(ext, ext_rows - 1, 0)[:CONV_ROWS]
                u2 = pltpu.roll(ext, ext_rows - 2, 0)[:CONV_ROWS]
                du_ref[h, pl.ds(r0, CONV_ROWS), :] = (w_ref[2:3, :] * du + w_ref[1:2, :] * u1
                                                      + w_ref[0:1, :] * u2).astype(du_ref.dtype)
                new_acc += [acc[4 * h] + fold(du * x2), acc[4 * h + 1] + fold(du * x1), acc[4 * h + 2] + fold(du * x0),
                            acc[4 * h + 3] + fold(du)]
            return (du_a[:SUBLANES], du_b[:SUBLANES]), tuple(new_acc)

        def step(i, carry):
            return chunk(pl.multiple_of((n - 1 - i) * CONV_ROWS, CONV_ROWS), carry)

        carry = lax.fori_loop(0, n - 1, step, ((zero8, zero8), (zero8,) * 8))
        _, acc = chunk(0, carry)
        for ref, val in zip((w0a, w1a, w2a, dba, w0b, w1b, w2b, dbb), acc):
            ref[...] = _colsum(val)

    col_a = pl.BlockSpec((s, tc), lambda n: (0, n))
    col_b = pl.BlockSpec((s, tc), lambda n: (0, n + nf))
    vec_a = pl.BlockSpec((1, tc), lambda n: (0, n))
    vec_b = pl.BlockSpec((1, tc), lambda n: (0, n + nf))
    vec = jax.ShapeDtypeStruct((1, f), _F32)
    outs = pl.pallas_call(
        body, name="conv_act_bwd", grid=(nf,),
        in_specs=[col_a, col_b, col_a, pl.BlockSpec((3, tc), lambda n: (0, n)),
                  pl.BlockSpec((3, tc), lambda n: (0, n + nf)), vec_a, vec_b],
        out_specs=[pl.BlockSpec((2, s, tc), lambda n: (0, 0, n))] + [vec_a] * 8,
        out_shape=[jax.ShapeDtypeStruct((2, s, f), _MXU)] + [vec] * 8, compiler_params=_cp("parallel"),
    )(up_pre, up_pre, d_act, conv_w, conv_w, conv_b, conv_b)
    du, w0a, w0b, w1a, w1b, w2a, w2b, dba, dbb = outs
    cat = lambda p, q: jnp.concatenate([p, q], axis=1)
    return du, cat(w0a, w0b), cat(w1a, w1b), cat(w2a, w2b), cat(dba, dbb)


def _sgu_recompute(zu_ref, zv_ref, lng_ref, lnb_ref, wm_ref, bs_ref, nh):
    zu, zv = zu_ref[...], zv_ref[...]
    u = _gelu(zu)
    gv = _gelu(zv)
    xc = gv - _rowmean(gv)
    rs = lax.rsqrt(_rowmean(xc * xc) + EPS)
    vh = xc * rs
    v = vh * lng_ref[...] + lnb_ref[...]
    mixed = []
    for h in range(nh):
        vhd = v[:, h * CHUNK:(h + 1) * CHUNK].astype(_MXU)
        mixed.append(jnp.dot(wm_ref[h].astype(_MXU), vhd, preferred_element_type=_F32) + bs_ref[h])
    return zu, zv, u, vh, rs, v, mixed


def _sgu_fwd(z, ln_g, ln_b, wm, bs):
    s = z.shape[0]
    nh = wm.shape[0]
    hd = nh * CHUNK

    def body(zu_ref, zv_ref, lng_ref, lnb_ref, wm_ref, bs_ref, y_ref):
        _, _, u, _, _, _, mixed = _sgu_recompute(zu_ref, zv_ref, lng_ref, lnb_ref, wm_ref, bs_ref, nh)
        for h in range(nh):
            y_ref[:, h * CHUNK:(h + 1) * CHUNK] = u[:, h * CHUNK:(h + 1) * CHUNK] * mixed[h]

    vec = pl.BlockSpec((1, hd), lambda i: (0, 0))
    return pl.pallas_call(
        body, name="sgu_fwd", grid=(s // CHUNK,), out_shape=jax.ShapeDtypeStruct((s, hd), _F32),
        in_specs=[pl.BlockSpec((CHUNK, hd), lambda i: (i, 1)), pl.BlockSpec((CHUNK, hd), lambda i: (i, 2)), vec, vec,
                  pl.BlockSpec((nh, CHUNK, CHUNK), lambda i: (0, 0, 0)), pl.BlockSpec((nh, CHUNK, 1), lambda i: (0, 0, 0))],
        out_specs=pl.BlockSpec((CHUNK, hd), lambda i: (i, 0)), compiler_params=_cp("parallel"),
    )(z, z, ln_g, ln_b, wm, bs)


def _sgu_bwd(z, dy, dz_ssm, ln_g, ln_b, wm, bs):
    s = z.shape[0]
    nh = wm.shape[0]
    hd = nh * CHUNK

    def body(zu_ref, zv_ref, dy_ref, dzs_ref, lng_ref, lnb_ref, wm_ref, bs_ref,
             dz_ref, dlg_ref, dlb_ref, dwm_ref, dbs_ref, dv_scr):
        first = pl.program_id(0) == 0
        _zero_first(first, dlg_ref, dlb_ref, dwm_ref, dbs_ref)
        zu, zv, u, vh, rs, v, mixed = _sgu_recompute(zu_ref, zv_ref, lng_ref, lnb_ref, wm_ref, bs_ref, nh)
        dy = dy_ref[...]
        dz_ref[:, 0:hd] = dzs_ref[...].astype(dz_ref.dtype)
        for h in range(nh):
            cols = slice(h * CHUNK, (h + 1) * CHUNK)
            dyh = dy[:, cols]
            dz_ref[:, hd + h * CHUNK:hd + (h + 1) * CHUNK] = (dyh * mixed[h] * _gelu_grad(zu[:, cols])).astype(dz_ref.dtype)
            dm = dyh * u[:, cols]
            dmx = dm.astype(_MXU)
            _acc(dbs_ref.at[h], first, jnp.sum(dm, axis=1, keepdims=True))
            _acc(dwm_ref.at[h], first,
                 lax.dot_general(dmx, v[:, cols].astype(_MXU), (((1,), (1,)), ((), ())), preferred_element_type=_F32))
            dv_scr[:, cols] = lax.dot_general(wm_ref[h].astype(_MXU), dmx, (((0,), (0,)), ((), ())),
                                              preferred_element_type=_F32)
        dv = dv_scr[...]
        _acc(dlg_ref, first, _colsum(dv * vh))
        _acc(dlb_ref, first, _colsum(dv))
        dvh = dv * lng_ref[...]
        dgv = rs * (dvh - _rowmean(dvh) - vh * _rowmean(dvh * vh))
        dz_ref[:, 2 * hd:3 * hd] = (dgv * _gelu_grad(zv)).astype(dz_ref.dtype)

    vec = pl.BlockSpec((1, hd), lambda i: (0, 0))
    wspec = pl.BlockSpec((nh, CHUNK, CHUNK), lambda i: (0, 0, 0))
    bspec = pl.BlockSpec((nh, CHUNK, 1), lambda i: (0, 0, 0))
    rows = pl.BlockSpec((CHUNK, hd), lambda i: (i, 0))
    return pl.pallas_call(
        body, name="sgu_bwd", grid=(s // CHUNK,),
        out_shape=[jax.ShapeDtypeStruct((s, 3 * hd), _MXU), jax.ShapeDtypeStruct((1, hd), _F32),
                   jax.ShapeDtypeStruct((1, hd), _F32), jax.ShapeDtypeStruct((nh, CHUNK, CHUNK), _F32),
                   jax.ShapeDtypeStruct((nh, CHUNK, 1), _F32)],
        in_specs=[pl.BlockSpec((CHUNK, hd), lambda i: (i, 1)), pl.BlockSpec((CHUNK, hd), lambda i: (i, 2)),
                  rows, rows, vec, vec, wspec, bspec],
        out_specs=[pl.BlockSpec((CHUNK, 3 * hd), lambda i: (i, 0)), vec, vec, wspec, bspec],
        scratch_shapes=[pltpu.VMEM((CHUNK, hd), _F32)], compiler_params=_cp("arbitrary"),
    )(z, z, dy, dz_ssm, ln_g, ln_b, wm, bs)


def _ssm_prep(log_dt, a_re, a_im, b_re_t, b_im_t, kvec):
    gn = a_re.shape[1]

    def body(ldt_ref, are_ref, aim_ref, br_ref, bi_ref, k_ref, pr_ref, pi_ref, bbr_ref, bbi_ref):
        dt = jnp.exp(ldt_ref[...])
        are, aim = are_ref[...], aim_ref[...]
        k = k_ref[...]
        mag = jnp.exp(k * (are * dt))
        ang = k * (aim * dt)
        pr_ref[...] = mag * jnp.cos(ang)
        pi_ref[...] = mag * jnp.sin(ang)
        m1 = jnp.exp(are * dt)
        lr, li = m1 * jnp.cos(aim * dt), m1 * jnp.sin(aim * dt)
        den = are * are + aim * aim
        nr = lr - 1.0
        f_re = (nr * are + li * aim) / den
        f_im = (li * are - nr * aim) / den
        bbr_ref[...] = f_re * br_ref[...] - f_im * bi_ref[...]
        bbi_ref[...] = f_re * bi_ref[...] + f_im * br_ref[...]

    pw = jax.ShapeDtypeStruct((kvec.shape[0], gn), _F32)
    bb = jax.ShapeDtypeStruct(b_re_t.shape, _F32)
    return pl.pallas_call(body, name="ssm_prep", out_shape=[pw, pw, bb, bb])(log_dt, a_re, a_im, b_re_t, b_im_t, kvec)


def _ssm_prep_bwd(log_dt, a_re, a_im, b_re_t, b_im_t, d_bbr, d_bbi, d_lr, d_li):
    def body(ldt_ref, are_ref, aim_ref, br_ref, bi_ref, dbr_ref, dbi_ref, dlr_ref, dli_ref,
             obr_ref, obi_ref, oar_ref, oai_ref, odt_ref):
        dt = jnp.exp(ldt_ref[...])
        are, aim = are_ref[...], aim_ref[...]
        m1 = jnp.exp(are * dt)
        lr, li = m1 * jnp.cos(aim * dt), m1 * jnp.sin(aim * dt)
        den = are * are + aim * aim
        nr = lr - 1.0
        f_re = (nr * are + li * aim) / den
        f_im = (li * are - nr * aim) / den
        br, bi, dbr, dbi = br_ref[...], bi_ref[...], dbr_ref[...], dbi_ref[...]
        obr_ref[...] = f_re * dbr + f_im * dbi
        obi_ref[...] = f_re * dbi - f_im * dbr
        gf_re = _colsum(br * dbr + bi * dbi)
        gf_im = _colsum(br * dbi - bi * dbr)
        il_re, il_im = are / den, -aim / den
        glb_re = dlr_ref[...] + (il_re * gf_re + il_im * gf_im)
        glb_im = dli_ref[...] + (il_re * gf_im - il_im * gf_re)
        q_re = -(f_re * il_re - f_im * il_im)
        q_im = -(f_re * il_im + f_im * il_re)
        gl_re = q_re * gf_re + q_im * gf_im
        gl_im = q_re * gf_im - q_im * gf_re
        gl_re = gl_re + dt * (lr * glb_re + li * glb_im)
        gl_im = gl_im + dt * (lr * glb_im - li * glb_re)
        w_re = are * lr - aim * li
        w_im = are * li + aim * lr
        oar_ref[...] = gl_re
        oai_ref[...] = gl_im
        odt_ref[...] = w_re * glb_re + w_im * glb_im

    bb = jax.ShapeDtypeStruct(b_re_t.shape, _F32)
    v = jax.ShapeDtypeStruct(a_re.shape, _F32)
    return pl.pallas_call(body, name="ssm_prep_bwd", out_shape=[bb, bb, v, v, v])(
        log_dt, a_re, a_im, b_re_t, b_im_t, d_bbr, d_bbi, d_lr, d_li)


def _group_sum(d_dt, log_dt):
    def body(d_ref, l_ref, o_ref):
        o_ref[...] = jnp.sum(d_ref[...], axis=1, keepdims=True) * jnp.exp(l_ref[...])

    return pl.pallas_call(body, name="ssm_dt_grad", out_shape=jax.ShapeDtypeStruct(log_dt.shape, _F32))(d_dt, log_dt)


def _load_strided(ref, nr):
    return jnp.concatenate([ref[pl.ds(r, SUBLANES, stride=nr), :] for r in range(nr)], axis=0)


def _store_strided(ref, val, nr):
    for r in range(nr):
        ref[pl.ds(r, SUBLANES, stride=nr), :] = val[r * SUBLANES:(r + 1) * SUBLANES]


def _scan_strided(src_ref, dst_ref, nr, p_ref, carry, reverse, h_ref=None, h_in=None):
    ns = BLOCK_ST
    row = lax.broadcasted_iota(jnp.int32, (SUBLANES, ns), 0)
    bc = lambda v: jnp.broadcast_to(v, (SUBLANES, ns))
    tile = lambda ref, r: (ref[r * SUBLANES:(r + 1) * SUBLANES, 0:ns], ref[r * SUBLANES:(r + 1) * SUBLANES, ns:2 * ns])
    one = nr - 1 if reverse else 0
    ar, ai = bc(p_ref[one:one + 1, 0:ns]), bc(p_ref[one:one + 1, ns:2 * ns])
    xr = xi = None
    for r in (range(nr - 1, -1, -1) if reverse else range(nr)):
        sr, si = tile(src_ref, r)
        xr, xi = (sr, si) if xr is None else (ar * xr - ai * xi + sr, ar * xi + ai * xr + si)
        dst_ref[r * SUBLANES:(r + 1) * SUBLANES, 0:ns] = xr
        dst_ref[r * SUBLANES:(r + 1) * SUBLANES, ns:2 * ns] = xi
    edge, shift = (SUBLANES - 1, SUBLANES - 1) if reverse else (0, 1)
    dr = jnp.where(row == edge, carry[0], pltpu.roll(xr, shift, 0))
    di = jnp.where(row == edge, carry[1], pltpu.roll(xi, shift, 0))
    for i, k in enumerate((1, 2, 4)):
        qr, qi = bc(p_ref[nr + i:nr + i + 1, 0:ns]), bc(p_ref[nr + i:nr + i + 1, ns:2 * ns])
        keep = (row < SUBLANES - k) if reverse else (row >= k)
        sr = jnp.where(keep, pltpu.roll(dr, (SUBLANES - k) if reverse else k, 0), 0.0)
        si = jnp.where(keep, pltpu.roll(di, (SUBLANES - k) if reverse else k, 0), 0.0)
        dr, di = dr + qr * sr - qi * si, di + qr * si + qi * sr
    acc_r = acc_i = jnp.zeros((SUBLANES, ns), _F32)
    out = None
    for r in range(nr):
        wr, wi = p_ref[r:r + 1, 0:ns], p_ref[r:r + 1, ns:2 * ns]
        xr, xi = tile(dst_ref, r)
        xr, xi = xr + wr * dr - wi * di, xi + wr * di + wi * dr
        dst_ref[r * SUBLANES:(r + 1) * SUBLANES, 0:ns] = xr
        dst_ref[r * SUBLANES:(r + 1) * SUBLANES, ns:2 * ns] = xi
        if h_ref is not None:
            if r == 0:
                lr, li = tile(h_ref, nr - 1)
                pr, pi = jnp.where(row == 0, h_in[0], pltpu.roll(lr, 1, 0)), jnp.where(row == 0, h_in[1], pltpu.roll(li, 1, 0))
            else:
                pr, pi = tile(h_ref, r - 1)
            acc_r = acc_r + (xr * pr + xi * pi)
            acc_i = acc_i + (xi * pr - xr * pi)
        if r == (0 if reverse else nr - 1):
            out = (xr[0:1, :], xi[0:1, :]) if reverse else (xr[SUBLANES - 1:SUBLANES, :], xi[SUBLANES - 1:SUBLANES, :])
    if h_ref is None:
        return out
    return out, (_colsum(acc_r), _colsum(acc_i))


def _ssm_gate(y, wg_ref, bg_ref):
    yg = _gelu(y)
    gate = _sigmoid(jnp.dot(yg.astype(_MXU), wg_ref[...].astype(_MXU), preferred_element_type=_F32) + bg_ref[...])
    return yg, gate


def _ssm_specs(nb, nt, t, reverse):
    tt = (lambda ti: nt - 1 - ti) if reverse else (lambda ti: ti)
    ns2 = 2 * BLOCK_ST
    return dict(
        z=pl.BlockSpec((t, BLOCK_CH), lambda b, ti: (tt(ti), b)),
        bbt=pl.BlockSpec((None, BLOCK_CH, ns2), lambda b, ti: (b, 0, 0)),
        ct=pl.BlockSpec((None, ns2, BLOCK_CH), lambda b, ti: (b, 0, 0)),
        vec=pl.BlockSpec((1, BLOCK_CH), lambda b, ti: (0, b)),
        wg=pl.BlockSpec((None, BLOCK_CH, BLOCK_CH), lambda b, ti: (b, 0, 0)),
        p=pl.BlockSpec((None, t // SUBLANES + SUBLANES, ns2), lambda b, ti: (b, 0, 0)),
        hb=pl.BlockSpec((None, None, SUBLANES, ns2), lambda b, ti: (b, tt(ti), 0, 0)),
        h=pl.BlockSpec((None, t, ns2), lambda b, ti: (b, tt(ti), 0)),
        acc_vec=pl.BlockSpec((None, 1, ns2), lambda b, ti: (b, 0, 0)),
    )


def _ssm_fwd(z, bbt, ct, dvec, wg, bglu, ptab):
    s = z.shape[0]
    nb = bbt.shape[0]
    t = _tile(s, TIME_TILE, SUBLANES)
    nt = s // t
    ns = BLOCK_ST
    sp = _ssm_specs(nb, nt, t, False)

    nr = t // SUBLANES

    def body(z_ref, bbt_ref, ct_ref, d_ref, wg_ref, bg_ref, p_ref, y2_ref, y_ref, h_ref, hb_ref, bu_scr, h_scr, carry_scr):
        _zero_first(pl.program_id(1) == 0, carry_scr)
        hb_ref[...] = carry_scr[...]
        carry_in = (carry_scr[0:1, 0:ns], carry_scr[0:1, ns:2 * ns])
        u = _load_strided(z_ref, nr)
        bu_scr[...] = jnp.dot(u.astype(_MXU), bbt_ref[...].astype(_MXU), preferred_element_type=_F32)
        cr, ci = _scan_strided(bu_scr, h_scr, nr, p_ref, carry_in, False)
        hx = h_scr[...].astype(_MXU)
        h_ref[...] = hx
        y = jnp.dot(hx, ct_ref[...].astype(_MXU), preferred_element_type=_F32) + d_ref[...] * u
        yg, gate = _ssm_gate(y, wg_ref, bg_ref)
        _store_strided(y2_ref, yg * gate, nr)
        _store_strided(y_ref, y, nr)
        carry_scr[:, 0:ns] = jnp.broadcast_to(cr, (SUBLANES, ns))
        carry_scr[:, ns:2 * ns] = jnp.broadcast_to(ci, (SUBLANES, ns))

    ych = jax.ShapeDtypeStruct((s, nb * BLOCK_CH), _F32)
    return pl.pallas_call(
        body, name="ssm_fwd", grid=(nb, nt),
        out_shape=[ych, ych, jax.ShapeDtypeStruct((nb, s, 2 * ns), _MXU),
                   jax.ShapeDtypeStruct((nb, nt, SUBLANES, 2 * ns), _F32)],
        in_specs=[sp["z"], sp["bbt"], sp["ct"], sp["vec"], sp["wg"], sp["vec"], sp["p"]],
        out_specs=[sp["z"], sp["z"], sp["h"], sp["hb"]],
        scratch_shapes=[pltpu.VMEM((t, 2 * ns), _F32), pltpu.VMEM((t, 2 * ns), _F32), pltpu.VMEM((SUBLANES, 2 * ns), _F32)],
        compiler_params=_cp("parallel", "arbitrary"),
    )(z, bbt, ct, dvec, wg, bglu, ptab)


def _ssm_bwd(z, y_pre, h_all, dy2, hb, bbt, ct, dvec, wg, bglu, ptab_rev):
    s = z.shape[0]
    nb = bbt.shape[0]
    t = _tile(s, TIME_TILE, SUBLANES)
    nt = s // t
    ns = BLOCK_ST
    sp = _ssm_specs(nb, nt, t, True)
    tn_dims = (((0,), (0,)), ((), ()))
    nt_dims = (((1,), (1,)), ((), ()))

    nr = t // SUBLANES

    def body(z_ref, y_ref, h_ref, dy2_ref, hb_ref, bbt_ref, ct_ref, d_ref, wg_ref, bg_ref, pr_ref,
             dz_ref, dbbt_ref, dct_ref, dwg_ref, dlb_ref, dd_ref, dbg_ref, bu_scr, g_scr, h_scr, gcarry_scr):
        first = pl.program_id(1) == 0

        _zero_first(first, gcarry_scr, dbbt_ref, dct_ref, dwg_ref, dlb_ref, dd_ref, dbg_ref)
        u = _load_strided(z_ref, nr)
        hin = hb_ref[...]
        y = _load_strided(y_ref, nr)
        yg, gate = _ssm_gate(y, wg_ref, bg_ref)
        dy2 = _load_strided(dy2_ref, nr)
        dpre = dy2 * yg * gate * (1.0 - gate)
        _acc(dbg_ref, first, _colsum(dpre))
        dpx = dpre.astype(_MXU)
        _acc(dwg_ref, first, lax.dot_general(yg.astype(_MXU), dpx, tn_dims, preferred_element_type=_F32))
        dyg = dy2 * gate + lax.dot_general(dpx, wg_ref[...].astype(_MXU), nt_dims, preferred_element_type=_F32)
        dy = dyg * _gelu_grad(y)
        _acc(dd_ref, first, _colsum(dy * u))
        dyx = dy.astype(_MXU)
        hx = h_ref[...]
        h_scr[...] = hx.astype(_F32)
        _acc(dct_ref, first, lax.dot_general(hx, dyx, tn_dims, preferred_element_type=_F32))
        bu_scr[...] = lax.dot_general(dyx, ct_ref[...].astype(_MXU), nt_dims, preferred_element_type=_F32)
        gin = (gcarry_scr[0:1, 0:ns], gcarry_scr[0:1, ns:2 * ns])
        (gr, gi), (d_ar, d_ai) = _scan_strided(bu_scr, g_scr, nr, pr_ref, gin, True, h_scr,
                                               (hin[0:1, 0:ns], hin[0:1, ns:2 * ns]))
        gcarry_scr[:, 0:ns] = jnp.broadcast_to(gr, (SUBLANES, ns))
        gcarry_scr[:, ns:2 * ns] = jnp.broadcast_to(gi, (SUBLANES, ns))
        _acc(dlb_ref, first, jnp.concatenate([d_ar, d_ai], axis=1))
        gx = g_scr[...].astype(_MXU)
        _acc(dbbt_ref, first, lax.dot_general(u.astype(_MXU), gx, tn_dims, preferred_element_type=_F32))
        _store_strided(dz_ref, dy * d_ref[...] + lax.dot_general(gx, bbt_ref[...].astype(_MXU), nt_dims,
                                                                 preferred_element_type=_F32), nr)

    f = lambda shape: jax.ShapeDtypeStruct(shape, _F32)
    return pl.pallas_call(
        body, name="ssm_bwd", grid=(nb, nt),
        out_shape=[f((s, nb * BLOCK_CH)), f(bbt.shape), f(ct.shape), f(wg.shape), f((nb, 1, 2 * ns)),
                   f((1, nb * BLOCK_CH)), f((1, nb * BLOCK_CH))],
        in_specs=[sp["z"], sp["z"], sp["h"], sp["z"], sp["hb"], sp["bbt"], sp["ct"], sp["vec"], sp["wg"], sp["vec"], sp["p"]],
        out_specs=[sp["z"], sp["bbt"], sp["ct"], sp["wg"], sp["acc_vec"], sp["vec"], sp["vec"]],
        scratch_shapes=[pltpu.VMEM((t, 2 * ns), _F32), pltpu.VMEM((t, 2 * ns), _F32), pltpu.VMEM((t, 2 * ns), _F32),
                        pltpu.VMEM((SUBLANES, 2 * ns), _F32)],
        compiler_params=_cp("parallel", "arbitrary"),
    )(z, y_pre, h_all, dy2, hb, bbt, ct, dvec, wg, bglu, ptab_rev)


def _mod_part(c_all, w, b):
    d, ns = w.shape
    tn = _tile(ns, 512)

    def body(c_ref, w_ref, b_ref, o_ref):
        c = c_ref[...]
        ca = (c * _sigmoid(c)).astype(_MXU)
        o_ref[...] = jnp.dot(ca, w_ref[...].astype(_MXU), preferred_element_type=_F32) + b_ref[...]

    return pl.pallas_call(
        body, name="mod_part", grid=(ns // tn,), out_shape=jax.ShapeDtypeStruct((8, ns), _F32),
        in_specs=[pl.BlockSpec((8, d), lambda n: (0, 0)), pl.BlockSpec((d, tn), lambda n: (0, n)),
                  pl.BlockSpec((1, tn), lambda n: (0, n))],
        out_specs=pl.BlockSpec((8, tn), lambda n: (0, n)), compiler_params=_cp("parallel"),
    )(c_all, w, b)


def _adamw_math(w, g, m, v):
    m = ADAM_B1 * m + (1.0 - ADAM_B1) * g
    v = ADAM_B2 * v + (1.0 - ADAM_B2) * (g * g)
    m_hat = m / (1.0 - ADAM_B1 ** ADAM_STEP)
    v_hat = v / (1.0 - ADAM_B2 ** ADAM_STEP)
    delta = -ADAM_LR * (m_hat / (jnp.sqrt(v_hat) + ADAM_EPS) + ADAM_WD * w)
    return delta, m, v


def _adamw(w, g, m, v, name):
    r, c = w.shape
    tc = c if c <= 4096 else _tile(c, 4096)
    tr = _tile(r, max(SUBLANES, (1 << 18) // tc), SUBLANES)

    def body(w_ref, g_ref, m_ref, v_ref, go_ref, d_ref, mo_ref, vo_ref):
        g = g_ref[...]
        go_ref[...] = g
        d_ref[...], mo_ref[...], vo_ref[...] = _adamw_math(w_ref[...], g, m_ref[...], v_ref[...])

    spec = pl.BlockSpec((tr, tc), lambda i, j: (i, j))
    out = jax.ShapeDtypeStruct((r, c), _F32)
    return pl.pallas_call(
        body, name=name, grid=(r // tr, c // tc), in_specs=[spec] * 4, out_specs=[spec] * 4, out_shape=[out] * 4,
        compiler_params=_cp("parallel", "parallel"),
    )(w, g, m, v)


def _adamw_halves(w, g2, m, v, name):
    r, c = w.shape
    tr, tc = _tile(r, 256, SUBLANES), _tile(c // 2, 1024)
    nph = (c // 2) // tc

    def body(w_ref, g_ref, m_ref, v_ref, go_ref, d_ref, mo_ref, vo_ref):
        g = g_ref[...]
        go_ref[...] = g
        d_ref[...], mo_ref[...], vo_ref[...] = _adamw_math(w_ref[...], g, m_ref[...], v_ref[...])

    spec = pl.BlockSpec((tr, tc), lambda i, j: (i, j))
    out = jax.ShapeDtypeStruct((r, c), _F32)
    return pl.pallas_call(
        body, name=name, grid=(r // tr, c // tc),
        in_specs=[spec, pl.BlockSpec((None, tr, tc), lambda i, j: (j // nph, i, j % nph)), spec, spec],
        out_specs=[spec] * 4, out_shape=[out] * 4, compiler_params=_cp("parallel", "parallel"),
    )(w, g2, m, v)


def _wada_update(c_t, dm, w, m, v):
    d, ns = w.shape
    tr, tc = _tile(d, 256, SUBLANES), _tile(ns, 1024)

    def body(c_ref, dm_ref, w_ref, m_ref, v_ref, g_ref, d_ref, mo_ref, vo_ref):
        c = c_ref[...]
        ca = c * _sigmoid(c)
        dmv = dm_ref[...]
        g = ca[:, 0:1] * dmv[0:1, :]
        for b in range(1, 8):
            g = g + ca[:, b:b + 1] * dmv[b:b + 1, :]
        g_ref[...] = g
        d_ref[...], mo_ref[...], vo_ref[...] = _adamw_math(w_ref[...], g, m_ref[...], v_ref[...])

    spec = pl.BlockSpec((tr, tc), lambda i, j: (i, j))
    out = jax.ShapeDtypeStruct((d, ns), _F32)
    return pl.pallas_call(
        body, name="wada_update", grid=(d // tr, ns // tc),
        in_specs=[pl.BlockSpec((tr, 8), lambda i, j: (i, 0)), pl.BlockSpec((8, tc), lambda i, j: (0, j)), spec, spec, spec],
        out_specs=[spec] * 4, out_shape=[out] * 4, compiler_params=_cp("parallel", "parallel"),
    )(c_t, dm, w, m, v)


def _small_reduce(gathered):
    _, r, c = gathered.shape
    tr = _tile(r, 512, SUBLANES)

    def body(q_ref, g_ref):
        g = q_ref[0]
        for k in range(1, 8):
            g = g + q_ref[k]
        g_ref[...] = g

    return pl.pallas_call(
        body, name="small_reduce", grid=(r // tr,), out_shape=jax.ShapeDtypeStruct((r, c), _F32),
        in_specs=[pl.BlockSpec((8, tr, c), lambda i: (0, i, 0))], out_specs=pl.BlockSpec((tr, c), lambda i: (i, 0)),
        compiler_params=_cp("parallel"),
    )(gathered)


def _adamw_many(ws, gs, ms, vs, steps, name):
    n = len(ws)

    def body(*refs):
        w_refs, g_refs, m_refs, v_refs = refs[0:n], refs[n:2 * n], refs[2 * n:3 * n], refs[3 * n:4 * n]
        d_refs, mo_refs, vo_refs = refs[4 * n:5 * n], refs[5 * n:6 * n], refs[6 * n:7 * n]
        for i in range(n):
            d_refs[i][...], mo_refs[i][...], vo_refs[i][...] = _adamw_math(
                w_refs[i][...], g_refs[i][...], m_refs[i][...], v_refs[i][...])

    def spec(a):
        nd = a.ndim
        if steps == 1:
            return pl.BlockSpec(a.shape, lambda i: (0,) * nd)
        return pl.BlockSpec((a.shape[0] // steps,) + a.shape[1:], lambda i: (i,) + (0,) * (nd - 1))

    specs = [spec(w) for w in ws]
    outs = pl.pallas_call(
        body, name=name, grid=(steps,), in_specs=specs * 4, out_specs=specs * 3,
        out_shape=[jax.ShapeDtypeStruct(w.shape, _F32) for w in ws] * 3, compiler_params=_cp("parallel"),
    )(*ws, *gs, *ms, *vs)
    return outs[0:n], outs[n:2 * n], outs[2 * n:3 * n]


def _block_diag(x, eye=None):
    nb, g, p, q = x.shape
    eye = jnp.eye(g, dtype=x.dtype) if eye is None else eye
    return (x[:, :, :, None, :] * eye[None, :, None, :, None]).reshape(nb, g * p, g * q)


def _block_diag_take(x, p, q):
    nb = x.shape[0]
    g = GROUPS_PER_BLOCK
    eye = jnp.eye(g, dtype=x.dtype)
    return jnp.sum(x.reshape(nb, g, p, g, q) * eye[None, :, None, :, None], axis=3)


_VIEWS = {"ssm_b_re": ((0, 2, 1), (0, 2, 1)), "ssm_b_im": ((0, 2, 1), (0, 2, 1)),
          "ssm_w_glu": ((1, 2, 0), (2, 0, 1)), "ssm_b_glu": ((1, 0), (1, 0))}


def _to_view(name, a):
    return a.transpose(_VIEWS[name][0]) if name in _VIEWS else a


def _from_view(name, a):
    return a.transpose(_VIEWS[name][1]) if name in _VIEWS else a


class _Pack:
    def __init__(self, shapes):
        self.shapes = shapes
        self.offsets = {}
        off = 0
        for name, shape in shapes.items():
            n = math.prod(shape)
            self.offsets[name] = (off, n)
            off += -(-n // (SUBLANES * LANES)) * (SUBLANES * LANES)
        self.rows = -(-off // (256 * LANES)) * 256

    def pack(self, arrays):
        parts = []
        off = 0
        for name, shape in self.shapes.items():
            start, n = self.offsets[name]
            if start > off:
                parts.append(jnp.zeros((start - off,), _F32))
            parts.append(arrays[name].reshape(-1).astype(_F32))
            off = start + n
        total = self.rows * LANES
        if total > off:
            parts.append(jnp.zeros((total - off,), _F32))
        return jnp.concatenate(parts).reshape(self.rows, LANES)

    def unpack(self, buf):
        flat = buf.reshape(-1)
        return {name: flat[start:start + n].reshape(self.shapes[name]) for name, (start, n) in self.offsets.items()}


_SMALL = ["b_ada", "g_pre_mix", "g_post_mix", "ssm_log_dt", "ssm_a_re", "ssm_a_im", "ssm_b_re", "ssm_b_im", "ssm_c_re",
          "ssm_c_im", "ssm_d", "ssm_w_glu", "ssm_b_glu", "sgu_ln_g", "sgu_ln_b", "sgu_w", "sgu_b", "g_out_ssm",
          "g_out_sgu", "g_pre_ffn", "g_post_ffn", "conv_b"]
_WEIGHTS = ["w_ada", "b_ada", "g_pre_mix", "g_post_mix", "w_in", "ssm_log_dt", "ssm_a_re", "ssm_a_im", "ssm_b_re",
            "ssm_b_im", "ssm_c_re", "ssm_c_im", "ssm_d", "ssm_w_glu", "ssm_b_glu", "sgu_ln_g", "sgu_ln_b", "sgu_w", "sgu_b",
            "g_out_ssm", "g_out_sgu", "w_out", "g_pre_ffn", "g_post_ffn", "w_up", "conv_w", "conv_b", "w_down"]


def _step(p, m, v, x, c, tgt):
    s, d = x.shape
    mx, my, mc = lax.axis_index("x"), lax.axis_index("y"), lax.axis_index("c")
    chip = 2 * mx + my
    dev = 4 * mx + 2 * my + mc
    sel = jnp.stack([chip, mc]).astype(jnp.int32)
    g_cnt, n_st = p["ssm_a_re"].shape
    nb = g_cnt // GROUPS_PER_BLOCK
    gn = g_cnt * n_st
    d_ssm = g_cnt * SSM_GROUP
    nh = p["sgu_w"].shape[0]
    assert nh * CHUNK == d_ssm and 2 * d_ssm == d and n_st == SSM_STATE

    shards = lambda g: g.reshape(4, g.shape[1] * g.shape[2], g.shape[3])
    buf_in = _cast_into_slot(p["w_in"], sel, sel, "cast_w_in")

    ns_ada = p["w_ada"].shape[1]
    nc_conv = p["conv_w"].shape[1]
    first = jnp.concatenate([jnp.broadcast_to(c, (8, d)), jnp.pad(p["conv_w"], ((0, 5), (0, 0)))], axis=1)
    first_all = _all_gather8(_own_slot(first, dev), "gather_c_conv", after=buf_in)
    (sems_in,), (buf_in,), tok = _gather_start([buf_in], first_all, "gather_start_in")
    c_all = _after(first_all[:, 0, :d], tok)
    conv_w_full = jnp.concatenate([first_all[2 * j, 0:3, d:] for j in range(4)], axis=1)
    b_ada_mine = lax.dynamic_slice_in_dim(p["b_ada"], chip * ns_ada, ns_ada, axis=1)
    mod_mine = _mod_part(c_all, p["w_ada"], b_ada_mine)
    buf_out, buf_up, buf_down = [_cast_into_slot(p[n], sel, tok, "cast_" + n) for n in ("w_out", "w_up", "w_down")]

    eye_t = jnp.eye(GROUPS_PER_BLOCK, dtype=_F32) + tok[0:1, 0:1]
    ldt_l = _after(jnp.repeat(p["ssm_log_dt"], n_st, axis=1), tok)
    are_l, aim_l = p["ssm_a_re"].reshape(1, gn), p["ssm_a_im"].reshape(1, gn)
    bre_t, bim_t = p["ssm_b_re"].reshape(gn, SSM_GROUP).T, p["ssm_b_im"].reshape(gn, SSM_GROUP).T
    nr = _tile(s, TIME_TILE, SUBLANES) // SUBLANES
    kvec = jnp.concatenate([jnp.arange(1, nr + 1, dtype=_F32), jnp.array([nr, 2 * nr, 4 * nr, 0, 0, 0, 0, 0], _F32)])
    pw_re, pw_im, bb_re, bb_im = _ssm_prep(ldt_l, are_l, aim_l, bre_t, bim_t, kvec.reshape(nr + SUBLANES, 1))
    blocks = lambda t: t.reshape(t.shape[0], nb, GROUPS_PER_BLOCK * n_st).transpose(1, 0, 2)
    ptab = jnp.concatenate([blocks(pw_re), blocks(pw_im)], axis=2)
    rev = lambda t: jnp.concatenate([t[:, :nr][:, ::-1], t[:, nr:]], axis=1)
    ptab_rev = jnp.concatenate([rev(blocks(pw_re)), -rev(blocks(pw_im))], axis=2)
    bd = lambda t: t.reshape(SSM_GROUP, nb, GROUPS_PER_BLOCK, n_st).transpose(1, 2, 0, 3)
    bbt = jnp.concatenate([_block_diag(bd(bb_re)), _block_diag(bd(bb_im))], axis=2).astype(_MXU)
    cd = lambda t: t.reshape(nb, GROUPS_PER_BLOCK, SSM_GROUP, n_st).transpose(0, 1, 3, 2)
    ct = jnp.concatenate([_block_diag(cd(p["ssm_c_re"]), eye_t), -_block_diag(cd(p["ssm_c_im"]), eye_t)], axis=1).astype(_MXU)
    wg = _block_diag(p["ssm_w_glu"].reshape(nb, GROUPS_PER_BLOCK, SSM_GROUP, SSM_GROUP), eye_t).astype(_MXU)
    dvec = p["ssm_d"]
    bglu = p["ssm_b_glu"].reshape(1, d_ssm)
    mask = jnp.tril(jnp.ones((CHUNK, CHUNK), _F32)) + tok[0:1, 0:1]
    wm = (p["sgu_w"] * mask[None]).astype(_MXU)
    bs = p["sgu_b"].reshape(nh, CHUNK, 1)

    mod_all = _all_gather8(_own_slot(mod_mine, dev), "gather_mod",
                           after=[buf_out, buf_up, buf_down, bbt, ct, wg, wm, bs, ptab, ptab_rev])
    (sems_out, sems_up), (buf_out, buf_up), tok_rest = _route_start([(buf_out, 1), (buf_up, 1)], mod_all, "route_start_a")
    mod_rows = lax.dynamic_index_in_dim(mod_all, dev, axis=1, keepdims=False)
    mod = jnp.concatenate([mod_rows[0], mod_rows[2], mod_rows[4], mod_rows[6]]).reshape(N_MOD, 1, d)
    sh1, sc1, gt1, sh2, sc2, gt2 = [mod[i] for i in range(N_MOD)]

    h1 = _fwd_pre_mix(x, p["g_pre_mix"], _after(sc1, tok_rest), sh1)
    buf_in = _gather_wait(sems_in, buf_in, h1, "gather_wait_in")
    w_in4 = shards(_pair_forward([buf_in], "pair_forward_in")[0])
    z = _mm_nn(h1, w_in4, _F32, "mm_in")
    y_ssm, y_pre, h_all, hb = _ssm_fwd(z, bbt, ct, dvec, wg, bglu, ptab)
    y_sgu = _sgu_fwd(z, p["sgu_ln_g"], p["sgu_ln_b"], wm, bs)
    buf_out = _route_wait(sems_out, buf_out, 1, y_sgu, "route_wait_out_1")
    buf_up = _route_wait(sems_up, buf_up, 1, y_ssm, "route_wait_up_1")
    (sems_out, sems_up, sems_down), (buf_out, buf_up, buf_down), tok = _route_start(
        [(buf_out, 2), (buf_up, 2), (buf_down, 1)], y_sgu, "route_start_b")
    ycat = _mix_norm_fwd(y_ssm, y_sgu, _after(p["g_out_ssm"], tok), p["g_out_sgu"])
    buf_out = _route_wait(sems_out, buf_out, 2, ycat, "route_wait_out_2")
    w_out_full = _pair_forward([buf_out], "pair_forward_out")[0].reshape(1, d, d)
    o = _mm_nn(ycat, w_out_full, _F32, "mm_out")
    x1, h2 = _fwd_mid(o, x, gt1, p["g_post_mix"], p["g_pre_ffn"], sc2, sh2)
    buf_up = _route_wait(sems_up, buf_up, 2, h2, "route_wait_up_2")
    w_up4 = shards(_pair_forward([buf_up], "pair_forward_up")[0])
    up_pre = _mm_nn(h2, w_up4, _F32, "mm_up")
    buf_down = _route_wait(sems_down, buf_down, 1, up_pre, "route_wait_down_1")
    (sems_down,), (buf_down,), tok = _route_start([(buf_down, 2)], up_pre, "route_start_c")
    act = _conv_act_fwd(up_pre, conv_w_full, _after(p["conv_b"], tok))
    buf_down = _route_wait(sems_down, buf_down, 2, act, "route_wait_down_2")
    w_down_full = _pair_forward([buf_down], "pair_forward_down")[0].reshape(1, -1, d)
    f = _mm_nn(act, w_down_full, _F32, "mm_down", tk=5632)
    dx2, df, d_gt2, d_g_post_ffn, loss = _loss_and_post_ffn_bwd(f, x1, tgt, gt2, p["g_post_ffn"])

    def reduce_next(swap, n, after):
        sems, gw, land, _ = swap
        gw, got = _swap_wait(sems, gw, land, after, "swap_wait_" + n)
        return _scatter_start(_pair_sum(gw, got, sel, "pair_sum_" + n), "scatter_start_" + n)

    d_act = _mm_nt(df, w_down_full, _F32, "mm_d_act", tk=2048)
    swap_down = _swap_start(_mm_tn_rows(act, df, "mm_gw_down"), "swap_start_w_down")
    d_up_pre, d_cw0, d_cw1, d_cw2, d_conv_b = _conv_act_bwd(up_pre, d_act, conv_w_full, _after(p["conv_b"], swap_down[3]))
    red_down = reduce_next(swap_down, "w_down", d_conv_b)
    dh2 = _mm_nt(d_up_pre, w_up4, _F32, "mm_dh2", tk=2816, after=red_down[3])
    swap_up = _swap_start(_mm_tn_cols(h2, d_up_pre, "mm_gw_up"), "swap_start_w_up")
    dx1, d_o, d_sc2, d_sh2, d_g_pre_ffn, d_gt1, d_g_post_mix = _bwd_mid(
        dh2, x1, dx2, o, p["g_pre_ffn"], _after(sc2, swap_up[3]), gt1, p["g_post_mix"])
    red_up = reduce_next(swap_up, "w_up", d_g_post_mix)
    d_ycat = _mm_nt(d_o, w_out_full, _F32, "mm_d_ycat", tn=1024, tk=2048, after=red_up[3])
    swap_out = _swap_start(_mm_tn_rows(ycat, d_o, "mm_gw_out"), "swap_start_w_out")
    dy_ssm, dy_sgu, d_g_out_ssm, d_g_out_sgu = _mix_norm_bwd(
        d_ycat, y_ssm, y_sgu, _after(p["g_out_ssm"], swap_out[3]), p["g_out_sgu"])
    red_out = reduce_next(swap_out, "w_out", d_g_out_sgu)
    dz_ssm, d_bbt, d_ct, d_wg, d_lb, d_ssm_d, d_bglu = _ssm_bwd(z, y_pre, h_all, dy_ssm, hb, bbt, ct,
                                                                _after(dvec, red_out[3]), wg, bglu, ptab_rev)
    dz, d_ln_g, d_ln_b, d_wm, d_bs = _sgu_bwd(z, dy_sgu, dz_ssm, p["sgu_ln_g"], p["sgu_ln_b"], wm, bs)
    dh1 = _mm_nt(dz, w_in4, _F32, "mm_dh1")
    swap_in = _swap_start(_mm_tn_cols(h1, dz, "mm_gw_in"), "swap_start_w_in")
    dx, d_sc1, d_sh1, d_g_pre_mix = _bwd_pre_mix(dh1, x, dx1, p["g_pre_mix"], _after(sc1, swap_in[3]))
    red_in = reduce_next(swap_in, "w_in", d_g_pre_mix)

    nsb = BLOCK_ST
    lanes = lambda t: t.transpose(2, 0, 1, 3).reshape(SSM_GROUP, gn)
    d_bbr = lanes(_block_diag_take(d_bbt[:, :, :nsb], SSM_GROUP, n_st))
    d_bbi = lanes(_block_diag_take(d_bbt[:, :, nsb:], SSM_GROUP, n_st))
    d_lr, d_li = d_lb[:, 0, :nsb].reshape(1, gn), d_lb[:, 0, nsb:].reshape(1, gn)
    d_bre_t, d_bim_t, d_are, d_aim, d_dt = _ssm_prep_bwd(ldt_l, are_l, aim_l, bre_t, bim_t, d_bbr, d_bbi, d_lr, d_li)
    d_log_dt = _group_sum(d_dt.reshape(g_cnt, n_st), p["ssm_log_dt"].reshape(g_cnt, 1))
    c_grad = lambda t: _block_diag_take(t, n_st, SSM_GROUP).transpose(0, 1, 3, 2).reshape(g_cnt, SSM_GROUP, n_st)
    small = {
        "b_ada": jnp.concatenate([d_sh1, _after(d_sc1, red_in[3]), d_gt1, d_sh2, d_sc2, d_gt2], axis=1),
        "g_pre_mix": d_g_pre_mix, "g_post_mix": d_g_post_mix,
        "ssm_log_dt": d_log_dt, "ssm_a_re": d_are, "ssm_a_im": d_aim,
        "ssm_b_re": d_bre_t.T, "ssm_b_im": d_bim_t.T,
        "ssm_c_re": c_grad(d_ct[:, :nsb, :]), "ssm_c_im": -c_grad(d_ct[:, nsb:, :]),
        "ssm_d": d_ssm_d, "ssm_w_glu": _block_diag_take(d_wg, SSM_GROUP, SSM_GROUP), "ssm_b_glu": d_bglu,
        "sgu_ln_g": d_ln_g, "sgu_ln_b": d_ln_b, "sgu_w": d_wm * mask[None], "sgu_b": d_bs,
        "g_out_ssm": d_g_out_ssm, "g_out_sgu": d_g_out_sgu, "g_pre_ffn": d_g_pre_ffn, "g_post_ffn": d_g_post_ffn,
        "conv_b": d_conv_b, "conv_w_all": jnp.concatenate([d_cw0, d_cw1, d_cw2], axis=0),
        "loss_sum": loss,
    }
    small = {n: _to_view(n, a.reshape(p[n].shape)) if n in p else a for n, a in small.items()}
    pk = _Pack({n: a.shape for n, a in small.items()})
    sems_small, small_buf, tok = _gather8_start(_own_slot(pk.pack(small), dev), "gather_small_start")

    big = ["w_down", "w_up", "w_out", "w_in"]
    joins = []
    after = tok
    for n, (sems, pair, land, _) in zip(big, (red_down, red_up, red_out, red_in)):
        pair, land = _scatter_wait(sems, pair, land, after, "scatter_wait_" + n)
        sems_j, half, after = _join_start(_chip_sum(pair, land, sel, "chip_sum_" + n), "join_start_" + n)
        joins.append((sems_j, half))
    big_out = {}
    for n, (sems_j, half) in zip(big, joins):
        j = _join_wait(sems_j, half, after, "join_wait_" + n)
        if n in ("w_in", "w_up"):
            big_out[n] = tuple(_adamw(p[n], j.reshape(p[n].shape), m[n], v[n], "adamw_" + n))
        else:
            big_out[n] = tuple(_adamw_halves(p[n], j, m[n], v[n], "adamw_" + n))
        after = big_out[n][1]

    gathered = _gather8_forward(_gather8_wait(sems_small, small_buf, after, "gather_small_wait"),
                                "gather_small_forward")
    gview = pk.unpack(_small_reduce(gathered))
    gview["conv_w"] = lax.dynamic_slice_in_dim(gview.pop("conv_w_all"), chip * nc_conv, nc_conv, axis=1)
    loss = gview.pop("loss_sum")
    small_names = _SMALL + ["conv_w"]
    per_group = [n for n in small_names if gview[n].ndim >= 2 and gview[n].shape[0] == g_cnt]
    others = [n for n in small_names if n not in per_group]
    grads = {n: _from_view(n, gview[n]) for n in small_names}
    deltas, new_m, new_v = {}, {}, {}
    for names, steps, call in ((per_group, g_cnt // GROUPS_PER_BLOCK, "adamw_s5"), (others, 1, "adamw_small")):
        res = _adamw_many([_to_view(n, p[n]) for n in names], [gview[n] for n in names],
                          [_to_view(n, m[n]) for n in names], [_to_view(n, v[n]) for n in names], steps, call)
        for n, dl, mo, vo in zip(names, *res):
            deltas[n], new_m[n], new_v[n] = _from_view(n, dl), _from_view(n, mo), _from_view(n, vo)

    d_mod_all = gathered.reshape(8, -1)[:, :N_MOD * d]
    d_mod_mine = lax.dynamic_slice_in_dim(d_mod_all, chip * ns_ada, ns_ada, axis=1)
    grads["w_ada"], deltas["w_ada"], new_m["w_ada"], new_v["w_ada"] = _wada_update(
        c_all.T, d_mod_mine, p["w_ada"], m["w_ada"], v["w_ada"])
    for n in big:
        grads[n], deltas[n], new_m[n], new_v[n] = big_out[n]
    return loss[0, 0], dx, grads, deltas, new_m, new_v


def kernel(x, c, w_ada, b_ada, g_pre_mix, g_post_mix, w_in, ssm_log_dt, ssm_a_re, ssm_a_im, ssm_b_re, ssm_b_im, ssm_c_re, ssm_c_im, ssm_d, ssm_w_glu, ssm_b_glu, sgu_ln_g, sgu_ln_b, sgu_w, sgu_b, g_out_ssm, g_out_sgu, w_out, g_pre_ffn, g_post_ffn, w_up, conv_w, conv_b, w_down, loss_target, m_w_ada, m_b_ada, m_g_pre_mix, m_g_post_mix, m_w_in, m_ssm_log_dt, m_ssm_a_re, m_ssm_a_im, m_ssm_b_re, m_ssm_b_im, m_ssm_c_re, m_ssm_c_im, m_ssm_d, m_ssm_w_glu, m_ssm_b_glu, m_sgu_ln_g, m_sgu_ln_b, m_sgu_w, m_sgu_b, m_g_out_ssm, m_g_out_sgu, m_w_out, m_g_pre_ffn, m_g_post_ffn, m_w_up, m_conv_w, m_conv_b, m_w_down, v_w_ada, v_b_ada, v_g_pre_mix, v_g_post_mix, v_w_in, v_ssm_log_dt, v_ssm_a_re, v_ssm_a_im, v_ssm_b_re, v_ssm_b_im, v_ssm_c_re, v_ssm_c_im, v_ssm_d, v_ssm_w_glu, v_ssm_b_glu, v_sgu_ln_g, v_sgu_ln_b, v_sgu_w, v_sgu_b, v_g_out_ssm, v_g_out_sgu, v_w_out, v_g_pre_ffn, v_g_post_ffn, v_w_up, v_conv_w, v_conv_b, v_w_down):
    given = dict(locals())
    drop = lambda a: a if a.ndim == 2 else a[0]
    p = {n: drop(given[n]) for n in _WEIGHTS}
    m = {n: drop(given["m_" + n]) for n in _WEIGHTS}
    v = {n: drop(given["v_" + n]) for n in _WEIGHTS}
    loss, dx, grads, deltas, new_m, new_v = _step(p, m, v, x[0], c, loss_target[0])
    outs = [loss, dx[None]]
    for group in (grads, deltas, new_m, new_v):
        outs += [group[n].reshape(given[n].shape) for n in _WEIGHTS]
    return tuple(outs)
```

```python
import functools
import math

import jax
import jax.numpy as jnp
from jax import lax
from jax.experimental import pallas as pl
from jax.experimental.pallas import tpu as pltpu

_F32 = jnp.float32
_MXU = jnp.bfloat16
_WIRE = jnp.bfloat16

EPS = 1e-6
SSM_GROUP = 16
SSM_STATE = 64
GROUPS_PER_BLOCK = 8
BLOCK_CH = SSM_GROUP * GROUPS_PER_BLOCK
BLOCK_ST = SSM_STATE * GROUPS_PER_BLOCK
CHUNK = 128
TIME_TILE = 512
SUBLANES = 8
LANES = 128
N_MOD = 6
ADAM_LR, ADAM_B1, ADAM_B2, ADAM_EPS, ADAM_WD, ADAM_STEP = 0.001, 0.9, 0.999, 1e-08, 0.01, 10
_VMEM_LIMIT = 56 * 1024 * 1024
_MESH = pl.DeviceIdType.MESH
_ANY = pl.BlockSpec(memory_space=pl.ANY)
_HBM = pl.BlockSpec(memory_space=pltpu.HBM)
_SEM = pl.BlockSpec(memory_space=pltpu.SEMAPHORE)
_VMEM_WHOLE = pl.BlockSpec(memory_space=pltpu.VMEM)
_EFFECT = pltpu.SideEffectType.DATAFLOW_SIDE_EFFECTING
_GELU_C = math.sqrt(2.0 / math.pi)


def _cp(*sem):
    return pltpu.CompilerParams(dimension_semantics=sem, vmem_limit_bytes=_VMEM_LIMIT)


def _tile(dim, target, align=LANES):
    if dim <= target:
        return dim
    best = None
    for t in range(align, target + 1, align):
        if dim % t == 0:
            best = t
    assert best is not None, (dim, target, align)
    return best


def _gelu(x):
    return 0.5 * x * (1.0 + jnp.tanh(_GELU_C * (x + 0.044715 * (x * x * x))))


def _gelu_grad(x):
    t = jnp.tanh(_GELU_C * (x + 0.044715 * (x * x * x)))
    return 0.5 * (1.0 + t) + 0.5 * x * (1.0 - t * t) * (_GELU_C * (1.0 + 3.0 * 0.044715 * x * x))


def _sigmoid(x):
    return 1.0 / (1.0 + jnp.exp(-x))


def _colsum(x):
    return jnp.sum(x, axis=0, keepdims=True)


def _rowmean(x):
    return jnp.mean(x, axis=-1, keepdims=True)


def _zero_first(first, *refs):
    @pl.when(first)
    def _():
        for ref in refs:
            ref[...] = jnp.zeros_like(ref)


def _acc(ref, first, val):
    del first
    ref[...] += val


def _place():
    mx, my, mc = lax.axis_index("x"), lax.axis_index("y"), lax.axis_index("c")
    chips = [(1 - mx, my), (mx, 1 - my), (1 - mx, 1 - my)]
    return mx, my, mc, chips


def _all_gather8(buf, name, after=None):
    extra = [] if after is None else (list(after) if isinstance(after, (list, tuple)) else [after])

    def body(in_ref, *rest):
        out_ref, send_sems, recv_sems = rest[len(extra):]
        mx, my, mc, chips = _place()
        me, sibling = (mx, my, mc), (mx, my, 1 - mc)

        def slot(ref, px, py, pc):
            return ref.at[4 * px + 2 * py + pc]

        def copy(k, block, to, src_ref=out_ref):
            return pltpu.make_async_remote_copy(
                src_ref=slot(src_ref, *block), dst_ref=slot(out_ref, *block),
                send_sem=send_sems.at[k], recv_sem=recv_sems.at[k], device_id=to, device_id_type=_MESH)

        first = [copy(0, me, sibling, in_ref)]
        first += [copy(1 + j, me, (*chip, mc), in_ref) for j, chip in enumerate(chips)]
        for cp in first:
            cp.start()
        passed = [copy(4 + j, (*chip, mc), sibling) for j, chip in enumerate(chips)]
        for j, chip in enumerate(chips):
            copy(1 + j, (*chip, mc), me).wait_recv()
            passed[j].start()
        copy(0, sibling, me).wait_recv()
        for j, chip in enumerate(chips):
            copy(4 + j, (*chip, 1 - mc), me).wait_recv()
        for cp in first + passed:
            cp.wait_send()

    return pl.pallas_call(
        body, name=name, out_shape=jax.ShapeDtypeStruct(buf.shape, buf.dtype),
        in_specs=[_ANY] * (1 + len(extra)), out_specs=_ANY, input_output_aliases={0: 0},
        scratch_shapes=[pltpu.SemaphoreType.DMA((7,)), pltpu.SemaphoreType.DMA((7,))],
    )(buf, *extra)


def _own_slot(x, dev):
    return lax.dynamic_update_slice(jnp.zeros((8,) + x.shape, x.dtype), x[None], (dev, 0, 0))


def _cast_into_slot(w, sel, after, name):
    r, c = w.shape
    hr = r // 2
    tr = _tile(hr, 256, 16)
    nr = hr // tr

    def body(sel_ref, w_ref, after_ref, o_ref):
        o_ref[...] = w_ref[...].astype(o_ref.dtype)

    return pl.pallas_call(
        body, name=name, out_shape=jax.ShapeDtypeStruct((4, 2, hr, c), _WIRE),
        grid_spec=pltpu.PrefetchScalarGridSpec(
            num_scalar_prefetch=1, grid=(2, nr),
            in_specs=[pl.BlockSpec((tr, c), lambda h, i, s: (h * nr + i, 0)), _ANY],
            out_specs=pl.BlockSpec((None, None, tr, c), lambda h, i, s: (s[0], h, i, 0))),
        compiler_params=_cp("parallel", "parallel"),
    )(sel, w, after)


def _hbm(a):
    return pltpu.with_memory_space_constraint(a, pltpu.HBM)


def _after(vec, token):
    return vec + token[0:1, 0:1]


def _gather_start(bufs, after, name):
    n = len(bufs)
    nc = 3 * n

    def body(*refs):
        ins, send, recv, token = refs[:n], refs[n + 1:n + 1 + nc], refs[n + 1 + nc:n + 1 + 2 * nc], refs[-1]
        mx, my, mc, chips = _place()
        j_me = 2 * mx + my
        for i in range(n):
            for k, chip in enumerate(chips):
                half = ins[i].at[j_me, mc]
                pltpu.make_async_remote_copy(
                    src_ref=half, dst_ref=half, send_sem=send[3 * i + k], recv_sem=recv[3 * i + k],
                    device_id=(*chip, mc), device_id_type=_MESH).start()
        token[...] = jnp.zeros_like(token)

    outs = pl.pallas_call(
        body, name=name,
        out_shape=tuple([pltpu.SemaphoreType.DMA(())] * (2 * nc) + [pltpu.HBM(b.shape, b.dtype) for b in bufs]
                        + [jax.ShapeDtypeStruct((SUBLANES, LANES), _F32)]),
        in_specs=tuple([_HBM] * n + [_ANY]), out_specs=tuple([_SEM] * (2 * nc) + [_HBM] * n + [_VMEM_WHOLE]),
        input_output_aliases={i: 2 * nc + i for i in range(n)},
        compiler_params=pltpu.CompilerParams(has_side_effects=_EFFECT),
    )(*[_hbm(b) for b in bufs], after)
    sems = [(outs[3 * i:3 * i + 3], outs[nc + 3 * i:nc + 3 * i + 3]) for i in range(n)]
    return sems, list(outs[2 * nc:2 * nc + n]), outs[-1]


def _gather_wait(sems, buf, after, name):
    send, recv = sems

    after = list(after) if isinstance(after, (list, tuple)) else [after]

    def body(buf_ref, s0, s1, s2, r0, r1, r2, *rest):
        mx, my, mc, chips = _place()
        j_me = 2 * mx + my
        for k, (chip, s_k, r_k) in enumerate(zip(chips, (s0, s1, s2), (r0, r1, r2))):
            cp = pltpu.make_async_remote_copy(
                src_ref=buf_ref.at[j_me, mc], dst_ref=buf_ref.at[2 * chip[0] + chip[1], mc], send_sem=s_k, recv_sem=r_k,
                device_id=(*chip, mc), device_id_type=_MESH)
            cp.wait_send()
            cp.wait_recv()

    return pl.pallas_call(
        body, name=name, out_shape=pltpu.HBM(buf.shape, buf.dtype),
        in_specs=(_HBM,) + (_SEM,) * 6 + (_ANY,) * len(after), out_specs=_HBM, input_output_aliases={0: 0},
        compiler_params=pltpu.CompilerParams(has_side_effects=_EFFECT),
    )(buf, *send, *recv, *after)


def _route_ends(buf_ref, phase):
    mx, my, mc, _ = _place()
    hq = buf_ref.shape[2] // 2
    xn, yn = (1 - mx, my), (mx, 1 - my)
    j_me, j_x, j_y, j_d = 2 * mx + my, 2 * (1 - mx) + my, 2 * mx + (1 - my), 2 * (1 - mx) + (1 - my)
    if phase == 1:
        mine = buf_ref.at[j_me, mc]
        return [((*xn, mc), mine, buf_ref.at[j_x, mc]), ((*yn, mc), mine, buf_ref.at[j_y, mc])]
    lo, hi = pl.ds(0, hq), pl.ds(hq, hq)
    return [((*xn, mc), buf_ref.at[j_y, mc, lo], buf_ref.at[j_d, mc, lo]),
            ((*yn, mc), buf_ref.at[j_x, mc, hi], buf_ref.at[j_d, mc, hi])]


def _route_start(items, after, name):
    n = len(items)

    def body(*refs):
        ins, send, recv, token = refs[:n], refs[n + 1:3 * n + 1], refs[3 * n + 1:5 * n + 1], refs[-1]
        for i, (_, phase) in enumerate(items):
            for k, (peer, src, _) in enumerate(_route_ends(ins[i], phase)):
                pltpu.make_async_remote_copy(src_ref=src, dst_ref=src, send_sem=send[2 * i + k], recv_sem=recv[2 * i + k],
                                             device_id=peer, device_id_type=_MESH).start()
        token[...] = jnp.zeros_like(token)

    bufs = [b for b, _ in items]
    outs = pl.pallas_call(
        body, name=name,
        out_shape=tuple([pltpu.SemaphoreType.DMA(())] * (4 * n) + [pltpu.HBM(b.shape, b.dtype) for b in bufs]
                        + [jax.ShapeDtypeStruct((SUBLANES, LANES), _F32)]),
        in_specs=tuple([_HBM] * n + [_ANY]), out_specs=tuple([_SEM] * (4 * n) + [_HBM] * n + [_VMEM_WHOLE]),
        input_output_aliases={i: 4 * n + i for i in range(n)},
        compiler_params=pltpu.CompilerParams(has_side_effects=_EFFECT),
    )(*[_hbm(b) for b in bufs], after)
    sems = [(outs[2 * i:2 * i + 2], outs[2 * n + 2 * i:2 * n + 2 * i + 2]) for i in range(n)]
    return sems, list(outs[4 * n:5 * n]), outs[-1]


def _route_wait(sems, buf, phase, after, name):
    send, recv = sems

    def body(buf_ref, s0, s1, r0, r1, after_ref, out_ref):
        for (peer, src, land), s_k, r_k in zip(_route_ends(buf_ref, phase), (s0, s1), (r0, r1)):
            cp = pltpu.make_async_remote_copy(src_ref=src, dst_ref=land, send_sem=s_k, recv_sem=r_k,
                                              device_id=peer, device_id_type=_MESH)
            cp.wait_send()
            cp.wait_recv()

    return pl.pallas_call(
        body, name=name, out_shape=pltpu.HBM(buf.shape, buf.dtype),
        in_specs=(_HBM,) + (_SEM,) * 4 + (_ANY,), out_specs=_HBM, input_output_aliases={0: 0},
        compiler_params=pltpu.CompilerParams(has_side_effects=_EFFECT),
    )(buf, *send, *recv, after)


def _pair_forward(bufs, name):
    n = len(bufs)

    def body(*refs):
        ins, outs = refs[:n], refs[n:2 * n]
        send_sems, recv_sems = refs[2 * n:]
        mx, my, mc, chips = _place()
        sibling = (mx, my, 1 - mc)
        cps = []
        for i in range(n):
            for k, chip in enumerate(chips):
                j_k = 2 * chip[0] + chip[1]
                cp = pltpu.make_async_remote_copy(
                    src_ref=ins[i].at[j_k, mc], dst_ref=outs[i].at[j_k, mc], send_sem=send_sems.at[3 * i + k],
                    recv_sem=recv_sems.at[3 * i + k], device_id=sibling, device_id_type=_MESH)
                cp.start()
                cps.append(cp)
        for i in range(n):
            for k, chip in enumerate(chips):
                other = outs[i].at[2 * chip[0] + chip[1], 1 - mc]
                pltpu.make_async_remote_copy(
                    src_ref=other, dst_ref=other, send_sem=send_sems.at[3 * i + k], recv_sem=recv_sems.at[3 * i + k],
                    device_id=sibling, device_id_type=_MESH).wait_recv()
        for cp in cps:
            cp.wait_send()

    return pl.pallas_call(
        body, name=name, out_shape=[jax.ShapeDtypeStruct(b.shape, b.dtype) for b in bufs],
        in_specs=[_ANY] * n, out_specs=[_ANY] * n, input_output_aliases={i: i for i in range(n)},
        scratch_shapes=[pltpu.SemaphoreType.DMA((3 * n,)), pltpu.SemaphoreType.DMA((3 * n,))],
    )(*bufs)


def _gather8_peers(buf_ref, mx, my, mc, chips):
    mine = buf_ref.at[4 * mx + 2 * my + mc]
    peers = [((mx, my, 1 - mc), mine, buf_ref.at[4 * mx + 2 * my + 1 - mc])]
    peers += [((*chip, mc), mine, buf_ref.at[4 * chip[0] + 2 * chip[1] + mc]) for chip in chips]
    return peers


def _gather8_start(buf, name):
    def body(buf_ref, *rest):
        send, recv, token = rest[0:4], rest[4:8], rest[-1]
        mx, my, mc, chips = _place()
        for k, (peer, src, _) in enumerate(_gather8_peers(buf_ref, mx, my, mc, chips)):
            pltpu.make_async_remote_copy(src_ref=src, dst_ref=src, send_sem=send[k], recv_sem=recv[k],
                                         device_id=peer, device_id_type=_MESH).start()
        token[...] = jnp.zeros_like(token)

    outs = pl.pallas_call(
        body, name=name,
        out_shape=tuple([pltpu.SemaphoreType.DMA(())] * 8 + [pltpu.HBM(buf.shape, buf.dtype),
                                                             jax.ShapeDtypeStruct((SUBLANES, LANES), _F32)]),
        in_specs=(_HBM,), out_specs=tuple([_SEM] * 8 + [_HBM, _VMEM_WHOLE]), input_output_aliases={0: 8},
        compiler_params=pltpu.CompilerParams(has_side_effects=_EFFECT),
    )(_hbm(buf))
    return (outs[0:4], outs[4:8]), outs[8], outs[9]


def _gather8_wait(sems, buf, after, name):
    send, recv = sems

    def body(buf_ref, s0, s1, s2, s3, r0, r1, r2, r3, after_ref, out_ref):
        mx, my, mc, chips = _place()
        for (peer, src, dst), s_k, r_k in zip(_gather8_peers(buf_ref, mx, my, mc, chips), (s0, s1, s2, s3), (r0, r1, r2, r3)):
            cp = pltpu.make_async_remote_copy(src_ref=src, dst_ref=dst, send_sem=s_k, recv_sem=r_k,
                                              device_id=peer, device_id_type=_MESH)
            cp.wait_send()
            cp.wait_recv()

    return pl.pallas_call(
        body, name=name, out_shape=pltpu.HBM(buf.shape, buf.dtype),
        in_specs=(_HBM,) + (_SEM,) * 8 + (_ANY,), out_specs=_HBM, input_output_aliases={0: 0},
        compiler_params=pltpu.CompilerParams(has_side_effects=_EFFECT),
    )(buf, *send, *recv, after)


def _gather8_forward(buf, name):
    def body(in_ref, out_ref, send_sems, recv_sems):
        mx, my, mc, chips = _place()
        sibling = (mx, my, 1 - mc)
        cps = []
        for k, chip in enumerate(chips):
            idx = 4 * chip[0] + 2 * chip[1] + mc
            cp = pltpu.make_async_remote_copy(src_ref=in_ref.at[idx], dst_ref=out_ref.at[idx], send_sem=send_sems.at[k],
                                              recv_sem=recv_sems.at[k], device_id=sibling, device_id_type=_MESH)
            cp.start()
            cps.append(cp)
        for k, chip in enumerate(chips):
            other = out_ref.at[4 * chip[0] + 2 * chip[1] + 1 - mc]
            pltpu.make_async_remote_copy(src_ref=other, dst_ref=other, send_sem=send_sems.at[k], recv_sem=recv_sems.at[k],
                                         device_id=sibling, device_id_type=_MESH).wait_recv()
        for cp in cps:
            cp.wait_send()

    return pl.pallas_call(
        body, name=name, out_shape=jax.ShapeDtypeStruct(buf.shape, buf.dtype),
        in_specs=[_ANY], out_specs=_ANY, input_output_aliases={0: 0},
        scratch_shapes=[pltpu.SemaphoreType.DMA((3,)), pltpu.SemaphoreType.DMA((3,))],
    )(buf)


def _scatter_start(pair, name):
    land = lax.empty((3,) + pair.shape[1:], pair.dtype)

    def body(pair_ref, land_ref, s0, s1, s2, r0, r1, r2, pair_thru, land_thru, token):
        mx, my, mc, chips = _place()
        for k, (chip, s_k, r_k) in enumerate(zip(chips, (s0, s1, s2), (r0, r1, r2))):
            pltpu.make_async_remote_copy(
                src_ref=pair_ref.at[2 * chip[0] + chip[1]], dst_ref=land_ref.at[k], send_sem=s_k, recv_sem=r_k,
                device_id=(*chip, mc), device_id_type=_MESH).start()
        token[...] = jnp.zeros_like(token)

    outs = pl.pallas_call(
        body, name=name,
        out_shape=tuple([pltpu.SemaphoreType.DMA(())] * 6 + [pltpu.HBM(pair.shape, pair.dtype), pltpu.HBM(land.shape, land.dtype),
                                                             jax.ShapeDtypeStruct((SUBLANES, LANES), _F32)]),
        in_specs=(_HBM, _HBM), out_specs=tuple([_SEM] * 6 + [_HBM, _HBM, _VMEM_WHOLE]),
        input_output_aliases={0: 6, 1: 7}, compiler_params=pltpu.CompilerParams(has_side_effects=_EFFECT),
    )(_hbm(pair), _hbm(land))
    return (outs[0:3], outs[3:6]), outs[6], outs[7], outs[8]


def _scatter_wait(sems, pair, land, after, name):
    send, recv = sems

    def body(pair_ref, land_ref, s0, s1, s2, r0, r1, r2, after_ref, pair_out, land_out):
        mx, my, mc, chips = _place()
        for k, (chip, s_k, r_k) in enumerate(zip(chips, (s0, s1, s2), (r0, r1, r2))):
            cp = pltpu.make_async_remote_copy(
                src_ref=pair_ref.at[2 * chip[0] + chip[1]], dst_ref=land_ref.at[k], send_sem=s_k, recv_sem=r_k,
                device_id=(*chip, mc), device_id_type=_MESH)
            cp.wait_send()
            cp.wait_recv()

    return pl.pallas_call(
        body, name=name, out_shape=(pltpu.HBM(pair.shape, pair.dtype), pltpu.HBM(land.shape, land.dtype)),
        in_specs=(_HBM, _HBM) + (_SEM,) * 6 + (_ANY,), out_specs=(_HBM, _HBM), input_output_aliases={0: 0, 1: 1},
        compiler_params=pltpu.CompilerParams(has_side_effects=_EFFECT),
    )(pair, land, *send, *recv, after)


def _sibling_copy(src_ref, dst_ref, send_sem, recv_sem):
    mx, my, mc, _ = _place()
    return pltpu.make_async_remote_copy(src_ref=src_ref, dst_ref=dst_ref, send_sem=send_sem, recv_sem=recv_sem,
                                        device_id=(mx, my, 1 - mc), device_id_type=_MESH)


def _sibling_handshake():
    mx, my, mc, _ = _place()
    barrier = pltpu.get_barrier_semaphore()
    pl.semaphore_signal(barrier, inc=1, device_id=(mx, my, 1 - mc), device_id_type=_MESH)
    pl.semaphore_wait(barrier, 1)


def _swap_start(g, name, cid):
    land = lax.empty(g.shape[1:], g.dtype)

    def body(g_ref, land_ref, send_sem, recv_sem, g_thru, land_thru, token):
        _sibling_handshake()
        _sibling_copy(g_ref.at[1 - lax.axis_index("c")], land_ref, send_sem, recv_sem).start()
        token[...] = jnp.zeros_like(token)

    outs = pl.pallas_call(
        body, name=name,
        out_shape=(pltpu.SemaphoreType.DMA(()), pltpu.SemaphoreType.DMA(()), pltpu.HBM(g.shape, g.dtype),
                   pltpu.HBM(land.shape, land.dtype), jax.ShapeDtypeStruct((SUBLANES, LANES), _F32)),
        in_specs=(_HBM, _HBM), out_specs=(_SEM, _SEM, _HBM, _HBM, _VMEM_WHOLE), input_output_aliases={0: 2, 1: 3},
        compiler_params=pltpu.CompilerParams(has_side_effects=_EFFECT, collective_id=cid),
    )(_hbm(g), _hbm(land))
    return (outs[0], outs[1]), outs[2], outs[3], outs[4]


def _swap_wait(sems, g, land, after, name):
    def body(g_ref, land_ref, send_sem, recv_sem, after_ref, g_out, land_out):
        cp = _sibling_copy(g_ref.at[1 - lax.axis_index("c")], land_ref, send_sem, recv_sem)
        cp.wait_send()
        cp.wait_recv()

    return pl.pallas_call(
        body, name=name, out_shape=(pltpu.HBM(g.shape, g.dtype), pltpu.HBM(land.shape, land.dtype)),
        in_specs=(_HBM, _HBM, _SEM, _SEM, _ANY), out_specs=(_HBM, _HBM), input_output_aliases={0: 0, 1: 1},
        compiler_params=pltpu.CompilerParams(has_side_effects=_EFFECT),
    )(g, land, *sems, after)


def _join_start(buf, name, cid):
    def body(buf_ref, send_sem, recv_sem, buf_thru, token):
        _sibling_handshake()
        mine = buf_ref.at[lax.axis_index("c")]
        _sibling_copy(mine, mine, send_sem, recv_sem).start()
        token[...] = jnp.zeros_like(token)

    outs = pl.pallas_call(
        body, name=name,
        out_shape=(pltpu.SemaphoreType.DMA(()), pltpu.SemaphoreType.DMA(()), pltpu.HBM(buf.shape, buf.dtype),
                   jax.ShapeDtypeStruct((SUBLANES, LANES), _F32)),
        in_specs=(_HBM,), out_specs=(_SEM, _SEM, _HBM, _VMEM_WHOLE), input_output_aliases={0: 2},
        compiler_params=pltpu.CompilerParams(has_side_effects=_EFFECT, collective_id=cid),
    )(_hbm(buf))
    return (outs[0], outs[1]), outs[2], outs[3]


def _join_wait(sems, buf, after, name):
    def body(buf_ref, send_sem, recv_sem, after_ref, buf_out):
        mc = lax.axis_index("c")
        cp = _sibling_copy(buf_ref.at[mc], buf_ref.at[1 - mc], send_sem, recv_sem)
        cp.wait_send()
        cp.wait_recv()

    return pl.pallas_call(
        body, name=name, out_shape=pltpu.HBM(buf.shape, buf.dtype),
        in_specs=(_HBM, _SEM, _SEM, _ANY), out_specs=_HBM, input_output_aliases={0: 0},
        compiler_params=pltpu.CompilerParams(has_side_effects=_EFFECT),
    )(buf, *sems, after)


def _pair_sum(g, got, sel, name):
    _, four, hr, c = g.shape
    tr = _tile(hr, 512, 16)

    def body(sel_ref, g_ref, p_ref, o_ref):
        o_ref[...] = (g_ref[...].astype(_F32) + p_ref[...].astype(_F32)).astype(o_ref.dtype)

    return pl.pallas_call(
        body, name=name, out_shape=jax.ShapeDtypeStruct((four, hr, c), g.dtype),
        grid_spec=pltpu.PrefetchScalarGridSpec(
            num_scalar_prefetch=1, grid=(four, hr // tr),
            in_specs=[pl.BlockSpec((None, None, tr, c), lambda j, i, s: (s[1], j, i, 0)),
                      pl.BlockSpec((None, tr, c), lambda j, i, s: (j, i, 0))],
            out_specs=pl.BlockSpec((None, tr, c), lambda j, i, s: (j, i, 0))),
        compiler_params=_cp("parallel", "parallel"),
    )(sel, g, got)


def _chip_sum(pair, got, sel, name):
    _, hr, c = pair.shape
    tr = _tile(hr, 512, 16)

    def body(sel_ref, p_ref, q_ref, o_ref):
        o_ref[...] = ((p_ref[...].astype(_F32) + q_ref[0].astype(_F32)) + q_ref[1].astype(_F32)) + q_ref[2].astype(_F32)

    return pl.pallas_call(
        body, name=name, out_shape=jax.ShapeDtypeStruct((2, hr, c), _F32),
        grid_spec=pltpu.PrefetchScalarGridSpec(
            num_scalar_prefetch=1, grid=(hr // tr,),
            in_specs=[pl.BlockSpec((None, tr, c), lambda i, s: (s[0], i, 0)),
                      pl.BlockSpec((3, tr, c), lambda i, s: (0, i, 0))],
            out_specs=pl.BlockSpec((None, tr, c), lambda i, s: (s[1], i, 0))),
        compiler_params=_cp("parallel"),
    )(sel, pair, got)


def _matmul(a, b, dims, out_struct, grid, a_spec, b_spec, o_spec, acc_shape, k_axis, name, after=None):
    nk = grid[k_axis]
    extra = [] if after is None else [after]

    def body(a_ref, b_ref, *rest):
        o_ref, acc = rest[len(extra)], rest[len(extra) + 1:]
        prod = lax.dot_general(a_ref[...].astype(_MXU), b_ref[...].astype(_MXU), dims, preferred_element_type=_F32)
        if nk == 1:
            o_ref[...] = prod.astype(o_ref.dtype)
        else:
            acc_ref, = acc
            k = pl.program_id(k_axis)
            _zero_first(k == 0, acc_ref)
            acc_ref[...] += prod

            @pl.when(k == nk - 1)
            def _():
                o_ref[...] = acc_ref[...].astype(o_ref.dtype)

    sem = ["parallel"] * len(grid)
    sem[k_axis] = "arbitrary"
    return pl.pallas_call(
        body, name=name, out_shape=out_struct, grid=grid, in_specs=[a_spec, b_spec] + [_ANY] * len(extra), out_specs=o_spec,
        scratch_shapes=[pltpu.VMEM(acc_shape, _F32)] if nk > 1 else [], compiler_params=_cp(*sem),
    )(a, b, *extra)


def _mm_nn(a, w4, out_dtype, name, tm=512, tn=1536, tk=2048, after=None):
    m, k = a.shape
    j, _, ns = w4.shape
    tm, tn, tk = _tile(m, tm, 16), _tile(ns, tn), _tile(k, tk)
    nps = ns // tn
    return _matmul(
        a, w4, (((1,), (0,)), ((), ())), jax.ShapeDtypeStruct((m, j * ns), out_dtype),
        (j * nps, m // tm, k // tk),
        pl.BlockSpec((tm, tk), lambda ni, mi, ki: (mi, ki)),
        pl.BlockSpec((None, tk, tn), lambda ni, mi, ki: (ni // nps, ki, ni % nps)),
        pl.BlockSpec((tm, tn), lambda ni, mi, ki: (mi, ni)), (tm, tn), 2, name, after)


def _mm_nt(a, w4, out_dtype, name, tm=512, tn=2048, tk=1536, after=None):
    m = a.shape[-2]
    j, kw, ns = w4.shape
    tm, tn, tk = _tile(m, tm, 16), _tile(kw, tn), _tile(ns, tk)
    kps = ns // tk
    if a.ndim == 3:
        kph = a.shape[2] // tk
        a_spec = pl.BlockSpec((None, tm, tk), lambda ni, mi, ki: (ki // kph, mi, ki % kph))
    else:
        a_spec = pl.BlockSpec((tm, tk), lambda ni, mi, ki: (mi, ki))
    return _matmul(
        a, w4, (((1,), (1,)), ((), ())), jax.ShapeDtypeStruct((m, kw), out_dtype),
        (kw // tn, m // tm, j * kps),
        a_spec,
        pl.BlockSpec((None, tn, tk), lambda ni, mi, ki: (ki // kps, ni, ki % kps)),
        pl.BlockSpec((tm, tn), lambda ni, mi, ki: (mi, ni)), (tm, tn), 2, name, after)


def _mm_tn_cols(a, b, name, tm=1024, tn=1536, tk=2048):
    m, ka = a.shape
    ns = (b.shape[-1] * (2 if b.ndim == 3 else 1)) // 4
    hr = ka // 2
    tm, tn, tk = _tile(hr, tm), _tile(ns, tn), _tile(m, tk, 16)
    mph, nps = hr // tm, ns // tn
    if b.ndim == 3:
        b_spec = pl.BlockSpec((None, tk, tn), lambda ni, mi, ki: (ni // (2 * nps), ki, ni % (2 * nps)))
    else:
        b_spec = pl.BlockSpec((tk, tn), lambda ni, mi, ki: (ki, ni))
    return _matmul(
        a, b, (((0,), (0,)), ((), ())), jax.ShapeDtypeStruct((2, 4, hr, ns), _WIRE),
        (4 * nps, 2 * mph, m // tk),
        pl.BlockSpec((tk, tm), lambda ni, mi, ki: (ki, mi)),
        b_spec,
        pl.BlockSpec((None, None, tm, tn), lambda ni, mi, ki: (mi // mph, ni // nps, mi % mph, ni % nps)),
        (tm, tn), 2, name)


def _mm_tn_rows(a, b, name, tm=1536, tn=1024, tk=2048):
    m, ka = a.shape
    r = ka // 4
    hc = b.shape[1] // 2
    tm, tn, tk = _tile(r, tm), _tile(hc, tn), _tile(m, tk, 16)
    mpr, nph = r // tm, hc // tn
    return _matmul(
        a, b, (((0,), (0,)), ((), ())), jax.ShapeDtypeStruct((2, 4, r, hc), _WIRE),
        (2 * nph, 4 * mpr, m // tk),
        pl.BlockSpec((tk, tm), lambda ni, mi, ki: (ki, mi)),
        pl.BlockSpec((tk, tn), lambda ni, mi, ki: (ki, ni)),
        pl.BlockSpec((None, None, tm, tn), lambda ni, mi, ki: (ni // nph, mi // mpr, mi % mpr, ni % nph)),
        (tm, tn), 2, name)


def _row_call(body, name, rows, ins, outs, tm=256):
    tm = _tile(rows, tm, 16)

    def spec(shape, kind):
        if kind == "rows":
            return pl.BlockSpec((tm, shape[1]), lambda i: (i, 0))
        return pl.BlockSpec(shape, lambda i: (0,) * len(shape))

    return pl.pallas_call(
        body, name=name, grid=(rows // tm,),
        in_specs=[spec(a.shape, kind) for a, kind in ins],
        out_specs=[spec(o.shape, kind) for o, kind in outs],
        out_shape=[o for o, _ in outs],
        compiler_params=_cp("arbitrary"),
    )(*[a for a, _ in ins])


def _rms(x):
    r = lax.rsqrt(_rowmean(x * x) + EPS)
    return x * r, r


def _rms_bwd(dxh, xh, r):
    return r * (dxh - xh * _rowmean(dxh * xh))


def _fwd_pre_mix(x, g, sc, sh):
    s, d = x.shape

    def body(x_ref, g_ref, sc_ref, sh_ref, h_ref):
        xh, _ = _rms(x_ref[...])
        h_ref[...] = (xh * g_ref[...] * (1.0 + sc_ref[...]) + sh_ref[...]).astype(h_ref.dtype)

    return _row_call(body, "fwd_pre_mix", s, [(x, "rows"), (g, "vec"), (sc, "vec"), (sh, "vec")],
                     [(jax.ShapeDtypeStruct((s, d), _MXU), "rows")])[0]


def _fwd_mid(o, x, gt1, g_post, g_pre2, sc2, sh2):
    s, d = x.shape

    def body(o_ref, x_ref, gt_ref, gp_ref, g2_ref, sc_ref, sh_ref, x1_ref, h2_ref):
        oh, _ = _rms(o_ref[...])
        x1 = x_ref[...] + gt_ref[...] * (oh * gp_ref[...])
        x1_ref[...] = x1
        xh, _ = _rms(x1)
        h2_ref[...] = (xh * g2_ref[...] * (1.0 + sc_ref[...]) + sh_ref[...]).astype(h2_ref.dtype)

    return _row_call(body, "fwd_mid", s,
                     [(o, "rows"), (x, "rows"), (gt1, "vec"), (g_post, "vec"), (g_pre2, "vec"), (sc2, "vec"),
                      (sh2, "vec")],
                     [(jax.ShapeDtypeStruct((s, d), _F32), "rows"), (jax.ShapeDtypeStruct((s, d), _MXU), "rows")])


def _loss_and_post_ffn_bwd(f, x1, tgt, gt2, g_post):
    s, d = x1.shape

    def body(f_ref, x1_ref, t_ref, gt_ref, g_ref, dx2_ref, df_ref, dgt_ref, dg_ref, loss_ref):
        first = pl.program_id(0) == 0
        _zero_first(first, dgt_ref, dg_ref, loss_ref)
        fh, r = _rms(f_ref[...])
        n = fh * g_ref[...]
        e = x1_ref[...] + gt_ref[...] * n - t_ref[...]
        _acc(loss_ref, first, jnp.sum(_colsum(e * e), axis=1, keepdims=True) * (0.5 / d))
        dx2 = e * (1.0 / d)
        dx2_ref[...] = dx2
        _acc(dgt_ref, first, _colsum(dx2 * n))
        dn = dx2 * gt_ref[...]
        _acc(dg_ref, first, _colsum(dn * fh))
        df_ref[...] = _rms_bwd(dn * g_ref[...], fh, r).astype(df_ref.dtype)

    vec = jax.ShapeDtypeStruct((1, d), _F32)
    return _row_call(body, "loss_post_ffn_bwd", s,
                     [(f, "rows"), (x1, "rows"), (tgt, "rows"), (gt2, "vec"), (g_post, "vec")],
                     [(jax.ShapeDtypeStruct((s, d), _F32), "rows"), (jax.ShapeDtypeStruct((s, d), _MXU), "rows"),
                      (vec, "vec"), (vec, "vec"), (jax.ShapeDtypeStruct((1, 1), _F32), "vec")])


def _bwd_mid(dh2, x1, dx2, o, g_pre2, sc2, gt1, g_post):
    s, d = x1.shape

    def body(dh_ref, x1_ref, dx2_ref, o_ref, g2_ref, sc_ref, gt_ref, gp_ref,
             dx1_ref, do_ref, dsc_ref, dsh_ref, dg2_ref, dgt_ref, dgp_ref):
        first = pl.program_id(0) == 0
        _zero_first(first, dsc_ref, dsh_ref, dg2_ref, dgt_ref, dgp_ref)
        dh = dh_ref[...]
        xh, r = _rms(x1_ref[...])
        _acc(dsh_ref, first, _colsum(dh))
        _acc(dsc_ref, first, _colsum(dh * (xh * g2_ref[...])))
        dn = dh * (1.0 + sc_ref[...])
        _acc(dg2_ref, first, _colsum(dn * xh))
        dx1 = dx2_ref[...] + _rms_bwd(dn * g2_ref[...], xh, r)
        dx1_ref[...] = dx1
        oh, ro = _rms(o_ref[...])
        _acc(dgt_ref, first, _colsum(dx1 * (oh * gp_ref[...])))
        dno = dx1 * gt_ref[...]
        _acc(dgp_ref, first, _colsum(dno * oh))
        do_ref[...] = _rms_bwd(dno * gp_ref[...], oh, ro).astype(do_ref.dtype)

    vec = jax.ShapeDtypeStruct((1, d), _F32)
    return _row_call(body, "bwd_mid", s,
                     [(dh2, "rows"), (x1, "rows"), (dx2, "rows"), (o, "rows"), (g_pre2, "vec"), (sc2, "vec"),
                      (gt1, "vec"), (g_post, "vec")],
                     [(jax.ShapeDtypeStruct((s, d), _F32), "rows"), (jax.ShapeDtypeStruct((s, d), _MXU), "rows"),
                      (vec, "vec"), (vec, "vec"), (vec, "vec"), (vec, "vec"), (vec, "vec")])


def _bwd_pre_mix(dh1, x, dx1, g, sc1):
    s, d = x.shape

    def body(dh_ref, x_ref, dx1_ref, g_ref, sc_ref, dx_ref, dsc_ref, dsh_ref, dg_ref):
        first = pl.program_id(0) == 0
        _zero_first(first, dsc_ref, dsh_ref, dg_ref)
        dh = dh_ref[...]
        xh, r = _rms(x_ref[...])
        _acc(dsh_ref, first, _colsum(dh))
        _acc(dsc_ref, first, _colsum(dh * (xh * g_ref[...])))
        dn = dh * (1.0 + sc_ref[...])
        _acc(dg_ref, first, _colsum(dn * xh))
        dx_ref[...] = dx1_ref[...] + _rms_bwd(dn * g_ref[...], xh, r)

    vec = jax.ShapeDtypeStruct((1, d), _F32)
    return _row_call(body, "bwd_pre_mix", s,
                     [(dh1, "rows"), (x, "rows"), (dx1, "rows"), (g, "vec"), (sc1, "vec")],
                     [(jax.ShapeDtypeStruct((s, d), _F32), "rows"), (vec, "vec"), (vec, "vec"), (vec, "vec")])


def _mix_norm_fwd(y_ssm, y_sgu, g_ssm, g_sgu):
    s, h = y_ssm.shape

    def body(a_ref, b_ref, ga_ref, gb_ref, o_ref):
        ah, _ = _rms(a_ref[...])
        bh, _ = _rms(b_ref[...])
        o_ref[:, 0:h] = (ah * ga_ref[...]).astype(o_ref.dtype)
        o_ref[:, h:2 * h] = (bh * gb_ref[...]).astype(o_ref.dtype)

    return _row_call(body, "mix_norm_fwd", s, [(y_ssm, "rows"), (y_sgu, "rows"), (g_ssm, "vec"), (g_sgu, "vec")],
                     [(jax.ShapeDtypeStruct((s, 2 * h), _MXU), "rows")])[0]


def _mix_norm_bwd(dyc, y_ssm, y_sgu, g_ssm, g_sgu):
    s, h = y_ssm.shape

    def body(d_ref, a_ref, b_ref, ga_ref, gb_ref, da_ref, db_ref, dga_ref, dgb_ref):
        first = pl.program_id(0) == 0
        _zero_first(first, dga_ref, dgb_ref)
        for lo, y_ref, g_ref, dy_ref, dg_ref in ((0, a_ref, ga_ref, da_ref, dga_ref), (h, b_ref, gb_ref, db_ref, dgb_ref)):
            d = d_ref[:, lo:lo + h]
            yh, r = _rms(y_ref[...])
            _acc(dg_ref, first, _colsum(d * yh))
            dy_ref[...] = _rms_bwd(d * g_ref[...], yh, r)

    vec = jax.ShapeDtypeStruct((1, h), _F32)
    full = jax.ShapeDtypeStruct((s, h), _F32)
    return _row_call(body, "mix_norm_bwd", s,
                     [(dyc, "rows"), (y_ssm, "rows"), (y_sgu, "rows"), (g_ssm, "vec"), (g_sgu, "vec")],
                     [(full, "rows"), (full, "rows"), (vec, "vec"), (vec, "vec")])


CONV_ROWS = 64


def _conv_rows(ext, w_ref, b_ref):
    x = ext[SUBLANES:]
    s1 = pltpu.roll(ext, 1, 0)[SUBLANES:]
    s2 = pltpu.roll(ext, 2, 0)[SUBLANES:]
    return b_ref[...] + w_ref[0:1, :] * s2 + w_ref[1:2, :] * s1 + w_ref[2:3, :] * x, x, s1, s2


def _conv_window(x_ref, r0):
    if isinstance(r0, int):
        assert r0 == 0
        return jnp.concatenate([jnp.zeros((SUBLANES, x_ref.shape[1]), _F32), x_ref[0:CONV_ROWS, :]], axis=0)
    return x_ref[pl.ds(pl.multiple_of(r0 - SUBLANES, SUBLANES), CONV_ROWS + SUBLANES), :]


def _conv_act_fwd(up_pre, conv_w, conv_b):
    s, f2 = up_pre.shape
    f = f2 // 2
    tc = _tile(f, 256)
    nf = f // tc

    def shift_down(x, k):
        row = lax.broadcasted_iota(jnp.int32, x.shape, 0)
        return jnp.where(row >= k, pltpu.roll(x, k, 0), 0.0)

    def conv(x, w_ref, b_ref):
        return b_ref[...] + w_ref[0:1, :] * shift_down(x, 2) + w_ref[1:2, :] * shift_down(x, 1) + w_ref[2:3, :] * x

    def body(a_ref, b_ref, wa_ref, wb_ref, ba_ref, bb_ref, o_ref):
        a = conv(a_ref[...], wa_ref, ba_ref)
        b = conv(b_ref[...], wb_ref, bb_ref)
        o_ref[...] = (a * _sigmoid(a) * b).astype(o_ref.dtype)

    return pl.pallas_call(
        body, name="conv_act_fwd", grid=(nf,), out_shape=jax.ShapeDtypeStruct((s, f), _MXU),
        in_specs=[pl.BlockSpec((s, tc), lambda n: (0, n)), pl.BlockSpec((s, tc), lambda n: (0, n + nf)),
                  pl.BlockSpec((3, tc), lambda n: (0, n)), pl.BlockSpec((3, tc), lambda n: (0, n + nf)),
                  pl.BlockSpec((1, tc), lambda n: (0, n)), pl.BlockSpec((1, tc), lambda n: (0, n + nf))],
        out_specs=pl.BlockSpec((s, tc), lambda n: (0, n)), compiler_params=_cp("parallel"),
    )(up_pre, up_pre, conv_w, conv_w, conv_b, conv_b)


def _conv_act_bwd(up_pre, d_act, conv_w, conv_b):
    s, f2 = up_pre.shape
    f = f2 // 2
    tc = _tile(f, 256)
    nf = f // tc

    def body(a_ref, b_ref, d_ref, wa_ref, wb_ref, ba_ref, bb_ref,
             du_ref, w0a, w0b, w1a, w1b, w2a, w2b, dba, dbb):
        n = s // CONV_ROWS
        zero8 = jnp.zeros((SUBLANES, tc), _F32)
        ext_rows = CONV_ROWS + SUBLANES

        def fold(x):
            out = x[0:SUBLANES]
            for k in range(1, CONV_ROWS // SUBLANES):
                out = out + x[k * SUBLANES:(k + 1) * SUBLANES]
            return out

        def chunk(r0, carry):
            nxt, acc = carry
            a, xa, xa1, xa2 = _conv_rows(_conv_window(a_ref, r0), wa_ref, ba_ref)
            b, xb, xb1, xb2 = _conv_rows(_conv_window(b_ref, r0), wb_ref, bb_ref)
            sg = _sigmoid(a)
            d = d_ref[pl.ds(r0, CONV_ROWS), :]
            du_a = d * b * (sg * (1.0 + a * (1.0 - sg)))
            du_b = d * (a * sg)
            new_acc = []
            for h, (du, x0, x1, x2, w_ref) in enumerate(((du_a, xa, xa1, xa2, wa_ref), (du_b, xb, xb1, xb2, wb_ref))):
                ext = jnp.concatenate([du, nxt[h]], axis=0)
                u1 = pltpu.roll(ext, ext_rows - 1, 0)[:CONV_ROWS]
                u2 = pltpu.roll(ext, ext_rows - 2, 0)[:CONV_ROWS]
                du_ref[h, pl.ds(r0, CONV_ROWS), :] = (w_ref[2:3, :] * du + w_ref[1:2, :] * u1
                                                      + w_ref[0:1, :] * u2).astype(du_ref.dtype)
                new_acc += [acc[4 * h] + fold(du * x2), acc[4 * h + 1] + fold(du * x1), acc[4 * h + 2] + fold(du * x0),
                            acc[4 * h + 3] + fold(du)]
            return (du_a[:SUBLANES], du_b[:SUBLANES]), tuple(new_acc)

        def step(i, carry):
            return chunk(pl.multiple_of((n - 1 - i) * CONV_ROWS, CONV_ROWS), carry)

        carry = lax.fori_loop(0, n - 1, step, ((zero8, zero8), (zero8,) * 8))
        _, acc = chunk(0, carry)
        for ref, val in zip((w0a, w1a, w2a, dba, w0b, w1b, w2b, dbb), acc):
            ref[...] = _colsum(val)

    col_a = pl.BlockSpec((s, tc), lambda n: (0, n))
    col_b = pl.BlockSpec((s, tc), lambda n: (0, n + nf))
    vec_a = pl.BlockSpec((1, tc), lambda n: (0, n))
    vec_b = pl.BlockSpec((1, tc), lambda n: (0, n + nf))
    vec = jax.ShapeDtypeStruct((1, f), _F32)
    outs = pl.pallas_call(
        body, name="conv_act_bwd", grid=(nf,),
        in_specs=[col_a, col_b, col_a, pl.BlockSpec((3, tc), lambda n: (0, n)),
                  pl.BlockSpec((3, tc), lambda n: (0, n + nf)), vec_a, vec_b],
        out_specs=[pl.BlockSpec((2, s, tc), lambda n: (0, 0, n))] + [vec_a] * 8,
        out_shape=[jax.ShapeDtypeStruct((2, s, f), _MXU)] + [vec] * 8, compiler_params=_cp("parallel"),
    )(up_pre, up_pre, d_act, conv_w, conv_w, conv_b, conv_b)
    du, w0a, w0b, w1a, w1b, w2a, w2b, dba, dbb = outs
    cat = lambda p, q: jnp.concatenate([p, q], axis=1)
    return du, cat(w0a, w0b), cat(w1a, w1b), cat(w2a, w2b), cat(dba, dbb)


def _sgu_recompute(zu_ref, zv_ref, lng_ref, lnb_ref, wm_ref, bs_ref, nh):
    zu, zv = zu_ref[...], zv_ref[...]
    u = _gelu(zu)
    gv = _gelu(zv)
    xc = gv - _rowmean(gv)
    rs = lax.rsqrt(_rowmean(xc * xc) + EPS)
    vh = xc * rs
    v = vh * lng_ref[...] + lnb_ref[...]
    mixed = []
    for h in range(nh):
        vhd = v[:, h * CHUNK:(h + 1) * CHUNK].astype(_MXU)
        mixed.append(jnp.dot(wm_ref[h].astype(_MXU), vhd, preferred_element_type=_F32) + bs_ref[h])
    return zu, zv, u, vh, rs, v, mixed


def _sgu_fwd(z, ln_g, ln_b, wm, bs):
    s = z.shape[0]
    nh = wm.shape[0]
    hd = nh * CHUNK

    def body(zu_ref, zv_ref, lng_ref, lnb_ref, wm_ref, bs_ref, y_ref):
        _, _, u, _, _, _, mixed = _sgu_recompute(zu_ref, zv_ref, lng_ref, lnb_ref, wm_ref, bs_ref, nh)
        for h in range(nh):
            y_ref[:, h * CHUNK:(h + 1) * CHUNK] = u[:, h * CHUNK:(h + 1) * CHUNK] * mixed[h]

    vec = pl.BlockSpec((1, hd), lambda i: (0, 0))
    return pl.pallas_call(
        body, name="sgu_fwd", grid=(s // CHUNK,), out_shape=jax.ShapeDtypeStruct((s, hd), _F32),
        in_specs=[pl.BlockSpec((CHUNK, hd), lambda i: (i, 1)), pl.BlockSpec((CHUNK, hd), lambda i: (i, 2)), vec, vec,
                  pl.BlockSpec((nh, CHUNK, CHUNK), lambda i: (0, 0, 0)), pl.BlockSpec((nh, CHUNK, 1), lambda i: (0, 0, 0))],
        out_specs=pl.BlockSpec((CHUNK, hd), lambda i: (i, 0)), compiler_params=_cp("parallel"),
    )(z, z, ln_g, ln_b, wm, bs)


def _sgu_bwd(z, dy, dz_ssm, ln_g, ln_b, wm, bs):
    s = z.shape[0]
    nh = wm.shape[0]
    hd = nh * CHUNK

    def body(zu_ref, zv_ref, dy_ref, dzs_ref, lng_ref, lnb_ref, wm_ref, bs_ref,
             dz_ref, dlg_ref, dlb_ref, dwm_ref, dbs_ref, dv_scr):
        first = pl.program_id(0) == 0
        _zero_first(first, dlg_ref, dlb_ref, dwm_ref, dbs_ref)
        zu, zv, u, vh, rs, v, mixed = _sgu_recompute(zu_ref, zv_ref, lng_ref, lnb_ref, wm_ref, bs_ref, nh)
        dy = dy_ref[...]
        dz_ref[:, 0:hd] = dzs_ref[...].astype(dz_ref.dtype)
        for h in range(nh):
            cols = slice(h * CHUNK, (h + 1) * CHUNK)
            dyh = dy[:, cols]
            dz_ref[:, hd + h * CHUNK:hd + (h + 1) * CHUNK] = (dyh * mixed[h] * _gelu_grad(zu[:, cols])).astype(dz_ref.dtype)
            dm = dyh * u[:, cols]
            dmx = dm.astype(_MXU)
            _acc(dbs_ref.at[h], first, jnp.sum(dm, axis=1, keepdims=True))
            _acc(dwm_ref.at[h], first,
                 lax.dot_general(dmx, v[:, cols].astype(_MXU), (((1,), (1,)), ((), ())), preferred_element_type=_F32))
            dv_scr[:, cols] = lax.dot_general(wm_ref[h].astype(_MXU), dmx, (((0,), (0,)), ((), ())),
                                              preferred_element_type=_F32)
        dv = dv_scr[...]
        _acc(dlg_ref, first, _colsum(dv * vh))
        _acc(dlb_ref, first, _colsum(dv))
        dvh = dv * lng_ref[...]
        dgv = rs * (dvh - _rowmean(dvh) - vh * _rowmean(dvh * vh))
        dz_ref[:, 2 * hd:3 * hd] = (dgv * _gelu_grad(zv)).astype(dz_ref.dtype)

    vec = pl.BlockSpec((1, hd), lambda i: (0, 0))
    wspec = pl.BlockSpec((nh, CHUNK, CHUNK), lambda i: (0, 0, 0))
    bspec = pl.BlockSpec((nh, CHUNK, 1), lambda i: (0, 0, 0))
    rows = pl.BlockSpec((CHUNK, hd), lambda i: (i, 0))
    return pl.pallas_call(
        body, name="sgu_bwd", grid=(s // CHUNK,),
        out_shape=[jax.ShapeDtypeStruct((s, 3 * hd), _MXU), jax.ShapeDtypeStruct((1, hd), _F32),
                   jax.ShapeDtypeStruct((1, hd), _F32), jax.ShapeDtypeStruct((nh, CHUNK, CHUNK), _F32),
                   jax.ShapeDtypeStruct((nh, CHUNK, 1), _F32)],
        in_specs=[pl.BlockSpec((CHUNK, hd), lambda i: (i, 1)), pl.BlockSpec((CHUNK, hd), lambda i: (i, 2)),
                  rows, rows, vec, vec, wspec, bspec],
        out_specs=[pl.BlockSpec((CHUNK, 3 * hd), lambda i: (i, 0)), vec, vec, wspec, bspec],
        scratch_shapes=[pltpu.VMEM((CHUNK, hd), _F32)], compiler_params=_cp("arbitrary"),
    )(z, z, dy, dz_ssm, ln_g, ln_b, wm, bs)


def _ssm_prep(log_dt, a_re, a_im, b_re_t, b_im_t, kvec):
    gn = a_re.shape[1]

    def body(ldt_ref, are_ref, aim_ref, br_ref, bi_ref, k_ref, pr_ref, pi_ref, bbr_ref, bbi_ref):
        dt = jnp.exp(ldt_ref[...])
        are, aim = are_ref[...], aim_ref[...]
        k = k_ref[...]
        mag = jnp.exp(k * (are * dt))
        ang = k * (aim * dt)
        pr_ref[...] = mag * jnp.cos(ang)
        pi_ref[...] = mag * jnp.sin(ang)
        m1 = jnp.exp(are * dt)
        lr, li = m1 * jnp.cos(aim * dt), m1 * jnp.sin(aim * dt)
        den = are * are + aim * aim
        nr = lr - 1.0
        f_re = (nr * are + li * aim) / den
        f_im = (li * are - nr * aim) / den
        bbr_ref[...] = f_re * br_ref[...] - f_im * bi_ref[...]
        bbi_ref[...] = f_re * bi_ref[...] + f_im * br_ref[...]

    pw = jax.ShapeDtypeStruct((kvec.shape[0], gn), _F32)
    bb = jax.ShapeDtypeStruct(b_re_t.shape, _F32)
    return pl.pallas_call(body, name="ssm_prep", out_shape=[pw, pw, bb, bb])(log_dt, a_re, a_im, b_re_t, b_im_t, kvec)


def _ssm_prep_bwd(log_dt, a_re, a_im, b_re_t, b_im_t, d_bbr, d_bbi, d_lr, d_li):
    def body(ldt_ref, are_ref, aim_ref, br_ref, bi_ref, dbr_ref, dbi_ref, dlr_ref, dli_ref,
             obr_ref, obi_ref, oar_ref, oai_ref, odt_ref):
        dt = jnp.exp(ldt_ref[...])
        are, aim = are_ref[...], aim_ref[...]
        m1 = jnp.exp(are * dt)
        lr, li = m1 * jnp.cos(aim * dt), m1 * jnp.sin(aim * dt)
        den = are * are + aim * aim
        nr = lr - 1.0
        f_re = (nr * are + li * aim) / den
        f_im = (li * are - nr * aim) / den
        br, bi, dbr, dbi = br_ref[...], bi_ref[...], dbr_ref[...], dbi_ref[...]
        obr_ref[...] = f_re * dbr + f_im * dbi
        obi_ref[...] = f_re * dbi - f_im * dbr
        gf_re = _colsum(br * dbr + bi * dbi)
        gf_im = _colsum(br * dbi - bi * dbr)
        il_re, il_im = are / den, -aim / den
        glb_re = dlr_ref[...] + (il_re * gf_re + il_im * gf_im)
        glb_im = dli_ref[...] + (il_re * gf_im - il_im * gf_re)
        q_re = -(f_re * il_re - f_im * il_im)
        q_im = -(f_re * il_im + f_im * il_re)
        gl_re = q_re * gf_re + q_im * gf_im
        gl_im = q_re * gf_im - q_im * gf_re
        gl_re = gl_re + dt * (lr * glb_re + li * glb_im)
        gl_im = gl_im + dt * (lr * glb_im - li * glb_re)
        w_re = are * lr - aim * li
        w_im = are * li + aim * lr
        oar_ref[...] = gl_re
        oai_ref[...] = gl_im
        odt_ref[...] = w_re * glb_re + w_im * glb_im

    bb = jax.ShapeDtypeStruct(b_re_t.shape, _F32)
    v = jax.ShapeDtypeStruct(a_re.shape, _F32)
    return pl.pallas_call(body, name="ssm_prep_bwd", out_shape=[bb, bb, v, v, v])(
        log_dt, a_re, a_im, b_re_t, b_im_t, d_bbr, d_bbi, d_lr, d_li)


def _group_sum(d_dt, log_dt):
    def body(d_ref, l_ref, o_ref):
        o_ref[...] = jnp.sum(d_ref[...], axis=1, keepdims=True) * jnp.exp(l_ref[...])

    return pl.pallas_call(body, name="ssm_dt_grad", out_shape=jax.ShapeDtypeStruct(log_dt.shape, _F32))(d_dt, log_dt)


def _load_strided(ref, nr):
    return jnp.concatenate([ref[pl.ds(r, SUBLANES, stride=nr), :] for r in range(nr)], axis=0)


def _store_strided(ref, val, nr):
    for r in range(nr):
        ref[pl.ds(r, SUBLANES, stride=nr), :] = val[r * SUBLANES:(r + 1) * SUBLANES]


def _scan_strided(src_ref, dst_ref, nr, p_ref, carry, reverse, h_ref=None, h_in=None):
    ns = BLOCK_ST
    row = lax.broadcasted_iota(jnp.int32, (SUBLANES, ns), 0)
    bc = lambda v: jnp.broadcast_to(v, (SUBLANES, ns))
    tile = lambda ref, r: (ref[r * SUBLANES:(r + 1) * SUBLANES, 0:ns], ref[r * SUBLANES:(r + 1) * SUBLANES, ns:2 * ns])
    one = nr - 1 if reverse else 0
    ar, ai = bc(p_ref[one:one + 1, 0:ns]), bc(p_ref[one:one + 1, ns:2 * ns])
    xr = xi = None
    for r in (range(nr - 1, -1, -1) if reverse else range(nr)):
        sr, si = tile(src_ref, r)
        xr, xi = (sr, si) if xr is None else (ar * xr - ai * xi + sr, ar * xi + ai * xr + si)
        dst_ref[r * SUBLANES:(r + 1) * SUBLANES, 0:ns] = xr
        dst_ref[r * SUBLANES:(r + 1) * SUBLANES, ns:2 * ns] = xi
    edge, shift = (SUBLANES - 1, SUBLANES - 1) if reverse else (0, 1)
    dr = jnp.where(row == edge, carry[0], pltpu.roll(xr, shift, 0))
    di = jnp.where(row == edge, carry[1], pltpu.roll(xi, shift, 0))
    for i, k in enumerate((1, 2, 4)):
        qr, qi = bc(p_ref[nr + i:nr + i + 1, 0:ns]), bc(p_ref[nr + i:nr + i + 1, ns:2 * ns])
        keep = (row < SUBLANES - k) if reverse else (row >= k)
        sr = jnp.where(keep, pltpu.roll(dr, (SUBLANES - k) if reverse else k, 0), 0.0)
        si = jnp.where(keep, pltpu.roll(di, (SUBLANES - k) if reverse else k, 0), 0.0)
        dr, di = dr + qr * sr - qi * si, di + qr * si + qi * sr
    acc_r = acc_i = jnp.zeros((SUBLANES, ns), _F32)
    out = None
    for r in range(nr):
        wr, wi = p_ref[r:r + 1, 0:ns], p_ref[r:r + 1, ns:2 * ns]
        xr, xi = tile(dst_ref, r)
        xr, xi = xr + wr * dr - wi * di, xi + wr * di + wi * dr
        dst_ref[r * SUBLANES:(r + 1) * SUBLANES, 0:ns] = xr
        dst_ref[r * SUBLANES:(r + 1) * SUBLANES, ns:2 * ns] = xi
        if h_ref is not None:
            if r == 0:
                lr, li = tile(h_ref, nr - 1)
                pr, pi = jnp.where(row == 0, h_in[0], pltpu.roll(lr, 1, 0)), jnp.where(row == 0, h_in[1], pltpu.roll(li, 1, 0))
            else:
                pr, pi = tile(h_ref, r - 1)
            acc_r = acc_r + (xr * pr + xi * pi)
            acc_i = acc_i + (xi * pr - xr * pi)
        if r == (0 if reverse else nr - 1):
            out = (xr[0:1, :], xi[0:1, :]) if reverse else (xr[SUBLANES - 1:SUBLANES, :], xi[SUBLANES - 1:SUBLANES, :])
    if h_ref is None:
        return out
    return out, (_colsum(acc_r), _colsum(acc_i))


def _ssm_gate(y, wg_ref, bg_ref):
    yg = _gelu(y)
    gate = _sigmoid(jnp.dot(yg.astype(_MXU), wg_ref[...].astype(_MXU), preferred_element_type=_F32) + bg_ref[...])
    return yg, gate


def _ssm_specs(nb, nt, t, reverse):
    tt = (lambda ti: nt - 1 - ti) if reverse else (lambda ti: ti)
    ns2 = 2 * BLOCK_ST
    return dict(
        z=pl.BlockSpec((t, BLOCK_CH), lambda b, ti: (tt(ti), b)),
        bbt=pl.BlockSpec((None, BLOCK_CH, ns2), lambda b, ti: (b, 0, 0)),
        ct=pl.BlockSpec((None, ns2, BLOCK_CH), lambda b, ti: (b, 0, 0)),
        vec=pl.BlockSpec((1, BLOCK_CH), lambda b, ti: (0, b)),
        wg=pl.BlockSpec((None, BLOCK_CH, BLOCK_CH), lambda b, ti: (b, 0, 0)),
        p=pl.BlockSpec((None, t // SUBLANES + SUBLANES, ns2), lambda b, ti: (b, 0, 0)),
        hb=pl.BlockSpec((None, None, SUBLANES, ns2), lambda b, ti: (b, tt(ti), 0, 0)),
        h=pl.BlockSpec((None, t, ns2), lambda b, ti: (b, tt(ti), 0)),
        acc_vec=pl.BlockSpec((None, 1, ns2), lambda b, ti: (b, 0, 0)),
    )


def _ssm_fwd(z, bbt, ct, dvec, wg, bglu, ptab):
    s = z.shape[0]
    nb = bbt.shape[0]
    t = _tile(s, TIME_TILE, SUBLANES)
    nt = s // t
    ns = BLOCK_ST
    sp = _ssm_specs(nb, nt, t, False)

    nr = t // SUBLANES

    def body(z_ref, bbt_ref, ct_ref, d_ref, wg_ref, bg_ref, p_ref, y2_ref, y_ref, h_ref, hb_ref, bu_scr, h_scr, carry_scr):
        _zero_first(pl.program_id(1) == 0, carry_scr)
        hb_ref[...] = carry_scr[...]
        carry_in = (carry_scr[0:1, 0:ns], carry_scr[0:1, ns:2 * ns])
        u = _load_strided(z_ref, nr)
        bu_scr[...] = jnp.dot(u.astype(_MXU), bbt_ref[...].astype(_MXU), preferred_element_type=_F32)
        cr, ci = _scan_strided(bu_scr, h_scr, nr, p_ref, carry_in, False)
        hx = h_scr[...].astype(_MXU)
        h_ref[...] = hx
        y = jnp.dot(hx, ct_ref[...].astype(_MXU), preferred_element_type=_F32) + d_ref[...] * u
        yg, gate = _ssm_gate(y, wg_ref, bg_ref)
        _store_strided(y2_ref, yg * gate, nr)
        _store_strided(y_ref, y, nr)
        carry_scr[:, 0:ns] = jnp.broadcast_to(cr, (SUBLANES, ns))
        carry_scr[:, ns:2 * ns] = jnp.broadcast_to(ci, (SUBLANES, ns))

    ych = jax.ShapeDtypeStruct((s, nb * BLOCK_CH), _F32)
    return pl.pallas_call(
        body, name="ssm_fwd", grid=(nb, nt),
        out_shape=[ych, ych, jax.ShapeDtypeStruct((nb, s, 2 * ns), _MXU),
                   jax.ShapeDtypeStruct((nb, nt, SUBLANES, 2 * ns), _F32)],
        in_specs=[sp["z"], sp["bbt"], sp["ct"], sp["vec"], sp["wg"], sp["vec"], sp["p"]],
        out_specs=[sp["z"], sp["z"], sp["h"], sp["hb"]],
        scratch_shapes=[pltpu.VMEM((t, 2 * ns), _F32), pltpu.VMEM((t, 2 * ns), _F32), pltpu.VMEM((SUBLANES, 2 * ns), _F32)],
        compiler_params=_cp("parallel", "arbitrary"),
    )(z, bbt, ct, dvec, wg, bglu, ptab)


def _ssm_bwd(z, y_pre, h_all, dy2, hb, bbt, ct, dvec, wg, bglu, ptab_rev):
    s = z.shape[0]
    nb = bbt.shape[0]
    t = _tile(s, TIME_TILE, SUBLANES)
    nt = s // t
    ns = BLOCK_ST
    sp = _ssm_specs(nb, nt, t, True)
    tn_dims = (((0,), (0,)), ((), ()))
    nt_dims = (((1,), (1,)), ((), ()))

    nr = t // SUBLANES

    def body(z_ref, y_ref, h_ref, dy2_ref, hb_ref, bbt_ref, ct_ref, d_ref, wg_ref, bg_ref, pr_ref,
             dz_ref, dbbt_ref, dct_ref, dwg_ref, dlb_ref, dd_ref, dbg_ref, bu_scr, g_scr, h_scr, gcarry_scr):
        first = pl.program_id(1) == 0

        _zero_first(first, gcarry_scr, dbbt_ref, dct_ref, dwg_ref, dlb_ref, dd_ref, dbg_ref)
        u = _load_strided(z_ref, nr)
        hin = hb_ref[...]
        y = _load_strided(y_ref, nr)
        yg, gate = _ssm_gate(y, wg_ref, bg_ref)
        dy2 = _load_strided(dy2_ref, nr)
        dpre = dy2 * yg * gate * (1.0 - gate)
        _acc(dbg_ref, first, _colsum(dpre))
        dpx = dpre.astype(_MXU)
        _acc(dwg_ref, first, lax.dot_general(yg.astype(_MXU), dpx, tn_dims, preferred_element_type=_F32))
        dyg = dy2 * gate + lax.dot_general(dpx, wg_ref[...].astype(_MXU), nt_dims, preferred_element_type=_F32)
        dy = dyg * _gelu_grad(y)
        _acc(dd_ref, first, _colsum(dy * u))
        dyx = dy.astype(_MXU)
        hx = h_ref[...]
        h_scr[...] = hx.astype(_F32)
        _acc(dct_ref, first, lax.dot_general(hx, dyx, tn_dims, preferred_element_type=_F32))
        bu_scr[...] = lax.dot_general(dyx, ct_ref[...].astype(_MXU), nt_dims, preferred_element_type=_F32)
        gin = (gcarry_scr[0:1, 0:ns], gcarry_scr[0:1, ns:2 * ns])
        (gr, gi), (d_ar, d_ai) = _scan_strided(bu_scr, g_scr, nr, pr_ref, gin, True, h_scr,
                                               (hin[0:1, 0:ns], hin[0:1, ns:2 * ns]))
        gcarry_scr[:, 0:ns] = jnp.broadcast_to(gr, (SUBLANES, ns))
        gcarry_scr[:, ns:2 * ns] = jnp.broadcast_to(gi, (SUBLANES, ns))
        _acc(dlb_ref, first, jnp.concatenate([d_ar, d_ai], axis=1))
        gx = g_scr[...].astype(_MXU)
        _acc(dbbt_ref, first, lax.dot_general(u.astype(_MXU), gx, tn_dims, preferred_element_type=_F32))
        _store_strided(dz_ref, dy * d_ref[...] + lax.dot_general(gx, bbt_ref[...].astype(_MXU), nt_dims,
                                                                 preferred_element_type=_F32), nr)

    f = lambda shape: jax.ShapeDtypeStruct(shape, _F32)
    return pl.pallas_call(
        body, name="ssm_bwd", grid=(nb, nt),
        out_shape=[f((s, nb * BLOCK_CH)), f(bbt.shape), f(ct.shape), f(wg.shape), f((nb, 1, 2 * ns)),
                   f((1, nb * BLOCK_CH)), f((1, nb * BLOCK_CH))],
        in_specs=[sp["z"], sp["z"], sp["h"], sp["z"], sp["hb"], sp["bbt"], sp["ct"], sp["vec"], sp["wg"], sp["vec"], sp["p"]],
        out_specs=[sp["z"], sp["bbt"], sp["ct"], sp["wg"], sp["acc_vec"], sp["vec"], sp["vec"]],
        scratch_shapes=[pltpu.VMEM((t, 2 * ns), _F32), pltpu.VMEM((t, 2 * ns), _F32), pltpu.VMEM((t, 2 * ns), _F32),
                        pltpu.VMEM((SUBLANES, 2 * ns), _F32)],
        compiler_params=_cp("parallel", "arbitrary"),
    )(z, y_pre, h_all, dy2, hb, bbt, ct, dvec, wg, bglu, ptab_rev)


def _mod_part(c_all, w, b):
    d, ns = w.shape
    tn = _tile(ns, 512)

    def body(c_ref, w_ref, b_ref, o_ref):
        c = c_ref[...]
        ca = (c * _sigmoid(c)).astype(_MXU)
        o_ref[...] = jnp.dot(ca, w_ref[...].astype(_MXU), preferred_element_type=_F32) + b_ref[...]

    return pl.pallas_call(
        body, name="mod_part", grid=(ns // tn,), out_shape=jax.ShapeDtypeStruct((8, ns), _F32),
        in_specs=[pl.BlockSpec((8, d), lambda n: (0, 0)), pl.BlockSpec((d, tn), lambda n: (0, n)),
                  pl.BlockSpec((1, tn), lambda n: (0, n))],
        out_specs=pl.BlockSpec((8, tn), lambda n: (0, n)), compiler_params=_cp("parallel"),
    )(c_all, w, b)


def _adamw_math(w, g, m, v):
    m = ADAM_B1 * m + (1.0 - ADAM_B1) * g
    v = ADAM_B2 * v + (1.0 - ADAM_B2) * (g * g)
    m_hat = m / (1.0 - ADAM_B1 ** ADAM_STEP)
    v_hat = v / (1.0 - ADAM_B2 ** ADAM_STEP)
    delta = -ADAM_LR * (m_hat / (jnp.sqrt(v_hat) + ADAM_EPS) + ADAM_WD * w)
    return delta, m, v


def _adamw(w, g, m, v, name):
    r, c = w.shape
    tc = c if c <= 4096 else _tile(c, 4096)
    tr = _tile(r, max(SUBLANES, (1 << 18) // tc), SUBLANES)

    def body(w_ref, g_ref, m_ref, v_ref, go_ref, d_ref, mo_ref, vo_ref):
        g = g_ref[...]
        go_ref[...] = g
        d_ref[...], mo_ref[...], vo_ref[...] = _adamw_math(w_ref[...], g, m_ref[...], v_ref[...])

    spec = pl.BlockSpec((tr, tc), lambda i, j: (i, j))
    out = jax.ShapeDtypeStruct((r, c), _F32)
    return pl.pallas_call(
        body, name=name, grid=(r // tr, c // tc), in_specs=[spec] * 4, out_specs=[spec] * 4, out_shape=[out] * 4,
        compiler_params=_cp("parallel", "parallel"),
    )(w, g, m, v)


def _adamw_halves(w, g2, m, v, name):
    r, c = w.shape
    tr, tc = _tile(r, 256, SUBLANES), _tile(c // 2, 1024)
    nph = (c // 2) // tc

    def body(w_ref, g_ref, m_ref, v_ref, go_ref, d_ref, mo_ref, vo_ref):
        g = g_ref[...]
        go_ref[...] = g
        d_ref[...], mo_ref[...], vo_ref[...] = _adamw_math(w_ref[...], g, m_ref[...], v_ref[...])

    spec = pl.BlockSpec((tr, tc), lambda i, j: (i, j))
    out = jax.ShapeDtypeStruct((r, c), _F32)
    return pl.pallas_call(
        body, name=name, grid=(r // tr, c // tc),
        in_specs=[spec, pl.BlockSpec((None, tr, tc), lambda i, j: (j // nph, i, j % nph)), spec, spec],
        out_specs=[spec] * 4, out_shape=[out] * 4, compiler_params=_cp("parallel", "parallel"),
    )(w, g2, m, v)


def _wada_update(c_t, dm, w, m, v):
    d, ns = w.shape
    tr, tc = _tile(d, 256, SUBLANES), _tile(ns, 1024)

    def body(c_ref, dm_ref, w_ref, m_ref, v_ref, g_ref, d_ref, mo_ref, vo_ref):
        c = c_ref[...]
        ca = c * _sigmoid(c)
        dmv = dm_ref[...]
        g = ca[:, 0:1] * dmv[0:1, :]
        for b in range(1, 8):
            g = g + ca[:, b:b + 1] * dmv[b:b + 1, :]
        g_ref[...] = g
        d_ref[...], mo_ref[...], vo_ref[...] = _adamw_math(w_ref[...], g, m_ref[...], v_ref[...])

    spec = pl.BlockSpec((tr, tc), lambda i, j: (i, j))
    out = jax.ShapeDtypeStruct((d, ns), _F32)
    return pl.pallas_call(
        body, name="wada_update", grid=(d // tr, ns // tc),
        in_specs=[pl.BlockSpec((tr, 8), lambda i, j: (i, 0)), pl.BlockSpec((8, tc), lambda i, j: (0, j)), spec, spec, spec],
        out_specs=[spec] * 4, out_shape=[out] * 4, compiler_params=_cp("parallel", "parallel"),
    )(c_t, dm, w, m, v)


def _small_reduce(gathered):
    _, r, c = gathered.shape
    tr = _tile(r, 512, SUBLANES)

    def body(q_ref, g_ref):
        g = q_ref[0]
        for k in range(1, 8):
            g = g + q_ref[k]
        g_ref[...] = g

    return pl.pallas_call(
        body, name="small_reduce", grid=(r // tr,), out_shape=jax.ShapeDtypeStruct((r, c), _F32),
        in_specs=[pl.BlockSpec((8, tr, c), lambda i: (0, i, 0))], out_specs=pl.BlockSpec((tr, c), lambda i: (i, 0)),
        compiler_params=_cp("parallel"),
    )(gathered)


def _adamw_many(ws, gs, ms, vs, steps, name):
    n = len(ws)

    def body(*refs):
        w_refs, g_refs, m_refs, v_refs = refs[0:n], refs[n:2 * n], refs[2 * n:3 * n], refs[3 * n:4 * n]
        d_refs, mo_refs, vo_refs = refs[4 * n:5 * n], refs[5 * n:6 * n], refs[6 * n:7 * n]
        for i in range(n):
            d_refs[i][...], mo_refs[i][...], vo_refs[i][...] = _adamw_math(
                w_refs[i][...], g_refs[i][...], m_refs[i][...], v_refs[i][...])

    def spec(a):
        nd = a.ndim
        if steps == 1:
            return pl.BlockSpec(a.shape, lambda i: (0,) * nd)
        return pl.BlockSpec((a.shape[0] // steps,) + a.shape[1:], lambda i: (i,) + (0,) * (nd - 1))

    specs = [spec(w) for w in ws]
    outs = pl.pallas_call(
        body, name=name, grid=(steps,), in_specs=specs * 4, out_specs=specs * 3,
        out_shape=[jax.ShapeDtypeStruct(w.shape, _F32) for w in ws] * 3, compiler_params=_cp("parallel"),
    )(*ws, *gs, *ms, *vs)
    return outs[0:n], outs[n:2 * n], outs[2 * n:3 * n]


def _block_diag(x, eye=None):
    nb, g, p, q = x.shape
    eye = jnp.eye(g, dtype=x.dtype) if eye is None else eye
    return (x[:, :, :, None, :] * eye[None, :, None, :, None]).reshape(nb, g * p, g * q)


def _block_diag_take(x, p, q):
    nb = x.shape[0]
    g = GROUPS_PER_BLOCK
    eye = jnp.eye(g, dtype=x.dtype)
    return jnp.sum(x.reshape(nb, g, p, g, q) * eye[None, :, None, :, None], axis=3)


_VIEWS = {"ssm_b_re": ((0, 2, 1), (0, 2, 1)), "ssm_b_im": ((0, 2, 1), (0, 2, 1)),
          "ssm_w_glu": ((1, 2, 0), (2, 0, 1)), "ssm_b_glu": ((1, 0), (1, 0))}


def _to_view(name, a):
    return a.transpose(_VIEWS[name][0]) if name in _VIEWS else a


def _from_view(name, a):
    return a.transpose(_VIEWS[name][1]) if name in _VIEWS else a


class _Pack:
    def __init__(self, shapes):
        self.shapes = shapes
        self.offsets = {}
        off = 0
        for name, shape in shapes.items():
            n = math.prod(shape)
            self.offsets[name] = (off, n)
            off += -(-n // (SUBLANES * LANES)) * (SUBLANES * LANES)
        self.rows = -(-off // (256 * LANES)) * 256

    def pack(self, arrays):
        parts = []
        off = 0
        for name, shape in self.shapes.items():
            start, n = self.offsets[name]
            if start > off:
                parts.append(jnp.zeros((start - off,), _F32))
            parts.append(arrays[name].reshape(-1).astype(_F32))
            off = start + n
        total = self.rows * LANES
        if total > off:
            parts.append(jnp.zeros((total - off,), _F32))
        return jnp.concatenate(parts).reshape(self.rows, LANES)

    def unpack(self, buf):
        flat = buf.reshape(-1)
        return {name: flat[start:start + n].reshape(self.shapes[name]) for name, (start, n) in self.offsets.items()}


_SMALL = ["b_ada", "g_pre_mix", "g_post_mix", "ssm_log_dt", "ssm_a_re", "ssm_a_im", "ssm_b_re", "ssm_b_im", "ssm_c_re",
          "ssm_c_im", "ssm_d", "ssm_w_glu", "ssm_b_glu", "sgu_ln_g", "sgu_ln_b", "sgu_w", "sgu_b", "g_out_ssm",
          "g_out_sgu", "g_pre_ffn", "g_post_ffn", "conv_b"]
_WEIGHTS = ["w_ada", "b_ada", "g_pre_mix", "g_post_mix", "w_in", "ssm_log_dt", "ssm_a_re", "ssm_a_im", "ssm_b_re",
            "ssm_b_im", "ssm_c_re", "ssm_c_im", "ssm_d", "ssm_w_glu", "ssm_b_glu", "sgu_ln_g", "sgu_ln_b", "sgu_w", "sgu_b",
            "g_out_ssm", "g_out_sgu", "w_out", "g_pre_ffn", "g_post_ffn", "w_up", "conv_w", "conv_b", "w_down"]


def _step(p, m, v, x, c, tgt):
    s, d = x.shape
    mx, my, mc = lax.axis_index("x"), lax.axis_index("y"), lax.axis_index("c")
    chip = 2 * mx + my
    dev = 4 * mx + 2 * my + mc
    sel = jnp.stack([chip, mc]).astype(jnp.int32)
    g_cnt, n_st = p["ssm_a_re"].shape
    nb = g_cnt // GROUPS_PER_BLOCK
    gn = g_cnt * n_st
    d_ssm = g_cnt * SSM_GROUP
    nh = p["sgu_w"].shape[0]
    assert nh * CHUNK == d_ssm and 2 * d_ssm == d and n_st == SSM_STATE

    shards = lambda g: g.reshape(4, g.shape[1] * g.shape[2], g.shape[3])
    buf_in = _cast_into_slot(p["w_in"], sel, sel, "cast_w_in")

    ns_ada = p["w_ada"].shape[1]
    nc_conv = p["conv_w"].shape[1]
    first = jnp.concatenate([jnp.broadcast_to(c, (8, d)), jnp.pad(p["conv_w"], ((0, 5), (0, 0)))], axis=1)
    first_all = _all_gather8(_own_slot(first, dev), "gather_c_conv", after=buf_in)
    (sems_in,), (buf_in,), tok = _gather_start([buf_in], first_all, "gather_start_in")
    c_all = _after(first_all[:, 0, :d], tok)
    conv_w_full = jnp.concatenate([first_all[2 * j, 0:3, d:] for j in range(4)], axis=1)
    b_ada_mine = lax.dynamic_slice_in_dim(p["b_ada"], chip * ns_ada, ns_ada, axis=1)
    mod_mine = _mod_part(c_all, p["w_ada"], b_ada_mine)
    buf_out, buf_up, buf_down = [_cast_into_slot(p[n], sel, tok, "cast_" + n) for n in ("w_out", "w_up", "w_down")]

    eye_t = jnp.eye(GROUPS_PER_BLOCK, dtype=_F32) + tok[0:1, 0:1]
    ldt_l = _after(jnp.repeat(p["ssm_log_dt"], n_st, axis=1), tok)
    are_l, aim_l = p["ssm_a_re"].reshape(1, gn), p["ssm_a_im"].reshape(1, gn)
    bre_t, bim_t = p["ssm_b_re"].reshape(gn, SSM_GROUP).T, p["ssm_b_im"].reshape(gn, SSM_GROUP).T
    nr = _tile(s, TIME_TILE, SUBLANES) // SUBLANES
    kvec = jnp.concatenate([jnp.arange(1, nr + 1, dtype=_F32), jnp.array([nr, 2 * nr, 4 * nr, 0, 0, 0, 0, 0], _F32)])
    pw_re, pw_im, bb_re, bb_im = _ssm_prep(ldt_l, are_l, aim_l, bre_t, bim_t, kvec.reshape(nr + SUBLANES, 1))
    blocks = lambda t: t.reshape(t.shape[0], nb, GROUPS_PER_BLOCK * n_st).transpose(1, 0, 2)
    ptab = jnp.concatenate([blocks(pw_re), blocks(pw_im)], axis=2)
    rev = lambda t: jnp.concatenate([t[:, :nr][:, ::-1], t[:, nr:]], axis=1)
    ptab_rev = jnp.concatenate([rev(blocks(pw_re)), -rev(blocks(pw_im))], axis=2)
    bd = lambda t: t.reshape(SSM_GROUP, nb, GROUPS_PER_BLOCK, n_st).transpose(1, 2, 0, 3)
    bbt = jnp.concatenate([_block_diag(bd(bb_re)), _block_diag(bd(bb_im))], axis=2).astype(_MXU)
    cd = lambda t: t.reshape(nb, GROUPS_PER_BLOCK, SSM_GROUP, n_st).transpose(0, 1, 3, 2)
    ct = jnp.concatenate([_block_diag(cd(p["ssm_c_re"]), eye_t), -_block_diag(cd(p["ssm_c_im"]), eye_t)], axis=1).astype(_MXU)
    wg = _block_diag(p["ssm_w_glu"].reshape(nb, GROUPS_PER_BLOCK, SSM_GROUP, SSM_GROUP), eye_t).astype(_MXU)
    dvec = p["ssm_d"]
    bglu = p["ssm_b_glu"].reshape(1, d_ssm)
    mask = jnp.tril(jnp.ones((CHUNK, CHUNK), _F32)) + tok[0:1, 0:1]
    wm = (p["sgu_w"] * mask[None]).astype(_MXU)
    bs = p["sgu_b"].reshape(nh, CHUNK, 1)

    mod_all = _all_gather8(_own_slot(mod_mine, dev), "gather_mod",
                           after=[buf_out, buf_up, buf_down, bbt, ct, wg, wm, bs, ptab, ptab_rev])
    (sems_out, sems_up), (buf_out, buf_up), tok_rest = _route_start([(buf_out, 1), (buf_up, 1)], mod_all, "route_start_a")
    mod_rows = lax.dynamic_index_in_dim(mod_all, dev, axis=1, keepdims=False)
    mod = jnp.concatenate([mod_rows[0], mod_rows[2], mod_rows[4], mod_rows[6]]).reshape(N_MOD, 1, d)
    sh1, sc1, gt1, sh2, sc2, gt2 = [mod[i] for i in range(N_MOD)]

    h1 = _fwd_pre_mix(x, p["g_pre_mix"], _after(sc1, tok_rest), sh1)
    buf_in = _gather_wait(sems_in, buf_in, h1, "gather_wait_in")
    w_in4 = shards(_pair_forward([buf_in], "pair_forward_in")[0])
    z = _mm_nn(h1, w_in4, _F32, "mm_in")
    y_ssm, y_pre, h_all, hb = _ssm_fwd(z, bbt, ct, dvec, wg, bglu, ptab)
    y_sgu = _sgu_fwd(z, p["sgu_ln_g"], p["sgu_ln_b"], wm, bs)
    buf_out = _route_wait(sems_out, buf_out, 1, y_sgu, "route_wait_out_1")
    buf_up = _route_wait(sems_up, buf_up, 1, y_ssm, "route_wait_up_1")
    (sems_out, sems_up, sems_down), (buf_out, buf_up, buf_down), tok = _route_start(
        [(buf_out, 2), (buf_up, 2), (buf_down, 1)], y_sgu, "route_start_b")
    ycat = _mix_norm_fwd(y_ssm, y_sgu, _after(p["g_out_ssm"], tok), p["g_out_sgu"])
    buf_out = _route_wait(sems_out, buf_out, 2, ycat, "route_wait_out_2")
    w_out_full = _pair_forward([buf_out], "pair_forward_out")[0].reshape(1, d, d)
    o = _mm_nn(ycat, w_out_full, _F32, "mm_out")
    x1, h2 = _fwd_mid(o, x, gt1, p["g_post_mix"], p["g_pre_ffn"], sc2, sh2)
    buf_up = _route_wait(sems_up, buf_up, 2, h2, "route_wait_up_2")
    w_up4 = shards(_pair_forward([buf_up], "pair_forward_up")[0])
    up_pre = _mm_nn(h2, w_up4, _F32, "mm_up")
    buf_down = _route_wait(sems_down, buf_down, 1, up_pre, "route_wait_down_1")
    (sems_down,), (buf_down,), tok = _route_start([(buf_down, 2)], up_pre, "route_start_c")
    act = _conv_act_fwd(up_pre, conv_w_full, _after(p["conv_b"], tok))
    buf_down = _route_wait(sems_down, buf_down, 2, act, "route_wait_down_2")
    w_down_full = _pair_forward([buf_down], "pair_forward_down")[0].reshape(1, -1, d)
    f = _mm_nn(act, w_down_full, _F32, "mm_down", tk=5632)
    dx2, df, d_gt2, d_g_post_ffn, loss = _loss_and_post_ffn_bwd(f, x1, tgt, gt2, p["g_post_ffn"])

    def reduce_next(swap, n, after):
        sems, gw, land, _ = swap
        gw, got = _swap_wait(sems, gw, land, after, "swap_wait_" + n)
        return _scatter_start(_pair_sum(gw, got, sel, "pair_sum_" + n), "scatter_start_" + n)

    d_act = _mm_nt(df, w_down_full, _F32, "mm_d_act", tk=2048)
    swap_down = _swap_start(_mm_tn_rows(act, df, "mm_gw_down"), "swap_start_w_down", 0)
    d_up_pre, d_cw0, d_cw1, d_cw2, d_conv_b = _conv_act_bwd(up_pre, d_act, conv_w_full, _after(p["conv_b"], swap_down[3]))
    red_down = reduce_next(swap_down, "w_down", d_conv_b)
    dh2 = _mm_nt(d_up_pre, w_up4, _F32, "mm_dh2", tk=2816, after=red_down[3])
    swap_up = _swap_start(_mm_tn_cols(h2, d_up_pre, "mm_gw_up"), "swap_start_w_up", 1)
    dx1, d_o, d_sc2, d_sh2, d_g_pre_ffn, d_gt1, d_g_post_mix = _bwd_mid(
        dh2, x1, dx2, o, p["g_pre_ffn"], _after(sc2, swap_up[3]), gt1, p["g_post_mix"])
    red_up = reduce_next(swap_up, "w_up", d_g_post_mix)
    d_ycat = _mm_nt(d_o, w_out_full, _F32, "mm_d_ycat", tn=1024, tk=2048, after=red_up[3])
    swap_out = _swap_start(_mm_tn_rows(ycat, d_o, "mm_gw_out"), "swap_start_w_out", 2)
    dy_ssm, dy_sgu, d_g_out_ssm, d_g_out_sgu = _mix_norm_bwd(
        d_ycat, y_ssm, y_sgu, _after(p["g_out_ssm"], swap_out[3]), p["g_out_sgu"])
    red_out = reduce_next(swap_out, "w_out", d_g_out_sgu)
    dz_ssm, d_bbt, d_ct, d_wg, d_lb, d_ssm_d, d_bglu = _ssm_bwd(z, y_pre, h_all, dy_ssm, hb, bbt, ct,
                                                                _after(dvec, red_out[3]), wg, bglu, ptab_rev)
    dz, d_ln_g, d_ln_b, d_wm, d_bs = _sgu_bwd(z, dy_sgu, dz_ssm, p["sgu_ln_g"], p["sgu_ln_b"], wm, bs)
    dh1 = _mm_nt(dz, w_in4, _F32, "mm_dh1")
    swap_in = _swap_start(_mm_tn_cols(h1, dz, "mm_gw_in"), "swap_start_w_in", 3)
    dx, d_sc1, d_sh1, d_g_pre_mix = _bwd_pre_mix(dh1, x, dx1, p["g_pre_mix"], _after(sc1, swap_in[3]))
    red_in = reduce_next(swap_in, "w_in", d_g_pre_mix)

    nsb = BLOCK_ST
    lanes = lambda t: t.transpose(2, 0, 1, 3).reshape(SSM_GROUP, gn)
    d_bbr = lanes(_block_diag_take(d_bbt[:, :, :nsb], SSM_GROUP, n_st))
    d_bbi = lanes(_block_diag_take(d_bbt[:, :, nsb:], SSM_GROUP, n_st))
    d_lr, d_li = d_lb[:, 0, :nsb].reshape(1, gn), d_lb[:, 0, nsb:].reshape(1, gn)
    d_bre_t, d_bim_t, d_are, d_aim, d_dt = _ssm_prep_bwd(ldt_l, are_l, aim_l, bre_t, bim_t, d_bbr, d_bbi, d_lr, d_li)
    d_log_dt = _group_sum(d_dt.reshape(g_cnt, n_st), p["ssm_log_dt"].reshape(g_cnt, 1))
    c_grad = lambda t: _block_diag_take(t, n_st, SSM_GROUP).transpose(0, 1, 3, 2).reshape(g_cnt, SSM_GROUP, n_st)
    small = {
        "b_ada": jnp.concatenate([d_sh1, _after(d_sc1, red_in[3]), d_gt1, d_sh2, d_sc2, d_gt2], axis=1),
        "g_pre_mix": d_g_pre_mix, "g_post_mix": d_g_post_mix,
        "ssm_log_dt": d_log_dt, "ssm_a_re": d_are, "ssm_a_im": d_aim,
        "ssm_b_re": d_bre_t.T, "ssm_b_im": d_bim_t.T,
        "ssm_c_re": c_grad(d_ct[:, :nsb, :]), "ssm_c_im": -c_grad(d_ct[:, nsb:, :]),
        "ssm_d": d_ssm_d, "ssm_w_glu": _block_diag_take(d_wg, SSM_GROUP, SSM_GROUP), "ssm_b_glu": d_bglu,
        "sgu_ln_g": d_ln_g, "sgu_ln_b": d_ln_b, "sgu_w": d_wm * mask[None], "sgu_b": d_bs,
        "g_out_ssm": d_g_out_ssm, "g_out_sgu": d_g_out_sgu, "g_pre_ffn": d_g_pre_ffn, "g_post_ffn": d_g_post_ffn,
        "conv_b": d_conv_b, "conv_w_all": jnp.concatenate([d_cw0, d_cw1, d_cw2], axis=0),
        "loss_sum": loss,
    }
    small = {n: _to_view(n, a.reshape(p[n].shape)) if n in p else a for n, a in small.items()}
    pk = _Pack({n: a.shape for n, a in small.items()})
    sems_small, small_buf, tok = _gather8_start(_own_slot(pk.pack(small), dev), "gather_small_start")

    big = ["w_down", "w_up", "w_out", "w_in"]
    joins = []
    after = tok
    for i, (n, (sems, pair, land, _)) in enumerate(zip(big, (red_down, red_up, red_out, red_in))):
        pair, land = _scatter_wait(sems, pair, land, after, "scatter_wait_" + n)
        sems_j, half, after = _join_start(_chip_sum(pair, land, sel, "chip_sum_" + n), "join_start_" + n, 4 + i)
        joins.append((sems_j, half))
    big_out = {}
    for n, (sems_j, half) in zip(big, joins):
        j = _join_wait(sems_j, half, after, "join_wait_" + n)
        if n in ("w_in", "w_up"):
            big_out[n] = tuple(_adamw(p[n], j.reshape(p[n].shape), m[n], v[n], "adamw_" + n))
        else:
            big_out[n] = tuple(_adamw_halves(p[n], j, m[n], v[n], "adamw_" + n))
        after = big_out[n][1]

    gathered = _gather8_forward(_gather8_wait(sems_small, small_buf, after, "gather_small_wait"),
                                "gather_small_forward")
    gview = pk.unpack(_small_reduce(gathered))
    gview["conv_w"] = lax.dynamic_slice_in_dim(gview.pop("conv_w_all"), chip * nc_conv, nc_conv, axis=1)
    loss = gview.pop("loss_sum")
    small_names = _SMALL + ["conv_w"]
    per_group = [n for n in small_names if gview[n].ndim >= 2 and gview[n].shape[0] == g_cnt]
    others = [n for n in small_names if n not in per_group]
    grads = {n: _from_view(n, gview[n]) for n in small_names}
    deltas, new_m, new_v = {}, {}, {}
    for names, steps, call in ((per_group, g_cnt // GROUPS_PER_BLOCK, "adamw_s5"), (others, 1, "adamw_small")):
        res = _adamw_many([_to_view(n, p[n]) for n in names], [gview[n] for n in names],
                          [_to_view(n, m[n]) for n in names], [_to_view(n, v[n]) for n in names], steps, call)
        for n, dl, mo, vo in zip(names, *res):
            deltas[n], new_m[n], new_v[n] = _from_view(n, dl), _from_view(n, mo), _from_view(n, vo)

    d_mod_all = gathered.reshape(8, -1)[:, :N_MOD * d]
    d_mod_mine = lax.dynamic_slice_in_dim(d_mod_all, chip * ns_ada, ns_ada, axis=1)
    grads["w_ada"], deltas["w_ada"], new_m["w_ada"], new_v["w_ada"] = _wada_update(
        c_all.T, d_mod_mine, p["w_ada"], m["w_ada"], v["w_ada"])
    for n in big:
        grads[n], deltas[n], new_m[n], new_v[n] = big_out[n]
    return loss[0, 0], dx, grads, deltas, new_m, new_v


def kernel(x, c, w_ada, b_ada, g_pre_mix, g_post_mix, w_in, ssm_log_dt, ssm_a_re, ssm_a_im, ssm_b_re, ssm_b_im, ssm_c_re, ssm_c_im, ssm_d, ssm_w_glu, ssm_b_glu, sgu_ln_g, sgu_ln_b, sgu_w, sgu_b, g_out_ssm, g_out_sgu, w_out, g_pre_ffn, g_post_ffn, w_up, conv_w, conv_b, w_down, loss_target, m_w_ada, m_b_ada, m_g_pre_mix, m_g_post_mix, m_w_in, m_ssm_log_dt, m_ssm_a_re, m_ssm_a_im, m_ssm_b_re, m_ssm_b_im, m_ssm_c_re, m_ssm_c_im, m_ssm_d, m_ssm_w_glu, m_ssm_b_glu, m_sgu_ln_g, m_sgu_ln_b, m_sgu_w, m_sgu_b, m_g_out_ssm, m_g_out_sgu, m_w_out, m_g_pre_ffn, m_g_post_ffn, m_w_up, m_conv_w, m_conv_b, m_w_down, v_w_ada, v_b_ada, v_g_pre_mix, v_g_post_mix, v_w_in, v_ssm_log_dt, v_ssm_a_re, v_ssm_a_im, v_ssm_b_re, v_ssm_b_im, v_ssm_c_re, v_ssm_c_im, v_ssm_d, v_ssm_w_glu, v_ssm_b_glu, v_sgu_ln_g, v_sgu_ln_b, v_sgu_w, v_sgu_b, v_g_out_ssm, v_g_out_sgu, v_w_out, v_g_pre_ffn, v_g_post_ffn, v_w_up, v_conv_w, v_conv_b, v_w_down):
    given = dict(locals())
    drop = lambda a: a if a.ndim == 2 else a[0]
    p = {n: drop(given[n]) for n in _WEIGHTS}
    m = {n: drop(given["m_" + n]) for n in _WEIGHTS}
    v = {n: drop(given["v_" + n]) for n in _WEIGHTS}
    loss, dx, grads, deltas, new_m, new_v = _step(p, m, v, x[0], c, loss_target[0])
    outs = [loss, dx[None]]
    for group in (grads, deltas, new_m, new_v):
        outs += [group[n].reshape(given[n].shape) for n in _WEIGHTS]
    return tuple(outs)
```

```python
import functools
import math

import jax
import jax.numpy as jnp
from jax import lax
from jax.experimental import pallas as pl
from jax.experimental.pallas import tpu as pltpu

_F32 = jnp.float32
_MXU = jnp.bfloat16
_WIRE = jnp.bfloat16

EPS = 1e-6
SSM_GROUP = 16
SSM_STATE = 64
GROUPS_PER_BLOCK = 8
BLOCK_CH = SSM_GROUP * GROUPS_PER_BLOCK
BLOCK_ST = SSM_STATE * GROUPS_PER_BLOCK
CHUNK = 128
TIME_TILE = 512
SUBLANES = 8
LANES = 128
N_MOD = 6
ADAM_LR, ADAM_B1, ADAM_B2, ADAM_EPS, ADAM_WD, ADAM_STEP = 0.001, 0.9, 0.999, 1e-08, 0.01, 10
_VMEM_LIMIT = 56 * 1024 * 1024
_MESH = pl.DeviceIdType.MESH
_ANY = pl.BlockSpec(memory_space=pl.ANY)
_HBM = pl.BlockSpec(memory_space=pltpu.HBM)
_SEM = pl.BlockSpec(memory_space=pltpu.SEMAPHORE)
_VMEM_WHOLE = pl.BlockSpec(memory_space=pltpu.VMEM)
_EFFECT = pltpu.SideEffectType.DATAFLOW_SIDE_EFFECTING
_GELU_C = math.sqrt(2.0 / math.pi)


def _cp(*sem):
    return pltpu.CompilerParams(dimension_semantics=sem, vmem_limit_bytes=_VMEM_LIMIT)


def _tile(dim, target, align=LANES):
    if dim <= target:
        return dim
    best = None
    for t in range(align, target + 1, align):
        if dim % t == 0:
            best = t
    assert best is not None, (dim, target, align)
    return best


def _gelu(x):
    return 0.5 * x * (1.0 + jnp.tanh(_GELU_C * (x + 0.044715 * (x * x * x))))


def _gelu_grad(x):
    t = jnp.tanh(_GELU_C * (x + 0.044715 * (x * x * x)))
    return 0.5 * (1.0 + t) + 0.5 * x * (1.0 - t * t) * (_GELU_C * (1.0 + 3.0 * 0.044715 * x * x))


def _sigmoid(x):
    return 1.0 / (1.0 + jnp.exp(-x))


def _colsum(x):
    return jnp.sum(x, axis=0, keepdims=True)


def _rowmean(x):
    return jnp.mean(x, axis=-1, keepdims=True)


def _zero_first(first, *refs):
    @pl.when(first)
    def _():
        for ref in refs:
            ref[...] = jnp.zeros_like(ref)


def _acc(ref, first, val):
    del first
    ref[...] += val


def _place():
    mx, my, mc = lax.axis_index("x"), lax.axis_index("y"), lax.axis_index("c")
    chips = [(1 - mx, my), (mx, 1 - my), (1 - mx, 1 - my)]
    return mx, my, mc, chips


def _all_gather8(buf, name, after=None):
    extra = [] if after is None else (list(after) if isinstance(after, (list, tuple)) else [after])

    def body(in_ref, *rest):
        out_ref, send_sems, recv_sems = rest[len(extra):]
        mx, my, mc, chips = _place()
        me, sibling = (mx, my, mc), (mx, my, 1 - mc)

        def slot(ref, px, py, pc):
            return ref.at[4 * px + 2 * py + pc]

        def copy(k, block, to, src_ref=out_ref):
            return pltpu.make_async_remote_copy(
                src_ref=slot(src_ref, *block), dst_ref=slot(out_ref, *block),
                send_sem=send_sems.at[k], recv_sem=recv_sems.at[k], device_id=to, device_id_type=_MESH)

        first = [copy(0, me, sibling, in_ref)]
        first += [copy(1 + j, me, (*chip, mc), in_ref) for j, chip in enumerate(chips)]
        for cp in first:
            cp.start()
        passed = [copy(4 + j, (*chip, mc), sibling) for j, chip in enumerate(chips)]
        for j, chip in enumerate(chips):
            copy(1 + j, (*chip, mc), me).wait_recv()
            passed[j].start()
        copy(0, sibling, me).wait_recv()
        for j, chip in enumerate(chips):
            copy(4 + j, (*chip, 1 - mc), me).wait_recv()
        for cp in first + passed:
            cp.wait_send()

    return pl.pallas_call(
        body, name=name, out_shape=jax.ShapeDtypeStruct(buf.shape, buf.dtype),
        in_specs=[_ANY] * (1 + len(extra)), out_specs=_ANY, input_output_aliases={0: 0},
        scratch_shapes=[pltpu.SemaphoreType.DMA((7,)), pltpu.SemaphoreType.DMA((7,))],
    )(buf, *extra)


def _own_slot(x, dev):
    return lax.dynamic_update_slice(jnp.zeros((8,) + x.shape, x.dtype), x[None], (dev, 0, 0))


def _cast_into_slot(w, sel, after, name):
    r, c = w.shape
    hr = r // 2
    tr = _tile(hr, 256, 16)
    nr = hr // tr

    def body(sel_ref, w_ref, after_ref, o_ref):
        o_ref[...] = w_ref[...].astype(o_ref.dtype)

    return pl.pallas_call(
        body, name=name, out_shape=jax.ShapeDtypeStruct((4, 2, hr, c), _WIRE),
        grid_spec=pltpu.PrefetchScalarGridSpec(
            num_scalar_prefetch=1, grid=(2, nr),
            in_specs=[pl.BlockSpec((tr, c), lambda h, i, s: (h * nr + i, 0)), _ANY],
            out_specs=pl.BlockSpec((None, None, tr, c), lambda h, i, s: (s[0], h, i, 0))),
        compiler_params=_cp("parallel", "parallel"),
    )(sel, w, after)


def _hbm(a):
    return pltpu.with_memory_space_constraint(a, pltpu.HBM)


def _after(vec, token):
    return vec + token[0:1, 0:1]


def _gather_start(bufs, after, name):
    n = len(bufs)
    nc = 3 * n

    def body(*refs):
        ins, send, recv, token = refs[:n], refs[n + 1:n + 1 + nc], refs[n + 1 + nc:n + 1 + 2 * nc], refs[-1]
        mx, my, mc, chips = _place()
        j_me = 2 * mx + my
        for i in range(n):
            for k, chip in enumerate(chips):
                half = ins[i].at[j_me, mc]
                pltpu.make_async_remote_copy(
                    src_ref=half, dst_ref=half, send_sem=send[3 * i + k], recv_sem=recv[3 * i + k],
                    device_id=(*chip, mc), device_id_type=_MESH).start()
        token[...] = jnp.zeros_like(token)

    outs = pl.pallas_call(
        body, name=name,
        out_shape=tuple([pltpu.SemaphoreType.DMA(())] * (2 * nc) + [pltpu.HBM(b.shape, b.dtype) for b in bufs]
                        + [jax.ShapeDtypeStruct((SUBLANES, LANES), _F32)]),
        in_specs=tuple([_HBM] * n + [_ANY]), out_specs=tuple([_SEM] * (2 * nc) + [_HBM] * n + [_VMEM_WHOLE]),
        input_output_aliases={i: 2 * nc + i for i in range(n)},
        compiler_params=pltpu.CompilerParams(has_side_effects=_EFFECT),
    )(*[_hbm(b) for b in bufs], after)
    sems = [(outs[3 * i:3 * i + 3], outs[nc + 3 * i:nc + 3 * i + 3]) for i in range(n)]
    return sems, list(outs[2 * nc:2 * nc + n]), outs[-1]


def _gather_wait(sems, buf, after, name):
    send, recv = sems

    after = list(after) if isinstance(after, (list, tuple)) else [after]

    def body(buf_ref, s0, s1, s2, r0, r1, r2, *rest):
        mx, my, mc, chips = _place()
        j_me = 2 * mx + my
        for k, (chip, s_k, r_k) in enumerate(zip(chips, (s0, s1, s2), (r0, r1, r2))):
            cp = pltpu.make_async_remote_copy(
                src_ref=buf_ref.at[j_me, mc], dst_ref=buf_ref.at[2 * chip[0] + chip[1], mc], send_sem=s_k, recv_sem=r_k,
                device_id=(*chip, mc), device_id_type=_MESH)
            cp.wait_send()
            cp.wait_recv()

    return pl.pallas_call(
        body, name=name, out_shape=pltpu.HBM(buf.shape, buf.dtype),
        in_specs=(_HBM,) + (_SEM,) * 6 + (_ANY,) * len(after), out_specs=_HBM, input_output_aliases={0: 0},
        compiler_params=pltpu.CompilerParams(has_side_effects=_EFFECT),
    )(buf, *send, *recv, *after)


def _route_ends(buf_ref, phase):
    mx, my, mc, _ = _place()
    hq = buf_ref.shape[2] // 2
    xn, yn = (1 - mx, my), (mx, 1 - my)
    j_me, j_x, j_y, j_d = 2 * mx + my, 2 * (1 - mx) + my, 2 * mx + (1 - my), 2 * (1 - mx) + (1 - my)
    if phase == 1:
        mine = buf_ref.at[j_me, mc]
        return [((*xn, mc), mine, buf_ref.at[j_x, mc]), ((*yn, mc), mine, buf_ref.at[j_y, mc])]
    lo, hi = pl.ds(0, hq), pl.ds(hq, hq)
    return [((*xn, mc), buf_ref.at[j_y, mc, lo], buf_ref.at[j_d, mc, lo]),
            ((*yn, mc), buf_ref.at[j_x, mc, hi], buf_ref.at[j_d, mc, hi])]


def _handshake(peers):
    barrier = pltpu.get_barrier_semaphore()
    for peer in peers:
        pl.semaphore_signal(barrier, inc=1, device_id=peer, device_id_type=_MESH)
    pl.semaphore_wait(barrier, len(peers))


def _route_start(items, after, name, cid):
    n = len(items)

    def body(*refs):
        ins, send, recv, token = refs[:n], refs[n + 1:3 * n + 1], refs[3 * n + 1:5 * n + 1], refs[-1]
        mx, my, mc, _ = _place()
        _handshake([(1 - mx, my, mc), (mx, 1 - my, mc)])
        for i, (_, phase) in enumerate(items):
            for k, (peer, src, _) in enumerate(_route_ends(ins[i], phase)):
                pltpu.make_async_remote_copy(src_ref=src, dst_ref=src, send_sem=send[2 * i + k], recv_sem=recv[2 * i + k],
                                             device_id=peer, device_id_type=_MESH).start()
        token[...] = jnp.zeros_like(token)

    bufs = [b for b, _ in items]
    outs = pl.pallas_call(
        body, name=name,
        out_shape=tuple([pltpu.SemaphoreType.DMA(())] * (4 * n) + [pltpu.HBM(b.shape, b.dtype) for b in bufs]
                        + [jax.ShapeDtypeStruct((SUBLANES, LANES), _F32)]),
        in_specs=tuple([_HBM] * n + [_ANY]), out_specs=tuple([_SEM] * (4 * n) + [_HBM] * n + [_VMEM_WHOLE]),
        input_output_aliases={i: 4 * n + i for i in range(n)},
        compiler_params=pltpu.CompilerParams(has_side_effects=_EFFECT, collective_id=cid),
    )(*[_hbm(b) for b in bufs], after)
    sems = [(outs[2 * i:2 * i + 2], outs[2 * n + 2 * i:2 * n + 2 * i + 2]) for i in range(n)]
    return sems, list(outs[4 * n:5 * n]), outs[-1]


def _route_wait(sems, buf, phase, after, name):
    send, recv = sems

    def body(buf_ref, s0, s1, r0, r1, after_ref, out_ref):
        for (peer, src, land), s_k, r_k in zip(_route_ends(buf_ref, phase), (s0, s1), (r0, r1)):
            cp = pltpu.make_async_remote_copy(src_ref=src, dst_ref=land, send_sem=s_k, recv_sem=r_k,
                                              device_id=peer, device_id_type=_MESH)
            cp.wait_send()
            cp.wait_recv()

    return pl.pallas_call(
        body, name=name, out_shape=pltpu.HBM(buf.shape, buf.dtype),
        in_specs=(_HBM,) + (_SEM,) * 4 + (_ANY,), out_specs=_HBM, input_output_aliases={0: 0},
        compiler_params=pltpu.CompilerParams(has_side_effects=_EFFECT),
    )(buf, *send, *recv, after)


def _pair_forward(bufs, name):
    n = len(bufs)

    def body(*refs):
        ins, outs = refs[:n], refs[n:2 * n]
        send_sems, recv_sems = refs[2 * n:]
        mx, my, mc, chips = _place()
        sibling = (mx, my, 1 - mc)
        cps = []
        for i in range(n):
            for k, chip in enumerate(chips):
                j_k = 2 * chip[0] + chip[1]
                cp = pltpu.make_async_remote_copy(
                    src_ref=ins[i].at[j_k, mc], dst_ref=outs[i].at[j_k, mc], send_sem=send_sems.at[3 * i + k],
                    recv_sem=recv_sems.at[3 * i + k], device_id=sibling, device_id_type=_MESH)
                cp.start()
                cps.append(cp)
        for i in range(n):
            for k, chip in enumerate(chips):
                other = outs[i].at[2 * chip[0] + chip[1], 1 - mc]
                pltpu.make_async_remote_copy(
                    src_ref=other, dst_ref=other, send_sem=send_sems.at[3 * i + k], recv_sem=recv_sems.at[3 * i + k],
                    device_id=sibling, device_id_type=_MESH).wait_recv()
        for cp in cps:
            cp.wait_send()

    return pl.pallas_call(
        body, name=name, out_shape=[jax.ShapeDtypeStruct(b.shape, b.dtype) for b in bufs],
        in_specs=[_ANY] * n, out_specs=[_ANY] * n, input_output_aliases={i: i for i in range(n)},
        scratch_shapes=[pltpu.SemaphoreType.DMA((3 * n,)), pltpu.SemaphoreType.DMA((3 * n,))],
    )(*bufs)


def _gather8_peers(buf_ref, mx, my, mc, chips):
    mine = buf_ref.at[4 * mx + 2 * my + mc]
    peers = [((mx, my, 1 - mc), mine, buf_ref.at[4 * mx + 2 * my + 1 - mc])]
    peers += [((*chip, mc), mine, buf_ref.at[4 * chip[0] + 2 * chip[1] + mc]) for chip in chips]
    return peers


def _gather8_start(buf, name):
    def body(buf_ref, *rest):
        send, recv, token = rest[0:4], rest[4:8], rest[-1]
        mx, my, mc, chips = _place()
        for k, (peer, src, _) in enumerate(_gather8_peers(buf_ref, mx, my, mc, chips)):
            pltpu.make_async_remote_copy(src_ref=src, dst_ref=src, send_sem=send[k], recv_sem=recv[k],
                                         device_id=peer, device_id_type=_MESH).start()
        token[...] = jnp.zeros_like(token)

    outs = pl.pallas_call(
        body, name=name,
        out_shape=tuple([pltpu.SemaphoreType.DMA(())] * 8 + [pltpu.HBM(buf.shape, buf.dtype),
                                                             jax.ShapeDtypeStruct((SUBLANES, LANES), _F32)]),
        in_specs=(_HBM,), out_specs=tuple([_SEM] * 8 + [_HBM, _VMEM_WHOLE]), input_output_aliases={0: 8},
        compiler_params=pltpu.CompilerParams(has_side_effects=_EFFECT),
    )(_hbm(buf))
    return (outs[0:4], outs[4:8]), outs[8], outs[9]


def _gather8_wait(sems, buf, after, name):
    send, recv = sems

    def body(buf_ref, s0, s1, s2, s3, r0, r1, r2, r3, after_ref, out_ref):
        mx, my, mc, chips = _place()
        for (peer, src, dst), s_k, r_k in zip(_gather8_peers(buf_ref, mx, my, mc, chips), (s0, s1, s2, s3), (r0, r1, r2, r3)):
            cp = pltpu.make_async_remote_copy(src_ref=src, dst_ref=dst, send_sem=s_k, recv_sem=r_k,
                                              device_id=peer, device_id_type=_MESH)
            cp.wait_send()
            cp.wait_recv()

    return pl.pallas_call(
        body, name=name, out_shape=pltpu.HBM(buf.shape, buf.dtype),
        in_specs=(_HBM,) + (_SEM,) * 8 + (_ANY,), out_specs=_HBM, input_output_aliases={0: 0},
        compiler_params=pltpu.CompilerParams(has_side_effects=_EFFECT),
    )(buf, *send, *recv, after)


def _gather8_forward(buf, name):
    def body(in_ref, out_ref, send_sems, recv_sems):
        mx, my, mc, chips = _place()
        sibling = (mx, my, 1 - mc)
        cps = []
        for k, chip in enumerate(chips):
            idx = 4 * chip[0] + 2 * chip[1] + mc
            cp = pltpu.make_async_remote_copy(src_ref=in_ref.at[idx], dst_ref=out_ref.at[idx], send_sem=send_sems.at[k],
                                              recv_sem=recv_sems.at[k], device_id=sibling, device_id_type=_MESH)
            cp.start()
            cps.append(cp)
        for k, chip in enumerate(chips):
            other = out_ref.at[4 * chip[0] + 2 * chip[1] + 1 - mc]
            pltpu.make_async_remote_copy(src_ref=other, dst_ref=other, send_sem=send_sems.at[k], recv_sem=recv_sems.at[k],
                                         device_id=sibling, device_id_type=_MESH).wait_recv()
        for cp in cps:
            cp.wait_send()

    return pl.pallas_call(
        body, name=name, out_shape=jax.ShapeDtypeStruct(buf.shape, buf.dtype),
        in_specs=[_ANY], out_specs=_ANY, input_output_aliases={0: 0},
        scratch_shapes=[pltpu.SemaphoreType.DMA((3,)), pltpu.SemaphoreType.DMA((3,))],
    )(buf)


def _scatter_start(pair, name, cid):
    land = lax.empty((3,) + pair.shape[1:], pair.dtype)

    def body(pair_ref, land_ref, s0, s1, s2, r0, r1, r2, pair_thru, land_thru, token):
        mx, my, mc, chips = _place()
        _handshake([(*chip, mc) for chip in chips])
        for k, (chip, s_k, r_k) in enumerate(zip(chips, (s0, s1, s2), (r0, r1, r2))):
            pltpu.make_async_remote_copy(
                src_ref=pair_ref.at[2 * chip[0] + chip[1]], dst_ref=land_ref.at[k], send_sem=s_k, recv_sem=r_k,
                device_id=(*chip, mc), device_id_type=_MESH).start()
        token[...] = jnp.zeros_like(token)

    outs = pl.pallas_call(
        body, name=name,
        out_shape=tuple([pltpu.SemaphoreType.DMA(())] * 6 + [pltpu.HBM(pair.shape, pair.dtype), pltpu.HBM(land.shape, land.dtype),
                                                             jax.ShapeDtypeStruct((SUBLANES, LANES), _F32)]),
        in_specs=(_HBM, _HBM), out_specs=tuple([_SEM] * 6 + [_HBM, _HBM, _VMEM_WHOLE]),
        input_output_aliases={0: 6, 1: 7}, compiler_params=pltpu.CompilerParams(has_side_effects=_EFFECT, collective_id=cid),
    )(_hbm(pair), _hbm(land))
    return (outs[0:3], outs[3:6]), outs[6], outs[7], outs[8]


def _scatter_wait(sems, pair, land, after, name):
    send, recv = sems

    def body(pair_ref, land_ref, s0, s1, s2, r0, r1, r2, after_ref, pair_out, land_out):
        mx, my, mc, chips = _place()
        for k, (chip, s_k, r_k) in enumerate(zip(chips, (s0, s1, s2), (r0, r1, r2))):
            cp = pltpu.make_async_remote_copy(
                src_ref=pair_ref.at[2 * chip[0] + chip[1]], dst_ref=land_ref.at[k], send_sem=s_k, recv_sem=r_k,
                device_id=(*chip, mc), device_id_type=_MESH)
            cp.wait_send()
            cp.wait_recv()

    return pl.pallas_call(
        body, name=name, out_shape=(pltpu.HBM(pair.shape, pair.dtype), pltpu.HBM(land.shape, land.dtype)),
        in_specs=(_HBM, _HBM) + (_SEM,) * 6 + (_ANY,), out_specs=(_HBM, _HBM), input_output_aliases={0: 0, 1: 1},
        compiler_params=pltpu.CompilerParams(has_side_effects=_EFFECT),
    )(pair, land, *send, *recv, after)


def _sibling_copy(src_ref, dst_ref, send_sem, recv_sem):
    mx, my, mc, _ = _place()
    return pltpu.make_async_remote_copy(src_ref=src_ref, dst_ref=dst_ref, send_sem=send_sem, recv_sem=recv_sem,
                                        device_id=(mx, my, 1 - mc), device_id_type=_MESH)


def _sibling_handshake():
    mx, my, mc, _ = _place()
    barrier = pltpu.get_barrier_semaphore()
    pl.semaphore_signal(barrier, inc=1, device_id=(mx, my, 1 - mc), device_id_type=_MESH)
    pl.semaphore_wait(barrier, 1)


def _swap_start(g, name, cid):
    land = lax.empty(g.shape[1:], g.dtype)

    def body(g_ref, land_ref, send_sem, recv_sem, g_thru, land_thru, token):
        _sibling_handshake()
        _sibling_copy(g_ref.at[1 - lax.axis_index("c")], land_ref, send_sem, recv_sem).start()
        token[...] = jnp.zeros_like(token)

    outs = pl.pallas_call(
        body, name=name,
        out_shape=(pltpu.SemaphoreType.DMA(()), pltpu.SemaphoreType.DMA(()), pltpu.HBM(g.shape, g.dtype),
                   pltpu.HBM(land.shape, land.dtype), jax.ShapeDtypeStruct((SUBLANES, LANES), _F32)),
        in_specs=(_HBM, _HBM), out_specs=(_SEM, _SEM, _HBM, _HBM, _VMEM_WHOLE), input_output_aliases={0: 2, 1: 3},
        compiler_params=pltpu.CompilerParams(has_side_effects=_EFFECT, collective_id=cid),
    )(_hbm(g), _hbm(land))
    return (outs[0], outs[1]), outs[2], outs[3], outs[4]


def _swap_wait(sems, g, land, after, name):
    def body(g_ref, land_ref, send_sem, recv_sem, after_ref, g_out, land_out):
        cp = _sibling_copy(g_ref.at[1 - lax.axis_index("c")], land_ref, send_sem, recv_sem)
        cp.wait_send()
        cp.wait_recv()

    return pl.pallas_call(
        body, name=name, out_shape=(pltpu.HBM(g.shape, g.dtype), pltpu.HBM(land.shape, land.dtype)),
        in_specs=(_HBM, _HBM, _SEM, _SEM, _ANY), out_specs=(_HBM, _HBM), input_output_aliases={0: 0, 1: 1},
        compiler_params=pltpu.CompilerParams(has_side_effects=_EFFECT),
    )(g, land, *sems, after)


def _join_start(buf, name, cid):
    def body(buf_ref, send_sem, recv_sem, buf_thru, token):
        _sibling_handshake()
        mine = buf_ref.at[lax.axis_index("c")]
        _sibling_copy(mine, mine, send_sem, recv_sem).start()
        token[...] = jnp.zeros_like(token)

    outs = pl.pallas_call(
        body, name=name,
        out_shape=(pltpu.SemaphoreType.DMA(()), pltpu.SemaphoreType.DMA(()), pltpu.HBM(buf.shape, buf.dtype),
                   jax.ShapeDtypeStruct((SUBLANES, LANES), _F32)),
        in_specs=(_HBM,), out_specs=(_SEM, _SEM, _HBM, _VMEM_WHOLE), input_output_aliases={0: 2},
        compiler_params=pltpu.CompilerParams(has_side_effects=_EFFECT, collective_id=cid),
    )(_hbm(buf))
    return (outs[0], outs[1]), outs[2], outs[3]


def _join_wait(sems, buf, after, name):
    def body(buf_ref, send_sem, recv_sem, after_ref, buf_out):
        mc = lax.axis_index("c")
        cp = _sibling_copy(buf_ref.at[mc], buf_ref.at[1 - mc], send_sem, recv_sem)
        cp.wait_send()
        cp.wait_recv()

    return pl.pallas_call(
        body, name=name, out_shape=pltpu.HBM(buf.shape, buf.dtype),
        in_specs=(_HBM, _SEM, _SEM, _ANY), out_specs=_HBM, input_output_aliases={0: 0},
        compiler_params=pltpu.CompilerParams(has_side_effects=_EFFECT),
    )(buf, *sems, after)


def _pair_sum(g, got, sel, name):
    _, four, hr, c = g.shape
    tr = _tile(hr, 512, 16)

    def body(sel_ref, g_ref, p_ref, o_ref):
        o_ref[...] = (g_ref[...].astype(_F32) + p_ref[...].astype(_F32)).astype(o_ref.dtype)

    return pl.pallas_call(
        body, name=name, out_shape=jax.ShapeDtypeStruct((four, hr, c), g.dtype),
        grid_spec=pltpu.PrefetchScalarGridSpec(
            num_scalar_prefetch=1, grid=(four, hr // tr),
            in_specs=[pl.BlockSpec((None, None, tr, c), lambda j, i, s: (s[1], j, i, 0)),
                      pl.BlockSpec((None, tr, c), lambda j, i, s: (j, i, 0))],
            out_specs=pl.BlockSpec((None, tr, c), lambda j, i, s: (j, i, 0))),
        compiler_params=_cp("parallel", "parallel"),
    )(sel, g, got)


def _chip_sum(pair, got, sel, name):
    _, hr, c = pair.shape
    tr = _tile(hr, 512, 16)

    def body(sel_ref, p_ref, q_ref, o_ref):
        o_ref[...] = ((p_ref[...].astype(_F32) + q_ref[0].astype(_F32)) + q_ref[1].astype(_F32)) + q_ref[2].astype(_F32)

    return pl.pallas_call(
        body, name=name, out_shape=jax.ShapeDtypeStruct((2, hr, c), _F32),
        grid_spec=pltpu.PrefetchScalarGridSpec(
            num_scalar_prefetch=1, grid=(hr // tr,),
            in_specs=[pl.BlockSpec((None, tr, c), lambda i, s: (s[0], i, 0)),
                      pl.BlockSpec((3, tr, c), lambda i, s: (0, i, 0))],
            out_specs=pl.BlockSpec((None, tr, c), lambda i, s: (s[1], i, 0))),
        compiler_params=_cp("parallel"),
    )(sel, pair, got)


def _matmul(a, b, dims, out_struct, grid, a_spec, b_spec, o_spec, acc_shape, k_axis, name, after=None):
    nk = grid[k_axis]
    extra = [] if after is None else [after]

    def body(a_ref, b_ref, *rest):
        o_ref, acc = rest[len(extra)], rest[len(extra) + 1:]
        prod = lax.dot_general(a_ref[...].astype(_MXU), b_ref[...].astype(_MXU), dims, preferred_element_type=_F32)
        if nk == 1:
            o_ref[...] = prod.astype(o_ref.dtype)
        else:
            acc_ref, = acc
            k = pl.program_id(k_axis)
            _zero_first(k == 0, acc_ref)
            acc_ref[...] += prod

            @pl.when(k == nk - 1)
            def _():
                o_ref[...] = acc_ref[...].astype(o_ref.dtype)

    sem = ["parallel"] * len(grid)
    sem[k_axis] = "arbitrary"
    return pl.pallas_call(
        body, name=name, out_shape=out_struct, grid=grid, in_specs=[a_spec, b_spec] + [_ANY] * len(extra), out_specs=o_spec,
        scratch_shapes=[pltpu.VMEM(acc_shape, _F32)] if nk > 1 else [], compiler_params=_cp(*sem),
    )(a, b, *extra)


def _mm_nn(a, w4, out_dtype, name, tm=512, tn=1536, tk=2048, after=None):
    m, k = a.shape
    j, _, ns = w4.shape
    tm, tn, tk = _tile(m, tm, 16), _tile(ns, tn), _tile(k, tk)
    nps = ns // tn
    return _matmul(
        a, w4, (((1,), (0,)), ((), ())), jax.ShapeDtypeStruct((m, j * ns), out_dtype),
        (j * nps, m // tm, k // tk),
        pl.BlockSpec((tm, tk), lambda ni, mi, ki: (mi, ki)),
        pl.BlockSpec((None, tk, tn), lambda ni, mi, ki: (ni // nps, ki, ni % nps)),
        pl.BlockSpec((tm, tn), lambda ni, mi, ki: (mi, ni)), (tm, tn), 2, name, after)


def _mm_nt(a, w4, out_dtype, name, tm=512, tn=2048, tk=1536, after=None):
    m = a.shape[-2]
    j, kw, ns = w4.shape
    tm, tn, tk = _tile(m, tm, 16), _tile(kw, tn), _tile(ns, tk)
    kps = ns // tk
    if a.ndim == 3:
        kph = a.shape[2] // tk
        a_spec = pl.BlockSpec((None, tm, tk), lambda ni, mi, ki: (ki // kph, mi, ki % kph))
    else:
        a_spec = pl.BlockSpec((tm, tk), lambda ni, mi, ki: (mi, ki))
    return _matmul(
        a, w4, (((1,), (1,)), ((), ())), jax.ShapeDtypeStruct((m, kw), out_dtype),
        (kw // tn, m // tm, j * kps),
        a_spec,
        pl.BlockSpec((None, tn, tk), lambda ni, mi, ki: (ki // kps, ni, ki % kps)),
        pl.BlockSpec((tm, tn), lambda ni, mi, ki: (mi, ni)), (tm, tn), 2, name, after)


def _mm_tn_cols(a, b, name, tm=1024, tn=1536, tk=2048):
    m, ka = a.shape
    ns = (b.shape[-1] * (2 if b.ndim == 3 else 1)) // 4
    hr = ka // 2
    tm, tn, tk = _tile(hr, tm), _tile(ns, tn), _tile(m, tk, 16)
    mph, nps = hr // tm, ns // tn
    if b.ndim == 3:
        b_spec = pl.BlockSpec((None, tk, tn), lambda ni, mi, ki: (ni // (2 * nps), ki, ni % (2 * nps)))
    else:
        b_spec = pl.BlockSpec((tk, tn), lambda ni, mi, ki: (ki, ni))
    return _matmul(
        a, b, (((0,), (0,)), ((), ())), jax.ShapeDtypeStruct((2, 4, hr, ns), _WIRE),
        (4 * nps, 2 * mph, m // tk),
        pl.BlockSpec((tk, tm), lambda ni, mi, ki: (ki, mi)),
        b_spec,
        pl.BlockSpec((None, None, tm, tn), lambda ni, mi, ki: (mi // mph, ni // nps, mi % mph, ni % nps)),
        (tm, tn), 2, name)


def _mm_tn_rows(a, b, name, tm=1536, tn=1024, tk=2048):
    m, ka = a.shape
    r = ka // 4
    hc = b.shape[1] // 2
    tm, tn, tk = _tile(r, tm), _tile(hc, tn), _tile(m, tk, 16)
    mpr, nph = r // tm, hc // tn
    return _matmul(
        a, b, (((0,), (0,)), ((), ())), jax.ShapeDtypeStruct((2, 4, r, hc), _WIRE),
        (2 * nph, 4 * mpr, m // tk),
        pl.BlockSpec((tk, tm), lambda ni, mi, ki: (ki, mi)),
        pl.BlockSpec((tk, tn), lambda ni, mi, ki: (ki, ni)),
        pl.BlockSpec((None, None, tm, tn), lambda ni, mi, ki: (ni // nph, mi // mpr, mi % mpr, ni % nph)),
        (tm, tn), 2, name)


def _row_call(body, name, rows, ins, outs, tm=256):
    tm = _tile(rows, tm, 16)

    def spec(shape, kind):
        if kind == "rows":
            return pl.BlockSpec((tm, shape[1]), lambda i: (i, 0))
        return pl.BlockSpec(shape, lambda i: (0,) * len(shape))

    return pl.pallas_call(
        body, name=name, grid=(rows // tm,),
        in_specs=[spec(a.shape, kind) for a, kind in ins],
        out_specs=[spec(o.shape, kind) for o, kind in outs],
        out_shape=[o for o, _ in outs],
        compiler_params=_cp("arbitrary"),
    )(*[a for a, _ in ins])


def _rms(x):
    r = lax.rsqrt(_rowmean(x * x) + EPS)
    return x * r, r


def _rms_bwd(dxh, xh, r):
    return r * (dxh - xh * _rowmean(dxh * xh))


def _fwd_pre_mix(x, g, sc, sh):
    s, d = x.shape

    def body(x_ref, g_ref, sc_ref, sh_ref, h_ref):
        xh, _ = _rms(x_ref[...])
        h_ref[...] = (xh * g_ref[...] * (1.0 + sc_ref[...]) + sh_ref[...]).astype(h_ref.dtype)

    return _row_call(body, "fwd_pre_mix", s, [(x, "rows"), (g, "vec"), (sc, "vec"), (sh, "vec")],
                     [(jax.ShapeDtypeStruct((s, d), _MXU), "rows")])[0]


def _fwd_mid(o, x, gt1, g_post, g_pre2, sc2, sh2):
    s, d = x.shape

    def body(o_ref, x_ref, gt_ref, gp_ref, g2_ref, sc_ref, sh_ref, x1_ref, h2_ref):
        oh, _ = _rms(o_ref[...])
        x1 = x_ref[...] + gt_ref[...] * (oh * gp_ref[...])
        x1_ref[...] = x1
        xh, _ = _rms(x1)
        h2_ref[...] = (xh * g2_ref[...] * (1.0 + sc_ref[...]) + sh_ref[...]).astype(h2_ref.dtype)

    return _row_call(body, "fwd_mid", s,
                     [(o, "rows"), (x, "rows"), (gt1, "vec"), (g_post, "vec"), (g_pre2, "vec"), (sc2, "vec"),
                      (sh2, "vec")],
                     [(jax.ShapeDtypeStruct((s, d), _F32), "rows"), (jax.ShapeDtypeStruct((s, d), _MXU), "rows")])


def _loss_and_post_ffn_bwd(f, x1, tgt, gt2, g_post):
    s, d = x1.shape

    def body(f_ref, x1_ref, t_ref, gt_ref, g_ref, dx2_ref, df_ref, dgt_ref, dg_ref, loss_ref):
        first = pl.program_id(0) == 0
        _zero_first(first, dgt_ref, dg_ref, loss_ref)
        fh, r = _rms(f_ref[...])
        n = fh * g_ref[...]
        e = x1_ref[...] + gt_ref[...] * n - t_ref[...]
        _acc(loss_ref, first, jnp.sum(_colsum(e * e), axis=1, keepdims=True) * (0.5 / d))
        dx2 = e * (1.0 / d)
        dx2_ref[...] = dx2
        _acc(dgt_ref, first, _colsum(dx2 * n))
        dn = dx2 * gt_ref[...]
        _acc(dg_ref, first, _colsum(dn * fh))
        df_ref[...] = _rms_bwd(dn * g_ref[...], fh, r).astype(df_ref.dtype)

    vec = jax.ShapeDtypeStruct((1, d), _F32)
    return _row_call(body, "loss_post_ffn_bwd", s,
                     [(f, "rows"), (x1, "rows"), (tgt, "rows"), (gt2, "vec"), (g_post, "vec")],
                     [(jax.ShapeDtypeStruct((s, d), _F32), "rows"), (jax.ShapeDtypeStruct((s, d), _MXU), "rows"),
                      (vec, "vec"), (vec, "vec"), (jax.ShapeDtypeStruct((1, 1), _F32), "vec")])


def _bwd_mid(dh2, x1, dx2, o, g_pre2, sc2, gt1, g_post):
    s, d = x1.shape

    def body(dh_ref, x1_ref, dx2_ref, o_ref, g2_ref, sc_ref, gt_ref, gp_ref,
             dx1_ref, do_ref, dsc_ref, dsh_ref, dg2_ref, dgt_ref, dgp_ref):
        first = pl.program_id(0) == 0
        _zero_first(first, dsc_ref, dsh_ref, dg2_ref, dgt_ref, dgp_ref)
        dh = dh_ref[...]
        xh, r = _rms(x1_ref[...])
        _acc(dsh_ref, first, _colsum(dh))
        _acc(dsc_ref, first, _colsum(dh * (xh * g2_ref[...])))
        dn = dh * (1.0 + sc_ref[...])
        _acc(dg2_ref, first, _colsum(dn * xh))
        dx1 = dx2_ref[...] + _rms_bwd(dn * g2_ref[...], xh, r)
        dx1_ref[...] = dx1
        oh, ro = _rms(o_ref[...])
        _acc(dgt_ref, first, _colsum(dx1 * (oh * gp_ref[...])))
        dno = dx1 * gt_ref[...]
        _acc(dgp_ref, first, _colsum(dno * oh))
        do_ref[...] = _rms_bwd(dno * gp_ref[...], oh, ro).astype(do_ref.dtype)

    vec = jax.ShapeDtypeStruct((1, d), _F32)
    return _row_call(body, "bwd_mid", s,
                     [(dh2, "rows"), (x1, "rows"), (dx2, "rows"), (o, "rows"), (g_pre2, "vec"), (sc2, "vec"),
                      (gt1, "vec"), (g_post, "vec")],
                     [(jax.ShapeDtypeStruct((s, d), _F32), "rows"), (jax.ShapeDtypeStruct((s, d), _MXU), "rows"),
                      (vec, "vec"), (vec, "vec"), (vec, "vec"), (vec, "vec"), (vec, "vec")])


def _bwd_pre_mix(dh1, x, dx1, g, sc1):
    s, d = x.shape

    def body(dh_ref, x_ref, dx1_ref, g_ref, sc_ref, dx_ref, dsc_ref, dsh_ref, dg_ref):
        first = pl.program_id(0) == 0
        _zero_first(first, dsc_ref, dsh_ref, dg_ref)
        dh = dh_ref[...]
        xh, r = _rms(x_ref[...])
        _acc(dsh_ref, first, _colsum(dh))
        _acc(dsc_ref, first, _colsum(dh * (xh * g_ref[...])))
        dn = dh * (1.0 + sc_ref[...])
        _acc(dg_ref, first, _colsum(dn * xh))
        dx_ref[...] = dx1_ref[...] + _rms_bwd(dn * g_ref[...], xh, r)

    vec = jax.ShapeDtypeStruct((1, d), _F32)
    return _row_call(body, "bwd_pre_mix", s,
                     [(dh1, "rows"), (x, "rows"), (dx1, "rows"), (g, "vec"), (sc1, "vec")],
                     [(jax.ShapeDtypeStruct((s, d), _F32), "rows"), (vec, "vec"), (vec, "vec"), (vec, "vec")])


def _mix_norm_fwd(y_ssm, y_sgu, g_ssm, g_sgu):
    s, h = y_ssm.shape

    def body(a_ref, b_ref, ga_ref, gb_ref, o_ref):
        ah, _ = _rms(a_ref[...])
        bh, _ = _rms(b_ref[...])
        o_ref[:, 0:h] = (ah * ga_ref[...]).astype(o_ref.dtype)
        o_ref[:, h:2 * h] = (bh * gb_ref[...]).astype(o_ref.dtype)

    return _row_call(body, "mix_norm_fwd", s, [(y_ssm, "rows"), (y_sgu, "rows"), (g_ssm, "vec"), (g_sgu, "vec")],
                     [(jax.ShapeDtypeStruct((s, 2 * h), _MXU), "rows")])[0]


def _mix_norm_bwd(dyc, y_ssm, y_sgu, g_ssm, g_sgu):
    s, h = y_ssm.shape

    def body(d_ref, a_ref, b_ref, ga_ref, gb_ref, da_ref, db_ref, dga_ref, dgb_ref):
        first = pl.program_id(0) == 0
        _zero_first(first, dga_ref, dgb_ref)
        for lo, y_ref, g_ref, dy_ref, dg_ref in ((0, a_ref, ga_ref, da_ref, dga_ref), (h, b_ref, gb_ref, db_ref, dgb_ref)):
            d = d_ref[:, lo:lo + h]
            yh, r = _rms(y_ref[...])
            _acc(dg_ref, first, _colsum(d * yh))
            dy_ref[...] = _rms_bwd(d * g_ref[...], yh, r)

    vec = jax.ShapeDtypeStruct((1, h), _F32)
    full = jax.ShapeDtypeStruct((s, h), _F32)
    return _row_call(body, "mix_norm_bwd", s,
                     [(dyc, "rows"), (y_ssm, "rows"), (y_sgu, "rows"), (g_ssm, "vec"), (g_sgu, "vec")],
                     [(full, "rows"), (full, "rows"), (vec, "vec"), (vec, "vec")])


CONV_ROWS = 64


def _conv_rows(ext, w_ref, b_ref):
    x = ext[SUBLANES:]
    s1 = pltpu.roll(ext, 1, 0)[SUBLANES:]
    s2 = pltpu.roll(ext, 2, 0)[SUBLANES:]
    return b_ref[...] + w_ref[0:1, :] * s2 + w_ref[1:2, :] * s1 + w_ref[2:3, :] * x, x, s1, s2


def _conv_window(x_ref, r0):
    if isinstance(r0, int):
        assert r0 == 0
        return jnp.concatenate([jnp.zeros((SUBLANES, x_ref.shape[1]), _F32), x_ref[0:CONV_ROWS, :]], axis=0)
    return x_ref[pl.ds(pl.multiple_of(r0 - SUBLANES, SUBLANES), CONV_ROWS + SUBLANES), :]


def _conv_act_fwd(up_pre, conv_w, conv_b):
    s, f2 = up_pre.shape
    f = f2 // 2
    tc = _tile(f, 256)
    nf = f // tc

    def shift_down(x, k):
        row = lax.broadcasted_iota(jnp.int32, x.shape, 0)
        return jnp.where(row >= k, pltpu.roll(x, k, 0), 0.0)

    def conv(x, w_ref, b_ref):
        return b_ref[...] + w_ref[0:1, :] * shift_down(x, 2) + w_ref[1:2, :] * shift_down(x, 1) + w_ref[2:3, :] * x

    def body(a_ref, b_ref, wa_ref, wb_ref, ba_ref, bb_ref, o_ref):
        a = conv(a_ref[...], wa_ref, ba_ref)
        b = conv(b_ref[...], wb_ref, bb_ref)
        o_ref[...] = (a * _sigmoid(a) * b).astype(o_ref.dtype)

    return pl.pallas_call(
        body, name="conv_act_fwd", grid=(nf,), out_shape=jax.ShapeDtypeStruct((s, f), _MXU),
        in_specs=[pl.BlockSpec((s, tc), lambda n: (0, n)), pl.BlockSpec((s, tc), lambda n: (0, n + nf)),
                  pl.BlockSpec((3, tc), lambda n: (0, n)), pl.BlockSpec((3, tc), lambda n: (0, n + nf)),
                  pl.BlockSpec((1, tc), lambda n: (0, n)), pl.BlockSpec((1, tc), lambda n: (0, n + nf))],
        out_specs=pl.BlockSpec((s, tc), lambda n: (0, n)), compiler_params=_cp("parallel"),
    )(up_pre, up_pre, conv_w, conv_w, conv_b, conv_b)


def _conv_act_bwd(up_pre, d_act, conv_w, conv_b):
    s, f2 = up_pre.shape
    f = f2 // 2
    tc = _tile(f, 256)
    nf = f // tc

    def body(a_ref, b_ref, d_ref, wa_ref, wb_ref, ba_ref, bb_ref,
             du_ref, w0a, w0b, w1a, w1b, w2a, w2b, dba, dbb):
        n = s // CONV_ROWS
        zero8 = jnp.zeros((SUBLANES, tc), _F32)
        ext_rows = CONV_ROWS + SUBLANES

        def fold(x):
            out = x[0:SUBLANES]
            for k in range(1, CONV_ROWS // SUBLANES):
                out = out + x[k * SUBLANES:(k + 1) * SUBLANES]
            return out

        def chunk(r0, carry):
            nxt, acc = carry
            a, xa, xa1, xa2 = _conv_rows(_conv_window(a_ref, r0), wa_ref, ba_ref)
            b, xb, xb1, xb2 = _conv_rows(_conv_window(b_ref, r0), wb_ref, bb_ref)
            sg = _sigmoid(a)
            d = d_ref[pl.ds(r0, CONV_ROWS), :]
            du_a = d * b * (sg * (1.0 + a * (1.0 - sg)))
            du_b = d * (a * sg)
            new_acc = []
            for h, (du, x0, x1, x2, w_ref) in enumerate(((du_a, xa, xa1, xa2, wa_ref), (du_b, xb, xb1, xb2, wb_ref))):
                ext = jnp.concatenate([du, nxt[h]], axis=0)
                u1 = pltpu.roll(ext, ext_rows - 1, 0)[:CONV_ROWS]
                u2 = pltpu.roll(ext, ext_rows - 2, 0)[:CONV_ROWS]
                du_ref[h, pl.ds(r0, CONV_ROWS), :] = (w_ref[2:3, :] * du + w_ref[1:2, :] * u1
                                                      + w_ref[0:1, :] * u2).astype(du_ref.dtype)
                new_acc += [acc[4 * h] + fold(du * x2), acc[4 * h + 1] + fold(du * x1), acc[4 * h + 2] + fold(du * x0),
                            acc[4 * h + 3] + fold(du)]
            return (du_a[:SUBLANES], du_b[:SUBLANES]), tuple(new_acc)

        def step(i, carry):
            return chunk(pl.multiple_of((n - 1 - i) * CONV_ROWS, CONV_ROWS), carry)

        carry = lax.fori_loop(0, n - 1, step, ((zero8, zero8), (zero8,) * 8))
        _, acc = chunk(0, carry)
        for ref, val in zip((w0a, w1a, w2a, dba, w0b, w1b, w2b, dbb), acc):
            ref[...] = _colsum(val)

    col_a = pl.BlockSpec((s, tc), lambda n: (0, n))
    col_b = pl.BlockSpec((s, tc), lambda n: (0, n + nf))
    vec_a = pl.BlockSpec((1, tc), lambda n: (0, n))
    vec_b = pl.BlockSpec((1, tc), lambda n: (0, n + nf))
    vec = jax.ShapeDtypeStruct((1, f), _F32)
    outs = pl.pallas_call(
        body, name="conv_act_bwd", grid=(nf,),
        in_specs=[col_a, col_b, col_a, pl.BlockSpec((3, tc), lambda n: (0, n)),
                  pl.BlockSpec((3, tc), lambda n: (0, n + nf)), vec_a, vec_b],
        out_specs=[pl.BlockSpec((2, s, tc), lambda n: (0, 0, n))] + [vec_a] * 8,
        out_shape=[jax.ShapeDtypeStruct((2, s, f), _MXU)] + [vec] * 8, compiler_params=_cp("parallel"),
    )(up_pre, up_pre, d_act, conv_w, conv_w, conv_b, conv_b)
    du, w0a, w0b, w1a, w1b, w2a, w2b, dba, dbb = outs
    cat = lambda p, q: jnp.concatenate([p, q], axis=1)
    return du, cat(w0a, w0b), cat(w1a, w1b), cat(w2a, w2b), cat(dba, dbb)


def _sgu_recompute(zu_ref, zv_ref, lng_ref, lnb_ref, wm_ref, bs_ref, nh):
    zu, zv = zu_ref[...], zv_ref[...]
    u = _gelu(zu)
    gv = _gelu(zv)
    xc = gv - _rowmean(gv)
    rs = lax.rsqrt(_rowmean(xc * xc) + EPS)
    vh = xc * rs
    v = vh * lng_ref[...] + lnb_ref[...]
    mixed = []
    for h in range(nh):
        vhd = v[:, h * CHUNK:(h + 1) * CHUNK].astype(_MXU)
        mixed.append(jnp.dot(wm_ref[h].astype(_MXU), vhd, preferred_element_type=_F32) + bs_ref[h])
    return zu, zv, u, vh, rs, v, mixed


def _sgu_fwd(z, ln_g, ln_b, wm, bs):
    s = z.shape[0]
    nh = wm.shape[0]
    hd = nh * CHUNK

    def body(zu_ref, zv_ref, lng_ref, lnb_ref, wm_ref, bs_ref, y_ref):
        _, _, u, _, _, _, mixed = _sgu_recompute(zu_ref, zv_ref, lng_ref, lnb_ref, wm_ref, bs_ref, nh)
        for h in range(nh):
            y_ref[:, h * CHUNK:(h + 1) * CHUNK] = u[:, h * CHUNK:(h + 1) * CHUNK] * mixed[h]

    vec = pl.BlockSpec((1, hd), lambda i: (0, 0))
    return pl.pallas_call(
        body, name="sgu_fwd", grid=(s // CHUNK,), out_shape=jax.ShapeDtypeStruct((s, hd), _F32),
        in_specs=[pl.BlockSpec((CHUNK, hd), lambda i: (i, 1)), pl.BlockSpec((CHUNK, hd), lambda i: (i, 2)), vec, vec,
                  pl.BlockSpec((nh, CHUNK, CHUNK), lambda i: (0, 0, 0)), pl.BlockSpec((nh, CHUNK, 1), lambda i: (0, 0, 0))],
        out_specs=pl.BlockSpec((CHUNK, hd), lambda i: (i, 0)), compiler_params=_cp("parallel"),
    )(z, z, ln_g, ln_b, wm, bs)


def _sgu_bwd(z, dy, dz_ssm, ln_g, ln_b, wm, bs):
    s = z.shape[0]
    nh = wm.shape[0]
    hd = nh * CHUNK

    def body(zu_ref, zv_ref, dy_ref, dzs_ref, lng_ref, lnb_ref, wm_ref, bs_ref,
             dz_ref, dlg_ref, dlb_ref, dwm_ref, dbs_ref, dv_scr):
        first = pl.program_id(0) == 0
        _zero_first(first, dlg_ref, dlb_ref, dwm_ref, dbs_ref)
        zu, zv, u, vh, rs, v, mixed = _sgu_recompute(zu_ref, zv_ref, lng_ref, lnb_ref, wm_ref, bs_ref, nh)
        dy = dy_ref[...]
        dz_ref[:, 0:hd] = dzs_ref[...].astype(dz_ref.dtype)
        for h in range(nh):
            cols = slice(h * CHUNK, (h + 1) * CHUNK)
            dyh = dy[:, cols]
            dz_ref[:, hd + h * CHUNK:hd + (h + 1) * CHUNK] = (dyh * mixed[h] * _gelu_grad(zu[:, cols])).astype(dz_ref.dtype)
            dm = dyh * u[:, cols]
            dmx = dm.astype(_MXU)
            _acc(dbs_ref.at[h], first, jnp.sum(dm, axis=1, keepdims=True))
            _acc(dwm_ref.at[h], first,
                 lax.dot_general(dmx, v[:, cols].astype(_MXU), (((1,), (1,)), ((), ())), preferred_element_type=_F32))
            dv_scr[:, cols] = lax.dot_general(wm_ref[h].astype(_MXU), dmx, (((0,), (0,)), ((), ())),
                                              preferred_element_type=_F32)
        dv = dv_scr[...]
        _acc(dlg_ref, first, _colsum(dv * vh))
        _acc(dlb_ref, first, _colsum(dv))
        dvh = dv * lng_ref[...]
        dgv = rs * (dvh - _rowmean(dvh) - vh * _rowmean(dvh * vh))
        dz_ref[:, 2 * hd:3 * hd] = (dgv * _gelu_grad(zv)).astype(dz_ref.dtype)

    vec = pl.BlockSpec((1, hd), lambda i: (0, 0))
    wspec = pl.BlockSpec((nh, CHUNK, CHUNK), lambda i: (0, 0, 0))
    bspec = pl.BlockSpec((nh, CHUNK, 1), lambda i: (0, 0, 0))
    rows = pl.BlockSpec((CHUNK, hd), lambda i: (i, 0))
    return pl.pallas_call(
        body, name="sgu_bwd", grid=(s // CHUNK,),
        out_shape=[jax.ShapeDtypeStruct((s, 3 * hd), _MXU), jax.ShapeDtypeStruct((1, hd), _F32),
                   jax.ShapeDtypeStruct((1, hd), _F32), jax.ShapeDtypeStruct((nh, CHUNK, CHUNK), _F32),
                   jax.ShapeDtypeStruct((nh, CHUNK, 1), _F32)],
        in_specs=[pl.BlockSpec((CHUNK, hd), lambda i: (i, 1)), pl.BlockSpec((CHUNK, hd), lambda i: (i, 2)),
                  rows, rows, vec, vec, wspec, bspec],
        out_specs=[pl.BlockSpec((CHUNK, 3 * hd), lambda i: (i, 0)), vec, vec, wspec, bspec],
        scratch_shapes=[pltpu.VMEM((CHUNK, hd), _F32)], compiler_params=_cp("arbitrary"),
    )(z, z, dy, dz_ssm, ln_g, ln_b, wm, bs)


def _ssm_prep(log_dt, a_re, a_im, b_re_t, b_im_t, kvec):
    gn = a_re.shape[1]

    def body(ldt_ref, are_ref, aim_ref, br_ref, bi_ref, k_ref, pr_ref, pi_ref, bbr_ref, bbi_ref):
        dt = jnp.exp(ldt_ref[...])
        are, aim = are_ref[...], aim_ref[...]
        k = k_ref[...]
        mag = jnp.exp(k * (are * dt))
        ang = k * (aim * dt)
        pr_ref[...] = mag * jnp.cos(ang)
        pi_ref[...] = mag * jnp.sin(ang)
        m1 = jnp.exp(are * dt)
        lr, li = m1 * jnp.cos(aim * dt), m1 * jnp.sin(aim * dt)
        den = are * are + aim * aim
        nr = lr - 1.0
        f_re = (nr * are + li * aim) / den
        f_im = (li * are - nr * aim) / den
        bbr_ref[...] = f_re * br_ref[...] - f_im * bi_ref[...]
        bbi_ref[...] = f_re * bi_ref[...] + f_im * br_ref[...]

    pw = jax.ShapeDtypeStruct((kvec.shape[0], gn), _F32)
    bb = jax.ShapeDtypeStruct(b_re_t.shape, _F32)
    return pl.pallas_call(body, name="ssm_prep", out_shape=[pw, pw, bb, bb])(log_dt, a_re, a_im, b_re_t, b_im_t, kvec)


def _ssm_prep_bwd(log_dt, a_re, a_im, b_re_t, b_im_t, d_bbr, d_bbi, d_lr, d_li):
    def body(ldt_ref, are_ref, aim_ref, br_ref, bi_ref, dbr_ref, dbi_ref, dlr_ref, dli_ref,
             obr_ref, obi_ref, oar_ref, oai_ref, odt_ref):
        dt = jnp.exp(ldt_ref[...])
        are, aim = are_ref[...], aim_ref[...]
        m1 = jnp.exp(are * dt)
        lr, li = m1 * jnp.cos(aim * dt), m1 * jnp.sin(aim * dt)
        den = are * are + aim * aim
        nr = lr - 1.0
        f_re = (nr * are + li * aim) / den
        f_im = (li * are - nr * aim) / den
        br, bi, dbr, dbi = br_ref[...], bi_ref[...], dbr_ref[...], dbi_ref[...]
        obr_ref[...] = f_re * dbr + f_im * dbi
        obi_ref[...] = f_re * dbi - f_im * dbr
        gf_re = _colsum(br * dbr + bi * dbi)
        gf_im = _colsum(br * dbi - bi * dbr)
        il_re, il_im = are / den, -aim / den
        glb_re = dlr_ref[...] + (il_re * gf_re + il_im * gf_im)
        glb_im = dli_ref[...] + (il_re * gf_im - il_im * gf_re)
        q_re = -(f_re * il_re - f_im * il_im)
        q_im = -(f_re * il_im + f_im * il_re)
        gl_re = q_re * gf_re + q_im * gf_im
        gl_im = q_re * gf_im - q_im * gf_re
        gl_re = gl_re + dt * (lr * glb_re + li * glb_im)
        gl_im = gl_im + dt * (lr * glb_im - li * glb_re)
        w_re = are * lr - aim * li
        w_im = are * li + aim * lr
        oar_ref[...] = gl_re
        oai_ref[...] = gl_im
        odt_ref[...] = w_re * glb_re + w_im * glb_im

    bb = jax.ShapeDtypeStruct(b_re_t.shape, _F32)
    v = jax.ShapeDtypeStruct(a_re.shape, _F32)
    return pl.pallas_call(body, name="ssm_prep_bwd", out_shape=[bb, bb, v, v, v])(
        log_dt, a_re, a_im, b_re_t, b_im_t, d_bbr, d_bbi, d_lr, d_li)


def _group_sum(d_dt, log_dt):
    def body(d_ref, l_ref, o_ref):
        o_ref[...] = jnp.sum(d_ref[...], axis=1, keepdims=True) * jnp.exp(l_ref[...])

    return pl.pallas_call(body, name="ssm_dt_grad", out_shape=jax.ShapeDtypeStruct(log_dt.shape, _F32))(d_dt, log_dt)


def _load_strided(ref, nr):
    return jnp.concatenate([ref[pl.ds(r, SUBLANES, stride=nr), :] for r in range(nr)], axis=0)


def _store_strided(ref, val, nr):
    for r in range(nr):
        ref[pl.ds(r, SUBLANES, stride=nr), :] = val[r * SUBLANES:(r + 1) * SUBLANES]


def _scan_strided(src_ref, dst_ref, nr, p_ref, carry, reverse, h_ref=None, h_in=None):
    ns = BLOCK_ST
    row = lax.broadcasted_iota(jnp.int32, (SUBLANES, ns), 0)
    bc = lambda v: jnp.broadcast_to(v, (SUBLANES, ns))
    tile = lambda ref, r: (ref[r * SUBLANES:(r + 1) * SUBLANES, 0:ns], ref[r * SUBLANES:(r + 1) * SUBLANES, ns:2 * ns])
    one = nr - 1 if reverse else 0
    ar, ai = bc(p_ref[one:one + 1, 0:ns]), bc(p_ref[one:one + 1, ns:2 * ns])
    xr = xi = None
    for r in (range(nr - 1, -1, -1) if reverse else range(nr)):
        sr, si = tile(src_ref, r)
        xr, xi = (sr, si) if xr is None else (ar * xr - ai * xi + sr, ar * xi + ai * xr + si)
        dst_ref[r * SUBLANES:(r + 1) * SUBLANES, 0:ns] = xr
        dst_ref[r * SUBLANES:(r + 1) * SUBLANES, ns:2 * ns] = xi
    edge, shift = (SUBLANES - 1, SUBLANES - 1) if reverse else (0, 1)
    dr = jnp.where(row == edge, carry[0], pltpu.roll(xr, shift, 0))
    di = jnp.where(row == edge, carry[1], pltpu.roll(xi, shift, 0))
    for i, k in enumerate((1, 2, 4)):
        qr, qi = bc(p_ref[nr + i:nr + i + 1, 0:ns]), bc(p_ref[nr + i:nr + i + 1, ns:2 * ns])
        keep = (row < SUBLANES - k) if reverse else (row >= k)
        sr = jnp.where(keep, pltpu.roll(dr, (SUBLANES - k) if reverse else k, 0), 0.0)
        si = jnp.where(keep, pltpu.roll(di, (SUBLANES - k) if reverse else k, 0), 0.0)
        dr, di = dr + qr * sr - qi * si, di + qr * si + qi * sr
    acc_r = acc_i = jnp.zeros((SUBLANES, ns), _F32)
    out = None
    for r in range(nr):
        wr, wi = p_ref[r:r + 1, 0:ns], p_ref[r:r + 1, ns:2 * ns]
        xr, xi = tile(dst_ref, r)
        xr, xi = xr + wr * dr - wi * di, xi + wr * di + wi * dr
        dst_ref[r * SUBLANES:(r + 1) * SUBLANES, 0:ns] = xr
        dst_ref[r * SUBLANES:(r + 1) * SUBLANES, ns:2 * ns] = xi
        if h_ref is not None:
            if r == 0:
                lr, li = tile(h_ref, nr - 1)
                pr, pi = jnp.where(row == 0, h_in[0], pltpu.roll(lr, 1, 0)), jnp.where(row == 0, h_in[1], pltpu.roll(li, 1, 0))
            else:
                pr, pi = tile(h_ref, r - 1)
            acc_r = acc_r + (xr * pr + xi * pi)
            acc_i = acc_i + (xi * pr - xr * pi)
        if r == (0 if reverse else nr - 1):
            out = (xr[0:1, :], xi[0:1, :]) if reverse else (xr[SUBLANES - 1:SUBLANES, :], xi[SUBLANES - 1:SUBLANES, :])
    if h_ref is None:
        return out
    return out, (_colsum(acc_r), _colsum(acc_i))


def _ssm_gate(y, wg_ref, bg_ref):
    yg = _gelu(y)
    gate = _sigmoid(jnp.dot(yg.astype(_MXU), wg_ref[...].astype(_MXU), preferred_element_type=_F32) + bg_ref[...])
    return yg, gate


def _ssm_specs(nb, nt, t, reverse):
    tt = (lambda ti: nt - 1 - ti) if reverse else (lambda ti: ti)
    ns2 = 2 * BLOCK_ST
    return dict(
        z=pl.BlockSpec((t, BLOCK_CH), lambda b, ti: (tt(ti), b)),
        bbt=pl.BlockSpec((None, BLOCK_CH, ns2), lambda b, ti: (b, 0, 0)),
        ct=pl.BlockSpec((None, ns2, BLOCK_CH), lambda b, ti: (b, 0, 0)),
        vec=pl.BlockSpec((1, BLOCK_CH), lambda b, ti: (0, b)),
        wg=pl.BlockSpec((None, BLOCK_CH, BLOCK_CH), lambda b, ti: (b, 0, 0)),
        p=pl.BlockSpec((None, t // SUBLANES + SUBLANES, ns2), lambda b, ti: (b, 0, 0)),
        hb=pl.BlockSpec((None, None, SUBLANES, ns2), lambda b, ti: (b, tt(ti), 0, 0)),
        h=pl.BlockSpec((None, t, ns2), lambda b, ti: (b, tt(ti), 0)),
        acc_vec=pl.BlockSpec((None, 1, ns2), lambda b, ti: (b, 0, 0)),
    )


def _ssm_fwd(z, bbt, ct, dvec, wg, bglu, ptab):
    s = z.shape[0]
    nb = bbt.shape[0]
    t = _tile(s, TIME_TILE, SUBLANES)
    nt = s // t
    ns = BLOCK_ST
    sp = _ssm_specs(nb, nt, t, False)

    nr = t // SUBLANES

    def body(z_ref, bbt_ref, ct_ref, d_ref, wg_ref, bg_ref, p_ref, y2_ref, y_ref, h_ref, hb_ref, bu_scr, h_scr, carry_scr):
        _zero_first(pl.program_id(1) == 0, carry_scr)
        hb_ref[...] = carry_scr[...]
        carry_in = (carry_scr[0:1, 0:ns], carry_scr[0:1, ns:2 * ns])
        u = _load_strided(z_ref, nr)
        bu_scr[...] = jnp.dot(u.astype(_MXU), bbt_ref[...].astype(_MXU), preferred_element_type=_F32)
        cr, ci = _scan_strided(bu_scr, h_scr, nr, p_ref, carry_in, False)
        hx = h_scr[...].astype(_MXU)
        h_ref[...] = hx
        y = jnp.dot(hx, ct_ref[...].astype(_MXU), preferred_element_type=_F32) + d_ref[...] * u
        yg, gate = _ssm_gate(y, wg_ref, bg_ref)
        _store_strided(y2_ref, yg * gate, nr)
        _store_strided(y_ref, y, nr)
        carry_scr[:, 0:ns] = jnp.broadcast_to(cr, (SUBLANES, ns))
        carry_scr[:, ns:2 * ns] = jnp.broadcast_to(ci, (SUBLANES, ns))

    ych = jax.ShapeDtypeStruct((s, nb * BLOCK_CH), _F32)
    return pl.pallas_call(
        body, name="ssm_fwd", grid=(nb, nt),
        out_shape=[ych, ych, jax.ShapeDtypeStruct((nb, s, 2 * ns), _MXU),
                   jax.ShapeDtypeStruct((nb, nt, SUBLANES, 2 * ns), _F32)],
        in_specs=[sp["z"], sp["bbt"], sp["ct"], sp["vec"], sp["wg"], sp["vec"], sp["p"]],
        out_specs=[sp["z"], sp["z"], sp["h"], sp["hb"]],
        scratch_shapes=[pltpu.VMEM((t, 2 * ns), _F32), pltpu.VMEM((t, 2 * ns), _F32), pltpu.VMEM((SUBLANES, 2 * ns), _F32)],
        compiler_params=_cp("parallel", "arbitrary"),
    )(z, bbt, ct, dvec, wg, bglu, ptab)


def _ssm_bwd(z, y_pre, h_all, dy2, hb, bbt, ct, dvec, wg, bglu, ptab_rev):
    s = z.shape[0]
    nb = bbt.shape[0]
    t = _tile(s, TIME_TILE, SUBLANES)
    nt = s // t
    ns = BLOCK_ST
    sp = _ssm_specs(nb, nt, t, True)
    tn_dims = (((0,), (0,)), ((), ()))
    nt_dims = (((1,), (1,)), ((), ()))

    nr = t // SUBLANES

    def body(z_ref, y_ref, h_ref, dy2_ref, hb_ref, bbt_ref, ct_ref, d_ref, wg_ref, bg_ref, pr_ref,
             dz_ref, dbbt_ref, dct_ref, dwg_ref, dlb_ref, dd_ref, dbg_ref, bu_scr, g_scr, h_scr, gcarry_scr):
        first = pl.program_id(1) == 0

        _zero_first(first, gcarry_scr, dbbt_ref, dct_ref, dwg_ref, dlb_ref, dd_ref, dbg_ref)
        u = _load_strided(z_ref, nr)
        hin = hb_ref[...]
        y = _load_strided(y_ref, nr)
        yg, gate = _ssm_gate(y, wg_ref, bg_ref)
        dy2 = _load_strided(dy2_ref, nr)
        dpre = dy2 * yg * gate * (1.0 - gate)
        _acc(dbg_ref, first, _colsum(dpre))
        dpx = dpre.astype(_MXU)
        _acc(dwg_ref, first, lax.dot_general(yg.astype(_MXU), dpx, tn_dims, preferred_element_type=_F32))
        dyg = dy2 * gate + lax.dot_general(dpx, wg_ref[...].astype(_MXU), nt_dims, preferred_element_type=_F32)
        dy = dyg * _gelu_grad(y)
        _acc(dd_ref, first, _colsum(dy * u))
        dyx = dy.astype(_MXU)
        hx = h_ref[...]
        h_scr[...] = hx.astype(_F32)
        _acc(dct_ref, first, lax.dot_general(hx, dyx, tn_dims, preferred_element_type=_F32))
        bu_scr[...] = lax.dot_general(dyx, ct_ref[...].astype(_MXU), nt_dims, preferred_element_type=_F32)
        gin = (gcarry_scr[0:1, 0:ns], gcarry_scr[0:1, ns:2 * ns])
        (gr, gi), (d_ar, d_ai) = _scan_strided(bu_scr, g_scr, nr, pr_ref, gin, True, h_scr,
                                               (hin[0:1, 0:ns], hin[0:1, ns:2 * ns]))
        gcarry_scr[:, 0:ns] = jnp.broadcast_to(gr, (SUBLANES, ns))
        gcarry_scr[:, ns:2 * ns] = jnp.broadcast_to(gi, (SUBLANES, ns))
        _acc(dlb_ref, first, jnp.concatenate([d_ar, d_ai], axis=1))
        gx = g_scr[...].astype(_MXU)
        _acc(dbbt_ref, first, lax.dot_general(u.astype(_MXU), gx, tn_dims, preferred_element_type=_F32))
        _store_strided(dz_ref, dy * d_ref[...] + lax.dot_general(gx, bbt_ref[...].astype(_MXU), nt_dims,
                                                                 preferred_element_type=_F32), nr)

    f = lambda shape: jax.ShapeDtypeStruct(shape, _F32)
    return pl.pallas_call(
        body, name="ssm_bwd", grid=(nb, nt),
        out_shape=[f((s, nb * BLOCK_CH)), f(bbt.shape), f(ct.shape), f(wg.shape), f((nb, 1, 2 * ns)),
                   f((1, nb * BLOCK_CH)), f((1, nb * BLOCK_CH))],
        in_specs=[sp["z"], sp["z"], sp["h"], sp["z"], sp["hb"], sp["bbt"], sp["ct"], sp["vec"], sp["wg"], sp["vec"], sp["p"]],
        out_specs=[sp["z"], sp["bbt"], sp["ct"], sp["wg"], sp["acc_vec"], sp["vec"], sp["vec"]],
        scratch_shapes=[pltpu.VMEM((t, 2 * ns), _F32), pltpu.VMEM((t, 2 * ns), _F32), pltpu.VMEM((t, 2 * ns), _F32),
                        pltpu.VMEM((SUBLANES, 2 * ns), _F32)],
        compiler_params=_cp("parallel", "arbitrary"),
    )(z, y_pre, h_all, dy2, hb, bbt, ct, dvec, wg, bglu, ptab_rev)


def _mod_part(c_all, w, b):
    d, ns = w.shape
    tn = _tile(ns, 512)

    def body(c_ref, w_ref, b_ref, o_ref):
        c = c_ref[...]
        ca = (c * _sigmoid(c)).astype(_MXU)
        o_ref[...] = jnp.dot(ca, w_ref[...].astype(_MXU), preferred_element_type=_F32) + b_ref[...]

    return pl.pallas_call(
        body, name="mod_part", grid=(ns // tn,), out_shape=jax.ShapeDtypeStruct((8, ns), _F32),
        in_specs=[pl.BlockSpec((8, d), lambda n: (0, 0)), pl.BlockSpec((d, tn), lambda n: (0, n)),
                  pl.BlockSpec((1, tn), lambda n: (0, n))],
        out_specs=pl.BlockSpec((8, tn), lambda n: (0, n)), compiler_params=_cp("parallel"),
    )(c_all, w, b)


def _adamw_math(w, g, m, v):
    m = ADAM_B1 * m + (1.0 - ADAM_B1) * g
    v = ADAM_B2 * v + (1.0 - ADAM_B2) * (g * g)
    m_hat = m / (1.0 - ADAM_B1 ** ADAM_STEP)
    v_hat = v / (1.0 - ADAM_B2 ** ADAM_STEP)
    delta = -ADAM_LR * (m_hat / (jnp.sqrt(v_hat) + ADAM_EPS) + ADAM_WD * w)
    return delta, m, v


def _adamw(w, g, m, v, name):
    r, c = w.shape
    tc = c if c <= 4096 else _tile(c, 4096)
    tr = _tile(r, max(SUBLANES, (1 << 18) // tc), SUBLANES)

    def body(w_ref, g_ref, m_ref, v_ref, go_ref, d_ref, mo_ref, vo_ref):
        g = g_ref[...]
        go_ref[...] = g
        d_ref[...], mo_ref[...], vo_ref[...] = _adamw_math(w_ref[...], g, m_ref[...], v_ref[...])

    spec = pl.BlockSpec((tr, tc), lambda i, j: (i, j))
    out = jax.ShapeDtypeStruct((r, c), _F32)
    return pl.pallas_call(
        body, name=name, grid=(r // tr, c // tc), in_specs=[spec] * 4, out_specs=[spec] * 4, out_shape=[out] * 4,
        compiler_params=_cp("parallel", "parallel"),
    )(w, g, m, v)


def _adamw_halves(w, g2, m, v, name):
    r, c = w.shape
    tr, tc = _tile(r, 256, SUBLANES), _tile(c // 2, 1024)
    nph = (c // 2) // tc

    def body(w_ref, g_ref, m_ref, v_ref, go_ref, d_ref, mo_ref, vo_ref):
        g = g_ref[...]
        go_ref[...] = g
        d_ref[...], mo_ref[...], vo_ref[...] = _adamw_math(w_ref[...], g, m_ref[...], v_ref[...])

    spec = pl.BlockSpec((tr, tc), lambda i, j: (i, j))
    out = jax.ShapeDtypeStruct((r, c), _F32)
    return pl.pallas_call(
        body, name=name, grid=(r // tr, c // tc),
        in_specs=[spec, pl.BlockSpec((None, tr, tc), lambda i, j: (j // nph, i, j % nph)), spec, spec],
        out_specs=[spec] * 4, out_shape=[out] * 4, compiler_params=_cp("parallel", "parallel"),
    )(w, g2, m, v)


def _wada_update(c_t, dm, w, m, v):
    d, ns = w.shape
    tr, tc = _tile(d, 256, SUBLANES), _tile(ns, 1024)

    def body(c_ref, dm_ref, w_ref, m_ref, v_ref, g_ref, d_ref, mo_ref, vo_ref):
        c = c_ref[...]
        ca = c * _sigmoid(c)
        dmv = dm_ref[...]
        g = ca[:, 0:1] * dmv[0:1, :]
        for b in range(1, 8):
            g = g + ca[:, b:b + 1] * dmv[b:b + 1, :]
        g_ref[...] = g
        d_ref[...], mo_ref[...], vo_ref[...] = _adamw_math(w_ref[...], g, m_ref[...], v_ref[...])

    spec = pl.BlockSpec((tr, tc), lambda i, j: (i, j))
    out = jax.ShapeDtypeStruct((d, ns), _F32)
    return pl.pallas_call(
        body, name="wada_update", grid=(d // tr, ns // tc),
        in_specs=[pl.BlockSpec((tr, 8), lambda i, j: (i, 0)), pl.BlockSpec((8, tc), lambda i, j: (0, j)), spec, spec, spec],
        out_specs=[spec] * 4, out_shape=[out] * 4, compiler_params=_cp("parallel", "parallel"),
    )(c_t, dm, w, m, v)


def _small_reduce(gathered):
    _, r, c = gathered.shape
    tr = _tile(r, 512, SUBLANES)

    def body(q_ref, g_ref):
        g = q_ref[0]
        for k in range(1, 8):
            g = g + q_ref[k]
        g_ref[...] = g

    return pl.pallas_call(
        body, name="small_reduce", grid=(r // tr,), out_shape=jax.ShapeDtypeStruct((r, c), _F32),
        in_specs=[pl.BlockSpec((8, tr, c), lambda i: (0, i, 0))], out_specs=pl.BlockSpec((tr, c), lambda i: (i, 0)),
        compiler_params=_cp("parallel"),
    )(gathered)


def _adamw_many(ws, gs, ms, vs, steps, name):
    n = len(ws)

    def body(*refs):
        w_refs, g_refs, m_refs, v_refs = refs[0:n], refs[n:2 * n], refs[2 * n:3 * n], refs[3 * n:4 * n]
        d_refs, mo_refs, vo_refs = refs[4 * n:5 * n], refs[5 * n:6 * n], refs[6 * n:7 * n]
        for i in range(n):
            d_refs[i][...], mo_refs[i][...], vo_refs[i][...] = _adamw_math(
                w_refs[i][...], g_refs[i][...], m_refs[i][...], v_refs[i][...])

    def spec(a):
        nd = a.ndim
        if steps == 1:
            return pl.BlockSpec(a.shape, lambda i: (0,) * nd)
        return pl.BlockSpec((a.shape[0] // steps,) + a.shape[1:], lambda i: (i,) + (0,) * (nd - 1))

    specs = [spec(w) for w in ws]
    outs = pl.pallas_call(
        body, name=name, grid=(steps,), in_specs=specs * 4, out_specs=specs * 3,
        out_shape=[jax.ShapeDtypeStruct(w.shape, _F32) for w in ws] * 3, compiler_params=_cp("parallel"),
    )(*ws, *gs, *ms, *vs)
    return outs[0:n], outs[n:2 * n], outs[2 * n:3 * n]


def _block_diag(x, eye=None):
    nb, g, p, q = x.shape
    eye = jnp.eye(g, dtype=x.dtype) if eye is None else eye
    return (x[:, :, :, None, :] * eye[None, :, None, :, None]).reshape(nb, g * p, g * q)


def _block_diag_take(x, p, q):
    nb = x.shape[0]
    g = GROUPS_PER_BLOCK
    eye = jnp.eye(g, dtype=x.dtype)
    return jnp.sum(x.reshape(nb, g, p, g, q) * eye[None, :, None, :, None], axis=3)


_VIEWS = {"ssm_b_re": ((0, 2, 1), (0, 2, 1)), "ssm_b_im": ((0, 2, 1), (0, 2, 1)),
          "ssm_w_glu": ((1, 2, 0), (2, 0, 1)), "ssm_b_glu": ((1, 0), (1, 0))}


def _to_view(name, a):
    return a.transpose(_VIEWS[name][0]) if name in _VIEWS else a


def _from_view(name, a):
    return a.transpose(_VIEWS[name][1]) if name in _VIEWS else a


class _Pack:
    def __init__(self, shapes):
        self.shapes = shapes
        self.offsets = {}
        off = 0
        for name, shape in shapes.items():
            n = math.prod(shape)
            self.offsets[name] = (off, n)
            off += -(-n // (SUBLANES * LANES)) * (SUBLANES * LANES)
        self.rows = -(-off // (256 * LANES)) * 256

    def pack(self, arrays):
        parts = []
        off = 0
        for name, shape in self.shapes.items():
            start, n = self.offsets[name]
            if start > off:
                parts.append(jnp.zeros((start - off,), _F32))
            parts.append(arrays[name].reshape(-1).astype(_F32))
            off = start + n
        total = self.rows * LANES
        if total > off:
            parts.append(jnp.zeros((total - off,), _F32))
        return jnp.concatenate(parts).reshape(self.rows, LANES)

    def unpack(self, buf):
        flat = buf.reshape(-1)
        return {name: flat[start:start + n].reshape(self.shapes[name]) for name, (start, n) in self.offsets.items()}


_SMALL = ["b_ada", "g_pre_mix", "g_post_mix", "ssm_log_dt", "ssm_a_re", "ssm_a_im", "ssm_b_re", "ssm_b_im", "ssm_c_re",
          "ssm_c_im", "ssm_d", "ssm_w_glu", "ssm_b_glu", "sgu_ln_g", "sgu_ln_b", "sgu_w", "sgu_b", "g_out_ssm",
          "g_out_sgu", "g_pre_ffn", "g_post_ffn", "conv_b"]
_WEIGHTS = ["w_ada", "b_ada", "g_pre_mix", "g_post_mix", "w_in", "ssm_log_dt", "ssm_a_re", "ssm_a_im", "ssm_b_re",
            "ssm_b_im", "ssm_c_re", "ssm_c_im", "ssm_d", "ssm_w_glu", "ssm_b_glu", "sgu_ln_g", "sgu_ln_b", "sgu_w", "sgu_b",
            "g_out_ssm", "g_out_sgu", "w_out", "g_pre_ffn", "g_post_ffn", "w_up", "conv_w", "conv_b", "w_down"]


def _step(p, m, v, x, c, tgt):
    s, d = x.shape
    mx, my, mc = lax.axis_index("x"), lax.axis_index("y"), lax.axis_index("c")
    chip = 2 * mx + my
    dev = 4 * mx + 2 * my + mc
    sel = jnp.stack([chip, mc]).astype(jnp.int32)
    g_cnt, n_st = p["ssm_a_re"].shape
    nb = g_cnt // GROUPS_PER_BLOCK
    gn = g_cnt * n_st
    d_ssm = g_cnt * SSM_GROUP
    nh = p["sgu_w"].shape[0]
    assert nh * CHUNK == d_ssm and 2 * d_ssm == d and n_st == SSM_STATE

    shards = lambda g: g.reshape(4, g.shape[1] * g.shape[2], g.shape[3])
    buf_in = _cast_into_slot(p["w_in"], sel, sel, "cast_w_in")

    ns_ada = p["w_ada"].shape[1]
    nc_conv = p["conv_w"].shape[1]
    first = jnp.concatenate([jnp.broadcast_to(c, (8, d)), jnp.pad(p["conv_w"], ((0, 5), (0, 0)))], axis=1)
    first_all = _all_gather8(_own_slot(first, dev), "gather_c_conv", after=buf_in)
    (sems_in,), (buf_in,), tok = _gather_start([buf_in], first_all, "gather_start_in")
    c_all = _after(first_all[:, 0, :d], tok)
    conv_w_full = jnp.concatenate([first_all[2 * j, 0:3, d:] for j in range(4)], axis=1)
    b_ada_mine = lax.dynamic_slice_in_dim(p["b_ada"], chip * ns_ada, ns_ada, axis=1)
    mod_mine = _mod_part(c_all, p["w_ada"], b_ada_mine)
    buf_out, buf_up, buf_down = [_cast_into_slot(p[n], sel, tok, "cast_" + n) for n in ("w_out", "w_up", "w_down")]

    eye_t = jnp.eye(GROUPS_PER_BLOCK, dtype=_F32) + tok[0:1, 0:1]
    ldt_l = _after(jnp.repeat(p["ssm_log_dt"], n_st, axis=1), tok)
    are_l, aim_l = p["ssm_a_re"].reshape(1, gn), p["ssm_a_im"].reshape(1, gn)
    bre_t, bim_t = p["ssm_b_re"].reshape(gn, SSM_GROUP).T, p["ssm_b_im"].reshape(gn, SSM_GROUP).T
    nr = _tile(s, TIME_TILE, SUBLANES) // SUBLANES
    kvec = jnp.concatenate([jnp.arange(1, nr + 1, dtype=_F32), jnp.array([nr, 2 * nr, 4 * nr, 0, 0, 0, 0, 0], _F32)])
    pw_re, pw_im, bb_re, bb_im = _ssm_prep(ldt_l, are_l, aim_l, bre_t, bim_t, kvec.reshape(nr + SUBLANES, 1))
    blocks = lambda t: t.reshape(t.shape[0], nb, GROUPS_PER_BLOCK * n_st).transpose(1, 0, 2)
    ptab = jnp.concatenate([blocks(pw_re), blocks(pw_im)], axis=2)
    rev = lambda t: jnp.concatenate([t[:, :nr][:, ::-1], t[:, nr:]], axis=1)
    ptab_rev = jnp.concatenate([rev(blocks(pw_re)), -rev(blocks(pw_im))], axis=2)
    bd = lambda t: t.reshape(SSM_GROUP, nb, GROUPS_PER_BLOCK, n_st).transpose(1, 2, 0, 3)
    bbt = jnp.concatenate([_block_diag(bd(bb_re)), _block_diag(bd(bb_im))], axis=2).astype(_MXU)
    cd = lambda t: t.reshape(nb, GROUPS_PER_BLOCK, SSM_GROUP, n_st).transpose(0, 1, 3, 2)
    ct = jnp.concatenate([_block_diag(cd(p["ssm_c_re"]), eye_t), -_block_diag(cd(p["ssm_c_im"]), eye_t)], axis=1).astype(_MXU)
    wg = _block_diag(p["ssm_w_glu"].reshape(nb, GROUPS_PER_BLOCK, SSM_GROUP, SSM_GROUP), eye_t).astype(_MXU)
    dvec = p["ssm_d"]
    bglu = p["ssm_b_glu"].reshape(1, d_ssm)
    mask = jnp.tril(jnp.ones((CHUNK, CHUNK), _F32)) + tok[0:1, 0:1]
    wm = (p["sgu_w"] * mask[None]).astype(_MXU)
    bs = p["sgu_b"].reshape(nh, CHUNK, 1)

    mod_all = _all_gather8(_own_slot(mod_mine, dev), "gather_mod",
                           after=[buf_out, buf_up, buf_down, bbt, ct, wg, wm, bs, ptab, ptab_rev])
    (sems_out, sems_up), (buf_out, buf_up), tok_rest = _route_start([(buf_out, 1), (buf_up, 1)], mod_all, "route_start_a", 8)
    mod_rows = lax.dynamic_index_in_dim(mod_all, dev, axis=1, keepdims=False)
    mod = jnp.concatenate([mod_rows[0], mod_rows[2], mod_rows[4], mod_rows[6]]).reshape(N_MOD, 1, d)
    sh1, sc1, gt1, sh2, sc2, gt2 = [mod[i] for i in range(N_MOD)]

    h1 = _fwd_pre_mix(x, p["g_pre_mix"], _after(sc1, tok_rest), sh1)
    buf_in = _gather_wait(sems_in, buf_in, h1, "gather_wait_in")
    w_in4 = shards(_pair_forward([buf_in], "pair_forward_in")[0])
    z = _mm_nn(h1, w_in4, _F32, "mm_in")
    y_ssm, y_pre, h_all, hb = _ssm_fwd(z, bbt, ct, dvec, wg, bglu, ptab)
    y_sgu = _sgu_fwd(z, p["sgu_ln_g"], p["sgu_ln_b"], wm, bs)
    buf_out = _route_wait(sems_out, buf_out, 1, y_sgu, "route_wait_out_1")
    buf_up = _route_wait(sems_up, buf_up, 1, y_ssm, "route_wait_up_1")
    (sems_out, sems_up, sems_down), (buf_out, buf_up, buf_down), tok = _route_start(
        [(buf_out, 2), (buf_up, 2), (buf_down, 1)], y_sgu, "route_start_b", 9)
    ycat = _mix_norm_fwd(y_ssm, y_sgu, _after(p["g_out_ssm"], tok), p["g_out_sgu"])
    buf_out = _route_wait(sems_out, buf_out, 2, ycat, "route_wait_out_2")
    w_out_full = _pair_forward([buf_out], "pair_forward_out")[0].reshape(1, d, d)
    o = _mm_nn(ycat, w_out_full, _F32, "mm_out")
    x1, h2 = _fwd_mid(o, x, gt1, p["g_post_mix"], p["g_pre_ffn"], sc2, sh2)
    buf_up = _route_wait(sems_up, buf_up, 2, h2, "route_wait_up_2")
    w_up4 = shards(_pair_forward([buf_up], "pair_forward_up")[0])
    up_pre = _mm_nn(h2, w_up4, _F32, "mm_up")
    buf_down = _route_wait(sems_down, buf_down, 1, up_pre, "route_wait_down_1")
    (sems_down,), (buf_down,), tok = _route_start([(buf_down, 2)], up_pre, "route_start_c", 10)
    act = _conv_act_fwd(up_pre, conv_w_full, _after(p["conv_b"], tok))
    buf_down = _route_wait(sems_down, buf_down, 2, act, "route_wait_down_2")
    w_down_full = _pair_forward([buf_down], "pair_forward_down")[0].reshape(1, -1, d)
    f = _mm_nn(act, w_down_full, _F32, "mm_down", tk=5632)
    dx2, df, d_gt2, d_g_post_ffn, loss = _loss_and_post_ffn_bwd(f, x1, tgt, gt2, p["g_post_ffn"])

    def reduce_next(swap, n, after):
        sems, gw, land, _ = swap
        gw, got = _swap_wait(sems, gw, land, after, "swap_wait_" + n)
        cid = 11 + ["w_down", "w_up", "w_out", "w_in"].index(n)
        return _scatter_start(_pair_sum(gw, got, sel, "pair_sum_" + n), "scatter_start_" + n, cid)

    d_act = _mm_nt(df, w_down_full, _F32, "mm_d_act", tk=2048)
    swap_down = _swap_start(_mm_tn_rows(act, df, "mm_gw_down"), "swap_start_w_down", 0)
    d_up_pre, d_cw0, d_cw1, d_cw2, d_conv_b = _conv_act_bwd(up_pre, d_act, conv_w_full, _after(p["conv_b"], swap_down[3]))
    red_down = reduce_next(swap_down, "w_down", d_conv_b)
    dh2 = _mm_nt(d_up_pre, w_up4, _F32, "mm_dh2", tk=2816, after=red_down[3])
    swap_up = _swap_start(_mm_tn_cols(h2, d_up_pre, "mm_gw_up"), "swap_start_w_up", 1)
    dx1, d_o, d_sc2, d_sh2, d_g_pre_ffn, d_gt1, d_g_post_mix = _bwd_mid(
        dh2, x1, dx2, o, p["g_pre_ffn"], _after(sc2, swap_up[3]), gt1, p["g_post_mix"])
    red_up = reduce_next(swap_up, "w_up", d_g_post_mix)
    d_ycat = _mm_nt(d_o, w_out_full, _F32, "mm_d_ycat", tn=1024, tk=2048, after=red_up[3])
    swap_out = _swap_start(_mm_tn_rows(ycat, d_o, "mm_gw_out"), "swap_start_w_out", 2)
    dy_ssm, dy_sgu, d_g_out_ssm, d_g_out_sgu = _mix_norm_bwd(
        d_ycat, y_ssm, y_sgu, _after(p["g_out_ssm"], swap_out[3]), p["g_out_sgu"])
    red_out = reduce_next(swap_out, "w_out", d_g_out_sgu)
    dz_ssm, d_bbt, d_ct, d_wg, d_lb, d_ssm_d, d_bglu = _ssm_bwd(z, y_pre, h_all, dy_ssm, hb, bbt, ct,
                                                                _after(dvec, red_out[3]), wg, bglu, ptab_rev)
    dz, d_ln_g, d_ln_b, d_wm, d_bs = _sgu_bwd(z, dy_sgu, dz_ssm, p["sgu_ln_g"], p["sgu_ln_b"], wm, bs)
    dh1 = _mm_nt(dz, w_in4, _F32, "mm_dh1")
    swap_in = _swap_start(_mm_tn_cols(h1, dz, "mm_gw_in"), "swap_start_w_in", 3)
    dx, d_sc1, d_sh1, d_g_pre_mix = _bwd_pre_mix(dh1, x, dx1, p["g_pre_mix"], _after(sc1, swap_in[3]))
    red_in = reduce_next(swap_in, "w_in", d_g_pre_mix)

    nsb = BLOCK_ST
    lanes = lambda t: t.transpose(2, 0, 1, 3).reshape(SSM_GROUP, gn)
    d_bbr = lanes(_block_diag_take(d_bbt[:, :, :nsb], SSM_GROUP, n_st))
    d_bbi = lanes(_block_diag_take(d_bbt[:, :, nsb:], SSM_GROUP, n_st))
    d_lr, d_li = d_lb[:, 0, :nsb].reshape(1, gn), d_lb[:, 0, nsb:].reshape(1, gn)
    d_bre_t, d_bim_t, d_are, d_aim, d_dt = _ssm_prep_bwd(ldt_l, are_l, aim_l, bre_t, bim_t, d_bbr, d_bbi, d_lr, d_li)
    d_log_dt = _group_sum(d_dt.reshape(g_cnt, n_st), p["ssm_log_dt"].reshape(g_cnt, 1))
    c_grad = lambda t: _block_diag_take(t, n_st, SSM_GROUP).transpose(0, 1, 3, 2).reshape(g_cnt, SSM_GROUP, n_st)
    small = {
        "b_ada": jnp.concatenate([d_sh1, _after(d_sc1, red_in[3]), d_gt1, d_sh2, d_sc2, d_gt2], axis=1),
        "g_pre_mix": d_g_pre_mix, "g_post_mix": d_g_post_mix,
        "ssm_log_dt": d_log_dt, "ssm_a_re": d_are, "ssm_a_im": d_aim,
        "ssm_b_re": d_bre_t.T, "ssm_b_im": d_bim_t.T,
        "ssm_c_re": c_grad(d_ct[:, :nsb, :]), "ssm_c_im": -c_grad(d_ct[:, nsb:, :]),
        "ssm_d": d_ssm_d, "ssm_w_glu": _block_diag_take(d_wg, SSM_GROUP, SSM_GROUP), "ssm_b_glu": d_bglu,
        "sgu_ln_g": d_ln_g, "sgu_ln_b": d_ln_b, "sgu_w": d_wm * mask[None], "sgu_b": d_bs,
        "g_out_ssm": d_g_out_ssm, "g_out_sgu": d_g_out_sgu, "g_pre_ffn": d_g_pre_ffn, "g_post_ffn": d_g_post_ffn,
        "conv_b": d_conv_b, "conv_w_all": jnp.concatenate([d_cw0, d_cw1, d_cw2], axis=0),
        "loss_sum": loss,
    }
    small = {n: _to_view(n, a.reshape(p[n].shape)) if n in p else a for n, a in small.items()}
    pk = _Pack({n: a.shape for n, a in small.items()})
    sems_small, small_buf, tok = _gather8_start(_own_slot(pk.pack(small), dev), "gather_small_start")

    big = ["w_down", "w_up", "w_out", "w_in"]
    joins = []
    after = tok
    for i, (n, (sems, pair, land, _)) in enumerate(zip(big, (red_down, red_up, red_out, red_in))):
        pair, land = _scatter_wait(sems, pair, land, after, "scatter_wait_" + n)
        sems_j, half, after = _join_start(_chip_sum(pair, land, sel, "chip_sum_" + n), "join_start_" + n, 4 + i)
        joins.append((sems_j, half))
    big_out = {}
    for n, (sems_j, half) in zip(big, joins):
        j = _join_wait(sems_j, half, after, "join_wait_" + n)
        if n in ("w_in", "w_up"):
            big_out[n] = tuple(_adamw(p[n], j.reshape(p[n].shape), m[n], v[n], "adamw_" + n))
        else:
            big_out[n] = tuple(_adamw_halves(p[n], j, m[n], v[n], "adamw_" + n))
        after = big_out[n][1]

    gathered = _gather8_forward(_gather8_wait(sems_small, small_buf, after, "gather_small_wait"),
                                "gather_small_forward")
    gview = pk.unpack(_small_reduce(gathered))
    gview["conv_w"] = lax.dynamic_slice_in_dim(gview.pop("conv_w_all"), chip * nc_conv, nc_conv, axis=1)
    loss = gview.pop("loss_sum")
    small_names = _SMALL + ["conv_w"]
    per_group = [n for n in small_names if gview[n].ndim >= 2 and gview[n].shape[0] == g_cnt]
    others = [n for n in small_names if n not in per_group]
    grads = {n: _from_view(n, gview[n]) for n in small_names}
    deltas, new_m, new_v = {}, {}, {}
    for names, steps, call in ((per_group, g_cnt // GROUPS_PER_BLOCK, "adamw_s5"), (others, 1, "adamw_small")):
        res = _adamw_many([_to_view(n, p[n]) for n in names], [gview[n] for n in names],
                          [_to_view(n, m[n]) for n in names], [_to_view(n, v[n]) for n in names], steps, call)
        for n, dl, mo, vo in zip(names, *res):
            deltas[n], new_m[n], new_v[n] = _from_view(n, dl), _from_view(n, mo), _from_view(n, vo)

    d_mod_all = gathered.reshape(8, -1)[:, :N_MOD * d]
    d_mod_mine = lax.dynamic_slice_in_dim(d_mod_all, chip * ns_ada, ns_ada, axis=1)
    grads["w_ada"], deltas["w_ada"], new_m["w_ada"], new_v["w_ada"] = _wada_update(
        c_all.T, d_mod_mine, p["w_ada"], m["w_ada"], v["w_ada"])
    for n in big:
        grads[n], deltas[n], new_m[n], new_v[n] = big_out[n]
    return loss[0, 0], dx, grads, deltas, new_m, new_v


def kernel(x, c, w_ada, b_ada, g_pre_mix, g_post_mix, w_in, ssm_log_dt, ssm_a_re, ssm_a_im, ssm_b_re, ssm_b_im, ssm_c_re, ssm_c_im, ssm_d, ssm_w_glu, ssm_b_glu, sgu_ln_g, sgu_ln_b, sgu_w, sgu_b, g_out_ssm, g_out_sgu, w_out, g_pre_ffn, g_post_ffn, w_up, conv_w, conv_b, w_down, loss_target, m_w_ada, m_b_ada, m_g_pre_mix, m_g_post_mix, m_w_in, m_ssm_log_dt, m_ssm_a_re, m_ssm_a_im, m_ssm_b_re, m_ssm_b_im, m_ssm_c_re, m_ssm_c_im, m_ssm_d, m_ssm_w_glu, m_ssm_b_glu, m_sgu_ln_g, m_sgu_ln_b, m_sgu_w, m_sgu_b, m_g_out_ssm, m_g_out_sgu, m_w_out, m_g_pre_ffn, m_g_post_ffn, m_w_up, m_conv_w, m_conv_b, m_w_down, v_w_ada, v_b_ada, v_g_pre_mix, v_g_post_mix, v_w_in, v_ssm_log_dt, v_ssm_a_re, v_ssm_a_im, v_ssm_b_re, v_ssm_b_im, v_ssm_c_re, v_ssm_c_im, v_ssm_d, v_ssm_w_glu, v_ssm_b_glu, v_sgu_ln_g, v_sgu_ln_b, v_sgu_w, v_sgu_b, v_g_out_ssm, v_g_out_sgu, v_w_out, v_g_pre_ffn, v_g_post_ffn, v_w_up, v_conv_w, v_conv_b, v_w_down):
    given = dict(locals())
    drop = lambda a: a if a.ndim == 2 else a[0]
    p = {n: drop(given[n]) for n in _WEIGHTS}
    m = {n: drop(given["m_" + n]) for n in _WEIGHTS}
    v = {n: drop(given["v_" + n]) for n in _WEIGHTS}
    loss, dx, grads, deltas, new_m, new_v = _step(p, m, v, x[0], c, loss_target[0])
    outs = [loss, dx[None]]
    for group in (grads, deltas, new_m, new_v):
        outs += [group[n].reshape(given[n].shape) for n in _WEIGHTS]
    return tuple(outs)
```
